```python
import jax, jax.numpy as jnp
from jax import lax
import numpy as np

D_MODEL = 2048
BATCH = 8
SEQ = 2048
DEPTH = 1

N_META = 16
CHUNK = 128
MIX_DIM = D_MODEL
N_RET_HEADS = 8
RET_HEAD_DIM = 128
RET_DIM = N_RET_HEADS * RET_HEAD_DIM
N_FOX_HEADS = 8
FOX_HEAD_DIM = 128
FOX_DIM = N_FOX_HEADS * FOX_HEAD_DIM
IN_DIM = 4 * RET_DIM + 3 * FOX_DIM + N_FOX_HEADS
D_FF = 5632
CONV_WIDTH = 3
ROPE_BASE = 10000.0
NORM_EPS = 1e-6

kernel_name = "hymba_retention_fox_convffn_block"


def _rmsnorm(x, gain):
    x32 = x.astype(jnp.float32)
    y = x32 * lax.rsqrt(jnp.mean(x32 * x32, axis=-1, keepdims=True) + NORM_EPS)
    return (y * gain.astype(jnp.float32)).astype(x.dtype)


def _heads(t, n_heads):
    b, l, _ = t.shape
    return t.reshape(b, l, n_heads, -1).transpose(0, 2, 1, 3)


def _rotary(t, pos):
    d = t.shape[-1]
    inv_freq = 1.0 / (ROPE_BASE ** (jnp.arange(0, d, 2, dtype=jnp.float32) / d))
    ang = pos[:, None] * inv_freq[None, :]
    cos, sin = jnp.cos(ang), jnp.sin(ang)
    t1, t2 = t[..., : d // 2], t[..., d // 2:]
    return jnp.concatenate([t1 * cos - t2 * sin, t1 * sin + t2 * cos], axis=-1)


def _decay_matrix(log_g, n):
    idx = jnp.arange(n, dtype=jnp.float32)
    diff = idx[:, None] - idx[None, :]
    return jnp.where(diff >= 0, jnp.exp(jnp.maximum(diff, 0.0)[None] * log_g[:, None, None]), 0.0)


def _ret_intra(q, k, v, dmat):
    s = jnp.einsum('bhid,bhjd->bhij', q, k) * dmat[None]
    return jnp.einsum('bhij,bhjv->bhiv', s, v)


def _retention(q, k, v, log_g):
    b, h, l, dv = v.shape
    m = N_META
    out_m = _ret_intra(q[:, :, :m], k[:, :, :m], v[:, :, :m], _decay_matrix(log_g, m))
    zeta_m = jnp.exp((m - 1 - jnp.arange(m, dtype=jnp.float32))[None, :] * log_g[:, None])
    state0 = jnp.einsum('bhjd,bhjv,hj->bhdv', k[:, :, :m], v[:, :, :m], zeta_m)
    n_chunks = (l - m) // CHUNK

    def to_chunks(t):
        return t[:, :, m:].reshape(b, h, n_chunks, CHUNK, t.shape[-1]).transpose(2, 0, 1, 3, 4)

    pos_c = jnp.arange(CHUNK, dtype=jnp.float32)
    d_c = _decay_matrix(log_g, CHUNK)
    xi = jnp.exp((pos_c + 1.0)[None, :] * log_g[:, None])
    zeta = jnp.exp((CHUNK - 1.0 - pos_c)[None, :] * log_g[:, None])
    g_chunk = jnp.exp(CHUNK * log_g)[None, :, None, None]

    def step(state, qkv):
        qc, kc, vc = qkv
        o = _ret_intra(qc, kc, vc, d_c) + jnp.einsum('bhid,bhdv,hi->bhiv', qc, state, xi)
        state = g_chunk * state + jnp.einsum('bhjd,bhjv,hj->bhdv', kc, vc, zeta)
        return state, o

    _, o = lax.scan(step, state0, (to_chunks(q), to_chunks(k), to_chunks(v)))
    o = o.transpose(1, 2, 0, 3, 4).reshape(b, h, n_chunks * CHUNK, dv)
    return jnp.concatenate([out_m, o], axis=2)


def _forgetting_attention(q, k, v, log_f):
    l = q.shape[2]
    scale = q.shape[-1] ** -0.5
    cum = jnp.cumsum(log_f, axis=-1)
    bounds = [0, N_META] + [N_META + CHUNK * (i + 1) for i in range((l - N_META) // CHUNK)]
    outs = []
    for s, e in zip(bounds[:-1], bounds[1:]):
        logits = jnp.einsum('bhqd,bhkd->bhqk', q[:, :, s:e], k[:, :, :e]).astype(jnp.float32) * scale
        logits = logits + cum[:, :, s:e, None] - cum[:, :, None, :e]
        causal = jnp.arange(s, e)[:, None] >= jnp.arange(e)[None, :]
        p = jax.nn.softmax(jnp.where(causal[None, None], logits, -jnp.inf), axis=-1)
        outs.append(jnp.einsum('bhqk,bhkd->bhqd', p.astype(v.dtype), v[:, :, :e]))
    return jnp.concatenate(outs, axis=2)


def _head_groupnorm(o, gain):
    mu = jnp.mean(o, axis=-1, keepdims=True)
    var = jnp.mean(jnp.square(o - mu), axis=-1, keepdims=True)
    y = (o - mu) * lax.rsqrt(var + NORM_EPS)
    b, h, l, d = o.shape
    return y.transpose(0, 2, 1, 3).reshape(b, l, h * d) * gain.astype(jnp.float32)


def _causal_dwconv(u, w, bias):
    kw = w.shape[0]
    l = u.shape[1]
    up = jnp.pad(u, ((0, 0), (kw - 1, 0), (0, 0)))
    y = bias
    for i in range(kw):
        y = y + w[i] * up[:, i:i + l]
    return y


def _fwd_setup_inputs(seed: int = 0) -> dict:
    key = jax.random.key(seed)
    ks = jax.random.split(key, 16)
    f32 = jnp.float32
    x = jax.random.normal(ks[0], (BATCH, SEQ, D_MODEL), f32)
    meta_tokens = jax.random.normal(ks[1], (N_META, D_MODEL), f32)
    norm1_gain = 1.0 + 0.01 * jax.random.normal(ks[2], (DEPTH, D_MODEL), f32)
    w_in = jax.random.normal(ks[3], (DEPTH, D_MODEL, IN_DIM), f32) * D_MODEL ** -0.5
    b_forget = (jnp.linspace(1.0, 5.0, N_FOX_HEADS, dtype=f32)[None, :]
                + 0.1 * jax.random.normal(ks[4], (DEPTH, N_FOX_HEADS), f32))
    ret_norm_gain = 1.0 + 0.01 * jax.random.normal(ks[5], (DEPTH, RET_DIM), f32)
    w_out = jax.random.normal(ks[6], (DEPTH, MIX_DIM, D_MODEL), f32) * MIX_DIM ** -0.5
    norm2_gain = 1.0 + 0.01 * jax.random.normal(ks[7], (DEPTH, D_MODEL), f32)
    w_up = jax.random.normal(ks[8], (DEPTH, D_MODEL, 2 * D_FF), f32) * D_MODEL ** -0.5
    conv_w = jax.random.normal(ks[9], (DEPTH, CONV_WIDTH, 2 * D_FF), f32) * CONV_WIDTH ** -0.5
    conv_b = 0.01 * jax.random.normal(ks[10], (DEPTH, 2 * D_FF), f32)
    w_down = jax.random.normal(ks[11], (DEPTH, D_FF, D_MODEL), f32) * D_FF ** -0.5
    final_norm_gain = 1.0 + 0.01 * jax.random.normal(ks[12], (D_MODEL,), f32)
    return {"x": x, "meta_tokens": meta_tokens, "norm1_gain": norm1_gain, "w_in": w_in,
            "b_forget": b_forget, "ret_norm_gain": ret_norm_gain, "w_out": w_out,
            "norm2_gain": norm2_gain, "w_up": w_up, "conv_w": conv_w, "conv_b": conv_b,
            "w_down": w_down, "final_norm_gain": final_norm_gain}


def _fwd_reference(x, meta_tokens, norm1_gain, w_in, b_forget, ret_norm_gain, w_out,
              norm2_gain, w_up, conv_w, conv_b, w_down, final_norm_gain):
    b = x.shape[0]
    f32 = jnp.float32
    h = jnp.concatenate([jnp.broadcast_to(meta_tokens[None].astype(x.dtype), (b, N_META, D_MODEL)), x], axis=1)
    l = h.shape[1]
    pos = jnp.arange(l, dtype=f32)
    log_g = jnp.log1p(-jnp.exp2(-5.0 - jnp.arange(N_RET_HEADS, dtype=f32)))
    split_at = np.cumsum([RET_DIM] * 4 + [FOX_DIM] * 3)[:].tolist()

    for layer in range(DEPTH):
        a = _rmsnorm(h, norm1_gain[layer])
        proj = a @ w_in[layer]
        r_q, r_k, r_v, r_g, f_q, f_k, f_v, f_f = jnp.split(proj, split_at, axis=-1)

        rq = _rotary(_heads(r_q, N_RET_HEADS).astype(f32), pos)
        rk = _rotary(_heads(r_k, N_RET_HEADS).astype(f32), pos) * RET_HEAD_DIM ** -0.5
        rv = _heads(r_v, N_RET_HEADS).astype(f32)
        ret = _head_groupnorm(_retention(rq, rk, rv, log_g), ret_norm_gain[layer])
        ret = (jax.nn.silu(r_g.astype(f32)) * ret).astype(x.dtype)

        log_f = jax.nn.log_sigmoid(f_f.astype(f32) + b_forget[layer].astype(f32)).transpose(0, 2, 1)
        fox = _forgetting_attention(_heads(f_q, N_FOX_HEADS), _heads(f_k, N_FOX_HEADS),
                                    _heads(f_v, N_FOX_HEADS), log_f)
        fox = fox.transpose(0, 2, 1, 3).reshape(b, l, FOX_DIM).astype(x.dtype)

        h = h + jnp.concatenate([ret, fox], axis=-1) @ w_out[layer]

        c = _rmsnorm(h, norm2_gain[layer])
        u = _causal_dwconv(c @ w_up[layer], conv_w[layer], conv_b[layer])
        gate, val = jnp.split(u, 2, axis=-1)
        h = h + (jax.nn.silu(gate) * val) @ w_down[layer]

    out = _rmsnorm(h, final_norm_gain)
    return out[:, N_META:]


import jax as _jax
import jax.numpy as _jnp

TWIN_FORMAT = 'train_step'
FWD_PARAMS = ['x', 'meta_tokens', 'norm1_gain', 'w_in', 'b_forget', 'ret_norm_gain', 'w_out', 'norm2_gain', 'w_up', 'conv_w', 'conv_b', 'w_down', 'final_norm_gain']
TWIN_WEIGHTS = ['meta_tokens', 'norm1_gain', 'w_in', 'b_forget', 'ret_norm_gain', 'w_out', 'norm2_gain', 'w_up', 'conv_w', 'conv_b', 'w_down', 'final_norm_gain']
TWIN_DIFF_INPUT = 'x'
TWIN_INPUTS = ['x', 'meta_tokens', 'norm1_gain', 'w_in', 'b_forget', 'ret_norm_gain', 'w_out', 'norm2_gain', 'w_up', 'conv_w', 'conv_b', 'w_down', 'final_norm_gain', 'loss_target', 'm_meta_tokens', 'm_norm1_gain', 'm_w_in', 'm_b_forget', 'm_ret_norm_gain', 'm_w_out', 'm_norm2_gain', 'm_w_up', 'm_conv_w', 'm_conv_b', 'm_w_down', 'm_final_norm_gain', 'v_meta_tokens', 'v_norm1_gain', 'v_w_in', 'v_b_forget', 'v_ret_norm_gain', 'v_w_out', 'v_norm2_gain', 'v_w_up', 'v_conv_w', 'v_conv_b', 'v_w_down', 'v_final_norm_gain']
TWIN_OUTPUTS = ['loss', 'grad_x', 'grad_meta_tokens', 'grad_norm1_gain', 'grad_w_in', 'grad_b_forget', 'grad_ret_norm_gain', 'grad_w_out', 'grad_norm2_gain', 'grad_w_up', 'grad_conv_w', 'grad_conv_b', 'grad_w_down', 'grad_final_norm_gain', 'delta_meta_tokens', 'delta_norm1_gain', 'delta_w_in', 'delta_b_forget', 'delta_ret_norm_gain', 'delta_w_out', 'delta_norm2_gain', 'delta_w_up', 'delta_conv_w', 'delta_conv_b', 'delta_w_down', 'delta_final_norm_gain', 'new_m_meta_tokens', 'new_m_norm1_gain', 'new_m_w_in', 'new_m_b_forget', 'new_m_ret_norm_gain', 'new_m_w_out', 'new_m_norm2_gain', 'new_m_w_up', 'new_m_conv_w', 'new_m_conv_b', 'new_m_w_down', 'new_m_final_norm_gain', 'new_v_meta_tokens', 'new_v_norm1_gain', 'new_v_w_in', 'new_v_b_forget', 'new_v_ret_norm_gain', 'new_v_w_out', 'new_v_norm2_gain', 'new_v_w_up', 'new_v_conv_w', 'new_v_conv_b', 'new_v_w_down', 'new_v_final_norm_gain']
TWIN_LEAF_KINDS = {'loss': 'loss', 'grad_x': 'grad_x', 'grad_meta_tokens': 'grad_w', 'grad_norm1_gain': 'grad_w', 'grad_w_in': 'grad_w', 'grad_b_forget': 'grad_w', 'grad_ret_norm_gain': 'grad_w', 'grad_w_out': 'grad_w', 'grad_norm2_gain': 'grad_w', 'grad_w_up': 'grad_w', 'grad_conv_w': 'grad_w', 'grad_conv_b': 'grad_w', 'grad_w_down': 'grad_w', 'grad_final_norm_gain': 'grad_w', 'delta_meta_tokens': 'delta_w', 'delta_norm1_gain': 'delta_w', 'delta_w_in': 'delta_w', 'delta_b_forget': 'delta_w', 'delta_ret_norm_gain': 'delta_w', 'delta_w_out': 'delta_w', 'delta_norm2_gain': 'delta_w', 'delta_w_up': 'delta_w', 'delta_conv_w': 'delta_w', 'delta_conv_b': 'delta_w', 'delta_w_down': 'delta_w', 'delta_final_norm_gain': 'delta_w', 'new_m_meta_tokens': 'new_m', 'new_m_norm1_gain': 'new_m', 'new_m_w_in': 'new_m', 'new_m_b_forget': 'new_m', 'new_m_ret_norm_gain': 'new_m', 'new_m_w_out': 'new_m', 'new_m_norm2_gain': 'new_m', 'new_m_w_up': 'new_m', 'new_m_conv_w': 'new_m', 'new_m_conv_b': 'new_m', 'new_m_w_down': 'new_m', 'new_m_final_norm_gain': 'new_m', 'new_v_meta_tokens': 'new_v', 'new_v_norm1_gain': 'new_v', 'new_v_w_in': 'new_v', 'new_v_b_forget': 'new_v', 'new_v_ret_norm_gain': 'new_v', 'new_v_w_out': 'new_v', 'new_v_norm2_gain': 'new_v', 'new_v_w_up': 'new_v', 'new_v_conv_w': 'new_v', 'new_v_conv_b': 'new_v', 'new_v_w_down': 'new_v', 'new_v_final_norm_gain': 'new_v'}


def _forward(args):
    return _fwd_reference(*[args[k] for k in FWD_PARAMS])


def _output_shape():
    out = _jax.eval_shape(lambda: _forward(_fwd_setup_inputs(0)))
    return out.shape, out.dtype

N_MICROBATCH = 1
ADAM_LR = 0.001
ADAM_B1 = 0.9
ADAM_B2 = 0.999
ADAM_EPS = 1e-08
ADAM_WD = 0.01
ADAM_STEP = 10
PER_EXAMPLE_BATCH_AXIS = {'x': 0, 'loss_target': 0}
SHARED_INPUTS = []
_WEIGHT_DTYPES = {'meta_tokens': _jnp.float32, 'norm1_gain': _jnp.float32, 'w_in': _jnp.float32, 'b_forget': _jnp.float32, 'ret_norm_gain': _jnp.float32, 'w_out': _jnp.float32, 'norm2_gain': _jnp.float32, 'w_up': _jnp.float32, 'conv_w': _jnp.float32, 'conv_b': _jnp.float32, 'w_down': _jnp.float32, 'final_norm_gain': _jnp.float32}
MOMENT_SCALE = {'meta_tokens': 3.502478e-03, 'norm1_gain': 6.019197e-02, 'w_in': 3.126356e-02, 'b_forget': 1.261371e-01, 'ret_norm_gain': 3.761548e-02, 'w_out': 3.066564e-02, 'norm2_gain': 4.299112e-02, 'w_up': 1.855813e-02, 'conv_w': 1.841837e-02, 'conv_b': 1.792074e-02, 'w_down': 3.031143e-02, 'final_norm_gain': 7.996188e+00}


def _to_microbatches(a, axis):
    t = _jnp.moveaxis(a, axis, 0)
    t = t.reshape((N_MICROBATCH, t.shape[0] // N_MICROBATCH) + t.shape[1:])
    return _jnp.moveaxis(t, 1, axis + 1)


def setup_inputs(seed: int = 0) -> dict:
    inp = _fwd_setup_inputs(seed)
    key = _jax.random.fold_in(_jax.random.key(seed), 7919)
    shape, _ = _output_shape()
    out = dict(inp)
    out["loss_target"] = _jax.random.normal(_jax.random.fold_in(key, 0), shape, _jnp.float32)
    for i, name in enumerate(TWIN_WEIGHTS):
        w = inp[name].astype(_jnp.float32)
        if MOMENT_SCALE is None:
            s = _jnp.sqrt(_jnp.mean(_jnp.square(w)) + 1e-30)
        else:
            s = MOMENT_SCALE[name]
        km, kv = _jax.random.split(_jax.random.fold_in(key, i + 1))
        out[name] = w
        out["m_" + name] = s * _jax.random.normal(km, w.shape, _jnp.float32)
        out["v_" + name] = (s * s) * _jax.random.uniform(kv, w.shape, _jnp.float32, 0.5, 1.5)
    if N_MICROBATCH > 1:
        for name, axis in PER_EXAMPLE_BATCH_AXIS.items():
            out[name] = _to_microbatches(out[name], axis)
    return {'x': out['x'], 'meta_tokens': out['meta_tokens'], 'norm1_gain': out['norm1_gain'], 'w_in': out['w_in'], 'b_forget': out['b_forget'], 'ret_norm_gain': out['ret_norm_gain'], 'w_out': out['w_out'], 'norm2_gain': out['norm2_gain'], 'w_up': out['w_up'], 'conv_w': out['conv_w'], 'conv_b': out['conv_b'], 'w_down': out['w_down'], 'final_norm_gain': out['final_norm_gain'], 'loss_target': out['loss_target'], 'm_meta_tokens': out['m_meta_tokens'], 'm_norm1_gain': out['m_norm1_gain'], 'm_w_in': out['m_w_in'], 'm_b_forget': out['m_b_forget'], 'm_ret_norm_gain': out['m_ret_norm_gain'], 'm_w_out': out['m_w_out'], 'm_norm2_gain': out['m_norm2_gain'], 'm_w_up': out['m_w_up'], 'm_conv_w': out['m_conv_w'], 'm_conv_b': out['m_conv_b'], 'm_w_down': out['m_w_down'], 'm_final_norm_gain': out['m_final_norm_gain'], 'v_meta_tokens': out['v_meta_tokens'], 'v_norm1_gain': out['v_norm1_gain'], 'v_w_in': out['v_w_in'], 'v_b_forget': out['v_b_forget'], 'v_ret_norm_gain': out['v_ret_norm_gain'], 'v_w_out': out['v_w_out'], 'v_norm2_gain': out['v_norm2_gain'], 'v_w_up': out['v_w_up'], 'v_conv_w': out['v_conv_w'], 'v_conv_b': out['v_conv_b'], 'v_w_down': out['v_w_down'], 'v_final_norm_gain': out['v_final_norm_gain']}


def _loss(weights, diff, rest, loss_target):
    with _jax.named_scope("forward"):
        args = {**rest, TWIN_DIFF_INPUT: diff, **{k: w.astype(_WEIGHT_DTYPES[k]) for k, w in weights.items()}}
        y = _forward(args)
    with _jax.named_scope("loss_head"):
        err = _jnp.square(y.astype(_jnp.float32) - loss_target)
        return 0.5 * _jnp.sum(_jnp.mean(err, axis=-1)) if err.ndim else 0.5 * err


def _adamw(w, g, m, v):
    m = ADAM_B1 * m + (1.0 - ADAM_B1) * g
    v = ADAM_B2 * v + (1.0 - ADAM_B2) * _jnp.square(g)
    m_hat = m / (1.0 - ADAM_B1 ** ADAM_STEP)
    v_hat = v / (1.0 - ADAM_B2 ** ADAM_STEP)
    delta = -ADAM_LR * (m_hat / (_jnp.sqrt(v_hat) + ADAM_EPS) + ADAM_WD * w)
    return delta, m, v


def reference(x, meta_tokens, norm1_gain, w_in, b_forget, ret_norm_gain, w_out, norm2_gain, w_up, conv_w, conv_b, w_down, final_norm_gain, loss_target, m_meta_tokens, m_norm1_gain, m_w_in, m_b_forget, m_ret_norm_gain, m_w_out, m_norm2_gain, m_w_up, m_conv_w, m_conv_b, m_w_down, m_final_norm_gain, v_meta_tokens, v_norm1_gain, v_w_in, v_b_forget, v_ret_norm_gain, v_w_out, v_norm2_gain, v_w_up, v_conv_w, v_conv_b, v_w_down, v_final_norm_gain):
    given = dict(x=x, meta_tokens=meta_tokens, norm1_gain=norm1_gain, w_in=w_in, b_forget=b_forget, ret_norm_gain=ret_norm_gain, w_out=w_out, norm2_gain=norm2_gain, w_up=w_up, conv_w=conv_w, conv_b=conv_b, w_down=w_down, final_norm_gain=final_norm_gain, loss_target=loss_target, m_meta_tokens=m_meta_tokens, m_norm1_gain=m_norm1_gain, m_w_in=m_w_in, m_b_forget=m_b_forget, m_ret_norm_gain=m_ret_norm_gain, m_w_out=m_w_out, m_norm2_gain=m_norm2_gain, m_w_up=m_w_up, m_conv_w=m_conv_w, m_conv_b=m_conv_b, m_w_down=m_w_down, m_final_norm_gain=m_final_norm_gain, v_meta_tokens=v_meta_tokens, v_norm1_gain=v_norm1_gain, v_w_in=v_w_in, v_b_forget=v_b_forget, v_ret_norm_gain=v_ret_norm_gain, v_w_out=v_w_out, v_norm2_gain=v_norm2_gain, v_w_up=v_w_up, v_conv_w=v_conv_w, v_conv_b=v_conv_b, v_w_down=v_w_down, v_final_norm_gain=v_final_norm_gain)
    weights = {n: given[n] for n in TWIN_WEIGHTS}
    shared = {n: given[n] for n in SHARED_INPUTS}
    per_example = {n: given[n] for n in ['x']}
    grad_fn = _jax.value_and_grad(_loss, argnums=(0, 1))

    def one_microbatch(ex, loss_target):
        ex = dict(ex)
        diff = ex.pop(TWIN_DIFF_INPUT)
        return grad_fn(weights, diff, {**shared, **ex}, loss_target)

    if N_MICROBATCH == 1:
        loss, (grad_w, grad_x) = one_microbatch(per_example, given["loss_target"])
    else:
        def body(carry, xs):
            loss_sum, grad_sum = carry
            l_k, (gw_k, gx_k) = one_microbatch(xs[0], xs[1])
            with _jax.named_scope("update"):
                return (loss_sum + l_k, _jax.tree.map(_jnp.add, grad_sum, gw_k)), gx_k

        init = (_jnp.zeros((), _jnp.float32), _jax.tree.map(_jnp.zeros_like, weights))
        (loss, grad_w), grad_x = _jax.lax.scan(body, init, (per_example, given["loss_target"]))
    with _jax.named_scope("update"):
        delta_w, new_m, new_v = {}, {}, {}
        for n in TWIN_WEIGHTS:
            delta_w[n], new_m[n], new_v[n] = _adamw(weights[n], grad_w[n], given["m_" + n], given["v_" + n])
    return (loss, grad_x, *[grad_w[n] for n in TWIN_WEIGHTS], *[delta_w[n] for n in TWIN_WEIGHTS],
            *[new_m[n] for n in TWIN_WEIGHTS], *[new_v[n] for n in TWIN_WEIGHTS])
```

```python
import functools

import numpy as np
import jax
import jax.numpy as jnp
from jax import lax
from jax.experimental import pallas as pl
from jax.experimental.pallas import tpu as pltpu

F32 = jnp.float32
MXU_DTYPE = jnp.bfloat16
WIRE_DTYPE = jnp.bfloat16

N_DEV = 8
N_META = 16
CHUNK = 128
PAD_ROWS = CHUNK - N_META
N_HEADS = 8
HEAD_DIM = 128
GROUP = N_HEADS * HEAD_DIM
IN_DIM = 7 * GROUP + N_HEADS
WIN_SHARD = IN_DIM // N_DEV
WIN_BLOCK = 1024
WIN_STRIDE = 896
WIN_N = 7680
ROPE_BASE = 10000.0
NORM_EPS = 1e-6
NEG_BIG = -1e30
ADAM_LR, ADAM_B1, ADAM_B2, ADAM_EPS, ADAM_WD, ADAM_STEP = 0.001, 0.9, 0.999, 1e-08, 0.01, 10
VMEM_LIMIT = 52 * 1024 * 1024
MESH = pl.DeviceIdType.MESH
ANY = pl.BlockSpec(memory_space=pl.ANY)
VMEM_SPEC = pl.BlockSpec(memory_space=pltpu.VMEM)


def _params(sem=None):
    kw = {"vmem_limit_bytes": VMEM_LIMIT}
    if sem is not None:
        kw["dimension_semantics"] = sem
    return pltpu.CompilerParams(**kw)


def _divisor_tile(n, cap, unit):
    if n <= cap:
        return n
    best = None
    for t in range(unit, cap + 1, unit):
        if n % t == 0:
            best = t
    assert best is not None, (n, cap, unit)
    return best


def _my_position():
    return lax.axis_index("x"), lax.axis_index("y"), lax.axis_index("c")


def _device_index():
    x, y, c = _my_position()
    return 4 * x + 2 * y + c


def _all_gather(shard, name):
    r, c = shard.shape

    def body(x_ref, out_ref, send_sems, recv_sems, local_sem):
        mx, my, mc = _my_position()
        me, sibling = (mx, my, mc), (mx, my, 1 - mc)
        chips = [(1 - mx, my), (mx, 1 - my), (1 - mx, 1 - my)]

        def slot(px, py, pc):
            return out_ref.at[4 * px + 2 * py + pc]

        def copy(k, block, to, src=None):
            return pltpu.make_async_remote_copy(
                src_ref=slot(*block) if src is None else src, dst_ref=slot(*block),
                send_sem=send_sems.at[k], recv_sem=recv_sems.at[k], device_id=to, device_id_type=MESH)

        mine = pltpu.make_async_copy(x_ref, slot(*me), local_sem)
        mine.start()
        first = [copy(0, me, sibling, src=x_ref)]
        first += [copy(1 + j, me, (*chip, mc), src=x_ref) for j, chip in enumerate(chips)]
        for cp in first:
            cp.start()
        passed = [copy(4 + j, (*chip, mc), sibling) for j, chip in enumerate(chips)]
        for j, chip in enumerate(chips):
            copy(1 + j, (*chip, mc), me).wait_recv()
            passed[j].start()
        copy(0, sibling, me).wait_recv()
        for j, chip in enumerate(chips):
            copy(4 + j, (*chip, 1 - mc), me).wait_recv()
        for cp in first + passed:
            cp.wait_send()
        mine.wait()

    return pl.pallas_call(
        body, name=name,
        out_shape=jax.ShapeDtypeStruct((N_DEV, r, c), shard.dtype),
        in_specs=[ANY], out_specs=ANY,
        scratch_shapes=[pltpu.SemaphoreType.DMA((7,)), pltpu.SemaphoreType.DMA((7,)), pltpu.SemaphoreType.DMA],
    )(shard)


def _exchange_sibling(g, name):
    _, r, c = g.shape

    def body(g_ref, out_ref, send_sems, recv_sems):
        mx, my, mc = _my_position()
        copies = [
            pltpu.make_async_remote_copy(
                src_ref=g_ref.at[2 * k + (1 - mc)], dst_ref=out_ref.at[k],
                send_sem=send_sems.at[k], recv_sem=recv_sems.at[k],
                device_id=(mx, my, 1 - mc), device_id_type=MESH)
            for k in range(4)]
        for cp in copies:
            cp.start()
        for cp in copies:
            cp.wait()

    return pl.pallas_call(
        body, name=name,
        out_shape=jax.ShapeDtypeStruct((4, r, c), g.dtype),
        in_specs=[ANY], out_specs=ANY,
        scratch_shapes=[pltpu.SemaphoreType.DMA((4,)), pltpu.SemaphoreType.DMA((4,))],
    )(g)


def _exchange_chips(p, name):
    _, r, c = p.shape

    def body(p_ref, out_ref, send_sems, recv_sems):
        mx, my, mc = _my_position()
        chips = [(1 - mx, my), (mx, 1 - my), (1 - mx, 1 - my)]
        copies = [
            pltpu.make_async_remote_copy(
                src_ref=p_ref.at[2 * cx + cy], dst_ref=out_ref.at[j],
                send_sem=send_sems.at[j], recv_sem=recv_sems.at[j],
                device_id=(cx, cy, mc), device_id_type=MESH)
            for j, (cx, cy) in enumerate(chips)]
        for cp in copies:
            cp.start()
        for cp in copies:
            cp.wait()

    return pl.pallas_call(
        body, name=name,
        out_shape=jax.ShapeDtypeStruct((3, r, c), p.dtype),
        in_specs=[ANY], out_specs=ANY,
        scratch_shapes=[pltpu.SemaphoreType.DMA((3,)), pltpu.SemaphoreType.DMA((3,))],
    )(p)


def _pair_sum(g, recv, core, name):
    _, r, c = g.shape
    tr = _divisor_tile(r, 256, 16)

    def body(s_ref, g_ref, r_ref, o_ref):
        o_ref[...] = (g_ref[...].astype(F32) + r_ref[...].astype(F32)).astype(o_ref.dtype)

    return pl.pallas_call(
        body, name=name,
        out_shape=jax.ShapeDtypeStruct((4, r, c), g.dtype),
        grid_spec=pltpu.PrefetchScalarGridSpec(
            num_scalar_prefetch=1, grid=(4, r // tr),
            in_specs=[pl.BlockSpec((None, tr, c), lambda k, i, s: (2 * k + s[0], i, 0)),
                      pl.BlockSpec((None, tr, c), lambda k, i, s: (k, i, 0))],
            out_specs=pl.BlockSpec((None, tr, c), lambda k, i, s: (k, i, 0))),
        compiler_params=_params(("parallel", "parallel")),
    )(core, g, recv)


def _final_sum(p, recv, chip, name):
    _, r, c = p.shape
    tr = _divisor_tile(r, 256, 16)

    def body(s_ref, p_ref, r_ref, o_ref):
        acc = p_ref[...].astype(F32)
        for j in range(3):
            acc = acc + r_ref[j].astype(F32)
        o_ref[...] = acc

    return pl.pallas_call(
        body, name=name,
        out_shape=jax.ShapeDtypeStruct((r, c), F32),
        grid_spec=pltpu.PrefetchScalarGridSpec(
            num_scalar_prefetch=1, grid=(r // tr,),
            in_specs=[pl.BlockSpec((None, tr, c), lambda i, s: (s[0], i, 0)),
                      pl.BlockSpec((3, tr, c), lambda i, s: (0, i, 0))],
            out_specs=pl.BlockSpec((tr, c), lambda i, s: (i, 0))),
        compiler_params=_params(("parallel",)),
    )(chip, p, recv)


def _reduce_scatter(g, core, chip, name):
    from_sibling = _exchange_sibling(g, name + "_d2d")
    pair = _pair_sum(g, from_sibling, core, name + "_pairsum")
    from_chips = _exchange_chips(pair, name + "_ici")
    return _final_sum(pair, from_chips, chip, name + "_sum")


def _small_all_reduce(v, name):
    rows = v.shape[0]

    def body(v_ref, o_ref, gathered, send_sems, recv_sems):
        mx, my, mc = _my_position()
        me = 4 * mx + 2 * my + mc
        gathered[me] = v_ref[...]
        copies = []
        for rel in range(1, N_DEV):
            bx, by, bc = (rel >> 2) & 1, (rel >> 1) & 1, rel & 1
            target = (1 - mx if bx else mx, 1 - my if by else my, 1 - mc if bc else mc)
            cp = pltpu.make_async_remote_copy(
                src_ref=v_ref, dst_ref=gathered.at[me],
                send_sem=send_sems.at[rel - 1], recv_sem=recv_sems.at[rel - 1],
                device_id=target, device_id_type=MESH)
            cp.start()
            copies.append(cp)
        for cp in copies:
            cp.wait()
        acc = gathered[0]
        for j in range(1, N_DEV):
            acc = acc + gathered[j]
        o_ref[...] = acc

    return pl.pallas_call(
        body, name=name,
        out_shape=jax.ShapeDtypeStruct((rows, 128), F32),
        in_specs=[VMEM_SPEC], out_specs=VMEM_SPEC,
        scratch_shapes=[pltpu.VMEM((N_DEV, rows, 128), F32),
                        pltpu.SemaphoreType.DMA((7,)), pltpu.SemaphoreType.DMA((7,))],
        compiler_params=_params(),
    )(v)


def _assemble_w_in(blocks):
    _, d, _ = blocks.shape
    tr = _divisor_tile(d, 256, 16)
    n_tiles = WIN_N // 128
    last = (N_DEV * WIN_STRIDE) // 128

    def body(b_ref, o_ref):
        for t in range(n_tiles):
            if t > last:
                o_ref[:, t * 128:(t + 1) * 128] = jnp.zeros((tr, 128), o_ref.dtype)
                continue
            i = min(t // 7, N_DEV - 1)
            k = t - 7 * i
            val = b_ref[i, :, k * 128:(k + 1) * 128]
            if k == 0 and i >= 1:
                val = val + b_ref[i - 1, :, 7 * 128:8 * 128]
            o_ref[:, t * 128:(t + 1) * 128] = val

    return pl.pallas_call(
        body, name="assemble_w_in",
        out_shape=jax.ShapeDtypeStruct((d, WIN_N), blocks.dtype),
        grid=(d // tr,),
        in_specs=[pl.BlockSpec((N_DEV, tr, WIN_BLOCK), lambda i: (0, i, 0))],
        out_specs=pl.BlockSpec((tr, WIN_N), lambda i: (i, 0)),
        compiler_params=_params(("parallel",)),
    )(blocks)


def _extract_w_in_windows(g):
    d, _ = g.shape
    tr = _divisor_tile(d, 256, 16)

    def body(g_ref, o_ref):
        for j in range(N_DEV):
            o_ref[j] = g_ref[:, WIN_STRIDE * j:WIN_STRIDE * j + WIN_BLOCK]

    return pl.pallas_call(
        body, name="extract_w_in_windows",
        out_shape=jax.ShapeDtypeStruct((N_DEV, d, WIN_BLOCK), g.dtype),
        grid=(d // tr,),
        in_specs=[pl.BlockSpec((tr, WIN_N), lambda i: (i, 0))],
        out_specs=pl.BlockSpec((N_DEV, tr, WIN_BLOCK), lambda i: (0, i, 0)),
        compiler_params=_params(("parallel",)),
    )(g)


def _mm(a, b, *, a_spec, b_spec, o_spec, out_shape, grid, contract, nk, name):
    dn = (((contract[0],), (contract[1],)), ((), ()))
    tm, tn = o_spec.block_shape[-2:]

    def body(a_ref, b_ref, o_ref, *scratch):
        part = lax.dot_general(a_ref[...], b_ref[...], dn, preferred_element_type=F32)
        if nk == 1:
            o_ref[...] = part.astype(o_ref.dtype)
            return
        acc = scratch[0]
        k = pl.program_id(2)

        @pl.when(k == 0)
        def _():
            acc[...] = part

        @pl.when(k > 0)
        def _():
            acc[...] += part

        @pl.when(k == nk - 1)
        def _():
            o_ref[...] = acc[...].astype(o_ref.dtype)

    return pl.pallas_call(
        body, name=name, out_shape=out_shape, grid=grid,
        in_specs=[a_spec, b_spec], out_specs=o_spec,
        scratch_shapes=[] if nk == 1 else [pltpu.VMEM((tm, tn), F32)],
        compiler_params=_params(("parallel", "parallel", "arbitrary")),
    )(a, b)


def _mm_nn(a, b, out_dtype, name, tm_cap=1088, tn_cap=512, tk_cap=2048):
    m, k = a.shape
    _, n = b.shape
    tm, tn, tk = _divisor_tile(m, tm_cap, 16), _divisor_tile(n, tn_cap, 128), _divisor_tile(k, tk_cap, 128)
    return _mm(a, b,
               a_spec=pl.BlockSpec((tm, tk), lambda i, j, kk: (i, kk)),
               b_spec=pl.BlockSpec((tk, tn), lambda i, j, kk: (kk, j)),
               o_spec=pl.BlockSpec((tm, tn), lambda i, j, kk: (i, j)),
               out_shape=jax.ShapeDtypeStruct((m, n), out_dtype),
               grid=(m // tm, n // tn, k // tk), contract=(1, 0), nk=k // tk, name=name)


def _mm_nt(a, b, out_dtype, name, tm_cap=1088, tn_cap=512, tk_cap=2048):
    m, k = a.shape
    n, _ = b.shape
    tm, tn, tk = _divisor_tile(m, tm_cap, 16), _divisor_tile(n, tn_cap, 128), _divisor_tile(k, tk_cap, 128)
    return _mm(a, b,
               a_spec=pl.BlockSpec((tm, tk), lambda i, j, kk: (i, kk)),
               b_spec=pl.BlockSpec((tn, tk), lambda i, j, kk: (j, kk)),
               o_spec=pl.BlockSpec((tm, tn), lambda i, j, kk: (i, j)),
               out_shape=jax.ShapeDtypeStruct((m, n), out_dtype),
               grid=(m // tm, n // tn, k // tk), contract=(1, 1), nk=k // tk, name=name)


def _mm_tn(a, b, out_dtype, name, tm_cap=1024, tn_cap=512):
    l, m = a.shape
    _, n = b.shape
    tm, tn = _divisor_tile(m, tm_cap, 128), _divisor_tile(n, tn_cap, 128)
    return _mm(a, b,
               a_spec=pl.BlockSpec((l, tm), lambda i, j, kk: (0, i)),
               b_spec=pl.BlockSpec((l, tn), lambda i, j, kk: (0, j)),
               o_spec=pl.BlockSpec((tm, tn), lambda i, j, kk: (i, j)),
               out_shape=jax.ShapeDtypeStruct((m, n), out_dtype),
               grid=(m // tm, n // tn, 1), contract=(0, 0), nk=1, name=name)


def _row_tile(l):
    return _divisor_tile(l, 544, 8)


def _rmsnorm_fwd(h, gain, name, res=None):
    l, d = h.shape
    tr = _row_tile(l)
    row = pl.BlockSpec((tr, d), lambda i: (i, 0))
    vec = pl.BlockSpec((1, d), lambda i: (0, 0))

    def body(*refs):
        if res is None:
            h_ref, g_ref, n_ref = refs
            x = h_ref[...]
        else:
            h_ref, r_ref, g_ref, s_ref, n_ref = refs
            x = h_ref[...] + r_ref[...]
            s_ref[...] = x
        y = x * lax.rsqrt(jnp.mean(x * x, axis=-1, keepdims=True) + NORM_EPS)
        n_ref[...] = (y * g_ref[...]).astype(n_ref.dtype)

    normed = jax.ShapeDtypeStruct((l, d), MXU_DTYPE)
    if res is None:
        return pl.pallas_call(body, name=name, out_shape=normed, grid=(l // tr,), in_specs=[row, vec],
                              out_specs=row, compiler_params=_params(("parallel",)))(h, gain)
    return pl.pallas_call(body, name=name, out_shape=(jax.ShapeDtypeStruct((l, d), F32), normed),
                          grid=(l // tr,), in_specs=[row, row, vec], out_specs=(row, row),
                          compiler_params=_params(("parallel",)))(h, res, gain)


def _rmsnorm_bwd(d_res, d_normed, x, gain, name, with_mxu_copy):
    l, d = x.shape
    tr = _row_tile(l)
    row = pl.BlockSpec((tr, d), lambda i: (i, 0))
    vec = pl.BlockSpec((1, d), lambda i: (0, 0))

    def body(dres_ref, dn_ref, x_ref, g_ref, dx_ref, *rest):
        dg_ref = rest[-1]
        xv = x_ref[...]
        r = lax.rsqrt(jnp.mean(xv * xv, axis=-1, keepdims=True) + NORM_EPS)
        xh = xv * r
        dn = dn_ref[...]
        dxh = dn * g_ref[...]
        dx = dres_ref[...] + r * (dxh - xh * jnp.mean(dxh * xh, axis=-1, keepdims=True))
        dx_ref[...] = dx
        if with_mxu_copy:
            rest[0][...] = dx.astype(MXU_DTYPE)

        @pl.when(pl.program_id(0) == 0)
        def _():
            dg_ref[...] = jnp.zeros_like(dg_ref)

        dg_ref[...] += jnp.sum(dn * xh, axis=0, keepdims=True)

    outs = [jax.ShapeDtypeStruct((l, d), F32)]
    specs = [row]
    if with_mxu_copy:
        outs.append(jax.ShapeDtypeStruct((l, d), MXU_DTYPE))
        specs.append(row)
    outs.append(jax.ShapeDtypeStruct((1, d), F32))
    specs.append(vec)
    return pl.pallas_call(body, name=name, out_shape=tuple(outs), grid=(l // tr,),
                          in_specs=[row, row, row, vec], out_specs=tuple(specs),
                          compiler_params=_params(("arbitrary",)))(d_res, d_normed, x, gain)


def _loss_head(h1, mlp_out, gain, target):
    l, d = h1.shape
    n_blocks = l // CHUNK
    row = pl.BlockSpec((CHUNK, d), lambda i: (i, 0))
    vec = pl.BlockSpec((1, d), lambda i: (0, 0))
    tgt = pl.BlockSpec((CHUNK, d), lambda i: (jnp.maximum(i - 1, 0), 0))

    def body(h_ref, m_ref, g_ref, t_ref, dh_ref, dhb_ref, dg_ref, loss_ref, sq_ref):
        i = pl.program_id(0)
        x = h_ref[...] + m_ref[...]
        r = lax.rsqrt(jnp.mean(x * x, axis=-1, keepdims=True) + NORM_EPS)
        xh = x * r
        g = g_ref[...]
        real = i >= 1
        err = jnp.where(real, xh * g - t_ref[...], 0.0)
        dy = err * (1.0 / d)
        dxh = dy * g
        dh = r * (dxh - xh * jnp.mean(dxh * xh, axis=-1, keepdims=True))
        dh_ref[...] = dh
        dhb_ref[...] = dh.astype(MXU_DTYPE)

        @pl.when(i == 0)
        def _():
            dg_ref[...] = jnp.zeros_like(dg_ref)
            sq_ref[...] = jnp.zeros_like(sq_ref)

        dg_ref[...] += jnp.sum(dy * xh, axis=0, keepdims=True)
        sq_ref[...] += jnp.sum(err * err, axis=0, keepdims=True)

        @pl.when(i == n_blocks - 1)
        def _():
            total = jnp.sum(sq_ref[...], axis=-1, keepdims=True) * (0.5 / d)
            loss_ref[...] = jnp.broadcast_to(total, (1, 128))

    return pl.pallas_call(
        body, name="loss_head",
        out_shape=(jax.ShapeDtypeStruct((l, d), F32), jax.ShapeDtypeStruct((l, d), MXU_DTYPE),
                   jax.ShapeDtypeStruct((1, d), F32), jax.ShapeDtypeStruct((1, 128), F32)),
        grid=(n_blocks,), in_specs=[row, row, vec, tgt],
        out_specs=(row, row, vec, pl.BlockSpec((1, 128), lambda i: (0, 0))),
        scratch_shapes=[pltpu.VMEM((1, d), F32)],
        compiler_params=_params(("arbitrary",)),
    )(h1, mlp_out, gain, target)


def _dot(a, b):
    return jnp.dot(a, b, preferred_element_type=F32)


def _dot_nt(a, b):
    return lax.dot_general(a, b, (((1,), (1,)), ((), ())), preferred_element_type=F32)


def _dot_tn(a, b):
    return lax.dot_general(a, b, (((0,), (0,)), ((), ())), preferred_element_type=F32)


def _rope(t, cos2, sin2):
    return t * cos2 + pltpu.roll(t, HEAD_DIM // 2, 1) * sin2


def _rope_bwd(dr, cos2, sin2):
    return dr * cos2 + pltpu.roll(dr * sin2, HEAD_DIM // 2, 1)


def _sigmoid(x):
    return 1.0 / (1.0 + jnp.exp(-x))


def _row_valid(block, rows):
    r = block * CHUNK + lax.broadcasted_iota(jnp.int32, (rows, 1), 0)
    return r >= PAD_ROWS


def _retention_consts(l):
    pos = jnp.arange(l, dtype=F32) - PAD_ROWS
    inv_freq = 1.0 / (ROPE_BASE ** (jnp.arange(0, HEAD_DIM, 2, dtype=F32) / HEAD_DIM))
    ang = pos[:, None] * inv_freq[None, :]
    cos, sin = jnp.cos(ang), jnp.sin(ang)
    cos2 = jnp.concatenate([cos, cos], axis=-1)
    sin2 = jnp.concatenate([-sin, sin], axis=-1)
    log_g = jnp.log1p(-jnp.exp2(-5.0 - jnp.arange(N_HEADS, dtype=F32)))
    idx = jnp.arange(CHUNK, dtype=F32)
    diff = idx[:, None] - idx[None, :]
    decay = jnp.where(diff >= 0, jnp.exp(jnp.maximum(diff, 0.0)[None] * log_g[:, None, None]), 0.0)
    xi = jnp.exp((idx + 1.0)[None, :] * log_g[:, None])
    zeta = jnp.exp((CHUNK - 1.0 - idx)[None, :] * log_g[:, None])
    g_chunk = jnp.exp(CHUNK * log_g)
    bcast = lambda v: jnp.broadcast_to(v[:, :, None], (N_HEADS, CHUNK, HEAD_DIM))
    g_rows = jnp.broadcast_to(g_chunk[:, None, None], (N_HEADS, 8, HEAD_DIM))
    return cos2, sin2, decay, bcast(xi), bcast(zeta), g_rows


def _retention_fwd(proj, ret_gain, consts):
    l = proj.shape[0]
    n_chunks = l // CHUNK
    cos2, sin2, decay, xi, zeta, g_rows = consts
    scale = HEAD_DIM ** -0.5

    def body(p_ref, cos_ref, sin_ref, dec_ref, xi_ref, zeta_ref, gr_ref, gain_ref,
             mix_ref, o_ref, st_ref, state):
        c = pl.program_id(0)

        @pl.when(c == 0)
        def _():
            state[...] = jnp.zeros_like(state)

        cos_v, sin_v = cos_ref[...], sin_ref[...]
        valid = _row_valid(c, CHUNK)
        for h in range(N_HEADS):
            cols = slice(h * HEAD_DIM, (h + 1) * HEAD_DIM)
            q = p_ref[:, h * HEAD_DIM:(h + 1) * HEAD_DIM]
            k = p_ref[:, GROUP + h * HEAD_DIM:GROUP + (h + 1) * HEAD_DIM]
            v = p_ref[:, 2 * GROUP + h * HEAD_DIM:2 * GROUP + (h + 1) * HEAD_DIM]
            g = p_ref[:, 3 * GROUP + h * HEAD_DIM:3 * GROUP + (h + 1) * HEAD_DIM]
            rq = _rope(q, cos_v, sin_v).astype(MXU_DTYPE)
            rk = _rope(k, cos_v, sin_v) * scale
            rkb = rk.astype(MXU_DTYPE)
            vb = v.astype(MXU_DTYPE)
            st = state[h]
            st_ref[h] = st
            s = _dot_nt(rq, rkb) * dec_ref[h]
            o = _dot(s.astype(MXU_DTYPE), vb) + _dot(rq, st.astype(MXU_DTYPE)) * xi_ref[h]
            kz = (rk * zeta_ref[h]).astype(MXU_DTYPE)
            state[h] = gr_ref[h, 0:1, :] * st + _dot_tn(kz, vb)
            o_ref[:, cols] = o
            mu = jnp.mean(o, axis=-1, keepdims=True)
            oc = o - mu
            yn = oc * lax.rsqrt(jnp.mean(oc * oc, axis=-1, keepdims=True) + NORM_EPS)
            ret = (g * _sigmoid(g)) * (yn * gain_ref[:, cols])
            mix_ref[:, cols] = jnp.where(valid, ret, 0.0).astype(mix_ref.dtype)

    head_tab = pl.BlockSpec((N_HEADS, CHUNK, HEAD_DIM), lambda c: (0, 0, 0))
    return pl.pallas_call(
        body, name="retention_fwd",
        out_shape=(jax.ShapeDtypeStruct((l, GROUP), MXU_DTYPE), jax.ShapeDtypeStruct((l, GROUP), F32),
                   jax.ShapeDtypeStruct((n_chunks, N_HEADS, HEAD_DIM, HEAD_DIM), F32)),
        grid=(n_chunks,),
        in_specs=[pl.BlockSpec((CHUNK, 4 * GROUP), lambda c: (c, 0)),
                  pl.BlockSpec((CHUNK, HEAD_DIM), lambda c: (c, 0)),
                  pl.BlockSpec((CHUNK, HEAD_DIM), lambda c: (c, 0)),
                  head_tab, head_tab, head_tab,
                  pl.BlockSpec((N_HEADS, 8, HEAD_DIM), lambda c: (0, 0, 0)),
                  pl.BlockSpec((1, GROUP), lambda c: (0, 0))],
        out_specs=(pl.BlockSpec((CHUNK, GROUP), lambda c: (c, 0)),
                   pl.BlockSpec((CHUNK, GROUP), lambda c: (c, 0)),
                   pl.BlockSpec((None, N_HEADS, HEAD_DIM, HEAD_DIM), lambda c: (c, 0, 0, 0))),
        scratch_shapes=[pltpu.VMEM((N_HEADS, HEAD_DIM, HEAD_DIM), F32)],
        compiler_params=_params(("arbitrary",)),
    )(proj, cos2, sin2, decay, xi, zeta, g_rows, ret_gain)


def _retention_bwd(proj, o_pre, states, d_mix, ret_gain, consts):
    l = proj.shape[0]
    n_chunks = l // CHUNK
    cos2, sin2, decay, xi, zeta, g_rows = consts
    scale = HEAD_DIM ** -0.5
    rev = lambda c: n_chunks - 1 - c

    def body(p_ref, o_ref, st_ref, dm_ref, cos_ref, sin_ref, dec_ref, xi_ref, zeta_ref, gr_ref, gain_ref,
             dp_ref, dgain_ref, dstate):
        step = pl.program_id(0)

        @pl.when(step == 0)
        def _():
            dstate[...] = jnp.zeros_like(dstate)
            dgain_ref[...] = jnp.zeros_like(dgain_ref)

        cos_v, sin_v = cos_ref[...], sin_ref[...]
        valid = _row_valid(rev(step), CHUNK)
        for h in range(N_HEADS):
            cols = slice(h * HEAD_DIM, (h + 1) * HEAD_DIM)
            q = p_ref[:, h * HEAD_DIM:(h + 1) * HEAD_DIM]
            k = p_ref[:, GROUP + h * HEAD_DIM:GROUP + (h + 1) * HEAD_DIM]
            v = p_ref[:, 2 * GROUP + h * HEAD_DIM:2 * GROUP + (h + 1) * HEAD_DIM]
            g = p_ref[:, 3 * GROUP + h * HEAD_DIM:3 * GROUP + (h + 1) * HEAD_DIM]
            o = o_ref[:, cols]
            gain = gain_ref[:, cols]
            d_ret = jnp.where(valid, dm_ref[:, cols], 0.0)
            mu = jnp.mean(o, axis=-1, keepdims=True)
            oc = o - mu
            rstd = lax.rsqrt(jnp.mean(oc * oc, axis=-1, keepdims=True) + NORM_EPS)
            yn = oc * rstd
            sig = _sigmoid(g)
            gate = g * sig
            dgain_ref[:, cols] += jnp.sum(d_ret * gate * yn, axis=0, keepdims=True)
            d_g = d_ret * (yn * gain) * (sig * (1.0 + g * (1.0 - sig)))
            d_yn = d_ret * gate * gain
            d_o = rstd * (d_yn - jnp.mean(d_yn, axis=-1, keepdims=True)
                          - yn * jnp.mean(d_yn * yn, axis=-1, keepdims=True))
            rq = _rope(q, cos_v, sin_v)
            rk = _rope(k, cos_v, sin_v) * scale
            rqb, rkb, vb = rq.astype(MXU_DTYPE), rk.astype(MXU_DTYPE), v.astype(MXU_DTYPE)
            dob = d_o.astype(MXU_DTYPE)
            dec = dec_ref[h]
            xi_h, zeta_h = xi_ref[h], zeta_ref[h]
            st_b = st_ref[h].astype(MXU_DTYPE)
            dst = dstate[h]
            dst_b = dst.astype(MXU_DTYPE)
            s_b = (_dot_nt(rqb, rkb) * dec).astype(MXU_DTYPE)
            da_b = (_dot_nt(dob, vb) * dec).astype(MXU_DTYPE)
            doxi_b = (d_o * xi_h).astype(MXU_DTYPE)
            kz_b = (rk * zeta_h).astype(MXU_DTYPE)
            d_rq = _dot(da_b, rkb) + _dot_nt(doxi_b, st_b)
            d_rk = _dot_tn(da_b, rqb) + _dot_nt(vb, dst_b) * zeta_h
            d_v = _dot_tn(s_b, dob) + _dot(kz_b, dst_b)
            dstate[h] = gr_ref[h, 0:1, :] * dst + _dot_tn(rqb, doxi_b)
            d_q = _rope_bwd(d_rq, cos_v, sin_v)
            d_k = _rope_bwd(d_rk * scale, cos_v, sin_v)
            dp_ref[:, h * HEAD_DIM:(h + 1) * HEAD_DIM] = d_q.astype(dp_ref.dtype)
            dp_ref[:, GROUP + h * HEAD_DIM:GROUP + (h + 1) * HEAD_DIM] = d_k.astype(dp_ref.dtype)
            dp_ref[:, 2 * GROUP + h * HEAD_DIM:2 * GROUP + (h + 1) * HEAD_DIM] = d_v.astype(dp_ref.dtype)
            dp_ref[:, 3 * GROUP + h * HEAD_DIM:3 * GROUP + (h + 1) * HEAD_DIM] = d_g.astype(dp_ref.dtype)

    head_tab = pl.BlockSpec((N_HEADS, CHUNK, HEAD_DIM), lambda c: (0, 0, 0))
    return pl.pallas_call(
        body, name="retention_bwd",
        out_shape=(jax.ShapeDtypeStruct((l, 4 * GROUP), MXU_DTYPE), jax.ShapeDtypeStruct((1, GROUP), F32)),
        grid=(n_chunks,),
        in_specs=[pl.BlockSpec((CHUNK, 4 * GROUP), lambda c: (rev(c), 0)),
                  pl.BlockSpec((CHUNK, GROUP), lambda c: (rev(c), 0)),
                  pl.BlockSpec((None, N_HEADS, HEAD_DIM, HEAD_DIM), lambda c: (rev(c), 0, 0, 0)),
                  pl.BlockSpec((CHUNK, GROUP), lambda c: (rev(c), 0)),
                  pl.BlockSpec((CHUNK, HEAD_DIM), lambda c: (rev(c), 0)),
                  pl.BlockSpec((CHUNK, HEAD_DIM), lambda c: (rev(c), 0)),
                  head_tab, head_tab, head_tab,
                  pl.BlockSpec((N_HEADS, 8, HEAD_DIM), lambda c: (0, 0, 0)),
                  pl.BlockSpec((1, GROUP), lambda c: (0, 0))],
        out_specs=(pl.BlockSpec((CHUNK, 4 * GROUP), lambda c: (rev(c), 0)),
                   pl.BlockSpec((1, GROUP), lambda c: (0, 0))),
        scratch_shapes=[pltpu.VMEM((N_HEADS, HEAD_DIM, HEAD_DIM), F32)],
        compiler_params=_params(("arbitrary",)),
    )(proj, o_pre, states, d_mix, cos2, sin2, decay, xi, zeta, g_rows, ret_gain)


FF_TILE = (7 * GROUP) // 128


def _log_forget(ff, bias_row, valid):
    x = ff + bias_row
    e = jnp.exp(-jnp.abs(x))
    lf = jnp.minimum(x, 0.0) - jnp.log(1.0 + e)
    head_lane = lax.broadcasted_iota(jnp.int32, x.shape, 1) < N_HEADS
    keep = lambda t: jnp.where(head_lane, jnp.where(valid, t, 0.0), 0.0)
    return keep(lf), keep(jnp.where(x >= 0, e, 1.0) / (1.0 + e))


def _fox_prep(proj, bias_row):
    l = proj.shape[0]
    n_blocks = l // CHUNK

    def body(ff_ref, b_ref, bc_ref, rows_ref, cum):
        r = lax.broadcasted_iota(jnp.int32, (CHUNK, CHUNK), 0)
        cidx = lax.broadcasted_iota(jnp.int32, (CHUNK, CHUNK), 1)
        tri = jnp.where(r >= cidx, 1.0, 0.0).astype(F32)
        carry = jnp.zeros((1, 128), F32)
        for blk in range(n_blocks):
            rows = slice(blk * CHUNK, (blk + 1) * CHUNK)
            valid = _row_valid(blk, CHUNK)
            lf, _ = _log_forget(ff_ref[rows, :], b_ref[...], valid)
            local = jnp.dot(tri, lf, precision=lax.Precision.HIGHEST, preferred_element_type=F32) + carry
            carry = local[CHUNK - 1:CHUNK, :]
            masked = jnp.where(valid, local, -NEG_BIG)
            cum[rows, :] = masked
            t = masked.T
            for h in range(N_HEADS):
                rows_ref[h, :, rows] = t[h:h + 1, :]
        full = cum[...]
        for h in range(N_HEADS):
            bc_ref[h] = jnp.broadcast_to(full[:, h:h + 1], (l, 128))

    return pl.pallas_call(
        body, name="fox_prep",
        out_shape=(jax.ShapeDtypeStruct((N_HEADS, l, 128), F32), jax.ShapeDtypeStruct((N_HEADS, 1, l), F32)),
        grid=(1,),
        in_specs=[pl.BlockSpec((l, 128), lambda i: (0, FF_TILE)), pl.BlockSpec((1, 128), lambda i: (0, 0))],
        out_specs=(pl.BlockSpec((N_HEADS, l, 128), lambda i: (0, 0, 0)),
                   pl.BlockSpec((N_HEADS, 1, l), lambda i: (0, 0, 0))),
        scratch_shapes=[pltpu.VMEM((l, 128), F32)],
        compiler_params=_params(("arbitrary",)),
    )(proj, bias_row)


def _fox_fwd(proj, cum_bc, cum_rows):
    l = proj.shape[0]
    n_blocks = l // CHUNK
    scale = HEAD_DIM ** -0.5
    qt, kt, vt = 4 * N_HEADS, 5 * N_HEADS, 6 * N_HEADS

    def body(q_ref, k_ref, v_ref, cq_ref, ck_ref, o_ref):
        i = pl.program_id(1)
        qb = q_ref[...].astype(MXU_DTYPE)
        kb = k_ref[...].astype(MXU_DTYPE)
        vb = v_ref[...].astype(MXU_DTYPE)
        s = _dot_nt(qb, kb) * scale
        bias = jnp.tile(cq_ref[...], (1, n_blocks)) - ck_ref[...]
        q_pos = i * CHUNK + lax.broadcasted_iota(jnp.int32, (CHUNK, l), 0)
        k_pos = lax.broadcasted_iota(jnp.int32, (CHUNK, l), 1)
        s = jnp.where(k_pos <= q_pos, s + bias, NEG_BIG)
        m = jnp.max(s, axis=-1, keepdims=True)
        e = jnp.exp(s - m)
        p = e * (1.0 / jnp.sum(e, axis=-1, keepdims=True))
        o = _dot(p.astype(MXU_DTYPE), vb)
        o_ref[...] = jnp.where(_row_valid(i, CHUNK), o, 0.0).astype(o_ref.dtype)

    return pl.pallas_call(
        body, name="fox_fwd",
        out_shape=jax.ShapeDtypeStruct((l, GROUP), MXU_DTYPE),
        grid=(N_HEADS, n_blocks),
        in_specs=[pl.BlockSpec((CHUNK, HEAD_DIM), lambda h, i: (i, qt + h)),
                  pl.BlockSpec((l, HEAD_DIM), lambda h, i: (0, kt + h)),
                  pl.BlockSpec((l, HEAD_DIM), lambda h, i: (0, vt + h)),
                  pl.BlockSpec((None, CHUNK, 128), lambda h, i: (h, i, 0)),
                  pl.BlockSpec((None, 1, l), lambda h, i: (h, 0, 0))],
        out_specs=pl.BlockSpec((CHUNK, HEAD_DIM), lambda h, i: (i, h)),
        compiler_params=_params(("parallel", "parallel")),
    )(proj, proj, proj, cum_bc, cum_rows)


def _fox_bwd(proj, cum_bc, cum_rows, d_mix):
    l = proj.shape[0]
    n_blocks = l // CHUNK
    scale = HEAD_DIM ** -0.5
    qt, kt, vt = 4 * N_HEADS, 5 * N_HEADS, 6 * N_HEADS

    def body(q_ref, k_ref, v_ref, do_ref, ck_ref, cq_ref, dq_ref, dk_ref, dv_ref, ds_ref, dk_acc, dv_acc):
        i = pl.program_id(1)

        @pl.when(i == 0)
        def _():
            dk_acc[...] = jnp.zeros_like(dk_acc)
            dv_acc[...] = jnp.zeros_like(dv_acc)
            ds_ref[...] = jnp.zeros_like(ds_ref)

        qb = q_ref[...].astype(MXU_DTYPE)
        kb = k_ref[...].astype(MXU_DTYPE)
        vb = v_ref[...].astype(MXU_DTYPE)
        dob = jnp.where(_row_valid(i, CHUNK), do_ref[...], 0.0).astype(MXU_DTYPE)
        k_pos = lax.broadcasted_iota(jnp.int32, (l, CHUNK), 0)
        q_pos = i * CHUNK + lax.broadcasted_iota(jnp.int32, (l, CHUNK), 1)
        s_t = _dot_nt(kb, qb) * scale + (cq_ref[...] - ck_ref[...])
        s_t = jnp.where(k_pos <= q_pos, s_t, NEG_BIG)
        m = jnp.max(s_t, axis=0, keepdims=True)
        e = jnp.exp(s_t - m)
        p_t = e * (1.0 / jnp.sum(e, axis=0, keepdims=True))
        dp_t = _dot_nt(vb, dob)
        delta = jnp.sum(p_t * dp_t, axis=0, keepdims=True)
        ds_t = p_t * (dp_t - delta)
        ds_b = ds_t.astype(MXU_DTYPE)
        dv_acc[...] += _dot(p_t.astype(MXU_DTYPE), dob)
        dk_acc[...] += _dot(ds_b, qb) * scale
        ds_ref[...] += ds_t
        dq_ref[...] = (_dot_tn(ds_b, kb) * scale).astype(dq_ref.dtype)

        @pl.when(i == n_blocks - 1)
        def _():
            dk_ref[...] = dk_acc[...].astype(dk_ref.dtype)
            dv_ref[...] = dv_acc[...].astype(dv_ref.dtype)

    col = jax.ShapeDtypeStruct((l, GROUP), MXU_DTYPE)
    return pl.pallas_call(
        body, name="fox_bwd",
        out_shape=(col, col, col, jax.ShapeDtypeStruct((N_HEADS, l, 128), F32)),
        grid=(N_HEADS, n_blocks),
        in_specs=[pl.BlockSpec((CHUNK, HEAD_DIM), lambda h, i: (i, qt + h)),
                  pl.BlockSpec((l, HEAD_DIM), lambda h, i: (0, kt + h)),
                  pl.BlockSpec((l, HEAD_DIM), lambda h, i: (0, vt + h)),
                  pl.BlockSpec((CHUNK, HEAD_DIM), lambda h, i: (i, N_HEADS + h)),
                  pl.BlockSpec((None, l, 128), lambda h, i: (h, 0, 0)),
                  pl.BlockSpec((None, 1, CHUNK), lambda h, i: (h, 0, i))],
        out_specs=(pl.BlockSpec((CHUNK, HEAD_DIM), lambda h, i: (i, h)),
                   pl.BlockSpec((l, HEAD_DIM), lambda h, i: (0, h)),
                   pl.BlockSpec((l, HEAD_DIM), lambda h, i: (0, h)),
                   pl.BlockSpec((None, l, 128), lambda h, i: (h, 0, 0))),
        scratch_shapes=[pltpu.VMEM((l, HEAD_DIM), F32), pltpu.VMEM((l, HEAD_DIM), F32)],
        compiler_params=_params(("parallel", "arbitrary")),
    )(proj, proj, proj, d_mix, cum_bc, cum_rows)


def _fox_gate_bwd(ds_sum, proj, bias_row):
    l = proj.shape[0]
    n_blocks = l // CHUNK

    def body(ds_ref, ff_ref, b_ref, dff_ref, db_ref):
        r = lax.broadcasted_iota(jnp.int32, (CHUNK, CHUNK), 0)
        cidx = lax.broadcasted_iota(jnp.int32, (CHUNK, CHUNK), 1)
        upper = jnp.where(cidx >= r, 1.0, 0.0).astype(F32)
        carry = jnp.zeros((1, 128), F32)
        db = jnp.zeros((1, 128), F32)
        for blk in reversed(range(n_blocks)):
            rows = slice(blk * CHUNK, (blk + 1) * CHUNK)
            key_sum = jnp.zeros((CHUNK, 128), F32)
            for h in range(N_HEADS):
                select = jnp.where(cidx == h, 1.0, 0.0).astype(F32)
                key_sum = key_sum + jnp.dot(ds_ref[h, rows, :], select, precision=lax.Precision.HIGHEST,
                                            preferred_element_type=F32)
            suffix = jnp.dot(upper, key_sum, precision=lax.Precision.HIGHEST, preferred_element_type=F32) + carry
            carry = suffix[0:1, :]
            _, dsig = _log_forget(ff_ref[rows, :], b_ref[...], _row_valid(blk, CHUNK))
            dff = -suffix * dsig
            dff_ref[rows, :] = dff.astype(dff_ref.dtype)
            db = db + jnp.sum(dff, axis=0, keepdims=True)
        db_ref[...] = db

    return pl.pallas_call(
        body, name="fox_gate_bwd",
        out_shape=(jax.ShapeDtypeStruct((l, 128), MXU_DTYPE), jax.ShapeDtypeStruct((1, 128), F32)),
        grid=(1,),
        in_specs=[pl.BlockSpec((N_HEADS, l, 128), lambda i: (0, 0, 0)),
                  pl.BlockSpec((l, 128), lambda i: (0, FF_TILE)),
                  pl.BlockSpec((1, 128), lambda i: (0, 0))],
        out_specs=(pl.BlockSpec((l, 128), lambda i: (0, 0)), pl.BlockSpec((1, 128), lambda i: (0, 0))),
        compiler_params=_params(("arbitrary",)),
    )(ds_sum, proj, bias_row)


def _conv(u, w, b):
    return b + w[0:1, :] * pltpu.roll(u, 2, 0) + w[1:2, :] * pltpu.roll(u, 1, 0) + w[2:3, :] * u


def _conv_act_fwd(u, conv_w, conv_b, d_ff):
    l = u.shape[0]
    tc = _divisor_tile(d_ff, 256, 128)
    nt = d_ff // tc

    def body(ug_ref, uv_ref, wg_ref, wv_ref, bg_ref, bv_ref, a_ref):
        yg = _conv(ug_ref[...], wg_ref[...], bg_ref[...])
        yv = _conv(uv_ref[...], wv_ref[...], bv_ref[...])
        act = yg * _sigmoid(yg) * yv
        a_ref[...] = jnp.where(_row_valid(0, l), act, 0.0).astype(a_ref.dtype)

    return pl.pallas_call(
        body, name="conv_act_fwd",
        out_shape=jax.ShapeDtypeStruct((l, d_ff), MXU_DTYPE),
        grid=(nt,),
        in_specs=[pl.BlockSpec((l, tc), lambda j: (0, j)), pl.BlockSpec((l, tc), lambda j: (0, j + nt)),
                  pl.BlockSpec((8, tc), lambda j: (0, j)), pl.BlockSpec((8, tc), lambda j: (0, j + nt)),
                  pl.BlockSpec((1, tc), lambda j: (0, j)), pl.BlockSpec((1, tc), lambda j: (0, j + nt))],
        out_specs=pl.BlockSpec((l, tc), lambda j: (0, j)),
        compiler_params=_params(("parallel",)),
    )(u, u, conv_w, conv_w, conv_b, conv_b)


def _conv_act_bwd(u, conv_w, conv_b, d_act, d_ff):
    l = u.shape[0]
    tc = _divisor_tile(d_ff, 256, 128)
    nt = d_ff // tc

    def body(ug_ref, uv_ref, wg_ref, wv_ref, bg_ref, bv_ref, da_ref, du_ref, dwb_ref):
        valid = _row_valid(0, l)
        ug, uv = ug_ref[...], uv_ref[...]
        wg, wv = wg_ref[...], wv_ref[...]
        yg = _conv(ug, wg, bg_ref[...])
        yv = _conv(uv, wv, bv_ref[...])
        sig = _sigmoid(yg)
        da = jnp.where(valid, da_ref[...], 0.0)
        d_yv = da * (yg * sig)
        d_yg = da * yv * (sig * (1.0 + yg * (1.0 - sig)))
        for idx, (dy, uu, w) in enumerate(((d_yg, ug, wg), (d_yv, uv, wv))):
            du = w[2:3, :] * dy + w[1:2, :] * pltpu.roll(dy, l - 1, 0) + w[0:1, :] * pltpu.roll(dy, l - 2, 0)
            du_ref[idx] = jnp.where(valid, du, 0.0).astype(du_ref.dtype)
            dwb_ref[idx, 0:1, :] = jnp.sum(dy * pltpu.roll(uu, 2, 0), axis=0, keepdims=True)
            dwb_ref[idx, 1:2, :] = jnp.sum(dy * pltpu.roll(uu, 1, 0), axis=0, keepdims=True)
            dwb_ref[idx, 2:3, :] = jnp.sum(dy * uu, axis=0, keepdims=True)
            dwb_ref[idx, 3:4, :] = jnp.sum(dy, axis=0, keepdims=True)
            dwb_ref[idx, 4:8, :] = jnp.zeros((4, tc), F32)

    return pl.pallas_call(
        body, name="conv_act_bwd",
        out_shape=(jax.ShapeDtypeStruct((2, l, d_ff), MXU_DTYPE), jax.ShapeDtypeStruct((2, 8, d_ff), F32)),
        grid=(nt,),
        in_specs=[pl.BlockSpec((l, tc), lambda j: (0, j)), pl.BlockSpec((l, tc), lambda j: (0, j + nt)),
                  pl.BlockSpec((8, tc), lambda j: (0, j)), pl.BlockSpec((8, tc), lambda j: (0, j + nt)),
                  pl.BlockSpec((1, tc), lambda j: (0, j)), pl.BlockSpec((1, tc), lambda j: (0, j + nt)),
                  pl.BlockSpec((l, tc), lambda j: (0, j))],
        out_specs=(pl.BlockSpec((2, l, tc), lambda j: (0, 0, j)), pl.BlockSpec((2, 8, tc), lambda j: (0, 0, j))),
        compiler_params=_params(("parallel",)),
    )(u, u, conv_w, conv_w, conv_b, conv_b, d_act)


def _adamw(w, g, m, v, name):
    shape = w.shape
    if w.ndim == 1:
        as2d = (1, shape[0])
    else:
        as2d = (int(np.prod(shape[:-1])), shape[-1])
    r, c = as2d
    tr = _divisor_tile(r, 256, 8)
    spec = pl.BlockSpec((tr, c), lambda i: (i, 0))

    def body(w_ref, g_ref, m_ref, v_ref, d_ref, nm_ref, nv_ref):
        gv = g_ref[...]
        nm = ADAM_B1 * m_ref[...] + (1.0 - ADAM_B1) * gv
        nv = ADAM_B2 * v_ref[...] + (1.0 - ADAM_B2) * (gv * gv)
        m_hat = nm / (1.0 - ADAM_B1 ** ADAM_STEP)
        v_hat = nv / (1.0 - ADAM_B2 ** ADAM_STEP)
        d_ref[...] = -ADAM_LR * (m_hat / (jnp.sqrt(v_hat) + ADAM_EPS) + ADAM_WD * w_ref[...])
        nm_ref[...] = nm
        nv_ref[...] = nv

    sds = jax.ShapeDtypeStruct(as2d, F32)
    outs = pl.pallas_call(
        body, name=name, out_shape=(sds, sds, sds), grid=(r // tr,),
        in_specs=[spec] * 4, out_specs=(spec,) * 3,
        compiler_params=_params(("parallel",)),
    )(w.reshape(as2d), g.reshape(as2d), m.reshape(as2d), v.reshape(as2d))
    return tuple(o.reshape(shape) for o in outs)


def _pad_rows(a, rows):
    return jnp.pad(a, ((0, rows - a.shape[0]), (0, 0)))


def kernel(x, meta_tokens, norm1_gain, w_in, b_forget, ret_norm_gain, w_out, norm2_gain, w_up, conv_w, conv_b, w_down, final_norm_gain, loss_target, m_meta_tokens, m_norm1_gain, m_w_in, m_b_forget, m_ret_norm_gain, m_w_out, m_norm2_gain, m_w_up, m_conv_w, m_conv_b, m_w_down, m_final_norm_gain, v_meta_tokens, v_norm1_gain, v_w_in, v_b_forget, v_ret_norm_gain, v_w_out, v_norm2_gain, v_w_up, v_conv_w, v_conv_b, v_w_down, v_final_norm_gain):
    seq, d = x.shape[1], x.shape[2]
    l = CHUNK + seq
    d_ff = w_down.shape[1] * N_DEV
    up_shard = w_up.shape[2]
    assert 4 * up_shard == d_ff and w_in.shape[2] == WIN_SHARD and d == 2 * GROUP
    dev = _device_index()
    mx, my, mc = _my_position()
    core = jnp.reshape(mc, (1,)).astype(jnp.int32)
    chip = jnp.reshape(2 * mx + my, (1,)).astype(jnp.int32)

    w_in_window = lax.dynamic_update_slice(
        jnp.zeros((d, WIN_BLOCK), WIRE_DTYPE), w_in[0].astype(WIRE_DTYPE), (jnp.int32(0), dev.astype(jnp.int32)))
    w_in_full = _assemble_w_in(_all_gather(w_in_window, "gather_w_in")).astype(MXU_DTYPE)
    w_out_full = _all_gather(w_out[0].astype(WIRE_DTYPE), "gather_w_out").reshape(d, d).astype(MXU_DTYPE)
    w_up_blocks = _all_gather(w_up[0].astype(WIRE_DTYPE), "gather_w_up").astype(MXU_DTYPE)
    w_down_full = _all_gather(w_down[0].astype(WIRE_DTYPE), "gather_w_down").reshape(d_ff, d).astype(MXU_DTYPE)
    conv_w_full = jnp.transpose(_all_gather(_pad_rows(conv_w[0], 8), "gather_conv_w"), (1, 0, 2)).reshape(8, 2 * d_ff)
    meta_full = jnp.transpose(_all_gather(meta_tokens, "gather_meta"), (1, 0, 2)).reshape(N_META, d)

    h0 = jnp.concatenate([jnp.zeros((PAD_ROWS, d), F32), meta_full, x[0]], axis=0)
    consts = _retention_consts(l)
    bias_row = jnp.pad(b_forget, ((0, 0), (0, 128 - N_HEADS)))
    a = _rmsnorm_fwd(h0, norm1_gain, "rmsnorm1")
    proj = _mm_nn(a, w_in_full, F32, "mm_proj")
    ret_mix, ret_pre, ret_states = _retention_fwd(proj, ret_norm_gain, consts)
    cum_bc, cum_rows = _fox_prep(proj, bias_row)
    fox_mix = _fox_fwd(proj, cum_bc, cum_rows)
    mix = jnp.concatenate([ret_mix, fox_mix], axis=1)
    h1, cn = _rmsnorm_fwd(h0, norm2_gain, "resid_rmsnorm2", res=_mm_nn(mix, w_out_full, F32, "mm_out"))
    u = _mm(cn, w_up_blocks,
            a_spec=pl.BlockSpec((_divisor_tile(l, 1088, 16), d), lambda i, j, k: (i, 0)),
            b_spec=pl.BlockSpec((None, d, up_shard), lambda i, j, k: (j, 0, 0)),
            o_spec=pl.BlockSpec((_divisor_tile(l, 1088, 16), up_shard), lambda i, j, k: (i, j)),
            out_shape=jax.ShapeDtypeStruct((l, 2 * d_ff), F32),
            grid=(l // _divisor_tile(l, 1088, 16), N_DEV, 1), contract=(1, 0), nk=1, name="mm_up")
    conv_b_row = conv_b
    act = _conv_act_fwd(u, conv_w_full, conv_b_row, d_ff)
    mlp_out = _mm_nn(act, w_down_full, F32, "mm_down", tk_cap=1408)
    d_h2, d_h2_b, dg_final, loss_part = _loss_head(h1, mlp_out, final_norm_gain.reshape(1, d), loss_target[0])

    d_act = _mm_nt(d_h2_b, w_down_full, F32, "mm_d_act")
    gw_down = _mm_tn(act, d_h2_b, WIRE_DTYPE, "mm_gw_down", tm_cap=1408, tn_cap=1024)
    d_u, d_conv = _conv_act_bwd(u, conv_w_full, conv_b_row, d_act, d_ff)
    tm = _divisor_tile(l, 1088, 16)
    d_cn = _mm(d_u, w_up_blocks,
               a_spec=pl.BlockSpec((None, tm, up_shard), lambda i, j, k: (k // 4, i, k % 4)),
               b_spec=pl.BlockSpec((None, d // 2, up_shard), lambda i, j, k: (k, j, 0)),
               o_spec=pl.BlockSpec((tm, d // 2), lambda i, j, k: (i, j)),
               out_shape=jax.ShapeDtypeStruct((l, d), F32),
               grid=(l // tm, 2, N_DEV), contract=(1, 1), nk=N_DEV, name="mm_d_cn")
    gw_up = _mm(cn, d_u,
                a_spec=pl.BlockSpec((l, d // 2), lambda i, j, k: (0, i)),
                b_spec=pl.BlockSpec((None, l, up_shard), lambda i, j, k: (j // 4, 0, j % 4)),
                o_spec=pl.BlockSpec((None, d // 2, up_shard), lambda i, j, k: (j, i, 0)),
                out_shape=jax.ShapeDtypeStruct((N_DEV, d, up_shard), WIRE_DTYPE),
                grid=(2, N_DEV, 1), contract=(0, 0), nk=1, name="mm_gw_up")
    d_h1, d_h1_b, dg_norm2 = _rmsnorm_bwd(d_h2, d_cn, h1, norm2_gain, "rmsnorm2_bwd", True)

    d_mix = _mm_nt(d_h1_b, w_out_full, F32, "mm_d_mix")
    gw_out = _mm_tn(mix, d_h1_b, WIRE_DTYPE, "mm_gw_out")
    d_fq, d_fk, d_fv, ds_sum = _fox_bwd(proj, cum_bc, cum_rows, d_mix)
    d_ff_tile, db_forget_row = _fox_gate_bwd(ds_sum, proj, bias_row)
    d_ret, dg_ret = _retention_bwd(proj, ret_pre, ret_states, d_mix, ret_norm_gain, consts)
    d_proj = jnp.concatenate(
        [d_ret, d_fq, d_fk, d_fv, d_ff_tile, jnp.zeros((l, WIN_N - 7 * GROUP - 128), MXU_DTYPE)], axis=1)
    d_a = _mm_nt(d_proj, w_in_full, F32, "mm_d_a", tk_cap=1536)
    gw_in = _mm_tn(a, d_proj, WIRE_DTYPE, "mm_gw_in")
    d_h0, dg_norm1 = _rmsnorm_bwd(d_h1, d_a, h0, norm1_gain, "rmsnorm1_bwd", False)
    grad_x = d_h0[CHUNK:][None]
    d_meta = d_h0[PAD_ROWS:CHUNK]

    g_w_in_window = _reduce_scatter(_extract_w_in_windows(gw_in), core, chip, "rs_w_in")
    g_w_in = lax.dynamic_slice(g_w_in_window, (jnp.int32(0), dev.astype(jnp.int32)), (d, WIN_SHARD))[None]
    g_w_out = _reduce_scatter(gw_out.reshape(N_DEV, d // N_DEV, d), core, chip, "rs_w_out")[None]
    g_w_up = _reduce_scatter(gw_up, core, chip, "rs_w_up")[None]
    g_w_down = _reduce_scatter(gw_down.reshape(N_DEV, d_ff // N_DEV, d), core, chip, "rs_w_down")[None]

    d_conv_w = jnp.concatenate([d_conv[0, 0:3], d_conv[1, 0:3]], axis=1)
    d_conv_b = jnp.concatenate([d_conv[0, 3:4], d_conv[1, 3:4]], axis=1)
    pieces = [loss_part[:, 0:1], dg_norm1, db_forget_row[:, 0:N_HEADS], dg_ret, dg_norm2, d_conv_b, dg_final,
              d_meta.reshape(1, -1), d_conv_w.reshape(1, -1)]
    sizes = [p.shape[1] for p in pieces]
    flat = jnp.concatenate(pieces, axis=1)
    padded = -(-flat.shape[1] // 1024) * 1024
    flat = jnp.pad(flat, ((0, 0), (0, padded - flat.shape[1]))).reshape(padded // 128, 128)
    total = _small_all_reduce(flat, "all_reduce_small").reshape(1, padded)
    offs = np.concatenate([[0], np.cumsum(sizes)])
    take = lambda k: total[:, int(offs[k]):int(offs[k + 1])]
    loss = take(0).reshape(())
    g_norm1, g_bf, g_ret_gain, g_norm2 = take(1), take(2), take(3), take(4)
    g_conv_b, g_final = take(5), take(6).reshape(d)
    g_meta = lax.dynamic_slice(take(7).reshape(N_META, d), (jnp.int32(0), (dev * (d // N_DEV)).astype(jnp.int32)),
                               (N_META, d // N_DEV))
    g_conv_w = lax.dynamic_slice(take(8).reshape(3, 2 * d_ff), (jnp.int32(0), (dev * up_shard).astype(jnp.int32)),
                                 (3, up_shard))[None]

    weights = [meta_tokens, norm1_gain, w_in, b_forget, ret_norm_gain, w_out, norm2_gain, w_up, conv_w, conv_b,
               w_down, final_norm_gain]
    grads = [g_meta, g_norm1, g_w_in, g_bf, g_ret_gain, g_w_out, g_norm2, g_w_up, g_conv_w, g_conv_b, g_w_down,
             g_final]
    ms = [m_meta_tokens, m_norm1_gain, m_w_in, m_b_forget, m_ret_norm_gain, m_w_out, m_norm2_gain, m_w_up, m_conv_w,
          m_conv_b, m_w_down, m_final_norm_gain]
    vs = [v_meta_tokens, v_norm1_gain, v_w_in, v_b_forget, v_ret_norm_gain, v_w_out, v_norm2_gain, v_w_up, v_conv_w,
          v_conv_b, v_w_down, v_final_norm_gain]
    names = ["meta", "norm1", "w_in", "b_forget", "ret_gain", "w_out", "norm2", "w_up", "conv_w", "conv_b", "w_down",
             "final_gain"]
    deltas, new_ms, new_vs = [], [], []
    for w, g, m, v, n in zip(weights, grads, ms, vs, names):
        dl, nm, nv = _adamw(w, g, m, v, "adamw_" + n)
        deltas.append(dl)
        new_ms.append(nm)
        new_vs.append(nv)
    return (loss, grad_x, *grads, *deltas, *new_ms, *new_vs)
```

```python
import functools

import numpy as np
import jax
import jax.numpy as jnp
from jax import lax
from jax.experimental import pallas as pl
from jax.experimental.pallas import tpu as pltpu

F32 = jnp.float32
MXU_DTYPE = jnp.bfloat16
WIRE_DTYPE = jnp.bfloat16

N_DEV = 8
N_META = 16
CHUNK = 128
PAD_ROWS = CHUNK - N_META
N_HEADS = 8
HEAD_DIM = 128
GROUP = N_HEADS * HEAD_DIM
IN_DIM = 7 * GROUP + N_HEADS
WIN_SHARD = IN_DIM // N_DEV
WIN_BLOCK = 1024
WIN_STRIDE = 896
WIN_N = 7680
ROPE_BASE = 10000.0
NORM_EPS = 1e-6
NEG_BIG = -1e30
ADAM_LR, ADAM_B1, ADAM_B2, ADAM_EPS, ADAM_WD, ADAM_STEP = 0.001, 0.9, 0.999, 1e-08, 0.01, 10
VMEM_LIMIT = 52 * 1024 * 1024
MESH = pl.DeviceIdType.MESH
ANY = pl.BlockSpec(memory_space=pl.ANY)
VMEM_SPEC = pl.BlockSpec(memory_space=pltpu.VMEM)


def _params(sem=None):
    kw = {"vmem_limit_bytes": VMEM_LIMIT}
    if sem is not None:
        kw["dimension_semantics"] = sem
    return pltpu.CompilerParams(**kw)


def _divisor_tile(n, cap, unit):
    if n <= cap:
        return n
    best = None
    for t in range(unit, cap + 1, unit):
        if n % t == 0:
            best = t
    assert best is not None, (n, cap, unit)
    return best


def _my_position():
    return lax.axis_index("x"), lax.axis_index("y"), lax.axis_index("c")


def _device_index():
    x, y, c = _my_position()
    return 4 * x + 2 * y + c


def _all_gather(shard, name):
    r, c = shard.shape

    def body(x_ref, out_ref, send_sems, recv_sems, local_sem):
        mx, my, mc = _my_position()
        me, sibling = (mx, my, mc), (mx, my, 1 - mc)
        chips = [(1 - mx, my), (mx, 1 - my), (1 - mx, 1 - my)]

        def slot(px, py, pc):
            return out_ref.at[4 * px + 2 * py + pc]

        def copy(k, block, to, src=None):
            return pltpu.make_async_remote_copy(
                src_ref=slot(*block) if src is None else src, dst_ref=slot(*block),
                send_sem=send_sems.at[k], recv_sem=recv_sems.at[k], device_id=to, device_id_type=MESH)

        mine = pltpu.make_async_copy(x_ref, slot(*me), local_sem)
        mine.start()
        first = [copy(0, me, sibling, src=x_ref)]
        first += [copy(1 + j, me, (*chip, mc), src=x_ref) for j, chip in enumerate(chips)]
        for cp in first:
            cp.start()
        passed = [copy(4 + j, (*chip, mc), sibling) for j, chip in enumerate(chips)]
        for j, chip in enumerate(chips):
            copy(1 + j, (*chip, mc), me).wait_recv()
            passed[j].start()
        copy(0, sibling, me).wait_recv()
        for j, chip in enumerate(chips):
            copy(4 + j, (*chip, 1 - mc), me).wait_recv()
        for cp in first + passed:
            cp.wait_send()
        mine.wait()

    return pl.pallas_call(
        body, name=name,
        out_shape=jax.ShapeDtypeStruct((N_DEV, r, c), shard.dtype),
        in_specs=[ANY], out_specs=ANY,
        scratch_shapes=[pltpu.SemaphoreType.DMA((7,)), pltpu.SemaphoreType.DMA((7,)), pltpu.SemaphoreType.DMA],
    )(shard)


HBM_SPEC = pl.BlockSpec(memory_space=pltpu.HBM)
SEM_SPEC = pl.BlockSpec(memory_space=pltpu.SEMAPHORE)
DATAFLOW_EFFECT = pltpu.SideEffectType.DATAFLOW_SIDE_EFFECTING


def _in_hbm(a):
    return pltpu.with_memory_space_constraint(a, pltpu.HBM)


def _split_start(src, land_shape, make_copies, n_copies, name):
    def body(src_ref, land_ref, send_sems, recv_sems, src_thru, land_thru, token):
        for cp in make_copies(src_ref, land_ref, send_sems, recv_sems):
            cp.start()
        token[...] = jnp.zeros_like(token)

    return pl.pallas_call(
        body, name=name,
        out_shape=(pltpu.SemaphoreType.DMA((n_copies,)), pltpu.SemaphoreType.DMA((n_copies,)),
                   pltpu.HBM(src.shape, src.dtype), pltpu.HBM(land_shape, src.dtype),
                   jax.ShapeDtypeStruct((8, 128), F32)),
        in_specs=(HBM_SPEC, HBM_SPEC), out_specs=(SEM_SPEC, SEM_SPEC, HBM_SPEC, HBM_SPEC, VMEM_SPEC),
        input_output_aliases={0: 2, 1: 3},
        compiler_params=pltpu.CompilerParams(has_side_effects=DATAFLOW_EFFECT),
    )(_in_hbm(src), _in_hbm(lax.empty(land_shape, src.dtype)))


def _split_wait(started, after, make_copies, name):
    send_sems, recv_sems, src_thru, land_thru, _ = started

    def body(src_ref, land_ref, send_sems_ref, recv_sems_ref, after_ref, src_dead, land_out):
        for cp in make_copies(src_ref, land_ref, send_sems_ref, recv_sems_ref):
            cp.wait_send()
            cp.wait_recv()

    return pl.pallas_call(
        body, name=name,
        out_shape=(pltpu.HBM(src_thru.shape, src_thru.dtype), pltpu.HBM(land_thru.shape, land_thru.dtype)),
        in_specs=(HBM_SPEC, HBM_SPEC, SEM_SPEC, SEM_SPEC, ANY), out_specs=(HBM_SPEC, HBM_SPEC),
        input_output_aliases={0: 0, 1: 1},
        compiler_params=pltpu.CompilerParams(has_side_effects=DATAFLOW_EFFECT),
    )(src_thru, land_thru, send_sems, recv_sems, after)


def _gather_copies(x_ref, land_ref, send_sems, recv_sems):
    mx, my, mc = _my_position()
    me = 4 * mx + 2 * my + mc
    targets = [(mx, my, 1 - mc), (1 - mx, my, mc), (mx, 1 - my, mc), (1 - mx, 1 - my, mc)]
    return [pltpu.make_async_remote_copy(
        src_ref=x_ref, dst_ref=land_ref.at[me], send_sem=send_sems.at[k], recv_sem=recv_sems.at[k],
        device_id=t, device_id_type=MESH) for k, t in enumerate(targets)]


def _gather_start(shard, name):
    return _split_start(shard, (N_DEV,) + shard.shape, _gather_copies, 4, name)


def _gather_finish(started, after, name):
    shard, land = _split_wait(started, after, _gather_copies, name + "_wait")

    def body(land_in, x_ref, land_ref, send_sems, recv_sems, local_sem):
        mx, my, mc = _my_position()
        chips = [(1 - mx, my), (mx, 1 - my), (1 - mx, 1 - my)]
        mine = pltpu.make_async_copy(x_ref, land_ref.at[4 * mx + 2 * my + mc], local_sem)
        mine.start()
        copies = [pltpu.make_async_remote_copy(
            src_ref=land_ref.at[4 * cx + 2 * cy + mc], dst_ref=land_ref.at[4 * cx + 2 * cy + mc],
            send_sem=send_sems.at[j], recv_sem=recv_sems.at[j],
            device_id=(mx, my, 1 - mc), device_id_type=MESH) for j, (cx, cy) in enumerate(chips)]
        for cp in copies:
            cp.start()
        for j, (cx, cy) in enumerate(chips):
            copies[j].wait_send()
            pltpu.make_async_remote_copy(
                src_ref=land_ref.at[4 * cx + 2 * cy + 1 - mc], dst_ref=land_ref.at[4 * cx + 2 * cy + 1 - mc],
                send_sem=send_sems.at[j], recv_sem=recv_sems.at[j],
                device_id=(mx, my, 1 - mc), device_id_type=MESH).wait_recv()
        mine.wait()

    return pl.pallas_call(
        body, name=name + "_pass",
        out_shape=jax.ShapeDtypeStruct(land.shape, land.dtype),
        in_specs=[ANY, ANY], out_specs=ANY,
        input_output_aliases={0: 0},
        scratch_shapes=[pltpu.SemaphoreType.DMA((3,)), pltpu.SemaphoreType.DMA((3,)), pltpu.SemaphoreType.DMA],
    )(land, shard)


def _chip_copies(p_ref, land_ref, send_sems, recv_sems):
    mx, my, mc = _my_position()
    chips = [(1 - mx, my), (mx, 1 - my), (1 - mx, 1 - my)]
    return [pltpu.make_async_remote_copy(
        src_ref=p_ref.at[2 * cx + cy], dst_ref=land_ref.at[j], send_sem=send_sems.at[j], recv_sem=recv_sems.at[j],
        device_id=(cx, cy, mc), device_id_type=MESH) for j, (cx, cy) in enumerate(chips)]


def _reduce_scatter_start(g, core, name):
    pair = _pair_sum(g, _exchange_sibling(g, name + "_d2d"), core, name + "_pairsum")
    return _split_start(pair, (3,) + pair.shape[1:], _chip_copies, 3, name + "_ici_start")


def _reduce_scatter_finish(started, after, chip, name):
    pair, from_chips = _split_wait(started, after, _chip_copies, name + "_ici_wait")
    return _final_sum(pair, from_chips, chip, name + "_sum")


def _exchange_sibling(g, name):
    _, r, c = g.shape

    def body(g_ref, out_ref, send_sems, recv_sems):
        mx, my, mc = _my_position()
        copies = [
            pltpu.make_async_remote_copy(
                src_ref=g_ref.at[2 * k + (1 - mc)], dst_ref=out_ref.at[k],
                send_sem=send_sems.at[k], recv_sem=recv_sems.at[k],
                device_id=(mx, my, 1 - mc), device_id_type=MESH)
            for k in range(4)]
        for cp in copies:
            cp.start()
        for cp in copies:
            cp.wait()

    return pl.pallas_call(
        body, name=name,
        out_shape=jax.ShapeDtypeStruct((4, r, c), g.dtype),
        in_specs=[ANY], out_specs=ANY,
        scratch_shapes=[pltpu.SemaphoreType.DMA((4,)), pltpu.SemaphoreType.DMA((4,))],
    )(g)


def _pair_sum(g, recv, core, name):
    _, r, c = g.shape
    tr = _divisor_tile(r, 256, 16)

    def body(s_ref, g_ref, r_ref, o_ref):
        o_ref[...] = (g_ref[...].astype(F32) + r_ref[...].astype(F32)).astype(o_ref.dtype)

    return pl.pallas_call(
        body, name=name,
        out_shape=jax.ShapeDtypeStruct((4, r, c), g.dtype),
        grid_spec=pltpu.PrefetchScalarGridSpec(
            num_scalar_prefetch=1, grid=(4, r // tr),
            in_specs=[pl.BlockSpec((None, tr, c), lambda k, i, s: (2 * k + s[0], i, 0)),
                      pl.BlockSpec((None, tr, c), lambda k, i, s: (k, i, 0))],
            out_specs=pl.BlockSpec((None, tr, c), lambda k, i, s: (k, i, 0))),
        compiler_params=_params(("parallel", "parallel")),
    )(core, g, recv)


def _final_sum(p, recv, chip, name):
    _, r, c = p.shape
    tr = _divisor_tile(r, 256, 16)

    def body(s_ref, p_ref, r_ref, o_ref):
        acc = p_ref[...].astype(F32)
        for j in range(3):
            acc = acc + r_ref[j].astype(F32)
        o_ref[...] = acc

    return pl.pallas_call(
        body, name=name,
        out_shape=jax.ShapeDtypeStruct((r, c), F32),
        grid_spec=pltpu.PrefetchScalarGridSpec(
            num_scalar_prefetch=1, grid=(r // tr,),
            in_specs=[pl.BlockSpec((None, tr, c), lambda i, s: (s[0], i, 0)),
                      pl.BlockSpec((3, tr, c), lambda i, s: (0, i, 0))],
            out_specs=pl.BlockSpec((tr, c), lambda i, s: (i, 0))),
        compiler_params=_params(("parallel",)),
    )(chip, p, recv)


def _small_all_reduce(v, name):
    rows = v.shape[0]

    def body(v_ref, o_ref, gathered, send_sems, recv_sems):
        mx, my, mc = _my_position()
        me = 4 * mx + 2 * my + mc
        gathered[me] = v_ref[...]
        copies = []
        for rel in range(1, N_DEV):
            bx, by, bc = (rel >> 2) & 1, (rel >> 1) & 1, rel & 1
            target = (1 - mx if bx else mx, 1 - my if by else my, 1 - mc if bc else mc)
            cp = pltpu.make_async_remote_copy(
                src_ref=v_ref, dst_ref=gathered.at[me],
                send_sem=send_sems.at[rel - 1], recv_sem=recv_sems.at[rel - 1],
                device_id=target, device_id_type=MESH)
            cp.start()
            copies.append(cp)
        for cp in copies:
            cp.wait()
        acc = gathered[0]
        for j in range(1, N_DEV):
            acc = acc + gathered[j]
        o_ref[...] = acc

    return pl.pallas_call(
        body, name=name,
        out_shape=jax.ShapeDtypeStruct((rows, 128), F32),
        in_specs=[VMEM_SPEC], out_specs=VMEM_SPEC,
        scratch_shapes=[pltpu.VMEM((N_DEV, rows, 128), F32),
                        pltpu.SemaphoreType.DMA((7,)), pltpu.SemaphoreType.DMA((7,))],
        compiler_params=_params(),
    )(v)


def _assemble_w_in(blocks):
    _, d, _ = blocks.shape
    tr = _divisor_tile(d, 256, 16)
    n_tiles = WIN_N // 128
    last = (N_DEV * WIN_STRIDE) // 128

    def body(b_ref, o_ref):
        for t in range(n_tiles):
            if t > last:
                o_ref[:, t * 128:(t + 1) * 128] = jnp.zeros((tr, 128), o_ref.dtype)
                continue
            i = min(t // 7, N_DEV - 1)
            k = t - 7 * i
            val = b_ref[i, :, k * 128:(k + 1) * 128]
            if k == 0 and i >= 1:
                val = val + b_ref[i - 1, :, 7 * 128:8 * 128]
            o_ref[:, t * 128:(t + 1) * 128] = val

    return pl.pallas_call(
        body, name="assemble_w_in",
        out_shape=jax.ShapeDtypeStruct((d, WIN_N), blocks.dtype),
        grid=(d // tr,),
        in_specs=[pl.BlockSpec((N_DEV, tr, WIN_BLOCK), lambda i: (0, i, 0))],
        out_specs=pl.BlockSpec((tr, WIN_N), lambda i: (i, 0)),
        compiler_params=_params(("parallel",)),
    )(blocks)


def _extract_w_in_windows(g):
    d, _ = g.shape
    tr = _divisor_tile(d, 256, 16)

    def body(g_ref, o_ref):
        for j in range(N_DEV):
            o_ref[j] = g_ref[:, WIN_STRIDE * j:WIN_STRIDE * j + WIN_BLOCK]

    return pl.pallas_call(
        body, name="extract_w_in_windows",
        out_shape=jax.ShapeDtypeStruct((N_DEV, d, WIN_BLOCK), g.dtype),
        grid=(d // tr,),
        in_specs=[pl.BlockSpec((tr, WIN_N), lambda i: (i, 0))],
        out_specs=pl.BlockSpec((N_DEV, tr, WIN_BLOCK), lambda i: (0, i, 0)),
        compiler_params=_params(("parallel",)),
    )(g)


def _mm(a, b, *, a_spec, b_spec, o_spec, out_shape, grid, contract, nk, name):
    dn = (((contract[0],), (contract[1],)), ((), ()))
    tm, tn = o_spec.block_shape[-2:]

    def body(a_ref, b_ref, o_ref, *scratch):
        part = lax.dot_general(a_ref[...], b_ref[...], dn, preferred_element_type=F32)
        if nk == 1:
            o_ref[...] = part.astype(o_ref.dtype)
            return
        acc = scratch[0]
        k = pl.program_id(2)

        @pl.when(k == 0)
        def _():
            acc[...] = part

        @pl.when(k > 0)
        def _():
            acc[...] += part

        @pl.when(k == nk - 1)
        def _():
            o_ref[...] = acc[...].astype(o_ref.dtype)

    return pl.pallas_call(
        body, name=name, out_shape=out_shape, grid=grid,
        in_specs=[a_spec, b_spec], out_specs=o_spec,
        scratch_shapes=[] if nk == 1 else [pltpu.VMEM((tm, tn), F32)],
        compiler_params=_params(("parallel", "parallel", "arbitrary")),
    )(a, b)


def _mm_nn(a, b, out_dtype, name, tm_cap=1088, tn_cap=512, tk_cap=2048):
    m, k = a.shape
    _, n = b.shape
    tm, tn, tk = _divisor_tile(m, tm_cap, 16), _divisor_tile(n, tn_cap, 128), _divisor_tile(k, tk_cap, 128)
    return _mm(a, b,
               a_spec=pl.BlockSpec((tm, tk), lambda i, j, kk: (i, kk)),
               b_spec=pl.BlockSpec((tk, tn), lambda i, j, kk: (kk, j)),
               o_spec=pl.BlockSpec((tm, tn), lambda i, j, kk: (i, j)),
               out_shape=jax.ShapeDtypeStruct((m, n), out_dtype),
               grid=(m // tm, n // tn, k // tk), contract=(1, 0), nk=k // tk, name=name)


def _mm_nt(a, b, out_dtype, name, tm_cap=1088, tn_cap=512, tk_cap=2048):
    m, k = a.shape
    n, _ = b.shape
    tm, tn, tk = _divisor_tile(m, tm_cap, 16), _divisor_tile(n, tn_cap, 128), _divisor_tile(k, tk_cap, 128)
    return _mm(a, b,
               a_spec=pl.BlockSpec((tm, tk), lambda i, j, kk: (i, kk)),
               b_spec=pl.BlockSpec((tn, tk), lambda i, j, kk: (j, kk)),
               o_spec=pl.BlockSpec((tm, tn), lambda i, j, kk: (i, j)),
               out_shape=jax.ShapeDtypeStruct((m, n), out_dtype),
               grid=(m // tm, n // tn, k // tk), contract=(1, 1), nk=k // tk, name=name)


def _mm_tn(a, b, out_dtype, name, tm_cap=1024, tn_cap=512):
    l, m = a.shape
    _, n = b.shape
    tm, tn = _divisor_tile(m, tm_cap, 128), _divisor_tile(n, tn_cap, 128)
    return _mm(a, b,
               a_spec=pl.BlockSpec((l, tm), lambda i, j, kk: (0, i)),
               b_spec=pl.BlockSpec((l, tn), lambda i, j, kk: (0, j)),
               o_spec=pl.BlockSpec((tm, tn), lambda i, j, kk: (i, j)),
               out_shape=jax.ShapeDtypeStruct((m, n), out_dtype),
               grid=(m // tm, n // tn, 1), contract=(0, 0), nk=1, name=name)


def _row_tile(l):
    return _divisor_tile(l, 544, 8)


def _rmsnorm_fwd(h, gain, name, res=None):
    l, d = h.shape
    tr = _row_tile(l)
    row = pl.BlockSpec((tr, d), lambda i: (i, 0))
    vec = pl.BlockSpec((1, d), lambda i: (0, 0))

    def body(*refs):
        if res is None:
            h_ref, g_ref, n_ref = refs
            x = h_ref[...]
        else:
            h_ref, r_ref, g_ref, s_ref, n_ref = refs
            x = h_ref[...] + r_ref[...]
            s_ref[...] = x
        y = x * lax.rsqrt(jnp.mean(x * x, axis=-1, keepdims=True) + NORM_EPS)
        n_ref[...] = (y * g_ref[...]).astype(n_ref.dtype)

    normed = jax.ShapeDtypeStruct((l, d), MXU_DTYPE)
    if res is None:
        return pl.pallas_call(body, name=name, out_shape=normed, grid=(l // tr,), in_specs=[row, vec],
                              out_specs=row, compiler_params=_params(("parallel",)))(h, gain)
    return pl.pallas_call(body, name=name, out_shape=(jax.ShapeDtypeStruct((l, d), F32), normed),
                          grid=(l // tr,), in_specs=[row, row, vec], out_specs=(row, row),
                          compiler_params=_params(("parallel",)))(h, res, gain)


def _rmsnorm_bwd(d_res, d_normed, x, gain, name, with_mxu_copy):
    l, d = x.shape
    tr = _row_tile(l)
    row = pl.BlockSpec((tr, d), lambda i: (i, 0))
    vec = pl.BlockSpec((1, d), lambda i: (0, 0))

    def body(dres_ref, dn_ref, x_ref, g_ref, dx_ref, *rest):
        dg_ref = rest[-1]
        xv = x_ref[...]
        r = lax.rsqrt(jnp.mean(xv * xv, axis=-1, keepdims=True) + NORM_EPS)
        xh = xv * r
        dn = dn_ref[...]
        dxh = dn * g_ref[...]
        dx = dres_ref[...] + r * (dxh - xh * jnp.mean(dxh * xh, axis=-1, keepdims=True))
        dx_ref[...] = dx
        if with_mxu_copy:
            rest[0][...] = dx.astype(MXU_DTYPE)

        @pl.when(pl.program_id(0) == 0)
        def _():
            dg_ref[...] = jnp.zeros_like(dg_ref)

        dg_ref[...] += jnp.sum(dn * xh, axis=0, keepdims=True)

    outs = [jax.ShapeDtypeStruct((l, d), F32)]
    specs = [row]
    if with_mxu_copy:
        outs.append(jax.ShapeDtypeStruct((l, d), MXU_DTYPE))
        specs.append(row)
    outs.append(jax.ShapeDtypeStruct((1, d), F32))
    specs.append(vec)
    return pl.pallas_call(body, name=name, out_shape=tuple(outs), grid=(l // tr,),
                          in_specs=[row, row, row, vec], out_specs=tuple(specs),
                          compiler_params=_params(("arbitrary",)))(d_res, d_normed, x, gain)


def _loss_head(h1, mlp_out, gain, target):
    l, d = h1.shape
    n_blocks = l // CHUNK
    row = pl.BlockSpec((CHUNK, d), lambda i: (i, 0))
    vec = pl.BlockSpec((1, d), lambda i: (0, 0))
    tgt = pl.BlockSpec((CHUNK, d), lambda i: (jnp.maximum(i - 1, 0), 0))

    def body(h_ref, m_ref, g_ref, t_ref, dh_ref, dhb_ref, dg_ref, loss_ref, sq_ref):
        i = pl.program_id(0)
        x = h_ref[...] + m_ref[...]
        r = lax.rsqrt(jnp.mean(x * x, axis=-1, keepdims=True) + NORM_EPS)
        xh = x * r
        g = g_ref[...]
        real = i >= 1
        err = jnp.where(real, xh * g - t_ref[...], 0.0)
        dy = err * (1.0 / d)
        dxh = dy * g
        dh = r * (dxh - xh * jnp.mean(dxh * xh, axis=-1, keepdims=True))
        dh_ref[...] = dh
        dhb_ref[...] = dh.astype(MXU_DTYPE)

        @pl.when(i == 0)
        def _():
            dg_ref[...] = jnp.zeros_like(dg_ref)
            sq_ref[...] = jnp.zeros_like(sq_ref)

        dg_ref[...] += jnp.sum(dy * xh, axis=0, keepdims=True)
        sq_ref[...] += jnp.sum(err * err, axis=0, keepdims=True)

        @pl.when(i == n_blocks - 1)
        def _():
            total = jnp.sum(sq_ref[...], axis=-1, keepdims=True) * (0.5 / d)
            loss_ref[...] = jnp.broadcast_to(total, (1, 128))

    return pl.pallas_call(
        body, name="loss_head",
        out_shape=(jax.ShapeDtypeStruct((l, d), F32), jax.ShapeDtypeStruct((l, d), MXU_DTYPE),
                   jax.ShapeDtypeStruct((1, d), F32), jax.ShapeDtypeStruct((1, 128), F32)),
        grid=(n_blocks,), in_specs=[row, row, vec, tgt],
        out_specs=(row, row, vec, pl.BlockSpec((1, 128), lambda i: (0, 0))),
        scratch_shapes=[pltpu.VMEM((1, d), F32)],
        compiler_params=_params(("arbitrary",)),
    )(h1, mlp_out, gain, target)


def _dot(a, b):
    return jnp.dot(a, b, preferred_element_type=F32)


def _dot_nt(a, b):
    return lax.dot_general(a, b, (((1,), (1,)), ((), ())), preferred_element_type=F32)


def _dot_tn(a, b):
    return lax.dot_general(a, b, (((0,), (0,)), ((), ())), preferred_element_type=F32)


def _rope(t, cos2, sin2):
    return t * cos2 + pltpu.roll(t, HEAD_DIM // 2, 1) * sin2


def _rope_bwd(dr, cos2, sin2):
    return dr * cos2 + pltpu.roll(dr * sin2, HEAD_DIM // 2, 1)


def _sigmoid(x):
    return 1.0 / (1.0 + jnp.exp(-x))


def _row_valid(block, rows):
    r = block * CHUNK + lax.broadcasted_iota(jnp.int32, (rows, 1), 0)
    return r >= PAD_ROWS


def _retention_consts(l):
    pos = jnp.arange(l, dtype=F32) - PAD_ROWS
    inv_freq = 1.0 / (ROPE_BASE ** (jnp.arange(0, HEAD_DIM, 2, dtype=F32) / HEAD_DIM))
    ang = pos[:, None] * inv_freq[None, :]
    cos, sin = jnp.cos(ang), jnp.sin(ang)
    cos2 = jnp.concatenate([cos, cos], axis=-1)
    sin2 = jnp.concatenate([-sin, sin], axis=-1)
    log_g = jnp.log1p(-jnp.exp2(-5.0 - jnp.arange(N_HEADS, dtype=F32)))
    idx = jnp.arange(CHUNK, dtype=F32)
    diff = idx[:, None] - idx[None, :]
    decay = jnp.where(diff >= 0, jnp.exp(jnp.maximum(diff, 0.0)[None] * log_g[:, None, None]), 0.0)
    xi = jnp.exp((idx + 1.0)[None, :] * log_g[:, None])
    zeta = jnp.exp((CHUNK - 1.0 - idx)[None, :] * log_g[:, None])
    g_chunk = jnp.exp(CHUNK * log_g)
    bcast = lambda v: jnp.broadcast_to(v[:, :, None], (N_HEADS, CHUNK, HEAD_DIM))
    g_rows = jnp.broadcast_to(g_chunk[:, None, None], (N_HEADS, 8, HEAD_DIM))
    return cos2, sin2, decay, bcast(xi), bcast(zeta), g_rows


def _retention_fwd(proj, ret_gain, consts):
    l = proj.shape[0]
    n_chunks = l // CHUNK
    cos2, sin2, decay, xi, zeta, g_rows = consts
    scale = HEAD_DIM ** -0.5

    def body(p_ref, cos_ref, sin_ref, dec_ref, xi_ref, zeta_ref, gr_ref, gain_ref,
             mix_ref, o_ref, st_ref, state):
        c = pl.program_id(0)

        @pl.when(c == 0)
        def _():
            state[...] = jnp.zeros_like(state)

        cos_v, sin_v = cos_ref[...], sin_ref[...]
        valid = _row_valid(c, CHUNK)
        for h in range(N_HEADS):
            cols = slice(h * HEAD_DIM, (h + 1) * HEAD_DIM)
            q = p_ref[:, h * HEAD_DIM:(h + 1) * HEAD_DIM]
            k = p_ref[:, GROUP + h * HEAD_DIM:GROUP + (h + 1) * HEAD_DIM]
            v = p_ref[:, 2 * GROUP + h * HEAD_DIM:2 * GROUP + (h + 1) * HEAD_DIM]
            g = p_ref[:, 3 * GROUP + h * HEAD_DIM:3 * GROUP + (h + 1) * HEAD_DIM]
            rq = _rope(q, cos_v, sin_v).astype(MXU_DTYPE)
            rk = _rope(k, cos_v, sin_v) * scale
            rkb = rk.astype(MXU_DTYPE)
            vb = v.astype(MXU_DTYPE)
            st = state[h]
            st_ref[h] = st
            s = _dot_nt(rq, rkb) * dec_ref[h]
            o = _dot(s.astype(MXU_DTYPE), vb) + _dot(rq, st.astype(MXU_DTYPE)) * xi_ref[h]
            kz = (rk * zeta_ref[h]).astype(MXU_DTYPE)
            state[h] = gr_ref[h, 0:1, :] * st + _dot_tn(kz, vb)
            o_ref[:, cols] = o
            mu = jnp.mean(o, axis=-1, keepdims=True)
            oc = o - mu
            yn = oc * lax.rsqrt(jnp.mean(oc * oc, axis=-1, keepdims=True) + NORM_EPS)
            ret = (g * _sigmoid(g)) * (yn * gain_ref[:, cols])
            mix_ref[:, cols] = jnp.where(valid, ret, 0.0).astype(mix_ref.dtype)

    head_tab = pl.BlockSpec((N_HEADS, CHUNK, HEAD_DIM), lambda c: (0, 0, 0))
    return pl.pallas_call(
        body, name="retention_fwd",
        out_shape=(jax.ShapeDtypeStruct((l, GROUP), MXU_DTYPE), jax.ShapeDtypeStruct((l, GROUP), F32),
                   jax.ShapeDtypeStruct((n_chunks, N_HEADS, HEAD_DIM, HEAD_DIM), F32)),
        grid=(n_chunks,),
        in_specs=[pl.BlockSpec((CHUNK, 4 * GROUP), lambda c: (c, 0)),
                  pl.BlockSpec((CHUNK, HEAD_DIM), lambda c: (c, 0)),
                  pl.BlockSpec((CHUNK, HEAD_DIM), lambda c: (c, 0)),
                  head_tab, head_tab, head_tab,
                  pl.BlockSpec((N_HEADS, 8, HEAD_DIM), lambda c: (0, 0, 0)),
                  pl.BlockSpec((1, GROUP), lambda c: (0, 0))],
        out_specs=(pl.BlockSpec((CHUNK, GROUP), lambda c: (c, 0)),
                   pl.BlockSpec((CHUNK, GROUP), lambda c: (c, 0)),
                   pl.BlockSpec((None, N_HEADS, HEAD_DIM, HEAD_DIM), lambda c: (c, 0, 0, 0))),
        scratch_shapes=[pltpu.VMEM((N_HEADS, HEAD_DIM, HEAD_DIM), F32)],
        compiler_params=_params(("arbitrary",)),
    )(proj, cos2, sin2, decay, xi, zeta, g_rows, ret_gain)


def _retention_bwd(proj, o_pre, states, d_mix, ret_gain, consts):
    l = proj.shape[0]
    n_chunks = l // CHUNK
    cos2, sin2, decay, xi, zeta, g_rows = consts
    scale = HEAD_DIM ** -0.5
    rev = lambda c: n_chunks - 1 - c

    def body(p_ref, o_ref, st_ref, dm_ref, cos_ref, sin_ref, dec_ref, xi_ref, zeta_ref, gr_ref, gain_ref,
             dp_ref, dgain_ref, dstate):
        step = pl.program_id(0)

        @pl.when(step == 0)
        def _():
            dstate[...] = jnp.zeros_like(dstate)
            dgain_ref[...] = jnp.zeros_like(dgain_ref)

        cos_v, sin_v = cos_ref[...], sin_ref[...]
        valid = _row_valid(rev(step), CHUNK)
        for h in range(N_HEADS):
            cols = slice(h * HEAD_DIM, (h + 1) * HEAD_DIM)
            q = p_ref[:, h * HEAD_DIM:(h + 1) * HEAD_DIM]
            k = p_ref[:, GROUP + h * HEAD_DIM:GROUP + (h + 1) * HEAD_DIM]
            v = p_ref[:, 2 * GROUP + h * HEAD_DIM:2 * GROUP + (h + 1) * HEAD_DIM]
            g = p_ref[:, 3 * GROUP + h * HEAD_DIM:3 * GROUP + (h + 1) * HEAD_DIM]
            o = o_ref[:, cols]
            gain = gain_ref[:, cols]
            d_ret = jnp.where(valid, dm_ref[:, cols], 0.0)
            mu = jnp.mean(o, axis=-1, keepdims=True)
            oc = o - mu
            rstd = lax.rsqrt(jnp.mean(oc * oc, axis=-1, keepdims=True) + NORM_EPS)
            yn = oc * rstd
            sig = _sigmoid(g)
            gate = g * sig
            dgain_ref[:, cols] += jnp.sum(d_ret * gate * yn, axis=0, keepdims=True)
            d_g = d_ret * (yn * gain) * (sig * (1.0 + g * (1.0 - sig)))
            d_yn = d_ret * gate * gain
            d_o = rstd * (d_yn - jnp.mean(d_yn, axis=-1, keepdims=True)
                          - yn * jnp.mean(d_yn * yn, axis=-1, keepdims=True))
            rq = _rope(q, cos_v, sin_v)
            rk = _rope(k, cos_v, sin_v) * scale
            rqb, rkb, vb = rq.astype(MXU_DTYPE), rk.astype(MXU_DTYPE), v.astype(MXU_DTYPE)
            dob = d_o.astype(MXU_DTYPE)
            dec = dec_ref[h]
            xi_h, zeta_h = xi_ref[h], zeta_ref[h]
            st_b = st_ref[h].astype(MXU_DTYPE)
            dst = dstate[h]
            dst_b = dst.astype(MXU_DTYPE)
            s_b = (_dot_nt(rqb, rkb) * dec).astype(MXU_DTYPE)
            da_b = (_dot_nt(dob, vb) * dec).astype(MXU_DTYPE)
            doxi_b = (d_o * xi_h).astype(MXU_DTYPE)
            kz_b = (rk * zeta_h).astype(MXU_DTYPE)
            d_rq = _dot(da_b, rkb) + _dot_nt(doxi_b, st_b)
            d_rk = _dot_tn(da_b, rqb) + _dot_nt(vb, dst_b) * zeta_h
            d_v = _dot_tn(s_b, dob) + _dot(kz_b, dst_b)
            dstate[h] = gr_ref[h, 0:1, :] * dst + _dot_tn(rqb, doxi_b)
            d_q = _rope_bwd(d_rq, cos_v, sin_v)
            d_k = _rope_bwd(d_rk * scale, cos_v, sin_v)
            dp_ref[:, h * HEAD_DIM:(h + 1) * HEAD_DIM] = d_q.astype(dp_ref.dtype)
            dp_ref[:, GROUP + h * HEAD_DIM:GROUP + (h + 1) * HEAD_DIM] = d_k.astype(dp_ref.dtype)
            dp_ref[:, 2 * GROUP + h * HEAD_DIM:2 * GROUP + (h + 1) * HEAD_DIM] = d_v.astype(dp_ref.dtype)
            dp_ref[:, 3 * GROUP + h * HEAD_DIM:3 * GROUP + (h + 1) * HEAD_DIM] = d_g.astype(dp_ref.dtype)

    head_tab = pl.BlockSpec((N_HEADS, CHUNK, HEAD_DIM), lambda c: (0, 0, 0))
    return pl.pallas_call(
        body, name="retention_bwd",
        out_shape=(jax.ShapeDtypeStruct((l, 4 * GROUP), MXU_DTYPE), jax.ShapeDtypeStruct((1, GROUP), F32)),
        grid=(n_chunks,),
        in_specs=[pl.BlockSpec((CHUNK, 4 * GROUP), lambda c: (rev(c), 0)),
                  pl.BlockSpec((CHUNK, GROUP), lambda c: (rev(c), 0)),
                  pl.BlockSpec((None, N_HEADS, HEAD_DIM, HEAD_DIM), lambda c: (rev(c), 0, 0, 0)),
                  pl.BlockSpec((CHUNK, GROUP), lambda c: (rev(c), 0)),
                  pl.BlockSpec((CHUNK, HEAD_DIM), lambda c: (rev(c), 0)),
                  pl.BlockSpec((CHUNK, HEAD_DIM), lambda c: (rev(c), 0)),
                  head_tab, head_tab, head_tab,
                  pl.BlockSpec((N_HEADS, 8, HEAD_DIM), lambda c: (0, 0, 0)),
                  pl.BlockSpec((1, GROUP), lambda c: (0, 0))],
        out_specs=(pl.BlockSpec((CHUNK, 4 * GROUP), lambda c: (rev(c), 0)),
                   pl.BlockSpec((1, GROUP), lambda c: (0, 0))),
        scratch_shapes=[pltpu.VMEM((N_HEADS, HEAD_DIM, HEAD_DIM), F32)],
        compiler_params=_params(("arbitrary",)),
    )(proj, o_pre, states, d_mix, cos2, sin2, decay, xi, zeta, g_rows, ret_gain)


FF_TILE = (7 * GROUP) // 128


def _log_forget(ff, bias_row, valid):
    x = ff + bias_row
    e = jnp.exp(-jnp.abs(x))
    lf = jnp.minimum(x, 0.0) - jnp.log(1.0 + e)
    head_lane = lax.broadcasted_iota(jnp.int32, x.shape, 1) < N_HEADS
    keep = lambda t: jnp.where(head_lane, jnp.where(valid, t, 0.0), 0.0)
    return keep(lf), keep(jnp.where(x >= 0, e, 1.0) / (1.0 + e))


def _fox_prep(proj, bias_row):
    l = proj.shape[0]
    n_blocks = l // CHUNK

    def body(ff_ref, b_ref, bc_ref, rows_ref, cum):
        r = lax.broadcasted_iota(jnp.int32, (CHUNK, CHUNK), 0)
        cidx = lax.broadcasted_iota(jnp.int32, (CHUNK, CHUNK), 1)
        tri = jnp.where(r >= cidx, 1.0, 0.0).astype(F32)
        carry = jnp.zeros((1, 128), F32)
        for blk in range(n_blocks):
            rows = slice(blk * CHUNK, (blk + 1) * CHUNK)
            valid = _row_valid(blk, CHUNK)
            lf, _ = _log_forget(ff_ref[rows, :], b_ref[...], valid)
            local = jnp.dot(tri, lf, precision=lax.Precision.HIGHEST, preferred_element_type=F32) + carry
            carry = local[CHUNK - 1:CHUNK, :]
            masked = jnp.where(valid, local, -NEG_BIG)
            cum[rows, :] = masked
            t = masked.T
            for h in range(N_HEADS):
                rows_ref[h, :, rows] = t[h:h + 1, :]
        full = cum[...]
        for h in range(N_HEADS):
            bc_ref[h] = jnp.broadcast_to(full[:, h:h + 1], (l, 128))

    return pl.pallas_call(
        body, name="fox_prep",
        out_shape=(jax.ShapeDtypeStruct((N_HEADS, l, 128), F32), jax.ShapeDtypeStruct((N_HEADS, 1, l), F32)),
        grid=(1,),
        in_specs=[pl.BlockSpec((l, 128), lambda i: (0, FF_TILE)), pl.BlockSpec((1, 128), lambda i: (0, 0))],
        out_specs=(pl.BlockSpec((N_HEADS, l, 128), lambda i: (0, 0, 0)),
                   pl.BlockSpec((N_HEADS, 1, l), lambda i: (0, 0, 0))),
        scratch_shapes=[pltpu.VMEM((l, 128), F32)],
        compiler_params=_params(("arbitrary",)),
    )(proj, bias_row)


def _fox_fwd(proj, cum_bc, cum_rows):
    l = proj.shape[0]
    n_blocks = l // CHUNK
    scale = HEAD_DIM ** -0.5
    qt, kt, vt = 4 * N_HEADS, 5 * N_HEADS, 6 * N_HEADS

    def body(q_ref, k_ref, v_ref, cq_ref, ck_ref, o_ref):
        i = pl.program_id(1)
        qb = q_ref[...].astype(MXU_DTYPE)
        kb = k_ref[...].astype(MXU_DTYPE)
        vb = v_ref[...].astype(MXU_DTYPE)
        s = _dot_nt(qb, kb) * scale
        bias = jnp.tile(cq_ref[...], (1, n_blocks)) - ck_ref[...]
        q_pos = i * CHUNK + lax.broadcasted_iota(jnp.int32, (CHUNK, l), 0)
        k_pos = lax.broadcasted_iota(jnp.int32, (CHUNK, l), 1)
        s = jnp.where(k_pos <= q_pos, s + bias, NEG_BIG)
        m = jnp.max(s, axis=-1, keepdims=True)
        e = jnp.exp(s - m)
        p = e * (1.0 / jnp.sum(e, axis=-1, keepdims=True))
        o = _dot(p.astype(MXU_DTYPE), vb)
        o_ref[...] = jnp.where(_row_valid(i, CHUNK), o, 0.0).astype(o_ref.dtype)

    return pl.pallas_call(
        body, name="fox_fwd",
        out_shape=jax.ShapeDtypeStruct((l, GROUP), MXU_DTYPE),
        grid=(N_HEADS, n_blocks),
        in_specs=[pl.BlockSpec((CHUNK, HEAD_DIM), lambda h, i: (i, qt + h)),
                  pl.BlockSpec((l, HEAD_DIM), lambda h, i: (0, kt + h)),
                  pl.BlockSpec((l, HEAD_DIM), lambda h, i: (0, vt + h)),
                  pl.BlockSpec((None, CHUNK, 128), lambda h, i: (h, i, 0)),
                  pl.BlockSpec((None, 1, l), lambda h, i: (h, 0, 0))],
        out_specs=pl.BlockSpec((CHUNK, HEAD_DIM), lambda h, i: (i, h)),
        compiler_params=_params(("parallel", "parallel")),
    )(proj, proj, proj, cum_bc, cum_rows)


def _fox_bwd(proj, cum_bc, cum_rows, d_mix):
    l = proj.shape[0]
    n_blocks = l // CHUNK
    scale = HEAD_DIM ** -0.5
    qt, kt, vt = 4 * N_HEADS, 5 * N_HEADS, 6 * N_HEADS

    def body(q_ref, k_ref, v_ref, do_ref, ck_ref, cq_ref, dq_ref, dk_ref, dv_ref, ds_ref, dk_acc, dv_acc):
        i = pl.program_id(1)

        @pl.when(i == 0)
        def _():
            dk_acc[...] = jnp.zeros_like(dk_acc)
            dv_acc[...] = jnp.zeros_like(dv_acc)
            ds_ref[...] = jnp.zeros_like(ds_ref)

        qb = q_ref[...].astype(MXU_DTYPE)
        kb = k_ref[...].astype(MXU_DTYPE)
        vb = v_ref[...].astype(MXU_DTYPE)
        dob = jnp.where(_row_valid(i, CHUNK), do_ref[...], 0.0).astype(MXU_DTYPE)
        k_pos = lax.broadcasted_iota(jnp.int32, (l, CHUNK), 0)
        q_pos = i * CHUNK + lax.broadcasted_iota(jnp.int32, (l, CHUNK), 1)
        s_t = _dot_nt(kb, qb) * scale + (cq_ref[...] - ck_ref[...])
        s_t = jnp.where(k_pos <= q_pos, s_t, NEG_BIG)
        m = jnp.max(s_t, axis=0, keepdims=True)
        e = jnp.exp(s_t - m)
        p_t = e * (1.0 / jnp.sum(e, axis=0, keepdims=True))
        dp_t = _dot_nt(vb, dob)
        delta = jnp.sum(p_t * dp_t, axis=0, keepdims=True)
        ds_t = p_t * (dp_t - delta)
        ds_b = ds_t.astype(MXU_DTYPE)
        dv_acc[...] += _dot(p_t.astype(MXU_DTYPE), dob)
        dk_acc[...] += _dot(ds_b, qb) * scale
        ds_ref[...] += ds_t
        dq_ref[...] = (_dot_tn(ds_b, kb) * scale).astype(dq_ref.dtype)

        @pl.when(i == n_blocks - 1)
        def _():
            dk_ref[...] = dk_acc[...].astype(dk_ref.dtype)
            dv_ref[...] = dv_acc[...].astype(dv_ref.dtype)

    col = jax.ShapeDtypeStruct((l, GROUP), MXU_DTYPE)
    return pl.pallas_call(
        body, name="fox_bwd",
        out_shape=(col, col, col, jax.ShapeDtypeStruct((N_HEADS, l, 128), F32)),
        grid=(N_HEADS, n_blocks),
        in_specs=[pl.BlockSpec((CHUNK, HEAD_DIM), lambda h, i: (i, qt + h)),
                  pl.BlockSpec((l, HEAD_DIM), lambda h, i: (0, kt + h)),
                  pl.BlockSpec((l, HEAD_DIM), lambda h, i: (0, vt + h)),
                  pl.BlockSpec((CHUNK, HEAD_DIM), lambda h, i: (i, N_HEADS + h)),
                  pl.BlockSpec((None, l, 128), lambda h, i: (h, 0, 0)),
                  pl.BlockSpec((None, 1, CHUNK), lambda h, i: (h, 0, i))],
        out_specs=(pl.BlockSpec((CHUNK, HEAD_DIM), lambda h, i: (i, h)),
                   pl.BlockSpec((l, HEAD_DIM), lambda h, i: (0, h)),
                   pl.BlockSpec((l, HEAD_DIM), lambda h, i: (0, h)),
                   pl.BlockSpec((None, l, 128), lambda h, i: (h, 0, 0))),
        scratch_shapes=[pltpu.VMEM((l, HEAD_DIM), F32), pltpu.VMEM((l, HEAD_DIM), F32)],
        compiler_params=_params(("parallel", "arbitrary")),
    )(proj, proj, proj, d_mix, cum_bc, cum_rows)


def _fox_gate_bwd(ds_sum, proj, bias_row):
    l = proj.shape[0]
    n_blocks = l // CHUNK

    def body(ds_ref, ff_ref, b_ref, dff_ref, db_ref):
        r = lax.broadcasted_iota(jnp.int32, (CHUNK, CHUNK), 0)
        cidx = lax.broadcasted_iota(jnp.int32, (CHUNK, CHUNK), 1)
        upper = jnp.where(cidx >= r, 1.0, 0.0).astype(F32)
        carry = jnp.zeros((1, 128), F32)
        db = jnp.zeros((1, 128), F32)
        for blk in reversed(range(n_blocks)):
            rows = slice(blk * CHUNK, (blk + 1) * CHUNK)
            key_sum = jnp.zeros((CHUNK, 128), F32)
            for h in range(N_HEADS):
                select = jnp.where(cidx == h, 1.0, 0.0).astype(F32)
                key_sum = key_sum + jnp.dot(ds_ref[h, rows, :], select, precision=lax.Precision.HIGHEST,
                                            preferred_element_type=F32)
            suffix = jnp.dot(upper, key_sum, precision=lax.Precision.HIGHEST, preferred_element_type=F32) + carry
            carry = suffix[0:1, :]
            _, dsig = _log_forget(ff_ref[rows, :], b_ref[...], _row_valid(blk, CHUNK))
            dff = -suffix * dsig
            dff_ref[rows, :] = dff.astype(dff_ref.dtype)
            db = db + jnp.sum(dff, axis=0, keepdims=True)
        db_ref[...] = db

    return pl.pallas_call(
        body, name="fox_gate_bwd",
        out_shape=(jax.ShapeDtypeStruct((l, 128), MXU_DTYPE), jax.ShapeDtypeStruct((1, 128), F32)),
        grid=(1,),
        in_specs=[pl.BlockSpec((N_HEADS, l, 128), lambda i: (0, 0, 0)),
                  pl.BlockSpec((l, 128), lambda i: (0, FF_TILE)),
                  pl.BlockSpec((1, 128), lambda i: (0, 0))],
        out_specs=(pl.BlockSpec((l, 128), lambda i: (0, 0)), pl.BlockSpec((1, 128), lambda i: (0, 0))),
        compiler_params=_params(("arbitrary",)),
    )(ds_sum, proj, bias_row)


def _conv(u, w, b):
    return b + w[0:1, :] * pltpu.roll(u, 2, 0) + w[1:2, :] * pltpu.roll(u, 1, 0) + w[2:3, :] * u


def _conv_act_fwd(u, conv_w, conv_b, d_ff):
    l = u.shape[0]
    tc = _divisor_tile(d_ff, 256, 128)
    nt = d_ff // tc

    def body(ug_ref, uv_ref, wg_ref, wv_ref, bg_ref, bv_ref, a_ref):
        yg = _conv(ug_ref[...], wg_ref[...], bg_ref[...])
        yv = _conv(uv_ref[...], wv_ref[...], bv_ref[...])
        act = yg * _sigmoid(yg) * yv
        a_ref[...] = jnp.where(_row_valid(0, l), act, 0.0).astype(a_ref.dtype)

    return pl.pallas_call(
        body, name="conv_act_fwd",
        out_shape=jax.ShapeDtypeStruct((l, d_ff), MXU_DTYPE),
        grid=(nt,),
        in_specs=[pl.BlockSpec((l, tc), lambda j: (0, j)), pl.BlockSpec((l, tc), lambda j: (0, j + nt)),
                  pl.BlockSpec((8, tc), lambda j: (0, j)), pl.BlockSpec((8, tc), lambda j: (0, j + nt)),
                  pl.BlockSpec((1, tc), lambda j: (0, j)), pl.BlockSpec((1, tc), lambda j: (0, j + nt))],
        out_specs=pl.BlockSpec((l, tc), lambda j: (0, j)),
        compiler_params=_params(("parallel",)),
    )(u, u, conv_w, conv_w, conv_b, conv_b)


def _conv_act_bwd(u, conv_w, conv_b, d_act, d_ff):
    l = u.shape[0]
    tc = _divisor_tile(d_ff, 256, 128)
    nt = d_ff // tc

    def body(ug_ref, uv_ref, wg_ref, wv_ref, bg_ref, bv_ref, da_ref, du_ref, dwb_ref):
        valid = _row_valid(0, l)
        ug, uv = ug_ref[...], uv_ref[...]
        wg, wv = wg_ref[...], wv_ref[...]
        yg = _conv(ug, wg, bg_ref[...])
        yv = _conv(uv, wv, bv_ref[...])
        sig = _sigmoid(yg)
        da = jnp.where(valid, da_ref[...], 0.0)
        d_yv = da * (yg * sig)
        d_yg = da * yv * (sig * (1.0 + yg * (1.0 - sig)))
        for idx, (dy, uu, w) in enumerate(((d_yg, ug, wg), (d_yv, uv, wv))):
            du = w[2:3, :] * dy + w[1:2, :] * pltpu.roll(dy, l - 1, 0) + w[0:1, :] * pltpu.roll(dy, l - 2, 0)
            du_ref[idx] = jnp.where(valid, du, 0.0).astype(du_ref.dtype)
            dwb_ref[idx, 0:1, :] = jnp.sum(dy * pltpu.roll(uu, 2, 0), axis=0, keepdims=True)
            dwb_ref[idx, 1:2, :] = jnp.sum(dy * pltpu.roll(uu, 1, 0), axis=0, keepdims=True)
            dwb_ref[idx, 2:3, :] = jnp.sum(dy * uu, axis=0, keepdims=True)
            dwb_ref[idx, 3:4, :] = jnp.sum(dy, axis=0, keepdims=True)
            dwb_ref[idx, 4:8, :] = jnp.zeros((4, tc), F32)

    return pl.pallas_call(
        body, name="conv_act_bwd",
        out_shape=(jax.ShapeDtypeStruct((2, l, d_ff), MXU_DTYPE), jax.ShapeDtypeStruct((2, 8, d_ff), F32)),
        grid=(nt,),
        in_specs=[pl.BlockSpec((l, tc), lambda j: (0, j)), pl.BlockSpec((l, tc), lambda j: (0, j + nt)),
                  pl.BlockSpec((8, tc), lambda j: (0, j)), pl.BlockSpec((8, tc), lambda j: (0, j + nt)),
                  pl.BlockSpec((1, tc), lambda j: (0, j)), pl.BlockSpec((1, tc), lambda j: (0, j + nt)),
                  pl.BlockSpec((l, tc), lambda j: (0, j))],
        out_specs=(pl.BlockSpec((2, l, tc), lambda j: (0, 0, j)), pl.BlockSpec((2, 8, tc), lambda j: (0, 0, j))),
        compiler_params=_params(("parallel",)),
    )(u, u, conv_w, conv_w, conv_b, conv_b, d_act)


def _adamw(w, g, m, v, name):
    shape = w.shape
    if w.ndim == 1:
        as2d = (1, shape[0])
    else:
        as2d = (int(np.prod(shape[:-1])), shape[-1])
    r, c = as2d
    tr = _divisor_tile(r, 256, 8)
    spec = pl.BlockSpec((tr, c), lambda i: (i, 0))

    def body(w_ref, g_ref, m_ref, v_ref, d_ref, nm_ref, nv_ref):
        gv = g_ref[...]
        nm = ADAM_B1 * m_ref[...] + (1.0 - ADAM_B1) * gv
        nv = ADAM_B2 * v_ref[...] + (1.0 - ADAM_B2) * (gv * gv)
        m_hat = nm / (1.0 - ADAM_B1 ** ADAM_STEP)
        v_hat = nv / (1.0 - ADAM_B2 ** ADAM_STEP)
        d_ref[...] = -ADAM_LR * (m_hat / (jnp.sqrt(v_hat) + ADAM_EPS) + ADAM_WD * w_ref[...])
        nm_ref[...] = nm
        nv_ref[...] = nv

    sds = jax.ShapeDtypeStruct(as2d, F32)
    outs = pl.pallas_call(
        body, name=name, out_shape=(sds, sds, sds), grid=(r // tr,),
        in_specs=[spec] * 4, out_specs=(spec,) * 3,
        compiler_params=_params(("parallel",)),
    )(w.reshape(as2d), g.reshape(as2d), m.reshape(as2d), v.reshape(as2d))
    return tuple(o.reshape(shape) for o in outs)


def _pad_rows(a, rows):
    return jnp.pad(a, ((0, rows - a.shape[0]), (0, 0)))


def kernel(x, meta_tokens, norm1_gain, w_in, b_forget, ret_norm_gain, w_out, norm2_gain, w_up, conv_w, conv_b, w_down, final_norm_gain, loss_target, m_meta_tokens, m_norm1_gain, m_w_in, m_b_forget, m_ret_norm_gain, m_w_out, m_norm2_gain, m_w_up, m_conv_w, m_conv_b, m_w_down, m_final_norm_gain, v_meta_tokens, v_norm1_gain, v_w_in, v_b_forget, v_ret_norm_gain, v_w_out, v_norm2_gain, v_w_up, v_conv_w, v_conv_b, v_w_down, v_final_norm_gain):
    seq, d = x.shape[1], x.shape[2]
    l = CHUNK + seq
    d_ff = w_down.shape[1] * N_DEV
    up_shard = w_up.shape[2]
    assert 4 * up_shard == d_ff and w_in.shape[2] == WIN_SHARD and d == 2 * GROUP
    dev = _device_index()
    mx, my, mc = _my_position()
    core = jnp.reshape(mc, (1,)).astype(jnp.int32)
    chip = jnp.reshape(2 * mx + my, (1,)).astype(jnp.int32)

    conv_w_full = jnp.transpose(_all_gather(_pad_rows(conv_w[0], 8), "gather_conv_w"), (1, 0, 2)).reshape(8, 2 * d_ff)
    meta_full = jnp.transpose(_all_gather(meta_tokens, "gather_meta"), (1, 0, 2)).reshape(N_META, d)
    w_in_window = lax.dynamic_update_slice(
        jnp.zeros((d, WIN_BLOCK), WIRE_DTYPE), w_in[0].astype(WIRE_DTYPE), (jnp.int32(0), dev.astype(jnp.int32)))
    start_in = _gather_start(w_in_window, "gather_w_in_start")
    start_out = _gather_start((w_out[0] + start_in[4][0, 0]).astype(WIRE_DTYPE), "gather_w_out_start")
    start_up = _gather_start((w_up[0] + start_out[4][0, 0]).astype(WIRE_DTYPE), "gather_w_up_start")
    start_down = _gather_start((w_down[0] + start_up[4][0, 0]).astype(WIRE_DTYPE), "gather_w_down_start")

    h0 = jnp.concatenate([jnp.zeros((PAD_ROWS, d), F32), meta_full, x[0]], axis=0)
    consts = _retention_consts(l)
    bias_row = jnp.pad(b_forget, ((0, 0), (0, 128 - N_HEADS)))
    a = _rmsnorm_fwd(h0, norm1_gain + start_down[4][0, 0], "rmsnorm1")
    w_in_full = _assemble_w_in(_gather_finish(start_in, a, "gather_w_in")).astype(MXU_DTYPE)
    proj = _mm_nn(a, w_in_full, F32, "mm_proj")
    ret_mix, ret_pre, ret_states = _retention_fwd(proj, ret_norm_gain, consts)
    cum_bc, cum_rows = _fox_prep(proj, bias_row)
    fox_mix = _fox_fwd(proj, cum_bc, cum_rows)
    mix = jnp.concatenate([ret_mix, fox_mix], axis=1)
    w_out_full = _gather_finish(start_out, mix, "gather_w_out").reshape(d, d).astype(MXU_DTYPE)
    h1, cn = _rmsnorm_fwd(h0, norm2_gain, "resid_rmsnorm2", res=_mm_nn(mix, w_out_full, F32, "mm_out"))
    w_up_blocks = _gather_finish(start_up, cn, "gather_w_up").astype(MXU_DTYPE)
    u = _mm(cn, w_up_blocks,
            a_spec=pl.BlockSpec((_divisor_tile(l, 1088, 16), d), lambda i, j, k: (i, 0)),
            b_spec=pl.BlockSpec((None, d, up_shard), lambda i, j, k: (j, 0, 0)),
            o_spec=pl.BlockSpec((_divisor_tile(l, 1088, 16), up_shard), lambda i, j, k: (i, j)),
            out_shape=jax.ShapeDtypeStruct((l, 2 * d_ff), F32),
            grid=(l // _divisor_tile(l, 1088, 16), N_DEV, 1), contract=(1, 0), nk=1, name="mm_up")
    act = _conv_act_fwd(u, conv_w_full, conv_b, d_ff)
    w_down_full = _gather_finish(start_down, act, "gather_w_down").reshape(d_ff, d).astype(MXU_DTYPE)
    mlp_out = _mm_nn(act, w_down_full, F32, "mm_down", tk_cap=1408)
    d_h2, d_h2_b, dg_final, loss_part = _loss_head(h1, mlp_out, final_norm_gain.reshape(1, d), loss_target[0])

    gw_down = _mm_tn(act, d_h2_b, WIRE_DTYPE, "mm_gw_down", tm_cap=1408, tn_cap=1024)
    rs_down = _reduce_scatter_start(gw_down.reshape(N_DEV, d_ff // N_DEV, d), core, "rs_w_down")
    d_act = _mm_nt(d_h2_b, w_down_full, F32, "mm_d_act")
    d_u, d_conv = _conv_act_bwd(u, conv_w_full, conv_b + rs_down[4][0, 0], d_act, d_ff)
    tm = _divisor_tile(l, 1088, 16)
    gw_up = _mm(cn, d_u,
                a_spec=pl.BlockSpec((l, d // 2), lambda i, j, k: (0, i)),
                b_spec=pl.BlockSpec((None, l, up_shard), lambda i, j, k: (j // 4, 0, j % 4)),
                o_spec=pl.BlockSpec((None, d // 2, up_shard), lambda i, j, k: (j, i, 0)),
                out_shape=jax.ShapeDtypeStruct((N_DEV, d, up_shard), WIRE_DTYPE),
                grid=(2, N_DEV, 1), contract=(0, 0), nk=1, name="mm_gw_up")
    rs_up = _reduce_scatter_start(gw_up, core, "rs_w_up")
    d_cn = _mm(d_u, w_up_blocks,
               a_spec=pl.BlockSpec((None, tm, up_shard), lambda i, j, k: (k // 4, i, k % 4)),
               b_spec=pl.BlockSpec((None, d // 2, up_shard), lambda i, j, k: (k, j, 0)),
               o_spec=pl.BlockSpec((tm, d // 2), lambda i, j, k: (i, j)),
               out_shape=jax.ShapeDtypeStruct((l, d), F32),
               grid=(l // tm, 2, N_DEV), contract=(1, 1), nk=N_DEV, name="mm_d_cn")
    d_h1, d_h1_b, dg_norm2 = _rmsnorm_bwd(d_h2, d_cn, h1, norm2_gain + rs_up[4][0, 0], "rmsnorm2_bwd", True)

    gw_out = _mm_tn(mix, d_h1_b, WIRE_DTYPE, "mm_gw_out")
    rs_out = _reduce_scatter_start(gw_out.reshape(N_DEV, d // N_DEV, d), core, "rs_w_out")
    d_mix = _mm_nt(d_h1_b, w_out_full, F32, "mm_d_mix")
    d_fq, d_fk, d_fv, ds_sum = _fox_bwd(proj, cum_bc, cum_rows, d_mix)
    d_ff_tile, db_forget_row = _fox_gate_bwd(ds_sum, proj, bias_row)
    d_ret, dg_ret = _retention_bwd(proj, ret_pre, ret_states, d_mix, ret_norm_gain + rs_out[4][0, 0], consts)
    d_proj = jnp.concatenate(
        [d_ret, d_fq, d_fk, d_fv, d_ff_tile, jnp.zeros((l, WIN_N - 7 * GROUP - 128), MXU_DTYPE)], axis=1)
    gw_in = _mm_tn(a, d_proj, WIRE_DTYPE, "mm_gw_in")
    rs_in = _reduce_scatter_start(_extract_w_in_windows(gw_in), core, "rs_w_in")
    d_a = _mm_nt(d_proj, w_in_full, F32, "mm_d_a", tk_cap=1536)
    d_h0, dg_norm1 = _rmsnorm_bwd(d_h1, d_a, h0, norm1_gain + rs_in[4][0, 0], "rmsnorm1_bwd", False)
    grad_x = d_h0[CHUNK:][None]
    d_meta = d_h0[PAD_ROWS:CHUNK]

    d_conv_w = jnp.concatenate([d_conv[0, 0:3], d_conv[1, 0:3]], axis=1)
    d_conv_b = jnp.concatenate([d_conv[0, 3:4], d_conv[1, 3:4]], axis=1)
    pieces = [loss_part[:, 0:1], dg_norm1, db_forget_row[:, 0:N_HEADS], dg_ret, dg_norm2, d_conv_b, dg_final,
              d_meta.reshape(1, -1), d_conv_w.reshape(1, -1)]
    sizes = [p.shape[1] for p in pieces]
    flat = jnp.concatenate(pieces, axis=1)
    padded = -(-flat.shape[1] // 1024) * 1024
    flat = jnp.pad(flat, ((0, 0), (0, padded - flat.shape[1]))).reshape(padded // 128, 128)
    total = _small_all_reduce(flat, "all_reduce_small").reshape(1, padded)
    offs = np.concatenate([[0], np.cumsum(sizes)])
    take = lambda k: total[:, int(offs[k]):int(offs[k + 1])]
    loss = take(0).reshape(())
    g_norm1, g_bf, g_ret_gain, g_norm2 = take(1), take(2), take(3), take(4)
    g_conv_b, g_final = take(5), take(6).reshape(d)
    g_meta = lax.dynamic_slice(take(7).reshape(N_META, d), (jnp.int32(0), (dev * (d // N_DEV)).astype(jnp.int32)),
                               (N_META, d // N_DEV))
    g_conv_w = lax.dynamic_slice(take(8).reshape(3, 2 * d_ff), (jnp.int32(0), (dev * up_shard).astype(jnp.int32)),
                                 (3, up_shard))[None]

    g_w_down = _reduce_scatter_finish(rs_down, total, chip, "rs_w_down")[None]
    g_w_up = _reduce_scatter_finish(rs_up, g_w_down, chip, "rs_w_up")[None]
    g_w_out = _reduce_scatter_finish(rs_out, g_w_up, chip, "rs_w_out")[None]
    early = [_adamw(w, g, m, v, "adamw_" + n) for w, g, m, v, n in (
        (w_down, g_w_down, m_w_down, v_w_down, "w_down"), (w_up, g_w_up, m_w_up, v_w_up, "w_up"),
        (w_out, g_w_out, m_w_out, v_w_out, "w_out"))]
    g_w_in_window = _reduce_scatter_finish(rs_in, early[1][2], chip, "rs_w_in")
    g_w_in = lax.dynamic_slice(g_w_in_window, (jnp.int32(0), dev.astype(jnp.int32)), (d, WIN_SHARD))[None]

    weights = [meta_tokens, norm1_gain, w_in, b_forget, ret_norm_gain, w_out, norm2_gain, w_up, conv_w, conv_b,
               w_down, final_norm_gain]
    grads = [g_meta, g_norm1, g_w_in, g_bf, g_ret_gain, g_w_out, g_norm2, g_w_up, g_conv_w, g_conv_b, g_w_down,
             g_final]
    done = {"w_down": early[0], "w_up": early[1], "w_out": early[2]}
    ms = [m_meta_tokens, m_norm1_gain, m_w_in, m_b_forget, m_ret_norm_gain, m_w_out, m_norm2_gain, m_w_up, m_conv_w,
          m_conv_b, m_w_down, m_final_norm_gain]
    vs = [v_meta_tokens, v_norm1_gain, v_w_in, v_b_forget, v_ret_norm_gain, v_w_out, v_norm2_gain, v_w_up, v_conv_w,
          v_conv_b, v_w_down, v_final_norm_gain]
    names = ["meta", "norm1", "w_in", "b_forget", "ret_gain", "w_out", "norm2", "w_up", "conv_w", "conv_b", "w_down",
             "final_gain"]
    deltas, new_ms, new_vs = [], [], []
    for w, g, m, v, n in zip(weights, grads, ms, vs, names):
        dl, nm, nv = done[n] if n in done else _adamw(w, g, m, v, "adamw_" + n)
        deltas.append(dl)
        new_ms.append(nm)
        new_vs.append(nv)
    return (loss, grad_x, *grads, *deltas, *new_ms, *new_vs)
```

```python
import functools

import numpy as np
import jax
import jax.numpy as jnp
from jax import lax
from jax.experimental import pallas as pl
from jax.experimental.pallas import tpu as pltpu

F32 = jnp.float32
MXU_DTYPE = jnp.bfloat16
WIRE_DTYPE = jnp.bfloat16

N_DEV = 8
N_META = 16
CHUNK = 128
PAD_ROWS = CHUNK - N_META
N_HEADS = 8
HEAD_DIM = 128
GROUP = N_HEADS * HEAD_DIM
IN_DIM = 7 * GROUP + N_HEADS
WIN_SHARD = IN_DIM // N_DEV
WIN_BLOCK = 1024
WIN_STRIDE = 896
WIN_N = 7680
ROPE_BASE = 10000.0
NORM_EPS = 1e-6
NEG_BIG = -1e30
ADAM_LR, ADAM_B1, ADAM_B2, ADAM_EPS, ADAM_WD, ADAM_STEP = 0.001, 0.9, 0.999, 1e-08, 0.01, 10
VMEM_LIMIT = 52 * 1024 * 1024
MESH = pl.DeviceIdType.MESH
ANY = pl.BlockSpec(memory_space=pl.ANY)
VMEM_SPEC = pl.BlockSpec(memory_space=pltpu.VMEM)


def _params(sem=None):
    kw = {"vmem_limit_bytes": VMEM_LIMIT}
    if sem is not None:
        kw["dimension_semantics"] = sem
    return pltpu.CompilerParams(**kw)


def _divisor_tile(n, cap, unit):
    if n <= cap:
        return n
    best = None
    for t in range(unit, cap + 1, unit):
        if n % t == 0:
            best = t
    assert best is not None, (n, cap, unit)
    return best


def _my_position():
    return lax.axis_index("x"), lax.axis_index("y"), lax.axis_index("c")


def _device_index():
    x, y, c = _my_position()
    return 4 * x + 2 * y + c


def _all_gather(shard, name):
    r, c = shard.shape

    def body(x_ref, out_ref, send_sems, recv_sems, local_sem):
        mx, my, mc = _my_position()
        me, sibling = (mx, my, mc), (mx, my, 1 - mc)
        chips = [(1 - mx, my), (mx, 1 - my), (1 - mx, 1 - my)]

        def slot(px, py, pc):
            return out_ref.at[4 * px + 2 * py + pc]

        def copy(k, block, to, src=None):
            return pltpu.make_async_remote_copy(
                src_ref=slot(*block) if src is None else src, dst_ref=slot(*block),
                send_sem=send_sems.at[k], recv_sem=recv_sems.at[k], device_id=to, device_id_type=MESH)

        mine = pltpu.make_async_copy(x_ref, slot(*me), local_sem)
        mine.start()
        first = [copy(0, me, sibling, src=x_ref)]
        first += [copy(1 + j, me, (*chip, mc), src=x_ref) for j, chip in enumerate(chips)]
        for cp in first:
            cp.start()
        passed = [copy(4 + j, (*chip, mc), sibling) for j, chip in enumerate(chips)]
        for j, chip in enumerate(chips):
            copy(1 + j, (*chip, mc), me).wait_recv()
            passed[j].start()
        copy(0, sibling, me).wait_recv()
        for j, chip in enumerate(chips):
            copy(4 + j, (*chip, 1 - mc), me).wait_recv()
        for cp in first + passed:
            cp.wait_send()
        mine.wait()

    return pl.pallas_call(
        body, name=name,
        out_shape=jax.ShapeDtypeStruct((N_DEV, r, c), shard.dtype),
        in_specs=[ANY], out_specs=ANY,
        scratch_shapes=[pltpu.SemaphoreType.DMA((7,)), pltpu.SemaphoreType.DMA((7,)), pltpu.SemaphoreType.DMA],
    )(shard)


HBM_SPEC = pl.BlockSpec(memory_space=pltpu.HBM)
SEM_SPEC = pl.BlockSpec(memory_space=pltpu.SEMAPHORE)
DATAFLOW_EFFECT = pltpu.SideEffectType.DATAFLOW_SIDE_EFFECTING


def _in_hbm(a):
    return pltpu.with_memory_space_constraint(a, pltpu.HBM)


def _split_start(src, land_shape, make_copies, n_copies, after, name):
    def body(src_ref, land_ref, after_ref, send_sems, recv_sems, src_thru, land_thru, token):
        for cp in make_copies(src_ref, land_ref, send_sems, recv_sems):
            cp.start()
        token[...] = jnp.zeros_like(token)

    return pl.pallas_call(
        body, name=name,
        out_shape=(pltpu.SemaphoreType.DMA((n_copies,)), pltpu.SemaphoreType.DMA((n_copies,)),
                   pltpu.HBM(src.shape, src.dtype), pltpu.HBM(land_shape, src.dtype),
                   jax.ShapeDtypeStruct((8, 128), F32)),
        in_specs=(HBM_SPEC, HBM_SPEC, ANY), out_specs=(SEM_SPEC, SEM_SPEC, HBM_SPEC, HBM_SPEC, VMEM_SPEC),
        input_output_aliases={0: 2, 1: 3},
        compiler_params=pltpu.CompilerParams(has_side_effects=DATAFLOW_EFFECT),
    )(_in_hbm(src), _in_hbm(lax.empty(land_shape, src.dtype)), after)


def _split_wait(started, after, make_copies, name):
    send_sems, recv_sems, src_thru, land_thru, _ = started

    def body(src_ref, land_ref, send_sems_ref, recv_sems_ref, after_ref, src_dead, land_out):
        for cp in make_copies(src_ref, land_ref, send_sems_ref, recv_sems_ref):
            cp.wait_send()
            cp.wait_recv()

    return pl.pallas_call(
        body, name=name,
        out_shape=(pltpu.HBM(src_thru.shape, src_thru.dtype), pltpu.HBM(land_thru.shape, land_thru.dtype)),
        in_specs=(HBM_SPEC, HBM_SPEC, SEM_SPEC, SEM_SPEC, ANY), out_specs=(HBM_SPEC, HBM_SPEC),
        input_output_aliases={0: 0, 1: 1},
        compiler_params=pltpu.CompilerParams(has_side_effects=DATAFLOW_EFFECT),
    )(src_thru, land_thru, send_sems, recv_sems, after)


def _gather_copies(x_ref, land_ref, send_sems, recv_sems):
    mx, my, mc = _my_position()
    me = 4 * mx + 2 * my + mc
    targets = [(mx, my, 1 - mc), (1 - mx, my, mc), (mx, 1 - my, mc), (1 - mx, 1 - my, mc)]
    return [pltpu.make_async_remote_copy(
        src_ref=x_ref, dst_ref=land_ref.at[me], send_sem=send_sems.at[k], recv_sem=recv_sems.at[k],
        device_id=t, device_id_type=MESH) for k, t in enumerate(targets)]


def _gather_start(shard, after, name):
    return _split_start(shard, (N_DEV,) + shard.shape, _gather_copies, 4, after, name)


def _gather_finish(started, after, name):
    shard, land = _split_wait(started, after, _gather_copies, name + "_wait")

    def body(land_in, x_ref, land_ref, send_sems, recv_sems, local_sem):
        mx, my, mc = _my_position()
        chips = [(1 - mx, my), (mx, 1 - my), (1 - mx, 1 - my)]
        mine = pltpu.make_async_copy(x_ref, land_ref.at[4 * mx + 2 * my + mc], local_sem)
        mine.start()
        copies = [pltpu.make_async_remote_copy(
            src_ref=land_ref.at[4 * cx + 2 * cy + mc], dst_ref=land_ref.at[4 * cx + 2 * cy + mc],
            send_sem=send_sems.at[j], recv_sem=recv_sems.at[j],
            device_id=(mx, my, 1 - mc), device_id_type=MESH) for j, (cx, cy) in enumerate(chips)]
        for cp in copies:
            cp.start()
        for j, (cx, cy) in enumerate(chips):
            copies[j].wait_send()
            pltpu.make_async_remote_copy(
                src_ref=land_ref.at[4 * cx + 2 * cy + 1 - mc], dst_ref=land_ref.at[4 * cx + 2 * cy + 1 - mc],
                send_sem=send_sems.at[j], recv_sem=recv_sems.at[j],
                device_id=(mx, my, 1 - mc), device_id_type=MESH).wait_recv()
        mine.wait()

    return pl.pallas_call(
        body, name=name + "_pass",
        out_shape=jax.ShapeDtypeStruct(land.shape, land.dtype),
        in_specs=[ANY, ANY], out_specs=ANY,
        input_output_aliases={0: 0},
        scratch_shapes=[pltpu.SemaphoreType.DMA((3,)), pltpu.SemaphoreType.DMA((3,)), pltpu.SemaphoreType.DMA],
    )(land, shard)


def _chip_copies(p_ref, land_ref, send_sems, recv_sems):
    mx, my, mc = _my_position()
    chips = [(1 - mx, my), (mx, 1 - my), (1 - mx, 1 - my)]
    return [pltpu.make_async_remote_copy(
        src_ref=p_ref.at[2 * cx + cy], dst_ref=land_ref.at[j], send_sem=send_sems.at[j], recv_sem=recv_sems.at[j],
        device_id=(cx, cy, mc), device_id_type=MESH) for j, (cx, cy) in enumerate(chips)]


def _reduce_scatter_start(g, core, name):
    pair = _pair_sum(g, _exchange_sibling(g, name + "_d2d"), core, name + "_pairsum")
    return _split_start(pair, (3,) + pair.shape[1:], _chip_copies, 3, g, name + "_ici_start")


def _reduce_scatter_finish(started, after, chip, name):
    pair, from_chips = _split_wait(started, after, _chip_copies, name + "_ici_wait")
    return _final_sum(pair, from_chips, chip, name + "_sum")


def _exchange_sibling(g, name):
    _, r, c = g.shape

    def body(g_ref, out_ref, send_sems, recv_sems):
        mx, my, mc = _my_position()
        copies = [
            pltpu.make_async_remote_copy(
                src_ref=g_ref.at[2 * k + (1 - mc)], dst_ref=out_ref.at[k],
                send_sem=send_sems.at[k], recv_sem=recv_sems.at[k],
                device_id=(mx, my, 1 - mc), device_id_type=MESH)
            for k in range(4)]
        for cp in copies:
            cp.start()
        for cp in copies:
            cp.wait()

    return pl.pallas_call(
        body, name=name,
        out_shape=jax.ShapeDtypeStruct((4, r, c), g.dtype),
        in_specs=[ANY], out_specs=ANY,
        scratch_shapes=[pltpu.SemaphoreType.DMA((4,)), pltpu.SemaphoreType.DMA((4,))],
    )(g)


def _pair_sum(g, recv, core, name):
    _, r, c = g.shape
    tr = _divisor_tile(r, 256, 16)

    def body(s_ref, g_ref, r_ref, o_ref):
        o_ref[...] = (g_ref[...].astype(F32) + r_ref[...].astype(F32)).astype(o_ref.dtype)

    return pl.pallas_call(
        body, name=name,
        out_shape=jax.ShapeDtypeStruct((4, r, c), g.dtype),
        grid_spec=pltpu.PrefetchScalarGridSpec(
            num_scalar_prefetch=1, grid=(4, r // tr),
            in_specs=[pl.BlockSpec((None, tr, c), lambda k, i, s: (2 * k + s[0], i, 0)),
                      pl.BlockSpec((None, tr, c), lambda k, i, s: (k, i, 0))],
            out_specs=pl.BlockSpec((None, tr, c), lambda k, i, s: (k, i, 0))),
        compiler_params=_params(("parallel", "parallel")),
    )(core, g, recv)


def _final_sum(p, recv, chip, name):
    _, r, c = p.shape
    tr = _divisor_tile(r, 256, 16)

    def body(s_ref, p_ref, r_ref, o_ref):
        acc = p_ref[...].astype(F32)
        for j in range(3):
            acc = acc + r_ref[j].astype(F32)
        o_ref[...] = acc

    return pl.pallas_call(
        body, name=name,
        out_shape=jax.ShapeDtypeStruct((r, c), F32),
        grid_spec=pltpu.PrefetchScalarGridSpec(
            num_scalar_prefetch=1, grid=(r // tr,),
            in_specs=[pl.BlockSpec((None, tr, c), lambda i, s: (s[0], i, 0)),
                      pl.BlockSpec((3, tr, c), lambda i, s: (0, i, 0))],
            out_specs=pl.BlockSpec((tr, c), lambda i, s: (i, 0))),
        compiler_params=_params(("parallel",)),
    )(chip, p, recv)


def _all_to_all_copies(v_ref, land_ref, send_sems, recv_sems):
    mx, my, mc = _my_position()
    me = 4 * mx + 2 * my + mc
    copies = []
    for rel in range(1, N_DEV):
        bx, by, bc = (rel >> 2) & 1, (rel >> 1) & 1, rel & 1
        target = (1 - mx if bx else mx, 1 - my if by else my, 1 - mc if bc else mc)
        copies.append(pltpu.make_async_remote_copy(
            src_ref=v_ref, dst_ref=land_ref.at[me], send_sem=send_sems.at[rel - 1], recv_sem=recv_sems.at[rel - 1],
            device_id=target, device_id_type=MESH))
    return copies


def _small_all_reduce_start(v, after, name):
    return _split_start(v, (N_DEV,) + v.shape, _all_to_all_copies, N_DEV - 1, after, name + "_start")


def _small_all_reduce_finish(started, after, dev, name):
    v, land = _split_wait(started, after, _all_to_all_copies, name + "_wait")
    rows = v.shape[0]

    def body(me_ref, v_ref, land_ref, o_ref):
        for j in range(N_DEV):
            @pl.when(me_ref[0] == j)
            def _():
                o_ref[...] = v_ref[...] if j == 0 else o_ref[...] + v_ref[...]

            @pl.when(me_ref[0] != j)
            def _():
                o_ref[...] = land_ref[j] if j == 0 else o_ref[...] + land_ref[j]

    return pl.pallas_call(
        body, name=name + "_sum",
        out_shape=jax.ShapeDtypeStruct((rows, 128), F32),
        grid_spec=pltpu.PrefetchScalarGridSpec(
            num_scalar_prefetch=1, grid=(1,),
            in_specs=[pl.BlockSpec((rows, 128), lambda i, s: (0, 0)),
                      pl.BlockSpec((N_DEV, rows, 128), lambda i, s: (0, 0, 0))],
            out_specs=pl.BlockSpec((rows, 128), lambda i, s: (0, 0))),
        compiler_params=_params(("arbitrary",)),
    )(dev, v, land)


def _assemble_w_in(blocks):
    _, d, _ = blocks.shape
    tr = _divisor_tile(d, 256, 16)
    n_tiles = WIN_N // 128
    last = (N_DEV * WIN_STRIDE) // 128

    def body(b_ref, o_ref):
        for t in range(n_tiles):
            if t > last:
                o_ref[:, t * 128:(t + 1) * 128] = jnp.zeros((tr, 128), o_ref.dtype)
                continue
            i = min(t // 7, N_DEV - 1)
            k = t - 7 * i
            val = b_ref[i, :, k * 128:(k + 1) * 128]
            if k == 0 and i >= 1:
                val = val + b_ref[i - 1, :, 7 * 128:8 * 128]
            o_ref[:, t * 128:(t + 1) * 128] = val

    return pl.pallas_call(
        body, name="assemble_w_in",
        out_shape=jax.ShapeDtypeStruct((d, WIN_N), blocks.dtype),
        grid=(d // tr,),
        in_specs=[pl.BlockSpec((N_DEV, tr, WIN_BLOCK), lambda i: (0, i, 0))],
        out_specs=pl.BlockSpec((tr, WIN_N), lambda i: (i, 0)),
        compiler_params=_params(("parallel",)),
    )(blocks)


def _extract_w_in_windows(g):
    d, _ = g.shape
    tr = _divisor_tile(d, 256, 16)

    def body(g_ref, o_ref):
        for j in range(N_DEV):
            o_ref[j] = g_ref[:, WIN_STRIDE * j:WIN_STRIDE * j + WIN_BLOCK]

    return pl.pallas_call(
        body, name="extract_w_in_windows",
        out_shape=jax.ShapeDtypeStruct((N_DEV, d, WIN_BLOCK), g.dtype),
        grid=(d // tr,),
        in_specs=[pl.BlockSpec((tr, WIN_N), lambda i: (i, 0))],
        out_specs=pl.BlockSpec((N_DEV, tr, WIN_BLOCK), lambda i: (0, i, 0)),
        compiler_params=_params(("parallel",)),
    )(g)


def _mm(a, b, *, a_spec, b_spec, o_spec, out_shape, grid, contract, nk, name, after=None):
    dn = (((contract[0],), (contract[1],)), ((), ()))
    tm, tn = o_spec.block_shape[-2:]
    behind = [] if after is None else [after]

    def body(a_ref, b_ref, *rest):
        o_ref, *scratch = rest[len(behind):]
        part = lax.dot_general(a_ref[...], b_ref[...], dn, preferred_element_type=F32)
        if nk == 1:
            o_ref[...] = part.astype(o_ref.dtype)
            return
        acc = scratch[0]
        k = pl.program_id(2)

        @pl.when(k == 0)
        def _():
            acc[...] = part

        @pl.when(k > 0)
        def _():
            acc[...] += part

        @pl.when(k == nk - 1)
        def _():
            o_ref[...] = acc[...].astype(o_ref.dtype)

    return pl.pallas_call(
        body, name=name, out_shape=out_shape, grid=grid,
        in_specs=[a_spec, b_spec] + [ANY] * len(behind), out_specs=o_spec,
        scratch_shapes=[] if nk == 1 else [pltpu.VMEM((tm, tn), F32)],
        compiler_params=_params(("parallel", "parallel", "arbitrary")),
    )(a, b, *behind)


def _mm_nn(a, b, out_dtype, name, tm_cap=1088, tn_cap=512, tk_cap=2048, after=None):
    m, k = a.shape
    _, n = b.shape
    tm, tn, tk = _divisor_tile(m, tm_cap, 16), _divisor_tile(n, tn_cap, 128), _divisor_tile(k, tk_cap, 128)
    return _mm(a, b,
               a_spec=pl.BlockSpec((tm, tk), lambda i, j, kk: (i, kk)),
               b_spec=pl.BlockSpec((tk, tn), lambda i, j, kk: (kk, j)),
               o_spec=pl.BlockSpec((tm, tn), lambda i, j, kk: (i, j)),
               out_shape=jax.ShapeDtypeStruct((m, n), out_dtype),
               grid=(m // tm, n // tn, k // tk), contract=(1, 0), nk=k // tk, name=name, after=after)


def _mm_nt(a, b, out_dtype, name, tm_cap=1088, tn_cap=512, tk_cap=2048, after=None):
    m, k = a.shape
    n, _ = b.shape
    tm, tn, tk = _divisor_tile(m, tm_cap, 16), _divisor_tile(n, tn_cap, 128), _divisor_tile(k, tk_cap, 128)
    return _mm(a, b,
               a_spec=pl.BlockSpec((tm, tk), lambda i, j, kk: (i, kk)),
               b_spec=pl.BlockSpec((tn, tk), lambda i, j, kk: (j, kk)),
               o_spec=pl.BlockSpec((tm, tn), lambda i, j, kk: (i, j)),
               out_shape=jax.ShapeDtypeStruct((m, n), out_dtype),
               grid=(m // tm, n // tn, k // tk), contract=(1, 1), nk=k // tk, name=name, after=after)


def _mm_tn(a, b, out_dtype, name, tm_cap=1024, tn_cap=512, after=None):
    l, m = a.shape
    _, n = b.shape
    tm, tn = _divisor_tile(m, tm_cap, 128), _divisor_tile(n, tn_cap, 128)
    return _mm(a, b,
               a_spec=pl.BlockSpec((l, tm), lambda i, j, kk: (0, i)),
               b_spec=pl.BlockSpec((l, tn), lambda i, j, kk: (0, j)),
               o_spec=pl.BlockSpec((tm, tn), lambda i, j, kk: (i, j)),
               out_shape=jax.ShapeDtypeStruct((m, n), out_dtype),
               grid=(m // tm, n // tn, 1), contract=(0, 0), nk=1, name=name, after=after)


def _row_tile(l):
    return _divisor_tile(l, 544, 8)


def _rmsnorm_fwd(h, gain, name, res=None):
    l, d = h.shape
    tr = _row_tile(l)
    row = pl.BlockSpec((tr, d), lambda i: (i, 0))
    vec = pl.BlockSpec((1, d), lambda i: (0, 0))

    def body(*refs):
        if res is None:
            h_ref, g_ref, n_ref = refs
            x = h_ref[...]
        else:
            h_ref, r_ref, g_ref, s_ref, n_ref = refs
            x = h_ref[...] + r_ref[...]
            s_ref[...] = x
        y = x * lax.rsqrt(jnp.mean(x * x, axis=-1, keepdims=True) + NORM_EPS)
        n_ref[...] = (y * g_ref[...]).astype(n_ref.dtype)

    normed = jax.ShapeDtypeStruct((l, d), MXU_DTYPE)
    if res is None:
        return pl.pallas_call(body, name=name, out_shape=normed, grid=(l // tr,), in_specs=[row, vec],
                              out_specs=row, compiler_params=_params(("parallel",)))(h, gain)
    return pl.pallas_call(body, name=name, out_shape=(jax.ShapeDtypeStruct((l, d), F32), normed),
                          grid=(l // tr,), in_specs=[row, row, vec], out_specs=(row, row),
                          compiler_params=_params(("parallel",)))(h, res, gain)


def _rmsnorm_bwd(d_res, d_normed, x, gain, name, with_mxu_copy):
    l, d = x.shape
    tr = _row_tile(l)
    row = pl.BlockSpec((tr, d), lambda i: (i, 0))
    vec = pl.BlockSpec((1, d), lambda i: (0, 0))

    def body(dres_ref, dn_ref, x_ref, g_ref, dx_ref, *rest):
        dg_ref = rest[-1]
        xv = x_ref[...]
        r = lax.rsqrt(jnp.mean(xv * xv, axis=-1, keepdims=True) + NORM_EPS)
        xh = xv * r
        dn = dn_ref[...]
        dxh = dn * g_ref[...]
        dx = dres_ref[...] + r * (dxh - xh * jnp.mean(dxh * xh, axis=-1, keepdims=True))
        dx_ref[...] = dx
        if with_mxu_copy:
            rest[0][...] = dx.astype(MXU_DTYPE)

        @pl.when(pl.program_id(0) == 0)
        def _():
            dg_ref[...] = jnp.zeros_like(dg_ref)

        dg_ref[...] += jnp.sum(dn * xh, axis=0, keepdims=True)

    outs = [jax.ShapeDtypeStruct((l, d), F32)]
    specs = [row]
    if with_mxu_copy:
        outs.append(jax.ShapeDtypeStruct((l, d), MXU_DTYPE))
        specs.append(row)
    outs.append(jax.ShapeDtypeStruct((1, d), F32))
    specs.append(vec)
    return pl.pallas_call(body, name=name, out_shape=tuple(outs), grid=(l // tr,),
                          in_specs=[row, row, row, vec], out_specs=tuple(specs),
                          compiler_params=_params(("arbitrary",)))(d_res, d_normed, x, gain)


def _loss_head(h1, mlp_out, gain, target):
    l, d = h1.shape
    n_blocks = l // CHUNK
    row = pl.BlockSpec((CHUNK, d), lambda i: (i, 0))
    vec = pl.BlockSpec((1, d), lambda i: (0, 0))
    tgt = pl.BlockSpec((CHUNK, d), lambda i: (jnp.maximum(i - 1, 0), 0))

    def body(h_ref, m_ref, g_ref, t_ref, dh_ref, dhb_ref, dg_ref, loss_ref, sq_ref):
        i = pl.program_id(0)
        x = h_ref[...] + m_ref[...]
        r = lax.rsqrt(jnp.mean(x * x, axis=-1, keepdims=True) + NORM_EPS)
        xh = x * r
        g = g_ref[...]
        real = i >= 1
        err = jnp.where(real, xh * g - t_ref[...], 0.0)
        dy = err * (1.0 / d)
        dxh = dy * g
        dh = r * (dxh - xh * jnp.mean(dxh * xh, axis=-1, keepdims=True))
        dh_ref[...] = dh
        dhb_ref[...] = dh.astype(MXU_DTYPE)

        @pl.when(i == 0)
        def _():
            dg_ref[...] = jnp.zeros_like(dg_ref)
            sq_ref[...] = jnp.zeros_like(sq_ref)

        dg_ref[...] += jnp.sum(dy * xh, axis=0, keepdims=True)
        sq_ref[...] += jnp.sum(err * err, axis=0, keepdims=True)

        @pl.when(i == n_blocks - 1)
        def _():
            total = jnp.sum(sq_ref[...], axis=-1, keepdims=True) * (0.5 / d)
            loss_ref[...] = jnp.broadcast_to(total, (1, 128))

    return pl.pallas_call(
        body, name="loss_head",
        out_shape=(jax.ShapeDtypeStruct((l, d), F32), jax.ShapeDtypeStruct((l, d), MXU_DTYPE),
                   jax.ShapeDtypeStruct((1, d), F32), jax.ShapeDtypeStruct((1, 128), F32)),
        grid=(n_blocks,), in_specs=[row, row, vec, tgt],
        out_specs=(row, row, vec, pl.BlockSpec((1, 128), lambda i: (0, 0))),
        scratch_shapes=[pltpu.VMEM((1, d), F32)],
        compiler_params=_params(("arbitrary",)),
    )(h1, mlp_out, gain, target)


def _dot(a, b):
    return jnp.dot(a, b, preferred_element_type=F32)


def _dot_nt(a, b):
    return lax.dot_general(a, b, (((1,), (1,)), ((), ())), preferred_element_type=F32)


def _dot_tn(a, b):
    return lax.dot_general(a, b, (((0,), (0,)), ((), ())), preferred_element_type=F32)


def _rope(t, cos2, sin2):
    return t * cos2 + pltpu.roll(t, HEAD_DIM // 2, 1) * sin2


def _rope_bwd(dr, cos2, sin2):
    return dr * cos2 + pltpu.roll(dr * sin2, HEAD_DIM // 2, 1)


def _sigmoid(x):
    return 1.0 / (1.0 + jnp.exp(-x))


def _row_valid(block, rows):
    r = block * CHUNK + lax.broadcasted_iota(jnp.int32, (rows, 1), 0)
    return r >= PAD_ROWS


def _retention_consts(l):
    pos = jnp.arange(l, dtype=F32) - PAD_ROWS
    inv_freq = 1.0 / (ROPE_BASE ** (jnp.arange(0, HEAD_DIM, 2, dtype=F32) / HEAD_DIM))
    ang = pos[:, None] * inv_freq[None, :]
    cos, sin = jnp.cos(ang), jnp.sin(ang)
    cos2 = jnp.concatenate([cos, cos], axis=-1)
    sin2 = jnp.concatenate([-sin, sin], axis=-1)
    log_g = jnp.log1p(-jnp.exp2(-5.0 - jnp.arange(N_HEADS, dtype=F32)))
    idx = jnp.arange(CHUNK, dtype=F32)
    diff = idx[:, None] - idx[None, :]
    decay = jnp.where(diff >= 0, jnp.exp(jnp.maximum(diff, 0.0)[None] * log_g[:, None, None]), 0.0)
    xi = jnp.exp((idx + 1.0)[None, :] * log_g[:, None])
    zeta = jnp.exp((CHUNK - 1.0 - idx)[None, :] * log_g[:, None])
    g_chunk = jnp.exp(CHUNK * log_g)
    bcast = lambda v: jnp.broadcast_to(v[:, :, None], (N_HEADS, CHUNK, HEAD_DIM))
    g_rows = jnp.broadcast_to(g_chunk[:, None, None], (N_HEADS, 8, HEAD_DIM))
    return cos2, sin2, decay, bcast(xi), bcast(zeta), g_rows


def _retention_fwd(proj, ret_gain, consts):
    l = proj.shape[0]
    n_chunks = l // CHUNK
    cos2, sin2, decay, xi, zeta, g_rows = consts
    scale = HEAD_DIM ** -0.5

    def body(p_ref, cos_ref, sin_ref, dec_ref, xi_ref, zeta_ref, gr_ref, gain_ref,
             mix_ref, o_ref, st_ref, state):
        c = pl.program_id(0)

        @pl.when(c == 0)
        def _():
            state[...] = jnp.zeros_like(state)

        cos_v, sin_v = cos_ref[...], sin_ref[...]
        valid = _row_valid(c, CHUNK)
        for h in range(N_HEADS):
            cols = slice(h * HEAD_DIM, (h + 1) * HEAD_DIM)
            q = p_ref[:, h * HEAD_DIM:(h + 1) * HEAD_DIM]
            k = p_ref[:, GROUP + h * HEAD_DIM:GROUP + (h + 1) * HEAD_DIM]
            v = p_ref[:, 2 * GROUP + h * HEAD_DIM:2 * GROUP + (h + 1) * HEAD_DIM]
            g = p_ref[:, 3 * GROUP + h * HEAD_DIM:3 * GROUP + (h + 1) * HEAD_DIM]
            rq = _rope(q, cos_v, sin_v).astype(MXU_DTYPE)
            rk = _rope(k, cos_v, sin_v) * scale
            rkb = rk.astype(MXU_DTYPE)
            vb = v.astype(MXU_DTYPE)
            st = state[h]
            st_ref[h] = st
            s = _dot_nt(rq, rkb) * dec_ref[h]
            o = _dot(s.astype(MXU_DTYPE), vb) + _dot(rq, st.astype(MXU_DTYPE)) * xi_ref[h]
            kz = (rk * zeta_ref[h]).astype(MXU_DTYPE)
            state[h] = gr_ref[h, 0:1, :] * st + _dot_tn(kz, vb)
            o_ref[:, cols] = o
            mu = jnp.mean(o, axis=-1, keepdims=True)
            oc = o - mu
            yn = oc * lax.rsqrt(jnp.mean(oc * oc, axis=-1, keepdims=True) + NORM_EPS)
            ret = (g * _sigmoid(g)) * (yn * gain_ref[:, cols])
            mix_ref[:, cols] = jnp.where(valid, ret, 0.0).astype(mix_ref.dtype)

    head_tab = pl.BlockSpec((N_HEADS, CHUNK, HEAD_DIM), lambda c: (0, 0, 0))
    return pl.pallas_call(
        body, name="retention_fwd",
        out_shape=(jax.ShapeDtypeStruct((l, GROUP), MXU_DTYPE), jax.ShapeDtypeStruct((l, GROUP), F32),
                   jax.ShapeDtypeStruct((n_chunks, N_HEADS, HEAD_DIM, HEAD_DIM), F32)),
        grid=(n_chunks,),
        in_specs=[pl.BlockSpec((CHUNK, 4 * GROUP), lambda c: (c, 0)),
                  pl.BlockSpec((CHUNK, HEAD_DIM), lambda c: (c, 0)),
                  pl.BlockSpec((CHUNK, HEAD_DIM), lambda c: (c, 0)),
                  head_tab, head_tab, head_tab,
                  pl.BlockSpec((N_HEADS, 8, HEAD_DIM), lambda c: (0, 0, 0)),
                  pl.BlockSpec((1, GROUP), lambda c: (0, 0))],
        out_specs=(pl.BlockSpec((CHUNK, GROUP), lambda c: (c, 0)),
                   pl.BlockSpec((CHUNK, GROUP), lambda c: (c, 0)),
                   pl.BlockSpec((None, N_HEADS, HEAD_DIM, HEAD_DIM), lambda c: (c, 0, 0, 0))),
        scratch_shapes=[pltpu.VMEM((N_HEADS, HEAD_DIM, HEAD_DIM), F32)],
        compiler_params=_params(("arbitrary",)),
    )(proj, cos2, sin2, decay, xi, zeta, g_rows, ret_gain)


def _retention_bwd(proj, o_pre, states, d_mix, ret_gain, consts):
    l = proj.shape[0]
    n_chunks = l // CHUNK
    cos2, sin2, decay, xi, zeta, g_rows = consts
    scale = HEAD_DIM ** -0.5
    rev = lambda c: n_chunks - 1 - c

    def body(p_ref, o_ref, st_ref, dm_ref, cos_ref, sin_ref, dec_ref, xi_ref, zeta_ref, gr_ref, gain_ref,
             dp_ref, dgain_ref, dstate):
        step = pl.program_id(0)

        @pl.when(step == 0)
        def _():
            dstate[...] = jnp.zeros_like(dstate)
            dgain_ref[...] = jnp.zeros_like(dgain_ref)

        cos_v, sin_v = cos_ref[...], sin_ref[...]
        valid = _row_valid(rev(step), CHUNK)
        for h in range(N_HEADS):
            cols = slice(h * HEAD_DIM, (h + 1) * HEAD_DIM)
            q = p_ref[:, h * HEAD_DIM:(h + 1) * HEAD_DIM]
            k = p_ref[:, GROUP + h * HEAD_DIM:GROUP + (h + 1) * HEAD_DIM]
            v = p_ref[:, 2 * GROUP + h * HEAD_DIM:2 * GROUP + (h + 1) * HEAD_DIM]
            g = p_ref[:, 3 * GROUP + h * HEAD_DIM:3 * GROUP + (h + 1) * HEAD_DIM]
            o = o_ref[:, cols]
            gain = gain_ref[:, cols]
            d_ret = jnp.where(valid, dm_ref[:, cols], 0.0)
            mu = jnp.mean(o, axis=-1, keepdims=True)
            oc = o - mu
            rstd = lax.rsqrt(jnp.mean(oc * oc, axis=-1, keepdims=True) + NORM_EPS)
            yn = oc * rstd
            sig = _sigmoid(g)
            gate = g * sig
            dgain_ref[:, cols] += jnp.sum(d_ret * gate * yn, axis=0, keepdims=True)
            d_g = d_ret * (yn * gain) * (sig * (1.0 + g * (1.0 - sig)))
            d_yn = d_ret * gate * gain
            d_o = rstd * (d_yn - jnp.mean(d_yn, axis=-1, keepdims=True)
                          - yn * jnp.mean(d_yn * yn, axis=-1, keepdims=True))
            rq = _rope(q, cos_v, sin_v)
            rk = _rope(k, cos_v, sin_v) * scale
            rqb, rkb, vb = rq.astype(MXU_DTYPE), rk.astype(MXU_DTYPE), v.astype(MXU_DTYPE)
            dob = d_o.astype(MXU_DTYPE)
            dec = dec_ref[h]
            xi_h, zeta_h = xi_ref[h], zeta_ref[h]
            st_b = st_ref[h].astype(MXU_DTYPE)
            dst = dstate[h]
            dst_b = dst.astype(MXU_DTYPE)
            s_b = (_dot_nt(rqb, rkb) * dec).astype(MXU_DTYPE)
            da_b = (_dot_nt(dob, vb) * dec).astype(MXU_DTYPE)
            doxi_b = (d_o * xi_h).astype(MXU_DTYPE)
            kz_b = (rk * zeta_h).astype(MXU_DTYPE)
            d_rq = _dot(da_b, rkb) + _dot_nt(doxi_b, st_b)
            d_rk = _dot_tn(da_b, rqb) + _dot_nt(vb, dst_b) * zeta_h
            d_v = _dot_tn(s_b, dob) + _dot(kz_b, dst_b)
            dstate[h] = gr_ref[h, 0:1, :] * dst + _dot_tn(rqb, doxi_b)
            d_q = _rope_bwd(d_rq, cos_v, sin_v)
            d_k = _rope_bwd(d_rk * scale, cos_v, sin_v)
            dp_ref[:, h * HEAD_DIM:(h + 1) * HEAD_DIM] = d_q.astype(dp_ref.dtype)
            dp_ref[:, GROUP + h * HEAD_DIM:GROUP + (h + 1) * HEAD_DIM] = d_k.astype(dp_ref.dtype)
            dp_ref[:, 2 * GROUP + h * HEAD_DIM:2 * GROUP + (h + 1) * HEAD_DIM] = d_v.astype(dp_ref.dtype)
            dp_ref[:, 3 * GROUP + h * HEAD_DIM:3 * GROUP + (h + 1) * HEAD_DIM] = d_g.astype(dp_ref.dtype)

    head_tab = pl.BlockSpec((N_HEADS, CHUNK, HEAD_DIM), lambda c: (0, 0, 0))
    return pl.pallas_call(
        body, name="retention_bwd",
        out_shape=(jax.ShapeDtypeStruct((l, 4 * GROUP), MXU_DTYPE), jax.ShapeDtypeStruct((1, GROUP), F32)),
        grid=(n_chunks,),
        in_specs=[pl.BlockSpec((CHUNK, 4 * GROUP), lambda c: (rev(c), 0)),
                  pl.BlockSpec((CHUNK, GROUP), lambda c: (rev(c), 0)),
                  pl.BlockSpec((None, N_HEADS, HEAD_DIM, HEAD_DIM), lambda c: (rev(c), 0, 0, 0)),
                  pl.BlockSpec((CHUNK, GROUP), lambda c: (rev(c), 0)),
                  pl.BlockSpec((CHUNK, HEAD_DIM), lambda c: (rev(c), 0)),
                  pl.BlockSpec((CHUNK, HEAD_DIM), lambda c: (rev(c), 0)),
                  head_tab, head_tab, head_tab,
                  pl.BlockSpec((N_HEADS, 8, HEAD_DIM), lambda c: (0, 0, 0)),
                  pl.BlockSpec((1, GROUP), lambda c: (0, 0))],
        out_specs=(pl.BlockSpec((CHUNK, 4 * GROUP), lambda c: (rev(c), 0)),
                   pl.BlockSpec((1, GROUP), lambda c: (0, 0))),
        scratch_shapes=[pltpu.VMEM((N_HEADS, HEAD_DIM, HEAD_DIM), F32)],
        compiler_params=_params(("arbitrary",)),
    )(proj, o_pre, states, d_mix, cos2, sin2, decay, xi, zeta, g_rows, ret_gain)


FF_TILE = (7 * GROUP) // 128


def _log_forget(ff, bias_row, valid):
    x = ff + bias_row
    e = jnp.exp(-jnp.abs(x))
    lf = jnp.minimum(x, 0.0) - jnp.log(1.0 + e)
    head_lane = lax.broadcasted_iota(jnp.int32, x.shape, 1) < N_HEADS
    keep = lambda t: jnp.where(head_lane, jnp.where(valid, t, 0.0), 0.0)
    return keep(lf), keep(jnp.where(x >= 0, e, 1.0) / (1.0 + e))


def _fox_prep(proj, bias_row):
    l = proj.shape[0]
    n_blocks = l // CHUNK

    def body(ff_ref, b_ref, bc_ref, rows_ref, cum):
        r = lax.broadcasted_iota(jnp.int32, (CHUNK, CHUNK), 0)
        cidx = lax.broadcasted_iota(jnp.int32, (CHUNK, CHUNK), 1)
        tri = jnp.where(r >= cidx, 1.0, 0.0).astype(F32)
        carry = jnp.zeros((1, 128), F32)
        for blk in range(n_blocks):
            rows = slice(blk * CHUNK, (blk + 1) * CHUNK)
            valid = _row_valid(blk, CHUNK)
            lf, _ = _log_forget(ff_ref[rows, :], b_ref[...], valid)
            local = jnp.dot(tri, lf, precision=lax.Precision.HIGHEST, preferred_element_type=F32) + carry
            carry = local[CHUNK - 1:CHUNK, :]
            masked = jnp.where(valid, local, -NEG_BIG)
            cum[rows, :] = masked
            t = masked.T
            for h in range(N_HEADS):
                rows_ref[h, :, rows] = t[h:h + 1, :]
        full = cum[...]
        for h in range(N_HEADS):
            bc_ref[h] = jnp.broadcast_to(full[:, h:h + 1], (l, 128))

    return pl.pallas_call(
        body, name="fox_prep",
        out_shape=(jax.ShapeDtypeStruct((N_HEADS, l, 128), F32), jax.ShapeDtypeStruct((N_HEADS, 1, l), F32)),
        grid=(1,),
        in_specs=[pl.BlockSpec((l, 128), lambda i: (0, FF_TILE)), pl.BlockSpec((1, 128), lambda i: (0, 0))],
        out_specs=(pl.BlockSpec((N_HEADS, l, 128), lambda i: (0, 0, 0)),
                   pl.BlockSpec((N_HEADS, 1, l), lambda i: (0, 0, 0))),
        scratch_shapes=[pltpu.VMEM((l, 128), F32)],
        compiler_params=_params(("arbitrary",)),
    )(proj, bias_row)


def _fox_fwd(proj, cum_bc, cum_rows):
    l = proj.shape[0]
    n_blocks = l // CHUNK
    scale = HEAD_DIM ** -0.5
    qt, kt, vt = 4 * N_HEADS, 5 * N_HEADS, 6 * N_HEADS

    def body(q_ref, k_ref, v_ref, cq_ref, ck_ref, o_ref):
        i = pl.program_id(1)
        qb = q_ref[...].astype(MXU_DTYPE)
        kb = k_ref[...].astype(MXU_DTYPE)
        vb = v_ref[...].astype(MXU_DTYPE)
        s = _dot_nt(qb, kb) * scale
        bias = jnp.tile(cq_ref[...], (1, n_blocks)) - ck_ref[...]
        q_pos = i * CHUNK + lax.broadcasted_iota(jnp.int32, (CHUNK, l), 0)
        k_pos = lax.broadcasted_iota(jnp.int32, (CHUNK, l), 1)
        s = jnp.where(k_pos <= q_pos, s + bias, NEG_BIG)
        m = jnp.max(s, axis=-1, keepdims=True)
        e = jnp.exp(s - m)
        p = e * (1.0 / jnp.sum(e, axis=-1, keepdims=True))
        o = _dot(p.astype(MXU_DTYPE), vb)
        o_ref[...] = jnp.where(_row_valid(i, CHUNK), o, 0.0).astype(o_ref.dtype)

    return pl.pallas_call(
        body, name="fox_fwd",
        out_shape=jax.ShapeDtypeStruct((l, GROUP), MXU_DTYPE),
        grid=(N_HEADS, n_blocks),
        in_specs=[pl.BlockSpec((CHUNK, HEAD_DIM), lambda h, i: (i, qt + h)),
                  pl.BlockSpec((l, HEAD_DIM), lambda h, i: (0, kt + h)),
                  pl.BlockSpec((l, HEAD_DIM), lambda h, i: (0, vt + h)),
                  pl.BlockSpec((None, CHUNK, 128), lambda h, i: (h, i, 0)),
                  pl.BlockSpec((None, 1, l), lambda h, i: (h, 0, 0))],
        out_specs=pl.BlockSpec((CHUNK, HEAD_DIM), lambda h, i: (i, h)),
        compiler_params=_params(("parallel", "parallel")),
    )(proj, proj, proj, cum_bc, cum_rows)


def _fox_bwd(proj, cum_bc, cum_rows, d_mix):
    l = proj.shape[0]
    n_blocks = l // CHUNK
    scale = HEAD_DIM ** -0.5
    qt, kt, vt = 4 * N_HEADS, 5 * N_HEADS, 6 * N_HEADS

    def body(q_ref, k_ref, v_ref, do_ref, ck_ref, cq_ref, dq_ref, dk_ref, dv_ref, ds_ref, dk_acc, dv_acc):
        i = pl.program_id(1)

        @pl.when(i == 0)
        def _():
            dk_acc[...] = jnp.zeros_like(dk_acc)
            dv_acc[...] = jnp.zeros_like(dv_acc)
            ds_ref[...] = jnp.zeros_like(ds_ref)

        qb = q_ref[...].astype(MXU_DTYPE)
        kb = k_ref[...].astype(MXU_DTYPE)
        vb = v_ref[...].astype(MXU_DTYPE)
        dob = jnp.where(_row_valid(i, CHUNK), do_ref[...], 0.0).astype(MXU_DTYPE)
        k_pos = lax.broadcasted_iota(jnp.int32, (l, CHUNK), 0)
        q_pos = i * CHUNK + lax.broadcasted_iota(jnp.int32, (l, CHUNK), 1)
        s_t = _dot_nt(kb, qb) * scale + (cq_ref[...] - ck_ref[...])
        s_t = jnp.where(k_pos <= q_pos, s_t, NEG_BIG)
        m = jnp.max(s_t, axis=0, keepdims=True)
        e = jnp.exp(s_t - m)
        p_t = e * (1.0 / jnp.sum(e, axis=0, keepdims=True))
        dp_t = _dot_nt(vb, dob)
        delta = jnp.sum(p_t * dp_t, axis=0, keepdims=True)
        ds_t = p_t * (dp_t - delta)
        ds_b = ds_t.astype(MXU_DTYPE)
        dv_acc[...] += _dot(p_t.astype(MXU_DTYPE), dob)
        dk_acc[...] += _dot(ds_b, qb) * scale
        ds_ref[...] += ds_t
        dq_ref[...] = (_dot_tn(ds_b, kb) * scale).astype(dq_ref.dtype)

        @pl.when(i == n_blocks - 1)
        def _():
            dk_ref[...] = dk_acc[...].astype(dk_ref.dtype)
            dv_ref[...] = dv_acc[...].astype(dv_ref.dtype)

    col = jax.ShapeDtypeStruct((l, GROUP), MXU_DTYPE)
    return pl.pallas_call(
        body, name="fox_bwd",
        out_shape=(col, col, col, jax.ShapeDtypeStruct((N_HEADS, l, 128), F32)),
        grid=(N_HEADS, n_blocks),
        in_specs=[pl.BlockSpec((CHUNK, HEAD_DIM), lambda h, i: (i, qt + h)),
                  pl.BlockSpec((l, HEAD_DIM), lambda h, i: (0, kt + h)),
                  pl.BlockSpec((l, HEAD_DIM), lambda h, i: (0, vt + h)),
                  pl.BlockSpec((CHUNK, HEAD_DIM), lambda h, i: (i, N_HEADS + h)),
                  pl.BlockSpec((None, l, 128), lambda h, i: (h, 0, 0)),
                  pl.BlockSpec((None, 1, CHUNK), lambda h, i: (h, 0, i))],
        out_specs=(pl.BlockSpec((CHUNK, HEAD_DIM), lambda h, i: (i, h)),
                   pl.BlockSpec((l, HEAD_DIM), lambda h, i: (0, h)),
                   pl.BlockSpec((l, HEAD_DIM), lambda h, i: (0, h)),
                   pl.BlockSpec((None, l, 128), lambda h, i: (h, 0, 0))),
        scratch_shapes=[pltpu.VMEM((l, HEAD_DIM), F32), pltpu.VMEM((l, HEAD_DIM), F32)],
        compiler_params=_params(("parallel", "arbitrary")),
    )(proj, proj, proj, d_mix, cum_bc, cum_rows)


def _fox_gate_bwd(ds_sum, proj, bias_row):
    l = proj.shape[0]
    n_blocks = l // CHUNK

    def body(ds_ref, ff_ref, b_ref, dff_ref, db_ref):
        r = lax.broadcasted_iota(jnp.int32, (CHUNK, CHUNK), 0)
        cidx = lax.broadcasted_iota(jnp.int32, (CHUNK, CHUNK), 1)
        upper = jnp.where(cidx >= r, 1.0, 0.0).astype(F32)
        carry = jnp.zeros((1, 128), F32)
        db = jnp.zeros((1, 128), F32)
        for blk in reversed(range(n_blocks)):
            rows = slice(blk * CHUNK, (blk + 1) * CHUNK)
            key_sum = jnp.zeros((CHUNK, 128), F32)
            for h in range(N_HEADS):
                select = jnp.where(cidx == h, 1.0, 0.0).astype(F32)
                key_sum = key_sum + jnp.dot(ds_ref[h, rows, :], select, precision=lax.Precision.HIGHEST,
                                            preferred_element_type=F32)
            suffix = jnp.dot(upper, key_sum, precision=lax.Precision.HIGHEST, preferred_element_type=F32) + carry
            carry = suffix[0:1, :]
            _, dsig = _log_forget(ff_ref[rows, :], b_ref[...], _row_valid(blk, CHUNK))
            dff = -suffix * dsig
            dff_ref[rows, :] = dff.astype(dff_ref.dtype)
            db = db + jnp.sum(dff, axis=0, keepdims=True)
        db_ref[...] = db

    return pl.pallas_call(
        body, name="fox_gate_bwd",
        out_shape=(jax.ShapeDtypeStruct((l, 128), MXU_DTYPE), jax.ShapeDtypeStruct((1, 128), F32)),
        grid=(1,),
        in_specs=[pl.BlockSpec((N_HEADS, l, 128), lambda i: (0, 0, 0)),
                  pl.BlockSpec((l, 128), lambda i: (0, FF_TILE)),
                  pl.BlockSpec((1, 128), lambda i: (0, 0))],
        out_specs=(pl.BlockSpec((l, 128), lambda i: (0, 0)), pl.BlockSpec((1, 128), lambda i: (0, 0))),
        compiler_params=_params(("arbitrary",)),
    )(ds_sum, proj, bias_row)


def _conv(u, w, b):
    return b + w[0:1, :] * pltpu.roll(u, 2, 0) + w[1:2, :] * pltpu.roll(u, 1, 0) + w[2:3, :] * u


def _conv_act_fwd(u, conv_w, conv_b, d_ff):
    l = u.shape[0]
    tc = _divisor_tile(d_ff, 256, 128)
    nt = d_ff // tc

    def body(ug_ref, uv_ref, wg_ref, wv_ref, bg_ref, bv_ref, a_ref):
        yg = _conv(ug_ref[...], wg_ref[...], bg_ref[...])
        yv = _conv(uv_ref[...], wv_ref[...], bv_ref[...])
        act = yg * _sigmoid(yg) * yv
        a_ref[...] = jnp.where(_row_valid(0, l), act, 0.0).astype(a_ref.dtype)

    return pl.pallas_call(
        body, name="conv_act_fwd",
        out_shape=jax.ShapeDtypeStruct((l, d_ff), MXU_DTYPE),
        grid=(nt,),
        in_specs=[pl.BlockSpec((l, tc), lambda j: (0, j)), pl.BlockSpec((l, tc), lambda j: (0, j + nt)),
                  pl.BlockSpec((8, tc), lambda j: (0, j)), pl.BlockSpec((8, tc), lambda j: (0, j + nt)),
                  pl.BlockSpec((1, tc), lambda j: (0, j)), pl.BlockSpec((1, tc), lambda j: (0, j + nt))],
        out_specs=pl.BlockSpec((l, tc), lambda j: (0, j)),
        compiler_params=_params(("parallel",)),
    )(u, u, conv_w, conv_w, conv_b, conv_b)


def _conv_act_bwd(u, conv_w, conv_b, d_act, d_ff):
    l = u.shape[0]
    tc = _divisor_tile(d_ff, 256, 128)
    nt = d_ff // tc

    def body(ug_ref, uv_ref, wg_ref, wv_ref, bg_ref, bv_ref, da_ref, du_ref, dwb_ref):
        valid = _row_valid(0, l)
        ug, uv = ug_ref[...], uv_ref[...]
        wg, wv = wg_ref[...], wv_ref[...]
        yg = _conv(ug, wg, bg_ref[...])
        yv = _conv(uv, wv, bv_ref[...])
        sig = _sigmoid(yg)
        da = jnp.where(valid, da_ref[...], 0.0)
        d_yv = da * (yg * sig)
        d_yg = da * yv * (sig * (1.0 + yg * (1.0 - sig)))
        for idx, (dy, uu, w) in enumerate(((d_yg, ug, wg), (d_yv, uv, wv))):
            du = w[2:3, :] * dy + w[1:2, :] * pltpu.roll(dy, l - 1, 0) + w[0:1, :] * pltpu.roll(dy, l - 2, 0)
            du_ref[idx] = jnp.where(valid, du, 0.0).astype(du_ref.dtype)
            dwb_ref[idx, 0:1, :] = jnp.sum(dy * pltpu.roll(uu, 2, 0), axis=0, keepdims=True)
            dwb_ref[idx, 1:2, :] = jnp.sum(dy * pltpu.roll(uu, 1, 0), axis=0, keepdims=True)
            dwb_ref[idx, 2:3, :] = jnp.sum(dy * uu, axis=0, keepdims=True)
            dwb_ref[idx, 3:4, :] = jnp.sum(dy, axis=0, keepdims=True)
            dwb_ref[idx, 4:8, :] = jnp.zeros((4, tc), F32)

    return pl.pallas_call(
        body, name="conv_act_bwd",
        out_shape=(jax.ShapeDtypeStruct((2, l, d_ff), MXU_DTYPE), jax.ShapeDtypeStruct((2, 8, d_ff), F32)),
        grid=(nt,),
        in_specs=[pl.BlockSpec((l, tc), lambda j: (0, j)), pl.BlockSpec((l, tc), lambda j: (0, j + nt)),
                  pl.BlockSpec((8, tc), lambda j: (0, j)), pl.BlockSpec((8, tc), lambda j: (0, j + nt)),
                  pl.BlockSpec((1, tc), lambda j: (0, j)), pl.BlockSpec((1, tc), lambda j: (0, j + nt)),
                  pl.BlockSpec((l, tc), lambda j: (0, j))],
        out_specs=(pl.BlockSpec((2, l, tc), lambda j: (0, 0, j)), pl.BlockSpec((2, 8, tc), lambda j: (0, 0, j))),
        compiler_params=_params(("parallel",)),
    )(u, u, conv_w, conv_w, conv_b, conv_b, d_act)


def _adamw(w, g, m, v, name):
    shape = w.shape
    if w.ndim == 1:
        as2d = (1, shape[0])
    else:
        as2d = (int(np.prod(shape[:-1])), shape[-1])
    r, c = as2d
    tr = _divisor_tile(r, 256, 8)
    spec = pl.BlockSpec((tr, c), lambda i: (i, 0))

    def body(w_ref, g_ref, m_ref, v_ref, d_ref, nm_ref, nv_ref):
        gv = g_ref[...]
        nm = ADAM_B1 * m_ref[...] + (1.0 - ADAM_B1) * gv
        nv = ADAM_B2 * v_ref[...] + (1.0 - ADAM_B2) * (gv * gv)
        m_hat = nm / (1.0 - ADAM_B1 ** ADAM_STEP)
        v_hat = nv / (1.0 - ADAM_B2 ** ADAM_STEP)
        d_ref[...] = -ADAM_LR * (m_hat / (jnp.sqrt(v_hat) + ADAM_EPS) + ADAM_WD * w_ref[...])
        nm_ref[...] = nm
        nv_ref[...] = nv

    sds = jax.ShapeDtypeStruct(as2d, F32)
    outs = pl.pallas_call(
        body, name=name, out_shape=(sds, sds, sds), grid=(r // tr,),
        in_specs=[spec] * 4, out_specs=(spec,) * 3,
        compiler_params=_params(("parallel",)),
    )(w.reshape(as2d), g.reshape(as2d), m.reshape(as2d), v.reshape(as2d))
    return tuple(o.reshape(shape) for o in outs)


def _pad_rows(a, rows):
    return jnp.pad(a, ((0, rows - a.shape[0]), (0, 0)))


def kernel(x, meta_tokens, norm1_gain, w_in, b_forget, ret_norm_gain, w_out, norm2_gain, w_up, conv_w, conv_b, w_down, final_norm_gain, loss_target, m_meta_tokens, m_norm1_gain, m_w_in, m_b_forget, m_ret_norm_gain, m_w_out, m_norm2_gain, m_w_up, m_conv_w, m_conv_b, m_w_down, m_final_norm_gain, v_meta_tokens, v_norm1_gain, v_w_in, v_b_forget, v_ret_norm_gain, v_w_out, v_norm2_gain, v_w_up, v_conv_w, v_conv_b, v_w_down, v_final_norm_gain):
    seq, d = x.shape[1], x.shape[2]
    l = CHUNK + seq
    d_ff = w_down.shape[1] * N_DEV
    up_shard = w_up.shape[2]
    assert 4 * up_shard == d_ff and w_in.shape[2] == WIN_SHARD and d == 2 * GROUP
    dev = _device_index()
    mx, my, mc = _my_position()
    core = jnp.reshape(mc, (1,)).astype(jnp.int32)
    chip = jnp.reshape(2 * mx + my, (1,)).astype(jnp.int32)

    small = jnp.concatenate([meta_tokens.reshape(-1, 128), conv_w[0].reshape(-1, 128)], axis=0)
    n_meta_rows = N_META * (d // N_DEV) // 128
    small_rows = small.shape[0]
    small_all = _all_gather(_pad_rows(small, -(-small_rows // 8) * 8), "gather_small")
    meta_full = jnp.transpose(small_all[:, :n_meta_rows].reshape(N_DEV, N_META, d // N_DEV), (1, 0, 2)).reshape(N_META, d)
    conv_w_full = _pad_rows(jnp.transpose(small_all[:, n_meta_rows:small_rows].reshape(N_DEV, 3, up_shard),
                                          (1, 0, 2)).reshape(3, 2 * d_ff), 8)
    w_in_window = lax.dynamic_update_slice(
        jnp.zeros((d, WIN_BLOCK), WIRE_DTYPE), w_in[0].astype(WIRE_DTYPE), (jnp.int32(0), dev.astype(jnp.int32)))
    start_in = _gather_start(w_in_window, small_all, "gather_w_in_start")

    h0 = jnp.concatenate([jnp.zeros((PAD_ROWS, d), F32), meta_full, x[0]], axis=0)
    consts = _retention_consts(l)
    bias_row = jnp.pad(b_forget, ((0, 0), (0, 128 - N_HEADS)))
    a = _rmsnorm_fwd(h0, norm1_gain + start_in[4][0, 0], "rmsnorm1")
    w_in_blocks = _gather_finish(start_in, a, "gather_w_in")
    start_out = _gather_start(w_out[0].astype(WIRE_DTYPE), w_in_blocks, "gather_w_out_start")
    w_in_full = _assemble_w_in(w_in_blocks).astype(MXU_DTYPE)
    proj = _mm_nn(a, w_in_full, F32, "mm_proj", after=start_out[4])
    w_out_blocks = _gather_finish(start_out, proj, "gather_w_out")
    start_up = _gather_start(w_up[0].astype(WIRE_DTYPE), w_out_blocks, "gather_w_up_start")
    ret_mix, ret_pre, ret_states = _retention_fwd(proj, ret_norm_gain + start_up[4][0, 0], consts)
    cum_bc, cum_rows = _fox_prep(proj, bias_row + start_up[4][0:1, :])
    fox_mix = _fox_fwd(proj, cum_bc, cum_rows)
    mix = jnp.concatenate([ret_mix, fox_mix], axis=1)
    w_out_full = w_out_blocks.reshape(d, d).astype(MXU_DTYPE)
    h1, cn = _rmsnorm_fwd(h0, norm2_gain, "resid_rmsnorm2", res=_mm_nn(mix, w_out_full, F32, "mm_out"))
    w_up_blocks = _gather_finish(start_up, cn, "gather_w_up").astype(MXU_DTYPE)
    start_down = _gather_start(w_down[0].astype(WIRE_DTYPE), w_up_blocks, "gather_w_down_start")
    u = _mm(cn, w_up_blocks,
            a_spec=pl.BlockSpec((_divisor_tile(l, 1088, 16), d), lambda i, j, k: (i, 0)),
            b_spec=pl.BlockSpec((None, d, up_shard), lambda i, j, k: (j, 0, 0)),
            o_spec=pl.BlockSpec((_divisor_tile(l, 1088, 16), up_shard), lambda i, j, k: (i, j)),
            out_shape=jax.ShapeDtypeStruct((l, 2 * d_ff), F32),
            grid=(l // _divisor_tile(l, 1088, 16), N_DEV, 1), contract=(1, 0), nk=1, name="mm_up",
            after=start_down[4])
    act = _conv_act_fwd(u, conv_w_full, conv_b + start_down[4][0, 0], d_ff)
    w_down_full = _gather_finish(start_down, act, "gather_w_down").reshape(d_ff, d).astype(MXU_DTYPE)
    mlp_out = _mm_nn(act, w_down_full, F32, "mm_down", tk_cap=1408)
    d_h2, d_h2_b, dg_final, loss_part = _loss_head(h1, mlp_out, final_norm_gain.reshape(1, d), loss_target[0])

    gw_down = _mm_tn(act, d_h2_b, WIRE_DTYPE, "mm_gw_down", tm_cap=1408, tn_cap=1024)
    rs_down = _reduce_scatter_start(gw_down.reshape(N_DEV, d_ff // N_DEV, d), core, "rs_w_down")
    d_act = _mm_nt(d_h2_b, w_down_full, F32, "mm_d_act", after=rs_down[4])
    d_u, d_conv = _conv_act_bwd(u, conv_w_full, conv_b + rs_down[4][0, 0], d_act, d_ff)
    tm = _divisor_tile(l, 1088, 16)
    gw_up = _mm(cn, d_u,
                a_spec=pl.BlockSpec((l, d // 2), lambda i, j, k: (0, i)),
                b_spec=pl.BlockSpec((None, l, up_shard), lambda i, j, k: (j // 4, 0, j % 4)),
                o_spec=pl.BlockSpec((None, d // 2, up_shard), lambda i, j, k: (j, i, 0)),
                out_shape=jax.ShapeDtypeStruct((N_DEV, d, up_shard), WIRE_DTYPE),
                grid=(2, N_DEV, 1), contract=(0, 0), nk=1, name="mm_gw_up")
    rs_up = _reduce_scatter_start(gw_up, core, "rs_w_up")
    d_cn = _mm(d_u, w_up_blocks,
               a_spec=pl.BlockSpec((None, tm, up_shard), lambda i, j, k: (k // 4, i, k % 4)),
               b_spec=pl.BlockSpec((None, d // 2, up_shard), lambda i, j, k: (k, j, 0)),
               o_spec=pl.BlockSpec((tm, d // 2), lambda i, j, k: (i, j)),
               out_shape=jax.ShapeDtypeStruct((l, d), F32),
               grid=(l // tm, 2, N_DEV), contract=(1, 1), nk=N_DEV, name="mm_d_cn", after=rs_up[4])
    d_h1, d_h1_b, dg_norm2 = _rmsnorm_bwd(d_h2, d_cn, h1, norm2_gain + rs_up[4][0, 0], "rmsnorm2_bwd", True)

    gw_out = _mm_tn(mix, d_h1_b, WIRE_DTYPE, "mm_gw_out")
    rs_out = _reduce_scatter_start(gw_out.reshape(N_DEV, d // N_DEV, d), core, "rs_w_out")
    d_mix = _mm_nt(d_h1_b, w_out_full, F32, "mm_d_mix", after=rs_out[4])
    d_fq, d_fk, d_fv, ds_sum = _fox_bwd(proj, cum_bc, cum_rows, d_mix)
    d_ff_tile, db_forget_row = _fox_gate_bwd(ds_sum, proj, bias_row)
    d_ret, dg_ret = _retention_bwd(proj, ret_pre, ret_states, d_mix, ret_norm_gain + rs_out[4][0, 0], consts)
    d_proj = jnp.concatenate(
        [d_ret, d_fq, d_fk, d_fv, d_ff_tile, jnp.zeros((l, WIN_N - 7 * GROUP - 128), MXU_DTYPE)], axis=1)
    gw_in = _mm_tn(a, d_proj, WIRE_DTYPE, "mm_gw_in")
    rs_in = _reduce_scatter_start(_extract_w_in_windows(gw_in), core, "rs_w_in")
    d_a = _mm_nt(d_proj, w_in_full, F32, "mm_d_a", tk_cap=1536, after=rs_in[4])
    d_h0, dg_norm1 = _rmsnorm_bwd(d_h1, d_a, h0, norm1_gain + rs_in[4][0, 0], "rmsnorm1_bwd", False)
    grad_x = d_h0[CHUNK:][None]
    d_meta = d_h0[PAD_ROWS:CHUNK]

    d_conv_w = jnp.concatenate([d_conv[0, 0:3], d_conv[1, 0:3]], axis=1)
    d_conv_b = jnp.concatenate([d_conv[0, 3:4], d_conv[1, 3:4]], axis=1)
    pieces = [loss_part[:, 0:1], dg_norm1, db_forget_row[:, 0:N_HEADS], dg_ret, dg_norm2, d_conv_b, dg_final,
              d_meta.reshape(1, -1), d_conv_w.reshape(1, -1)]
    sizes = [p.shape[1] for p in pieces]
    flat = jnp.concatenate(pieces, axis=1)
    padded = -(-flat.shape[1] // 1024) * 1024
    flat = jnp.pad(flat, ((0, 0), (0, padded - flat.shape[1]))).reshape(padded // 128, 128)
    small_ar = _small_all_reduce_start(flat, d_h0, "all_reduce_small")

    g_w_down = _reduce_scatter_finish(rs_down, small_ar[4], chip, "rs_w_down")[None]
    g_w_up = _reduce_scatter_finish(rs_up, g_w_down, chip, "rs_w_up")[None]
    g_w_out = _reduce_scatter_finish(rs_out, g_w_up, chip, "rs_w_out")[None]
    early = [_adamw(w, g, m, v, "adamw_" + n) for w, g, m, v, n in (
        (w_down, g_w_down, m_w_down, v_w_down, "w_down"), (w_up, g_w_up, m_w_up, v_w_up, "w_up"),
        (w_out, g_w_out, m_w_out, v_w_out, "w_out"))]
    g_w_in_window = _reduce_scatter_finish(rs_in, early[1][2], chip, "rs_w_in")
    g_w_in = lax.dynamic_slice(g_w_in_window, (jnp.int32(0), dev.astype(jnp.int32)), (d, WIN_SHARD))[None]
    early.append(_adamw(w_in, g_w_in, m_w_in, v_w_in, "adamw_w_in"))
    total = _small_all_reduce_finish(small_ar, early[3][2], jnp.reshape(dev, (1,)).astype(jnp.int32),
                                     "all_reduce_small").reshape(1, padded)
    offs = np.concatenate([[0], np.cumsum(sizes)])
    take = lambda k: total[:, int(offs[k]):int(offs[k + 1])]
    loss = take(0).reshape(())
    g_norm1, g_bf, g_ret_gain, g_norm2 = take(1), take(2), take(3), take(4)
    g_conv_b, g_final = take(5), take(6).reshape(d)
    g_meta = lax.dynamic_slice(take(7).reshape(N_META, d), (jnp.int32(0), (dev * (d // N_DEV)).astype(jnp.int32)),
                               (N_META, d // N_DEV))
    g_conv_w = lax.dynamic_slice(take(8).reshape(3, 2 * d_ff), (jnp.int32(0), (dev * up_shard).astype(jnp.int32)),
                                 (3, up_shard))[None]

    weights = [meta_tokens, norm1_gain, w_in, b_forget, ret_norm_gain, w_out, norm2_gain, w_up, conv_w, conv_b,
               w_down, final_norm_gain]
    grads = [g_meta, g_norm1, g_w_in, g_bf, g_ret_gain, g_w_out, g_norm2, g_w_up, g_conv_w, g_conv_b, g_w_down,
             g_final]
    done = {"w_down": early[0], "w_up": early[1], "w_out": early[2], "w_in": early[3]}
    ms = [m_meta_tokens, m_norm1_gain, m_w_in, m_b_forget, m_ret_norm_gain, m_w_out, m_norm2_gain, m_w_up, m_conv_w,
          m_conv_b, m_w_down, m_final_norm_gain]
    vs = [v_meta_tokens, v_norm1_gain, v_w_in, v_b_forget, v_ret_norm_gain, v_w_out, v_norm2_gain, v_w_up, v_conv_w,
          v_conv_b, v_w_down, v_final_norm_gain]
    names = ["meta", "norm1", "w_in", "b_forget", "ret_gain", "w_out", "norm2", "w_up", "conv_w", "conv_b", "w_down",
             "final_gain"]
    deltas, new_ms, new_vs = [], [], []
    for w, g, m, v, n in zip(weights, grads, ms, vs, names):
        dl, nm, nv = done[n] if n in done else _adamw(w, g, m, v, "adamw_" + n)
        deltas.append(dl)
        new_ms.append(nm)
        new_vs.append(nv)
    return (loss, grad_x, *grads, *deltas, *new_ms, *new_vs)
```

```python
import functools

import numpy as np
import jax
import jax.numpy as jnp
from jax import lax
from jax.experimental import pallas as pl
from jax.experimental.pallas import tpu as pltpu

F32 = jnp.float32
MXU_DTYPE = jnp.bfloat16
WIRE_DTYPE = jnp.bfloat16

N_DEV = 8
N_META = 16
CHUNK = 128
PAD_ROWS = CHUNK - N_META
N_HEADS = 8
HEAD_DIM = 128
GROUP = N_HEADS * HEAD_DIM
IN_DIM = 7 * GROUP + N_HEADS
WIN_SHARD = IN_DIM // N_DEV
WIN_BLOCK = 1024
WIN_STRIDE = 896
WIN_N = 7680
ROPE_BASE = 10000.0
NORM_EPS = 1e-6
NEG_BIG = -1e30
ADAM_LR, ADAM_B1, ADAM_B2, ADAM_EPS, ADAM_WD, ADAM_STEP = 0.001, 0.9, 0.999, 1e-08, 0.01, 10
VMEM_LIMIT = 52 * 1024 * 1024
MESH = pl.DeviceIdType.MESH
ANY = pl.BlockSpec(memory_space=pl.ANY)
VMEM_SPEC = pl.BlockSpec(memory_space=pltpu.VMEM)


def _params(sem=None):
    kw = {"vmem_limit_bytes": VMEM_LIMIT}
    if sem is not None:
        kw["dimension_semantics"] = sem
    return pltpu.CompilerParams(**kw)


def _divisor_tile(n, cap, unit):
    if n <= cap:
        return n
    best = None
    for t in range(unit, cap + 1, unit):
        if n % t == 0:
            best = t
    assert best is not None, (n, cap, unit)
    return best


def _my_position():
    return lax.axis_index("x"), lax.axis_index("y"), lax.axis_index("c")


def _device_index():
    x, y, c = _my_position()
    return 4 * x + 2 * y + c


def _all_gather(shard, name):
    r, c = shard.shape

    def body(x_ref, out_ref, send_sems, recv_sems, local_sem):
        mx, my, mc = _my_position()
        me, sibling = (mx, my, mc), (mx, my, 1 - mc)
        chips = [(1 - mx, my), (mx, 1 - my), (1 - mx, 1 - my)]

        def slot(px, py, pc):
            return out_ref.at[4 * px + 2 * py + pc]

        def copy(k, block, to, src=None):
            return pltpu.make_async_remote_copy(
                src_ref=slot(*block) if src is None else src, dst_ref=slot(*block),
                send_sem=send_sems.at[k], recv_sem=recv_sems.at[k], device_id=to, device_id_type=MESH)

        mine = pltpu.make_async_copy(x_ref, slot(*me), local_sem)
        mine.start()
        first = [copy(0, me, sibling, src=x_ref)]
        first += [copy(1 + j, me, (*chip, mc), src=x_ref) for j, chip in enumerate(chips)]
        for cp in first:
            cp.start()
        passed = [copy(4 + j, (*chip, mc), sibling) for j, chip in enumerate(chips)]
        for j, chip in enumerate(chips):
            copy(1 + j, (*chip, mc), me).wait_recv()
            passed[j].start()
        copy(0, sibling, me).wait_recv()
        for j, chip in enumerate(chips):
            copy(4 + j, (*chip, 1 - mc), me).wait_recv()
        for cp in first + passed:
            cp.wait_send()
        mine.wait()

    return pl.pallas_call(
        body, name=name,
        out_shape=jax.ShapeDtypeStruct((N_DEV, r, c), shard.dtype),
        in_specs=[ANY], out_specs=ANY,
        scratch_shapes=[pltpu.SemaphoreType.DMA((7,)), pltpu.SemaphoreType.DMA((7,)), pltpu.SemaphoreType.DMA],
    )(shard)


HBM_SPEC = pl.BlockSpec(memory_space=pltpu.HBM)
SEM_SPEC = pl.BlockSpec(memory_space=pltpu.SEMAPHORE)
DATAFLOW_EFFECT = pltpu.SideEffectType.DATAFLOW_SIDE_EFFECTING


def _in_hbm(a):
    return pltpu.with_memory_space_constraint(a, pltpu.HBM)


def _split_start(src, land, make_copies, n_copies, after, name):
    if isinstance(land, tuple):
        land = lax.empty(land, src.dtype)
    land_shape = land.shape
    def body(src_ref, land_ref, after_ref, send_sems, recv_sems, src_thru, land_thru, token):
        for cp in make_copies(src_ref, land_ref, send_sems, recv_sems):
            cp.start()
        token[...] = jnp.zeros_like(token)

    return pl.pallas_call(
        body, name=name,
        out_shape=(pltpu.SemaphoreType.DMA((n_copies,)), pltpu.SemaphoreType.DMA((n_copies,)),
                   pltpu.HBM(src.shape, src.dtype), pltpu.HBM(land_shape, src.dtype),
                   jax.ShapeDtypeStruct((8, 128), F32)),
        in_specs=(HBM_SPEC, HBM_SPEC, ANY), out_specs=(SEM_SPEC, SEM_SPEC, HBM_SPEC, HBM_SPEC, VMEM_SPEC),
        input_output_aliases={0: 2, 1: 3},
        compiler_params=pltpu.CompilerParams(has_side_effects=DATAFLOW_EFFECT),
    )(_in_hbm(src), _in_hbm(land), after)


def _split_wait(started, after, make_copies, name):
    send_sems, recv_sems, src_thru, land_thru, _ = started

    def body(src_ref, land_ref, send_sems_ref, recv_sems_ref, after_ref, src_dead, land_out):
        for cp in make_copies(src_ref, land_ref, send_sems_ref, recv_sems_ref):
            cp.wait_send()
            cp.wait_recv()

    return pl.pallas_call(
        body, name=name,
        out_shape=(pltpu.HBM(src_thru.shape, src_thru.dtype), pltpu.HBM(land_thru.shape, land_thru.dtype)),
        in_specs=(HBM_SPEC, HBM_SPEC, SEM_SPEC, SEM_SPEC, ANY), out_specs=(HBM_SPEC, HBM_SPEC),
        input_output_aliases={0: 0, 1: 1},
        compiler_params=pltpu.CompilerParams(has_side_effects=DATAFLOW_EFFECT),
    )(src_thru, land_thru, send_sems, recv_sems, after)


def _gather_copies(x_ref, land_ref, send_sems, recv_sems):
    mx, my, mc = _my_position()
    me = 4 * mx + 2 * my + mc
    targets = [(mx, my, 1 - mc), (1 - mx, my, mc), (mx, 1 - my, mc), (1 - mx, 1 - my, mc)]
    return [pltpu.make_async_remote_copy(
        src_ref=x_ref, dst_ref=land_ref.at[me], send_sem=send_sems.at[k], recv_sem=recv_sems.at[k],
        device_id=t, device_id_type=MESH) for k, t in enumerate(targets)]


def _gather_start(shard, dev, after, name):
    r, c = shard.shape
    tr = _divisor_tile(r, 256, 16)

    def body(s_ref, x_ref, o_ref):
        o_ref[...] = x_ref[...]

    land = pl.pallas_call(
        body, name=name + "_own",
        out_shape=jax.ShapeDtypeStruct((N_DEV, r, c), shard.dtype),
        grid_spec=pltpu.PrefetchScalarGridSpec(
            num_scalar_prefetch=1, grid=(r // tr,),
            in_specs=[pl.BlockSpec((tr, c), lambda i, s: (i, 0))],
            out_specs=pl.BlockSpec((None, tr, c), lambda i, s: (s[0], i, 0))),
        compiler_params=_params(("parallel",)),
    )(dev, shard)
    return _split_start(shard, land, _gather_copies, 4, after, name)


def _gather_finish(started, after, name):
    _, land = _split_wait(started, after, _gather_copies, name + "_wait")

    def body(land_in, land_ref, send_sems, recv_sems):
        mx, my, mc = _my_position()
        chips = [(1 - mx, my), (mx, 1 - my), (1 - mx, 1 - my)]
        copies = [pltpu.make_async_remote_copy(
            src_ref=land_ref.at[4 * cx + 2 * cy + mc], dst_ref=land_ref.at[4 * cx + 2 * cy + mc],
            send_sem=send_sems.at[j], recv_sem=recv_sems.at[j],
            device_id=(mx, my, 1 - mc), device_id_type=MESH) for j, (cx, cy) in enumerate(chips)]
        for cp in copies:
            cp.start()
        for j, (cx, cy) in enumerate(chips):
            copies[j].wait_send()
            pltpu.make_async_remote_copy(
                src_ref=land_ref.at[4 * cx + 2 * cy + 1 - mc], dst_ref=land_ref.at[4 * cx + 2 * cy + 1 - mc],
                send_sem=send_sems.at[j], recv_sem=recv_sems.at[j],
                device_id=(mx, my, 1 - mc), device_id_type=MESH).wait_recv()

    return pl.pallas_call(
        body, name=name + "_pass",
        out_shape=jax.ShapeDtypeStruct(land.shape, land.dtype),
        in_specs=[ANY], out_specs=ANY,
        input_output_aliases={0: 0},
        scratch_shapes=[pltpu.SemaphoreType.DMA((3,)), pltpu.SemaphoreType.DMA((3,))],
    )(land)


def _chip_copies(p_ref, land_ref, send_sems, recv_sems):
    mx, my, mc = _my_position()
    chips = [(1 - mx, my), (mx, 1 - my), (1 - mx, 1 - my)]
    return [pltpu.make_async_remote_copy(
        src_ref=p_ref.at[2 * cx + cy], dst_ref=land_ref.at[j], send_sem=send_sems.at[j], recv_sem=recv_sems.at[j],
        device_id=(cx, cy, mc), device_id_type=MESH) for j, (cx, cy) in enumerate(chips)]


def _reduce_scatter_start(g, core, name):
    pair = _pair_sum(g, _exchange_sibling(g, name + "_d2d"), core, name + "_pairsum")
    return _split_start(pair, (3,) + pair.shape[1:], _chip_copies, 3, g, name + "_ici_start")


def _reduce_scatter_finish(started, after, chip, name):
    pair, from_chips = _split_wait(started, after, _chip_copies, name + "_ici_wait")
    return _final_sum(pair, from_chips, chip, name + "_sum")


def _exchange_sibling(g, name):
    _, r, c = g.shape

    def body(g_ref, out_ref, send_sems, recv_sems):
        mx, my, mc = _my_position()
        copies = [
            pltpu.make_async_remote_copy(
                src_ref=g_ref.at[2 * k + (1 - mc)], dst_ref=out_ref.at[k],
                send_sem=send_sems.at[k], recv_sem=recv_sems.at[k],
                device_id=(mx, my, 1 - mc), device_id_type=MESH)
            for k in range(4)]
        for cp in copies:
            cp.start()
        for cp in copies:
            cp.wait()

    return pl.pallas_call(
        body, name=name,
        out_shape=jax.ShapeDtypeStruct((4, r, c), g.dtype),
        in_specs=[ANY], out_specs=ANY,
        scratch_shapes=[pltpu.SemaphoreType.DMA((4,)), pltpu.SemaphoreType.DMA((4,))],
    )(g)


def _pair_sum(g, recv, core, name):
    _, r, c = g.shape
    tr = _divisor_tile(r, 256, 16)

    def body(s_ref, g_ref, r_ref, o_ref):
        o_ref[...] = (g_ref[...].astype(F32) + r_ref[...].astype(F32)).astype(o_ref.dtype)

    return pl.pallas_call(
        body, name=name,
        out_shape=jax.ShapeDtypeStruct((4, r, c), g.dtype),
        grid_spec=pltpu.PrefetchScalarGridSpec(
            num_scalar_prefetch=1, grid=(4, r // tr),
            in_specs=[pl.BlockSpec((None, tr, c), lambda k, i, s: (2 * k + s[0], i, 0)),
                      pl.BlockSpec((None, tr, c), lambda k, i, s: (k, i, 0))],
            out_specs=pl.BlockSpec((None, tr, c), lambda k, i, s: (k, i, 0))),
        compiler_params=_params(("parallel", "parallel")),
    )(core, g, recv)


def _final_sum(p, recv, chip, name):
    _, r, c = p.shape
    tr = _divisor_tile(r, 256, 16)

    def body(s_ref, p_ref, r_ref, o_ref):
        acc = p_ref[...].astype(F32)
        for j in range(3):
            acc = acc + r_ref[j].astype(F32)
        o_ref[...] = acc

    return pl.pallas_call(
        body, name=name,
        out_shape=jax.ShapeDtypeStruct((r, c), F32),
        grid_spec=pltpu.PrefetchScalarGridSpec(
            num_scalar_prefetch=1, grid=(r // tr,),
            in_specs=[pl.BlockSpec((None, tr, c), lambda i, s: (s[0], i, 0)),
                      pl.BlockSpec((3, tr, c), lambda i, s: (0, i, 0))],
            out_specs=pl.BlockSpec((tr, c), lambda i, s: (i, 0))),
        compiler_params=_params(("parallel",)),
    )(chip, p, recv)


def _all_to_all_copies(v_ref, land_ref, send_sems, recv_sems):
    mx, my, mc = _my_position()
    me = 4 * mx + 2 * my + mc
    copies = []
    for rel in range(1, N_DEV):
        bx, by, bc = (rel >> 2) & 1, (rel >> 1) & 1, rel & 1
        target = (1 - mx if bx else mx, 1 - my if by else my, 1 - mc if bc else mc)
        copies.append(pltpu.make_async_remote_copy(
            src_ref=v_ref, dst_ref=land_ref.at[me], send_sem=send_sems.at[rel - 1], recv_sem=recv_sems.at[rel - 1],
            device_id=target, device_id_type=MESH))
    return copies


def _small_all_reduce_start(v, after, name):
    return _split_start(v, (N_DEV,) + v.shape, _all_to_all_copies, N_DEV - 1, after, name + "_start")


def _small_all_reduce_finish(started, after, dev, name):
    v, land = _split_wait(started, after, _all_to_all_copies, name + "_wait")
    rows = v.shape[0]

    def body(me_ref, v_ref, land_ref, o_ref):
        for j in range(N_DEV):
            @pl.when(me_ref[0] == j)
            def _():
                o_ref[...] = v_ref[...] if j == 0 else o_ref[...] + v_ref[...]

            @pl.when(me_ref[0] != j)
            def _():
                o_ref[...] = land_ref[j] if j == 0 else o_ref[...] + land_ref[j]

    return pl.pallas_call(
        body, name=name + "_sum",
        out_shape=jax.ShapeDtypeStruct((rows, 128), F32),
        grid_spec=pltpu.PrefetchScalarGridSpec(
            num_scalar_prefetch=1, grid=(1,),
            in_specs=[pl.BlockSpec((rows, 128), lambda i, s: (0, 0)),
                      pl.BlockSpec((N_DEV, rows, 128), lambda i, s: (0, 0, 0))],
            out_specs=pl.BlockSpec((rows, 128), lambda i, s: (0, 0))),
        compiler_params=_params(("arbitrary",)),
    )(dev, v, land)


def _assemble_w_in(blocks):
    _, d, _ = blocks.shape
    tr = _divisor_tile(d, 128, 16)
    n_tiles = WIN_N // 128
    last = (N_DEV * WIN_STRIDE) // 128

    def body(b_ref, o_ref):
        win = []
        for i in range(N_DEV):
            w = b_ref[i].astype(F32)
            win.append(pltpu.roll(w, i, 1) if i else w)
        for t in range(n_tiles):
            if t > last:
                o_ref[:, t * 128:(t + 1) * 128] = jnp.zeros((tr, 128), o_ref.dtype)
                continue
            i = min(t // 7, N_DEV - 1)
            k = t - 7 * i
            val = win[i][:, k * 128:(k + 1) * 128]
            if k == 0 and i >= 1:
                val = val + win[i - 1][:, 7 * 128:8 * 128]
            o_ref[:, t * 128:(t + 1) * 128] = val.astype(o_ref.dtype)

    return pl.pallas_call(
        body, name="assemble_w_in",
        out_shape=jax.ShapeDtypeStruct((d, WIN_N), blocks.dtype),
        grid=(d // tr,),
        in_specs=[pl.BlockSpec((N_DEV, tr, WIN_BLOCK), lambda i: (0, i, 0))],
        out_specs=pl.BlockSpec((tr, WIN_N), lambda i: (i, 0)),
        compiler_params=_params(("parallel",)),
    )(blocks)


def _extract_w_in_windows(g):
    d, _ = g.shape
    tr = _divisor_tile(d, 128, 16)

    def body(g_ref, o_ref):
        for j in range(N_DEV):
            w = g_ref[:, WIN_STRIDE * j:WIN_STRIDE * j + WIN_BLOCK].astype(F32)
            o_ref[j] = (pltpu.roll(w, WIN_BLOCK - j, 1) if j else w).astype(o_ref.dtype)

    return pl.pallas_call(
        body, name="extract_w_in_windows",
        out_shape=jax.ShapeDtypeStruct((N_DEV, d, WIN_BLOCK), g.dtype),
        grid=(d // tr,),
        in_specs=[pl.BlockSpec((tr, WIN_N), lambda i: (i, 0))],
        out_specs=pl.BlockSpec((N_DEV, tr, WIN_BLOCK), lambda i: (0, i, 0)),
        compiler_params=_params(("parallel",)),
    )(g)


def _mm(a, b, *, a_spec, b_spec, o_spec, out_shape, grid, contract, nk, name, after=None):
    dn = (((contract[0],), (contract[1],)), ((), ()))
    tm, tn = o_spec.block_shape[-2:]
    behind = [] if after is None else [after]

    def body(a_ref, b_ref, *rest):
        o_ref, *scratch = rest[len(behind):]
        part = lax.dot_general(a_ref[...], b_ref[...], dn, preferred_element_type=F32)
        if nk == 1:
            o_ref[...] = part.astype(o_ref.dtype)
            return
        acc = scratch[0]
        k = pl.program_id(2)

        @pl.when(k == 0)
        def _():
            acc[...] = part

        @pl.when(k > 0)
        def _():
            acc[...] += part

        @pl.when(k == nk - 1)
        def _():
            o_ref[...] = acc[...].astype(o_ref.dtype)

    return pl.pallas_call(
        body, name=name, out_shape=out_shape, grid=grid,
        in_specs=[a_spec, b_spec] + [ANY] * len(behind), out_specs=o_spec,
        scratch_shapes=[] if nk == 1 else [pltpu.VMEM((tm, tn), F32)],
        compiler_params=_params(("parallel", "parallel", "arbitrary")),
    )(a, b, *behind)


def _mm_nn(a, b, out_dtype, name, tm_cap=1088, tn_cap=512, tk_cap=2048, after=None):
    m, k = a.shape
    _, n = b.shape
    tm, tn, tk = _divisor_tile(m, tm_cap, 16), _divisor_tile(n, tn_cap, 128), _divisor_tile(k, tk_cap, 128)
    return _mm(a, b,
               a_spec=pl.BlockSpec((tm, tk), lambda i, j, kk: (i, kk)),
               b_spec=pl.BlockSpec((tk, tn), lambda i, j, kk: (kk, j)),
               o_spec=pl.BlockSpec((tm, tn), lambda i, j, kk: (i, j)),
               out_shape=jax.ShapeDtypeStruct((m, n), out_dtype),
               grid=(m // tm, n // tn, k // tk), contract=(1, 0), nk=k // tk, name=name, after=after)


def _mm_nt(a, b, out_dtype, name, tm_cap=1088, tn_cap=512, tk_cap=2048, after=None):
    m, k = a.shape
    n, _ = b.shape
    tm, tn, tk = _divisor_tile(m, tm_cap, 16), _divisor_tile(n, tn_cap, 128), _divisor_tile(k, tk_cap, 128)
    return _mm(a, b,
               a_spec=pl.BlockSpec((tm, tk), lambda i, j, kk: (i, kk)),
               b_spec=pl.BlockSpec((tn, tk), lambda i, j, kk: (j, kk)),
               o_spec=pl.BlockSpec((tm, tn), lambda i, j, kk: (i, j)),
               out_shape=jax.ShapeDtypeStruct((m, n), out_dtype),
               grid=(m // tm, n // tn, k // tk), contract=(1, 1), nk=k // tk, name=name, after=after)


def _mm_tn(a, b, out_dtype, name, tm_cap=1024, tn_cap=512, after=None):
    l, m = a.shape
    _, n = b.shape
    tm, tn = _divisor_tile(m, tm_cap, 128), _divisor_tile(n, tn_cap, 128)
    return _mm(a, b,
               a_spec=pl.BlockSpec((l, tm), lambda i, j, kk: (0, i)),
               b_spec=pl.BlockSpec((l, tn), lambda i, j, kk: (0, j)),
               o_spec=pl.BlockSpec((tm, tn), lambda i, j, kk: (i, j)),
               out_shape=jax.ShapeDtypeStruct((m, n), out_dtype),
               grid=(m // tm, n // tn, 1), contract=(0, 0), nk=1, name=name, after=after)


def _row_tile(l):
    return _divisor_tile(l, 544, 8)


def _rmsnorm_fwd(h, gain, name, res=None):
    l, d = h.shape
    tr = _row_tile(l)
    row = pl.BlockSpec((tr, d), lambda i: (i, 0))
    vec = pl.BlockSpec((1, d), lambda i: (0, 0))

    def body(*refs):
        if res is None:
            h_ref, g_ref, n_ref = refs
            x = h_ref[...]
        else:
            h_ref, r_ref, g_ref, s_ref, n_ref = refs
            x = h_ref[...] + r_ref[...]
            s_ref[...] = x
        y = x * lax.rsqrt(jnp.mean(x * x, axis=-1, keepdims=True) + NORM_EPS)
        n_ref[...] = (y * g_ref[...]).astype(n_ref.dtype)

    normed = jax.ShapeDtypeStruct((l, d), MXU_DTYPE)
    if res is None:
        return pl.pallas_call(body, name=name, out_shape=normed, grid=(l // tr,), in_specs=[row, vec],
                              out_specs=row, compiler_params=_params(("parallel",)))(h, gain)
    return pl.pallas_call(body, name=name, out_shape=(jax.ShapeDtypeStruct((l, d), F32), normed),
                          grid=(l // tr,), in_specs=[row, row, vec], out_specs=(row, row),
                          compiler_params=_params(("parallel",)))(h, res, gain)


def _rmsnorm_bwd(d_res, d_normed, x, gain, name, with_mxu_copy):
    l, d = x.shape
    tr = _row_tile(l)
    row = pl.BlockSpec((tr, d), lambda i: (i, 0))
    vec = pl.BlockSpec((1, d), lambda i: (0, 0))

    def body(dres_ref, dn_ref, x_ref, g_ref, dx_ref, *rest):
        dg_ref = rest[-1]
        xv = x_ref[...]
        r = lax.rsqrt(jnp.mean(xv * xv, axis=-1, keepdims=True) + NORM_EPS)
        xh = xv * r
        dn = dn_ref[...]
        dxh = dn * g_ref[...]
        dx = dres_ref[...] + r * (dxh - xh * jnp.mean(dxh * xh, axis=-1, keepdims=True))
        dx_ref[...] = dx
        if with_mxu_copy:
            rest[0][...] = dx.astype(MXU_DTYPE)

        @pl.when(pl.program_id(0) == 0)
        def _():
            dg_ref[...] = jnp.zeros_like(dg_ref)

        dg_ref[...] += jnp.sum(dn * xh, axis=0, keepdims=True)

    outs = [jax.ShapeDtypeStruct((l, d), F32)]
    specs = [row]
    if with_mxu_copy:
        outs.append(jax.ShapeDtypeStruct((l, d), MXU_DTYPE))
        specs.append(row)
    outs.append(jax.ShapeDtypeStruct((1, d), F32))
    specs.append(vec)
    return pl.pallas_call(body, name=name, out_shape=tuple(outs), grid=(l // tr,),
                          in_specs=[row, row, row, vec], out_specs=tuple(specs),
                          compiler_params=_params(("arbitrary",)))(d_res, d_normed, x, gain)


def _loss_head(h1, mlp_out, gain, target):
    l, d = h1.shape
    n_blocks = l // CHUNK
    row = pl.BlockSpec((CHUNK, d), lambda i: (i, 0))
    vec = pl.BlockSpec((1, d), lambda i: (0, 0))
    tgt = pl.BlockSpec((CHUNK, d), lambda i: (jnp.maximum(i - 1, 0), 0))

    def body(h_ref, m_ref, g_ref, t_ref, dh_ref, dhb_ref, dg_ref, loss_ref, sq_ref):
        i = pl.program_id(0)
        x = h_ref[...] + m_ref[...]
        r = lax.rsqrt(jnp.mean(x * x, axis=-1, keepdims=True) + NORM_EPS)
        xh = x * r
        g = g_ref[...]
        real = i >= 1
        err = jnp.where(real, xh * g - t_ref[...], 0.0)
        dy = err * (1.0 / d)
        dxh = dy * g
        dh = r * (dxh - xh * jnp.mean(dxh * xh, axis=-1, keepdims=True))
        dh_ref[...] = dh
        dhb_ref[...] = dh.astype(MXU_DTYPE)

        @pl.when(i == 0)
        def _():
            dg_ref[...] = jnp.zeros_like(dg_ref)
            sq_ref[...] = jnp.zeros_like(sq_ref)

        dg_ref[...] += jnp.sum(dy * xh, axis=0, keepdims=True)
        sq_ref[...] += jnp.sum(err * err, axis=0, keepdims=True)

        @pl.when(i == n_blocks - 1)
        def _():
            total = jnp.sum(sq_ref[...], axis=-1, keepdims=True) * (0.5 / d)
            loss_ref[...] = jnp.broadcast_to(total, (1, 128))

    return pl.pallas_call(
        body, name="loss_head",
        out_shape=(jax.ShapeDtypeStruct((l, d), F32), jax.ShapeDtypeStruct((l, d), MXU_DTYPE),
                   jax.ShapeDtypeStruct((1, d), F32), jax.ShapeDtypeStruct((1, 128), F32)),
        grid=(n_blocks,), in_specs=[row, row, vec, tgt],
        out_specs=(row, row, vec, pl.BlockSpec((1, 128), lambda i: (0, 0))),
        scratch_shapes=[pltpu.VMEM((1, d), F32)],
        compiler_params=_params(("arbitrary",)),
    )(h1, mlp_out, gain, target)


def _dot(a, b):
    return jnp.dot(a, b, preferred_element_type=F32)


def _dot_nt(a, b):
    return lax.dot_general(a, b, (((1,), (1,)), ((), ())), preferred_element_type=F32)


def _dot_tn(a, b):
    return lax.dot_general(a, b, (((0,), (0,)), ((), ())), preferred_element_type=F32)


def _rope(t, cos2, sin2):
    return t * cos2 + pltpu.roll(t, HEAD_DIM // 2, 1) * sin2


def _rope_bwd(dr, cos2, sin2):
    return dr * cos2 + pltpu.roll(dr * sin2, HEAD_DIM // 2, 1)


def _sigmoid(x):
    return 1.0 / (1.0 + jnp.exp(-x))


def _row_valid(block, rows):
    r = block * CHUNK + lax.broadcasted_iota(jnp.int32, (rows, 1), 0)
    return r >= PAD_ROWS


def _retention_consts(l):
    pos = jnp.arange(l, dtype=F32) - PAD_ROWS
    inv_freq = 1.0 / (ROPE_BASE ** (jnp.arange(0, HEAD_DIM, 2, dtype=F32) / HEAD_DIM))
    ang = pos[:, None] * inv_freq[None, :]
    cos, sin = jnp.cos(ang), jnp.sin(ang)
    cos2 = jnp.concatenate([cos, cos], axis=-1)
    sin2 = jnp.concatenate([-sin, sin], axis=-1)
    log_g = jnp.log1p(-jnp.exp2(-5.0 - jnp.arange(N_HEADS, dtype=F32)))
    idx = jnp.arange(CHUNK, dtype=F32)
    diff = idx[:, None] - idx[None, :]
    decay = jnp.where(diff >= 0, jnp.exp(jnp.maximum(diff, 0.0)[None] * log_g[:, None, None]), 0.0)
    xi = jnp.exp((idx + 1.0)[None, :] * log_g[:, None])
    zeta = jnp.exp((CHUNK - 1.0 - idx)[None, :] * log_g[:, None])
    g_chunk = jnp.exp(CHUNK * log_g)
    bcast = lambda v: jnp.broadcast_to(v[:, :, None], (N_HEADS, CHUNK, HEAD_DIM))
    g_rows = jnp.broadcast_to(g_chunk[:, None, None], (N_HEADS, 8, HEAD_DIM))
    return cos2, sin2, decay, bcast(xi), bcast(zeta), g_rows


def _retention_fwd(proj, ret_gain, consts):
    l = proj.shape[0]
    n_chunks = l // CHUNK
    cos2, sin2, decay, xi, zeta, g_rows = consts
    scale = HEAD_DIM ** -0.5

    def body(p_ref, cos_ref, sin_ref, dec_ref, xi_ref, zeta_ref, gr_ref, gain_ref,
             mix_ref, o_ref, st_ref, state):
        c = pl.program_id(0)

        @pl.when(c == 0)
        def _():
            state[...] = jnp.zeros_like(state)

        cos_v, sin_v = cos_ref[...], sin_ref[...]
        valid = _row_valid(c, CHUNK)
        for h in range(N_HEADS):
            cols = slice(h * HEAD_DIM, (h + 1) * HEAD_DIM)
            q = p_ref[:, h * HEAD_DIM:(h + 1) * HEAD_DIM]
            k = p_ref[:, GROUP + h * HEAD_DIM:GROUP + (h + 1) * HEAD_DIM]
            v = p_ref[:, 2 * GROUP + h * HEAD_DIM:2 * GROUP + (h + 1) * HEAD_DIM]
            g = p_ref[:, 3 * GROUP + h * HEAD_DIM:3 * GROUP + (h + 1) * HEAD_DIM]
            rq = _rope(q, cos_v, sin_v).astype(MXU_DTYPE)
            rk = _rope(k, cos_v, sin_v) * scale
            rkb = rk.astype(MXU_DTYPE)
            vb = v.astype(MXU_DTYPE)
            st = state[h]
            st_ref[h] = st
            s = _dot_nt(rq, rkb) * dec_ref[h]
            o = _dot(s.astype(MXU_DTYPE), vb) + _dot(rq, st.astype(MXU_DTYPE)) * xi_ref[h]
            kz = (rk * zeta_ref[h]).astype(MXU_DTYPE)
            state[h] = gr_ref[h, 0:1, :] * st + _dot_tn(kz, vb)
            o_ref[:, cols] = o
            mu = jnp.mean(o, axis=-1, keepdims=True)
            oc = o - mu
            yn = oc * lax.rsqrt(jnp.mean(oc * oc, axis=-1, keepdims=True) + NORM_EPS)
            ret = (g * _sigmoid(g)) * (yn * gain_ref[:, cols])
            mix_ref[:, cols] = jnp.where(valid, ret, 0.0).astype(mix_ref.dtype)

    head_tab = pl.BlockSpec((N_HEADS, CHUNK, HEAD_DIM), lambda c: (0, 0, 0))
    return pl.pallas_call(
        body, name="retention_fwd",
        out_shape=(jax.ShapeDtypeStruct((l, GROUP), MXU_DTYPE), jax.ShapeDtypeStruct((l, GROUP), F32),
                   jax.ShapeDtypeStruct((n_chunks, N_HEADS, HEAD_DIM, HEAD_DIM), F32)),
        grid=(n_chunks,),
        in_specs=[pl.BlockSpec((CHUNK, 4 * GROUP), lambda c: (c, 0)),
                  pl.BlockSpec((CHUNK, HEAD_DIM), lambda c: (c, 0)),
                  pl.BlockSpec((CHUNK, HEAD_DIM), lambda c: (c, 0)),
                  head_tab, head_tab, head_tab,
                  pl.BlockSpec((N_HEADS, 8, HEAD_DIM), lambda c: (0, 0, 0)),
                  pl.BlockSpec((1, GROUP), lambda c: (0, 0))],
        out_specs=(pl.BlockSpec((CHUNK, GROUP), lambda c: (c, 0)),
                   pl.BlockSpec((CHUNK, GROUP), lambda c: (c, 0)),
                   pl.BlockSpec((None, N_HEADS, HEAD_DIM, HEAD_DIM), lambda c: (c, 0, 0, 0))),
        scratch_shapes=[pltpu.VMEM((N_HEADS, HEAD_DIM, HEAD_DIM), F32)],
        compiler_params=_params(("arbitrary",)),
    )(proj, cos2, sin2, decay, xi, zeta, g_rows, ret_gain)


def _retention_bwd(proj, o_pre, states, d_mix, ret_gain, consts):
    l = proj.shape[0]
    n_chunks = l // CHUNK
    cos2, sin2, decay, xi, zeta, g_rows = consts
    scale = HEAD_DIM ** -0.5
    rev = lambda c: n_chunks - 1 - c

    def body(p_ref, o_ref, st_ref, dm_ref, cos_ref, sin_ref, dec_ref, xi_ref, zeta_ref, gr_ref, gain_ref,
             dp_ref, dgain_ref, dstate):
        step = pl.program_id(0)

        @pl.when(step == 0)
        def _():
            dstate[...] = jnp.zeros_like(dstate)
            dgain_ref[...] = jnp.zeros_like(dgain_ref)

        cos_v, sin_v = cos_ref[...], sin_ref[...]
        valid = _row_valid(rev(step), CHUNK)
        for h in range(N_HEADS):
            cols = slice(h * HEAD_DIM, (h + 1) * HEAD_DIM)
            q = p_ref[:, h * HEAD_DIM:(h + 1) * HEAD_DIM]
            k = p_ref[:, GROUP + h * HEAD_DIM:GROUP + (h + 1) * HEAD_DIM]
            v = p_ref[:, 2 * GROUP + h * HEAD_DIM:2 * GROUP + (h + 1) * HEAD_DIM]
            g = p_ref[:, 3 * GROUP + h * HEAD_DIM:3 * GROUP + (h + 1) * HEAD_DIM]
            o = o_ref[:, cols]
            gain = gain_ref[:, cols]
            d_ret = jnp.where(valid, dm_ref[:, cols], 0.0)
            mu = jnp.mean(o, axis=-1, keepdims=True)
            oc = o - mu
            rstd = lax.rsqrt(jnp.mean(oc * oc, axis=-1, keepdims=True) + NORM_EPS)
            yn = oc * rstd
            sig = _sigmoid(g)
            gate = g * sig
            dgain_ref[:, cols] += jnp.sum(d_ret * gate * yn, axis=0, keepdims=True)
            d_g = d_ret * (yn * gain) * (sig * (1.0 + g * (1.0 - sig)))
            d_yn = d_ret * gate * gain
            d_o = rstd * (d_yn - jnp.mean(d_yn, axis=-1, keepdims=True)
                          - yn * jnp.mean(d_yn * yn, axis=-1, keepdims=True))
            rq = _rope(q, cos_v, sin_v)
            rk = _rope(k, cos_v, sin_v) * scale
            rqb, rkb, vb = rq.astype(MXU_DTYPE), rk.astype(MXU_DTYPE), v.astype(MXU_DTYPE)
            dob = d_o.astype(MXU_DTYPE)
            dec = dec_ref[h]
            xi_h, zeta_h = xi_ref[h], zeta_ref[h]
            st_b = st_ref[h].astype(MXU_DTYPE)
            dst = dstate[h]
            dst_b = dst.astype(MXU_DTYPE)
            s_b = (_dot_nt(rqb, rkb) * dec).astype(MXU_DTYPE)
            da_b = (_dot_nt(dob, vb) * dec).astype(MXU_DTYPE)
            doxi_b = (d_o * xi_h).astype(MXU_DTYPE)
            kz_b = (rk * zeta_h).astype(MXU_DTYPE)
            d_rq = _dot(da_b, rkb) + _dot_nt(doxi_b, st_b)
            d_rk = _dot_tn(da_b, rqb) + _dot_nt(vb, dst_b) * zeta_h
            d_v = _dot_tn(s_b, dob) + _dot(kz_b, dst_b)
            dstate[h] = gr_ref[h, 0:1, :] * dst + _dot_tn(rqb, doxi_b)
            d_q = _rope_bwd(d_rq, cos_v, sin_v)
            d_k = _rope_bwd(d_rk * scale, cos_v, sin_v)
            dp_ref[:, h * HEAD_DIM:(h + 1) * HEAD_DIM] = d_q.astype(dp_ref.dtype)
            dp_ref[:, GROUP + h * HEAD_DIM:GROUP + (h + 1) * HEAD_DIM] = d_k.astype(dp_ref.dtype)
            dp_ref[:, 2 * GROUP + h * HEAD_DIM:2 * GROUP + (h + 1) * HEAD_DIM] = d_v.astype(dp_ref.dtype)
            dp_ref[:, 3 * GROUP + h * HEAD_DIM:3 * GROUP + (h + 1) * HEAD_DIM] = d_g.astype(dp_ref.dtype)

    head_tab = pl.BlockSpec((N_HEADS, CHUNK, HEAD_DIM), lambda c: (0, 0, 0))
    return pl.pallas_call(
        body, name="retention_bwd",
        out_shape=(jax.ShapeDtypeStruct((l, 4 * GROUP), MXU_DTYPE), jax.ShapeDtypeStruct((1, GROUP), F32)),
        grid=(n_chunks,),
        in_specs=[pl.BlockSpec((CHUNK, 4 * GROUP), lambda c: (rev(c), 0)),
                  pl.BlockSpec((CHUNK, GROUP), lambda c: (rev(c), 0)),
                  pl.BlockSpec((None, N_HEADS, HEAD_DIM, HEAD_DIM), lambda c: (rev(c), 0, 0, 0)),
                  pl.BlockSpec((CHUNK, GROUP), lambda c: (rev(c), 0)),
                  pl.BlockSpec((CHUNK, HEAD_DIM), lambda c: (rev(c), 0)),
                  pl.BlockSpec((CHUNK, HEAD_DIM), lambda c: (rev(c), 0)),
                  head_tab, head_tab, head_tab,
                  pl.BlockSpec((N_HEADS, 8, HEAD_DIM), lambda c: (0, 0, 0)),
                  pl.BlockSpec((1, GROUP), lambda c: (0, 0))],
        out_specs=(pl.BlockSpec((CHUNK, 4 * GROUP), lambda c: (rev(c), 0)),
                   pl.BlockSpec((1, GROUP), lambda c: (0, 0))),
        scratch_shapes=[pltpu.VMEM((N_HEADS, HEAD_DIM, HEAD_DIM), F32)],
        compiler_params=_params(("arbitrary",)),
    )(proj, o_pre, states, d_mix, cos2, sin2, decay, xi, zeta, g_rows, ret_gain)


FF_TILE = (7 * GROUP) // 128


def _log_forget(ff, bias_row, valid):
    x = ff + bias_row
    e = jnp.exp(-jnp.abs(x))
    lf = jnp.minimum(x, 0.0) - jnp.log(1.0 + e)
    head_lane = lax.broadcasted_iota(jnp.int32, x.shape, 1) < N_HEADS
    keep = lambda t: jnp.where(head_lane, jnp.where(valid, t, 0.0), 0.0)
    return keep(lf), keep(jnp.where(x >= 0, e, 1.0) / (1.0 + e))


def _fox_prep(proj, bias_row):
    l = proj.shape[0]
    n_blocks = l // CHUNK

    def body(ff_ref, b_ref, bc_ref, rows_ref, cum):
        r = lax.broadcasted_iota(jnp.int32, (CHUNK, CHUNK), 0)
        cidx = lax.broadcasted_iota(jnp.int32, (CHUNK, CHUNK), 1)
        tri = jnp.where(r >= cidx, 1.0, 0.0).astype(F32)
        carry = jnp.zeros((1, 128), F32)
        for blk in range(n_blocks):
            rows = slice(blk * CHUNK, (blk + 1) * CHUNK)
            valid = _row_valid(blk, CHUNK)
            lf, _ = _log_forget(ff_ref[rows, :], b_ref[...], valid)
            local = jnp.dot(tri, lf, precision=lax.Precision.HIGHEST, preferred_element_type=F32) + carry
            carry = local[CHUNK - 1:CHUNK, :]
            masked = jnp.where(valid, local, -NEG_BIG)
            cum[rows, :] = masked
            t = masked.T
            for h in range(N_HEADS):
                rows_ref[h, :, rows] = t[h:h + 1, :]
        full = cum[...]
        for h in range(N_HEADS):
            bc_ref[h] = jnp.broadcast_to(full[:, h:h + 1], (l, 128))

    return pl.pallas_call(
        body, name="fox_prep",
        out_shape=(jax.ShapeDtypeStruct((N_HEADS, l, 128), F32), jax.ShapeDtypeStruct((N_HEADS, 1, l), F32)),
        grid=(1,),
        in_specs=[pl.BlockSpec((l, 128), lambda i: (0, FF_TILE)), pl.BlockSpec((1, 128), lambda i: (0, 0))],
        out_specs=(pl.BlockSpec((N_HEADS, l, 128), lambda i: (0, 0, 0)),
                   pl.BlockSpec((N_HEADS, 1, l), lambda i: (0, 0, 0))),
        scratch_shapes=[pltpu.VMEM((l, 128), F32)],
        compiler_params=_params(("arbitrary",)),
    )(proj, bias_row)


def _fox_fwd(proj, cum_bc, cum_rows):
    l = proj.shape[0]
    n_blocks = l // CHUNK
    scale = HEAD_DIM ** -0.5
    qt, kt, vt = 4 * N_HEADS, 5 * N_HEADS, 6 * N_HEADS

    def body(q_ref, k_ref, v_ref, cq_ref, ck_ref, o_ref):
        i = pl.program_id(1)
        qb = q_ref[...].astype(MXU_DTYPE)
        kb = k_ref[...].astype(MXU_DTYPE)
        vb = v_ref[...].astype(MXU_DTYPE)
        s = _dot_nt(qb, kb) * scale
        bias = jnp.tile(cq_ref[...], (1, n_blocks)) - ck_ref[...]
        q_pos = i * CHUNK + lax.broadcasted_iota(jnp.int32, (CHUNK, l), 0)
        k_pos = lax.broadcasted_iota(jnp.int32, (CHUNK, l), 1)
        s = jnp.where(k_pos <= q_pos, s + bias, NEG_BIG)
        m = jnp.max(s, axis=-1, keepdims=True)
        e = jnp.exp(s - m)
        p = e * (1.0 / jnp.sum(e, axis=-1, keepdims=True))
        o = _dot(p.astype(MXU_DTYPE), vb)
        o_ref[...] = jnp.where(_row_valid(i, CHUNK), o, 0.0).astype(o_ref.dtype)

    return pl.pallas_call(
        body, name="fox_fwd",
        out_shape=jax.ShapeDtypeStruct((l, GROUP), MXU_DTYPE),
        grid=(N_HEADS, n_blocks),
        in_specs=[pl.BlockSpec((CHUNK, HEAD_DIM), lambda h, i: (i, qt + h)),
                  pl.BlockSpec((l, HEAD_DIM), lambda h, i: (0, kt + h)),
                  pl.BlockSpec((l, HEAD_DIM), lambda h, i: (0, vt + h)),
                  pl.BlockSpec((None, CHUNK, 128), lambda h, i: (h, i, 0)),
                  pl.BlockSpec((None, 1, l), lambda h, i: (h, 0, 0))],
        out_specs=pl.BlockSpec((CHUNK, HEAD_DIM), lambda h, i: (i, h)),
        compiler_params=_params(("parallel", "parallel")),
    )(proj, proj, proj, cum_bc, cum_rows)


def _fox_bwd(proj, cum_bc, cum_rows, d_mix):
    l = proj.shape[0]
    n_blocks = l // CHUNK
    scale = HEAD_DIM ** -0.5
    qt, kt, vt = 4 * N_HEADS, 5 * N_HEADS, 6 * N_HEADS

    def body(q_ref, k_ref, v_ref, do_ref, ck_ref, cq_ref, dq_ref, dk_ref, dv_ref, ds_ref, dk_acc, dv_acc):
        i = pl.program_id(1)

        @pl.when(i == 0)
        def _():
            dk_acc[...] = jnp.zeros_like(dk_acc)
            dv_acc[...] = jnp.zeros_like(dv_acc)
            ds_ref[...] = jnp.zeros_like(ds_ref)

        qb = q_ref[...].astype(MXU_DTYPE)
        kb = k_ref[...].astype(MXU_DTYPE)
        vb = v_ref[...].astype(MXU_DTYPE)
        dob = jnp.where(_row_valid(i, CHUNK), do_ref[...], 0.0).astype(MXU_DTYPE)
        k_pos = lax.broadcasted_iota(jnp.int32, (l, CHUNK), 0)
        q_pos = i * CHUNK + lax.broadcasted_iota(jnp.int32, (l, CHUNK), 1)
        s_t = _dot_nt(kb, qb) * scale + (cq_ref[...] - ck_ref[...])
        s_t = jnp.where(k_pos <= q_pos, s_t, NEG_BIG)
        m = jnp.max(s_t, axis=0, keepdims=True)
        e = jnp.exp(s_t - m)
        p_t = e * (1.0 / jnp.sum(e, axis=0, keepdims=True))
        dp_t = _dot_nt(vb, dob)
        delta = jnp.sum(p_t * dp_t, axis=0, keepdims=True)
        ds_t = p_t * (dp_t - delta)
        ds_b = ds_t.astype(MXU_DTYPE)
        dv_acc[...] += _dot(p_t.astype(MXU_DTYPE), dob)
        dk_acc[...] += _dot(ds_b, qb) * scale
        ds_ref[...] += ds_t
        dq_ref[...] = (_dot_tn(ds_b, kb) * scale).astype(dq_ref.dtype)

        @pl.when(i == n_blocks - 1)
        def _():
            dk_ref[...] = dk_acc[...].astype(dk_ref.dtype)
            dv_ref[...] = dv_acc[...].astype(dv_ref.dtype)

    col = jax.ShapeDtypeStruct((l, GROUP), MXU_DTYPE)
    return pl.pallas_call(
        body, name="fox_bwd",
        out_shape=(col, col, col, jax.ShapeDtypeStruct((N_HEADS, l, 128), F32)),
        grid=(N_HEADS, n_blocks),
        in_specs=[pl.BlockSpec((CHUNK, HEAD_DIM), lambda h, i: (i, qt + h)),
                  pl.BlockSpec((l, HEAD_DIM), lambda h, i: (0, kt + h)),
                  pl.BlockSpec((l, HEAD_DIM), lambda h, i: (0, vt + h)),
                  pl.BlockSpec((CHUNK, HEAD_DIM), lambda h, i: (i, N_HEADS + h)),
                  pl.BlockSpec((None, l, 128), lambda h, i: (h, 0, 0)),
                  pl.BlockSpec((None, 1, CHUNK), lambda h, i: (h, 0, i))],
        out_specs=(pl.BlockSpec((CHUNK, HEAD_DIM), lambda h, i: (i, h)),
                   pl.BlockSpec((l, HEAD_DIM), lambda h, i: (0, h)),
                   pl.BlockSpec((l, HEAD_DIM), lambda h, i: (0, h)),
                   pl.BlockSpec((None, l, 128), lambda h, i: (h, 0, 0))),
        scratch_shapes=[pltpu.VMEM((l, HEAD_DIM), F32), pltpu.VMEM((l, HEAD_DIM), F32)],
        compiler_params=_params(("parallel", "arbitrary")),
    )(proj, proj, proj, d_mix, cum_bc, cum_rows)


def _fox_gate_bwd(ds_sum, proj, bias_row):
    l = proj.shape[0]
    n_blocks = l // CHUNK

    def body(ds_ref, ff_ref, b_ref, dff_ref, db_ref):
        r = lax.broadcasted_iota(jnp.int32, (CHUNK, CHUNK), 0)
        cidx = lax.broadcasted_iota(jnp.int32, (CHUNK, CHUNK), 1)
        upper = jnp.where(cidx >= r, 1.0, 0.0).astype(F32)
        carry = jnp.zeros((1, 128), F32)
        db = jnp.zeros((1, 128), F32)
        for blk in reversed(range(n_blocks)):
            rows = slice(blk * CHUNK, (blk + 1) * CHUNK)
            key_sum = jnp.zeros((CHUNK, 128), F32)
            for h in range(N_HEADS):
                select = jnp.where(cidx == h, 1.0, 0.0).astype(F32)
                key_sum = key_sum + jnp.dot(ds_ref[h, rows, :], select, precision=lax.Precision.HIGHEST,
                                            preferred_element_type=F32)
            suffix = jnp.dot(upper, key_sum, precision=lax.Precision.HIGHEST, preferred_element_type=F32) + carry
            carry = suffix[0:1, :]
            _, dsig = _log_forget(ff_ref[rows, :], b_ref[...], _row_valid(blk, CHUNK))
            dff = -suffix * dsig
            dff_ref[rows, :] = dff.astype(dff_ref.dtype)
            db = db + jnp.sum(dff, axis=0, keepdims=True)
        db_ref[...] = db

    return pl.pallas_call(
        body, name="fox_gate_bwd",
        out_shape=(jax.ShapeDtypeStruct((l, 128), MXU_DTYPE), jax.ShapeDtypeStruct((1, 128), F32)),
        grid=(1,),
        in_specs=[pl.BlockSpec((N_HEADS, l, 128), lambda i: (0, 0, 0)),
                  pl.BlockSpec((l, 128), lambda i: (0, FF_TILE)),
                  pl.BlockSpec((1, 128), lambda i: (0, 0))],
        out_specs=(pl.BlockSpec((l, 128), lambda i: (0, 0)), pl.BlockSpec((1, 128), lambda i: (0, 0))),
        compiler_params=_params(("arbitrary",)),
    )(ds_sum, proj, bias_row)


def _conv(u, w, b):
    return b + w[0:1, :] * pltpu.roll(u, 2, 0) + w[1:2, :] * pltpu.roll(u, 1, 0) + w[2:3, :] * u


def _conv_act_fwd(u, conv_w, conv_b, d_ff):
    l = u.shape[0]
    tc = _divisor_tile(d_ff, 256, 128)
    nt = d_ff // tc

    def body(ug_ref, uv_ref, wg_ref, wv_ref, bg_ref, bv_ref, a_ref):
        yg = _conv(ug_ref[...], wg_ref[...], bg_ref[...])
        yv = _conv(uv_ref[...], wv_ref[...], bv_ref[...])
        act = yg * _sigmoid(yg) * yv
        a_ref[...] = jnp.where(_row_valid(0, l), act, 0.0).astype(a_ref.dtype)

    return pl.pallas_call(
        body, name="conv_act_fwd",
        out_shape=jax.ShapeDtypeStruct((l, d_ff), MXU_DTYPE),
        grid=(nt,),
        in_specs=[pl.BlockSpec((l, tc), lambda j: (0, j)), pl.BlockSpec((l, tc), lambda j: (0, j + nt)),
                  pl.BlockSpec((8, tc), lambda j: (0, j)), pl.BlockSpec((8, tc), lambda j: (0, j + nt)),
                  pl.BlockSpec((1, tc), lambda j: (0, j)), pl.BlockSpec((1, tc), lambda j: (0, j + nt))],
        out_specs=pl.BlockSpec((l, tc), lambda j: (0, j)),
        compiler_params=_params(("parallel",)),
    )(u, u, conv_w, conv_w, conv_b, conv_b)


def _conv_act_bwd(u, conv_w, conv_b, d_act, d_ff):
    l = u.shape[0]
    tc = _divisor_tile(d_ff, 256, 128)
    nt = d_ff // tc

    def body(ug_ref, uv_ref, wg_ref, wv_ref, bg_ref, bv_ref, da_ref, du_ref, dwb_ref):
        valid = _row_valid(0, l)
        ug, uv = ug_ref[...], uv_ref[...]
        wg, wv = wg_ref[...], wv_ref[...]
        yg = _conv(ug, wg, bg_ref[...])
        yv = _conv(uv, wv, bv_ref[...])
        sig = _sigmoid(yg)
        da = jnp.where(valid, da_ref[...], 0.0)
        d_yv = da * (yg * sig)
        d_yg = da * yv * (sig * (1.0 + yg * (1.0 - sig)))
        for idx, (dy, uu, w) in enumerate(((d_yg, ug, wg), (d_yv, uv, wv))):
            du = w[2:3, :] * dy + w[1:2, :] * pltpu.roll(dy, l - 1, 0) + w[0:1, :] * pltpu.roll(dy, l - 2, 0)
            du_ref[idx] = jnp.where(valid, du, 0.0).astype(du_ref.dtype)
            dwb_ref[idx, 0:1, :] = jnp.sum(dy * pltpu.roll(uu, 2, 0), axis=0, keepdims=True)
            dwb_ref[idx, 1:2, :] = jnp.sum(dy * pltpu.roll(uu, 1, 0), axis=0, keepdims=True)
            dwb_ref[idx, 2:3, :] = jnp.sum(dy * uu, axis=0, keepdims=True)
            dwb_ref[idx, 3:4, :] = jnp.sum(dy, axis=0, keepdims=True)
            dwb_ref[idx, 4:8, :] = jnp.zeros((4, tc), F32)

    return pl.pallas_call(
        body, name="conv_act_bwd",
        out_shape=(jax.ShapeDtypeStruct((2, l, d_ff), MXU_DTYPE), jax.ShapeDtypeStruct((2, 8, d_ff), F32)),
        grid=(nt,),
        in_specs=[pl.BlockSpec((l, tc), lambda j: (0, j)), pl.BlockSpec((l, tc), lambda j: (0, j + nt)),
                  pl.BlockSpec((8, tc), lambda j: (0, j)), pl.BlockSpec((8, tc), lambda j: (0, j + nt)),
                  pl.BlockSpec((1, tc), lambda j: (0, j)), pl.BlockSpec((1, tc), lambda j: (0, j + nt)),
                  pl.BlockSpec((l, tc), lambda j: (0, j))],
        out_specs=(pl.BlockSpec((2, l, tc), lambda j: (0, 0, j)), pl.BlockSpec((2, 8, tc), lambda j: (0, 0, j))),
        compiler_params=_params(("parallel",)),
    )(u, u, conv_w, conv_w, conv_b, conv_b, d_act)


def _adamw(w, g, m, v, name):
    shape = w.shape
    if w.ndim == 1:
        as2d = (1, shape[0])
    else:
        as2d = (int(np.prod(shape[:-1])), shape[-1])
    r, c = as2d
    tr = _divisor_tile(r, 256, 8)
    spec = pl.BlockSpec((tr, c), lambda i: (i, 0))

    def body(w_ref, g_ref, m_ref, v_ref, d_ref, nm_ref, nv_ref):
        gv = g_ref[...]
        nm = ADAM_B1 * m_ref[...] + (1.0 - ADAM_B1) * gv
        nv = ADAM_B2 * v_ref[...] + (1.0 - ADAM_B2) * (gv * gv)
        m_hat = nm / (1.0 - ADAM_B1 ** ADAM_STEP)
        v_hat = nv / (1.0 - ADAM_B2 ** ADAM_STEP)
        d_ref[...] = -ADAM_LR * (m_hat / (jnp.sqrt(v_hat) + ADAM_EPS) + ADAM_WD * w_ref[...])
        nm_ref[...] = nm
        nv_ref[...] = nv

    sds = jax.ShapeDtypeStruct(as2d, F32)
    outs = pl.pallas_call(
        body, name=name, out_shape=(sds, sds, sds), grid=(r // tr,),
        in_specs=[spec] * 4, out_specs=(spec,) * 3,
        compiler_params=_params(("parallel",)),
    )(w.reshape(as2d), g.reshape(as2d), m.reshape(as2d), v.reshape(as2d))
    return tuple(o.reshape(shape) for o in outs)


def _pad_rows(a, rows):
    return jnp.pad(a, ((0, rows - a.shape[0]), (0, 0)))


def kernel(x, meta_tokens, norm1_gain, w_in, b_forget, ret_norm_gain, w_out, norm2_gain, w_up, conv_w, conv_b, w_down, final_norm_gain, loss_target, m_meta_tokens, m_norm1_gain, m_w_in, m_b_forget, m_ret_norm_gain, m_w_out, m_norm2_gain, m_w_up, m_conv_w, m_conv_b, m_w_down, m_final_norm_gain, v_meta_tokens, v_norm1_gain, v_w_in, v_b_forget, v_ret_norm_gain, v_w_out, v_norm2_gain, v_w_up, v_conv_w, v_conv_b, v_w_down, v_final_norm_gain):
    seq, d = x.shape[1], x.shape[2]
    l = CHUNK + seq
    d_ff = w_down.shape[1] * N_DEV
    up_shard = w_up.shape[2]
    assert 4 * up_shard == d_ff and w_in.shape[2] == WIN_SHARD and d == 2 * GROUP
    dev = _device_index()
    mx, my, mc = _my_position()
    core = jnp.reshape(mc, (1,)).astype(jnp.int32)
    chip = jnp.reshape(2 * mx + my, (1,)).astype(jnp.int32)
    dev1 = jnp.reshape(dev, (1,)).astype(jnp.int32)

    small = jnp.concatenate([meta_tokens.reshape(-1, 128), conv_w[0].reshape(-1, 128)], axis=0)
    n_meta_rows = N_META * (d // N_DEV) // 128
    small_rows = small.shape[0]
    small_all = _all_gather(_pad_rows(small, -(-small_rows // 8) * 8), "gather_small")
    meta_full = jnp.transpose(small_all[:, :n_meta_rows].reshape(N_DEV, N_META, d // N_DEV), (1, 0, 2)).reshape(N_META, d)
    conv_w_full = _pad_rows(jnp.transpose(small_all[:, n_meta_rows:small_rows].reshape(N_DEV, 3, up_shard),
                                          (1, 0, 2)).reshape(3, 2 * d_ff), 8)
    w_in_padded = jnp.pad(w_in[0], ((0, 0), (0, WIN_BLOCK - WIN_SHARD))).astype(WIRE_DTYPE)
    start_in = _gather_start(w_in_padded, dev1, small_all, "gather_w_in_start")

    h0 = jnp.concatenate([jnp.zeros((PAD_ROWS, d), F32), meta_full, x[0]], axis=0)
    consts = _retention_consts(l)
    bias_row = jnp.pad(b_forget, ((0, 0), (0, 128 - N_HEADS)))
    a = _rmsnorm_fwd(h0, norm1_gain + start_in[4][0, 0], "rmsnorm1")
    w_in_blocks = _gather_finish(start_in, a, "gather_w_in")
    start_out = _gather_start(w_out[0].astype(WIRE_DTYPE), dev1, w_in_blocks, "gather_w_out_start")
    w_in_full = _assemble_w_in(w_in_blocks).astype(MXU_DTYPE)
    proj = _mm_nn(a, w_in_full, F32, "mm_proj", after=start_out[4])
    w_out_blocks = _gather_finish(start_out, proj, "gather_w_out")
    start_up = _gather_start(w_up[0].astype(WIRE_DTYPE), dev1, w_out_blocks, "gather_w_up_start")
    ret_mix, ret_pre, ret_states = _retention_fwd(proj, ret_norm_gain + start_up[4][0, 0], consts)
    cum_bc, cum_rows = _fox_prep(proj, bias_row + start_up[4][0:1, :])
    fox_mix = _fox_fwd(proj, cum_bc, cum_rows)
    mix = jnp.concatenate([ret_mix, fox_mix], axis=1)
    w_out_full = w_out_blocks.reshape(d, d).astype(MXU_DTYPE)
    h1, cn = _rmsnorm_fwd(h0, norm2_gain, "resid_rmsnorm2", res=_mm_nn(mix, w_out_full, F32, "mm_out"))
    w_up_blocks = _gather_finish(start_up, cn, "gather_w_up").astype(MXU_DTYPE)
    start_down = _gather_start(w_down[0].astype(WIRE_DTYPE), dev1, w_up_blocks, "gather_w_down_start")
    u = _mm(cn, w_up_blocks,
            a_spec=pl.BlockSpec((_divisor_tile(l, 1088, 16), d), lambda i, j, k: (i, 0)),
            b_spec=pl.BlockSpec((None, d, up_shard), lambda i, j, k: (j, 0, 0)),
            o_spec=pl.BlockSpec((_divisor_tile(l, 1088, 16), up_shard), lambda i, j, k: (i, j)),
            out_shape=jax.ShapeDtypeStruct((l, 2 * d_ff), F32),
            grid=(l // _divisor_tile(l, 1088, 16), N_DEV, 1), contract=(1, 0), nk=1, name="mm_up",
            after=start_down[4])
    act = _conv_act_fwd(u, conv_w_full, conv_b + start_down[4][0, 0], d_ff)
    w_down_full = _gather_finish(start_down, act, "gather_w_down").reshape(d_ff, d).astype(MXU_DTYPE)
    mlp_out = _mm_nn(act, w_down_full, F32, "mm_down", tk_cap=1408)
    d_h2, d_h2_b, dg_final, loss_part = _loss_head(h1, mlp_out, final_norm_gain.reshape(1, d), loss_target[0])

    gw_down = _mm_tn(act, d_h2_b, WIRE_DTYPE, "mm_gw_down", tm_cap=1408, tn_cap=1024)
    rs_down = _reduce_scatter_start(gw_down.reshape(N_DEV, d_ff // N_DEV, d), core, "rs_w_down")
    d_act = _mm_nt(d_h2_b, w_down_full, F32, "mm_d_act", after=rs_down[4])
    d_u, d_conv = _conv_act_bwd(u, conv_w_full, conv_b + rs_down[4][0, 0], d_act, d_ff)
    tm = _divisor_tile(l, 1088, 16)
    gw_up = _mm(cn, d_u,
                a_spec=pl.BlockSpec((l, d // 2), lambda i, j, k: (0, i)),
                b_spec=pl.BlockSpec((None, l, up_shard), lambda i, j, k: (j // 4, 0, j % 4)),
                o_spec=pl.BlockSpec((None, d // 2, up_shard), lambda i, j, k: (j, i, 0)),
                out_shape=jax.ShapeDtypeStruct((N_DEV, d, up_shard), WIRE_DTYPE),
                grid=(2, N_DEV, 1), contract=(0, 0), nk=1, name="mm_gw_up")
    rs_up = _reduce_scatter_start(gw_up, core, "rs_w_up")
    d_cn = _mm(d_u, w_up_blocks,
               a_spec=pl.BlockSpec((None, tm, up_shard), lambda i, j, k: (k // 4, i, k % 4)),
               b_spec=pl.BlockSpec((None, d // 2, up_shard), lambda i, j, k: (k, j, 0)),
               o_spec=pl.BlockSpec((tm, d // 2), lambda i, j, k: (i, j)),
               out_shape=jax.ShapeDtypeStruct((l, d), F32),
               grid=(l // tm, 2, N_DEV), contract=(1, 1), nk=N_DEV, name="mm_d_cn", after=rs_up[4])
    d_h1, d_h1_b, dg_norm2 = _rmsnorm_bwd(d_h2, d_cn, h1, norm2_gain + rs_up[4][0, 0], "rmsnorm2_bwd", True)

    gw_out = _mm_tn(mix, d_h1_b, WIRE_DTYPE, "mm_gw_out")
    rs_out = _reduce_scatter_start(gw_out.reshape(N_DEV, d // N_DEV, d), core, "rs_w_out")
    d_mix = _mm_nt(d_h1_b, w_out_full, F32, "mm_d_mix", after=rs_out[4])
    d_fq, d_fk, d_fv, ds_sum = _fox_bwd(proj, cum_bc, cum_rows, d_mix)
    d_ff_tile, db_forget_row = _fox_gate_bwd(ds_sum, proj, bias_row)
    d_ret, dg_ret = _retention_bwd(proj, ret_pre, ret_states, d_mix, ret_norm_gain + rs_out[4][0, 0], consts)
    d_proj = jnp.concatenate(
        [d_ret, d_fq, d_fk, d_fv, d_ff_tile, jnp.zeros((l, WIN_N - 7 * GROUP - 128), MXU_DTYPE)], axis=1)
    gw_in = _mm_tn(a, d_proj, WIRE_DTYPE, "mm_gw_in")
    rs_in = _reduce_scatter_start(_extract_w_in_windows(gw_in), core, "rs_w_in")
    d_a = _mm_nt(d_proj, w_in_full, F32, "mm_d_a", tk_cap=1536, after=rs_in[4])
    d_h0, dg_norm1 = _rmsnorm_bwd(d_h1, d_a, h0, norm1_gain + rs_in[4][0, 0], "rmsnorm1_bwd", False)
    grad_x = d_h0[CHUNK:][None]
    d_meta = d_h0[PAD_ROWS:CHUNK]

    d_conv_w = jnp.concatenate([d_conv[0, 0:3], d_conv[1, 0:3]], axis=1)
    d_conv_b = jnp.concatenate([d_conv[0, 3:4], d_conv[1, 3:4]], axis=1)
    pieces = [loss_part[:, 0:1], dg_norm1, db_forget_row[:, 0:N_HEADS], dg_ret, dg_norm2, d_conv_b, dg_final,
              d_meta.reshape(1, -1), d_conv_w.reshape(1, -1)]
    sizes = [p.shape[1] for p in pieces]
    flat = jnp.concatenate(pieces, axis=1)
    padded = -(-flat.shape[1] // 1024) * 1024
    flat = jnp.pad(flat, ((0, 0), (0, padded - flat.shape[1]))).reshape(padded // 128, 128)
    small_ar = _small_all_reduce_start(flat, d_h0, "all_reduce_small")

    g_w_down = _reduce_scatter_finish(rs_down, small_ar[4], chip, "rs_w_down")[None]
    g_w_up = _reduce_scatter_finish(rs_up, g_w_down, chip, "rs_w_up")[None]
    g_w_out = _reduce_scatter_finish(rs_out, g_w_up, chip, "rs_w_out")[None]
    early = [_adamw(w, g, m, v, "adamw_" + n) for w, g, m, v, n in (
        (w_down, g_w_down, m_w_down, v_w_down, "w_down"), (w_up, g_w_up, m_w_up, v_w_up, "w_up"),
        (w_out, g_w_out, m_w_out, v_w_out, "w_out"))]
    g_w_in_window = _reduce_scatter_finish(rs_in, early[1][2], chip, "rs_w_in")
    g_w_in = g_w_in_window[:, :WIN_SHARD][None]
    early.append(_adamw(w_in, g_w_in, m_w_in, v_w_in, "adamw_w_in"))
    total = _small_all_reduce_finish(small_ar, early[3][2], dev1, "all_reduce_small").reshape(1, padded)
    offs = np.concatenate([[0], np.cumsum(sizes)])
    take = lambda k: total[:, int(offs[k]):int(offs[k + 1])]
    loss = take(0).reshape(())
    g_norm1, g_bf, g_ret_gain, g_norm2 = take(1), take(2), take(3), take(4)
    g_conv_b, g_final = take(5), take(6).reshape(d)
    g_meta = lax.dynamic_slice(take(7).reshape(N_META, d), (jnp.int32(0), (dev * (d // N_DEV)).astype(jnp.int32)),
                               (N_META, d // N_DEV))
    g_conv_w = lax.dynamic_slice(take(8).reshape(3, 2 * d_ff), (jnp.int32(0), (dev * up_shard).astype(jnp.int32)),
                                 (3, up_shard))[None]

    weights = [meta_tokens, norm1_gain, w_in, b_forget, ret_norm_gain, w_out, norm2_gain, w_up, conv_w, conv_b,
               w_down, final_norm_gain]
    grads = [g_meta, g_norm1, g_w_in, g_bf, g_ret_gain, g_w_out, g_norm2, g_w_up, g_conv_w, g_conv_b, g_w_down,
             g_final]
    done = {"w_down": early[0], "w_up": early[1], "w_out": early[2], "w_in": early[3]}
    ms = [m_meta_tokens, m_norm1_gain, m_w_in, m_b_forget, m_ret_norm_gain, m_w_out, m_norm2_gain, m_w_up, m_conv_w,
          m_conv_b, m_w_down, m_final_norm_gain]
    vs = [v_meta_tokens, v_norm1_gain, v_w_in, v_b_forget, v_ret_norm_gain, v_w_out, v_norm2_gain, v_w_up, v_conv_w,
          v_conv_b, v_w_down, v_final_norm_gain]
    names = ["meta", "norm1", "w_in", "b_forget", "ret_gain", "w_out", "norm2", "w_up", "conv_w", "conv_b", "w_down",
             "final_gain"]
    deltas, new_ms, new_vs = [], [], []
    for w, g, m, v, n in zip(weights, grads, ms, vs, names):
        dl, nm, nv = done[n] if n in done else _adamw(w, g, m, v, "adamw_" + n)
        deltas.append(dl)
        new_ms.append(nm)
        new_vs.append(nv)
    return (loss, grad_x, *grads, *deltas, *new_ms, *new_vs)
```

```python
import functools

import numpy as np
import jax
import jax.numpy as jnp
from jax import lax
from jax.experimental import pallas as pl
from jax.experimental.pallas import tpu as pltpu

F32 = jnp.float32
MXU_DTYPE = jnp.bfloat16
WIRE_DTYPE = jnp.bfloat16

N_DEV = 8
N_META = 16
CHUNK = 128
PAD_ROWS = CHUNK - N_META
N_HEADS = 8
HEAD_DIM = 128
GROUP = N_HEADS * HEAD_DIM
IN_DIM = 7 * GROUP + N_HEADS
WIN_SHARD = IN_DIM // N_DEV
WIN_BLOCK = 1024
WIN_STRIDE = 896
WIN_N = 7680
ROPE_BASE = 10000.0
NORM_EPS = 1e-6
NEG_BIG = -1e30
ADAM_LR, ADAM_B1, ADAM_B2, ADAM_EPS, ADAM_WD, ADAM_STEP = 0.001, 0.9, 0.999, 1e-08, 0.01, 10
VMEM_LIMIT = 52 * 1024 * 1024
MESH = pl.DeviceIdType.MESH
ANY = pl.BlockSpec(memory_space=pl.ANY)
VMEM_SPEC = pl.BlockSpec(memory_space=pltpu.VMEM)


def _params(sem=None):
    kw = {"vmem_limit_bytes": VMEM_LIMIT}
    if sem is not None:
        kw["dimension_semantics"] = sem
    return pltpu.CompilerParams(**kw)


def _divisor_tile(n, cap, unit):
    if n <= cap:
        return n
    best = None
    for t in range(unit, cap + 1, unit):
        if n % t == 0:
            best = t
    assert best is not None, (n, cap, unit)
    return best


def _my_position():
    return lax.axis_index("x"), lax.axis_index("y"), lax.axis_index("c")


def _device_index():
    x, y, c = _my_position()
    return 4 * x + 2 * y + c


def _all_gather(shard, name):
    r, c = shard.shape

    def body(x_ref, out_ref, send_sems, recv_sems, local_sem):
        mx, my, mc = _my_position()
        me, sibling = (mx, my, mc), (mx, my, 1 - mc)
        chips = [(1 - mx, my), (mx, 1 - my), (1 - mx, 1 - my)]

        def slot(px, py, pc):
            return out_ref.at[4 * px + 2 * py + pc]

        def copy(k, block, to, src=None):
            return pltpu.make_async_remote_copy(
                src_ref=slot(*block) if src is None else src, dst_ref=slot(*block),
                send_sem=send_sems.at[k], recv_sem=recv_sems.at[k], device_id=to, device_id_type=MESH)

        mine = pltpu.make_async_copy(x_ref, slot(*me), local_sem)
        mine.start()
        first = [copy(0, me, sibling, src=x_ref)]
        first += [copy(1 + j, me, (*chip, mc), src=x_ref) for j, chip in enumerate(chips)]
        for cp in first:
            cp.start()
        passed = [copy(4 + j, (*chip, mc), sibling) for j, chip in enumerate(chips)]
        for j, chip in enumerate(chips):
            copy(1 + j, (*chip, mc), me).wait_recv()
            passed[j].start()
        copy(0, sibling, me).wait_recv()
        for j, chip in enumerate(chips):
            copy(4 + j, (*chip, 1 - mc), me).wait_recv()
        for cp in first + passed:
            cp.wait_send()
        mine.wait()

    return pl.pallas_call(
        body, name=name,
        out_shape=jax.ShapeDtypeStruct((N_DEV, r, c), shard.dtype),
        in_specs=[ANY], out_specs=ANY,
        scratch_shapes=[pltpu.SemaphoreType.DMA((7,)), pltpu.SemaphoreType.DMA((7,)), pltpu.SemaphoreType.DMA],
    )(shard)


HBM_SPEC = pl.BlockSpec(memory_space=pltpu.HBM)
SEM_SPEC = pl.BlockSpec(memory_space=pltpu.SEMAPHORE)
DATAFLOW_EFFECT = pltpu.SideEffectType.DATAFLOW_SIDE_EFFECTING


def _in_hbm(a):
    return pltpu.with_memory_space_constraint(a, pltpu.HBM)


def _split_start(src, land, make_copies, n_copies, after, name):
    if isinstance(land, tuple):
        land = lax.empty(land, src.dtype)
    land_shape = land.shape
    def body(src_ref, land_ref, after_ref, send_sems, recv_sems, src_thru, land_thru, token):
        for cp in make_copies(src_ref, land_ref, send_sems, recv_sems):
            cp.start()
        token[...] = jnp.zeros_like(token)

    return pl.pallas_call(
        body, name=name,
        out_shape=(pltpu.SemaphoreType.DMA((n_copies,)), pltpu.SemaphoreType.DMA((n_copies,)),
                   pltpu.HBM(src.shape, src.dtype), pltpu.HBM(land_shape, src.dtype),
                   jax.ShapeDtypeStruct((8, 128), F32)),
        in_specs=(HBM_SPEC, HBM_SPEC, ANY), out_specs=(SEM_SPEC, SEM_SPEC, HBM_SPEC, HBM_SPEC, VMEM_SPEC),
        input_output_aliases={0: 2, 1: 3},
        compiler_params=pltpu.CompilerParams(has_side_effects=DATAFLOW_EFFECT),
    )(_in_hbm(src), _in_hbm(land), after)


def _split_wait(started, after, make_copies, name):
    send_sems, recv_sems, src_thru, land_thru, _ = started

    def body(src_ref, land_ref, send_sems_ref, recv_sems_ref, after_ref, src_dead, land_out):
        for cp in make_copies(src_ref, land_ref, send_sems_ref, recv_sems_ref):
            cp.wait_send()
            cp.wait_recv()

    return pl.pallas_call(
        body, name=name,
        out_shape=(pltpu.HBM(src_thru.shape, src_thru.dtype), pltpu.HBM(land_thru.shape, land_thru.dtype)),
        in_specs=(HBM_SPEC, HBM_SPEC, SEM_SPEC, SEM_SPEC, ANY), out_specs=(HBM_SPEC, HBM_SPEC),
        input_output_aliases={0: 0, 1: 1},
        compiler_params=pltpu.CompilerParams(has_side_effects=DATAFLOW_EFFECT),
    )(src_thru, land_thru, send_sems, recv_sems, after)


def _gather_copies(x_ref, land_ref, send_sems, recv_sems):
    mx, my, mc = _my_position()
    me = 4 * mx + 2 * my + mc
    targets = [(mx, my, 1 - mc), (1 - mx, my, mc), (mx, 1 - my, mc), (1 - mx, 1 - my, mc)]
    return [pltpu.make_async_remote_copy(
        src_ref=x_ref, dst_ref=land_ref.at[me], send_sem=send_sems.at[k], recv_sem=recv_sems.at[k],
        device_id=t, device_id_type=MESH) for k, t in enumerate(targets)]


def _gather_start(shard, dev, after, name):
    r, c = shard.shape
    tr = _divisor_tile(r, 256, 16)

    def body(s_ref, x_ref, o_ref):
        o_ref[...] = x_ref[...]

    land = pl.pallas_call(
        body, name=name + "_own",
        out_shape=jax.ShapeDtypeStruct((N_DEV, r, c), shard.dtype),
        grid_spec=pltpu.PrefetchScalarGridSpec(
            num_scalar_prefetch=1, grid=(r // tr,),
            in_specs=[pl.BlockSpec((tr, c), lambda i, s: (i, 0))],
            out_specs=pl.BlockSpec((None, tr, c), lambda i, s: (s[0], i, 0))),
        compiler_params=_params(("parallel",)),
    )(dev, shard)
    return _split_start(shard, land, _gather_copies, 4, after, name)


def _gather_finish(started, after, name):
    _, land = _split_wait(started, after, _gather_copies, name + "_wait")

    def body(land_in, land_ref, send_sems, recv_sems):
        mx, my, mc = _my_position()
        chips = [(1 - mx, my), (mx, 1 - my), (1 - mx, 1 - my)]
        copies = [pltpu.make_async_remote_copy(
            src_ref=land_ref.at[4 * cx + 2 * cy + mc], dst_ref=land_ref.at[4 * cx + 2 * cy + mc],
            send_sem=send_sems.at[j], recv_sem=recv_sems.at[j],
            device_id=(mx, my, 1 - mc), device_id_type=MESH) for j, (cx, cy) in enumerate(chips)]
        for cp in copies:
            cp.start()
        for j, (cx, cy) in enumerate(chips):
            copies[j].wait_send()
            pltpu.make_async_remote_copy(
                src_ref=land_ref.at[4 * cx + 2 * cy + 1 - mc], dst_ref=land_ref.at[4 * cx + 2 * cy + 1 - mc],
                send_sem=send_sems.at[j], recv_sem=recv_sems.at[j],
                device_id=(mx, my, 1 - mc), device_id_type=MESH).wait_recv()

    return pl.pallas_call(
        body, name=name + "_pass",
        out_shape=jax.ShapeDtypeStruct(land.shape, land.dtype),
        in_specs=[ANY], out_specs=ANY,
        input_output_aliases={0: 0},
        scratch_shapes=[pltpu.SemaphoreType.DMA((3,)), pltpu.SemaphoreType.DMA((3,))],
    )(land)


def _chip_copies(p_ref, land_ref, send_sems, recv_sems):
    mx, my, mc = _my_position()
    chips = [(1 - mx, my), (mx, 1 - my), (1 - mx, 1 - my)]
    return [pltpu.make_async_remote_copy(
        src_ref=p_ref.at[2 * cx + cy], dst_ref=land_ref.at[j], send_sem=send_sems.at[j], recv_sem=recv_sems.at[j],
        device_id=(cx, cy, mc), device_id_type=MESH) for j, (cx, cy) in enumerate(chips)]


def _reduce_scatter_start(g, core, name):
    pair = _pair_sum(g, _exchange_sibling(g, name + "_d2d"), core, name + "_pairsum")
    return _split_start(pair, (3,) + pair.shape[1:], _chip_copies, 3, g, name + "_ici_start")


def _reduce_scatter_finish(started, after, chip, name):
    pair, from_chips = _split_wait(started, after, _chip_copies, name + "_ici_wait")
    return _final_sum(pair, from_chips, chip, name + "_sum")


def _exchange_sibling(g, name):
    _, r, c = g.shape

    def body(g_ref, out_ref, send_sems, recv_sems):
        mx, my, mc = _my_position()
        copies = [
            pltpu.make_async_remote_copy(
                src_ref=g_ref.at[2 * k + (1 - mc)], dst_ref=out_ref.at[k],
                send_sem=send_sems.at[k], recv_sem=recv_sems.at[k],
                device_id=(mx, my, 1 - mc), device_id_type=MESH)
            for k in range(4)]
        for cp in copies:
            cp.start()
        for cp in copies:
            cp.wait()

    return pl.pallas_call(
        body, name=name,
        out_shape=jax.ShapeDtypeStruct((4, r, c), g.dtype),
        in_specs=[ANY], out_specs=ANY,
        scratch_shapes=[pltpu.SemaphoreType.DMA((4,)), pltpu.SemaphoreType.DMA((4,))],
    )(g)


def _pair_sum(g, recv, core, name):
    _, r, c = g.shape
    tr = _divisor_tile(r, 256, 16)

    def body(s_ref, g_ref, r_ref, o_ref):
        o_ref[...] = (g_ref[...].astype(F32) + r_ref[...].astype(F32)).astype(o_ref.dtype)

    return pl.pallas_call(
        body, name=name,
        out_shape=jax.ShapeDtypeStruct((4, r, c), g.dtype),
        grid_spec=pltpu.PrefetchScalarGridSpec(
            num_scalar_prefetch=1, grid=(4, r // tr),
            in_specs=[pl.BlockSpec((None, tr, c), lambda k, i, s: (2 * k + s[0], i, 0)),
                      pl.BlockSpec((None, tr, c), lambda k, i, s: (k, i, 0))],
            out_specs=pl.BlockSpec((None, tr, c), lambda k, i, s: (k, i, 0))),
        compiler_params=_params(("parallel", "parallel")),
    )(core, g, recv)


def _final_sum(p, recv, chip, name):
    _, r, c = p.shape
    tr = _divisor_tile(r, 256, 16)

    def body(s_ref, p_ref, r_ref, o_ref):
        acc = p_ref[...].astype(F32)
        for j in range(3):
            acc = acc + r_ref[j].astype(F32)
        o_ref[...] = acc

    return pl.pallas_call(
        body, name=name,
        out_shape=jax.ShapeDtypeStruct((r, c), F32),
        grid_spec=pltpu.PrefetchScalarGridSpec(
            num_scalar_prefetch=1, grid=(r // tr,),
            in_specs=[pl.BlockSpec((None, tr, c), lambda i, s: (s[0], i, 0)),
                      pl.BlockSpec((3, tr, c), lambda i, s: (0, i, 0))],
            out_specs=pl.BlockSpec((tr, c), lambda i, s: (i, 0))),
        compiler_params=_params(("parallel",)),
    )(chip, p, recv)


def _all_to_all_copies(v_ref, land_ref, send_sems, recv_sems):
    mx, my, mc = _my_position()
    me = 4 * mx + 2 * my + mc
    copies = []
    for rel in range(1, N_DEV):
        bx, by, bc = (rel >> 2) & 1, (rel >> 1) & 1, rel & 1
        target = (1 - mx if bx else mx, 1 - my if by else my, 1 - mc if bc else mc)
        copies.append(pltpu.make_async_remote_copy(
            src_ref=v_ref, dst_ref=land_ref.at[me], send_sem=send_sems.at[rel - 1], recv_sem=recv_sems.at[rel - 1],
            device_id=target, device_id_type=MESH))
    return copies


def _small_all_reduce_start(v, after, name):
    return _split_start(v, (N_DEV,) + v.shape, _all_to_all_copies, N_DEV - 1, after, name + "_start")


def _small_all_reduce_finish(started, after, dev, name):
    v, land = _split_wait(started, after, _all_to_all_copies, name + "_wait")
    rows = v.shape[0]

    def body(me_ref, v_ref, land_ref, o_ref):
        for j in range(N_DEV):
            @pl.when(me_ref[0] == j)
            def _():
                o_ref[...] = v_ref[...] if j == 0 else o_ref[...] + v_ref[...]

            @pl.when(me_ref[0] != j)
            def _():
                o_ref[...] = land_ref[j] if j == 0 else o_ref[...] + land_ref[j]

    return pl.pallas_call(
        body, name=name + "_sum",
        out_shape=jax.ShapeDtypeStruct((rows, 128), F32),
        grid_spec=pltpu.PrefetchScalarGridSpec(
            num_scalar_prefetch=1, grid=(1,),
            in_specs=[pl.BlockSpec((rows, 128), lambda i, s: (0, 0)),
                      pl.BlockSpec((N_DEV, rows, 128), lambda i, s: (0, 0, 0))],
            out_specs=pl.BlockSpec((rows, 128), lambda i, s: (0, 0))),
        compiler_params=_params(("arbitrary",)),
    )(dev, v, land)


def _assemble_w_in(blocks):
    _, d, _ = blocks.shape
    tr = _divisor_tile(d, 128, 16)
    n_tiles = WIN_N // 128
    last = (N_DEV * WIN_STRIDE) // 128

    def body(b_ref, o_ref):
        win = []
        for i in range(N_DEV):
            w = b_ref[i].astype(F32)
            win.append(pltpu.roll(w, i, 1) if i else w)
        for t in range(n_tiles):
            if t > last:
                o_ref[:, t * 128:(t + 1) * 128] = jnp.zeros((tr, 128), o_ref.dtype)
                continue
            i = min(t // 7, N_DEV - 1)
            k = t - 7 * i
            val = win[i][:, k * 128:(k + 1) * 128]
            if k == 0 and i >= 1:
                val = val + win[i - 1][:, 7 * 128:8 * 128]
            o_ref[:, t * 128:(t + 1) * 128] = val.astype(o_ref.dtype)

    return pl.pallas_call(
        body, name="assemble_w_in",
        out_shape=jax.ShapeDtypeStruct((d, WIN_N), blocks.dtype),
        grid=(d // tr,),
        in_specs=[pl.BlockSpec((N_DEV, tr, WIN_BLOCK), lambda i: (0, i, 0))],
        out_specs=pl.BlockSpec((tr, WIN_N), lambda i: (i, 0)),
        compiler_params=_params(("parallel",)),
    )(blocks)


def _extract_w_in_windows(g):
    d, _ = g.shape
    tr = _divisor_tile(d, 128, 16)

    def body(g_ref, o_ref):
        for j in range(N_DEV):
            w = g_ref[:, WIN_STRIDE * j:WIN_STRIDE * j + WIN_BLOCK].astype(F32)
            o_ref[j] = (pltpu.roll(w, WIN_BLOCK - j, 1) if j else w).astype(o_ref.dtype)

    return pl.pallas_call(
        body, name="extract_w_in_windows",
        out_shape=jax.ShapeDtypeStruct((N_DEV, d, WIN_BLOCK), g.dtype),
        grid=(d // tr,),
        in_specs=[pl.BlockSpec((tr, WIN_N), lambda i: (i, 0))],
        out_specs=pl.BlockSpec((N_DEV, tr, WIN_BLOCK), lambda i: (0, i, 0)),
        compiler_params=_params(("parallel",)),
    )(g)


def _mm(a, b, *, a_spec, b_spec, o_spec, out_shape, grid, contract, nk, name, after=None):
    dn = (((contract[0],), (contract[1],)), ((), ()))
    tm, tn = o_spec.block_shape[-2:]
    behind = [] if after is None else [after]

    def body(a_ref, b_ref, *rest):
        o_ref, *scratch = rest[len(behind):]
        part = lax.dot_general(a_ref[...], b_ref[...], dn, preferred_element_type=F32)
        if nk == 1:
            o_ref[...] = part.astype(o_ref.dtype)
            return
        acc = scratch[0]
        k = pl.program_id(2)

        @pl.when(k == 0)
        def _():
            acc[...] = part

        @pl.when(k > 0)
        def _():
            acc[...] += part

        @pl.when(k == nk - 1)
        def _():
            o_ref[...] = acc[...].astype(o_ref.dtype)

    return pl.pallas_call(
        body, name=name, out_shape=out_shape, grid=grid,
        in_specs=[a_spec, b_spec] + [ANY] * len(behind), out_specs=o_spec,
        scratch_shapes=[] if nk == 1 else [pltpu.VMEM((tm, tn), F32)],
        compiler_params=_params(("parallel", "parallel", "arbitrary")),
    )(a, b, *behind)


def _mm_nn(a, b, out_dtype, name, tm_cap=1088, tn_cap=512, tk_cap=2048, after=None):
    m, k = a.shape
    _, n = b.shape
    tm, tn, tk = _divisor_tile(m, tm_cap, 16), _divisor_tile(n, tn_cap, 128), _divisor_tile(k, tk_cap, 128)
    return _mm(a, b,
               a_spec=pl.BlockSpec((tm, tk), lambda i, j, kk: (i, kk)),
               b_spec=pl.BlockSpec((tk, tn), lambda i, j, kk: (kk, j)),
               o_spec=pl.BlockSpec((tm, tn), lambda i, j, kk: (i, j)),
               out_shape=jax.ShapeDtypeStruct((m, n), out_dtype),
               grid=(m // tm, n // tn, k // tk), contract=(1, 0), nk=k // tk, name=name, after=after)


def _mm_nt(a, b, out_dtype, name, tm_cap=1088, tn_cap=512, tk_cap=2048, after=None):
    m, k = a.shape
    n, _ = b.shape
    tm, tn, tk = _divisor_tile(m, tm_cap, 16), _divisor_tile(n, tn_cap, 128), _divisor_tile(k, tk_cap, 128)
    return _mm(a, b,
               a_spec=pl.BlockSpec((tm, tk), lambda i, j, kk: (i, kk)),
               b_spec=pl.BlockSpec((tn, tk), lambda i, j, kk: (j, kk)),
               o_spec=pl.BlockSpec((tm, tn), lambda i, j, kk: (i, j)),
               out_shape=jax.ShapeDtypeStruct((m, n), out_dtype),
               grid=(m // tm, n // tn, k // tk), contract=(1, 1), nk=k // tk, name=name, after=after)


def _mm_tn(a, b, out_dtype, name, tm_cap=1024, tn_cap=512, after=None):
    l, m = a.shape
    _, n = b.shape
    tm, tn = _divisor_tile(m, tm_cap, 128), _divisor_tile(n, tn_cap, 128)
    return _mm(a, b,
               a_spec=pl.BlockSpec((l, tm), lambda i, j, kk: (0, i)),
               b_spec=pl.BlockSpec((l, tn), lambda i, j, kk: (0, j)),
               o_spec=pl.BlockSpec((tm, tn), lambda i, j, kk: (i, j)),
               out_shape=jax.ShapeDtypeStruct((m, n), out_dtype),
               grid=(m // tm, n // tn, 1), contract=(0, 0), nk=1, name=name, after=after)


def _mm_d_cn(d_u, w_up_blocks, after):
    _, l, d_ff = d_u.shape
    n, d, shard = w_up_blocks.shape
    per = d_ff // shard
    tm, tn = _divisor_tile(l, 544, 16), _divisor_tile(d, 256, 128)

    def body(a_ref, b_ref, after_ref, o_ref):
        acc = None
        for k in range(n):
            part = _dot_nt(a_ref[k // per, :, (k % per) * shard:(k % per + 1) * shard], b_ref[k])
            acc = part if acc is None else acc + part
        o_ref[...] = acc

    return pl.pallas_call(
        body, name="mm_d_cn", out_shape=jax.ShapeDtypeStruct((l, d), F32), grid=(l // tm, d // tn),
        in_specs=[pl.BlockSpec((2, tm, d_ff), lambda i, j: (0, i, 0)),
                  pl.BlockSpec((n, tn, shard), lambda i, j: (0, j, 0)), ANY],
        out_specs=pl.BlockSpec((tm, tn), lambda i, j: (i, j)),
        compiler_params=_params(("parallel", "parallel")),
    )(d_u, w_up_blocks, after)


def _row_tile(l):
    return _divisor_tile(l, 544, 8)


def _rmsnorm_fwd(h, gain, name, res=None):
    l, d = h.shape
    tr = _row_tile(l)
    row = pl.BlockSpec((tr, d), lambda i: (i, 0))
    vec = pl.BlockSpec((1, d), lambda i: (0, 0))

    def body(*refs):
        if res is None:
            h_ref, g_ref, n_ref = refs
            x = h_ref[...]
        else:
            h_ref, r_ref, g_ref, s_ref, n_ref = refs
            x = h_ref[...] + r_ref[...]
            s_ref[...] = x
        y = x * lax.rsqrt(jnp.mean(x * x, axis=-1, keepdims=True) + NORM_EPS)
        n_ref[...] = (y * g_ref[...]).astype(n_ref.dtype)

    normed = jax.ShapeDtypeStruct((l, d), MXU_DTYPE)
    if res is None:
        return pl.pallas_call(body, name=name, out_shape=normed, grid=(l // tr,), in_specs=[row, vec],
                              out_specs=row, compiler_params=_params(("parallel",)))(h, gain)
    return pl.pallas_call(body, name=name, out_shape=(jax.ShapeDtypeStruct((l, d), F32), normed),
                          grid=(l // tr,), in_specs=[row, row, vec], out_specs=(row, row),
                          compiler_params=_params(("parallel",)))(h, res, gain)


def _rmsnorm_bwd(d_res, d_normed, x, gain, name, with_mxu_copy):
    l, d = x.shape
    tr = _row_tile(l)
    row = pl.BlockSpec((tr, d), lambda i: (i, 0))
    vec = pl.BlockSpec((1, d), lambda i: (0, 0))

    def body(dres_ref, dn_ref, x_ref, g_ref, dx_ref, *rest):
        dg_ref = rest[-1]
        xv = x_ref[...]
        r = lax.rsqrt(jnp.mean(xv * xv, axis=-1, keepdims=True) + NORM_EPS)
        xh = xv * r
        dn = dn_ref[...]
        dxh = dn * g_ref[...]
        dx = dres_ref[...] + r * (dxh - xh * jnp.mean(dxh * xh, axis=-1, keepdims=True))
        dx_ref[...] = dx
        if with_mxu_copy:
            rest[0][...] = dx.astype(MXU_DTYPE)

        @pl.when(pl.program_id(0) == 0)
        def _():
            dg_ref[...] = jnp.zeros_like(dg_ref)

        dg_ref[...] += jnp.sum(dn * xh, axis=0, keepdims=True)

    outs = [jax.ShapeDtypeStruct((l, d), F32)]
    specs = [row]
    if with_mxu_copy:
        outs.append(jax.ShapeDtypeStruct((l, d), MXU_DTYPE))
        specs.append(row)
    outs.append(jax.ShapeDtypeStruct((1, d), F32))
    specs.append(vec)
    return pl.pallas_call(body, name=name, out_shape=tuple(outs), grid=(l // tr,),
                          in_specs=[row, row, row, vec], out_specs=tuple(specs),
                          compiler_params=_params(("arbitrary",)))(d_res, d_normed, x, gain)


def _loss_head(h1, mlp_out, gain, target):
    l, d = h1.shape
    n_blocks = l // CHUNK
    row = pl.BlockSpec((CHUNK, d), lambda i: (i, 0))
    vec = pl.BlockSpec((1, d), lambda i: (0, 0))
    tgt = pl.BlockSpec((CHUNK, d), lambda i: (jnp.maximum(i - 1, 0), 0))

    def body(h_ref, m_ref, g_ref, t_ref, dh_ref, dhb_ref, dg_ref, loss_ref, sq_ref):
        i = pl.program_id(0)
        x = h_ref[...] + m_ref[...]
        r = lax.rsqrt(jnp.mean(x * x, axis=-1, keepdims=True) + NORM_EPS)
        xh = x * r
        g = g_ref[...]
        real = i >= 1
        err = jnp.where(real, xh * g - t_ref[...], 0.0)
        dy = err * (1.0 / d)
        dxh = dy * g
        dh = r * (dxh - xh * jnp.mean(dxh * xh, axis=-1, keepdims=True))
        dh_ref[...] = dh
        dhb_ref[...] = dh.astype(MXU_DTYPE)

        @pl.when(i == 0)
        def _():
            dg_ref[...] = jnp.zeros_like(dg_ref)
            sq_ref[...] = jnp.zeros_like(sq_ref)

        dg_ref[...] += jnp.sum(dy * xh, axis=0, keepdims=True)
        sq_ref[...] += jnp.sum(err * err, axis=0, keepdims=True)

        @pl.when(i == n_blocks - 1)
        def _():
            total = jnp.sum(sq_ref[...], axis=-1, keepdims=True) * (0.5 / d)
            loss_ref[...] = jnp.broadcast_to(total, (1, 128))

    return pl.pallas_call(
        body, name="loss_head",
        out_shape=(jax.ShapeDtypeStruct((l, d), F32), jax.ShapeDtypeStruct((l, d), MXU_DTYPE),
                   jax.ShapeDtypeStruct((1, d), F32), jax.ShapeDtypeStruct((1, 128), F32)),
        grid=(n_blocks,), in_specs=[row, row, vec, tgt],
        out_specs=(row, row, vec, pl.BlockSpec((1, 128), lambda i: (0, 0))),
        scratch_shapes=[pltpu.VMEM((1, d), F32)],
        compiler_params=_params(("arbitrary",)),
    )(h1, mlp_out, gain, target)


def _dot(a, b):
    return jnp.dot(a, b, preferred_element_type=F32)


def _dot_nt(a, b):
    return lax.dot_general(a, b, (((1,), (1,)), ((), ())), preferred_element_type=F32)


def _dot_tn(a, b):
    return lax.dot_general(a, b, (((0,), (0,)), ((), ())), preferred_element_type=F32)


def _rope(t, cos2, sin2):
    return t * cos2 + pltpu.roll(t, HEAD_DIM // 2, 1) * sin2


def _rope_bwd(dr, cos2, sin2):
    return dr * cos2 + pltpu.roll(dr * sin2, HEAD_DIM // 2, 1)


def _sigmoid(x):
    return 1.0 / (1.0 + jnp.exp(-x))


def _row_valid(block, rows):
    r = block * CHUNK + lax.broadcasted_iota(jnp.int32, (rows, 1), 0)
    return r >= PAD_ROWS


def _retention_consts(l):
    pos = jnp.arange(l, dtype=F32) - PAD_ROWS
    inv_freq = 1.0 / (ROPE_BASE ** (jnp.arange(0, HEAD_DIM, 2, dtype=F32) / HEAD_DIM))
    ang = pos[:, None] * inv_freq[None, :]
    cos, sin = jnp.cos(ang), jnp.sin(ang)
    cos2 = jnp.concatenate([cos, cos], axis=-1)
    sin2 = jnp.concatenate([-sin, sin], axis=-1)
    log_g = jnp.log1p(-jnp.exp2(-5.0 - jnp.arange(N_HEADS, dtype=F32)))
    idx = jnp.arange(CHUNK, dtype=F32)
    diff = idx[:, None] - idx[None, :]
    decay = jnp.where(diff >= 0, jnp.exp(jnp.maximum(diff, 0.0)[None] * log_g[:, None, None]), 0.0)
    xi = jnp.exp((idx + 1.0)[None, :] * log_g[:, None])
    zeta = jnp.exp((CHUNK - 1.0 - idx)[None, :] * log_g[:, None])
    g_chunk = jnp.exp(CHUNK * log_g)
    bcast = lambda v: jnp.broadcast_to(v[:, :, None], (N_HEADS, CHUNK, HEAD_DIM))
    g_rows = jnp.broadcast_to(g_chunk[:, None, None], (N_HEADS, 8, HEAD_DIM))
    return cos2, sin2, decay, bcast(xi), bcast(zeta), g_rows


def _retention_fwd(proj, ret_gain, consts):
    l = proj.shape[0]
    n_chunks = l // CHUNK
    cos2, sin2, decay, xi, zeta, g_rows = consts
    scale = HEAD_DIM ** -0.5

    def body(p_ref, cos_ref, sin_ref, dec_ref, xi_ref, zeta_ref, gr_ref, gain_ref,
             mix_ref, o_ref, st_ref, state):
        c = pl.program_id(0)

        @pl.when(c == 0)
        def _():
            state[...] = jnp.zeros_like(state)

        cos_v, sin_v = cos_ref[...], sin_ref[...]
        valid = _row_valid(c, CHUNK)
        for h in range(N_HEADS):
            cols = slice(h * HEAD_DIM, (h + 1) * HEAD_DIM)
            q = p_ref[:, h * HEAD_DIM:(h + 1) * HEAD_DIM]
            k = p_ref[:, GROUP + h * HEAD_DIM:GROUP + (h + 1) * HEAD_DIM]
            v = p_ref[:, 2 * GROUP + h * HEAD_DIM:2 * GROUP + (h + 1) * HEAD_DIM]
            g = p_ref[:, 3 * GROUP + h * HEAD_DIM:3 * GROUP + (h + 1) * HEAD_DIM]
            rq = _rope(q, cos_v, sin_v).astype(MXU_DTYPE)
            rk = _rope(k, cos_v, sin_v) * scale
            rkb = rk.astype(MXU_DTYPE)
            vb = v.astype(MXU_DTYPE)
            st = state[h]
            st_ref[h] = st
            s = _dot_nt(rq, rkb) * dec_ref[h]
            o = _dot(s.astype(MXU_DTYPE), vb) + _dot(rq, st.astype(MXU_DTYPE)) * xi_ref[h]
            kz = (rk * zeta_ref[h]).astype(MXU_DTYPE)
            state[h] = gr_ref[h, 0:1, :] * st + _dot_tn(kz, vb)
            o_ref[:, cols] = o
            mu = jnp.mean(o, axis=-1, keepdims=True)
            oc = o - mu
            yn = oc * lax.rsqrt(jnp.mean(oc * oc, axis=-1, keepdims=True) + NORM_EPS)
            ret = (g * _sigmoid(g)) * (yn * gain_ref[:, cols])
            mix_ref[:, cols] = jnp.where(valid, ret, 0.0).astype(mix_ref.dtype)

    head_tab = pl.BlockSpec((N_HEADS, CHUNK, HEAD_DIM), lambda c: (0, 0, 0))
    return pl.pallas_call(
        body, name="retention_fwd",
        out_shape=(jax.ShapeDtypeStruct((l, 2 * GROUP), MXU_DTYPE), jax.ShapeDtypeStruct((l, GROUP), F32),
                   jax.ShapeDtypeStruct((n_chunks, N_HEADS, HEAD_DIM, HEAD_DIM), F32)),
        grid=(n_chunks,),
        in_specs=[pl.BlockSpec((CHUNK, 4 * GROUP), lambda c: (c, 0)),
                  pl.BlockSpec((CHUNK, HEAD_DIM), lambda c: (c, 0)),
                  pl.BlockSpec((CHUNK, HEAD_DIM), lambda c: (c, 0)),
                  head_tab, head_tab, head_tab,
                  pl.BlockSpec((N_HEADS, 8, HEAD_DIM), lambda c: (0, 0, 0)),
                  pl.BlockSpec((1, GROUP), lambda c: (0, 0))],
        out_specs=(pl.BlockSpec((CHUNK, GROUP), lambda c: (c, 0)),
                   pl.BlockSpec((CHUNK, GROUP), lambda c: (c, 0)),
                   pl.BlockSpec((None, N_HEADS, HEAD_DIM, HEAD_DIM), lambda c: (c, 0, 0, 0))),
        scratch_shapes=[pltpu.VMEM((N_HEADS, HEAD_DIM, HEAD_DIM), F32)],
        compiler_params=_params(("arbitrary",)),
    )(proj, cos2, sin2, decay, xi, zeta, g_rows, ret_gain)


def _retention_bwd(proj, o_pre, states, d_mix, ret_gain, consts):
    l = proj.shape[0]
    n_chunks = l // CHUNK
    cos2, sin2, decay, xi, zeta, g_rows = consts
    scale = HEAD_DIM ** -0.5
    rev = lambda c: n_chunks - 1 - c

    def body(p_ref, o_ref, st_ref, dm_ref, cos_ref, sin_ref, dec_ref, xi_ref, zeta_ref, gr_ref, gain_ref,
             dp_ref, dgain_ref, dstate):
        step = pl.program_id(0)

        @pl.when(step == 0)
        def _():
            dstate[...] = jnp.zeros_like(dstate)
            dgain_ref[...] = jnp.zeros_like(dgain_ref)

        cos_v, sin_v = cos_ref[...], sin_ref[...]
        valid = _row_valid(rev(step), CHUNK)
        for h in range(N_HEADS):
            cols = slice(h * HEAD_DIM, (h + 1) * HEAD_DIM)
            q = p_ref[:, h * HEAD_DIM:(h + 1) * HEAD_DIM]
            k = p_ref[:, GROUP + h * HEAD_DIM:GROUP + (h + 1) * HEAD_DIM]
            v = p_ref[:, 2 * GROUP + h * HEAD_DIM:2 * GROUP + (h + 1) * HEAD_DIM]
            g = p_ref[:, 3 * GROUP + h * HEAD_DIM:3 * GROUP + (h + 1) * HEAD_DIM]
            o = o_ref[:, cols]
            gain = gain_ref[:, cols]
            d_ret = jnp.where(valid, dm_ref[:, cols], 0.0)
            mu = jnp.mean(o, axis=-1, keepdims=True)
            oc = o - mu
            rstd = lax.rsqrt(jnp.mean(oc * oc, axis=-1, keepdims=True) + NORM_EPS)
            yn = oc * rstd
            sig = _sigmoid(g)
            gate = g * sig
            dgain_ref[:, cols] += jnp.sum(d_ret * gate * yn, axis=0, keepdims=True)
            d_g = d_ret * (yn * gain) * (sig * (1.0 + g * (1.0 - sig)))
            d_yn = d_ret * gate * gain
            d_o = rstd * (d_yn - jnp.mean(d_yn, axis=-1, keepdims=True)
                          - yn * jnp.mean(d_yn * yn, axis=-1, keepdims=True))
            rq = _rope(q, cos_v, sin_v)
            rk = _rope(k, cos_v, sin_v) * scale
            rqb, rkb, vb = rq.astype(MXU_DTYPE), rk.astype(MXU_DTYPE), v.astype(MXU_DTYPE)
            dob = d_o.astype(MXU_DTYPE)
            dec = dec_ref[h]
            xi_h, zeta_h = xi_ref[h], zeta_ref[h]
            st_b = st_ref[h].astype(MXU_DTYPE)
            dst = dstate[h]
            dst_b = dst.astype(MXU_DTYPE)
            s_b = (_dot_nt(rqb, rkb) * dec).astype(MXU_DTYPE)
            da_b = (_dot_nt(dob, vb) * dec).astype(MXU_DTYPE)
            doxi_b = (d_o * xi_h).astype(MXU_DTYPE)
            kz_b = (rk * zeta_h).astype(MXU_DTYPE)
            d_rq = _dot(da_b, rkb) + _dot_nt(doxi_b, st_b)
            d_rk = _dot_tn(da_b, rqb) + _dot_nt(vb, dst_b) * zeta_h
            d_v = _dot_tn(s_b, dob) + _dot(kz_b, dst_b)
            dstate[h] = gr_ref[h, 0:1, :] * dst + _dot_tn(rqb, doxi_b)
            d_q = _rope_bwd(d_rq, cos_v, sin_v)
            d_k = _rope_bwd(d_rk * scale, cos_v, sin_v)
            dp_ref[:, h * HEAD_DIM:(h + 1) * HEAD_DIM] = d_q.astype(dp_ref.dtype)
            dp_ref[:, GROUP + h * HEAD_DIM:GROUP + (h + 1) * HEAD_DIM] = d_k.astype(dp_ref.dtype)
            dp_ref[:, 2 * GROUP + h * HEAD_DIM:2 * GROUP + (h + 1) * HEAD_DIM] = d_v.astype(dp_ref.dtype)
            dp_ref[:, 3 * GROUP + h * HEAD_DIM:3 * GROUP + (h + 1) * HEAD_DIM] = d_g.astype(dp_ref.dtype)

    head_tab = pl.BlockSpec((N_HEADS, CHUNK, HEAD_DIM), lambda c: (0, 0, 0))
    return pl.pallas_call(
        body, name="retention_bwd",
        out_shape=(jax.ShapeDtypeStruct((l, 4 * GROUP), MXU_DTYPE), jax.ShapeDtypeStruct((1, GROUP), F32)),
        grid=(n_chunks,),
        in_specs=[pl.BlockSpec((CHUNK, 4 * GROUP), lambda c: (rev(c), 0)),
                  pl.BlockSpec((CHUNK, GROUP), lambda c: (rev(c), 0)),
                  pl.BlockSpec((None, N_HEADS, HEAD_DIM, HEAD_DIM), lambda c: (rev(c), 0, 0, 0)),
                  pl.BlockSpec((CHUNK, GROUP), lambda c: (rev(c), 0)),
                  pl.BlockSpec((CHUNK, HEAD_DIM), lambda c: (rev(c), 0)),
                  pl.BlockSpec((CHUNK, HEAD_DIM), lambda c: (rev(c), 0)),
                  head_tab, head_tab, head_tab,
                  pl.BlockSpec((N_HEADS, 8, HEAD_DIM), lambda c: (0, 0, 0)),
                  pl.BlockSpec((1, GROUP), lambda c: (0, 0))],
        out_specs=(pl.BlockSpec((CHUNK, 4 * GROUP), lambda c: (rev(c), 0)),
                   pl.BlockSpec((1, GROUP), lambda c: (0, 0))),
        scratch_shapes=[pltpu.VMEM((N_HEADS, HEAD_DIM, HEAD_DIM), F32)],
        compiler_params=_params(("arbitrary",)),
    )(proj, o_pre, states, d_mix, cos2, sin2, decay, xi, zeta, g_rows, ret_gain)


FF_TILE = (7 * GROUP) // 128


def _log_forget(ff, bias_row, valid):
    x = ff + bias_row
    e = jnp.exp(-jnp.abs(x))
    lf = jnp.minimum(x, 0.0) - jnp.log(1.0 + e)
    head_lane = lax.broadcasted_iota(jnp.int32, x.shape, 1) < N_HEADS
    keep = lambda t: jnp.where(head_lane, jnp.where(valid, t, 0.0), 0.0)
    return keep(lf), keep(jnp.where(x >= 0, e, 1.0) / (1.0 + e))


def _fox_prep(proj, bias_row):
    l = proj.shape[0]
    n_blocks = l // CHUNK

    def body(ff_ref, b_ref, bc_ref, rows_ref, cum):
        r = lax.broadcasted_iota(jnp.int32, (CHUNK, CHUNK), 0)
        cidx = lax.broadcasted_iota(jnp.int32, (CHUNK, CHUNK), 1)
        tri = jnp.where(r >= cidx, 1.0, 0.0).astype(F32)
        carry = jnp.zeros((1, 128), F32)
        for blk in range(n_blocks):
            rows = slice(blk * CHUNK, (blk + 1) * CHUNK)
            valid = _row_valid(blk, CHUNK)
            lf, _ = _log_forget(ff_ref[rows, :], b_ref[...], valid)
            local = jnp.dot(tri, lf, precision=lax.Precision.HIGHEST, preferred_element_type=F32) + carry
            carry = local[CHUNK - 1:CHUNK, :]
            masked = jnp.where(valid, local, -NEG_BIG)
            cum[rows, :] = masked
            t = masked.T
            for h in range(N_HEADS):
                rows_ref[h, :, rows] = t[h:h + 1, :]
        full = cum[...]
        for h in range(N_HEADS):
            bc_ref[h] = jnp.broadcast_to(full[:, h:h + 1], (l, 128))

    return pl.pallas_call(
        body, name="fox_prep",
        out_shape=(jax.ShapeDtypeStruct((N_HEADS, l, 128), F32), jax.ShapeDtypeStruct((N_HEADS, 1, l), F32)),
        grid=(1,),
        in_specs=[pl.BlockSpec((l, 128), lambda i: (0, FF_TILE)), pl.BlockSpec((1, 128), lambda i: (0, 0))],
        out_specs=(pl.BlockSpec((N_HEADS, l, 128), lambda i: (0, 0, 0)),
                   pl.BlockSpec((N_HEADS, 1, l), lambda i: (0, 0, 0))),
        scratch_shapes=[pltpu.VMEM((l, 128), F32)],
        compiler_params=_params(("arbitrary",)),
    )(proj, bias_row)


def _key_buckets(l):
    n_blocks = l // CHUNK
    return [min((b + 1) * 2 * CHUNK, l) for b in range(-(-n_blocks // 2))]


def _per_bucket(i, lks, step):
    for b, lk in enumerate(lks):
        @pl.when(i // 2 == b)
        def _(lk=lk):
            step(lk)


def _fox_fwd(proj, cum_bc, cum_rows, mix):
    l = proj.shape[0]
    n_blocks = l // CHUNK
    lks = _key_buckets(l)
    scale = HEAD_DIM ** -0.5
    qt, kt, vt = 4 * N_HEADS, 5 * N_HEADS, 6 * N_HEADS

    def body(q_ref, k_ref, v_ref, cq_ref, ck_ref, mix_in, o_ref, kb_s, vb_s):
        i = pl.program_id(1)

        @pl.when(i == 0)
        def _():
            kb_s[...] = k_ref[...].astype(MXU_DTYPE)
            vb_s[...] = v_ref[...].astype(MXU_DTYPE)

        qb = q_ref[...].astype(MXU_DTYPE)
        cq = cq_ref[...]

        def step(lk):
            s = _dot_nt(qb, kb_s[0:lk, :]) * scale
            bias = jnp.tile(cq, (1, lk // CHUNK)) - ck_ref[:, 0:lk]
            q_pos = i * CHUNK + lax.broadcasted_iota(jnp.int32, (CHUNK, lk), 0)
            k_pos = lax.broadcasted_iota(jnp.int32, (CHUNK, lk), 1)
            s = jnp.where(k_pos <= q_pos, s + bias, NEG_BIG)
            m = jnp.max(s, axis=-1, keepdims=True)
            e = jnp.exp(s - m)
            p = e * (1.0 / jnp.sum(e, axis=-1, keepdims=True))
            o = _dot(p.astype(MXU_DTYPE), vb_s[0:lk, :])
            o_ref[...] = jnp.where(_row_valid(i, CHUNK), o, 0.0).astype(o_ref.dtype)

        _per_bucket(i, lks, step)

    return pl.pallas_call(
        body, name="fox_fwd",
        out_shape=jax.ShapeDtypeStruct(mix.shape, mix.dtype),
        grid=(N_HEADS, n_blocks),
        in_specs=[pl.BlockSpec((CHUNK, HEAD_DIM), lambda h, i: (i, qt + h)),
                  pl.BlockSpec((l, HEAD_DIM), lambda h, i: (0, kt + h)),
                  pl.BlockSpec((l, HEAD_DIM), lambda h, i: (0, vt + h)),
                  pl.BlockSpec((None, CHUNK, 128), lambda h, i: (h, i, 0)),
                  pl.BlockSpec((None, 1, l), lambda h, i: (h, 0, 0)),
                  ANY],
        out_specs=pl.BlockSpec((CHUNK, HEAD_DIM), lambda h, i: (i, N_HEADS + h)),
        input_output_aliases={5: 0},
        scratch_shapes=[pltpu.VMEM((l, HEAD_DIM), MXU_DTYPE), pltpu.VMEM((l, HEAD_DIM), MXU_DTYPE)],
        compiler_params=_params(("parallel", "arbitrary")),
    )(proj, proj, proj, cum_bc, cum_rows, mix)


def _fox_bwd(proj, cum_bc, cum_rows, d_mix):
    l = proj.shape[0]
    n_blocks = l // CHUNK
    lks = _key_buckets(l)
    scale = HEAD_DIM ** -0.5
    qt, kt, vt = 4 * N_HEADS, 5 * N_HEADS, 6 * N_HEADS

    def body(q_ref, k_ref, v_ref, do_ref, ck_ref, cq_ref, dq_ref, dk_ref, dv_ref, ds_ref, dk_acc, dv_acc, kb_s, vb_s):
        i = pl.program_id(1)

        @pl.when(i == 0)
        def _():
            dk_acc[...] = jnp.zeros_like(dk_acc)
            dv_acc[...] = jnp.zeros_like(dv_acc)
            ds_ref[...] = jnp.zeros_like(ds_ref)
            kb_s[...] = k_ref[...].astype(MXU_DTYPE)
            vb_s[...] = v_ref[...].astype(MXU_DTYPE)

        qb = q_ref[...].astype(MXU_DTYPE)
        dob = jnp.where(_row_valid(i, CHUNK), do_ref[...], 0.0).astype(MXU_DTYPE)
        cq = cq_ref[...]

        def step(lk):
            kb, vb = kb_s[0:lk, :], vb_s[0:lk, :]
            k_pos = lax.broadcasted_iota(jnp.int32, (lk, CHUNK), 0)
            q_pos = i * CHUNK + lax.broadcasted_iota(jnp.int32, (lk, CHUNK), 1)
            s_t = _dot_nt(kb, qb) * scale + (cq - ck_ref[0:lk, :])
            s_t = jnp.where(k_pos <= q_pos, s_t, NEG_BIG)
            m = jnp.max(s_t, axis=0, keepdims=True)
            e = jnp.exp(s_t - m)
            p_t = e * (1.0 / jnp.sum(e, axis=0, keepdims=True))
            dp_t = _dot_nt(vb, dob)
            delta = jnp.sum(p_t * dp_t, axis=0, keepdims=True)
            ds_t = p_t * (dp_t - delta)
            ds_b = ds_t.astype(MXU_DTYPE)
            dv_acc[0:lk, :] += _dot(p_t.astype(MXU_DTYPE), dob)
            dk_acc[0:lk, :] += _dot(ds_b, qb) * scale
            ds_ref[0:lk, :] += ds_t
            dq_ref[...] = (_dot_tn(ds_b, kb) * scale).astype(dq_ref.dtype)

        _per_bucket(i, lks, step)

        @pl.when(i == n_blocks - 1)
        def _():
            dk_ref[...] = dk_acc[...].astype(dk_ref.dtype)
            dv_ref[...] = dv_acc[...].astype(dv_ref.dtype)

    col = jax.ShapeDtypeStruct((l, GROUP), MXU_DTYPE)
    return pl.pallas_call(
        body, name="fox_bwd",
        out_shape=(col, col, col, jax.ShapeDtypeStruct((N_HEADS, l, 128), F32)),
        grid=(N_HEADS, n_blocks),
        in_specs=[pl.BlockSpec((CHUNK, HEAD_DIM), lambda h, i: (i, qt + h)),
                  pl.BlockSpec((l, HEAD_DIM), lambda h, i: (0, kt + h)),
                  pl.BlockSpec((l, HEAD_DIM), lambda h, i: (0, vt + h)),
                  pl.BlockSpec((CHUNK, HEAD_DIM), lambda h, i: (i, N_HEADS + h)),
                  pl.BlockSpec((None, l, 128), lambda h, i: (h, 0, 0)),
                  pl.BlockSpec((None, 1, CHUNK), lambda h, i: (h, 0, i))],
        out_specs=(pl.BlockSpec((CHUNK, HEAD_DIM), lambda h, i: (i, h)),
                   pl.BlockSpec((l, HEAD_DIM), lambda h, i: (0, h)),
                   pl.BlockSpec((l, HEAD_DIM), lambda h, i: (0, h)),
                   pl.BlockSpec((None, l, 128), lambda h, i: (h, 0, 0))),
        scratch_shapes=[pltpu.VMEM((l, HEAD_DIM), F32), pltpu.VMEM((l, HEAD_DIM), F32),
                        pltpu.VMEM((l, HEAD_DIM), MXU_DTYPE), pltpu.VMEM((l, HEAD_DIM), MXU_DTYPE)],
        compiler_params=_params(("parallel", "arbitrary")),
    )(proj, proj, proj, d_mix, cum_bc, cum_rows)


def _fox_gate_bwd(ds_sum, proj, bias_row):
    l = proj.shape[0]
    n_blocks = l // CHUNK

    def body(ds_ref, ff_ref, b_ref, dff_ref, db_ref):
        r = lax.broadcasted_iota(jnp.int32, (CHUNK, CHUNK), 0)
        cidx = lax.broadcasted_iota(jnp.int32, (CHUNK, CHUNK), 1)
        upper = jnp.where(cidx >= r, 1.0, 0.0).astype(F32)
        carry = jnp.zeros((1, 128), F32)
        db = jnp.zeros((1, 128), F32)
        for blk in reversed(range(n_blocks)):
            rows = slice(blk * CHUNK, (blk + 1) * CHUNK)
            key_sum = jnp.zeros((CHUNK, 128), F32)
            for h in range(N_HEADS):
                select = jnp.where(cidx == h, 1.0, 0.0).astype(F32)
                key_sum = key_sum + jnp.dot(ds_ref[h, rows, :], select, precision=lax.Precision.HIGHEST,
                                            preferred_element_type=F32)
            suffix = jnp.dot(upper, key_sum, precision=lax.Precision.HIGHEST, preferred_element_type=F32) + carry
            carry = suffix[0:1, :]
            _, dsig = _log_forget(ff_ref[rows, :], b_ref[...], _row_valid(blk, CHUNK))
            dff = -suffix * dsig
            dff_ref[rows, :] = dff.astype(dff_ref.dtype)
            db = db + jnp.sum(dff, axis=0, keepdims=True)
        db_ref[...] = db

    return pl.pallas_call(
        body, name="fox_gate_bwd",
        out_shape=(jax.ShapeDtypeStruct((l, 128), MXU_DTYPE), jax.ShapeDtypeStruct((1, 128), F32)),
        grid=(1,),
        in_specs=[pl.BlockSpec((N_HEADS, l, 128), lambda i: (0, 0, 0)),
                  pl.BlockSpec((l, 128), lambda i: (0, FF_TILE)),
                  pl.BlockSpec((1, 128), lambda i: (0, 0))],
        out_specs=(pl.BlockSpec((l, 128), lambda i: (0, 0)), pl.BlockSpec((1, 128), lambda i: (0, 0))),
        compiler_params=_params(("arbitrary",)),
    )(ds_sum, proj, bias_row)


def _conv(u, w, b):
    return b + w[0:1, :] * pltpu.roll(u, 2, 0) + w[1:2, :] * pltpu.roll(u, 1, 0) + w[2:3, :] * u


def _conv_act_fwd(u, conv_w, conv_b, d_ff):
    l = u.shape[0]
    tc = _divisor_tile(d_ff, 256, 128)
    nt = d_ff // tc

    def body(ug_ref, uv_ref, wg_ref, wv_ref, bg_ref, bv_ref, a_ref):
        yg = _conv(ug_ref[...], wg_ref[...], bg_ref[...])
        yv = _conv(uv_ref[...], wv_ref[...], bv_ref[...])
        act = yg * _sigmoid(yg) * yv
        a_ref[...] = jnp.where(_row_valid(0, l), act, 0.0).astype(a_ref.dtype)

    return pl.pallas_call(
        body, name="conv_act_fwd",
        out_shape=jax.ShapeDtypeStruct((l, d_ff), MXU_DTYPE),
        grid=(nt,),
        in_specs=[pl.BlockSpec((l, tc), lambda j: (0, j)), pl.BlockSpec((l, tc), lambda j: (0, j + nt)),
                  pl.BlockSpec((8, tc), lambda j: (0, j)), pl.BlockSpec((8, tc), lambda j: (0, j + nt)),
                  pl.BlockSpec((1, tc), lambda j: (0, j)), pl.BlockSpec((1, tc), lambda j: (0, j + nt))],
        out_specs=pl.BlockSpec((l, tc), lambda j: (0, j)),
        compiler_params=_params(("parallel",)),
    )(u, u, conv_w, conv_w, conv_b, conv_b)


def _conv_act_bwd(u, conv_w, conv_b, d_act, d_ff):
    l = u.shape[0]
    tc = _divisor_tile(d_ff, 256, 128)
    nt = d_ff // tc

    def body(ug_ref, uv_ref, wg_ref, wv_ref, bg_ref, bv_ref, da_ref, du_ref, dwb_ref):
        valid = _row_valid(0, l)
        ug, uv = ug_ref[...], uv_ref[...]
        wg, wv = wg_ref[...], wv_ref[...]
        yg = _conv(ug, wg, bg_ref[...])
        yv = _conv(uv, wv, bv_ref[...])
        sig = _sigmoid(yg)
        da = jnp.where(valid, da_ref[...], 0.0)
        d_yv = da * (yg * sig)
        d_yg = da * yv * (sig * (1.0 + yg * (1.0 - sig)))
        for idx, (dy, uu, w) in enumerate(((d_yg, ug, wg), (d_yv, uv, wv))):
            du = w[2:3, :] * dy + w[1:2, :] * pltpu.roll(dy, l - 1, 0) + w[0:1, :] * pltpu.roll(dy, l - 2, 0)
            du_ref[idx] = jnp.where(valid, du, 0.0).astype(du_ref.dtype)
            dwb_ref[idx, 0:1, :] = jnp.sum(dy * pltpu.roll(uu, 2, 0), axis=0, keepdims=True)
            dwb_ref[idx, 1:2, :] = jnp.sum(dy * pltpu.roll(uu, 1, 0), axis=0, keepdims=True)
            dwb_ref[idx, 2:3, :] = jnp.sum(dy * uu, axis=0, keepdims=True)
            dwb_ref[idx, 3:4, :] = jnp.sum(dy, axis=0, keepdims=True)
            dwb_ref[idx, 4:8, :] = jnp.zeros((4, tc), F32)

    return pl.pallas_call(
        body, name="conv_act_bwd",
        out_shape=(jax.ShapeDtypeStruct((2, l, d_ff), MXU_DTYPE), jax.ShapeDtypeStruct((2, 8, d_ff), F32)),
        grid=(nt,),
        in_specs=[pl.BlockSpec((l, tc), lambda j: (0, j)), pl.BlockSpec((l, tc), lambda j: (0, j + nt)),
                  pl.BlockSpec((8, tc), lambda j: (0, j)), pl.BlockSpec((8, tc), lambda j: (0, j + nt)),
                  pl.BlockSpec((1, tc), lambda j: (0, j)), pl.BlockSpec((1, tc), lambda j: (0, j + nt)),
                  pl.BlockSpec((l, tc), lambda j: (0, j))],
        out_specs=(pl.BlockSpec((2, l, tc), lambda j: (0, 0, j)), pl.BlockSpec((2, 8, tc), lambda j: (0, 0, j))),
        compiler_params=_params(("parallel",)),
    )(u, u, conv_w, conv_w, conv_b, conv_b, d_act)


def _adamw(w, g, m, v, name):
    shape = w.shape
    if w.ndim == 1:
        as2d = (1, shape[0])
    else:
        as2d = (int(np.prod(shape[:-1])), shape[-1])
    r, c = as2d
    tr = _divisor_tile(r, 256, 8)
    spec = pl.BlockSpec((tr, c), lambda i: (i, 0))

    def body(w_ref, g_ref, m_ref, v_ref, d_ref, nm_ref, nv_ref):
        gv = g_ref[...]
        nm = ADAM_B1 * m_ref[...] + (1.0 - ADAM_B1) * gv
        nv = ADAM_B2 * v_ref[...] + (1.0 - ADAM_B2) * (gv * gv)
        m_hat = nm / (1.0 - ADAM_B1 ** ADAM_STEP)
        v_hat = nv / (1.0 - ADAM_B2 ** ADAM_STEP)
        d_ref[...] = -ADAM_LR * (m_hat / (jnp.sqrt(v_hat) + ADAM_EPS) + ADAM_WD * w_ref[...])
        nm_ref[...] = nm
        nv_ref[...] = nv

    sds = jax.ShapeDtypeStruct(as2d, F32)
    outs = pl.pallas_call(
        body, name=name, out_shape=(sds, sds, sds), grid=(r // tr,),
        in_specs=[spec] * 4, out_specs=(spec,) * 3,
        compiler_params=_params(("parallel",)),
    )(w.reshape(as2d), g.reshape(as2d), m.reshape(as2d), v.reshape(as2d))
    return tuple(o.reshape(shape) for o in outs)


def _pad_rows(a, rows):
    return jnp.pad(a, ((0, rows - a.shape[0]), (0, 0)))


def kernel(x, meta_tokens, norm1_gain, w_in, b_forget, ret_norm_gain, w_out, norm2_gain, w_up, conv_w, conv_b, w_down, final_norm_gain, loss_target, m_meta_tokens, m_norm1_gain, m_w_in, m_b_forget, m_ret_norm_gain, m_w_out, m_norm2_gain, m_w_up, m_conv_w, m_conv_b, m_w_down, m_final_norm_gain, v_meta_tokens, v_norm1_gain, v_w_in, v_b_forget, v_ret_norm_gain, v_w_out, v_norm2_gain, v_w_up, v_conv_w, v_conv_b, v_w_down, v_final_norm_gain):
    seq, d = x.shape[1], x.shape[2]
    l = CHUNK + seq
    d_ff = w_down.shape[1] * N_DEV
    up_shard = w_up.shape[2]
    assert 4 * up_shard == d_ff and w_in.shape[2] == WIN_SHARD and d == 2 * GROUP
    dev = _device_index()
    mx, my, mc = _my_position()
    core = jnp.reshape(mc, (1,)).astype(jnp.int32)
    chip = jnp.reshape(2 * mx + my, (1,)).astype(jnp.int32)
    dev1 = jnp.reshape(dev, (1,)).astype(jnp.int32)

    small = jnp.concatenate([meta_tokens.reshape(-1, 128), conv_w[0].reshape(-1, 128)], axis=0)
    n_meta_rows = N_META * (d // N_DEV) // 128
    small_rows = small.shape[0]
    small_all = _all_gather(_pad_rows(small, -(-small_rows // 8) * 8), "gather_small")
    meta_full = jnp.transpose(small_all[:, :n_meta_rows].reshape(N_DEV, N_META, d // N_DEV), (1, 0, 2)).reshape(N_META, d)
    conv_w_full = _pad_rows(jnp.transpose(small_all[:, n_meta_rows:small_rows].reshape(N_DEV, 3, up_shard),
                                          (1, 0, 2)).reshape(3, 2 * d_ff), 8)
    w_in_padded = jnp.pad(w_in[0], ((0, 0), (0, WIN_BLOCK - WIN_SHARD))).astype(WIRE_DTYPE)
    start_in = _gather_start(w_in_padded, dev1, small_all, "gather_w_in_start")

    h0 = jnp.concatenate([jnp.zeros((PAD_ROWS, d), F32), meta_full, x[0]], axis=0)
    consts = _retention_consts(l)
    bias_row = jnp.pad(b_forget, ((0, 0), (0, 128 - N_HEADS)))
    a = _rmsnorm_fwd(h0, norm1_gain + start_in[4][0, 0], "rmsnorm1")
    w_in_blocks = _gather_finish(start_in, a, "gather_w_in")
    start_out = _gather_start(w_out[0].astype(WIRE_DTYPE), dev1, w_in_blocks, "gather_w_out_start")
    w_in_full = _assemble_w_in(w_in_blocks).astype(MXU_DTYPE)
    proj = _mm_nn(a, w_in_full, F32, "mm_proj", after=start_out[4])
    w_out_blocks = _gather_finish(start_out, proj, "gather_w_out")
    start_up = _gather_start(w_up[0].astype(WIRE_DTYPE), dev1, w_out_blocks, "gather_w_up_start")
    ret_mix, ret_pre, ret_states = _retention_fwd(proj, ret_norm_gain + start_up[4][0, 0], consts)
    cum_bc, cum_rows = _fox_prep(proj, bias_row + start_up[4][0:1, :])
    mix = _fox_fwd(proj, cum_bc, cum_rows, ret_mix)
    w_out_full = w_out_blocks.reshape(d, d).astype(MXU_DTYPE)
    h1, cn = _rmsnorm_fwd(h0, norm2_gain, "resid_rmsnorm2", res=_mm_nn(mix, w_out_full, F32, "mm_out"))
    w_up_blocks = _gather_finish(start_up, cn, "gather_w_up").astype(MXU_DTYPE)
    start_down = _gather_start(w_down[0].astype(WIRE_DTYPE), dev1, w_up_blocks, "gather_w_down_start")
    u = _mm(cn, w_up_blocks,
            a_spec=pl.BlockSpec((_divisor_tile(l, 1088, 16), d), lambda i, j, k: (i, 0)),
            b_spec=pl.BlockSpec((None, d, up_shard), lambda i, j, k: (j, 0, 0)),
            o_spec=pl.BlockSpec((_divisor_tile(l, 1088, 16), up_shard), lambda i, j, k: (i, j)),
            out_shape=jax.ShapeDtypeStruct((l, 2 * d_ff), F32),
            grid=(l // _divisor_tile(l, 1088, 16), N_DEV, 1), contract=(1, 0), nk=1, name="mm_up",
            after=start_down[4])
    act = _conv_act_fwd(u, conv_w_full, conv_b + start_down[4][0, 0], d_ff)
    w_down_full = _gather_finish(start_down, act, "gather_w_down").reshape(d_ff, d).astype(MXU_DTYPE)
    mlp_out = _mm_nn(act, w_down_full, F32, "mm_down", tm_cap=544, tk_cap=d_ff)
    d_h2, d_h2_b, dg_final, loss_part = _loss_head(h1, mlp_out, final_norm_gain.reshape(1, d), loss_target[0])

    gw_down = _mm_tn(act, d_h2_b, WIRE_DTYPE, "mm_gw_down", tm_cap=1408, tn_cap=1024)
    rs_down = _reduce_scatter_start(gw_down.reshape(N_DEV, d_ff // N_DEV, d), core, "rs_w_down")
    d_act = _mm_nt(d_h2_b, w_down_full, F32, "mm_d_act", after=rs_down[4])
    d_u, d_conv = _conv_act_bwd(u, conv_w_full, conv_b + rs_down[4][0, 0], d_act, d_ff)
    tm = _divisor_tile(l, 1088, 16)
    gw_up = _mm(cn, d_u,
                a_spec=pl.BlockSpec((l, d // 2), lambda i, j, k: (0, i)),
                b_spec=pl.BlockSpec((None, l, up_shard), lambda i, j, k: (j // 4, 0, j % 4)),
                o_spec=pl.BlockSpec((None, d // 2, up_shard), lambda i, j, k: (j, i, 0)),
                out_shape=jax.ShapeDtypeStruct((N_DEV, d, up_shard), WIRE_DTYPE),
                grid=(2, N_DEV, 1), contract=(0, 0), nk=1, name="mm_gw_up")
    rs_up = _reduce_scatter_start(gw_up, core, "rs_w_up")
    d_cn = _mm_d_cn(d_u, w_up_blocks, rs_up[4])
    d_h1, d_h1_b, dg_norm2 = _rmsnorm_bwd(d_h2, d_cn, h1, norm2_gain + rs_up[4][0, 0], "rmsnorm2_bwd", True)

    gw_out = _mm_tn(mix, d_h1_b, WIRE_DTYPE, "mm_gw_out")
    rs_out = _reduce_scatter_start(gw_out.reshape(N_DEV, d // N_DEV, d), core, "rs_w_out")
    d_mix = _mm_nt(d_h1_b, w_out_full, F32, "mm_d_mix", after=rs_out[4])
    d_fq, d_fk, d_fv, ds_sum = _fox_bwd(proj, cum_bc, cum_rows, d_mix)
    d_ff_tile, db_forget_row = _fox_gate_bwd(ds_sum, proj, bias_row)
    d_ret, dg_ret = _retention_bwd(proj, ret_pre, ret_states, d_mix, ret_norm_gain + rs_out[4][0, 0], consts)
    d_proj = jnp.concatenate(
        [d_ret, d_fq, d_fk, d_fv, d_ff_tile, jnp.zeros((l, WIN_N - 7 * GROUP - 128), MXU_DTYPE)], axis=1)
    gw_in = _mm_tn(a, d_proj, WIRE_DTYPE, "mm_gw_in")
    rs_in = _reduce_scatter_start(_extract_w_in_windows(gw_in), core, "rs_w_in")
    d_a = _mm_nt(d_proj, w_in_full, F32, "mm_d_a", tm_cap=544, tn_cap=256, tk_cap=WIN_N, after=rs_in[4])
    d_h0, dg_norm1 = _rmsnorm_bwd(d_h1, d_a, h0, norm1_gain + rs_in[4][0, 0], "rmsnorm1_bwd", False)
    grad_x = d_h0[CHUNK:][None]
    d_meta = d_h0[PAD_ROWS:CHUNK]

    d_conv_w = jnp.concatenate([d_conv[0, 0:3], d_conv[1, 0:3]], axis=1)
    d_conv_b = jnp.concatenate([d_conv[0, 3:4], d_conv[1, 3:4]], axis=1)
    pieces = [loss_part[:, 0:1], dg_norm1, db_forget_row[:, 0:N_HEADS], dg_ret, dg_norm2, d_conv_b, dg_final,
              d_meta.reshape(1, -1), d_conv_w.reshape(1, -1)]
    sizes = [p.shape[1] for p in pieces]
    flat = jnp.concatenate(pieces, axis=1)
    padded = -(-flat.shape[1] // 1024) * 1024
    flat = jnp.pad(flat, ((0, 0), (0, padded - flat.shape[1]))).reshape(padded // 128, 128)
    small_ar = _small_all_reduce_start(flat, d_h0, "all_reduce_small")

    g_w_down = _reduce_scatter_finish(rs_down, small_ar[4], chip, "rs_w_down")[None]
    g_w_up = _reduce_scatter_finish(rs_up, g_w_down, chip, "rs_w_up")[None]
    g_w_out = _reduce_scatter_finish(rs_out, g_w_up, chip, "rs_w_out")[None]
    early = [_adamw(w, g, m, v, "adamw_" + n) for w, g, m, v, n in (
        (w_down, g_w_down, m_w_down, v_w_down, "w_down"), (w_up, g_w_up, m_w_up, v_w_up, "w_up"),
        (w_out, g_w_out, m_w_out, v_w_out, "w_out"))]
    g_w_in_window = _reduce_scatter_finish(rs_in, early[1][2], chip, "rs_w_in")
    g_w_in = g_w_in_window[:, :WIN_SHARD][None]
    early.append(_adamw(w_in, g_w_in, m_w_in, v_w_in, "adamw_w_in"))
    total = _small_all_reduce_finish(small_ar, early[3][2], dev1, "all_reduce_small").reshape(1, padded)
    offs = np.concatenate([[0], np.cumsum(sizes)])
    take = lambda k: total[:, int(offs[k]):int(offs[k + 1])]
    loss = take(0).reshape(())
    g_norm1, g_bf, g_ret_gain, g_norm2 = take(1), take(2), take(3), take(4)
    g_conv_b, g_final = take(5), take(6).reshape(d)
    g_meta = lax.dynamic_slice(take(7).reshape(N_META, d), (jnp.int32(0), (dev * (d // N_DEV)).astype(jnp.int32)),
                               (N_META, d // N_DEV))
    g_conv_w = lax.dynamic_slice(take(8).reshape(3, 2 * d_ff), (jnp.int32(0), (dev * up_shard).astype(jnp.int32)),
                                 (3, up_shard))[None]

    weights = [meta_tokens, norm1_gain, w_in, b_forget, ret_norm_gain, w_out, norm2_gain, w_up, conv_w, conv_b,
               w_down, final_norm_gain]
    grads = [g_meta, g_norm1, g_w_in, g_bf, g_ret_gain, g_w_out, g_norm2, g_w_up, g_conv_w, g_conv_b, g_w_down,
             g_final]
    done = {"w_down": early[0], "w_up": early[1], "w_out": early[2], "w_in": early[3]}
    ms = [m_meta_tokens, m_norm1_gain, m_w_in, m_b_forget, m_ret_norm_gain, m_w_out, m_norm2_gain, m_w_up, m_conv_w,
          m_conv_b, m_w_down, m_final_norm_gain]
    vs = [v_meta_tokens, v_norm1_gain, v_w_in, v_b_forget, v_ret_norm_gain, v_w_out, v_norm2_gain, v_w_up, v_conv_w,
          v_conv_b, v_w_down, v_final_norm_gain]
    names = ["meta", "norm1", "w_in", "b_forget", "ret_gain", "w_out", "norm2", "w_up", "conv_w", "conv_b", "w_down",
             "final_gain"]
    deltas, new_ms, new_vs = [], [], []
    for w, g, m, v, n in zip(weights, grads, ms, vs, names):
        dl, nm, nv = done[n] if n in done else _adamw(w, g, m, v, "adamw_" + n)
        deltas.append(dl)
        new_ms.append(nm)
        new_vs.append(nv)
    return (loss, grad_x, *grads, *deltas, *new_ms, *new_vs)
```

```python
import functools

import numpy as np
import jax
import jax.numpy as jnp
from jax import lax
from jax.experimental import pallas as pl
from jax.experimental.pallas import tpu as pltpu

F32 = jnp.float32
MXU_DTYPE = jnp.bfloat16
WIRE_DTYPE = jnp.bfloat16

N_DEV = 8
N_META = 16
CHUNK = 128
PAD_ROWS = CHUNK - N_META
N_HEADS = 8
HEAD_DIM = 128
GROUP = N_HEADS * HEAD_DIM
IN_DIM = 7 * GROUP + N_HEADS
WIN_SHARD = IN_DIM // N_DEV
WIN_BLOCK = 1024
WIN_STRIDE = 896
WIN_N = 7680
ROPE_BASE = 10000.0
NORM_EPS = 1e-6
NEG_BIG = -1e30
ADAM_LR, ADAM_B1, ADAM_B2, ADAM_EPS, ADAM_WD, ADAM_STEP = 0.001, 0.9, 0.999, 1e-08, 0.01, 10
VMEM_LIMIT = 52 * 1024 * 1024
MESH = pl.DeviceIdType.MESH
ANY = pl.BlockSpec(memory_space=pl.ANY)
VMEM_SPEC = pl.BlockSpec(memory_space=pltpu.VMEM)


def _params(sem=None):
    kw = {"vmem_limit_bytes": VMEM_LIMIT}
    if sem is not None:
        kw["dimension_semantics"] = sem
    return pltpu.CompilerParams(**kw)


def _divisor_tile(n, cap, unit):
    if n <= cap:
        return n
    best = None
    for t in range(unit, cap + 1, unit):
        if n % t == 0:
            best = t
    assert best is not None, (n, cap, unit)
    return best


def _my_position():
    return lax.axis_index("x"), lax.axis_index("y"), lax.axis_index("c")


def _device_index():
    x, y, c = _my_position()
    return 4 * x + 2 * y + c


def _all_gather(shard, name):
    r, c = shard.shape

    def body(x_ref, out_ref, send_sems, recv_sems, local_sem):
        mx, my, mc = _my_position()
        me, sibling = (mx, my, mc), (mx, my, 1 - mc)
        chips = [(1 - mx, my), (mx, 1 - my), (1 - mx, 1 - my)]

        def slot(px, py, pc):
            return out_ref.at[4 * px + 2 * py + pc]

        def copy(k, block, to, src=None):
            return pltpu.make_async_remote_copy(
                src_ref=slot(*block) if src is None else src, dst_ref=slot(*block),
                send_sem=send_sems.at[k], recv_sem=recv_sems.at[k], device_id=to, device_id_type=MESH)

        mine = pltpu.make_async_copy(x_ref, slot(*me), local_sem)
        mine.start()
        first = [copy(0, me, sibling, src=x_ref)]
        first += [copy(1 + j, me, (*chip, mc), src=x_ref) for j, chip in enumerate(chips)]
        for cp in first:
            cp.start()
        passed = [copy(4 + j, (*chip, mc), sibling) for j, chip in enumerate(chips)]
        for j, chip in enumerate(chips):
            copy(1 + j, (*chip, mc), me).wait_recv()
            passed[j].start()
        copy(0, sibling, me).wait_recv()
        for j, chip in enumerate(chips):
            copy(4 + j, (*chip, 1 - mc), me).wait_recv()
        for cp in first + passed:
            cp.wait_send()
        mine.wait()

    return pl.pallas_call(
        body, name=name,
        out_shape=jax.ShapeDtypeStruct((N_DEV, r, c), shard.dtype),
        in_specs=[ANY], out_specs=ANY,
        scratch_shapes=[pltpu.SemaphoreType.DMA((7,)), pltpu.SemaphoreType.DMA((7,)), pltpu.SemaphoreType.DMA],
    )(shard)


HBM_SPEC = pl.BlockSpec(memory_space=pltpu.HBM)
SEM_SPEC = pl.BlockSpec(memory_space=pltpu.SEMAPHORE)
DATAFLOW_EFFECT = pltpu.SideEffectType.DATAFLOW_SIDE_EFFECTING


def _in_hbm(a):
    return pltpu.with_memory_space_constraint(a, pltpu.HBM)


def _split_start(src, land, make_copies, n_copies, after, name):
    if isinstance(land, tuple):
        land = lax.empty(land, src.dtype)
    land_shape = land.shape
    def body(src_ref, land_ref, after_ref, send_sems, recv_sems, src_thru, land_thru, token):
        for cp in make_copies(src_ref, land_ref, send_sems, recv_sems):
            cp.start()
        token[...] = jnp.zeros_like(token)

    return pl.pallas_call(
        body, name=name,
        out_shape=(pltpu.SemaphoreType.DMA((n_copies,)), pltpu.SemaphoreType.DMA((n_copies,)),
                   pltpu.HBM(src.shape, src.dtype), pltpu.HBM(land_shape, src.dtype),
                   jax.ShapeDtypeStruct((8, 128), F32)),
        in_specs=(HBM_SPEC, HBM_SPEC, ANY), out_specs=(SEM_SPEC, SEM_SPEC, HBM_SPEC, HBM_SPEC, VMEM_SPEC),
        input_output_aliases={0: 2, 1: 3},
        compiler_params=pltpu.CompilerParams(has_side_effects=DATAFLOW_EFFECT),
    )(_in_hbm(src), _in_hbm(land), after)


def _split_wait(started, after, make_copies, name):
    send_sems, recv_sems, src_thru, land_thru, _ = started

    def body(src_ref, land_ref, send_sems_ref, recv_sems_ref, after_ref, src_dead, land_out):
        for cp in make_copies(src_ref, land_ref, send_sems_ref, recv_sems_ref):
            cp.wait_send()
            cp.wait_recv()

    return pl.pallas_call(
        body, name=name,
        out_shape=(pltpu.HBM(src_thru.shape, src_thru.dtype), pltpu.HBM(land_thru.shape, land_thru.dtype)),
        in_specs=(HBM_SPEC, HBM_SPEC, SEM_SPEC, SEM_SPEC, ANY), out_specs=(HBM_SPEC, HBM_SPEC),
        input_output_aliases={0: 0, 1: 1},
        compiler_params=pltpu.CompilerParams(has_side_effects=DATAFLOW_EFFECT),
    )(src_thru, land_thru, send_sems, recv_sems, after)


def _gather_copies(x_ref, land_ref, send_sems, recv_sems):
    mx, my, mc = _my_position()
    me = 4 * mx + 2 * my + mc
    targets = [(mx, my, 1 - mc), (1 - mx, my, mc), (mx, 1 - my, mc), (1 - mx, 1 - my, mc)]
    return [pltpu.make_async_remote_copy(
        src_ref=x_ref, dst_ref=land_ref.at[me], send_sem=send_sems.at[k], recv_sem=recv_sems.at[k],
        device_id=t, device_id_type=MESH) for k, t in enumerate(targets)]


def _gather_start(shard, dev, after, name):
    r, c = shard.shape
    tr = _divisor_tile(r, 512, 16)

    def body(s_ref, x_ref, o_ref):
        o_ref[...] = x_ref[...]

    land = pl.pallas_call(
        body, name=name + "_own",
        out_shape=jax.ShapeDtypeStruct((N_DEV, r, c), shard.dtype),
        grid_spec=pltpu.PrefetchScalarGridSpec(
            num_scalar_prefetch=1, grid=(r // tr,),
            in_specs=[pl.BlockSpec((tr, c), lambda i, s: (i, 0))],
            out_specs=pl.BlockSpec((None, tr, c), lambda i, s: (s[0], i, 0))),
        compiler_params=_params(("parallel",)),
    )(dev, shard)
    return _split_start(shard, land, _gather_copies, 4, after, name)


def _gather_finish(started, after, name):
    _, land = _split_wait(started, after, _gather_copies, name + "_wait")

    def body(land_in, land_ref, send_sems, recv_sems):
        mx, my, mc = _my_position()
        chips = [(1 - mx, my), (mx, 1 - my), (1 - mx, 1 - my)]
        copies = [pltpu.make_async_remote_copy(
            src_ref=land_ref.at[4 * cx + 2 * cy + mc], dst_ref=land_ref.at[4 * cx + 2 * cy + mc],
            send_sem=send_sems.at[j], recv_sem=recv_sems.at[j],
            device_id=(mx, my, 1 - mc), device_id_type=MESH) for j, (cx, cy) in enumerate(chips)]
        for cp in copies:
            cp.start()
        for j, (cx, cy) in enumerate(chips):
            copies[j].wait_send()
            pltpu.make_async_remote_copy(
                src_ref=land_ref.at[4 * cx + 2 * cy + 1 - mc], dst_ref=land_ref.at[4 * cx + 2 * cy + 1 - mc],
                send_sem=send_sems.at[j], recv_sem=recv_sems.at[j],
                device_id=(mx, my, 1 - mc), device_id_type=MESH).wait_recv()

    return pl.pallas_call(
        body, name=name + "_pass",
        out_shape=jax.ShapeDtypeStruct(land.shape, land.dtype),
        in_specs=[ANY], out_specs=ANY,
        input_output_aliases={0: 0},
        scratch_shapes=[pltpu.SemaphoreType.DMA((3,)), pltpu.SemaphoreType.DMA((3,))],
    )(land)


def _chip_copies(p_ref, land_ref, send_sems, recv_sems):
    mx, my, mc = _my_position()
    chips = [(1 - mx, my), (mx, 1 - my), (1 - mx, 1 - my)]
    return [pltpu.make_async_remote_copy(
        src_ref=p_ref.at[2 * cx + cy], dst_ref=land_ref.at[j], send_sem=send_sems.at[j], recv_sem=recv_sems.at[j],
        device_id=(cx, cy, mc), device_id_type=MESH) for j, (cx, cy) in enumerate(chips)]


def _reduce_scatter_start(g, core, name):
    pair = _pair_sum(g, _exchange_sibling(g, name + "_d2d"), core, name + "_pairsum")
    return _split_start(pair, (3,) + pair.shape[1:], _chip_copies, 3, g, name + "_ici_start")


def _reduce_scatter_finish(started, after, chip, name):
    pair, from_chips = _split_wait(started, after, _chip_copies, name + "_ici_wait")
    return _final_sum(pair, from_chips, chip, name + "_sum")


def _exchange_sibling(g, name):
    _, r, c = g.shape

    def body(g_ref, out_ref, send_sems, recv_sems):
        mx, my, mc = _my_position()
        copies = [
            pltpu.make_async_remote_copy(
                src_ref=g_ref.at[2 * k + (1 - mc)], dst_ref=out_ref.at[k],
                send_sem=send_sems.at[k], recv_sem=recv_sems.at[k],
                device_id=(mx, my, 1 - mc), device_id_type=MESH)
            for k in range(4)]
        for cp in copies:
            cp.start()
        for cp in copies:
            cp.wait()

    return pl.pallas_call(
        body, name=name,
        out_shape=jax.ShapeDtypeStruct((4, r, c), g.dtype),
        in_specs=[ANY], out_specs=ANY,
        scratch_shapes=[pltpu.SemaphoreType.DMA((4,)), pltpu.SemaphoreType.DMA((4,))],
    )(g)


def _pair_sum(g, recv, core, name):
    _, r, c = g.shape
    tr = _divisor_tile(r, 512, 16)

    def body(s_ref, g_ref, r_ref, o_ref):
        o_ref[...] = (g_ref[...].astype(F32) + r_ref[...].astype(F32)).astype(o_ref.dtype)

    return pl.pallas_call(
        body, name=name,
        out_shape=jax.ShapeDtypeStruct((4, r, c), g.dtype),
        grid_spec=pltpu.PrefetchScalarGridSpec(
            num_scalar_prefetch=1, grid=(4, r // tr),
            in_specs=[pl.BlockSpec((None, tr, c), lambda k, i, s: (2 * k + s[0], i, 0)),
                      pl.BlockSpec((None, tr, c), lambda k, i, s: (k, i, 0))],
            out_specs=pl.BlockSpec((None, tr, c), lambda k, i, s: (k, i, 0))),
        compiler_params=_params(("parallel", "parallel")),
    )(core, g, recv)


def _final_sum(p, recv, chip, name):
    _, r, c = p.shape
    tr = _divisor_tile(r, 512, 16)

    def body(s_ref, p_ref, r_ref, o_ref):
        acc = p_ref[...].astype(F32)
        for j in range(3):
            acc = acc + r_ref[j].astype(F32)
        o_ref[...] = acc

    return pl.pallas_call(
        body, name=name,
        out_shape=jax.ShapeDtypeStruct((r, c), F32),
        grid_spec=pltpu.PrefetchScalarGridSpec(
            num_scalar_prefetch=1, grid=(r // tr,),
            in_specs=[pl.BlockSpec((None, tr, c), lambda i, s: (s[0], i, 0)),
                      pl.BlockSpec((3, tr, c), lambda i, s: (0, i, 0))],
            out_specs=pl.BlockSpec((tr, c), lambda i, s: (i, 0))),
        compiler_params=_params(("parallel",)),
    )(chip, p, recv)


def _all_to_all_copies(v_ref, land_ref, send_sems, recv_sems):
    mx, my, mc = _my_position()
    me = 4 * mx + 2 * my + mc
    copies = []
    for rel in range(1, N_DEV):
        bx, by, bc = (rel >> 2) & 1, (rel >> 1) & 1, rel & 1
        target = (1 - mx if bx else mx, 1 - my if by else my, 1 - mc if bc else mc)
        copies.append(pltpu.make_async_remote_copy(
            src_ref=v_ref, dst_ref=land_ref.at[me], send_sem=send_sems.at[rel - 1], recv_sem=recv_sems.at[rel - 1],
            device_id=target, device_id_type=MESH))
    return copies


def _small_all_reduce_start(v, after, name):
    return _split_start(v, (N_DEV,) + v.shape, _all_to_all_copies, N_DEV - 1, after, name + "_start")


def _small_all_reduce_finish(started, after, dev, name):
    v, land = _split_wait(started, after, _all_to_all_copies, name + "_wait")
    rows = v.shape[0]

    def body(me_ref, v_ref, land_ref, o_ref):
        for j in range(N_DEV):
            @pl.when(me_ref[0] == j)
            def _():
                o_ref[...] = v_ref[...] if j == 0 else o_ref[...] + v_ref[...]

            @pl.when(me_ref[0] != j)
            def _():
                o_ref[...] = land_ref[j] if j == 0 else o_ref[...] + land_ref[j]

    return pl.pallas_call(
        body, name=name + "_sum",
        out_shape=jax.ShapeDtypeStruct((rows, 128), F32),
        grid_spec=pltpu.PrefetchScalarGridSpec(
            num_scalar_prefetch=1, grid=(1,),
            in_specs=[pl.BlockSpec((rows, 128), lambda i, s: (0, 0)),
                      pl.BlockSpec((N_DEV, rows, 128), lambda i, s: (0, 0, 0))],
            out_specs=pl.BlockSpec((rows, 128), lambda i, s: (0, 0))),
        compiler_params=_params(("arbitrary",)),
    )(dev, v, land)


def _assemble_w_in(blocks):
    _, d, _ = blocks.shape
    tr = _divisor_tile(d, 128, 16)
    n_tiles = WIN_N // 128
    last = (N_DEV * WIN_STRIDE) // 128

    def body(b_ref, o_ref):
        win = []
        for i in range(N_DEV):
            w = b_ref[i].astype(F32)
            win.append(pltpu.roll(w, i, 1) if i else w)
        for t in range(n_tiles):
            if t > last:
                o_ref[:, t * 128:(t + 1) * 128] = jnp.zeros((tr, 128), o_ref.dtype)
                continue
            i = min(t // 7, N_DEV - 1)
            k = t - 7 * i
            val = win[i][:, k * 128:(k + 1) * 128]
            if k == 0 and i >= 1:
                val = val + win[i - 1][:, 7 * 128:8 * 128]
            o_ref[:, t * 128:(t + 1) * 128] = val.astype(o_ref.dtype)

    return pl.pallas_call(
        body, name="assemble_w_in",
        out_shape=jax.ShapeDtypeStruct((d, WIN_N), blocks.dtype),
        grid=(d // tr,),
        in_specs=[pl.BlockSpec((N_DEV, tr, WIN_BLOCK), lambda i: (0, i, 0))],
        out_specs=pl.BlockSpec((tr, WIN_N), lambda i: (i, 0)),
        compiler_params=_params(("parallel",)),
    )(blocks)


def _extract_w_in_windows(g):
    d, _ = g.shape
    tr = _divisor_tile(d, 128, 16)

    def body(g_ref, o_ref):
        for j in range(N_DEV):
            w = g_ref[:, WIN_STRIDE * j:WIN_STRIDE * j + WIN_BLOCK].astype(F32)
            o_ref[j] = (pltpu.roll(w, WIN_BLOCK - j, 1) if j else w).astype(o_ref.dtype)

    return pl.pallas_call(
        body, name="extract_w_in_windows",
        out_shape=jax.ShapeDtypeStruct((N_DEV, d, WIN_BLOCK), g.dtype),
        grid=(d // tr,),
        in_specs=[pl.BlockSpec((tr, WIN_N), lambda i: (i, 0))],
        out_specs=pl.BlockSpec((N_DEV, tr, WIN_BLOCK), lambda i: (0, i, 0)),
        compiler_params=_params(("parallel",)),
    )(g)


def _mm(a, b, *, a_spec, b_spec, o_spec, out_shape, grid, contract, nk, name, after=None):
    dn = (((contract[0],), (contract[1],)), ((), ()))
    tm, tn = o_spec.block_shape[-2:]
    behind = [] if after is None else [after]

    def body(a_ref, b_ref, *rest):
        o_ref, *scratch = rest[len(behind):]
        part = lax.dot_general(a_ref[...], b_ref[...], dn, preferred_element_type=F32)
        if nk == 1:
            o_ref[...] = part.astype(o_ref.dtype)
            return
        acc = scratch[0]
        k = pl.program_id(2)

        @pl.when(k == 0)
        def _():
            acc[...] = part

        @pl.when(k > 0)
        def _():
            acc[...] += part

        @pl.when(k == nk - 1)
        def _():
            o_ref[...] = acc[...].astype(o_ref.dtype)

    return pl.pallas_call(
        body, name=name, out_shape=out_shape, grid=grid,
        in_specs=[a_spec, b_spec] + [ANY] * len(behind), out_specs=o_spec,
        scratch_shapes=[] if nk == 1 else [pltpu.VMEM((tm, tn), F32)],
        compiler_params=_params(("parallel", "parallel", "arbitrary")),
    )(a, b, *behind)


def _mm_nn(a, b, out_dtype, name, tm_cap=1088, tn_cap=512, tk_cap=2048, after=None):
    m, k = a.shape
    _, n = b.shape
    tm, tn, tk = _divisor_tile(m, tm_cap, 16), _divisor_tile(n, tn_cap, 128), _divisor_tile(k, tk_cap, 128)
    return _mm(a, b,
               a_spec=pl.BlockSpec((tm, tk), lambda i, j, kk: (i, kk)),
               b_spec=pl.BlockSpec((tk, tn), lambda i, j, kk: (kk, j)),
               o_spec=pl.BlockSpec((tm, tn), lambda i, j, kk: (i, j)),
               out_shape=jax.ShapeDtypeStruct((m, n), out_dtype),
               grid=(m // tm, n // tn, k // tk), contract=(1, 0), nk=k // tk, name=name, after=after)


def _mm_nt(a, b, out_dtype, name, tm_cap=1088, tn_cap=512, tk_cap=2048, after=None):
    m, k = a.shape
    n, _ = b.shape
    tm, tn, tk = _divisor_tile(m, tm_cap, 16), _divisor_tile(n, tn_cap, 128), _divisor_tile(k, tk_cap, 128)
    return _mm(a, b,
               a_spec=pl.BlockSpec((tm, tk), lambda i, j, kk: (i, kk)),
               b_spec=pl.BlockSpec((tn, tk), lambda i, j, kk: (j, kk)),
               o_spec=pl.BlockSpec((tm, tn), lambda i, j, kk: (i, j)),
               out_shape=jax.ShapeDtypeStruct((m, n), out_dtype),
               grid=(m // tm, n // tn, k // tk), contract=(1, 1), nk=k // tk, name=name, after=after)


def _mm_tn(a, b, out_dtype, name, tm_cap=1024, tn_cap=512, after=None):
    l, m = a.shape
    _, n = b.shape
    tm, tn = _divisor_tile(m, tm_cap, 128), _divisor_tile(n, tn_cap, 128)
    return _mm(a, b,
               a_spec=pl.BlockSpec((l, tm), lambda i, j, kk: (0, i)),
               b_spec=pl.BlockSpec((l, tn), lambda i, j, kk: (0, j)),
               o_spec=pl.BlockSpec((tm, tn), lambda i, j, kk: (i, j)),
               out_shape=jax.ShapeDtypeStruct((m, n), out_dtype),
               grid=(m // tm, n // tn, 1), contract=(0, 0), nk=1, name=name, after=after)


def _mm_d_cn(d_u, w_up_blocks, after):
    _, l, d_ff = d_u.shape
    n, d, shard = w_up_blocks.shape
    per = d_ff // shard
    tm, tn = _divisor_tile(l, 544, 16), _divisor_tile(d, 256, 128)

    def body(a_ref, b_ref, after_ref, o_ref):
        acc = None
        for k in range(n):
            part = _dot_nt(a_ref[k // per, :, (k % per) * shard:(k % per + 1) * shard], b_ref[k])
            acc = part if acc is None else acc + part
        o_ref[...] = acc

    return pl.pallas_call(
        body, name="mm_d_cn", out_shape=jax.ShapeDtypeStruct((l, d), F32), grid=(l // tm, d // tn),
        in_specs=[pl.BlockSpec((2, tm, d_ff), lambda i, j: (0, i, 0)),
                  pl.BlockSpec((n, tn, shard), lambda i, j: (0, j, 0)), ANY],
        out_specs=pl.BlockSpec((tm, tn), lambda i, j: (i, j)),
        compiler_params=_params(("parallel", "parallel")),
    )(d_u, w_up_blocks, after)


def _row_tile(l):
    return _divisor_tile(l, 544, 8)


def _rmsnorm_fwd(h, gain, name, res=None):
    l, d = h.shape
    tr = _row_tile(l)
    row = pl.BlockSpec((tr, d), lambda i: (i, 0))
    vec = pl.BlockSpec((1, d), lambda i: (0, 0))

    def body(*refs):
        if res is None:
            h_ref, g_ref, n_ref = refs
            x = h_ref[...]
        else:
            h_ref, r_ref, g_ref, s_ref, n_ref = refs
            x = h_ref[...] + r_ref[...]
            s_ref[...] = x
        y = x * lax.rsqrt(jnp.mean(x * x, axis=-1, keepdims=True) + NORM_EPS)
        n_ref[...] = (y * g_ref[...]).astype(n_ref.dtype)

    normed = jax.ShapeDtypeStruct((l, d), MXU_DTYPE)
    if res is None:
        return pl.pallas_call(body, name=name, out_shape=normed, grid=(l // tr,), in_specs=[row, vec],
                              out_specs=row, compiler_params=_params(("parallel",)))(h, gain)
    return pl.pallas_call(body, name=name, out_shape=(jax.ShapeDtypeStruct((l, d), F32), normed),
                          grid=(l // tr,), in_specs=[row, row, vec], out_specs=(row, row),
                          compiler_params=_params(("parallel",)))(h, res, gain)


def _rmsnorm_bwd(d_res, d_normed, x, gain, name, with_mxu_copy):
    l, d = x.shape
    tr = _row_tile(l)
    row = pl.BlockSpec((tr, d), lambda i: (i, 0))
    vec = pl.BlockSpec((1, d), lambda i: (0, 0))

    def body(dres_ref, dn_ref, x_ref, g_ref, dx_ref, *rest):
        dg_ref = rest[-1]
        xv = x_ref[...]
        r = lax.rsqrt(jnp.mean(xv * xv, axis=-1, keepdims=True) + NORM_EPS)
        xh = xv * r
        dn = dn_ref[...]
        dxh = dn * g_ref[...]
        dx = dres_ref[...] + r * (dxh - xh * jnp.mean(dxh * xh, axis=-1, keepdims=True))
        dx_ref[...] = dx
        if with_mxu_copy:
            rest[0][...] = dx.astype(MXU_DTYPE)

        @pl.when(pl.program_id(0) == 0)
        def _():
            dg_ref[...] = jnp.zeros_like(dg_ref)

        dg_ref[...] += jnp.sum(dn * xh, axis=0, keepdims=True)

    outs = [jax.ShapeDtypeStruct((l, d), F32)]
    specs = [row]
    if with_mxu_copy:
        outs.append(jax.ShapeDtypeStruct((l, d), MXU_DTYPE))
        specs.append(row)
    outs.append(jax.ShapeDtypeStruct((1, d), F32))
    specs.append(vec)
    return pl.pallas_call(body, name=name, out_shape=tuple(outs), grid=(l // tr,),
                          in_specs=[row, row, row, vec], out_specs=tuple(specs),
                          compiler_params=_params(("arbitrary",)))(d_res, d_normed, x, gain)


def _loss_head(h1, mlp_out, gain, target):
    l, d = h1.shape
    n_blocks = l // CHUNK
    row = pl.BlockSpec((CHUNK, d), lambda i: (i, 0))
    vec = pl.BlockSpec((1, d), lambda i: (0, 0))
    tgt = pl.BlockSpec((CHUNK, d), lambda i: (jnp.maximum(i - 1, 0), 0))

    def body(h_ref, m_ref, g_ref, t_ref, dh_ref, dhb_ref, dg_ref, loss_ref, sq_ref):
        i = pl.program_id(0)
        x = h_ref[...] + m_ref[...]
        r = lax.rsqrt(jnp.mean(x * x, axis=-1, keepdims=True) + NORM_EPS)
        xh = x * r
        g = g_ref[...]
        real = i >= 1
        err = jnp.where(real, xh * g - t_ref[...], 0.0)
        dy = err * (1.0 / d)
        dxh = dy * g
        dh = r * (dxh - xh * jnp.mean(dxh * xh, axis=-1, keepdims=True))
        dh_ref[...] = dh
        dhb_ref[...] = dh.astype(MXU_DTYPE)

        @pl.when(i == 0)
        def _():
            dg_ref[...] = jnp.zeros_like(dg_ref)
            sq_ref[...] = jnp.zeros_like(sq_ref)

        dg_ref[...] += jnp.sum(dy * xh, axis=0, keepdims=True)
        sq_ref[...] += jnp.sum(err * err, axis=0, keepdims=True)

        @pl.when(i == n_blocks - 1)
        def _():
            total = jnp.sum(sq_ref[...], axis=-1, keepdims=True) * (0.5 / d)
            loss_ref[...] = jnp.broadcast_to(total, (1, 128))

    return pl.pallas_call(
        body, name="loss_head",
        out_shape=(jax.ShapeDtypeStruct((l, d), F32), jax.ShapeDtypeStruct((l, d), MXU_DTYPE),
                   jax.ShapeDtypeStruct((1, d), F32), jax.ShapeDtypeStruct((1, 128), F32)),
        grid=(n_blocks,), in_specs=[row, row, vec, tgt],
        out_specs=(row, row, vec, pl.BlockSpec((1, 128), lambda i: (0, 0))),
        scratch_shapes=[pltpu.VMEM((1, d), F32)],
        compiler_params=_params(("arbitrary",)),
    )(h1, mlp_out, gain, target)


def _dot(a, b):
    return jnp.dot(a, b, preferred_element_type=F32)


def _dot_nt(a, b):
    return lax.dot_general(a, b, (((1,), (1,)), ((), ())), preferred_element_type=F32)


def _dot_tn(a, b):
    return lax.dot_general(a, b, (((0,), (0,)), ((), ())), preferred_element_type=F32)


def _rope(t, cos2, sin2):
    return t * cos2 + pltpu.roll(t, HEAD_DIM // 2, 1) * sin2


def _rope_bwd(dr, cos2, sin2):
    return dr * cos2 + pltpu.roll(dr * sin2, HEAD_DIM // 2, 1)


def _sigmoid(x):
    return 1.0 / (1.0 + jnp.exp(-x))


def _row_valid(block, rows):
    r = block * CHUNK + lax.broadcasted_iota(jnp.int32, (rows, 1), 0)
    return r >= PAD_ROWS


def _retention_consts(l):
    pos = jnp.arange(l, dtype=F32) - PAD_ROWS
    inv_freq = 1.0 / (ROPE_BASE ** (jnp.arange(0, HEAD_DIM, 2, dtype=F32) / HEAD_DIM))
    ang = pos[:, None] * inv_freq[None, :]
    cos, sin = jnp.cos(ang), jnp.sin(ang)
    cos2 = jnp.concatenate([cos, cos], axis=-1)
    sin2 = jnp.concatenate([-sin, sin], axis=-1)
    log_g = jnp.log1p(-jnp.exp2(-5.0 - jnp.arange(N_HEADS, dtype=F32)))
    idx = jnp.arange(CHUNK, dtype=F32)
    diff = idx[:, None] - idx[None, :]
    decay = jnp.where(diff >= 0, jnp.exp(jnp.maximum(diff, 0.0)[None] * log_g[:, None, None]), 0.0)
    xi = jnp.exp((idx + 1.0)[None, :] * log_g[:, None])
    zeta = jnp.exp((CHUNK - 1.0 - idx)[None, :] * log_g[:, None])
    g_chunk = jnp.exp(CHUNK * log_g)
    bcast = lambda v: jnp.broadcast_to(v[:, :, None], (N_HEADS, CHUNK, HEAD_DIM))
    g_rows = jnp.broadcast_to(g_chunk[:, None, None], (N_HEADS, 8, HEAD_DIM))
    return cos2, sin2, decay, bcast(xi), bcast(zeta), g_rows


def _retention_fwd(proj, ret_gain, consts):
    l = proj.shape[0]
    n_chunks = l // CHUNK
    cos2, sin2, decay, xi, zeta, g_rows = consts
    scale = HEAD_DIM ** -0.5

    def body(p_ref, cos_ref, sin_ref, dec_ref, xi_ref, zeta_ref, gr_ref, gain_ref,
             mix_ref, o_ref, st_ref, state):
        c = pl.program_id(0)

        @pl.when(c == 0)
        def _():
            state[...] = jnp.zeros_like(state)

        cos_v, sin_v = cos_ref[...], sin_ref[...]
        valid = _row_valid(c, CHUNK)
        for h in range(N_HEADS):
            cols = slice(h * HEAD_DIM, (h + 1) * HEAD_DIM)
            q = p_ref[:, h * HEAD_DIM:(h + 1) * HEAD_DIM]
            k = p_ref[:, GROUP + h * HEAD_DIM:GROUP + (h + 1) * HEAD_DIM]
            v = p_ref[:, 2 * GROUP + h * HEAD_DIM:2 * GROUP + (h + 1) * HEAD_DIM]
            g = p_ref[:, 3 * GROUP + h * HEAD_DIM:3 * GROUP + (h + 1) * HEAD_DIM]
            rq = _rope(q, cos_v, sin_v).astype(MXU_DTYPE)
            rk = _rope(k, cos_v, sin_v) * scale
            rkb = rk.astype(MXU_DTYPE)
            vb = v.astype(MXU_DTYPE)
            st = state[h]
            st_ref[h] = st
            s = _dot_nt(rq, rkb) * dec_ref[h]
            o = _dot(s.astype(MXU_DTYPE), vb) + _dot(rq, st.astype(MXU_DTYPE)) * xi_ref[h]
            kz = (rk * zeta_ref[h]).astype(MXU_DTYPE)
            state[h] = gr_ref[h, 0:1, :] * st + _dot_tn(kz, vb)
            o_ref[:, cols] = o
            mu = jnp.mean(o, axis=-1, keepdims=True)
            oc = o - mu
            yn = oc * lax.rsqrt(jnp.mean(oc * oc, axis=-1, keepdims=True) + NORM_EPS)
            ret = (g * _sigmoid(g)) * (yn * gain_ref[:, cols])
            mix_ref[:, cols] = jnp.where(valid, ret, 0.0).astype(mix_ref.dtype)

    head_tab = pl.BlockSpec((N_HEADS, CHUNK, HEAD_DIM), lambda c: (0, 0, 0))
    return pl.pallas_call(
        body, name="retention_fwd",
        out_shape=(jax.ShapeDtypeStruct((l, 2 * GROUP), MXU_DTYPE), jax.ShapeDtypeStruct((l, GROUP), F32),
                   jax.ShapeDtypeStruct((n_chunks, N_HEADS, HEAD_DIM, HEAD_DIM), F32)),
        grid=(n_chunks,),
        in_specs=[pl.BlockSpec((CHUNK, 4 * GROUP), lambda c: (c, 0)),
                  pl.BlockSpec((CHUNK, HEAD_DIM), lambda c: (c, 0)),
                  pl.BlockSpec((CHUNK, HEAD_DIM), lambda c: (c, 0)),
                  head_tab, head_tab, head_tab,
                  pl.BlockSpec((N_HEADS, 8, HEAD_DIM), lambda c: (0, 0, 0)),
                  pl.BlockSpec((1, GROUP), lambda c: (0, 0))],
        out_specs=(pl.BlockSpec((CHUNK, GROUP), lambda c: (c, 0)),
                   pl.BlockSpec((CHUNK, GROUP), lambda c: (c, 0)),
                   pl.BlockSpec((None, N_HEADS, HEAD_DIM, HEAD_DIM), lambda c: (c, 0, 0, 0))),
        scratch_shapes=[pltpu.VMEM((N_HEADS, HEAD_DIM, HEAD_DIM), F32)],
        compiler_params=_params(("arbitrary",)),
    )(proj, cos2, sin2, decay, xi, zeta, g_rows, ret_gain)


def _retention_bwd(proj, o_pre, states, d_mix, ret_gain, consts):
    l = proj.shape[0]
    n_chunks = l // CHUNK
    cos2, sin2, decay, xi, zeta, g_rows = consts
    scale = HEAD_DIM ** -0.5
    rev = lambda c: n_chunks - 1 - c

    def body(p_ref, o_ref, st_ref, dm_ref, cos_ref, sin_ref, dec_ref, xi_ref, zeta_ref, gr_ref, gain_ref,
             dp_ref, dgain_ref, dstate):
        step = pl.program_id(0)

        @pl.when(step == 0)
        def _():
            dstate[...] = jnp.zeros_like(dstate)
            dgain_ref[...] = jnp.zeros_like(dgain_ref)

        cos_v, sin_v = cos_ref[...], sin_ref[...]
        valid = _row_valid(rev(step), CHUNK)
        for h in range(N_HEADS):
            cols = slice(h * HEAD_DIM, (h + 1) * HEAD_DIM)
            q = p_ref[:, h * HEAD_DIM:(h + 1) * HEAD_DIM]
            k = p_ref[:, GROUP + h * HEAD_DIM:GROUP + (h + 1) * HEAD_DIM]
            v = p_ref[:, 2 * GROUP + h * HEAD_DIM:2 * GROUP + (h + 1) * HEAD_DIM]
            g = p_ref[:, 3 * GROUP + h * HEAD_DIM:3 * GROUP + (h + 1) * HEAD_DIM]
            o = o_ref[:, cols]
            gain = gain_ref[:, cols]
            d_ret = jnp.where(valid, dm_ref[:, cols], 0.0)
            mu = jnp.mean(o, axis=-1, keepdims=True)
            oc = o - mu
            rstd = lax.rsqrt(jnp.mean(oc * oc, axis=-1, keepdims=True) + NORM_EPS)
            yn = oc * rstd
            sig = _sigmoid(g)
            gate = g * sig
            dgain_ref[:, cols] += jnp.sum(d_ret * gate * yn, axis=0, keepdims=True)
            d_g = d_ret * (yn * gain) * (sig * (1.0 + g * (1.0 - sig)))
            d_yn = d_ret * gate * gain
            d_o = rstd * (d_yn - jnp.mean(d_yn, axis=-1, keepdims=True)
                          - yn * jnp.mean(d_yn * yn, axis=-1, keepdims=True))
            rq = _rope(q, cos_v, sin_v)
            rk = _rope(k, cos_v, sin_v) * scale
            rqb, rkb, vb = rq.astype(MXU_DTYPE), rk.astype(MXU_DTYPE), v.astype(MXU_DTYPE)
            dob = d_o.astype(MXU_DTYPE)
            dec = dec_ref[h]
            xi_h, zeta_h = xi_ref[h], zeta_ref[h]
            st_b = st_ref[h].astype(MXU_DTYPE)
            dst = dstate[h]
            dst_b = dst.astype(MXU_DTYPE)
            s_b = (_dot_nt(rqb, rkb) * dec).astype(MXU_DTYPE)
            da_b = (_dot_nt(dob, vb) * dec).astype(MXU_DTYPE)
            doxi_b = (d_o * xi_h).astype(MXU_DTYPE)
            kz_b = (rk * zeta_h).astype(MXU_DTYPE)
            d_rq = _dot(da_b, rkb) + _dot_nt(doxi_b, st_b)
            d_rk = _dot_tn(da_b, rqb) + _dot_nt(vb, dst_b) * zeta_h
            d_v = _dot_tn(s_b, dob) + _dot(kz_b, dst_b)
            dstate[h] = gr_ref[h, 0:1, :] * dst + _dot_tn(rqb, doxi_b)
            d_q = _rope_bwd(d_rq, cos_v, sin_v)
            d_k = _rope_bwd(d_rk * scale, cos_v, sin_v)
            dp_ref[:, h * HEAD_DIM:(h + 1) * HEAD_DIM] = d_q.astype(dp_ref.dtype)
            dp_ref[:, GROUP + h * HEAD_DIM:GROUP + (h + 1) * HEAD_DIM] = d_k.astype(dp_ref.dtype)
            dp_ref[:, 2 * GROUP + h * HEAD_DIM:2 * GROUP + (h + 1) * HEAD_DIM] = d_v.astype(dp_ref.dtype)
            dp_ref[:, 3 * GROUP + h * HEAD_DIM:3 * GROUP + (h + 1) * HEAD_DIM] = d_g.astype(dp_ref.dtype)

    head_tab = pl.BlockSpec((N_HEADS, CHUNK, HEAD_DIM), lambda c: (0, 0, 0))
    return pl.pallas_call(
        body, name="retention_bwd",
        out_shape=(jax.ShapeDtypeStruct((l, 4 * GROUP), MXU_DTYPE), jax.ShapeDtypeStruct((1, GROUP), F32)),
        grid=(n_chunks,),
        in_specs=[pl.BlockSpec((CHUNK, 4 * GROUP), lambda c: (rev(c), 0)),
                  pl.BlockSpec((CHUNK, GROUP), lambda c: (rev(c), 0)),
                  pl.BlockSpec((None, N_HEADS, HEAD_DIM, HEAD_DIM), lambda c: (rev(c), 0, 0, 0)),
                  pl.BlockSpec((CHUNK, GROUP), lambda c: (rev(c), 0)),
                  pl.BlockSpec((CHUNK, HEAD_DIM), lambda c: (rev(c), 0)),
                  pl.BlockSpec((CHUNK, HEAD_DIM), lambda c: (rev(c), 0)),
                  head_tab, head_tab, head_tab,
                  pl.BlockSpec((N_HEADS, 8, HEAD_DIM), lambda c: (0, 0, 0)),
                  pl.BlockSpec((1, GROUP), lambda c: (0, 0))],
        out_specs=(pl.BlockSpec((CHUNK, 4 * GROUP), lambda c: (rev(c), 0)),
                   pl.BlockSpec((1, GROUP), lambda c: (0, 0))),
        scratch_shapes=[pltpu.VMEM((N_HEADS, HEAD_DIM, HEAD_DIM), F32)],
        compiler_params=_params(("arbitrary",)),
    )(proj, o_pre, states, d_mix, cos2, sin2, decay, xi, zeta, g_rows, ret_gain)


FF_TILE = (7 * GROUP) // 128


def _log_forget(ff, bias_row, valid):
    x = ff + bias_row
    e = jnp.exp(-jnp.abs(x))
    lf = jnp.minimum(x, 0.0) - jnp.log(1.0 + e)
    head_lane = lax.broadcasted_iota(jnp.int32, x.shape, 1) < N_HEADS
    keep = lambda t: jnp.where(head_lane, jnp.where(valid, t, 0.0), 0.0)
    return keep(lf), keep(jnp.where(x >= 0, e, 1.0) / (1.0 + e))


def _fox_prep(proj, bias_row):
    l = proj.shape[0]
    n_blocks = l // CHUNK

    def body(ff_ref, b_ref, bc_ref, rows_ref, cum):
        r = lax.broadcasted_iota(jnp.int32, (CHUNK, CHUNK), 0)
        cidx = lax.broadcasted_iota(jnp.int32, (CHUNK, CHUNK), 1)
        tri = jnp.where(r >= cidx, 1.0, 0.0).astype(F32)
        carry = jnp.zeros((1, 128), F32)
        for blk in range(n_blocks):
            rows = slice(blk * CHUNK, (blk + 1) * CHUNK)
            valid = _row_valid(blk, CHUNK)
            lf, _ = _log_forget(ff_ref[rows, :], b_ref[...], valid)
            local = jnp.dot(tri, lf, precision=lax.Precision.HIGHEST, preferred_element_type=F32) + carry
            carry = local[CHUNK - 1:CHUNK, :]
            masked = jnp.where(valid, local, -NEG_BIG)
            cum[rows, :] = masked
            t = masked.T
            for h in range(N_HEADS):
                rows_ref[h, :, rows] = t[h:h + 1, :]
        full = cum[...]
        for h in range(N_HEADS):
            bc_ref[h] = jnp.broadcast_to(full[:, h:h + 1], (l, 128))

    return pl.pallas_call(
        body, name="fox_prep",
        out_shape=(jax.ShapeDtypeStruct((N_HEADS, l, 128), F32), jax.ShapeDtypeStruct((N_HEADS, 1, l), F32)),
        grid=(1,),
        in_specs=[pl.BlockSpec((l, 128), lambda i: (0, FF_TILE)), pl.BlockSpec((1, 128), lambda i: (0, 0))],
        out_specs=(pl.BlockSpec((N_HEADS, l, 128), lambda i: (0, 0, 0)),
                   pl.BlockSpec((N_HEADS, 1, l), lambda i: (0, 0, 0))),
        scratch_shapes=[pltpu.VMEM((l, 128), F32)],
        compiler_params=_params(("arbitrary",)),
    )(proj, bias_row)


KEY_BLOCK = 2 * CHUNK


def _key_blocks(pair, l):
    end = min((pair + 1) * KEY_BLOCK, l)
    return [(s, min(KEY_BLOCK, end - s)) for s in range(0, end, KEY_BLOCK)]


def _per_pair(i, l, step):
    for pair in range(-(-(l // CHUNK) // 2)):
        @pl.when(i // 2 == pair)
        def _(pair=pair):
            step(_key_blocks(pair, l))


def _fox_fwd(proj, cum_bc, cum_rows, mix):
    l = proj.shape[0]
    n_blocks = l // CHUNK
    scale = HEAD_DIM ** -0.5
    qt, kt, vt = 4 * N_HEADS, 5 * N_HEADS, 6 * N_HEADS

    def body(q_ref, k_ref, v_ref, cq_ref, ck_ref, mix_in, o_ref, lse_ref, kb_s, vb_s):
        i = pl.program_id(1)

        @pl.when(i == 0)
        def _():
            kb_s[...] = k_ref[...].astype(MXU_DTYPE)
            vb_s[...] = v_ref[...].astype(MXU_DTYPE)

        qb = q_ref[...].astype(MXU_DTYPE)
        cq = cq_ref[...]

        def step(blocks):
            m = jnp.full((CHUNK, 1), NEG_BIG, F32)
            lsum = jnp.zeros((CHUNK, 1), F32)
            acc = jnp.zeros((CHUNK, HEAD_DIM), F32)
            for n, (start, size) in enumerate(blocks):
                bias = jnp.tile(cq, (1, size // CHUNK)) - ck_ref[:, start:start + size]
                s = _dot_nt(qb, kb_s[start:start + size, :]) * scale + bias
                if n == len(blocks) - 1:
                    q_pos = i * CHUNK + lax.broadcasted_iota(jnp.int32, (CHUNK, size), 0)
                    k_pos = start + lax.broadcasted_iota(jnp.int32, (CHUNK, size), 1)
                    s = jnp.where(k_pos <= q_pos, s, NEG_BIG)
                m_new = jnp.maximum(m, jnp.max(s, axis=-1, keepdims=True))
                alpha = jnp.exp(m - m_new)
                p = jnp.exp(s - m_new)
                lsum = lsum * alpha + jnp.sum(p, axis=-1, keepdims=True)
                acc = acc * alpha + _dot(p.astype(MXU_DTYPE), vb_s[start:start + size, :])
                m = m_new
            o = jnp.where(_row_valid(i, CHUNK), acc * (1.0 / lsum), 0.0)
            o_ref[...] = o.astype(o_ref.dtype)
            lse = m + jnp.log(lsum)
            lse_ref[...] = jnp.broadcast_to(lse, (CHUNK, CHUNK)).T[0:1, :]

        _per_pair(i, l, step)

    return pl.pallas_call(
        body, name="fox_fwd",
        out_shape=(jax.ShapeDtypeStruct(mix.shape, mix.dtype), jax.ShapeDtypeStruct((N_HEADS, 1, l), F32)),
        grid=(N_HEADS, n_blocks),
        in_specs=[pl.BlockSpec((CHUNK, HEAD_DIM), lambda h, i: (i, qt + h)),
                  pl.BlockSpec((l, HEAD_DIM), lambda h, i: (0, kt + h)),
                  pl.BlockSpec((l, HEAD_DIM), lambda h, i: (0, vt + h)),
                  pl.BlockSpec((None, CHUNK, 128), lambda h, i: (h, i, 0)),
                  pl.BlockSpec((None, 1, l), lambda h, i: (h, 0, 0)),
                  ANY],
        out_specs=(pl.BlockSpec((CHUNK, HEAD_DIM), lambda h, i: (i, N_HEADS + h)),
                   pl.BlockSpec((None, 1, CHUNK), lambda h, i: (h, 0, i))),
        input_output_aliases={5: 0},
        scratch_shapes=[pltpu.VMEM((l, HEAD_DIM), MXU_DTYPE), pltpu.VMEM((l, HEAD_DIM), MXU_DTYPE)],
        compiler_params=_params(("parallel", "arbitrary")),
    )(proj, proj, proj, cum_bc, cum_rows, mix)


def _fox_bwd(proj, cum_bc, cum_rows, d_mix, lse_rows):
    l = proj.shape[0]
    n_blocks = l // CHUNK
    scale = HEAD_DIM ** -0.5
    qt, kt, vt = 4 * N_HEADS, 5 * N_HEADS, 6 * N_HEADS

    def body(q_ref, k_ref, v_ref, do_ref, ck_ref, cq_ref, lse_ref,
             dq_ref, dk_ref, dv_ref, ds_ref, dk_acc, dv_acc, kb_s, vb_s):
        i = pl.program_id(1)

        @pl.when(i == 0)
        def _():
            dk_acc[...] = jnp.zeros_like(dk_acc)
            dv_acc[...] = jnp.zeros_like(dv_acc)
            ds_ref[...] = jnp.zeros_like(ds_ref)
            kb_s[...] = k_ref[...].astype(MXU_DTYPE)
            vb_s[...] = v_ref[...].astype(MXU_DTYPE)

        qb = q_ref[...].astype(MXU_DTYPE)
        dob = jnp.where(_row_valid(i, CHUNK), do_ref[...], 0.0).astype(MXU_DTYPE)
        shift = cq_ref[...] - lse_ref[...]

        def step(blocks):
            def probs(n, start, size):
                rows = slice(start, start + size)
                s_t = _dot_nt(kb_s[rows, :], qb) * scale + (shift - ck_ref[rows, :])
                if n == len(blocks) - 1:
                    k_pos = start + lax.broadcasted_iota(jnp.int32, (size, CHUNK), 0)
                    q_pos = i * CHUNK + lax.broadcasted_iota(jnp.int32, (size, CHUNK), 1)
                    s_t = jnp.where(k_pos <= q_pos, s_t, NEG_BIG)
                return jnp.exp(s_t), _dot_nt(vb_s[rows, :], dob)

            delta = jnp.zeros((1, CHUNK), F32)
            for n, (start, size) in enumerate(blocks):
                p_t, dp_t = probs(n, start, size)
                delta = delta + jnp.sum(p_t * dp_t, axis=0, keepdims=True)
            dq = jnp.zeros((CHUNK, HEAD_DIM), F32)
            for n, (start, size) in enumerate(blocks):
                rows = slice(start, start + size)
                p_t, dp_t = probs(n, start, size)
                ds_t = p_t * (dp_t - delta)
                ds_b = ds_t.astype(MXU_DTYPE)
                dv_acc[rows, :] += _dot(p_t.astype(MXU_DTYPE), dob)
                dk_acc[rows, :] += _dot(ds_b, qb) * scale
                ds_ref[rows, :] += ds_t
                dq = dq + _dot_tn(ds_b, kb_s[rows, :])
            dq_ref[...] = (dq * scale).astype(dq_ref.dtype)

        _per_pair(i, l, step)

        @pl.when(i == n_blocks - 1)
        def _():
            dk_ref[...] = dk_acc[...].astype(dk_ref.dtype)
            dv_ref[...] = dv_acc[...].astype(dv_ref.dtype)

    col = jax.ShapeDtypeStruct((l, GROUP), MXU_DTYPE)
    return pl.pallas_call(
        body, name="fox_bwd",
        out_shape=(col, col, col, jax.ShapeDtypeStruct((N_HEADS, l, 128), F32)),
        grid=(N_HEADS, n_blocks),
        in_specs=[pl.BlockSpec((CHUNK, HEAD_DIM), lambda h, i: (i, qt + h)),
                  pl.BlockSpec((l, HEAD_DIM), lambda h, i: (0, kt + h)),
                  pl.BlockSpec((l, HEAD_DIM), lambda h, i: (0, vt + h)),
                  pl.BlockSpec((CHUNK, HEAD_DIM), lambda h, i: (i, N_HEADS + h)),
                  pl.BlockSpec((None, l, 128), lambda h, i: (h, 0, 0)),
                  pl.BlockSpec((None, 1, CHUNK), lambda h, i: (h, 0, i)),
                  pl.BlockSpec((None, 1, CHUNK), lambda h, i: (h, 0, i))],
        out_specs=(pl.BlockSpec((CHUNK, HEAD_DIM), lambda h, i: (i, h)),
                   pl.BlockSpec((l, HEAD_DIM), lambda h, i: (0, h)),
                   pl.BlockSpec((l, HEAD_DIM), lambda h, i: (0, h)),
                   pl.BlockSpec((None, l, 128), lambda h, i: (h, 0, 0))),
        scratch_shapes=[pltpu.VMEM((l, HEAD_DIM), F32), pltpu.VMEM((l, HEAD_DIM), F32),
                        pltpu.VMEM((l, HEAD_DIM), MXU_DTYPE), pltpu.VMEM((l, HEAD_DIM), MXU_DTYPE)],
        compiler_params=_params(("parallel", "arbitrary")),
    )(proj, proj, proj, d_mix, cum_bc, cum_rows, lse_rows)


def _fox_gate_bwd(ds_sum, proj, bias_row):
    l = proj.shape[0]
    n_blocks = l // CHUNK

    def body(ds_ref, ff_ref, b_ref, dff_ref, db_ref):
        r = lax.broadcasted_iota(jnp.int32, (CHUNK, CHUNK), 0)
        cidx = lax.broadcasted_iota(jnp.int32, (CHUNK, CHUNK), 1)
        upper = jnp.where(cidx >= r, 1.0, 0.0).astype(F32)
        carry = jnp.zeros((1, 128), F32)
        db = jnp.zeros((1, 128), F32)
        for blk in reversed(range(n_blocks)):
            rows = slice(blk * CHUNK, (blk + 1) * CHUNK)
            key_sum = jnp.zeros((CHUNK, 128), F32)
            for h in range(N_HEADS):
                select = jnp.where(cidx == h, 1.0, 0.0).astype(F32)
                key_sum = key_sum + jnp.dot(ds_ref[h, rows, :], select, precision=lax.Precision.HIGHEST,
                                            preferred_element_type=F32)
            suffix = jnp.dot(upper, key_sum, precision=lax.Precision.HIGHEST, preferred_element_type=F32) + carry
            carry = suffix[0:1, :]
            _, dsig = _log_forget(ff_ref[rows, :], b_ref[...], _row_valid(blk, CHUNK))
            dff = -suffix * dsig
            dff_ref[rows, :] = dff.astype(dff_ref.dtype)
            db = db + jnp.sum(dff, axis=0, keepdims=True)
        db_ref[...] = db

    return pl.pallas_call(
        body, name="fox_gate_bwd",
        out_shape=(jax.ShapeDtypeStruct((l, 128), MXU_DTYPE), jax.ShapeDtypeStruct((1, 128), F32)),
        grid=(1,),
        in_specs=[pl.BlockSpec((N_HEADS, l, 128), lambda i: (0, 0, 0)),
                  pl.BlockSpec((l, 128), lambda i: (0, FF_TILE)),
                  pl.BlockSpec((1, 128), lambda i: (0, 0))],
        out_specs=(pl.BlockSpec((l, 128), lambda i: (0, 0)), pl.BlockSpec((1, 128), lambda i: (0, 0))),
        compiler_params=_params(("arbitrary",)),
    )(ds_sum, proj, bias_row)


def _conv(u, w, b):
    return b + w[0:1, :] * pltpu.roll(u, 2, 0) + w[1:2, :] * pltpu.roll(u, 1, 0) + w[2:3, :] * u


def _conv_act_fwd(u, conv_w, conv_b, d_ff):
    l = u.shape[0]
    tc = _divisor_tile(d_ff, 256, 128)
    nt = d_ff // tc

    def body(ug_ref, uv_ref, wg_ref, wv_ref, bg_ref, bv_ref, a_ref):
        yg = _conv(ug_ref[...], wg_ref[...], bg_ref[...])
        yv = _conv(uv_ref[...], wv_ref[...], bv_ref[...])
        act = yg * _sigmoid(yg) * yv
        a_ref[...] = jnp.where(_row_valid(0, l), act, 0.0).astype(a_ref.dtype)

    return pl.pallas_call(
        body, name="conv_act_fwd",
        out_shape=jax.ShapeDtypeStruct((l, d_ff), MXU_DTYPE),
        grid=(nt,),
        in_specs=[pl.BlockSpec((l, tc), lambda j: (0, j)), pl.BlockSpec((l, tc), lambda j: (0, j + nt)),
                  pl.BlockSpec((8, tc), lambda j: (0, j)), pl.BlockSpec((8, tc), lambda j: (0, j + nt)),
                  pl.BlockSpec((1, tc), lambda j: (0, j)), pl.BlockSpec((1, tc), lambda j: (0, j + nt))],
        out_specs=pl.BlockSpec((l, tc), lambda j: (0, j)),
        compiler_params=_params(("parallel",)),
    )(u, u, conv_w, conv_w, conv_b, conv_b)


def _conv_act_bwd(u, conv_w, conv_b, d_act, d_ff):
    l = u.shape[0]
    tc = _divisor_tile(d_ff, 256, 128)
    nt = d_ff // tc

    def body(ug_ref, uv_ref, wg_ref, wv_ref, bg_ref, bv_ref, da_ref, du_ref, dwb_ref):
        valid = _row_valid(0, l)
        ug, uv = ug_ref[...], uv_ref[...]
        wg, wv = wg_ref[...], wv_ref[...]
        yg = _conv(ug, wg, bg_ref[...])
        yv = _conv(uv, wv, bv_ref[...])
        sig = _sigmoid(yg)
        da = jnp.where(valid, da_ref[...], 0.0)
        d_yv = da * (yg * sig)
        d_yg = da * yv * (sig * (1.0 + yg * (1.0 - sig)))
        for idx, (dy, uu, w) in enumerate(((d_yg, ug, wg), (d_yv, uv, wv))):
            du = w[2:3, :] * dy + w[1:2, :] * pltpu.roll(dy, l - 1, 0) + w[0:1, :] * pltpu.roll(dy, l - 2, 0)
            du_ref[idx] = jnp.where(valid, du, 0.0).astype(du_ref.dtype)
            dwb_ref[idx, 0:1, :] = jnp.sum(dy * pltpu.roll(uu, 2, 0), axis=0, keepdims=True)
            dwb_ref[idx, 1:2, :] = jnp.sum(dy * pltpu.roll(uu, 1, 0), axis=0, keepdims=True)
            dwb_ref[idx, 2:3, :] = jnp.sum(dy * uu, axis=0, keepdims=True)
            dwb_ref[idx, 3:4, :] = jnp.sum(dy, axis=0, keepdims=True)
            dwb_ref[idx, 4:8, :] = jnp.zeros((4, tc), F32)

    return pl.pallas_call(
        body, name="conv_act_bwd",
        out_shape=(jax.ShapeDtypeStruct((2, l, d_ff), MXU_DTYPE), jax.ShapeDtypeStruct((2, 8, d_ff), F32)),
        grid=(nt,),
        in_specs=[pl.BlockSpec((l, tc), lambda j: (0, j)), pl.BlockSpec((l, tc), lambda j: (0, j + nt)),
                  pl.BlockSpec((8, tc), lambda j: (0, j)), pl.BlockSpec((8, tc), lambda j: (0, j + nt)),
                  pl.BlockSpec((1, tc), lambda j: (0, j)), pl.BlockSpec((1, tc), lambda j: (0, j + nt)),
                  pl.BlockSpec((l, tc), lambda j: (0, j))],
        out_specs=(pl.BlockSpec((2, l, tc), lambda j: (0, 0, j)), pl.BlockSpec((2, 8, tc), lambda j: (0, 0, j))),
        compiler_params=_params(("parallel",)),
    )(u, u, conv_w, conv_w, conv_b, conv_b, d_act)


def _adamw(w, g, m, v, name):
    shape = w.shape
    if w.ndim == 1:
        as2d = (1, shape[0])
    else:
        as2d = (int(np.prod(shape[:-1])), shape[-1])
    r, c = as2d
    tr = _divisor_tile(r, 256, 8)
    spec = pl.BlockSpec((tr, c), lambda i: (i, 0))

    def body(w_ref, g_ref, m_ref, v_ref, d_ref, nm_ref, nv_ref):
        gv = g_ref[...]
        nm = ADAM_B1 * m_ref[...] + (1.0 - ADAM_B1) * gv
        nv = ADAM_B2 * v_ref[...] + (1.0 - ADAM_B2) * (gv * gv)
        m_hat = nm / (1.0 - ADAM_B1 ** ADAM_STEP)
        v_hat = nv / (1.0 - ADAM_B2 ** ADAM_STEP)
        d_ref[...] = -ADAM_LR * (m_hat / (jnp.sqrt(v_hat) + ADAM_EPS) + ADAM_WD * w_ref[...])
        nm_ref[...] = nm
        nv_ref[...] = nv

    sds = jax.ShapeDtypeStruct(as2d, F32)
    outs = pl.pallas_call(
        body, name=name, out_shape=(sds, sds, sds), grid=(r // tr,),
        in_specs=[spec] * 4, out_specs=(spec,) * 3,
        compiler_params=_params(("parallel",)),
    )(w.reshape(as2d), g.reshape(as2d), m.reshape(as2d), v.reshape(as2d))
    return tuple(o.reshape(shape) for o in outs)


def _pad_rows(a, rows):
    return jnp.pad(a, ((0, rows - a.shape[0]), (0, 0)))


def kernel(x, meta_tokens, norm1_gain, w_in, b_forget, ret_norm_gain, w_out, norm2_gain, w_up, conv_w, conv_b, w_down, final_norm_gain, loss_target, m_meta_tokens, m_norm1_gain, m_w_in, m_b_forget, m_ret_norm_gain, m_w_out, m_norm2_gain, m_w_up, m_conv_w, m_conv_b, m_w_down, m_final_norm_gain, v_meta_tokens, v_norm1_gain, v_w_in, v_b_forget, v_ret_norm_gain, v_w_out, v_norm2_gain, v_w_up, v_conv_w, v_conv_b, v_w_down, v_final_norm_gain):
    seq, d = x.shape[1], x.shape[2]
    l = CHUNK + seq
    d_ff = w_down.shape[1] * N_DEV
    up_shard = w_up.shape[2]
    assert 4 * up_shard == d_ff and w_in.shape[2] == WIN_SHARD and d == 2 * GROUP
    dev = _device_index()
    mx, my, mc = _my_position()
    core = jnp.reshape(mc, (1,)).astype(jnp.int32)
    chip = jnp.reshape(2 * mx + my, (1,)).astype(jnp.int32)
    dev1 = jnp.reshape(dev, (1,)).astype(jnp.int32)

    small = jnp.concatenate([meta_tokens.reshape(-1, 128), conv_w[0].reshape(-1, 128)], axis=0)
    n_meta_rows = N_META * (d // N_DEV) // 128
    small_rows = small.shape[0]
    small_all = _all_gather(_pad_rows(small, -(-small_rows // 8) * 8), "gather_small")
    meta_full = jnp.transpose(small_all[:, :n_meta_rows].reshape(N_DEV, N_META, d // N_DEV), (1, 0, 2)).reshape(N_META, d)
    conv_w_full = _pad_rows(jnp.transpose(small_all[:, n_meta_rows:small_rows].reshape(N_DEV, 3, up_shard),
                                          (1, 0, 2)).reshape(3, 2 * d_ff), 8)
    w_in_padded = jnp.pad(w_in[0], ((0, 0), (0, WIN_BLOCK - WIN_SHARD))).astype(WIRE_DTYPE)
    start_in = _gather_start(w_in_padded, dev1, small_all, "gather_w_in_start")

    h0 = jnp.concatenate([jnp.zeros((PAD_ROWS, d), F32), meta_full, x[0]], axis=0)
    consts = _retention_consts(l)
    bias_row = jnp.pad(b_forget, ((0, 0), (0, 128 - N_HEADS)))
    a = _rmsnorm_fwd(h0, norm1_gain + start_in[4][0, 0], "rmsnorm1")
    w_in_blocks = _gather_finish(start_in, a, "gather_w_in")
    start_out = _gather_start(w_out[0].astype(WIRE_DTYPE), dev1, w_in_blocks, "gather_w_out_start")
    w_in_full = _assemble_w_in(w_in_blocks).astype(MXU_DTYPE)
    proj = _mm_nn(a, w_in_full, F32, "mm_proj", after=start_out[4])
    w_out_blocks = _gather_finish(start_out, proj, "gather_w_out")
    start_up = _gather_start(w_up[0].astype(WIRE_DTYPE), dev1, w_out_blocks, "gather_w_up_start")
    ret_mix, ret_pre, ret_states = _retention_fwd(proj, ret_norm_gain + start_up[4][0, 0], consts)
    cum_bc, cum_rows = _fox_prep(proj, bias_row + start_up[4][0:1, :])
    mix, lse_rows = _fox_fwd(proj, cum_bc, cum_rows, ret_mix)
    w_out_full = w_out_blocks.reshape(d, d).astype(MXU_DTYPE)
    h1, cn = _rmsnorm_fwd(h0, norm2_gain, "resid_rmsnorm2", res=_mm_nn(mix, w_out_full, F32, "mm_out"))
    w_up_blocks = _gather_finish(start_up, cn, "gather_w_up").astype(MXU_DTYPE)
    start_down = _gather_start(w_down[0].astype(WIRE_DTYPE), dev1, w_up_blocks, "gather_w_down_start")
    u = _mm(cn, w_up_blocks,
            a_spec=pl.BlockSpec((_divisor_tile(l, 1088, 16), d), lambda i, j, k: (i, 0)),
            b_spec=pl.BlockSpec((None, d, up_shard), lambda i, j, k: (j, 0, 0)),
            o_spec=pl.BlockSpec((_divisor_tile(l, 1088, 16), up_shard), lambda i, j, k: (i, j)),
            out_shape=jax.ShapeDtypeStruct((l, 2 * d_ff), F32),
            grid=(l // _divisor_tile(l, 1088, 16), N_DEV, 1), contract=(1, 0), nk=1, name="mm_up",
            after=start_down[4])
    act = _conv_act_fwd(u, conv_w_full, conv_b + start_down[4][0, 0], d_ff)
    w_down_full = _gather_finish(start_down, act, "gather_w_down").reshape(d_ff, d).astype(MXU_DTYPE)
    mlp_out = _mm_nn(act, w_down_full, F32, "mm_down", tm_cap=544, tk_cap=d_ff)
    d_h2, d_h2_b, dg_final, loss_part = _loss_head(h1, mlp_out, final_norm_gain.reshape(1, d), loss_target[0])

    gw_down = _mm_tn(act, d_h2_b, WIRE_DTYPE, "mm_gw_down", tm_cap=1408, tn_cap=1024)
    rs_down = _reduce_scatter_start(gw_down.reshape(N_DEV, d_ff // N_DEV, d), core, "rs_w_down")
    d_act = _mm_nt(d_h2_b, w_down_full, F32, "mm_d_act", after=rs_down[4])
    d_u, d_conv = _conv_act_bwd(u, conv_w_full, conv_b + rs_down[4][0, 0], d_act, d_ff)
    tm = _divisor_tile(l, 1088, 16)
    gw_up = _mm(cn, d_u,
                a_spec=pl.BlockSpec((l, d // 2), lambda i, j, k: (0, i)),
                b_spec=pl.BlockSpec((None, l, up_shard), lambda i, j, k: (j // 4, 0, j % 4)),
                o_spec=pl.BlockSpec((None, d // 2, up_shard), lambda i, j, k: (j, i, 0)),
                out_shape=jax.ShapeDtypeStruct((N_DEV, d, up_shard), WIRE_DTYPE),
                grid=(2, N_DEV, 1), contract=(0, 0), nk=1, name="mm_gw_up")
    rs_up = _reduce_scatter_start(gw_up, core, "rs_w_up")
    d_cn = _mm_d_cn(d_u, w_up_blocks, rs_up[4])
    d_h1, d_h1_b, dg_norm2 = _rmsnorm_bwd(d_h2, d_cn, h1, norm2_gain + rs_up[4][0, 0], "rmsnorm2_bwd", True)

    gw_out = _mm_tn(mix, d_h1_b, WIRE_DTYPE, "mm_gw_out")
    rs_out = _reduce_scatter_start(gw_out.reshape(N_DEV, d // N_DEV, d), core, "rs_w_out")
    d_mix = _mm_nt(d_h1_b, w_out_full, F32, "mm_d_mix", after=rs_out[4])
    d_fq, d_fk, d_fv, ds_sum = _fox_bwd(proj, cum_bc, cum_rows, d_mix, lse_rows)
    d_ff_tile, db_forget_row = _fox_gate_bwd(ds_sum, proj, bias_row)
    d_ret, dg_ret = _retention_bwd(proj, ret_pre, ret_states, d_mix, ret_norm_gain + rs_out[4][0, 0], consts)
    d_proj = jnp.concatenate(
        [d_ret, d_fq, d_fk, d_fv, d_ff_tile, jnp.zeros((l, WIN_N - 7 * GROUP - 128), MXU_DTYPE)], axis=1)
    gw_in = _mm_tn(a, d_proj, WIRE_DTYPE, "mm_gw_in")
    rs_in = _reduce_scatter_start(_extract_w_in_windows(gw_in), core, "rs_w_in")
    d_a = _mm_nt(d_proj, w_in_full, F32, "mm_d_a", tm_cap=544, tn_cap=256, tk_cap=WIN_N, after=rs_in[4])
    d_h0, dg_norm1 = _rmsnorm_bwd(d_h1, d_a, h0, norm1_gain + rs_in[4][0, 0], "rmsnorm1_bwd", False)
    grad_x = d_h0[CHUNK:][None]
    d_meta = d_h0[PAD_ROWS:CHUNK]

    d_conv_w = jnp.concatenate([d_conv[0, 0:3], d_conv[1, 0:3]], axis=1)
    d_conv_b = jnp.concatenate([d_conv[0, 3:4], d_conv[1, 3:4]], axis=1)
    pieces = [loss_part[:, 0:1], dg_norm1, db_forget_row[:, 0:N_HEADS], dg_ret, dg_norm2, d_conv_b, dg_final,
              d_meta.reshape(1, -1), d_conv_w.reshape(1, -1)]
    sizes = [p.shape[1] for p in pieces]
    flat = jnp.concatenate(pieces, axis=1)
    padded = -(-flat.shape[1] // 1024) * 1024
    flat = jnp.pad(flat, ((0, 0), (0, padded - flat.shape[1]))).reshape(padded // 128, 128)
    small_ar = _small_all_reduce_start(flat, d_h0, "all_reduce_small")

    g_w_down = _reduce_scatter_finish(rs_down, small_ar[4], chip, "rs_w_down")[None]
    g_w_up = _reduce_scatter_finish(rs_up, g_w_down, chip, "rs_w_up")[None]
    g_w_out = _reduce_scatter_finish(rs_out, g_w_up, chip, "rs_w_out")[None]
    early = [_adamw(w, g, m, v, "adamw_" + n) for w, g, m, v, n in (
        (w_down, g_w_down, m_w_down, v_w_down, "w_down"), (w_up, g_w_up, m_w_up, v_w_up, "w_up"),
        (w_out, g_w_out, m_w_out, v_w_out, "w_out"))]
    g_w_in_window = _reduce_scatter_finish(rs_in, early[1][2], chip, "rs_w_in")
    g_w_in = g_w_in_window[:, :WIN_SHARD][None]
    early.append(_adamw(w_in, g_w_in, m_w_in, v_w_in, "adamw_w_in"))
    total = _small_all_reduce_finish(small_ar, early[3][2], dev1, "all_reduce_small").reshape(1, padded)
    offs = np.concatenate([[0], np.cumsum(sizes)])
    take = lambda k: total[:, int(offs[k]):int(offs[k + 1])]
    loss = take(0).reshape(())
    g_norm1, g_bf, g_ret_gain, g_norm2 = take(1), take(2), take(3), take(4)
    g_conv_b, g_final = take(5), take(6).reshape(d)
    g_meta = lax.dynamic_slice(take(7).reshape(N_META, d), (jnp.int32(0), (dev * (d // N_DEV)).astype(jnp.int32)),
                               (N_META, d // N_DEV))
    g_conv_w = lax.dynamic_slice(take(8).reshape(3, 2 * d_ff), (jnp.int32(0), (dev * up_shard).astype(jnp.int32)),
                                 (3, up_shard))[None]

    weights = [meta_tokens, norm1_gain, w_in, b_forget, ret_norm_gain, w_out, norm2_gain, w_up, conv_w, conv_b,
               w_down, final_norm_gain]
    grads = [g_meta, g_norm1, g_w_in, g_bf, g_ret_gain, g_w_out, g_norm2, g_w_up, g_conv_w, g_conv_b, g_w_down,
             g_final]
    done = {"w_down": early[0], "w_up": early[1], "w_out": early[2], "w_in": early[3]}
    ms = [m_meta_tokens, m_norm1_gain, m_w_in, m_b_forget, m_ret_norm_gain, m_w_out, m_norm2_gain, m_w_up, m_conv_w,
          m_conv_b, m_w_down, m_final_norm_gain]
    vs = [v_meta_tokens, v_norm1_gain, v_w_in, v_b_forget, v_ret_norm_gain, v_w_out, v_norm2_gain, v_w_up, v_conv_w,
          v_conv_b, v_w_down, v_final_norm_gain]
    names = ["meta", "norm1", "w_in", "b_forget", "ret_gain", "w_out", "norm2", "w_up", "conv_w", "conv_b", "w_down",
             "final_gain"]
    deltas, new_ms, new_vs = [], [], []
    for w, g, m, v, n in zip(weights, grads, ms, vs, names):
        dl, nm, nv = done[n] if n in done else _adamw(w, g, m, v, "adamw_" + n)
        deltas.append(dl)
        new_ms.append(nm)
        new_vs.append(nv)
    return (loss, grad_x, *grads, *deltas, *new_ms, *new_vs)
```

```python
import functools

import numpy as np
import jax
import jax.numpy as jnp
from jax import lax
from jax.experimental import pallas as pl
from jax.experimental.pallas import tpu as pltpu

F32 = jnp.float32
MXU_DTYPE = jnp.bfloat16
WIRE_DTYPE = jnp.bfloat16

N_DEV = 8
N_META = 16
CHUNK = 128
PAD_ROWS = CHUNK - N_META
N_HEADS = 8
HEAD_DIM = 128
GROUP = N_HEADS * HEAD_DIM
IN_DIM = 7 * GROUP + N_HEADS
WIN_SHARD = IN_DIM // N_DEV
WIN_BLOCK = 1024
WIN_STRIDE = 896
WIN_N = 7680
ROPE_BASE = 10000.0
NORM_EPS = 1e-6
NEG_BIG = -1e30
ADAM_LR, ADAM_B1, ADAM_B2, ADAM_EPS, ADAM_WD, ADAM_STEP = 0.001, 0.9, 0.999, 1e-08, 0.01, 10
VMEM_LIMIT = 52 * 1024 * 1024
MESH = pl.DeviceIdType.MESH
ANY = pl.BlockSpec(memory_space=pl.ANY)
VMEM_SPEC = pl.BlockSpec(memory_space=pltpu.VMEM)


def _params(sem=None):
    kw = {"vmem_limit_bytes": VMEM_LIMIT}
    if sem is not None:
        kw["dimension_semantics"] = sem
    return pltpu.CompilerParams(**kw)


def _divisor_tile(n, cap, unit):
    if n <= cap:
        return n
    best = None
    for t in range(unit, cap + 1, unit):
        if n % t == 0:
            best = t
    assert best is not None, (n, cap, unit)
    return best


def _my_position():
    return lax.axis_index("x"), lax.axis_index("y"), lax.axis_index("c")


def _device_index():
    x, y, c = _my_position()
    return 4 * x + 2 * y + c


def _all_gather(shard, name):
    r, c = shard.shape

    def body(x_ref, out_ref, send_sems, recv_sems, local_sem):
        mx, my, mc = _my_position()
        me, sibling = (mx, my, mc), (mx, my, 1 - mc)
        chips = [(1 - mx, my), (mx, 1 - my), (1 - mx, 1 - my)]

        def slot(px, py, pc):
            return out_ref.at[4 * px + 2 * py + pc]

        def copy(k, block, to, src=None):
            return pltpu.make_async_remote_copy(
                src_ref=slot(*block) if src is None else src, dst_ref=slot(*block),
                send_sem=send_sems.at[k], recv_sem=recv_sems.at[k], device_id=to, device_id_type=MESH)

        mine = pltpu.make_async_copy(x_ref, slot(*me), local_sem)
        mine.start()
        first = [copy(0, me, sibling, src=x_ref)]
        first += [copy(1 + j, me, (*chip, mc), src=x_ref) for j, chip in enumerate(chips)]
        for cp in first:
            cp.start()
        passed = [copy(4 + j, (*chip, mc), sibling) for j, chip in enumerate(chips)]
        for j, chip in enumerate(chips):
            copy(1 + j, (*chip, mc), me).wait_recv()
            passed[j].start()
        copy(0, sibling, me).wait_recv()
        for j, chip in enumerate(chips):
            copy(4 + j, (*chip, 1 - mc), me).wait_recv()
        for cp in first + passed:
            cp.wait_send()
        mine.wait()

    return pl.pallas_call(
        body, name=name,
        out_shape=jax.ShapeDtypeStruct((N_DEV, r, c), shard.dtype),
        in_specs=[ANY], out_specs=ANY,
        scratch_shapes=[pltpu.SemaphoreType.DMA((7,)), pltpu.SemaphoreType.DMA((7,)), pltpu.SemaphoreType.DMA],
    )(shard)


HBM_SPEC = pl.BlockSpec(memory_space=pltpu.HBM)
SEM_SPEC = pl.BlockSpec(memory_space=pltpu.SEMAPHORE)
DATAFLOW_EFFECT = pltpu.SideEffectType.DATAFLOW_SIDE_EFFECTING


def _in_hbm(a):
    return pltpu.with_memory_space_constraint(a, pltpu.HBM)


def _split_start(src, land, make_copies, n_copies, after, name):
    if isinstance(land, tuple):
        land = lax.empty(land, src.dtype)
    land_shape = land.shape
    def body(src_ref, land_ref, after_ref, send_sems, recv_sems, src_thru, land_thru, token):
        for cp in make_copies(src_ref, land_ref, send_sems, recv_sems):
            cp.start()
        token[...] = jnp.zeros_like(token)

    return pl.pallas_call(
        body, name=name,
        out_shape=(pltpu.SemaphoreType.DMA((n_copies,)), pltpu.SemaphoreType.DMA((n_copies,)),
                   pltpu.HBM(src.shape, src.dtype), pltpu.HBM(land_shape, src.dtype),
                   jax.ShapeDtypeStruct((8, 128), F32)),
        in_specs=(HBM_SPEC, HBM_SPEC, ANY), out_specs=(SEM_SPEC, SEM_SPEC, HBM_SPEC, HBM_SPEC, VMEM_SPEC),
        input_output_aliases={0: 2, 1: 3},
        compiler_params=pltpu.CompilerParams(has_side_effects=DATAFLOW_EFFECT),
    )(_in_hbm(src), _in_hbm(land), after)


def _split_wait(started, after, make_copies, name):
    send_sems, recv_sems, src_thru, land_thru, _ = started

    def body(src_ref, land_ref, send_sems_ref, recv_sems_ref, after_ref, src_dead, land_out):
        for cp in make_copies(src_ref, land_ref, send_sems_ref, recv_sems_ref):
            cp.wait_send()
            cp.wait_recv()

    return pl.pallas_call(
        body, name=name,
        out_shape=(pltpu.HBM(src_thru.shape, src_thru.dtype), pltpu.HBM(land_thru.shape, land_thru.dtype)),
        in_specs=(HBM_SPEC, HBM_SPEC, SEM_SPEC, SEM_SPEC, ANY), out_specs=(HBM_SPEC, HBM_SPEC),
        input_output_aliases={0: 0, 1: 1},
        compiler_params=pltpu.CompilerParams(has_side_effects=DATAFLOW_EFFECT),
    )(src_thru, land_thru, send_sems, recv_sems, after)


def _gather_copies(x_ref, land_ref, send_sems, recv_sems):
    mx, my, mc = _my_position()
    me = 4 * mx + 2 * my + mc
    targets = [(mx, my, 1 - mc), (1 - mx, my, mc), (mx, 1 - my, mc), (1 - mx, 1 - my, mc)]
    return [pltpu.make_async_remote_copy(
        src_ref=x_ref, dst_ref=land_ref.at[me], send_sem=send_sems.at[k], recv_sem=recv_sems.at[k],
        device_id=t, device_id_type=MESH) for k, t in enumerate(targets)]


def _gather_start(shard, dev, after, name):
    r, c = shard.shape
    tr = _divisor_tile(r, 512, 16)

    def body(s_ref, x_ref, o_ref):
        o_ref[...] = x_ref[...]

    land = pl.pallas_call(
        body, name=name + "_own",
        out_shape=jax.ShapeDtypeStruct((N_DEV, r, c), shard.dtype),
        grid_spec=pltpu.PrefetchScalarGridSpec(
            num_scalar_prefetch=1, grid=(r // tr,),
            in_specs=[pl.BlockSpec((tr, c), lambda i, s: (i, 0))],
            out_specs=pl.BlockSpec((None, tr, c), lambda i, s: (s[0], i, 0))),
        compiler_params=_params(("parallel",)),
    )(dev, shard)
    return _split_start(shard, land, _gather_copies, 4, after, name)


def _gather_finish(started, after, name):
    _, land = _split_wait(started, after, _gather_copies, name + "_wait")

    def body(land_in, land_ref, send_sems, recv_sems):
        mx, my, mc = _my_position()
        chips = [(1 - mx, my), (mx, 1 - my), (1 - mx, 1 - my)]
        copies = [pltpu.make_async_remote_copy(
            src_ref=land_ref.at[4 * cx + 2 * cy + mc], dst_ref=land_ref.at[4 * cx + 2 * cy + mc],
            send_sem=send_sems.at[j], recv_sem=recv_sems.at[j],
            device_id=(mx, my, 1 - mc), device_id_type=MESH) for j, (cx, cy) in enumerate(chips)]
        for cp in copies:
            cp.start()
        for j, (cx, cy) in enumerate(chips):
            copies[j].wait_send()
            pltpu.make_async_remote_copy(
                src_ref=land_ref.at[4 * cx + 2 * cy + 1 - mc], dst_ref=land_ref.at[4 * cx + 2 * cy + 1 - mc],
                send_sem=send_sems.at[j], recv_sem=recv_sems.at[j],
                device_id=(mx, my, 1 - mc), device_id_type=MESH).wait_recv()

    return pl.pallas_call(
        body, name=name + "_pass",
        out_shape=jax.ShapeDtypeStruct(land.shape, land.dtype),
        in_specs=[ANY], out_specs=ANY,
        input_output_aliases={0: 0},
        scratch_shapes=[pltpu.SemaphoreType.DMA((3,)), pltpu.SemaphoreType.DMA((3,))],
    )(land)


def _chip_copies(p_ref, land_ref, send_sems, recv_sems):
    mx, my, mc = _my_position()
    chips = [(1 - mx, my), (mx, 1 - my), (1 - mx, 1 - my)]
    return [pltpu.make_async_remote_copy(
        src_ref=p_ref.at[2 * cx + cy], dst_ref=land_ref.at[j], send_sem=send_sems.at[j], recv_sem=recv_sems.at[j],
        device_id=(cx, cy, mc), device_id_type=MESH) for j, (cx, cy) in enumerate(chips)]


def _reduce_scatter_start(g, core, name):
    pair = _pair_sum(g, _exchange_sibling(g, name + "_d2d"), core, name + "_pairsum")
    return _split_start(pair, (3,) + pair.shape[1:], _chip_copies, 3, g, name + "_ici_start")


def _reduce_scatter_finish(started, after, chip, name):
    pair, from_chips = _split_wait(started, after, _chip_copies, name + "_ici_wait")
    return _final_sum(pair, from_chips, chip, name + "_sum")


def _exchange_sibling(g, name):
    _, r, c = g.shape

    def body(g_ref, out_ref, send_sems, recv_sems):
        mx, my, mc = _my_position()
        copies = [
            pltpu.make_async_remote_copy(
                src_ref=g_ref.at[2 * k + (1 - mc)], dst_ref=out_ref.at[k],
                send_sem=send_sems.at[k], recv_sem=recv_sems.at[k],
                device_id=(mx, my, 1 - mc), device_id_type=MESH)
            for k in range(4)]
        for cp in copies:
            cp.start()
        for cp in copies:
            cp.wait()

    return pl.pallas_call(
        body, name=name,
        out_shape=jax.ShapeDtypeStruct((4, r, c), g.dtype),
        in_specs=[ANY], out_specs=ANY,
        scratch_shapes=[pltpu.SemaphoreType.DMA((4,)), pltpu.SemaphoreType.DMA((4,))],
    )(g)


def _pair_sum(g, recv, core, name):
    _, r, c = g.shape
    tr = _divisor_tile(r, 512, 16)

    def body(s_ref, g_ref, r_ref, o_ref):
        o_ref[...] = (g_ref[...].astype(F32) + r_ref[...].astype(F32)).astype(o_ref.dtype)

    return pl.pallas_call(
        body, name=name,
        out_shape=jax.ShapeDtypeStruct((4, r, c), g.dtype),
        grid_spec=pltpu.PrefetchScalarGridSpec(
            num_scalar_prefetch=1, grid=(4, r // tr),
            in_specs=[pl.BlockSpec((None, tr, c), lambda k, i, s: (2 * k + s[0], i, 0)),
                      pl.BlockSpec((None, tr, c), lambda k, i, s: (k, i, 0))],
            out_specs=pl.BlockSpec((None, tr, c), lambda k, i, s: (k, i, 0))),
        compiler_params=_params(("parallel", "parallel")),
    )(core, g, recv)


def _final_sum(p, recv, chip, name):
    _, r, c = p.shape
    tr = _divisor_tile(r, 512, 16)

    def body(s_ref, p_ref, r_ref, o_ref):
        acc = p_ref[...].astype(F32)
        for j in range(3):
            acc = acc + r_ref[j].astype(F32)
        o_ref[...] = acc

    return pl.pallas_call(
        body, name=name,
        out_shape=jax.ShapeDtypeStruct((r, c), F32),
        grid_spec=pltpu.PrefetchScalarGridSpec(
            num_scalar_prefetch=1, grid=(r // tr,),
            in_specs=[pl.BlockSpec((None, tr, c), lambda i, s: (s[0], i, 0)),
                      pl.BlockSpec((3, tr, c), lambda i, s: (0, i, 0))],
            out_specs=pl.BlockSpec((tr, c), lambda i, s: (i, 0))),
        compiler_params=_params(("parallel",)),
    )(chip, p, recv)


def _all_to_all_copies(v_ref, land_ref, send_sems, recv_sems):
    mx, my, mc = _my_position()
    me = 4 * mx + 2 * my + mc
    copies = []
    for rel in range(1, N_DEV):
        bx, by, bc = (rel >> 2) & 1, (rel >> 1) & 1, rel & 1
        target = (1 - mx if bx else mx, 1 - my if by else my, 1 - mc if bc else mc)
        copies.append(pltpu.make_async_remote_copy(
            src_ref=v_ref, dst_ref=land_ref.at[me], send_sem=send_sems.at[rel - 1], recv_sem=recv_sems.at[rel - 1],
            device_id=target, device_id_type=MESH))
    return copies


def _small_all_reduce_start(v, after, name):
    return _split_start(v, (N_DEV,) + v.shape, _all_to_all_copies, N_DEV - 1, after, name + "_start")


def _small_all_reduce_finish(started, after, dev, name):
    v, land = _split_wait(started, after, _all_to_all_copies, name + "_wait")
    rows = v.shape[0]

    def body(me_ref, v_ref, land_ref, o_ref):
        for j in range(N_DEV):
            @pl.when(me_ref[0] == j)
            def _():
                o_ref[...] = v_ref[...] if j == 0 else o_ref[...] + v_ref[...]

            @pl.when(me_ref[0] != j)
            def _():
                o_ref[...] = land_ref[j] if j == 0 else o_ref[...] + land_ref[j]

    return pl.pallas_call(
        body, name=name + "_sum",
        out_shape=jax.ShapeDtypeStruct((rows, 128), F32),
        grid_spec=pltpu.PrefetchScalarGridSpec(
            num_scalar_prefetch=1, grid=(1,),
            in_specs=[pl.BlockSpec((rows, 128), lambda i, s: (0, 0)),
                      pl.BlockSpec((N_DEV, rows, 128), lambda i, s: (0, 0, 0))],
            out_specs=pl.BlockSpec((rows, 128), lambda i, s: (0, 0))),
        compiler_params=_params(("arbitrary",)),
    )(dev, v, land)


def _assemble_w_in(blocks):
    _, d, _ = blocks.shape
    tr = _divisor_tile(d, 128, 16)
    n_tiles = WIN_N // 128
    last = (N_DEV * WIN_STRIDE) // 128

    def body(b_ref, o_ref):
        win = []
        for i in range(N_DEV):
            w = b_ref[i].astype(F32)
            win.append(pltpu.roll(w, i, 1) if i else w)
        for t in range(n_tiles):
            if t > last:
                o_ref[:, t * 128:(t + 1) * 128] = jnp.zeros((tr, 128), o_ref.dtype)
                continue
            i = min(t // 7, N_DEV - 1)
            k = t - 7 * i
            val = win[i][:, k * 128:(k + 1) * 128]
            if k == 0 and i >= 1:
                val = val + win[i - 1][:, 7 * 128:8 * 128]
            o_ref[:, t * 128:(t + 1) * 128] = val.astype(o_ref.dtype)

    return pl.pallas_call(
        body, name="assemble_w_in",
        out_shape=jax.ShapeDtypeStruct((d, WIN_N), blocks.dtype),
        grid=(d // tr,),
        in_specs=[pl.BlockSpec((N_DEV, tr, WIN_BLOCK), lambda i: (0, i, 0))],
        out_specs=pl.BlockSpec((tr, WIN_N), lambda i: (i, 0)),
        compiler_params=_params(("parallel",)),
    )(blocks)


def _extract_w_in_windows(g):
    d, _ = g.shape
    tr = _divisor_tile(d, 128, 16)

    def body(g_ref, o_ref):
        for j in range(N_DEV):
            w = g_ref[:, WIN_STRIDE * j:WIN_STRIDE * j + WIN_BLOCK].astype(F32)
            o_ref[j] = (pltpu.roll(w, WIN_BLOCK - j, 1) if j else w).astype(o_ref.dtype)

    return pl.pallas_call(
        body, name="extract_w_in_windows",
        out_shape=jax.ShapeDtypeStruct((N_DEV, d, WIN_BLOCK), g.dtype),
        grid=(d // tr,),
        in_specs=[pl.BlockSpec((tr, WIN_N), lambda i: (i, 0))],
        out_specs=pl.BlockSpec((N_DEV, tr, WIN_BLOCK), lambda i: (0, i, 0)),
        compiler_params=_params(("parallel",)),
    )(g)


def _mm(a, b, *, a_spec, b_spec, o_spec, out_shape, grid, contract, nk, name, after=None):
    dn = (((contract[0],), (contract[1],)), ((), ()))
    tm, tn = o_spec.block_shape[-2:]
    behind = [] if after is None else [after]

    def body(a_ref, b_ref, *rest):
        o_ref, *scratch = rest[len(behind):]
        part = lax.dot_general(a_ref[...], b_ref[...], dn, preferred_element_type=F32)
        if nk == 1:
            o_ref[...] = part.astype(o_ref.dtype)
            return
        acc = scratch[0]
        k = pl.program_id(2)

        @pl.when(k == 0)
        def _():
            acc[...] = part

        @pl.when(k > 0)
        def _():
            acc[...] += part

        @pl.when(k == nk - 1)
        def _():
            o_ref[...] = acc[...].astype(o_ref.dtype)

    return pl.pallas_call(
        body, name=name, out_shape=out_shape, grid=grid,
        in_specs=[a_spec, b_spec] + [ANY] * len(behind), out_specs=o_spec,
        scratch_shapes=[] if nk == 1 else [pltpu.VMEM((tm, tn), F32)],
        compiler_params=_params(("parallel", "parallel", "arbitrary")),
    )(a, b, *behind)


def _mm_nn(a, b, out_dtype, name, tm_cap=1088, tn_cap=512, tk_cap=2048, after=None):
    m, k = a.shape
    _, n = b.shape
    tm, tn, tk = _divisor_tile(m, tm_cap, 16), _divisor_tile(n, tn_cap, 128), _divisor_tile(k, tk_cap, 128)
    return _mm(a, b,
               a_spec=pl.BlockSpec((tm, tk), lambda i, j, kk: (i, kk)),
               b_spec=pl.BlockSpec((tk, tn), lambda i, j, kk: (kk, j)),
               o_spec=pl.BlockSpec((tm, tn), lambda i, j, kk: (i, j)),
               out_shape=jax.ShapeDtypeStruct((m, n), out_dtype),
               grid=(m // tm, n // tn, k // tk), contract=(1, 0), nk=k // tk, name=name, after=after)


def _mm_nt(a, b, out_dtype, name, tm_cap=1088, tn_cap=512, tk_cap=2048, after=None):
    m, k = a.shape
    n, _ = b.shape
    tm, tn, tk = _divisor_tile(m, tm_cap, 16), _divisor_tile(n, tn_cap, 128), _divisor_tile(k, tk_cap, 128)
    return _mm(a, b,
               a_spec=pl.BlockSpec((tm, tk), lambda i, j, kk: (i, kk)),
               b_spec=pl.BlockSpec((tn, tk), lambda i, j, kk: (j, kk)),
               o_spec=pl.BlockSpec((tm, tn), lambda i, j, kk: (i, j)),
               out_shape=jax.ShapeDtypeStruct((m, n), out_dtype),
               grid=(m // tm, n // tn, k // tk), contract=(1, 1), nk=k // tk, name=name, after=after)


def _mm_tn(a, b, out_dtype, name, tm_cap=1024, tn_cap=512, after=None):
    l, m = a.shape
    _, n = b.shape
    tm, tn = _divisor_tile(m, tm_cap, 128), _divisor_tile(n, tn_cap, 128)
    return _mm(a, b,
               a_spec=pl.BlockSpec((l, tm), lambda i, j, kk: (0, i)),
               b_spec=pl.BlockSpec((l, tn), lambda i, j, kk: (0, j)),
               o_spec=pl.BlockSpec((tm, tn), lambda i, j, kk: (i, j)),
               out_shape=jax.ShapeDtypeStruct((m, n), out_dtype),
               grid=(m // tm, n // tn, 1), contract=(0, 0), nk=1, name=name, after=after)


def _mm_d_cn(d_u, w_up_blocks, after):
    _, l, d_ff = d_u.shape
    n, d, shard = w_up_blocks.shape
    per = d_ff // shard
    tm, tn = _divisor_tile(l, 544, 16), _divisor_tile(d, 256, 128)

    def body(a_ref, b_ref, after_ref, o_ref):
        acc = None
        for k in range(n):
            part = _dot_nt(a_ref[k // per, :, (k % per) * shard:(k % per + 1) * shard], b_ref[k])
            acc = part if acc is None else acc + part
        o_ref[...] = acc

    return pl.pallas_call(
        body, name="mm_d_cn", out_shape=jax.ShapeDtypeStruct((l, d), F32), grid=(l // tm, d // tn),
        in_specs=[pl.BlockSpec((2, tm, d_ff), lambda i, j: (0, i, 0)),
                  pl.BlockSpec((n, tn, shard), lambda i, j: (0, j, 0)), ANY],
        out_specs=pl.BlockSpec((tm, tn), lambda i, j: (i, j)),
        compiler_params=_params(("parallel", "parallel")),
    )(d_u, w_up_blocks, after)


def _row_tile(l):
    return _divisor_tile(l, 544, 8)


def _rmsnorm_fwd(h, gain, name, res=None):
    l, d = h.shape
    tr = _row_tile(l)
    row = pl.BlockSpec((tr, d), lambda i: (i, 0))
    vec = pl.BlockSpec((1, d), lambda i: (0, 0))

    def body(*refs):
        if res is None:
            h_ref, g_ref, n_ref = refs
            x = h_ref[...]
        else:
            h_ref, r_ref, g_ref, s_ref, n_ref = refs
            x = h_ref[...] + r_ref[...]
            s_ref[...] = x
        y = x * lax.rsqrt(jnp.mean(x * x, axis=-1, keepdims=True) + NORM_EPS)
        n_ref[...] = (y * g_ref[...]).astype(n_ref.dtype)

    normed = jax.ShapeDtypeStruct((l, d), MXU_DTYPE)
    if res is None:
        return pl.pallas_call(body, name=name, out_shape=normed, grid=(l // tr,), in_specs=[row, vec],
                              out_specs=row, compiler_params=_params(("parallel",)))(h, gain)
    return pl.pallas_call(body, name=name, out_shape=(jax.ShapeDtypeStruct((l, d), F32), normed),
                          grid=(l // tr,), in_specs=[row, row, vec], out_specs=(row, row),
                          compiler_params=_params(("parallel",)))(h, res, gain)


def _rmsnorm_bwd(d_res, d_normed, x, gain, name, with_mxu_copy):
    l, d = x.shape
    tr = _row_tile(l)
    row = pl.BlockSpec((tr, d), lambda i: (i, 0))
    vec = pl.BlockSpec((1, d), lambda i: (0, 0))

    def body(dres_ref, dn_ref, x_ref, g_ref, dx_ref, *rest):
        dg_ref = rest[-1]
        xv = x_ref[...]
        r = lax.rsqrt(jnp.mean(xv * xv, axis=-1, keepdims=True) + NORM_EPS)
        xh = xv * r
        dn = dn_ref[...]
        dxh = dn * g_ref[...]
        dx = dres_ref[...] + r * (dxh - xh * jnp.mean(dxh * xh, axis=-1, keepdims=True))
        dx_ref[...] = dx
        if with_mxu_copy:
            rest[0][...] = dx.astype(MXU_DTYPE)

        @pl.when(pl.program_id(0) == 0)
        def _():
            dg_ref[...] = jnp.zeros_like(dg_ref)

        dg_ref[...] += jnp.sum(dn * xh, axis=0, keepdims=True)

    outs = [jax.ShapeDtypeStruct((l, d), F32)]
    specs = [row]
    if with_mxu_copy:
        outs.append(jax.ShapeDtypeStruct((l, d), MXU_DTYPE))
        specs.append(row)
    outs.append(jax.ShapeDtypeStruct((1, d), F32))
    specs.append(vec)
    return pl.pallas_call(body, name=name, out_shape=tuple(outs), grid=(l // tr,),
                          in_specs=[row, row, row, vec], out_specs=tuple(specs),
                          compiler_params=_params(("arbitrary",)))(d_res, d_normed, x, gain)


def _loss_head(h1, mlp_out, gain, target):
    l, d = h1.shape
    n_blocks = l // CHUNK
    row = pl.BlockSpec((CHUNK, d), lambda i: (i, 0))
    vec = pl.BlockSpec((1, d), lambda i: (0, 0))
    tgt = pl.BlockSpec((CHUNK, d), lambda i: (jnp.maximum(i - 1, 0), 0))

    def body(h_ref, m_ref, g_ref, t_ref, dh_ref, dhb_ref, dg_ref, loss_ref, sq_ref):
        i = pl.program_id(0)
        x = h_ref[...] + m_ref[...]
        r = lax.rsqrt(jnp.mean(x * x, axis=-1, keepdims=True) + NORM_EPS)
        xh = x * r
        g = g_ref[...]
        real = i >= 1
        err = jnp.where(real, xh * g - t_ref[...], 0.0)
        dy = err * (1.0 / d)
        dxh = dy * g
        dh = r * (dxh - xh * jnp.mean(dxh * xh, axis=-1, keepdims=True))
        dh_ref[...] = dh
        dhb_ref[...] = dh.astype(MXU_DTYPE)

        @pl.when(i == 0)
        def _():
            dg_ref[...] = jnp.zeros_like(dg_ref)
            sq_ref[...] = jnp.zeros_like(sq_ref)

        dg_ref[...] += jnp.sum(dy * xh, axis=0, keepdims=True)
        sq_ref[...] += jnp.sum(err * err, axis=0, keepdims=True)

        @pl.when(i == n_blocks - 1)
        def _():
            total = jnp.sum(sq_ref[...], axis=-1, keepdims=True) * (0.5 / d)
            loss_ref[...] = jnp.broadcast_to(total, (1, 128))

    return pl.pallas_call(
        body, name="loss_head",
        out_shape=(jax.ShapeDtypeStruct((l, d), F32), jax.ShapeDtypeStruct((l, d), MXU_DTYPE),
                   jax.ShapeDtypeStruct((1, d), F32), jax.ShapeDtypeStruct((1, 128), F32)),
        grid=(n_blocks,), in_specs=[row, row, vec, tgt],
        out_specs=(row, row, vec, pl.BlockSpec((1, 128), lambda i: (0, 0))),
        scratch_shapes=[pltpu.VMEM((1, d), F32)],
        compiler_params=_params(("arbitrary",)),
    )(h1, mlp_out, gain, target)


def _dot(a, b):
    return jnp.dot(a, b, preferred_element_type=F32)


def _dot_nt(a, b):
    return lax.dot_general(a, b, (((1,), (1,)), ((), ())), preferred_element_type=F32)


def _dot_tn(a, b):
    return lax.dot_general(a, b, (((0,), (0,)), ((), ())), preferred_element_type=F32)


def _rope(t, cos2, sin2):
    return t * cos2 + pltpu.roll(t, HEAD_DIM // 2, 1) * sin2


def _rope_bwd(dr, cos2, sin2):
    return dr * cos2 + pltpu.roll(dr * sin2, HEAD_DIM // 2, 1)


def _sigmoid(x):
    return 1.0 / (1.0 + jnp.exp(-x))


def _row_valid(block, rows):
    r = block * CHUNK + lax.broadcasted_iota(jnp.int32, (rows, 1), 0)
    return r >= PAD_ROWS


def _retention_consts(l):
    pos = jnp.arange(l, dtype=F32) - PAD_ROWS
    inv_freq = 1.0 / (ROPE_BASE ** (jnp.arange(0, HEAD_DIM, 2, dtype=F32) / HEAD_DIM))
    ang = pos[:, None] * inv_freq[None, :]
    cos, sin = jnp.cos(ang), jnp.sin(ang)
    cos2 = jnp.concatenate([cos, cos], axis=-1)
    sin2 = jnp.concatenate([-sin, sin], axis=-1)
    log_g = jnp.log1p(-jnp.exp2(-5.0 - jnp.arange(N_HEADS, dtype=F32)))
    idx = jnp.arange(CHUNK, dtype=F32)
    diff = idx[:, None] - idx[None, :]
    decay = jnp.where(diff >= 0, jnp.exp(jnp.maximum(diff, 0.0)[None] * log_g[:, None, None]), 0.0)
    xi = jnp.exp((idx + 1.0)[None, :] * log_g[:, None])
    zeta = jnp.exp((CHUNK - 1.0 - idx)[None, :] * log_g[:, None])
    g_chunk = jnp.exp(CHUNK * log_g)
    bcast = lambda v: jnp.broadcast_to(v[:, :, None], (N_HEADS, CHUNK, HEAD_DIM))
    g_rows = jnp.broadcast_to(g_chunk[:, None, None], (N_HEADS, 8, HEAD_DIM))
    return cos2, sin2, decay, bcast(xi), bcast(zeta), g_rows


def _retention_fwd(proj, ret_gain, consts):
    l = proj.shape[0]
    n_chunks = l // CHUNK
    cos2, sin2, decay, xi, zeta, g_rows = consts
    scale = HEAD_DIM ** -0.5

    def body(p_ref, cos_ref, sin_ref, dec_ref, xi_ref, zeta_ref, gr_ref, gain_ref,
             mix_ref, o_ref, st_ref, state):
        c = pl.program_id(0)

        @pl.when(c == 0)
        def _():
            state[...] = jnp.zeros_like(state)

        cos_v, sin_v = cos_ref[...], sin_ref[...]
        valid = _row_valid(c, CHUNK)
        for h in range(N_HEADS):
            cols = slice(h * HEAD_DIM, (h + 1) * HEAD_DIM)
            q = p_ref[:, h * HEAD_DIM:(h + 1) * HEAD_DIM]
            k = p_ref[:, GROUP + h * HEAD_DIM:GROUP + (h + 1) * HEAD_DIM]
            v = p_ref[:, 2 * GROUP + h * HEAD_DIM:2 * GROUP + (h + 1) * HEAD_DIM]
            g = p_ref[:, 3 * GROUP + h * HEAD_DIM:3 * GROUP + (h + 1) * HEAD_DIM]
            rq = _rope(q, cos_v, sin_v).astype(MXU_DTYPE)
            rk = _rope(k, cos_v, sin_v) * scale
            rkb = rk.astype(MXU_DTYPE)
            vb = v.astype(MXU_DTYPE)
            st = state[h]
            st_ref[h] = st
            s = _dot_nt(rq, rkb) * dec_ref[h]
            o = _dot(s.astype(MXU_DTYPE), vb) + _dot(rq, st.astype(MXU_DTYPE)) * xi_ref[h]
            kz = (rk * zeta_ref[h]).astype(MXU_DTYPE)
            state[h] = gr_ref[h, 0:1, :] * st + _dot_tn(kz, vb)
            o_ref[:, cols] = o
            mu = jnp.mean(o, axis=-1, keepdims=True)
            oc = o - mu
            yn = oc * lax.rsqrt(jnp.mean(oc * oc, axis=-1, keepdims=True) + NORM_EPS)
            ret = (g * _sigmoid(g)) * (yn * gain_ref[:, cols])
            mix_ref[:, cols] = jnp.where(valid, ret, 0.0).astype(mix_ref.dtype)

    head_tab = pl.BlockSpec((N_HEADS, CHUNK, HEAD_DIM), lambda c: (0, 0, 0))
    return pl.pallas_call(
        body, name="retention_fwd",
        out_shape=(jax.ShapeDtypeStruct((l, 2 * GROUP), MXU_DTYPE), jax.ShapeDtypeStruct((l, GROUP), F32),
                   jax.ShapeDtypeStruct((n_chunks, N_HEADS, HEAD_DIM, HEAD_DIM), F32)),
        grid=(n_chunks,),
        in_specs=[pl.BlockSpec((CHUNK, 4 * GROUP), lambda c: (c, 0)),
                  pl.BlockSpec((CHUNK, HEAD_DIM), lambda c: (c, 0)),
                  pl.BlockSpec((CHUNK, HEAD_DIM), lambda c: (c, 0)),
                  head_tab, head_tab, head_tab,
                  pl.BlockSpec((N_HEADS, 8, HEAD_DIM), lambda c: (0, 0, 0)),
                  pl.BlockSpec((1, GROUP), lambda c: (0, 0))],
        out_specs=(pl.BlockSpec((CHUNK, GROUP), lambda c: (c, 0)),
                   pl.BlockSpec((CHUNK, GROUP), lambda c: (c, 0)),
                   pl.BlockSpec((None, N_HEADS, HEAD_DIM, HEAD_DIM), lambda c: (c, 0, 0, 0))),
        scratch_shapes=[pltpu.VMEM((N_HEADS, HEAD_DIM, HEAD_DIM), F32)],
        compiler_params=_params(("arbitrary",)),
    )(proj, cos2, sin2, decay, xi, zeta, g_rows, ret_gain)


def _retention_bwd(proj, o_pre, states, d_mix, ret_gain, consts):
    l = proj.shape[0]
    n_chunks = l // CHUNK
    cos2, sin2, decay, xi, zeta, g_rows = consts
    scale = HEAD_DIM ** -0.5
    rev = lambda c: n_chunks - 1 - c

    def body(p_ref, o_ref, st_ref, dm_ref, cos_ref, sin_ref, dec_ref, xi_ref, zeta_ref, gr_ref, gain_ref,
             dp_ref, dgain_ref, dstate):
        step = pl.program_id(0)

        @pl.when(step == 0)
        def _():
            dstate[...] = jnp.zeros_like(dstate)
            dgain_ref[...] = jnp.zeros_like(dgain_ref)

        cos_v, sin_v = cos_ref[...], sin_ref[...]
        valid = _row_valid(rev(step), CHUNK)
        for h in range(N_HEADS):
            cols = slice(h * HEAD_DIM, (h + 1) * HEAD_DIM)
            q = p_ref[:, h * HEAD_DIM:(h + 1) * HEAD_DIM]
            k = p_ref[:, GROUP + h * HEAD_DIM:GROUP + (h + 1) * HEAD_DIM]
            v = p_ref[:, 2 * GROUP + h * HEAD_DIM:2 * GROUP + (h + 1) * HEAD_DIM]
            g = p_ref[:, 3 * GROUP + h * HEAD_DIM:3 * GROUP + (h + 1) * HEAD_DIM]
            o = o_ref[:, cols]
            gain = gain_ref[:, cols]
            d_ret = jnp.where(valid, dm_ref[:, cols], 0.0)
            mu = jnp.mean(o, axis=-1, keepdims=True)
            oc = o - mu
            rstd = lax.rsqrt(jnp.mean(oc * oc, axis=-1, keepdims=True) + NORM_EPS)
            yn = oc * rstd
            sig = _sigmoid(g)
            gate = g * sig
            dgain_ref[:, cols] += jnp.sum(d_ret * gate * yn, axis=0, keepdims=True)
            d_g = d_ret * (yn * gain) * (sig * (1.0 + g * (1.0 - sig)))
            d_yn = d_ret * gate * gain
            d_o = rstd * (d_yn - jnp.mean(d_yn, axis=-1, keepdims=True)
                          - yn * jnp.mean(d_yn * yn, axis=-1, keepdims=True))
            rq = _rope(q, cos_v, sin_v)
            rk = _rope(k, cos_v, sin_v) * scale
            rqb, rkb, vb = rq.astype(MXU_DTYPE), rk.astype(MXU_DTYPE), v.astype(MXU_DTYPE)
            dob = d_o.astype(MXU_DTYPE)
            dec = dec_ref[h]
            xi_h, zeta_h = xi_ref[h], zeta_ref[h]
            st_b = st_ref[h].astype(MXU_DTYPE)
            dst = dstate[h]
            dst_b = dst.astype(MXU_DTYPE)
            s_b = (_dot_nt(rqb, rkb) * dec).astype(MXU_DTYPE)
            da_b = (_dot_nt(dob, vb) * dec).astype(MXU_DTYPE)
            doxi_b = (d_o * xi_h).astype(MXU_DTYPE)
            kz_b = (rk * zeta_h).astype(MXU_DTYPE)
            d_rq = _dot(da_b, rkb) + _dot_nt(doxi_b, st_b)
            d_rk = _dot_tn(da_b, rqb) + _dot_nt(vb, dst_b) * zeta_h
            d_v = _dot_tn(s_b, dob) + _dot(kz_b, dst_b)
            dstate[h] = gr_ref[h, 0:1, :] * dst + _dot_tn(rqb, doxi_b)
            d_q = _rope_bwd(d_rq, cos_v, sin_v)
            d_k = _rope_bwd(d_rk * scale, cos_v, sin_v)
            dp_ref[:, h * HEAD_DIM:(h + 1) * HEAD_DIM] = d_q.astype(dp_ref.dtype)
            dp_ref[:, GROUP + h * HEAD_DIM:GROUP + (h + 1) * HEAD_DIM] = d_k.astype(dp_ref.dtype)
            dp_ref[:, 2 * GROUP + h * HEAD_DIM:2 * GROUP + (h + 1) * HEAD_DIM] = d_v.astype(dp_ref.dtype)
            dp_ref[:, 3 * GROUP + h * HEAD_DIM:3 * GROUP + (h + 1) * HEAD_DIM] = d_g.astype(dp_ref.dtype)

    head_tab = pl.BlockSpec((N_HEADS, CHUNK, HEAD_DIM), lambda c: (0, 0, 0))
    return pl.pallas_call(
        body, name="retention_bwd",
        out_shape=(jax.ShapeDtypeStruct((l, 4 * GROUP), MXU_DTYPE), jax.ShapeDtypeStruct((1, GROUP), F32)),
        grid=(n_chunks,),
        in_specs=[pl.BlockSpec((CHUNK, 4 * GROUP), lambda c: (rev(c), 0)),
                  pl.BlockSpec((CHUNK, GROUP), lambda c: (rev(c), 0)),
                  pl.BlockSpec((None, N_HEADS, HEAD_DIM, HEAD_DIM), lambda c: (rev(c), 0, 0, 0)),
                  pl.BlockSpec((CHUNK, GROUP), lambda c: (rev(c), 0)),
                  pl.BlockSpec((CHUNK, HEAD_DIM), lambda c: (rev(c), 0)),
                  pl.BlockSpec((CHUNK, HEAD_DIM), lambda c: (rev(c), 0)),
                  head_tab, head_tab, head_tab,
                  pl.BlockSpec((N_HEADS, 8, HEAD_DIM), lambda c: (0, 0, 0)),
                  pl.BlockSpec((1, GROUP), lambda c: (0, 0))],
        out_specs=(pl.BlockSpec((CHUNK, 4 * GROUP), lambda c: (rev(c), 0)),
                   pl.BlockSpec((1, GROUP), lambda c: (0, 0))),
        scratch_shapes=[pltpu.VMEM((N_HEADS, HEAD_DIM, HEAD_DIM), F32)],
        compiler_params=_params(("arbitrary",)),
    )(proj, o_pre, states, d_mix, cos2, sin2, decay, xi, zeta, g_rows, ret_gain)


FF_TILE = (7 * GROUP) // 128


def _log_forget(ff, bias_row, valid):
    x = ff + bias_row
    e = jnp.exp(-jnp.abs(x))
    lf = jnp.minimum(x, 0.0) - jnp.log(1.0 + e)
    head_lane = lax.broadcasted_iota(jnp.int32, x.shape, 1) < N_HEADS
    keep = lambda t: jnp.where(head_lane, jnp.where(valid, t, 0.0), 0.0)
    return keep(lf), keep(jnp.where(x >= 0, e, 1.0) / (1.0 + e))


def _fox_prep(proj, bias_row):
    l = proj.shape[0]
    n_blocks = l // CHUNK

    def body(ff_ref, b_ref, bc_ref, rows_ref, cum):
        r = lax.broadcasted_iota(jnp.int32, (CHUNK, CHUNK), 0)
        cidx = lax.broadcasted_iota(jnp.int32, (CHUNK, CHUNK), 1)
        tri = jnp.where(r >= cidx, 1.0, 0.0).astype(F32)
        carry = jnp.zeros((1, 128), F32)
        for blk in range(n_blocks):
            rows = slice(blk * CHUNK, (blk + 1) * CHUNK)
            valid = _row_valid(blk, CHUNK)
            lf, _ = _log_forget(ff_ref[rows, :], b_ref[...], valid)
            local = jnp.dot(tri, lf, precision=lax.Precision.HIGHEST, preferred_element_type=F32) + carry
            carry = local[CHUNK - 1:CHUNK, :]
            masked = jnp.where(valid, local, -NEG_BIG)
            cum[rows, :] = masked
            t = masked.T
            for h in range(N_HEADS):
                rows_ref[h, :, rows] = t[h:h + 1, :]
        full = cum[...]
        for h in range(N_HEADS):
            bc_ref[h] = jnp.broadcast_to(full[:, h:h + 1], (l, 128))

    return pl.pallas_call(
        body, name="fox_prep",
        out_shape=(jax.ShapeDtypeStruct((N_HEADS, l, 128), F32), jax.ShapeDtypeStruct((N_HEADS, 1, l), F32)),
        grid=(1,),
        in_specs=[pl.BlockSpec((l, 128), lambda i: (0, FF_TILE)), pl.BlockSpec((1, 128), lambda i: (0, 0))],
        out_specs=(pl.BlockSpec((N_HEADS, l, 128), lambda i: (0, 0, 0)),
                   pl.BlockSpec((N_HEADS, 1, l), lambda i: (0, 0, 0))),
        scratch_shapes=[pltpu.VMEM((l, 128), F32)],
        compiler_params=_params(("arbitrary",)),
    )(proj, bias_row)


ATTN_BLOCK = 2 * CHUNK


def _attn_blocks(l):
    assert (l - CHUNK) % ATTN_BLOCK == 0
    return [(0, CHUNK)] + [(s, ATTN_BLOCK) for s in range(CHUNK, l, ATTN_BLOCK)]


def _rows_valid(start, size):
    return start + lax.broadcasted_iota(jnp.int32, (size, 1), 0) >= PAD_ROWS


def _fox_fwd(proj, cum_bc, cum_rows, mix):
    l = proj.shape[0]
    blocks = _attn_blocks(l)
    scale = HEAD_DIM ** -0.5
    qt, kt, vt = 4 * N_HEADS, 5 * N_HEADS, 6 * N_HEADS

    def body(q_ref, k_ref, v_ref, cbc_ref, crow_ref, mix_in, o_ref, lse_ref, qb_s, kb_s, vb_s):
        qb_s[...] = q_ref[...].astype(MXU_DTYPE)
        kb_s[...] = k_ref[...].astype(MXU_DTYPE)
        vb_s[...] = v_ref[...].astype(MXU_DTYPE)
        for p, (qs, qn) in enumerate(blocks):
            qb = qb_s[qs:qs + qn, :]
            cq = cbc_ref[qs:qs + qn, :]
            m = jnp.full((qn, 1), NEG_BIG, F32)
            lsum = jnp.zeros((qn, 1), F32)
            acc = jnp.zeros((qn, HEAD_DIM), F32)
            for j in range(p + 1):
                ks, kn = blocks[j]
                bias = jnp.tile(cq, (1, kn // CHUNK)) - crow_ref[:, ks:ks + kn]
                s = _dot_nt(qb, kb_s[ks:ks + kn, :]) * scale + bias
                if j == p:
                    q_pos = qs + lax.broadcasted_iota(jnp.int32, (qn, kn), 0)
                    k_pos = ks + lax.broadcasted_iota(jnp.int32, (qn, kn), 1)
                    s = jnp.where(k_pos <= q_pos, s, NEG_BIG)
                m_new = jnp.maximum(m, jnp.max(s, axis=-1, keepdims=True))
                alpha = jnp.exp(m - m_new)
                pr = jnp.exp(s - m_new)
                lsum = lsum * alpha + jnp.sum(pr, axis=-1, keepdims=True)
                acc = acc * alpha + _dot(pr.astype(MXU_DTYPE), vb_s[ks:ks + kn, :])
                m = m_new
            o = jnp.where(_rows_valid(qs, qn), acc * (1.0 / lsum), 0.0)
            o_ref[qs:qs + qn, :] = o.astype(o_ref.dtype)
            lse = m + jnp.log(lsum)
            lse_ref[:, qs:qs + qn] = jnp.broadcast_to(lse, (qn, CHUNK)).T[0:1, :]

    head_col = lambda t: pl.BlockSpec((l, HEAD_DIM), lambda h: (0, t + h))
    return pl.pallas_call(
        body, name="fox_fwd",
        out_shape=(jax.ShapeDtypeStruct(mix.shape, mix.dtype), jax.ShapeDtypeStruct((N_HEADS, 1, l), F32)),
        grid=(N_HEADS,),
        in_specs=[head_col(qt), head_col(kt), head_col(vt),
                  pl.BlockSpec((None, l, 128), lambda h: (h, 0, 0)),
                  pl.BlockSpec((None, 1, l), lambda h: (h, 0, 0)),
                  ANY],
        out_specs=(head_col(N_HEADS), pl.BlockSpec((None, 1, l), lambda h: (h, 0, 0))),
        input_output_aliases={5: 0},
        scratch_shapes=[pltpu.VMEM((l, HEAD_DIM), MXU_DTYPE)] * 3,
        compiler_params=_params(("parallel",)),
    )(proj, proj, proj, cum_bc, cum_rows, mix)


def _fox_bwd(proj, cum_bc, cum_rows, d_mix, lse_rows):
    l = proj.shape[0]
    blocks = _attn_blocks(l)
    scale = HEAD_DIM ** -0.5
    qt, kt, vt = 4 * N_HEADS, 5 * N_HEADS, 6 * N_HEADS

    def body(q_ref, k_ref, v_ref, do_ref, cbc_ref, crow_ref, lse_ref,
             dq_ref, dk_ref, dv_ref, ds_ref, dk_acc, dv_acc, qb_s, kb_s, vb_s, dob_s):
        qb_s[...] = q_ref[...].astype(MXU_DTYPE)
        kb_s[...] = k_ref[...].astype(MXU_DTYPE)
        vb_s[...] = v_ref[...].astype(MXU_DTYPE)
        dob_s[...] = jnp.where(_rows_valid(0, l), do_ref[...], 0.0).astype(MXU_DTYPE)
        dk_acc[...] = jnp.zeros_like(dk_acc)
        dv_acc[...] = jnp.zeros_like(dv_acc)
        ds_ref[...] = jnp.zeros_like(ds_ref)
        shift_row = crow_ref[...] - lse_ref[...]

        for p, (qs, qn) in enumerate(blocks):
            qb, dob = qb_s[qs:qs + qn, :], dob_s[qs:qs + qn, :]
            shift = shift_row[:, qs:qs + qn]

            def probs(j):
                ks, kn = blocks[j]
                ck = jnp.tile(cbc_ref[ks:ks + kn, :], (1, qn // CHUNK))
                s_t = _dot_nt(kb_s[ks:ks + kn, :], qb) * scale + (shift - ck)
                if j == p:
                    k_pos = ks + lax.broadcasted_iota(jnp.int32, (kn, qn), 0)
                    q_pos = qs + lax.broadcasted_iota(jnp.int32, (kn, qn), 1)
                    s_t = jnp.where(k_pos <= q_pos, s_t, NEG_BIG)
                return jnp.exp(s_t), _dot_nt(vb_s[ks:ks + kn, :], dob)

            delta = jnp.zeros((1, qn), F32)
            for j in range(p + 1):
                p_t, dp_t = probs(j)
                delta = delta + jnp.sum(p_t * dp_t, axis=0, keepdims=True)
            dq = jnp.zeros((qn, HEAD_DIM), F32)
            for j in range(p + 1):
                ks, kn = blocks[j]
                rows = slice(ks, ks + kn)
                p_t, dp_t = probs(j)
                ds_t = p_t * (dp_t - delta)
                ds_b = ds_t.astype(MXU_DTYPE)
                dv_acc[rows, :] += _dot(p_t.astype(MXU_DTYPE), dob)
                dk_acc[rows, :] += _dot(ds_b, qb) * scale
                ds_ref[rows, :] += sum(ds_t[:, c:c + CHUNK] for c in range(0, qn, CHUNK))
                dq = dq + _dot_tn(ds_b, kb_s[rows, :])
            dq_ref[qs:qs + qn, :] = (dq * scale).astype(dq_ref.dtype)

        dk_ref[...] = dk_acc[...].astype(dk_ref.dtype)
        dv_ref[...] = dv_acc[...].astype(dv_ref.dtype)

    col = jax.ShapeDtypeStruct((l, GROUP), MXU_DTYPE)
    head_col = lambda t: pl.BlockSpec((l, HEAD_DIM), lambda h: (0, t + h))
    return pl.pallas_call(
        body, name="fox_bwd",
        out_shape=(col, col, col, jax.ShapeDtypeStruct((N_HEADS, l, 128), F32)),
        grid=(N_HEADS,),
        in_specs=[head_col(qt), head_col(kt), head_col(vt), head_col(N_HEADS),
                  pl.BlockSpec((None, l, 128), lambda h: (h, 0, 0)),
                  pl.BlockSpec((None, 1, l), lambda h: (h, 0, 0)),
                  pl.BlockSpec((None, 1, l), lambda h: (h, 0, 0))],
        out_specs=(head_col(0), head_col(0), head_col(0), pl.BlockSpec((None, l, 128), lambda h: (h, 0, 0))),
        scratch_shapes=[pltpu.VMEM((l, HEAD_DIM), F32)] * 2 + [pltpu.VMEM((l, HEAD_DIM), MXU_DTYPE)] * 4,
        compiler_params=_params(("parallel",)),
    )(proj, proj, proj, d_mix, cum_bc, cum_rows, lse_rows)


def _fox_gate_bwd(ds_sum, proj, bias_row):
    l = proj.shape[0]
    n_blocks = l // CHUNK

    def body(ds_ref, ff_ref, b_ref, dff_ref, db_ref):
        r = lax.broadcasted_iota(jnp.int32, (CHUNK, CHUNK), 0)
        cidx = lax.broadcasted_iota(jnp.int32, (CHUNK, CHUNK), 1)
        upper = jnp.where(cidx >= r, 1.0, 0.0).astype(F32)
        carry = jnp.zeros((1, 128), F32)
        db = jnp.zeros((1, 128), F32)
        for blk in reversed(range(n_blocks)):
            rows = slice(blk * CHUNK, (blk + 1) * CHUNK)
            key_sum = jnp.zeros((CHUNK, 128), F32)
            for h in range(N_HEADS):
                select = jnp.where(cidx == h, 1.0, 0.0).astype(F32)
                key_sum = key_sum + jnp.dot(ds_ref[h, rows, :], select, precision=lax.Precision.HIGHEST,
                                            preferred_element_type=F32)
            suffix = jnp.dot(upper, key_sum, precision=lax.Precision.HIGHEST, preferred_element_type=F32) + carry
            carry = suffix[0:1, :]
            _, dsig = _log_forget(ff_ref[rows, :], b_ref[...], _row_valid(blk, CHUNK))
            dff = -suffix * dsig
            dff_ref[rows, :] = dff.astype(dff_ref.dtype)
            db = db + jnp.sum(dff, axis=0, keepdims=True)
        db_ref[...] = db

    return pl.pallas_call(
        body, name="fox_gate_bwd",
        out_shape=(jax.ShapeDtypeStruct((l, 128), MXU_DTYPE), jax.ShapeDtypeStruct((1, 128), F32)),
        grid=(1,),
        in_specs=[pl.BlockSpec((N_HEADS, l, 128), lambda i: (0, 0, 0)),
                  pl.BlockSpec((l, 128), lambda i: (0, FF_TILE)),
                  pl.BlockSpec((1, 128), lambda i: (0, 0))],
        out_specs=(pl.BlockSpec((l, 128), lambda i: (0, 0)), pl.BlockSpec((1, 128), lambda i: (0, 0))),
        compiler_params=_params(("arbitrary",)),
    )(ds_sum, proj, bias_row)


def _conv(u, w, b):
    return b + w[0:1, :] * pltpu.roll(u, 2, 0) + w[1:2, :] * pltpu.roll(u, 1, 0) + w[2:3, :] * u


def _conv_act_fwd(u, conv_w, conv_b, d_ff):
    l = u.shape[0]
    tc = _divisor_tile(d_ff, 256, 128)
    nt = d_ff // tc

    def body(ug_ref, uv_ref, wg_ref, wv_ref, bg_ref, bv_ref, a_ref):
        yg = _conv(ug_ref[...], wg_ref[...], bg_ref[...])
        yv = _conv(uv_ref[...], wv_ref[...], bv_ref[...])
        act = yg * _sigmoid(yg) * yv
        a_ref[...] = jnp.where(_row_valid(0, l), act, 0.0).astype(a_ref.dtype)

    return pl.pallas_call(
        body, name="conv_act_fwd",
        out_shape=jax.ShapeDtypeStruct((l, d_ff), MXU_DTYPE),
        grid=(nt,),
        in_specs=[pl.BlockSpec((l, tc), lambda j: (0, j)), pl.BlockSpec((l, tc), lambda j: (0, j + nt)),
                  pl.BlockSpec((8, tc), lambda j: (0, j)), pl.BlockSpec((8, tc), lambda j: (0, j + nt)),
                  pl.BlockSpec((1, tc), lambda j: (0, j)), pl.BlockSpec((1, tc), lambda j: (0, j + nt))],
        out_specs=pl.BlockSpec((l, tc), lambda j: (0, j)),
        compiler_params=_params(("parallel",)),
    )(u, u, conv_w, conv_w, conv_b, conv_b)


def _conv_act_bwd(u, conv_w, conv_b, d_act, d_ff):
    l = u.shape[0]
    tc = _divisor_tile(d_ff, 256, 128)
    nt = d_ff // tc

    def body(ug_ref, uv_ref, wg_ref, wv_ref, bg_ref, bv_ref, da_ref, du_ref, dwb_ref):
        valid = _row_valid(0, l)
        ug, uv = ug_ref[...], uv_ref[...]
        wg, wv = wg_ref[...], wv_ref[...]
        yg = _conv(ug, wg, bg_ref[...])
        yv = _conv(uv, wv, bv_ref[...])
        sig = _sigmoid(yg)
        da = jnp.where(valid, da_ref[...], 0.0)
        d_yv = da * (yg * sig)
        d_yg = da * yv * (sig * (1.0 + yg * (1.0 - sig)))
        for idx, (dy, uu, w) in enumerate(((d_yg, ug, wg), (d_yv, uv, wv))):
            du = w[2:3, :] * dy + w[1:2, :] * pltpu.roll(dy, l - 1, 0) + w[0:1, :] * pltpu.roll(dy, l - 2, 0)
            du_ref[idx] = jnp.where(valid, du, 0.0).astype(du_ref.dtype)
            dwb_ref[idx, 0:1, :] = jnp.sum(dy * pltpu.roll(uu, 2, 0), axis=0, keepdims=True)
            dwb_ref[idx, 1:2, :] = jnp.sum(dy * pltpu.roll(uu, 1, 0), axis=0, keepdims=True)
            dwb_ref[idx, 2:3, :] = jnp.sum(dy * uu, axis=0, keepdims=True)
            dwb_ref[idx, 3:4, :] = jnp.sum(dy, axis=0, keepdims=True)
            dwb_ref[idx, 4:8, :] = jnp.zeros((4, tc), F32)

    return pl.pallas_call(
        body, name="conv_act_bwd",
        out_shape=(jax.ShapeDtypeStruct((2, l, d_ff), MXU_DTYPE), jax.ShapeDtypeStruct((2, 8, d_ff), F32)),
        grid=(nt,),
        in_specs=[pl.BlockSpec((l, tc), lambda j: (0, j)), pl.BlockSpec((l, tc), lambda j: (0, j + nt)),
                  pl.BlockSpec((8, tc), lambda j: (0, j)), pl.BlockSpec((8, tc), lambda j: (0, j + nt)),
                  pl.BlockSpec((1, tc), lambda j: (0, j)), pl.BlockSpec((1, tc), lambda j: (0, j + nt)),
                  pl.BlockSpec((l, tc), lambda j: (0, j))],
        out_specs=(pl.BlockSpec((2, l, tc), lambda j: (0, 0, j)), pl.BlockSpec((2, 8, tc), lambda j: (0, 0, j))),
        compiler_params=_params(("parallel",)),
    )(u, u, conv_w, conv_w, conv_b, conv_b, d_act)


def _adamw(w, g, m, v, name):
    shape = w.shape
    if w.ndim == 1:
        as2d = (1, shape[0])
    else:
        as2d = (int(np.prod(shape[:-1])), shape[-1])
    r, c = as2d
    tr = _divisor_tile(r, 256, 8)
    spec = pl.BlockSpec((tr, c), lambda i: (i, 0))

    def body(w_ref, g_ref, m_ref, v_ref, d_ref, nm_ref, nv_ref):
        gv = g_ref[...]
        nm = ADAM_B1 * m_ref[...] + (1.0 - ADAM_B1) * gv
        nv = ADAM_B2 * v_ref[...] + (1.0 - ADAM_B2) * (gv * gv)
        m_hat = nm / (1.0 - ADAM_B1 ** ADAM_STEP)
        v_hat = nv / (1.0 - ADAM_B2 ** ADAM_STEP)
        d_ref[...] = -ADAM_LR * (m_hat / (jnp.sqrt(v_hat) + ADAM_EPS) + ADAM_WD * w_ref[...])
        nm_ref[...] = nm
        nv_ref[...] = nv

    sds = jax.ShapeDtypeStruct(as2d, F32)
    outs = pl.pallas_call(
        body, name=name, out_shape=(sds, sds, sds), grid=(r // tr,),
        in_specs=[spec] * 4, out_specs=(spec,) * 3,
        compiler_params=_params(("parallel",)),
    )(w.reshape(as2d), g.reshape(as2d), m.reshape(as2d), v.reshape(as2d))
    return tuple(o.reshape(shape) for o in outs)


def _pad_rows(a, rows):
    return jnp.pad(a, ((0, rows - a.shape[0]), (0, 0)))


def kernel(x, meta_tokens, norm1_gain, w_in, b_forget, ret_norm_gain, w_out, norm2_gain, w_up, conv_w, conv_b, w_down, final_norm_gain, loss_target, m_meta_tokens, m_norm1_gain, m_w_in, m_b_forget, m_ret_norm_gain, m_w_out, m_norm2_gain, m_w_up, m_conv_w, m_conv_b, m_w_down, m_final_norm_gain, v_meta_tokens, v_norm1_gain, v_w_in, v_b_forget, v_ret_norm_gain, v_w_out, v_norm2_gain, v_w_up, v_conv_w, v_conv_b, v_w_down, v_final_norm_gain):
    seq, d = x.shape[1], x.shape[2]
    l = CHUNK + seq
    d_ff = w_down.shape[1] * N_DEV
    up_shard = w_up.shape[2]
    assert 4 * up_shard == d_ff and w_in.shape[2] == WIN_SHARD and d == 2 * GROUP
    dev = _device_index()
    mx, my, mc = _my_position()
    core = jnp.reshape(mc, (1,)).astype(jnp.int32)
    chip = jnp.reshape(2 * mx + my, (1,)).astype(jnp.int32)
    dev1 = jnp.reshape(dev, (1,)).astype(jnp.int32)

    small = jnp.concatenate([meta_tokens.reshape(-1, 128), conv_w[0].reshape(-1, 128)], axis=0)
    n_meta_rows = N_META * (d // N_DEV) // 128
    small_rows = small.shape[0]
    small_all = _all_gather(_pad_rows(small, -(-small_rows // 8) * 8), "gather_small")
    meta_full = jnp.transpose(small_all[:, :n_meta_rows].reshape(N_DEV, N_META, d // N_DEV), (1, 0, 2)).reshape(N_META, d)
    conv_w_full = _pad_rows(jnp.transpose(small_all[:, n_meta_rows:small_rows].reshape(N_DEV, 3, up_shard),
                                          (1, 0, 2)).reshape(3, 2 * d_ff), 8)
    w_in_padded = jnp.pad(w_in[0], ((0, 0), (0, WIN_BLOCK - WIN_SHARD))).astype(WIRE_DTYPE)
    start_in = _gather_start(w_in_padded, dev1, small_all, "gather_w_in_start")

    h0 = jnp.concatenate([jnp.zeros((PAD_ROWS, d), F32), meta_full, x[0]], axis=0)
    consts = _retention_consts(l)
    bias_row = jnp.pad(b_forget, ((0, 0), (0, 128 - N_HEADS)))
    a = _rmsnorm_fwd(h0, norm1_gain + start_in[4][0, 0], "rmsnorm1")
    w_in_blocks = _gather_finish(start_in, a, "gather_w_in")
    start_out = _gather_start(w_out[0].astype(WIRE_DTYPE), dev1, w_in_blocks, "gather_w_out_start")
    w_in_full = _assemble_w_in(w_in_blocks).astype(MXU_DTYPE)
    proj = _mm_nn(a, w_in_full, F32, "mm_proj", after=start_out[4])
    w_out_blocks = _gather_finish(start_out, proj, "gather_w_out")
    start_up = _gather_start(w_up[0].astype(WIRE_DTYPE), dev1, w_out_blocks, "gather_w_up_start")
    ret_mix, ret_pre, ret_states = _retention_fwd(proj, ret_norm_gain + start_up[4][0, 0], consts)
    cum_bc, cum_rows = _fox_prep(proj, bias_row + start_up[4][0:1, :])
    mix, lse_rows = _fox_fwd(proj, cum_bc, cum_rows, ret_mix)
    w_out_full = w_out_blocks.reshape(d, d).astype(MXU_DTYPE)
    h1, cn = _rmsnorm_fwd(h0, norm2_gain, "resid_rmsnorm2", res=_mm_nn(mix, w_out_full, F32, "mm_out"))
    w_up_blocks = _gather_finish(start_up, cn, "gather_w_up").astype(MXU_DTYPE)
    start_down = _gather_start(w_down[0].astype(WIRE_DTYPE), dev1, w_up_blocks, "gather_w_down_start")
    u = _mm(cn, w_up_blocks,
            a_spec=pl.BlockSpec((_divisor_tile(l, 1088, 16), d), lambda i, j, k: (i, 0)),
            b_spec=pl.BlockSpec((None, d, up_shard), lambda i, j, k: (j, 0, 0)),
            o_spec=pl.BlockSpec((_divisor_tile(l, 1088, 16), up_shard), lambda i, j, k: (i, j)),
            out_shape=jax.ShapeDtypeStruct((l, 2 * d_ff), F32),
            grid=(l // _divisor_tile(l, 1088, 16), N_DEV, 1), contract=(1, 0), nk=1, name="mm_up",
            after=start_down[4])
    act = _conv_act_fwd(u, conv_w_full, conv_b + start_down[4][0, 0], d_ff)
    w_down_full = _gather_finish(start_down, act, "gather_w_down").reshape(d_ff, d).astype(MXU_DTYPE)
    mlp_out = _mm_nn(act, w_down_full, F32, "mm_down", tm_cap=544, tk_cap=d_ff)
    d_h2, d_h2_b, dg_final, loss_part = _loss_head(h1, mlp_out, final_norm_gain.reshape(1, d), loss_target[0])

    gw_down = _mm_tn(act, d_h2_b, WIRE_DTYPE, "mm_gw_down", tm_cap=1408, tn_cap=1024)
    rs_down = _reduce_scatter_start(gw_down.reshape(N_DEV, d_ff // N_DEV, d), core, "rs_w_down")
    d_act = _mm_nt(d_h2_b, w_down_full, F32, "mm_d_act", after=rs_down[4])
    d_u, d_conv = _conv_act_bwd(u, conv_w_full, conv_b + rs_down[4][0, 0], d_act, d_ff)
    tm = _divisor_tile(l, 1088, 16)
    gw_up = _mm(cn, d_u,
                a_spec=pl.BlockSpec((l, d // 2), lambda i, j, k: (0, i)),
                b_spec=pl.BlockSpec((None, l, up_shard), lambda i, j, k: (j // 4, 0, j % 4)),
                o_spec=pl.BlockSpec((None, d // 2, up_shard), lambda i, j, k: (j, i, 0)),
                out_shape=jax.ShapeDtypeStruct((N_DEV, d, up_shard), WIRE_DTYPE),
                grid=(2, N_DEV, 1), contract=(0, 0), nk=1, name="mm_gw_up")
    rs_up = _reduce_scatter_start(gw_up, core, "rs_w_up")
    d_cn = _mm_d_cn(d_u, w_up_blocks, rs_up[4])
    d_h1, d_h1_b, dg_norm2 = _rmsnorm_bwd(d_h2, d_cn, h1, norm2_gain + rs_up[4][0, 0], "rmsnorm2_bwd", True)

    gw_out = _mm_tn(mix, d_h1_b, WIRE_DTYPE, "mm_gw_out")
    rs_out = _reduce_scatter_start(gw_out.reshape(N_DEV, d // N_DEV, d), core, "rs_w_out")
    d_mix = _mm_nt(d_h1_b, w_out_full, F32, "mm_d_mix", after=rs_out[4])
    d_fq, d_fk, d_fv, ds_sum = _fox_bwd(proj, cum_bc, cum_rows, d_mix, lse_rows)
    d_ff_tile, db_forget_row = _fox_gate_bwd(ds_sum, proj, bias_row)
    d_ret, dg_ret = _retention_bwd(proj, ret_pre, ret_states, d_mix, ret_norm_gain + rs_out[4][0, 0], consts)
    d_proj = jnp.concatenate(
        [d_ret, d_fq, d_fk, d_fv, d_ff_tile, jnp.zeros((l, WIN_N - 7 * GROUP - 128), MXU_DTYPE)], axis=1)
    gw_in = _mm_tn(a, d_proj, WIRE_DTYPE, "mm_gw_in")
    rs_in = _reduce_scatter_start(_extract_w_in_windows(gw_in), core, "rs_w_in")
    d_a = _mm_nt(d_proj, w_in_full, F32, "mm_d_a", tm_cap=544, tn_cap=256, tk_cap=WIN_N, after=rs_in[4])
    d_h0, dg_norm1 = _rmsnorm_bwd(d_h1, d_a, h0, norm1_gain + rs_in[4][0, 0], "rmsnorm1_bwd", False)
    grad_x = d_h0[CHUNK:][None]
    d_meta = d_h0[PAD_ROWS:CHUNK]

    d_conv_w = jnp.concatenate([d_conv[0, 0:3], d_conv[1, 0:3]], axis=1)
    d_conv_b = jnp.concatenate([d_conv[0, 3:4], d_conv[1, 3:4]], axis=1)
    pieces = [loss_part[:, 0:1], dg_norm1, db_forget_row[:, 0:N_HEADS], dg_ret, dg_norm2, d_conv_b, dg_final,
              d_meta.reshape(1, -1), d_conv_w.reshape(1, -1)]
    sizes = [p.shape[1] for p in pieces]
    flat = jnp.concatenate(pieces, axis=1)
    padded = -(-flat.shape[1] // 1024) * 1024
    flat = jnp.pad(flat, ((0, 0), (0, padded - flat.shape[1]))).reshape(padded // 128, 128)
    small_ar = _small_all_reduce_start(flat, d_h0, "all_reduce_small")

    g_w_down = _reduce_scatter_finish(rs_down, small_ar[4], chip, "rs_w_down")[None]
    g_w_up = _reduce_scatter_finish(rs_up, g_w_down, chip, "rs_w_up")[None]
    g_w_out = _reduce_scatter_finish(rs_out, g_w_up, chip, "rs_w_out")[None]
    early = [_adamw(w, g, m, v, "adamw_" + n) for w, g, m, v, n in (
        (w_down, g_w_down, m_w_down, v_w_down, "w_down"), (w_up, g_w_up, m_w_up, v_w_up, "w_up"),
        (w_out, g_w_out, m_w_out, v_w_out, "w_out"))]
    g_w_in_window = _reduce_scatter_finish(rs_in, early[1][2], chip, "rs_w_in")
    g_w_in = g_w_in_window[:, :WIN_SHARD][None]
    early.append(_adamw(w_in, g_w_in, m_w_in, v_w_in, "adamw_w_in"))
    total = _small_all_reduce_finish(small_ar, early[3][2], dev1, "all_reduce_small").reshape(1, padded)
    offs = np.concatenate([[0], np.cumsum(sizes)])
    take = lambda k: total[:, int(offs[k]):int(offs[k + 1])]
    loss = take(0).reshape(())
    g_norm1, g_bf, g_ret_gain, g_norm2 = take(1), take(2), take(3), take(4)
    g_conv_b, g_final = take(5), take(6).reshape(d)
    g_meta = lax.dynamic_slice(take(7).reshape(N_META, d), (jnp.int32(0), (dev * (d // N_DEV)).astype(jnp.int32)),
                               (N_META, d // N_DEV))
    g_conv_w = lax.dynamic_slice(take(8).reshape(3, 2 * d_ff), (jnp.int32(0), (dev * up_shard).astype(jnp.int32)),
                                 (3, up_shard))[None]

    weights = [meta_tokens, norm1_gain, w_in, b_forget, ret_norm_gain, w_out, norm2_gain, w_up, conv_w, conv_b,
               w_down, final_norm_gain]
    grads = [g_meta, g_norm1, g_w_in, g_bf, g_ret_gain, g_w_out, g_norm2, g_w_up, g_conv_w, g_conv_b, g_w_down,
             g_final]
    done = {"w_down": early[0], "w_up": early[1], "w_out": early[2], "w_in": early[3]}
    ms = [m_meta_tokens, m_norm1_gain, m_w_in, m_b_forget, m_ret_norm_gain, m_w_out, m_norm2_gain, m_w_up, m_conv_w,
          m_conv_b, m_w_down, m_final_norm_gain]
    vs = [v_meta_tokens, v_norm1_gain, v_w_in, v_b_forget, v_ret_norm_gain, v_w_out, v_norm2_gain, v_w_up, v_conv_w,
          v_conv_b, v_w_down, v_final_norm_gain]
    names = ["meta", "norm1", "w_in", "b_forget", "ret_gain", "w_out", "norm2", "w_up", "conv_w", "conv_b", "w_down",
             "final_gain"]
    deltas, new_ms, new_vs = [], [], []
    for w, g, m, v, n in zip(weights, grads, ms, vs, names):
        dl, nm, nv = done[n] if n in done else _adamw(w, g, m, v, "adamw_" + n)
        deltas.append(dl)
        new_ms.append(nm)
        new_vs.append(nv)
    return (loss, grad_x, *grads, *deltas, *new_ms, *new_vs)
```

```python
import functools

import numpy as np
import jax
import jax.numpy as jnp
from jax import lax
from jax.experimental import pallas as pl
from jax.experimental.pallas import tpu as pltpu

F32 = jnp.float32
MXU_DTYPE = jnp.bfloat16
WIRE_DTYPE = jnp.bfloat16

N_DEV = 8
N_META = 16
CHUNK = 128
PAD_ROWS = CHUNK - N_META
N_HEADS = 8
HEAD_DIM = 128
GROUP = N_HEADS * HEAD_DIM
IN_DIM = 7 * GROUP + N_HEADS
WIN_SHARD = IN_DIM // N_DEV
WIN_BLOCK = 1024
WIN_STRIDE = 896
WIN_N = 7680
ROPE_BASE = 10000.0
NORM_EPS = 1e-6
NEG_BIG = -1e30
ADAM_LR, ADAM_B1, ADAM_B2, ADAM_EPS, ADAM_WD, ADAM_STEP = 0.001, 0.9, 0.999, 1e-08, 0.01, 10
VMEM_LIMIT = 52 * 1024 * 1024
MESH = pl.DeviceIdType.MESH
ANY = pl.BlockSpec(memory_space=pl.ANY)
VMEM_SPEC = pl.BlockSpec(memory_space=pltpu.VMEM)


def _params(sem=None):
    kw = {"vmem_limit_bytes": VMEM_LIMIT}
    if sem is not None:
        kw["dimension_semantics"] = sem
    return pltpu.CompilerParams(**kw)


def _divisor_tile(n, cap, unit):
    if n <= cap:
        return n
    best = None
    for t in range(unit, cap + 1, unit):
        if n % t == 0:
            best = t
    assert best is not None, (n, cap, unit)
    return best


def _my_position():
    return lax.axis_index("x"), lax.axis_index("y"), lax.axis_index("c")


def _device_index():
    x, y, c = _my_position()
    return 4 * x + 2 * y + c


def _all_gather(shard, name):
    r, c = shard.shape

    def body(x_ref, out_ref, send_sems, recv_sems, local_sem):
        mx, my, mc = _my_position()
        me, sibling = (mx, my, mc), (mx, my, 1 - mc)
        chips = [(1 - mx, my), (mx, 1 - my), (1 - mx, 1 - my)]

        def slot(px, py, pc):
            return out_ref.at[4 * px + 2 * py + pc]

        def copy(k, block, to, src=None):
            return pltpu.make_async_remote_copy(
                src_ref=slot(*block) if src is None else src, dst_ref=slot(*block),
                send_sem=send_sems.at[k], recv_sem=recv_sems.at[k], device_id=to, device_id_type=MESH)

        mine = pltpu.make_async_copy(x_ref, slot(*me), local_sem)
        mine.start()
        first = [copy(0, me, sibling, src=x_ref)]
        first += [copy(1 + j, me, (*chip, mc), src=x_ref) for j, chip in enumerate(chips)]
        for cp in first:
            cp.start()
        passed = [copy(4 + j, (*chip, mc), sibling) for j, chip in enumerate(chips)]
        for j, chip in enumerate(chips):
            copy(1 + j, (*chip, mc), me).wait_recv()
            passed[j].start()
        copy(0, sibling, me).wait_recv()
        for j, chip in enumerate(chips):
            copy(4 + j, (*chip, 1 - mc), me).wait_recv()
        for cp in first + passed:
            cp.wait_send()
        mine.wait()

    return pl.pallas_call(
        body, name=name,
        out_shape=jax.ShapeDtypeStruct((N_DEV, r, c), shard.dtype),
        in_specs=[ANY], out_specs=ANY,
        scratch_shapes=[pltpu.SemaphoreType.DMA((7,)), pltpu.SemaphoreType.DMA((7,)), pltpu.SemaphoreType.DMA],
    )(shard)


HBM_SPEC = pl.BlockSpec(memory_space=pltpu.HBM)
SEM_SPEC = pl.BlockSpec(memory_space=pltpu.SEMAPHORE)
DATAFLOW_EFFECT = pltpu.SideEffectType.DATAFLOW_SIDE_EFFECTING


def _in_hbm(a):
    return pltpu.with_memory_space_constraint(a, pltpu.HBM)


def _split_start(src, land, make_copies, n_copies, after, name):
    if isinstance(land, tuple):
        land = lax.empty(land, src.dtype)
    land_shape = land.shape
    def body(src_ref, land_ref, after_ref, send_sems, recv_sems, src_thru, land_thru, token):
        for cp in make_copies(src_ref, land_ref, send_sems, recv_sems):
            cp.start()
        token[...] = jnp.zeros_like(token)

    return pl.pallas_call(
        body, name=name,
        out_shape=(pltpu.SemaphoreType.DMA((n_copies,)), pltpu.SemaphoreType.DMA((n_copies,)),
                   pltpu.HBM(src.shape, src.dtype), pltpu.HBM(land_shape, src.dtype),
                   jax.ShapeDtypeStruct((8, 128), F32)),
        in_specs=(HBM_SPEC, HBM_SPEC, ANY), out_specs=(SEM_SPEC, SEM_SPEC, HBM_SPEC, HBM_SPEC, VMEM_SPEC),
        input_output_aliases={0: 2, 1: 3},
        compiler_params=pltpu.CompilerParams(has_side_effects=DATAFLOW_EFFECT),
    )(_in_hbm(src), _in_hbm(land), after)


def _split_wait(started, after, make_copies, name):
    send_sems, recv_sems, src_thru, land_thru, _ = started

    def body(src_ref, land_ref, send_sems_ref, recv_sems_ref, after_ref, src_dead, land_out):
        for cp in make_copies(src_ref, land_ref, send_sems_ref, recv_sems_ref):
            cp.wait_send()
            cp.wait_recv()

    return pl.pallas_call(
        body, name=name,
        out_shape=(pltpu.HBM(src_thru.shape, src_thru.dtype), pltpu.HBM(land_thru.shape, land_thru.dtype)),
        in_specs=(HBM_SPEC, HBM_SPEC, SEM_SPEC, SEM_SPEC, ANY), out_specs=(HBM_SPEC, HBM_SPEC),
        input_output_aliases={0: 0, 1: 1},
        compiler_params=pltpu.CompilerParams(has_side_effects=DATAFLOW_EFFECT),
    )(src_thru, land_thru, send_sems, recv_sems, after)


def _gather_copies(x_ref, land_ref, send_sems, recv_sems):
    mx, my, mc = _my_position()
    me = 4 * mx + 2 * my + mc
    targets = [(mx, my, 1 - mc), (1 - mx, my, mc), (mx, 1 - my, mc), (1 - mx, 1 - my, mc)]
    return [pltpu.make_async_remote_copy(
        src_ref=x_ref, dst_ref=land_ref.at[me], send_sem=send_sems.at[k], recv_sem=recv_sems.at[k],
        device_id=t, device_id_type=MESH) for k, t in enumerate(targets)]


def _gather_start(shard, dev, after, name):
    r, c = shard.shape
    tr = _divisor_tile(r, 512, 16)

    def body(s_ref, x_ref, o_ref):
        o_ref[...] = x_ref[...]

    land = pl.pallas_call(
        body, name=name + "_own",
        out_shape=jax.ShapeDtypeStruct((N_DEV, r, c), shard.dtype),
        grid_spec=pltpu.PrefetchScalarGridSpec(
            num_scalar_prefetch=1, grid=(r // tr,),
            in_specs=[pl.BlockSpec((tr, c), lambda i, s: (i, 0))],
            out_specs=pl.BlockSpec((None, tr, c), lambda i, s: (s[0], i, 0))),
        compiler_params=_params(("parallel",)),
    )(dev, shard)
    return _split_start(shard, land, _gather_copies, 4, after, name)


def _gather_finish(started, after, name):
    _, land = _split_wait(started, after, _gather_copies, name + "_wait")

    def body(land_in, land_ref, send_sems, recv_sems):
        mx, my, mc = _my_position()
        chips = [(1 - mx, my), (mx, 1 - my), (1 - mx, 1 - my)]
        copies = [pltpu.make_async_remote_copy(
            src_ref=land_ref.at[4 * cx + 2 * cy + mc], dst_ref=land_ref.at[4 * cx + 2 * cy + mc],
            send_sem=send_sems.at[j], recv_sem=recv_sems.at[j],
            device_id=(mx, my, 1 - mc), device_id_type=MESH) for j, (cx, cy) in enumerate(chips)]
        for cp in copies:
            cp.start()
        for j, (cx, cy) in enumerate(chips):
            copies[j].wait_send()
            pltpu.make_async_remote_copy(
                src_ref=land_ref.at[4 * cx + 2 * cy + 1 - mc], dst_ref=land_ref.at[4 * cx + 2 * cy + 1 - mc],
                send_sem=send_sems.at[j], recv_sem=recv_sems.at[j],
                device_id=(mx, my, 1 - mc), device_id_type=MESH).wait_recv()

    return pl.pallas_call(
        body, name=name + "_pass",
        out_shape=jax.ShapeDtypeStruct(land.shape, land.dtype),
        in_specs=[ANY], out_specs=ANY,
        input_output_aliases={0: 0},
        scratch_shapes=[pltpu.SemaphoreType.DMA((3,)), pltpu.SemaphoreType.DMA((3,))],
    )(land)


def _chip_copies(p_ref, land_ref, send_sems, recv_sems):
    mx, my, mc = _my_position()
    chips = [(1 - mx, my), (mx, 1 - my), (1 - mx, 1 - my)]
    return [pltpu.make_async_remote_copy(
        src_ref=p_ref.at[2 * cx + cy], dst_ref=land_ref.at[j], send_sem=send_sems.at[j], recv_sem=recv_sems.at[j],
        device_id=(cx, cy, mc), device_id_type=MESH) for j, (cx, cy) in enumerate(chips)]


def _reduce_scatter_start(g, core, name):
    pair = _pair_sum(g, _exchange_sibling(g, name + "_d2d"), core, name + "_pairsum")
    return _split_start(pair, (3,) + pair.shape[1:], _chip_copies, 3, g, name + "_ici_start")


def _sibling_copies(g_ref, land_ref, send_sems, recv_sems):
    mx, my, mc = _my_position()
    return [pltpu.make_async_remote_copy(
        src_ref=g_ref.at[2 * k + (1 - mc)], dst_ref=land_ref.at[k], send_sem=send_sems.at[k], recv_sem=recv_sems.at[k],
        device_id=(mx, my, 1 - mc), device_id_type=MESH) for k in range(4)]


def _reduce_scatter_d2d_start(g, after, name):
    return _split_start(g, (4,) + g.shape[1:], _sibling_copies, 4, after, name + "_d2d_start")


def _reduce_scatter_ici_start(d2d_started, after, core, name):
    g, from_sibling = _split_wait(d2d_started, after, _sibling_copies, name + "_d2d_wait")
    pair = _pair_sum(g, from_sibling, core, name + "_pairsum")
    return _split_start(pair, (3,) + pair.shape[1:], _chip_copies, 3, g, name + "_ici_start")


def _reduce_scatter_finish(started, after, chip, name):
    pair, from_chips = _split_wait(started, after, _chip_copies, name + "_ici_wait")
    return _final_sum(pair, from_chips, chip, name + "_sum")


def _exchange_sibling(g, name):
    _, r, c = g.shape

    def body(g_ref, out_ref, send_sems, recv_sems):
        mx, my, mc = _my_position()
        copies = [
            pltpu.make_async_remote_copy(
                src_ref=g_ref.at[2 * k + (1 - mc)], dst_ref=out_ref.at[k],
                send_sem=send_sems.at[k], recv_sem=recv_sems.at[k],
                device_id=(mx, my, 1 - mc), device_id_type=MESH)
            for k in range(4)]
        for cp in copies:
            cp.start()
        for cp in copies:
            cp.wait()

    return pl.pallas_call(
        body, name=name,
        out_shape=jax.ShapeDtypeStruct((4, r, c), g.dtype),
        in_specs=[ANY], out_specs=ANY,
        scratch_shapes=[pltpu.SemaphoreType.DMA((4,)), pltpu.SemaphoreType.DMA((4,))],
    )(g)


def _pair_sum(g, recv, core, name):
    _, r, c = g.shape
    tr = _divisor_tile(r, 512, 16)

    def body(s_ref, g_ref, r_ref, o_ref):
        o_ref[...] = (g_ref[...].astype(F32) + r_ref[...].astype(F32)).astype(o_ref.dtype)

    return pl.pallas_call(
        body, name=name,
        out_shape=jax.ShapeDtypeStruct((4, r, c), g.dtype),
        grid_spec=pltpu.PrefetchScalarGridSpec(
            num_scalar_prefetch=1, grid=(4, r // tr),
            in_specs=[pl.BlockSpec((None, tr, c), lambda k, i, s: (2 * k + s[0], i, 0)),
                      pl.BlockSpec((None, tr, c), lambda k, i, s: (k, i, 0))],
            out_specs=pl.BlockSpec((None, tr, c), lambda k, i, s: (k, i, 0))),
        compiler_params=_params(("parallel", "parallel")),
    )(core, g, recv)


def _final_sum(p, recv, chip, name):
    _, r, c = p.shape
    tr = _divisor_tile(r, 512, 16)

    def body(s_ref, p_ref, r_ref, o_ref):
        acc = p_ref[...].astype(F32)
        for j in range(3):
            acc = acc + r_ref[j].astype(F32)
        o_ref[...] = acc

    return pl.pallas_call(
        body, name=name,
        out_shape=jax.ShapeDtypeStruct((r, c), F32),
        grid_spec=pltpu.PrefetchScalarGridSpec(
            num_scalar_prefetch=1, grid=(r // tr,),
            in_specs=[pl.BlockSpec((None, tr, c), lambda i, s: (s[0], i, 0)),
                      pl.BlockSpec((3, tr, c), lambda i, s: (0, i, 0))],
            out_specs=pl.BlockSpec((tr, c), lambda i, s: (i, 0))),
        compiler_params=_params(("parallel",)),
    )(chip, p, recv)


def _all_to_all_copies(v_ref, land_ref, send_sems, recv_sems):
    mx, my, mc = _my_position()
    me = 4 * mx + 2 * my + mc
    copies = []
    for rel in range(1, N_DEV):
        bx, by, bc = (rel >> 2) & 1, (rel >> 1) & 1, rel & 1
        target = (1 - mx if bx else mx, 1 - my if by else my, 1 - mc if bc else mc)
        copies.append(pltpu.make_async_remote_copy(
            src_ref=v_ref, dst_ref=land_ref.at[me], send_sem=send_sems.at[rel - 1], recv_sem=recv_sems.at[rel - 1],
            device_id=target, device_id_type=MESH))
    return copies


def _small_all_reduce_start(v, after, name):
    return _split_start(v, (N_DEV,) + v.shape, _all_to_all_copies, N_DEV - 1, after, name + "_start")


def _small_all_reduce_finish(started, after, dev, name):
    v, land = _split_wait(started, after, _all_to_all_copies, name + "_wait")
    rows = v.shape[0]

    def body(me_ref, v_ref, land_ref, o_ref):
        for j in range(N_DEV):
            @pl.when(me_ref[0] == j)
            def _():
                o_ref[...] = v_ref[...] if j == 0 else o_ref[...] + v_ref[...]

            @pl.when(me_ref[0] != j)
            def _():
                o_ref[...] = land_ref[j] if j == 0 else o_ref[...] + land_ref[j]

    return pl.pallas_call(
        body, name=name + "_sum",
        out_shape=jax.ShapeDtypeStruct((rows, 128), F32),
        grid_spec=pltpu.PrefetchScalarGridSpec(
            num_scalar_prefetch=1, grid=(1,),
            in_specs=[pl.BlockSpec((rows, 128), lambda i, s: (0, 0)),
                      pl.BlockSpec((N_DEV, rows, 128), lambda i, s: (0, 0, 0))],
            out_specs=pl.BlockSpec((rows, 128), lambda i, s: (0, 0))),
        compiler_params=_params(("arbitrary",)),
    )(dev, v, land)


def _assemble_w_in(blocks):
    _, d, _ = blocks.shape
    tr = _divisor_tile(d, 128, 16)
    n_tiles = WIN_N // 128
    last = (N_DEV * WIN_STRIDE) // 128

    def body(b_ref, o_ref):
        win = []
        for i in range(N_DEV):
            w = b_ref[i].astype(F32)
            win.append(pltpu.roll(w, i, 1) if i else w)
        for t in range(n_tiles):
            if t > last:
                o_ref[:, t * 128:(t + 1) * 128] = jnp.zeros((tr, 128), o_ref.dtype)
                continue
            i = min(t // 7, N_DEV - 1)
            k = t - 7 * i
            val = win[i][:, k * 128:(k + 1) * 128]
            if k == 0 and i >= 1:
                val = val + win[i - 1][:, 7 * 128:8 * 128]
            o_ref[:, t * 128:(t + 1) * 128] = val.astype(o_ref.dtype)

    return pl.pallas_call(
        body, name="assemble_w_in",
        out_shape=jax.ShapeDtypeStruct((d, WIN_N), blocks.dtype),
        grid=(d // tr,),
        in_specs=[pl.BlockSpec((N_DEV, tr, WIN_BLOCK), lambda i: (0, i, 0))],
        out_specs=pl.BlockSpec((tr, WIN_N), lambda i: (i, 0)),
        compiler_params=_params(("parallel",)),
    )(blocks)


def _extract_w_in_windows(g):
    d, _ = g.shape
    tr = _divisor_tile(d, 128, 16)

    def body(g_ref, o_ref):
        for j in range(N_DEV):
            w = g_ref[:, WIN_STRIDE * j:WIN_STRIDE * j + WIN_BLOCK].astype(F32)
            o_ref[j] = (pltpu.roll(w, WIN_BLOCK - j, 1) if j else w).astype(o_ref.dtype)

    return pl.pallas_call(
        body, name="extract_w_in_windows",
        out_shape=jax.ShapeDtypeStruct((N_DEV, d, WIN_BLOCK), g.dtype),
        grid=(d // tr,),
        in_specs=[pl.BlockSpec((tr, WIN_N), lambda i: (i, 0))],
        out_specs=pl.BlockSpec((N_DEV, tr, WIN_BLOCK), lambda i: (0, i, 0)),
        compiler_params=_params(("parallel",)),
    )(g)


def _mm(a, b, *, a_spec, b_spec, o_spec, out_shape, grid, contract, nk, name, after=None):
    dn = (((contract[0],), (contract[1],)), ((), ()))
    tm, tn = o_spec.block_shape[-2:]
    behind = [] if after is None else [after]

    def body(a_ref, b_ref, *rest):
        o_ref, *scratch = rest[len(behind):]
        part = lax.dot_general(a_ref[...], b_ref[...], dn, preferred_element_type=F32)
        if nk == 1:
            o_ref[...] = part.astype(o_ref.dtype)
            return
        acc = scratch[0]
        k = pl.program_id(2)

        @pl.when(k == 0)
        def _():
            acc[...] = part

        @pl.when(k > 0)
        def _():
            acc[...] += part

        @pl.when(k == nk - 1)
        def _():
            o_ref[...] = acc[...].astype(o_ref.dtype)

    return pl.pallas_call(
        body, name=name, out_shape=out_shape, grid=grid,
        in_specs=[a_spec, b_spec] + [ANY] * len(behind), out_specs=o_spec,
        scratch_shapes=[] if nk == 1 else [pltpu.VMEM((tm, tn), F32)],
        compiler_params=_params(("parallel", "parallel", "arbitrary")),
    )(a, b, *behind)


def _mm_nn(a, b, out_dtype, name, tm_cap=1088, tn_cap=512, tk_cap=2048, after=None):
    m, k = a.shape
    _, n = b.shape
    tm, tn, tk = _divisor_tile(m, tm_cap, 16), _divisor_tile(n, tn_cap, 128), _divisor_tile(k, tk_cap, 128)
    return _mm(a, b,
               a_spec=pl.BlockSpec((tm, tk), lambda i, j, kk: (i, kk)),
               b_spec=pl.BlockSpec((tk, tn), lambda i, j, kk: (kk, j)),
               o_spec=pl.BlockSpec((tm, tn), lambda i, j, kk: (i, j)),
               out_shape=jax.ShapeDtypeStruct((m, n), out_dtype),
               grid=(m // tm, n // tn, k // tk), contract=(1, 0), nk=k // tk, name=name, after=after)


def _mm_nt(a, b, out_dtype, name, tm_cap=1088, tn_cap=512, tk_cap=2048, after=None):
    m, k = a.shape
    n, _ = b.shape
    tm, tn, tk = _divisor_tile(m, tm_cap, 16), _divisor_tile(n, tn_cap, 128), _divisor_tile(k, tk_cap, 128)
    return _mm(a, b,
               a_spec=pl.BlockSpec((tm, tk), lambda i, j, kk: (i, kk)),
               b_spec=pl.BlockSpec((tn, tk), lambda i, j, kk: (j, kk)),
               o_spec=pl.BlockSpec((tm, tn), lambda i, j, kk: (i, j)),
               out_shape=jax.ShapeDtypeStruct((m, n), out_dtype),
               grid=(m // tm, n // tn, k // tk), contract=(1, 1), nk=k // tk, name=name, after=after)


def _mm_tn(a, b, out_dtype, name, tm_cap=1024, tn_cap=512, after=None):
    l, m = a.shape
    _, n = b.shape
    tm, tn = _divisor_tile(m, tm_cap, 128), _divisor_tile(n, tn_cap, 128)
    return _mm(a, b,
               a_spec=pl.BlockSpec((l, tm), lambda i, j, kk: (0, i)),
               b_spec=pl.BlockSpec((l, tn), lambda i, j, kk: (0, j)),
               o_spec=pl.BlockSpec((tm, tn), lambda i, j, kk: (i, j)),
               out_shape=jax.ShapeDtypeStruct((m, n), out_dtype),
               grid=(m // tm, n // tn, 1), contract=(0, 0), nk=1, name=name, after=after)


def _mm_d_cn(d_u, w_up_blocks, after):
    _, l, d_ff = d_u.shape
    n, d, shard = w_up_blocks.shape
    per = d_ff // shard
    tm, tn = _divisor_tile(l, 544, 16), _divisor_tile(d, 256, 128)

    def body(a_ref, b_ref, after_ref, o_ref):
        acc = None
        for k in range(n):
            part = _dot_nt(a_ref[k // per, :, (k % per) * shard:(k % per + 1) * shard], b_ref[k])
            acc = part if acc is None else acc + part
        o_ref[...] = acc

    return pl.pallas_call(
        body, name="mm_d_cn", out_shape=jax.ShapeDtypeStruct((l, d), F32), grid=(l // tm, d // tn),
        in_specs=[pl.BlockSpec((2, tm, d_ff), lambda i, j: (0, i, 0)),
                  pl.BlockSpec((n, tn, shard), lambda i, j: (0, j, 0)), ANY],
        out_specs=pl.BlockSpec((tm, tn), lambda i, j: (i, j)),
        compiler_params=_params(("parallel", "parallel")),
    )(d_u, w_up_blocks, after)


def _row_tile(l):
    return _divisor_tile(l, 544, 8)


def _rmsnorm_fwd(h, gain, name, res=None):
    l, d = h.shape
    tr = _row_tile(l)
    row = pl.BlockSpec((tr, d), lambda i: (i, 0))
    vec = pl.BlockSpec((1, d), lambda i: (0, 0))

    def body(*refs):
        if res is None:
            h_ref, g_ref, n_ref = refs
            x = h_ref[...]
        else:
            h_ref, r_ref, g_ref, s_ref, n_ref = refs
            x = h_ref[...] + r_ref[...]
            s_ref[...] = x
        y = x * lax.rsqrt(jnp.mean(x * x, axis=-1, keepdims=True) + NORM_EPS)
        n_ref[...] = (y * g_ref[...]).astype(n_ref.dtype)

    normed = jax.ShapeDtypeStruct((l, d), MXU_DTYPE)
    if res is None:
        return pl.pallas_call(body, name=name, out_shape=normed, grid=(l // tr,), in_specs=[row, vec],
                              out_specs=row, compiler_params=_params(("parallel",)))(h, gain)
    return pl.pallas_call(body, name=name, out_shape=(jax.ShapeDtypeStruct((l, d), F32), normed),
                          grid=(l // tr,), in_specs=[row, row, vec], out_specs=(row, row),
                          compiler_params=_params(("parallel",)))(h, res, gain)


def _rmsnorm_bwd(d_res, d_normed, x, gain, name, with_mxu_copy):
    l, d = x.shape
    tr = _row_tile(l)
    row = pl.BlockSpec((tr, d), lambda i: (i, 0))
    vec = pl.BlockSpec((1, d), lambda i: (0, 0))

    def body(dres_ref, dn_ref, x_ref, g_ref, dx_ref, *rest):
        dg_ref = rest[-1]
        xv = x_ref[...]
        r = lax.rsqrt(jnp.mean(xv * xv, axis=-1, keepdims=True) + NORM_EPS)
        xh = xv * r
        dn = dn_ref[...]
        dxh = dn * g_ref[...]
        dx = dres_ref[...] + r * (dxh - xh * jnp.mean(dxh * xh, axis=-1, keepdims=True))
        dx_ref[...] = dx
        if with_mxu_copy:
            rest[0][...] = dx.astype(MXU_DTYPE)

        @pl.when(pl.program_id(0) == 0)
        def _():
            dg_ref[...] = jnp.zeros_like(dg_ref)

        dg_ref[...] += jnp.sum(dn * xh, axis=0, keepdims=True)

    outs = [jax.ShapeDtypeStruct((l, d), F32)]
    specs = [row]
    if with_mxu_copy:
        outs.append(jax.ShapeDtypeStruct((l, d), MXU_DTYPE))
        specs.append(row)
    outs.append(jax.ShapeDtypeStruct((1, d), F32))
    specs.append(vec)
    return pl.pallas_call(body, name=name, out_shape=tuple(outs), grid=(l // tr,),
                          in_specs=[row, row, row, vec], out_specs=tuple(specs),
                          compiler_params=_params(("arbitrary",)))(d_res, d_normed, x, gain)


def _loss_head(h1, mlp_out, gain, target):
    l, d = h1.shape
    n_blocks = l // CHUNK
    row = pl.BlockSpec((CHUNK, d), lambda i: (i, 0))
    vec = pl.BlockSpec((1, d), lambda i: (0, 0))
    tgt = pl.BlockSpec((CHUNK, d), lambda i: (jnp.maximum(i - 1, 0), 0))

    def body(h_ref, m_ref, g_ref, t_ref, dh_ref, dhb_ref, dg_ref, loss_ref, sq_ref):
        i = pl.program_id(0)
        x = h_ref[...] + m_ref[...]
        r = lax.rsqrt(jnp.mean(x * x, axis=-1, keepdims=True) + NORM_EPS)
        xh = x * r
        g = g_ref[...]
        real = i >= 1
        err = jnp.where(real, xh * g - t_ref[...], 0.0)
        dy = err * (1.0 / d)
        dxh = dy * g
        dh = r * (dxh - xh * jnp.mean(dxh * xh, axis=-1, keepdims=True))
        dh_ref[...] = dh
        dhb_ref[...] = dh.astype(MXU_DTYPE)

        @pl.when(i == 0)
        def _():
            dg_ref[...] = jnp.zeros_like(dg_ref)
            sq_ref[...] = jnp.zeros_like(sq_ref)

        dg_ref[...] += jnp.sum(dy * xh, axis=0, keepdims=True)
        sq_ref[...] += jnp.sum(err * err, axis=0, keepdims=True)

        @pl.when(i == n_blocks - 1)
        def _():
            total = jnp.sum(sq_ref[...], axis=-1, keepdims=True) * (0.5 / d)
            loss_ref[...] = jnp.broadcast_to(total, (1, 128))

    return pl.pallas_call(
        body, name="loss_head",
        out_shape=(jax.ShapeDtypeStruct((l, d), F32), jax.ShapeDtypeStruct((l, d), MXU_DTYPE),
                   jax.ShapeDtypeStruct((1, d), F32), jax.ShapeDtypeStruct((1, 128), F32)),
        grid=(n_blocks,), in_specs=[row, row, vec, tgt],
        out_specs=(row, row, vec, pl.BlockSpec((1, 128), lambda i: (0, 0))),
        scratch_shapes=[pltpu.VMEM((1, d), F32)],
        compiler_params=_params(("arbitrary",)),
    )(h1, mlp_out, gain, target)


def _dot(a, b):
    return jnp.dot(a, b, preferred_element_type=F32)


def _dot_nt(a, b):
    return lax.dot_general(a, b, (((1,), (1,)), ((), ())), preferred_element_type=F32)


def _dot_tn(a, b):
    return lax.dot_general(a, b, (((0,), (0,)), ((), ())), preferred_element_type=F32)


def _rope(t, cos2, sin2):
    return t * cos2 + pltpu.roll(t, HEAD_DIM // 2, 1) * sin2


def _rope_bwd(dr, cos2, sin2):
    return dr * cos2 + pltpu.roll(dr * sin2, HEAD_DIM // 2, 1)


def _sigmoid(x):
    return 1.0 / (1.0 + jnp.exp(-x))


def _row_valid(block, rows):
    r = block * CHUNK + lax.broadcasted_iota(jnp.int32, (rows, 1), 0)
    return r >= PAD_ROWS


def _retention_consts(l):
    pos = jnp.arange(l, dtype=F32) - PAD_ROWS
    inv_freq = 1.0 / (ROPE_BASE ** (jnp.arange(0, HEAD_DIM, 2, dtype=F32) / HEAD_DIM))
    ang = pos[:, None] * inv_freq[None, :]
    cos, sin = jnp.cos(ang), jnp.sin(ang)
    cos2 = jnp.concatenate([cos, cos], axis=-1)
    sin2 = jnp.concatenate([-sin, sin], axis=-1)
    log_g = jnp.log1p(-jnp.exp2(-5.0 - jnp.arange(N_HEADS, dtype=F32)))
    idx = jnp.arange(CHUNK, dtype=F32)
    diff = idx[:, None] - idx[None, :]
    decay = jnp.where(diff >= 0, jnp.exp(jnp.maximum(diff, 0.0)[None] * log_g[:, None, None]), 0.0)
    xi = jnp.exp((idx + 1.0)[None, :] * log_g[:, None])
    zeta = jnp.exp((CHUNK - 1.0 - idx)[None, :] * log_g[:, None])
    g_chunk = jnp.exp(CHUNK * log_g)
    bcast = lambda v: jnp.broadcast_to(v[:, :, None], (N_HEADS, CHUNK, HEAD_DIM))
    g_rows = jnp.broadcast_to(g_chunk[:, None, None], (N_HEADS, 8, HEAD_DIM))
    return cos2, sin2, decay, bcast(xi), bcast(zeta), g_rows


def _retention_fwd(proj, ret_gain, consts):
    l = proj.shape[0]
    n_chunks = l // CHUNK
    cos2, sin2, decay, xi, zeta, g_rows = consts
    scale = HEAD_DIM ** -0.5

    def body(p_ref, cos_ref, sin_ref, dec_ref, xi_ref, zeta_ref, gr_ref, gain_ref,
             mix_ref, o_ref, st_ref, state):
        c = pl.program_id(0)

        @pl.when(c == 0)
        def _():
            state[...] = jnp.zeros_like(state)

        cos_v, sin_v = cos_ref[...], sin_ref[...]
        valid = _row_valid(c, CHUNK)
        for h in range(N_HEADS):
            cols = slice(h * HEAD_DIM, (h + 1) * HEAD_DIM)
            q = p_ref[:, h * HEAD_DIM:(h + 1) * HEAD_DIM]
            k = p_ref[:, GROUP + h * HEAD_DIM:GROUP + (h + 1) * HEAD_DIM]
            v = p_ref[:, 2 * GROUP + h * HEAD_DIM:2 * GROUP + (h + 1) * HEAD_DIM]
            g = p_ref[:, 3 * GROUP + h * HEAD_DIM:3 * GROUP + (h + 1) * HEAD_DIM]
            rq = _rope(q, cos_v, sin_v).astype(MXU_DTYPE)
            rk = _rope(k, cos_v, sin_v) * scale
            rkb = rk.astype(MXU_DTYPE)
            vb = v.astype(MXU_DTYPE)
            st = state[h]
            st_ref[h] = st
            s = _dot_nt(rq, rkb) * dec_ref[h]
            o = _dot(s.astype(MXU_DTYPE), vb) + _dot(rq, st.astype(MXU_DTYPE)) * xi_ref[h]
            kz = (rk * zeta_ref[h]).astype(MXU_DTYPE)
            state[h] = gr_ref[h, 0:1, :] * st + _dot_tn(kz, vb)
            o_ref[:, cols] = o
            mu = jnp.mean(o, axis=-1, keepdims=True)
            oc = o - mu
            yn = oc * lax.rsqrt(jnp.mean(oc * oc, axis=-1, keepdims=True) + NORM_EPS)
            ret = (g * _sigmoid(g)) * (yn * gain_ref[:, cols])
            mix_ref[:, cols] = jnp.where(valid, ret, 0.0).astype(mix_ref.dtype)

    head_tab = pl.BlockSpec((N_HEADS, CHUNK, HEAD_DIM), lambda c: (0, 0, 0))
    return pl.pallas_call(
        body, name="retention_fwd",
        out_shape=(jax.ShapeDtypeStruct((l, 2 * GROUP), MXU_DTYPE), jax.ShapeDtypeStruct((l, GROUP), F32),
                   jax.ShapeDtypeStruct((n_chunks, N_HEADS, HEAD_DIM, HEAD_DIM), F32)),
        grid=(n_chunks,),
        in_specs=[pl.BlockSpec((CHUNK, 4 * GROUP), lambda c: (c, 0)),
                  pl.BlockSpec((CHUNK, HEAD_DIM), lambda c: (c, 0)),
                  pl.BlockSpec((CHUNK, HEAD_DIM), lambda c: (c, 0)),
                  head_tab, head_tab, head_tab,
                  pl.BlockSpec((N_HEADS, 8, HEAD_DIM), lambda c: (0, 0, 0)),
                  pl.BlockSpec((1, GROUP), lambda c: (0, 0))],
        out_specs=(pl.BlockSpec((CHUNK, GROUP), lambda c: (c, 0)),
                   pl.BlockSpec((CHUNK, GROUP), lambda c: (c, 0)),
                   pl.BlockSpec((None, N_HEADS, HEAD_DIM, HEAD_DIM), lambda c: (c, 0, 0, 0))),
        scratch_shapes=[pltpu.VMEM((N_HEADS, HEAD_DIM, HEAD_DIM), F32)],
        compiler_params=_params(("arbitrary",)),
    )(proj, cos2, sin2, decay, xi, zeta, g_rows, ret_gain)


def _retention_bwd(proj, o_pre, states, d_mix, ret_gain, consts):
    l = proj.shape[0]
    n_chunks = l // CHUNK
    cos2, sin2, decay, xi, zeta, g_rows = consts
    scale = HEAD_DIM ** -0.5
    rev = lambda c: n_chunks - 1 - c

    def body(p_ref, o_ref, st_ref, dm_ref, cos_ref, sin_ref, dec_ref, xi_ref, zeta_ref, gr_ref, gain_ref,
             dp_ref, dgain_ref, dstate):
        step = pl.program_id(0)

        @pl.when(step == 0)
        def _():
            dstate[...] = jnp.zeros_like(dstate)
            dgain_ref[...] = jnp.zeros_like(dgain_ref)

        cos_v, sin_v = cos_ref[...], sin_ref[...]
        valid = _row_valid(rev(step), CHUNK)
        for h in range(N_HEADS):
            cols = slice(h * HEAD_DIM, (h + 1) * HEAD_DIM)
            q = p_ref[:, h * HEAD_DIM:(h + 1) * HEAD_DIM]
            k = p_ref[:, GROUP + h * HEAD_DIM:GROUP + (h + 1) * HEAD_DIM]
            v = p_ref[:, 2 * GROUP + h * HEAD_DIM:2 * GROUP + (h + 1) * HEAD_DIM]
            g = p_ref[:, 3 * GROUP + h * HEAD_DIM:3 * GROUP + (h + 1) * HEAD_DIM]
            o = o_ref[:, cols]
            gain = gain_ref[:, cols]
            d_ret = jnp.where(valid, dm_ref[:, cols], 0.0)
            mu = jnp.mean(o, axis=-1, keepdims=True)
            oc = o - mu
            rstd = lax.rsqrt(jnp.mean(oc * oc, axis=-1, keepdims=True) + NORM_EPS)
            yn = oc * rstd
            sig = _sigmoid(g)
            gate = g * sig
            dgain_ref[:, cols] += jnp.sum(d_ret * gate * yn, axis=0, keepdims=True)
            d_g = d_ret * (yn * gain) * (sig * (1.0 + g * (1.0 - sig)))
            d_yn = d_ret * gate * gain
            d_o = rstd * (d_yn - jnp.mean(d_yn, axis=-1, keepdims=True)
                          - yn * jnp.mean(d_yn * yn, axis=-1, keepdims=True))
            rq = _rope(q, cos_v, sin_v)
            rk = _rope(k, cos_v, sin_v) * scale
            rqb, rkb, vb = rq.astype(MXU_DTYPE), rk.astype(MXU_DTYPE), v.astype(MXU_DTYPE)
            dob = d_o.astype(MXU_DTYPE)
            dec = dec_ref[h]
            xi_h, zeta_h = xi_ref[h], zeta_ref[h]
            st_b = st_ref[h].astype(MXU_DTYPE)
            dst = dstate[h]
            dst_b = dst.astype(MXU_DTYPE)
            s_b = (_dot_nt(rqb, rkb) * dec).astype(MXU_DTYPE)
            da_b = (_dot_nt(dob, vb) * dec).astype(MXU_DTYPE)
            doxi_b = (d_o * xi_h).astype(MXU_DTYPE)
            kz_b = (rk * zeta_h).astype(MXU_DTYPE)
            d_rq = _dot(da_b, rkb) + _dot_nt(doxi_b, st_b)
            d_rk = _dot_tn(da_b, rqb) + _dot_nt(vb, dst_b) * zeta_h
            d_v = _dot_tn(s_b, dob) + _dot(kz_b, dst_b)
            dstate[h] = gr_ref[h, 0:1, :] * dst + _dot_tn(rqb, doxi_b)
            d_q = _rope_bwd(d_rq, cos_v, sin_v)
            d_k = _rope_bwd(d_rk * scale, cos_v, sin_v)
            dp_ref[:, h * HEAD_DIM:(h + 1) * HEAD_DIM] = d_q.astype(dp_ref.dtype)
            dp_ref[:, GROUP + h * HEAD_DIM:GROUP + (h + 1) * HEAD_DIM] = d_k.astype(dp_ref.dtype)
            dp_ref[:, 2 * GROUP + h * HEAD_DIM:2 * GROUP + (h + 1) * HEAD_DIM] = d_v.astype(dp_ref.dtype)
            dp_ref[:, 3 * GROUP + h * HEAD_DIM:3 * GROUP + (h + 1) * HEAD_DIM] = d_g.astype(dp_ref.dtype)

    head_tab = pl.BlockSpec((N_HEADS, CHUNK, HEAD_DIM), lambda c: (0, 0, 0))
    return pl.pallas_call(
        body, name="retention_bwd",
        out_shape=(jax.ShapeDtypeStruct((l, 4 * GROUP), MXU_DTYPE), jax.ShapeDtypeStruct((1, GROUP), F32)),
        grid=(n_chunks,),
        in_specs=[pl.BlockSpec((CHUNK, 4 * GROUP), lambda c: (rev(c), 0)),
                  pl.BlockSpec((CHUNK, GROUP), lambda c: (rev(c), 0)),
                  pl.BlockSpec((None, N_HEADS, HEAD_DIM, HEAD_DIM), lambda c: (rev(c), 0, 0, 0)),
                  pl.BlockSpec((CHUNK, GROUP), lambda c: (rev(c), 0)),
                  pl.BlockSpec((CHUNK, HEAD_DIM), lambda c: (rev(c), 0)),
                  pl.BlockSpec((CHUNK, HEAD_DIM), lambda c: (rev(c), 0)),
                  head_tab, head_tab, head_tab,
                  pl.BlockSpec((N_HEADS, 8, HEAD_DIM), lambda c: (0, 0, 0)),
                  pl.BlockSpec((1, GROUP), lambda c: (0, 0))],
        out_specs=(pl.BlockSpec((CHUNK, 4 * GROUP), lambda c: (rev(c), 0)),
                   pl.BlockSpec((1, GROUP), lambda c: (0, 0))),
        scratch_shapes=[pltpu.VMEM((N_HEADS, HEAD_DIM, HEAD_DIM), F32)],
        compiler_params=_params(("arbitrary",)),
    )(proj, o_pre, states, d_mix, cos2, sin2, decay, xi, zeta, g_rows, ret_gain)


FF_TILE = (7 * GROUP) // 128


def _log_forget(ff, bias_row, valid):
    x = ff + bias_row
    e = jnp.exp(-jnp.abs(x))
    lf = jnp.minimum(x, 0.0) - jnp.log(1.0 + e)
    head_lane = lax.broadcasted_iota(jnp.int32, x.shape, 1) < N_HEADS
    keep = lambda t: jnp.where(head_lane, jnp.where(valid, t, 0.0), 0.0)
    return keep(lf), keep(jnp.where(x >= 0, e, 1.0) / (1.0 + e))


def _fox_prep(proj, bias_row):
    l = proj.shape[0]
    n_blocks = l // CHUNK

    def body(ff_ref, b_ref, bc_ref, rows_ref, cum):
        r = lax.broadcasted_iota(jnp.int32, (CHUNK, CHUNK), 0)
        cidx = lax.broadcasted_iota(jnp.int32, (CHUNK, CHUNK), 1)
        tri = jnp.where(r >= cidx, 1.0, 0.0).astype(F32)
        carry = jnp.zeros((1, 128), F32)
        for blk in range(n_blocks):
            rows = slice(blk * CHUNK, (blk + 1) * CHUNK)
            valid = _row_valid(blk, CHUNK)
            lf, _ = _log_forget(ff_ref[rows, :], b_ref[...], valid)
            local = jnp.dot(tri, lf, precision=lax.Precision.HIGHEST, preferred_element_type=F32) + carry
            carry = local[CHUNK - 1:CHUNK, :]
            masked = jnp.where(valid, local, -NEG_BIG)
            cum[rows, :] = masked
            t = masked.T
            for h in range(N_HEADS):
                rows_ref[h, :, rows] = t[h:h + 1, :]
        full = cum[...]
        for h in range(N_HEADS):
            bc_ref[h] = jnp.broadcast_to(full[:, h:h + 1], (l, 128))

    return pl.pallas_call(
        body, name="fox_prep",
        out_shape=(jax.ShapeDtypeStruct((N_HEADS, l, 128), F32), jax.ShapeDtypeStruct((N_HEADS, 1, l), F32)),
        grid=(1,),
        in_specs=[pl.BlockSpec((l, 128), lambda i: (0, FF_TILE)), pl.BlockSpec((1, 128), lambda i: (0, 0))],
        out_specs=(pl.BlockSpec((N_HEADS, l, 128), lambda i: (0, 0, 0)),
                   pl.BlockSpec((N_HEADS, 1, l), lambda i: (0, 0, 0))),
        scratch_shapes=[pltpu.VMEM((l, 128), F32)],
        compiler_params=_params(("arbitrary",)),
    )(proj, bias_row)


ATTN_BLOCK = 2 * CHUNK


def _attn_blocks(l):
    assert (l - CHUNK) % ATTN_BLOCK == 0
    return [(0, CHUNK)] + [(s, ATTN_BLOCK) for s in range(CHUNK, l, ATTN_BLOCK)]


def _rows_valid(start, size):
    return start + lax.broadcasted_iota(jnp.int32, (size, 1), 0) >= PAD_ROWS


def _fox_fwd(proj, cum_bc, cum_rows, mix):
    l = proj.shape[0]
    blocks = _attn_blocks(l)
    scale = HEAD_DIM ** -0.5
    qt, kt, vt = 4 * N_HEADS, 5 * N_HEADS, 6 * N_HEADS

    def body(q_ref, k_ref, v_ref, cbc_ref, crow_ref, mix_in, o_ref, lse_ref, qb_s, kb_s, vb_s):
        qb_s[...] = q_ref[...].astype(MXU_DTYPE)
        kb_s[...] = k_ref[...].astype(MXU_DTYPE)
        vb_s[...] = v_ref[...].astype(MXU_DTYPE)
        for p, (qs, qn) in enumerate(blocks):
            qb = qb_s[qs:qs + qn, :]
            cq = cbc_ref[qs:qs + qn, :]
            m = jnp.full((qn, 1), NEG_BIG, F32)
            lsum = jnp.zeros((qn, 1), F32)
            acc = jnp.zeros((qn, HEAD_DIM), F32)
            for j in range(p + 1):
                ks, kn = blocks[j]
                bias = jnp.tile(cq, (1, kn // CHUNK)) - crow_ref[:, ks:ks + kn]
                s = _dot_nt(qb, kb_s[ks:ks + kn, :]) * scale + bias
                if j == p:
                    q_pos = qs + lax.broadcasted_iota(jnp.int32, (qn, kn), 0)
                    k_pos = ks + lax.broadcasted_iota(jnp.int32, (qn, kn), 1)
                    s = jnp.where(k_pos <= q_pos, s, NEG_BIG)
                m_new = jnp.maximum(m, jnp.max(s, axis=-1, keepdims=True))
                alpha = jnp.exp(m - m_new)
                pr = jnp.exp(s - m_new)
                lsum = lsum * alpha + jnp.sum(pr, axis=-1, keepdims=True)
                acc = acc * alpha + _dot(pr.astype(MXU_DTYPE), vb_s[ks:ks + kn, :])
                m = m_new
            o = jnp.where(_rows_valid(qs, qn), acc * (1.0 / lsum), 0.0)
            o_ref[qs:qs + qn, :] = o.astype(o_ref.dtype)
            lse = m + jnp.log(lsum)
            lse_ref[:, qs:qs + qn] = jnp.broadcast_to(lse, (qn, CHUNK)).T[0:1, :]

    head_col = lambda t: pl.BlockSpec((l, HEAD_DIM), lambda h: (0, t + h))
    return pl.pallas_call(
        body, name="fox_fwd",
        out_shape=(jax.ShapeDtypeStruct(mix.shape, mix.dtype), jax.ShapeDtypeStruct((N_HEADS, 1, l), F32)),
        grid=(N_HEADS,),
        in_specs=[head_col(qt), head_col(kt), head_col(vt),
                  pl.BlockSpec((None, l, 128), lambda h: (h, 0, 0)),
                  pl.BlockSpec((None, 1, l), lambda h: (h, 0, 0)),
                  ANY],
        out_specs=(head_col(N_HEADS), pl.BlockSpec((None, 1, l), lambda h: (h, 0, 0))),
        input_output_aliases={5: 0},
        scratch_shapes=[pltpu.VMEM((l, HEAD_DIM), MXU_DTYPE)] * 3,
        compiler_params=_params(("parallel",)),
    )(proj, proj, proj, cum_bc, cum_rows, mix)


def _fox_bwd(proj, cum_bc, cum_rows, d_mix, lse_rows):
    l = proj.shape[0]
    blocks = _attn_blocks(l)
    scale = HEAD_DIM ** -0.5
    qt, kt, vt = 4 * N_HEADS, 5 * N_HEADS, 6 * N_HEADS

    def body(q_ref, k_ref, v_ref, do_ref, cbc_ref, crow_ref, lse_ref,
             dq_ref, dk_ref, dv_ref, ds_ref, dk_acc, dv_acc, qb_s, kb_s, vb_s, dob_s):
        qb_s[...] = q_ref[...].astype(MXU_DTYPE)
        kb_s[...] = k_ref[...].astype(MXU_DTYPE)
        vb_s[...] = v_ref[...].astype(MXU_DTYPE)
        dob_s[...] = jnp.where(_rows_valid(0, l), do_ref[...], 0.0).astype(MXU_DTYPE)
        dk_acc[...] = jnp.zeros_like(dk_acc)
        dv_acc[...] = jnp.zeros_like(dv_acc)
        ds_ref[...] = jnp.zeros_like(ds_ref)
        shift_row = crow_ref[...] - lse_ref[...]

        for p, (qs, qn) in enumerate(blocks):
            qb, dob = qb_s[qs:qs + qn, :], dob_s[qs:qs + qn, :]
            shift = shift_row[:, qs:qs + qn]

            def probs(j):
                ks, kn = blocks[j]
                ck = jnp.tile(cbc_ref[ks:ks + kn, :], (1, qn // CHUNK))
                s_t = _dot_nt(kb_s[ks:ks + kn, :], qb) * scale + (shift - ck)
                if j == p:
                    k_pos = ks + lax.broadcasted_iota(jnp.int32, (kn, qn), 0)
                    q_pos = qs + lax.broadcasted_iota(jnp.int32, (kn, qn), 1)
                    s_t = jnp.where(k_pos <= q_pos, s_t, NEG_BIG)
                return jnp.exp(s_t), _dot_nt(vb_s[ks:ks + kn, :], dob)

            delta = jnp.zeros((1, qn), F32)
            for j in range(p + 1):
                p_t, dp_t = probs(j)
                delta = delta + jnp.sum(p_t * dp_t, axis=0, keepdims=True)
            dq = jnp.zeros((qn, HEAD_DIM), F32)
            for j in range(p + 1):
                ks, kn = blocks[j]
                rows = slice(ks, ks + kn)
                p_t, dp_t = probs(j)
                ds_t = p_t * (dp_t - delta)
                ds_b = ds_t.astype(MXU_DTYPE)
                dv_acc[rows, :] += _dot(p_t.astype(MXU_DTYPE), dob)
                dk_acc[rows, :] += _dot(ds_b, qb) * scale
                ds_ref[rows, :] += sum(ds_t[:, c:c + CHUNK] for c in range(0, qn, CHUNK))
                dq = dq + _dot_tn(ds_b, kb_s[rows, :])
            dq_ref[qs:qs + qn, :] = (dq * scale).astype(dq_ref.dtype)

        dk_ref[...] = dk_acc[...].astype(dk_ref.dtype)
        dv_ref[...] = dv_acc[...].astype(dv_ref.dtype)

    col = jax.ShapeDtypeStruct((l, GROUP), MXU_DTYPE)
    head_col = lambda t: pl.BlockSpec((l, HEAD_DIM), lambda h: (0, t + h))
    return pl.pallas_call(
        body, name="fox_bwd",
        out_shape=(col, col, col, jax.ShapeDtypeStruct((N_HEADS, l, 128), F32)),
        grid=(N_HEADS,),
        in_specs=[head_col(qt), head_col(kt), head_col(vt), head_col(N_HEADS),
                  pl.BlockSpec((None, l, 128), lambda h: (h, 0, 0)),
                  pl.BlockSpec((None, 1, l), lambda h: (h, 0, 0)),
                  pl.BlockSpec((None, 1, l), lambda h: (h, 0, 0))],
        out_specs=(head_col(0), head_col(0), head_col(0), pl.BlockSpec((None, l, 128), lambda h: (h, 0, 0))),
        scratch_shapes=[pltpu.VMEM((l, HEAD_DIM), F32)] * 2 + [pltpu.VMEM((l, HEAD_DIM), MXU_DTYPE)] * 4,
        compiler_params=_params(("parallel",)),
    )(proj, proj, proj, d_mix, cum_bc, cum_rows, lse_rows)


def _fox_gate_bwd(ds_sum, proj, bias_row):
    l = proj.shape[0]
    n_blocks = l // CHUNK

    def body(ds_ref, ff_ref, b_ref, dff_ref, db_ref):
        r = lax.broadcasted_iota(jnp.int32, (CHUNK, CHUNK), 0)
        cidx = lax.broadcasted_iota(jnp.int32, (CHUNK, CHUNK), 1)
        upper = jnp.where(cidx >= r, 1.0, 0.0).astype(F32)
        carry = jnp.zeros((1, 128), F32)
        db = jnp.zeros((1, 128), F32)
        for blk in reversed(range(n_blocks)):
            rows = slice(blk * CHUNK, (blk + 1) * CHUNK)
            key_sum = jnp.zeros((CHUNK, 128), F32)
            for h in range(N_HEADS):
                select = jnp.where(cidx == h, 1.0, 0.0).astype(F32)
                key_sum = key_sum + jnp.dot(ds_ref[h, rows, :], select, precision=lax.Precision.HIGHEST,
                                            preferred_element_type=F32)
            suffix = jnp.dot(upper, key_sum, precision=lax.Precision.HIGHEST, preferred_element_type=F32) + carry
            carry = suffix[0:1, :]
            _, dsig = _log_forget(ff_ref[rows, :], b_ref[...], _row_valid(blk, CHUNK))
            dff = -suffix * dsig
            dff_ref[rows, :] = dff.astype(dff_ref.dtype)
            db = db + jnp.sum(dff, axis=0, keepdims=True)
        db_ref[...] = db

    return pl.pallas_call(
        body, name="fox_gate_bwd",
        out_shape=(jax.ShapeDtypeStruct((l, 128), MXU_DTYPE), jax.ShapeDtypeStruct((1, 128), F32)),
        grid=(1,),
        in_specs=[pl.BlockSpec((N_HEADS, l, 128), lambda i: (0, 0, 0)),
                  pl.BlockSpec((l, 128), lambda i: (0, FF_TILE)),
                  pl.BlockSpec((1, 128), lambda i: (0, 0))],
        out_specs=(pl.BlockSpec((l, 128), lambda i: (0, 0)), pl.BlockSpec((1, 128), lambda i: (0, 0))),
        compiler_params=_params(("arbitrary",)),
    )(ds_sum, proj, bias_row)


def _conv(u, w, b):
    return b + w[0:1, :] * pltpu.roll(u, 2, 0) + w[1:2, :] * pltpu.roll(u, 1, 0) + w[2:3, :] * u


def _conv_act_fwd(u, conv_w, conv_b, d_ff):
    l = u.shape[0]
    tc = _divisor_tile(d_ff, 256, 128)
    nt = d_ff // tc

    def body(ug_ref, uv_ref, wg_ref, wv_ref, bg_ref, bv_ref, a_ref):
        yg = _conv(ug_ref[...], wg_ref[...], bg_ref[...])
        yv = _conv(uv_ref[...], wv_ref[...], bv_ref[...])
        act = yg * _sigmoid(yg) * yv
        a_ref[...] = jnp.where(_row_valid(0, l), act, 0.0).astype(a_ref.dtype)

    return pl.pallas_call(
        body, name="conv_act_fwd",
        out_shape=jax.ShapeDtypeStruct((l, d_ff), MXU_DTYPE),
        grid=(nt,),
        in_specs=[pl.BlockSpec((l, tc), lambda j: (0, j)), pl.BlockSpec((l, tc), lambda j: (0, j + nt)),
                  pl.BlockSpec((8, tc), lambda j: (0, j)), pl.BlockSpec((8, tc), lambda j: (0, j + nt)),
                  pl.BlockSpec((1, tc), lambda j: (0, j)), pl.BlockSpec((1, tc), lambda j: (0, j + nt))],
        out_specs=pl.BlockSpec((l, tc), lambda j: (0, j)),
        compiler_params=_params(("parallel",)),
    )(u, u, conv_w, conv_w, conv_b, conv_b)


def _conv_act_bwd(u, conv_w, conv_b, d_act, d_ff):
    l = u.shape[0]
    tc = _divisor_tile(d_ff, 256, 128)
    nt = d_ff // tc

    def body(ug_ref, uv_ref, wg_ref, wv_ref, bg_ref, bv_ref, da_ref, du_ref, dwb_ref):
        valid = _row_valid(0, l)
        ug, uv = ug_ref[...], uv_ref[...]
        wg, wv = wg_ref[...], wv_ref[...]
        yg = _conv(ug, wg, bg_ref[...])
        yv = _conv(uv, wv, bv_ref[...])
        sig = _sigmoid(yg)
        da = jnp.where(valid, da_ref[...], 0.0)
        d_yv = da * (yg * sig)
        d_yg = da * yv * (sig * (1.0 + yg * (1.0 - sig)))
        for idx, (dy, uu, w) in enumerate(((d_yg, ug, wg), (d_yv, uv, wv))):
            du = w[2:3, :] * dy + w[1:2, :] * pltpu.roll(dy, l - 1, 0) + w[0:1, :] * pltpu.roll(dy, l - 2, 0)
            du_ref[idx] = jnp.where(valid, du, 0.0).astype(du_ref.dtype)
            dwb_ref[idx, 0:1, :] = jnp.sum(dy * pltpu.roll(uu, 2, 0), axis=0, keepdims=True)
            dwb_ref[idx, 1:2, :] = jnp.sum(dy * pltpu.roll(uu, 1, 0), axis=0, keepdims=True)
            dwb_ref[idx, 2:3, :] = jnp.sum(dy * uu, axis=0, keepdims=True)
            dwb_ref[idx, 3:4, :] = jnp.sum(dy, axis=0, keepdims=True)
            dwb_ref[idx, 4:8, :] = jnp.zeros((4, tc), F32)

    return pl.pallas_call(
        body, name="conv_act_bwd",
        out_shape=(jax.ShapeDtypeStruct((2, l, d_ff), MXU_DTYPE), jax.ShapeDtypeStruct((2, 8, d_ff), F32)),
        grid=(nt,),
        in_specs=[pl.BlockSpec((l, tc), lambda j: (0, j)), pl.BlockSpec((l, tc), lambda j: (0, j + nt)),
                  pl.BlockSpec((8, tc), lambda j: (0, j)), pl.BlockSpec((8, tc), lambda j: (0, j + nt)),
                  pl.BlockSpec((1, tc), lambda j: (0, j)), pl.BlockSpec((1, tc), lambda j: (0, j + nt)),
                  pl.BlockSpec((l, tc), lambda j: (0, j))],
        out_specs=(pl.BlockSpec((2, l, tc), lambda j: (0, 0, j)), pl.BlockSpec((2, 8, tc), lambda j: (0, 0, j))),
        compiler_params=_params(("parallel",)),
    )(u, u, conv_w, conv_w, conv_b, conv_b, d_act)


def _adamw(w, g, m, v, name):
    shape = w.shape
    if w.ndim == 1:
        as2d = (1, shape[0])
    else:
        as2d = (int(np.prod(shape[:-1])), shape[-1])
    r, c = as2d
    tr = _divisor_tile(r, 256, 8)
    spec = pl.BlockSpec((tr, c), lambda i: (i, 0))

    def body(w_ref, g_ref, m_ref, v_ref, d_ref, nm_ref, nv_ref):
        gv = g_ref[...]
        nm = ADAM_B1 * m_ref[...] + (1.0 - ADAM_B1) * gv
        nv = ADAM_B2 * v_ref[...] + (1.0 - ADAM_B2) * (gv * gv)
        m_hat = nm / (1.0 - ADAM_B1 ** ADAM_STEP)
        v_hat = nv / (1.0 - ADAM_B2 ** ADAM_STEP)
        d_ref[...] = -ADAM_LR * (m_hat / (jnp.sqrt(v_hat) + ADAM_EPS) + ADAM_WD * w_ref[...])
        nm_ref[...] = nm
        nv_ref[...] = nv

    sds = jax.ShapeDtypeStruct(as2d, F32)
    outs = pl.pallas_call(
        body, name=name, out_shape=(sds, sds, sds), grid=(r // tr,),
        in_specs=[spec] * 4, out_specs=(spec,) * 3,
        compiler_params=_params(("parallel",)),
    )(w.reshape(as2d), g.reshape(as2d), m.reshape(as2d), v.reshape(as2d))
    return tuple(o.reshape(shape) for o in outs)


def _pad_rows(a, rows):
    return jnp.pad(a, ((0, rows - a.shape[0]), (0, 0)))


def kernel(x, meta_tokens, norm1_gain, w_in, b_forget, ret_norm_gain, w_out, norm2_gain, w_up, conv_w, conv_b, w_down, final_norm_gain, loss_target, m_meta_tokens, m_norm1_gain, m_w_in, m_b_forget, m_ret_norm_gain, m_w_out, m_norm2_gain, m_w_up, m_conv_w, m_conv_b, m_w_down, m_final_norm_gain, v_meta_tokens, v_norm1_gain, v_w_in, v_b_forget, v_ret_norm_gain, v_w_out, v_norm2_gain, v_w_up, v_conv_w, v_conv_b, v_w_down, v_final_norm_gain):
    seq, d = x.shape[1], x.shape[2]
    l = CHUNK + seq
    d_ff = w_down.shape[1] * N_DEV
    up_shard = w_up.shape[2]
    assert 4 * up_shard == d_ff and w_in.shape[2] == WIN_SHARD and d == 2 * GROUP
    dev = _device_index()
    mx, my, mc = _my_position()
    core = jnp.reshape(mc, (1,)).astype(jnp.int32)
    chip = jnp.reshape(2 * mx + my, (1,)).astype(jnp.int32)
    dev1 = jnp.reshape(dev, (1,)).astype(jnp.int32)

    small = jnp.concatenate([meta_tokens.reshape(-1, 128), conv_w[0].reshape(-1, 128)], axis=0)
    n_meta_rows = N_META * (d // N_DEV) // 128
    small_rows = small.shape[0]
    small_all = _all_gather(_pad_rows(small, -(-small_rows // 8) * 8), "gather_small")
    meta_full = jnp.transpose(small_all[:, :n_meta_rows].reshape(N_DEV, N_META, d // N_DEV), (1, 0, 2)).reshape(N_META, d)
    conv_w_full = _pad_rows(jnp.transpose(small_all[:, n_meta_rows:small_rows].reshape(N_DEV, 3, up_shard),
                                          (1, 0, 2)).reshape(3, 2 * d_ff), 8)
    w_in_padded = jnp.pad(w_in[0], ((0, 0), (0, WIN_BLOCK - WIN_SHARD))).astype(WIRE_DTYPE)
    start_in = _gather_start(w_in_padded, dev1, small_all, "gather_w_in_start")

    h0 = jnp.concatenate([jnp.zeros((PAD_ROWS, d), F32), meta_full, x[0]], axis=0)
    consts = _retention_consts(l)
    bias_row = jnp.pad(b_forget, ((0, 0), (0, 128 - N_HEADS)))
    a = _rmsnorm_fwd(h0, norm1_gain + start_in[4][0, 0], "rmsnorm1")
    w_in_blocks = _gather_finish(start_in, a, "gather_w_in")
    start_out = _gather_start(w_out[0].astype(WIRE_DTYPE), dev1, w_in_blocks, "gather_w_out_start")
    w_in_full = _assemble_w_in(w_in_blocks).astype(MXU_DTYPE)
    proj = _mm_nn(a, w_in_full, F32, "mm_proj", after=start_out[4])
    w_out_blocks = _gather_finish(start_out, proj, "gather_w_out")
    start_up = _gather_start(w_up[0].astype(WIRE_DTYPE), dev1, w_out_blocks, "gather_w_up_start")
    ret_mix, ret_pre, ret_states = _retention_fwd(proj, ret_norm_gain + start_up[4][0, 0], consts)
    cum_bc, cum_rows = _fox_prep(proj, bias_row + start_up[4][0:1, :])
    mix, lse_rows = _fox_fwd(proj, cum_bc, cum_rows, ret_mix)
    w_out_full = w_out_blocks.reshape(d, d).astype(MXU_DTYPE)
    h1, cn = _rmsnorm_fwd(h0, norm2_gain, "resid_rmsnorm2", res=_mm_nn(mix, w_out_full, F32, "mm_out"))
    w_up_blocks = _gather_finish(start_up, cn, "gather_w_up").astype(MXU_DTYPE)
    start_down = _gather_start(w_down[0].astype(WIRE_DTYPE), dev1, w_up_blocks, "gather_w_down_start")
    u = _mm(cn, w_up_blocks,
            a_spec=pl.BlockSpec((_divisor_tile(l, 1088, 16), d), lambda i, j, k: (i, 0)),
            b_spec=pl.BlockSpec((None, d, up_shard), lambda i, j, k: (j, 0, 0)),
            o_spec=pl.BlockSpec((_divisor_tile(l, 1088, 16), up_shard), lambda i, j, k: (i, j)),
            out_shape=jax.ShapeDtypeStruct((l, 2 * d_ff), F32),
            grid=(l // _divisor_tile(l, 1088, 16), N_DEV, 1), contract=(1, 0), nk=1, name="mm_up",
            after=start_down[4])
    act = _conv_act_fwd(u, conv_w_full, conv_b + start_down[4][0, 0], d_ff)
    w_down_full = _gather_finish(start_down, act, "gather_w_down").reshape(d_ff, d).astype(MXU_DTYPE)
    mlp_out = _mm_nn(act, w_down_full, F32, "mm_down", tm_cap=544, tk_cap=d_ff)
    d_h2, d_h2_b, dg_final, loss_part = _loss_head(h1, mlp_out, final_norm_gain.reshape(1, d), loss_target[0])

    gw_down = _mm_tn(act, d_h2_b, WIRE_DTYPE, "mm_gw_down", tm_cap=1408, tn_cap=1024)
    d2d_down = _reduce_scatter_d2d_start(gw_down.reshape(N_DEV, d_ff // N_DEV, d), d_h2, "rs_w_down")
    d_act = _mm_nt(d_h2_b, w_down_full, F32, "mm_d_act", after=d2d_down[4])
    rs_down = _reduce_scatter_ici_start(d2d_down, d_act, core, "rs_w_down")
    d_u, d_conv = _conv_act_bwd(u, conv_w_full, conv_b + rs_down[4][0, 0], d_act, d_ff)
    tm = _divisor_tile(l, 1088, 16)
    gw_up = _mm(cn, d_u,
                a_spec=pl.BlockSpec((l, d // 2), lambda i, j, k: (0, i)),
                b_spec=pl.BlockSpec((None, l, up_shard), lambda i, j, k: (j // 4, 0, j % 4)),
                o_spec=pl.BlockSpec((None, d // 2, up_shard), lambda i, j, k: (j, i, 0)),
                out_shape=jax.ShapeDtypeStruct((N_DEV, d, up_shard), WIRE_DTYPE),
                grid=(2, N_DEV, 1), contract=(0, 0), nk=1, name="mm_gw_up")
    d2d_up = _reduce_scatter_d2d_start(gw_up, d_act, "rs_w_up")
    d_cn = _mm_d_cn(d_u, w_up_blocks, d2d_up[4])
    rs_up = _reduce_scatter_ici_start(d2d_up, d_cn, core, "rs_w_up")
    d_h1, d_h1_b, dg_norm2 = _rmsnorm_bwd(d_h2, d_cn, h1, norm2_gain + rs_up[4][0, 0], "rmsnorm2_bwd", True)

    gw_out = _mm_tn(mix, d_h1_b, WIRE_DTYPE, "mm_gw_out")
    d2d_out = _reduce_scatter_d2d_start(gw_out.reshape(N_DEV, d // N_DEV, d), d_cn, "rs_w_out")
    d_mix = _mm_nt(d_h1_b, w_out_full, F32, "mm_d_mix", after=d2d_out[4])
    d_fq, d_fk, d_fv, ds_sum = _fox_bwd(proj, cum_bc, cum_rows, d_mix, lse_rows)
    d_ff_tile, db_forget_row = _fox_gate_bwd(ds_sum, proj, bias_row)
    d_ret, dg_ret = _retention_bwd(proj, ret_pre, ret_states, d_mix, ret_norm_gain, consts)
    rs_out = _reduce_scatter_ici_start(d2d_out, d_ret, core, "rs_w_out")
    d_proj = jnp.concatenate(
        [d_ret, d_fq, d_fk, d_fv, d_ff_tile, jnp.zeros((l, WIN_N - 7 * GROUP - 128), MXU_DTYPE)], axis=1)
    gw_in = _mm_tn(a, d_proj, WIRE_DTYPE, "mm_gw_in", after=rs_out[4])
    rs_in = _reduce_scatter_start(_extract_w_in_windows(gw_in), core, "rs_w_in")
    d_a = _mm_nt(d_proj, w_in_full, F32, "mm_d_a", tm_cap=544, tn_cap=256, tk_cap=WIN_N, after=rs_in[4])
    d_h0, dg_norm1 = _rmsnorm_bwd(d_h1, d_a, h0, norm1_gain + rs_in[4][0, 0], "rmsnorm1_bwd", False)
    grad_x = d_h0[CHUNK:][None]
    d_meta = d_h0[PAD_ROWS:CHUNK]

    d_conv_w = jnp.concatenate([d_conv[0, 0:3], d_conv[1, 0:3]], axis=1)
    d_conv_b = jnp.concatenate([d_conv[0, 3:4], d_conv[1, 3:4]], axis=1)
    pieces = [loss_part[:, 0:1], dg_norm1, db_forget_row[:, 0:N_HEADS], dg_ret, dg_norm2, d_conv_b, dg_final,
              d_meta.reshape(1, -1), d_conv_w.reshape(1, -1)]
    sizes = [p.shape[1] for p in pieces]
    flat = jnp.concatenate(pieces, axis=1)
    padded = -(-flat.shape[1] // 1024) * 1024
    flat = jnp.pad(flat, ((0, 0), (0, padded - flat.shape[1]))).reshape(padded // 128, 128)
    small_ar = _small_all_reduce_start(flat, d_h0, "all_reduce_small")

    g_w_down = _reduce_scatter_finish(rs_down, small_ar[4], chip, "rs_w_down")[None]
    g_w_up = _reduce_scatter_finish(rs_up, g_w_down, chip, "rs_w_up")[None]
    g_w_out = _reduce_scatter_finish(rs_out, g_w_up, chip, "rs_w_out")[None]
    early = [_adamw(w, g, m, v, "adamw_" + n) for w, g, m, v, n in (
        (w_down, g_w_down, m_w_down, v_w_down, "w_down"), (w_up, g_w_up, m_w_up, v_w_up, "w_up"),
        (w_out, g_w_out, m_w_out, v_w_out, "w_out"))]
    g_w_in_window = _reduce_scatter_finish(rs_in, early[1][2], chip, "rs_w_in")
    g_w_in = g_w_in_window[:, :WIN_SHARD][None]
    early.append(_adamw(w_in, g_w_in, m_w_in, v_w_in, "adamw_w_in"))
    total = _small_all_reduce_finish(small_ar, early[3][2], dev1, "all_reduce_small").reshape(1, padded)
    offs = np.concatenate([[0], np.cumsum(sizes)])
    take = lambda k: total[:, int(offs[k]):int(offs[k + 1])]
    loss = take(0).reshape(())
    g_norm1, g_bf, g_ret_gain, g_norm2 = take(1), take(2), take(3), take(4)
    g_conv_b, g_final = take(5), take(6).reshape(d)
    g_meta = lax.dynamic_slice(take(7).reshape(N_META, d), (jnp.int32(0), (dev * (d // N_DEV)).astype(jnp.int32)),
                               (N_META, d // N_DEV))
    g_conv_w = lax.dynamic_slice(take(8).reshape(3, 2 * d_ff), (jnp.int32(0), (dev * up_shard).astype(jnp.int32)),
                                 (3, up_shard))[None]

    weights = [meta_tokens, norm1_gain, w_in, b_forget, ret_norm_gain, w_out, norm2_gain, w_up, conv_w, conv_b,
               w_down, final_norm_gain]
    grads = [g_meta, g_norm1, g_w_in, g_bf, g_ret_gain, g_w_out, g_norm2, g_w_up, g_conv_w, g_conv_b, g_w_down,
             g_final]
    done = {"w_down": early[0], "w_up": early[1], "w_out": early[2], "w_in": early[3]}
    ms = [m_meta_tokens, m_norm1_gain, m_w_in, m_b_forget, m_ret_norm_gain, m_w_out, m_norm2_gain, m_w_up, m_conv_w,
          m_conv_b, m_w_down, m_final_norm_gain]
    vs = [v_meta_tokens, v_norm1_gain, v_w_in, v_b_forget, v_ret_norm_gain, v_w_out, v_norm2_gain, v_w_up, v_conv_w,
          v_conv_b, v_w_down, v_final_norm_gain]
    names = ["meta", "norm1", "w_in", "b_forget", "ret_gain", "w_out", "norm2", "w_up", "conv_w", "conv_b", "w_down",
             "final_gain"]
    deltas, new_ms, new_vs = [], [], []
    for w, g, m, v, n in zip(weights, grads, ms, vs, names):
        dl, nm, nv = done[n] if n in done else _adamw(w, g, m, v, "adamw_" + n)
        deltas.append(dl)
        new_ms.append(nm)
        new_vs.append(nv)
    return (loss, grad_x, *grads, *deltas, *new_ms, *new_vs)
```

```python
import functools

import numpy as np
import jax
import jax.numpy as jnp
from jax import lax
from jax.experimental import pallas as pl
from jax.experimental.pallas import tpu as pltpu

F32 = jnp.float32
MXU_DTYPE = jnp.bfloat16
WIRE_DTYPE = jnp.bfloat16

N_DEV = 8
N_META = 16
CHUNK = 128
PAD_ROWS = CHUNK - N_META
N_HEADS = 8
HEAD_DIM = 128
GROUP = N_HEADS * HEAD_DIM
IN_DIM = 7 * GROUP + N_HEADS
WIN_SHARD = IN_DIM // N_DEV
WIN_ROWS = 912
WIN_BLOCK = 1024
WIN_STRIDE = 896
WIN_N = 7680
ROPE_BASE = 10000.0
NORM_EPS = 1e-6
NEG_BIG = -1e30
ADAM_LR, ADAM_B1, ADAM_B2, ADAM_EPS, ADAM_WD, ADAM_STEP = 0.001, 0.9, 0.999, 1e-08, 0.01, 10
VMEM_LIMIT = 52 * 1024 * 1024
MESH = pl.DeviceIdType.MESH
ANY = pl.BlockSpec(memory_space=pl.ANY)
VMEM_SPEC = pl.BlockSpec(memory_space=pltpu.VMEM)


def _params(sem=None):
    kw = {"vmem_limit_bytes": VMEM_LIMIT}
    if sem is not None:
        kw["dimension_semantics"] = sem
    return pltpu.CompilerParams(**kw)


def _divisor_tile(n, cap, unit):
    if n <= cap:
        return n
    best = None
    for t in range(unit, cap + 1, unit):
        if n % t == 0:
            best = t
    assert best is not None, (n, cap, unit)
    return best


def _my_position():
    return lax.axis_index("x"), lax.axis_index("y"), lax.axis_index("c")


def _device_index():
    x, y, c = _my_position()
    return 4 * x + 2 * y + c


def _all_gather(shard, name):
    r, c = shard.shape

    def body(x_ref, out_ref, send_sems, recv_sems, local_sem):
        mx, my, mc = _my_position()
        me, sibling = (mx, my, mc), (mx, my, 1 - mc)
        chips = [(1 - mx, my), (mx, 1 - my), (1 - mx, 1 - my)]

        def slot(px, py, pc):
            return out_ref.at[4 * px + 2 * py + pc]

        def copy(k, block, to, src=None):
            return pltpu.make_async_remote_copy(
                src_ref=slot(*block) if src is None else src, dst_ref=slot(*block),
                send_sem=send_sems.at[k], recv_sem=recv_sems.at[k], device_id=to, device_id_type=MESH)

        mine = pltpu.make_async_copy(x_ref, slot(*me), local_sem)
        mine.start()
        first = [copy(0, me, sibling, src=x_ref)]
        first += [copy(1 + j, me, (*chip, mc), src=x_ref) for j, chip in enumerate(chips)]
        for cp in first:
            cp.start()
        passed = [copy(4 + j, (*chip, mc), sibling) for j, chip in enumerate(chips)]
        for j, chip in enumerate(chips):
            copy(1 + j, (*chip, mc), me).wait_recv()
            passed[j].start()
        copy(0, sibling, me).wait_recv()
        for j, chip in enumerate(chips):
            copy(4 + j, (*chip, 1 - mc), me).wait_recv()
        for cp in first + passed:
            cp.wait_send()
        mine.wait()

    return pl.pallas_call(
        body, name=name,
        out_shape=jax.ShapeDtypeStruct((N_DEV, r, c), shard.dtype),
        in_specs=[ANY], out_specs=ANY,
        scratch_shapes=[pltpu.SemaphoreType.DMA((7,)), pltpu.SemaphoreType.DMA((7,)), pltpu.SemaphoreType.DMA],
    )(shard)


HBM_SPEC = pl.BlockSpec(memory_space=pltpu.HBM)
SEM_SPEC = pl.BlockSpec(memory_space=pltpu.SEMAPHORE)
DATAFLOW_EFFECT = pltpu.SideEffectType.DATAFLOW_SIDE_EFFECTING


def _in_hbm(a):
    return pltpu.with_memory_space_constraint(a, pltpu.HBM)


def _split_start(src, land, make_copies, n_copies, after, name):
    if isinstance(land, tuple):
        land = lax.empty(land, src.dtype)
    land_shape = land.shape
    def body(src_ref, land_ref, after_ref, send_sems, recv_sems, src_thru, land_thru, token):
        for cp in make_copies(src_ref, land_ref, send_sems, recv_sems):
            cp.start()
        token[...] = jnp.zeros_like(token)

    return pl.pallas_call(
        body, name=name,
        out_shape=(pltpu.SemaphoreType.DMA((n_copies,)), pltpu.SemaphoreType.DMA((n_copies,)),
                   pltpu.HBM(src.shape, src.dtype), pltpu.HBM(land_shape, src.dtype),
                   jax.ShapeDtypeStruct((8, 128), F32)),
        in_specs=(HBM_SPEC, HBM_SPEC, ANY), out_specs=(SEM_SPEC, SEM_SPEC, HBM_SPEC, HBM_SPEC, VMEM_SPEC),
        input_output_aliases={0: 2, 1: 3},
        compiler_params=pltpu.CompilerParams(has_side_effects=DATAFLOW_EFFECT),
    )(_in_hbm(src), _in_hbm(land), after)


def _split_wait(started, after, make_copies, name):
    send_sems, recv_sems, src_thru, land_thru, _ = started

    def body(src_ref, land_ref, send_sems_ref, recv_sems_ref, after_ref, src_dead, land_out):
        for cp in make_copies(src_ref, land_ref, send_sems_ref, recv_sems_ref):
            cp.wait_send()
            cp.wait_recv()

    return pl.pallas_call(
        body, name=name,
        out_shape=(pltpu.HBM(src_thru.shape, src_thru.dtype), pltpu.HBM(land_thru.shape, land_thru.dtype)),
        in_specs=(HBM_SPEC, HBM_SPEC, SEM_SPEC, SEM_SPEC, ANY), out_specs=(HBM_SPEC, HBM_SPEC),
        input_output_aliases={0: 0, 1: 1},
        compiler_params=pltpu.CompilerParams(has_side_effects=DATAFLOW_EFFECT),
    )(src_thru, land_thru, send_sems, recv_sems, after)


def _gather_copies(x_ref, land_ref, send_sems, recv_sems):
    mx, my, mc = _my_position()
    me = 4 * mx + 2 * my + mc
    targets = [(mx, my, 1 - mc), (1 - mx, my, mc), (mx, 1 - my, mc), (1 - mx, 1 - my, mc)]
    return [pltpu.make_async_remote_copy(
        src_ref=x_ref, dst_ref=land_ref.at[me], send_sem=send_sems.at[k], recv_sem=recv_sems.at[k],
        device_id=t, device_id_type=MESH) for k, t in enumerate(targets)]


def _gather_start(shard, dev, after, name):
    r, c = shard.shape
    tr = _divisor_tile(r, 512, 16)

    def body(s_ref, x_ref, o_ref):
        o_ref[...] = x_ref[...]

    land = pl.pallas_call(
        body, name=name + "_own",
        out_shape=jax.ShapeDtypeStruct((N_DEV, r, c), shard.dtype),
        grid_spec=pltpu.PrefetchScalarGridSpec(
            num_scalar_prefetch=1, grid=(r // tr,),
            in_specs=[pl.BlockSpec((tr, c), lambda i, s: (i, 0))],
            out_specs=pl.BlockSpec((None, tr, c), lambda i, s: (s[0], i, 0))),
        compiler_params=_params(("parallel",)),
    )(dev, shard)
    return _split_start(shard, land, _gather_copies, 4, after, name)


def _gather_finish(started, after, name):
    _, land = _split_wait(started, after, _gather_copies, name + "_wait")

    def body(land_in, land_ref, send_sems, recv_sems):
        mx, my, mc = _my_position()
        chips = [(1 - mx, my), (mx, 1 - my), (1 - mx, 1 - my)]
        copies = [pltpu.make_async_remote_copy(
            src_ref=land_ref.at[4 * cx + 2 * cy + mc], dst_ref=land_ref.at[4 * cx + 2 * cy + mc],
            send_sem=send_sems.at[j], recv_sem=recv_sems.at[j],
            device_id=(mx, my, 1 - mc), device_id_type=MESH) for j, (cx, cy) in enumerate(chips)]
        for cp in copies:
            cp.start()
        for j, (cx, cy) in enumerate(chips):
            copies[j].wait_send()
            pltpu.make_async_remote_copy(
                src_ref=land_ref.at[4 * cx + 2 * cy + 1 - mc], dst_ref=land_ref.at[4 * cx + 2 * cy + 1 - mc],
                send_sem=send_sems.at[j], recv_sem=recv_sems.at[j],
                device_id=(mx, my, 1 - mc), device_id_type=MESH).wait_recv()

    return pl.pallas_call(
        body, name=name + "_pass",
        out_shape=jax.ShapeDtypeStruct(land.shape, land.dtype),
        in_specs=[ANY], out_specs=ANY,
        input_output_aliases={0: 0},
        scratch_shapes=[pltpu.SemaphoreType.DMA((3,)), pltpu.SemaphoreType.DMA((3,))],
    )(land)


def _chip_copies(p_ref, land_ref, send_sems, recv_sems):
    mx, my, mc = _my_position()
    chips = [(1 - mx, my), (mx, 1 - my), (1 - mx, 1 - my)]
    return [pltpu.make_async_remote_copy(
        src_ref=p_ref.at[2 * cx + cy], dst_ref=land_ref.at[j], send_sem=send_sems.at[j], recv_sem=recv_sems.at[j],
        device_id=(cx, cy, mc), device_id_type=MESH) for j, (cx, cy) in enumerate(chips)]


def _reduce_scatter_start(g, core, name):
    pair = _pair_sum(g, _exchange_sibling(g, name + "_d2d"), core, name + "_pairsum")
    return _split_start(pair, (3,) + pair.shape[1:], _chip_copies, 3, g, name + "_ici_start")


def _sibling_copies(g_ref, land_ref, send_sems, recv_sems):
    mx, my, mc = _my_position()
    return [pltpu.make_async_remote_copy(
        src_ref=g_ref.at[2 * k + (1 - mc)], dst_ref=land_ref.at[k], send_sem=send_sems.at[k], recv_sem=recv_sems.at[k],
        device_id=(mx, my, 1 - mc), device_id_type=MESH) for k in range(4)]


def _reduce_scatter_d2d_start(g, after, name):
    return _split_start(g, (4,) + g.shape[1:], _sibling_copies, 4, after, name + "_d2d_start")


def _reduce_scatter_ici_start(d2d_started, after, core, name):
    g, from_sibling = _split_wait(d2d_started, after, _sibling_copies, name + "_d2d_wait")
    pair = _pair_sum(g, from_sibling, core, name + "_pairsum")
    return _split_start(pair, (3,) + pair.shape[1:], _chip_copies, 3, g, name + "_ici_start")


def _reduce_scatter_finish(started, after, chip, name):
    pair, from_chips = _split_wait(started, after, _chip_copies, name + "_ici_wait")
    return _final_sum(pair, from_chips, chip, name + "_sum")


def _exchange_sibling(g, name):
    _, r, c = g.shape

    def body(g_ref, out_ref, send_sems, recv_sems):
        mx, my, mc = _my_position()
        copies = [
            pltpu.make_async_remote_copy(
                src_ref=g_ref.at[2 * k + (1 - mc)], dst_ref=out_ref.at[k],
                send_sem=send_sems.at[k], recv_sem=recv_sems.at[k],
                device_id=(mx, my, 1 - mc), device_id_type=MESH)
            for k in range(4)]
        for cp in copies:
            cp.start()
        for cp in copies:
            cp.wait()

    return pl.pallas_call(
        body, name=name,
        out_shape=jax.ShapeDtypeStruct((4, r, c), g.dtype),
        in_specs=[ANY], out_specs=ANY,
        scratch_shapes=[pltpu.SemaphoreType.DMA((4,)), pltpu.SemaphoreType.DMA((4,))],
    )(g)


def _pair_sum(g, recv, core, name):
    _, r, c = g.shape
    tr = _divisor_tile(r, 512, 16)

    def body(s_ref, g_ref, r_ref, o_ref):
        o_ref[...] = (g_ref[...].astype(F32) + r_ref[...].astype(F32)).astype(o_ref.dtype)

    return pl.pallas_call(
        body, name=name,
        out_shape=jax.ShapeDtypeStruct((4, r, c), g.dtype),
        grid_spec=pltpu.PrefetchScalarGridSpec(
            num_scalar_prefetch=1, grid=(4, r // tr),
            in_specs=[pl.BlockSpec((None, tr, c), lambda k, i, s: (2 * k + s[0], i, 0)),
                      pl.BlockSpec((None, tr, c), lambda k, i, s: (k, i, 0))],
            out_specs=pl.BlockSpec((None, tr, c), lambda k, i, s: (k, i, 0))),
        compiler_params=_params(("parallel", "parallel")),
    )(core, g, recv)


def _final_sum(p, recv, chip, name):
    _, r, c = p.shape
    tr = _divisor_tile(r, 512, 16)

    def body(s_ref, p_ref, r_ref, o_ref):
        acc = p_ref[...].astype(F32)
        for j in range(3):
            acc = acc + r_ref[j].astype(F32)
        o_ref[...] = acc

    return pl.pallas_call(
        body, name=name,
        out_shape=jax.ShapeDtypeStruct((r, c), F32),
        grid_spec=pltpu.PrefetchScalarGridSpec(
            num_scalar_prefetch=1, grid=(r // tr,),
            in_specs=[pl.BlockSpec((None, tr, c), lambda i, s: (s[0], i, 0)),
                      pl.BlockSpec((3, tr, c), lambda i, s: (0, i, 0))],
            out_specs=pl.BlockSpec((tr, c), lambda i, s: (i, 0))),
        compiler_params=_params(("parallel",)),
    )(chip, p, recv)


def _all_to_all_copies(v_ref, land_ref, send_sems, recv_sems):
    mx, my, mc = _my_position()
    me = 4 * mx + 2 * my + mc
    copies = []
    for rel in range(1, N_DEV):
        bx, by, bc = (rel >> 2) & 1, (rel >> 1) & 1, rel & 1
        target = (1 - mx if bx else mx, 1 - my if by else my, 1 - mc if bc else mc)
        copies.append(pltpu.make_async_remote_copy(
            src_ref=v_ref, dst_ref=land_ref.at[me], send_sem=send_sems.at[rel - 1], recv_sem=recv_sems.at[rel - 1],
            device_id=target, device_id_type=MESH))
    return copies


def _small_all_reduce_start(v, after, name):
    return _split_start(v, (N_DEV,) + v.shape, _all_to_all_copies, N_DEV - 1, after, name + "_start")


def _small_all_reduce_finish(started, after, dev, name):
    v, land = _split_wait(started, after, _all_to_all_copies, name + "_wait")
    rows = v.shape[0]

    def body(me_ref, v_ref, land_ref, o_ref):
        for j in range(N_DEV):
            @pl.when(me_ref[0] == j)
            def _():
                o_ref[...] = v_ref[...] if j == 0 else o_ref[...] + v_ref[...]

            @pl.when(me_ref[0] != j)
            def _():
                o_ref[...] = land_ref[j] if j == 0 else o_ref[...] + land_ref[j]

    return pl.pallas_call(
        body, name=name + "_sum",
        out_shape=jax.ShapeDtypeStruct((rows, 128), F32),
        grid_spec=pltpu.PrefetchScalarGridSpec(
            num_scalar_prefetch=1, grid=(1,),
            in_specs=[pl.BlockSpec((rows, 128), lambda i, s: (0, 0)),
                      pl.BlockSpec((N_DEV, rows, 128), lambda i, s: (0, 0, 0))],
            out_specs=pl.BlockSpec((rows, 128), lambda i, s: (0, 0))),
        compiler_params=_params(("arbitrary",)),
    )(dev, v, land)


def _assemble_w_in(blocks):
    _, rows, d = blocks.shape
    tc = _divisor_tile(d, 256, 128)
    n_tiles = WIN_N // 128
    last = (N_DEV * WIN_STRIDE) // 128

    def body(b_ref, o_ref):
        win = []
        for i in range(N_DEV):
            w = jnp.concatenate([b_ref[i].astype(F32), jnp.zeros((WIN_BLOCK - rows, tc), F32)], axis=0)
            win.append(pltpu.roll(w, i, 0) if i else w)
        for t in range(n_tiles):
            if t > last:
                o_ref[t * 128:(t + 1) * 128, :] = jnp.zeros((128, tc), o_ref.dtype)
                continue
            i = min(t // 7, N_DEV - 1)
            k = t - 7 * i
            val = win[i][k * 128:(k + 1) * 128, :]
            if k == 0 and i >= 1:
                val = val + win[i - 1][7 * 128:8 * 128, :]
            o_ref[t * 128:(t + 1) * 128, :] = val.astype(o_ref.dtype)

    return pl.pallas_call(
        body, name="assemble_w_in",
        out_shape=jax.ShapeDtypeStruct((WIN_N, d), blocks.dtype),
        grid=(d // tc,),
        in_specs=[pl.BlockSpec((N_DEV, rows, tc), lambda j: (0, 0, j))],
        out_specs=pl.BlockSpec((WIN_N, tc), lambda j: (0, j)),
        compiler_params=_params(("parallel",)),
    )(blocks)


def _extract_w_in_windows(g):
    _, d = g.shape
    tc = _divisor_tile(d, 256, 128)

    def body(g_ref, o_ref):
        for j in range(N_DEV):
            w = g_ref[WIN_STRIDE * j:WIN_STRIDE * j + WIN_BLOCK, :].astype(F32)
            w = pltpu.roll(w, WIN_BLOCK - j, 0) if j else w
            o_ref[j] = w[0:WIN_ROWS, :].astype(o_ref.dtype)

    return pl.pallas_call(
        body, name="extract_w_in_windows",
        out_shape=jax.ShapeDtypeStruct((N_DEV, WIN_ROWS, d), g.dtype),
        grid=(d // tc,),
        in_specs=[pl.BlockSpec((WIN_N, tc), lambda j: (0, j))],
        out_specs=pl.BlockSpec((N_DEV, WIN_ROWS, tc), lambda j: (0, 0, j)),
        compiler_params=_params(("parallel",)),
    )(g)


def _mm(a, b, *, a_spec, b_spec, o_spec, out_shape, grid, contract, nk, name, after=None):
    dn = (((contract[0],), (contract[1],)), ((), ()))
    tm, tn = o_spec.block_shape[-2:]
    behind = [] if after is None else [after]

    def body(a_ref, b_ref, *rest):
        o_ref, *scratch = rest[len(behind):]
        part = lax.dot_general(a_ref[...], b_ref[...], dn, preferred_element_type=F32)
        if nk == 1:
            o_ref[...] = part.astype(o_ref.dtype)
            return
        acc = scratch[0]
        k = pl.program_id(2)

        @pl.when(k == 0)
        def _():
            acc[...] = part

        @pl.when(k > 0)
        def _():
            acc[...] += part

        @pl.when(k == nk - 1)
        def _():
            o_ref[...] = acc[...].astype(o_ref.dtype)

    return pl.pallas_call(
        body, name=name, out_shape=out_shape, grid=grid,
        in_specs=[a_spec, b_spec] + [ANY] * len(behind), out_specs=o_spec,
        scratch_shapes=[] if nk == 1 else [pltpu.VMEM((tm, tn), F32)],
        compiler_params=_params(("parallel", "parallel", "arbitrary")),
    )(a, b, *behind)


def _mm_nn(a, b, out_dtype, name, tm_cap=1088, tn_cap=512, tk_cap=2048, after=None):
    m, k = a.shape
    _, n = b.shape
    tm, tn, tk = _divisor_tile(m, tm_cap, 16), _divisor_tile(n, tn_cap, 128), _divisor_tile(k, tk_cap, 128)
    return _mm(a, b,
               a_spec=pl.BlockSpec((tm, tk), lambda i, j, kk: (i, kk)),
               b_spec=pl.BlockSpec((tk, tn), lambda i, j, kk: (kk, j)),
               o_spec=pl.BlockSpec((tm, tn), lambda i, j, kk: (i, j)),
               out_shape=jax.ShapeDtypeStruct((m, n), out_dtype),
               grid=(m // tm, n // tn, k // tk), contract=(1, 0), nk=k // tk, name=name, after=after)


def _mm_nt(a, b, out_dtype, name, tm_cap=1088, tn_cap=512, tk_cap=2048, after=None):
    m, k = a.shape
    n, _ = b.shape
    tm, tn, tk = _divisor_tile(m, tm_cap, 16), _divisor_tile(n, tn_cap, 128), _divisor_tile(k, tk_cap, 128)
    return _mm(a, b,
               a_spec=pl.BlockSpec((tm, tk), lambda i, j, kk: (i, kk)),
               b_spec=pl.BlockSpec((tn, tk), lambda i, j, kk: (j, kk)),
               o_spec=pl.BlockSpec((tm, tn), lambda i, j, kk: (i, j)),
               out_shape=jax.ShapeDtypeStruct((m, n), out_dtype),
               grid=(m // tm, n // tn, k // tk), contract=(1, 1), nk=k // tk, name=name, after=after)


def _mm_tn(a, b, out_dtype, name, tm_cap=1024, tn_cap=512, after=None):
    l, m = a.shape
    _, n = b.shape
    tm, tn = _divisor_tile(m, tm_cap, 128), _divisor_tile(n, tn_cap, 128)
    return _mm(a, b,
               a_spec=pl.BlockSpec((l, tm), lambda i, j, kk: (0, i)),
               b_spec=pl.BlockSpec((l, tn), lambda i, j, kk: (0, j)),
               o_spec=pl.BlockSpec((tm, tn), lambda i, j, kk: (i, j)),
               out_shape=jax.ShapeDtypeStruct((m, n), out_dtype),
               grid=(m // tm, n // tn, 1), contract=(0, 0), nk=1, name=name, after=after)


def _mm_d_cn(d_u, w_up_blocks, after):
    _, l, d_ff = d_u.shape
    n, d, shard = w_up_blocks.shape
    per = d_ff // shard
    tm, tn = _divisor_tile(l, 544, 16), _divisor_tile(d, 256, 128)

    def body(a_ref, b_ref, after_ref, o_ref):
        acc = None
        for k in range(n):
            part = _dot_nt(a_ref[k // per, :, (k % per) * shard:(k % per + 1) * shard], b_ref[k])
            acc = part if acc is None else acc + part
        o_ref[...] = acc

    return pl.pallas_call(
        body, name="mm_d_cn", out_shape=jax.ShapeDtypeStruct((l, d), F32), grid=(l // tm, d // tn),
        in_specs=[pl.BlockSpec((2, tm, d_ff), lambda i, j: (0, i, 0)),
                  pl.BlockSpec((n, tn, shard), lambda i, j: (0, j, 0)), ANY],
        out_specs=pl.BlockSpec((tm, tn), lambda i, j: (i, j)),
        compiler_params=_params(("parallel", "parallel")),
    )(d_u, w_up_blocks, after)


def _row_tile(l):
    return _divisor_tile(l, 544, 8)


def _rmsnorm_fwd(h, gain, name, res=None):
    l, d = h.shape
    tr = _row_tile(l)
    row = pl.BlockSpec((tr, d), lambda i: (i, 0))
    vec = pl.BlockSpec((1, d), lambda i: (0, 0))

    def body(*refs):
        if res is None:
            h_ref, g_ref, n_ref = refs
            x = h_ref[...]
        else:
            h_ref, r_ref, g_ref, s_ref, n_ref = refs
            x = h_ref[...] + r_ref[...]
            s_ref[...] = x
        y = x * lax.rsqrt(jnp.mean(x * x, axis=-1, keepdims=True) + NORM_EPS)
        n_ref[...] = (y * g_ref[...]).astype(n_ref.dtype)

    normed = jax.ShapeDtypeStruct((l, d), MXU_DTYPE)
    if res is None:
        return pl.pallas_call(body, name=name, out_shape=normed, grid=(l // tr,), in_specs=[row, vec],
                              out_specs=row, compiler_params=_params(("parallel",)))(h, gain)
    return pl.pallas_call(body, name=name, out_shape=(jax.ShapeDtypeStruct((l, d), F32), normed),
                          grid=(l // tr,), in_specs=[row, row, vec], out_specs=(row, row),
                          compiler_params=_params(("parallel",)))(h, res, gain)


def _rmsnorm_bwd(d_res, d_normed, x, gain, name, with_mxu_copy):
    l, d = x.shape
    tr = _row_tile(l)
    row = pl.BlockSpec((tr, d), lambda i: (i, 0))
    vec = pl.BlockSpec((1, d), lambda i: (0, 0))

    def body(dres_ref, dn_ref, x_ref, g_ref, dx_ref, *rest):
        dg_ref = rest[-1]
        xv = x_ref[...]
        r = lax.rsqrt(jnp.mean(xv * xv, axis=-1, keepdims=True) + NORM_EPS)
        xh = xv * r
        dn = dn_ref[...]
        dxh = dn * g_ref[...]
        dx = dres_ref[...] + r * (dxh - xh * jnp.mean(dxh * xh, axis=-1, keepdims=True))
        dx_ref[...] = dx
        if with_mxu_copy:
            rest[0][...] = dx.astype(MXU_DTYPE)

        @pl.when(pl.program_id(0) == 0)
        def _():
            dg_ref[...] = jnp.zeros_like(dg_ref)

        dg_ref[...] += jnp.sum(dn * xh, axis=0, keepdims=True)

    outs = [jax.ShapeDtypeStruct((l, d), F32)]
    specs = [row]
    if with_mxu_copy:
        outs.append(jax.ShapeDtypeStruct((l, d), MXU_DTYPE))
        specs.append(row)
    outs.append(jax.ShapeDtypeStruct((1, d), F32))
    specs.append(vec)
    return pl.pallas_call(body, name=name, out_shape=tuple(outs), grid=(l // tr,),
                          in_specs=[row, row, row, vec], out_specs=tuple(specs),
                          compiler_params=_params(("arbitrary",)))(d_res, d_normed, x, gain)


def _loss_head(h1, mlp_out, gain, target):
    l, d = h1.shape
    n_blocks = l // CHUNK
    row = pl.BlockSpec((CHUNK, d), lambda i: (i, 0))
    vec = pl.BlockSpec((1, d), lambda i: (0, 0))
    tgt = pl.BlockSpec((CHUNK, d), lambda i: (jnp.maximum(i - 1, 0), 0))

    def body(h_ref, m_ref, g_ref, t_ref, dh_ref, dhb_ref, dg_ref, loss_ref, sq_ref):
        i = pl.program_id(0)
        x = h_ref[...] + m_ref[...]
        r = lax.rsqrt(jnp.mean(x * x, axis=-1, keepdims=True) + NORM_EPS)
        xh = x * r
        g = g_ref[...]
        real = i >= 1
        err = jnp.where(real, xh * g - t_ref[...], 0.0)
        dy = err * (1.0 / d)
        dxh = dy * g
        dh = r * (dxh - xh * jnp.mean(dxh * xh, axis=-1, keepdims=True))
        dh_ref[...] = dh
        dhb_ref[...] = dh.astype(MXU_DTYPE)

        @pl.when(i == 0)
        def _():
            dg_ref[...] = jnp.zeros_like(dg_ref)
            sq_ref[...] = jnp.zeros_like(sq_ref)

        dg_ref[...] += jnp.sum(dy * xh, axis=0, keepdims=True)
        sq_ref[...] += jnp.sum(err * err, axis=0, keepdims=True)

        @pl.when(i == n_blocks - 1)
        def _():
            total = jnp.sum(sq_ref[...], axis=-1, keepdims=True) * (0.5 / d)
            loss_ref[...] = jnp.broadcast_to(total, (1, 128))

    return pl.pallas_call(
        body, name="loss_head",
        out_shape=(jax.ShapeDtypeStruct((l, d), F32), jax.ShapeDtypeStruct((l, d), MXU_DTYPE),
                   jax.ShapeDtypeStruct((1, d), F32), jax.ShapeDtypeStruct((1, 128), F32)),
        grid=(n_blocks,), in_specs=[row, row, vec, tgt],
        out_specs=(row, row, vec, pl.BlockSpec((1, 128), lambda i: (0, 0))),
        scratch_shapes=[pltpu.VMEM((1, d), F32)],
        compiler_params=_params(("arbitrary",)),
    )(h1, mlp_out, gain, target)


def _dot(a, b):
    return jnp.dot(a, b, preferred_element_type=F32)


def _dot_nt(a, b):
    return lax.dot_general(a, b, (((1,), (1,)), ((), ())), preferred_element_type=F32)


def _dot_tn(a, b):
    return lax.dot_general(a, b, (((0,), (0,)), ((), ())), preferred_element_type=F32)


def _rope(t, cos2, sin2):
    return t * cos2 + pltpu.roll(t, HEAD_DIM // 2, 1) * sin2


def _rope_bwd(dr, cos2, sin2):
    return dr * cos2 + pltpu.roll(dr * sin2, HEAD_DIM // 2, 1)


def _sigmoid(x):
    return 1.0 / (1.0 + jnp.exp(-x))


def _row_valid(block, rows):
    r = block * CHUNK + lax.broadcasted_iota(jnp.int32, (rows, 1), 0)
    return r >= PAD_ROWS


def _retention_consts(l):
    pos = jnp.arange(l, dtype=F32) - PAD_ROWS
    inv_freq = 1.0 / (ROPE_BASE ** (jnp.arange(0, HEAD_DIM, 2, dtype=F32) / HEAD_DIM))
    ang = pos[:, None] * inv_freq[None, :]
    cos, sin = jnp.cos(ang), jnp.sin(ang)
    cos2 = jnp.concatenate([cos, cos], axis=-1)
    sin2 = jnp.concatenate([-sin, sin], axis=-1)
    log_g = jnp.log1p(-jnp.exp2(-5.0 - jnp.arange(N_HEADS, dtype=F32)))
    idx = jnp.arange(CHUNK, dtype=F32)
    diff = idx[:, None] - idx[None, :]
    decay = jnp.where(diff >= 0, jnp.exp(jnp.maximum(diff, 0.0)[None] * log_g[:, None, None]), 0.0)
    xi = jnp.exp((idx + 1.0)[None, :] * log_g[:, None])
    zeta = jnp.exp((CHUNK - 1.0 - idx)[None, :] * log_g[:, None])
    g_chunk = jnp.exp(CHUNK * log_g)
    bcast = lambda v: jnp.broadcast_to(v[:, :, None], (N_HEADS, CHUNK, HEAD_DIM))
    g_rows = jnp.broadcast_to(g_chunk[:, None, None], (N_HEADS, 8, HEAD_DIM))
    return cos2, sin2, decay, bcast(xi), bcast(zeta), g_rows


def _retention_fwd(proj, ret_gain, consts):
    l = proj.shape[0]
    n_chunks = l // CHUNK
    cos2, sin2, decay, xi, zeta, g_rows = consts
    scale = HEAD_DIM ** -0.5

    def body(p_ref, cos_ref, sin_ref, dec_ref, xi_ref, zeta_ref, gr_ref, gain_ref,
             mix_ref, o_ref, st_ref, state):
        c = pl.program_id(0)

        @pl.when(c == 0)
        def _():
            state[...] = jnp.zeros_like(state)

        cos_v, sin_v = cos_ref[...], sin_ref[...]
        valid = _row_valid(c, CHUNK)
        for h in range(N_HEADS):
            cols = slice(h * HEAD_DIM, (h + 1) * HEAD_DIM)
            q = p_ref[:, h * HEAD_DIM:(h + 1) * HEAD_DIM]
            k = p_ref[:, GROUP + h * HEAD_DIM:GROUP + (h + 1) * HEAD_DIM]
            v = p_ref[:, 2 * GROUP + h * HEAD_DIM:2 * GROUP + (h + 1) * HEAD_DIM]
            g = p_ref[:, 3 * GROUP + h * HEAD_DIM:3 * GROUP + (h + 1) * HEAD_DIM]
            rq = _rope(q, cos_v, sin_v).astype(MXU_DTYPE)
            rk = _rope(k, cos_v, sin_v) * scale
            rkb = rk.astype(MXU_DTYPE)
            vb = v.astype(MXU_DTYPE)
            st = state[h]
            st_ref[h] = st
            s = _dot_nt(rq, rkb) * dec_ref[h]
            o = _dot(s.astype(MXU_DTYPE), vb) + _dot(rq, st.astype(MXU_DTYPE)) * xi_ref[h]
            kz = (rk * zeta_ref[h]).astype(MXU_DTYPE)
            state[h] = gr_ref[h, 0:1, :] * st + _dot_tn(kz, vb)
            o_ref[:, cols] = o
            mu = jnp.mean(o, axis=-1, keepdims=True)
            oc = o - mu
            yn = oc * lax.rsqrt(jnp.mean(oc * oc, axis=-1, keepdims=True) + NORM_EPS)
            ret = (g * _sigmoid(g)) * (yn * gain_ref[:, cols])
            mix_ref[:, cols] = jnp.where(valid, ret, 0.0).astype(mix_ref.dtype)

    head_tab = pl.BlockSpec((N_HEADS, CHUNK, HEAD_DIM), lambda c: (0, 0, 0))
    return pl.pallas_call(
        body, name="retention_fwd",
        out_shape=(jax.ShapeDtypeStruct((l, 2 * GROUP), MXU_DTYPE), jax.ShapeDtypeStruct((l, GROUP), F32),
                   jax.ShapeDtypeStruct((n_chunks, N_HEADS, HEAD_DIM, HEAD_DIM), F32)),
        grid=(n_chunks,),
        in_specs=[pl.BlockSpec((CHUNK, 4 * GROUP), lambda c: (c, 0)),
                  pl.BlockSpec((CHUNK, HEAD_DIM), lambda c: (c, 0)),
                  pl.BlockSpec((CHUNK, HEAD_DIM), lambda c: (c, 0)),
                  head_tab, head_tab, head_tab,
                  pl.BlockSpec((N_HEADS, 8, HEAD_DIM), lambda c: (0, 0, 0)),
                  pl.BlockSpec((1, GROUP), lambda c: (0, 0))],
        out_specs=(pl.BlockSpec((CHUNK, GROUP), lambda c: (c, 0)),
                   pl.BlockSpec((CHUNK, GROUP), lambda c: (c, 0)),
                   pl.BlockSpec((None, N_HEADS, HEAD_DIM, HEAD_DIM), lambda c: (c, 0, 0, 0))),
        scratch_shapes=[pltpu.VMEM((N_HEADS, HEAD_DIM, HEAD_DIM), F32)],
        compiler_params=_params(("arbitrary",)),
    )(proj, cos2, sin2, decay, xi, zeta, g_rows, ret_gain)


def _retention_bwd(proj, o_pre, states, d_mix, ret_gain, consts):
    l = proj.shape[0]
    n_chunks = l // CHUNK
    cos2, sin2, decay, xi, zeta, g_rows = consts
    scale = HEAD_DIM ** -0.5
    rev = lambda c: n_chunks - 1 - c

    def body(p_ref, o_ref, st_ref, dm_ref, cos_ref, sin_ref, dec_ref, xi_ref, zeta_ref, gr_ref, gain_ref,
             dp_ref, dgain_ref, dstate):
        step = pl.program_id(0)

        @pl.when(step == 0)
        def _():
            dstate[...] = jnp.zeros_like(dstate)
            dgain_ref[...] = jnp.zeros_like(dgain_ref)

        cos_v, sin_v = cos_ref[...], sin_ref[...]
        valid = _row_valid(rev(step), CHUNK)
        for h in range(N_HEADS):
            cols = slice(h * HEAD_DIM, (h + 1) * HEAD_DIM)
            q = p_ref[:, h * HEAD_DIM:(h + 1) * HEAD_DIM]
            k = p_ref[:, GROUP + h * HEAD_DIM:GROUP + (h + 1) * HEAD_DIM]
            v = p_ref[:, 2 * GROUP + h * HEAD_DIM:2 * GROUP + (h + 1) * HEAD_DIM]
            g = p_ref[:, 3 * GROUP + h * HEAD_DIM:3 * GROUP + (h + 1) * HEAD_DIM]
            o = o_ref[:, cols]
            gain = gain_ref[:, cols]
            d_ret = jnp.where(valid, dm_ref[:, cols], 0.0)
            mu = jnp.mean(o, axis=-1, keepdims=True)
            oc = o - mu
            rstd = lax.rsqrt(jnp.mean(oc * oc, axis=-1, keepdims=True) + NORM_EPS)
            yn = oc * rstd
            sig = _sigmoid(g)
            gate = g * sig
            dgain_ref[:, cols] += jnp.sum(d_ret * gate * yn, axis=0, keepdims=True)
            d_g = d_ret * (yn * gain) * (sig * (1.0 + g * (1.0 - sig)))
            d_yn = d_ret * gate * gain
            d_o = rstd * (d_yn - jnp.mean(d_yn, axis=-1, keepdims=True)
                          - yn * jnp.mean(d_yn * yn, axis=-1, keepdims=True))
            rq = _rope(q, cos_v, sin_v)
            rk = _rope(k, cos_v, sin_v) * scale
            rqb, rkb, vb = rq.astype(MXU_DTYPE), rk.astype(MXU_DTYPE), v.astype(MXU_DTYPE)
            dob = d_o.astype(MXU_DTYPE)
            dec = dec_ref[h]
            xi_h, zeta_h = xi_ref[h], zeta_ref[h]
            st_b = st_ref[h].astype(MXU_DTYPE)
            dst = dstate[h]
            dst_b = dst.astype(MXU_DTYPE)
            s_b = (_dot_nt(rqb, rkb) * dec).astype(MXU_DTYPE)
            da_b = (_dot_nt(dob, vb) * dec).astype(MXU_DTYPE)
            doxi_b = (d_o * xi_h).astype(MXU_DTYPE)
            kz_b = (rk * zeta_h).astype(MXU_DTYPE)
            d_rq = _dot(da_b, rkb) + _dot_nt(doxi_b, st_b)
            d_rk = _dot_tn(da_b, rqb) + _dot_nt(vb, dst_b) * zeta_h
            d_v = _dot_tn(s_b, dob) + _dot(kz_b, dst_b)
            dstate[h] = gr_ref[h, 0:1, :] * dst + _dot_tn(rqb, doxi_b)
            d_q = _rope_bwd(d_rq, cos_v, sin_v)
            d_k = _rope_bwd(d_rk * scale, cos_v, sin_v)
            dp_ref[:, h * HEAD_DIM:(h + 1) * HEAD_DIM] = d_q.astype(dp_ref.dtype)
            dp_ref[:, GROUP + h * HEAD_DIM:GROUP + (h + 1) * HEAD_DIM] = d_k.astype(dp_ref.dtype)
            dp_ref[:, 2 * GROUP + h * HEAD_DIM:2 * GROUP + (h + 1) * HEAD_DIM] = d_v.astype(dp_ref.dtype)
            dp_ref[:, 3 * GROUP + h * HEAD_DIM:3 * GROUP + (h + 1) * HEAD_DIM] = d_g.astype(dp_ref.dtype)

    head_tab = pl.BlockSpec((N_HEADS, CHUNK, HEAD_DIM), lambda c: (0, 0, 0))
    return pl.pallas_call(
        body, name="retention_bwd",
        out_shape=(jax.ShapeDtypeStruct((l, 4 * GROUP), MXU_DTYPE), jax.ShapeDtypeStruct((1, GROUP), F32)),
        grid=(n_chunks,),
        in_specs=[pl.BlockSpec((CHUNK, 4 * GROUP), lambda c: (rev(c), 0)),
                  pl.BlockSpec((CHUNK, GROUP), lambda c: (rev(c), 0)),
                  pl.BlockSpec((None, N_HEADS, HEAD_DIM, HEAD_DIM), lambda c: (rev(c), 0, 0, 0)),
                  pl.BlockSpec((CHUNK, GROUP), lambda c: (rev(c), 0)),
                  pl.BlockSpec((CHUNK, HEAD_DIM), lambda c: (rev(c), 0)),
                  pl.BlockSpec((CHUNK, HEAD_DIM), lambda c: (rev(c), 0)),
                  head_tab, head_tab, head_tab,
                  pl.BlockSpec((N_HEADS, 8, HEAD_DIM), lambda c: (0, 0, 0)),
                  pl.BlockSpec((1, GROUP), lambda c: (0, 0))],
        out_specs=(pl.BlockSpec((CHUNK, 4 * GROUP), lambda c: (rev(c), 0)),
                   pl.BlockSpec((1, GROUP), lambda c: (0, 0))),
        scratch_shapes=[pltpu.VMEM((N_HEADS, HEAD_DIM, HEAD_DIM), F32)],
        compiler_params=_params(("arbitrary",)),
    )(proj, o_pre, states, d_mix, cos2, sin2, decay, xi, zeta, g_rows, ret_gain)


FF_TILE = (7 * GROUP) // 128


def _log_forget(ff, bias_row, valid):
    x = ff + bias_row
    e = jnp.exp(-jnp.abs(x))
    lf = jnp.minimum(x, 0.0) - jnp.log(1.0 + e)
    head_lane = lax.broadcasted_iota(jnp.int32, x.shape, 1) < N_HEADS
    keep = lambda t: jnp.where(head_lane, jnp.where(valid, t, 0.0), 0.0)
    return keep(lf), keep(jnp.where(x >= 0, e, 1.0) / (1.0 + e))


def _fox_prep(proj, bias_row):
    l = proj.shape[0]
    n_blocks = l // CHUNK

    def body(ff_ref, b_ref, bc_ref, rows_ref, cum):
        r = lax.broadcasted_iota(jnp.int32, (CHUNK, CHUNK), 0)
        cidx = lax.broadcasted_iota(jnp.int32, (CHUNK, CHUNK), 1)
        tri = jnp.where(r >= cidx, 1.0, 0.0).astype(F32)
        carry = jnp.zeros((1, 128), F32)
        for blk in range(n_blocks):
            rows = slice(blk * CHUNK, (blk + 1) * CHUNK)
            valid = _row_valid(blk, CHUNK)
            lf, _ = _log_forget(ff_ref[rows, :], b_ref[...], valid)
            local = jnp.dot(tri, lf, precision=lax.Precision.HIGHEST, preferred_element_type=F32) + carry
            carry = local[CHUNK - 1:CHUNK, :]
            masked = jnp.where(valid, local, -NEG_BIG)
            cum[rows, :] = masked
            t = masked.T
            for h in range(N_HEADS):
                rows_ref[h, :, rows] = t[h:h + 1, :]
        full = cum[...]
        for h in range(N_HEADS):
            bc_ref[h] = jnp.broadcast_to(full[:, h:h + 1], (l, 128))

    return pl.pallas_call(
        body, name="fox_prep",
        out_shape=(jax.ShapeDtypeStruct((N_HEADS, l, 128), F32), jax.ShapeDtypeStruct((N_HEADS, 1, l), F32)),
        grid=(1,),
        in_specs=[pl.BlockSpec((l, 128), lambda i: (0, FF_TILE)), pl.BlockSpec((1, 128), lambda i: (0, 0))],
        out_specs=(pl.BlockSpec((N_HEADS, l, 128), lambda i: (0, 0, 0)),
                   pl.BlockSpec((N_HEADS, 1, l), lambda i: (0, 0, 0))),
        scratch_shapes=[pltpu.VMEM((l, 128), F32)],
        compiler_params=_params(("arbitrary",)),
    )(proj, bias_row)


ATTN_BLOCK = 2 * CHUNK


def _attn_blocks(l):
    assert (l - CHUNK) % ATTN_BLOCK == 0
    return [(0, CHUNK)] + [(s, ATTN_BLOCK) for s in range(CHUNK, l, ATTN_BLOCK)]


def _rows_valid(start, size):
    return start + lax.broadcasted_iota(jnp.int32, (size, 1), 0) >= PAD_ROWS


def _fox_fwd(proj, cum_bc, cum_rows, mix):
    l = proj.shape[0]
    blocks = _attn_blocks(l)
    scale = HEAD_DIM ** -0.5
    qt, kt, vt = 4 * N_HEADS, 5 * N_HEADS, 6 * N_HEADS

    def body(q_ref, k_ref, v_ref, cbc_ref, crow_ref, mix_in, o_ref, lse_ref, qb_s, kb_s, vb_s):
        qb_s[...] = q_ref[...].astype(MXU_DTYPE)
        kb_s[...] = k_ref[...].astype(MXU_DTYPE)
        vb_s[...] = v_ref[...].astype(MXU_DTYPE)
        for p, (qs, qn) in enumerate(blocks):
            qb = qb_s[qs:qs + qn, :]
            cq = cbc_ref[qs:qs + qn, :]
            m = jnp.full((qn, 1), NEG_BIG, F32)
            lsum = jnp.zeros((qn, 1), F32)
            acc = jnp.zeros((qn, HEAD_DIM), F32)
            for j in range(p + 1):
                ks, kn = blocks[j]
                bias = jnp.tile(cq, (1, kn // CHUNK)) - crow_ref[:, ks:ks + kn]
                s = _dot_nt(qb, kb_s[ks:ks + kn, :]) * scale + bias
                if j == p:
                    q_pos = qs + lax.broadcasted_iota(jnp.int32, (qn, kn), 0)
                    k_pos = ks + lax.broadcasted_iota(jnp.int32, (qn, kn), 1)
                    s = jnp.where(k_pos <= q_pos, s, NEG_BIG)
                m_new = jnp.maximum(m, jnp.max(s, axis=-1, keepdims=True))
                alpha = jnp.exp(m - m_new)
                pr = jnp.exp(s - m_new)
                lsum = lsum * alpha + jnp.sum(pr, axis=-1, keepdims=True)
                acc = acc * alpha + _dot(pr.astype(MXU_DTYPE), vb_s[ks:ks + kn, :])
                m = m_new
            o = jnp.where(_rows_valid(qs, qn), acc * (1.0 / lsum), 0.0)
            o_ref[qs:qs + qn, :] = o.astype(o_ref.dtype)
            lse = m + jnp.log(lsum)
            lse_ref[:, qs:qs + qn] = jnp.broadcast_to(lse, (qn, CHUNK)).T[0:1, :]

    head_col = lambda t: pl.BlockSpec((l, HEAD_DIM), lambda h: (0, t + h))
    return pl.pallas_call(
        body, name="fox_fwd",
        out_shape=(jax.ShapeDtypeStruct(mix.shape, mix.dtype), jax.ShapeDtypeStruct((N_HEADS, 1, l), F32)),
        grid=(N_HEADS,),
        in_specs=[head_col(qt), head_col(kt), head_col(vt),
                  pl.BlockSpec((None, l, 128), lambda h: (h, 0, 0)),
                  pl.BlockSpec((None, 1, l), lambda h: (h, 0, 0)),
                  ANY],
        out_specs=(head_col(N_HEADS), pl.BlockSpec((None, 1, l), lambda h: (h, 0, 0))),
        input_output_aliases={5: 0},
        scratch_shapes=[pltpu.VMEM((l, HEAD_DIM), MXU_DTYPE)] * 3,
        compiler_params=_params(("parallel",)),
    )(proj, proj, proj, cum_bc, cum_rows, mix)


def _fox_bwd(proj, cum_bc, cum_rows, d_mix, lse_rows):
    l = proj.shape[0]
    blocks = _attn_blocks(l)
    scale = HEAD_DIM ** -0.5
    qt, kt, vt = 4 * N_HEADS, 5 * N_HEADS, 6 * N_HEADS

    def body(q_ref, k_ref, v_ref, do_ref, cbc_ref, crow_ref, lse_ref,
             dq_ref, dk_ref, dv_ref, ds_ref, dk_acc, dv_acc, qb_s, kb_s, vb_s, dob_s):
        qb_s[...] = q_ref[...].astype(MXU_DTYPE)
        kb_s[...] = k_ref[...].astype(MXU_DTYPE)
        vb_s[...] = v_ref[...].astype(MXU_DTYPE)
        dob_s[...] = jnp.where(_rows_valid(0, l), do_ref[...], 0.0).astype(MXU_DTYPE)
        dk_acc[...] = jnp.zeros_like(dk_acc)
        dv_acc[...] = jnp.zeros_like(dv_acc)
        ds_ref[...] = jnp.zeros_like(ds_ref)
        shift_row = crow_ref[...] - lse_ref[...]

        for p, (qs, qn) in enumerate(blocks):
            qb, dob = qb_s[qs:qs + qn, :], dob_s[qs:qs + qn, :]
            shift = shift_row[:, qs:qs + qn]

            def probs(j):
                ks, kn = blocks[j]
                ck = jnp.tile(cbc_ref[ks:ks + kn, :], (1, qn // CHUNK))
                s_t = _dot_nt(kb_s[ks:ks + kn, :], qb) * scale + (shift - ck)
                if j == p:
                    k_pos = ks + lax.broadcasted_iota(jnp.int32, (kn, qn), 0)
                    q_pos = qs + lax.broadcasted_iota(jnp.int32, (kn, qn), 1)
                    s_t = jnp.where(k_pos <= q_pos, s_t, NEG_BIG)
                return jnp.exp(s_t), _dot_nt(vb_s[ks:ks + kn, :], dob)

            delta = jnp.zeros((1, qn), F32)
            for j in range(p + 1):
                p_t, dp_t = probs(j)
                delta = delta + jnp.sum(p_t * dp_t, axis=0, keepdims=True)
            dq = jnp.zeros((qn, HEAD_DIM), F32)
            for j in range(p + 1):
                ks, kn = blocks[j]
                rows = slice(ks, ks + kn)
                p_t, dp_t = probs(j)
                ds_t = p_t * (dp_t - delta)
                ds_b = ds_t.astype(MXU_DTYPE)
                dv_acc[rows, :] += _dot(p_t.astype(MXU_DTYPE), dob)
                dk_acc[rows, :] += _dot(ds_b, qb) * scale
                ds_ref[rows, :] += sum(ds_t[:, c:c + CHUNK] for c in range(0, qn, CHUNK))
                dq = dq + _dot_tn(ds_b, kb_s[rows, :])
            dq_ref[qs:qs + qn, :] = (dq * scale).astype(dq_ref.dtype)

        dk_ref[...] = dk_acc[...].astype(dk_ref.dtype)
        dv_ref[...] = dv_acc[...].astype(dv_ref.dtype)

    col = jax.ShapeDtypeStruct((l, GROUP), MXU_DTYPE)
    head_col = lambda t: pl.BlockSpec((l, HEAD_DIM), lambda h: (0, t + h))
    return pl.pallas_call(
        body, name="fox_bwd",
        out_shape=(col, col, col, jax.ShapeDtypeStruct((N_HEADS, l, 128), F32)),
        grid=(N_HEADS,),
        in_specs=[head_col(qt), head_col(kt), head_col(vt), head_col(N_HEADS),
                  pl.BlockSpec((None, l, 128), lambda h: (h, 0, 0)),
                  pl.BlockSpec((None, 1, l), lambda h: (h, 0, 0)),
                  pl.BlockSpec((None, 1, l), lambda h: (h, 0, 0))],
        out_specs=(head_col(0), head_col(0), head_col(0), pl.BlockSpec((None, l, 128), lambda h: (h, 0, 0))),
        scratch_shapes=[pltpu.VMEM((l, HEAD_DIM), F32)] * 2 + [pltpu.VMEM((l, HEAD_DIM), MXU_DTYPE)] * 4,
        compiler_params=_params(("parallel",)),
    )(proj, proj, proj, d_mix, cum_bc, cum_rows, lse_rows)


def _fox_gate_bwd(ds_sum, proj, bias_row):
    l = proj.shape[0]
    n_blocks = l // CHUNK

    def body(ds_ref, ff_ref, b_ref, dff_ref, db_ref):
        r = lax.broadcasted_iota(jnp.int32, (CHUNK, CHUNK), 0)
        cidx = lax.broadcasted_iota(jnp.int32, (CHUNK, CHUNK), 1)
        upper = jnp.where(cidx >= r, 1.0, 0.0).astype(F32)
        carry = jnp.zeros((1, 128), F32)
        db = jnp.zeros((1, 128), F32)
        for blk in reversed(range(n_blocks)):
            rows = slice(blk * CHUNK, (blk + 1) * CHUNK)
            key_sum = jnp.zeros((CHUNK, 128), F32)
            for h in range(N_HEADS):
                select = jnp.where(cidx == h, 1.0, 0.0).astype(F32)
                key_sum = key_sum + jnp.dot(ds_ref[h, rows, :], select, precision=lax.Precision.HIGHEST,
                                            preferred_element_type=F32)
            suffix = jnp.dot(upper, key_sum, precision=lax.Precision.HIGHEST, preferred_element_type=F32) + carry
            carry = suffix[0:1, :]
            _, dsig = _log_forget(ff_ref[rows, :], b_ref[...], _row_valid(blk, CHUNK))
            dff = -suffix * dsig
            dff_ref[rows, :] = dff.astype(dff_ref.dtype)
            db = db + jnp.sum(dff, axis=0, keepdims=True)
        db_ref[...] = db

    return pl.pallas_call(
        body, name="fox_gate_bwd",
        out_shape=(jax.ShapeDtypeStruct((l, 128), MXU_DTYPE), jax.ShapeDtypeStruct((1, 128), F32)),
        grid=(1,),
        in_specs=[pl.BlockSpec((N_HEADS, l, 128), lambda i: (0, 0, 0)),
                  pl.BlockSpec((l, 128), lambda i: (0, FF_TILE)),
                  pl.BlockSpec((1, 128), lambda i: (0, 0))],
        out_specs=(pl.BlockSpec((l, 128), lambda i: (0, 0)), pl.BlockSpec((1, 128), lambda i: (0, 0))),
        compiler_params=_params(("arbitrary",)),
    )(ds_sum, proj, bias_row)


def _conv(u, w, b):
    return b + w[0:1, :] * pltpu.roll(u, 2, 0) + w[1:2, :] * pltpu.roll(u, 1, 0) + w[2:3, :] * u


def _conv_act_fwd(u, conv_w, conv_b, d_ff):
    l = u.shape[0]
    tc = _divisor_tile(d_ff, 256, 128)
    nt = d_ff // tc

    def body(ug_ref, uv_ref, wg_ref, wv_ref, bg_ref, bv_ref, a_ref):
        yg = _conv(ug_ref[...], wg_ref[...], bg_ref[...])
        yv = _conv(uv_ref[...], wv_ref[...], bv_ref[...])
        act = yg * _sigmoid(yg) * yv
        a_ref[...] = jnp.where(_row_valid(0, l), act, 0.0).astype(a_ref.dtype)

    return pl.pallas_call(
        body, name="conv_act_fwd",
        out_shape=jax.ShapeDtypeStruct((l, d_ff), MXU_DTYPE),
        grid=(nt,),
        in_specs=[pl.BlockSpec((l, tc), lambda j: (0, j)), pl.BlockSpec((l, tc), lambda j: (0, j + nt)),
                  pl.BlockSpec((8, tc), lambda j: (0, j)), pl.BlockSpec((8, tc), lambda j: (0, j + nt)),
                  pl.BlockSpec((1, tc), lambda j: (0, j)), pl.BlockSpec((1, tc), lambda j: (0, j + nt))],
        out_specs=pl.BlockSpec((l, tc), lambda j: (0, j)),
        compiler_params=_params(("parallel",)),
    )(u, u, conv_w, conv_w, conv_b, conv_b)


def _conv_act_bwd(u, conv_w, conv_b, d_act, d_ff):
    l = u.shape[0]
    tc = _divisor_tile(d_ff, 256, 128)
    nt = d_ff // tc

    def body(ug_ref, uv_ref, wg_ref, wv_ref, bg_ref, bv_ref, da_ref, du_ref, dwb_ref):
        valid = _row_valid(0, l)
        ug, uv = ug_ref[...], uv_ref[...]
        wg, wv = wg_ref[...], wv_ref[...]
        yg = _conv(ug, wg, bg_ref[...])
        yv = _conv(uv, wv, bv_ref[...])
        sig = _sigmoid(yg)
        da = jnp.where(valid, da_ref[...], 0.0)
        d_yv = da * (yg * sig)
        d_yg = da * yv * (sig * (1.0 + yg * (1.0 - sig)))
        for idx, (dy, uu, w) in enumerate(((d_yg, ug, wg), (d_yv, uv, wv))):
            du = w[2:3, :] * dy + w[1:2, :] * pltpu.roll(dy, l - 1, 0) + w[0:1, :] * pltpu.roll(dy, l - 2, 0)
            du_ref[idx] = jnp.where(valid, du, 0.0).astype(du_ref.dtype)
            dwb_ref[idx, 0:1, :] = jnp.sum(dy * pltpu.roll(uu, 2, 0), axis=0, keepdims=True)
            dwb_ref[idx, 1:2, :] = jnp.sum(dy * pltpu.roll(uu, 1, 0), axis=0, keepdims=True)
            dwb_ref[idx, 2:3, :] = jnp.sum(dy * uu, axis=0, keepdims=True)
            dwb_ref[idx, 3:4, :] = jnp.sum(dy, axis=0, keepdims=True)
            dwb_ref[idx, 4:8, :] = jnp.zeros((4, tc), F32)

    return pl.pallas_call(
        body, name="conv_act_bwd",
        out_shape=(jax.ShapeDtypeStruct((2, l, d_ff), MXU_DTYPE), jax.ShapeDtypeStruct((2, 8, d_ff), F32)),
        grid=(nt,),
        in_specs=[pl.BlockSpec((l, tc), lambda j: (0, j)), pl.BlockSpec((l, tc), lambda j: (0, j + nt)),
                  pl.BlockSpec((8, tc), lambda j: (0, j)), pl.BlockSpec((8, tc), lambda j: (0, j + nt)),
                  pl.BlockSpec((1, tc), lambda j: (0, j)), pl.BlockSpec((1, tc), lambda j: (0, j + nt)),
                  pl.BlockSpec((l, tc), lambda j: (0, j))],
        out_specs=(pl.BlockSpec((2, l, tc), lambda j: (0, 0, j)), pl.BlockSpec((2, 8, tc), lambda j: (0, 0, j))),
        compiler_params=_params(("parallel",)),
    )(u, u, conv_w, conv_w, conv_b, conv_b, d_act)


def _adamw(w, g, m, v, name):
    shape = w.shape
    if w.ndim == 1:
        as2d = (1, shape[0])
    else:
        as2d = (int(np.prod(shape[:-1])), shape[-1])
    r, c = as2d
    tr = _divisor_tile(r, 256, 8)
    spec = pl.BlockSpec((tr, c), lambda i: (i, 0))

    def body(w_ref, g_ref, m_ref, v_ref, d_ref, nm_ref, nv_ref):
        gv = g_ref[...]
        nm = ADAM_B1 * m_ref[...] + (1.0 - ADAM_B1) * gv
        nv = ADAM_B2 * v_ref[...] + (1.0 - ADAM_B2) * (gv * gv)
        m_hat = nm / (1.0 - ADAM_B1 ** ADAM_STEP)
        v_hat = nv / (1.0 - ADAM_B2 ** ADAM_STEP)
        d_ref[...] = -ADAM_LR * (m_hat / (jnp.sqrt(v_hat) + ADAM_EPS) + ADAM_WD * w_ref[...])
        nm_ref[...] = nm
        nv_ref[...] = nv

    sds = jax.ShapeDtypeStruct(as2d, F32)
    outs = pl.pallas_call(
        body, name=name, out_shape=(sds, sds, sds), grid=(r // tr,),
        in_specs=[spec] * 4, out_specs=(spec,) * 3,
        compiler_params=_params(("parallel",)),
    )(w.reshape(as2d), g.reshape(as2d), m.reshape(as2d), v.reshape(as2d))
    return tuple(o.reshape(shape) for o in outs)


def _pad_rows(a, rows):
    return jnp.pad(a, ((0, rows - a.shape[0]), (0, 0)))


def kernel(x, meta_tokens, norm1_gain, w_in, b_forget, ret_norm_gain, w_out, norm2_gain, w_up, conv_w, conv_b, w_down, final_norm_gain, loss_target, m_meta_tokens, m_norm1_gain, m_w_in, m_b_forget, m_ret_norm_gain, m_w_out, m_norm2_gain, m_w_up, m_conv_w, m_conv_b, m_w_down, m_final_norm_gain, v_meta_tokens, v_norm1_gain, v_w_in, v_b_forget, v_ret_norm_gain, v_w_out, v_norm2_gain, v_w_up, v_conv_w, v_conv_b, v_w_down, v_final_norm_gain):
    seq, d = x.shape[1], x.shape[2]
    l = CHUNK + seq
    d_ff = w_down.shape[1] * N_DEV
    up_shard = w_up.shape[2]
    assert 4 * up_shard == d_ff and w_in.shape[2] == WIN_SHARD and d == 2 * GROUP
    dev = _device_index()
    mx, my, mc = _my_position()
    core = jnp.reshape(mc, (1,)).astype(jnp.int32)
    chip = jnp.reshape(2 * mx + my, (1,)).astype(jnp.int32)
    dev1 = jnp.reshape(dev, (1,)).astype(jnp.int32)

    small = jnp.concatenate([meta_tokens.reshape(-1, 128), conv_w[0].reshape(-1, 128)], axis=0)
    n_meta_rows = N_META * (d // N_DEV) // 128
    small_rows = small.shape[0]
    small_all = _all_gather(_pad_rows(small, -(-small_rows // 8) * 8), "gather_small")
    meta_full = jnp.transpose(small_all[:, :n_meta_rows].reshape(N_DEV, N_META, d // N_DEV), (1, 0, 2)).reshape(N_META, d)
    conv_w_full = _pad_rows(jnp.transpose(small_all[:, n_meta_rows:small_rows].reshape(N_DEV, 3, up_shard),
                                          (1, 0, 2)).reshape(3, 2 * d_ff), 8)
    to_rows = lambda t: jnp.pad(jnp.transpose(t[0]), ((0, WIN_ROWS - WIN_SHARD), (0, 0)))
    from_rows = lambda t: jnp.transpose(t[:WIN_SHARD])[None]
    w_in_rows = to_rows(w_in)
    start_in = _gather_start(w_in_rows.astype(WIRE_DTYPE), dev1, small_all, "gather_w_in_start")

    h0 = jnp.concatenate([jnp.zeros((PAD_ROWS, d), F32), meta_full, x[0]], axis=0)
    consts = _retention_consts(l)
    bias_row = jnp.pad(b_forget, ((0, 0), (0, 128 - N_HEADS)))
    a = _rmsnorm_fwd(h0, norm1_gain + start_in[4][0, 0], "rmsnorm1")
    w_in_blocks = _gather_finish(start_in, a, "gather_w_in")
    start_out = _gather_start(w_out[0].astype(WIRE_DTYPE), dev1, w_in_blocks, "gather_w_out_start")
    w_in_full = _assemble_w_in(w_in_blocks).astype(MXU_DTYPE)
    proj = _mm_nt(a, w_in_full, F32, "mm_proj", after=start_out[4])
    w_out_blocks = _gather_finish(start_out, proj, "gather_w_out")
    start_up = _gather_start(w_up[0].astype(WIRE_DTYPE), dev1, w_out_blocks, "gather_w_up_start")
    ret_mix, ret_pre, ret_states = _retention_fwd(proj, ret_norm_gain + start_up[4][0, 0], consts)
    cum_bc, cum_rows = _fox_prep(proj, bias_row + start_up[4][0:1, :])
    mix, lse_rows = _fox_fwd(proj, cum_bc, cum_rows, ret_mix)
    w_out_full = w_out_blocks.reshape(d, d).astype(MXU_DTYPE)
    h1, cn = _rmsnorm_fwd(h0, norm2_gain, "resid_rmsnorm2", res=_mm_nn(mix, w_out_full, F32, "mm_out"))
    w_up_blocks = _gather_finish(start_up, cn, "gather_w_up").astype(MXU_DTYPE)
    start_down = _gather_start(w_down[0].astype(WIRE_DTYPE), dev1, w_up_blocks, "gather_w_down_start")
    u = _mm(cn, w_up_blocks,
            a_spec=pl.BlockSpec((_divisor_tile(l, 1088, 16), d), lambda i, j, k: (i, 0)),
            b_spec=pl.BlockSpec((None, d, up_shard), lambda i, j, k: (j, 0, 0)),
            o_spec=pl.BlockSpec((_divisor_tile(l, 1088, 16), up_shard), lambda i, j, k: (i, j)),
            out_shape=jax.ShapeDtypeStruct((l, 2 * d_ff), F32),
            grid=(l // _divisor_tile(l, 1088, 16), N_DEV, 1), contract=(1, 0), nk=1, name="mm_up",
            after=start_down[4])
    act = _conv_act_fwd(u, conv_w_full, conv_b + start_down[4][0, 0], d_ff)
    w_down_full = _gather_finish(start_down, act, "gather_w_down").reshape(d_ff, d).astype(MXU_DTYPE)
    mlp_out = _mm_nn(act, w_down_full, F32, "mm_down", tm_cap=544, tk_cap=d_ff)
    d_h2, d_h2_b, dg_final, loss_part = _loss_head(h1, mlp_out, final_norm_gain.reshape(1, d), loss_target[0])

    gw_down = _mm_tn(act, d_h2_b, WIRE_DTYPE, "mm_gw_down", tm_cap=1408, tn_cap=1024)
    d2d_down = _reduce_scatter_d2d_start(gw_down.reshape(N_DEV, d_ff // N_DEV, d), d_h2, "rs_w_down")
    d_act = _mm_nt(d_h2_b, w_down_full, F32, "mm_d_act", after=d2d_down[4])
    rs_down = _reduce_scatter_ici_start(d2d_down, d_act, core, "rs_w_down")
    d_u, d_conv = _conv_act_bwd(u, conv_w_full, conv_b + rs_down[4][0, 0], d_act, d_ff)
    tm = _divisor_tile(l, 1088, 16)
    gw_up = _mm(cn, d_u,
                a_spec=pl.BlockSpec((l, d // 2), lambda i, j, k: (0, i)),
                b_spec=pl.BlockSpec((None, l, up_shard), lambda i, j, k: (j // 4, 0, j % 4)),
                o_spec=pl.BlockSpec((None, d // 2, up_shard), lambda i, j, k: (j, i, 0)),
                out_shape=jax.ShapeDtypeStruct((N_DEV, d, up_shard), WIRE_DTYPE),
                grid=(2, N_DEV, 1), contract=(0, 0), nk=1, name="mm_gw_up")
    d2d_up = _reduce_scatter_d2d_start(gw_up, d_act, "rs_w_up")
    d_cn = _mm_d_cn(d_u, w_up_blocks, d2d_up[4])
    rs_up = _reduce_scatter_ici_start(d2d_up, d_cn, core, "rs_w_up")
    d_h1, d_h1_b, dg_norm2 = _rmsnorm_bwd(d_h2, d_cn, h1, norm2_gain + rs_up[4][0, 0], "rmsnorm2_bwd", True)

    gw_out = _mm_tn(mix, d_h1_b, WIRE_DTYPE, "mm_gw_out")
    d2d_out = _reduce_scatter_d2d_start(gw_out.reshape(N_DEV, d // N_DEV, d), d_cn, "rs_w_out")
    d_mix = _mm_nt(d_h1_b, w_out_full, F32, "mm_d_mix", after=d2d_out[4])
    d_fq, d_fk, d_fv, ds_sum = _fox_bwd(proj, cum_bc, cum_rows, d_mix, lse_rows)
    d_ff_tile, db_forget_row = _fox_gate_bwd(ds_sum, proj, bias_row)
    d_ret, dg_ret = _retention_bwd(proj, ret_pre, ret_states, d_mix, ret_norm_gain, consts)
    rs_out = _reduce_scatter_ici_start(d2d_out, d_ret, core, "rs_w_out")
    d_proj = jnp.concatenate(
        [d_ret, d_fq, d_fk, d_fv, d_ff_tile, jnp.zeros((l, WIN_N - 7 * GROUP - 128), MXU_DTYPE)], axis=1)
    gw_in = _mm_tn(d_proj, a, WIRE_DTYPE, "mm_gw_in", tm_cap=768, after=rs_out[4])
    rs_in = _reduce_scatter_start(_extract_w_in_windows(gw_in), core, "rs_w_in")
    d_a = _mm_nn(d_proj, w_in_full, F32, "mm_d_a", tm_cap=544, tn_cap=256, tk_cap=WIN_N, after=rs_in[4])
    d_h0, dg_norm1 = _rmsnorm_bwd(d_h1, d_a, h0, norm1_gain + rs_in[4][0, 0], "rmsnorm1_bwd", False)
    grad_x = d_h0[CHUNK:][None]
    d_meta = d_h0[PAD_ROWS:CHUNK]

    d_conv_w = jnp.concatenate([d_conv[0, 0:3], d_conv[1, 0:3]], axis=1)
    d_conv_b = jnp.concatenate([d_conv[0, 3:4], d_conv[1, 3:4]], axis=1)
    pieces = [loss_part[:, 0:1], dg_norm1, db_forget_row[:, 0:N_HEADS], dg_ret, dg_norm2, d_conv_b, dg_final,
              d_meta.reshape(1, -1), d_conv_w.reshape(1, -1)]
    sizes = [p.shape[1] for p in pieces]
    flat = jnp.concatenate(pieces, axis=1)
    padded = -(-flat.shape[1] // 1024) * 1024
    flat = jnp.pad(flat, ((0, 0), (0, padded - flat.shape[1]))).reshape(padded // 128, 128)
    small_ar = _small_all_reduce_start(flat, d_h0, "all_reduce_small")

    g_w_down = _reduce_scatter_finish(rs_down, small_ar[4], chip, "rs_w_down")[None]
    g_w_up = _reduce_scatter_finish(rs_up, g_w_down, chip, "rs_w_up")[None]
    g_w_out = _reduce_scatter_finish(rs_out, g_w_up, chip, "rs_w_out")[None]
    early = [_adamw(w, g, m, v, "adamw_" + n) for w, g, m, v, n in (
        (w_down, g_w_down, m_w_down, v_w_down, "w_down"), (w_up, g_w_up, m_w_up, v_w_up, "w_up"),
        (w_out, g_w_out, m_w_out, v_w_out, "w_out"))]
    g_w_in_rows = _reduce_scatter_finish(rs_in, early[1][2], chip, "rs_w_in")
    g_w_in = from_rows(g_w_in_rows)
    early.append(tuple(from_rows(t) for t in _adamw(
        w_in_rows, g_w_in_rows, to_rows(m_w_in), to_rows(v_w_in), "adamw_w_in")))
    total = _small_all_reduce_finish(small_ar, early[3][2], dev1, "all_reduce_small").reshape(1, padded)
    offs = np.concatenate([[0], np.cumsum(sizes)])
    take = lambda k: total[:, int(offs[k]):int(offs[k + 1])]
    loss = take(0).reshape(())
    g_norm1, g_bf, g_ret_gain, g_norm2 = take(1), take(2), take(3), take(4)
    g_conv_b, g_final = take(5), take(6).reshape(d)
    g_meta = lax.dynamic_slice(take(7).reshape(N_META, d), (jnp.int32(0), (dev * (d // N_DEV)).astype(jnp.int32)),
                               (N_META, d // N_DEV))
    g_conv_w = lax.dynamic_slice(take(8).reshape(3, 2 * d_ff), (jnp.int32(0), (dev * up_shard).astype(jnp.int32)),
                                 (3, up_shard))[None]

    weights = [meta_tokens, norm1_gain, w_in, b_forget, ret_norm_gain, w_out, norm2_gain, w_up, conv_w, conv_b,
               w_down, final_norm_gain]
    grads = [g_meta, g_norm1, g_w_in, g_bf, g_ret_gain, g_w_out, g_norm2, g_w_up, g_conv_w, g_conv_b, g_w_down,
             g_final]
    done = {"w_down": early[0], "w_up": early[1], "w_out": early[2], "w_in": early[3]}
    ms = [m_meta_tokens, m_norm1_gain, m_w_in, m_b_forget, m_ret_norm_gain, m_w_out, m_norm2_gain, m_w_up, m_conv_w,
          m_conv_b, m_w_down, m_final_norm_gain]
    vs = [v_meta_tokens, v_norm1_gain, v_w_in, v_b_forget, v_ret_norm_gain, v_w_out, v_norm2_gain, v_w_up, v_conv_w,
          v_conv_b, v_w_down, v_final_norm_gain]
    names = ["meta", "norm1", "w_in", "b_forget", "ret_gain", "w_out", "norm2", "w_up", "conv_w", "conv_b", "w_down",
             "final_gain"]
    deltas, new_ms, new_vs = [], [], []
    for w, g, m, v, n in zip(weights, grads, ms, vs, names):
        dl, nm, nv = done[n] if n in done else _adamw(w, g, m, v, "adamw_" + n)
        deltas.append(dl)
        new_ms.append(nm)
        new_vs.append(nv)
    return (loss, grad_x, *grads, *deltas, *new_ms, *new_vs)
```

```python
import functools

import numpy as np
import jax
import jax.numpy as jnp
from jax import lax
from jax.experimental import pallas as pl
from jax.experimental.pallas import tpu as pltpu

F32 = jnp.float32
MXU_DTYPE = jnp.bfloat16
WIRE_DTYPE = jnp.bfloat16

N_DEV = 8
N_META = 16
CHUNK = 128
PAD_ROWS = CHUNK - N_META
N_HEADS = 8
HEAD_DIM = 128
GROUP = N_HEADS * HEAD_DIM
IN_DIM = 7 * GROUP + N_HEADS
WIN_SHARD = IN_DIM // N_DEV
WIN_ROWS = 912
WIN_BLOCK = 1024
WIN_STRIDE = 896
WIN_N = 7680
ROPE_BASE = 10000.0
NORM_EPS = 1e-6
NEG_BIG = -1e30
ADAM_LR, ADAM_B1, ADAM_B2, ADAM_EPS, ADAM_WD, ADAM_STEP = 0.001, 0.9, 0.999, 1e-08, 0.01, 10
VMEM_LIMIT = 52 * 1024 * 1024
MESH = pl.DeviceIdType.MESH
ANY = pl.BlockSpec(memory_space=pl.ANY)
VMEM_SPEC = pl.BlockSpec(memory_space=pltpu.VMEM)


def _params(sem=None):
    kw = {"vmem_limit_bytes": VMEM_LIMIT}
    if sem is not None:
        kw["dimension_semantics"] = sem
    return pltpu.CompilerParams(**kw)


def _divisor_tile(n, cap, unit):
    if n <= cap:
        return n
    best = None
    for t in range(unit, cap + 1, unit):
        if n % t == 0:
            best = t
    assert best is not None, (n, cap, unit)
    return best


def _my_position():
    return lax.axis_index("x"), lax.axis_index("y"), lax.axis_index("c")


def _device_index():
    x, y, c = _my_position()
    return 4 * x + 2 * y + c


def _all_gather(shard, name):
    r, c = shard.shape

    def body(x_ref, out_ref, send_sems, recv_sems, local_sem):
        mx, my, mc = _my_position()
        me, sibling = (mx, my, mc), (mx, my, 1 - mc)
        chips = [(1 - mx, my), (mx, 1 - my), (1 - mx, 1 - my)]

        def slot(px, py, pc):
            return out_ref.at[4 * px + 2 * py + pc]

        def copy(k, block, to, src=None):
            return pltpu.make_async_remote_copy(
                src_ref=slot(*block) if src is None else src, dst_ref=slot(*block),
                send_sem=send_sems.at[k], recv_sem=recv_sems.at[k], device_id=to, device_id_type=MESH)

        mine = pltpu.make_async_copy(x_ref, slot(*me), local_sem)
        mine.start()
        first = [copy(0, me, sibling, src=x_ref)]
        first += [copy(1 + j, me, (*chip, mc), src=x_ref) for j, chip in enumerate(chips)]
        for cp in first:
            cp.start()
        passed = [copy(4 + j, (*chip, mc), sibling) for j, chip in enumerate(chips)]
        for j, chip in enumerate(chips):
            copy(1 + j, (*chip, mc), me).wait_recv()
            passed[j].start()
        copy(0, sibling, me).wait_recv()
        for j, chip in enumerate(chips):
            copy(4 + j, (*chip, 1 - mc), me).wait_recv()
        for cp in first + passed:
            cp.wait_send()
        mine.wait()

    return pl.pallas_call(
        body, name=name,
        out_shape=jax.ShapeDtypeStruct((N_DEV, r, c), shard.dtype),
        in_specs=[ANY], out_specs=ANY,
        scratch_shapes=[pltpu.SemaphoreType.DMA((7,)), pltpu.SemaphoreType.DMA((7,)), pltpu.SemaphoreType.DMA],
    )(shard)


HBM_SPEC = pl.BlockSpec(memory_space=pltpu.HBM)
SEM_SPEC = pl.BlockSpec(memory_space=pltpu.SEMAPHORE)
DATAFLOW_EFFECT = pltpu.SideEffectType.DATAFLOW_SIDE_EFFECTING


def _in_hbm(a):
    return pltpu.with_memory_space_constraint(a, pltpu.HBM)


def _split_start(src, land, make_copies, n_copies, after, name):
    if isinstance(land, tuple):
        land = lax.empty(land, src.dtype)
    land_shape = land.shape
    def body(src_ref, land_ref, after_ref, send_sems, recv_sems, src_thru, land_thru, token):
        for cp in make_copies(src_ref, land_ref, send_sems, recv_sems):
            cp.start()
        token[...] = jnp.zeros_like(token)

    return pl.pallas_call(
        body, name=name,
        out_shape=(pltpu.SemaphoreType.DMA((n_copies,)), pltpu.SemaphoreType.DMA((n_copies,)),
                   pltpu.HBM(src.shape, src.dtype), pltpu.HBM(land_shape, src.dtype),
                   jax.ShapeDtypeStruct((8, 128), F32)),
        in_specs=(HBM_SPEC, HBM_SPEC, ANY), out_specs=(SEM_SPEC, SEM_SPEC, HBM_SPEC, HBM_SPEC, VMEM_SPEC),
        input_output_aliases={0: 2, 1: 3},
        compiler_params=pltpu.CompilerParams(has_side_effects=DATAFLOW_EFFECT),
    )(_in_hbm(src), _in_hbm(land), after)


def _split_wait(started, after, make_copies, name):
    send_sems, recv_sems, src_thru, land_thru, _ = started

    def body(src_ref, land_ref, send_sems_ref, recv_sems_ref, after_ref, src_dead, land_out):
        for cp in make_copies(src_ref, land_ref, send_sems_ref, recv_sems_ref):
            cp.wait_send()
            cp.wait_recv()

    return pl.pallas_call(
        body, name=name,
        out_shape=(pltpu.HBM(src_thru.shape, src_thru.dtype), pltpu.HBM(land_thru.shape, land_thru.dtype)),
        in_specs=(HBM_SPEC, HBM_SPEC, SEM_SPEC, SEM_SPEC, ANY), out_specs=(HBM_SPEC, HBM_SPEC),
        input_output_aliases={0: 0, 1: 1},
        compiler_params=pltpu.CompilerParams(has_side_effects=DATAFLOW_EFFECT),
    )(src_thru, land_thru, send_sems, recv_sems, after)


def _gather_copies(x_ref, land_ref, send_sems, recv_sems):
    mx, my, mc = _my_position()
    me = 4 * mx + 2 * my + mc
    targets = [(mx, my, 1 - mc), (1 - mx, my, mc), (mx, 1 - my, mc), (1 - mx, 1 - my, mc)]
    return [pltpu.make_async_remote_copy(
        src_ref=x_ref, dst_ref=land_ref.at[me], send_sem=send_sems.at[k], recv_sem=recv_sems.at[k],
        device_id=t, device_id_type=MESH) for k, t in enumerate(targets)]


def _gather_start(shard, dev, after, name):
    r, c = shard.shape
    tr = _divisor_tile(r, 512, 16)

    def body(s_ref, x_ref, o_ref):
        o_ref[...] = x_ref[...]

    land = pl.pallas_call(
        body, name=name + "_own",
        out_shape=jax.ShapeDtypeStruct((N_DEV, r, c), shard.dtype),
        grid_spec=pltpu.PrefetchScalarGridSpec(
            num_scalar_prefetch=1, grid=(r // tr,),
            in_specs=[pl.BlockSpec((tr, c), lambda i, s: (i, 0))],
            out_specs=pl.BlockSpec((None, tr, c), lambda i, s: (s[0], i, 0))),
        compiler_params=_params(("parallel",)),
    )(dev, shard)
    return _split_start(shard, land, _gather_copies, 4, after, name)


def _gather_finish(started, after, name):
    _, land = _split_wait(started, after, _gather_copies, name + "_wait")

    def body(land_in, land_ref, send_sems, recv_sems):
        mx, my, mc = _my_position()
        chips = [(1 - mx, my), (mx, 1 - my), (1 - mx, 1 - my)]
        copies = [pltpu.make_async_remote_copy(
            src_ref=land_ref.at[4 * cx + 2 * cy + mc], dst_ref=land_ref.at[4 * cx + 2 * cy + mc],
            send_sem=send_sems.at[j], recv_sem=recv_sems.at[j],
            device_id=(mx, my, 1 - mc), device_id_type=MESH) for j, (cx, cy) in enumerate(chips)]
        for cp in copies:
            cp.start()
        for j, (cx, cy) in enumerate(chips):
            copies[j].wait_send()
            pltpu.make_async_remote_copy(
                src_ref=land_ref.at[4 * cx + 2 * cy + 1 - mc], dst_ref=land_ref.at[4 * cx + 2 * cy + 1 - mc],
                send_sem=send_sems.at[j], recv_sem=recv_sems.at[j],
                device_id=(mx, my, 1 - mc), device_id_type=MESH).wait_recv()

    return pl.pallas_call(
        body, name=name + "_pass",
        out_shape=jax.ShapeDtypeStruct(land.shape, land.dtype),
        in_specs=[ANY], out_specs=ANY,
        input_output_aliases={0: 0},
        scratch_shapes=[pltpu.SemaphoreType.DMA((3,)), pltpu.SemaphoreType.DMA((3,))],
    )(land)


def _chip_copies(p_ref, land_ref, send_sems, recv_sems):
    mx, my, mc = _my_position()
    chips = [(1 - mx, my), (mx, 1 - my), (1 - mx, 1 - my)]
    return [pltpu.make_async_remote_copy(
        src_ref=p_ref.at[2 * cx + cy], dst_ref=land_ref.at[j], send_sem=send_sems.at[j], recv_sem=recv_sems.at[j],
        device_id=(cx, cy, mc), device_id_type=MESH) for j, (cx, cy) in enumerate(chips)]


def _reduce_scatter_start(g, core, name):
    pair = _pair_sum(g, _exchange_sibling(g, name + "_d2d"), core, name + "_pairsum")
    return _split_start(pair, (3,) + pair.shape[1:], _chip_copies, 3, g, name + "_ici_start")


def _sibling_copies(g_ref, land_ref, send_sems, recv_sems):
    mx, my, mc = _my_position()
    return [pltpu.make_async_remote_copy(
        src_ref=g_ref.at[2 * k + (1 - mc)], dst_ref=land_ref.at[k], send_sem=send_sems.at[k], recv_sem=recv_sems.at[k],
        device_id=(mx, my, 1 - mc), device_id_type=MESH) for k in range(4)]


def _reduce_scatter_d2d_start(g, after, name):
    return _split_start(g, (4,) + g.shape[1:], _sibling_copies, 4, after, name + "_d2d_start")


def _reduce_scatter_ici_start(d2d_started, after, core, name):
    g, from_sibling = _split_wait(d2d_started, after, _sibling_copies, name + "_d2d_wait")
    pair = _pair_sum(g, from_sibling, core, name + "_pairsum")
    return _split_start(pair, (3,) + pair.shape[1:], _chip_copies, 3, g, name + "_ici_start")


def _reduce_scatter_finish(started, after, chip, name):
    pair, from_chips = _split_wait(started, after, _chip_copies, name + "_ici_wait")
    return _final_sum(pair, from_chips, chip, name + "_sum")


def _exchange_sibling(g, name):
    _, r, c = g.shape

    def body(g_ref, out_ref, send_sems, recv_sems):
        mx, my, mc = _my_position()
        copies = [
            pltpu.make_async_remote_copy(
                src_ref=g_ref.at[2 * k + (1 - mc)], dst_ref=out_ref.at[k],
                send_sem=send_sems.at[k], recv_sem=recv_sems.at[k],
                device_id=(mx, my, 1 - mc), device_id_type=MESH)
            for k in range(4)]
        for cp in copies:
            cp.start()
        for cp in copies:
            cp.wait()

    return pl.pallas_call(
        body, name=name,
        out_shape=jax.ShapeDtypeStruct((4, r, c), g.dtype),
        in_specs=[ANY], out_specs=ANY,
        scratch_shapes=[pltpu.SemaphoreType.DMA((4,)), pltpu.SemaphoreType.DMA((4,))],
    )(g)


def _pair_sum(g, recv, core, name):
    _, r, c = g.shape
    tr = _divisor_tile(r, 512, 16)

    def body(s_ref, g_ref, r_ref, o_ref):
        o_ref[...] = (g_ref[...].astype(F32) + r_ref[...].astype(F32)).astype(o_ref.dtype)

    return pl.pallas_call(
        body, name=name,
        out_shape=jax.ShapeDtypeStruct((4, r, c), g.dtype),
        grid_spec=pltpu.PrefetchScalarGridSpec(
            num_scalar_prefetch=1, grid=(4, r // tr),
            in_specs=[pl.BlockSpec((None, tr, c), lambda k, i, s: (2 * k + s[0], i, 0)),
                      pl.BlockSpec((None, tr, c), lambda k, i, s: (k, i, 0))],
            out_specs=pl.BlockSpec((None, tr, c), lambda k, i, s: (k, i, 0))),
        compiler_params=_params(("parallel", "parallel")),
    )(core, g, recv)


def _final_sum(p, recv, chip, name):
    _, r, c = p.shape
    tr = _divisor_tile(r, 512, 16)

    def body(s_ref, p_ref, r_ref, o_ref):
        acc = p_ref[...].astype(F32)
        for j in range(3):
            acc = acc + r_ref[j].astype(F32)
        o_ref[...] = acc

    return pl.pallas_call(
        body, name=name,
        out_shape=jax.ShapeDtypeStruct((r, c), F32),
        grid_spec=pltpu.PrefetchScalarGridSpec(
            num_scalar_prefetch=1, grid=(r // tr,),
            in_specs=[pl.BlockSpec((None, tr, c), lambda i, s: (s[0], i, 0)),
                      pl.BlockSpec((3, tr, c), lambda i, s: (0, i, 0))],
            out_specs=pl.BlockSpec((tr, c), lambda i, s: (i, 0))),
        compiler_params=_params(("parallel",)),
    )(chip, p, recv)


def _all_to_all_copies(v_ref, land_ref, send_sems, recv_sems):
    mx, my, mc = _my_position()
    me = 4 * mx + 2 * my + mc
    copies = []
    for rel in range(1, N_DEV):
        bx, by, bc = (rel >> 2) & 1, (rel >> 1) & 1, rel & 1
        target = (1 - mx if bx else mx, 1 - my if by else my, 1 - mc if bc else mc)
        copies.append(pltpu.make_async_remote_copy(
            src_ref=v_ref, dst_ref=land_ref.at[me], send_sem=send_sems.at[rel - 1], recv_sem=recv_sems.at[rel - 1],
            device_id=target, device_id_type=MESH))
    return copies


def _small_all_reduce_start(v, after, name):
    return _split_start(v, (N_DEV,) + v.shape, _all_to_all_copies, N_DEV - 1, after, name + "_start")


def _small_all_reduce_finish(started, after, dev, name):
    v, land = _split_wait(started, after, _all_to_all_copies, name + "_wait")
    rows = v.shape[0]

    def body(me_ref, v_ref, land_ref, o_ref):
        for j in range(N_DEV):
            @pl.when(me_ref[0] == j)
            def _():
                o_ref[...] = v_ref[...] if j == 0 else o_ref[...] + v_ref[...]

            @pl.when(me_ref[0] != j)
            def _():
                o_ref[...] = land_ref[j] if j == 0 else o_ref[...] + land_ref[j]

    return pl.pallas_call(
        body, name=name + "_sum",
        out_shape=jax.ShapeDtypeStruct((rows, 128), F32),
        grid_spec=pltpu.PrefetchScalarGridSpec(
            num_scalar_prefetch=1, grid=(1,),
            in_specs=[pl.BlockSpec((rows, 128), lambda i, s: (0, 0)),
                      pl.BlockSpec((N_DEV, rows, 128), lambda i, s: (0, 0, 0))],
            out_specs=pl.BlockSpec((rows, 128), lambda i, s: (0, 0))),
        compiler_params=_params(("arbitrary",)),
    )(dev, v, land)


def _assemble_w_in(blocks):
    rows, d = WIN_ROWS, blocks.shape[2]
    tc = _divisor_tile(d, 256, 128)
    n_tiles = WIN_N // 128
    last = (N_DEV * WIN_STRIDE) // 128

    def body(b_ref, o_ref):
        win = []
        for i in range(N_DEV):
            w = jnp.concatenate([b_ref[i].astype(F32), jnp.zeros((WIN_BLOCK - rows, tc), F32)], axis=0)
            win.append(pltpu.roll(w, i, 0) if i else w)
        for t in range(n_tiles):
            if t > last:
                o_ref[t * 128:(t + 1) * 128, :] = jnp.zeros((128, tc), o_ref.dtype)
                continue
            i = min(t // 7, N_DEV - 1)
            k = t - 7 * i
            val = win[i][k * 128:(k + 1) * 128, :]
            if k == 0 and i >= 1:
                val = val + win[i - 1][7 * 128:8 * 128, :]
            o_ref[t * 128:(t + 1) * 128, :] = val.astype(o_ref.dtype)

    return pl.pallas_call(
        body, name="assemble_w_in",
        out_shape=jax.ShapeDtypeStruct((WIN_N, d), blocks.dtype),
        grid=(d // tc,),
        in_specs=[pl.BlockSpec((N_DEV, rows, tc), lambda j: (0, 0, j))],
        out_specs=pl.BlockSpec((WIN_N, tc), lambda j: (0, j)),
        compiler_params=_params(("parallel",)),
    )(blocks)


def _extract_w_in_windows(g):
    _, d = g.shape
    tc = _divisor_tile(d, 256, 128)

    def body(g_ref, o_ref):
        for j in range(N_DEV):
            w = g_ref[WIN_STRIDE * j:WIN_STRIDE * j + WIN_BLOCK, :].astype(F32)
            w = pltpu.roll(w, WIN_BLOCK - j, 0) if j else w
            o_ref[j] = w[0:WIN_ROWS, :].astype(o_ref.dtype)

    return pl.pallas_call(
        body, name="extract_w_in_windows",
        out_shape=jax.ShapeDtypeStruct((N_DEV, WIN_ROWS, d), g.dtype),
        grid=(d // tc,),
        in_specs=[pl.BlockSpec((WIN_N, tc), lambda j: (0, j))],
        out_specs=pl.BlockSpec((N_DEV, WIN_ROWS, tc), lambda j: (0, 0, j)),
        compiler_params=_params(("parallel",)),
    )(g)


def _mm(a, b, *, a_spec, b_spec, o_spec, out_shape, grid, contract, nk, name, after=None):
    dn = (((contract[0],), (contract[1],)), ((), ()))
    tm, tn = o_spec.block_shape[-2:]
    behind = [] if after is None else [after]

    def body(a_ref, b_ref, *rest):
        o_ref, *scratch = rest[len(behind):]
        part = lax.dot_general(a_ref[...], b_ref[...], dn, preferred_element_type=F32)
        if nk == 1:
            o_ref[...] = part.astype(o_ref.dtype)
            return
        acc = scratch[0]
        k = pl.program_id(2)

        @pl.when(k == 0)
        def _():
            acc[...] = part

        @pl.when(k > 0)
        def _():
            acc[...] += part

        @pl.when(k == nk - 1)
        def _():
            o_ref[...] = acc[...].astype(o_ref.dtype)

    return pl.pallas_call(
        body, name=name, out_shape=out_shape, grid=grid,
        in_specs=[a_spec, b_spec] + [ANY] * len(behind), out_specs=o_spec,
        scratch_shapes=[] if nk == 1 else [pltpu.VMEM((tm, tn), F32)],
        compiler_params=_params(("parallel", "parallel", "arbitrary")),
    )(a, b, *behind)


def _mm_nn(a, b, out_dtype, name, tm_cap=1088, tn_cap=512, tk_cap=2048, after=None):
    m, k = a.shape
    _, n = b.shape
    tm, tn, tk = _divisor_tile(m, tm_cap, 16), _divisor_tile(n, tn_cap, 128), _divisor_tile(k, tk_cap, 128)
    return _mm(a, b,
               a_spec=pl.BlockSpec((tm, tk), lambda i, j, kk: (i, kk)),
               b_spec=pl.BlockSpec((tk, tn), lambda i, j, kk: (kk, j)),
               o_spec=pl.BlockSpec((tm, tn), lambda i, j, kk: (i, j)),
               out_shape=jax.ShapeDtypeStruct((m, n), out_dtype),
               grid=(m // tm, n // tn, k // tk), contract=(1, 0), nk=k // tk, name=name, after=after)


def _mm_nt(a, b, out_dtype, name, tm_cap=1088, tn_cap=512, tk_cap=2048, after=None):
    m, k = a.shape
    n, _ = b.shape
    tm, tn, tk = _divisor_tile(m, tm_cap, 16), _divisor_tile(n, tn_cap, 128), _divisor_tile(k, tk_cap, 128)
    return _mm(a, b,
               a_spec=pl.BlockSpec((tm, tk), lambda i, j, kk: (i, kk)),
               b_spec=pl.BlockSpec((tn, tk), lambda i, j, kk: (j, kk)),
               o_spec=pl.BlockSpec((tm, tn), lambda i, j, kk: (i, j)),
               out_shape=jax.ShapeDtypeStruct((m, n), out_dtype),
               grid=(m // tm, n // tn, k // tk), contract=(1, 1), nk=k // tk, name=name, after=after)


def _mm_tn(a, b, out_dtype, name, tm_cap=1024, tn_cap=512, after=None):
    l, m = a.shape
    _, n = b.shape
    tm, tn = _divisor_tile(m, tm_cap, 128), _divisor_tile(n, tn_cap, 128)
    return _mm(a, b,
               a_spec=pl.BlockSpec((l, tm), lambda i, j, kk: (0, i)),
               b_spec=pl.BlockSpec((l, tn), lambda i, j, kk: (0, j)),
               o_spec=pl.BlockSpec((tm, tn), lambda i, j, kk: (i, j)),
               out_shape=jax.ShapeDtypeStruct((m, n), out_dtype),
               grid=(m // tm, n // tn, 1), contract=(0, 0), nk=1, name=name, after=after)


def _mm_d_cn(d_u, w_up_blocks, after):
    _, l, d_ff = d_u.shape
    n, d, shard = w_up_blocks.shape
    per = d_ff // shard
    tm, tn = _divisor_tile(l, 544, 16), _divisor_tile(d, 256, 128)

    def body(a_ref, b_ref, after_ref, o_ref):
        acc = None
        for k in range(n):
            part = _dot_nt(a_ref[k // per, :, (k % per) * shard:(k % per + 1) * shard], b_ref[k])
            acc = part if acc is None else acc + part
        o_ref[...] = acc

    return pl.pallas_call(
        body, name="mm_d_cn", out_shape=jax.ShapeDtypeStruct((l, d), F32), grid=(l // tm, d // tn),
        in_specs=[pl.BlockSpec((2, tm, d_ff), lambda i, j: (0, i, 0)),
                  pl.BlockSpec((n, tn, shard), lambda i, j: (0, j, 0)), ANY],
        out_specs=pl.BlockSpec((tm, tn), lambda i, j: (i, j)),
        compiler_params=_params(("parallel", "parallel")),
    )(d_u, w_up_blocks, after)


def _row_tile(l):
    return _divisor_tile(l, 544, 8)


def _rmsnorm_fwd(h, gain, name, res=None):
    l, d = h.shape
    tr = _row_tile(l)
    row = pl.BlockSpec((tr, d), lambda i: (i, 0))
    vec = pl.BlockSpec((1, d), lambda i: (0, 0))

    def body(*refs):
        if res is None:
            h_ref, g_ref, n_ref = refs
            x = h_ref[...]
        else:
            h_ref, r_ref, g_ref, s_ref, n_ref = refs
            x = h_ref[...] + r_ref[...]
            s_ref[...] = x
        y = x * lax.rsqrt(jnp.mean(x * x, axis=-1, keepdims=True) + NORM_EPS)
        n_ref[...] = (y * g_ref[...]).astype(n_ref.dtype)

    normed = jax.ShapeDtypeStruct((l, d), MXU_DTYPE)
    if res is None:
        return pl.pallas_call(body, name=name, out_shape=normed, grid=(l // tr,), in_specs=[row, vec],
                              out_specs=row, compiler_params=_params(("parallel",)))(h, gain)
    return pl.pallas_call(body, name=name, out_shape=(jax.ShapeDtypeStruct((l, d), F32), normed),
                          grid=(l // tr,), in_specs=[row, row, vec], out_specs=(row, row),
                          compiler_params=_params(("parallel",)))(h, res, gain)


def _rmsnorm_bwd(d_res, d_normed, x, gain, name, with_mxu_copy):
    l, d = x.shape
    tr = _row_tile(l)
    row = pl.BlockSpec((tr, d), lambda i: (i, 0))
    vec = pl.BlockSpec((1, d), lambda i: (0, 0))

    def body(dres_ref, dn_ref, x_ref, g_ref, dx_ref, *rest):
        dg_ref = rest[-1]
        xv = x_ref[...]
        r = lax.rsqrt(jnp.mean(xv * xv, axis=-1, keepdims=True) + NORM_EPS)
        xh = xv * r
        dn = dn_ref[...]
        dxh = dn * g_ref[...]
        dx = dres_ref[...] + r * (dxh - xh * jnp.mean(dxh * xh, axis=-1, keepdims=True))
        dx_ref[...] = dx
        if with_mxu_copy:
            rest[0][...] = dx.astype(MXU_DTYPE)

        @pl.when(pl.program_id(0) == 0)
        def _():
            dg_ref[...] = jnp.zeros_like(dg_ref)

        dg_ref[...] += jnp.sum(dn * xh, axis=0, keepdims=True)

    outs = [jax.ShapeDtypeStruct((l, d), F32)]
    specs = [row]
    if with_mxu_copy:
        outs.append(jax.ShapeDtypeStruct((l, d), MXU_DTYPE))
        specs.append(row)
    outs.append(jax.ShapeDtypeStruct((1, d), F32))
    specs.append(vec)
    return pl.pallas_call(body, name=name, out_shape=tuple(outs), grid=(l // tr,),
                          in_specs=[row, row, row, vec], out_specs=tuple(specs),
                          compiler_params=_params(("arbitrary",)))(d_res, d_normed, x, gain)


def _loss_head(h1, mlp_out, gain, target):
    l, d = h1.shape
    n_blocks = l // CHUNK
    row = pl.BlockSpec((CHUNK, d), lambda i: (i, 0))
    vec = pl.BlockSpec((1, d), lambda i: (0, 0))
    tgt = pl.BlockSpec((CHUNK, d), lambda i: (jnp.maximum(i - 1, 0), 0))

    def body(h_ref, m_ref, g_ref, t_ref, dh_ref, dhb_ref, dg_ref, loss_ref, sq_ref):
        i = pl.program_id(0)
        x = h_ref[...] + m_ref[...]
        r = lax.rsqrt(jnp.mean(x * x, axis=-1, keepdims=True) + NORM_EPS)
        xh = x * r
        g = g_ref[...]
        real = i >= 1
        err = jnp.where(real, xh * g - t_ref[...], 0.0)
        dy = err * (1.0 / d)
        dxh = dy * g
        dh = r * (dxh - xh * jnp.mean(dxh * xh, axis=-1, keepdims=True))
        dh_ref[...] = dh
        dhb_ref[...] = dh.astype(MXU_DTYPE)

        @pl.when(i == 0)
        def _():
            dg_ref[...] = jnp.zeros_like(dg_ref)
            sq_ref[...] = jnp.zeros_like(sq_ref)

        dg_ref[...] += jnp.sum(dy * xh, axis=0, keepdims=True)
        sq_ref[...] += jnp.sum(err * err, axis=0, keepdims=True)

        @pl.when(i == n_blocks - 1)
        def _():
            total = jnp.sum(sq_ref[...], axis=-1, keepdims=True) * (0.5 / d)
            loss_ref[...] = jnp.broadcast_to(total, (1, 128))

    return pl.pallas_call(
        body, name="loss_head",
        out_shape=(jax.ShapeDtypeStruct((l, d), F32), jax.ShapeDtypeStruct((l, d), MXU_DTYPE),
                   jax.ShapeDtypeStruct((1, d), F32), jax.ShapeDtypeStruct((1, 128), F32)),
        grid=(n_blocks,), in_specs=[row, row, vec, tgt],
        out_specs=(row, row, vec, pl.BlockSpec((1, 128), lambda i: (0, 0))),
        scratch_shapes=[pltpu.VMEM((1, d), F32)],
        compiler_params=_params(("arbitrary",)),
    )(h1, mlp_out, gain, target)


def _dot(a, b):
    return jnp.dot(a, b, preferred_element_type=F32)


def _dot_nt(a, b):
    return lax.dot_general(a, b, (((1,), (1,)), ((), ())), preferred_element_type=F32)


def _dot_tn(a, b):
    return lax.dot_general(a, b, (((0,), (0,)), ((), ())), preferred_element_type=F32)


def _rope(t, cos2, sin2):
    return t * cos2 + pltpu.roll(t, HEAD_DIM // 2, 1) * sin2


def _rope_bwd(dr, cos2, sin2):
    return dr * cos2 + pltpu.roll(dr * sin2, HEAD_DIM // 2, 1)


def _sigmoid(x):
    return 1.0 / (1.0 + jnp.exp(-x))


def _row_valid(block, rows):
    r = block * CHUNK + lax.broadcasted_iota(jnp.int32, (rows, 1), 0)
    return r >= PAD_ROWS


def _retention_consts(l):
    pos = jnp.arange(l, dtype=F32) - PAD_ROWS
    inv_freq = 1.0 / (ROPE_BASE ** (jnp.arange(0, HEAD_DIM, 2, dtype=F32) / HEAD_DIM))
    ang = pos[:, None] * inv_freq[None, :]
    cos, sin = jnp.cos(ang), jnp.sin(ang)
    cos2 = jnp.concatenate([cos, cos], axis=-1)
    sin2 = jnp.concatenate([-sin, sin], axis=-1)
    log_g = jnp.log1p(-jnp.exp2(-5.0 - jnp.arange(N_HEADS, dtype=F32)))
    idx = jnp.arange(CHUNK, dtype=F32)
    diff = idx[:, None] - idx[None, :]
    decay = jnp.where(diff >= 0, jnp.exp(jnp.maximum(diff, 0.0)[None] * log_g[:, None, None]), 0.0)
    xi = jnp.exp((idx + 1.0)[None, :] * log_g[:, None])
    zeta = jnp.exp((CHUNK - 1.0 - idx)[None, :] * log_g[:, None])
    g_chunk = jnp.exp(CHUNK * log_g)
    bcast = lambda v: jnp.broadcast_to(v[:, :, None], (N_HEADS, CHUNK, HEAD_DIM))
    g_rows = jnp.broadcast_to(g_chunk[:, None, None], (N_HEADS, 8, HEAD_DIM))
    return cos2, sin2, decay, bcast(xi), bcast(zeta), g_rows


def _retention_fwd(proj, ret_gain, consts):
    l = proj.shape[0]
    n_chunks = l // CHUNK
    cos2, sin2, decay, xi, zeta, g_rows = consts
    scale = HEAD_DIM ** -0.5

    def body(p_ref, cos_ref, sin_ref, dec_ref, xi_ref, zeta_ref, gr_ref, gain_ref,
             mix_ref, o_ref, st_ref, state):
        c = pl.program_id(0)

        @pl.when(c == 0)
        def _():
            state[...] = jnp.zeros_like(state)

        cos_v, sin_v = cos_ref[...], sin_ref[...]
        valid = _row_valid(c, CHUNK)
        for h in range(N_HEADS):
            cols = slice(h * HEAD_DIM, (h + 1) * HEAD_DIM)
            q = p_ref[:, h * HEAD_DIM:(h + 1) * HEAD_DIM]
            k = p_ref[:, GROUP + h * HEAD_DIM:GROUP + (h + 1) * HEAD_DIM]
            v = p_ref[:, 2 * GROUP + h * HEAD_DIM:2 * GROUP + (h + 1) * HEAD_DIM]
            g = p_ref[:, 3 * GROUP + h * HEAD_DIM:3 * GROUP + (h + 1) * HEAD_DIM]
            rq = _rope(q, cos_v, sin_v).astype(MXU_DTYPE)
            rk = _rope(k, cos_v, sin_v) * scale
            rkb = rk.astype(MXU_DTYPE)
            vb = v.astype(MXU_DTYPE)
            st = state[h]
            st_ref[h] = st
            s = _dot_nt(rq, rkb) * dec_ref[h]
            o = _dot(s.astype(MXU_DTYPE), vb) + _dot(rq, st.astype(MXU_DTYPE)) * xi_ref[h]
            kz = (rk * zeta_ref[h]).astype(MXU_DTYPE)
            state[h] = gr_ref[h, 0:1, :] * st + _dot_tn(kz, vb)
            o_ref[:, cols] = o
            mu = jnp.mean(o, axis=-1, keepdims=True)
            oc = o - mu
            yn = oc * lax.rsqrt(jnp.mean(oc * oc, axis=-1, keepdims=True) + NORM_EPS)
            ret = (g * _sigmoid(g)) * (yn * gain_ref[:, cols])
            mix_ref[:, cols] = jnp.where(valid, ret, 0.0).astype(mix_ref.dtype)

    head_tab = pl.BlockSpec((N_HEADS, CHUNK, HEAD_DIM), lambda c: (0, 0, 0))
    return pl.pallas_call(
        body, name="retention_fwd",
        out_shape=(jax.ShapeDtypeStruct((l, 2 * GROUP), MXU_DTYPE), jax.ShapeDtypeStruct((l, GROUP), F32),
                   jax.ShapeDtypeStruct((n_chunks, N_HEADS, HEAD_DIM, HEAD_DIM), F32)),
        grid=(n_chunks,),
        in_specs=[pl.BlockSpec((CHUNK, 4 * GROUP), lambda c: (c, 0)),
                  pl.BlockSpec((CHUNK, HEAD_DIM), lambda c: (c, 0)),
                  pl.BlockSpec((CHUNK, HEAD_DIM), lambda c: (c, 0)),
                  head_tab, head_tab, head_tab,
                  pl.BlockSpec((N_HEADS, 8, HEAD_DIM), lambda c: (0, 0, 0)),
                  pl.BlockSpec((1, GROUP), lambda c: (0, 0))],
        out_specs=(pl.BlockSpec((CHUNK, GROUP), lambda c: (c, 0)),
                   pl.BlockSpec((CHUNK, GROUP), lambda c: (c, 0)),
                   pl.BlockSpec((None, N_HEADS, HEAD_DIM, HEAD_DIM), lambda c: (c, 0, 0, 0))),
        scratch_shapes=[pltpu.VMEM((N_HEADS, HEAD_DIM, HEAD_DIM), F32)],
        compiler_params=_params(("arbitrary",)),
    )(proj, cos2, sin2, decay, xi, zeta, g_rows, ret_gain)


def _retention_bwd(proj, o_pre, states, d_mix, ret_gain, consts):
    l = proj.shape[0]
    n_chunks = l // CHUNK
    cos2, sin2, decay, xi, zeta, g_rows = consts
    scale = HEAD_DIM ** -0.5
    rev = lambda c: n_chunks - 1 - c

    def body(p_ref, o_ref, st_ref, dm_ref, cos_ref, sin_ref, dec_ref, xi_ref, zeta_ref, gr_ref, gain_ref,
             dp_ref, dgain_ref, dstate):
        step = pl.program_id(0)

        @pl.when(step == 0)
        def _():
            dstate[...] = jnp.zeros_like(dstate)
            dgain_ref[...] = jnp.zeros_like(dgain_ref)

        cos_v, sin_v = cos_ref[...], sin_ref[...]
        valid = _row_valid(rev(step), CHUNK)
        for h in range(N_HEADS):
            cols = slice(h * HEAD_DIM, (h + 1) * HEAD_DIM)
            q = p_ref[:, h * HEAD_DIM:(h + 1) * HEAD_DIM]
            k = p_ref[:, GROUP + h * HEAD_DIM:GROUP + (h + 1) * HEAD_DIM]
            v = p_ref[:, 2 * GROUP + h * HEAD_DIM:2 * GROUP + (h + 1) * HEAD_DIM]
            g = p_ref[:, 3 * GROUP + h * HEAD_DIM:3 * GROUP + (h + 1) * HEAD_DIM]
            o = o_ref[:, cols]
            gain = gain_ref[:, cols]
            d_ret = jnp.where(valid, dm_ref[:, cols], 0.0)
            mu = jnp.mean(o, axis=-1, keepdims=True)
            oc = o - mu
            rstd = lax.rsqrt(jnp.mean(oc * oc, axis=-1, keepdims=True) + NORM_EPS)
            yn = oc * rstd
            sig = _sigmoid(g)
            gate = g * sig
            dgain_ref[:, cols] += jnp.sum(d_ret * gate * yn, axis=0, keepdims=True)
            d_g = d_ret * (yn * gain) * (sig * (1.0 + g * (1.0 - sig)))
            d_yn = d_ret * gate * gain
            d_o = rstd * (d_yn - jnp.mean(d_yn, axis=-1, keepdims=True)
                          - yn * jnp.mean(d_yn * yn, axis=-1, keepdims=True))
            rq = _rope(q, cos_v, sin_v)
            rk = _rope(k, cos_v, sin_v) * scale
            rqb, rkb, vb = rq.astype(MXU_DTYPE), rk.astype(MXU_DTYPE), v.astype(MXU_DTYPE)
            dob = d_o.astype(MXU_DTYPE)
            dec = dec_ref[h]
            xi_h, zeta_h = xi_ref[h], zeta_ref[h]
            st_b = st_ref[h].astype(MXU_DTYPE)
            dst = dstate[h]
            dst_b = dst.astype(MXU_DTYPE)
            s_b = (_dot_nt(rqb, rkb) * dec).astype(MXU_DTYPE)
            da_b = (_dot_nt(dob, vb) * dec).astype(MXU_DTYPE)
            doxi_b = (d_o * xi_h).astype(MXU_DTYPE)
            kz_b = (rk * zeta_h).astype(MXU_DTYPE)
            d_rq = _dot(da_b, rkb) + _dot_nt(doxi_b, st_b)
            d_rk = _dot_tn(da_b, rqb) + _dot_nt(vb, dst_b) * zeta_h
            d_v = _dot_tn(s_b, dob) + _dot(kz_b, dst_b)
            dstate[h] = gr_ref[h, 0:1, :] * dst + _dot_tn(rqb, doxi_b)
            d_q = _rope_bwd(d_rq, cos_v, sin_v)
            d_k = _rope_bwd(d_rk * scale, cos_v, sin_v)
            dp_ref[:, h * HEAD_DIM:(h + 1) * HEAD_DIM] = d_q.astype(dp_ref.dtype)
            dp_ref[:, GROUP + h * HEAD_DIM:GROUP + (h + 1) * HEAD_DIM] = d_k.astype(dp_ref.dtype)
            dp_ref[:, 2 * GROUP + h * HEAD_DIM:2 * GROUP + (h + 1) * HEAD_DIM] = d_v.astype(dp_ref.dtype)
            dp_ref[:, 3 * GROUP + h * HEAD_DIM:3 * GROUP + (h + 1) * HEAD_DIM] = d_g.astype(dp_ref.dtype)

    head_tab = pl.BlockSpec((N_HEADS, CHUNK, HEAD_DIM), lambda c: (0, 0, 0))
    return pl.pallas_call(
        body, name="retention_bwd",
        out_shape=(jax.ShapeDtypeStruct((l, 4 * GROUP), MXU_DTYPE), jax.ShapeDtypeStruct((1, GROUP), F32)),
        grid=(n_chunks,),
        in_specs=[pl.BlockSpec((CHUNK, 4 * GROUP), lambda c: (rev(c), 0)),
                  pl.BlockSpec((CHUNK, GROUP), lambda c: (rev(c), 0)),
                  pl.BlockSpec((None, N_HEADS, HEAD_DIM, HEAD_DIM), lambda c: (rev(c), 0, 0, 0)),
                  pl.BlockSpec((CHUNK, GROUP), lambda c: (rev(c), 0)),
                  pl.BlockSpec((CHUNK, HEAD_DIM), lambda c: (rev(c), 0)),
                  pl.BlockSpec((CHUNK, HEAD_DIM), lambda c: (rev(c), 0)),
                  head_tab, head_tab, head_tab,
                  pl.BlockSpec((N_HEADS, 8, HEAD_DIM), lambda c: (0, 0, 0)),
                  pl.BlockSpec((1, GROUP), lambda c: (0, 0))],
        out_specs=(pl.BlockSpec((CHUNK, 4 * GROUP), lambda c: (rev(c), 0)),
                   pl.BlockSpec((1, GROUP), lambda c: (0, 0))),
        scratch_shapes=[pltpu.VMEM((N_HEADS, HEAD_DIM, HEAD_DIM), F32)],
        compiler_params=_params(("arbitrary",)),
    )(proj, o_pre, states, d_mix, cos2, sin2, decay, xi, zeta, g_rows, ret_gain)


FF_TILE = (7 * GROUP) // 128


def _log_forget(ff, bias_row, valid):
    x = ff + bias_row
    e = jnp.exp(-jnp.abs(x))
    lf = jnp.minimum(x, 0.0) - jnp.log(1.0 + e)
    head_lane = lax.broadcasted_iota(jnp.int32, x.shape, 1) < N_HEADS
    keep = lambda t: jnp.where(head_lane, jnp.where(valid, t, 0.0), 0.0)
    return keep(lf), keep(jnp.where(x >= 0, e, 1.0) / (1.0 + e))


def _fox_prep(proj, bias_row):
    l = proj.shape[0]
    n_blocks = l // CHUNK

    def body(ff_ref, b_ref, bc_ref, rows_ref, cum):
        r = lax.broadcasted_iota(jnp.int32, (CHUNK, CHUNK), 0)
        cidx = lax.broadcasted_iota(jnp.int32, (CHUNK, CHUNK), 1)
        tri = jnp.where(r >= cidx, 1.0, 0.0).astype(F32)
        carry = jnp.zeros((1, 128), F32)
        for blk in range(n_blocks):
            rows = slice(blk * CHUNK, (blk + 1) * CHUNK)
            valid = _row_valid(blk, CHUNK)
            lf, _ = _log_forget(ff_ref[rows, :], b_ref[...], valid)
            local = jnp.dot(tri, lf, precision=lax.Precision.HIGHEST, preferred_element_type=F32) + carry
            carry = local[CHUNK - 1:CHUNK, :]
            masked = jnp.where(valid, local, -NEG_BIG)
            cum[rows, :] = masked
            t = masked.T
            for h in range(N_HEADS):
                rows_ref[h, :, rows] = t[h:h + 1, :]
        full = cum[...]
        for h in range(N_HEADS):
            bc_ref[h] = jnp.broadcast_to(full[:, h:h + 1], (l, 128))

    return pl.pallas_call(
        body, name="fox_prep",
        out_shape=(jax.ShapeDtypeStruct((N_HEADS, l, 128), F32), jax.ShapeDtypeStruct((N_HEADS, 1, l), F32)),
        grid=(1,),
        in_specs=[pl.BlockSpec((l, 128), lambda i: (0, FF_TILE)), pl.BlockSpec((1, 128), lambda i: (0, 0))],
        out_specs=(pl.BlockSpec((N_HEADS, l, 128), lambda i: (0, 0, 0)),
                   pl.BlockSpec((N_HEADS, 1, l), lambda i: (0, 0, 0))),
        scratch_shapes=[pltpu.VMEM((l, 128), F32)],
        compiler_params=_params(("arbitrary",)),
    )(proj, bias_row)


ATTN_BLOCK = 2 * CHUNK


def _attn_blocks(l):
    assert (l - CHUNK) % ATTN_BLOCK == 0
    return [(0, CHUNK)] + [(s, ATTN_BLOCK) for s in range(CHUNK, l, ATTN_BLOCK)]


def _rows_valid(start, size):
    return start + lax.broadcasted_iota(jnp.int32, (size, 1), 0) >= PAD_ROWS


def _fox_fwd(proj, cum_bc, cum_rows, mix):
    l = proj.shape[0]
    blocks = _attn_blocks(l)
    scale = HEAD_DIM ** -0.5
    qt, kt, vt = 4 * N_HEADS, 5 * N_HEADS, 6 * N_HEADS

    def body(q_ref, k_ref, v_ref, cbc_ref, crow_ref, mix_in, o_ref, lse_ref, qb_s, kb_s, vb_s):
        qb_s[...] = q_ref[...].astype(MXU_DTYPE)
        kb_s[...] = k_ref[...].astype(MXU_DTYPE)
        vb_s[...] = v_ref[...].astype(MXU_DTYPE)
        for p, (qs, qn) in enumerate(blocks):
            qb = qb_s[qs:qs + qn, :]
            cq = cbc_ref[qs:qs + qn, :]
            m = jnp.full((qn, 1), NEG_BIG, F32)
            lsum = jnp.zeros((qn, 1), F32)
            acc = jnp.zeros((qn, HEAD_DIM), F32)
            for j in range(p + 1):
                ks, kn = blocks[j]
                bias = jnp.tile(cq, (1, kn // CHUNK)) - crow_ref[:, ks:ks + kn]
                s = _dot_nt(qb, kb_s[ks:ks + kn, :]) * scale + bias
                if j == p:
                    q_pos = qs + lax.broadcasted_iota(jnp.int32, (qn, kn), 0)
                    k_pos = ks + lax.broadcasted_iota(jnp.int32, (qn, kn), 1)
                    s = jnp.where(k_pos <= q_pos, s, NEG_BIG)
                m_new = jnp.maximum(m, jnp.max(s, axis=-1, keepdims=True))
                alpha = jnp.exp(m - m_new)
                pr = jnp.exp(s - m_new)
                lsum = lsum * alpha + jnp.sum(pr, axis=-1, keepdims=True)
                acc = acc * alpha + _dot(pr.astype(MXU_DTYPE), vb_s[ks:ks + kn, :])
                m = m_new
            o = jnp.where(_rows_valid(qs, qn), acc * (1.0 / lsum), 0.0)
            o_ref[qs:qs + qn, :] = o.astype(o_ref.dtype)
            lse = m + jnp.log(lsum)
            lse_ref[:, qs:qs + qn] = jnp.broadcast_to(lse, (qn, CHUNK)).T[0:1, :]

    head_col = lambda t: pl.BlockSpec((l, HEAD_DIM), lambda h: (0, t + h))
    return pl.pallas_call(
        body, name="fox_fwd",
        out_shape=(jax.ShapeDtypeStruct(mix.shape, mix.dtype), jax.ShapeDtypeStruct((N_HEADS, 1, l), F32)),
        grid=(N_HEADS,),
        in_specs=[head_col(qt), head_col(kt), head_col(vt),
                  pl.BlockSpec((None, l, 128), lambda h: (h, 0, 0)),
                  pl.BlockSpec((None, 1, l), lambda h: (h, 0, 0)),
                  ANY],
        out_specs=(head_col(N_HEADS), pl.BlockSpec((None, 1, l), lambda h: (h, 0, 0))),
        input_output_aliases={5: 0},
        scratch_shapes=[pltpu.VMEM((l, HEAD_DIM), MXU_DTYPE)] * 3,
        compiler_params=_params(("parallel",)),
    )(proj, proj, proj, cum_bc, cum_rows, mix)


def _fox_bwd(proj, cum_bc, cum_rows, d_mix, lse_rows):
    l = proj.shape[0]
    blocks = _attn_blocks(l)
    scale = HEAD_DIM ** -0.5
    qt, kt, vt = 4 * N_HEADS, 5 * N_HEADS, 6 * N_HEADS

    def body(q_ref, k_ref, v_ref, do_ref, cbc_ref, crow_ref, lse_ref,
             dq_ref, dk_ref, dv_ref, ds_ref, dk_acc, dv_acc, qb_s, kb_s, vb_s, dob_s):
        qb_s[...] = q_ref[...].astype(MXU_DTYPE)
        kb_s[...] = k_ref[...].astype(MXU_DTYPE)
        vb_s[...] = v_ref[...].astype(MXU_DTYPE)
        dob_s[...] = jnp.where(_rows_valid(0, l), do_ref[...], 0.0).astype(MXU_DTYPE)
        dk_acc[...] = jnp.zeros_like(dk_acc)
        dv_acc[...] = jnp.zeros_like(dv_acc)
        ds_ref[...] = jnp.zeros_like(ds_ref)
        shift_row = crow_ref[...] - lse_ref[...]

        for p, (qs, qn) in enumerate(blocks):
            qb, dob = qb_s[qs:qs + qn, :], dob_s[qs:qs + qn, :]
            shift = shift_row[:, qs:qs + qn]

            def probs(j):
                ks, kn = blocks[j]
                ck = jnp.tile(cbc_ref[ks:ks + kn, :], (1, qn // CHUNK))
                s_t = _dot_nt(kb_s[ks:ks + kn, :], qb) * scale + (shift - ck)
                if j == p:
                    k_pos = ks + lax.broadcasted_iota(jnp.int32, (kn, qn), 0)
                    q_pos = qs + lax.broadcasted_iota(jnp.int32, (kn, qn), 1)
                    s_t = jnp.where(k_pos <= q_pos, s_t, NEG_BIG)
                return jnp.exp(s_t), _dot_nt(vb_s[ks:ks + kn, :], dob)

            delta = jnp.zeros((1, qn), F32)
            for j in range(p + 1):
                p_t, dp_t = probs(j)
                delta = delta + jnp.sum(p_t * dp_t, axis=0, keepdims=True)
            dq = jnp.zeros((qn, HEAD_DIM), F32)
            for j in range(p + 1):
                ks, kn = blocks[j]
                rows = slice(ks, ks + kn)
                p_t, dp_t = probs(j)
                ds_t = p_t * (dp_t - delta)
                ds_b = ds_t.astype(MXU_DTYPE)
                dv_acc[rows, :] += _dot(p_t.astype(MXU_DTYPE), dob)
                dk_acc[rows, :] += _dot(ds_b, qb) * scale
                ds_ref[rows, :] += sum(ds_t[:, c:c + CHUNK] for c in range(0, qn, CHUNK))
                dq = dq + _dot_tn(ds_b, kb_s[rows, :])
            dq_ref[qs:qs + qn, :] = (dq * scale).astype(dq_ref.dtype)

        dk_ref[...] = dk_acc[...].astype(dk_ref.dtype)
        dv_ref[...] = dv_acc[...].astype(dv_ref.dtype)

    col = jax.ShapeDtypeStruct((l, GROUP), MXU_DTYPE)
    head_col = lambda t: pl.BlockSpec((l, HEAD_DIM), lambda h: (0, t + h))
    return pl.pallas_call(
        body, name="fox_bwd",
        out_shape=(col, col, col, jax.ShapeDtypeStruct((N_HEADS, l, 128), F32)),
        grid=(N_HEADS,),
        in_specs=[head_col(qt), head_col(kt), head_col(vt), head_col(N_HEADS),
                  pl.BlockSpec((None, l, 128), lambda h: (h, 0, 0)),
                  pl.BlockSpec((None, 1, l), lambda h: (h, 0, 0)),
                  pl.BlockSpec((None, 1, l), lambda h: (h, 0, 0))],
        out_specs=(head_col(0), head_col(0), head_col(0), pl.BlockSpec((None, l, 128), lambda h: (h, 0, 0))),
        scratch_shapes=[pltpu.VMEM((l, HEAD_DIM), F32)] * 2 + [pltpu.VMEM((l, HEAD_DIM), MXU_DTYPE)] * 4,
        compiler_params=_params(("parallel",)),
    )(proj, proj, proj, d_mix, cum_bc, cum_rows, lse_rows)


def _fox_gate_bwd(ds_sum, proj, bias_row):
    l = proj.shape[0]
    n_blocks = l // CHUNK

    def body(ds_ref, ff_ref, b_ref, dff_ref, db_ref):
        r = lax.broadcasted_iota(jnp.int32, (CHUNK, CHUNK), 0)
        cidx = lax.broadcasted_iota(jnp.int32, (CHUNK, CHUNK), 1)
        upper = jnp.where(cidx >= r, 1.0, 0.0).astype(F32)
        carry = jnp.zeros((1, 128), F32)
        db = jnp.zeros((1, 128), F32)
        for blk in reversed(range(n_blocks)):
            rows = slice(blk * CHUNK, (blk + 1) * CHUNK)
            key_sum = jnp.zeros((CHUNK, 128), F32)
            for h in range(N_HEADS):
                select = jnp.where(cidx == h, 1.0, 0.0).astype(F32)
                key_sum = key_sum + jnp.dot(ds_ref[h, rows, :], select, precision=lax.Precision.HIGHEST,
                                            preferred_element_type=F32)
            suffix = jnp.dot(upper, key_sum, precision=lax.Precision.HIGHEST, preferred_element_type=F32) + carry
            carry = suffix[0:1, :]
            _, dsig = _log_forget(ff_ref[rows, :], b_ref[...], _row_valid(blk, CHUNK))
            dff = -suffix * dsig
            dff_ref[rows, :] = dff.astype(dff_ref.dtype)
            db = db + jnp.sum(dff, axis=0, keepdims=True)
        db_ref[...] = db

    return pl.pallas_call(
        body, name="fox_gate_bwd",
        out_shape=(jax.ShapeDtypeStruct((l, 128), MXU_DTYPE), jax.ShapeDtypeStruct((1, 128), F32)),
        grid=(1,),
        in_specs=[pl.BlockSpec((N_HEADS, l, 128), lambda i: (0, 0, 0)),
                  pl.BlockSpec((l, 128), lambda i: (0, FF_TILE)),
                  pl.BlockSpec((1, 128), lambda i: (0, 0))],
        out_specs=(pl.BlockSpec((l, 128), lambda i: (0, 0)), pl.BlockSpec((1, 128), lambda i: (0, 0))),
        compiler_params=_params(("arbitrary",)),
    )(ds_sum, proj, bias_row)


def _conv(u, w, b):
    return b + w[0:1, :] * pltpu.roll(u, 2, 0) + w[1:2, :] * pltpu.roll(u, 1, 0) + w[2:3, :] * u


def _conv_act_fwd(u, conv_w, conv_b, d_ff):
    l = u.shape[0]
    tc = _divisor_tile(d_ff, 256, 128)
    nt = d_ff // tc

    def body(ug_ref, uv_ref, wg_ref, wv_ref, bg_ref, bv_ref, a_ref):
        yg = _conv(ug_ref[...], wg_ref[...], bg_ref[...])
        yv = _conv(uv_ref[...], wv_ref[...], bv_ref[...])
        act = yg * _sigmoid(yg) * yv
        a_ref[...] = jnp.where(_row_valid(0, l), act, 0.0).astype(a_ref.dtype)

    return pl.pallas_call(
        body, name="conv_act_fwd",
        out_shape=jax.ShapeDtypeStruct((l, d_ff), MXU_DTYPE),
        grid=(nt,),
        in_specs=[pl.BlockSpec((l, tc), lambda j: (0, j)), pl.BlockSpec((l, tc), lambda j: (0, j + nt)),
                  pl.BlockSpec((8, tc), lambda j: (0, j)), pl.BlockSpec((8, tc), lambda j: (0, j + nt)),
                  pl.BlockSpec((1, tc), lambda j: (0, j)), pl.BlockSpec((1, tc), lambda j: (0, j + nt))],
        out_specs=pl.BlockSpec((l, tc), lambda j: (0, j)),
        compiler_params=_params(("parallel",)),
    )(u, u, conv_w, conv_w, conv_b, conv_b)


def _conv_act_bwd(u, conv_w, conv_b, d_act, d_ff):
    l = u.shape[0]
    tc = _divisor_tile(d_ff, 256, 128)
    nt = d_ff // tc

    def body(ug_ref, uv_ref, wg_ref, wv_ref, bg_ref, bv_ref, da_ref, du_ref, dwb_ref):
        valid = _row_valid(0, l)
        ug, uv = ug_ref[...], uv_ref[...]
        wg, wv = wg_ref[...], wv_ref[...]
        yg = _conv(ug, wg, bg_ref[...])
        yv = _conv(uv, wv, bv_ref[...])
        sig = _sigmoid(yg)
        da = jnp.where(valid, da_ref[...], 0.0)
        d_yv = da * (yg * sig)
        d_yg = da * yv * (sig * (1.0 + yg * (1.0 - sig)))
        for idx, (dy, uu, w) in enumerate(((d_yg, ug, wg), (d_yv, uv, wv))):
            du = w[2:3, :] * dy + w[1:2, :] * pltpu.roll(dy, l - 1, 0) + w[0:1, :] * pltpu.roll(dy, l - 2, 0)
            du_ref[idx] = jnp.where(valid, du, 0.0).astype(du_ref.dtype)
            dwb_ref[idx, 0:1, :] = jnp.sum(dy * pltpu.roll(uu, 2, 0), axis=0, keepdims=True)
            dwb_ref[idx, 1:2, :] = jnp.sum(dy * pltpu.roll(uu, 1, 0), axis=0, keepdims=True)
            dwb_ref[idx, 2:3, :] = jnp.sum(dy * uu, axis=0, keepdims=True)
            dwb_ref[idx, 3:4, :] = jnp.sum(dy, axis=0, keepdims=True)
            dwb_ref[idx, 4:8, :] = jnp.zeros((4, tc), F32)

    return pl.pallas_call(
        body, name="conv_act_bwd",
        out_shape=(jax.ShapeDtypeStruct((2, l, d_ff), MXU_DTYPE), jax.ShapeDtypeStruct((2, 8, d_ff), F32)),
        grid=(nt,),
        in_specs=[pl.BlockSpec((l, tc), lambda j: (0, j)), pl.BlockSpec((l, tc), lambda j: (0, j + nt)),
                  pl.BlockSpec((8, tc), lambda j: (0, j)), pl.BlockSpec((8, tc), lambda j: (0, j + nt)),
                  pl.BlockSpec((1, tc), lambda j: (0, j)), pl.BlockSpec((1, tc), lambda j: (0, j + nt)),
                  pl.BlockSpec((l, tc), lambda j: (0, j))],
        out_specs=(pl.BlockSpec((2, l, tc), lambda j: (0, 0, j)), pl.BlockSpec((2, 8, tc), lambda j: (0, 0, j))),
        compiler_params=_params(("parallel",)),
    )(u, u, conv_w, conv_w, conv_b, conv_b, d_act)


def _adamw(w, g, m, v, name):
    shape = w.shape
    if w.ndim == 1:
        as2d = (1, shape[0])
    else:
        as2d = (int(np.prod(shape[:-1])), shape[-1])
    r, c = as2d
    tr = _divisor_tile(r, 256, 8)
    spec = pl.BlockSpec((tr, c), lambda i: (i, 0))

    def body(w_ref, g_ref, m_ref, v_ref, d_ref, nm_ref, nv_ref):
        gv = g_ref[...]
        nm = ADAM_B1 * m_ref[...] + (1.0 - ADAM_B1) * gv
        nv = ADAM_B2 * v_ref[...] + (1.0 - ADAM_B2) * (gv * gv)
        m_hat = nm / (1.0 - ADAM_B1 ** ADAM_STEP)
        v_hat = nv / (1.0 - ADAM_B2 ** ADAM_STEP)
        d_ref[...] = -ADAM_LR * (m_hat / (jnp.sqrt(v_hat) + ADAM_EPS) + ADAM_WD * w_ref[...])
        nm_ref[...] = nm
        nv_ref[...] = nv

    sds = jax.ShapeDtypeStruct(as2d, F32)
    outs = pl.pallas_call(
        body, name=name, out_shape=(sds, sds, sds), grid=(r // tr,),
        in_specs=[spec] * 4, out_specs=(spec,) * 3,
        compiler_params=_params(("parallel",)),
    )(w.reshape(as2d), g.reshape(as2d), m.reshape(as2d), v.reshape(as2d))
    return tuple(o.reshape(shape) for o in outs)


def _pad_rows(a, rows):
    return jnp.pad(a, ((0, rows - a.shape[0]), (0, 0)))


def kernel(x, meta_tokens, norm1_gain, w_in, b_forget, ret_norm_gain, w_out, norm2_gain, w_up, conv_w, conv_b, w_down, final_norm_gain, loss_target, m_meta_tokens, m_norm1_gain, m_w_in, m_b_forget, m_ret_norm_gain, m_w_out, m_norm2_gain, m_w_up, m_conv_w, m_conv_b, m_w_down, m_final_norm_gain, v_meta_tokens, v_norm1_gain, v_w_in, v_b_forget, v_ret_norm_gain, v_w_out, v_norm2_gain, v_w_up, v_conv_w, v_conv_b, v_w_down, v_final_norm_gain):
    seq, d = x.shape[1], x.shape[2]
    l = CHUNK + seq
    d_ff = w_down.shape[1] * N_DEV
    up_shard = w_up.shape[2]
    assert 4 * up_shard == d_ff and w_in.shape[2] == WIN_SHARD and d == 2 * GROUP
    dev = _device_index()
    mx, my, mc = _my_position()
    core = jnp.reshape(mc, (1,)).astype(jnp.int32)
    chip = jnp.reshape(2 * mx + my, (1,)).astype(jnp.int32)
    dev1 = jnp.reshape(dev, (1,)).astype(jnp.int32)

    small = jnp.concatenate([meta_tokens.reshape(-1, 128), conv_w[0].reshape(-1, 128)], axis=0)
    n_meta_rows = N_META * (d // N_DEV) // 128
    small_rows = small.shape[0]
    small_all = _all_gather(_pad_rows(small, -(-small_rows // 8) * 8), "gather_small")
    meta_full = jnp.transpose(small_all[:, :n_meta_rows].reshape(N_DEV, N_META, d // N_DEV), (1, 0, 2)).reshape(N_META, d)
    conv_w_full = _pad_rows(jnp.transpose(small_all[:, n_meta_rows:small_rows].reshape(N_DEV, 3, up_shard),
                                          (1, 0, 2)).reshape(3, 2 * d_ff), 8)
    to_rows = lambda t: jnp.pad(jnp.transpose(t[0]), ((0, WIN_ROWS - WIN_SHARD), (0, 0)))
    from_rows = lambda t: jnp.transpose(t[:WIN_SHARD])[None]
    w_in_rows = to_rows(w_in)
    out_rows = d // N_DEV
    mixer_rows = -(-(WIN_ROWS + out_rows) // 304) * 304
    mixer_shard = jnp.concatenate([w_in_rows.astype(WIRE_DTYPE), w_out[0].astype(WIRE_DTYPE),
                                   jnp.zeros((mixer_rows - WIN_ROWS - out_rows, d), WIRE_DTYPE)], axis=0)
    start_in = _gather_start(mixer_shard, dev1, small_all, "gather_w_in_start")

    h0 = jnp.concatenate([jnp.zeros((PAD_ROWS, d), F32), meta_full, x[0]], axis=0)
    consts = _retention_consts(l)
    bias_row = jnp.pad(b_forget, ((0, 0), (0, 128 - N_HEADS)))
    a = _rmsnorm_fwd(h0, norm1_gain + start_in[4][0, 0], "rmsnorm1")
    mixer_blocks = _gather_finish(start_in, a, "gather_w_in")
    start_up = _gather_start(w_up[0].astype(WIRE_DTYPE), dev1, mixer_blocks, "gather_w_up_start")
    w_in_full = _assemble_w_in(mixer_blocks).astype(MXU_DTYPE)
    proj = _mm_nt(a, w_in_full, F32, "mm_proj", after=start_up[4])
    ret_mix, ret_pre, ret_states = _retention_fwd(proj, ret_norm_gain, consts)
    cum_bc, cum_rows = _fox_prep(proj, bias_row)
    mix, lse_rows = _fox_fwd(proj, cum_bc, cum_rows, ret_mix)
    w_out_full = mixer_blocks[:, WIN_ROWS:WIN_ROWS + out_rows].reshape(d, d).astype(MXU_DTYPE)
    h1, cn = _rmsnorm_fwd(h0, norm2_gain, "resid_rmsnorm2", res=_mm_nn(mix, w_out_full, F32, "mm_out"))
    w_up_blocks = _gather_finish(start_up, cn, "gather_w_up").astype(MXU_DTYPE)
    start_down = _gather_start(w_down[0].astype(WIRE_DTYPE), dev1, w_up_blocks, "gather_w_down_start")
    u = _mm(cn, w_up_blocks,
            a_spec=pl.BlockSpec((_divisor_tile(l, 1088, 16), d), lambda i, j, k: (i, 0)),
            b_spec=pl.BlockSpec((None, d, up_shard), lambda i, j, k: (j, 0, 0)),
            o_spec=pl.BlockSpec((_divisor_tile(l, 1088, 16), up_shard), lambda i, j, k: (i, j)),
            out_shape=jax.ShapeDtypeStruct((l, 2 * d_ff), F32),
            grid=(l // _divisor_tile(l, 1088, 16), N_DEV, 1), contract=(1, 0), nk=1, name="mm_up",
            after=start_down[4])
    act = _conv_act_fwd(u, conv_w_full, conv_b + start_down[4][0, 0], d_ff)
    w_down_full = _gather_finish(start_down, act, "gather_w_down").reshape(d_ff, d).astype(MXU_DTYPE)
    mlp_out = _mm_nn(act, w_down_full, F32, "mm_down", tm_cap=544, tk_cap=d_ff)
    d_h2, d_h2_b, dg_final, loss_part = _loss_head(h1, mlp_out, final_norm_gain.reshape(1, d), loss_target[0])

    gw_down = _mm_tn(act, d_h2_b, WIRE_DTYPE, "mm_gw_down", tm_cap=1408, tn_cap=1024)
    d2d_down = _reduce_scatter_d2d_start(gw_down.reshape(N_DEV, d_ff // N_DEV, d), d_h2, "rs_w_down")
    d_act = _mm_nt(d_h2_b, w_down_full, F32, "mm_d_act", after=d2d_down[4])
    rs_down = _reduce_scatter_ici_start(d2d_down, d_act, core, "rs_w_down")
    d_u, d_conv = _conv_act_bwd(u, conv_w_full, conv_b + rs_down[4][0, 0], d_act, d_ff)
    tm = _divisor_tile(l, 1088, 16)
    gw_up = _mm(cn, d_u,
                a_spec=pl.BlockSpec((l, d // 2), lambda i, j, k: (0, i)),
                b_spec=pl.BlockSpec((None, l, up_shard), lambda i, j, k: (j // 4, 0, j % 4)),
                o_spec=pl.BlockSpec((None, d // 2, up_shard), lambda i, j, k: (j, i, 0)),
                out_shape=jax.ShapeDtypeStruct((N_DEV, d, up_shard), WIRE_DTYPE),
                grid=(2, N_DEV, 1), contract=(0, 0), nk=1, name="mm_gw_up")
    d2d_up = _reduce_scatter_d2d_start(gw_up, d_act, "rs_w_up")
    d_cn = _mm_d_cn(d_u, w_up_blocks, d2d_up[4])
    rs_up = _reduce_scatter_ici_start(d2d_up, d_cn, core, "rs_w_up")
    d_h1, d_h1_b, dg_norm2 = _rmsnorm_bwd(d_h2, d_cn, h1, norm2_gain + rs_up[4][0, 0], "rmsnorm2_bwd", True)

    gw_out = _mm_tn(mix, d_h1_b, WIRE_DTYPE, "mm_gw_out")
    d2d_out = _reduce_scatter_d2d_start(gw_out.reshape(N_DEV, d // N_DEV, d), d_cn, "rs_w_out")
    d_mix = _mm_nt(d_h1_b, w_out_full, F32, "mm_d_mix", after=d2d_out[4])
    d_fq, d_fk, d_fv, ds_sum = _fox_bwd(proj, cum_bc, cum_rows, d_mix, lse_rows)
    d_ff_tile, db_forget_row = _fox_gate_bwd(ds_sum, proj, bias_row)
    d_ret, dg_ret = _retention_bwd(proj, ret_pre, ret_states, d_mix, ret_norm_gain, consts)
    rs_out = _reduce_scatter_ici_start(d2d_out, d_ret, core, "rs_w_out")
    d_proj = jnp.concatenate(
        [d_ret, d_fq, d_fk, d_fv, d_ff_tile, jnp.zeros((l, WIN_N - 7 * GROUP - 128), MXU_DTYPE)], axis=1)
    gw_in = _mm_tn(d_proj, a, WIRE_DTYPE, "mm_gw_in", tm_cap=768, after=rs_out[4])
    rs_in = _reduce_scatter_start(_extract_w_in_windows(gw_in), core, "rs_w_in")
    d_a = _mm_nn(d_proj, w_in_full, F32, "mm_d_a", tm_cap=544, tn_cap=256, tk_cap=WIN_N, after=rs_in[4])
    d_h0, dg_norm1 = _rmsnorm_bwd(d_h1, d_a, h0, norm1_gain + rs_in[4][0, 0], "rmsnorm1_bwd", False)
    grad_x = d_h0[CHUNK:][None]
    d_meta = d_h0[PAD_ROWS:CHUNK]

    d_conv_w = jnp.concatenate([d_conv[0, 0:3], d_conv[1, 0:3]], axis=1)
    d_conv_b = jnp.concatenate([d_conv[0, 3:4], d_conv[1, 3:4]], axis=1)
    pieces = [loss_part[:, 0:1], dg_norm1, db_forget_row[:, 0:N_HEADS], dg_ret, dg_norm2, d_conv_b, dg_final,
              d_meta.reshape(1, -1), d_conv_w.reshape(1, -1)]
    sizes = [p.shape[1] for p in pieces]
    flat = jnp.concatenate(pieces, axis=1)
    padded = -(-flat.shape[1] // 1024) * 1024
    flat = jnp.pad(flat, ((0, 0), (0, padded - flat.shape[1]))).reshape(padded // 128, 128)
    small_ar = _small_all_reduce_start(flat, d_h0, "all_reduce_small")

    g_w_down = _reduce_scatter_finish(rs_down, small_ar[4], chip, "rs_w_down")[None]
    g_w_up = _reduce_scatter_finish(rs_up, g_w_down, chip, "rs_w_up")[None]
    g_w_out = _reduce_scatter_finish(rs_out, g_w_up, chip, "rs_w_out")[None]
    early = [_adamw(w, g, m, v, "adamw_" + n) for w, g, m, v, n in (
        (w_down, g_w_down, m_w_down, v_w_down, "w_down"), (w_up, g_w_up, m_w_up, v_w_up, "w_up"),
        (w_out, g_w_out, m_w_out, v_w_out, "w_out"))]
    g_w_in_rows = _reduce_scatter_finish(rs_in, early[1][2], chip, "rs_w_in")
    g_w_in = from_rows(g_w_in_rows)
    early.append(tuple(from_rows(t) for t in _adamw(
        w_in_rows, g_w_in_rows, to_rows(m_w_in), to_rows(v_w_in), "adamw_w_in")))
    total = _small_all_reduce_finish(small_ar, early[3][2], dev1, "all_reduce_small").reshape(1, padded)
    offs = np.concatenate([[0], np.cumsum(sizes)])
    take = lambda k: total[:, int(offs[k]):int(offs[k + 1])]
    loss = take(0).reshape(())
    g_norm1, g_bf, g_ret_gain, g_norm2 = take(1), take(2), take(3), take(4)
    g_conv_b, g_final = take(5), take(6).reshape(d)
    g_meta = lax.dynamic_slice(take(7).reshape(N_META, d), (jnp.int32(0), (dev * (d // N_DEV)).astype(jnp.int32)),
                               (N_META, d // N_DEV))
    g_conv_w = lax.dynamic_slice(take(8).reshape(3, 2 * d_ff), (jnp.int32(0), (dev * up_shard).astype(jnp.int32)),
                                 (3, up_shard))[None]

    weights = [meta_tokens, norm1_gain, w_in, b_forget, ret_norm_gain, w_out, norm2_gain, w_up, conv_w, conv_b,
               w_down, final_norm_gain]
    grads = [g_meta, g_norm1, g_w_in, g_bf, g_ret_gain, g_w_out, g_norm2, g_w_up, g_conv_w, g_conv_b, g_w_down,
             g_final]
    done = {"w_down": early[0], "w_up": early[1], "w_out": early[2], "w_in": early[3]}
    ms = [m_meta_tokens, m_norm1_gain, m_w_in, m_b_forget, m_ret_norm_gain, m_w_out, m_norm2_gain, m_w_up, m_conv_w,
          m_conv_b, m_w_down, m_final_norm_gain]
    vs = [v_meta_tokens, v_norm1_gain, v_w_in, v_b_forget, v_ret_norm_gain, v_w_out, v_norm2_gain, v_w_up, v_conv_w,
          v_conv_b, v_w_down, v_final_norm_gain]
    names = ["meta", "norm1", "w_in", "b_forget", "ret_gain", "w_out", "norm2", "w_up", "conv_w", "conv_b", "w_down",
             "final_gain"]
    deltas, new_ms, new_vs = [], [], []
    for w, g, m, v, n in zip(weights, grads, ms, vs, names):
        dl, nm, nv = done[n] if n in done else _adamw(w, g, m, v, "adamw_" + n)
        deltas.append(dl)
        new_ms.append(nm)
        new_vs.append(nv)
    return (loss, grad_x, *grads, *deltas, *new_ms, *new_vs)
```

```python
import functools

import numpy as np
import jax
import jax.numpy as jnp
from jax import lax
from jax.experimental import pallas as pl
from jax.experimental.pallas import tpu as pltpu

F32 = jnp.float32
MXU_DTYPE = jnp.bfloat16
WIRE_DTYPE = jnp.bfloat16

N_DEV = 8
N_META = 16
CHUNK = 128
PAD_ROWS = CHUNK - N_META
N_HEADS = 8
HEAD_DIM = 128
GROUP = N_HEADS * HEAD_DIM
IN_DIM = 7 * GROUP + N_HEADS
WIN_SHARD = IN_DIM // N_DEV
WIN_ROWS = 912
WIN_BLOCK = 1024
WIN_STRIDE = 896
WIN_N = 7680
ROPE_BASE = 10000.0
NORM_EPS = 1e-6
NEG_BIG = -1e30
ADAM_LR, ADAM_B1, ADAM_B2, ADAM_EPS, ADAM_WD, ADAM_STEP = 0.001, 0.9, 0.999, 1e-08, 0.01, 10
VMEM_LIMIT = 52 * 1024 * 1024
MESH = pl.DeviceIdType.MESH
ANY = pl.BlockSpec(memory_space=pl.ANY)
VMEM_SPEC = pl.BlockSpec(memory_space=pltpu.VMEM)


def _params(sem=None):
    kw = {"vmem_limit_bytes": VMEM_LIMIT}
    if sem is not None:
        kw["dimension_semantics"] = sem
    return pltpu.CompilerParams(**kw)


def _divisor_tile(n, cap, unit):
    if n <= cap:
        return n
    best = None
    for t in range(unit, cap + 1, unit):
        if n % t == 0:
            best = t
    assert best is not None, (n, cap, unit)
    return best


def _my_position():
    return lax.axis_index("x"), lax.axis_index("y"), lax.axis_index("c")


def _device_index():
    x, y, c = _my_position()
    return 4 * x + 2 * y + c


def _all_gather(shard, name):
    r, c = shard.shape

    def body(x_ref, out_ref, send_sems, recv_sems, local_sem):
        mx, my, mc = _my_position()
        me, sibling = (mx, my, mc), (mx, my, 1 - mc)
        chips = [(1 - mx, my), (mx, 1 - my), (1 - mx, 1 - my)]

        def slot(px, py, pc):
            return out_ref.at[4 * px + 2 * py + pc]

        def copy(k, block, to, src=None):
            return pltpu.make_async_remote_copy(
                src_ref=slot(*block) if src is None else src, dst_ref=slot(*block),
                send_sem=send_sems.at[k], recv_sem=recv_sems.at[k], device_id=to, device_id_type=MESH)

        mine = pltpu.make_async_copy(x_ref, slot(*me), local_sem)
        mine.start()
        first = [copy(0, me, sibling, src=x_ref)]
        first += [copy(1 + j, me, (*chip, mc), src=x_ref) for j, chip in enumerate(chips)]
        for cp in first:
            cp.start()
        passed = [copy(4 + j, (*chip, mc), sibling) for j, chip in enumerate(chips)]
        for j, chip in enumerate(chips):
            copy(1 + j, (*chip, mc), me).wait_recv()
            passed[j].start()
        copy(0, sibling, me).wait_recv()
        for j, chip in enumerate(chips):
            copy(4 + j, (*chip, 1 - mc), me).wait_recv()
        for cp in first + passed:
            cp.wait_send()
        mine.wait()

    return pl.pallas_call(
        body, name=name,
        out_shape=jax.ShapeDtypeStruct((N_DEV, r, c), shard.dtype),
        in_specs=[ANY], out_specs=ANY,
        scratch_shapes=[pltpu.SemaphoreType.DMA((7,)), pltpu.SemaphoreType.DMA((7,)), pltpu.SemaphoreType.DMA],
    )(shard)


HBM_SPEC = pl.BlockSpec(memory_space=pltpu.HBM)
SEM_SPEC = pl.BlockSpec(memory_space=pltpu.SEMAPHORE)
DATAFLOW_EFFECT = pltpu.SideEffectType.DATAFLOW_SIDE_EFFECTING


def _in_hbm(a):
    return pltpu.with_memory_space_constraint(a, pltpu.HBM)


def _split_start(src, land, make_copies, n_copies, after, name):
    if isinstance(land, tuple):
        land = lax.empty(land, src.dtype)
    land_shape = land.shape
    def body(src_ref, land_ref, after_ref, send_sems, recv_sems, src_thru, land_thru, token):
        for cp in make_copies(src_ref, land_ref, send_sems, recv_sems):
            cp.start()
        token[...] = jnp.zeros_like(token)

    return pl.pallas_call(
        body, name=name,
        out_shape=(pltpu.SemaphoreType.DMA((n_copies,)), pltpu.SemaphoreType.DMA((n_copies,)),
                   pltpu.HBM(src.shape, src.dtype), pltpu.HBM(land_shape, src.dtype),
                   jax.ShapeDtypeStruct((8, 128), F32)),
        in_specs=(HBM_SPEC, HBM_SPEC, ANY), out_specs=(SEM_SPEC, SEM_SPEC, HBM_SPEC, HBM_SPEC, VMEM_SPEC),
        input_output_aliases={0: 2, 1: 3},
        compiler_params=pltpu.CompilerParams(has_side_effects=DATAFLOW_EFFECT),
    )(_in_hbm(src), _in_hbm(land), after)


def _split_wait(started, after, make_copies, name):
    send_sems, recv_sems, src_thru, land_thru, _ = started

    def body(src_ref, land_ref, send_sems_ref, recv_sems_ref, after_ref, src_dead, land_out):
        for cp in make_copies(src_ref, land_ref, send_sems_ref, recv_sems_ref):
            cp.wait_send()
            cp.wait_recv()

    return pl.pallas_call(
        body, name=name,
        out_shape=(pltpu.HBM(src_thru.shape, src_thru.dtype), pltpu.HBM(land_thru.shape, land_thru.dtype)),
        in_specs=(HBM_SPEC, HBM_SPEC, SEM_SPEC, SEM_SPEC, ANY), out_specs=(HBM_SPEC, HBM_SPEC),
        input_output_aliases={0: 0, 1: 1},
        compiler_params=pltpu.CompilerParams(has_side_effects=DATAFLOW_EFFECT),
    )(src_thru, land_thru, send_sems, recv_sems, after)


def _gather_copies(x_ref, land_ref, send_sems, recv_sems):
    mx, my, mc = _my_position()
    me = 4 * mx + 2 * my + mc
    targets = [(mx, my, 1 - mc), (1 - mx, my, mc), (mx, 1 - my, mc), (1 - mx, 1 - my, mc)]
    return [pltpu.make_async_remote_copy(
        src_ref=x_ref, dst_ref=land_ref.at[me], send_sem=send_sems.at[k], recv_sem=recv_sems.at[k],
        device_id=t, device_id_type=MESH) for k, t in enumerate(targets)]


def _own_slot(shard, dev, name):
    r, c = shard.shape
    tr = _divisor_tile(r, 512, 16)

    def body(s_ref, x_ref, o_ref):
        o_ref[...] = x_ref[...]

    return pl.pallas_call(
        body, name=name,
        out_shape=jax.ShapeDtypeStruct((N_DEV, r, c), shard.dtype),
        grid_spec=pltpu.PrefetchScalarGridSpec(
            num_scalar_prefetch=1, grid=(r // tr,),
            in_specs=[pl.BlockSpec((tr, c), lambda i, s: (i, 0))],
            out_specs=pl.BlockSpec((None, tr, c), lambda i, s: (s[0], i, 0))),
        compiler_params=_params(("parallel",)),
    )(dev, shard)


def _gather_start(shard, dev, after, name):
    return _split_start(shard, _own_slot(shard, dev, name + "_own"), _gather_copies, 4, after, name)


def _gather_ring(shard, dev, after, name):
    r, c = shard.shape
    half = r // 2
    assert half % 16 == 0

    def body(x_ref, after_ref, land_in, land_ref, send_sems, recv_sems):
        mx, my, mc = _my_position()
        sibling, x_nbr, y_nbr = (mx, my, 1 - mc), (1 - mx, my, mc), (mx, 1 - my, mc)
        first, second = pl.ds(0, half), pl.ds(half, half)

        def slot(px, py, pc):
            return land_ref.at[4 * px + 2 * py + pc]

        def copy(k, src, dst, to):
            return pltpu.make_async_remote_copy(src_ref=src, dst_ref=dst, send_sem=send_sems.at[k],
                                                recv_sem=recv_sems.at[k], device_id=to, device_id_type=MESH)

        def arrived(k, dst):
            copy(k, dst, dst, sibling).wait_recv()

        mine = slot(mx, my, mc)
        from_x, from_y, from_d = slot(1 - mx, my, mc), slot(mx, 1 - my, mc), slot(1 - mx, 1 - my, mc)
        sent = [copy(0, x_ref, mine, sibling), copy(1, x_ref, mine, x_nbr), copy(2, x_ref, mine, y_nbr)]
        for cp in sent:
            cp.start()

        def send(k, src, to):
            cp = copy(k, src, src, to)
            cp.start()
            sent.append(cp)

        arrived(1, from_x)
        send(3, from_x.at[first], y_nbr)
        send(5, from_x, sibling)
        arrived(2, from_y)
        send(4, from_y.at[second], x_nbr)
        send(6, from_y, sibling)
        arrived(3, from_d.at[first])
        send(7, from_d.at[first], sibling)
        arrived(4, from_d.at[second])
        send(8, from_d.at[second], sibling)
        arrived(0, slot(mx, my, 1 - mc))
        arrived(5, slot(1 - mx, my, 1 - mc))
        arrived(6, slot(mx, 1 - my, 1 - mc))
        arrived(7, slot(1 - mx, 1 - my, 1 - mc).at[first])
        arrived(8, slot(1 - mx, 1 - my, 1 - mc).at[second])
        for cp in sent:
            cp.wait_send()

    land = _own_slot(shard, dev, name + "_own")
    return pl.pallas_call(
        body, name=name,
        out_shape=jax.ShapeDtypeStruct(land.shape, land.dtype),
        in_specs=[ANY, ANY, ANY], out_specs=ANY,
        input_output_aliases={2: 0},
        scratch_shapes=[pltpu.SemaphoreType.DMA((9,)), pltpu.SemaphoreType.DMA((9,))],
    )(shard, after, land)


def _gather_finish(started, after, name):
    _, land = _split_wait(started, after, _gather_copies, name + "_wait")

    def body(land_in, land_ref, send_sems, recv_sems):
        mx, my, mc = _my_position()
        chips = [(1 - mx, my), (mx, 1 - my), (1 - mx, 1 - my)]
        copies = [pltpu.make_async_remote_copy(
            src_ref=land_ref.at[4 * cx + 2 * cy + mc], dst_ref=land_ref.at[4 * cx + 2 * cy + mc],
            send_sem=send_sems.at[j], recv_sem=recv_sems.at[j],
            device_id=(mx, my, 1 - mc), device_id_type=MESH) for j, (cx, cy) in enumerate(chips)]
        for cp in copies:
            cp.start()
        for j, (cx, cy) in enumerate(chips):
            copies[j].wait_send()
            pltpu.make_async_remote_copy(
                src_ref=land_ref.at[4 * cx + 2 * cy + 1 - mc], dst_ref=land_ref.at[4 * cx + 2 * cy + 1 - mc],
                send_sem=send_sems.at[j], recv_sem=recv_sems.at[j],
                device_id=(mx, my, 1 - mc), device_id_type=MESH).wait_recv()

    return pl.pallas_call(
        body, name=name + "_pass",
        out_shape=jax.ShapeDtypeStruct(land.shape, land.dtype),
        in_specs=[ANY], out_specs=ANY,
        input_output_aliases={0: 0},
        scratch_shapes=[pltpu.SemaphoreType.DMA((3,)), pltpu.SemaphoreType.DMA((3,))],
    )(land)


def _chip_copies(p_ref, land_ref, send_sems, recv_sems):
    mx, my, mc = _my_position()
    chips = [(1 - mx, my), (mx, 1 - my), (1 - mx, 1 - my)]
    return [pltpu.make_async_remote_copy(
        src_ref=p_ref.at[2 * cx + cy], dst_ref=land_ref.at[j], send_sem=send_sems.at[j], recv_sem=recv_sems.at[j],
        device_id=(cx, cy, mc), device_id_type=MESH) for j, (cx, cy) in enumerate(chips)]


def _reduce_scatter_start(g, core, name):
    pair = _pair_sum(g, _exchange_sibling(g, name + "_d2d"), core, name + "_pairsum")
    return _split_start(pair, (3,) + pair.shape[1:], _chip_copies, 3, g, name + "_ici_start")


def _sibling_copies(g_ref, land_ref, send_sems, recv_sems):
    mx, my, mc = _my_position()
    return [pltpu.make_async_remote_copy(
        src_ref=g_ref.at[2 * k + (1 - mc)], dst_ref=land_ref.at[k], send_sem=send_sems.at[k], recv_sem=recv_sems.at[k],
        device_id=(mx, my, 1 - mc), device_id_type=MESH) for k in range(4)]


def _reduce_scatter_d2d_start(g, after, name):
    return _split_start(g, (4,) + g.shape[1:], _sibling_copies, 4, after, name + "_d2d_start")


def _reduce_scatter_ici_start(d2d_started, after, core, name):
    g, from_sibling = _split_wait(d2d_started, after, _sibling_copies, name + "_d2d_wait")
    pair = _pair_sum(g, from_sibling, core, name + "_pairsum")
    return _split_start(pair, (3,) + pair.shape[1:], _chip_copies, 3, g, name + "_ici_start")


def _reduce_scatter_finish(started, after, chip, name):
    pair, from_chips = _split_wait(started, after, _chip_copies, name + "_ici_wait")
    return _final_sum(pair, from_chips, chip, name + "_sum")


def _exchange_sibling(g, name):
    _, r, c = g.shape

    def body(g_ref, out_ref, send_sems, recv_sems):
        mx, my, mc = _my_position()
        copies = [
            pltpu.make_async_remote_copy(
                src_ref=g_ref.at[2 * k + (1 - mc)], dst_ref=out_ref.at[k],
                send_sem=send_sems.at[k], recv_sem=recv_sems.at[k],
                device_id=(mx, my, 1 - mc), device_id_type=MESH)
            for k in range(4)]
        for cp in copies:
            cp.start()
        for cp in copies:
            cp.wait()

    return pl.pallas_call(
        body, name=name,
        out_shape=jax.ShapeDtypeStruct((4, r, c), g.dtype),
        in_specs=[ANY], out_specs=ANY,
        scratch_shapes=[pltpu.SemaphoreType.DMA((4,)), pltpu.SemaphoreType.DMA((4,))],
    )(g)


def _pair_sum(g, recv, core, name):
    _, r, c = g.shape
    tr = _divisor_tile(r, 512, 16)

    def body(s_ref, g_ref, r_ref, o_ref):
        o_ref[...] = (g_ref[...].astype(F32) + r_ref[...].astype(F32)).astype(o_ref.dtype)

    return pl.pallas_call(
        body, name=name,
        out_shape=jax.ShapeDtypeStruct((4, r, c), g.dtype),
        grid_spec=pltpu.PrefetchScalarGridSpec(
            num_scalar_prefetch=1, grid=(4, r // tr),
            in_specs=[pl.BlockSpec((None, tr, c), lambda k, i, s: (2 * k + s[0], i, 0)),
                      pl.BlockSpec((None, tr, c), lambda k, i, s: (k, i, 0))],
            out_specs=pl.BlockSpec((None, tr, c), lambda k, i, s: (k, i, 0))),
        compiler_params=_params(("parallel", "parallel")),
    )(core, g, recv)


def _final_sum(p, recv, chip, name):
    _, r, c = p.shape
    tr = _divisor_tile(r, 512, 16)

    def body(s_ref, p_ref, r_ref, o_ref):
        acc = p_ref[...].astype(F32)
        for j in range(3):
            acc = acc + r_ref[j].astype(F32)
        o_ref[...] = acc

    return pl.pallas_call(
        body, name=name,
        out_shape=jax.ShapeDtypeStruct((r, c), F32),
        grid_spec=pltpu.PrefetchScalarGridSpec(
            num_scalar_prefetch=1, grid=(r // tr,),
            in_specs=[pl.BlockSpec((None, tr, c), lambda i, s: (s[0], i, 0)),
                      pl.BlockSpec((3, tr, c), lambda i, s: (0, i, 0))],
            out_specs=pl.BlockSpec((tr, c), lambda i, s: (i, 0))),
        compiler_params=_params(("parallel",)),
    )(chip, p, recv)


def _all_to_all_copies(v_ref, land_ref, send_sems, recv_sems):
    mx, my, mc = _my_position()
    me = 4 * mx + 2 * my + mc
    copies = []
    for rel in range(1, N_DEV):
        bx, by, bc = (rel >> 2) & 1, (rel >> 1) & 1, rel & 1
        target = (1 - mx if bx else mx, 1 - my if by else my, 1 - mc if bc else mc)
        copies.append(pltpu.make_async_remote_copy(
            src_ref=v_ref, dst_ref=land_ref.at[me], send_sem=send_sems.at[rel - 1], recv_sem=recv_sems.at[rel - 1],
            device_id=target, device_id_type=MESH))
    return copies


def _small_all_reduce_start(v, after, name):
    return _split_start(v, (N_DEV,) + v.shape, _all_to_all_copies, N_DEV - 1, after, name + "_start")


def _small_all_reduce_finish(started, after, dev, name):
    v, land = _split_wait(started, after, _all_to_all_copies, name + "_wait")
    rows = v.shape[0]

    def body(me_ref, v_ref, land_ref, o_ref):
        for j in range(N_DEV):
            @pl.when(me_ref[0] == j)
            def _():
                o_ref[...] = v_ref[...] if j == 0 else o_ref[...] + v_ref[...]

            @pl.when(me_ref[0] != j)
            def _():
                o_ref[...] = land_ref[j] if j == 0 else o_ref[...] + land_ref[j]

    return pl.pallas_call(
        body, name=name + "_sum",
        out_shape=jax.ShapeDtypeStruct((rows, 128), F32),
        grid_spec=pltpu.PrefetchScalarGridSpec(
            num_scalar_prefetch=1, grid=(1,),
            in_specs=[pl.BlockSpec((rows, 128), lambda i, s: (0, 0)),
                      pl.BlockSpec((N_DEV, rows, 128), lambda i, s: (0, 0, 0))],
            out_specs=pl.BlockSpec((rows, 128), lambda i, s: (0, 0))),
        compiler_params=_params(("arbitrary",)),
    )(dev, v, land)


def _assemble_w_in(blocks):
    rows, d = WIN_ROWS, blocks.shape[2]
    tc = _divisor_tile(d, 256, 128)
    n_tiles = WIN_N // 128
    last = (N_DEV * WIN_STRIDE) // 128

    def body(b_ref, o_ref):
        win = []
        for i in range(N_DEV):
            w = jnp.concatenate([b_ref[i].astype(F32), jnp.zeros((WIN_BLOCK - rows, tc), F32)], axis=0)
            win.append(pltpu.roll(w, i, 0) if i else w)
        for t in range(n_tiles):
            if t > last:
                o_ref[t * 128:(t + 1) * 128, :] = jnp.zeros((128, tc), o_ref.dtype)
                continue
            i = min(t // 7, N_DEV - 1)
            k = t - 7 * i
            val = win[i][k * 128:(k + 1) * 128, :]
            if k == 0 and i >= 1:
                val = val + win[i - 1][7 * 128:8 * 128, :]
            o_ref[t * 128:(t + 1) * 128, :] = val.astype(o_ref.dtype)

    return pl.pallas_call(
        body, name="assemble_w_in",
        out_shape=jax.ShapeDtypeStruct((WIN_N, d), blocks.dtype),
        grid=(d // tc,),
        in_specs=[pl.BlockSpec((N_DEV, rows, tc), lambda j: (0, 0, j))],
        out_specs=pl.BlockSpec((WIN_N, tc), lambda j: (0, j)),
        compiler_params=_params(("parallel",)),
    )(blocks)


def _extract_w_in_windows(g):
    _, d = g.shape
    tc = _divisor_tile(d, 256, 128)

    def body(g_ref, o_ref):
        for j in range(N_DEV):
            w = g_ref[WIN_STRIDE * j:WIN_STRIDE * j + WIN_BLOCK, :].astype(F32)
            w = pltpu.roll(w, WIN_BLOCK - j, 0) if j else w
            o_ref[j] = w[0:WIN_ROWS, :].astype(o_ref.dtype)

    return pl.pallas_call(
        body, name="extract_w_in_windows",
        out_shape=jax.ShapeDtypeStruct((N_DEV, WIN_ROWS, d), g.dtype),
        grid=(d // tc,),
        in_specs=[pl.BlockSpec((WIN_N, tc), lambda j: (0, j))],
        out_specs=pl.BlockSpec((N_DEV, WIN_ROWS, tc), lambda j: (0, 0, j)),
        compiler_params=_params(("parallel",)),
    )(g)


def _mm(a, b, *, a_spec, b_spec, o_spec, out_shape, grid, contract, nk, name, after=None):
    dn = (((contract[0],), (contract[1],)), ((), ()))
    tm, tn = o_spec.block_shape[-2:]
    behind = [] if after is None else [after]

    def body(a_ref, b_ref, *rest):
        o_ref, *scratch = rest[len(behind):]
        part = lax.dot_general(a_ref[...], b_ref[...], dn, preferred_element_type=F32)
        if nk == 1:
            o_ref[...] = part.astype(o_ref.dtype)
            return
        acc = scratch[0]
        k = pl.program_id(2)

        @pl.when(k == 0)
        def _():
            acc[...] = part

        @pl.when(k > 0)
        def _():
            acc[...] += part

        @pl.when(k == nk - 1)
        def _():
            o_ref[...] = acc[...].astype(o_ref.dtype)

    return pl.pallas_call(
        body, name=name, out_shape=out_shape, grid=grid,
        in_specs=[a_spec, b_spec] + [ANY] * len(behind), out_specs=o_spec,
        scratch_shapes=[] if nk == 1 else [pltpu.VMEM((tm, tn), F32)],
        compiler_params=_params(("parallel", "parallel", "arbitrary")),
    )(a, b, *behind)


def _mm_nn(a, b, out_dtype, name, tm_cap=1088, tn_cap=512, tk_cap=2048, after=None):
    m, k = a.shape
    _, n = b.shape
    tm, tn, tk = _divisor_tile(m, tm_cap, 16), _divisor_tile(n, tn_cap, 128), _divisor_tile(k, tk_cap, 128)
    return _mm(a, b,
               a_spec=pl.BlockSpec((tm, tk), lambda i, j, kk: (i, kk)),
               b_spec=pl.BlockSpec((tk, tn), lambda i, j, kk: (kk, j)),
               o_spec=pl.BlockSpec((tm, tn), lambda i, j, kk: (i, j)),
               out_shape=jax.ShapeDtypeStruct((m, n), out_dtype),
               grid=(m // tm, n // tn, k // tk), contract=(1, 0), nk=k // tk, name=name, after=after)


def _mm_nt(a, b, out_dtype, name, tm_cap=1088, tn_cap=512, tk_cap=2048, after=None):
    m, k = a.shape
    n, _ = b.shape
    tm, tn, tk = _divisor_tile(m, tm_cap, 16), _divisor_tile(n, tn_cap, 128), _divisor_tile(k, tk_cap, 128)
    return _mm(a, b,
               a_spec=pl.BlockSpec((tm, tk), lambda i, j, kk: (i, kk)),
               b_spec=pl.BlockSpec((tn, tk), lambda i, j, kk: (j, kk)),
               o_spec=pl.BlockSpec((tm, tn), lambda i, j, kk: (i, j)),
               out_shape=jax.ShapeDtypeStruct((m, n), out_dtype),
               grid=(m // tm, n // tn, k // tk), contract=(1, 1), nk=k // tk, name=name, after=after)


def _mm_tn(a, b, out_dtype, name, tm_cap=1024, tn_cap=512, after=None):
    l, m = a.shape
    _, n = b.shape
    tm, tn = _divisor_tile(m, tm_cap, 128), _divisor_tile(n, tn_cap, 128)
    return _mm(a, b,
               a_spec=pl.BlockSpec((l, tm), lambda i, j, kk: (0, i)),
               b_spec=pl.BlockSpec((l, tn), lambda i, j, kk: (0, j)),
               o_spec=pl.BlockSpec((tm, tn), lambda i, j, kk: (i, j)),
               out_shape=jax.ShapeDtypeStruct((m, n), out_dtype),
               grid=(m // tm, n // tn, 1), contract=(0, 0), nk=1, name=name, after=after)


def _mm_d_cn(d_u, w_up_blocks, after):
    _, l, d_ff = d_u.shape
    n, d, shard = w_up_blocks.shape
    per = d_ff // shard
    tm, tn = _divisor_tile(l, 544, 16), _divisor_tile(d, 256, 128)

    def body(a_ref, b_ref, after_ref, o_ref):
        acc = None
        for k in range(n):
            part = _dot_nt(a_ref[k // per, :, (k % per) * shard:(k % per + 1) * shard], b_ref[k])
            acc = part if acc is None else acc + part
        o_ref[...] = acc

    return pl.pallas_call(
        body, name="mm_d_cn", out_shape=jax.ShapeDtypeStruct((l, d), F32), grid=(l // tm, d // tn),
        in_specs=[pl.BlockSpec((2, tm, d_ff), lambda i, j: (0, i, 0)),
                  pl.BlockSpec((n, tn, shard), lambda i, j: (0, j, 0)), ANY],
        out_specs=pl.BlockSpec((tm, tn), lambda i, j: (i, j)),
        compiler_params=_params(("parallel", "parallel")),
    )(d_u, w_up_blocks, after)


def _row_tile(l):
    return _divisor_tile(l, 544, 8)


def _rmsnorm_fwd(h, gain, name, res=None):
    l, d = h.shape
    tr = _row_tile(l)
    row = pl.BlockSpec((tr, d), lambda i: (i, 0))
    vec = pl.BlockSpec((1, d), lambda i: (0, 0))

    def body(*refs):
        if res is None:
            h_ref, g_ref, n_ref = refs
            x = h_ref[...]
        else:
            h_ref, r_ref, g_ref, s_ref, n_ref = refs
            x = h_ref[...] + r_ref[...]
            s_ref[...] = x
        y = x * lax.rsqrt(jnp.mean(x * x, axis=-1, keepdims=True) + NORM_EPS)
        n_ref[...] = (y * g_ref[...]).astype(n_ref.dtype)

    normed = jax.ShapeDtypeStruct((l, d), MXU_DTYPE)
    if res is None:
        return pl.pallas_call(body, name=name, out_shape=normed, grid=(l // tr,), in_specs=[row, vec],
                              out_specs=row, compiler_params=_params(("parallel",)))(h, gain)
    return pl.pallas_call(body, name=name, out_shape=(jax.ShapeDtypeStruct((l, d), F32), normed),
                          grid=(l // tr,), in_specs=[row, row, vec], out_specs=(row, row),
                          compiler_params=_params(("parallel",)))(h, res, gain)


def _rmsnorm_bwd(d_res, d_normed, x, gain, name, with_mxu_copy):
    l, d = x.shape
    tr = _row_tile(l)
    row = pl.BlockSpec((tr, d), lambda i: (i, 0))
    vec = pl.BlockSpec((1, d), lambda i: (0, 0))

    def body(dres_ref, dn_ref, x_ref, g_ref, dx_ref, *rest):
        dg_ref = rest[-1]
        xv = x_ref[...]
        r = lax.rsqrt(jnp.mean(xv * xv, axis=-1, keepdims=True) + NORM_EPS)
        xh = xv * r
        dn = dn_ref[...]
        dxh = dn * g_ref[...]
        dx = dres_ref[...] + r * (dxh - xh * jnp.mean(dxh * xh, axis=-1, keepdims=True))
        dx_ref[...] = dx
        if with_mxu_copy:
            rest[0][...] = dx.astype(MXU_DTYPE)

        @pl.when(pl.program_id(0) == 0)
        def _():
            dg_ref[...] = jnp.zeros_like(dg_ref)

        dg_ref[...] += jnp.sum(dn * xh, axis=0, keepdims=True)

    outs = [jax.ShapeDtypeStruct((l, d), F32)]
    specs = [row]
    if with_mxu_copy:
        outs.append(jax.ShapeDtypeStruct((l, d), MXU_DTYPE))
        specs.append(row)
    outs.append(jax.ShapeDtypeStruct((1, d), F32))
    specs.append(vec)
    return pl.pallas_call(body, name=name, out_shape=tuple(outs), grid=(l // tr,),
                          in_specs=[row, row, row, vec], out_specs=tuple(specs),
                          compiler_params=_params(("arbitrary",)))(d_res, d_normed, x, gain)


def _loss_head(h1, mlp_out, gain, target):
    l, d = h1.shape
    n_blocks = l // CHUNK
    row = pl.BlockSpec((CHUNK, d), lambda i: (i, 0))
    vec = pl.BlockSpec((1, d), lambda i: (0, 0))
    tgt = pl.BlockSpec((CHUNK, d), lambda i: (jnp.maximum(i - 1, 0), 0))

    def body(h_ref, m_ref, g_ref, t_ref, dh_ref, dhb_ref, dg_ref, loss_ref, sq_ref):
        i = pl.program_id(0)
        x = h_ref[...] + m_ref[...]
        r = lax.rsqrt(jnp.mean(x * x, axis=-1, keepdims=True) + NORM_EPS)
        xh = x * r
        g = g_ref[...]
        real = i >= 1
        err = jnp.where(real, xh * g - t_ref[...], 0.0)
        dy = err * (1.0 / d)
        dxh = dy * g
        dh = r * (dxh - xh * jnp.mean(dxh * xh, axis=-1, keepdims=True))
        dh_ref[...] = dh
        dhb_ref[...] = dh.astype(MXU_DTYPE)

        @pl.when(i == 0)
        def _():
            dg_ref[...] = jnp.zeros_like(dg_ref)
            sq_ref[...] = jnp.zeros_like(sq_ref)

        dg_ref[...] += jnp.sum(dy * xh, axis=0, keepdims=True)
        sq_ref[...] += jnp.sum(err * err, axis=0, keepdims=True)

        @pl.when(i == n_blocks - 1)
        def _():
            total = jnp.sum(sq_ref[...], axis=-1, keepdims=True) * (0.5 / d)
            loss_ref[...] = jnp.broadcast_to(total, (1, 128))

    return pl.pallas_call(
        body, name="loss_head",
        out_shape=(jax.ShapeDtypeStruct((l, d), F32), jax.ShapeDtypeStruct((l, d), MXU_DTYPE),
                   jax.ShapeDtypeStruct((1, d), F32), jax.ShapeDtypeStruct((1, 128), F32)),
        grid=(n_blocks,), in_specs=[row, row, vec, tgt],
        out_specs=(row, row, vec, pl.BlockSpec((1, 128), lambda i: (0, 0))),
        scratch_shapes=[pltpu.VMEM((1, d), F32)],
        compiler_params=_params(("arbitrary",)),
    )(h1, mlp_out, gain, target)


def _dot(a, b):
    return jnp.dot(a, b, preferred_element_type=F32)


def _dot_nt(a, b):
    return lax.dot_general(a, b, (((1,), (1,)), ((), ())), preferred_element_type=F32)


def _dot_tn(a, b):
    return lax.dot_general(a, b, (((0,), (0,)), ((), ())), preferred_element_type=F32)


def _rope(t, cos2, sin2):
    return t * cos2 + pltpu.roll(t, HEAD_DIM // 2, 1) * sin2


def _rope_bwd(dr, cos2, sin2):
    return dr * cos2 + pltpu.roll(dr * sin2, HEAD_DIM // 2, 1)


def _sigmoid(x):
    return 1.0 / (1.0 + jnp.exp(-x))


def _row_valid(block, rows):
    r = block * CHUNK + lax.broadcasted_iota(jnp.int32, (rows, 1), 0)
    return r >= PAD_ROWS


def _retention_consts(l):
    pos = jnp.arange(l, dtype=F32) - PAD_ROWS
    inv_freq = 1.0 / (ROPE_BASE ** (jnp.arange(0, HEAD_DIM, 2, dtype=F32) / HEAD_DIM))
    ang = pos[:, None] * inv_freq[None, :]
    cos, sin = jnp.cos(ang), jnp.sin(ang)
    cos2 = jnp.concatenate([cos, cos], axis=-1)
    sin2 = jnp.concatenate([-sin, sin], axis=-1)
    log_g = jnp.log1p(-jnp.exp2(-5.0 - jnp.arange(N_HEADS, dtype=F32)))
    idx = jnp.arange(CHUNK, dtype=F32)
    diff = idx[:, None] - idx[None, :]
    decay = jnp.where(diff >= 0, jnp.exp(jnp.maximum(diff, 0.0)[None] * log_g[:, None, None]), 0.0)
    xi = jnp.exp((idx + 1.0)[None, :] * log_g[:, None])
    zeta = jnp.exp((CHUNK - 1.0 - idx)[None, :] * log_g[:, None])
    g_chunk = jnp.exp(CHUNK * log_g)
    bcast = lambda v: jnp.broadcast_to(v[:, :, None], (N_HEADS, CHUNK, HEAD_DIM))
    g_rows = jnp.broadcast_to(g_chunk[:, None, None], (N_HEADS, 8, HEAD_DIM))
    return cos2, sin2, decay, bcast(xi), bcast(zeta), g_rows


def _retention_fwd(proj, ret_gain, consts):
    l = proj.shape[0]
    n_chunks = l // CHUNK
    cos2, sin2, decay, xi, zeta, g_rows = consts
    scale = HEAD_DIM ** -0.5

    def body(p_ref, cos_ref, sin_ref, dec_ref, xi_ref, zeta_ref, gr_ref, gain_ref,
             mix_ref, o_ref, st_ref, state):
        c = pl.program_id(0)

        @pl.when(c == 0)
        def _():
            state[...] = jnp.zeros_like(state)

        cos_v, sin_v = cos_ref[...], sin_ref[...]
        valid = _row_valid(c, CHUNK)
        for h in range(N_HEADS):
            cols = slice(h * HEAD_DIM, (h + 1) * HEAD_DIM)
            q = p_ref[:, h * HEAD_DIM:(h + 1) * HEAD_DIM]
            k = p_ref[:, GROUP + h * HEAD_DIM:GROUP + (h + 1) * HEAD_DIM]
            v = p_ref[:, 2 * GROUP + h * HEAD_DIM:2 * GROUP + (h + 1) * HEAD_DIM]
            g = p_ref[:, 3 * GROUP + h * HEAD_DIM:3 * GROUP + (h + 1) * HEAD_DIM]
            rq = _rope(q, cos_v, sin_v).astype(MXU_DTYPE)
            rk = _rope(k, cos_v, sin_v) * scale
            rkb = rk.astype(MXU_DTYPE)
            vb = v.astype(MXU_DTYPE)
            st = state[h]
            st_ref[h] = st
            s = _dot_nt(rq, rkb) * dec_ref[h]
            o = _dot(s.astype(MXU_DTYPE), vb) + _dot(rq, st.astype(MXU_DTYPE)) * xi_ref[h]
            kz = (rk * zeta_ref[h]).astype(MXU_DTYPE)
            state[h] = gr_ref[h, 0:1, :] * st + _dot_tn(kz, vb)
            o_ref[:, cols] = o
            mu = jnp.mean(o, axis=-1, keepdims=True)
            oc = o - mu
            yn = oc * lax.rsqrt(jnp.mean(oc * oc, axis=-1, keepdims=True) + NORM_EPS)
            ret = (g * _sigmoid(g)) * (yn * gain_ref[:, cols])
            mix_ref[:, cols] = jnp.where(valid, ret, 0.0).astype(mix_ref.dtype)

    head_tab = pl.BlockSpec((N_HEADS, CHUNK, HEAD_DIM), lambda c: (0, 0, 0))
    return pl.pallas_call(
        body, name="retention_fwd",
        out_shape=(jax.ShapeDtypeStruct((l, 2 * GROUP), MXU_DTYPE), jax.ShapeDtypeStruct((l, GROUP), F32),
                   jax.ShapeDtypeStruct((n_chunks, N_HEADS, HEAD_DIM, HEAD_DIM), F32)),
        grid=(n_chunks,),
        in_specs=[pl.BlockSpec((CHUNK, 4 * GROUP), lambda c: (c, 0)),
                  pl.BlockSpec((CHUNK, HEAD_DIM), lambda c: (c, 0)),
                  pl.BlockSpec((CHUNK, HEAD_DIM), lambda c: (c, 0)),
                  head_tab, head_tab, head_tab,
                  pl.BlockSpec((N_HEADS, 8, HEAD_DIM), lambda c: (0, 0, 0)),
                  pl.BlockSpec((1, GROUP), lambda c: (0, 0))],
        out_specs=(pl.BlockSpec((CHUNK, GROUP), lambda c: (c, 0)),
                   pl.BlockSpec((CHUNK, GROUP), lambda c: (c, 0)),
                   pl.BlockSpec((None, N_HEADS, HEAD_DIM, HEAD_DIM), lambda c: (c, 0, 0, 0))),
        scratch_shapes=[pltpu.VMEM((N_HEADS, HEAD_DIM, HEAD_DIM), F32)],
        compiler_params=_params(("arbitrary",)),
    )(proj, cos2, sin2, decay, xi, zeta, g_rows, ret_gain)


def _retention_bwd(proj, o_pre, states, d_mix, ret_gain, consts):
    l = proj.shape[0]
    n_chunks = l // CHUNK
    cos2, sin2, decay, xi, zeta, g_rows = consts
    scale = HEAD_DIM ** -0.5
    rev = lambda c: n_chunks - 1 - c

    def body(p_ref, o_ref, st_ref, dm_ref, cos_ref, sin_ref, dec_ref, xi_ref, zeta_ref, gr_ref, gain_ref,
             dp_ref, dgain_ref, dstate):
        step = pl.program_id(0)

        @pl.when(step == 0)
        def _():
            dstate[...] = jnp.zeros_like(dstate)
            dgain_ref[...] = jnp.zeros_like(dgain_ref)

        cos_v, sin_v = cos_ref[...], sin_ref[...]
        valid = _row_valid(rev(step), CHUNK)
        for h in range(N_HEADS):
            cols = slice(h * HEAD_DIM, (h + 1) * HEAD_DIM)
            q = p_ref[:, h * HEAD_DIM:(h + 1) * HEAD_DIM]
            k = p_ref[:, GROUP + h * HEAD_DIM:GROUP + (h + 1) * HEAD_DIM]
            v = p_ref[:, 2 * GROUP + h * HEAD_DIM:2 * GROUP + (h + 1) * HEAD_DIM]
            g = p_ref[:, 3 * GROUP + h * HEAD_DIM:3 * GROUP + (h + 1) * HEAD_DIM]
            o = o_ref[:, cols]
            gain = gain_ref[:, cols]
            d_ret = jnp.where(valid, dm_ref[:, cols], 0.0)
            mu = jnp.mean(o, axis=-1, keepdims=True)
            oc = o - mu
            rstd = lax.rsqrt(jnp.mean(oc * oc, axis=-1, keepdims=True) + NORM_EPS)
            yn = oc * rstd
            sig = _sigmoid(g)
            gate = g * sig
            dgain_ref[:, cols] += jnp.sum(d_ret * gate * yn, axis=0, keepdims=True)
            d_g = d_ret * (yn * gain) * (sig * (1.0 + g * (1.0 - sig)))
            d_yn = d_ret * gate * gain
            d_o = rstd * (d_yn - jnp.mean(d_yn, axis=-1, keepdims=True)
                          - yn * jnp.mean(d_yn * yn, axis=-1, keepdims=True))
            rq = _rope(q, cos_v, sin_v)
            rk = _rope(k, cos_v, sin_v) * scale
            rqb, rkb, vb = rq.astype(MXU_DTYPE), rk.astype(MXU_DTYPE), v.astype(MXU_DTYPE)
            dob = d_o.astype(MXU_DTYPE)
            dec = dec_ref[h]
            xi_h, zeta_h = xi_ref[h], zeta_ref[h]
            st_b = st_ref[h].astype(MXU_DTYPE)
            dst = dstate[h]
            dst_b = dst.astype(MXU_DTYPE)
            s_b = (_dot_nt(rqb, rkb) * dec).astype(MXU_DTYPE)
            da_b = (_dot_nt(dob, vb) * dec).astype(MXU_DTYPE)
            doxi_b = (d_o * xi_h).astype(MXU_DTYPE)
            kz_b = (rk * zeta_h).astype(MXU_DTYPE)
            d_rq = _dot(da_b, rkb) + _dot_nt(doxi_b, st_b)
            d_rk = _dot_tn(da_b, rqb) + _dot_nt(vb, dst_b) * zeta_h
            d_v = _dot_tn(s_b, dob) + _dot(kz_b, dst_b)
            dstate[h] = gr_ref[h, 0:1, :] * dst + _dot_tn(rqb, doxi_b)
            d_q = _rope_bwd(d_rq, cos_v, sin_v)
            d_k = _rope_bwd(d_rk * scale, cos_v, sin_v)
            dp_ref[:, h * HEAD_DIM:(h + 1) * HEAD_DIM] = d_q.astype(dp_ref.dtype)
            dp_ref[:, GROUP + h * HEAD_DIM:GROUP + (h + 1) * HEAD_DIM] = d_k.astype(dp_ref.dtype)
            dp_ref[:, 2 * GROUP + h * HEAD_DIM:2 * GROUP + (h + 1) * HEAD_DIM] = d_v.astype(dp_ref.dtype)
            dp_ref[:, 3 * GROUP + h * HEAD_DIM:3 * GROUP + (h + 1) * HEAD_DIM] = d_g.astype(dp_ref.dtype)

    head_tab = pl.BlockSpec((N_HEADS, CHUNK, HEAD_DIM), lambda c: (0, 0, 0))
    return pl.pallas_call(
        body, name="retention_bwd",
        out_shape=(jax.ShapeDtypeStruct((l, 4 * GROUP), MXU_DTYPE), jax.ShapeDtypeStruct((1, GROUP), F32)),
        grid=(n_chunks,),
        in_specs=[pl.BlockSpec((CHUNK, 4 * GROUP), lambda c: (rev(c), 0)),
                  pl.BlockSpec((CHUNK, GROUP), lambda c: (rev(c), 0)),
                  pl.BlockSpec((None, N_HEADS, HEAD_DIM, HEAD_DIM), lambda c: (rev(c), 0, 0, 0)),
                  pl.BlockSpec((CHUNK, GROUP), lambda c: (rev(c), 0)),
                  pl.BlockSpec((CHUNK, HEAD_DIM), lambda c: (rev(c), 0)),
                  pl.BlockSpec((CHUNK, HEAD_DIM), lambda c: (rev(c), 0)),
                  head_tab, head_tab, head_tab,
                  pl.BlockSpec((N_HEADS, 8, HEAD_DIM), lambda c: (0, 0, 0)),
                  pl.BlockSpec((1, GROUP), lambda c: (0, 0))],
        out_specs=(pl.BlockSpec((CHUNK, 4 * GROUP), lambda c: (rev(c), 0)),
                   pl.BlockSpec((1, GROUP), lambda c: (0, 0))),
        scratch_shapes=[pltpu.VMEM((N_HEADS, HEAD_DIM, HEAD_DIM), F32)],
        compiler_params=_params(("arbitrary",)),
    )(proj, o_pre, states, d_mix, cos2, sin2, decay, xi, zeta, g_rows, ret_gain)


FF_TILE = (7 * GROUP) // 128


def _log_forget(ff, bias_row, valid):
    x = ff + bias_row
    e = jnp.exp(-jnp.abs(x))
    lf = jnp.minimum(x, 0.0) - jnp.log(1.0 + e)
    head_lane = lax.broadcasted_iota(jnp.int32, x.shape, 1) < N_HEADS
    keep = lambda t: jnp.where(head_lane, jnp.where(valid, t, 0.0), 0.0)
    return keep(lf), keep(jnp.where(x >= 0, e, 1.0) / (1.0 + e))


def _fox_prep(proj, bias_row):
    l = proj.shape[0]
    n_blocks = l // CHUNK

    def body(ff_ref, b_ref, bc_ref, rows_ref, cum):
        r = lax.broadcasted_iota(jnp.int32, (CHUNK, CHUNK), 0)
        cidx = lax.broadcasted_iota(jnp.int32, (CHUNK, CHUNK), 1)
        tri = jnp.where(r >= cidx, 1.0, 0.0).astype(F32)
        carry = jnp.zeros((1, 128), F32)
        for blk in range(n_blocks):
            rows = slice(blk * CHUNK, (blk + 1) * CHUNK)
            valid = _row_valid(blk, CHUNK)
            lf, _ = _log_forget(ff_ref[rows, :], b_ref[...], valid)
            local = jnp.dot(tri, lf, precision=lax.Precision.HIGHEST, preferred_element_type=F32) + carry
            carry = local[CHUNK - 1:CHUNK, :]
            masked = jnp.where(valid, local, -NEG_BIG)
            cum[rows, :] = masked
            t = masked.T
            for h in range(N_HEADS):
                rows_ref[h, :, rows] = t[h:h + 1, :]
        full = cum[...]
        for h in range(N_HEADS):
            bc_ref[h] = jnp.broadcast_to(full[:, h:h + 1], (l, 128))

    return pl.pallas_call(
        body, name="fox_prep",
        out_shape=(jax.ShapeDtypeStruct((N_HEADS, l, 128), F32), jax.ShapeDtypeStruct((N_HEADS, 1, l), F32)),
        grid=(1,),
        in_specs=[pl.BlockSpec((l, 128), lambda i: (0, FF_TILE)), pl.BlockSpec((1, 128), lambda i: (0, 0))],
        out_specs=(pl.BlockSpec((N_HEADS, l, 128), lambda i: (0, 0, 0)),
                   pl.BlockSpec((N_HEADS, 1, l), lambda i: (0, 0, 0))),
        scratch_shapes=[pltpu.VMEM((l, 128), F32)],
        compiler_params=_params(("arbitrary",)),
    )(proj, bias_row)


ATTN_BLOCK = 2 * CHUNK


def _attn_blocks(l):
    assert (l - CHUNK) % ATTN_BLOCK == 0
    return [(0, CHUNK)] + [(s, ATTN_BLOCK) for s in range(CHUNK, l, ATTN_BLOCK)]


def _rows_valid(start, size):
    return start + lax.broadcasted_iota(jnp.int32, (size, 1), 0) >= PAD_ROWS


def _fox_fwd(proj, cum_bc, cum_rows, mix):
    l = proj.shape[0]
    blocks = _attn_blocks(l)
    scale = HEAD_DIM ** -0.5
    qt, kt, vt = 4 * N_HEADS, 5 * N_HEADS, 6 * N_HEADS

    def body(q_ref, k_ref, v_ref, cbc_ref, crow_ref, mix_in, o_ref, lse_ref, qb_s, kb_s, vb_s):
        qb_s[...] = q_ref[...].astype(MXU_DTYPE)
        kb_s[...] = k_ref[...].astype(MXU_DTYPE)
        vb_s[...] = v_ref[...].astype(MXU_DTYPE)
        for p, (qs, qn) in enumerate(blocks):
            qb = qb_s[qs:qs + qn, :]
            cq = cbc_ref[qs:qs + qn, :]
            m = jnp.full((qn, 1), NEG_BIG, F32)
            lsum = jnp.zeros((qn, 1), F32)
            acc = jnp.zeros((qn, HEAD_DIM), F32)
            for j in range(p + 1):
                ks, kn = blocks[j]
                bias = jnp.tile(cq, (1, kn // CHUNK)) - crow_ref[:, ks:ks + kn]
                s = _dot_nt(qb, kb_s[ks:ks + kn, :]) * scale + bias
                if j == p:
                    q_pos = qs + lax.broadcasted_iota(jnp.int32, (qn, kn), 0)
                    k_pos = ks + lax.broadcasted_iota(jnp.int32, (qn, kn), 1)
                    s = jnp.where(k_pos <= q_pos, s, NEG_BIG)
                m_new = jnp.maximum(m, jnp.max(s, axis=-1, keepdims=True))
                alpha = jnp.exp(m - m_new)
                pr = jnp.exp(s - m_new)
                lsum = lsum * alpha + jnp.sum(pr, axis=-1, keepdims=True)
                acc = acc * alpha + _dot(pr.astype(MXU_DTYPE), vb_s[ks:ks + kn, :])
                m = m_new
            o = jnp.where(_rows_valid(qs, qn), acc * (1.0 / lsum), 0.0)
            o_ref[qs:qs + qn, :] = o.astype(o_ref.dtype)
            lse = m + jnp.log(lsum)
            lse_ref[:, qs:qs + qn] = jnp.broadcast_to(lse, (qn, CHUNK)).T[0:1, :]

    head_col = lambda t: pl.BlockSpec((l, HEAD_DIM), lambda h: (0, t + h))
    return pl.pallas_call(
        body, name="fox_fwd",
        out_shape=(jax.ShapeDtypeStruct(mix.shape, mix.dtype), jax.ShapeDtypeStruct((N_HEADS, 1, l), F32)),
        grid=(N_HEADS,),
        in_specs=[head_col(qt), head_col(kt), head_col(vt),
                  pl.BlockSpec((None, l, 128), lambda h: (h, 0, 0)),
                  pl.BlockSpec((None, 1, l), lambda h: (h, 0, 0)),
                  ANY],
        out_specs=(head_col(N_HEADS), pl.BlockSpec((None, 1, l), lambda h: (h, 0, 0))),
        input_output_aliases={5: 0},
        scratch_shapes=[pltpu.VMEM((l, HEAD_DIM), MXU_DTYPE)] * 3,
        compiler_params=_params(("parallel",)),
    )(proj, proj, proj, cum_bc, cum_rows, mix)


def _fox_bwd(proj, cum_bc, cum_rows, d_mix, lse_rows):
    l = proj.shape[0]
    blocks = _attn_blocks(l)
    scale = HEAD_DIM ** -0.5
    qt, kt, vt = 4 * N_HEADS, 5 * N_HEADS, 6 * N_HEADS

    def body(q_ref, k_ref, v_ref, do_ref, cbc_ref, crow_ref, lse_ref,
             dq_ref, dk_ref, dv_ref, ds_ref, dk_acc, dv_acc, qb_s, kb_s, vb_s, dob_s):
        qb_s[...] = q_ref[...].astype(MXU_DTYPE)
        kb_s[...] = k_ref[...].astype(MXU_DTYPE)
        vb_s[...] = v_ref[...].astype(MXU_DTYPE)
        dob_s[...] = jnp.where(_rows_valid(0, l), do_ref[...], 0.0).astype(MXU_DTYPE)
        dk_acc[...] = jnp.zeros_like(dk_acc)
        dv_acc[...] = jnp.zeros_like(dv_acc)
        ds_ref[...] = jnp.zeros_like(ds_ref)
        shift_row = crow_ref[...] - lse_ref[...]

        for p, (qs, qn) in enumerate(blocks):
            qb, dob = qb_s[qs:qs + qn, :], dob_s[qs:qs + qn, :]
            shift = shift_row[:, qs:qs + qn]

            def probs(j):
                ks, kn = blocks[j]
                ck = jnp.tile(cbc_ref[ks:ks + kn, :], (1, qn // CHUNK))
                s_t = _dot_nt(kb_s[ks:ks + kn, :], qb) * scale + (shift - ck)
                if j == p:
                    k_pos = ks + lax.broadcasted_iota(jnp.int32, (kn, qn), 0)
                    q_pos = qs + lax.broadcasted_iota(jnp.int32, (kn, qn), 1)
                    s_t = jnp.where(k_pos <= q_pos, s_t, NEG_BIG)
                return jnp.exp(s_t), _dot_nt(vb_s[ks:ks + kn, :], dob)

            delta = jnp.zeros((1, qn), F32)
            for j in range(p + 1):
                p_t, dp_t = probs(j)
                delta = delta + jnp.sum(p_t * dp_t, axis=0, keepdims=True)
            dq = jnp.zeros((qn, HEAD_DIM), F32)
            for j in range(p + 1):
                ks, kn = blocks[j]
                rows = slice(ks, ks + kn)
                p_t, dp_t = probs(j)
                ds_t = p_t * (dp_t - delta)
                ds_b = ds_t.astype(MXU_DTYPE)
                dv_acc[rows, :] += _dot(p_t.astype(MXU_DTYPE), dob)
                dk_acc[rows, :] += _dot(ds_b, qb) * scale
                ds_ref[rows, :] += sum(ds_t[:, c:c + CHUNK] for c in range(0, qn, CHUNK))
                dq = dq + _dot_tn(ds_b, kb_s[rows, :])
            dq_ref[qs:qs + qn, :] = (dq * scale).astype(dq_ref.dtype)

        dk_ref[...] = dk_acc[...].astype(dk_ref.dtype)
        dv_ref[...] = dv_acc[...].astype(dv_ref.dtype)

    col = jax.ShapeDtypeStruct((l, GROUP), MXU_DTYPE)
    head_col = lambda t: pl.BlockSpec((l, HEAD_DIM), lambda h: (0, t + h))
    return pl.pallas_call(
        body, name="fox_bwd",
        out_shape=(col, col, col, jax.ShapeDtypeStruct((N_HEADS, l, 128), F32)),
        grid=(N_HEADS,),
        in_specs=[head_col(qt), head_col(kt), head_col(vt), head_col(N_HEADS),
                  pl.BlockSpec((None, l, 128), lambda h: (h, 0, 0)),
                  pl.BlockSpec((None, 1, l), lambda h: (h, 0, 0)),
                  pl.BlockSpec((None, 1, l), lambda h: (h, 0, 0))],
        out_specs=(head_col(0), head_col(0), head_col(0), pl.BlockSpec((None, l, 128), lambda h: (h, 0, 0))),
        scratch_shapes=[pltpu.VMEM((l, HEAD_DIM), F32)] * 2 + [pltpu.VMEM((l, HEAD_DIM), MXU_DTYPE)] * 4,
        compiler_params=_params(("parallel",)),
    )(proj, proj, proj, d_mix, cum_bc, cum_rows, lse_rows)


def _fox_gate_bwd(ds_sum, proj, bias_row):
    l = proj.shape[0]
    n_blocks = l // CHUNK

    def body(ds_ref, ff_ref, b_ref, dff_ref, db_ref):
        r = lax.broadcasted_iota(jnp.int32, (CHUNK, CHUNK), 0)
        cidx = lax.broadcasted_iota(jnp.int32, (CHUNK, CHUNK), 1)
        upper = jnp.where(cidx >= r, 1.0, 0.0).astype(F32)
        carry = jnp.zeros((1, 128), F32)
        db = jnp.zeros((1, 128), F32)
        for blk in reversed(range(n_blocks)):
            rows = slice(blk * CHUNK, (blk + 1) * CHUNK)
            key_sum = jnp.zeros((CHUNK, 128), F32)
            for h in range(N_HEADS):
                select = jnp.where(cidx == h, 1.0, 0.0).astype(F32)
                key_sum = key_sum + jnp.dot(ds_ref[h, rows, :], select, precision=lax.Precision.HIGHEST,
                                            preferred_element_type=F32)
            suffix = jnp.dot(upper, key_sum, precision=lax.Precision.HIGHEST, preferred_element_type=F32) + carry
            carry = suffix[0:1, :]
            _, dsig = _log_forget(ff_ref[rows, :], b_ref[...], _row_valid(blk, CHUNK))
            dff = -suffix * dsig
            dff_ref[rows, :] = dff.astype(dff_ref.dtype)
            db = db + jnp.sum(dff, axis=0, keepdims=True)
        db_ref[...] = db

    return pl.pallas_call(
        body, name="fox_gate_bwd",
        out_shape=(jax.ShapeDtypeStruct((l, 128), MXU_DTYPE), jax.ShapeDtypeStruct((1, 128), F32)),
        grid=(1,),
        in_specs=[pl.BlockSpec((N_HEADS, l, 128), lambda i: (0, 0, 0)),
                  pl.BlockSpec((l, 128), lambda i: (0, FF_TILE)),
                  pl.BlockSpec((1, 128), lambda i: (0, 0))],
        out_specs=(pl.BlockSpec((l, 128), lambda i: (0, 0)), pl.BlockSpec((1, 128), lambda i: (0, 0))),
        compiler_params=_params(("arbitrary",)),
    )(ds_sum, proj, bias_row)


def _conv(u, w, b):
    return b + w[0:1, :] * pltpu.roll(u, 2, 0) + w[1:2, :] * pltpu.roll(u, 1, 0) + w[2:3, :] * u


def _conv_act_fwd(u, conv_w, conv_b, d_ff):
    l = u.shape[0]
    tc = _divisor_tile(d_ff, 256, 128)
    nt = d_ff // tc

    def body(ug_ref, uv_ref, wg_ref, wv_ref, bg_ref, bv_ref, a_ref):
        yg = _conv(ug_ref[...], wg_ref[...], bg_ref[...])
        yv = _conv(uv_ref[...], wv_ref[...], bv_ref[...])
        act = yg * _sigmoid(yg) * yv
        a_ref[...] = jnp.where(_row_valid(0, l), act, 0.0).astype(a_ref.dtype)

    return pl.pallas_call(
        body, name="conv_act_fwd",
        out_shape=jax.ShapeDtypeStruct((l, d_ff), MXU_DTYPE),
        grid=(nt,),
        in_specs=[pl.BlockSpec((l, tc), lambda j: (0, j)), pl.BlockSpec((l, tc), lambda j: (0, j + nt)),
                  pl.BlockSpec((8, tc), lambda j: (0, j)), pl.BlockSpec((8, tc), lambda j: (0, j + nt)),
                  pl.BlockSpec((1, tc), lambda j: (0, j)), pl.BlockSpec((1, tc), lambda j: (0, j + nt))],
        out_specs=pl.BlockSpec((l, tc), lambda j: (0, j)),
        compiler_params=_params(("parallel",)),
    )(u, u, conv_w, conv_w, conv_b, conv_b)


def _conv_act_bwd(u, conv_w, conv_b, d_act, d_ff):
    l = u.shape[0]
    tc = _divisor_tile(d_ff, 256, 128)
    nt = d_ff // tc

    def body(ug_ref, uv_ref, wg_ref, wv_ref, bg_ref, bv_ref, da_ref, du_ref, dwb_ref):
        valid = _row_valid(0, l)
        ug, uv = ug_ref[...], uv_ref[...]
        wg, wv = wg_ref[...], wv_ref[...]
        yg = _conv(ug, wg, bg_ref[...])
        yv = _conv(uv, wv, bv_ref[...])
        sig = _sigmoid(yg)
        da = jnp.where(valid, da_ref[...], 0.0)
        d_yv = da * (yg * sig)
        d_yg = da * yv * (sig * (1.0 + yg * (1.0 - sig)))
        for idx, (dy, uu, w) in enumerate(((d_yg, ug, wg), (d_yv, uv, wv))):
            du = w[2:3, :] * dy + w[1:2, :] * pltpu.roll(dy, l - 1, 0) + w[0:1, :] * pltpu.roll(dy, l - 2, 0)
            du_ref[idx] = jnp.where(valid, du, 0.0).astype(du_ref.dtype)
            dwb_ref[idx, 0:1, :] = jnp.sum(dy * pltpu.roll(uu, 2, 0), axis=0, keepdims=True)
            dwb_ref[idx, 1:2, :] = jnp.sum(dy * pltpu.roll(uu, 1, 0), axis=0, keepdims=True)
            dwb_ref[idx, 2:3, :] = jnp.sum(dy * uu, axis=0, keepdims=True)
            dwb_ref[idx, 3:4, :] = jnp.sum(dy, axis=0, keepdims=True)
            dwb_ref[idx, 4:8, :] = jnp.zeros((4, tc), F32)

    return pl.pallas_call(
        body, name="conv_act_bwd",
        out_shape=(jax.ShapeDtypeStruct((2, l, d_ff), MXU_DTYPE), jax.ShapeDtypeStruct((2, 8, d_ff), F32)),
        grid=(nt,),
        in_specs=[pl.BlockSpec((l, tc), lambda j: (0, j)), pl.BlockSpec((l, tc), lambda j: (0, j + nt)),
                  pl.BlockSpec((8, tc), lambda j: (0, j)), pl.BlockSpec((8, tc), lambda j: (0, j + nt)),
                  pl.BlockSpec((1, tc), lambda j: (0, j)), pl.BlockSpec((1, tc), lambda j: (0, j + nt)),
                  pl.BlockSpec((l, tc), lambda j: (0, j))],
        out_specs=(pl.BlockSpec((2, l, tc), lambda j: (0, 0, j)), pl.BlockSpec((2, 8, tc), lambda j: (0, 0, j))),
        compiler_params=_params(("parallel",)),
    )(u, u, conv_w, conv_w, conv_b, conv_b, d_act)


def _adamw(w, g, m, v, name):
    shape = w.shape
    if w.ndim == 1:
        as2d = (1, shape[0])
    else:
        as2d = (int(np.prod(shape[:-1])), shape[-1])
    r, c = as2d
    tr = _divisor_tile(r, 256, 8)
    spec = pl.BlockSpec((tr, c), lambda i: (i, 0))

    def body(w_ref, g_ref, m_ref, v_ref, d_ref, nm_ref, nv_ref):
        gv = g_ref[...]
        nm = ADAM_B1 * m_ref[...] + (1.0 - ADAM_B1) * gv
        nv = ADAM_B2 * v_ref[...] + (1.0 - ADAM_B2) * (gv * gv)
        m_hat = nm / (1.0 - ADAM_B1 ** ADAM_STEP)
        v_hat = nv / (1.0 - ADAM_B2 ** ADAM_STEP)
        d_ref[...] = -ADAM_LR * (m_hat / (jnp.sqrt(v_hat) + ADAM_EPS) + ADAM_WD * w_ref[...])
        nm_ref[...] = nm
        nv_ref[...] = nv

    sds = jax.ShapeDtypeStruct(as2d, F32)
    outs = pl.pallas_call(
        body, name=name, out_shape=(sds, sds, sds), grid=(r // tr,),
        in_specs=[spec] * 4, out_specs=(spec,) * 3,
        compiler_params=_params(("parallel",)),
    )(w.reshape(as2d), g.reshape(as2d), m.reshape(as2d), v.reshape(as2d))
    return tuple(o.reshape(shape) for o in outs)


def _pad_rows(a, rows):
    return jnp.pad(a, ((0, rows - a.shape[0]), (0, 0)))


def kernel(x, meta_tokens, norm1_gain, w_in, b_forget, ret_norm_gain, w_out, norm2_gain, w_up, conv_w, conv_b, w_down, final_norm_gain, loss_target, m_meta_tokens, m_norm1_gain, m_w_in, m_b_forget, m_ret_norm_gain, m_w_out, m_norm2_gain, m_w_up, m_conv_w, m_conv_b, m_w_down, m_final_norm_gain, v_meta_tokens, v_norm1_gain, v_w_in, v_b_forget, v_ret_norm_gain, v_w_out, v_norm2_gain, v_w_up, v_conv_w, v_conv_b, v_w_down, v_final_norm_gain):
    seq, d = x.shape[1], x.shape[2]
    l = CHUNK + seq
    d_ff = w_down.shape[1] * N_DEV
    up_shard = w_up.shape[2]
    assert 4 * up_shard == d_ff and w_in.shape[2] == WIN_SHARD and d == 2 * GROUP
    dev = _device_index()
    mx, my, mc = _my_position()
    core = jnp.reshape(mc, (1,)).astype(jnp.int32)
    chip = jnp.reshape(2 * mx + my, (1,)).astype(jnp.int32)
    dev1 = jnp.reshape(dev, (1,)).astype(jnp.int32)

    small = jnp.concatenate([meta_tokens.reshape(-1, 128), conv_w[0].reshape(-1, 128)], axis=0)
    n_meta_rows = N_META * (d // N_DEV) // 128
    small_rows = small.shape[0]
    small_all = _all_gather(_pad_rows(small, -(-small_rows // 8) * 8), "gather_small")
    meta_full = jnp.transpose(small_all[:, :n_meta_rows].reshape(N_DEV, N_META, d // N_DEV), (1, 0, 2)).reshape(N_META, d)
    conv_w_full = _pad_rows(jnp.transpose(small_all[:, n_meta_rows:small_rows].reshape(N_DEV, 3, up_shard),
                                          (1, 0, 2)).reshape(3, 2 * d_ff), 8)
    to_rows = lambda t: jnp.pad(jnp.transpose(t[0]), ((0, WIN_ROWS - WIN_SHARD), (0, 0)))
    from_rows = lambda t: jnp.transpose(t[:WIN_SHARD])[None]
    w_in_rows = to_rows(w_in)
    out_rows = d // N_DEV
    mixer_rows = -(-(WIN_ROWS + out_rows) // 304) * 304
    mixer_shard = jnp.concatenate([w_in_rows.astype(WIRE_DTYPE), w_out[0].astype(WIRE_DTYPE),
                                   jnp.zeros((mixer_rows - WIN_ROWS - out_rows, d), WIRE_DTYPE)], axis=0)

    h0 = jnp.concatenate([jnp.zeros((PAD_ROWS, d), F32), meta_full, x[0]], axis=0)
    consts = _retention_consts(l)
    bias_row = jnp.pad(b_forget, ((0, 0), (0, 128 - N_HEADS)))
    a = _rmsnorm_fwd(h0, norm1_gain, "rmsnorm1")
    mixer_blocks = _gather_ring(mixer_shard, dev1, a, "gather_w_in")
    start_up = _gather_start(w_up[0].astype(WIRE_DTYPE), dev1, mixer_blocks, "gather_w_up_start")
    w_in_full = _assemble_w_in(mixer_blocks).astype(MXU_DTYPE)
    proj = _mm_nt(a, w_in_full, F32, "mm_proj", after=start_up[4])
    ret_mix, ret_pre, ret_states = _retention_fwd(proj, ret_norm_gain, consts)
    cum_bc, cum_rows = _fox_prep(proj, bias_row)
    mix, lse_rows = _fox_fwd(proj, cum_bc, cum_rows, ret_mix)
    w_out_full = mixer_blocks[:, WIN_ROWS:WIN_ROWS + out_rows].reshape(d, d).astype(MXU_DTYPE)
    h1, cn = _rmsnorm_fwd(h0, norm2_gain, "resid_rmsnorm2", res=_mm_nn(mix, w_out_full, F32, "mm_out"))
    w_up_blocks = _gather_finish(start_up, cn, "gather_w_up").astype(MXU_DTYPE)
    start_down = _gather_start(w_down[0].astype(WIRE_DTYPE), dev1, w_up_blocks, "gather_w_down_start")
    u = _mm(cn, w_up_blocks,
            a_spec=pl.BlockSpec((_divisor_tile(l, 1088, 16), d), lambda i, j, k: (i, 0)),
            b_spec=pl.BlockSpec((None, d, up_shard), lambda i, j, k: (j, 0, 0)),
            o_spec=pl.BlockSpec((_divisor_tile(l, 1088, 16), up_shard), lambda i, j, k: (i, j)),
            out_shape=jax.ShapeDtypeStruct((l, 2 * d_ff), F32),
            grid=(l // _divisor_tile(l, 1088, 16), N_DEV, 1), contract=(1, 0), nk=1, name="mm_up",
            after=start_down[4])
    act = _conv_act_fwd(u, conv_w_full, conv_b + start_down[4][0, 0], d_ff)
    w_down_full = _gather_finish(start_down, act, "gather_w_down").reshape(d_ff, d).astype(MXU_DTYPE)
    mlp_out = _mm_nn(act, w_down_full, F32, "mm_down", tm_cap=544, tk_cap=d_ff)
    d_h2, d_h2_b, dg_final, loss_part = _loss_head(h1, mlp_out, final_norm_gain.reshape(1, d), loss_target[0])

    gw_down = _mm_tn(act, d_h2_b, WIRE_DTYPE, "mm_gw_down", tm_cap=1408, tn_cap=1024)
    d2d_down = _reduce_scatter_d2d_start(gw_down.reshape(N_DEV, d_ff // N_DEV, d), d_h2, "rs_w_down")
    d_act = _mm_nt(d_h2_b, w_down_full, F32, "mm_d_act", after=d2d_down[4])
    rs_down = _reduce_scatter_ici_start(d2d_down, d_act, core, "rs_w_down")
    d_u, d_conv = _conv_act_bwd(u, conv_w_full, conv_b + rs_down[4][0, 0], d_act, d_ff)
    tm = _divisor_tile(l, 1088, 16)
    gw_up = _mm(cn, d_u,
                a_spec=pl.BlockSpec((l, d // 2), lambda i, j, k: (0, i)),
                b_spec=pl.BlockSpec((None, l, up_shard), lambda i, j, k: (j // 4, 0, j % 4)),
                o_spec=pl.BlockSpec((None, d // 2, up_shard), lambda i, j, k: (j, i, 0)),
                out_shape=jax.ShapeDtypeStruct((N_DEV, d, up_shard), WIRE_DTYPE),
                grid=(2, N_DEV, 1), contract=(0, 0), nk=1, name="mm_gw_up")
    d2d_up = _reduce_scatter_d2d_start(gw_up, d_act, "rs_w_up")
    d_cn = _mm_d_cn(d_u, w_up_blocks, d2d_up[4])
    rs_up = _reduce_scatter_ici_start(d2d_up, d_cn, core, "rs_w_up")
    d_h1, d_h1_b, dg_norm2 = _rmsnorm_bwd(d_h2, d_cn, h1, norm2_gain + rs_up[4][0, 0], "rmsnorm2_bwd", True)

    gw_out = _mm_tn(mix, d_h1_b, WIRE_DTYPE, "mm_gw_out")
    d2d_out = _reduce_scatter_d2d_start(gw_out.reshape(N_DEV, d // N_DEV, d), d_cn, "rs_w_out")
    d_mix = _mm_nt(d_h1_b, w_out_full, F32, "mm_d_mix", after=d2d_out[4])
    d_fq, d_fk, d_fv, ds_sum = _fox_bwd(proj, cum_bc, cum_rows, d_mix, lse_rows)
    d_ff_tile, db_forget_row = _fox_gate_bwd(ds_sum, proj, bias_row)
    d_ret, dg_ret = _retention_bwd(proj, ret_pre, ret_states, d_mix, ret_norm_gain, consts)
    rs_out = _reduce_scatter_ici_start(d2d_out, d_ret, core, "rs_w_out")
    d_proj = jnp.concatenate(
        [d_ret, d_fq, d_fk, d_fv, d_ff_tile, jnp.zeros((l, WIN_N - 7 * GROUP - 128), MXU_DTYPE)], axis=1)
    gw_in = _mm_tn(d_proj, a, WIRE_DTYPE, "mm_gw_in", tm_cap=768, after=rs_out[4])
    rs_in = _reduce_scatter_start(_extract_w_in_windows(gw_in), core, "rs_w_in")
    d_a = _mm_nn(d_proj, w_in_full, F32, "mm_d_a", tm_cap=544, tn_cap=256, tk_cap=WIN_N, after=rs_in[4])
    d_h0, dg_norm1 = _rmsnorm_bwd(d_h1, d_a, h0, norm1_gain + rs_in[4][0, 0], "rmsnorm1_bwd", False)
    grad_x = d_h0[CHUNK:][None]
    d_meta = d_h0[PAD_ROWS:CHUNK]

    d_conv_w = jnp.concatenate([d_conv[0, 0:3], d_conv[1, 0:3]], axis=1)
    d_conv_b = jnp.concatenate([d_conv[0, 3:4], d_conv[1, 3:4]], axis=1)
    pieces = [loss_part[:, 0:1], dg_norm1, db_forget_row[:, 0:N_HEADS], dg_ret, dg_norm2, d_conv_b, dg_final,
              d_meta.reshape(1, -1), d_conv_w.reshape(1, -1)]
    sizes = [p.shape[1] for p in pieces]
    flat = jnp.concatenate(pieces, axis=1)
    padded = -(-flat.shape[1] // 1024) * 1024
    flat = jnp.pad(flat, ((0, 0), (0, padded - flat.shape[1]))).reshape(padded // 128, 128)
    small_ar = _small_all_reduce_start(flat, d_h0, "all_reduce_small")

    g_w_down = _reduce_scatter_finish(rs_down, small_ar[4], chip, "rs_w_down")[None]
    g_w_up = _reduce_scatter_finish(rs_up, g_w_down, chip, "rs_w_up")[None]
    g_w_out = _reduce_scatter_finish(rs_out, g_w_up, chip, "rs_w_out")[None]
    early = [_adamw(w, g, m, v, "adamw_" + n) for w, g, m, v, n in (
        (w_down, g_w_down, m_w_down, v_w_down, "w_down"), (w_up, g_w_up, m_w_up, v_w_up, "w_up"),
        (w_out, g_w_out, m_w_out, v_w_out, "w_out"))]
    g_w_in_rows = _reduce_scatter_finish(rs_in, early[1][2], chip, "rs_w_in")
    g_w_in = from_rows(g_w_in_rows)
    early.append(tuple(from_rows(t) for t in _adamw(
        w_in_rows, g_w_in_rows, to_rows(m_w_in), to_rows(v_w_in), "adamw_w_in")))
    total = _small_all_reduce_finish(small_ar, early[3][2], dev1, "all_reduce_small").reshape(1, padded)
    offs = np.concatenate([[0], np.cumsum(sizes)])
    take = lambda k: total[:, int(offs[k]):int(offs[k + 1])]
    loss = take(0).reshape(())
    g_norm1, g_bf, g_ret_gain, g_norm2 = take(1), take(2), take(3), take(4)
    g_conv_b, g_final = take(5), take(6).reshape(d)
    g_meta = lax.dynamic_slice(take(7).reshape(N_META, d), (jnp.int32(0), (dev * (d // N_DEV)).astype(jnp.int32)),
                               (N_META, d // N_DEV))
    g_conv_w = lax.dynamic_slice(take(8).reshape(3, 2 * d_ff), (jnp.int32(0), (dev * up_shard).astype(jnp.int32)),
                                 (3, up_shard))[None]

    weights = [meta_tokens, norm1_gain, w_in, b_forget, ret_norm_gain, w_out, norm2_gain, w_up, conv_w, conv_b,
               w_down, final_norm_gain]
    grads = [g_meta, g_norm1, g_w_in, g_bf, g_ret_gain, g_w_out, g_norm2, g_w_up, g_conv_w, g_conv_b, g_w_down,
             g_final]
    done = {"w_down": early[0], "w_up": early[1], "w_out": early[2], "w_in": early[3]}
    ms = [m_meta_tokens, m_norm1_gain, m_w_in, m_b_forget, m_ret_norm_gain, m_w_out, m_norm2_gain, m_w_up, m_conv_w,
          m_conv_b, m_w_down, m_final_norm_gain]
    vs = [v_meta_tokens, v_norm1_gain, v_w_in, v_b_forget, v_ret_norm_gain, v_w_out, v_norm2_gain, v_w_up, v_conv_w,
          v_conv_b, v_w_down, v_final_norm_gain]
    names = ["meta", "norm1", "w_in", "b_forget", "ret_gain", "w_out", "norm2", "w_up", "conv_w", "conv_b", "w_down",
             "final_gain"]
    deltas, new_ms, new_vs = [], [], []
    for w, g, m, v, n in zip(weights, grads, ms, vs, names):
        dl, nm, nv = done[n] if n in done else _adamw(w, g, m, v, "adamw_" + n)
        deltas.append(dl)
        new_ms.append(nm)
        new_vs.append(nv)
    return (loss, grad_x, *grads, *deltas, *new_ms, *new_vs)
```

```python
import functools

import numpy as np
import jax
import jax.numpy as jnp
from jax import lax
from jax.experimental import pallas as pl
from jax.experimental.pallas import tpu as pltpu

F32 = jnp.float32
MXU_DTYPE = jnp.bfloat16
WIRE_DTYPE = jnp.bfloat16

N_DEV = 8
N_META = 16
CHUNK = 128
PAD_ROWS = CHUNK - N_META
N_HEADS = 8
HEAD_DIM = 128
GROUP = N_HEADS * HEAD_DIM
IN_DIM = 7 * GROUP + N_HEADS
WIN_SHARD = IN_DIM // N_DEV
WIN_ROWS = 912
WIN_BLOCK = 1024
WIN_STRIDE = 896
WIN_N = 7680
ROPE_BASE = 10000.0
NORM_EPS = 1e-6
NEG_BIG = -1e30
ADAM_LR, ADAM_B1, ADAM_B2, ADAM_EPS, ADAM_WD, ADAM_STEP = 0.001, 0.9, 0.999, 1e-08, 0.01, 10
VMEM_LIMIT = 52 * 1024 * 1024
MESH = pl.DeviceIdType.MESH
ANY = pl.BlockSpec(memory_space=pl.ANY)
VMEM_SPEC = pl.BlockSpec(memory_space=pltpu.VMEM)


def _params(sem=None):
    kw = {"vmem_limit_bytes": VMEM_LIMIT}
    if sem is not None:
        kw["dimension_semantics"] = sem
    return pltpu.CompilerParams(**kw)


def _divisor_tile(n, cap, unit):
    if n <= cap:
        return n
    best = None
    for t in range(unit, cap + 1, unit):
        if n % t == 0:
            best = t
    assert best is not None, (n, cap, unit)
    return best


def _my_position():
    return lax.axis_index("x"), lax.axis_index("y"), lax.axis_index("c")


def _device_index():
    x, y, c = _my_position()
    return 4 * x + 2 * y + c


def _all_gather(shard, name):
    r, c = shard.shape

    def body(x_ref, out_ref, send_sems, recv_sems, local_sem):
        mx, my, mc = _my_position()
        me, sibling = (mx, my, mc), (mx, my, 1 - mc)
        chips = [(1 - mx, my), (mx, 1 - my), (1 - mx, 1 - my)]

        def slot(px, py, pc):
            return out_ref.at[4 * px + 2 * py + pc]

        def copy(k, block, to, src=None):
            return pltpu.make_async_remote_copy(
                src_ref=slot(*block) if src is None else src, dst_ref=slot(*block),
                send_sem=send_sems.at[k], recv_sem=recv_sems.at[k], device_id=to, device_id_type=MESH)

        mine = pltpu.make_async_copy(x_ref, slot(*me), local_sem)
        mine.start()
        first = [copy(0, me, sibling, src=x_ref)]
        first += [copy(1 + j, me, (*chip, mc), src=x_ref) for j, chip in enumerate(chips)]
        for cp in first:
            cp.start()
        passed = [copy(4 + j, (*chip, mc), sibling) for j, chip in enumerate(chips)]
        for j, chip in enumerate(chips):
            copy(1 + j, (*chip, mc), me).wait_recv()
            passed[j].start()
        copy(0, sibling, me).wait_recv()
        for j, chip in enumerate(chips):
            copy(4 + j, (*chip, 1 - mc), me).wait_recv()
        for cp in first + passed:
            cp.wait_send()
        mine.wait()

    return pl.pallas_call(
        body, name=name,
        out_shape=jax.ShapeDtypeStruct((N_DEV, r, c), shard.dtype),
        in_specs=[ANY], out_specs=ANY,
        scratch_shapes=[pltpu.SemaphoreType.DMA((7,)), pltpu.SemaphoreType.DMA((7,)), pltpu.SemaphoreType.DMA],
    )(shard)


HBM_SPEC = pl.BlockSpec(memory_space=pltpu.HBM)
SEM_SPEC = pl.BlockSpec(memory_space=pltpu.SEMAPHORE)
DATAFLOW_EFFECT = pltpu.SideEffectType.DATAFLOW_SIDE_EFFECTING


def _in_hbm(a):
    return pltpu.with_memory_space_constraint(a, pltpu.HBM)


def _split_start(src, land, make_copies, n_copies, after, name):
    if isinstance(land, tuple):
        land = lax.empty(land, src.dtype)
    land_shape = land.shape
    def body(src_ref, land_ref, after_ref, send_sems, recv_sems, src_thru, land_thru, token):
        for cp in make_copies(src_ref, land_ref, send_sems, recv_sems):
            cp.start()
        token[...] = jnp.zeros_like(token)

    return pl.pallas_call(
        body, name=name,
        out_shape=(pltpu.SemaphoreType.DMA((n_copies,)), pltpu.SemaphoreType.DMA((n_copies,)),
                   pltpu.HBM(src.shape, src.dtype), pltpu.HBM(land_shape, land.dtype),
                   jax.ShapeDtypeStruct((8, 128), F32)),
        in_specs=(HBM_SPEC, HBM_SPEC, ANY), out_specs=(SEM_SPEC, SEM_SPEC, HBM_SPEC, HBM_SPEC, VMEM_SPEC),
        input_output_aliases={0: 2, 1: 3},
        compiler_params=pltpu.CompilerParams(has_side_effects=DATAFLOW_EFFECT),
    )(_in_hbm(src), _in_hbm(land), after)


def _split_wait(started, after, make_copies, name):
    send_sems, recv_sems, src_thru, land_thru, _ = started

    def body(src_ref, land_ref, send_sems_ref, recv_sems_ref, after_ref, src_dead, land_out):
        for cp in make_copies(src_ref, land_ref, send_sems_ref, recv_sems_ref):
            cp.wait_send()
            cp.wait_recv()

    return pl.pallas_call(
        body, name=name,
        out_shape=(pltpu.HBM(src_thru.shape, src_thru.dtype), pltpu.HBM(land_thru.shape, land_thru.dtype)),
        in_specs=(HBM_SPEC, HBM_SPEC, SEM_SPEC, SEM_SPEC, ANY), out_specs=(HBM_SPEC, HBM_SPEC),
        input_output_aliases={0: 0, 1: 1},
        compiler_params=pltpu.CompilerParams(has_side_effects=DATAFLOW_EFFECT),
    )(src_thru, land_thru, send_sems, recv_sems, after)


def _gather_copies(x_ref, land_ref, send_sems, recv_sems):
    mx, my, mc = _my_position()
    me = 4 * mx + 2 * my + mc
    targets = [(mx, my, 1 - mc), (1 - mx, my, mc), (mx, 1 - my, mc), (1 - mx, 1 - my, mc)]
    return [pltpu.make_async_remote_copy(
        src_ref=land_ref.at[me], dst_ref=land_ref.at[me], send_sem=send_sems.at[k], recv_sem=recv_sems.at[k],
        device_id=t, device_id_type=MESH) for k, t in enumerate(targets)]


def _own_slot(shard, dev, name):
    r, c = shard.shape
    tr = _divisor_tile(r, 512, 16)

    def body(s_ref, x_ref, o_ref):
        o_ref[...] = x_ref[...].astype(o_ref.dtype)

    return pl.pallas_call(
        body, name=name,
        out_shape=jax.ShapeDtypeStruct((N_DEV, r, c), WIRE_DTYPE),
        grid_spec=pltpu.PrefetchScalarGridSpec(
            num_scalar_prefetch=1, grid=(r // tr,),
            in_specs=[pl.BlockSpec((tr, c), lambda i, s: (i, 0))],
            out_specs=pl.BlockSpec((None, tr, c), lambda i, s: (s[0], i, 0))),
        compiler_params=_params(("parallel",)),
    )(dev, shard)


def _gather_start(shard, dev, after, name):
    return _split_start(jnp.zeros((8, 128), F32), _own_slot(shard, dev, name + "_own"), _gather_copies, 4, after, name)


def _gather_ring(shard, dev, after, name):
    r, c = shard.shape
    half = r // 2
    assert half % 16 == 0

    def body(x_ref, after_ref, land_in, land_ref, send_sems, recv_sems):
        mx, my, mc = _my_position()
        sibling, x_nbr, y_nbr = (mx, my, 1 - mc), (1 - mx, my, mc), (mx, 1 - my, mc)
        first, second = pl.ds(0, half), pl.ds(half, half)

        def slot(px, py, pc):
            return land_ref.at[4 * px + 2 * py + pc]

        def copy(k, src, dst, to):
            return pltpu.make_async_remote_copy(src_ref=src, dst_ref=dst, send_sem=send_sems.at[k],
                                                recv_sem=recv_sems.at[k], device_id=to, device_id_type=MESH)

        def arrived(k, dst):
            copy(k, dst, dst, sibling).wait_recv()

        mine = slot(mx, my, mc)
        from_x, from_y, from_d = slot(1 - mx, my, mc), slot(mx, 1 - my, mc), slot(1 - mx, 1 - my, mc)
        sent = [copy(0, x_ref, mine, sibling), copy(1, x_ref, mine, x_nbr), copy(2, x_ref, mine, y_nbr)]
        for cp in sent:
            cp.start()

        def send(k, src, to):
            cp = copy(k, src, src, to)
            cp.start()
            sent.append(cp)

        arrived(1, from_x)
        send(3, from_x.at[first], y_nbr)
        send(5, from_x, sibling)
        arrived(2, from_y)
        send(4, from_y.at[second], x_nbr)
        send(6, from_y, sibling)
        arrived(3, from_d.at[first])
        send(7, from_d.at[first], sibling)
        arrived(4, from_d.at[second])
        send(8, from_d.at[second], sibling)
        arrived(0, slot(mx, my, 1 - mc))
        arrived(5, slot(1 - mx, my, 1 - mc))
        arrived(6, slot(mx, 1 - my, 1 - mc))
        arrived(7, slot(1 - mx, 1 - my, 1 - mc).at[first])
        arrived(8, slot(1 - mx, 1 - my, 1 - mc).at[second])
        for cp in sent:
            cp.wait_send()

    land = _own_slot(shard, dev, name + "_own")
    return pl.pallas_call(
        body, name=name,
        out_shape=jax.ShapeDtypeStruct(land.shape, land.dtype),
        in_specs=[ANY, ANY, ANY], out_specs=ANY,
        input_output_aliases={2: 0},
        scratch_shapes=[pltpu.SemaphoreType.DMA((9,)), pltpu.SemaphoreType.DMA((9,))],
    )(shard, after, land)


def _gather_finish(started, after, name):
    _, land = _split_wait(started, after, _gather_copies, name + "_wait")

    def body(land_in, land_ref, send_sems, recv_sems):
        mx, my, mc = _my_position()
        chips = [(1 - mx, my), (mx, 1 - my), (1 - mx, 1 - my)]
        copies = [pltpu.make_async_remote_copy(
            src_ref=land_ref.at[4 * cx + 2 * cy + mc], dst_ref=land_ref.at[4 * cx + 2 * cy + mc],
            send_sem=send_sems.at[j], recv_sem=recv_sems.at[j],
            device_id=(mx, my, 1 - mc), device_id_type=MESH) for j, (cx, cy) in enumerate(chips)]
        for cp in copies:
            cp.start()
        for j, (cx, cy) in enumerate(chips):
            copies[j].wait_send()
            pltpu.make_async_remote_copy(
                src_ref=land_ref.at[4 * cx + 2 * cy + 1 - mc], dst_ref=land_ref.at[4 * cx + 2 * cy + 1 - mc],
                send_sem=send_sems.at[j], recv_sem=recv_sems.at[j],
                device_id=(mx, my, 1 - mc), device_id_type=MESH).wait_recv()

    return pl.pallas_call(
        body, name=name + "_pass",
        out_shape=jax.ShapeDtypeStruct(land.shape, land.dtype),
        in_specs=[ANY], out_specs=ANY,
        input_output_aliases={0: 0},
        scratch_shapes=[pltpu.SemaphoreType.DMA((3,)), pltpu.SemaphoreType.DMA((3,))],
    )(land)


def _chip_copies(p_ref, land_ref, send_sems, recv_sems):
    mx, my, mc = _my_position()
    chips = [(1 - mx, my), (mx, 1 - my), (1 - mx, 1 - my)]
    return [pltpu.make_async_remote_copy(
        src_ref=p_ref.at[2 * cx + cy], dst_ref=land_ref.at[j], send_sem=send_sems.at[j], recv_sem=recv_sems.at[j],
        device_id=(cx, cy, mc), device_id_type=MESH) for j, (cx, cy) in enumerate(chips)]


def _reduce_scatter_start(g, core, name):
    pair = _pair_sum(g, _exchange_sibling(g, name + "_d2d"), core, name + "_pairsum")
    return _split_start(pair, (3,) + pair.shape[1:], _chip_copies, 3, g, name + "_ici_start")


def _sibling_copies(g_ref, land_ref, send_sems, recv_sems):
    mx, my, mc = _my_position()
    return [pltpu.make_async_remote_copy(
        src_ref=g_ref.at[2 * k + (1 - mc)], dst_ref=land_ref.at[k], send_sem=send_sems.at[k], recv_sem=recv_sems.at[k],
        device_id=(mx, my, 1 - mc), device_id_type=MESH) for k in range(4)]


def _reduce_scatter_d2d_start(g, after, name):
    return _split_start(g, (4,) + g.shape[1:], _sibling_copies, 4, after, name + "_d2d_start")


def _reduce_scatter_ici_start(d2d_started, after, core, name):
    g, from_sibling = _split_wait(d2d_started, after, _sibling_copies, name + "_d2d_wait")
    pair = _pair_sum(g, from_sibling, core, name + "_pairsum")
    return _split_start(pair, (3,) + pair.shape[1:], _chip_copies, 3, g, name + "_ici_start")


def _reduce_scatter_finish(started, after, chip, w, m, v, name):
    pair, from_chips = _split_wait(started, after, _chip_copies, name + "_ici_wait")
    return _final_sum_adamw(pair, from_chips, chip, w, m, v, name + "_sum_adamw")


def _exchange_sibling(g, name):
    _, r, c = g.shape

    def body(g_ref, out_ref, send_sems, recv_sems):
        mx, my, mc = _my_position()
        copies = [
            pltpu.make_async_remote_copy(
                src_ref=g_ref.at[2 * k + (1 - mc)], dst_ref=out_ref.at[k],
                send_sem=send_sems.at[k], recv_sem=recv_sems.at[k],
                device_id=(mx, my, 1 - mc), device_id_type=MESH)
            for k in range(4)]
        for cp in copies:
            cp.start()
        for cp in copies:
            cp.wait()

    return pl.pallas_call(
        body, name=name,
        out_shape=jax.ShapeDtypeStruct((4, r, c), g.dtype),
        in_specs=[ANY], out_specs=ANY,
        scratch_shapes=[pltpu.SemaphoreType.DMA((4,)), pltpu.SemaphoreType.DMA((4,))],
    )(g)


def _pair_sum(g, recv, core, name):
    _, r, c = g.shape
    tr = _divisor_tile(r, 512, 16)

    def body(s_ref, g_ref, r_ref, o_ref):
        o_ref[...] = (g_ref[...].astype(F32) + r_ref[...].astype(F32)).astype(o_ref.dtype)

    return pl.pallas_call(
        body, name=name,
        out_shape=jax.ShapeDtypeStruct((4, r, c), g.dtype),
        grid_spec=pltpu.PrefetchScalarGridSpec(
            num_scalar_prefetch=1, grid=(4, r // tr),
            in_specs=[pl.BlockSpec((None, tr, c), lambda k, i, s: (2 * k + s[0], i, 0)),
                      pl.BlockSpec((None, tr, c), lambda k, i, s: (k, i, 0))],
            out_specs=pl.BlockSpec((None, tr, c), lambda k, i, s: (k, i, 0))),
        compiler_params=_params(("parallel", "parallel")),
    )(core, g, recv)


def _adamw_math(w, g, m, v):
    nm = ADAM_B1 * m + (1.0 - ADAM_B1) * g
    nv = ADAM_B2 * v + (1.0 - ADAM_B2) * (g * g)
    m_hat = nm / (1.0 - ADAM_B1 ** ADAM_STEP)
    v_hat = nv / (1.0 - ADAM_B2 ** ADAM_STEP)
    return -ADAM_LR * (m_hat / (jnp.sqrt(v_hat) + ADAM_EPS) + ADAM_WD * w), nm, nv


def _final_sum_adamw(p, recv, chip, w, m, v, name):
    _, r, c = p.shape
    tr = _divisor_tile(r, 256, 16)
    tile = lambda: pl.BlockSpec((tr, c), lambda i, s: (i, 0))

    def body(s_ref, p_ref, r_ref, w_ref, m_ref, v_ref, g_ref, d_ref, nm_ref, nv_ref):
        g = p_ref[...].astype(F32)
        for j in range(3):
            g = g + r_ref[j].astype(F32)
        g_ref[...] = g
        d_ref[...], nm_ref[...], nv_ref[...] = _adamw_math(w_ref[...], g, m_ref[...], v_ref[...])

    sds = jax.ShapeDtypeStruct((r, c), F32)
    return pl.pallas_call(
        body, name=name,
        out_shape=(sds, sds, sds, sds),
        grid_spec=pltpu.PrefetchScalarGridSpec(
            num_scalar_prefetch=1, grid=(r // tr,),
            in_specs=[pl.BlockSpec((None, tr, c), lambda i, s: (s[0], i, 0)),
                      pl.BlockSpec((3, tr, c), lambda i, s: (0, i, 0)), tile(), tile(), tile()],
            out_specs=(tile(), tile(), tile(), tile())),
        compiler_params=_params(("parallel",)),
    )(chip, p, recv, w, m, v)


def _all_to_all_copies(v_ref, land_ref, send_sems, recv_sems):
    mx, my, mc = _my_position()
    me = 4 * mx + 2 * my + mc
    copies = []
    for rel in range(1, N_DEV):
        bx, by, bc = (rel >> 2) & 1, (rel >> 1) & 1, rel & 1
        target = (1 - mx if bx else mx, 1 - my if by else my, 1 - mc if bc else mc)
        copies.append(pltpu.make_async_remote_copy(
            src_ref=v_ref, dst_ref=land_ref.at[me], send_sem=send_sems.at[rel - 1], recv_sem=recv_sems.at[rel - 1],
            device_id=target, device_id_type=MESH))
    return copies


def _small_all_reduce_start(v, after, name):
    return _split_start(v, (N_DEV,) + v.shape, _all_to_all_copies, N_DEV - 1, after, name + "_start")


def _small_all_reduce_finish(started, after, dev, name):
    v, land = _split_wait(started, after, _all_to_all_copies, name + "_wait")
    rows = v.shape[0]

    def body(me_ref, v_ref, land_ref, o_ref):
        for j in range(N_DEV):
            @pl.when(me_ref[0] == j)
            def _():
                o_ref[...] = v_ref[...] if j == 0 else o_ref[...] + v_ref[...]

            @pl.when(me_ref[0] != j)
            def _():
                o_ref[...] = land_ref[j] if j == 0 else o_ref[...] + land_ref[j]

    return pl.pallas_call(
        body, name=name + "_sum",
        out_shape=jax.ShapeDtypeStruct((rows, 128), F32),
        grid_spec=pltpu.PrefetchScalarGridSpec(
            num_scalar_prefetch=1, grid=(1,),
            in_specs=[pl.BlockSpec((rows, 128), lambda i, s: (0, 0)),
                      pl.BlockSpec((N_DEV, rows, 128), lambda i, s: (0, 0, 0))],
            out_specs=pl.BlockSpec((rows, 128), lambda i, s: (0, 0))),
        compiler_params=_params(("arbitrary",)),
    )(dev, v, land)


def _assemble_w_in(blocks):
    rows, d = WIN_ROWS, blocks.shape[2]
    tc = _divisor_tile(d, 256, 128)
    n_tiles = WIN_N // 128
    last = (N_DEV * WIN_STRIDE) // 128

    def body(b_ref, o_ref):
        win = []
        for i in range(N_DEV):
            w = jnp.concatenate([b_ref[i].astype(F32), jnp.zeros((WIN_BLOCK - rows, tc), F32)], axis=0)
            win.append(pltpu.roll(w, i, 0) if i else w)
        for t in range(n_tiles):
            if t > last:
                o_ref[t * 128:(t + 1) * 128, :] = jnp.zeros((128, tc), o_ref.dtype)
                continue
            i = min(t // 7, N_DEV - 1)
            k = t - 7 * i
            val = win[i][k * 128:(k + 1) * 128, :]
            if k == 0 and i >= 1:
                val = val + win[i - 1][7 * 128:8 * 128, :]
            o_ref[t * 128:(t + 1) * 128, :] = val.astype(o_ref.dtype)

    return pl.pallas_call(
        body, name="assemble_w_in",
        out_shape=jax.ShapeDtypeStruct((WIN_N, d), blocks.dtype),
        grid=(d // tc,),
        in_specs=[pl.BlockSpec((N_DEV, rows, tc), lambda j: (0, 0, j))],
        out_specs=pl.BlockSpec((WIN_N, tc), lambda j: (0, j)),
        compiler_params=_params(("parallel",)),
    )(blocks)


def _extract_w_in_windows(g):
    _, d = g.shape
    tc = _divisor_tile(d, 256, 128)

    def body(g_ref, o_ref):
        for j in range(N_DEV):
            w = g_ref[WIN_STRIDE * j:WIN_STRIDE * j + WIN_BLOCK, :].astype(F32)
            w = pltpu.roll(w, WIN_BLOCK - j, 0) if j else w
            o_ref[j] = w[0:WIN_ROWS, :].astype(o_ref.dtype)

    return pl.pallas_call(
        body, name="extract_w_in_windows",
        out_shape=jax.ShapeDtypeStruct((N_DEV, WIN_ROWS, d), g.dtype),
        grid=(d // tc,),
        in_specs=[pl.BlockSpec((WIN_N, tc), lambda j: (0, j))],
        out_specs=pl.BlockSpec((N_DEV, WIN_ROWS, tc), lambda j: (0, 0, j)),
        compiler_params=_params(("parallel",)),
    )(g)


def _mm(a, b, *, a_spec, b_spec, o_spec, out_shape, grid, contract, nk, name, after=None):
    dn = (((contract[0],), (contract[1],)), ((), ()))
    tm, tn = o_spec.block_shape[-2:]
    behind = [] if after is None else [after]

    def body(a_ref, b_ref, *rest):
        o_ref, *scratch = rest[len(behind):]
        part = lax.dot_general(a_ref[...], b_ref[...], dn, preferred_element_type=F32)
        if nk == 1:
            o_ref[...] = part.astype(o_ref.dtype)
            return
        acc = scratch[0]
        k = pl.program_id(2)

        @pl.when(k == 0)
        def _():
            acc[...] = part

        @pl.when(k > 0)
        def _():
            acc[...] += part

        @pl.when(k == nk - 1)
        def _():
            o_ref[...] = acc[...].astype(o_ref.dtype)

    return pl.pallas_call(
        body, name=name, out_shape=out_shape, grid=grid,
        in_specs=[a_spec, b_spec] + [ANY] * len(behind), out_specs=o_spec,
        scratch_shapes=[] if nk == 1 else [pltpu.VMEM((tm, tn), F32)],
        compiler_params=_params(("parallel", "parallel", "arbitrary")),
    )(a, b, *behind)


def _mm_nn(a, b, out_dtype, name, tm_cap=1088, tn_cap=512, tk_cap=2048, after=None):
    m, k = a.shape
    _, n = b.shape
    tm, tn, tk = _divisor_tile(m, tm_cap, 16), _divisor_tile(n, tn_cap, 128), _divisor_tile(k, tk_cap, 128)
    return _mm(a, b,
               a_spec=pl.BlockSpec((tm, tk), lambda i, j, kk: (i, kk)),
               b_spec=pl.BlockSpec((tk, tn), lambda i, j, kk: (kk, j)),
               o_spec=pl.BlockSpec((tm, tn), lambda i, j, kk: (i, j)),
               out_shape=jax.ShapeDtypeStruct((m, n), out_dtype),
               grid=(m // tm, n // tn, k // tk), contract=(1, 0), nk=k // tk, name=name, after=after)


def _mm_nt(a, b, out_dtype, name, tm_cap=1088, tn_cap=512, tk_cap=2048, after=None):
    m, k = a.shape
    n, _ = b.shape
    tm, tn, tk = _divisor_tile(m, tm_cap, 16), _divisor_tile(n, tn_cap, 128), _divisor_tile(k, tk_cap, 128)
    return _mm(a, b,
               a_spec=pl.BlockSpec((tm, tk), lambda i, j, kk: (i, kk)),
               b_spec=pl.BlockSpec((tn, tk), lambda i, j, kk: (j, kk)),
               o_spec=pl.BlockSpec((tm, tn), lambda i, j, kk: (i, j)),
               out_shape=jax.ShapeDtypeStruct((m, n), out_dtype),
               grid=(m // tm, n // tn, k // tk), contract=(1, 1), nk=k // tk, name=name, after=after)


def _mm_tn(a, b, out_dtype, name, tm_cap=1024, tn_cap=512, after=None):
    l, m = a.shape
    _, n = b.shape
    tm, tn = _divisor_tile(m, tm_cap, 128), _divisor_tile(n, tn_cap, 128)
    return _mm(a, b,
               a_spec=pl.BlockSpec((l, tm), lambda i, j, kk: (0, i)),
               b_spec=pl.BlockSpec((l, tn), lambda i, j, kk: (0, j)),
               o_spec=pl.BlockSpec((tm, tn), lambda i, j, kk: (i, j)),
               out_shape=jax.ShapeDtypeStruct((m, n), out_dtype),
               grid=(m // tm, n // tn, 1), contract=(0, 0), nk=1, name=name, after=after)


def _mm_d_cn(d_u, w_up_blocks, after):
    _, l, d_ff = d_u.shape
    n, d, shard = w_up_blocks.shape
    per = d_ff // shard
    tm, tn = _divisor_tile(l, 544, 16), _divisor_tile(d, 256, 128)

    def body(a_ref, b_ref, after_ref, o_ref):
        acc = None
        for k in range(n):
            part = _dot_nt(a_ref[k // per, :, (k % per) * shard:(k % per + 1) * shard], b_ref[k])
            acc = part if acc is None else acc + part
        o_ref[...] = acc

    return pl.pallas_call(
        body, name="mm_d_cn", out_shape=jax.ShapeDtypeStruct((l, d), F32), grid=(l // tm, d // tn),
        in_specs=[pl.BlockSpec((2, tm, d_ff), lambda i, j: (0, i, 0)),
                  pl.BlockSpec((n, tn, shard), lambda i, j: (0, j, 0)), ANY],
        out_specs=pl.BlockSpec((tm, tn), lambda i, j: (i, j)),
        compiler_params=_params(("parallel", "parallel")),
    )(d_u, w_up_blocks, after)


def _row_tile(l):
    return _divisor_tile(l, 544, 8)


def _rmsnorm_fwd(h, gain, name, res=None):
    l, d = h.shape
    tr = _row_tile(l)
    row = pl.BlockSpec((tr, d), lambda i: (i, 0))
    vec = pl.BlockSpec((1, d), lambda i: (0, 0))

    def body(*refs):
        if res is None:
            h_ref, g_ref, n_ref = refs
            x = h_ref[...]
        else:
            h_ref, r_ref, g_ref, s_ref, n_ref = refs
            x = h_ref[...] + r_ref[...]
            s_ref[...] = x
        y = x * lax.rsqrt(jnp.mean(x * x, axis=-1, keepdims=True) + NORM_EPS)
        n_ref[...] = (y * g_ref[...]).astype(n_ref.dtype)

    normed = jax.ShapeDtypeStruct((l, d), MXU_DTYPE)
    if res is None:
        return pl.pallas_call(body, name=name, out_shape=normed, grid=(l // tr,), in_specs=[row, vec],
                              out_specs=row, compiler_params=_params(("parallel",)))(h, gain)
    return pl.pallas_call(body, name=name, out_shape=(jax.ShapeDtypeStruct((l, d), F32), normed),
                          grid=(l // tr,), in_specs=[row, row, vec], out_specs=(row, row),
                          compiler_params=_params(("parallel",)))(h, res, gain)


def _rmsnorm_bwd(d_res, d_normed, x, gain, name, with_mxu_copy):
    l, d = x.shape
    tr = _row_tile(l) if with_mxu_copy else CHUNK
    row = pl.BlockSpec((tr, d), lambda i: (i, 0))
    vec = pl.BlockSpec((1, d), lambda i: (0, 0))

    def body(dres_ref, dn_ref, x_ref, g_ref, dx_ref, other_ref, dg_ref):
        i = pl.program_id(0)
        xv = x_ref[...]
        r = lax.rsqrt(jnp.mean(xv * xv, axis=-1, keepdims=True) + NORM_EPS)
        xh = xv * r
        dn = dn_ref[...]
        dxh = dn * g_ref[...]
        dx = dres_ref[...] + r * (dxh - xh * jnp.mean(dxh * xh, axis=-1, keepdims=True))
        if with_mxu_copy:
            dx_ref[...] = dx
            other_ref[...] = dx.astype(MXU_DTYPE)
        else:
            @pl.when(i == 0)
            def _():
                dx_ref[...] = dx

            @pl.when(i > 0)
            def _():
                other_ref[...] = dx

        @pl.when(i == 0)
        def _():
            dg_ref[...] = jnp.zeros_like(dg_ref)

        dg_ref[...] += jnp.sum(dn * xh, axis=0, keepdims=True)

    if with_mxu_copy:
        outs = [jax.ShapeDtypeStruct((l, d), F32), jax.ShapeDtypeStruct((l, d), MXU_DTYPE)]
        specs = [row, row]
    else:
        outs = [jax.ShapeDtypeStruct((CHUNK, d), F32), jax.ShapeDtypeStruct((l - CHUNK, d), F32)]
        specs = [pl.BlockSpec((CHUNK, d), lambda i: (0, 0)), pl.BlockSpec((CHUNK, d), lambda i: (jnp.maximum(i - 1, 0), 0))]
    outs.append(jax.ShapeDtypeStruct((1, d), F32))
    specs.append(vec)
    return pl.pallas_call(body, name=name, out_shape=tuple(outs), grid=(l // tr,),
                          in_specs=[row, row, row, vec], out_specs=tuple(specs),
                          compiler_params=_params(("arbitrary",)))(d_res, d_normed, x, gain)


def _loss_head(h1, mlp_out, gain, target):
    l, d = h1.shape
    n_blocks = l // CHUNK
    row = pl.BlockSpec((CHUNK, d), lambda i: (i, 0))
    vec = pl.BlockSpec((1, d), lambda i: (0, 0))
    tgt = pl.BlockSpec((CHUNK, d), lambda i: (jnp.maximum(i - 1, 0), 0))

    def body(h_ref, m_ref, g_ref, t_ref, dh_ref, dhb_ref, dg_ref, loss_ref, sq_ref):
        i = pl.program_id(0)
        x = h_ref[...] + m_ref[...]
        r = lax.rsqrt(jnp.mean(x * x, axis=-1, keepdims=True) + NORM_EPS)
        xh = x * r
        g = g_ref[...]
        real = i >= 1
        err = jnp.where(real, xh * g - t_ref[...], 0.0)
        dy = err * (1.0 / d)
        dxh = dy * g
        dh = r * (dxh - xh * jnp.mean(dxh * xh, axis=-1, keepdims=True))
        dh_ref[...] = dh
        dhb_ref[...] = dh.astype(MXU_DTYPE)

        @pl.when(i == 0)
        def _():
            dg_ref[...] = jnp.zeros_like(dg_ref)
            sq_ref[...] = jnp.zeros_like(sq_ref)

        dg_ref[...] += jnp.sum(dy * xh, axis=0, keepdims=True)
        sq_ref[...] += jnp.sum(err * err, axis=0, keepdims=True)

        @pl.when(i == n_blocks - 1)
        def _():
            total = jnp.sum(sq_ref[...], axis=-1, keepdims=True) * (0.5 / d)
            loss_ref[...] = jnp.broadcast_to(total, (1, 128))

    return pl.pallas_call(
        body, name="loss_head",
        out_shape=(jax.ShapeDtypeStruct((l, d), F32), jax.ShapeDtypeStruct((l, d), MXU_DTYPE),
                   jax.ShapeDtypeStruct((1, d), F32), jax.ShapeDtypeStruct((1, 128), F32)),
        grid=(n_blocks,), in_specs=[row, row, vec, tgt],
        out_specs=(row, row, vec, pl.BlockSpec((1, 128), lambda i: (0, 0))),
        scratch_shapes=[pltpu.VMEM((1, d), F32)],
        compiler_params=_params(("arbitrary",)),
    )(h1, mlp_out, gain, target)


def _dot(a, b):
    return jnp.dot(a, b, preferred_element_type=F32)


def _dot_nt(a, b):
    return lax.dot_general(a, b, (((1,), (1,)), ((), ())), preferred_element_type=F32)


def _dot_tn(a, b):
    return lax.dot_general(a, b, (((0,), (0,)), ((), ())), preferred_element_type=F32)


def _rope(t, cos2, sin2):
    return t * cos2 + pltpu.roll(t, HEAD_DIM // 2, 1) * sin2


def _rope_bwd(dr, cos2, sin2):
    return dr * cos2 + pltpu.roll(dr * sin2, HEAD_DIM // 2, 1)


def _sigmoid(x):
    return 1.0 / (1.0 + jnp.exp(-x))


def _row_valid(block, rows):
    r = block * CHUNK + lax.broadcasted_iota(jnp.int32, (rows, 1), 0)
    return r >= PAD_ROWS


def _retention_consts(l):
    pos = jnp.arange(l, dtype=F32) - PAD_ROWS
    inv_freq = 1.0 / (ROPE_BASE ** (jnp.arange(0, HEAD_DIM, 2, dtype=F32) / HEAD_DIM))
    ang = pos[:, None] * inv_freq[None, :]
    cos, sin = jnp.cos(ang), jnp.sin(ang)
    cos2 = jnp.concatenate([cos, cos], axis=-1)
    sin2 = jnp.concatenate([-sin, sin], axis=-1)
    log_g = jnp.log1p(-jnp.exp2(-5.0 - jnp.arange(N_HEADS, dtype=F32)))
    idx = jnp.arange(CHUNK, dtype=F32)
    diff = idx[:, None] - idx[None, :]
    decay = jnp.where(diff >= 0, jnp.exp(jnp.maximum(diff, 0.0)[None] * log_g[:, None, None]), 0.0)
    xi = jnp.exp((idx + 1.0)[None, :] * log_g[:, None])
    zeta = jnp.exp((CHUNK - 1.0 - idx)[None, :] * log_g[:, None])
    g_chunk = jnp.exp(CHUNK * log_g)
    bcast = lambda v: jnp.broadcast_to(v[:, :, None], (N_HEADS, CHUNK, HEAD_DIM))
    g_rows = jnp.broadcast_to(g_chunk[:, None, None], (N_HEADS, 8, HEAD_DIM))
    return cos2, sin2, decay, bcast(xi), bcast(zeta), g_rows


def _retention_fwd(proj, ret_gain, consts):
    l = proj.shape[0]
    n_chunks = l // CHUNK
    cos2, sin2, decay, xi, zeta, g_rows = consts
    scale = HEAD_DIM ** -0.5

    def body(p_ref, cos_ref, sin_ref, dec_ref, xi_ref, zeta_ref, gr_ref, gain_ref,
             mix_ref, o_ref, st_ref, state):
        c = pl.program_id(0)

        @pl.when(c == 0)
        def _():
            state[...] = jnp.zeros_like(state)

        cos_v, sin_v = cos_ref[...], sin_ref[...]
        valid = _row_valid(c, CHUNK)
        for h in range(N_HEADS):
            cols = slice(h * HEAD_DIM, (h + 1) * HEAD_DIM)
            q = p_ref[:, h * HEAD_DIM:(h + 1) * HEAD_DIM]
            k = p_ref[:, GROUP + h * HEAD_DIM:GROUP + (h + 1) * HEAD_DIM]
            v = p_ref[:, 2 * GROUP + h * HEAD_DIM:2 * GROUP + (h + 1) * HEAD_DIM]
            g = p_ref[:, 3 * GROUP + h * HEAD_DIM:3 * GROUP + (h + 1) * HEAD_DIM]
            rq = _rope(q, cos_v, sin_v).astype(MXU_DTYPE)
            rk = _rope(k, cos_v, sin_v) * scale
            rkb = rk.astype(MXU_DTYPE)
            vb = v.astype(MXU_DTYPE)
            st = state[h]
            st_ref[h] = st
            s = _dot_nt(rq, rkb) * dec_ref[h]
            o = _dot(s.astype(MXU_DTYPE), vb) + _dot(rq, st.astype(MXU_DTYPE)) * xi_ref[h]
            kz = (rk * zeta_ref[h]).astype(MXU_DTYPE)
            state[h] = gr_ref[h, 0:1, :] * st + _dot_tn(kz, vb)
            o_ref[:, cols] = o
            mu = jnp.mean(o, axis=-1, keepdims=True)
            oc = o - mu
            yn = oc * lax.rsqrt(jnp.mean(oc * oc, axis=-1, keepdims=True) + NORM_EPS)
            ret = (g * _sigmoid(g)) * (yn * gain_ref[:, cols])
            mix_ref[:, cols] = jnp.where(valid, ret, 0.0).astype(mix_ref.dtype)

    head_tab = pl.BlockSpec((N_HEADS, CHUNK, HEAD_DIM), lambda c: (0, 0, 0))
    return pl.pallas_call(
        body, name="retention_fwd",
        out_shape=(jax.ShapeDtypeStruct((l, 2 * GROUP), MXU_DTYPE), jax.ShapeDtypeStruct((l, GROUP), F32),
                   jax.ShapeDtypeStruct((n_chunks, N_HEADS, HEAD_DIM, HEAD_DIM), F32)),
        grid=(n_chunks,),
        in_specs=[pl.BlockSpec((CHUNK, 4 * GROUP), lambda c: (c, 0)),
                  pl.BlockSpec((CHUNK, HEAD_DIM), lambda c: (c, 0)),
                  pl.BlockSpec((CHUNK, HEAD_DIM), lambda c: (c, 0)),
                  head_tab, head_tab, head_tab,
                  pl.BlockSpec((N_HEADS, 8, HEAD_DIM), lambda c: (0, 0, 0)),
                  pl.BlockSpec((1, GROUP), lambda c: (0, 0))],
        out_specs=(pl.BlockSpec((CHUNK, GROUP), lambda c: (c, 0)),
                   pl.BlockSpec((CHUNK, GROUP), lambda c: (c, 0)),
                   pl.BlockSpec((None, N_HEADS, HEAD_DIM, HEAD_DIM), lambda c: (c, 0, 0, 0))),
        scratch_shapes=[pltpu.VMEM((N_HEADS, HEAD_DIM, HEAD_DIM), F32)],
        compiler_params=_params(("arbitrary",)),
    )(proj, cos2, sin2, decay, xi, zeta, g_rows, ret_gain)


def _retention_bwd(proj, o_pre, states, d_mix, ret_gain, consts):
    l = proj.shape[0]
    n_chunks = l // CHUNK
    cos2, sin2, decay, xi, zeta, g_rows = consts
    scale = HEAD_DIM ** -0.5
    rev = lambda c: n_chunks - 1 - c

    def body(p_ref, o_ref, st_ref, dm_ref, cos_ref, sin_ref, dec_ref, xi_ref, zeta_ref, gr_ref, gain_ref,
             dp_ref, dgain_ref, dstate):
        step = pl.program_id(0)

        @pl.when(step == 0)
        def _():
            dstate[...] = jnp.zeros_like(dstate)
            dgain_ref[...] = jnp.zeros_like(dgain_ref)

        cos_v, sin_v = cos_ref[...], sin_ref[...]
        valid = _row_valid(rev(step), CHUNK)
        for h in range(N_HEADS):
            cols = slice(h * HEAD_DIM, (h + 1) * HEAD_DIM)
            q = p_ref[:, h * HEAD_DIM:(h + 1) * HEAD_DIM]
            k = p_ref[:, GROUP + h * HEAD_DIM:GROUP + (h + 1) * HEAD_DIM]
            v = p_ref[:, 2 * GROUP + h * HEAD_DIM:2 * GROUP + (h + 1) * HEAD_DIM]
            g = p_ref[:, 3 * GROUP + h * HEAD_DIM:3 * GROUP + (h + 1) * HEAD_DIM]
            o = o_ref[:, cols]
            gain = gain_ref[:, cols]
            d_ret = jnp.where(valid, dm_ref[:, cols], 0.0)
            mu = jnp.mean(o, axis=-1, keepdims=True)
            oc = o - mu
            rstd = lax.rsqrt(jnp.mean(oc * oc, axis=-1, keepdims=True) + NORM_EPS)
            yn = oc * rstd
            sig = _sigmoid(g)
            gate = g * sig
            dgain_ref[:, cols] += jnp.sum(d_ret * gate * yn, axis=0, keepdims=True)
            d_g = d_ret * (yn * gain) * (sig * (1.0 + g * (1.0 - sig)))
            d_yn = d_ret * gate * gain
            d_o = rstd * (d_yn - jnp.mean(d_yn, axis=-1, keepdims=True)
                          - yn * jnp.mean(d_yn * yn, axis=-1, keepdims=True))
            rq = _rope(q, cos_v, sin_v)
            rk = _rope(k, cos_v, sin_v) * scale
            rqb, rkb, vb = rq.astype(MXU_DTYPE), rk.astype(MXU_DTYPE), v.astype(MXU_DTYPE)
            dob = d_o.astype(MXU_DTYPE)
            dec = dec_ref[h]
            xi_h, zeta_h = xi_ref[h], zeta_ref[h]
            st_b = st_ref[h].astype(MXU_DTYPE)
            dst = dstate[h]
            dst_b = dst.astype(MXU_DTYPE)
            s_b = (_dot_nt(rqb, rkb) * dec).astype(MXU_DTYPE)
            da_b = (_dot_nt(dob, vb) * dec).astype(MXU_DTYPE)
            doxi_b = (d_o * xi_h).astype(MXU_DTYPE)
            kz_b = (rk * zeta_h).astype(MXU_DTYPE)
            d_rq = _dot(da_b, rkb) + _dot_nt(doxi_b, st_b)
            d_rk = _dot_tn(da_b, rqb) + _dot_nt(vb, dst_b) * zeta_h
            d_v = _dot_tn(s_b, dob) + _dot(kz_b, dst_b)
            dstate[h] = gr_ref[h, 0:1, :] * dst + _dot_tn(rqb, doxi_b)
            d_q = _rope_bwd(d_rq, cos_v, sin_v)
            d_k = _rope_bwd(d_rk * scale, cos_v, sin_v)
            dp_ref[:, h * HEAD_DIM:(h + 1) * HEAD_DIM] = d_q.astype(dp_ref.dtype)
            dp_ref[:, GROUP + h * HEAD_DIM:GROUP + (h + 1) * HEAD_DIM] = d_k.astype(dp_ref.dtype)
            dp_ref[:, 2 * GROUP + h * HEAD_DIM:2 * GROUP + (h + 1) * HEAD_DIM] = d_v.astype(dp_ref.dtype)
            dp_ref[:, 3 * GROUP + h * HEAD_DIM:3 * GROUP + (h + 1) * HEAD_DIM] = d_g.astype(dp_ref.dtype)

    head_tab = pl.BlockSpec((N_HEADS, CHUNK, HEAD_DIM), lambda c: (0, 0, 0))
    return pl.pallas_call(
        body, name="retention_bwd",
        out_shape=(jax.ShapeDtypeStruct((l, 4 * GROUP), MXU_DTYPE), jax.ShapeDtypeStruct((1, GROUP), F32)),
        grid=(n_chunks,),
        in_specs=[pl.BlockSpec((CHUNK, 4 * GROUP), lambda c: (rev(c), 0)),
                  pl.BlockSpec((CHUNK, GROUP), lambda c: (rev(c), 0)),
                  pl.BlockSpec((None, N_HEADS, HEAD_DIM, HEAD_DIM), lambda c: (rev(c), 0, 0, 0)),
                  pl.BlockSpec((CHUNK, GROUP), lambda c: (rev(c), 0)),
                  pl.BlockSpec((CHUNK, HEAD_DIM), lambda c: (rev(c), 0)),
                  pl.BlockSpec((CHUNK, HEAD_DIM), lambda c: (rev(c), 0)),
                  head_tab, head_tab, head_tab,
                  pl.BlockSpec((N_HEADS, 8, HEAD_DIM), lambda c: (0, 0, 0)),
                  pl.BlockSpec((1, GROUP), lambda c: (0, 0))],
        out_specs=(pl.BlockSpec((CHUNK, 4 * GROUP), lambda c: (rev(c), 0)),
                   pl.BlockSpec((1, GROUP), lambda c: (0, 0))),
        scratch_shapes=[pltpu.VMEM((N_HEADS, HEAD_DIM, HEAD_DIM), F32)],
        compiler_params=_params(("arbitrary",)),
    )(proj, o_pre, states, d_mix, cos2, sin2, decay, xi, zeta, g_rows, ret_gain)


FF_TILE = (7 * GROUP) // 128


def _log_forget(ff, bias_row, valid):
    x = ff + bias_row
    e = jnp.exp(-jnp.abs(x))
    lf = jnp.minimum(x, 0.0) - jnp.log(1.0 + e)
    head_lane = lax.broadcasted_iota(jnp.int32, x.shape, 1) < N_HEADS
    keep = lambda t: jnp.where(head_lane, jnp.where(valid, t, 0.0), 0.0)
    return keep(lf), keep(jnp.where(x >= 0, e, 1.0) / (1.0 + e))


def _fox_prep(proj, bias_row):
    l = proj.shape[0]
    n_blocks = l // CHUNK

    def body(ff_ref, b_ref, bc_ref, rows_ref, cum):
        r = lax.broadcasted_iota(jnp.int32, (CHUNK, CHUNK), 0)
        cidx = lax.broadcasted_iota(jnp.int32, (CHUNK, CHUNK), 1)
        tri = jnp.where(r >= cidx, 1.0, 0.0).astype(F32)
        carry = jnp.zeros((1, 128), F32)
        for blk in range(n_blocks):
            rows = slice(blk * CHUNK, (blk + 1) * CHUNK)
            valid = _row_valid(blk, CHUNK)
            lf, _ = _log_forget(ff_ref[rows, :], b_ref[...], valid)
            local = jnp.dot(tri, lf, precision=lax.Precision.HIGHEST, preferred_element_type=F32) + carry
            carry = local[CHUNK - 1:CHUNK, :]
            masked = jnp.where(valid, local, -NEG_BIG)
            cum[rows, :] = masked
            t = masked.T
            for h in range(N_HEADS):
                rows_ref[h, :, rows] = t[h:h + 1, :]
        full = cum[...]
        for h in range(N_HEADS):
            bc_ref[h] = jnp.broadcast_to(full[:, h:h + 1], (l, 128))

    return pl.pallas_call(
        body, name="fox_prep",
        out_shape=(jax.ShapeDtypeStruct((N_HEADS, l, 128), F32), jax.ShapeDtypeStruct((N_HEADS, 1, l), F32)),
        grid=(1,),
        in_specs=[pl.BlockSpec((l, 128), lambda i: (0, FF_TILE)), pl.BlockSpec((1, 128), lambda i: (0, 0))],
        out_specs=(pl.BlockSpec((N_HEADS, l, 128), lambda i: (0, 0, 0)),
                   pl.BlockSpec((N_HEADS, 1, l), lambda i: (0, 0, 0))),
        scratch_shapes=[pltpu.VMEM((l, 128), F32)],
        compiler_params=_params(("arbitrary",)),
    )(proj, bias_row)


ATTN_BLOCK = 2 * CHUNK


def _attn_blocks(l):
    assert (l - CHUNK) % ATTN_BLOCK == 0
    return [(0, CHUNK)] + [(s, ATTN_BLOCK) for s in range(CHUNK, l, ATTN_BLOCK)]


def _rows_valid(start, size):
    return start + lax.broadcasted_iota(jnp.int32, (size, 1), 0) >= PAD_ROWS


def _fox_fwd(proj, cum_bc, cum_rows, mix):
    l = proj.shape[0]
    blocks = _attn_blocks(l)
    scale = HEAD_DIM ** -0.5
    qt, kt, vt = 4 * N_HEADS, 5 * N_HEADS, 6 * N_HEADS

    def body(q_ref, k_ref, v_ref, cbc_ref, crow_ref, mix_in, o_ref, lse_ref, qb_s, kb_s, vb_s):
        qb_s[...] = q_ref[...].astype(MXU_DTYPE)
        kb_s[...] = k_ref[...].astype(MXU_DTYPE)
        vb_s[...] = v_ref[...].astype(MXU_DTYPE)
        for p, (qs, qn) in enumerate(blocks):
            qb = qb_s[qs:qs + qn, :]
            cq = cbc_ref[qs:qs + qn, :]
            m = jnp.full((qn, 1), NEG_BIG, F32)
            lsum = jnp.zeros((qn, 1), F32)
            acc = jnp.zeros((qn, HEAD_DIM), F32)
            for j in range(p + 1):
                ks, kn = blocks[j]
                bias = jnp.tile(cq, (1, kn // CHUNK)) - crow_ref[:, ks:ks + kn]
                s = _dot_nt(qb, kb_s[ks:ks + kn, :]) * scale + bias
                if j == p:
                    q_pos = qs + lax.broadcasted_iota(jnp.int32, (qn, kn), 0)
                    k_pos = ks + lax.broadcasted_iota(jnp.int32, (qn, kn), 1)
                    s = jnp.where(k_pos <= q_pos, s, NEG_BIG)
                m_new = jnp.maximum(m, jnp.max(s, axis=-1, keepdims=True))
                alpha = jnp.exp(m - m_new)
                pr = jnp.exp(s - m_new)
                lsum = lsum * alpha + jnp.sum(pr, axis=-1, keepdims=True)
                acc = acc * alpha + _dot(pr.astype(MXU_DTYPE), vb_s[ks:ks + kn, :])
                m = m_new
            o = jnp.where(_rows_valid(qs, qn), acc * (1.0 / lsum), 0.0)
            o_ref[qs:qs + qn, :] = o.astype(o_ref.dtype)
            lse = m + jnp.log(lsum)
            lse_ref[:, qs:qs + qn] = jnp.broadcast_to(lse, (qn, CHUNK)).T[0:1, :]

    head_col = lambda t: pl.BlockSpec((l, HEAD_DIM), lambda h: (0, t + h))
    return pl.pallas_call(
        body, name="fox_fwd",
        out_shape=(jax.ShapeDtypeStruct(mix.shape, mix.dtype), jax.ShapeDtypeStruct((N_HEADS, 1, l), F32)),
        grid=(N_HEADS,),
        in_specs=[head_col(qt), head_col(kt), head_col(vt),
                  pl.BlockSpec((None, l, 128), lambda h: (h, 0, 0)),
                  pl.BlockSpec((None, 1, l), lambda h: (h, 0, 0)),
                  ANY],
        out_specs=(head_col(N_HEADS), pl.BlockSpec((None, 1, l), lambda h: (h, 0, 0))),
        input_output_aliases={5: 0},
        scratch_shapes=[pltpu.VMEM((l, HEAD_DIM), MXU_DTYPE)] * 3,
        compiler_params=_params(("parallel",)),
    )(proj, proj, proj, cum_bc, cum_rows, mix)


def _fox_bwd(proj, cum_bc, cum_rows, d_mix, lse_rows):
    l = proj.shape[0]
    blocks = _attn_blocks(l)
    scale = HEAD_DIM ** -0.5
    qt, kt, vt = 4 * N_HEADS, 5 * N_HEADS, 6 * N_HEADS

    def body(q_ref, k_ref, v_ref, do_ref, cbc_ref, crow_ref, lse_ref,
             dq_ref, dk_ref, dv_ref, ds_ref, dk_acc, dv_acc, qb_s, kb_s, vb_s, dob_s):
        qb_s[...] = q_ref[...].astype(MXU_DTYPE)
        kb_s[...] = k_ref[...].astype(MXU_DTYPE)
        vb_s[...] = v_ref[...].astype(MXU_DTYPE)
        dob_s[...] = jnp.where(_rows_valid(0, l), do_ref[...], 0.0).astype(MXU_DTYPE)
        dk_acc[...] = jnp.zeros_like(dk_acc)
        dv_acc[...] = jnp.zeros_like(dv_acc)
        ds_ref[...] = jnp.zeros_like(ds_ref)
        shift_row = crow_ref[...] - lse_ref[...]

        for p, (qs, qn) in enumerate(blocks):
            qb, dob = qb_s[qs:qs + qn, :], dob_s[qs:qs + qn, :]
            shift = shift_row[:, qs:qs + qn]

            def probs(j):
                ks, kn = blocks[j]
                ck = jnp.tile(cbc_ref[ks:ks + kn, :], (1, qn // CHUNK))
                s_t = _dot_nt(kb_s[ks:ks + kn, :], qb) * scale + (shift - ck)
                if j == p:
                    k_pos = ks + lax.broadcasted_iota(jnp.int32, (kn, qn), 0)
                    q_pos = qs + lax.broadcasted_iota(jnp.int32, (kn, qn), 1)
                    s_t = jnp.where(k_pos <= q_pos, s_t, NEG_BIG)
                return jnp.exp(s_t), _dot_nt(vb_s[ks:ks + kn, :], dob)

            delta = jnp.zeros((1, qn), F32)
            for j in range(p + 1):
                p_t, dp_t = probs(j)
                delta = delta + jnp.sum(p_t * dp_t, axis=0, keepdims=True)
            dq = jnp.zeros((qn, HEAD_DIM), F32)
            for j in range(p + 1):
                ks, kn = blocks[j]
                rows = slice(ks, ks + kn)
                p_t, dp_t = probs(j)
                ds_t = p_t * (dp_t - delta)
                ds_b = ds_t.astype(MXU_DTYPE)
                dv_acc[rows, :] += _dot(p_t.astype(MXU_DTYPE), dob)
                dk_acc[rows, :] += _dot(ds_b, qb) * scale
                ds_ref[rows, :] += sum(ds_t[:, c:c + CHUNK] for c in range(0, qn, CHUNK))
                dq = dq + _dot_tn(ds_b, kb_s[rows, :])
            dq_ref[qs:qs + qn, :] = (dq * scale).astype(dq_ref.dtype)

        dk_ref[...] = dk_acc[...].astype(dk_ref.dtype)
        dv_ref[...] = dv_acc[...].astype(dv_ref.dtype)

    col = jax.ShapeDtypeStruct((l, GROUP), MXU_DTYPE)
    head_col = lambda t: pl.BlockSpec((l, HEAD_DIM), lambda h: (0, t + h))
    return pl.pallas_call(
        body, name="fox_bwd",
        out_shape=(col, col, col, jax.ShapeDtypeStruct((N_HEADS, l, 128), F32)),
        grid=(N_HEADS,),
        in_specs=[head_col(qt), head_col(kt), head_col(vt), head_col(N_HEADS),
                  pl.BlockSpec((None, l, 128), lambda h: (h, 0, 0)),
                  pl.BlockSpec((None, 1, l), lambda h: (h, 0, 0)),
                  pl.BlockSpec((None, 1, l), lambda h: (h, 0, 0))],
        out_specs=(head_col(0), head_col(0), head_col(0), pl.BlockSpec((None, l, 128), lambda h: (h, 0, 0))),
        scratch_shapes=[pltpu.VMEM((l, HEAD_DIM), F32)] * 2 + [pltpu.VMEM((l, HEAD_DIM), MXU_DTYPE)] * 4,
        compiler_params=_params(("parallel",)),
    )(proj, proj, proj, d_mix, cum_bc, cum_rows, lse_rows)


def _fox_gate_bwd(ds_sum, proj, bias_row):
    l = proj.shape[0]
    n_blocks = l // CHUNK

    def body(ds_ref, ff_ref, b_ref, dff_ref, db_ref):
        r = lax.broadcasted_iota(jnp.int32, (CHUNK, CHUNK), 0)
        cidx = lax.broadcasted_iota(jnp.int32, (CHUNK, CHUNK), 1)
        upper = jnp.where(cidx >= r, 1.0, 0.0).astype(F32)
        carry = jnp.zeros((1, 128), F32)
        db = jnp.zeros((1, 128), F32)
        for blk in reversed(range(n_blocks)):
            rows = slice(blk * CHUNK, (blk + 1) * CHUNK)
            key_sum = jnp.zeros((CHUNK, 128), F32)
            for h in range(N_HEADS):
                select = jnp.where(cidx == h, 1.0, 0.0).astype(F32)
                key_sum = key_sum + jnp.dot(ds_ref[h, rows, :], select, precision=lax.Precision.HIGHEST,
                                            preferred_element_type=F32)
            suffix = jnp.dot(upper, key_sum, precision=lax.Precision.HIGHEST, preferred_element_type=F32) + carry
            carry = suffix[0:1, :]
            _, dsig = _log_forget(ff_ref[rows, :], b_ref[...], _row_valid(blk, CHUNK))
            dff = -suffix * dsig
            dff_ref[rows, :] = dff.astype(dff_ref.dtype)
            db = db + jnp.sum(dff, axis=0, keepdims=True)
        db_ref[...] = db

    return pl.pallas_call(
        body, name="fox_gate_bwd",
        out_shape=(jax.ShapeDtypeStruct((l, 128), MXU_DTYPE), jax.ShapeDtypeStruct((1, 128), F32)),
        grid=(1,),
        in_specs=[pl.BlockSpec((N_HEADS, l, 128), lambda i: (0, 0, 0)),
                  pl.BlockSpec((l, 128), lambda i: (0, FF_TILE)),
                  pl.BlockSpec((1, 128), lambda i: (0, 0))],
        out_specs=(pl.BlockSpec((l, 128), lambda i: (0, 0)), pl.BlockSpec((1, 128), lambda i: (0, 0))),
        compiler_params=_params(("arbitrary",)),
    )(ds_sum, proj, bias_row)


def _conv(u, w, b):
    return b + w[0:1, :] * pltpu.roll(u, 2, 0) + w[1:2, :] * pltpu.roll(u, 1, 0) + w[2:3, :] * u


def _conv_act_fwd(u, conv_w, conv_b, d_ff):
    l = u.shape[0]
    tc = _divisor_tile(d_ff, 256, 128)
    nt = d_ff // tc

    def body(ug_ref, uv_ref, wg_ref, wv_ref, bg_ref, bv_ref, a_ref):
        yg = _conv(ug_ref[...], wg_ref[...], bg_ref[...])
        yv = _conv(uv_ref[...], wv_ref[...], bv_ref[...])
        act = yg * _sigmoid(yg) * yv
        a_ref[...] = jnp.where(_row_valid(0, l), act, 0.0).astype(a_ref.dtype)

    return pl.pallas_call(
        body, name="conv_act_fwd",
        out_shape=jax.ShapeDtypeStruct((l, d_ff), MXU_DTYPE),
        grid=(nt,),
        in_specs=[pl.BlockSpec((l, tc), lambda j: (0, j)), pl.BlockSpec((l, tc), lambda j: (0, j + nt)),
                  pl.BlockSpec((8, tc), lambda j: (0, j)), pl.BlockSpec((8, tc), lambda j: (0, j + nt)),
                  pl.BlockSpec((1, tc), lambda j: (0, j)), pl.BlockSpec((1, tc), lambda j: (0, j + nt))],
        out_specs=pl.BlockSpec((l, tc), lambda j: (0, j)),
        compiler_params=_params(("parallel",)),
    )(u, u, conv_w, conv_w, conv_b, conv_b)


def _conv_act_bwd(u, conv_w, conv_b, d_act, d_ff):
    l = u.shape[0]
    tc = _divisor_tile(d_ff, 256, 128)
    nt = d_ff // tc

    def body(ug_ref, uv_ref, wg_ref, wv_ref, bg_ref, bv_ref, da_ref, du_ref, dwb_ref):
        valid = _row_valid(0, l)
        ug, uv = ug_ref[...], uv_ref[...]
        wg, wv = wg_ref[...], wv_ref[...]
        yg = _conv(ug, wg, bg_ref[...])
        yv = _conv(uv, wv, bv_ref[...])
        sig = _sigmoid(yg)
        da = jnp.where(valid, da_ref[...], 0.0)
        d_yv = da * (yg * sig)
        d_yg = da * yv * (sig * (1.0 + yg * (1.0 - sig)))
        for idx, (dy, uu, w) in enumerate(((d_yg, ug, wg), (d_yv, uv, wv))):
            du = w[2:3, :] * dy + w[1:2, :] * pltpu.roll(dy, l - 1, 0) + w[0:1, :] * pltpu.roll(dy, l - 2, 0)
            du_ref[idx] = jnp.where(valid, du, 0.0).astype(du_ref.dtype)
            dwb_ref[idx, 0:1, :] = jnp.sum(dy * pltpu.roll(uu, 2, 0), axis=0, keepdims=True)
            dwb_ref[idx, 1:2, :] = jnp.sum(dy * pltpu.roll(uu, 1, 0), axis=0, keepdims=True)
            dwb_ref[idx, 2:3, :] = jnp.sum(dy * uu, axis=0, keepdims=True)
            dwb_ref[idx, 3:4, :] = jnp.sum(dy, axis=0, keepdims=True)
            dwb_ref[idx, 4:8, :] = jnp.zeros((4, tc), F32)

    return pl.pallas_call(
        body, name="conv_act_bwd",
        out_shape=(jax.ShapeDtypeStruct((2, l, d_ff), MXU_DTYPE), jax.ShapeDtypeStruct((2, 8, d_ff), F32)),
        grid=(nt,),
        in_specs=[pl.BlockSpec((l, tc), lambda j: (0, j)), pl.BlockSpec((l, tc), lambda j: (0, j + nt)),
                  pl.BlockSpec((8, tc), lambda j: (0, j)), pl.BlockSpec((8, tc), lambda j: (0, j + nt)),
                  pl.BlockSpec((1, tc), lambda j: (0, j)), pl.BlockSpec((1, tc), lambda j: (0, j + nt)),
                  pl.BlockSpec((l, tc), lambda j: (0, j))],
        out_specs=(pl.BlockSpec((2, l, tc), lambda j: (0, 0, j)), pl.BlockSpec((2, 8, tc), lambda j: (0, 0, j))),
        compiler_params=_params(("parallel",)),
    )(u, u, conv_w, conv_w, conv_b, conv_b, d_act)


def _adamw(w, g, m, v, name):
    shape = w.shape
    if w.ndim == 1:
        as2d = (1, shape[0])
    else:
        as2d = (int(np.prod(shape[:-1])), shape[-1])
    r, c = as2d
    tr = _divisor_tile(r, 256, 8)
    spec = pl.BlockSpec((tr, c), lambda i: (i, 0))

    def body(w_ref, g_ref, m_ref, v_ref, d_ref, nm_ref, nv_ref):
        d_ref[...], nm_ref[...], nv_ref[...] = _adamw_math(w_ref[...], g_ref[...], m_ref[...], v_ref[...])

    sds = jax.ShapeDtypeStruct(as2d, F32)
    outs = pl.pallas_call(
        body, name=name, out_shape=(sds, sds, sds), grid=(r // tr,),
        in_specs=[spec] * 4, out_specs=(spec,) * 3,
        compiler_params=_params(("parallel",)),
    )(w.reshape(as2d), g.reshape(as2d), m.reshape(as2d), v.reshape(as2d))
    return tuple(o.reshape(shape) for o in outs)


def _pad_rows(a, rows):
    return jnp.pad(a, ((0, rows - a.shape[0]), (0, 0)))


def kernel(x, meta_tokens, norm1_gain, w_in, b_forget, ret_norm_gain, w_out, norm2_gain, w_up, conv_w, conv_b, w_down, final_norm_gain, loss_target, m_meta_tokens, m_norm1_gain, m_w_in, m_b_forget, m_ret_norm_gain, m_w_out, m_norm2_gain, m_w_up, m_conv_w, m_conv_b, m_w_down, m_final_norm_gain, v_meta_tokens, v_norm1_gain, v_w_in, v_b_forget, v_ret_norm_gain, v_w_out, v_norm2_gain, v_w_up, v_conv_w, v_conv_b, v_w_down, v_final_norm_gain):
    seq, d = x.shape[1], x.shape[2]
    l = CHUNK + seq
    d_ff = w_down.shape[1] * N_DEV
    up_shard = w_up.shape[2]
    assert 4 * up_shard == d_ff and w_in.shape[2] == WIN_SHARD and d == 2 * GROUP
    dev = _device_index()
    mx, my, mc = _my_position()
    core = jnp.reshape(mc, (1,)).astype(jnp.int32)
    chip = jnp.reshape(2 * mx + my, (1,)).astype(jnp.int32)
    dev1 = jnp.reshape(dev, (1,)).astype(jnp.int32)

    small = jnp.concatenate([meta_tokens.reshape(-1, 128), conv_w[0].reshape(-1, 128)], axis=0)
    n_meta_rows = N_META * (d // N_DEV) // 128
    small_rows = small.shape[0]
    small_all = _all_gather(_pad_rows(small, -(-small_rows // 8) * 8), "gather_small")
    meta_full = jnp.transpose(small_all[:, :n_meta_rows].reshape(N_DEV, N_META, d // N_DEV), (1, 0, 2)).reshape(N_META, d)
    conv_w_full = _pad_rows(jnp.transpose(small_all[:, n_meta_rows:small_rows].reshape(N_DEV, 3, up_shard),
                                          (1, 0, 2)).reshape(3, 2 * d_ff), 8)
    to_rows = lambda t: jnp.pad(jnp.transpose(t[0]), ((0, WIN_ROWS - WIN_SHARD), (0, 0)))
    from_rows = lambda t: jnp.transpose(t[:WIN_SHARD])[None]
    w_in_rows = to_rows(w_in)
    out_rows = d // N_DEV
    mixer_rows = -(-(WIN_ROWS + out_rows) // 304) * 304
    mixer_shard = jnp.concatenate([w_in_rows.astype(WIRE_DTYPE), w_out[0].astype(WIRE_DTYPE),
                                   jnp.zeros((mixer_rows - WIN_ROWS - out_rows, d), WIRE_DTYPE)], axis=0)

    h0 = jnp.concatenate([jnp.zeros((PAD_ROWS, d), F32), meta_full, x[0]], axis=0)
    consts = _retention_consts(l)
    bias_row = jnp.pad(b_forget, ((0, 0), (0, 128 - N_HEADS)))
    a = _rmsnorm_fwd(h0, norm1_gain, "rmsnorm1")
    mixer_blocks = _gather_ring(mixer_shard, dev1, a, "gather_w_in")
    start_up = _gather_start(w_up[0], dev1, mixer_blocks, "gather_w_up_start")
    w_in_full = _assemble_w_in(mixer_blocks).astype(MXU_DTYPE)
    proj = _mm_nt(a, w_in_full, F32, "mm_proj", after=start_up[4])
    ret_mix, ret_pre, ret_states = _retention_fwd(proj, ret_norm_gain, consts)
    cum_bc, cum_rows = _fox_prep(proj, bias_row)
    mix, lse_rows = _fox_fwd(proj, cum_bc, cum_rows, ret_mix)
    w_out_full = mixer_blocks[:, WIN_ROWS:WIN_ROWS + out_rows].reshape(d, d).astype(MXU_DTYPE)
    h1, cn = _rmsnorm_fwd(h0, norm2_gain, "resid_rmsnorm2", res=_mm_nn(mix, w_out_full, F32, "mm_out"))
    w_up_blocks = _gather_finish(start_up, cn, "gather_w_up").astype(MXU_DTYPE)
    start_down = _gather_start(w_down[0], dev1, w_up_blocks, "gather_w_down_start")
    u = _mm(cn, w_up_blocks,
            a_spec=pl.BlockSpec((_divisor_tile(l, 1088, 16), d), lambda i, j, k: (i, 0)),
            b_spec=pl.BlockSpec((None, d, up_shard), lambda i, j, k: (j, 0, 0)),
            o_spec=pl.BlockSpec((_divisor_tile(l, 1088, 16), up_shard), lambda i, j, k: (i, j)),
            out_shape=jax.ShapeDtypeStruct((l, 2 * d_ff), F32),
            grid=(l // _divisor_tile(l, 1088, 16), N_DEV, 1), contract=(1, 0), nk=1, name="mm_up",
            after=start_down[4])
    act = _conv_act_fwd(u, conv_w_full, conv_b + start_down[4][0, 0], d_ff)
    w_down_full = _gather_finish(start_down, act, "gather_w_down").reshape(d_ff, d).astype(MXU_DTYPE)
    mlp_out = _mm_nn(act, w_down_full, F32, "mm_down", tm_cap=544, tk_cap=d_ff)
    d_h2, d_h2_b, dg_final, loss_part = _loss_head(h1, mlp_out, final_norm_gain.reshape(1, d), loss_target[0])

    gw_down = _mm_tn(act, d_h2_b, WIRE_DTYPE, "mm_gw_down", tm_cap=1408, tn_cap=1024)
    d2d_down = _reduce_scatter_d2d_start(gw_down.reshape(N_DEV, d_ff // N_DEV, d), d_h2, "rs_w_down")
    d_act = _mm_nt(d_h2_b, w_down_full, F32, "mm_d_act", after=d2d_down[4])
    rs_down = _reduce_scatter_ici_start(d2d_down, d_act, core, "rs_w_down")
    d_u, d_conv = _conv_act_bwd(u, conv_w_full, conv_b + rs_down[4][0, 0], d_act, d_ff)
    tm = _divisor_tile(l, 1088, 16)
    gw_up = _mm(cn, d_u,
                a_spec=pl.BlockSpec((l, d // 2), lambda i, j, k: (0, i)),
                b_spec=pl.BlockSpec((None, l, up_shard), lambda i, j, k: (j // 4, 0, j % 4)),
                o_spec=pl.BlockSpec((None, d // 2, up_shard), lambda i, j, k: (j, i, 0)),
                out_shape=jax.ShapeDtypeStruct((N_DEV, d, up_shard), WIRE_DTYPE),
                grid=(2, N_DEV, 1), contract=(0, 0), nk=1, name="mm_gw_up")
    d2d_up = _reduce_scatter_d2d_start(gw_up, d_act, "rs_w_up")
    d_cn = _mm_d_cn(d_u, w_up_blocks, d2d_up[4])
    rs_up = _reduce_scatter_ici_start(d2d_up, d_cn, core, "rs_w_up")
    d_h1, d_h1_b, dg_norm2 = _rmsnorm_bwd(d_h2, d_cn, h1, norm2_gain + rs_up[4][0, 0], "rmsnorm2_bwd", True)

    gw_out = _mm_tn(mix, d_h1_b, WIRE_DTYPE, "mm_gw_out")
    d2d_out = _reduce_scatter_d2d_start(gw_out.reshape(N_DEV, d // N_DEV, d), d_cn, "rs_w_out")
    d_mix = _mm_nt(d_h1_b, w_out_full, F32, "mm_d_mix", after=d2d_out[4])
    d_fq, d_fk, d_fv, ds_sum = _fox_bwd(proj, cum_bc, cum_rows, d_mix, lse_rows)
    d_ff_tile, db_forget_row = _fox_gate_bwd(ds_sum, proj, bias_row)
    d_ret, dg_ret = _retention_bwd(proj, ret_pre, ret_states, d_mix, ret_norm_gain, consts)
    rs_out = _reduce_scatter_ici_start(d2d_out, d_ret, core, "rs_w_out")
    d_proj = jnp.concatenate(
        [d_ret, d_fq, d_fk, d_fv, d_ff_tile, jnp.zeros((l, WIN_N - 7 * GROUP - 128), MXU_DTYPE)], axis=1)
    gw_in = _mm_tn(d_proj, a, WIRE_DTYPE, "mm_gw_in", tm_cap=1536, after=rs_out[4])
    rs_in = _reduce_scatter_start(_extract_w_in_windows(gw_in), core, "rs_w_in")
    d_a = _mm_nn(d_proj, w_in_full, F32, "mm_d_a", tm_cap=544, tn_cap=256, tk_cap=WIN_N, after=rs_in[4])
    d_front, d_tokens, dg_norm1 = _rmsnorm_bwd(d_h1, d_a, h0, norm1_gain + rs_in[4][0, 0], "rmsnorm1_bwd", False)
    grad_x = d_tokens[None]
    d_meta = d_front[PAD_ROWS:CHUNK]

    d_conv_w = jnp.concatenate([d_conv[0, 0:3], d_conv[1, 0:3]], axis=1)
    d_conv_b = jnp.concatenate([d_conv[0, 3:4], d_conv[1, 3:4]], axis=1)
    pieces = [loss_part[:, 0:1], dg_norm1, db_forget_row[:, 0:N_HEADS], dg_ret, dg_norm2, d_conv_b, dg_final,
              d_meta.reshape(1, -1), d_conv_w.reshape(1, -1)]
    sizes = [p.shape[1] for p in pieces]
    flat = jnp.concatenate(pieces, axis=1)
    padded = -(-flat.shape[1] // 1024) * 1024
    flat = jnp.pad(flat, ((0, 0), (0, padded - flat.shape[1]))).reshape(padded // 128, 128)
    small_ar = _small_all_reduce_start(flat, d_tokens, "all_reduce_small")

    lead = lambda outs: tuple(o[None] for o in outs)
    fin_down = lead(_reduce_scatter_finish(rs_down, small_ar[4], chip, w_down[0], m_w_down[0], v_w_down[0], "rs_w_down"))
    fin_up = lead(_reduce_scatter_finish(rs_up, fin_down[3], chip, w_up[0], m_w_up[0], v_w_up[0], "rs_w_up"))
    fin_out = lead(_reduce_scatter_finish(rs_out, fin_up[3], chip, w_out[0], m_w_out[0], v_w_out[0], "rs_w_out"))
    fin_in = tuple(from_rows(o) for o in _reduce_scatter_finish(
        rs_in, fin_out[3], chip, w_in_rows, to_rows(m_w_in), to_rows(v_w_in), "rs_w_in"))
    g_w_down, g_w_up, g_w_out, g_w_in = fin_down[0], fin_up[0], fin_out[0], fin_in[0]
    early = [fin_down[1:], fin_up[1:], fin_out[1:], fin_in[1:]]
    total = _small_all_reduce_finish(small_ar, fin_in[3], dev1, "all_reduce_small").reshape(1, padded)
    offs = np.concatenate([[0], np.cumsum(sizes)])
    take = lambda k: total[:, int(offs[k]):int(offs[k + 1])]
    loss = take(0).reshape(())
    g_norm1, g_bf, g_ret_gain, g_norm2 = take(1), take(2), take(3), take(4)
    g_conv_b, g_final = take(5), take(6).reshape(d)
    g_meta = lax.dynamic_slice(take(7).reshape(N_META, d), (jnp.int32(0), (dev * (d // N_DEV)).astype(jnp.int32)),
                               (N_META, d // N_DEV))
    g_conv_w = lax.dynamic_slice(take(8).reshape(3, 2 * d_ff), (jnp.int32(0), (dev * up_shard).astype(jnp.int32)),
                                 (3, up_shard))[None]

    weights = [meta_tokens, norm1_gain, w_in, b_forget, ret_norm_gain, w_out, norm2_gain, w_up, conv_w, conv_b,
               w_down, final_norm_gain]
    grads = [g_meta, g_norm1, g_w_in, g_bf, g_ret_gain, g_w_out, g_norm2, g_w_up, g_conv_w, g_conv_b, g_w_down,
             g_final]
    done = {"w_down": early[0], "w_up": early[1], "w_out": early[2], "w_in": early[3]}
    ms = [m_meta_tokens, m_norm1_gain, m_w_in, m_b_forget, m_ret_norm_gain, m_w_out, m_norm2_gain, m_w_up, m_conv_w,
          m_conv_b, m_w_down, m_final_norm_gain]
    vs = [v_meta_tokens, v_norm1_gain, v_w_in, v_b_forget, v_ret_norm_gain, v_w_out, v_norm2_gain, v_w_up, v_conv_w,
          v_conv_b, v_w_down, v_final_norm_gain]
    names = ["meta", "norm1", "w_in", "b_forget", "ret_gain", "w_out", "norm2", "w_up", "conv_w", "conv_b", "w_down",
             "final_gain"]
    deltas, new_ms, new_vs = [], [], []
    for w, g, m, v, n in zip(weights, grads, ms, vs, names):
        dl, nm, nv = done[n] if n in done else _adamw(w, g, m, v, "adamw_" + n)
        deltas.append(dl)
        new_ms.append(nm)
        new_vs.append(nv)
    return (loss, grad_x, *grads, *deltas, *new_ms, *new_vs)
```

```python
import functools

import numpy as np
import jax
import jax.numpy as jnp
from jax import lax
from jax.experimental import pallas as pl
from jax.experimental.pallas import tpu as pltpu

F32 = jnp.float32
MXU_DTYPE = jnp.bfloat16
WIRE_DTYPE = jnp.bfloat16

N_DEV = 8
N_META = 16
CHUNK = 128
PAD_ROWS = CHUNK - N_META
N_HEADS = 8
HEAD_DIM = 128
GROUP = N_HEADS * HEAD_DIM
IN_DIM = 7 * GROUP + N_HEADS
WIN_SHARD = IN_DIM // N_DEV
WIN_ROWS = 912
WIN_BLOCK = 1024
WIN_STRIDE = 896
WIN_N = 7680
ROPE_BASE = 10000.0
NORM_EPS = 1e-6
NEG_BIG = -1e30
ADAM_LR, ADAM_B1, ADAM_B2, ADAM_EPS, ADAM_WD, ADAM_STEP = 0.001, 0.9, 0.999, 1e-08, 0.01, 10
VMEM_LIMIT = 52 * 1024 * 1024
MESH = pl.DeviceIdType.MESH
ANY = pl.BlockSpec(memory_space=pl.ANY)
VMEM_SPEC = pl.BlockSpec(memory_space=pltpu.VMEM)


def _params(sem=None):
    kw = {"vmem_limit_bytes": VMEM_LIMIT}
    if sem is not None:
        kw["dimension_semantics"] = sem
    return pltpu.CompilerParams(**kw)


def _divisor_tile(n, cap, unit):
    if n <= cap:
        return n
    best = None
    for t in range(unit, cap + 1, unit):
        if n % t == 0:
            best = t
    assert best is not None, (n, cap, unit)
    return best


def _my_position():
    return lax.axis_index("x"), lax.axis_index("y"), lax.axis_index("c")


def _device_index():
    x, y, c = _my_position()
    return 4 * x + 2 * y + c


def _all_gather(shard, name):
    r, c = shard.shape

    def body(x_ref, out_ref, send_sems, recv_sems, local_sem):
        mx, my, mc = _my_position()
        me, sibling = (mx, my, mc), (mx, my, 1 - mc)
        chips = [(1 - mx, my), (mx, 1 - my), (1 - mx, 1 - my)]

        def slot(px, py, pc):
            return out_ref.at[4 * px + 2 * py + pc]

        def copy(k, block, to, src=None):
            return pltpu.make_async_remote_copy(
                src_ref=slot(*block) if src is None else src, dst_ref=slot(*block),
                send_sem=send_sems.at[k], recv_sem=recv_sems.at[k], device_id=to, device_id_type=MESH)

        mine = pltpu.make_async_copy(x_ref, slot(*me), local_sem)
        mine.start()
        first = [copy(0, me, sibling, src=x_ref)]
        first += [copy(1 + j, me, (*chip, mc), src=x_ref) for j, chip in enumerate(chips)]
        for cp in first:
            cp.start()
        passed = [copy(4 + j, (*chip, mc), sibling) for j, chip in enumerate(chips)]
        for j, chip in enumerate(chips):
            copy(1 + j, (*chip, mc), me).wait_recv()
            passed[j].start()
        copy(0, sibling, me).wait_recv()
        for j, chip in enumerate(chips):
            copy(4 + j, (*chip, 1 - mc), me).wait_recv()
        for cp in first + passed:
            cp.wait_send()
        mine.wait()

    return pl.pallas_call(
        body, name=name,
        out_shape=jax.ShapeDtypeStruct((N_DEV, r, c), shard.dtype),
        in_specs=[ANY], out_specs=ANY,
        scratch_shapes=[pltpu.SemaphoreType.DMA((7,)), pltpu.SemaphoreType.DMA((7,)), pltpu.SemaphoreType.DMA],
    )(shard)


HBM_SPEC = pl.BlockSpec(memory_space=pltpu.HBM)
SEM_SPEC = pl.BlockSpec(memory_space=pltpu.SEMAPHORE)
DATAFLOW_EFFECT = pltpu.SideEffectType.DATAFLOW_SIDE_EFFECTING


def _in_hbm(a):
    return pltpu.with_memory_space_constraint(a, pltpu.HBM)


def _split_start(src, land, make_copies, n_copies, after, name):
    if isinstance(land, tuple):
        land = lax.empty(land, src.dtype)
    land_shape = land.shape
    def body(src_ref, land_ref, after_ref, send_sems, recv_sems, src_thru, land_thru, token):
        for cp in make_copies(src_ref, land_ref, send_sems, recv_sems):
            cp.start()
        token[...] = jnp.zeros_like(token)

    return pl.pallas_call(
        body, name=name,
        out_shape=(pltpu.SemaphoreType.DMA((n_copies,)), pltpu.SemaphoreType.DMA((n_copies,)),
                   pltpu.HBM(src.shape, src.dtype), pltpu.HBM(land_shape, land.dtype),
                   jax.ShapeDtypeStruct((8, 128), F32)),
        in_specs=(HBM_SPEC, HBM_SPEC, ANY), out_specs=(SEM_SPEC, SEM_SPEC, HBM_SPEC, HBM_SPEC, VMEM_SPEC),
        input_output_aliases={0: 2, 1: 3},
        compiler_params=pltpu.CompilerParams(has_side_effects=DATAFLOW_EFFECT),
    )(_in_hbm(src), _in_hbm(land), after)


def _split_wait(started, after, make_copies, name):
    send_sems, recv_sems, src_thru, land_thru, _ = started

    def body(src_ref, land_ref, send_sems_ref, recv_sems_ref, after_ref, src_dead, land_out):
        for cp in make_copies(src_ref, land_ref, send_sems_ref, recv_sems_ref):
            cp.wait_send()
            cp.wait_recv()

    return pl.pallas_call(
        body, name=name,
        out_shape=(pltpu.HBM(src_thru.shape, src_thru.dtype), pltpu.HBM(land_thru.shape, land_thru.dtype)),
        in_specs=(HBM_SPEC, HBM_SPEC, SEM_SPEC, SEM_SPEC, ANY), out_specs=(HBM_SPEC, HBM_SPEC),
        input_output_aliases={0: 0, 1: 1},
        compiler_params=pltpu.CompilerParams(has_side_effects=DATAFLOW_EFFECT),
    )(src_thru, land_thru, send_sems, recv_sems, after)


def _gather_copies(x_ref, land_ref, send_sems, recv_sems):
    mx, my, mc = _my_position()
    me = 4 * mx + 2 * my + mc
    targets = [(mx, my, 1 - mc), (1 - mx, my, mc), (mx, 1 - my, mc), (1 - mx, 1 - my, mc)]
    return [pltpu.make_async_remote_copy(
        src_ref=land_ref.at[me], dst_ref=land_ref.at[me], send_sem=send_sems.at[k], recv_sem=recv_sems.at[k],
        device_id=t, device_id_type=MESH) for k, t in enumerate(targets)]


def _own_slot(shard, dev, name):
    r, c = shard.shape
    tr = _divisor_tile(r, 512, 16)

    def body(s_ref, x_ref, o_ref):
        o_ref[...] = x_ref[...].astype(o_ref.dtype)

    return pl.pallas_call(
        body, name=name,
        out_shape=jax.ShapeDtypeStruct((N_DEV, r, c), WIRE_DTYPE),
        grid_spec=pltpu.PrefetchScalarGridSpec(
            num_scalar_prefetch=1, grid=(r // tr,),
            in_specs=[pl.BlockSpec((tr, c), lambda i, s: (i, 0))],
            out_specs=pl.BlockSpec((None, tr, c), lambda i, s: (s[0], i, 0))),
        compiler_params=_params(("parallel",)),
    )(dev, shard)


def _gather_start(shard, dev, after, name):
    return _split_start(jnp.zeros((8, 128), F32), _own_slot(shard, dev, name + "_own"), _gather_copies, 4, after, name)


def _gather_ring(shard, dev, after, name):
    r, c = shard.shape
    half = r // 2
    assert half % 16 == 0

    def body(x_ref, after_ref, land_in, land_ref, send_sems, recv_sems):
        mx, my, mc = _my_position()
        sibling, x_nbr, y_nbr = (mx, my, 1 - mc), (1 - mx, my, mc), (mx, 1 - my, mc)
        first, second = pl.ds(0, half), pl.ds(half, half)

        def slot(px, py, pc):
            return land_ref.at[4 * px + 2 * py + pc]

        def copy(k, src, dst, to):
            return pltpu.make_async_remote_copy(src_ref=src, dst_ref=dst, send_sem=send_sems.at[k],
                                                recv_sem=recv_sems.at[k], device_id=to, device_id_type=MESH)

        def arrived(k, dst):
            copy(k, dst, dst, sibling).wait_recv()

        mine = slot(mx, my, mc)
        from_x, from_y, from_d = slot(1 - mx, my, mc), slot(mx, 1 - my, mc), slot(1 - mx, 1 - my, mc)
        sent = [copy(0, x_ref, mine, sibling), copy(1, x_ref, mine, x_nbr), copy(2, x_ref, mine, y_nbr)]
        for cp in sent:
            cp.start()

        def send(k, src, to):
            cp = copy(k, src, src, to)
            cp.start()
            sent.append(cp)

        arrived(1, from_x)
        send(3, from_x.at[first], y_nbr)
        send(5, from_x, sibling)
        arrived(2, from_y)
        send(4, from_y.at[second], x_nbr)
        send(6, from_y, sibling)
        arrived(3, from_d.at[first])
        send(7, from_d.at[first], sibling)
        arrived(4, from_d.at[second])
        send(8, from_d.at[second], sibling)
        arrived(0, slot(mx, my, 1 - mc))
        arrived(5, slot(1 - mx, my, 1 - mc))
        arrived(6, slot(mx, 1 - my, 1 - mc))
        arrived(7, slot(1 - mx, 1 - my, 1 - mc).at[first])
        arrived(8, slot(1 - mx, 1 - my, 1 - mc).at[second])
        for cp in sent:
            cp.wait_send()

    land = _own_slot(shard, dev, name + "_own")
    return pl.pallas_call(
        body, name=name,
        out_shape=jax.ShapeDtypeStruct(land.shape, land.dtype),
        in_specs=[ANY, ANY, ANY], out_specs=ANY,
        input_output_aliases={2: 0},
        scratch_shapes=[pltpu.SemaphoreType.DMA((9,)), pltpu.SemaphoreType.DMA((9,))],
    )(shard, after, land)


def _gather_finish(started, after, name):
    _, land = _split_wait(started, after, _gather_copies, name + "_wait")

    def body(land_in, land_ref, send_sems, recv_sems):
        mx, my, mc = _my_position()
        chips = [(1 - mx, my), (mx, 1 - my), (1 - mx, 1 - my)]
        copies = [pltpu.make_async_remote_copy(
            src_ref=land_ref.at[4 * cx + 2 * cy + mc], dst_ref=land_ref.at[4 * cx + 2 * cy + mc],
            send_sem=send_sems.at[j], recv_sem=recv_sems.at[j],
            device_id=(mx, my, 1 - mc), device_id_type=MESH) for j, (cx, cy) in enumerate(chips)]
        for cp in copies:
            cp.start()
        for j, (cx, cy) in enumerate(chips):
            copies[j].wait_send()
            pltpu.make_async_remote_copy(
                src_ref=land_ref.at[4 * cx + 2 * cy + 1 - mc], dst_ref=land_ref.at[4 * cx + 2 * cy + 1 - mc],
                send_sem=send_sems.at[j], recv_sem=recv_sems.at[j],
                device_id=(mx, my, 1 - mc), device_id_type=MESH).wait_recv()

    return pl.pallas_call(
        body, name=name + "_pass",
        out_shape=jax.ShapeDtypeStruct(land.shape, land.dtype),
        in_specs=[ANY], out_specs=ANY,
        input_output_aliases={0: 0},
        scratch_shapes=[pltpu.SemaphoreType.DMA((3,)), pltpu.SemaphoreType.DMA((3,))],
    )(land)


def _chip_copies(p_ref, land_ref, send_sems, recv_sems):
    mx, my, mc = _my_position()
    chips = [(1 - mx, my), (mx, 1 - my), (1 - mx, 1 - my)]
    return [pltpu.make_async_remote_copy(
        src_ref=p_ref.at[2 * cx + cy], dst_ref=land_ref.at[j], send_sem=send_sems.at[j], recv_sem=recv_sems.at[j],
        device_id=(cx, cy, mc), device_id_type=MESH) for j, (cx, cy) in enumerate(chips)]


def _reduce_scatter_start(g, core, name):
    pair = _pair_sum(g, _exchange_sibling(g, name + "_d2d"), core, name + "_pairsum")
    return _split_start(pair, (3,) + pair.shape[1:], _chip_copies, 3, g, name + "_ici_start")


def _sibling_copies(g_ref, land_ref, send_sems, recv_sems):
    mx, my, mc = _my_position()
    return [pltpu.make_async_remote_copy(
        src_ref=g_ref.at[2 * k + (1 - mc)], dst_ref=land_ref.at[k], send_sem=send_sems.at[k], recv_sem=recv_sems.at[k],
        device_id=(mx, my, 1 - mc), device_id_type=MESH) for k in range(4)]


def _reduce_scatter_d2d_start(g, after, name):
    return _split_start(g, (4,) + g.shape[1:], _sibling_copies, 4, after, name + "_d2d_start")


def _reduce_scatter_ici_start(d2d_started, after, core, name):
    g, from_sibling = _split_wait(d2d_started, after, _sibling_copies, name + "_d2d_wait")
    pair = _pair_sum(g, from_sibling, core, name + "_pairsum")
    return _split_start(pair, (3,) + pair.shape[1:], _chip_copies, 3, g, name + "_ici_start")


def _reduce_scatter_finish(started, after, chip, w, m, v, name):
    pair, from_chips = _split_wait(started, after, _chip_copies, name + "_ici_wait")
    return _final_sum_adamw(pair, from_chips, chip, w, m, v, name + "_sum_adamw")


def _exchange_sibling(g, name):
    _, r, c = g.shape

    def body(g_ref, out_ref, send_sems, recv_sems):
        mx, my, mc = _my_position()
        copies = [
            pltpu.make_async_remote_copy(
                src_ref=g_ref.at[2 * k + (1 - mc)], dst_ref=out_ref.at[k],
                send_sem=send_sems.at[k], recv_sem=recv_sems.at[k],
                device_id=(mx, my, 1 - mc), device_id_type=MESH)
            for k in range(4)]
        for cp in copies:
            cp.start()
        for cp in copies:
            cp.wait()

    return pl.pallas_call(
        body, name=name,
        out_shape=jax.ShapeDtypeStruct((4, r, c), g.dtype),
        in_specs=[ANY], out_specs=ANY,
        scratch_shapes=[pltpu.SemaphoreType.DMA((4,)), pltpu.SemaphoreType.DMA((4,))],
    )(g)


def _pair_sum(g, recv, core, name):
    _, r, c = g.shape
    tr = _divisor_tile(r, 512, 16)

    def body(s_ref, g_ref, r_ref, o_ref):
        o_ref[...] = (g_ref[...].astype(F32) + r_ref[...].astype(F32)).astype(o_ref.dtype)

    return pl.pallas_call(
        body, name=name,
        out_shape=jax.ShapeDtypeStruct((4, r, c), g.dtype),
        grid_spec=pltpu.PrefetchScalarGridSpec(
            num_scalar_prefetch=1, grid=(4, r // tr),
            in_specs=[pl.BlockSpec((None, tr, c), lambda k, i, s: (2 * k + s[0], i, 0)),
                      pl.BlockSpec((None, tr, c), lambda k, i, s: (k, i, 0))],
            out_specs=pl.BlockSpec((None, tr, c), lambda k, i, s: (k, i, 0))),
        compiler_params=_params(("parallel", "parallel")),
    )(core, g, recv)


def _adamw_math(w, g, m, v):
    nm = ADAM_B1 * m + (1.0 - ADAM_B1) * g
    nv = ADAM_B2 * v + (1.0 - ADAM_B2) * (g * g)
    m_hat = nm / (1.0 - ADAM_B1 ** ADAM_STEP)
    v_hat = nv / (1.0 - ADAM_B2 ** ADAM_STEP)
    return -ADAM_LR * (m_hat / (jnp.sqrt(v_hat) + ADAM_EPS) + ADAM_WD * w), nm, nv


def _final_sum_adamw(p, recv, chip, w, m, v, name):
    _, r, c = p.shape
    tr = _divisor_tile(r, 256, 16)
    tile = lambda: pl.BlockSpec((tr, c), lambda i, s: (i, 0))

    def body(s_ref, p_ref, r_ref, w_ref, m_ref, v_ref, g_ref, d_ref, nm_ref, nv_ref):
        g = p_ref[...].astype(F32)
        for j in range(3):
            g = g + r_ref[j].astype(F32)
        g_ref[...] = g
        d_ref[...], nm_ref[...], nv_ref[...] = _adamw_math(w_ref[...], g, m_ref[...], v_ref[...])

    sds = jax.ShapeDtypeStruct((r, c), F32)
    return pl.pallas_call(
        body, name=name,
        out_shape=(sds, sds, sds, sds),
        grid_spec=pltpu.PrefetchScalarGridSpec(
            num_scalar_prefetch=1, grid=(r // tr,),
            in_specs=[pl.BlockSpec((None, tr, c), lambda i, s: (s[0], i, 0)),
                      pl.BlockSpec((3, tr, c), lambda i, s: (0, i, 0)), tile(), tile(), tile()],
            out_specs=(tile(), tile(), tile(), tile())),
        compiler_params=_params(("parallel",)),
    )(chip, p, recv, w, m, v)


def _all_to_all_copies(v_ref, land_ref, send_sems, recv_sems):
    mx, my, mc = _my_position()
    me = 4 * mx + 2 * my + mc
    copies = []
    for rel in range(1, N_DEV):
        bx, by, bc = (rel >> 2) & 1, (rel >> 1) & 1, rel & 1
        target = (1 - mx if bx else mx, 1 - my if by else my, 1 - mc if bc else mc)
        copies.append(pltpu.make_async_remote_copy(
            src_ref=v_ref, dst_ref=land_ref.at[me], send_sem=send_sems.at[rel - 1], recv_sem=recv_sems.at[rel - 1],
            device_id=target, device_id_type=MESH))
    return copies


def _small_all_reduce_start(v, after, name):
    return _split_start(v, (N_DEV,) + v.shape, _all_to_all_copies, N_DEV - 1, after, name + "_start")


def _small_all_reduce_finish(started, after, dev, name):
    v, land = _split_wait(started, after, _all_to_all_copies, name + "_wait")
    rows = v.shape[0]

    def body(me_ref, v_ref, land_ref, o_ref):
        for j in range(N_DEV):
            @pl.when(me_ref[0] == j)
            def _():
                o_ref[...] = v_ref[...] if j == 0 else o_ref[...] + v_ref[...]

            @pl.when(me_ref[0] != j)
            def _():
                o_ref[...] = land_ref[j] if j == 0 else o_ref[...] + land_ref[j]

    return pl.pallas_call(
        body, name=name + "_sum",
        out_shape=jax.ShapeDtypeStruct((rows, 128), F32),
        grid_spec=pltpu.PrefetchScalarGridSpec(
            num_scalar_prefetch=1, grid=(1,),
            in_specs=[pl.BlockSpec((rows, 128), lambda i, s: (0, 0)),
                      pl.BlockSpec((N_DEV, rows, 128), lambda i, s: (0, 0, 0))],
            out_specs=pl.BlockSpec((rows, 128), lambda i, s: (0, 0))),
        compiler_params=_params(("arbitrary",)),
    )(dev, v, land)


def _assemble_w_in(blocks):
    rows, d = WIN_ROWS, blocks.shape[2]
    tc = _divisor_tile(d, 256, 128)
    n_tiles = WIN_N // 128
    last = (N_DEV * WIN_STRIDE) // 128

    def body(b_ref, o_ref):
        win = []
        for i in range(N_DEV):
            w = jnp.concatenate([b_ref[i].astype(F32), jnp.zeros((WIN_BLOCK - rows, tc), F32)], axis=0)
            win.append(pltpu.roll(w, i, 0) if i else w)
        for t in range(n_tiles):
            if t > last:
                o_ref[t * 128:(t + 1) * 128, :] = jnp.zeros((128, tc), o_ref.dtype)
                continue
            i = min(t // 7, N_DEV - 1)
            k = t - 7 * i
            val = win[i][k * 128:(k + 1) * 128, :]
            if k == 0 and i >= 1:
                val = val + win[i - 1][7 * 128:8 * 128, :]
            o_ref[t * 128:(t + 1) * 128, :] = val.astype(o_ref.dtype)

    return pl.pallas_call(
        body, name="assemble_w_in",
        out_shape=jax.ShapeDtypeStruct((WIN_N, d), blocks.dtype),
        grid=(d // tc,),
        in_specs=[pl.BlockSpec((N_DEV, rows, tc), lambda j: (0, 0, j))],
        out_specs=pl.BlockSpec((WIN_N, tc), lambda j: (0, j)),
        compiler_params=_params(("parallel",)),
    )(blocks)


def _extract_w_in_windows(g):
    _, d = g.shape
    tc = _divisor_tile(d, 256, 128)

    def body(g_ref, o_ref):
        for j in range(N_DEV):
            w = g_ref[WIN_STRIDE * j:WIN_STRIDE * j + WIN_BLOCK, :].astype(F32)
            w = pltpu.roll(w, WIN_BLOCK - j, 0) if j else w
            o_ref[j] = w[0:WIN_ROWS, :].astype(o_ref.dtype)

    return pl.pallas_call(
        body, name="extract_w_in_windows",
        out_shape=jax.ShapeDtypeStruct((N_DEV, WIN_ROWS, d), g.dtype),
        grid=(d // tc,),
        in_specs=[pl.BlockSpec((WIN_N, tc), lambda j: (0, j))],
        out_specs=pl.BlockSpec((N_DEV, WIN_ROWS, tc), lambda j: (0, 0, j)),
        compiler_params=_params(("parallel",)),
    )(g)


def _mm(a, b, *, a_spec, b_spec, o_spec, out_shape, grid, contract, nk, name, after=None):
    dn = (((contract[0],), (contract[1],)), ((), ()))
    tm, tn = o_spec.block_shape[-2:]
    behind = [] if after is None else [after]

    def body(a_ref, b_ref, *rest):
        o_ref, *scratch = rest[len(behind):]
        part = lax.dot_general(a_ref[...], b_ref[...], dn, preferred_element_type=F32)
        if nk == 1:
            o_ref[...] = part.astype(o_ref.dtype)
            return
        acc = scratch[0]
        k = pl.program_id(2)

        @pl.when(k == 0)
        def _():
            acc[...] = part

        @pl.when(k > 0)
        def _():
            acc[...] += part

        @pl.when(k == nk - 1)
        def _():
            o_ref[...] = acc[...].astype(o_ref.dtype)

    return pl.pallas_call(
        body, name=name, out_shape=out_shape, grid=grid,
        in_specs=[a_spec, b_spec] + [ANY] * len(behind), out_specs=o_spec,
        scratch_shapes=[] if nk == 1 else [pltpu.VMEM((tm, tn), F32)],
        compiler_params=_params(("parallel", "parallel", "arbitrary")),
    )(a, b, *behind)


def _mm_nn(a, b, out_dtype, name, tm_cap=1088, tn_cap=512, tk_cap=2048, after=None):
    m, k = a.shape
    _, n = b.shape
    tm, tn, tk = _divisor_tile(m, tm_cap, 16), _divisor_tile(n, tn_cap, 128), _divisor_tile(k, tk_cap, 128)
    return _mm(a, b,
               a_spec=pl.BlockSpec((tm, tk), lambda i, j, kk: (i, kk)),
               b_spec=pl.BlockSpec((tk, tn), lambda i, j, kk: (kk, j)),
               o_spec=pl.BlockSpec((tm, tn), lambda i, j, kk: (i, j)),
               out_shape=jax.ShapeDtypeStruct((m, n), out_dtype),
               grid=(m // tm, n // tn, k // tk), contract=(1, 0), nk=k // tk, name=name, after=after)


def _mm_nt(a, b, out_dtype, name, tm_cap=1088, tn_cap=512, tk_cap=2048, after=None):
    m, k = a.shape
    n, _ = b.shape
    tm, tn, tk = _divisor_tile(m, tm_cap, 16), _divisor_tile(n, tn_cap, 128), _divisor_tile(k, tk_cap, 128)
    return _mm(a, b,
               a_spec=pl.BlockSpec((tm, tk), lambda i, j, kk: (i, kk)),
               b_spec=pl.BlockSpec((tn, tk), lambda i, j, kk: (j, kk)),
               o_spec=pl.BlockSpec((tm, tn), lambda i, j, kk: (i, j)),
               out_shape=jax.ShapeDtypeStruct((m, n), out_dtype),
               grid=(m // tm, n // tn, k // tk), contract=(1, 1), nk=k // tk, name=name, after=after)


def _mm_tn(a, b, out_dtype, name, tm_cap=1024, tn_cap=512, after=None):
    l, m = a.shape
    _, n = b.shape
    tm, tn = _divisor_tile(m, tm_cap, 128), _divisor_tile(n, tn_cap, 128)
    return _mm(a, b,
               a_spec=pl.BlockSpec((l, tm), lambda i, j, kk: (0, i)),
               b_spec=pl.BlockSpec((l, tn), lambda i, j, kk: (0, j)),
               o_spec=pl.BlockSpec((tm, tn), lambda i, j, kk: (i, j)),
               out_shape=jax.ShapeDtypeStruct((m, n), out_dtype),
               grid=(m // tm, n // tn, 1), contract=(0, 0), nk=1, name=name, after=after)


def _pair_split(shard):
    left = shard % ATTN_BLOCK
    assert left in (0, CHUNK) and shard > left
    return shard - left, left


def _mm_up(cn, w_up_blocks, after):
    l, d = cn.shape
    n, _, shard = w_up_blocks.shape
    main, left = _pair_split(shard)
    tm = _divisor_tile(l, 544, 16)

    def body(a_ref, b_ref, after_ref, o_ref):
        a = a_ref[...]
        for s in range(2):
            o_ref[:, s * shard:s * shard + main] = _dot(a, b_ref[s, :, 0:main])
        if left:
            tail = _dot(a, jnp.concatenate([b_ref[0, :, main:], b_ref[1, :, main:]], axis=1))
            o_ref[:, main:shard] = tail[:, 0:left]
            o_ref[:, shard + main:2 * shard] = tail[:, left:]

    return pl.pallas_call(
        body, name="mm_up", out_shape=jax.ShapeDtypeStruct((l, n * shard), F32), grid=(l // tm, n // 2),
        in_specs=[pl.BlockSpec((tm, d), lambda i, j: (i, 0)),
                  pl.BlockSpec((2, d, shard), lambda i, j: (j, 0, 0)), ANY],
        out_specs=pl.BlockSpec((tm, 2 * shard), lambda i, j: (i, j)),
        compiler_params=_params(("parallel", "parallel")),
    )(cn, w_up_blocks, after)


def _mm_gw_up(cn, d_u):
    l, d = cn.shape
    _, _, d_ff = d_u.shape
    shard = 2 * d_ff // N_DEV
    pairs_per_half = d_ff // (2 * shard)
    tm = _divisor_tile(d, 512, 128)

    def body(a_ref, b_ref, o_ref):
        res = _dot_tn(a_ref[...], b_ref[...])
        o_ref[0] = res[:, 0:shard].astype(o_ref.dtype)
        o_ref[1] = res[:, shard:].astype(o_ref.dtype)

    return pl.pallas_call(
        body, name="mm_gw_up", out_shape=jax.ShapeDtypeStruct((N_DEV, d, shard), WIRE_DTYPE),
        grid=(d // tm, N_DEV // 2),
        in_specs=[pl.BlockSpec((l, tm), lambda i, j: (0, i)),
                  pl.BlockSpec((None, l, 2 * shard), lambda i, j: (j // pairs_per_half, 0, j % pairs_per_half))],
        out_specs=pl.BlockSpec((2, tm, shard), lambda i, j: (j, i, 0)),
        compiler_params=_params(("parallel", "parallel")),
    )(cn, d_u)


def _mm_d_cn(d_u, w_up_blocks, after):
    _, l, d_ff = d_u.shape
    n, d, shard = w_up_blocks.shape
    per = d_ff // shard
    main, left = _pair_split(shard)
    tm, tn = _divisor_tile(l, 544, 16), _divisor_tile(d, 256, 128)

    def body(a_ref, b_ref, after_ref, o_ref):
        acc = None
        for k in range(0, n, 2):
            half, c0 = k // per, (k % per) * shard
            parts = [_dot_nt(a_ref[half, :, c0 + s * shard:c0 + s * shard + main], b_ref[k + s, :, 0:main])
                     for s in range(2)]
            if left:
                a_tail = jnp.concatenate([a_ref[half, :, c0 + s * shard + main:c0 + (s + 1) * shard] for s in range(2)],
                                         axis=1)
                b_tail = jnp.concatenate([b_ref[k + s, :, main:] for s in range(2)], axis=1)
                parts.append(_dot_nt(a_tail, b_tail))
            for part in parts:
                acc = part if acc is None else acc + part
        o_ref[...] = acc

    return pl.pallas_call(
        body, name="mm_d_cn", out_shape=jax.ShapeDtypeStruct((l, d), F32), grid=(l // tm, d // tn),
        in_specs=[pl.BlockSpec((2, tm, d_ff), lambda i, j: (0, i, 0)),
                  pl.BlockSpec((n, tn, shard), lambda i, j: (0, j, 0)), ANY],
        out_specs=pl.BlockSpec((tm, tn), lambda i, j: (i, j)),
        compiler_params=_params(("parallel", "parallel")),
    )(d_u, w_up_blocks, after)


def _row_tile(l):
    return _divisor_tile(l, 544, 8)


def _rmsnorm_fwd(h, gain, name, res=None):
    l, d = h.shape
    tr = _row_tile(l)
    row = pl.BlockSpec((tr, d), lambda i: (i, 0))
    vec = pl.BlockSpec((1, d), lambda i: (0, 0))

    def body(*refs):
        if res is None:
            h_ref, g_ref, n_ref = refs
            x = h_ref[...]
        else:
            h_ref, r_ref, g_ref, s_ref, n_ref = refs
            x = h_ref[...] + r_ref[...]
            s_ref[...] = x
        y = x * lax.rsqrt(jnp.mean(x * x, axis=-1, keepdims=True) + NORM_EPS)
        n_ref[...] = (y * g_ref[...]).astype(n_ref.dtype)

    normed = jax.ShapeDtypeStruct((l, d), MXU_DTYPE)
    if res is None:
        return pl.pallas_call(body, name=name, out_shape=normed, grid=(l // tr,), in_specs=[row, vec],
                              out_specs=row, compiler_params=_params(("parallel",)))(h, gain)
    return pl.pallas_call(body, name=name, out_shape=(jax.ShapeDtypeStruct((l, d), F32), normed),
                          grid=(l // tr,), in_specs=[row, row, vec], out_specs=(row, row),
                          compiler_params=_params(("parallel",)))(h, res, gain)


def _rmsnorm_bwd(d_res, d_normed, x, gain, name, with_mxu_copy):
    l, d = x.shape
    tr = _row_tile(l) if with_mxu_copy else CHUNK
    row = pl.BlockSpec((tr, d), lambda i: (i, 0))
    vec = pl.BlockSpec((1, d), lambda i: (0, 0))

    def body(dres_ref, dn_ref, x_ref, g_ref, dx_ref, other_ref, dg_ref):
        i = pl.program_id(0)
        xv = x_ref[...]
        r = lax.rsqrt(jnp.mean(xv * xv, axis=-1, keepdims=True) + NORM_EPS)
        xh = xv * r
        dn = dn_ref[...]
        dxh = dn * g_ref[...]
        dx = dres_ref[...] + r * (dxh - xh * jnp.mean(dxh * xh, axis=-1, keepdims=True))
        if with_mxu_copy:
            dx_ref[...] = dx
            other_ref[...] = dx.astype(MXU_DTYPE)
        else:
            @pl.when(i == 0)
            def _():
                dx_ref[...] = dx

            @pl.when(i > 0)
            def _():
                other_ref[...] = dx

        @pl.when(i == 0)
        def _():
            dg_ref[...] = jnp.zeros_like(dg_ref)

        dg_ref[...] += jnp.sum(dn * xh, axis=0, keepdims=True)

    if with_mxu_copy:
        outs = [jax.ShapeDtypeStruct((l, d), F32), jax.ShapeDtypeStruct((l, d), MXU_DTYPE)]
        specs = [row, row]
    else:
        outs = [jax.ShapeDtypeStruct((CHUNK, d), F32), jax.ShapeDtypeStruct((l - CHUNK, d), F32)]
        specs = [pl.BlockSpec((CHUNK, d), lambda i: (0, 0)), pl.BlockSpec((CHUNK, d), lambda i: (jnp.maximum(i - 1, 0), 0))]
    outs.append(jax.ShapeDtypeStruct((1, d), F32))
    specs.append(vec)
    return pl.pallas_call(body, name=name, out_shape=tuple(outs), grid=(l // tr,),
                          in_specs=[row, row, row, vec], out_specs=tuple(specs),
                          compiler_params=_params(("arbitrary",)))(d_res, d_normed, x, gain)


def _loss_head(h1, mlp_out, gain, target):
    l, d = h1.shape
    n_blocks = l // CHUNK
    row = pl.BlockSpec((CHUNK, d), lambda i: (i, 0))
    vec = pl.BlockSpec((1, d), lambda i: (0, 0))
    tgt = pl.BlockSpec((CHUNK, d), lambda i: (jnp.maximum(i - 1, 0), 0))

    def body(h_ref, m_ref, g_ref, t_ref, dh_ref, dhb_ref, dg_ref, loss_ref, sq_ref):
        i = pl.program_id(0)
        x = h_ref[...] + m_ref[...]
        r = lax.rsqrt(jnp.mean(x * x, axis=-1, keepdims=True) + NORM_EPS)
        xh = x * r
        g = g_ref[...]
        real = i >= 1
        err = jnp.where(real, xh * g - t_ref[...], 0.0)
        dy = err * (1.0 / d)
        dxh = dy * g
        dh = r * (dxh - xh * jnp.mean(dxh * xh, axis=-1, keepdims=True))
        dh_ref[...] = dh
        dhb_ref[...] = dh.astype(MXU_DTYPE)

        @pl.when(i == 0)
        def _():
            dg_ref[...] = jnp.zeros_like(dg_ref)
            sq_ref[...] = jnp.zeros_like(sq_ref)

        dg_ref[...] += jnp.sum(dy * xh, axis=0, keepdims=True)
        sq_ref[...] += jnp.sum(err * err, axis=0, keepdims=True)

        @pl.when(i == n_blocks - 1)
        def _():
            total = jnp.sum(sq_ref[...], axis=-1, keepdims=True) * (0.5 / d)
            loss_ref[...] = jnp.broadcast_to(total, (1, 128))

    return pl.pallas_call(
        body, name="loss_head",
        out_shape=(jax.ShapeDtypeStruct((l, d), F32), jax.ShapeDtypeStruct((l, d), MXU_DTYPE),
                   jax.ShapeDtypeStruct((1, d), F32), jax.ShapeDtypeStruct((1, 128), F32)),
        grid=(n_blocks,), in_specs=[row, row, vec, tgt],
        out_specs=(row, row, vec, pl.BlockSpec((1, 128), lambda i: (0, 0))),
        scratch_shapes=[pltpu.VMEM((1, d), F32)],
        compiler_params=_params(("arbitrary",)),
    )(h1, mlp_out, gain, target)


def _dot(a, b):
    return jnp.dot(a, b, preferred_element_type=F32)


def _dot_nt(a, b):
    return lax.dot_general(a, b, (((1,), (1,)), ((), ())), preferred_element_type=F32)


def _dot_tn(a, b):
    return lax.dot_general(a, b, (((0,), (0,)), ((), ())), preferred_element_type=F32)


def _rope(t, cos2, sin2):
    return t * cos2 + pltpu.roll(t, HEAD_DIM // 2, 1) * sin2


def _rope_bwd(dr, cos2, sin2):
    return dr * cos2 + pltpu.roll(dr * sin2, HEAD_DIM // 2, 1)


def _sigmoid(x):
    return 1.0 / (1.0 + jnp.exp(-x))


def _row_valid(block, rows):
    r = block * CHUNK + lax.broadcasted_iota(jnp.int32, (rows, 1), 0)
    return r >= PAD_ROWS


def _retention_consts(l):
    pos = jnp.arange(l, dtype=F32) - PAD_ROWS
    inv_freq = 1.0 / (ROPE_BASE ** (jnp.arange(0, HEAD_DIM, 2, dtype=F32) / HEAD_DIM))
    ang = pos[:, None] * inv_freq[None, :]
    cos, sin = jnp.cos(ang), jnp.sin(ang)
    cos2 = jnp.concatenate([cos, cos], axis=-1)
    sin2 = jnp.concatenate([-sin, sin], axis=-1)
    log_g = jnp.log1p(-jnp.exp2(-5.0 - jnp.arange(N_HEADS, dtype=F32)))
    idx = jnp.arange(CHUNK, dtype=F32)
    diff = idx[:, None] - idx[None, :]
    decay = jnp.where(diff >= 0, jnp.exp(jnp.maximum(diff, 0.0)[None] * log_g[:, None, None]), 0.0)
    xi = jnp.exp((idx + 1.0)[None, :] * log_g[:, None])
    zeta = jnp.exp((CHUNK - 1.0 - idx)[None, :] * log_g[:, None])
    g_chunk = jnp.exp(CHUNK * log_g)
    bcast = lambda v: jnp.broadcast_to(v[:, :, None], (N_HEADS, CHUNK, HEAD_DIM))
    g_rows = jnp.broadcast_to(g_chunk[:, None, None], (N_HEADS, 8, HEAD_DIM))
    return cos2, sin2, decay, bcast(xi), bcast(zeta), g_rows


def _retention_fwd(proj, ret_gain, consts):
    l = proj.shape[0]
    n_chunks = l // CHUNK
    cos2, sin2, decay, xi, zeta, g_rows = consts
    scale = HEAD_DIM ** -0.5

    def body(p_ref, cos_ref, sin_ref, dec_ref, xi_ref, zeta_ref, gr_ref, gain_ref,
             mix_ref, o_ref, st_ref, state):
        c = pl.program_id(0)

        @pl.when(c == 0)
        def _():
            state[...] = jnp.zeros_like(state)

        cos_v, sin_v = cos_ref[...], sin_ref[...]
        valid = _row_valid(c, CHUNK)
        for h in range(N_HEADS):
            cols = slice(h * HEAD_DIM, (h + 1) * HEAD_DIM)
            q = p_ref[:, h * HEAD_DIM:(h + 1) * HEAD_DIM]
            k = p_ref[:, GROUP + h * HEAD_DIM:GROUP + (h + 1) * HEAD_DIM]
            v = p_ref[:, 2 * GROUP + h * HEAD_DIM:2 * GROUP + (h + 1) * HEAD_DIM]
            g = p_ref[:, 3 * GROUP + h * HEAD_DIM:3 * GROUP + (h + 1) * HEAD_DIM]
            rq = _rope(q, cos_v, sin_v).astype(MXU_DTYPE)
            rk = _rope(k, cos_v, sin_v) * scale
            rkb = rk.astype(MXU_DTYPE)
            vb = v.astype(MXU_DTYPE)
            st = state[h]
            st_ref[h] = st
            s = _dot_nt(rq, rkb) * dec_ref[h]
            o = _dot(s.astype(MXU_DTYPE), vb) + _dot(rq, st.astype(MXU_DTYPE)) * xi_ref[h]
            kz = (rk * zeta_ref[h]).astype(MXU_DTYPE)
            state[h] = gr_ref[h, 0:1, :] * st + _dot_tn(kz, vb)
            o_ref[:, cols] = o
            mu = jnp.mean(o, axis=-1, keepdims=True)
            oc = o - mu
            yn = oc * lax.rsqrt(jnp.mean(oc * oc, axis=-1, keepdims=True) + NORM_EPS)
            ret = (g * _sigmoid(g)) * (yn * gain_ref[:, cols])
            mix_ref[:, cols] = jnp.where(valid, ret, 0.0).astype(mix_ref.dtype)

    head_tab = pl.BlockSpec((N_HEADS, CHUNK, HEAD_DIM), lambda c: (0, 0, 0))
    return pl.pallas_call(
        body, name="retention_fwd",
        out_shape=(jax.ShapeDtypeStruct((l, 2 * GROUP), MXU_DTYPE), jax.ShapeDtypeStruct((l, GROUP), F32),
                   jax.ShapeDtypeStruct((n_chunks, N_HEADS, HEAD_DIM, HEAD_DIM), F32)),
        grid=(n_chunks,),
        in_specs=[pl.BlockSpec((CHUNK, 4 * GROUP), lambda c: (c, 0)),
                  pl.BlockSpec((CHUNK, HEAD_DIM), lambda c: (c, 0)),
                  pl.BlockSpec((CHUNK, HEAD_DIM), lambda c: (c, 0)),
                  head_tab, head_tab, head_tab,
                  pl.BlockSpec((N_HEADS, 8, HEAD_DIM), lambda c: (0, 0, 0)),
                  pl.BlockSpec((1, GROUP), lambda c: (0, 0))],
        out_specs=(pl.BlockSpec((CHUNK, GROUP), lambda c: (c, 0)),
                   pl.BlockSpec((CHUNK, GROUP), lambda c: (c, 0)),
                   pl.BlockSpec((None, N_HEADS, HEAD_DIM, HEAD_DIM), lambda c: (c, 0, 0, 0))),
        scratch_shapes=[pltpu.VMEM((N_HEADS, HEAD_DIM, HEAD_DIM), F32)],
        compiler_params=_params(("arbitrary",)),
    )(proj, cos2, sin2, decay, xi, zeta, g_rows, ret_gain)


def _retention_bwd(proj, o_pre, states, d_mix, ret_gain, consts):
    l = proj.shape[0]
    n_chunks = l // CHUNK
    cos2, sin2, decay, xi, zeta, g_rows = consts
    scale = HEAD_DIM ** -0.5
    rev = lambda c: n_chunks - 1 - c

    def body(p_ref, o_ref, st_ref, dm_ref, cos_ref, sin_ref, dec_ref, xi_ref, zeta_ref, gr_ref, gain_ref,
             dp_ref, dgain_ref, dstate):
        step = pl.program_id(0)

        @pl.when(step == 0)
        def _():
            dstate[...] = jnp.zeros_like(dstate)
            dgain_ref[...] = jnp.zeros_like(dgain_ref)

        cos_v, sin_v = cos_ref[...], sin_ref[...]
        valid = _row_valid(rev(step), CHUNK)
        for h in range(N_HEADS):
            cols = slice(h * HEAD_DIM, (h + 1) * HEAD_DIM)
            q = p_ref[:, h * HEAD_DIM:(h + 1) * HEAD_DIM]
            k = p_ref[:, GROUP + h * HEAD_DIM:GROUP + (h + 1) * HEAD_DIM]
            v = p_ref[:, 2 * GROUP + h * HEAD_DIM:2 * GROUP + (h + 1) * HEAD_DIM]
            g = p_ref[:, 3 * GROUP + h * HEAD_DIM:3 * GROUP + (h + 1) * HEAD_DIM]
            o = o_ref[:, cols]
            gain = gain_ref[:, cols]
            d_ret = jnp.where(valid, dm_ref[:, cols], 0.0)
            mu = jnp.mean(o, axis=-1, keepdims=True)
            oc = o - mu
            rstd = lax.rsqrt(jnp.mean(oc * oc, axis=-1, keepdims=True) + NORM_EPS)
            yn = oc * rstd
            sig = _sigmoid(g)
            gate = g * sig
            dgain_ref[:, cols] += jnp.sum(d_ret * gate * yn, axis=0, keepdims=True)
            d_g = d_ret * (yn * gain) * (sig * (1.0 + g * (1.0 - sig)))
            d_yn = d_ret * gate * gain
            d_o = rstd * (d_yn - jnp.mean(d_yn, axis=-1, keepdims=True)
                          - yn * jnp.mean(d_yn * yn, axis=-1, keepdims=True))
            rq = _rope(q, cos_v, sin_v)
            rk = _rope(k, cos_v, sin_v) * scale
            rqb, rkb, vb = rq.astype(MXU_DTYPE), rk.astype(MXU_DTYPE), v.astype(MXU_DTYPE)
            dob = d_o.astype(MXU_DTYPE)
            dec = dec_ref[h]
            xi_h, zeta_h = xi_ref[h], zeta_ref[h]
            st_b = st_ref[h].astype(MXU_DTYPE)
            dst = dstate[h]
            dst_b = dst.astype(MXU_DTYPE)
            s_b = (_dot_nt(rqb, rkb) * dec).astype(MXU_DTYPE)
            da_b = (_dot_nt(dob, vb) * dec).astype(MXU_DTYPE)
            doxi_b = (d_o * xi_h).astype(MXU_DTYPE)
            kz_b = (rk * zeta_h).astype(MXU_DTYPE)
            d_rq = _dot(da_b, rkb) + _dot_nt(doxi_b, st_b)
            d_rk = _dot_tn(da_b, rqb) + _dot_nt(vb, dst_b) * zeta_h
            d_v = _dot_tn(s_b, dob) + _dot(kz_b, dst_b)
            dstate[h] = gr_ref[h, 0:1, :] * dst + _dot_tn(rqb, doxi_b)
            d_q = _rope_bwd(d_rq, cos_v, sin_v)
            d_k = _rope_bwd(d_rk * scale, cos_v, sin_v)
            dp_ref[:, h * HEAD_DIM:(h + 1) * HEAD_DIM] = d_q.astype(dp_ref.dtype)
            dp_ref[:, GROUP + h * HEAD_DIM:GROUP + (h + 1) * HEAD_DIM] = d_k.astype(dp_ref.dtype)
            dp_ref[:, 2 * GROUP + h * HEAD_DIM:2 * GROUP + (h + 1) * HEAD_DIM] = d_v.astype(dp_ref.dtype)
            dp_ref[:, 3 * GROUP + h * HEAD_DIM:3 * GROUP + (h + 1) * HEAD_DIM] = d_g.astype(dp_ref.dtype)

    head_tab = pl.BlockSpec((N_HEADS, CHUNK, HEAD_DIM), lambda c: (0, 0, 0))
    return pl.pallas_call(
        body, name="retention_bwd",
        out_shape=(jax.ShapeDtypeStruct((l, 4 * GROUP), MXU_DTYPE), jax.ShapeDtypeStruct((1, GROUP), F32)),
        grid=(n_chunks,),
        in_specs=[pl.BlockSpec((CHUNK, 4 * GROUP), lambda c: (rev(c), 0)),
                  pl.BlockSpec((CHUNK, GROUP), lambda c: (rev(c), 0)),
                  pl.BlockSpec((None, N_HEADS, HEAD_DIM, HEAD_DIM), lambda c: (rev(c), 0, 0, 0)),
                  pl.BlockSpec((CHUNK, GROUP), lambda c: (rev(c), 0)),
                  pl.BlockSpec((CHUNK, HEAD_DIM), lambda c: (rev(c), 0)),
                  pl.BlockSpec((CHUNK, HEAD_DIM), lambda c: (rev(c), 0)),
                  head_tab, head_tab, head_tab,
                  pl.BlockSpec((N_HEADS, 8, HEAD_DIM), lambda c: (0, 0, 0)),
                  pl.BlockSpec((1, GROUP), lambda c: (0, 0))],
        out_specs=(pl.BlockSpec((CHUNK, 4 * GROUP), lambda c: (rev(c), 0)),
                   pl.BlockSpec((1, GROUP), lambda c: (0, 0))),
        scratch_shapes=[pltpu.VMEM((N_HEADS, HEAD_DIM, HEAD_DIM), F32)],
        compiler_params=_params(("arbitrary",)),
    )(proj, o_pre, states, d_mix, cos2, sin2, decay, xi, zeta, g_rows, ret_gain)


FF_TILE = (7 * GROUP) // 128


def _log_forget(ff, bias_row, valid):
    x = ff + bias_row
    e = jnp.exp(-jnp.abs(x))
    lf = jnp.minimum(x, 0.0) - jnp.log(1.0 + e)
    head_lane = lax.broadcasted_iota(jnp.int32, x.shape, 1) < N_HEADS
    keep = lambda t: jnp.where(head_lane, jnp.where(valid, t, 0.0), 0.0)
    return keep(lf), keep(jnp.where(x >= 0, e, 1.0) / (1.0 + e))


def _fox_prep(proj, bias_row):
    l = proj.shape[0]
    n_blocks = l // CHUNK

    def body(ff_ref, b_ref, bc_ref, rows_ref, cum):
        r = lax.broadcasted_iota(jnp.int32, (CHUNK, CHUNK), 0)
        cidx = lax.broadcasted_iota(jnp.int32, (CHUNK, CHUNK), 1)
        tri = jnp.where(r >= cidx, 1.0, 0.0).astype(F32)
        carry = jnp.zeros((1, 128), F32)
        for blk in range(n_blocks):
            rows = slice(blk * CHUNK, (blk + 1) * CHUNK)
            valid = _row_valid(blk, CHUNK)
            lf, _ = _log_forget(ff_ref[rows, :], b_ref[...], valid)
            local = jnp.dot(tri, lf, precision=lax.Precision.HIGHEST, preferred_element_type=F32) + carry
            carry = local[CHUNK - 1:CHUNK, :]
            masked = jnp.where(valid, local, -NEG_BIG)
            cum[rows, :] = masked
            t = masked.T
            for h in range(N_HEADS):
                rows_ref[h, :, rows] = t[h:h + 1, :]
        full = cum[...]
        for h in range(N_HEADS):
            bc_ref[h] = jnp.broadcast_to(full[:, h:h + 1], (l, 128))

    return pl.pallas_call(
        body, name="fox_prep",
        out_shape=(jax.ShapeDtypeStruct((N_HEADS, l, 128), F32), jax.ShapeDtypeStruct((N_HEADS, 1, l), F32)),
        grid=(1,),
        in_specs=[pl.BlockSpec((l, 128), lambda i: (0, FF_TILE)), pl.BlockSpec((1, 128), lambda i: (0, 0))],
        out_specs=(pl.BlockSpec((N_HEADS, l, 128), lambda i: (0, 0, 0)),
                   pl.BlockSpec((N_HEADS, 1, l), lambda i: (0, 0, 0))),
        scratch_shapes=[pltpu.VMEM((l, 128), F32)],
        compiler_params=_params(("arbitrary",)),
    )(proj, bias_row)


ATTN_BLOCK = 2 * CHUNK


def _attn_blocks(l):
    assert (l - CHUNK) % ATTN_BLOCK == 0
    return [(0, CHUNK)] + [(s, ATTN_BLOCK) for s in range(CHUNK, l, ATTN_BLOCK)]


def _rows_valid(start, size):
    return start + lax.broadcasted_iota(jnp.int32, (size, 1), 0) >= PAD_ROWS


def _fox_fwd(proj, cum_bc, cum_rows, mix):
    l = proj.shape[0]
    blocks = _attn_blocks(l)
    scale = HEAD_DIM ** -0.5
    qt, kt, vt = 4 * N_HEADS, 5 * N_HEADS, 6 * N_HEADS

    def body(q_ref, k_ref, v_ref, cbc_ref, crow_ref, mix_in, o_ref, lse_ref, qb_s, kb_s, vb_s):
        qb_s[...] = q_ref[...].astype(MXU_DTYPE)
        kb_s[...] = k_ref[...].astype(MXU_DTYPE)
        vb_s[...] = v_ref[...].astype(MXU_DTYPE)
        for p, (qs, qn) in enumerate(blocks):
            qb = qb_s[qs:qs + qn, :]
            cq = cbc_ref[qs:qs + qn, :]
            m = jnp.full((qn, 1), NEG_BIG, F32)
            lsum = jnp.zeros((qn, 1), F32)
            acc = jnp.zeros((qn, HEAD_DIM), F32)
            for j in range(p + 1):
                ks, kn = blocks[j]
                bias = jnp.tile(cq, (1, kn // CHUNK)) - crow_ref[:, ks:ks + kn]
                s = _dot_nt(qb, kb_s[ks:ks + kn, :]) * scale + bias
                if j == p:
                    q_pos = qs + lax.broadcasted_iota(jnp.int32, (qn, kn), 0)
                    k_pos = ks + lax.broadcasted_iota(jnp.int32, (qn, kn), 1)
                    s = jnp.where(k_pos <= q_pos, s, NEG_BIG)
                m_new = jnp.maximum(m, jnp.max(s, axis=-1, keepdims=True))
                alpha = jnp.exp(m - m_new)
                pr = jnp.exp(s - m_new)
                lsum = lsum * alpha + jnp.sum(pr, axis=-1, keepdims=True)
                acc = acc * alpha + _dot(pr.astype(MXU_DTYPE), vb_s[ks:ks + kn, :])
                m = m_new
            o = jnp.where(_rows_valid(qs, qn), acc * (1.0 / lsum), 0.0)
            o_ref[qs:qs + qn, :] = o.astype(o_ref.dtype)
            lse = m + jnp.log(lsum)
            lse_ref[:, qs:qs + qn] = jnp.broadcast_to(lse, (qn, CHUNK)).T[0:1, :]

    head_col = lambda t: pl.BlockSpec((l, HEAD_DIM), lambda h: (0, t + h))
    return pl.pallas_call(
        body, name="fox_fwd",
        out_shape=(jax.ShapeDtypeStruct(mix.shape, mix.dtype), jax.ShapeDtypeStruct((N_HEADS, 1, l), F32)),
        grid=(N_HEADS,),
        in_specs=[head_col(qt), head_col(kt), head_col(vt),
                  pl.BlockSpec((None, l, 128), lambda h: (h, 0, 0)),
                  pl.BlockSpec((None, 1, l), lambda h: (h, 0, 0)),
                  ANY],
        out_specs=(head_col(N_HEADS), pl.BlockSpec((None, 1, l), lambda h: (h, 0, 0))),
        input_output_aliases={5: 0},
        scratch_shapes=[pltpu.VMEM((l, HEAD_DIM), MXU_DTYPE)] * 3,
        compiler_params=_params(("parallel",)),
    )(proj, proj, proj, cum_bc, cum_rows, mix)


def _fox_bwd(proj, cum_bc, cum_rows, d_mix, lse_rows):
    l = proj.shape[0]
    blocks = _attn_blocks(l)
    scale = HEAD_DIM ** -0.5
    qt, kt, vt = 4 * N_HEADS, 5 * N_HEADS, 6 * N_HEADS

    def body(q_ref, k_ref, v_ref, do_ref, cbc_ref, crow_ref, lse_ref,
             dq_ref, dk_ref, dv_ref, ds_ref, dk_acc, dv_acc, qb_s, kb_s, vb_s, dob_s):
        qb_s[...] = q_ref[...].astype(MXU_DTYPE)
        kb_s[...] = k_ref[...].astype(MXU_DTYPE)
        vb_s[...] = v_ref[...].astype(MXU_DTYPE)
        dob_s[...] = jnp.where(_rows_valid(0, l), do_ref[...], 0.0).astype(MXU_DTYPE)
        dk_acc[...] = jnp.zeros_like(dk_acc)
        dv_acc[...] = jnp.zeros_like(dv_acc)
        ds_ref[...] = jnp.zeros_like(ds_ref)
        shift_row = crow_ref[...] - lse_ref[...]

        for p, (qs, qn) in enumerate(blocks):
            qb, dob = qb_s[qs:qs + qn, :], dob_s[qs:qs + qn, :]
            shift = shift_row[:, qs:qs + qn]

            def probs(j):
                ks, kn = blocks[j]
                ck = jnp.tile(cbc_ref[ks:ks + kn, :], (1, qn // CHUNK))
                s_t = _dot_nt(kb_s[ks:ks + kn, :], qb) * scale + (shift - ck)
                if j == p:
                    k_pos = ks + lax.broadcasted_iota(jnp.int32, (kn, qn), 0)
                    q_pos = qs + lax.broadcasted_iota(jnp.int32, (kn, qn), 1)
                    s_t = jnp.where(k_pos <= q_pos, s_t, NEG_BIG)
                return jnp.exp(s_t), _dot_nt(vb_s[ks:ks + kn, :], dob)

            delta = jnp.zeros((1, qn), F32)
            for j in range(p + 1):
                p_t, dp_t = probs(j)
                delta = delta + jnp.sum(p_t * dp_t, axis=0, keepdims=True)
            dq = jnp.zeros((qn, HEAD_DIM), F32)
            for j in range(p + 1):
                ks, kn = blocks[j]
                rows = slice(ks, ks + kn)
                p_t, dp_t = probs(j)
                ds_t = p_t * (dp_t - delta)
                ds_b = ds_t.astype(MXU_DTYPE)
                dv_acc[rows, :] += _dot(p_t.astype(MXU_DTYPE), dob)
                dk_acc[rows, :] += _dot(ds_b, qb) * scale
                ds_ref[rows, :] += sum(ds_t[:, c:c + CHUNK] for c in range(0, qn, CHUNK))
                dq = dq + _dot_tn(ds_b, kb_s[rows, :])
            dq_ref[qs:qs + qn, :] = (dq * scale).astype(dq_ref.dtype)

        dk_ref[...] = dk_acc[...].astype(dk_ref.dtype)
        dv_ref[...] = dv_acc[...].astype(dv_ref.dtype)

    col = jax.ShapeDtypeStruct((l, GROUP), MXU_DTYPE)
    head_col = lambda t: pl.BlockSpec((l, HEAD_DIM), lambda h: (0, t + h))
    return pl.pallas_call(
        body, name="fox_bwd",
        out_shape=(col, col, col, jax.ShapeDtypeStruct((N_HEADS, l, 128), F32)),
        grid=(N_HEADS,),
        in_specs=[head_col(qt), head_col(kt), head_col(vt), head_col(N_HEADS),
                  pl.BlockSpec((None, l, 128), lambda h: (h, 0, 0)),
                  pl.BlockSpec((None, 1, l), lambda h: (h, 0, 0)),
                  pl.BlockSpec((None, 1, l), lambda h: (h, 0, 0))],
        out_specs=(head_col(0), head_col(0), head_col(0), pl.BlockSpec((None, l, 128), lambda h: (h, 0, 0))),
        scratch_shapes=[pltpu.VMEM((l, HEAD_DIM), F32)] * 2 + [pltpu.VMEM((l, HEAD_DIM), MXU_DTYPE)] * 4,
        compiler_params=_params(("parallel",)),
    )(proj, proj, proj, d_mix, cum_bc, cum_rows, lse_rows)


def _fox_gate_bwd(ds_sum, proj, bias_row):
    l = proj.shape[0]
    n_blocks = l // CHUNK

    def body(ds_ref, ff_ref, b_ref, dff_ref, db_ref):
        r = lax.broadcasted_iota(jnp.int32, (CHUNK, CHUNK), 0)
        cidx = lax.broadcasted_iota(jnp.int32, (CHUNK, CHUNK), 1)
        upper = jnp.where(cidx >= r, 1.0, 0.0).astype(F32)
        carry = jnp.zeros((1, 128), F32)
        db = jnp.zeros((1, 128), F32)
        for blk in reversed(range(n_blocks)):
            rows = slice(blk * CHUNK, (blk + 1) * CHUNK)
            key_sum = jnp.zeros((CHUNK, 128), F32)
            for h in range(N_HEADS):
                select = jnp.where(cidx == h, 1.0, 0.0).astype(F32)
                key_sum = key_sum + jnp.dot(ds_ref[h, rows, :], select, precision=lax.Precision.HIGHEST,
                                            preferred_element_type=F32)
            suffix = jnp.dot(upper, key_sum, precision=lax.Precision.HIGHEST, preferred_element_type=F32) + carry
            carry = suffix[0:1, :]
            _, dsig = _log_forget(ff_ref[rows, :], b_ref[...], _row_valid(blk, CHUNK))
            dff = -suffix * dsig
            dff_ref[rows, :] = dff.astype(dff_ref.dtype)
            db = db + jnp.sum(dff, axis=0, keepdims=True)
        db_ref[...] = db

    return pl.pallas_call(
        body, name="fox_gate_bwd",
        out_shape=(jax.ShapeDtypeStruct((l, 128), MXU_DTYPE), jax.ShapeDtypeStruct((1, 128), F32)),
        grid=(1,),
        in_specs=[pl.BlockSpec((N_HEADS, l, 128), lambda i: (0, 0, 0)),
                  pl.BlockSpec((l, 128), lambda i: (0, FF_TILE)),
                  pl.BlockSpec((1, 128), lambda i: (0, 0))],
        out_specs=(pl.BlockSpec((l, 128), lambda i: (0, 0)), pl.BlockSpec((1, 128), lambda i: (0, 0))),
        compiler_params=_params(("arbitrary",)),
    )(ds_sum, proj, bias_row)


def _conv(u, w, b):
    return b + w[0:1, :] * pltpu.roll(u, 2, 0) + w[1:2, :] * pltpu.roll(u, 1, 0) + w[2:3, :] * u


def _conv_act_fwd(u, conv_w, conv_b, d_ff):
    l = u.shape[0]
    tc = _divisor_tile(d_ff, 256, 128)
    nt = d_ff // tc

    def body(ug_ref, uv_ref, wg_ref, wv_ref, bg_ref, bv_ref, a_ref):
        yg = _conv(ug_ref[...], wg_ref[...], bg_ref[...])
        yv = _conv(uv_ref[...], wv_ref[...], bv_ref[...])
        act = yg * _sigmoid(yg) * yv
        a_ref[...] = jnp.where(_row_valid(0, l), act, 0.0).astype(a_ref.dtype)

    return pl.pallas_call(
        body, name="conv_act_fwd",
        out_shape=jax.ShapeDtypeStruct((l, d_ff), MXU_DTYPE),
        grid=(nt,),
        in_specs=[pl.BlockSpec((l, tc), lambda j: (0, j)), pl.BlockSpec((l, tc), lambda j: (0, j + nt)),
                  pl.BlockSpec((8, tc), lambda j: (0, j)), pl.BlockSpec((8, tc), lambda j: (0, j + nt)),
                  pl.BlockSpec((1, tc), lambda j: (0, j)), pl.BlockSpec((1, tc), lambda j: (0, j + nt))],
        out_specs=pl.BlockSpec((l, tc), lambda j: (0, j)),
        compiler_params=_params(("parallel",)),
    )(u, u, conv_w, conv_w, conv_b, conv_b)


def _conv_act_bwd(u, conv_w, conv_b, d_act, d_ff):
    l = u.shape[0]
    tc = _divisor_tile(d_ff, 256, 128)
    nt = d_ff // tc

    def body(ug_ref, uv_ref, wg_ref, wv_ref, bg_ref, bv_ref, da_ref, du_ref, dwb_ref):
        valid = _row_valid(0, l)
        ug, uv = ug_ref[...], uv_ref[...]
        wg, wv = wg_ref[...], wv_ref[...]
        yg = _conv(ug, wg, bg_ref[...])
        yv = _conv(uv, wv, bv_ref[...])
        sig = _sigmoid(yg)
        da = jnp.where(valid, da_ref[...], 0.0)
        d_yv = da * (yg * sig)
        d_yg = da * yv * (sig * (1.0 + yg * (1.0 - sig)))
        for idx, (dy, uu, w) in enumerate(((d_yg, ug, wg), (d_yv, uv, wv))):
            du = w[2:3, :] * dy + w[1:2, :] * pltpu.roll(dy, l - 1, 0) + w[0:1, :] * pltpu.roll(dy, l - 2, 0)
            du_ref[idx] = jnp.where(valid, du, 0.0).astype(du_ref.dtype)
            dwb_ref[idx, 0:1, :] = jnp.sum(dy * pltpu.roll(uu, 2, 0), axis=0, keepdims=True)
            dwb_ref[idx, 1:2, :] = jnp.sum(dy * pltpu.roll(uu, 1, 0), axis=0, keepdims=True)
            dwb_ref[idx, 2:3, :] = jnp.sum(dy * uu, axis=0, keepdims=True)
            dwb_ref[idx, 3:4, :] = jnp.sum(dy, axis=0, keepdims=True)
            dwb_ref[idx, 4:8, :] = jnp.zeros((4, tc), F32)

    return pl.pallas_call(
        body, name="conv_act_bwd",
        out_shape=(jax.ShapeDtypeStruct((2, l, d_ff), MXU_DTYPE), jax.ShapeDtypeStruct((2, 8, d_ff), F32)),
        grid=(nt,),
        in_specs=[pl.BlockSpec((l, tc), lambda j: (0, j)), pl.BlockSpec((l, tc), lambda j: (0, j + nt)),
                  pl.BlockSpec((8, tc), lambda j: (0, j)), pl.BlockSpec((8, tc), lambda j: (0, j + nt)),
                  pl.BlockSpec((1, tc), lambda j: (0, j)), pl.BlockSpec((1, tc), lambda j: (0, j + nt)),
                  pl.BlockSpec((l, tc), lambda j: (0, j))],
        out_specs=(pl.BlockSpec((2, l, tc), lambda j: (0, 0, j)), pl.BlockSpec((2, 8, tc), lambda j: (0, 0, j))),
        compiler_params=_params(("parallel",)),
    )(u, u, conv_w, conv_w, conv_b, conv_b, d_act)


def _adamw(w, g, m, v, name):
    shape = w.shape
    if w.ndim == 1:
        as2d = (1, shape[0])
    else:
        as2d = (int(np.prod(shape[:-1])), shape[-1])
    r, c = as2d
    tr = _divisor_tile(r, 256, 8)
    spec = pl.BlockSpec((tr, c), lambda i: (i, 0))

    def body(w_ref, g_ref, m_ref, v_ref, d_ref, nm_ref, nv_ref):
        d_ref[...], nm_ref[...], nv_ref[...] = _adamw_math(w_ref[...], g_ref[...], m_ref[...], v_ref[...])

    sds = jax.ShapeDtypeStruct(as2d, F32)
    outs = pl.pallas_call(
        body, name=name, out_shape=(sds, sds, sds), grid=(r // tr,),
        in_specs=[spec] * 4, out_specs=(spec,) * 3,
        compiler_params=_params(("parallel",)),
    )(w.reshape(as2d), g.reshape(as2d), m.reshape(as2d), v.reshape(as2d))
    return tuple(o.reshape(shape) for o in outs)


def _pad_rows(a, rows):
    return jnp.pad(a, ((0, rows - a.shape[0]), (0, 0)))


def kernel(x, meta_tokens, norm1_gain, w_in, b_forget, ret_norm_gain, w_out, norm2_gain, w_up, conv_w, conv_b, w_down, final_norm_gain, loss_target, m_meta_tokens, m_norm1_gain, m_w_in, m_b_forget, m_ret_norm_gain, m_w_out, m_norm2_gain, m_w_up, m_conv_w, m_conv_b, m_w_down, m_final_norm_gain, v_meta_tokens, v_norm1_gain, v_w_in, v_b_forget, v_ret_norm_gain, v_w_out, v_norm2_gain, v_w_up, v_conv_w, v_conv_b, v_w_down, v_final_norm_gain):
    seq, d = x.shape[1], x.shape[2]
    l = CHUNK + seq
    d_ff = w_down.shape[1] * N_DEV
    up_shard = w_up.shape[2]
    assert 4 * up_shard == d_ff and w_in.shape[2] == WIN_SHARD and d == 2 * GROUP
    dev = _device_index()
    mx, my, mc = _my_position()
    core = jnp.reshape(mc, (1,)).astype(jnp.int32)
    chip = jnp.reshape(2 * mx + my, (1,)).astype(jnp.int32)
    dev1 = jnp.reshape(dev, (1,)).astype(jnp.int32)

    small = jnp.concatenate([meta_tokens.reshape(-1, 128), conv_w[0].reshape(-1, 128)], axis=0)
    n_meta_rows = N_META * (d // N_DEV) // 128
    small_rows = small.shape[0]
    small_all = _all_gather(_pad_rows(small, -(-small_rows // 8) * 8), "gather_small")
    meta_full = jnp.transpose(small_all[:, :n_meta_rows].reshape(N_DEV, N_META, d // N_DEV), (1, 0, 2)).reshape(N_META, d)
    conv_w_full = _pad_rows(jnp.transpose(small_all[:, n_meta_rows:small_rows].reshape(N_DEV, 3, up_shard),
                                          (1, 0, 2)).reshape(3, 2 * d_ff), 8)
    to_rows = lambda t: jnp.pad(jnp.transpose(t[0]), ((0, WIN_ROWS - WIN_SHARD), (0, 0)))
    from_rows = lambda t: jnp.transpose(t[:WIN_SHARD])[None]
    w_in_rows = to_rows(w_in)
    out_rows = d // N_DEV
    mixer_rows = -(-(WIN_ROWS + out_rows) // 304) * 304
    mixer_shard = jnp.concatenate([w_in_rows.astype(WIRE_DTYPE), w_out[0].astype(WIRE_DTYPE),
                                   jnp.zeros((mixer_rows - WIN_ROWS - out_rows, d), WIRE_DTYPE)], axis=0)

    h0 = jnp.concatenate([jnp.zeros((PAD_ROWS, d), F32), meta_full, x[0]], axis=0)
    consts = _retention_consts(l)
    bias_row = jnp.pad(b_forget, ((0, 0), (0, 128 - N_HEADS)))
    a = _rmsnorm_fwd(h0, norm1_gain, "rmsnorm1")
    mixer_blocks = _gather_ring(mixer_shard, dev1, a, "gather_w_in")
    start_up = _gather_start(w_up[0], dev1, mixer_blocks, "gather_w_up_start")
    w_in_full = _assemble_w_in(mixer_blocks).astype(MXU_DTYPE)
    proj = _mm_nt(a, w_in_full, F32, "mm_proj", after=start_up[4])
    ret_mix, ret_pre, ret_states = _retention_fwd(proj, ret_norm_gain, consts)
    cum_bc, cum_rows = _fox_prep(proj, bias_row)
    mix, lse_rows = _fox_fwd(proj, cum_bc, cum_rows, ret_mix)
    w_out_full = mixer_blocks[:, WIN_ROWS:WIN_ROWS + out_rows].reshape(d, d).astype(MXU_DTYPE)
    h1, cn = _rmsnorm_fwd(h0, norm2_gain, "resid_rmsnorm2", res=_mm_nn(mix, w_out_full, F32, "mm_out"))
    w_up_blocks = _gather_finish(start_up, cn, "gather_w_up").astype(MXU_DTYPE)
    start_down = _gather_start(w_down[0], dev1, w_up_blocks, "gather_w_down_start")
    u = _mm_up(cn, w_up_blocks, start_down[4])
    act = _conv_act_fwd(u, conv_w_full, conv_b + start_down[4][0, 0], d_ff)
    w_down_full = _gather_finish(start_down, act, "gather_w_down").reshape(d_ff, d).astype(MXU_DTYPE)
    mlp_out = _mm_nn(act, w_down_full, F32, "mm_down", tm_cap=544, tk_cap=d_ff)
    d_h2, d_h2_b, dg_final, loss_part = _loss_head(h1, mlp_out, final_norm_gain.reshape(1, d), loss_target[0])

    gw_down = _mm_tn(act, d_h2_b, WIRE_DTYPE, "mm_gw_down", tm_cap=1408, tn_cap=1024)
    d2d_down = _reduce_scatter_d2d_start(gw_down.reshape(N_DEV, d_ff // N_DEV, d), d_h2, "rs_w_down")
    d_act = _mm_nt(d_h2_b, w_down_full, F32, "mm_d_act", after=d2d_down[4])
    rs_down = _reduce_scatter_ici_start(d2d_down, d_act, core, "rs_w_down")
    d_u, d_conv = _conv_act_bwd(u, conv_w_full, conv_b + rs_down[4][0, 0], d_act, d_ff)
    tm = _divisor_tile(l, 1088, 16)
    gw_up = _mm_gw_up(cn, d_u)
    d2d_up = _reduce_scatter_d2d_start(gw_up, d_act, "rs_w_up")
    d_cn = _mm_d_cn(d_u, w_up_blocks, d2d_up[4])
    rs_up = _reduce_scatter_ici_start(d2d_up, d_cn, core, "rs_w_up")
    d_h1, d_h1_b, dg_norm2 = _rmsnorm_bwd(d_h2, d_cn, h1, norm2_gain + rs_up[4][0, 0], "rmsnorm2_bwd", True)

    gw_out = _mm_tn(mix, d_h1_b, WIRE_DTYPE, "mm_gw_out")
    d2d_out = _reduce_scatter_d2d_start(gw_out.reshape(N_DEV, d // N_DEV, d), d_cn, "rs_w_out")
    d_mix = _mm_nt(d_h1_b, w_out_full, F32, "mm_d_mix", after=d2d_out[4])
    d_fq, d_fk, d_fv, ds_sum = _fox_bwd(proj, cum_bc, cum_rows, d_mix, lse_rows)
    d_ff_tile, db_forget_row = _fox_gate_bwd(ds_sum, proj, bias_row)
    d_ret, dg_ret = _retention_bwd(proj, ret_pre, ret_states, d_mix, ret_norm_gain, consts)
    rs_out = _reduce_scatter_ici_start(d2d_out, d_ret, core, "rs_w_out")
    d_proj = jnp.concatenate(
        [d_ret, d_fq, d_fk, d_fv, d_ff_tile, jnp.zeros((l, WIN_N - 7 * GROUP - 128), MXU_DTYPE)], axis=1)
    gw_in = _mm_tn(d_proj, a, WIRE_DTYPE, "mm_gw_in", tm_cap=1536, after=rs_out[4])
    rs_in = _reduce_scatter_start(_extract_w_in_windows(gw_in), core, "rs_w_in")
    d_a = _mm_nn(d_proj, w_in_full, F32, "mm_d_a", tm_cap=544, tn_cap=256, tk_cap=WIN_N, after=rs_in[4])
    d_front, d_tokens, dg_norm1 = _rmsnorm_bwd(d_h1, d_a, h0, norm1_gain + rs_in[4][0, 0], "rmsnorm1_bwd", False)
    grad_x = d_tokens[None]
    d_meta = d_front[PAD_ROWS:CHUNK]

    d_conv_w = jnp.concatenate([d_conv[0, 0:3], d_conv[1, 0:3]], axis=1)
    d_conv_b = jnp.concatenate([d_conv[0, 3:4], d_conv[1, 3:4]], axis=1)
    pieces = [loss_part[:, 0:1], dg_norm1, db_forget_row[:, 0:N_HEADS], dg_ret, dg_norm2, d_conv_b, dg_final,
              d_meta.reshape(1, -1), d_conv_w.reshape(1, -1)]
    sizes = [p.shape[1] for p in pieces]
    flat = jnp.concatenate(pieces, axis=1)
    padded = -(-flat.shape[1] // 1024) * 1024
    flat = jnp.pad(flat, ((0, 0), (0, padded - flat.shape[1]))).reshape(padded // 128, 128)
    small_ar = _small_all_reduce_start(flat, d_tokens, "all_reduce_small")

    lead = lambda outs: tuple(o[None] for o in outs)
    fin_down = lead(_reduce_scatter_finish(rs_down, small_ar[4], chip, w_down[0], m_w_down[0], v_w_down[0], "rs_w_down"))
    fin_up = lead(_reduce_scatter_finish(rs_up, fin_down[3], chip, w_up[0], m_w_up[0], v_w_up[0], "rs_w_up"))
    fin_out = lead(_reduce_scatter_finish(rs_out, fin_up[3], chip, w_out[0], m_w_out[0], v_w_out[0], "rs_w_out"))
    fin_in = tuple(from_rows(o) for o in _reduce_scatter_finish(
        rs_in, fin_out[3], chip, w_in_rows, to_rows(m_w_in), to_rows(v_w_in), "rs_w_in"))
    g_w_down, g_w_up, g_w_out, g_w_in = fin_down[0], fin_up[0], fin_out[0], fin_in[0]
    early = [fin_down[1:], fin_up[1:], fin_out[1:], fin_in[1:]]
    total = _small_all_reduce_finish(small_ar, fin_in[3], dev1, "all_reduce_small").reshape(1, padded)
    offs = np.concatenate([[0], np.cumsum(sizes)])
    take = lambda k: total[:, int(offs[k]):int(offs[k + 1])]
    loss = take(0).reshape(())
    g_norm1, g_bf, g_ret_gain, g_norm2 = take(1), take(2), take(3), take(4)
    g_conv_b, g_final = take(5), take(6).reshape(d)
    g_meta = lax.dynamic_slice(take(7).reshape(N_META, d), (jnp.int32(0), (dev * (d // N_DEV)).astype(jnp.int32)),
                               (N_META, d // N_DEV))
    g_conv_w = lax.dynamic_slice(take(8).reshape(3, 2 * d_ff), (jnp.int32(0), (dev * up_shard).astype(jnp.int32)),
                                 (3, up_shard))[None]

    weights = [meta_tokens, norm1_gain, w_in, b_forget, ret_norm_gain, w_out, norm2_gain, w_up, conv_w, conv_b,
               w_down, final_norm_gain]
    grads = [g_meta, g_norm1, g_w_in, g_bf, g_ret_gain, g_w_out, g_norm2, g_w_up, g_conv_w, g_conv_b, g_w_down,
             g_final]
    done = {"w_down": early[0], "w_up": early[1], "w_out": early[2], "w_in": early[3]}
    ms = [m_meta_tokens, m_norm1_gain, m_w_in, m_b_forget, m_ret_norm_gain, m_w_out, m_norm2_gain, m_w_up, m_conv_w,
          m_conv_b, m_w_down, m_final_norm_gain]
    vs = [v_meta_tokens, v_norm1_gain, v_w_in, v_b_forget, v_ret_norm_gain, v_w_out, v_norm2_gain, v_w_up, v_conv_w,
          v_conv_b, v_w_down, v_final_norm_gain]
    names = ["meta", "norm1", "w_in", "b_forget", "ret_gain", "w_out", "norm2", "w_up", "conv_w", "conv_b", "w_down",
             "final_gain"]
    deltas, new_ms, new_vs = [], [], []
    for w, g, m, v, n in zip(weights, grads, ms, vs, names):
        dl, nm, nv = done[n] if n in done else _adamw(w, g, m, v, "adamw_" + n)
        deltas.append(dl)
        new_ms.append(nm)
        new_vs.append(nv)
    return (loss, grad_x, *grads, *deltas, *new_ms, *new_vs)
```

```python
import functools

import numpy as np
import jax
import jax.numpy as jnp
from jax import lax
from jax.experimental import pallas as pl
from jax.experimental.pallas import tpu as pltpu

F32 = jnp.float32
MXU_DTYPE = jnp.bfloat16
WIRE_DTYPE = jnp.bfloat16

N_DEV = 8
N_META = 16
CHUNK = 128
PAD_ROWS = CHUNK - N_META
N_HEADS = 8
HEAD_DIM = 128
GROUP = N_HEADS * HEAD_DIM
IN_DIM = 7 * GROUP + N_HEADS
WIN_SHARD = IN_DIM // N_DEV
WIN_ROWS = 912
WIN_BLOCK = 1024
WIN_STRIDE = 896
WIN_N = 7680
ROPE_BASE = 10000.0
NORM_EPS = 1e-6
NEG_BIG = -1e30
ADAM_LR, ADAM_B1, ADAM_B2, ADAM_EPS, ADAM_WD, ADAM_STEP = 0.001, 0.9, 0.999, 1e-08, 0.01, 10
VMEM_LIMIT = 52 * 1024 * 1024
MESH = pl.DeviceIdType.MESH
ANY = pl.BlockSpec(memory_space=pl.ANY)
VMEM_SPEC = pl.BlockSpec(memory_space=pltpu.VMEM)


def _params(sem=None):
    kw = {"vmem_limit_bytes": VMEM_LIMIT}
    if sem is not None:
        kw["dimension_semantics"] = sem
    return pltpu.CompilerParams(**kw)


def _divisor_tile(n, cap, unit):
    if n <= cap:
        return n
    best = None
    for t in range(unit, cap + 1, unit):
        if n % t == 0:
            best = t
    assert best is not None, (n, cap, unit)
    return best


def _my_position():
    return lax.axis_index("x"), lax.axis_index("y"), lax.axis_index("c")


def _device_index():
    x, y, c = _my_position()
    return 4 * x + 2 * y + c


def _all_gather(shard, name):
    r, c = shard.shape

    def body(x_ref, out_ref, send_sems, recv_sems, local_sem):
        mx, my, mc = _my_position()
        me, sibling = (mx, my, mc), (mx, my, 1 - mc)
        chips = [(1 - mx, my), (mx, 1 - my), (1 - mx, 1 - my)]

        def slot(px, py, pc):
            return out_ref.at[4 * px + 2 * py + pc]

        def copy(k, block, to, src=None):
            return pltpu.make_async_remote_copy(
                src_ref=slot(*block) if src is None else src, dst_ref=slot(*block),
                send_sem=send_sems.at[k], recv_sem=recv_sems.at[k], device_id=to, device_id_type=MESH)

        mine = pltpu.make_async_copy(x_ref, slot(*me), local_sem)
        mine.start()
        first = [copy(0, me, sibling, src=x_ref)]
        first += [copy(1 + j, me, (*chip, mc), src=x_ref) for j, chip in enumerate(chips)]
        for cp in first:
            cp.start()
        passed = [copy(4 + j, (*chip, mc), sibling) for j, chip in enumerate(chips)]
        for j, chip in enumerate(chips):
            copy(1 + j, (*chip, mc), me).wait_recv()
            passed[j].start()
        copy(0, sibling, me).wait_recv()
        for j, chip in enumerate(chips):
            copy(4 + j, (*chip, 1 - mc), me).wait_recv()
        for cp in first + passed:
            cp.wait_send()
        mine.wait()

    return pl.pallas_call(
        body, name=name,
        out_shape=jax.ShapeDtypeStruct((N_DEV, r, c), shard.dtype),
        in_specs=[ANY], out_specs=ANY,
        scratch_shapes=[pltpu.SemaphoreType.DMA((7,)), pltpu.SemaphoreType.DMA((7,)), pltpu.SemaphoreType.DMA],
    )(shard)


HBM_SPEC = pl.BlockSpec(memory_space=pltpu.HBM)
SEM_SPEC = pl.BlockSpec(memory_space=pltpu.SEMAPHORE)
DATAFLOW_EFFECT = pltpu.SideEffectType.DATAFLOW_SIDE_EFFECTING


def _in_hbm(a):
    return pltpu.with_memory_space_constraint(a, pltpu.HBM)


def _split_start(src, land, make_copies, n_copies, after, name):
    if isinstance(land, tuple):
        land = lax.empty(land, src.dtype)
    land_shape = land.shape
    def body(src_ref, land_ref, after_ref, send_sems, recv_sems, src_thru, land_thru, token):
        for cp in make_copies(src_ref, land_ref, send_sems, recv_sems):
            cp.start()
        token[...] = jnp.zeros_like(token)

    return pl.pallas_call(
        body, name=name,
        out_shape=(pltpu.SemaphoreType.DMA((n_copies,)), pltpu.SemaphoreType.DMA((n_copies,)),
                   pltpu.HBM(src.shape, src.dtype), pltpu.HBM(land_shape, land.dtype),
                   jax.ShapeDtypeStruct((8, 128), F32)),
        in_specs=(HBM_SPEC, HBM_SPEC, ANY), out_specs=(SEM_SPEC, SEM_SPEC, HBM_SPEC, HBM_SPEC, VMEM_SPEC),
        input_output_aliases={0: 2, 1: 3},
        compiler_params=pltpu.CompilerParams(has_side_effects=DATAFLOW_EFFECT),
    )(_in_hbm(src), _in_hbm(land), after)


def _split_wait(started, after, make_copies, name):
    send_sems, recv_sems, src_thru, land_thru, _ = started

    def body(src_ref, land_ref, send_sems_ref, recv_sems_ref, after_ref, src_dead, land_out):
        for cp in make_copies(src_ref, land_ref, send_sems_ref, recv_sems_ref):
            cp.wait_send()
            cp.wait_recv()

    return pl.pallas_call(
        body, name=name,
        out_shape=(pltpu.HBM(src_thru.shape, src_thru.dtype), pltpu.HBM(land_thru.shape, land_thru.dtype)),
        in_specs=(HBM_SPEC, HBM_SPEC, SEM_SPEC, SEM_SPEC, ANY), out_specs=(HBM_SPEC, HBM_SPEC),
        input_output_aliases={0: 0, 1: 1},
        compiler_params=pltpu.CompilerParams(has_side_effects=DATAFLOW_EFFECT),
    )(src_thru, land_thru, send_sems, recv_sems, after)


def _gather_copies(x_ref, land_ref, send_sems, recv_sems):
    mx, my, mc = _my_position()
    me = 4 * mx + 2 * my + mc
    targets = [(mx, my, 1 - mc), (1 - mx, my, mc), (mx, 1 - my, mc), (1 - mx, 1 - my, mc)]
    return [pltpu.make_async_remote_copy(
        src_ref=land_ref.at[me], dst_ref=land_ref.at[me], send_sem=send_sems.at[k], recv_sem=recv_sems.at[k],
        device_id=t, device_id_type=MESH) for k, t in enumerate(targets)]


def _own_slot(shard, dev, name):
    r, c = shard.shape
    tr = _divisor_tile(r, 512, 16)

    def body(s_ref, x_ref, o_ref):
        o_ref[...] = x_ref[...].astype(o_ref.dtype)

    return pl.pallas_call(
        body, name=name,
        out_shape=jax.ShapeDtypeStruct((N_DEV, r, c), WIRE_DTYPE),
        grid_spec=pltpu.PrefetchScalarGridSpec(
            num_scalar_prefetch=1, grid=(r // tr,),
            in_specs=[pl.BlockSpec((tr, c), lambda i, s: (i, 0))],
            out_specs=pl.BlockSpec((None, tr, c), lambda i, s: (s[0], i, 0))),
        compiler_params=_params(("parallel",)),
    )(dev, shard)


def _gather_start(shard, dev, after, name):
    return _split_start(jnp.zeros((8, 128), F32), _own_slot(shard, dev, name + "_own"), _gather_copies, 4, after, name)


def _gather_ring(shard, dev, after, name):
    r, c = shard.shape
    half = r // 2
    assert half % 16 == 0

    def body(x_ref, after_ref, land_in, land_ref, send_sems, recv_sems):
        mx, my, mc = _my_position()
        sibling, x_nbr, y_nbr = (mx, my, 1 - mc), (1 - mx, my, mc), (mx, 1 - my, mc)
        first, second = pl.ds(0, half), pl.ds(half, half)

        def slot(px, py, pc):
            return land_ref.at[4 * px + 2 * py + pc]

        def copy(k, src, dst, to):
            return pltpu.make_async_remote_copy(src_ref=src, dst_ref=dst, send_sem=send_sems.at[k],
                                                recv_sem=recv_sems.at[k], device_id=to, device_id_type=MESH)

        def arrived(k, dst):
            copy(k, dst, dst, sibling).wait_recv()

        mine = slot(mx, my, mc)
        from_x, from_y, from_d = slot(1 - mx, my, mc), slot(mx, 1 - my, mc), slot(1 - mx, 1 - my, mc)
        sent = [copy(0, x_ref, mine, sibling), copy(1, x_ref, mine, x_nbr), copy(2, x_ref, mine, y_nbr)]
        for cp in sent:
            cp.start()

        def send(k, src, to):
            cp = copy(k, src, src, to)
            cp.start()
            sent.append(cp)

        arrived(1, from_x)
        send(3, from_x.at[first], y_nbr)
        send(5, from_x, sibling)
        arrived(2, from_y)
        send(4, from_y.at[second], x_nbr)
        send(6, from_y, sibling)
        arrived(3, from_d.at[first])
        send(7, from_d.at[first], sibling)
        arrived(4, from_d.at[second])
        send(8, from_d.at[second], sibling)
        arrived(0, slot(mx, my, 1 - mc))
        arrived(5, slot(1 - mx, my, 1 - mc))
        arrived(6, slot(mx, 1 - my, 1 - mc))
        arrived(7, slot(1 - mx, 1 - my, 1 - mc).at[first])
        arrived(8, slot(1 - mx, 1 - my, 1 - mc).at[second])
        for cp in sent:
            cp.wait_send()

    land = _own_slot(shard, dev, name + "_own")
    return pl.pallas_call(
        body, name=name,
        out_shape=jax.ShapeDtypeStruct(land.shape, land.dtype),
        in_specs=[ANY, ANY, ANY], out_specs=ANY,
        input_output_aliases={2: 0},
        scratch_shapes=[pltpu.SemaphoreType.DMA((9,)), pltpu.SemaphoreType.DMA((9,))],
    )(shard, after, land)


def _gather_finish(started, after, name):
    _, land = _split_wait(started, after, _gather_copies, name + "_wait")

    def body(land_in, land_ref, send_sems, recv_sems):
        mx, my, mc = _my_position()
        chips = [(1 - mx, my), (mx, 1 - my), (1 - mx, 1 - my)]
        copies = [pltpu.make_async_remote_copy(
            src_ref=land_ref.at[4 * cx + 2 * cy + mc], dst_ref=land_ref.at[4 * cx + 2 * cy + mc],
            send_sem=send_sems.at[j], recv_sem=recv_sems.at[j],
            device_id=(mx, my, 1 - mc), device_id_type=MESH) for j, (cx, cy) in enumerate(chips)]
        for cp in copies:
            cp.start()
        for j, (cx, cy) in enumerate(chips):
            copies[j].wait_send()
            pltpu.make_async_remote_copy(
                src_ref=land_ref.at[4 * cx + 2 * cy + 1 - mc], dst_ref=land_ref.at[4 * cx + 2 * cy + 1 - mc],
                send_sem=send_sems.at[j], recv_sem=recv_sems.at[j],
                device_id=(mx, my, 1 - mc), device_id_type=MESH).wait_recv()

    return pl.pallas_call(
        body, name=name + "_pass",
        out_shape=jax.ShapeDtypeStruct(land.shape, land.dtype),
        in_specs=[ANY], out_specs=ANY,
        input_output_aliases={0: 0},
        scratch_shapes=[pltpu.SemaphoreType.DMA((3,)), pltpu.SemaphoreType.DMA((3,))],
    )(land)


def _chip_copies(p_ref, land_ref, send_sems, recv_sems):
    mx, my, mc = _my_position()
    chips = [(1 - mx, my), (mx, 1 - my), (1 - mx, 1 - my)]
    return [pltpu.make_async_remote_copy(
        src_ref=p_ref.at[2 * cx + cy], dst_ref=land_ref.at[j], send_sem=send_sems.at[j], recv_sem=recv_sems.at[j],
        device_id=(cx, cy, mc), device_id_type=MESH) for j, (cx, cy) in enumerate(chips)]


def _reduce_scatter_start(g, core, name):
    pair = _pair_sum(g, _exchange_sibling(g, name + "_d2d"), core, name + "_pairsum")
    return _split_start(pair, (3,) + pair.shape[1:], _chip_copies, 3, g, name + "_ici_start")


def _sibling_copies(g_ref, land_ref, send_sems, recv_sems):
    mx, my, mc = _my_position()
    return [pltpu.make_async_remote_copy(
        src_ref=g_ref.at[2 * k + (1 - mc)], dst_ref=land_ref.at[k], send_sem=send_sems.at[k], recv_sem=recv_sems.at[k],
        device_id=(mx, my, 1 - mc), device_id_type=MESH) for k in range(4)]


def _reduce_scatter_d2d_start(g, after, name):
    return _split_start(g, (4,) + g.shape[1:], _sibling_copies, 4, after, name + "_d2d_start")


def _reduce_scatter_ici_start(d2d_started, after, core, name):
    g, from_sibling = _split_wait(d2d_started, after, _sibling_copies, name + "_d2d_wait")
    pair = _pair_sum(g, from_sibling, core, name + "_pairsum")
    return _split_start(pair, (3,) + pair.shape[1:], _chip_copies, 3, g, name + "_ici_start")


def _reduce_scatter_finish(started, after, chip, w, m, v, name):
    pair, from_chips = _split_wait(started, after, _chip_copies, name + "_ici_wait")
    return _final_sum_adamw(pair, from_chips, chip, w, m, v, name + "_sum_adamw")


def _exchange_sibling(g, name):
    _, r, c = g.shape

    def body(g_ref, out_ref, send_sems, recv_sems):
        mx, my, mc = _my_position()
        copies = [
            pltpu.make_async_remote_copy(
                src_ref=g_ref.at[2 * k + (1 - mc)], dst_ref=out_ref.at[k],
                send_sem=send_sems.at[k], recv_sem=recv_sems.at[k],
                device_id=(mx, my, 1 - mc), device_id_type=MESH)
            for k in range(4)]
        for cp in copies:
            cp.start()
        for cp in copies:
            cp.wait()

    return pl.pallas_call(
        body, name=name,
        out_shape=jax.ShapeDtypeStruct((4, r, c), g.dtype),
        in_specs=[ANY], out_specs=ANY,
        scratch_shapes=[pltpu.SemaphoreType.DMA((4,)), pltpu.SemaphoreType.DMA((4,))],
    )(g)


def _pair_sum(g, recv, core, name):
    _, r, c = g.shape
    tr = _divisor_tile(r, 512, 16)

    def body(s_ref, g_ref, r_ref, o_ref):
        o_ref[...] = (g_ref[...].astype(F32) + r_ref[...].astype(F32)).astype(o_ref.dtype)

    return pl.pallas_call(
        body, name=name,
        out_shape=jax.ShapeDtypeStruct((4, r, c), g.dtype),
        grid_spec=pltpu.PrefetchScalarGridSpec(
            num_scalar_prefetch=1, grid=(4, r // tr),
            in_specs=[pl.BlockSpec((None, tr, c), lambda k, i, s: (2 * k + s[0], i, 0)),
                      pl.BlockSpec((None, tr, c), lambda k, i, s: (k, i, 0))],
            out_specs=pl.BlockSpec((None, tr, c), lambda k, i, s: (k, i, 0))),
        compiler_params=_params(("parallel", "parallel")),
    )(core, g, recv)


def _adamw_math(w, g, m, v):
    nm = ADAM_B1 * m + (1.0 - ADAM_B1) * g
    nv = ADAM_B2 * v + (1.0 - ADAM_B2) * (g * g)
    m_hat = nm / (1.0 - ADAM_B1 ** ADAM_STEP)
    v_hat = nv / (1.0 - ADAM_B2 ** ADAM_STEP)
    return -ADAM_LR * (m_hat / (jnp.sqrt(v_hat) + ADAM_EPS) + ADAM_WD * w), nm, nv


def _final_sum_adamw(p, recv, chip, w, m, v, name):
    _, r, c = p.shape
    tr = _divisor_tile(r, 256, 16)
    tile = lambda: pl.BlockSpec((tr, c), lambda i, s: (i, 0))

    def body(s_ref, p_ref, r_ref, w_ref, m_ref, v_ref, g_ref, d_ref, nm_ref, nv_ref):
        g = p_ref[...].astype(F32)
        for j in range(3):
            g = g + r_ref[j].astype(F32)
        g_ref[...] = g
        d_ref[...], nm_ref[...], nv_ref[...] = _adamw_math(w_ref[...], g, m_ref[...], v_ref[...])

    sds = jax.ShapeDtypeStruct((r, c), F32)
    return pl.pallas_call(
        body, name=name,
        out_shape=(sds, sds, sds, sds),
        grid_spec=pltpu.PrefetchScalarGridSpec(
            num_scalar_prefetch=1, grid=(r // tr,),
            in_specs=[pl.BlockSpec((None, tr, c), lambda i, s: (s[0], i, 0)),
                      pl.BlockSpec((3, tr, c), lambda i, s: (0, i, 0)), tile(), tile(), tile()],
            out_specs=(tile(), tile(), tile(), tile())),
        compiler_params=_params(("parallel",)),
    )(chip, p, recv, w, m, v)


def _all_to_all_copies(v_ref, land_ref, send_sems, recv_sems):
    mx, my, mc = _my_position()
    me = 4 * mx + 2 * my + mc
    copies = []
    for rel in range(1, N_DEV):
        bx, by, bc = (rel >> 2) & 1, (rel >> 1) & 1, rel & 1
        target = (1 - mx if bx else mx, 1 - my if by else my, 1 - mc if bc else mc)
        copies.append(pltpu.make_async_remote_copy(
            src_ref=v_ref, dst_ref=land_ref.at[me], send_sem=send_sems.at[rel - 1], recv_sem=recv_sems.at[rel - 1],
            device_id=target, device_id_type=MESH))
    return copies


def _small_all_reduce_start(v, after, name):
    return _split_start(v, (N_DEV,) + v.shape, _all_to_all_copies, N_DEV - 1, after, name + "_start")


def _small_all_reduce_finish(started, after, dev, name):
    v, land = _split_wait(started, after, _all_to_all_copies, name + "_wait")
    rows = v.shape[0]

    def body(me_ref, v_ref, land_ref, o_ref):
        for j in range(N_DEV):
            @pl.when(me_ref[0] == j)
            def _():
                o_ref[...] = v_ref[...] if j == 0 else o_ref[...] + v_ref[...]

            @pl.when(me_ref[0] != j)
            def _():
                o_ref[...] = land_ref[j] if j == 0 else o_ref[...] + land_ref[j]

    return pl.pallas_call(
        body, name=name + "_sum",
        out_shape=jax.ShapeDtypeStruct((rows, 128), F32),
        grid_spec=pltpu.PrefetchScalarGridSpec(
            num_scalar_prefetch=1, grid=(1,),
            in_specs=[pl.BlockSpec((rows, 128), lambda i, s: (0, 0)),
                      pl.BlockSpec((N_DEV, rows, 128), lambda i, s: (0, 0, 0))],
            out_specs=pl.BlockSpec((rows, 128), lambda i, s: (0, 0))),
        compiler_params=_params(("arbitrary",)),
    )(dev, v, land)


def _assemble_w_in(blocks):
    rows, d = WIN_ROWS, blocks.shape[2]
    tc = _divisor_tile(d, 256, 128)
    n_tiles = WIN_N // 128
    last = (N_DEV * WIN_STRIDE) // 128

    def body(b_ref, o_ref):
        win = []
        for i in range(N_DEV):
            w = jnp.concatenate([b_ref[i].astype(F32), jnp.zeros((WIN_BLOCK - rows, tc), F32)], axis=0)
            win.append(pltpu.roll(w, i, 0) if i else w)
        for t in range(n_tiles):
            if t > last:
                o_ref[t * 128:(t + 1) * 128, :] = jnp.zeros((128, tc), o_ref.dtype)
                continue
            i = min(t // 7, N_DEV - 1)
            k = t - 7 * i
            val = win[i][k * 128:(k + 1) * 128, :]
            if k == 0 and i >= 1:
                val = val + win[i - 1][7 * 128:8 * 128, :]
            o_ref[t * 128:(t + 1) * 128, :] = val.astype(o_ref.dtype)

    return pl.pallas_call(
        body, name="assemble_w_in",
        out_shape=jax.ShapeDtypeStruct((WIN_N, d), blocks.dtype),
        grid=(d // tc,),
        in_specs=[pl.BlockSpec((N_DEV, rows, tc), lambda j: (0, 0, j))],
        out_specs=pl.BlockSpec((WIN_N, tc), lambda j: (0, j)),
        compiler_params=_params(("parallel",)),
    )(blocks)


def _extract_w_in_windows(g):
    _, d = g.shape
    tc = _divisor_tile(d, 256, 128)

    def body(g_ref, o_ref):
        for j in range(N_DEV):
            w = g_ref[WIN_STRIDE * j:WIN_STRIDE * j + WIN_BLOCK, :].astype(F32)
            w = pltpu.roll(w, WIN_BLOCK - j, 0) if j else w
            o_ref[j] = w[0:WIN_ROWS, :].astype(o_ref.dtype)

    return pl.pallas_call(
        body, name="extract_w_in_windows",
        out_shape=jax.ShapeDtypeStruct((N_DEV, WIN_ROWS, d), g.dtype),
        grid=(d // tc,),
        in_specs=[pl.BlockSpec((WIN_N, tc), lambda j: (0, j))],
        out_specs=pl.BlockSpec((N_DEV, WIN_ROWS, tc), lambda j: (0, 0, j)),
        compiler_params=_params(("parallel",)),
    )(g)


def _mm(a, b, *, a_spec, b_spec, o_spec, out_shape, grid, contract, nk, name, after=None):
    dn = (((contract[0],), (contract[1],)), ((), ()))
    tm, tn = o_spec.block_shape[-2:]
    behind = [] if after is None else [after]

    def body(a_ref, b_ref, *rest):
        o_ref, *scratch = rest[len(behind):]
        part = lax.dot_general(a_ref[...], b_ref[...], dn, preferred_element_type=F32)
        if nk == 1:
            o_ref[...] = part.astype(o_ref.dtype)
            return
        acc = scratch[0]
        k = pl.program_id(2)

        @pl.when(k == 0)
        def _():
            acc[...] = part

        @pl.when(k > 0)
        def _():
            acc[...] += part

        @pl.when(k == nk - 1)
        def _():
            o_ref[...] = acc[...].astype(o_ref.dtype)

    return pl.pallas_call(
        body, name=name, out_shape=out_shape, grid=grid,
        in_specs=[a_spec, b_spec] + [ANY] * len(behind), out_specs=o_spec,
        scratch_shapes=[] if nk == 1 else [pltpu.VMEM((tm, tn), F32)],
        compiler_params=_params(("parallel", "parallel", "arbitrary")),
    )(a, b, *behind)


def _mm_nn(a, b, out_dtype, name, tm_cap=1088, tn_cap=512, tk_cap=2048, after=None):
    m, k = a.shape
    _, n = b.shape
    tm, tn, tk = _divisor_tile(m, tm_cap, 16), _divisor_tile(n, tn_cap, 128), _divisor_tile(k, tk_cap, 128)
    return _mm(a, b,
               a_spec=pl.BlockSpec((tm, tk), lambda i, j, kk: (i, kk)),
               b_spec=pl.BlockSpec((tk, tn), lambda i, j, kk: (kk, j)),
               o_spec=pl.BlockSpec((tm, tn), lambda i, j, kk: (i, j)),
               out_shape=jax.ShapeDtypeStruct((m, n), out_dtype),
               grid=(m // tm, n // tn, k // tk), contract=(1, 0), nk=k // tk, name=name, after=after)


def _mm_nt(a, b, out_dtype, name, tm_cap=1088, tn_cap=512, tk_cap=2048, after=None):
    m, k = a.shape
    n, _ = b.shape
    tm, tn, tk = _divisor_tile(m, tm_cap, 16), _divisor_tile(n, tn_cap, 128), _divisor_tile(k, tk_cap, 128)
    return _mm(a, b,
               a_spec=pl.BlockSpec((tm, tk), lambda i, j, kk: (i, kk)),
               b_spec=pl.BlockSpec((tn, tk), lambda i, j, kk: (j, kk)),
               o_spec=pl.BlockSpec((tm, tn), lambda i, j, kk: (i, j)),
               out_shape=jax.ShapeDtypeStruct((m, n), out_dtype),
               grid=(m // tm, n // tn, k // tk), contract=(1, 1), nk=k // tk, name=name, after=after)


def _mm_tn(a, b, out_dtype, name, tm_cap=1024, tn_cap=512, after=None):
    l, m = a.shape
    _, n = b.shape
    tm, tn = _divisor_tile(m, tm_cap, 128), _divisor_tile(n, tn_cap, 128)
    return _mm(a, b,
               a_spec=pl.BlockSpec((l, tm), lambda i, j, kk: (0, i)),
               b_spec=pl.BlockSpec((l, tn), lambda i, j, kk: (0, j)),
               o_spec=pl.BlockSpec((tm, tn), lambda i, j, kk: (i, j)),
               out_shape=jax.ShapeDtypeStruct((m, n), out_dtype),
               grid=(m // tm, n // tn, 1), contract=(0, 0), nk=1, name=name, after=after)


def _pair_split(shard):
    left = shard % ATTN_BLOCK
    assert left in (0, CHUNK) and shard > left
    return shard - left, left


def _mm_up(cn, w_up_blocks, after):
    l, d = cn.shape
    n, _, shard = w_up_blocks.shape
    main, left = _pair_split(shard)
    tm = _divisor_tile(l, 544, 16)

    def body(a_ref, b_ref, after_ref, o_ref):
        a = a_ref[...]
        for s in range(2):
            o_ref[:, s * shard:s * shard + main] = _dot(a, b_ref[s, :, 0:main])
        if left:
            tail = _dot(a, jnp.concatenate([b_ref[0, :, main:], b_ref[1, :, main:]], axis=1))
            o_ref[:, main:shard] = tail[:, 0:left]
            o_ref[:, shard + main:2 * shard] = tail[:, left:]

    return pl.pallas_call(
        body, name="mm_up", out_shape=jax.ShapeDtypeStruct((l, n * shard), F32), grid=(l // tm, n // 2),
        in_specs=[pl.BlockSpec((tm, d), lambda i, j: (i, 0)),
                  pl.BlockSpec((2, d, shard), lambda i, j: (j, 0, 0)), ANY],
        out_specs=pl.BlockSpec((tm, 2 * shard), lambda i, j: (i, j)),
        compiler_params=_params(("parallel", "parallel")),
    )(cn, w_up_blocks, after)


def _mm_gw_up(cn, d_u):
    l, d = cn.shape
    _, _, d_ff = d_u.shape
    shard = 2 * d_ff // N_DEV
    pairs_per_half = d_ff // (2 * shard)
    tm = _divisor_tile(d, 512, 128)

    def body(a_ref, b_ref, o_ref):
        res = _dot_tn(a_ref[...], b_ref[...])
        o_ref[0] = res[:, 0:shard].astype(o_ref.dtype)
        o_ref[1] = res[:, shard:].astype(o_ref.dtype)

    return pl.pallas_call(
        body, name="mm_gw_up", out_shape=jax.ShapeDtypeStruct((N_DEV, d, shard), WIRE_DTYPE),
        grid=(d // tm, N_DEV // 2),
        in_specs=[pl.BlockSpec((l, tm), lambda i, j: (0, i)),
                  pl.BlockSpec((None, l, 2 * shard), lambda i, j: (j // pairs_per_half, 0, j % pairs_per_half))],
        out_specs=pl.BlockSpec((2, tm, shard), lambda i, j: (j, i, 0)),
        compiler_params=_params(("parallel", "parallel")),
    )(cn, d_u)


def _mm_d_cn(d_u, w_up_blocks, after):
    _, l, d_ff = d_u.shape
    n, d, shard = w_up_blocks.shape
    per = d_ff // shard
    main, left = _pair_split(shard)
    tm, tn = _divisor_tile(l, 544, 16), _divisor_tile(d, 256, 128)

    def body(a_ref, b_ref, after_ref, o_ref):
        acc = None
        for k in range(0, n, 2):
            half, c0 = k // per, (k % per) * shard
            parts = [_dot_nt(a_ref[half, :, c0 + s * shard:c0 + s * shard + main], b_ref[k + s, :, 0:main])
                     for s in range(2)]
            if left:
                a_tail = jnp.concatenate([a_ref[half, :, c0 + s * shard + main:c0 + (s + 1) * shard] for s in range(2)],
                                         axis=1)
                b_tail = jnp.concatenate([b_ref[k + s, :, main:] for s in range(2)], axis=1)
                parts.append(_dot_nt(a_tail, b_tail))
            for part in parts:
                acc = part if acc is None else acc + part
        o_ref[...] = acc

    return pl.pallas_call(
        body, name="mm_d_cn", out_shape=jax.ShapeDtypeStruct((l, d), F32), grid=(l // tm, d // tn),
        in_specs=[pl.BlockSpec((2, tm, d_ff), lambda i, j: (0, i, 0)),
                  pl.BlockSpec((n, tn, shard), lambda i, j: (0, j, 0)), ANY],
        out_specs=pl.BlockSpec((tm, tn), lambda i, j: (i, j)),
        compiler_params=_params(("parallel", "parallel")),
    )(d_u, w_up_blocks, after)


def _row_tile(l):
    return _divisor_tile(l, 544, 8)


def _rmsnorm_fwd(h, gain, name, res=None):
    l, d = h.shape
    tr = _row_tile(l)
    row = pl.BlockSpec((tr, d), lambda i: (i, 0))
    vec = pl.BlockSpec((1, d), lambda i: (0, 0))

    def body(*refs):
        if res is None:
            h_ref, g_ref, n_ref = refs
            x = h_ref[...]
        else:
            h_ref, r_ref, g_ref, s_ref, n_ref = refs
            x = h_ref[...] + r_ref[...]
            s_ref[...] = x
        y = x * lax.rsqrt(jnp.mean(x * x, axis=-1, keepdims=True) + NORM_EPS)
        n_ref[...] = (y * g_ref[...]).astype(n_ref.dtype)

    normed = jax.ShapeDtypeStruct((l, d), MXU_DTYPE)
    if res is None:
        return pl.pallas_call(body, name=name, out_shape=normed, grid=(l // tr,), in_specs=[row, vec],
                              out_specs=row, compiler_params=_params(("parallel",)))(h, gain)
    return pl.pallas_call(body, name=name, out_shape=(jax.ShapeDtypeStruct((l, d), F32), normed),
                          grid=(l // tr,), in_specs=[row, row, vec], out_specs=(row, row),
                          compiler_params=_params(("parallel",)))(h, res, gain)


def _rmsnorm_bwd(d_res, d_normed, x, gain, name, with_mxu_copy):
    l, d = x.shape
    tr = _row_tile(l) if with_mxu_copy else CHUNK
    row = pl.BlockSpec((tr, d), lambda i: (i, 0))
    vec = pl.BlockSpec((1, d), lambda i: (0, 0))

    def body(dres_ref, dn_ref, x_ref, g_ref, dx_ref, other_ref, dg_ref):
        i = pl.program_id(0)
        xv = x_ref[...]
        r = lax.rsqrt(jnp.mean(xv * xv, axis=-1, keepdims=True) + NORM_EPS)
        xh = xv * r
        dn = dn_ref[...]
        dxh = dn * g_ref[...]
        dx = dres_ref[...] + r * (dxh - xh * jnp.mean(dxh * xh, axis=-1, keepdims=True))
        if with_mxu_copy:
            dx_ref[...] = dx
            other_ref[...] = dx.astype(MXU_DTYPE)
        else:
            @pl.when(i == 0)
            def _():
                dx_ref[...] = dx

            @pl.when(i > 0)
            def _():
                other_ref[...] = dx

        @pl.when(i == 0)
        def _():
            dg_ref[...] = jnp.zeros_like(dg_ref)

        dg_ref[...] += jnp.sum(dn * xh, axis=0, keepdims=True)

    if with_mxu_copy:
        outs = [jax.ShapeDtypeStruct((l, d), F32), jax.ShapeDtypeStruct((l, d), MXU_DTYPE)]
        specs = [row, row]
    else:
        outs = [jax.ShapeDtypeStruct((CHUNK, d), F32), jax.ShapeDtypeStruct((l - CHUNK, d), F32)]
        specs = [pl.BlockSpec((CHUNK, d), lambda i: (0, 0)), pl.BlockSpec((CHUNK, d), lambda i: (jnp.maximum(i - 1, 0), 0))]
    outs.append(jax.ShapeDtypeStruct((1, d), F32))
    specs.append(vec)
    return pl.pallas_call(body, name=name, out_shape=tuple(outs), grid=(l // tr,),
                          in_specs=[row, row, row, vec], out_specs=tuple(specs),
                          compiler_params=_params(("arbitrary",)))(d_res, d_normed, x, gain)


def _loss_head(h1, mlp_out, gain, target):
    l, d = h1.shape
    n_blocks = l // CHUNK
    row = pl.BlockSpec((CHUNK, d), lambda i: (i, 0))
    vec = pl.BlockSpec((1, d), lambda i: (0, 0))
    tgt = pl.BlockSpec((CHUNK, d), lambda i: (jnp.maximum(i - 1, 0), 0))

    def body(h_ref, m_ref, g_ref, t_ref, dh_ref, dhb_ref, dg_ref, loss_ref, sq_ref):
        i = pl.program_id(0)
        x = h_ref[...] + m_ref[...]
        r = lax.rsqrt(jnp.mean(x * x, axis=-1, keepdims=True) + NORM_EPS)
        xh = x * r
        g = g_ref[...]
        real = i >= 1
        err = jnp.where(real, xh * g - t_ref[...], 0.0)
        dy = err * (1.0 / d)
        dxh = dy * g
        dh = r * (dxh - xh * jnp.mean(dxh * xh, axis=-1, keepdims=True))
        dh_ref[...] = dh
        dhb_ref[...] = dh.astype(MXU_DTYPE)

        @pl.when(i == 0)
        def _():
            dg_ref[...] = jnp.zeros_like(dg_ref)
            sq_ref[...] = jnp.zeros_like(sq_ref)

        dg_ref[...] += jnp.sum(dy * xh, axis=0, keepdims=True)
        sq_ref[...] += jnp.sum(err * err, axis=0, keepdims=True)

        @pl.when(i == n_blocks - 1)
        def _():
            total = jnp.sum(sq_ref[...], axis=-1, keepdims=True) * (0.5 / d)
            loss_ref[...] = jnp.broadcast_to(total, (1, 128))

    return pl.pallas_call(
        body, name="loss_head",
        out_shape=(jax.ShapeDtypeStruct((l, d), F32), jax.ShapeDtypeStruct((l, d), MXU_DTYPE),
                   jax.ShapeDtypeStruct((1, d), F32), jax.ShapeDtypeStruct((1, 128), F32)),
        grid=(n_blocks,), in_specs=[row, row, vec, tgt],
        out_specs=(row, row, vec, pl.BlockSpec((1, 128), lambda i: (0, 0))),
        scratch_shapes=[pltpu.VMEM((1, d), F32)],
        compiler_params=_params(("arbitrary",)),
    )(h1, mlp_out, gain, target)


def _dot(a, b):
    return jnp.dot(a, b, preferred_element_type=F32)


def _dot_nt(a, b):
    return lax.dot_general(a, b, (((1,), (1,)), ((), ())), preferred_element_type=F32)


def _dot_tn(a, b):
    return lax.dot_general(a, b, (((0,), (0,)), ((), ())), preferred_element_type=F32)


def _rope(t, cos2, sin2):
    return t * cos2 + pltpu.roll(t, HEAD_DIM // 2, 1) * sin2


def _rope_bwd(dr, cos2, sin2):
    return dr * cos2 + pltpu.roll(dr * sin2, HEAD_DIM // 2, 1)


def _sigmoid(x):
    return 1.0 / (1.0 + jnp.exp(-x))


def _row_valid(block, rows):
    r = block * CHUNK + lax.broadcasted_iota(jnp.int32, (rows, 1), 0)
    return r >= PAD_ROWS


def _retention_consts(l):
    pos = jnp.arange(l, dtype=F32) - PAD_ROWS
    inv_freq = 1.0 / (ROPE_BASE ** (jnp.arange(0, HEAD_DIM, 2, dtype=F32) / HEAD_DIM))
    ang = pos[:, None] * inv_freq[None, :]
    cos, sin = jnp.cos(ang), jnp.sin(ang)
    cos2 = jnp.concatenate([cos, cos], axis=-1)
    sin2 = jnp.concatenate([-sin, sin], axis=-1)
    log_g = jnp.log1p(-jnp.exp2(-5.0 - jnp.arange(N_HEADS, dtype=F32)))
    idx = jnp.arange(CHUNK, dtype=F32)
    diff = idx[:, None] - idx[None, :]
    decay = jnp.where(diff >= 0, jnp.exp(jnp.maximum(diff, 0.0)[None] * log_g[:, None, None]), 0.0)
    xi = jnp.exp((idx + 1.0)[None, :] * log_g[:, None])
    zeta = jnp.exp((CHUNK - 1.0 - idx)[None, :] * log_g[:, None])
    g_chunk = jnp.exp(CHUNK * log_g)
    bcast = lambda v: jnp.broadcast_to(v[:, :, None], (N_HEADS, CHUNK, HEAD_DIM))
    g_rows = jnp.broadcast_to(g_chunk[:, None, None], (N_HEADS, 8, HEAD_DIM))
    return cos2, sin2, decay, bcast(xi), bcast(zeta), g_rows


def _retention_fwd(proj, ret_gain, consts):
    l = proj.shape[0]
    n_chunks = l // CHUNK
    cos2, sin2, decay, xi, zeta, g_rows = consts
    scale = HEAD_DIM ** -0.5

    def body(p_ref, cos_ref, sin_ref, dec_ref, xi_ref, zeta_ref, gr_ref, gain_ref,
             mix_ref, o_ref, st_ref, state):
        c = pl.program_id(0)

        @pl.when(c == 0)
        def _():
            state[...] = jnp.zeros_like(state)

        cos_v, sin_v = cos_ref[...], sin_ref[...]
        valid = _row_valid(c, CHUNK)
        for h in range(N_HEADS):
            cols = slice(h * HEAD_DIM, (h + 1) * HEAD_DIM)
            q = p_ref[:, h * HEAD_DIM:(h + 1) * HEAD_DIM]
            k = p_ref[:, GROUP + h * HEAD_DIM:GROUP + (h + 1) * HEAD_DIM]
            v = p_ref[:, 2 * GROUP + h * HEAD_DIM:2 * GROUP + (h + 1) * HEAD_DIM]
            g = p_ref[:, 3 * GROUP + h * HEAD_DIM:3 * GROUP + (h + 1) * HEAD_DIM]
            rq = _rope(q, cos_v, sin_v).astype(MXU_DTYPE)
            rk = _rope(k, cos_v, sin_v) * scale
            rkb = rk.astype(MXU_DTYPE)
            vb = v.astype(MXU_DTYPE)
            st = state[h]
            st_ref[h] = st
            s = _dot_nt(rq, rkb) * dec_ref[h]
            o = _dot(s.astype(MXU_DTYPE), vb) + _dot(rq, st.astype(MXU_DTYPE)) * xi_ref[h]
            kz = (rk * zeta_ref[h]).astype(MXU_DTYPE)
            state[h] = gr_ref[h, 0:1, :] * st + _dot_tn(kz, vb)
            o_ref[:, cols] = o
            mu = jnp.mean(o, axis=-1, keepdims=True)
            oc = o - mu
            yn = oc * lax.rsqrt(jnp.mean(oc * oc, axis=-1, keepdims=True) + NORM_EPS)
            ret = (g * _sigmoid(g)) * (yn * gain_ref[:, cols])
            mix_ref[:, cols] = jnp.where(valid, ret, 0.0).astype(mix_ref.dtype)

    head_tab = pl.BlockSpec((N_HEADS, CHUNK, HEAD_DIM), lambda c: (0, 0, 0))
    return pl.pallas_call(
        body, name="retention_fwd",
        out_shape=(jax.ShapeDtypeStruct((l, 2 * GROUP), MXU_DTYPE), jax.ShapeDtypeStruct((l, GROUP), F32),
                   jax.ShapeDtypeStruct((n_chunks, N_HEADS, HEAD_DIM, HEAD_DIM), F32)),
        grid=(n_chunks,),
        in_specs=[pl.BlockSpec((CHUNK, 4 * GROUP), lambda c: (c, 0)),
                  pl.BlockSpec((CHUNK, HEAD_DIM), lambda c: (c, 0)),
                  pl.BlockSpec((CHUNK, HEAD_DIM), lambda c: (c, 0)),
                  head_tab, head_tab, head_tab,
                  pl.BlockSpec((N_HEADS, 8, HEAD_DIM), lambda c: (0, 0, 0)),
                  pl.BlockSpec((1, GROUP), lambda c: (0, 0))],
        out_specs=(pl.BlockSpec((CHUNK, GROUP), lambda c: (c, 0)),
                   pl.BlockSpec((CHUNK, GROUP), lambda c: (c, 0)),
                   pl.BlockSpec((None, N_HEADS, HEAD_DIM, HEAD_DIM), lambda c: (c, 0, 0, 0))),
        scratch_shapes=[pltpu.VMEM((N_HEADS, HEAD_DIM, HEAD_DIM), F32)],
        compiler_params=_params(("arbitrary",)),
    )(proj, cos2, sin2, decay, xi, zeta, g_rows, ret_gain)


def _retention_bwd(proj, o_pre, states, d_mix, ret_gain, consts):
    l = proj.shape[0]
    n_chunks = l // CHUNK
    cos2, sin2, decay, xi, zeta, g_rows = consts
    scale = HEAD_DIM ** -0.5
    rev = lambda c: n_chunks - 1 - c

    def body(p_ref, o_ref, st_ref, dm_ref, cos_ref, sin_ref, dec_ref, xi_ref, zeta_ref, gr_ref, gain_ref,
             dp_ref, dgain_ref, dstate):
        step = pl.program_id(0)

        @pl.when(step == 0)
        def _():
            dstate[...] = jnp.zeros_like(dstate)
            dgain_ref[...] = jnp.zeros_like(dgain_ref)

        cos_v, sin_v = cos_ref[...], sin_ref[...]
        valid = _row_valid(rev(step), CHUNK)
        for h in range(N_HEADS):
            cols = slice(h * HEAD_DIM, (h + 1) * HEAD_DIM)
            q = p_ref[:, h * HEAD_DIM:(h + 1) * HEAD_DIM]
            k = p_ref[:, GROUP + h * HEAD_DIM:GROUP + (h + 1) * HEAD_DIM]
            v = p_ref[:, 2 * GROUP + h * HEAD_DIM:2 * GROUP + (h + 1) * HEAD_DIM]
            g = p_ref[:, 3 * GROUP + h * HEAD_DIM:3 * GROUP + (h + 1) * HEAD_DIM]
            o = o_ref[:, cols]
            gain = gain_ref[:, cols]
            d_ret = jnp.where(valid, dm_ref[:, cols], 0.0)
            mu = jnp.mean(o, axis=-1, keepdims=True)
            oc = o - mu
            rstd = lax.rsqrt(jnp.mean(oc * oc, axis=-1, keepdims=True) + NORM_EPS)
            yn = oc * rstd
            sig = _sigmoid(g)
            gate = g * sig
            dgain_ref[:, cols] += jnp.sum(d_ret * gate * yn, axis=0, keepdims=True)
            d_g = d_ret * (yn * gain) * (sig * (1.0 + g * (1.0 - sig)))
            d_yn = d_ret * gate * gain
            d_o = rstd * (d_yn - jnp.mean(d_yn, axis=-1, keepdims=True)
                          - yn * jnp.mean(d_yn * yn, axis=-1, keepdims=True))
            rq = _rope(q, cos_v, sin_v)
            rk = _rope(k, cos_v, sin_v) * scale
            rqb, rkb, vb = rq.astype(MXU_DTYPE), rk.astype(MXU_DTYPE), v.astype(MXU_DTYPE)
            dob = d_o.astype(MXU_DTYPE)
            dec = dec_ref[h]
            xi_h, zeta_h = xi_ref[h], zeta_ref[h]
            st_b = st_ref[h].astype(MXU_DTYPE)
            dst = dstate[h]
            dst_b = dst.astype(MXU_DTYPE)
            s_b = (_dot_nt(rqb, rkb) * dec).astype(MXU_DTYPE)
            da_b = (_dot_nt(dob, vb) * dec).astype(MXU_DTYPE)
            doxi_b = (d_o * xi_h).astype(MXU_DTYPE)
            kz_b = (rk * zeta_h).astype(MXU_DTYPE)
            d_rq = _dot(da_b, rkb) + _dot_nt(doxi_b, st_b)
            d_rk = _dot_tn(da_b, rqb) + _dot_nt(vb, dst_b) * zeta_h
            d_v = _dot_tn(s_b, dob) + _dot(kz_b, dst_b)
            dstate[h] = gr_ref[h, 0:1, :] * dst + _dot_tn(rqb, doxi_b)
            d_q = _rope_bwd(d_rq, cos_v, sin_v)
            d_k = _rope_bwd(d_rk * scale, cos_v, sin_v)
            dp_ref[:, h * HEAD_DIM:(h + 1) * HEAD_DIM] = d_q.astype(dp_ref.dtype)
            dp_ref[:, GROUP + h * HEAD_DIM:GROUP + (h + 1) * HEAD_DIM] = d_k.astype(dp_ref.dtype)
            dp_ref[:, 2 * GROUP + h * HEAD_DIM:2 * GROUP + (h + 1) * HEAD_DIM] = d_v.astype(dp_ref.dtype)
            dp_ref[:, 3 * GROUP + h * HEAD_DIM:3 * GROUP + (h + 1) * HEAD_DIM] = d_g.astype(dp_ref.dtype)

    head_tab = pl.BlockSpec((N_HEADS, CHUNK, HEAD_DIM), lambda c: (0, 0, 0))
    return pl.pallas_call(
        body, name="retention_bwd",
        out_shape=(jax.ShapeDtypeStruct((l, 4 * GROUP), MXU_DTYPE), jax.ShapeDtypeStruct((1, GROUP), F32)),
        grid=(n_chunks,),
        in_specs=[pl.BlockSpec((CHUNK, 4 * GROUP), lambda c: (rev(c), 0)),
                  pl.BlockSpec((CHUNK, GROUP), lambda c: (rev(c), 0)),
                  pl.BlockSpec((None, N_HEADS, HEAD_DIM, HEAD_DIM), lambda c: (rev(c), 0, 0, 0)),
                  pl.BlockSpec((CHUNK, GROUP), lambda c: (rev(c), 0)),
                  pl.BlockSpec((CHUNK, HEAD_DIM), lambda c: (rev(c), 0)),
                  pl.BlockSpec((CHUNK, HEAD_DIM), lambda c: (rev(c), 0)),
                  head_tab, head_tab, head_tab,
                  pl.BlockSpec((N_HEADS, 8, HEAD_DIM), lambda c: (0, 0, 0)),
                  pl.BlockSpec((1, GROUP), lambda c: (0, 0))],
        out_specs=(pl.BlockSpec((CHUNK, 4 * GROUP), lambda c: (rev(c), 0)),
                   pl.BlockSpec((1, GROUP), lambda c: (0, 0))),
        scratch_shapes=[pltpu.VMEM((N_HEADS, HEAD_DIM, HEAD_DIM), F32)],
        compiler_params=_params(("arbitrary",)),
    )(proj, o_pre, states, d_mix, cos2, sin2, decay, xi, zeta, g_rows, ret_gain)


FF_TILE = (7 * GROUP) // 128


def _log_forget(ff, bias_row, valid):
    x = ff + bias_row
    e = jnp.exp(-jnp.abs(x))
    lf = jnp.minimum(x, 0.0) - jnp.log(1.0 + e)
    head_lane = lax.broadcasted_iota(jnp.int32, x.shape, 1) < N_HEADS
    keep = lambda t: jnp.where(head_lane, jnp.where(valid, t, 0.0), 0.0)
    return keep(lf), keep(jnp.where(x >= 0, e, 1.0) / (1.0 + e))


def _fox_prep(proj, bias_row):
    l = proj.shape[0]
    n_blocks = l // CHUNK

    def body(ff_ref, b_ref, bc_ref, rows_ref, cum):
        r = lax.broadcasted_iota(jnp.int32, (CHUNK, CHUNK), 0)
        cidx = lax.broadcasted_iota(jnp.int32, (CHUNK, CHUNK), 1)
        tri = jnp.where(r >= cidx, 1.0, 0.0).astype(F32)
        carry = jnp.zeros((1, 128), F32)
        for blk in range(n_blocks):
            rows = slice(blk * CHUNK, (blk + 1) * CHUNK)
            valid = _row_valid(blk, CHUNK)
            lf, _ = _log_forget(ff_ref[rows, :], b_ref[...], valid)
            local = jnp.dot(tri, lf, precision=lax.Precision.HIGHEST, preferred_element_type=F32) + carry
            carry = local[CHUNK - 1:CHUNK, :]
            masked = jnp.where(valid, local, -NEG_BIG)
            cum[rows, :] = masked
            t = masked.T
            for h in range(N_HEADS):
                rows_ref[h, :, rows] = t[h:h + 1, :]
        full = cum[...]
        for h in range(N_HEADS):
            bc_ref[h] = jnp.broadcast_to(full[:, h:h + 1], (l, 128))

    return pl.pallas_call(
        body, name="fox_prep",
        out_shape=(jax.ShapeDtypeStruct((N_HEADS, l, 128), F32), jax.ShapeDtypeStruct((N_HEADS, 1, l), F32)),
        grid=(1,),
        in_specs=[pl.BlockSpec((l, 128), lambda i: (0, FF_TILE)), pl.BlockSpec((1, 128), lambda i: (0, 0))],
        out_specs=(pl.BlockSpec((N_HEADS, l, 128), lambda i: (0, 0, 0)),
                   pl.BlockSpec((N_HEADS, 1, l), lambda i: (0, 0, 0))),
        scratch_shapes=[pltpu.VMEM((l, 128), F32)],
        compiler_params=_params(("arbitrary",)),
    )(proj, bias_row)


ATTN_BLOCK = 2 * CHUNK


def _attn_blocks(l):
    assert (l - CHUNK) % ATTN_BLOCK == 0
    return [(0, CHUNK)] + [(s, ATTN_BLOCK) for s in range(CHUNK, l, ATTN_BLOCK)]


def _rows_valid(start, size):
    return start + lax.broadcasted_iota(jnp.int32, (size, 1), 0) >= PAD_ROWS


def _fox_fwd(proj, cum_bc, cum_rows, mix):
    l = proj.shape[0]
    blocks = _attn_blocks(l)
    scale = HEAD_DIM ** -0.5
    qt, kt, vt = 4 * N_HEADS, 5 * N_HEADS, 6 * N_HEADS

    def body(q_ref, k_ref, v_ref, cbc_ref, crow_ref, mix_in, o_ref, lse_ref, qb_s, kb_s, vb_s):
        qb_s[...] = q_ref[...].astype(MXU_DTYPE)
        kb_s[...] = k_ref[...].astype(MXU_DTYPE)
        vb_s[...] = v_ref[...].astype(MXU_DTYPE)
        for p, (qs, qn) in enumerate(blocks):
            qb = qb_s[qs:qs + qn, :]
            cq = cbc_ref[qs:qs + qn, :]
            m = jnp.full((qn, 1), NEG_BIG, F32)
            lsum = jnp.zeros((qn, 1), F32)
            acc = jnp.zeros((qn, HEAD_DIM), F32)
            for j in range(p + 1):
                ks, kn = blocks[j]
                bias = jnp.tile(cq, (1, kn // CHUNK)) - crow_ref[:, ks:ks + kn]
                s = _dot_nt(qb, kb_s[ks:ks + kn, :]) * scale + bias
                if j == p:
                    q_pos = qs + lax.broadcasted_iota(jnp.int32, (qn, kn), 0)
                    k_pos = ks + lax.broadcasted_iota(jnp.int32, (qn, kn), 1)
                    s = jnp.where(k_pos <= q_pos, s, NEG_BIG)
                m_new = jnp.maximum(m, jnp.max(s, axis=-1, keepdims=True))
                alpha = jnp.exp(m - m_new)
                pr = jnp.exp(s - m_new)
                lsum = lsum * alpha + jnp.sum(pr, axis=-1, keepdims=True)
                acc = acc * alpha + _dot(pr.astype(MXU_DTYPE), vb_s[ks:ks + kn, :])
                m = m_new
            o = jnp.where(_rows_valid(qs, qn), acc * (1.0 / lsum), 0.0)
            o_ref[qs:qs + qn, :] = o.astype(o_ref.dtype)
            lse = m + jnp.log(lsum)
            lse_ref[:, qs:qs + qn] = jnp.broadcast_to(lse, (qn, CHUNK)).T[0:1, :]

    head_col = lambda t: pl.BlockSpec((l, HEAD_DIM), lambda h: (0, t + h))
    return pl.pallas_call(
        body, name="fox_fwd",
        out_shape=(jax.ShapeDtypeStruct(mix.shape, mix.dtype), jax.ShapeDtypeStruct((N_HEADS, 1, l), F32)),
        grid=(N_HEADS,),
        in_specs=[head_col(qt), head_col(kt), head_col(vt),
                  pl.BlockSpec((None, l, 128), lambda h: (h, 0, 0)),
                  pl.BlockSpec((None, 1, l), lambda h: (h, 0, 0)),
                  ANY],
        out_specs=(head_col(N_HEADS), pl.BlockSpec((None, 1, l), lambda h: (h, 0, 0))),
        input_output_aliases={5: 0},
        scratch_shapes=[pltpu.VMEM((l, HEAD_DIM), MXU_DTYPE)] * 3,
        compiler_params=_params(("parallel",)),
    )(proj, proj, proj, cum_bc, cum_rows, mix)


def _fox_bwd(proj, cum_bc, cum_rows, d_mix, lse_rows):
    l = proj.shape[0]
    blocks = _attn_blocks(l)
    scale = HEAD_DIM ** -0.5
    qt, kt, vt = 4 * N_HEADS, 5 * N_HEADS, 6 * N_HEADS

    def body(q_ref, k_ref, v_ref, do_ref, cbc_ref, crow_ref, lse_ref,
             dq_ref, dk_ref, dv_ref, ds_ref, dk_acc, dv_acc, qb_s, kb_s, vb_s, dob_s):
        qb_s[...] = q_ref[...].astype(MXU_DTYPE)
        kb_s[...] = k_ref[...].astype(MXU_DTYPE)
        vb_s[...] = v_ref[...].astype(MXU_DTYPE)
        dob_s[...] = jnp.where(_rows_valid(0, l), do_ref[...], 0.0).astype(MXU_DTYPE)
        dk_acc[...] = jnp.zeros_like(dk_acc)
        dv_acc[...] = jnp.zeros_like(dv_acc)
        ds_ref[...] = jnp.zeros_like(ds_ref)
        shift_row = crow_ref[...] - lse_ref[...]

        for p, (qs, qn) in enumerate(blocks):
            qb, dob = qb_s[qs:qs + qn, :], dob_s[qs:qs + qn, :]
            shift = shift_row[:, qs:qs + qn]

            def probs(j):
                ks, kn = blocks[j]
                ck = jnp.tile(cbc_ref[ks:ks + kn, :], (1, qn // CHUNK))
                s_t = _dot_nt(kb_s[ks:ks + kn, :], qb) * scale + (shift - ck)
                if j == p:
                    k_pos = ks + lax.broadcasted_iota(jnp.int32, (kn, qn), 0)
                    q_pos = qs + lax.broadcasted_iota(jnp.int32, (kn, qn), 1)
                    s_t = jnp.where(k_pos <= q_pos, s_t, NEG_BIG)
                return jnp.exp(s_t), _dot_nt(vb_s[ks:ks + kn, :], dob)

            delta = jnp.zeros((1, qn), F32)
            for j in range(p + 1):
                p_t, dp_t = probs(j)
                delta = delta + jnp.sum(p_t * dp_t, axis=0, keepdims=True)
            dq = jnp.zeros((qn, HEAD_DIM), F32)
            for j in range(p + 1):
                ks, kn = blocks[j]
                rows = slice(ks, ks + kn)
                p_t, dp_t = probs(j)
                ds_t = p_t * (dp_t - delta)
                ds_b = ds_t.astype(MXU_DTYPE)
                dv_acc[rows, :] += _dot(p_t.astype(MXU_DTYPE), dob)
                dk_acc[rows, :] += _dot(ds_b, qb) * scale
                ds_ref[rows, :] += sum(ds_t[:, c:c + CHUNK] for c in range(0, qn, CHUNK))
                dq = dq + _dot_tn(ds_b, kb_s[rows, :])
            dq_ref[qs:qs + qn, :] = (dq * scale).astype(dq_ref.dtype)

        dk_ref[...] = dk_acc[...].astype(dk_ref.dtype)
        dv_ref[...] = dv_acc[...].astype(dv_ref.dtype)

    col = jax.ShapeDtypeStruct((l, GROUP), MXU_DTYPE)
    head_col = lambda t: pl.BlockSpec((l, HEAD_DIM), lambda h: (0, t + h))
    return pl.pallas_call(
        body, name="fox_bwd",
        out_shape=(col, col, col, jax.ShapeDtypeStruct((N_HEADS, l, 128), F32)),
        grid=(N_HEADS,),
        in_specs=[head_col(qt), head_col(kt), head_col(vt), head_col(N_HEADS),
                  pl.BlockSpec((None, l, 128), lambda h: (h, 0, 0)),
                  pl.BlockSpec((None, 1, l), lambda h: (h, 0, 0)),
                  pl.BlockSpec((None, 1, l), lambda h: (h, 0, 0))],
        out_specs=(head_col(0), head_col(0), head_col(0), pl.BlockSpec((None, l, 128), lambda h: (h, 0, 0))),
        scratch_shapes=[pltpu.VMEM((l, HEAD_DIM), F32)] * 2 + [pltpu.VMEM((l, HEAD_DIM), MXU_DTYPE)] * 4,
        compiler_params=_params(("parallel",)),
    )(proj, proj, proj, d_mix, cum_bc, cum_rows, lse_rows)


def _fox_gate_bwd(ds_sum, proj, bias_row):
    l = proj.shape[0]
    n_blocks = l // CHUNK

    def body(ds_ref, ff_ref, b_ref, dff_ref, db_ref):
        r = lax.broadcasted_iota(jnp.int32, (CHUNK, CHUNK), 0)
        cidx = lax.broadcasted_iota(jnp.int32, (CHUNK, CHUNK), 1)
        upper = jnp.where(cidx >= r, 1.0, 0.0).astype(F32)
        carry = jnp.zeros((1, 128), F32)
        db = jnp.zeros((1, 128), F32)
        for blk in reversed(range(n_blocks)):
            rows = slice(blk * CHUNK, (blk + 1) * CHUNK)
            key_sum = jnp.zeros((CHUNK, 128), F32)
            for h in range(N_HEADS):
                select = jnp.where(cidx == h, 1.0, 0.0).astype(F32)
                key_sum = key_sum + jnp.dot(ds_ref[h, rows, :], select, precision=lax.Precision.HIGHEST,
                                            preferred_element_type=F32)
            suffix = jnp.dot(upper, key_sum, precision=lax.Precision.HIGHEST, preferred_element_type=F32) + carry
            carry = suffix[0:1, :]
            _, dsig = _log_forget(ff_ref[rows, :], b_ref[...], _row_valid(blk, CHUNK))
            dff = -suffix * dsig
            dff_ref[rows, :] = dff.astype(dff_ref.dtype)
            db = db + jnp.sum(dff, axis=0, keepdims=True)
        db_ref[...] = db

    return pl.pallas_call(
        body, name="fox_gate_bwd",
        out_shape=(jax.ShapeDtypeStruct((l, 128), MXU_DTYPE), jax.ShapeDtypeStruct((1, 128), F32)),
        grid=(1,),
        in_specs=[pl.BlockSpec((N_HEADS, l, 128), lambda i: (0, 0, 0)),
                  pl.BlockSpec((l, 128), lambda i: (0, FF_TILE)),
                  pl.BlockSpec((1, 128), lambda i: (0, 0))],
        out_specs=(pl.BlockSpec((l, 128), lambda i: (0, 0)), pl.BlockSpec((1, 128), lambda i: (0, 0))),
        compiler_params=_params(("arbitrary",)),
    )(ds_sum, proj, bias_row)


def _conv(u, w, b):
    return b + w[0:1, :] * pltpu.roll(u, 2, 0) + w[1:2, :] * pltpu.roll(u, 1, 0) + w[2:3, :] * u


def _conv_act_fwd(u, conv_w, conv_b, d_ff):
    l = u.shape[0]
    tc = _divisor_tile(d_ff, 256, 128)
    nt = d_ff // tc

    def body(ug_ref, uv_ref, wg_ref, wv_ref, bg_ref, bv_ref, a_ref, y_ref):
        yg = _conv(ug_ref[...], wg_ref[...], bg_ref[...])
        yv = _conv(uv_ref[...], wv_ref[...], bv_ref[...])
        act = yg * _sigmoid(yg) * yv
        a_ref[...] = jnp.where(_row_valid(0, l), act, 0.0).astype(a_ref.dtype)
        y_ref[0] = yg.astype(y_ref.dtype)
        y_ref[1] = yv.astype(y_ref.dtype)

    return pl.pallas_call(
        body, name="conv_act_fwd",
        out_shape=(jax.ShapeDtypeStruct((l, d_ff), MXU_DTYPE), jax.ShapeDtypeStruct((2, l, d_ff), MXU_DTYPE)),
        grid=(nt,),
        in_specs=[pl.BlockSpec((l, tc), lambda j: (0, j)), pl.BlockSpec((l, tc), lambda j: (0, j + nt)),
                  pl.BlockSpec((8, tc), lambda j: (0, j)), pl.BlockSpec((8, tc), lambda j: (0, j + nt)),
                  pl.BlockSpec((1, tc), lambda j: (0, j)), pl.BlockSpec((1, tc), lambda j: (0, j + nt))],
        out_specs=(pl.BlockSpec((l, tc), lambda j: (0, j)), pl.BlockSpec((2, l, tc), lambda j: (0, 0, j))),
        compiler_params=_params(("parallel",)),
    )(u, u, conv_w, conv_w, conv_b, conv_b)


def _conv_act_bwd(u, y, conv_w, d_act, d_ff):
    l = u.shape[0]
    tc = _divisor_tile(d_ff, 256, 128)
    nt = d_ff // tc

    def body(ug_ref, uv_ref, y_ref, wg_ref, wv_ref, da_ref, du_ref, dwb_ref):
        valid = _row_valid(0, l)
        ug, uv = ug_ref[...], uv_ref[...]
        wg, wv = wg_ref[...], wv_ref[...]
        yg, yv = y_ref[0].astype(F32), y_ref[1].astype(F32)
        sig = _sigmoid(yg)
        da = jnp.where(valid, da_ref[...], 0.0)
        d_yv = da * (yg * sig)
        d_yg = da * yv * (sig * (1.0 + yg * (1.0 - sig)))
        for idx, (dy, uu, w) in enumerate(((d_yg, ug, wg), (d_yv, uv, wv))):
            du = w[2:3, :] * dy + w[1:2, :] * pltpu.roll(dy, l - 1, 0) + w[0:1, :] * pltpu.roll(dy, l - 2, 0)
            du_ref[idx] = jnp.where(valid, du, 0.0).astype(du_ref.dtype)
            dwb_ref[idx, 0:1, :] = jnp.sum(dy * pltpu.roll(uu, 2, 0), axis=0, keepdims=True)
            dwb_ref[idx, 1:2, :] = jnp.sum(dy * pltpu.roll(uu, 1, 0), axis=0, keepdims=True)
            dwb_ref[idx, 2:3, :] = jnp.sum(dy * uu, axis=0, keepdims=True)
            dwb_ref[idx, 3:4, :] = jnp.sum(dy, axis=0, keepdims=True)
            dwb_ref[idx, 4:8, :] = jnp.zeros((4, tc), F32)

    return pl.pallas_call(
        body, name="conv_act_bwd",
        out_shape=(jax.ShapeDtypeStruct((2, l, d_ff), MXU_DTYPE), jax.ShapeDtypeStruct((2, 8, d_ff), F32)),
        grid=(nt,),
        in_specs=[pl.BlockSpec((l, tc), lambda j: (0, j)), pl.BlockSpec((l, tc), lambda j: (0, j + nt)),
                  pl.BlockSpec((2, l, tc), lambda j: (0, 0, j)),
                  pl.BlockSpec((8, tc), lambda j: (0, j)), pl.BlockSpec((8, tc), lambda j: (0, j + nt)),
                  pl.BlockSpec((l, tc), lambda j: (0, j))],
        out_specs=(pl.BlockSpec((2, l, tc), lambda j: (0, 0, j)), pl.BlockSpec((2, 8, tc), lambda j: (0, 0, j))),
        compiler_params=_params(("parallel",)),
    )(u, u, y, conv_w, conv_w, d_act)


def _adamw(w, g, m, v, name):
    shape = w.shape
    if w.ndim == 1:
        as2d = (1, shape[0])
    else:
        as2d = (int(np.prod(shape[:-1])), shape[-1])
    r, c = as2d
    tr = _divisor_tile(r, 256, 8)
    spec = pl.BlockSpec((tr, c), lambda i: (i, 0))

    def body(w_ref, g_ref, m_ref, v_ref, d_ref, nm_ref, nv_ref):
        d_ref[...], nm_ref[...], nv_ref[...] = _adamw_math(w_ref[...], g_ref[...], m_ref[...], v_ref[...])

    sds = jax.ShapeDtypeStruct(as2d, F32)
    outs = pl.pallas_call(
        body, name=name, out_shape=(sds, sds, sds), grid=(r // tr,),
        in_specs=[spec] * 4, out_specs=(spec,) * 3,
        compiler_params=_params(("parallel",)),
    )(w.reshape(as2d), g.reshape(as2d), m.reshape(as2d), v.reshape(as2d))
    return tuple(o.reshape(shape) for o in outs)


def _pad_rows(a, rows):
    return jnp.pad(a, ((0, rows - a.shape[0]), (0, 0)))


def kernel(x, meta_tokens, norm1_gain, w_in, b_forget, ret_norm_gain, w_out, norm2_gain, w_up, conv_w, conv_b, w_down, final_norm_gain, loss_target, m_meta_tokens, m_norm1_gain, m_w_in, m_b_forget, m_ret_norm_gain, m_w_out, m_norm2_gain, m_w_up, m_conv_w, m_conv_b, m_w_down, m_final_norm_gain, v_meta_tokens, v_norm1_gain, v_w_in, v_b_forget, v_ret_norm_gain, v_w_out, v_norm2_gain, v_w_up, v_conv_w, v_conv_b, v_w_down, v_final_norm_gain):
    seq, d = x.shape[1], x.shape[2]
    l = CHUNK + seq
    d_ff = w_down.shape[1] * N_DEV
    up_shard = w_up.shape[2]
    assert 4 * up_shard == d_ff and w_in.shape[2] == WIN_SHARD and d == 2 * GROUP
    dev = _device_index()
    mx, my, mc = _my_position()
    core = jnp.reshape(mc, (1,)).astype(jnp.int32)
    chip = jnp.reshape(2 * mx + my, (1,)).astype(jnp.int32)
    dev1 = jnp.reshape(dev, (1,)).astype(jnp.int32)

    small = jnp.concatenate([meta_tokens.reshape(-1, 128), conv_w[0].reshape(-1, 128)], axis=0)
    n_meta_rows = N_META * (d // N_DEV) // 128
    small_rows = small.shape[0]
    small_all = _all_gather(_pad_rows(small, -(-small_rows // 8) * 8), "gather_small")
    meta_full = jnp.transpose(small_all[:, :n_meta_rows].reshape(N_DEV, N_META, d // N_DEV), (1, 0, 2)).reshape(N_META, d)
    conv_w_full = _pad_rows(jnp.transpose(small_all[:, n_meta_rows:small_rows].reshape(N_DEV, 3, up_shard),
                                          (1, 0, 2)).reshape(3, 2 * d_ff), 8)
    to_rows = lambda t: jnp.pad(jnp.transpose(t[0]), ((0, WIN_ROWS - WIN_SHARD), (0, 0)))
    from_rows = lambda t: jnp.transpose(t[:WIN_SHARD])[None]
    w_in_rows = to_rows(w_in)
    out_rows = d // N_DEV
    mixer_rows = -(-(WIN_ROWS + out_rows) // 304) * 304
    mixer_shard = jnp.concatenate([w_in_rows.astype(WIRE_DTYPE), w_out[0].astype(WIRE_DTYPE),
                                   jnp.zeros((mixer_rows - WIN_ROWS - out_rows, d), WIRE_DTYPE)], axis=0)

    h0 = jnp.concatenate([jnp.zeros((PAD_ROWS, d), F32), meta_full, x[0]], axis=0)
    consts = _retention_consts(l)
    bias_row = jnp.pad(b_forget, ((0, 0), (0, 128 - N_HEADS)))
    a = _rmsnorm_fwd(h0, norm1_gain, "rmsnorm1")
    mixer_blocks = _gather_ring(mixer_shard, dev1, a, "gather_w_in")
    start_up = _gather_start(w_up[0], dev1, mixer_blocks, "gather_w_up_start")
    w_in_full = _assemble_w_in(mixer_blocks).astype(MXU_DTYPE)
    proj = _mm_nt(a, w_in_full, F32, "mm_proj", after=start_up[4])
    ret_mix, ret_pre, ret_states = _retention_fwd(proj, ret_norm_gain, consts)
    cum_bc, cum_rows = _fox_prep(proj, bias_row)
    mix, lse_rows = _fox_fwd(proj, cum_bc, cum_rows, ret_mix)
    w_out_full = mixer_blocks[:, WIN_ROWS:WIN_ROWS + out_rows].reshape(d, d).astype(MXU_DTYPE)
    h1, cn = _rmsnorm_fwd(h0, norm2_gain, "resid_rmsnorm2", res=_mm_nn(mix, w_out_full, F32, "mm_out"))
    w_up_blocks = _gather_finish(start_up, cn, "gather_w_up").astype(MXU_DTYPE)
    start_down = _gather_start(w_down[0], dev1, w_up_blocks, "gather_w_down_start")
    u = _mm_up(cn, w_up_blocks, start_down[4])
    act, conv_y = _conv_act_fwd(u, conv_w_full, conv_b + start_down[4][0, 0], d_ff)
    w_down_full = _gather_finish(start_down, act, "gather_w_down").reshape(d_ff, d).astype(MXU_DTYPE)
    mlp_out = _mm_nn(act, w_down_full, F32, "mm_down", tm_cap=544, tk_cap=d_ff)
    d_h2, d_h2_b, dg_final, loss_part = _loss_head(h1, mlp_out, final_norm_gain.reshape(1, d), loss_target[0])

    gw_down = _mm_tn(act, d_h2_b, WIRE_DTYPE, "mm_gw_down", tm_cap=1408, tn_cap=1024)
    d2d_down = _reduce_scatter_d2d_start(gw_down.reshape(N_DEV, d_ff // N_DEV, d), d_h2, "rs_w_down")
    d_act = _mm_nt(d_h2_b, w_down_full, F32, "mm_d_act", after=d2d_down[4])
    rs_down = _reduce_scatter_ici_start(d2d_down, d_act, core, "rs_w_down")
    d_u, d_conv = _conv_act_bwd(u, conv_y, conv_w_full + rs_down[4][0, 0], d_act, d_ff)
    tm = _divisor_tile(l, 1088, 16)
    gw_up = _mm_gw_up(cn, d_u)
    d2d_up = _reduce_scatter_d2d_start(gw_up, d_act, "rs_w_up")
    d_cn = _mm_d_cn(d_u, w_up_blocks, d2d_up[4])
    rs_up = _reduce_scatter_ici_start(d2d_up, d_cn, core, "rs_w_up")
    d_h1, d_h1_b, dg_norm2 = _rmsnorm_bwd(d_h2, d_cn, h1, norm2_gain + rs_up[4][0, 0], "rmsnorm2_bwd", True)

    gw_out = _mm_tn(mix, d_h1_b, WIRE_DTYPE, "mm_gw_out")
    d2d_out = _reduce_scatter_d2d_start(gw_out.reshape(N_DEV, d // N_DEV, d), d_cn, "rs_w_out")
    d_mix = _mm_nt(d_h1_b, w_out_full, F32, "mm_d_mix", after=d2d_out[4])
    d_fq, d_fk, d_fv, ds_sum = _fox_bwd(proj, cum_bc, cum_rows, d_mix, lse_rows)
    d_ff_tile, db_forget_row = _fox_gate_bwd(ds_sum, proj, bias_row)
    d_ret, dg_ret = _retention_bwd(proj, ret_pre, ret_states, d_mix, ret_norm_gain, consts)
    rs_out = _reduce_scatter_ici_start(d2d_out, d_ret, core, "rs_w_out")
    d_proj = jnp.concatenate(
        [d_ret, d_fq, d_fk, d_fv, d_ff_tile, jnp.zeros((l, WIN_N - 7 * GROUP - 128), MXU_DTYPE)], axis=1)
    gw_in = _mm_tn(d_proj, a, WIRE_DTYPE, "mm_gw_in", tm_cap=1536, after=rs_out[4])
    rs_in = _reduce_scatter_start(_extract_w_in_windows(gw_in), core, "rs_w_in")
    d_a = _mm_nn(d_proj, w_in_full, F32, "mm_d_a", tm_cap=544, tn_cap=256, tk_cap=WIN_N, after=rs_in[4])
    d_front, d_tokens, dg_norm1 = _rmsnorm_bwd(d_h1, d_a, h0, norm1_gain + rs_in[4][0, 0], "rmsnorm1_bwd", False)
    grad_x = d_tokens[None]
    d_meta = d_front[PAD_ROWS:CHUNK]

    d_conv_w = jnp.concatenate([d_conv[0, 0:3], d_conv[1, 0:3]], axis=1)
    d_conv_b = jnp.concatenate([d_conv[0, 3:4], d_conv[1, 3:4]], axis=1)
    pieces = [loss_part[:, 0:1], dg_norm1, db_forget_row[:, 0:N_HEADS], dg_ret, dg_norm2, d_conv_b, dg_final,
              d_meta.reshape(1, -1), d_conv_w.reshape(1, -1)]
    sizes = [p.shape[1] for p in pieces]
    flat = jnp.concatenate(pieces, axis=1)
    padded = -(-flat.shape[1] // 1024) * 1024
    flat = jnp.pad(flat, ((0, 0), (0, padded - flat.shape[1]))).reshape(padded // 128, 128)
    small_ar = _small_all_reduce_start(flat, d_tokens, "all_reduce_small")

    lead = lambda outs: tuple(o[None] for o in outs)
    fin_down = lead(_reduce_scatter_finish(rs_down, small_ar[4], chip, w_down[0], m_w_down[0], v_w_down[0], "rs_w_down"))
    fin_up = lead(_reduce_scatter_finish(rs_up, fin_down[3], chip, w_up[0], m_w_up[0], v_w_up[0], "rs_w_up"))
    fin_out = lead(_reduce_scatter_finish(rs_out, fin_up[3], chip, w_out[0], m_w_out[0], v_w_out[0], "rs_w_out"))
    fin_in = tuple(from_rows(o) for o in _reduce_scatter_finish(
        rs_in, fin_out[3], chip, w_in_rows, to_rows(m_w_in), to_rows(v_w_in), "rs_w_in"))
    g_w_down, g_w_up, g_w_out, g_w_in = fin_down[0], fin_up[0], fin_out[0], fin_in[0]
    early = [fin_down[1:], fin_up[1:], fin_out[1:], fin_in[1:]]
    total = _small_all_reduce_finish(small_ar, fin_in[3], dev1, "all_reduce_small").reshape(1, padded)
    offs = np.concatenate([[0], np.cumsum(sizes)])
    take = lambda k: total[:, int(offs[k]):int(offs[k + 1])]
    loss = take(0).reshape(())
    g_norm1, g_bf, g_ret_gain, g_norm2 = take(1), take(2), take(3), take(4)
    g_conv_b, g_final = take(5), take(6).reshape(d)
    g_meta = lax.dynamic_slice(take(7).reshape(N_META, d), (jnp.int32(0), (dev * (d // N_DEV)).astype(jnp.int32)),
                               (N_META, d // N_DEV))
    g_conv_w = lax.dynamic_slice(take(8).reshape(3, 2 * d_ff), (jnp.int32(0), (dev * up_shard).astype(jnp.int32)),
                                 (3, up_shard))[None]

    weights = [meta_tokens, norm1_gain, w_in, b_forget, ret_norm_gain, w_out, norm2_gain, w_up, conv_w, conv_b,
               w_down, final_norm_gain]
    grads = [g_meta, g_norm1, g_w_in, g_bf, g_ret_gain, g_w_out, g_norm2, g_w_up, g_conv_w, g_conv_b, g_w_down,
             g_final]
    done = {"w_down": early[0], "w_up": early[1], "w_out": early[2], "w_in": early[3]}
    ms = [m_meta_tokens, m_norm1_gain, m_w_in, m_b_forget, m_ret_norm_gain, m_w_out, m_norm2_gain, m_w_up, m_conv_w,
          m_conv_b, m_w_down, m_final_norm_gain]
    vs = [v_meta_tokens, v_norm1_gain, v_w_in, v_b_forget, v_ret_norm_gain, v_w_out, v_norm2_gain, v_w_up, v_conv_w,
          v_conv_b, v_w_down, v_final_norm_gain]
    names = ["meta", "norm1", "w_in", "b_forget", "ret_gain", "w_out", "norm2", "w_up", "conv_w", "conv_b", "w_down",
             "final_gain"]
    deltas, new_ms, new_vs = [], [], []
    for w, g, m, v, n in zip(weights, grads, ms, vs, names):
        dl, nm, nv = done[n] if n in done else _adamw(w, g, m, v, "adamw_" + n)
        deltas.append(dl)
        new_ms.append(nm)
        new_vs.append(nv)
    return (loss, grad_x, *grads, *deltas, *new_ms, *new_vs)
```

```python
import functools

import numpy as np
import jax
import jax.numpy as jnp
from jax import lax
from jax.experimental import pallas as pl
from jax.experimental.pallas import tpu as pltpu

F32 = jnp.float32
MXU_DTYPE = jnp.bfloat16
WIRE_DTYPE = jnp.bfloat16

N_DEV = 8
N_META = 16
CHUNK = 128
PAD_ROWS = CHUNK - N_META
N_HEADS = 8
HEAD_DIM = 128
GROUP = N_HEADS * HEAD_DIM
IN_DIM = 7 * GROUP + N_HEADS
WIN_SHARD = IN_DIM // N_DEV
WIN_ROWS = 912
WIN_BLOCK = 1024
WIN_STRIDE = 896
WIN_N = 7680
ROPE_BASE = 10000.0
NORM_EPS = 1e-6
NEG_BIG = -1e30
ADAM_LR, ADAM_B1, ADAM_B2, ADAM_EPS, ADAM_WD, ADAM_STEP = 0.001, 0.9, 0.999, 1e-08, 0.01, 10
VMEM_LIMIT = 52 * 1024 * 1024
MESH = pl.DeviceIdType.MESH
ANY = pl.BlockSpec(memory_space=pl.ANY)
VMEM_SPEC = pl.BlockSpec(memory_space=pltpu.VMEM)


def _params(sem=None):
    kw = {"vmem_limit_bytes": VMEM_LIMIT}
    if sem is not None:
        kw["dimension_semantics"] = sem
    return pltpu.CompilerParams(**kw)


def _divisor_tile(n, cap, unit):
    if n <= cap:
        return n
    best = None
    for t in range(unit, cap + 1, unit):
        if n % t == 0:
            best = t
    assert best is not None, (n, cap, unit)
    return best


def _my_position():
    return lax.axis_index("x"), lax.axis_index("y"), lax.axis_index("c")


def _device_index():
    x, y, c = _my_position()
    return 4 * x + 2 * y + c


def _all_gather(shard, name):
    r, c = shard.shape

    def body(x_ref, out_ref, send_sems, recv_sems, local_sem):
        mx, my, mc = _my_position()
        me, sibling = (mx, my, mc), (mx, my, 1 - mc)
        chips = [(1 - mx, my), (mx, 1 - my), (1 - mx, 1 - my)]

        def slot(px, py, pc):
            return out_ref.at[4 * px + 2 * py + pc]

        def copy(k, block, to, src=None):
            return pltpu.make_async_remote_copy(
                src_ref=slot(*block) if src is None else src, dst_ref=slot(*block),
                send_sem=send_sems.at[k], recv_sem=recv_sems.at[k], device_id=to, device_id_type=MESH)

        mine = pltpu.make_async_copy(x_ref, slot(*me), local_sem)
        mine.start()
        first = [copy(0, me, sibling, src=x_ref)]
        first += [copy(1 + j, me, (*chip, mc), src=x_ref) for j, chip in enumerate(chips)]
        for cp in first:
            cp.start()
        passed = [copy(4 + j, (*chip, mc), sibling) for j, chip in enumerate(chips)]
        for j, chip in enumerate(chips):
            copy(1 + j, (*chip, mc), me).wait_recv()
            passed[j].start()
        copy(0, sibling, me).wait_recv()
        for j, chip in enumerate(chips):
            copy(4 + j, (*chip, 1 - mc), me).wait_recv()
        for cp in first + passed:
            cp.wait_send()
        mine.wait()

    return pl.pallas_call(
        body, name=name,
        out_shape=jax.ShapeDtypeStruct((N_DEV, r, c), shard.dtype),
        in_specs=[ANY], out_specs=ANY,
        scratch_shapes=[pltpu.SemaphoreType.DMA((7,)), pltpu.SemaphoreType.DMA((7,)), pltpu.SemaphoreType.DMA],
    )(shard)


HBM_SPEC = pl.BlockSpec(memory_space=pltpu.HBM)
SEM_SPEC = pl.BlockSpec(memory_space=pltpu.SEMAPHORE)
DATAFLOW_EFFECT = pltpu.SideEffectType.DATAFLOW_SIDE_EFFECTING


def _in_hbm(a):
    return pltpu.with_memory_space_constraint(a, pltpu.HBM)


def _split_start(src, land, make_copies, n_copies, after, name):
    if isinstance(land, tuple):
        land = lax.empty(land, src.dtype)
    land_shape = land.shape
    def body(src_ref, land_ref, after_ref, send_sems, recv_sems, src_thru, land_thru, token):
        for cp in make_copies(src_ref, land_ref, send_sems, recv_sems):
            cp.start()
        token[...] = jnp.zeros_like(token)

    return pl.pallas_call(
        body, name=name,
        out_shape=(pltpu.SemaphoreType.DMA((n_copies,)), pltpu.SemaphoreType.DMA((n_copies,)),
                   pltpu.HBM(src.shape, src.dtype), pltpu.HBM(land_shape, land.dtype),
                   jax.ShapeDtypeStruct((8, 128), F32)),
        in_specs=(HBM_SPEC, HBM_SPEC, ANY), out_specs=(SEM_SPEC, SEM_SPEC, HBM_SPEC, HBM_SPEC, VMEM_SPEC),
        input_output_aliases={0: 2, 1: 3},
        compiler_params=pltpu.CompilerParams(has_side_effects=DATAFLOW_EFFECT),
    )(_in_hbm(src), _in_hbm(land), after)


def _split_wait(started, after, make_copies, name):
    send_sems, recv_sems, src_thru, land_thru, _ = started

    def body(src_ref, land_ref, send_sems_ref, recv_sems_ref, after_ref, src_dead, land_out):
        for cp in make_copies(src_ref, land_ref, send_sems_ref, recv_sems_ref):
            cp.wait_send()
            cp.wait_recv()

    return pl.pallas_call(
        body, name=name,
        out_shape=(pltpu.HBM(src_thru.shape, src_thru.dtype), pltpu.HBM(land_thru.shape, land_thru.dtype)),
        in_specs=(HBM_SPEC, HBM_SPEC, SEM_SPEC, SEM_SPEC, ANY), out_specs=(HBM_SPEC, HBM_SPEC),
        input_output_aliases={0: 0, 1: 1},
        compiler_params=pltpu.CompilerParams(has_side_effects=DATAFLOW_EFFECT),
    )(src_thru, land_thru, send_sems, recv_sems, after)


def _gather_copies(x_ref, land_ref, send_sems, recv_sems):
    mx, my, mc = _my_position()
    me = 4 * mx + 2 * my + mc
    targets = [(mx, my, 1 - mc), (1 - mx, my, mc), (mx, 1 - my, mc), (1 - mx, 1 - my, mc)]
    return [pltpu.make_async_remote_copy(
        src_ref=land_ref.at[me], dst_ref=land_ref.at[me], send_sem=send_sems.at[k], recv_sem=recv_sems.at[k],
        device_id=t, device_id_type=MESH) for k, t in enumerate(targets)]


def _own_slot(shard, dev, name):
    r, c = shard.shape
    tr = _divisor_tile(r, 512, 16)

    def body(s_ref, x_ref, o_ref):
        o_ref[...] = x_ref[...].astype(o_ref.dtype)

    return pl.pallas_call(
        body, name=name,
        out_shape=jax.ShapeDtypeStruct((N_DEV, r, c), WIRE_DTYPE),
        grid_spec=pltpu.PrefetchScalarGridSpec(
            num_scalar_prefetch=1, grid=(r // tr,),
            in_specs=[pl.BlockSpec((tr, c), lambda i, s: (i, 0))],
            out_specs=pl.BlockSpec((None, tr, c), lambda i, s: (s[0], i, 0))),
        compiler_params=_params(("parallel",)),
    )(dev, shard)


def _gather_start(shard, dev, after, name):
    return _split_start(jnp.zeros((8, 128), F32), _own_slot(shard, dev, name + "_own"), _gather_copies, 4, after, name)


def _gather_ring(shard, dev, after, name):
    r, c = shard.shape
    half = r // 2
    assert half % 16 == 0

    def body(x_ref, after_ref, land_in, land_ref, send_sems, recv_sems):
        mx, my, mc = _my_position()
        sibling, x_nbr, y_nbr = (mx, my, 1 - mc), (1 - mx, my, mc), (mx, 1 - my, mc)
        first, second = pl.ds(0, half), pl.ds(half, half)

        def slot(px, py, pc):
            return land_ref.at[4 * px + 2 * py + pc]

        def copy(k, src, dst, to):
            return pltpu.make_async_remote_copy(src_ref=src, dst_ref=dst, send_sem=send_sems.at[k],
                                                recv_sem=recv_sems.at[k], device_id=to, device_id_type=MESH)

        def arrived(k, dst):
            copy(k, dst, dst, sibling).wait_recv()

        mine = slot(mx, my, mc)
        from_x, from_y, from_d = slot(1 - mx, my, mc), slot(mx, 1 - my, mc), slot(1 - mx, 1 - my, mc)
        sent = [copy(0, x_ref, mine, sibling), copy(1, x_ref, mine, x_nbr), copy(2, x_ref, mine, y_nbr)]
        for cp in sent:
            cp.start()

        def send(k, src, to):
            cp = copy(k, src, src, to)
            cp.start()
            sent.append(cp)

        arrived(1, from_x)
        send(3, from_x.at[first], y_nbr)
        send(5, from_x, sibling)
        arrived(2, from_y)
        send(4, from_y.at[second], x_nbr)
        send(6, from_y, sibling)
        arrived(3, from_d.at[first])
        send(7, from_d.at[first], sibling)
        arrived(4, from_d.at[second])
        send(8, from_d.at[second], sibling)
        arrived(0, slot(mx, my, 1 - mc))
        arrived(5, slot(1 - mx, my, 1 - mc))
        arrived(6, slot(mx, 1 - my, 1 - mc))
        arrived(7, slot(1 - mx, 1 - my, 1 - mc).at[first])
        arrived(8, slot(1 - mx, 1 - my, 1 - mc).at[second])
        for cp in sent:
            cp.wait_send()

    land = _own_slot(shard, dev, name + "_own")
    return pl.pallas_call(
        body, name=name,
        out_shape=jax.ShapeDtypeStruct(land.shape, land.dtype),
        in_specs=[ANY, ANY, ANY], out_specs=ANY,
        input_output_aliases={2: 0},
        scratch_shapes=[pltpu.SemaphoreType.DMA((9,)), pltpu.SemaphoreType.DMA((9,))],
    )(shard, after, land)


def _gather_finish(started, after, name):
    _, land = _split_wait(started, after, _gather_copies, name + "_wait")

    def body(land_in, land_ref, send_sems, recv_sems):
        mx, my, mc = _my_position()
        chips = [(1 - mx, my), (mx, 1 - my), (1 - mx, 1 - my)]
        copies = [pltpu.make_async_remote_copy(
            src_ref=land_ref.at[4 * cx + 2 * cy + mc], dst_ref=land_ref.at[4 * cx + 2 * cy + mc],
            send_sem=send_sems.at[j], recv_sem=recv_sems.at[j],
            device_id=(mx, my, 1 - mc), device_id_type=MESH) for j, (cx, cy) in enumerate(chips)]
        for cp in copies:
            cp.start()
        for j, (cx, cy) in enumerate(chips):
            copies[j].wait_send()
            pltpu.make_async_remote_copy(
                src_ref=land_ref.at[4 * cx + 2 * cy + 1 - mc], dst_ref=land_ref.at[4 * cx + 2 * cy + 1 - mc],
                send_sem=send_sems.at[j], recv_sem=recv_sems.at[j],
                device_id=(mx, my, 1 - mc), device_id_type=MESH).wait_recv()

    return pl.pallas_call(
        body, name=name + "_pass",
        out_shape=jax.ShapeDtypeStruct(land.shape, land.dtype),
        in_specs=[ANY], out_specs=ANY,
        input_output_aliases={0: 0},
        scratch_shapes=[pltpu.SemaphoreType.DMA((3,)), pltpu.SemaphoreType.DMA((3,))],
    )(land)


def _chip_copies(p_ref, land_ref, send_sems, recv_sems):
    mx, my, mc = _my_position()
    chips = [(1 - mx, my), (mx, 1 - my), (1 - mx, 1 - my)]
    return [pltpu.make_async_remote_copy(
        src_ref=p_ref.at[2 * cx + cy], dst_ref=land_ref.at[j], send_sem=send_sems.at[j], recv_sem=recv_sems.at[j],
        device_id=(cx, cy, mc), device_id_type=MESH) for j, (cx, cy) in enumerate(chips)]


def _reduce_scatter_start(g, core, name):
    pair = _pair_sum(g, _exchange_sibling(g, name + "_d2d"), core, name + "_pairsum")
    return _split_start(pair, (3,) + pair.shape[1:], _chip_copies, 3, g, name + "_ici_start")


def _sibling_copies(g_ref, land_ref, send_sems, recv_sems):
    mx, my, mc = _my_position()
    return [pltpu.make_async_remote_copy(
        src_ref=g_ref.at[2 * k + (1 - mc)], dst_ref=land_ref.at[k], send_sem=send_sems.at[k], recv_sem=recv_sems.at[k],
        device_id=(mx, my, 1 - mc), device_id_type=MESH) for k in range(4)]


def _reduce_scatter_d2d_start(g, after, name):
    return _split_start(g, (4,) + g.shape[1:], _sibling_copies, 4, after, name + "_d2d_start")


def _reduce_scatter_ici_start(d2d_started, after, core, name):
    g, from_sibling = _split_wait(d2d_started, after, _sibling_copies, name + "_d2d_wait")
    pair = _pair_sum(g, from_sibling, core, name + "_pairsum")
    return _split_start(pair, (3,) + pair.shape[1:], _chip_copies, 3, g, name + "_ici_start")


def _reduce_scatter_finish(started, after, chip, w, m, v, name):
    pair, from_chips = _split_wait(started, after, _chip_copies, name + "_ici_wait")
    return _final_sum_adamw(pair, from_chips, chip, w, m, v, name + "_sum_adamw")


def _exchange_sibling(g, name):
    _, r, c = g.shape

    def body(g_ref, out_ref, send_sems, recv_sems):
        mx, my, mc = _my_position()
        copies = [
            pltpu.make_async_remote_copy(
                src_ref=g_ref.at[2 * k + (1 - mc)], dst_ref=out_ref.at[k],
                send_sem=send_sems.at[k], recv_sem=recv_sems.at[k],
                device_id=(mx, my, 1 - mc), device_id_type=MESH)
            for k in range(4)]
        for cp in copies:
            cp.start()
        for cp in copies:
            cp.wait()

    return pl.pallas_call(
        body, name=name,
        out_shape=jax.ShapeDtypeStruct((4, r, c), g.dtype),
        in_specs=[ANY], out_specs=ANY,
        scratch_shapes=[pltpu.SemaphoreType.DMA((4,)), pltpu.SemaphoreType.DMA((4,))],
    )(g)


def _pair_sum(g, recv, core, name):
    _, r, c = g.shape
    tr = _divisor_tile(r, 512, 16)

    def body(s_ref, g_ref, r_ref, o_ref):
        o_ref[...] = (g_ref[...].astype(F32) + r_ref[...].astype(F32)).astype(o_ref.dtype)

    return pl.pallas_call(
        body, name=name,
        out_shape=jax.ShapeDtypeStruct((4, r, c), g.dtype),
        grid_spec=pltpu.PrefetchScalarGridSpec(
            num_scalar_prefetch=1, grid=(4, r // tr),
            in_specs=[pl.BlockSpec((None, tr, c), lambda k, i, s: (2 * k + s[0], i, 0)),
                      pl.BlockSpec((None, tr, c), lambda k, i, s: (k, i, 0))],
            out_specs=pl.BlockSpec((None, tr, c), lambda k, i, s: (k, i, 0))),
        compiler_params=_params(("parallel", "parallel")),
    )(core, g, recv)


def _adamw_math(w, g, m, v):
    nm = ADAM_B1 * m + (1.0 - ADAM_B1) * g
    nv = ADAM_B2 * v + (1.0 - ADAM_B2) * (g * g)
    m_hat = nm / (1.0 - ADAM_B1 ** ADAM_STEP)
    v_hat = nv / (1.0 - ADAM_B2 ** ADAM_STEP)
    return -ADAM_LR * (m_hat / (jnp.sqrt(v_hat) + ADAM_EPS) + ADAM_WD * w), nm, nv


def _final_sum_adamw(p, recv, chip, w, m, v, name):
    _, r, c = p.shape
    tr = _divisor_tile(r, 256, 16)
    tile = lambda: pl.BlockSpec((tr, c), lambda i, s: (i, 0))

    def body(s_ref, p_ref, r_ref, w_ref, m_ref, v_ref, g_ref, d_ref, nm_ref, nv_ref):
        g = p_ref[...].astype(F32)
        for j in range(3):
            g = g + r_ref[j].astype(F32)
        g_ref[...] = g
        d_ref[...], nm_ref[...], nv_ref[...] = _adamw_math(w_ref[...], g, m_ref[...], v_ref[...])

    sds = jax.ShapeDtypeStruct((r, c), F32)
    return pl.pallas_call(
        body, name=name,
        out_shape=(sds, sds, sds, sds),
        grid_spec=pltpu.PrefetchScalarGridSpec(
            num_scalar_prefetch=1, grid=(r // tr,),
            in_specs=[pl.BlockSpec((None, tr, c), lambda i, s: (s[0], i, 0)),
                      pl.BlockSpec((3, tr, c), lambda i, s: (0, i, 0)), tile(), tile(), tile()],
            out_specs=(tile(), tile(), tile(), tile())),
        compiler_params=_params(("parallel",)),
    )(chip, p, recv, w, m, v)


def _all_to_all_copies(v_ref, land_ref, send_sems, recv_sems):
    mx, my, mc = _my_position()
    me = 4 * mx + 2 * my + mc
    copies = []
    for rel in range(1, N_DEV):
        bx, by, bc = (rel >> 2) & 1, (rel >> 1) & 1, rel & 1
        target = (1 - mx if bx else mx, 1 - my if by else my, 1 - mc if bc else mc)
        copies.append(pltpu.make_async_remote_copy(
            src_ref=v_ref, dst_ref=land_ref.at[me], send_sem=send_sems.at[rel - 1], recv_sem=recv_sems.at[rel - 1],
            device_id=target, device_id_type=MESH))
    return copies


def _small_all_reduce_start(v, after, name):
    return _split_start(v, (N_DEV,) + v.shape, _all_to_all_copies, N_DEV - 1, after, name + "_start")


def _small_all_reduce_finish(started, after, dev, name):
    v, land = _split_wait(started, after, _all_to_all_copies, name + "_wait")
    rows = v.shape[0]

    def body(me_ref, v_ref, land_ref, o_ref):
        for j in range(N_DEV):
            @pl.when(me_ref[0] == j)
            def _():
                o_ref[...] = v_ref[...] if j == 0 else o_ref[...] + v_ref[...]

            @pl.when(me_ref[0] != j)
            def _():
                o_ref[...] = land_ref[j] if j == 0 else o_ref[...] + land_ref[j]

    return pl.pallas_call(
        body, name=name + "_sum",
        out_shape=jax.ShapeDtypeStruct((rows, 128), F32),
        grid_spec=pltpu.PrefetchScalarGridSpec(
            num_scalar_prefetch=1, grid=(1,),
            in_specs=[pl.BlockSpec((rows, 128), lambda i, s: (0, 0)),
                      pl.BlockSpec((N_DEV, rows, 128), lambda i, s: (0, 0, 0))],
            out_specs=pl.BlockSpec((rows, 128), lambda i, s: (0, 0))),
        compiler_params=_params(("arbitrary",)),
    )(dev, v, land)


def _assemble_w_in(blocks):
    rows, d = WIN_ROWS, blocks.shape[2]
    tc = _divisor_tile(d, 256, 128)
    n_tiles = WIN_N // 128
    last = (N_DEV * WIN_STRIDE) // 128

    def body(b_ref, o_ref):
        win = []
        for i in range(N_DEV):
            w = jnp.concatenate([b_ref[i].astype(F32), jnp.zeros((WIN_BLOCK - rows, tc), F32)], axis=0)
            win.append(pltpu.roll(w, i, 0) if i else w)
        for t in range(n_tiles):
            if t > last:
                o_ref[t * 128:(t + 1) * 128, :] = jnp.zeros((128, tc), o_ref.dtype)
                continue
            i = min(t // 7, N_DEV - 1)
            k = t - 7 * i
            val = win[i][k * 128:(k + 1) * 128, :]
            if k == 0 and i >= 1:
                val = val + win[i - 1][7 * 128:8 * 128, :]
            o_ref[t * 128:(t + 1) * 128, :] = val.astype(o_ref.dtype)

    return pl.pallas_call(
        body, name="assemble_w_in",
        out_shape=jax.ShapeDtypeStruct((WIN_N, d), blocks.dtype),
        grid=(d // tc,),
        in_specs=[pl.BlockSpec((N_DEV, rows, tc), lambda j: (0, 0, j))],
        out_specs=pl.BlockSpec((WIN_N, tc), lambda j: (0, j)),
        compiler_params=_params(("parallel",)),
    )(blocks)


def _extract_w_in_windows(g):
    _, d = g.shape
    tc = _divisor_tile(d, 256, 128)

    def body(g_ref, o_ref):
        for j in range(N_DEV):
            w = g_ref[WIN_STRIDE * j:WIN_STRIDE * j + WIN_BLOCK, :].astype(F32)
            w = pltpu.roll(w, WIN_BLOCK - j, 0) if j else w
            o_ref[j] = w[0:WIN_ROWS, :].astype(o_ref.dtype)

    return pl.pallas_call(
        body, name="extract_w_in_windows",
        out_shape=jax.ShapeDtypeStruct((N_DEV, WIN_ROWS, d), g.dtype),
        grid=(d // tc,),
        in_specs=[pl.BlockSpec((WIN_N, tc), lambda j: (0, j))],
        out_specs=pl.BlockSpec((N_DEV, WIN_ROWS, tc), lambda j: (0, 0, j)),
        compiler_params=_params(("parallel",)),
    )(g)


def _mm(a, b, *, a_spec, b_spec, o_spec, out_shape, grid, contract, nk, name, after=None):
    dn = (((contract[0],), (contract[1],)), ((), ()))
    tm, tn = o_spec.block_shape[-2:]
    behind = [] if after is None else [after]

    def body(a_ref, b_ref, *rest):
        o_ref, *scratch = rest[len(behind):]
        part = lax.dot_general(a_ref[...], b_ref[...], dn, preferred_element_type=F32)
        if nk == 1:
            o_ref[...] = part.astype(o_ref.dtype)
            return
        acc = scratch[0]
        k = pl.program_id(2)

        @pl.when(k == 0)
        def _():
            acc[...] = part

        @pl.when(k > 0)
        def _():
            acc[...] += part

        @pl.when(k == nk - 1)
        def _():
            o_ref[...] = acc[...].astype(o_ref.dtype)

    return pl.pallas_call(
        body, name=name, out_shape=out_shape, grid=grid,
        in_specs=[a_spec, b_spec] + [ANY] * len(behind), out_specs=o_spec,
        scratch_shapes=[] if nk == 1 else [pltpu.VMEM((tm, tn), F32)],
        compiler_params=_params(("parallel", "parallel", "arbitrary")),
    )(a, b, *behind)


def _mm_nn(a, b, out_dtype, name, tm_cap=1088, tn_cap=512, tk_cap=2048, after=None):
    m, k = a.shape
    _, n = b.shape
    tm, tn, tk = _divisor_tile(m, tm_cap, 16), _divisor_tile(n, tn_cap, 128), _divisor_tile(k, tk_cap, 128)
    return _mm(a, b,
               a_spec=pl.BlockSpec((tm, tk), lambda i, j, kk: (i, kk)),
               b_spec=pl.BlockSpec((tk, tn), lambda i, j, kk: (kk, j)),
               o_spec=pl.BlockSpec((tm, tn), lambda i, j, kk: (i, j)),
               out_shape=jax.ShapeDtypeStruct((m, n), out_dtype),
               grid=(m // tm, n // tn, k // tk), contract=(1, 0), nk=k // tk, name=name, after=after)


def _mm_nt(a, b, out_dtype, name, tm_cap=1088, tn_cap=512, tk_cap=2048, after=None):
    m, k = a.shape
    n, _ = b.shape
    tm, tn, tk = _divisor_tile(m, tm_cap, 16), _divisor_tile(n, tn_cap, 128), _divisor_tile(k, tk_cap, 128)
    return _mm(a, b,
               a_spec=pl.BlockSpec((tm, tk), lambda i, j, kk: (i, kk)),
               b_spec=pl.BlockSpec((tn, tk), lambda i, j, kk: (j, kk)),
               o_spec=pl.BlockSpec((tm, tn), lambda i, j, kk: (i, j)),
               out_shape=jax.ShapeDtypeStruct((m, n), out_dtype),
               grid=(m // tm, n // tn, k // tk), contract=(1, 1), nk=k // tk, name=name, after=after)


def _mm_tn(a, b, out_dtype, name, tm_cap=1024, tn_cap=512, after=None):
    l, m = a.shape
    _, n = b.shape
    tm, tn = _divisor_tile(m, tm_cap, 128), _divisor_tile(n, tn_cap, 128)
    return _mm(a, b,
               a_spec=pl.BlockSpec((l, tm), lambda i, j, kk: (0, i)),
               b_spec=pl.BlockSpec((l, tn), lambda i, j, kk: (0, j)),
               o_spec=pl.BlockSpec((tm, tn), lambda i, j, kk: (i, j)),
               out_shape=jax.ShapeDtypeStruct((m, n), out_dtype),
               grid=(m // tm, n // tn, 1), contract=(0, 0), nk=1, name=name, after=after)


def _pair_split(shard):
    left = shard % ATTN_BLOCK
    assert left in (0, CHUNK) and shard > left
    return shard - left, left


def _mm_up(cn, w_up_blocks, after):
    l, d = cn.shape
    n, _, shard = w_up_blocks.shape
    main, left = _pair_split(shard)
    tm = _divisor_tile(l, 544, 16)

    def body(a_ref, b_ref, after_ref, o_ref):
        a = a_ref[...]
        for s in range(2):
            o_ref[:, s * shard:s * shard + main] = _dot(a, b_ref[s, :, 0:main])
        if left:
            tail = _dot(a, jnp.concatenate([b_ref[0, :, main:], b_ref[1, :, main:]], axis=1))
            o_ref[:, main:shard] = tail[:, 0:left]
            o_ref[:, shard + main:2 * shard] = tail[:, left:]

    return pl.pallas_call(
        body, name="mm_up", out_shape=jax.ShapeDtypeStruct((l, n * shard), F32), grid=(l // tm, n // 2),
        in_specs=[pl.BlockSpec((tm, d), lambda i, j: (i, 0)),
                  pl.BlockSpec((2, d, shard), lambda i, j: (j, 0, 0)), ANY],
        out_specs=pl.BlockSpec((tm, 2 * shard), lambda i, j: (i, j)),
        compiler_params=_params(("parallel", "parallel")),
    )(cn, w_up_blocks, after)


def _mm_gw_up(cn, d_u):
    l, d = cn.shape
    _, _, d_ff = d_u.shape
    shard = 2 * d_ff // N_DEV
    pairs_per_half = d_ff // (2 * shard)
    tm = _divisor_tile(d, 512, 128)

    def body(a_ref, b_ref, o_ref):
        res = _dot_tn(a_ref[...], b_ref[...])
        o_ref[0] = res[:, 0:shard].astype(o_ref.dtype)
        o_ref[1] = res[:, shard:].astype(o_ref.dtype)

    return pl.pallas_call(
        body, name="mm_gw_up", out_shape=jax.ShapeDtypeStruct((N_DEV, d, shard), WIRE_DTYPE),
        grid=(d // tm, N_DEV // 2),
        in_specs=[pl.BlockSpec((l, tm), lambda i, j: (0, i)),
                  pl.BlockSpec((None, l, 2 * shard), lambda i, j: (j // pairs_per_half, 0, j % pairs_per_half))],
        out_specs=pl.BlockSpec((2, tm, shard), lambda i, j: (j, i, 0)),
        compiler_params=_params(("parallel", "parallel")),
    )(cn, d_u)


def _mm_d_cn(d_u, w_up_blocks, after):
    _, l, d_ff = d_u.shape
    n, d, shard = w_up_blocks.shape
    per = d_ff // shard
    main, left = _pair_split(shard)
    tm, tn = _divisor_tile(l, 544, 16), _divisor_tile(d, 256, 128)

    def body(a_ref, b_ref, after_ref, o_ref):
        acc = None
        for k in range(0, n, 2):
            half, c0 = k // per, (k % per) * shard
            parts = [_dot_nt(a_ref[half, :, c0 + s * shard:c0 + s * shard + main], b_ref[k + s, :, 0:main])
                     for s in range(2)]
            if left:
                a_tail = jnp.concatenate([a_ref[half, :, c0 + s * shard + main:c0 + (s + 1) * shard] for s in range(2)],
                                         axis=1)
                b_tail = jnp.concatenate([b_ref[k + s, :, main:] for s in range(2)], axis=1)
                parts.append(_dot_nt(a_tail, b_tail))
            for part in parts:
                acc = part if acc is None else acc + part
        o_ref[...] = acc

    return pl.pallas_call(
        body, name="mm_d_cn", out_shape=jax.ShapeDtypeStruct((l, d), F32), grid=(l // tm, d // tn),
        in_specs=[pl.BlockSpec((2, tm, d_ff), lambda i, j: (0, i, 0)),
                  pl.BlockSpec((n, tn, shard), lambda i, j: (0, j, 0)), ANY],
        out_specs=pl.BlockSpec((tm, tn), lambda i, j: (i, j)),
        compiler_params=_params(("parallel", "parallel")),
    )(d_u, w_up_blocks, after)


def _row_tile(l):
    return _divisor_tile(l, 544, 8)


def _rmsnorm_fwd(h, gain, name, res=None):
    l, d = h.shape
    tr = _row_tile(l)
    row = pl.BlockSpec((tr, d), lambda i: (i, 0))
    vec = pl.BlockSpec((1, d), lambda i: (0, 0))

    def body(*refs):
        if res is None:
            h_ref, g_ref, n_ref = refs
            x = h_ref[...]
        else:
            h_ref, r_ref, g_ref, s_ref, n_ref = refs
            x = h_ref[...] + r_ref[...]
            s_ref[...] = x
        y = x * lax.rsqrt(jnp.mean(x * x, axis=-1, keepdims=True) + NORM_EPS)
        n_ref[...] = (y * g_ref[...]).astype(n_ref.dtype)

    normed = jax.ShapeDtypeStruct((l, d), MXU_DTYPE)
    if res is None:
        return pl.pallas_call(body, name=name, out_shape=normed, grid=(l // tr,), in_specs=[row, vec],
                              out_specs=row, compiler_params=_params(("parallel",)))(h, gain)
    return pl.pallas_call(body, name=name, out_shape=(jax.ShapeDtypeStruct((l, d), F32), normed),
                          grid=(l // tr,), in_specs=[row, row, vec], out_specs=(row, row),
                          compiler_params=_params(("parallel",)))(h, res, gain)


def _rmsnorm_bwd(d_res, d_normed, x, gain, name, with_mxu_copy):
    l, d = x.shape
    tr = _row_tile(l) if with_mxu_copy else CHUNK
    row = pl.BlockSpec((tr, d), lambda i: (i, 0))
    vec = pl.BlockSpec((1, d), lambda i: (0, 0))

    def body(dres_ref, dn_ref, x_ref, g_ref, dx_ref, other_ref, dg_ref):
        i = pl.program_id(0)
        xv = x_ref[...]
        r = lax.rsqrt(jnp.mean(xv * xv, axis=-1, keepdims=True) + NORM_EPS)
        xh = xv * r
        dn = dn_ref[...]
        dxh = dn * g_ref[...]
        dx = dres_ref[...] + r * (dxh - xh * jnp.mean(dxh * xh, axis=-1, keepdims=True))
        if with_mxu_copy:
            dx_ref[...] = dx
            other_ref[...] = dx.astype(MXU_DTYPE)
        else:
            @pl.when(i == 0)
            def _():
                dx_ref[...] = dx

            @pl.when(i > 0)
            def _():
                other_ref[...] = dx

        @pl.when(i == 0)
        def _():
            dg_ref[...] = jnp.zeros_like(dg_ref)

        dg_ref[...] += jnp.sum(dn * xh, axis=0, keepdims=True)

    if with_mxu_copy:
        outs = [jax.ShapeDtypeStruct((l, d), F32), jax.ShapeDtypeStruct((l, d), MXU_DTYPE)]
        specs = [row, row]
    else:
        outs = [jax.ShapeDtypeStruct((CHUNK, d), F32), jax.ShapeDtypeStruct((l - CHUNK, d), F32)]
        specs = [pl.BlockSpec((CHUNK, d), lambda i: (0, 0)), pl.BlockSpec((CHUNK, d), lambda i: (jnp.maximum(i - 1, 0), 0))]
    outs.append(jax.ShapeDtypeStruct((1, d), F32))
    specs.append(vec)
    return pl.pallas_call(body, name=name, out_shape=tuple(outs), grid=(l // tr,),
                          in_specs=[row, row, row, vec], out_specs=tuple(specs),
                          compiler_params=_params(("arbitrary",)))(d_res, d_normed, x, gain)


def _loss_head(h1, mlp_out, gain, target):
    l, d = h1.shape
    n_blocks = l // CHUNK
    row = pl.BlockSpec((CHUNK, d), lambda i: (i, 0))
    vec = pl.BlockSpec((1, d), lambda i: (0, 0))
    tgt = pl.BlockSpec((CHUNK, d), lambda i: (jnp.maximum(i - 1, 0), 0))

    def body(h_ref, m_ref, g_ref, t_ref, dh_ref, dhb_ref, dg_ref, loss_ref, sq_ref):
        i = pl.program_id(0)
        x = h_ref[...] + m_ref[...]
        r = lax.rsqrt(jnp.mean(x * x, axis=-1, keepdims=True) + NORM_EPS)
        xh = x * r
        g = g_ref[...]
        real = i >= 1
        err = jnp.where(real, xh * g - t_ref[...], 0.0)
        dy = err * (1.0 / d)
        dxh = dy * g
        dh = r * (dxh - xh * jnp.mean(dxh * xh, axis=-1, keepdims=True))
        dh_ref[...] = dh
        dhb_ref[...] = dh.astype(MXU_DTYPE)

        @pl.when(i == 0)
        def _():
            dg_ref[...] = jnp.zeros_like(dg_ref)
            sq_ref[...] = jnp.zeros_like(sq_ref)

        dg_ref[...] += jnp.sum(dy * xh, axis=0, keepdims=True)
        sq_ref[...] += jnp.sum(err * err, axis=0, keepdims=True)

        @pl.when(i == n_blocks - 1)
        def _():
            total = jnp.sum(sq_ref[...], axis=-1, keepdims=True) * (0.5 / d)
            loss_ref[...] = jnp.broadcast_to(total, (1, 128))

    return pl.pallas_call(
        body, name="loss_head",
        out_shape=(jax.ShapeDtypeStruct((l, d), F32), jax.ShapeDtypeStruct((l, d), MXU_DTYPE),
                   jax.ShapeDtypeStruct((1, d), F32), jax.ShapeDtypeStruct((1, 128), F32)),
        grid=(n_blocks,), in_specs=[row, row, vec, tgt],
        out_specs=(row, row, vec, pl.BlockSpec((1, 128), lambda i: (0, 0))),
        scratch_shapes=[pltpu.VMEM((1, d), F32)],
        compiler_params=_params(("arbitrary",)),
    )(h1, mlp_out, gain, target)


def _dot(a, b):
    return jnp.dot(a, b, preferred_element_type=F32)


def _dot_nt(a, b):
    return lax.dot_general(a, b, (((1,), (1,)), ((), ())), preferred_element_type=F32)


def _dot_tn(a, b):
    return lax.dot_general(a, b, (((0,), (0,)), ((), ())), preferred_element_type=F32)


def _rope(t, cos2, sin2):
    return t * cos2 + pltpu.roll(t, HEAD_DIM // 2, 1) * sin2


def _rope_bwd(dr, cos2, sin2):
    return dr * cos2 + pltpu.roll(dr * sin2, HEAD_DIM // 2, 1)


def _sigmoid(x):
    return 1.0 / (1.0 + jnp.exp(-x))


def _row_valid(block, rows):
    r = block * CHUNK + lax.broadcasted_iota(jnp.int32, (rows, 1), 0)
    return r >= PAD_ROWS


def _retention_consts(l):
    pos = jnp.arange(l, dtype=F32) - PAD_ROWS
    inv_freq = 1.0 / (ROPE_BASE ** (jnp.arange(0, HEAD_DIM, 2, dtype=F32) / HEAD_DIM))
    ang = pos[:, None] * inv_freq[None, :]
    cos, sin = jnp.cos(ang), jnp.sin(ang)
    cos2 = jnp.concatenate([cos, cos], axis=-1)
    sin2 = jnp.concatenate([-sin, sin], axis=-1)
    log_g = jnp.log1p(-jnp.exp2(-5.0 - jnp.arange(N_HEADS, dtype=F32)))
    idx = jnp.arange(CHUNK, dtype=F32)
    diff = idx[:, None] - idx[None, :]
    decay = jnp.where(diff >= 0, jnp.exp(jnp.maximum(diff, 0.0)[None] * log_g[:, None, None]), 0.0)
    xi = jnp.exp((idx + 1.0)[None, :] * log_g[:, None])
    zeta = jnp.exp((CHUNK - 1.0 - idx)[None, :] * log_g[:, None])
    g_chunk = jnp.exp(CHUNK * log_g)
    bcast = lambda v: jnp.broadcast_to(v[:, :, None], (N_HEADS, CHUNK, HEAD_DIM))
    g_rows = jnp.broadcast_to(g_chunk[:, None, None], (N_HEADS, 8, HEAD_DIM))
    return cos2, sin2, decay, bcast(xi), bcast(zeta), g_rows


def _retention_fwd(proj, ret_gain, consts):
    l = proj.shape[0]
    n_chunks = l // CHUNK
    cos2, sin2, decay, xi, zeta, g_rows = consts
    scale = HEAD_DIM ** -0.5

    def body(p_ref, cos_ref, sin_ref, dec_ref, xi_ref, zeta_ref, gr_ref, gain_ref,
             mix_ref, o_ref, st_ref, state):
        c = pl.program_id(0)

        @pl.when(c == 0)
        def _():
            state[...] = jnp.zeros_like(state)

        cos_v, sin_v = cos_ref[...], sin_ref[...]
        valid = _row_valid(c, CHUNK)
        for h in range(N_HEADS):
            cols = slice(h * HEAD_DIM, (h + 1) * HEAD_DIM)
            q = p_ref[:, h * HEAD_DIM:(h + 1) * HEAD_DIM]
            k = p_ref[:, GROUP + h * HEAD_DIM:GROUP + (h + 1) * HEAD_DIM]
            v = p_ref[:, 2 * GROUP + h * HEAD_DIM:2 * GROUP + (h + 1) * HEAD_DIM]
            g = p_ref[:, 3 * GROUP + h * HEAD_DIM:3 * GROUP + (h + 1) * HEAD_DIM]
            rq = _rope(q, cos_v, sin_v).astype(MXU_DTYPE)
            rk = _rope(k, cos_v, sin_v) * scale
            rkb = rk.astype(MXU_DTYPE)
            vb = v.astype(MXU_DTYPE)
            st = state[h]
            st_ref[h] = st
            s = _dot_nt(rq, rkb) * dec_ref[h]
            o = _dot(s.astype(MXU_DTYPE), vb) + _dot(rq, st.astype(MXU_DTYPE)) * xi_ref[h]
            kz = (rk * zeta_ref[h]).astype(MXU_DTYPE)
            state[h] = gr_ref[h, 0:1, :] * st + _dot_tn(kz, vb)
            o_ref[:, cols] = o
            mu = jnp.mean(o, axis=-1, keepdims=True)
            oc = o - mu
            yn = oc * lax.rsqrt(jnp.mean(oc * oc, axis=-1, keepdims=True) + NORM_EPS)
            ret = (g * _sigmoid(g)) * (yn * gain_ref[:, cols])
            mix_ref[:, cols] = jnp.where(valid, ret, 0.0).astype(mix_ref.dtype)

    head_tab = pl.BlockSpec((N_HEADS, CHUNK, HEAD_DIM), lambda c: (0, 0, 0))
    return pl.pallas_call(
        body, name="retention_fwd",
        out_shape=(jax.ShapeDtypeStruct((l, 2 * GROUP), MXU_DTYPE), jax.ShapeDtypeStruct((l, GROUP), F32),
                   jax.ShapeDtypeStruct((n_chunks, N_HEADS, HEAD_DIM, HEAD_DIM), F32)),
        grid=(n_chunks,),
        in_specs=[pl.BlockSpec((CHUNK, 4 * GROUP), lambda c: (c, 0)),
                  pl.BlockSpec((CHUNK, HEAD_DIM), lambda c: (c, 0)),
                  pl.BlockSpec((CHUNK, HEAD_DIM), lambda c: (c, 0)),
                  head_tab, head_tab, head_tab,
                  pl.BlockSpec((N_HEADS, 8, HEAD_DIM), lambda c: (0, 0, 0)),
                  pl.BlockSpec((1, GROUP), lambda c: (0, 0))],
        out_specs=(pl.BlockSpec((CHUNK, GROUP), lambda c: (c, 0)),
                   pl.BlockSpec((CHUNK, GROUP), lambda c: (c, 0)),
                   pl.BlockSpec((None, N_HEADS, HEAD_DIM, HEAD_DIM), lambda c: (c, 0, 0, 0))),
        scratch_shapes=[pltpu.VMEM((N_HEADS, HEAD_DIM, HEAD_DIM), F32)],
        compiler_params=_params(("arbitrary",)),
    )(proj, cos2, sin2, decay, xi, zeta, g_rows, ret_gain)


def _retention_bwd(proj, o_pre, states, d_mix, ret_gain, consts):
    l = proj.shape[0]
    n_chunks = l // CHUNK
    cos2, sin2, decay, xi, zeta, g_rows = consts
    scale = HEAD_DIM ** -0.5
    rev = lambda c: n_chunks - 1 - c

    def body(p_ref, o_ref, st_ref, dm_ref, cos_ref, sin_ref, dec_ref, xi_ref, zeta_ref, gr_ref, gain_ref,
             dp_ref, dgain_ref, dstate):
        step = pl.program_id(0)

        @pl.when(step == 0)
        def _():
            dstate[...] = jnp.zeros_like(dstate)
            dgain_ref[...] = jnp.zeros_like(dgain_ref)

        cos_v, sin_v = cos_ref[...], sin_ref[...]
        valid = _row_valid(rev(step), CHUNK)
        for h in range(N_HEADS):
            cols = slice(h * HEAD_DIM, (h + 1) * HEAD_DIM)
            q = p_ref[:, h * HEAD_DIM:(h + 1) * HEAD_DIM]
            k = p_ref[:, GROUP + h * HEAD_DIM:GROUP + (h + 1) * HEAD_DIM]
            v = p_ref[:, 2 * GROUP + h * HEAD_DIM:2 * GROUP + (h + 1) * HEAD_DIM]
            g = p_ref[:, 3 * GROUP + h * HEAD_DIM:3 * GROUP + (h + 1) * HEAD_DIM]
            o = o_ref[:, cols]
            gain = gain_ref[:, cols]
            d_ret = jnp.where(valid, dm_ref[:, cols], 0.0)
            mu = jnp.mean(o, axis=-1, keepdims=True)
            oc = o - mu
            rstd = lax.rsqrt(jnp.mean(oc * oc, axis=-1, keepdims=True) + NORM_EPS)
            yn = oc * rstd
            sig = _sigmoid(g)
            gate = g * sig
            dgain_ref[:, cols] += jnp.sum(d_ret * gate * yn, axis=0, keepdims=True)
            d_g = d_ret * (yn * gain) * (sig * (1.0 + g * (1.0 - sig)))
            d_yn = d_ret * gate * gain
            d_o = rstd * (d_yn - jnp.mean(d_yn, axis=-1, keepdims=True)
                          - yn * jnp.mean(d_yn * yn, axis=-1, keepdims=True))
            rq = _rope(q, cos_v, sin_v)
            rk = _rope(k, cos_v, sin_v) * scale
            rqb, rkb, vb = rq.astype(MXU_DTYPE), rk.astype(MXU_DTYPE), v.astype(MXU_DTYPE)
            dob = d_o.astype(MXU_DTYPE)
            dec = dec_ref[h]
            xi_h, zeta_h = xi_ref[h], zeta_ref[h]
            st_b = st_ref[h].astype(MXU_DTYPE)
            dst = dstate[h]
            dst_b = dst.astype(MXU_DTYPE)
            s_b = (_dot_nt(rqb, rkb) * dec).astype(MXU_DTYPE)
            da_b = (_dot_nt(dob, vb) * dec).astype(MXU_DTYPE)
            doxi_b = (d_o * xi_h).astype(MXU_DTYPE)
            kz_b = (rk * zeta_h).astype(MXU_DTYPE)
            d_rq = _dot(da_b, rkb) + _dot_nt(doxi_b, st_b)
            d_rk = _dot_tn(da_b, rqb) + _dot_nt(vb, dst_b) * zeta_h
            d_v = _dot_tn(s_b, dob) + _dot(kz_b, dst_b)
            dstate[h] = gr_ref[h, 0:1, :] * dst + _dot_tn(rqb, doxi_b)
            d_q = _rope_bwd(d_rq, cos_v, sin_v)
            d_k = _rope_bwd(d_rk * scale, cos_v, sin_v)
            dp_ref[:, h * HEAD_DIM:(h + 1) * HEAD_DIM] = d_q.astype(dp_ref.dtype)
            dp_ref[:, GROUP + h * HEAD_DIM:GROUP + (h + 1) * HEAD_DIM] = d_k.astype(dp_ref.dtype)
            dp_ref[:, 2 * GROUP + h * HEAD_DIM:2 * GROUP + (h + 1) * HEAD_DIM] = d_v.astype(dp_ref.dtype)
            dp_ref[:, 3 * GROUP + h * HEAD_DIM:3 * GROUP + (h + 1) * HEAD_DIM] = d_g.astype(dp_ref.dtype)

    head_tab = pl.BlockSpec((N_HEADS, CHUNK, HEAD_DIM), lambda c: (0, 0, 0))
    return pl.pallas_call(
        body, name="retention_bwd",
        out_shape=(jax.ShapeDtypeStruct((l, 4 * GROUP), MXU_DTYPE), jax.ShapeDtypeStruct((1, GROUP), F32)),
        grid=(n_chunks,),
        in_specs=[pl.BlockSpec((CHUNK, 4 * GROUP), lambda c: (rev(c), 0)),
                  pl.BlockSpec((CHUNK, GROUP), lambda c: (rev(c), 0)),
                  pl.BlockSpec((None, N_HEADS, HEAD_DIM, HEAD_DIM), lambda c: (rev(c), 0, 0, 0)),
                  pl.BlockSpec((CHUNK, GROUP), lambda c: (rev(c), 0)),
                  pl.BlockSpec((CHUNK, HEAD_DIM), lambda c: (rev(c), 0)),
                  pl.BlockSpec((CHUNK, HEAD_DIM), lambda c: (rev(c), 0)),
                  head_tab, head_tab, head_tab,
                  pl.BlockSpec((N_HEADS, 8, HEAD_DIM), lambda c: (0, 0, 0)),
                  pl.BlockSpec((1, GROUP), lambda c: (0, 0))],
        out_specs=(pl.BlockSpec((CHUNK, 4 * GROUP), lambda c: (rev(c), 0)),
                   pl.BlockSpec((1, GROUP), lambda c: (0, 0))),
        scratch_shapes=[pltpu.VMEM((N_HEADS, HEAD_DIM, HEAD_DIM), F32)],
        compiler_params=_params(("arbitrary",)),
    )(proj, o_pre, states, d_mix, cos2, sin2, decay, xi, zeta, g_rows, ret_gain)


FF_TILE = (7 * GROUP) // 128


def _log_forget(ff, bias_row, valid):
    x = ff + bias_row
    e = jnp.exp(-jnp.abs(x))
    lf = jnp.minimum(x, 0.0) - jnp.log(1.0 + e)
    head_lane = lax.broadcasted_iota(jnp.int32, x.shape, 1) < N_HEADS
    keep = lambda t: jnp.where(head_lane, jnp.where(valid, t, 0.0), 0.0)
    return keep(lf), keep(jnp.where(x >= 0, e, 1.0) / (1.0 + e))


def _fox_prep(proj, bias_row):
    l = proj.shape[0]
    n_blocks = l // CHUNK

    def body(ff_ref, b_ref, bc_ref, rows_ref, cum):
        r = lax.broadcasted_iota(jnp.int32, (CHUNK, CHUNK), 0)
        cidx = lax.broadcasted_iota(jnp.int32, (CHUNK, CHUNK), 1)
        tri = jnp.where(r >= cidx, 1.0, 0.0).astype(F32)
        carry = jnp.zeros((1, 128), F32)
        for blk in range(n_blocks):
            rows = slice(blk * CHUNK, (blk + 1) * CHUNK)
            valid = _row_valid(blk, CHUNK)
            lf, _ = _log_forget(ff_ref[rows, :], b_ref[...], valid)
            local = jnp.dot(tri, lf, precision=lax.Precision.HIGHEST, preferred_element_type=F32) + carry
            carry = local[CHUNK - 1:CHUNK, :]
            masked = jnp.where(valid, local, -NEG_BIG)
            cum[rows, :] = masked
            t = masked.T
            for h in range(N_HEADS):
                rows_ref[h, :, rows] = t[h:h + 1, :]
        full = cum[...]
        for h in range(N_HEADS):
            bc_ref[h] = jnp.broadcast_to(full[:, h:h + 1], (l, 128))

    return pl.pallas_call(
        body, name="fox_prep",
        out_shape=(jax.ShapeDtypeStruct((N_HEADS, l, 128), F32), jax.ShapeDtypeStruct((N_HEADS, 1, l), F32)),
        grid=(1,),
        in_specs=[pl.BlockSpec((l, 128), lambda i: (0, FF_TILE)), pl.BlockSpec((1, 128), lambda i: (0, 0))],
        out_specs=(pl.BlockSpec((N_HEADS, l, 128), lambda i: (0, 0, 0)),
                   pl.BlockSpec((N_HEADS, 1, l), lambda i: (0, 0, 0))),
        scratch_shapes=[pltpu.VMEM((l, 128), F32)],
        compiler_params=_params(("arbitrary",)),
    )(proj, bias_row)


ATTN_BLOCK = 2 * CHUNK


def _attn_blocks(l):
    assert (l - CHUNK) % ATTN_BLOCK == 0
    return [(0, CHUNK)] + [(s, ATTN_BLOCK) for s in range(CHUNK, l, ATTN_BLOCK)]


def _rows_valid(start, size):
    return start + lax.broadcasted_iota(jnp.int32, (size, 1), 0) >= PAD_ROWS


def _fox_fwd(proj, cum_bc, cum_rows, mix):
    l = proj.shape[0]
    blocks = _attn_blocks(l)
    scale = HEAD_DIM ** -0.5
    qt, kt, vt = 4 * N_HEADS, 5 * N_HEADS, 6 * N_HEADS

    def body(q_ref, k_ref, v_ref, cbc_ref, crow_ref, mix_in, o_ref, lse_ref, qb_s, kb_s, vb_s):
        qb_s[...] = q_ref[...].astype(MXU_DTYPE)
        kb_s[...] = k_ref[...].astype(MXU_DTYPE)
        vb_s[...] = v_ref[...].astype(MXU_DTYPE)
        for p, (qs, qn) in enumerate(blocks):
            qb = qb_s[qs:qs + qn, :]
            cq = cbc_ref[qs:qs + qn, :]
            m = jnp.full((qn, 1), NEG_BIG, F32)
            lsum = jnp.zeros((qn, 1), F32)
            acc = jnp.zeros((qn, HEAD_DIM), F32)
            for j in range(p + 1):
                ks, kn = blocks[j]
                bias = jnp.tile(cq, (1, kn // CHUNK)) - crow_ref[:, ks:ks + kn]
                s = _dot_nt(qb, kb_s[ks:ks + kn, :]) * scale + bias
                if j == p:
                    q_pos = qs + lax.broadcasted_iota(jnp.int32, (qn, kn), 0)
                    k_pos = ks + lax.broadcasted_iota(jnp.int32, (qn, kn), 1)
                    s = jnp.where(k_pos <= q_pos, s, NEG_BIG)
                m_new = jnp.maximum(m, jnp.max(s, axis=-1, keepdims=True))
                alpha = jnp.exp(m - m_new)
                pr = jnp.exp(s - m_new)
                lsum = lsum * alpha + jnp.sum(pr, axis=-1, keepdims=True)
                acc = acc * alpha + _dot(pr.astype(MXU_DTYPE), vb_s[ks:ks + kn, :])
                m = m_new
            o = jnp.where(_rows_valid(qs, qn), acc * (1.0 / lsum), 0.0)
            o_ref[qs:qs + qn, :] = o.astype(o_ref.dtype)
            lse = m + jnp.log(lsum)
            lse_ref[:, qs:qs + qn] = jnp.broadcast_to(lse, (qn, CHUNK)).T[0:1, :]

    head_col = lambda t: pl.BlockSpec((l, HEAD_DIM), lambda h: (0, t + h))
    return pl.pallas_call(
        body, name="fox_fwd",
        out_shape=(jax.ShapeDtypeStruct(mix.shape, mix.dtype), jax.ShapeDtypeStruct((N_HEADS, 1, l), F32)),
        grid=(N_HEADS,),
        in_specs=[head_col(qt), head_col(kt), head_col(vt),
                  pl.BlockSpec((None, l, 128), lambda h: (h, 0, 0)),
                  pl.BlockSpec((None, 1, l), lambda h: (h, 0, 0)),
                  ANY],
        out_specs=(head_col(N_HEADS), pl.BlockSpec((None, 1, l), lambda h: (h, 0, 0))),
        input_output_aliases={5: 0},
        scratch_shapes=[pltpu.VMEM((l, HEAD_DIM), MXU_DTYPE)] * 3,
        compiler_params=_params(("parallel",)),
    )(proj, proj, proj, cum_bc, cum_rows, mix)


def _fox_bwd(proj, cum_bc, cum_rows, d_mix, lse_rows):
    l = proj.shape[0]
    blocks = _attn_blocks(l)
    scale = HEAD_DIM ** -0.5
    qt, kt, vt = 4 * N_HEADS, 5 * N_HEADS, 6 * N_HEADS

    def body(q_ref, k_ref, v_ref, do_ref, cbc_ref, crow_ref, lse_ref,
             dq_ref, dk_ref, dv_ref, ds_ref, dk_acc, dv_acc, qb_s, kb_s, vb_s, dob_s, p_s, dp_s):
        qb_s[...] = q_ref[...].astype(MXU_DTYPE)
        kb_s[...] = k_ref[...].astype(MXU_DTYPE)
        vb_s[...] = v_ref[...].astype(MXU_DTYPE)
        dob_s[...] = jnp.where(_rows_valid(0, l), do_ref[...], 0.0).astype(MXU_DTYPE)
        dk_acc[...] = jnp.zeros_like(dk_acc)
        dv_acc[...] = jnp.zeros_like(dv_acc)
        ds_ref[...] = jnp.zeros_like(ds_ref)
        shift_row = crow_ref[...] - lse_ref[...]

        for p, (qs, qn) in enumerate(blocks):
            qb, dob = qb_s[qs:qs + qn, :], dob_s[qs:qs + qn, :]
            shift = shift_row[:, qs:qs + qn]

            delta = jnp.zeros((1, qn), F32)
            for j in range(p + 1):
                ks, kn = blocks[j]
                ck = jnp.tile(cbc_ref[ks:ks + kn, :], (1, qn // CHUNK))
                s_t = _dot_nt(kb_s[ks:ks + kn, :], qb) * scale + (shift - ck)
                if j == p:
                    k_pos = ks + lax.broadcasted_iota(jnp.int32, (kn, qn), 0)
                    q_pos = qs + lax.broadcasted_iota(jnp.int32, (kn, qn), 1)
                    s_t = jnp.where(k_pos <= q_pos, s_t, NEG_BIG)
                p_t, dp_t = jnp.exp(s_t), _dot_nt(vb_s[ks:ks + kn, :], dob)
                p_s[j, 0:kn, 0:qn] = p_t
                dp_s[j, 0:kn, 0:qn] = dp_t
                delta = delta + jnp.sum(p_t * dp_t, axis=0, keepdims=True)
            dq = jnp.zeros((qn, HEAD_DIM), F32)
            for j in range(p + 1):
                ks, kn = blocks[j]
                rows = slice(ks, ks + kn)
                p_t, dp_t = p_s[j, 0:kn, 0:qn], dp_s[j, 0:kn, 0:qn]
                ds_t = p_t * (dp_t - delta)
                ds_b = ds_t.astype(MXU_DTYPE)
                dv_acc[rows, :] += _dot(p_t.astype(MXU_DTYPE), dob)
                dk_acc[rows, :] += _dot(ds_b, qb) * scale
                ds_ref[rows, :] += sum(ds_t[:, c:c + CHUNK] for c in range(0, qn, CHUNK))
                dq = dq + _dot_tn(ds_b, kb_s[rows, :])
            dq_ref[qs:qs + qn, :] = (dq * scale).astype(dq_ref.dtype)

        dk_ref[...] = dk_acc[...].astype(dk_ref.dtype)
        dv_ref[...] = dv_acc[...].astype(dv_ref.dtype)

    col = jax.ShapeDtypeStruct((l, GROUP), MXU_DTYPE)
    head_col = lambda t: pl.BlockSpec((l, HEAD_DIM), lambda h: (0, t + h))
    return pl.pallas_call(
        body, name="fox_bwd",
        out_shape=(col, col, col, jax.ShapeDtypeStruct((N_HEADS, l, 128), F32)),
        grid=(N_HEADS,),
        in_specs=[head_col(qt), head_col(kt), head_col(vt), head_col(N_HEADS),
                  pl.BlockSpec((None, l, 128), lambda h: (h, 0, 0)),
                  pl.BlockSpec((None, 1, l), lambda h: (h, 0, 0)),
                  pl.BlockSpec((None, 1, l), lambda h: (h, 0, 0))],
        out_specs=(head_col(0), head_col(0), head_col(0), pl.BlockSpec((None, l, 128), lambda h: (h, 0, 0))),
        scratch_shapes=([pltpu.VMEM((l, HEAD_DIM), F32)] * 2 + [pltpu.VMEM((l, HEAD_DIM), MXU_DTYPE)] * 4
                        + [pltpu.VMEM((len(blocks), ATTN_BLOCK, ATTN_BLOCK), F32)] * 2),
        compiler_params=_params(("parallel",)),
    )(proj, proj, proj, d_mix, cum_bc, cum_rows, lse_rows)


def _fox_gate_bwd(ds_sum, proj, bias_row):
    l = proj.shape[0]
    n_blocks = l // CHUNK

    def body(ds_ref, ff_ref, b_ref, dff_ref, db_ref):
        r = lax.broadcasted_iota(jnp.int32, (CHUNK, CHUNK), 0)
        cidx = lax.broadcasted_iota(jnp.int32, (CHUNK, CHUNK), 1)
        upper = jnp.where(cidx >= r, 1.0, 0.0).astype(F32)
        carry = jnp.zeros((1, 128), F32)
        db = jnp.zeros((1, 128), F32)
        for blk in reversed(range(n_blocks)):
            rows = slice(blk * CHUNK, (blk + 1) * CHUNK)
            key_sum = jnp.zeros((CHUNK, 128), F32)
            for h in range(N_HEADS):
                select = jnp.where(cidx == h, 1.0, 0.0).astype(F32)
                key_sum = key_sum + jnp.dot(ds_ref[h, rows, :], select, precision=lax.Precision.HIGHEST,
                                            preferred_element_type=F32)
            suffix = jnp.dot(upper, key_sum, precision=lax.Precision.HIGHEST, preferred_element_type=F32) + carry
            carry = suffix[0:1, :]
            _, dsig = _log_forget(ff_ref[rows, :], b_ref[...], _row_valid(blk, CHUNK))
            dff = -suffix * dsig
            dff_ref[rows, :] = dff.astype(dff_ref.dtype)
            db = db + jnp.sum(dff, axis=0, keepdims=True)
        db_ref[...] = db

    return pl.pallas_call(
        body, name="fox_gate_bwd",
        out_shape=(jax.ShapeDtypeStruct((l, 128), MXU_DTYPE), jax.ShapeDtypeStruct((1, 128), F32)),
        grid=(1,),
        in_specs=[pl.BlockSpec((N_HEADS, l, 128), lambda i: (0, 0, 0)),
                  pl.BlockSpec((l, 128), lambda i: (0, FF_TILE)),
                  pl.BlockSpec((1, 128), lambda i: (0, 0))],
        out_specs=(pl.BlockSpec((l, 128), lambda i: (0, 0)), pl.BlockSpec((1, 128), lambda i: (0, 0))),
        compiler_params=_params(("arbitrary",)),
    )(ds_sum, proj, bias_row)


def _conv(u, w, b):
    return b + w[0:1, :] * pltpu.roll(u, 2, 0) + w[1:2, :] * pltpu.roll(u, 1, 0) + w[2:3, :] * u


def _conv_act_fwd(u, conv_w, conv_b, d_ff):
    l = u.shape[0]
    tc = _divisor_tile(d_ff, 256, 128)
    nt = d_ff // tc

    def body(ug_ref, uv_ref, wg_ref, wv_ref, bg_ref, bv_ref, a_ref, y_ref):
        yg = _conv(ug_ref[...], wg_ref[...], bg_ref[...])
        yv = _conv(uv_ref[...], wv_ref[...], bv_ref[...])
        act = yg * _sigmoid(yg) * yv
        a_ref[...] = jnp.where(_row_valid(0, l), act, 0.0).astype(a_ref.dtype)
        y_ref[0] = yg.astype(y_ref.dtype)
        y_ref[1] = yv.astype(y_ref.dtype)

    return pl.pallas_call(
        body, name="conv_act_fwd",
        out_shape=(jax.ShapeDtypeStruct((l, d_ff), MXU_DTYPE), jax.ShapeDtypeStruct((2, l, d_ff), MXU_DTYPE)),
        grid=(nt,),
        in_specs=[pl.BlockSpec((l, tc), lambda j: (0, j)), pl.BlockSpec((l, tc), lambda j: (0, j + nt)),
                  pl.BlockSpec((8, tc), lambda j: (0, j)), pl.BlockSpec((8, tc), lambda j: (0, j + nt)),
                  pl.BlockSpec((1, tc), lambda j: (0, j)), pl.BlockSpec((1, tc), lambda j: (0, j + nt))],
        out_specs=(pl.BlockSpec((l, tc), lambda j: (0, j)), pl.BlockSpec((2, l, tc), lambda j: (0, 0, j))),
        compiler_params=_params(("parallel",)),
    )(u, u, conv_w, conv_w, conv_b, conv_b)


def _conv_act_bwd(u, y, conv_w, d_act, d_ff):
    l = u.shape[0]
    tc = _divisor_tile(d_ff, 256, 128)
    nt = d_ff // tc

    def body(ug_ref, uv_ref, y_ref, wg_ref, wv_ref, da_ref, du_ref, dwb_ref):
        valid = _row_valid(0, l)
        ug, uv = ug_ref[...], uv_ref[...]
        wg, wv = wg_ref[...], wv_ref[...]
        yg, yv = y_ref[0].astype(F32), y_ref[1].astype(F32)
        sig = _sigmoid(yg)
        da = jnp.where(valid, da_ref[...], 0.0)
        d_yv = da * (yg * sig)
        d_yg = da * yv * (sig * (1.0 + yg * (1.0 - sig)))
        for idx, (dy, uu, w) in enumerate(((d_yg, ug, wg), (d_yv, uv, wv))):
            du = w[2:3, :] * dy + w[1:2, :] * pltpu.roll(dy, l - 1, 0) + w[0:1, :] * pltpu.roll(dy, l - 2, 0)
            du_ref[idx] = jnp.where(valid, du, 0.0).astype(du_ref.dtype)
            dwb_ref[idx, 0:1, :] = jnp.sum(dy * pltpu.roll(uu, 2, 0), axis=0, keepdims=True)
            dwb_ref[idx, 1:2, :] = jnp.sum(dy * pltpu.roll(uu, 1, 0), axis=0, keepdims=True)
            dwb_ref[idx, 2:3, :] = jnp.sum(dy * uu, axis=0, keepdims=True)
            dwb_ref[idx, 3:4, :] = jnp.sum(dy, axis=0, keepdims=True)
            dwb_ref[idx, 4:8, :] = jnp.zeros((4, tc), F32)

    return pl.pallas_call(
        body, name="conv_act_bwd",
        out_shape=(jax.ShapeDtypeStruct((2, l, d_ff), MXU_DTYPE), jax.ShapeDtypeStruct((2, 8, d_ff), F32)),
        grid=(nt,),
        in_specs=[pl.BlockSpec((l, tc), lambda j: (0, j)), pl.BlockSpec((l, tc), lambda j: (0, j + nt)),
                  pl.BlockSpec((2, l, tc), lambda j: (0, 0, j)),
                  pl.BlockSpec((8, tc), lambda j: (0, j)), pl.BlockSpec((8, tc), lambda j: (0, j + nt)),
                  pl.BlockSpec((l, tc), lambda j: (0, j))],
        out_specs=(pl.BlockSpec((2, l, tc), lambda j: (0, 0, j)), pl.BlockSpec((2, 8, tc), lambda j: (0, 0, j))),
        compiler_params=_params(("parallel",)),
    )(u, u, y, conv_w, conv_w, d_act)


def _adamw(w, g, m, v, name):
    shape = w.shape
    if w.ndim == 1:
        as2d = (1, shape[0])
    else:
        as2d = (int(np.prod(shape[:-1])), shape[-1])
    r, c = as2d
    tr = _divisor_tile(r, 256, 8)
    spec = pl.BlockSpec((tr, c), lambda i: (i, 0))

    def body(w_ref, g_ref, m_ref, v_ref, d_ref, nm_ref, nv_ref):
        d_ref[...], nm_ref[...], nv_ref[...] = _adamw_math(w_ref[...], g_ref[...], m_ref[...], v_ref[...])

    sds = jax.ShapeDtypeStruct(as2d, F32)
    outs = pl.pallas_call(
        body, name=name, out_shape=(sds, sds, sds), grid=(r // tr,),
        in_specs=[spec] * 4, out_specs=(spec,) * 3,
        compiler_params=_params(("parallel",)),
    )(w.reshape(as2d), g.reshape(as2d), m.reshape(as2d), v.reshape(as2d))
    return tuple(o.reshape(shape) for o in outs)


def _pad_rows(a, rows):
    return jnp.pad(a, ((0, rows - a.shape[0]), (0, 0)))


def kernel(x, meta_tokens, norm1_gain, w_in, b_forget, ret_norm_gain, w_out, norm2_gain, w_up, conv_w, conv_b, w_down, final_norm_gain, loss_target, m_meta_tokens, m_norm1_gain, m_w_in, m_b_forget, m_ret_norm_gain, m_w_out, m_norm2_gain, m_w_up, m_conv_w, m_conv_b, m_w_down, m_final_norm_gain, v_meta_tokens, v_norm1_gain, v_w_in, v_b_forget, v_ret_norm_gain, v_w_out, v_norm2_gain, v_w_up, v_conv_w, v_conv_b, v_w_down, v_final_norm_gain):
    seq, d = x.shape[1], x.shape[2]
    l = CHUNK + seq
    d_ff = w_down.shape[1] * N_DEV
    up_shard = w_up.shape[2]
    assert 4 * up_shard == d_ff and w_in.shape[2] == WIN_SHARD and d == 2 * GROUP
    dev = _device_index()
    mx, my, mc = _my_position()
    core = jnp.reshape(mc, (1,)).astype(jnp.int32)
    chip = jnp.reshape(2 * mx + my, (1,)).astype(jnp.int32)
    dev1 = jnp.reshape(dev, (1,)).astype(jnp.int32)

    small = jnp.concatenate([meta_tokens.reshape(-1, 128), conv_w[0].reshape(-1, 128)], axis=0)
    n_meta_rows = N_META * (d // N_DEV) // 128
    small_rows = small.shape[0]
    small_all = _all_gather(_pad_rows(small, -(-small_rows // 8) * 8), "gather_small")
    meta_full = jnp.transpose(small_all[:, :n_meta_rows].reshape(N_DEV, N_META, d // N_DEV), (1, 0, 2)).reshape(N_META, d)
    conv_w_full = _pad_rows(jnp.transpose(small_all[:, n_meta_rows:small_rows].reshape(N_DEV, 3, up_shard),
                                          (1, 0, 2)).reshape(3, 2 * d_ff), 8)
    to_rows = lambda t: jnp.pad(jnp.transpose(t[0]), ((0, WIN_ROWS - WIN_SHARD), (0, 0)))
    from_rows = lambda t: jnp.transpose(t[:WIN_SHARD])[None]
    w_in_rows = to_rows(w_in)
    out_rows = d // N_DEV
    mixer_rows = -(-(WIN_ROWS + out_rows) // 304) * 304
    mixer_shard = jnp.concatenate([w_in_rows.astype(WIRE_DTYPE), w_out[0].astype(WIRE_DTYPE),
                                   jnp.zeros((mixer_rows - WIN_ROWS - out_rows, d), WIRE_DTYPE)], axis=0)

    h0 = jnp.concatenate([jnp.zeros((PAD_ROWS, d), F32), meta_full, x[0]], axis=0)
    consts = _retention_consts(l)
    bias_row = jnp.pad(b_forget, ((0, 0), (0, 128 - N_HEADS)))
    a = _rmsnorm_fwd(h0, norm1_gain, "rmsnorm1")
    mixer_blocks = _gather_ring(mixer_shard, dev1, a, "gather_w_in")
    start_up = _gather_start(w_up[0], dev1, mixer_blocks, "gather_w_up_start")
    w_in_full = _assemble_w_in(mixer_blocks).astype(MXU_DTYPE)
    proj = _mm_nt(a, w_in_full, F32, "mm_proj", after=start_up[4])
    ret_mix, ret_pre, ret_states = _retention_fwd(proj, ret_norm_gain, consts)
    cum_bc, cum_rows = _fox_prep(proj, bias_row)
    mix, lse_rows = _fox_fwd(proj, cum_bc, cum_rows, ret_mix)
    w_out_full = mixer_blocks[:, WIN_ROWS:WIN_ROWS + out_rows].reshape(d, d).astype(MXU_DTYPE)
    h1, cn = _rmsnorm_fwd(h0, norm2_gain, "resid_rmsnorm2", res=_mm_nn(mix, w_out_full, F32, "mm_out"))
    w_up_blocks = _gather_finish(start_up, cn, "gather_w_up").astype(MXU_DTYPE)
    start_down = _gather_start(w_down[0], dev1, w_up_blocks, "gather_w_down_start")
    u = _mm_up(cn, w_up_blocks, start_down[4])
    act, conv_y = _conv_act_fwd(u, conv_w_full, conv_b + start_down[4][0, 0], d_ff)
    w_down_full = _gather_finish(start_down, act, "gather_w_down").reshape(d_ff, d).astype(MXU_DTYPE)
    mlp_out = _mm_nn(act, w_down_full, F32, "mm_down", tm_cap=544, tk_cap=d_ff)
    d_h2, d_h2_b, dg_final, loss_part = _loss_head(h1, mlp_out, final_norm_gain.reshape(1, d), loss_target[0])

    gw_down = _mm_tn(act, d_h2_b, WIRE_DTYPE, "mm_gw_down", tm_cap=1408, tn_cap=1024)
    d2d_down = _reduce_scatter_d2d_start(gw_down.reshape(N_DEV, d_ff // N_DEV, d), d_h2, "rs_w_down")
    d_act = _mm_nt(d_h2_b, w_down_full, F32, "mm_d_act", after=d2d_down[4])
    rs_down = _reduce_scatter_ici_start(d2d_down, d_act, core, "rs_w_down")
    d_u, d_conv = _conv_act_bwd(u, conv_y, conv_w_full + rs_down[4][0, 0], d_act, d_ff)
    tm = _divisor_tile(l, 1088, 16)
    gw_up = _mm_gw_up(cn, d_u)
    d2d_up = _reduce_scatter_d2d_start(gw_up, d_act, "rs_w_up")
    d_cn = _mm_d_cn(d_u, w_up_blocks, d2d_up[4])
    rs_up = _reduce_scatter_ici_start(d2d_up, d_cn, core, "rs_w_up")
    d_h1, d_h1_b, dg_norm2 = _rmsnorm_bwd(d_h2, d_cn, h1, norm2_gain + rs_up[4][0, 0], "rmsnorm2_bwd", True)

    gw_out = _mm_tn(mix, d_h1_b, WIRE_DTYPE, "mm_gw_out")
    d2d_out = _reduce_scatter_d2d_start(gw_out.reshape(N_DEV, d // N_DEV, d), d_cn, "rs_w_out")
    d_mix = _mm_nt(d_h1_b, w_out_full, F32, "mm_d_mix", after=d2d_out[4])
    d_fq, d_fk, d_fv, ds_sum = _fox_bwd(proj, cum_bc, cum_rows, d_mix, lse_rows)
    d_ff_tile, db_forget_row = _fox_gate_bwd(ds_sum, proj, bias_row)
    d_ret, dg_ret = _retention_bwd(proj, ret_pre, ret_states, d_mix, ret_norm_gain, consts)
    rs_out = _reduce_scatter_ici_start(d2d_out, d_ret, core, "rs_w_out")
    d_proj = jnp.concatenate(
        [d_ret, d_fq, d_fk, d_fv, d_ff_tile, jnp.zeros((l, WIN_N - 7 * GROUP - 128), MXU_DTYPE)], axis=1)
    gw_in = _mm_tn(d_proj, a, WIRE_DTYPE, "mm_gw_in", tm_cap=1536, after=rs_out[4])
    rs_in = _reduce_scatter_start(_extract_w_in_windows(gw_in), core, "rs_w_in")
    d_a = _mm_nn(d_proj, w_in_full, F32, "mm_d_a", tm_cap=544, tn_cap=256, tk_cap=WIN_N, after=rs_in[4])
    d_front, d_tokens, dg_norm1 = _rmsnorm_bwd(d_h1, d_a, h0, norm1_gain + rs_in[4][0, 0], "rmsnorm1_bwd", False)
    grad_x = d_tokens[None]
    d_meta = d_front[PAD_ROWS:CHUNK]

    d_conv_w = jnp.concatenate([d_conv[0, 0:3], d_conv[1, 0:3]], axis=1)
    d_conv_b = jnp.concatenate([d_conv[0, 3:4], d_conv[1, 3:4]], axis=1)
    pieces = [loss_part[:, 0:1], dg_norm1, db_forget_row[:, 0:N_HEADS], dg_ret, dg_norm2, d_conv_b, dg_final,
              d_meta.reshape(1, -1), d_conv_w.reshape(1, -1)]
    sizes = [p.shape[1] for p in pieces]
    flat = jnp.concatenate(pieces, axis=1)
    padded = -(-flat.shape[1] // 1024) * 1024
    flat = jnp.pad(flat, ((0, 0), (0, padded - flat.shape[1]))).reshape(padded // 128, 128)
    small_ar = _small_all_reduce_start(flat, d_tokens, "all_reduce_small")

    lead = lambda outs: tuple(o[None] for o in outs)
    fin_down = lead(_reduce_scatter_finish(rs_down, small_ar[4], chip, w_down[0], m_w_down[0], v_w_down[0], "rs_w_down"))
    fin_up = lead(_reduce_scatter_finish(rs_up, fin_down[3], chip, w_up[0], m_w_up[0], v_w_up[0], "rs_w_up"))
    fin_out = lead(_reduce_scatter_finish(rs_out, fin_up[3], chip, w_out[0], m_w_out[0], v_w_out[0], "rs_w_out"))
    fin_in = tuple(from_rows(o) for o in _reduce_scatter_finish(
        rs_in, fin_out[3], chip, w_in_rows, to_rows(m_w_in), to_rows(v_w_in), "rs_w_in"))
    g_w_down, g_w_up, g_w_out, g_w_in = fin_down[0], fin_up[0], fin_out[0], fin_in[0]
    early = [fin_down[1:], fin_up[1:], fin_out[1:], fin_in[1:]]
    total = _small_all_reduce_finish(small_ar, fin_in[3], dev1, "all_reduce_small").reshape(1, padded)
    offs = np.concatenate([[0], np.cumsum(sizes)])
    take = lambda k: total[:, int(offs[k]):int(offs[k + 1])]
    loss = take(0).reshape(())
    g_norm1, g_bf, g_ret_gain, g_norm2 = take(1), take(2), take(3), take(4)
    g_conv_b, g_final = take(5), take(6).reshape(d)
    g_meta = lax.dynamic_slice(take(7).reshape(N_META, d), (jnp.int32(0), (dev * (d // N_DEV)).astype(jnp.int32)),
                               (N_META, d // N_DEV))
    g_conv_w = lax.dynamic_slice(take(8).reshape(3, 2 * d_ff), (jnp.int32(0), (dev * up_shard).astype(jnp.int32)),
                                 (3, up_shard))[None]

    weights = [meta_tokens, norm1_gain, w_in, b_forget, ret_norm_gain, w_out, norm2_gain, w_up, conv_w, conv_b,
               w_down, final_norm_gain]
    grads = [g_meta, g_norm1, g_w_in, g_bf, g_ret_gain, g_w_out, g_norm2, g_w_up, g_conv_w, g_conv_b, g_w_down,
             g_final]
    done = {"w_down": early[0], "w_up": early[1], "w_out": early[2], "w_in": early[3]}
    ms = [m_meta_tokens, m_norm1_gain, m_w_in, m_b_forget, m_ret_norm_gain, m_w_out, m_norm2_gain, m_w_up, m_conv_w,
          m_conv_b, m_w_down, m_final_norm_gain]
    vs = [v_meta_tokens, v_norm1_gain, v_w_in, v_b_forget, v_ret_norm_gain, v_w_out, v_norm2_gain, v_w_up, v_conv_w,
          v_conv_b, v_w_down, v_final_norm_gain]
    names = ["meta", "norm1", "w_in", "b_forget", "ret_gain", "w_out", "norm2", "w_up", "conv_w", "conv_b", "w_down",
             "final_gain"]
    deltas, new_ms, new_vs = [], [], []
    for w, g, m, v, n in zip(weights, grads, ms, vs, names):
        dl, nm, nv = done[n] if n in done else _adamw(w, g, m, v, "adamw_" + n)
        deltas.append(dl)
        new_ms.append(nm)
        new_vs.append(nv)
    return (loss, grad_x, *grads, *deltas, *new_ms, *new_vs)
```

```python
import functools

import numpy as np
import jax
import jax.numpy as jnp
from jax import lax
from jax.experimental import pallas as pl
from jax.experimental.pallas import tpu as pltpu

F32 = jnp.float32
MXU_DTYPE = jnp.bfloat16
WIRE_DTYPE = jnp.bfloat16

N_DEV = 8
N_META = 16
CHUNK = 128
PAD_ROWS = CHUNK - N_META
N_HEADS = 8
HEAD_DIM = 128
GROUP = N_HEADS * HEAD_DIM
IN_DIM = 7 * GROUP + N_HEADS
WIN_SHARD = IN_DIM // N_DEV
WIN_ROWS = 912
WIN_BLOCK = 1024
WIN_STRIDE = 896
WIN_N = 7680
ROPE_BASE = 10000.0
NORM_EPS = 1e-6
NEG_BIG = -1e30
ADAM_LR, ADAM_B1, ADAM_B2, ADAM_EPS, ADAM_WD, ADAM_STEP = 0.001, 0.9, 0.999, 1e-08, 0.01, 10
VMEM_LIMIT = 52 * 1024 * 1024
MESH = pl.DeviceIdType.MESH
ANY = pl.BlockSpec(memory_space=pl.ANY)
VMEM_SPEC = pl.BlockSpec(memory_space=pltpu.VMEM)


def _params(sem=None):
    kw = {"vmem_limit_bytes": VMEM_LIMIT}
    if sem is not None:
        kw["dimension_semantics"] = sem
    return pltpu.CompilerParams(**kw)


def _divisor_tile(n, cap, unit):
    if n <= cap:
        return n
    best = None
    for t in range(unit, cap + 1, unit):
        if n % t == 0:
            best = t
    assert best is not None, (n, cap, unit)
    return best


def _my_position():
    return lax.axis_index("x"), lax.axis_index("y"), lax.axis_index("c")


def _device_index():
    x, y, c = _my_position()
    return 4 * x + 2 * y + c


def _all_gather(shard, name):
    r, c = shard.shape

    def body(x_ref, out_ref, send_sems, recv_sems, local_sem):
        mx, my, mc = _my_position()
        me, sibling = (mx, my, mc), (mx, my, 1 - mc)
        chips = [(1 - mx, my), (mx, 1 - my), (1 - mx, 1 - my)]

        def slot(px, py, pc):
            return out_ref.at[4 * px + 2 * py + pc]

        def copy(k, block, to, src=None):
            return pltpu.make_async_remote_copy(
                src_ref=slot(*block) if src is None else src, dst_ref=slot(*block),
                send_sem=send_sems.at[k], recv_sem=recv_sems.at[k], device_id=to, device_id_type=MESH)

        mine = pltpu.make_async_copy(x_ref, slot(*me), local_sem)
        mine.start()
        first = [copy(0, me, sibling, src=x_ref)]
        first += [copy(1 + j, me, (*chip, mc), src=x_ref) for j, chip in enumerate(chips)]
        for cp in first:
            cp.start()
        passed = [copy(4 + j, (*chip, mc), sibling) for j, chip in enumerate(chips)]
        for j, chip in enumerate(chips):
            copy(1 + j, (*chip, mc), me).wait_recv()
            passed[j].start()
        copy(0, sibling, me).wait_recv()
        for j, chip in enumerate(chips):
            copy(4 + j, (*chip, 1 - mc), me).wait_recv()
        for cp in first + passed:
            cp.wait_send()
        mine.wait()

    return pl.pallas_call(
        body, name=name,
        out_shape=jax.ShapeDtypeStruct((N_DEV, r, c), shard.dtype),
        in_specs=[ANY], out_specs=ANY,
        scratch_shapes=[pltpu.SemaphoreType.DMA((7,)), pltpu.SemaphoreType.DMA((7,)), pltpu.SemaphoreType.DMA],
    )(shard)


HBM_SPEC = pl.BlockSpec(memory_space=pltpu.HBM)
SEM_SPEC = pl.BlockSpec(memory_space=pltpu.SEMAPHORE)
DATAFLOW_EFFECT = pltpu.SideEffectType.DATAFLOW_SIDE_EFFECTING


def _in_hbm(a):
    return pltpu.with_memory_space_constraint(a, pltpu.HBM)


def _split_start(src, land, make_copies, n_copies, after, name):
    if isinstance(land, tuple):
        land = lax.empty(land, src.dtype)
    land_shape = land.shape
    def body(src_ref, land_ref, after_ref, send_sems, recv_sems, src_thru, land_thru, token):
        for cp in make_copies(src_ref, land_ref, send_sems, recv_sems):
            cp.start()
        token[...] = jnp.zeros_like(token)

    return pl.pallas_call(
        body, name=name,
        out_shape=(pltpu.SemaphoreType.DMA((n_copies,)), pltpu.SemaphoreType.DMA((n_copies,)),
                   pltpu.HBM(src.shape, src.dtype), pltpu.HBM(land_shape, land.dtype),
                   jax.ShapeDtypeStruct((8, 128), F32)),
        in_specs=(HBM_SPEC, HBM_SPEC, ANY), out_specs=(SEM_SPEC, SEM_SPEC, HBM_SPEC, HBM_SPEC, VMEM_SPEC),
        input_output_aliases={0: 2, 1: 3},
        compiler_params=pltpu.CompilerParams(has_side_effects=DATAFLOW_EFFECT),
    )(_in_hbm(src), _in_hbm(land), after)


def _split_wait(started, after, make_copies, name):
    send_sems, recv_sems, src_thru, land_thru, _ = started

    def body(src_ref, land_ref, send_sems_ref, recv_sems_ref, after_ref, src_dead, land_out):
        for cp in make_copies(src_ref, land_ref, send_sems_ref, recv_sems_ref):
            cp.wait_send()
            cp.wait_recv()

    return pl.pallas_call(
        body, name=name,
        out_shape=(pltpu.HBM(src_thru.shape, src_thru.dtype), pltpu.HBM(land_thru.shape, land_thru.dtype)),
        in_specs=(HBM_SPEC, HBM_SPEC, SEM_SPEC, SEM_SPEC, ANY), out_specs=(HBM_SPEC, HBM_SPEC),
        input_output_aliases={0: 0, 1: 1},
        compiler_params=pltpu.CompilerParams(has_side_effects=DATAFLOW_EFFECT),
    )(src_thru, land_thru, send_sems, recv_sems, after)


def _gather_copies(x_ref, land_ref, send_sems, recv_sems):
    mx, my, mc = _my_position()
    me = 4 * mx + 2 * my + mc
    targets = [(mx, my, 1 - mc), (1 - mx, my, mc), (mx, 1 - my, mc), (1 - mx, 1 - my, mc)]
    return [pltpu.make_async_remote_copy(
        src_ref=land_ref.at[me], dst_ref=land_ref.at[me], send_sem=send_sems.at[k], recv_sem=recv_sems.at[k],
        device_id=t, device_id_type=MESH) for k, t in enumerate(targets)]


def _own_slot(shard, dev, name):
    r, c = shard.shape
    tr = _divisor_tile(r, 512, 16)

    def body(s_ref, x_ref, o_ref):
        o_ref[...] = x_ref[...].astype(o_ref.dtype)

    return pl.pallas_call(
        body, name=name,
        out_shape=jax.ShapeDtypeStruct((N_DEV, r, c), WIRE_DTYPE),
        grid_spec=pltpu.PrefetchScalarGridSpec(
            num_scalar_prefetch=1, grid=(r // tr,),
            in_specs=[pl.BlockSpec((tr, c), lambda i, s: (i, 0))],
            out_specs=pl.BlockSpec((None, tr, c), lambda i, s: (s[0], i, 0))),
        compiler_params=_params(("parallel",)),
    )(dev, shard)


def _gather_start(shard, dev, after, name):
    return _split_start(jnp.zeros((8, 128), F32), _own_slot(shard, dev, name + "_own"), _gather_copies, 4, after, name)


def _gather_ring(shard, dev, after, name):
    r, c = shard.shape
    half = r // 2
    assert half % 16 == 0

    def body(x_ref, after_ref, land_in, land_ref, send_sems, recv_sems):
        mx, my, mc = _my_position()
        sibling, x_nbr, y_nbr = (mx, my, 1 - mc), (1 - mx, my, mc), (mx, 1 - my, mc)
        first, second = pl.ds(0, half), pl.ds(half, half)

        def slot(px, py, pc):
            return land_ref.at[4 * px + 2 * py + pc]

        def copy(k, src, dst, to):
            return pltpu.make_async_remote_copy(src_ref=src, dst_ref=dst, send_sem=send_sems.at[k],
                                                recv_sem=recv_sems.at[k], device_id=to, device_id_type=MESH)

        def arrived(k, dst):
            copy(k, dst, dst, sibling).wait_recv()

        mine = slot(mx, my, mc)
        from_x, from_y, from_d = slot(1 - mx, my, mc), slot(mx, 1 - my, mc), slot(1 - mx, 1 - my, mc)
        sent = [copy(0, x_ref, mine, sibling), copy(1, x_ref, mine, x_nbr), copy(2, x_ref, mine, y_nbr)]
        for cp in sent:
            cp.start()

        def send(k, src, to):
            cp = copy(k, src, src, to)
            cp.start()
            sent.append(cp)

        arrived(1, from_x)
        send(3, from_x.at[first], y_nbr)
        send(5, from_x, sibling)
        arrived(2, from_y)
        send(4, from_y.at[second], x_nbr)
        send(6, from_y, sibling)
        arrived(3, from_d.at[first])
        send(7, from_d.at[first], sibling)
        arrived(4, from_d.at[second])
        send(8, from_d.at[second], sibling)
        arrived(0, slot(mx, my, 1 - mc))
        arrived(5, slot(1 - mx, my, 1 - mc))
        arrived(6, slot(mx, 1 - my, 1 - mc))
        arrived(7, slot(1 - mx, 1 - my, 1 - mc).at[first])
        arrived(8, slot(1 - mx, 1 - my, 1 - mc).at[second])
        for cp in sent:
            cp.wait_send()

    land = _own_slot(shard, dev, name + "_own")
    return pl.pallas_call(
        body, name=name,
        out_shape=jax.ShapeDtypeStruct(land.shape, land.dtype),
        in_specs=[ANY, ANY, ANY], out_specs=ANY,
        input_output_aliases={2: 0},
        scratch_shapes=[pltpu.SemaphoreType.DMA((9,)), pltpu.SemaphoreType.DMA((9,))],
    )(shard, after, land)


def _pass_copies(unused_ref, land_ref, send_sems, recv_sems):
    mx, my, mc = _my_position()
    chips = [(1 - mx, my), (mx, 1 - my), (1 - mx, 1 - my)]
    return [pltpu.make_async_remote_copy(
        src_ref=land_ref.at[4 * cx + 2 * cy + mc], dst_ref=land_ref.at[4 * cx + 2 * cy + mc],
        send_sem=send_sems.at[j], recv_sem=recv_sems.at[j],
        device_id=(mx, my, 1 - mc), device_id_type=MESH) for j, (cx, cy) in enumerate(chips)]


def _gather_pass_start(started, after, name):
    _, land = _split_wait(started, after, _gather_copies, name + "_wait")
    return _split_start(jnp.zeros((8, 128), F32), land, _pass_copies, 3, after, name + "_pass_start")


def _gather_pass_finish(pass_started, after, name):
    return _split_wait(pass_started, after, _pass_copies, name + "_pass_wait")[1]


def _gather_finish(started, after, name):
    _, land = _split_wait(started, after, _gather_copies, name + "_wait")

    def body(land_in, land_ref, send_sems, recv_sems):
        mx, my, mc = _my_position()
        chips = [(1 - mx, my), (mx, 1 - my), (1 - mx, 1 - my)]
        copies = [pltpu.make_async_remote_copy(
            src_ref=land_ref.at[4 * cx + 2 * cy + mc], dst_ref=land_ref.at[4 * cx + 2 * cy + mc],
            send_sem=send_sems.at[j], recv_sem=recv_sems.at[j],
            device_id=(mx, my, 1 - mc), device_id_type=MESH) for j, (cx, cy) in enumerate(chips)]
        for cp in copies:
            cp.start()
        for j, (cx, cy) in enumerate(chips):
            copies[j].wait_send()
            pltpu.make_async_remote_copy(
                src_ref=land_ref.at[4 * cx + 2 * cy + 1 - mc], dst_ref=land_ref.at[4 * cx + 2 * cy + 1 - mc],
                send_sem=send_sems.at[j], recv_sem=recv_sems.at[j],
                device_id=(mx, my, 1 - mc), device_id_type=MESH).wait_recv()

    return pl.pallas_call(
        body, name=name + "_pass",
        out_shape=jax.ShapeDtypeStruct(land.shape, land.dtype),
        in_specs=[ANY], out_specs=ANY,
        input_output_aliases={0: 0},
        scratch_shapes=[pltpu.SemaphoreType.DMA((3,)), pltpu.SemaphoreType.DMA((3,))],
    )(land)


def _chip_copies(p_ref, land_ref, send_sems, recv_sems):
    mx, my, mc = _my_position()
    chips = [(1 - mx, my), (mx, 1 - my), (1 - mx, 1 - my)]
    return [pltpu.make_async_remote_copy(
        src_ref=p_ref.at[2 * cx + cy], dst_ref=land_ref.at[j], send_sem=send_sems.at[j], recv_sem=recv_sems.at[j],
        device_id=(cx, cy, mc), device_id_type=MESH) for j, (cx, cy) in enumerate(chips)]


def _reduce_scatter_start(g, core, name):
    pair = _pair_sum(g, _exchange_sibling(g, name + "_d2d"), core, name + "_pairsum")
    return _split_start(pair, (3,) + pair.shape[1:], _chip_copies, 3, g, name + "_ici_start")


def _sibling_copies(g_ref, land_ref, send_sems, recv_sems):
    mx, my, mc = _my_position()
    return [pltpu.make_async_remote_copy(
        src_ref=g_ref.at[2 * k + (1 - mc)], dst_ref=land_ref.at[k], send_sem=send_sems.at[k], recv_sem=recv_sems.at[k],
        device_id=(mx, my, 1 - mc), device_id_type=MESH) for k in range(4)]


def _reduce_scatter_d2d_start(g, after, name):
    return _split_start(g, (4,) + g.shape[1:], _sibling_copies, 4, after, name + "_d2d_start")


def _reduce_scatter_ici_start(d2d_started, after, core, name):
    g, from_sibling = _split_wait(d2d_started, after, _sibling_copies, name + "_d2d_wait")
    pair = _pair_sum(g, from_sibling, core, name + "_pairsum")
    return _split_start(pair, (3,) + pair.shape[1:], _chip_copies, 3, g, name + "_ici_start")


def _reduce_scatter_finish(started, after, chip, w, m, v, name):
    pair, from_chips = _split_wait(started, after, _chip_copies, name + "_ici_wait")
    return _final_sum_adamw(pair, from_chips, chip, w, m, v, name + "_sum_adamw")


def _exchange_sibling(g, name):
    _, r, c = g.shape

    def body(g_ref, out_ref, send_sems, recv_sems):
        mx, my, mc = _my_position()
        copies = [
            pltpu.make_async_remote_copy(
                src_ref=g_ref.at[2 * k + (1 - mc)], dst_ref=out_ref.at[k],
                send_sem=send_sems.at[k], recv_sem=recv_sems.at[k],
                device_id=(mx, my, 1 - mc), device_id_type=MESH)
            for k in range(4)]
        for cp in copies:
            cp.start()
        for cp in copies:
            cp.wait()

    return pl.pallas_call(
        body, name=name,
        out_shape=jax.ShapeDtypeStruct((4, r, c), g.dtype),
        in_specs=[ANY], out_specs=ANY,
        scratch_shapes=[pltpu.SemaphoreType.DMA((4,)), pltpu.SemaphoreType.DMA((4,))],
    )(g)


def _pair_sum(g, recv, core, name):
    _, r, c = g.shape
    tr = _divisor_tile(r, 512, 16)

    def body(s_ref, g_ref, r_ref, o_ref):
        o_ref[...] = (g_ref[...].astype(F32) + r_ref[...].astype(F32)).astype(o_ref.dtype)

    return pl.pallas_call(
        body, name=name,
        out_shape=jax.ShapeDtypeStruct((4, r, c), g.dtype),
        grid_spec=pltpu.PrefetchScalarGridSpec(
            num_scalar_prefetch=1, grid=(4, r // tr),
            in_specs=[pl.BlockSpec((None, tr, c), lambda k, i, s: (2 * k + s[0], i, 0)),
                      pl.BlockSpec((None, tr, c), lambda k, i, s: (k, i, 0))],
            out_specs=pl.BlockSpec((None, tr, c), lambda k, i, s: (k, i, 0))),
        compiler_params=_params(("parallel", "parallel")),
    )(core, g, recv)


def _adamw_math(w, g, m, v):
    nm = ADAM_B1 * m + (1.0 - ADAM_B1) * g
    nv = ADAM_B2 * v + (1.0 - ADAM_B2) * (g * g)
    m_hat = nm / (1.0 - ADAM_B1 ** ADAM_STEP)
    v_hat = nv / (1.0 - ADAM_B2 ** ADAM_STEP)
    return -ADAM_LR * (m_hat / (jnp.sqrt(v_hat) + ADAM_EPS) + ADAM_WD * w), nm, nv


def _final_sum_adamw(p, recv, chip, w, m, v, name):
    _, r, c = p.shape
    tr = _divisor_tile(r, 256, 16)
    tile = lambda: pl.BlockSpec((tr, c), lambda i, s: (i, 0))

    def body(s_ref, p_ref, r_ref, w_ref, m_ref, v_ref, g_ref, d_ref, nm_ref, nv_ref):
        g = p_ref[...].astype(F32)
        for j in range(3):
            g = g + r_ref[j].astype(F32)
        g_ref[...] = g
        d_ref[...], nm_ref[...], nv_ref[...] = _adamw_math(w_ref[...], g, m_ref[...], v_ref[...])

    sds = jax.ShapeDtypeStruct((r, c), F32)
    return pl.pallas_call(
        body, name=name,
        out_shape=(sds, sds, sds, sds),
        grid_spec=pltpu.PrefetchScalarGridSpec(
            num_scalar_prefetch=1, grid=(r // tr,),
            in_specs=[pl.BlockSpec((None, tr, c), lambda i, s: (s[0], i, 0)),
                      pl.BlockSpec((3, tr, c), lambda i, s: (0, i, 0)), tile(), tile(), tile()],
            out_specs=(tile(), tile(), tile(), tile())),
        compiler_params=_params(("parallel",)),
    )(chip, p, recv, w, m, v)


def _all_to_all_copies(v_ref, land_ref, send_sems, recv_sems):
    mx, my, mc = _my_position()
    me = 4 * mx + 2 * my + mc
    copies = []
    for rel in range(1, N_DEV):
        bx, by, bc = (rel >> 2) & 1, (rel >> 1) & 1, rel & 1
        target = (1 - mx if bx else mx, 1 - my if by else my, 1 - mc if bc else mc)
        copies.append(pltpu.make_async_remote_copy(
            src_ref=v_ref, dst_ref=land_ref.at[me], send_sem=send_sems.at[rel - 1], recv_sem=recv_sems.at[rel - 1],
            device_id=target, device_id_type=MESH))
    return copies


def _small_all_reduce_start(v, after, name):
    return _split_start(v, (N_DEV,) + v.shape, _all_to_all_copies, N_DEV - 1, after, name + "_start")


def _small_all_reduce_finish(started, after, dev, name):
    v, land = _split_wait(started, after, _all_to_all_copies, name + "_wait")
    rows = v.shape[0]

    def body(me_ref, v_ref, land_ref, o_ref):
        for j in range(N_DEV):
            @pl.when(me_ref[0] == j)
            def _():
                o_ref[...] = v_ref[...] if j == 0 else o_ref[...] + v_ref[...]

            @pl.when(me_ref[0] != j)
            def _():
                o_ref[...] = land_ref[j] if j == 0 else o_ref[...] + land_ref[j]

    return pl.pallas_call(
        body, name=name + "_sum",
        out_shape=jax.ShapeDtypeStruct((rows, 128), F32),
        grid_spec=pltpu.PrefetchScalarGridSpec(
            num_scalar_prefetch=1, grid=(1,),
            in_specs=[pl.BlockSpec((rows, 128), lambda i, s: (0, 0)),
                      pl.BlockSpec((N_DEV, rows, 128), lambda i, s: (0, 0, 0))],
            out_specs=pl.BlockSpec((rows, 128), lambda i, s: (0, 0))),
        compiler_params=_params(("arbitrary",)),
    )(dev, v, land)


def _assemble_w_in(blocks):
    rows, d = WIN_ROWS, blocks.shape[2]
    tc = _divisor_tile(d, 256, 128)
    n_tiles = WIN_N // 128
    last = (N_DEV * WIN_STRIDE) // 128

    def body(b_ref, o_ref):
        win = []
        for i in range(N_DEV):
            w = jnp.concatenate([b_ref[i].astype(F32), jnp.zeros((WIN_BLOCK - rows, tc), F32)], axis=0)
            win.append(pltpu.roll(w, i, 0) if i else w)
        for t in range(n_tiles):
            if t > last:
                o_ref[t * 128:(t + 1) * 128, :] = jnp.zeros((128, tc), o_ref.dtype)
                continue
            i = min(t // 7, N_DEV - 1)
            k = t - 7 * i
            val = win[i][k * 128:(k + 1) * 128, :]
            if k == 0 and i >= 1:
                val = val + win[i - 1][7 * 128:8 * 128, :]
            o_ref[t * 128:(t + 1) * 128, :] = val.astype(o_ref.dtype)

    return pl.pallas_call(
        body, name="assemble_w_in",
        out_shape=jax.ShapeDtypeStruct((WIN_N, d), blocks.dtype),
        grid=(d // tc,),
        in_specs=[pl.BlockSpec((N_DEV, rows, tc), lambda j: (0, 0, j))],
        out_specs=pl.BlockSpec((WIN_N, tc), lambda j: (0, j)),
        compiler_params=_params(("parallel",)),
    )(blocks)


def _extract_w_in_windows(g):
    _, d = g.shape
    tc = _divisor_tile(d, 256, 128)

    def body(g_ref, o_ref):
        for j in range(N_DEV):
            w = g_ref[WIN_STRIDE * j:WIN_STRIDE * j + WIN_BLOCK, :].astype(F32)
            w = pltpu.roll(w, WIN_BLOCK - j, 0) if j else w
            o_ref[j] = w[0:WIN_ROWS, :].astype(o_ref.dtype)

    return pl.pallas_call(
        body, name="extract_w_in_windows",
        out_shape=jax.ShapeDtypeStruct((N_DEV, WIN_ROWS, d), g.dtype),
        grid=(d // tc,),
        in_specs=[pl.BlockSpec((WIN_N, tc), lambda j: (0, j))],
        out_specs=pl.BlockSpec((N_DEV, WIN_ROWS, tc), lambda j: (0, 0, j)),
        compiler_params=_params(("parallel",)),
    )(g)


def _mm(a, b, *, a_spec, b_spec, o_spec, out_shape, grid, contract, nk, name, after=None):
    dn = (((contract[0],), (contract[1],)), ((), ()))
    tm, tn = o_spec.block_shape[-2:]
    behind = [] if after is None else [after]

    def body(a_ref, b_ref, *rest):
        o_ref, *scratch = rest[len(behind):]
        part = lax.dot_general(a_ref[...], b_ref[...], dn, preferred_element_type=F32)
        if nk == 1:
            o_ref[...] = part.astype(o_ref.dtype)
            return
        acc = scratch[0]
        k = pl.program_id(2)

        @pl.when(k == 0)
        def _():
            acc[...] = part

        @pl.when(k > 0)
        def _():
            acc[...] += part

        @pl.when(k == nk - 1)
        def _():
            o_ref[...] = acc[...].astype(o_ref.dtype)

    return pl.pallas_call(
        body, name=name, out_shape=out_shape, grid=grid,
        in_specs=[a_spec, b_spec] + [ANY] * len(behind), out_specs=o_spec,
        scratch_shapes=[] if nk == 1 else [pltpu.VMEM((tm, tn), F32)],
        compiler_params=_params(("parallel", "parallel", "arbitrary")),
    )(a, b, *behind)


def _mm_nn(a, b, out_dtype, name, tm_cap=1088, tn_cap=512, tk_cap=2048, after=None):
    m, k = a.shape
    _, n = b.shape
    tm, tn, tk = _divisor_tile(m, tm_cap, 16), _divisor_tile(n, tn_cap, 128), _divisor_tile(k, tk_cap, 128)
    return _mm(a, b,
               a_spec=pl.BlockSpec((tm, tk), lambda i, j, kk: (i, kk)),
               b_spec=pl.BlockSpec((tk, tn), lambda i, j, kk: (kk, j)),
               o_spec=pl.BlockSpec((tm, tn), lambda i, j, kk: (i, j)),
               out_shape=jax.ShapeDtypeStruct((m, n), out_dtype),
               grid=(m // tm, n // tn, k // tk), contract=(1, 0), nk=k // tk, name=name, after=after)


def _mm_nt(a, b, out_dtype, name, tm_cap=1088, tn_cap=512, tk_cap=2048, after=None):
    m, k = a.shape
    n, _ = b.shape
    tm, tn, tk = _divisor_tile(m, tm_cap, 16), _divisor_tile(n, tn_cap, 128), _divisor_tile(k, tk_cap, 128)
    return _mm(a, b,
               a_spec=pl.BlockSpec((tm, tk), lambda i, j, kk: (i, kk)),
               b_spec=pl.BlockSpec((tn, tk), lambda i, j, kk: (j, kk)),
               o_spec=pl.BlockSpec((tm, tn), lambda i, j, kk: (i, j)),
               out_shape=jax.ShapeDtypeStruct((m, n), out_dtype),
               grid=(m // tm, n // tn, k // tk), contract=(1, 1), nk=k // tk, name=name, after=after)


def _mm_tn(a, b, out_dtype, name, tm_cap=1024, tn_cap=512, after=None):
    l, m = a.shape
    _, n = b.shape
    tm, tn = _divisor_tile(m, tm_cap, 128), _divisor_tile(n, tn_cap, 128)
    return _mm(a, b,
               a_spec=pl.BlockSpec((l, tm), lambda i, j, kk: (0, i)),
               b_spec=pl.BlockSpec((l, tn), lambda i, j, kk: (0, j)),
               o_spec=pl.BlockSpec((tm, tn), lambda i, j, kk: (i, j)),
               out_shape=jax.ShapeDtypeStruct((m, n), out_dtype),
               grid=(m // tm, n // tn, 1), contract=(0, 0), nk=1, name=name, after=after)


def _pair_split(shard):
    left = shard % ATTN_BLOCK
    assert left in (0, CHUNK) and shard > left
    return shard - left, left


def _mm_up(cn, w_up_blocks, after):
    l, d = cn.shape
    n, _, shard = w_up_blocks.shape
    main, left = _pair_split(shard)
    tm = _divisor_tile(l, 544, 16)

    def body(a_ref, b_ref, after_ref, o_ref):
        a = a_ref[...]
        for s in range(2):
            o_ref[:, s * shard:s * shard + main] = _dot(a, b_ref[s, :, 0:main])
        if left:
            tail = _dot(a, jnp.concatenate([b_ref[0, :, main:], b_ref[1, :, main:]], axis=1))
            o_ref[:, main:shard] = tail[:, 0:left]
            o_ref[:, shard + main:2 * shard] = tail[:, left:]

    return pl.pallas_call(
        body, name="mm_up", out_shape=jax.ShapeDtypeStruct((l, n * shard), F32), grid=(l // tm, n // 2),
        in_specs=[pl.BlockSpec((tm, d), lambda i, j: (i, 0)),
                  pl.BlockSpec((2, d, shard), lambda i, j: (j, 0, 0)), ANY],
        out_specs=pl.BlockSpec((tm, 2 * shard), lambda i, j: (i, j)),
        compiler_params=_params(("parallel", "parallel")),
    )(cn, w_up_blocks, after)


def _mm_gw_up(cn, d_u):
    l, d = cn.shape
    _, _, d_ff = d_u.shape
    shard = 2 * d_ff // N_DEV
    pairs_per_half = d_ff // (2 * shard)
    tm = _divisor_tile(d, 512, 128)

    def body(a_ref, b_ref, o_ref):
        res = _dot_tn(a_ref[...], b_ref[...])
        o_ref[0] = res[:, 0:shard].astype(o_ref.dtype)
        o_ref[1] = res[:, shard:].astype(o_ref.dtype)

    return pl.pallas_call(
        body, name="mm_gw_up", out_shape=jax.ShapeDtypeStruct((N_DEV, d, shard), WIRE_DTYPE),
        grid=(d // tm, N_DEV // 2),
        in_specs=[pl.BlockSpec((l, tm), lambda i, j: (0, i)),
                  pl.BlockSpec((None, l, 2 * shard), lambda i, j: (j // pairs_per_half, 0, j % pairs_per_half))],
        out_specs=pl.BlockSpec((2, tm, shard), lambda i, j: (j, i, 0)),
        compiler_params=_params(("parallel", "parallel")),
    )(cn, d_u)


def _mm_d_cn(d_u, w_up_blocks, after):
    _, l, d_ff = d_u.shape
    n, d, shard = w_up_blocks.shape
    per = d_ff // shard
    main, left = _pair_split(shard)
    tm, tn = _divisor_tile(l, 544, 16), _divisor_tile(d, 256, 128)

    def body(a_ref, b_ref, after_ref, o_ref):
        acc = None
        for k in range(0, n, 2):
            half, c0 = k // per, (k % per) * shard
            parts = [_dot_nt(a_ref[half, :, c0 + s * shard:c0 + s * shard + main], b_ref[k + s, :, 0:main])
                     for s in range(2)]
            if left:
                a_tail = jnp.concatenate([a_ref[half, :, c0 + s * shard + main:c0 + (s + 1) * shard] for s in range(2)],
                                         axis=1)
                b_tail = jnp.concatenate([b_ref[k + s, :, main:] for s in range(2)], axis=1)
                parts.append(_dot_nt(a_tail, b_tail))
            for part in parts:
                acc = part if acc is None else acc + part
        o_ref[...] = acc

    return pl.pallas_call(
        body, name="mm_d_cn", out_shape=jax.ShapeDtypeStruct((l, d), F32), grid=(l // tm, d // tn),
        in_specs=[pl.BlockSpec((2, tm, d_ff), lambda i, j: (0, i, 0)),
                  pl.BlockSpec((n, tn, shard), lambda i, j: (0, j, 0)), ANY],
        out_specs=pl.BlockSpec((tm, tn), lambda i, j: (i, j)),
        compiler_params=_params(("parallel", "parallel")),
    )(d_u, w_up_blocks, after)


def _row_tile(l):
    return _divisor_tile(l, 544, 8)


def _rmsnorm_fwd(h, gain, name, res=None):
    l, d = h.shape
    tr = _row_tile(l)
    row = pl.BlockSpec((tr, d), lambda i: (i, 0))
    vec = pl.BlockSpec((1, d), lambda i: (0, 0))

    def body(*refs):
        if res is None:
            h_ref, g_ref, n_ref = refs
            x = h_ref[...]
        else:
            h_ref, r_ref, g_ref, s_ref, n_ref = refs
            x = h_ref[...] + r_ref[...]
            s_ref[...] = x
        y = x * lax.rsqrt(jnp.mean(x * x, axis=-1, keepdims=True) + NORM_EPS)
        n_ref[...] = (y * g_ref[...]).astype(n_ref.dtype)

    normed = jax.ShapeDtypeStruct((l, d), MXU_DTYPE)
    if res is None:
        return pl.pallas_call(body, name=name, out_shape=normed, grid=(l // tr,), in_specs=[row, vec],
                              out_specs=row, compiler_params=_params(("parallel",)))(h, gain)
    return pl.pallas_call(body, name=name, out_shape=(jax.ShapeDtypeStruct((l, d), F32), normed),
                          grid=(l // tr,), in_specs=[row, row, vec], out_specs=(row, row),
                          compiler_params=_params(("parallel",)))(h, res, gain)


def _rmsnorm_bwd(d_res, d_normed, x, gain, name, with_mxu_copy):
    l, d = x.shape
    tr = _row_tile(l) if with_mxu_copy else CHUNK
    row = pl.BlockSpec((tr, d), lambda i: (i, 0))
    vec = pl.BlockSpec((1, d), lambda i: (0, 0))

    def body(dres_ref, dn_ref, x_ref, g_ref, dx_ref, other_ref, dg_ref):
        i = pl.program_id(0)
        xv = x_ref[...]
        r = lax.rsqrt(jnp.mean(xv * xv, axis=-1, keepdims=True) + NORM_EPS)
        xh = xv * r
        dn = dn_ref[...]
        dxh = dn * g_ref[...]
        dx = dres_ref[...] + r * (dxh - xh * jnp.mean(dxh * xh, axis=-1, keepdims=True))
        if with_mxu_copy:
            dx_ref[...] = dx
            other_ref[...] = dx.astype(MXU_DTYPE)
        else:
            @pl.when(i == 0)
            def _():
                dx_ref[...] = dx

            @pl.when(i > 0)
            def _():
                other_ref[...] = dx

        @pl.when(i == 0)
        def _():
            dg_ref[...] = jnp.zeros_like(dg_ref)

        dg_ref[...] += jnp.sum(dn * xh, axis=0, keepdims=True)

    if with_mxu_copy:
        outs = [jax.ShapeDtypeStruct((l, d), F32), jax.ShapeDtypeStruct((l, d), MXU_DTYPE)]
        specs = [row, row]
    else:
        outs = [jax.ShapeDtypeStruct((CHUNK, d), F32), jax.ShapeDtypeStruct((l - CHUNK, d), F32)]
        specs = [pl.BlockSpec((CHUNK, d), lambda i: (0, 0)), pl.BlockSpec((CHUNK, d), lambda i: (jnp.maximum(i - 1, 0), 0))]
    outs.append(jax.ShapeDtypeStruct((1, d), F32))
    specs.append(vec)
    return pl.pallas_call(body, name=name, out_shape=tuple(outs), grid=(l // tr,),
                          in_specs=[row, row, row, vec], out_specs=tuple(specs),
                          compiler_params=_params(("arbitrary",)))(d_res, d_normed, x, gain)


def _loss_head(h1, mlp_out, gain, target):
    l, d = h1.shape
    n_blocks = l // CHUNK
    row = pl.BlockSpec((CHUNK, d), lambda i: (i, 0))
    vec = pl.BlockSpec((1, d), lambda i: (0, 0))
    tgt = pl.BlockSpec((CHUNK, d), lambda i: (jnp.maximum(i - 1, 0), 0))

    def body(h_ref, m_ref, g_ref, t_ref, dh_ref, dhb_ref, dg_ref, loss_ref, sq_ref):
        i = pl.program_id(0)
        x = h_ref[...] + m_ref[...]
        r = lax.rsqrt(jnp.mean(x * x, axis=-1, keepdims=True) + NORM_EPS)
        xh = x * r
        g = g_ref[...]
        real = i >= 1
        err = jnp.where(real, xh * g - t_ref[...], 0.0)
        dy = err * (1.0 / d)
        dxh = dy * g
        dh = r * (dxh - xh * jnp.mean(dxh * xh, axis=-1, keepdims=True))
        dh_ref[...] = dh
        dhb_ref[...] = dh.astype(MXU_DTYPE)

        @pl.when(i == 0)
        def _():
            dg_ref[...] = jnp.zeros_like(dg_ref)
            sq_ref[...] = jnp.zeros_like(sq_ref)

        dg_ref[...] += jnp.sum(dy * xh, axis=0, keepdims=True)
        sq_ref[...] += jnp.sum(err * err, axis=0, keepdims=True)

        @pl.when(i == n_blocks - 1)
        def _():
            total = jnp.sum(sq_ref[...], axis=-1, keepdims=True) * (0.5 / d)
            loss_ref[...] = jnp.broadcast_to(total, (1, 128))

    return pl.pallas_call(
        body, name="loss_head",
        out_shape=(jax.ShapeDtypeStruct((l, d), F32), jax.ShapeDtypeStruct((l, d), MXU_DTYPE),
                   jax.ShapeDtypeStruct((1, d), F32), jax.ShapeDtypeStruct((1, 128), F32)),
        grid=(n_blocks,), in_specs=[row, row, vec, tgt],
        out_specs=(row, row, vec, pl.BlockSpec((1, 128), lambda i: (0, 0))),
        scratch_shapes=[pltpu.VMEM((1, d), F32)],
        compiler_params=_params(("arbitrary",)),
    )(h1, mlp_out, gain, target)


def _dot(a, b):
    return jnp.dot(a, b, preferred_element_type=F32)


def _dot_nt(a, b):
    return lax.dot_general(a, b, (((1,), (1,)), ((), ())), preferred_element_type=F32)


def _dot_tn(a, b):
    return lax.dot_general(a, b, (((0,), (0,)), ((), ())), preferred_element_type=F32)


def _rope(t, cos2, sin2):
    return t * cos2 + pltpu.roll(t, HEAD_DIM // 2, 1) * sin2


def _rope_bwd(dr, cos2, sin2):
    return dr * cos2 + pltpu.roll(dr * sin2, HEAD_DIM // 2, 1)


def _sigmoid(x):
    return 1.0 / (1.0 + jnp.exp(-x))


def _row_valid(block, rows):
    r = block * CHUNK + lax.broadcasted_iota(jnp.int32, (rows, 1), 0)
    return r >= PAD_ROWS


def _retention_consts(l):
    pos = jnp.arange(l, dtype=F32) - PAD_ROWS
    inv_freq = 1.0 / (ROPE_BASE ** (jnp.arange(0, HEAD_DIM, 2, dtype=F32) / HEAD_DIM))
    ang = pos[:, None] * inv_freq[None, :]
    cos, sin = jnp.cos(ang), jnp.sin(ang)
    cos2 = jnp.concatenate([cos, cos], axis=-1)
    sin2 = jnp.concatenate([-sin, sin], axis=-1)
    log_g = jnp.log1p(-jnp.exp2(-5.0 - jnp.arange(N_HEADS, dtype=F32)))
    idx = jnp.arange(CHUNK, dtype=F32)
    diff = idx[:, None] - idx[None, :]
    decay = jnp.where(diff >= 0, jnp.exp(jnp.maximum(diff, 0.0)[None] * log_g[:, None, None]), 0.0)
    xi = jnp.exp((idx + 1.0)[None, :] * log_g[:, None])
    zeta = jnp.exp((CHUNK - 1.0 - idx)[None, :] * log_g[:, None])
    g_chunk = jnp.exp(CHUNK * log_g)
    bcast = lambda v: jnp.broadcast_to(v[:, :, None], (N_HEADS, CHUNK, HEAD_DIM))
    g_rows = jnp.broadcast_to(g_chunk[:, None, None], (N_HEADS, 8, HEAD_DIM))
    return cos2, sin2, decay, bcast(xi), bcast(zeta), g_rows


def _retention_fwd(proj, ret_gain, consts):
    l = proj.shape[0]
    n_chunks = l // CHUNK
    cos2, sin2, decay, xi, zeta, g_rows = consts
    scale = HEAD_DIM ** -0.5

    def body(p_ref, cos_ref, sin_ref, dec_ref, xi_ref, zeta_ref, gr_ref, gain_ref,
             mix_ref, o_ref, st_ref, state):
        c = pl.program_id(0)

        @pl.when(c == 0)
        def _():
            state[...] = jnp.zeros_like(state)

        cos_v, sin_v = cos_ref[...], sin_ref[...]
        valid = _row_valid(c, CHUNK)
        for h in range(N_HEADS):
            cols = slice(h * HEAD_DIM, (h + 1) * HEAD_DIM)
            q = p_ref[:, h * HEAD_DIM:(h + 1) * HEAD_DIM]
            k = p_ref[:, GROUP + h * HEAD_DIM:GROUP + (h + 1) * HEAD_DIM]
            v = p_ref[:, 2 * GROUP + h * HEAD_DIM:2 * GROUP + (h + 1) * HEAD_DIM]
            g = p_ref[:, 3 * GROUP + h * HEAD_DIM:3 * GROUP + (h + 1) * HEAD_DIM]
            rq = _rope(q, cos_v, sin_v).astype(MXU_DTYPE)
            rk = _rope(k, cos_v, sin_v) * scale
            rkb = rk.astype(MXU_DTYPE)
            vb = v.astype(MXU_DTYPE)
            st = state[h]
            st_ref[h] = st
            s = _dot_nt(rq, rkb) * dec_ref[h]
            o = _dot(s.astype(MXU_DTYPE), vb) + _dot(rq, st.astype(MXU_DTYPE)) * xi_ref[h]
            kz = (rk * zeta_ref[h]).astype(MXU_DTYPE)
            state[h] = gr_ref[h, 0:1, :] * st + _dot_tn(kz, vb)
            o_ref[:, cols] = o
            mu = jnp.mean(o, axis=-1, keepdims=True)
            oc = o - mu
            yn = oc * lax.rsqrt(jnp.mean(oc * oc, axis=-1, keepdims=True) + NORM_EPS)
            ret = (g * _sigmoid(g)) * (yn * gain_ref[:, cols])
            mix_ref[:, cols] = jnp.where(valid, ret, 0.0).astype(mix_ref.dtype)

    head_tab = pl.BlockSpec((N_HEADS, CHUNK, HEAD_DIM), lambda c: (0, 0, 0))
    return pl.pallas_call(
        body, name="retention_fwd",
        out_shape=(jax.ShapeDtypeStruct((l, 2 * GROUP), MXU_DTYPE), jax.ShapeDtypeStruct((l, GROUP), F32),
                   jax.ShapeDtypeStruct((n_chunks, N_HEADS, HEAD_DIM, HEAD_DIM), F32)),
        grid=(n_chunks,),
        in_specs=[pl.BlockSpec((CHUNK, 4 * GROUP), lambda c: (c, 0)),
                  pl.BlockSpec((CHUNK, HEAD_DIM), lambda c: (c, 0)),
                  pl.BlockSpec((CHUNK, HEAD_DIM), lambda c: (c, 0)),
                  head_tab, head_tab, head_tab,
                  pl.BlockSpec((N_HEADS, 8, HEAD_DIM), lambda c: (0, 0, 0)),
                  pl.BlockSpec((1, GROUP), lambda c: (0, 0))],
        out_specs=(pl.BlockSpec((CHUNK, GROUP), lambda c: (c, 0)),
                   pl.BlockSpec((CHUNK, GROUP), lambda c: (c, 0)),
                   pl.BlockSpec((None, N_HEADS, HEAD_DIM, HEAD_DIM), lambda c: (c, 0, 0, 0))),
        scratch_shapes=[pltpu.VMEM((N_HEADS, HEAD_DIM, HEAD_DIM), F32)],
        compiler_params=_params(("arbitrary",)),
    )(proj, cos2, sin2, decay, xi, zeta, g_rows, ret_gain)


def _retention_bwd(proj, o_pre, states, d_mix, ret_gain, consts):
    l = proj.shape[0]
    n_chunks = l // CHUNK
    cos2, sin2, decay, xi, zeta, g_rows = consts
    scale = HEAD_DIM ** -0.5
    rev = lambda c: n_chunks - 1 - c

    def body(p_ref, o_ref, st_ref, dm_ref, cos_ref, sin_ref, dec_ref, dect_ref, xi_ref, zeta_ref, gr_ref, gain_ref,
             dp_ref, dgain_ref, dstate):
        step = pl.program_id(0)

        @pl.when(step == 0)
        def _():
            dstate[...] = jnp.zeros_like(dstate)
            dgain_ref[...] = jnp.zeros_like(dgain_ref)

        cos_v, sin_v = cos_ref[...], sin_ref[...]
        valid = _row_valid(rev(step), CHUNK)
        for h in range(N_HEADS):
            cols = slice(h * HEAD_DIM, (h + 1) * HEAD_DIM)
            q = p_ref[:, h * HEAD_DIM:(h + 1) * HEAD_DIM]
            k = p_ref[:, GROUP + h * HEAD_DIM:GROUP + (h + 1) * HEAD_DIM]
            v = p_ref[:, 2 * GROUP + h * HEAD_DIM:2 * GROUP + (h + 1) * HEAD_DIM]
            g = p_ref[:, 3 * GROUP + h * HEAD_DIM:3 * GROUP + (h + 1) * HEAD_DIM]
            o = o_ref[:, cols]
            gain = gain_ref[:, cols]
            d_ret = jnp.where(valid, dm_ref[:, cols], 0.0)
            mu = jnp.mean(o, axis=-1, keepdims=True)
            oc = o - mu
            rstd = lax.rsqrt(jnp.mean(oc * oc, axis=-1, keepdims=True) + NORM_EPS)
            yn = oc * rstd
            sig = _sigmoid(g)
            gate = g * sig
            dgain_ref[:, cols] += jnp.sum(d_ret * gate * yn, axis=0, keepdims=True)
            d_g = d_ret * (yn * gain) * (sig * (1.0 + g * (1.0 - sig)))
            d_yn = d_ret * gate * gain
            d_o = rstd * (d_yn - jnp.mean(d_yn, axis=-1, keepdims=True)
                          - yn * jnp.mean(d_yn * yn, axis=-1, keepdims=True))
            rq = _rope(q, cos_v, sin_v)
            rk = _rope(k, cos_v, sin_v) * scale
            rqb, rkb, vb = rq.astype(MXU_DTYPE), rk.astype(MXU_DTYPE), v.astype(MXU_DTYPE)
            dob = d_o.astype(MXU_DTYPE)
            dec = dec_ref[h]
            xi_h, zeta_h = xi_ref[h], zeta_ref[h]
            st_b = st_ref[h].astype(MXU_DTYPE)
            dst = dstate[h]
            dst_b = dst.astype(MXU_DTYPE)
            dec_t = dect_ref[h]
            s_t_b = (_dot_nt(rkb, rqb) * dec_t).astype(MXU_DTYPE)
            da_b = (_dot_nt(dob, vb) * dec).astype(MXU_DTYPE)
            da_t_b = (_dot_nt(vb, dob) * dec_t).astype(MXU_DTYPE)
            doxi_b = (d_o * xi_h).astype(MXU_DTYPE)
            kz_b = (rk * zeta_h).astype(MXU_DTYPE)
            d_rq = _dot(da_b, rkb) + _dot_nt(doxi_b, st_b)
            d_rk = _dot(da_t_b, rqb) + _dot_nt(vb, dst_b) * zeta_h
            d_v = _dot(s_t_b, dob) + _dot(kz_b, dst_b)
            dstate[h] = gr_ref[h, 0:1, :] * dst + _dot_tn(rqb, doxi_b)
            d_q = _rope_bwd(d_rq, cos_v, sin_v)
            d_k = _rope_bwd(d_rk * scale, cos_v, sin_v)
            dp_ref[:, h * HEAD_DIM:(h + 1) * HEAD_DIM] = d_q.astype(dp_ref.dtype)
            dp_ref[:, GROUP + h * HEAD_DIM:GROUP + (h + 1) * HEAD_DIM] = d_k.astype(dp_ref.dtype)
            dp_ref[:, 2 * GROUP + h * HEAD_DIM:2 * GROUP + (h + 1) * HEAD_DIM] = d_v.astype(dp_ref.dtype)
            dp_ref[:, 3 * GROUP + h * HEAD_DIM:3 * GROUP + (h + 1) * HEAD_DIM] = d_g.astype(dp_ref.dtype)

    head_tab = pl.BlockSpec((N_HEADS, CHUNK, HEAD_DIM), lambda c: (0, 0, 0))
    return pl.pallas_call(
        body, name="retention_bwd",
        out_shape=(jax.ShapeDtypeStruct((l, 4 * GROUP), MXU_DTYPE), jax.ShapeDtypeStruct((1, GROUP), F32)),
        grid=(n_chunks,),
        in_specs=[pl.BlockSpec((CHUNK, 4 * GROUP), lambda c: (rev(c), 0)),
                  pl.BlockSpec((CHUNK, GROUP), lambda c: (rev(c), 0)),
                  pl.BlockSpec((None, N_HEADS, HEAD_DIM, HEAD_DIM), lambda c: (rev(c), 0, 0, 0)),
                  pl.BlockSpec((CHUNK, GROUP), lambda c: (rev(c), 0)),
                  pl.BlockSpec((CHUNK, HEAD_DIM), lambda c: (rev(c), 0)),
                  pl.BlockSpec((CHUNK, HEAD_DIM), lambda c: (rev(c), 0)),
                  head_tab, head_tab, head_tab, head_tab,
                  pl.BlockSpec((N_HEADS, 8, HEAD_DIM), lambda c: (0, 0, 0)),
                  pl.BlockSpec((1, GROUP), lambda c: (0, 0))],
        out_specs=(pl.BlockSpec((CHUNK, 4 * GROUP), lambda c: (rev(c), 0)),
                   pl.BlockSpec((1, GROUP), lambda c: (0, 0))),
        scratch_shapes=[pltpu.VMEM((N_HEADS, HEAD_DIM, HEAD_DIM), F32)],
        compiler_params=_params(("arbitrary",)),
    )(proj, o_pre, states, d_mix, cos2, sin2, decay, jnp.transpose(decay, (0, 2, 1)), xi, zeta, g_rows, ret_gain)


FF_TILE = (7 * GROUP) // 128


def _log_forget(ff, bias_row, valid):
    x = ff + bias_row
    e = jnp.exp(-jnp.abs(x))
    lf = jnp.minimum(x, 0.0) - jnp.log(1.0 + e)
    head_lane = lax.broadcasted_iota(jnp.int32, x.shape, 1) < N_HEADS
    keep = lambda t: jnp.where(head_lane, jnp.where(valid, t, 0.0), 0.0)
    return keep(lf), keep(jnp.where(x >= 0, e, 1.0) / (1.0 + e))


def _fox_prep(proj, bias_row):
    l = proj.shape[0]
    n_blocks = l // CHUNK

    def body(ff_ref, b_ref, bc_ref, rows_ref, cum):
        r = lax.broadcasted_iota(jnp.int32, (CHUNK, CHUNK), 0)
        cidx = lax.broadcasted_iota(jnp.int32, (CHUNK, CHUNK), 1)
        tri = jnp.where(r >= cidx, 1.0, 0.0).astype(F32)
        carry = jnp.zeros((1, 128), F32)
        for blk in range(n_blocks):
            rows = slice(blk * CHUNK, (blk + 1) * CHUNK)
            valid = _row_valid(blk, CHUNK)
            lf, _ = _log_forget(ff_ref[rows, :], b_ref[...], valid)
            local = jnp.dot(tri, lf, precision=lax.Precision.HIGHEST, preferred_element_type=F32) + carry
            carry = local[CHUNK - 1:CHUNK, :]
            masked = jnp.where(valid, local, -NEG_BIG)
            cum[rows, :] = masked
            t = masked.T
            for h in range(N_HEADS):
                rows_ref[h, :, rows] = t[h:h + 1, :]
        full = cum[...]
        for h in range(N_HEADS):
            bc_ref[h] = jnp.broadcast_to(full[:, h:h + 1], (l, 128))

    return pl.pallas_call(
        body, name="fox_prep",
        out_shape=(jax.ShapeDtypeStruct((N_HEADS, l, 128), F32), jax.ShapeDtypeStruct((N_HEADS, 1, l), F32)),
        grid=(1,),
        in_specs=[pl.BlockSpec((l, 128), lambda i: (0, FF_TILE)), pl.BlockSpec((1, 128), lambda i: (0, 0))],
        out_specs=(pl.BlockSpec((N_HEADS, l, 128), lambda i: (0, 0, 0)),
                   pl.BlockSpec((N_HEADS, 1, l), lambda i: (0, 0, 0))),
        scratch_shapes=[pltpu.VMEM((l, 128), F32)],
        compiler_params=_params(("arbitrary",)),
    )(proj, bias_row)


ATTN_BLOCK = 2 * CHUNK


def _attn_blocks(l):
    assert (l - CHUNK) % ATTN_BLOCK == 0
    return [(0, CHUNK)] + [(s, ATTN_BLOCK) for s in range(CHUNK, l, ATTN_BLOCK)]


def _rows_valid(start, size):
    return start + lax.broadcasted_iota(jnp.int32, (size, 1), 0) >= PAD_ROWS


def _fox_fwd(proj, cum_bc, cum_rows, mix):
    l = proj.shape[0]
    blocks = _attn_blocks(l)
    scale = HEAD_DIM ** -0.5
    qt, kt, vt = 4 * N_HEADS, 5 * N_HEADS, 6 * N_HEADS

    def body(q_ref, k_ref, v_ref, cbc_ref, crow_ref, mix_in, o_ref, lse_ref, qb_s, kb_s, vb_s):
        qb_s[...] = q_ref[...].astype(MXU_DTYPE)
        kb_s[...] = k_ref[...].astype(MXU_DTYPE)
        vb_s[...] = v_ref[...].astype(MXU_DTYPE)
        for p, (qs, qn) in enumerate(blocks):
            qb = qb_s[qs:qs + qn, :]
            cq = cbc_ref[qs:qs + qn, :]
            m = jnp.full((qn, 1), NEG_BIG, F32)
            lsum = jnp.zeros((qn, 1), F32)
            acc = jnp.zeros((qn, HEAD_DIM), F32)
            for j in range(p + 1):
                ks, kn = blocks[j]
                bias = jnp.tile(cq, (1, kn // CHUNK)) - crow_ref[:, ks:ks + kn]
                s = _dot_nt(qb, kb_s[ks:ks + kn, :]) * scale + bias
                if j == p:
                    q_pos = qs + lax.broadcasted_iota(jnp.int32, (qn, kn), 0)
                    k_pos = ks + lax.broadcasted_iota(jnp.int32, (qn, kn), 1)
                    s = jnp.where(k_pos <= q_pos, s, NEG_BIG)
                m_new = jnp.maximum(m, jnp.max(s, axis=-1, keepdims=True))
                alpha = jnp.exp(m - m_new)
                pr = jnp.exp(s - m_new)
                lsum = lsum * alpha + jnp.sum(pr, axis=-1, keepdims=True)
                acc = acc * alpha + _dot(pr.astype(MXU_DTYPE), vb_s[ks:ks + kn, :])
                m = m_new
            o = jnp.where(_rows_valid(qs, qn), acc * (1.0 / lsum), 0.0)
            o_ref[qs:qs + qn, :] = o.astype(o_ref.dtype)
            lse = m + jnp.log(lsum)
            lse_ref[:, qs:qs + qn] = jnp.broadcast_to(lse, (qn, CHUNK)).T[0:1, :]

    head_col = lambda t: pl.BlockSpec((l, HEAD_DIM), lambda h: (0, t + h))
    return pl.pallas_call(
        body, name="fox_fwd",
        out_shape=(jax.ShapeDtypeStruct(mix.shape, mix.dtype), jax.ShapeDtypeStruct((N_HEADS, 1, l), F32)),
        grid=(N_HEADS,),
        in_specs=[head_col(qt), head_col(kt), head_col(vt),
                  pl.BlockSpec((None, l, 128), lambda h: (h, 0, 0)),
                  pl.BlockSpec((None, 1, l), lambda h: (h, 0, 0)),
                  ANY],
        out_specs=(head_col(N_HEADS), pl.BlockSpec((None, 1, l), lambda h: (h, 0, 0))),
        input_output_aliases={5: 0},
        scratch_shapes=[pltpu.VMEM((l, HEAD_DIM), MXU_DTYPE)] * 3,
        compiler_params=_params(("parallel",)),
    )(proj, proj, proj, cum_bc, cum_rows, mix)


def _fox_bwd(proj, cum_bc, cum_rows, d_mix, lse_rows):
    l = proj.shape[0]
    blocks = _attn_blocks(l)
    scale = HEAD_DIM ** -0.5
    qt, kt, vt = 4 * N_HEADS, 5 * N_HEADS, 6 * N_HEADS

    def body(q_ref, k_ref, v_ref, do_ref, cbc_ref, crow_ref, lse_ref,
             dq_ref, dk_ref, dv_ref, ds_ref, dk_acc, dv_acc, qb_s, kb_s, vb_s, dob_s, p_s, dp_s):
        qb_s[...] = q_ref[...].astype(MXU_DTYPE)
        kb_s[...] = k_ref[...].astype(MXU_DTYPE)
        vb_s[...] = v_ref[...].astype(MXU_DTYPE)
        dob_s[...] = jnp.where(_rows_valid(0, l), do_ref[...], 0.0).astype(MXU_DTYPE)
        dk_acc[...] = jnp.zeros_like(dk_acc)
        dv_acc[...] = jnp.zeros_like(dv_acc)
        ds_ref[...] = jnp.zeros_like(ds_ref)
        shift_row = crow_ref[...] - lse_ref[...]

        for p, (qs, qn) in enumerate(blocks):
            qb, dob = qb_s[qs:qs + qn, :], dob_s[qs:qs + qn, :]
            shift = shift_row[:, qs:qs + qn]

            delta = jnp.zeros((1, qn), F32)
            for j in range(p + 1):
                ks, kn = blocks[j]
                ck = jnp.tile(cbc_ref[ks:ks + kn, :], (1, qn // CHUNK))
                s_t = _dot_nt(kb_s[ks:ks + kn, :], qb) * scale + (shift - ck)
                if j == p:
                    k_pos = ks + lax.broadcasted_iota(jnp.int32, (kn, qn), 0)
                    q_pos = qs + lax.broadcasted_iota(jnp.int32, (kn, qn), 1)
                    s_t = jnp.where(k_pos <= q_pos, s_t, NEG_BIG)
                p_t, dp_t = jnp.exp(s_t), _dot_nt(vb_s[ks:ks + kn, :], dob)
                p_s[j, 0:kn, 0:qn] = p_t
                dp_s[j, 0:kn, 0:qn] = dp_t
                delta = delta + jnp.sum(p_t * dp_t, axis=0, keepdims=True)
            dq = jnp.zeros((qn, HEAD_DIM), F32)
            for j in range(p + 1):
                ks, kn = blocks[j]
                rows = slice(ks, ks + kn)
                p_t, dp_t = p_s[j, 0:kn, 0:qn], dp_s[j, 0:kn, 0:qn]
                ds_t = p_t * (dp_t - delta)
                ds_b = ds_t.astype(MXU_DTYPE)
                dv_acc[rows, :] += _dot(p_t.astype(MXU_DTYPE), dob)
                dk_acc[rows, :] += _dot(ds_b, qb) * scale
                ds_ref[rows, :] += sum(ds_t[:, c:c + CHUNK] for c in range(0, qn, CHUNK))
                dq = dq + _dot_tn(ds_b, kb_s[rows, :])
            dq_ref[qs:qs + qn, :] = (dq * scale).astype(dq_ref.dtype)

        dk_ref[...] = dk_acc[...].astype(dk_ref.dtype)
        dv_ref[...] = dv_acc[...].astype(dv_ref.dtype)

    col = jax.ShapeDtypeStruct((l, GROUP), MXU_DTYPE)
    head_col = lambda t: pl.BlockSpec((l, HEAD_DIM), lambda h: (0, t + h))
    return pl.pallas_call(
        body, name="fox_bwd",
        out_shape=(col, col, col, jax.ShapeDtypeStruct((N_HEADS, l, 128), F32)),
        grid=(N_HEADS,),
        in_specs=[head_col(qt), head_col(kt), head_col(vt), head_col(N_HEADS),
                  pl.BlockSpec((None, l, 128), lambda h: (h, 0, 0)),
                  pl.BlockSpec((None, 1, l), lambda h: (h, 0, 0)),
                  pl.BlockSpec((None, 1, l), lambda h: (h, 0, 0))],
        out_specs=(head_col(0), head_col(0), head_col(0), pl.BlockSpec((None, l, 128), lambda h: (h, 0, 0))),
        scratch_shapes=([pltpu.VMEM((l, HEAD_DIM), F32)] * 2 + [pltpu.VMEM((l, HEAD_DIM), MXU_DTYPE)] * 4
                        + [pltpu.VMEM((len(blocks), ATTN_BLOCK, ATTN_BLOCK), F32)] * 2),
        compiler_params=_params(("parallel",)),
    )(proj, proj, proj, d_mix, cum_bc, cum_rows, lse_rows)


def _fox_gate_bwd(ds_sum, proj, bias_row):
    l = proj.shape[0]
    n_blocks = l // CHUNK

    def body(ds_ref, ff_ref, b_ref, dff_ref, db_ref):
        r = lax.broadcasted_iota(jnp.int32, (CHUNK, CHUNK), 0)
        cidx = lax.broadcasted_iota(jnp.int32, (CHUNK, CHUNK), 1)
        upper = jnp.where(cidx >= r, 1.0, 0.0).astype(F32)
        carry = jnp.zeros((1, 128), F32)
        db = jnp.zeros((1, 128), F32)
        for blk in reversed(range(n_blocks)):
            rows = slice(blk * CHUNK, (blk + 1) * CHUNK)
            key_sum = jnp.zeros((CHUNK, 128), F32)
            for h in range(N_HEADS):
                select = jnp.where(cidx == h, 1.0, 0.0).astype(F32)
                key_sum = key_sum + jnp.dot(ds_ref[h, rows, :], select, precision=lax.Precision.HIGHEST,
                                            preferred_element_type=F32)
            suffix = jnp.dot(upper, key_sum, precision=lax.Precision.HIGHEST, preferred_element_type=F32) + carry
            carry = suffix[0:1, :]
            _, dsig = _log_forget(ff_ref[rows, :], b_ref[...], _row_valid(blk, CHUNK))
            dff = -suffix * dsig
            dff_ref[rows, :] = dff.astype(dff_ref.dtype)
            db = db + jnp.sum(dff, axis=0, keepdims=True)
        db_ref[...] = db

    return pl.pallas_call(
        body, name="fox_gate_bwd",
        out_shape=(jax.ShapeDtypeStruct((l, 128), MXU_DTYPE), jax.ShapeDtypeStruct((1, 128), F32)),
        grid=(1,),
        in_specs=[pl.BlockSpec((N_HEADS, l, 128), lambda i: (0, 0, 0)),
                  pl.BlockSpec((l, 128), lambda i: (0, FF_TILE)),
                  pl.BlockSpec((1, 128), lambda i: (0, 0))],
        out_specs=(pl.BlockSpec((l, 128), lambda i: (0, 0)), pl.BlockSpec((1, 128), lambda i: (0, 0))),
        compiler_params=_params(("arbitrary",)),
    )(ds_sum, proj, bias_row)


def _conv(u, w, b):
    return b + w[0:1, :] * pltpu.roll(u, 2, 0) + w[1:2, :] * pltpu.roll(u, 1, 0) + w[2:3, :] * u


def _conv_act_fwd(u, conv_w, conv_b, d_ff):
    l = u.shape[0]
    tc = _divisor_tile(d_ff, 256, 128)
    nt = d_ff // tc

    def body(ug_ref, uv_ref, wg_ref, wv_ref, bg_ref, bv_ref, a_ref, y_ref):
        yg = _conv(ug_ref[...], wg_ref[...], bg_ref[...])
        yv = _conv(uv_ref[...], wv_ref[...], bv_ref[...])
        act = yg * _sigmoid(yg) * yv
        a_ref[...] = jnp.where(_row_valid(0, l), act, 0.0).astype(a_ref.dtype)
        y_ref[0] = yg.astype(y_ref.dtype)
        y_ref[1] = yv.astype(y_ref.dtype)

    return pl.pallas_call(
        body, name="conv_act_fwd",
        out_shape=(jax.ShapeDtypeStruct((l, d_ff), MXU_DTYPE), jax.ShapeDtypeStruct((2, l, d_ff), MXU_DTYPE)),
        grid=(nt,),
        in_specs=[pl.BlockSpec((l, tc), lambda j: (0, j)), pl.BlockSpec((l, tc), lambda j: (0, j + nt)),
                  pl.BlockSpec((8, tc), lambda j: (0, j)), pl.BlockSpec((8, tc), lambda j: (0, j + nt)),
                  pl.BlockSpec((1, tc), lambda j: (0, j)), pl.BlockSpec((1, tc), lambda j: (0, j + nt))],
        out_specs=(pl.BlockSpec((l, tc), lambda j: (0, j)), pl.BlockSpec((2, l, tc), lambda j: (0, 0, j))),
        compiler_params=_params(("parallel",)),
    )(u, u, conv_w, conv_w, conv_b, conv_b)


def _conv_act_bwd(u, y, conv_w, d_act, d_ff):
    l = u.shape[0]
    tc = _divisor_tile(d_ff, 256, 128)
    nt = d_ff // tc

    def body(ug_ref, uv_ref, y_ref, wg_ref, wv_ref, da_ref, du_ref, dwb_ref):
        valid = _row_valid(0, l)
        ug, uv = ug_ref[...], uv_ref[...]
        wg, wv = wg_ref[...], wv_ref[...]
        yg, yv = y_ref[0].astype(F32), y_ref[1].astype(F32)
        sig = _sigmoid(yg)
        da = jnp.where(valid, da_ref[...], 0.0)
        d_yv = da * (yg * sig)
        d_yg = da * yv * (sig * (1.0 + yg * (1.0 - sig)))
        for idx, (dy, uu, w) in enumerate(((d_yg, ug, wg), (d_yv, uv, wv))):
            du = w[2:3, :] * dy + w[1:2, :] * pltpu.roll(dy, l - 1, 0) + w[0:1, :] * pltpu.roll(dy, l - 2, 0)
            du_ref[idx] = jnp.where(valid, du, 0.0).astype(du_ref.dtype)
            dwb_ref[idx, 0:1, :] = jnp.sum(dy * pltpu.roll(uu, 2, 0), axis=0, keepdims=True)
            dwb_ref[idx, 1:2, :] = jnp.sum(dy * pltpu.roll(uu, 1, 0), axis=0, keepdims=True)
            dwb_ref[idx, 2:3, :] = jnp.sum(dy * uu, axis=0, keepdims=True)
            dwb_ref[idx, 3:4, :] = jnp.sum(dy, axis=0, keepdims=True)
            dwb_ref[idx, 4:8, :] = jnp.zeros((4, tc), F32)

    return pl.pallas_call(
        body, name="conv_act_bwd",
        out_shape=(jax.ShapeDtypeStruct((2, l, d_ff), MXU_DTYPE), jax.ShapeDtypeStruct((2, 8, d_ff), F32)),
        grid=(nt,),
        in_specs=[pl.BlockSpec((l, tc), lambda j: (0, j)), pl.BlockSpec((l, tc), lambda j: (0, j + nt)),
                  pl.BlockSpec((2, l, tc), lambda j: (0, 0, j)),
                  pl.BlockSpec((8, tc), lambda j: (0, j)), pl.BlockSpec((8, tc), lambda j: (0, j + nt)),
                  pl.BlockSpec((l, tc), lambda j: (0, j))],
        out_specs=(pl.BlockSpec((2, l, tc), lambda j: (0, 0, j)), pl.BlockSpec((2, 8, tc), lambda j: (0, 0, j))),
        compiler_params=_params(("parallel",)),
    )(u, u, y, conv_w, conv_w, d_act)


def _adamw(w, g, m, v, name):
    shape = w.shape
    if w.ndim == 1:
        as2d = (1, shape[0])
    else:
        as2d = (int(np.prod(shape[:-1])), shape[-1])
    r, c = as2d
    tr = _divisor_tile(r, 256, 8)
    spec = pl.BlockSpec((tr, c), lambda i: (i, 0))

    def body(w_ref, g_ref, m_ref, v_ref, d_ref, nm_ref, nv_ref):
        d_ref[...], nm_ref[...], nv_ref[...] = _adamw_math(w_ref[...], g_ref[...], m_ref[...], v_ref[...])

    sds = jax.ShapeDtypeStruct(as2d, F32)
    outs = pl.pallas_call(
        body, name=name, out_shape=(sds, sds, sds), grid=(r // tr,),
        in_specs=[spec] * 4, out_specs=(spec,) * 3,
        compiler_params=_params(("parallel",)),
    )(w.reshape(as2d), g.reshape(as2d), m.reshape(as2d), v.reshape(as2d))
    return tuple(o.reshape(shape) for o in outs)


def _pad_rows(a, rows):
    return jnp.pad(a, ((0, rows - a.shape[0]), (0, 0)))


def kernel(x, meta_tokens, norm1_gain, w_in, b_forget, ret_norm_gain, w_out, norm2_gain, w_up, conv_w, conv_b, w_down, final_norm_gain, loss_target, m_meta_tokens, m_norm1_gain, m_w_in, m_b_forget, m_ret_norm_gain, m_w_out, m_norm2_gain, m_w_up, m_conv_w, m_conv_b, m_w_down, m_final_norm_gain, v_meta_tokens, v_norm1_gain, v_w_in, v_b_forget, v_ret_norm_gain, v_w_out, v_norm2_gain, v_w_up, v_conv_w, v_conv_b, v_w_down, v_final_norm_gain):
    seq, d = x.shape[1], x.shape[2]
    l = CHUNK + seq
    d_ff = w_down.shape[1] * N_DEV
    up_shard = w_up.shape[2]
    assert 4 * up_shard == d_ff and w_in.shape[2] == WIN_SHARD and d == 2 * GROUP
    dev = _device_index()
    mx, my, mc = _my_position()
    core = jnp.reshape(mc, (1,)).astype(jnp.int32)
    chip = jnp.reshape(2 * mx + my, (1,)).astype(jnp.int32)
    dev1 = jnp.reshape(dev, (1,)).astype(jnp.int32)

    small = jnp.concatenate([meta_tokens.reshape(-1, 128), conv_w[0].reshape(-1, 128)], axis=0)
    n_meta_rows = N_META * (d // N_DEV) // 128
    small_rows = small.shape[0]
    small_all = _all_gather(_pad_rows(small, -(-small_rows // 8) * 8), "gather_small")
    meta_full = jnp.transpose(small_all[:, :n_meta_rows].reshape(N_DEV, N_META, d // N_DEV), (1, 0, 2)).reshape(N_META, d)
    conv_w_full = _pad_rows(jnp.transpose(small_all[:, n_meta_rows:small_rows].reshape(N_DEV, 3, up_shard),
                                          (1, 0, 2)).reshape(3, 2 * d_ff), 8)
    to_rows = lambda t: jnp.pad(jnp.transpose(t[0]), ((0, WIN_ROWS - WIN_SHARD), (0, 0)))
    from_rows = lambda t: jnp.transpose(t[:WIN_SHARD])[None]
    w_in_rows = to_rows(w_in)
    out_rows = d // N_DEV
    mixer_rows = -(-(WIN_ROWS + out_rows) // 304) * 304
    mixer_shard = jnp.concatenate([w_in_rows.astype(WIRE_DTYPE), w_out[0].astype(WIRE_DTYPE),
                                   jnp.zeros((mixer_rows - WIN_ROWS - out_rows, d), WIRE_DTYPE)], axis=0)

    h0 = jnp.concatenate([jnp.zeros((PAD_ROWS, d), F32), meta_full, x[0]], axis=0)
    consts = _retention_consts(l)
    bias_row = jnp.pad(b_forget, ((0, 0), (0, 128 - N_HEADS)))
    a = _rmsnorm_fwd(h0, norm1_gain, "rmsnorm1")
    mixer_blocks = _gather_ring(mixer_shard, dev1, a, "gather_w_in")
    start_up = _gather_start(w_up[0], dev1, mixer_blocks, "gather_w_up_start")
    w_in_full = _assemble_w_in(mixer_blocks).astype(MXU_DTYPE)
    proj = _mm_nt(a, w_in_full, F32, "mm_proj", after=start_up[4])
    ret_mix, ret_pre, ret_states = _retention_fwd(proj, ret_norm_gain, consts)
    cum_bc, cum_rows = _fox_prep(proj, bias_row)
    mix, lse_rows = _fox_fwd(proj, cum_bc, cum_rows, ret_mix)
    w_out_full = mixer_blocks[:, WIN_ROWS:WIN_ROWS + out_rows].reshape(d, d).astype(MXU_DTYPE)
    h1, cn = _rmsnorm_fwd(h0, norm2_gain, "resid_rmsnorm2", res=_mm_nn(mix, w_out_full, F32, "mm_out"))
    w_up_blocks = _gather_finish(start_up, cn, "gather_w_up").astype(MXU_DTYPE)
    start_down = _gather_start(w_down[0], dev1, w_up_blocks, "gather_w_down_start")
    u = _mm_up(cn, w_up_blocks, start_down[4])
    pass_down = _gather_pass_start(start_down, u, "gather_w_down")
    act, conv_y = _conv_act_fwd(u, conv_w_full, conv_b + pass_down[4][0, 0], d_ff)
    w_down_full = _gather_pass_finish(pass_down, act, "gather_w_down").reshape(d_ff, d).astype(MXU_DTYPE)
    mlp_out = _mm_nn(act, w_down_full, F32, "mm_down", tm_cap=544, tk_cap=d_ff)
    d_h2, d_h2_b, dg_final, loss_part = _loss_head(h1, mlp_out, final_norm_gain.reshape(1, d), loss_target[0])

    gw_down = _mm_tn(act, d_h2_b, WIRE_DTYPE, "mm_gw_down", tm_cap=1408, tn_cap=1024)
    d2d_down = _reduce_scatter_d2d_start(gw_down.reshape(N_DEV, d_ff // N_DEV, d), d_h2, "rs_w_down")
    d_act = _mm_nt(d_h2_b, w_down_full, F32, "mm_d_act", after=d2d_down[4])
    rs_down = _reduce_scatter_ici_start(d2d_down, d_act, core, "rs_w_down")
    d_u, d_conv = _conv_act_bwd(u, conv_y, conv_w_full + rs_down[4][0, 0], d_act, d_ff)
    tm = _divisor_tile(l, 1088, 16)
    gw_up = _mm_gw_up(cn, d_u)
    d2d_up = _reduce_scatter_d2d_start(gw_up, d_act, "rs_w_up")
    d_cn = _mm_d_cn(d_u, w_up_blocks, d2d_up[4])
    rs_up = _reduce_scatter_ici_start(d2d_up, d_cn, core, "rs_w_up")
    d_h1, d_h1_b, dg_norm2 = _rmsnorm_bwd(d_h2, d_cn, h1, norm2_gain + rs_up[4][0, 0], "rmsnorm2_bwd", True)

    gw_out = _mm_tn(mix, d_h1_b, WIRE_DTYPE, "mm_gw_out")
    d2d_out = _reduce_scatter_d2d_start(gw_out.reshape(N_DEV, d // N_DEV, d), d_cn, "rs_w_out")
    d_mix = _mm_nt(d_h1_b, w_out_full, F32, "mm_d_mix", after=d2d_out[4])
    d_fq, d_fk, d_fv, ds_sum = _fox_bwd(proj, cum_bc, cum_rows, d_mix, lse_rows)
    d_ff_tile, db_forget_row = _fox_gate_bwd(ds_sum, proj, bias_row)
    d_ret, dg_ret = _retention_bwd(proj, ret_pre, ret_states, d_mix, ret_norm_gain, consts)
    rs_out = _reduce_scatter_ici_start(d2d_out, d_ret, core, "rs_w_out")
    d_proj = jnp.concatenate(
        [d_ret, d_fq, d_fk, d_fv, d_ff_tile, jnp.zeros((l, WIN_N - 7 * GROUP - 128), MXU_DTYPE)], axis=1)
    gw_in = _mm_tn(d_proj, a, WIRE_DTYPE, "mm_gw_in", tm_cap=1536, after=rs_out[4])
    rs_in = _reduce_scatter_start(_extract_w_in_windows(gw_in), core, "rs_w_in")
    d_a = _mm_nn(d_proj, w_in_full, F32, "mm_d_a", tm_cap=544, tn_cap=256, tk_cap=WIN_N, after=rs_in[4])
    d_front, d_tokens, dg_norm1 = _rmsnorm_bwd(d_h1, d_a, h0, norm1_gain + rs_in[4][0, 0], "rmsnorm1_bwd", False)
    grad_x = d_tokens[None]
    d_meta = d_front[PAD_ROWS:CHUNK]

    d_conv_w = jnp.concatenate([d_conv[0, 0:3], d_conv[1, 0:3]], axis=1)
    d_conv_b = jnp.concatenate([d_conv[0, 3:4], d_conv[1, 3:4]], axis=1)
    pieces = [loss_part[:, 0:1], dg_norm1, db_forget_row[:, 0:N_HEADS], dg_ret, dg_norm2, d_conv_b, dg_final,
              d_meta.reshape(1, -1), d_conv_w.reshape(1, -1)]
    sizes = [p.shape[1] for p in pieces]
    flat = jnp.concatenate(pieces, axis=1)
    padded = -(-flat.shape[1] // 1024) * 1024
    flat = jnp.pad(flat, ((0, 0), (0, padded - flat.shape[1]))).reshape(padded // 128, 128)
    small_ar = _small_all_reduce_start(flat, d_tokens, "all_reduce_small")

    lead = lambda outs: tuple(o[None] for o in outs)
    fin_down = lead(_reduce_scatter_finish(rs_down, small_ar[4], chip, w_down[0], m_w_down[0], v_w_down[0], "rs_w_down"))
    fin_up = lead(_reduce_scatter_finish(rs_up, fin_down[3], chip, w_up[0], m_w_up[0], v_w_up[0], "rs_w_up"))
    fin_out = lead(_reduce_scatter_finish(rs_out, fin_up[3], chip, w_out[0], m_w_out[0], v_w_out[0], "rs_w_out"))
    fin_in = tuple(from_rows(o) for o in _reduce_scatter_finish(
        rs_in, fin_out[3], chip, w_in_rows, to_rows(m_w_in), to_rows(v_w_in), "rs_w_in"))
    g_w_down, g_w_up, g_w_out, g_w_in = fin_down[0], fin_up[0], fin_out[0], fin_in[0]
    early = [fin_down[1:], fin_up[1:], fin_out[1:], fin_in[1:]]
    total = _small_all_reduce_finish(small_ar, fin_in[3], dev1, "all_reduce_small").reshape(1, padded)
    offs = np.concatenate([[0], np.cumsum(sizes)])
    take = lambda k: total[:, int(offs[k]):int(offs[k + 1])]
    loss = take(0).reshape(())
    g_norm1, g_bf, g_ret_gain, g_norm2 = take(1), take(2), take(3), take(4)
    g_conv_b, g_final = take(5), take(6).reshape(d)
    g_meta = lax.dynamic_slice(take(7).reshape(N_META, d), (jnp.int32(0), (dev * (d // N_DEV)).astype(jnp.int32)),
                               (N_META, d // N_DEV))
    g_conv_w = lax.dynamic_slice(take(8).reshape(3, 2 * d_ff), (jnp.int32(0), (dev * up_shard).astype(jnp.int32)),
                                 (3, up_shard))[None]

    weights = [meta_tokens, norm1_gain, w_in, b_forget, ret_norm_gain, w_out, norm2_gain, w_up, conv_w, conv_b,
               w_down, final_norm_gain]
    grads = [g_meta, g_norm1, g_w_in, g_bf, g_ret_gain, g_w_out, g_norm2, g_w_up, g_conv_w, g_conv_b, g_w_down,
             g_final]
    done = {"w_down": early[0], "w_up": early[1], "w_out": early[2], "w_in": early[3]}
    ms = [m_meta_tokens, m_norm1_gain, m_w_in, m_b_forget, m_ret_norm_gain, m_w_out, m_norm2_gain, m_w_up, m_conv_w,
          m_conv_b, m_w_down, m_final_norm_gain]
    vs = [v_meta_tokens, v_norm1_gain, v_w_in, v_b_forget, v_ret_norm_gain, v_w_out, v_norm2_gain, v_w_up, v_conv_w,
          v_conv_b, v_w_down, v_final_norm_gain]
    names = ["meta", "norm1", "w_in", "b_forget", "ret_gain", "w_out", "norm2", "w_up", "conv_w", "conv_b", "w_down",
             "final_gain"]
    deltas, new_ms, new_vs = [], [], []
    for w, g, m, v, n in zip(weights, grads, ms, vs, names):
        dl, nm, nv = done[n] if n in done else _adamw(w, g, m, v, "adamw_" + n)
        deltas.append(dl)
        new_ms.append(nm)
        new_vs.append(nv)
    return (loss, grad_x, *grads, *deltas, *new_ms, *new_vs)
```

```python
import functools

import numpy as np
import jax
import jax.numpy as jnp
from jax import lax
from jax.experimental import pallas as pl
from jax.experimental.pallas import tpu as pltpu

F32 = jnp.float32
MXU_DTYPE = jnp.bfloat16
WIRE_DTYPE = jnp.bfloat16

N_DEV = 8
N_META = 16
CHUNK = 128
PAD_ROWS = CHUNK - N_META
N_HEADS = 8
HEAD_DIM = 128
GROUP = N_HEADS * HEAD_DIM
IN_DIM = 7 * GROUP + N_HEADS
WIN_SHARD = IN_DIM // N_DEV
WIN_ROWS = 912
WIN_BLOCK = 1024
WIN_STRIDE = 896
WIN_N = 7680
ROPE_BASE = 10000.0
NORM_EPS = 1e-6
NEG_BIG = -1e30
ADAM_LR, ADAM_B1, ADAM_B2, ADAM_EPS, ADAM_WD, ADAM_STEP = 0.001, 0.9, 0.999, 1e-08, 0.01, 10
VMEM_LIMIT = 52 * 1024 * 1024
MESH = pl.DeviceIdType.MESH
ANY = pl.BlockSpec(memory_space=pl.ANY)
VMEM_SPEC = pl.BlockSpec(memory_space=pltpu.VMEM)


def _params(sem=None):
    kw = {"vmem_limit_bytes": VMEM_LIMIT}
    if sem is not None:
        kw["dimension_semantics"] = sem
    return pltpu.CompilerParams(**kw)


def _divisor_tile(n, cap, unit):
    if n <= cap:
        return n
    best = None
    for t in range(unit, cap + 1, unit):
        if n % t == 0:
            best = t
    assert best is not None, (n, cap, unit)
    return best


def _my_position():
    return lax.axis_index("x"), lax.axis_index("y"), lax.axis_index("c")


def _device_index():
    x, y, c = _my_position()
    return 4 * x + 2 * y + c


def _all_gather(shard, name):
    r, c = shard.shape

    def body(x_ref, out_ref, send_sems, recv_sems, local_sem):
        mx, my, mc = _my_position()
        me, sibling = (mx, my, mc), (mx, my, 1 - mc)
        chips = [(1 - mx, my), (mx, 1 - my), (1 - mx, 1 - my)]

        def slot(px, py, pc):
            return out_ref.at[4 * px + 2 * py + pc]

        def copy(k, block, to, src=None):
            return pltpu.make_async_remote_copy(
                src_ref=slot(*block) if src is None else src, dst_ref=slot(*block),
                send_sem=send_sems.at[k], recv_sem=recv_sems.at[k], device_id=to, device_id_type=MESH)

        mine = pltpu.make_async_copy(x_ref, slot(*me), local_sem)
        mine.start()
        first = [copy(0, me, sibling, src=x_ref)]
        first += [copy(1 + j, me, (*chip, mc), src=x_ref) for j, chip in enumerate(chips)]
        for cp in first:
            cp.start()
        passed = [copy(4 + j, (*chip, mc), sibling) for j, chip in enumerate(chips)]
        for j, chip in enumerate(chips):
            copy(1 + j, (*chip, mc), me).wait_recv()
            passed[j].start()
        copy(0, sibling, me).wait_recv()
        for j, chip in enumerate(chips):
            copy(4 + j, (*chip, 1 - mc), me).wait_recv()
        for cp in first + passed:
            cp.wait_send()
        mine.wait()

    return pl.pallas_call(
        body, name=name,
        out_shape=jax.ShapeDtypeStruct((N_DEV, r, c), shard.dtype),
        in_specs=[ANY], out_specs=ANY,
        scratch_shapes=[pltpu.SemaphoreType.DMA((7,)), pltpu.SemaphoreType.DMA((7,)), pltpu.SemaphoreType.DMA],
    )(shard)


HBM_SPEC = pl.BlockSpec(memory_space=pltpu.HBM)
SEM_SPEC = pl.BlockSpec(memory_space=pltpu.SEMAPHORE)
DATAFLOW_EFFECT = pltpu.SideEffectType.DATAFLOW_SIDE_EFFECTING


def _in_hbm(a):
    return pltpu.with_memory_space_constraint(a, pltpu.HBM)


def _split_start(src, land, make_copies, n_copies, after, name):
    if isinstance(land, tuple):
        land = lax.empty(land, src.dtype)
    land_shape = land.shape
    def body(src_ref, land_ref, after_ref, send_sems, recv_sems, src_thru, land_thru, token):
        for cp in make_copies(src_ref, land_ref, send_sems, recv_sems):
            cp.start()
        token[...] = jnp.zeros_like(token)

    return pl.pallas_call(
        body, name=name,
        out_shape=(pltpu.SemaphoreType.DMA((n_copies,)), pltpu.SemaphoreType.DMA((n_copies,)),
                   pltpu.HBM(src.shape, src.dtype), pltpu.HBM(land_shape, land.dtype),
                   jax.ShapeDtypeStruct((8, 128), F32)),
        in_specs=(HBM_SPEC, HBM_SPEC, ANY), out_specs=(SEM_SPEC, SEM_SPEC, HBM_SPEC, HBM_SPEC, VMEM_SPEC),
        input_output_aliases={0: 2, 1: 3},
        compiler_params=pltpu.CompilerParams(has_side_effects=DATAFLOW_EFFECT),
    )(_in_hbm(src), _in_hbm(land), after)


def _split_wait(started, after, make_copies, name):
    send_sems, recv_sems, src_thru, land_thru, _ = started

    def body(src_ref, land_ref, send_sems_ref, recv_sems_ref, after_ref, src_dead, land_out):
        for cp in make_copies(src_ref, land_ref, send_sems_ref, recv_sems_ref):
            cp.wait_send()
            cp.wait_recv()

    return pl.pallas_call(
        body, name=name,
        out_shape=(pltpu.HBM(src_thru.shape, src_thru.dtype), pltpu.HBM(land_thru.shape, land_thru.dtype)),
        in_specs=(HBM_SPEC, HBM_SPEC, SEM_SPEC, SEM_SPEC, ANY), out_specs=(HBM_SPEC, HBM_SPEC),
        input_output_aliases={0: 0, 1: 1},
        compiler_params=pltpu.CompilerParams(has_side_effects=DATAFLOW_EFFECT),
    )(src_thru, land_thru, send_sems, recv_sems, after)


def _gather_copies(x_ref, land_ref, send_sems, recv_sems):
    mx, my, mc = _my_position()
    me = 4 * mx + 2 * my + mc
    targets = [(mx, my, 1 - mc), (1 - mx, my, mc), (mx, 1 - my, mc), (1 - mx, 1 - my, mc)]
    return [pltpu.make_async_remote_copy(
        src_ref=land_ref.at[me], dst_ref=land_ref.at[me], send_sem=send_sems.at[k], recv_sem=recv_sems.at[k],
        device_id=t, device_id_type=MESH) for k, t in enumerate(targets)]


def _own_slot(shard, dev, name):
    r, c = shard.shape
    tr = _divisor_tile(r, 640, 16)

    def body(s_ref, x_ref, o_ref):
        o_ref[...] = x_ref[...].astype(o_ref.dtype)

    return pl.pallas_call(
        body, name=name,
        out_shape=jax.ShapeDtypeStruct((N_DEV, r, c), WIRE_DTYPE),
        grid_spec=pltpu.PrefetchScalarGridSpec(
            num_scalar_prefetch=1, grid=(r // tr,),
            in_specs=[pl.BlockSpec((tr, c), lambda i, s: (i, 0))],
            out_specs=pl.BlockSpec((None, tr, c), lambda i, s: (s[0], i, 0))),
        compiler_params=_params(("parallel",)),
    )(dev, shard)


def _gather_start(shard, dev, after, name):
    return _split_start(jnp.zeros((8, 128), F32), _own_slot(shard, dev, name + "_own"), _gather_copies, 4, after, name)


def _gather_ring(shard, dev, after, name):
    r, c = shard.shape
    half = r // 2
    assert half % 16 == 0

    def body(x_ref, after_ref, land_in, land_ref, send_sems, recv_sems):
        mx, my, mc = _my_position()
        sibling, x_nbr, y_nbr = (mx, my, 1 - mc), (1 - mx, my, mc), (mx, 1 - my, mc)
        first, second = pl.ds(0, half), pl.ds(half, half)

        def slot(px, py, pc):
            return land_ref.at[4 * px + 2 * py + pc]

        def copy(k, src, dst, to):
            return pltpu.make_async_remote_copy(src_ref=src, dst_ref=dst, send_sem=send_sems.at[k],
                                                recv_sem=recv_sems.at[k], device_id=to, device_id_type=MESH)

        def arrived(k, dst):
            copy(k, dst, dst, sibling).wait_recv()

        mine = slot(mx, my, mc)
        from_x, from_y, from_d = slot(1 - mx, my, mc), slot(mx, 1 - my, mc), slot(1 - mx, 1 - my, mc)
        sent = [copy(0, x_ref, mine, sibling), copy(1, x_ref, mine, x_nbr), copy(2, x_ref, mine, y_nbr)]
        for cp in sent:
            cp.start()

        def send(k, src, to):
            cp = copy(k, src, src, to)
            cp.start()
            sent.append(cp)

        arrived(1, from_x)
        send(3, from_x.at[first], y_nbr)
        send(5, from_x, sibling)
        arrived(2, from_y)
        send(4, from_y.at[second], x_nbr)
        send(6, from_y, sibling)
        arrived(3, from_d.at[first])
        send(7, from_d.at[first], sibling)
        arrived(4, from_d.at[second])
        send(8, from_d.at[second], sibling)
        arrived(0, slot(mx, my, 1 - mc))
        arrived(5, slot(1 - mx, my, 1 - mc))
        arrived(6, slot(mx, 1 - my, 1 - mc))
        arrived(7, slot(1 - mx, 1 - my, 1 - mc).at[first])
        arrived(8, slot(1 - mx, 1 - my, 1 - mc).at[second])
        for cp in sent:
            cp.wait_send()

    land = _own_slot(shard, dev, name + "_own")
    return pl.pallas_call(
        body, name=name,
        out_shape=jax.ShapeDtypeStruct(land.shape, land.dtype),
        in_specs=[ANY, ANY, ANY], out_specs=ANY,
        input_output_aliases={2: 0},
        scratch_shapes=[pltpu.SemaphoreType.DMA((9,)), pltpu.SemaphoreType.DMA((9,))],
    )(shard, after, land)


def _pass_copies(unused_ref, land_ref, send_sems, recv_sems):
    mx, my, mc = _my_position()
    chips = [(1 - mx, my), (mx, 1 - my), (1 - mx, 1 - my)]
    return [pltpu.make_async_remote_copy(
        src_ref=land_ref.at[4 * cx + 2 * cy + mc], dst_ref=land_ref.at[4 * cx + 2 * cy + mc],
        send_sem=send_sems.at[j], recv_sem=recv_sems.at[j],
        device_id=(mx, my, 1 - mc), device_id_type=MESH) for j, (cx, cy) in enumerate(chips)]


def _gather_pass_start(started, after, name):
    _, land = _split_wait(started, after, _gather_copies, name + "_wait")
    return _split_start(jnp.zeros((8, 128), F32), land, _pass_copies, 3, after, name + "_pass_start")


def _gather_pass_finish(pass_started, after, name):
    return _split_wait(pass_started, after, _pass_copies, name + "_pass_wait")[1]


def _gather_finish(started, after, name):
    _, land = _split_wait(started, after, _gather_copies, name + "_wait")

    def body(land_in, land_ref, send_sems, recv_sems):
        mx, my, mc = _my_position()
        chips = [(1 - mx, my), (mx, 1 - my), (1 - mx, 1 - my)]
        copies = [pltpu.make_async_remote_copy(
            src_ref=land_ref.at[4 * cx + 2 * cy + mc], dst_ref=land_ref.at[4 * cx + 2 * cy + mc],
            send_sem=send_sems.at[j], recv_sem=recv_sems.at[j],
            device_id=(mx, my, 1 - mc), device_id_type=MESH) for j, (cx, cy) in enumerate(chips)]
        for cp in copies:
            cp.start()
        for j, (cx, cy) in enumerate(chips):
            copies[j].wait_send()
            pltpu.make_async_remote_copy(
                src_ref=land_ref.at[4 * cx + 2 * cy + 1 - mc], dst_ref=land_ref.at[4 * cx + 2 * cy + 1 - mc],
                send_sem=send_sems.at[j], recv_sem=recv_sems.at[j],
                device_id=(mx, my, 1 - mc), device_id_type=MESH).wait_recv()

    return pl.pallas_call(
        body, name=name + "_pass",
        out_shape=jax.ShapeDtypeStruct(land.shape, land.dtype),
        in_specs=[ANY], out_specs=ANY,
        input_output_aliases={0: 0},
        scratch_shapes=[pltpu.SemaphoreType.DMA((3,)), pltpu.SemaphoreType.DMA((3,))],
    )(land)


def _chip_copies(p_ref, land_ref, send_sems, recv_sems):
    mx, my, mc = _my_position()
    chips = [(1 - mx, my), (mx, 1 - my), (1 - mx, 1 - my)]
    return [pltpu.make_async_remote_copy(
        src_ref=p_ref.at[2 * cx + cy], dst_ref=land_ref.at[j], send_sem=send_sems.at[j], recv_sem=recv_sems.at[j],
        device_id=(cx, cy, mc), device_id_type=MESH) for j, (cx, cy) in enumerate(chips)]


def _reduce_scatter_start(g, core, name):
    pair = _pair_sum(g, _exchange_sibling(g, name + "_d2d"), core, name + "_pairsum")
    return _split_start(pair, (3,) + pair.shape[1:], _chip_copies, 3, g, name + "_ici_start")


def _sibling_copies(g_ref, land_ref, send_sems, recv_sems):
    mx, my, mc = _my_position()
    return [pltpu.make_async_remote_copy(
        src_ref=g_ref.at[2 * k + (1 - mc)], dst_ref=land_ref.at[k], send_sem=send_sems.at[k], recv_sem=recv_sems.at[k],
        device_id=(mx, my, 1 - mc), device_id_type=MESH) for k in range(4)]


def _reduce_scatter_d2d_start(g, after, name):
    return _split_start(g, (4,) + g.shape[1:], _sibling_copies, 4, after, name + "_d2d_start")


def _reduce_scatter_ici_start(d2d_started, after, core, name):
    g, from_sibling = _split_wait(d2d_started, after, _sibling_copies, name + "_d2d_wait")
    pair = _pair_sum(g, from_sibling, core, name + "_pairsum")
    return _split_start(pair, (3,) + pair.shape[1:], _chip_copies, 3, g, name + "_ici_start")


def _reduce_scatter_finish(started, after, chip, w, m, v, name):
    pair, from_chips = _split_wait(started, after, _chip_copies, name + "_ici_wait")
    return _final_sum_adamw(pair, from_chips, chip, w, m, v, name + "_sum_adamw")


def _exchange_sibling(g, name):
    _, r, c = g.shape

    def body(g_ref, out_ref, send_sems, recv_sems):
        mx, my, mc = _my_position()
        copies = [
            pltpu.make_async_remote_copy(
                src_ref=g_ref.at[2 * k + (1 - mc)], dst_ref=out_ref.at[k],
                send_sem=send_sems.at[k], recv_sem=recv_sems.at[k],
                device_id=(mx, my, 1 - mc), device_id_type=MESH)
            for k in range(4)]
        for cp in copies:
            cp.start()
        for cp in copies:
            cp.wait()

    return pl.pallas_call(
        body, name=name,
        out_shape=jax.ShapeDtypeStruct((4, r, c), g.dtype),
        in_specs=[ANY], out_specs=ANY,
        scratch_shapes=[pltpu.SemaphoreType.DMA((4,)), pltpu.SemaphoreType.DMA((4,))],
    )(g)


def _pair_sum(g, recv, core, name):
    _, r, c = g.shape
    tr = _divisor_tile(r, 512, 16)

    def body(s_ref, g_ref, r_ref, o_ref):
        o_ref[...] = (g_ref[...].astype(F32) + r_ref[...].astype(F32)).astype(o_ref.dtype)

    return pl.pallas_call(
        body, name=name,
        out_shape=jax.ShapeDtypeStruct((4, r, c), g.dtype),
        grid_spec=pltpu.PrefetchScalarGridSpec(
            num_scalar_prefetch=1, grid=(4, r // tr),
            in_specs=[pl.BlockSpec((None, tr, c), lambda k, i, s: (2 * k + s[0], i, 0)),
                      pl.BlockSpec((None, tr, c), lambda k, i, s: (k, i, 0))],
            out_specs=pl.BlockSpec((None, tr, c), lambda k, i, s: (k, i, 0))),
        compiler_params=_params(("parallel", "parallel")),
    )(core, g, recv)


def _adamw_math(w, g, m, v):
    nm = ADAM_B1 * m + (1.0 - ADAM_B1) * g
    nv = ADAM_B2 * v + (1.0 - ADAM_B2) * (g * g)
    m_hat = nm / (1.0 - ADAM_B1 ** ADAM_STEP)
    v_hat = nv / (1.0 - ADAM_B2 ** ADAM_STEP)
    return -ADAM_LR * (m_hat / (jnp.sqrt(v_hat) + ADAM_EPS) + ADAM_WD * w), nm, nv


def _final_sum_adamw(p, recv, chip, w, m, v, name):
    _, r, c = p.shape
    tr = _divisor_tile(r, 256, 16)
    tile = lambda: pl.BlockSpec((tr, c), lambda i, s: (i, 0))

    def body(s_ref, p_ref, r_ref, w_ref, m_ref, v_ref, g_ref, d_ref, nm_ref, nv_ref):
        g = p_ref[...].astype(F32)
        for j in range(3):
            g = g + r_ref[j].astype(F32)
        g_ref[...] = g
        d_ref[...], nm_ref[...], nv_ref[...] = _adamw_math(w_ref[...], g, m_ref[...], v_ref[...])

    sds = jax.ShapeDtypeStruct((r, c), F32)
    return pl.pallas_call(
        body, name=name,
        out_shape=(sds, sds, sds, sds),
        grid_spec=pltpu.PrefetchScalarGridSpec(
            num_scalar_prefetch=1, grid=(r // tr,),
            in_specs=[pl.BlockSpec((None, tr, c), lambda i, s: (s[0], i, 0)),
                      pl.BlockSpec((3, tr, c), lambda i, s: (0, i, 0)), tile(), tile(), tile()],
            out_specs=(tile(), tile(), tile(), tile())),
        compiler_params=_params(("parallel",)),
    )(chip, p, recv, w, m, v)


def _all_to_all_copies(v_ref, land_ref, send_sems, recv_sems):
    mx, my, mc = _my_position()
    me = 4 * mx + 2 * my + mc
    copies = []
    for rel in range(1, N_DEV):
        bx, by, bc = (rel >> 2) & 1, (rel >> 1) & 1, rel & 1
        target = (1 - mx if bx else mx, 1 - my if by else my, 1 - mc if bc else mc)
        copies.append(pltpu.make_async_remote_copy(
            src_ref=v_ref, dst_ref=land_ref.at[me], send_sem=send_sems.at[rel - 1], recv_sem=recv_sems.at[rel - 1],
            device_id=target, device_id_type=MESH))
    return copies


def _small_all_reduce_start(v, after, name):
    return _split_start(v, (N_DEV,) + v.shape, _all_to_all_copies, N_DEV - 1, after, name + "_start")


def _small_all_reduce_finish(started, after, dev, name):
    v, land = _split_wait(started, after, _all_to_all_copies, name + "_wait")
    rows = v.shape[0]

    def body(me_ref, v_ref, land_ref, o_ref):
        for j in range(N_DEV):
            @pl.when(me_ref[0] == j)
            def _():
                o_ref[...] = v_ref[...] if j == 0 else o_ref[...] + v_ref[...]

            @pl.when(me_ref[0] != j)
            def _():
                o_ref[...] = land_ref[j] if j == 0 else o_ref[...] + land_ref[j]

    return pl.pallas_call(
        body, name=name + "_sum",
        out_shape=jax.ShapeDtypeStruct((rows, 128), F32),
        grid_spec=pltpu.PrefetchScalarGridSpec(
            num_scalar_prefetch=1, grid=(1,),
            in_specs=[pl.BlockSpec((rows, 128), lambda i, s: (0, 0)),
                      pl.BlockSpec((N_DEV, rows, 128), lambda i, s: (0, 0, 0))],
            out_specs=pl.BlockSpec((rows, 128), lambda i, s: (0, 0))),
        compiler_params=_params(("arbitrary",)),
    )(dev, v, land)


def _assemble_w_in(blocks):
    rows, d = WIN_ROWS, blocks.shape[2]
    tc = _divisor_tile(d, 256, 128)
    n_tiles = WIN_N // 128
    last = (N_DEV * WIN_STRIDE) // 128

    def body(b_ref, o_ref):
        win = []
        for i in range(N_DEV):
            w = jnp.concatenate([b_ref[i].astype(F32), jnp.zeros((WIN_BLOCK - rows, tc), F32)], axis=0)
            win.append(pltpu.roll(w, i, 0) if i else w)
        for t in range(n_tiles):
            if t > last:
                o_ref[t * 128:(t + 1) * 128, :] = jnp.zeros((128, tc), o_ref.dtype)
                continue
            i = min(t // 7, N_DEV - 1)
            k = t - 7 * i
            val = win[i][k * 128:(k + 1) * 128, :]
            if k == 0 and i >= 1:
                val = val + win[i - 1][7 * 128:8 * 128, :]
            o_ref[t * 128:(t + 1) * 128, :] = val.astype(o_ref.dtype)

    return pl.pallas_call(
        body, name="assemble_w_in",
        out_shape=jax.ShapeDtypeStruct((WIN_N, d), blocks.dtype),
        grid=(d // tc,),
        in_specs=[pl.BlockSpec((N_DEV, rows, tc), lambda j: (0, 0, j))],
        out_specs=pl.BlockSpec((WIN_N, tc), lambda j: (0, j)),
        compiler_params=_params(("parallel",)),
    )(blocks)


def _extract_w_in_windows(g):
    _, d = g.shape
    tc = _divisor_tile(d, 256, 128)

    def body(g_ref, o_ref):
        for j in range(N_DEV):
            w = g_ref[WIN_STRIDE * j:WIN_STRIDE * j + WIN_BLOCK, :].astype(F32)
            w = pltpu.roll(w, WIN_BLOCK - j, 0) if j else w
            o_ref[j] = w[0:WIN_ROWS, :].astype(o_ref.dtype)

    return pl.pallas_call(
        body, name="extract_w_in_windows",
        out_shape=jax.ShapeDtypeStruct((N_DEV, WIN_ROWS, d), g.dtype),
        grid=(d // tc,),
        in_specs=[pl.BlockSpec((WIN_N, tc), lambda j: (0, j))],
        out_specs=pl.BlockSpec((N_DEV, WIN_ROWS, tc), lambda j: (0, 0, j)),
        compiler_params=_params(("parallel",)),
    )(g)


def _mm(a, b, *, a_spec, b_spec, o_spec, out_shape, grid, contract, nk, name, after=None):
    dn = (((contract[0],), (contract[1],)), ((), ()))
    tm, tn = o_spec.block_shape[-2:]
    behind = [] if after is None else [after]

    def body(a_ref, b_ref, *rest):
        o_ref, *scratch = rest[len(behind):]
        part = lax.dot_general(a_ref[...], b_ref[...], dn, preferred_element_type=F32)
        if nk == 1:
            o_ref[...] = part.astype(o_ref.dtype)
            return
        acc = scratch[0]
        k = pl.program_id(2)

        @pl.when(k == 0)
        def _():
            acc[...] = part

        @pl.when(k > 0)
        def _():
            acc[...] += part

        @pl.when(k == nk - 1)
        def _():
            o_ref[...] = acc[...].astype(o_ref.dtype)

    return pl.pallas_call(
        body, name=name, out_shape=out_shape, grid=grid,
        in_specs=[a_spec, b_spec] + [ANY] * len(behind), out_specs=o_spec,
        scratch_shapes=[] if nk == 1 else [pltpu.VMEM((tm, tn), F32)],
        compiler_params=_params(("parallel", "parallel", "arbitrary")),
    )(a, b, *behind)


def _mm_nn(a, b, out_dtype, name, tm_cap=1088, tn_cap=512, tk_cap=2048, after=None):
    m, k = a.shape
    _, n = b.shape
    tm, tn, tk = _divisor_tile(m, tm_cap, 16), _divisor_tile(n, tn_cap, 128), _divisor_tile(k, tk_cap, 128)
    return _mm(a, b,
               a_spec=pl.BlockSpec((tm, tk), lambda i, j, kk: (i, kk)),
               b_spec=pl.BlockSpec((tk, tn), lambda i, j, kk: (kk, j)),
               o_spec=pl.BlockSpec((tm, tn), lambda i, j, kk: (i, j)),
               out_shape=jax.ShapeDtypeStruct((m, n), out_dtype),
               grid=(m // tm, n // tn, k // tk), contract=(1, 0), nk=k // tk, name=name, after=after)


def _mm_nt(a, b, out_dtype, name, tm_cap=1088, tn_cap=512, tk_cap=2048, after=None):
    m, k = a.shape
    n, _ = b.shape
    tm, tn, tk = _divisor_tile(m, tm_cap, 16), _divisor_tile(n, tn_cap, 128), _divisor_tile(k, tk_cap, 128)
    return _mm(a, b,
               a_spec=pl.BlockSpec((tm, tk), lambda i, j, kk: (i, kk)),
               b_spec=pl.BlockSpec((tn, tk), lambda i, j, kk: (j, kk)),
               o_spec=pl.BlockSpec((tm, tn), lambda i, j, kk: (i, j)),
               out_shape=jax.ShapeDtypeStruct((m, n), out_dtype),
               grid=(m // tm, n // tn, k // tk), contract=(1, 1), nk=k // tk, name=name, after=after)


def _mm_tn(a, b, out_dtype, name, tm_cap=1024, tn_cap=512, after=None):
    l, m = a.shape
    _, n = b.shape
    tm, tn = _divisor_tile(m, tm_cap, 128), _divisor_tile(n, tn_cap, 128)
    return _mm(a, b,
               a_spec=pl.BlockSpec((l, tm), lambda i, j, kk: (0, i)),
               b_spec=pl.BlockSpec((l, tn), lambda i, j, kk: (0, j)),
               o_spec=pl.BlockSpec((tm, tn), lambda i, j, kk: (i, j)),
               out_shape=jax.ShapeDtypeStruct((m, n), out_dtype),
               grid=(m // tm, n // tn, 1), contract=(0, 0), nk=1, name=name, after=after)


def _pair_split(shard):
    left = shard % ATTN_BLOCK
    assert left in (0, CHUNK) and shard > left
    return shard - left, left


def _mm_up(cn, w_up_blocks, after):
    l, d = cn.shape
    n, _, shard = w_up_blocks.shape
    main, left = _pair_split(shard)
    tm = _divisor_tile(l, 544, 16)

    def body(a_ref, b_ref, after_ref, o_ref):
        a = a_ref[...]
        for s in range(2):
            o_ref[:, s * shard:s * shard + main] = _dot(a, b_ref[s, :, 0:main])
        if left:
            tail = _dot(a, jnp.concatenate([b_ref[0, :, main:], b_ref[1, :, main:]], axis=1))
            o_ref[:, main:shard] = tail[:, 0:left]
            o_ref[:, shard + main:2 * shard] = tail[:, left:]

    return pl.pallas_call(
        body, name="mm_up", out_shape=jax.ShapeDtypeStruct((l, n * shard), F32), grid=(l // tm, n // 2),
        in_specs=[pl.BlockSpec((tm, d), lambda i, j: (i, 0)),
                  pl.BlockSpec((2, d, shard), lambda i, j: (j, 0, 0)), ANY],
        out_specs=pl.BlockSpec((tm, 2 * shard), lambda i, j: (i, j)),
        compiler_params=_params(("parallel", "parallel")),
    )(cn, w_up_blocks, after)


def _mm_gw_up(cn, d_u):
    l, d = cn.shape
    _, _, d_ff = d_u.shape
    shard = 2 * d_ff // N_DEV
    pairs_per_half = d_ff // (2 * shard)
    tm = _divisor_tile(d, 512, 128)

    def body(a_ref, b_ref, o_ref):
        res = _dot_tn(a_ref[...], b_ref[...])
        o_ref[0] = res[:, 0:shard].astype(o_ref.dtype)
        o_ref[1] = res[:, shard:].astype(o_ref.dtype)

    return pl.pallas_call(
        body, name="mm_gw_up", out_shape=jax.ShapeDtypeStruct((N_DEV, d, shard), WIRE_DTYPE),
        grid=(d // tm, N_DEV // 2),
        in_specs=[pl.BlockSpec((l, tm), lambda i, j: (0, i)),
                  pl.BlockSpec((None, l, 2 * shard), lambda i, j: (j // pairs_per_half, 0, j % pairs_per_half))],
        out_specs=pl.BlockSpec((2, tm, shard), lambda i, j: (j, i, 0)),
        compiler_params=_params(("parallel", "parallel")),
    )(cn, d_u)


def _mm_d_cn(d_u, w_up_blocks, after):
    _, l, d_ff = d_u.shape
    n, d, shard = w_up_blocks.shape
    per = d_ff // shard
    main, left = _pair_split(shard)
    tm, tn = _divisor_tile(l, 544, 16), _divisor_tile(d, 256, 128)

    def body(a_ref, b_ref, after_ref, o_ref):
        acc = None
        for k in range(0, n, 2):
            half, c0 = k // per, (k % per) * shard
            parts = [_dot_nt(a_ref[half, :, c0 + s * shard:c0 + s * shard + main], b_ref[k + s, :, 0:main])
                     for s in range(2)]
            if left:
                a_tail = jnp.concatenate([a_ref[half, :, c0 + s * shard + main:c0 + (s + 1) * shard] for s in range(2)],
                                         axis=1)
                b_tail = jnp.concatenate([b_ref[k + s, :, main:] for s in range(2)], axis=1)
                parts.append(_dot_nt(a_tail, b_tail))
            for part in parts:
                acc = part if acc is None else acc + part
        o_ref[...] = acc

    return pl.pallas_call(
        body, name="mm_d_cn", out_shape=jax.ShapeDtypeStruct((l, d), F32), grid=(l // tm, d // tn),
        in_specs=[pl.BlockSpec((2, tm, d_ff), lambda i, j: (0, i, 0)),
                  pl.BlockSpec((n, tn, shard), lambda i, j: (0, j, 0)), ANY],
        out_specs=pl.BlockSpec((tm, tn), lambda i, j: (i, j)),
        compiler_params=_params(("parallel", "parallel")),
    )(d_u, w_up_blocks, after)


def _row_tile(l):
    return _divisor_tile(l, 544, 8)


def _rms(x, gain):
    return (x * lax.rsqrt(jnp.mean(x * x, axis=-1, keepdims=True) + NORM_EPS) * gain).astype(MXU_DTYPE)


def _embed_rmsnorm(x, meta, gain):
    seq, d = x.shape
    l = CHUNK + seq
    row = pl.BlockSpec((CHUNK, d), lambda i: (i, 0))

    def body(x_ref, m_ref, g_ref, h_ref, n_ref):
        @pl.when(pl.program_id(0) == 0)
        def _():
            h_ref[...] = jnp.concatenate([jnp.zeros((PAD_ROWS, d), F32), m_ref[...]], axis=0)

        @pl.when(pl.program_id(0) > 0)
        def _():
            h_ref[...] = x_ref[...]

        n_ref[...] = _rms(h_ref[...], g_ref[...])

    return pl.pallas_call(
        body, name="embed_rmsnorm1",
        out_shape=(jax.ShapeDtypeStruct((l, d), F32), jax.ShapeDtypeStruct((l, d), MXU_DTYPE)),
        grid=(l // CHUNK,),
        in_specs=[pl.BlockSpec((CHUNK, d), lambda i: (jnp.maximum(i - 1, 0), 0)),
                  pl.BlockSpec((N_META, d), lambda i: (0, 0)), pl.BlockSpec((1, d), lambda i: (0, 0))],
        out_specs=(row, row),
        compiler_params=_params(("parallel",)),
    )(x, meta, gain)


def _out_proj_resid_rmsnorm(mix, w_out, h0, gain):
    l, d = h0.shape
    tm = _divisor_tile(l, 272, 16)
    row = pl.BlockSpec((tm, d), lambda i: (i, 0))

    def body(a_ref, b_ref, h_ref, g_ref, s_ref, n_ref):
        x = h_ref[...] + _dot(a_ref[...], b_ref[...])
        s_ref[...] = x
        n_ref[...] = _rms(x, g_ref[...])

    return pl.pallas_call(
        body, name="mm_out_resid_rmsnorm2",
        out_shape=(jax.ShapeDtypeStruct((l, d), F32), jax.ShapeDtypeStruct((l, d), MXU_DTYPE)),
        grid=(l // tm,),
        in_specs=[row, pl.BlockSpec((d, d), lambda i: (0, 0)), row, pl.BlockSpec((1, d), lambda i: (0, 0))],
        out_specs=(row, row),
        compiler_params=_params(("parallel",)),
    )(mix, w_out, h0, gain)


def _rmsnorm_bwd(d_res, d_normed, x, gain, name, with_mxu_copy):
    l, d = x.shape
    tr = _row_tile(l) if with_mxu_copy else CHUNK
    row = pl.BlockSpec((tr, d), lambda i: (i, 0))
    vec = pl.BlockSpec((1, d), lambda i: (0, 0))

    def body(dres_ref, dn_ref, x_ref, g_ref, dx_ref, other_ref, dg_ref):
        i = pl.program_id(0)
        xv = x_ref[...]
        r = lax.rsqrt(jnp.mean(xv * xv, axis=-1, keepdims=True) + NORM_EPS)
        xh = xv * r
        dn = dn_ref[...]
        dxh = dn * g_ref[...]
        dx = dres_ref[...] + r * (dxh - xh * jnp.mean(dxh * xh, axis=-1, keepdims=True))
        if with_mxu_copy:
            dx_ref[...] = dx
            other_ref[...] = dx.astype(MXU_DTYPE)
        else:
            @pl.when(i == 0)
            def _():
                dx_ref[...] = dx

            @pl.when(i > 0)
            def _():
                other_ref[...] = dx

        @pl.when(i == 0)
        def _():
            dg_ref[...] = jnp.zeros_like(dg_ref)

        dg_ref[...] += jnp.sum(dn * xh, axis=0, keepdims=True)

    if with_mxu_copy:
        outs = [jax.ShapeDtypeStruct((l, d), F32), jax.ShapeDtypeStruct((l, d), MXU_DTYPE)]
        specs = [row, row]
    else:
        outs = [jax.ShapeDtypeStruct((CHUNK, d), F32), jax.ShapeDtypeStruct((l - CHUNK, d), F32)]
        specs = [pl.BlockSpec((CHUNK, d), lambda i: (0, 0)), pl.BlockSpec((CHUNK, d), lambda i: (jnp.maximum(i - 1, 0), 0))]
    outs.append(jax.ShapeDtypeStruct((1, d), F32))
    specs.append(vec)
    return pl.pallas_call(body, name=name, out_shape=tuple(outs), grid=(l // tr,),
                          in_specs=[row, row, row, vec], out_specs=tuple(specs),
                          compiler_params=_params(("arbitrary",)))(d_res, d_normed, x, gain)


def _loss_head(h1, mlp_out, gain, target):
    l, d = h1.shape
    n_blocks = l // CHUNK
    row = pl.BlockSpec((CHUNK, d), lambda i: (i, 0))
    vec = pl.BlockSpec((1, d), lambda i: (0, 0))
    tgt = pl.BlockSpec((CHUNK, d), lambda i: (jnp.maximum(i - 1, 0), 0))

    def body(h_ref, m_ref, g_ref, t_ref, dh_ref, dhb_ref, dg_ref, loss_ref, sq_ref):
        i = pl.program_id(0)
        x = h_ref[...] + m_ref[...]
        r = lax.rsqrt(jnp.mean(x * x, axis=-1, keepdims=True) + NORM_EPS)
        xh = x * r
        g = g_ref[...]
        real = i >= 1
        err = jnp.where(real, xh * g - t_ref[...], 0.0)
        dy = err * (1.0 / d)
        dxh = dy * g
        dh = r * (dxh - xh * jnp.mean(dxh * xh, axis=-1, keepdims=True))
        dh_ref[...] = dh
        dhb_ref[...] = dh.astype(MXU_DTYPE)

        @pl.when(i == 0)
        def _():
            dg_ref[...] = jnp.zeros_like(dg_ref)
            sq_ref[...] = jnp.zeros_like(sq_ref)

        dg_ref[...] += jnp.sum(dy * xh, axis=0, keepdims=True)
        sq_ref[...] += jnp.sum(err * err, axis=0, keepdims=True)

        @pl.when(i == n_blocks - 1)
        def _():
            total = jnp.sum(sq_ref[...], axis=-1, keepdims=True) * (0.5 / d)
            loss_ref[...] = jnp.broadcast_to(total, (1, 128))

    return pl.pallas_call(
        body, name="loss_head",
        out_shape=(jax.ShapeDtypeStruct((l, d), F32), jax.ShapeDtypeStruct((l, d), MXU_DTYPE),
                   jax.ShapeDtypeStruct((1, d), F32), jax.ShapeDtypeStruct((1, 128), F32)),
        grid=(n_blocks,), in_specs=[row, row, vec, tgt],
        out_specs=(row, row, vec, pl.BlockSpec((1, 128), lambda i: (0, 0))),
        scratch_shapes=[pltpu.VMEM((1, d), F32)],
        compiler_params=_params(("arbitrary",)),
    )(h1, mlp_out, gain, target)


def _dot(a, b):
    return jnp.dot(a, b, preferred_element_type=F32)


def _dot_nt(a, b):
    return lax.dot_general(a, b, (((1,), (1,)), ((), ())), preferred_element_type=F32)


def _dot_tn(a, b):
    return lax.dot_general(a, b, (((0,), (0,)), ((), ())), preferred_element_type=F32)


def _rope(t, cos2, sin2):
    return t * cos2 + pltpu.roll(t, HEAD_DIM // 2, 1) * sin2


def _rope_bwd(dr, cos2, sin2):
    return dr * cos2 + pltpu.roll(dr * sin2, HEAD_DIM // 2, 1)


def _sigmoid(x):
    return 1.0 / (1.0 + jnp.exp(-x))


def _row_valid(block, rows):
    r = block * CHUNK + lax.broadcasted_iota(jnp.int32, (rows, 1), 0)
    return r >= PAD_ROWS


def _retention_consts(l):
    pos = jnp.arange(l, dtype=F32) - PAD_ROWS
    inv_freq = 1.0 / (ROPE_BASE ** (jnp.arange(0, HEAD_DIM, 2, dtype=F32) / HEAD_DIM))
    ang = pos[:, None] * inv_freq[None, :]
    cos, sin = jnp.cos(ang), jnp.sin(ang)
    cos2 = jnp.concatenate([cos, cos], axis=-1)
    sin2 = jnp.concatenate([-sin, sin], axis=-1)
    log_g = jnp.log1p(-jnp.exp2(-5.0 - jnp.arange(N_HEADS, dtype=F32)))
    idx = jnp.arange(CHUNK, dtype=F32)
    diff = idx[:, None] - idx[None, :]
    decay = jnp.where(diff >= 0, jnp.exp(jnp.maximum(diff, 0.0)[None] * log_g[:, None, None]), 0.0)
    xi = jnp.exp((idx + 1.0)[None, :] * log_g[:, None])
    zeta = jnp.exp((CHUNK - 1.0 - idx)[None, :] * log_g[:, None])
    g_chunk = jnp.exp(CHUNK * log_g)
    bcast = lambda v: jnp.broadcast_to(v[:, :, None], (N_HEADS, CHUNK, HEAD_DIM))
    g_rows = jnp.broadcast_to(g_chunk[:, None, None], (N_HEADS, 8, HEAD_DIM))
    return cos2, sin2, decay, bcast(xi), bcast(zeta), g_rows


def _retention_fwd(proj, ret_gain, consts):
    l = proj.shape[0]
    n_chunks = l // CHUNK
    cos2, sin2, decay, xi, zeta, g_rows = consts
    scale = HEAD_DIM ** -0.5

    def body(p_ref, cos_ref, sin_ref, dec_ref, xi_ref, zeta_ref, gr_ref, gain_ref,
             mix_ref, o_ref, st_ref, state):
        c = pl.program_id(0)

        @pl.when(c == 0)
        def _():
            state[...] = jnp.zeros_like(state)

        cos_v, sin_v = cos_ref[...], sin_ref[...]
        valid = _row_valid(c, CHUNK)
        for h in range(N_HEADS):
            cols = slice(h * HEAD_DIM, (h + 1) * HEAD_DIM)
            q = p_ref[:, h * HEAD_DIM:(h + 1) * HEAD_DIM]
            k = p_ref[:, GROUP + h * HEAD_DIM:GROUP + (h + 1) * HEAD_DIM]
            v = p_ref[:, 2 * GROUP + h * HEAD_DIM:2 * GROUP + (h + 1) * HEAD_DIM]
            g = p_ref[:, 3 * GROUP + h * HEAD_DIM:3 * GROUP + (h + 1) * HEAD_DIM]
            rq = _rope(q, cos_v, sin_v).astype(MXU_DTYPE)
            rk = _rope(k, cos_v, sin_v) * scale
            rkb = rk.astype(MXU_DTYPE)
            vb = v.astype(MXU_DTYPE)
            st = state[h]
            st_ref[h] = st
            s = _dot_nt(rq, rkb) * dec_ref[h]
            o = _dot(s.astype(MXU_DTYPE), vb) + _dot(rq, st.astype(MXU_DTYPE)) * xi_ref[h]
            kz = (rk * zeta_ref[h]).astype(MXU_DTYPE)
            state[h] = gr_ref[h, 0:1, :] * st + _dot_tn(kz, vb)
            o_ref[:, cols] = o
            mu = jnp.mean(o, axis=-1, keepdims=True)
            oc = o - mu
            yn = oc * lax.rsqrt(jnp.mean(oc * oc, axis=-1, keepdims=True) + NORM_EPS)
            ret = (g * _sigmoid(g)) * (yn * gain_ref[:, cols])
            mix_ref[:, cols] = jnp.where(valid, ret, 0.0).astype(mix_ref.dtype)

    head_tab = pl.BlockSpec((N_HEADS, CHUNK, HEAD_DIM), lambda c: (0, 0, 0))
    return pl.pallas_call(
        body, name="retention_fwd",
        out_shape=(jax.ShapeDtypeStruct((l, 2 * GROUP), MXU_DTYPE), jax.ShapeDtypeStruct((l, GROUP), F32),
                   jax.ShapeDtypeStruct((n_chunks, N_HEADS, HEAD_DIM, HEAD_DIM), F32)),
        grid=(n_chunks,),
        in_specs=[pl.BlockSpec((CHUNK, 4 * GROUP), lambda c: (c, 0)),
                  pl.BlockSpec((CHUNK, HEAD_DIM), lambda c: (c, 0)),
                  pl.BlockSpec((CHUNK, HEAD_DIM), lambda c: (c, 0)),
                  head_tab, head_tab, head_tab,
                  pl.BlockSpec((N_HEADS, 8, HEAD_DIM), lambda c: (0, 0, 0)),
                  pl.BlockSpec((1, GROUP), lambda c: (0, 0))],
        out_specs=(pl.BlockSpec((CHUNK, GROUP), lambda c: (c, 0)),
                   pl.BlockSpec((CHUNK, GROUP), lambda c: (c, 0)),
                   pl.BlockSpec((None, N_HEADS, HEAD_DIM, HEAD_DIM), lambda c: (c, 0, 0, 0))),
        scratch_shapes=[pltpu.VMEM((N_HEADS, HEAD_DIM, HEAD_DIM), F32)],
        compiler_params=_params(("arbitrary",)),
    )(proj, cos2, sin2, decay, xi, zeta, g_rows, ret_gain)


def _retention_bwd(proj, o_pre, states, d_mix, ret_gain, consts):
    l = proj.shape[0]
    n_chunks = l // CHUNK
    cos2, sin2, decay, xi, zeta, g_rows = consts
    scale = HEAD_DIM ** -0.5
    rev = lambda c: n_chunks - 1 - c

    def body(p_ref, o_ref, st_ref, dm_ref, cos_ref, sin_ref, dec_ref, dect_ref, xi_ref, zeta_ref, gr_ref, gain_ref,
             dp_ref, dgain_ref, dstate):
        step = pl.program_id(0)

        @pl.when(step == 0)
        def _():
            dstate[...] = jnp.zeros_like(dstate)
            dgain_ref[...] = jnp.zeros_like(dgain_ref)

        cos_v, sin_v = cos_ref[...], sin_ref[...]
        valid = _row_valid(rev(step), CHUNK)
        for h in range(N_HEADS):
            cols = slice(h * HEAD_DIM, (h + 1) * HEAD_DIM)
            q = p_ref[:, h * HEAD_DIM:(h + 1) * HEAD_DIM]
            k = p_ref[:, GROUP + h * HEAD_DIM:GROUP + (h + 1) * HEAD_DIM]
            v = p_ref[:, 2 * GROUP + h * HEAD_DIM:2 * GROUP + (h + 1) * HEAD_DIM]
            g = p_ref[:, 3 * GROUP + h * HEAD_DIM:3 * GROUP + (h + 1) * HEAD_DIM]
            o = o_ref[:, cols]
            gain = gain_ref[:, cols]
            d_ret = jnp.where(valid, dm_ref[:, cols], 0.0)
            mu = jnp.mean(o, axis=-1, keepdims=True)
            oc = o - mu
            rstd = lax.rsqrt(jnp.mean(oc * oc, axis=-1, keepdims=True) + NORM_EPS)
            yn = oc * rstd
            sig = _sigmoid(g)
            gate = g * sig
            dgain_ref[:, cols] += jnp.sum(d_ret * gate * yn, axis=0, keepdims=True)
            d_g = d_ret * (yn * gain) * (sig * (1.0 + g * (1.0 - sig)))
            d_yn = d_ret * gate * gain
            d_o = rstd * (d_yn - jnp.mean(d_yn, axis=-1, keepdims=True)
                          - yn * jnp.mean(d_yn * yn, axis=-1, keepdims=True))
            rq = _rope(q, cos_v, sin_v)
            rk = _rope(k, cos_v, sin_v) * scale
            rqb, rkb, vb = rq.astype(MXU_DTYPE), rk.astype(MXU_DTYPE), v.astype(MXU_DTYPE)
            dob = d_o.astype(MXU_DTYPE)
            dec = dec_ref[h]
            xi_h, zeta_h = xi_ref[h], zeta_ref[h]
            st_b = st_ref[h].astype(MXU_DTYPE)
            dst = dstate[h]
            dst_b = dst.astype(MXU_DTYPE)
            dec_t = dect_ref[h]
            s_t_b = (_dot_nt(rkb, rqb) * dec_t).astype(MXU_DTYPE)
            da_b = (_dot_nt(dob, vb) * dec).astype(MXU_DTYPE)
            da_t_b = (_dot_nt(vb, dob) * dec_t).astype(MXU_DTYPE)
            doxi_b = (d_o * xi_h).astype(MXU_DTYPE)
            kz_b = (rk * zeta_h).astype(MXU_DTYPE)
            d_rq = _dot(da_b, rkb) + _dot_nt(doxi_b, st_b)
            d_rk = _dot(da_t_b, rqb) + _dot_nt(vb, dst_b) * zeta_h
            d_v = _dot(s_t_b, dob) + _dot(kz_b, dst_b)
            dstate[h] = gr_ref[h, 0:1, :] * dst + _dot_tn(rqb, doxi_b)
            d_q = _rope_bwd(d_rq, cos_v, sin_v)
            d_k = _rope_bwd(d_rk * scale, cos_v, sin_v)
            dp_ref[:, h * HEAD_DIM:(h + 1) * HEAD_DIM] = d_q.astype(dp_ref.dtype)
            dp_ref[:, GROUP + h * HEAD_DIM:GROUP + (h + 1) * HEAD_DIM] = d_k.astype(dp_ref.dtype)
            dp_ref[:, 2 * GROUP + h * HEAD_DIM:2 * GROUP + (h + 1) * HEAD_DIM] = d_v.astype(dp_ref.dtype)
            dp_ref[:, 3 * GROUP + h * HEAD_DIM:3 * GROUP + (h + 1) * HEAD_DIM] = d_g.astype(dp_ref.dtype)

    head_tab = pl.BlockSpec((N_HEADS, CHUNK, HEAD_DIM), lambda c: (0, 0, 0))
    return pl.pallas_call(
        body, name="retention_bwd",
        out_shape=(jax.ShapeDtypeStruct((l, 4 * GROUP), MXU_DTYPE), jax.ShapeDtypeStruct((1, GROUP), F32)),
        grid=(n_chunks,),
        in_specs=[pl.BlockSpec((CHUNK, 4 * GROUP), lambda c: (rev(c), 0)),
                  pl.BlockSpec((CHUNK, GROUP), lambda c: (rev(c), 0)),
                  pl.BlockSpec((None, N_HEADS, HEAD_DIM, HEAD_DIM), lambda c: (rev(c), 0, 0, 0)),
                  pl.BlockSpec((CHUNK, GROUP), lambda c: (rev(c), 0)),
                  pl.BlockSpec((CHUNK, HEAD_DIM), lambda c: (rev(c), 0)),
                  pl.BlockSpec((CHUNK, HEAD_DIM), lambda c: (rev(c), 0)),
                  head_tab, head_tab, head_tab, head_tab,
                  pl.BlockSpec((N_HEADS, 8, HEAD_DIM), lambda c: (0, 0, 0)),
                  pl.BlockSpec((1, GROUP), lambda c: (0, 0))],
        out_specs=(pl.BlockSpec((CHUNK, 4 * GROUP), lambda c: (rev(c), 0)),
                   pl.BlockSpec((1, GROUP), lambda c: (0, 0))),
        scratch_shapes=[pltpu.VMEM((N_HEADS, HEAD_DIM, HEAD_DIM), F32)],
        compiler_params=_params(("arbitrary",)),
    )(proj, o_pre, states, d_mix, cos2, sin2, decay, jnp.transpose(decay, (0, 2, 1)), xi, zeta, g_rows, ret_gain)


FF_TILE = (7 * GROUP) // 128


def _log_forget(ff, bias_row, valid):
    x = ff + bias_row
    e = jnp.exp(-jnp.abs(x))
    lf = jnp.minimum(x, 0.0) - jnp.log(1.0 + e)
    head_lane = lax.broadcasted_iota(jnp.int32, x.shape, 1) < N_HEADS
    keep = lambda t: jnp.where(head_lane, jnp.where(valid, t, 0.0), 0.0)
    return keep(lf), keep(jnp.where(x >= 0, e, 1.0) / (1.0 + e))


def _fox_prep(proj, bias_row):
    l = proj.shape[0]
    n_blocks = l // CHUNK

    def body(ff_ref, b_ref, bc_ref, rows_ref, cum):
        r = lax.broadcasted_iota(jnp.int32, (CHUNK, CHUNK), 0)
        cidx = lax.broadcasted_iota(jnp.int32, (CHUNK, CHUNK), 1)
        tri = jnp.where(r >= cidx, 1.0, 0.0).astype(F32)
        carry = jnp.zeros((1, 128), F32)
        for blk in range(n_blocks):
            rows = slice(blk * CHUNK, (blk + 1) * CHUNK)
            valid = _row_valid(blk, CHUNK)
            lf, _ = _log_forget(ff_ref[rows, :], b_ref[...], valid)
            local = jnp.dot(tri, lf, precision=lax.Precision.HIGHEST, preferred_element_type=F32) + carry
            carry = local[CHUNK - 1:CHUNK, :]
            masked = jnp.where(valid, local, -NEG_BIG)
            cum[rows, :] = masked
            t = masked.T
            for h in range(N_HEADS):
                rows_ref[h, :, rows] = t[h:h + 1, :]
        full = cum[...]
        for h in range(N_HEADS):
            bc_ref[h] = jnp.broadcast_to(full[:, h:h + 1], (l, 128))

    return pl.pallas_call(
        body, name="fox_prep",
        out_shape=(jax.ShapeDtypeStruct((N_HEADS, l, 128), F32), jax.ShapeDtypeStruct((N_HEADS, 1, l), F32)),
        grid=(1,),
        in_specs=[pl.BlockSpec((l, 128), lambda i: (0, FF_TILE)), pl.BlockSpec((1, 128), lambda i: (0, 0))],
        out_specs=(pl.BlockSpec((N_HEADS, l, 128), lambda i: (0, 0, 0)),
                   pl.BlockSpec((N_HEADS, 1, l), lambda i: (0, 0, 0))),
        scratch_shapes=[pltpu.VMEM((l, 128), F32)],
        compiler_params=_params(("arbitrary",)),
    )(proj, bias_row)


ATTN_BLOCK = 2 * CHUNK


def _attn_blocks(l):
    assert (l - CHUNK) % ATTN_BLOCK == 0
    return [(0, CHUNK)] + [(s, ATTN_BLOCK) for s in range(CHUNK, l, ATTN_BLOCK)]


def _rows_valid(start, size):
    return start + lax.broadcasted_iota(jnp.int32, (size, 1), 0) >= PAD_ROWS


def _fox_fwd(proj, cum_bc, cum_rows, mix):
    l = proj.shape[0]
    blocks = _attn_blocks(l)
    scale = HEAD_DIM ** -0.5
    qt, kt, vt = 4 * N_HEADS, 5 * N_HEADS, 6 * N_HEADS

    def body(q_ref, k_ref, v_ref, cbc_ref, crow_ref, mix_in, o_ref, lse_ref, qb_s, kb_s, vb_s):
        qb_s[...] = q_ref[...].astype(MXU_DTYPE)
        kb_s[...] = k_ref[...].astype(MXU_DTYPE)
        vb_s[...] = v_ref[...].astype(MXU_DTYPE)
        for p, (qs, qn) in enumerate(blocks):
            qb = qb_s[qs:qs + qn, :]
            cq = cbc_ref[qs:qs + qn, :]
            m = jnp.full((qn, 1), NEG_BIG, F32)
            lsum = jnp.zeros((qn, 1), F32)
            acc = jnp.zeros((qn, HEAD_DIM), F32)
            for j in range(p + 1):
                ks, kn = blocks[j]
                bias = jnp.tile(cq, (1, kn // CHUNK)) - crow_ref[:, ks:ks + kn]
                s = _dot_nt(qb, kb_s[ks:ks + kn, :]) * scale + bias
                if j == p:
                    q_pos = qs + lax.broadcasted_iota(jnp.int32, (qn, kn), 0)
                    k_pos = ks + lax.broadcasted_iota(jnp.int32, (qn, kn), 1)
                    s = jnp.where(k_pos <= q_pos, s, NEG_BIG)
                m_new = jnp.maximum(m, jnp.max(s, axis=-1, keepdims=True))
                alpha = jnp.exp(m - m_new)
                pr = jnp.exp(s - m_new)
                lsum = lsum * alpha + jnp.sum(pr, axis=-1, keepdims=True)
                acc = acc * alpha + _dot(pr.astype(MXU_DTYPE), vb_s[ks:ks + kn, :])
                m = m_new
            o = jnp.where(_rows_valid(qs, qn), acc * (1.0 / lsum), 0.0)
            o_ref[qs:qs + qn, :] = o.astype(o_ref.dtype)
            lse = m + jnp.log(lsum)
            lse_ref[:, qs:qs + qn] = jnp.broadcast_to(lse, (qn, CHUNK)).T[0:1, :]

    head_col = lambda t: pl.BlockSpec((l, HEAD_DIM), lambda h: (0, t + h))
    return pl.pallas_call(
        body, name="fox_fwd",
        out_shape=(jax.ShapeDtypeStruct(mix.shape, mix.dtype), jax.ShapeDtypeStruct((N_HEADS, 1, l), F32)),
        grid=(N_HEADS,),
        in_specs=[head_col(qt), head_col(kt), head_col(vt),
                  pl.BlockSpec((None, l, 128), lambda h: (h, 0, 0)),
                  pl.BlockSpec((None, 1, l), lambda h: (h, 0, 0)),
                  ANY],
        out_specs=(head_col(N_HEADS), pl.BlockSpec((None, 1, l), lambda h: (h, 0, 0))),
        input_output_aliases={5: 0},
        scratch_shapes=[pltpu.VMEM((l, HEAD_DIM), MXU_DTYPE)] * 3,
        compiler_params=_params(("parallel",)),
    )(proj, proj, proj, cum_bc, cum_rows, mix)


def _fox_bwd(proj, cum_bc, cum_rows, d_mix, lse_rows):
    l = proj.shape[0]
    blocks = _attn_blocks(l)
    scale = HEAD_DIM ** -0.5
    qt, kt, vt = 4 * N_HEADS, 5 * N_HEADS, 6 * N_HEADS

    def body(q_ref, k_ref, v_ref, do_ref, cbc_ref, crow_ref, lse_ref,
             dq_ref, dk_ref, dv_ref, ds_ref, dk_acc, dv_acc, qb_s, kb_s, vb_s, dob_s, p_s, dp_s):
        qb_s[...] = q_ref[...].astype(MXU_DTYPE)
        kb_s[...] = k_ref[...].astype(MXU_DTYPE)
        vb_s[...] = v_ref[...].astype(MXU_DTYPE)
        dob_s[...] = jnp.where(_rows_valid(0, l), do_ref[...], 0.0).astype(MXU_DTYPE)
        dk_acc[...] = jnp.zeros_like(dk_acc)
        dv_acc[...] = jnp.zeros_like(dv_acc)
        ds_ref[...] = jnp.zeros_like(ds_ref)
        shift_row = crow_ref[...] - lse_ref[...]

        for p, (qs, qn) in enumerate(blocks):
            qb, dob = qb_s[qs:qs + qn, :], dob_s[qs:qs + qn, :]
            shift = shift_row[:, qs:qs + qn]

            delta = jnp.zeros((1, qn), F32)
            for j in range(p + 1):
                ks, kn = blocks[j]
                ck = jnp.tile(cbc_ref[ks:ks + kn, :], (1, qn // CHUNK))
                s_t = _dot_nt(kb_s[ks:ks + kn, :], qb) * scale + (shift - ck)
                if j == p:
                    k_pos = ks + lax.broadcasted_iota(jnp.int32, (kn, qn), 0)
                    q_pos = qs + lax.broadcasted_iota(jnp.int32, (kn, qn), 1)
                    s_t = jnp.where(k_pos <= q_pos, s_t, NEG_BIG)
                p_t, dp_t = jnp.exp(s_t), _dot_nt(vb_s[ks:ks + kn, :], dob)
                p_s[j, 0:kn, 0:qn] = p_t
                dp_s[j, 0:kn, 0:qn] = dp_t
                delta = delta + jnp.sum(p_t * dp_t, axis=0, keepdims=True)
            dq = jnp.zeros((qn, HEAD_DIM), F32)
            for j in range(p + 1):
                ks, kn = blocks[j]
                rows = slice(ks, ks + kn)
                p_t, dp_t = p_s[j, 0:kn, 0:qn], dp_s[j, 0:kn, 0:qn]
                ds_t = p_t * (dp_t - delta)
                ds_b = ds_t.astype(MXU_DTYPE)
                dv_acc[rows, :] += _dot(p_t.astype(MXU_DTYPE), dob)
                dk_acc[rows, :] += _dot(ds_b, qb) * scale
                ds_ref[rows, :] += sum(ds_t[:, c:c + CHUNK] for c in range(0, qn, CHUNK))
                dq = dq + _dot_tn(ds_b, kb_s[rows, :])
            dq_ref[qs:qs + qn, :] = (dq * scale).astype(dq_ref.dtype)

        dk_ref[...] = dk_acc[...].astype(dk_ref.dtype)
        dv_ref[...] = dv_acc[...].astype(dv_ref.dtype)

    col = jax.ShapeDtypeStruct((l, GROUP), MXU_DTYPE)
    head_col = lambda t: pl.BlockSpec((l, HEAD_DIM), lambda h: (0, t + h))
    return pl.pallas_call(
        body, name="fox_bwd",
        out_shape=(col, col, col, jax.ShapeDtypeStruct((N_HEADS, l, 128), F32)),
        grid=(N_HEADS,),
        in_specs=[head_col(qt), head_col(kt), head_col(vt), head_col(N_HEADS),
                  pl.BlockSpec((None, l, 128), lambda h: (h, 0, 0)),
                  pl.BlockSpec((None, 1, l), lambda h: (h, 0, 0)),
                  pl.BlockSpec((None, 1, l), lambda h: (h, 0, 0))],
        out_specs=(head_col(0), head_col(0), head_col(0), pl.BlockSpec((None, l, 128), lambda h: (h, 0, 0))),
        scratch_shapes=([pltpu.VMEM((l, HEAD_DIM), F32)] * 2 + [pltpu.VMEM((l, HEAD_DIM), MXU_DTYPE)] * 4
                        + [pltpu.VMEM((len(blocks), ATTN_BLOCK, ATTN_BLOCK), F32)] * 2),
        compiler_params=_params(("parallel",)),
    )(proj, proj, proj, d_mix, cum_bc, cum_rows, lse_rows)


def _fox_gate_bwd(ds_sum, proj, bias_row):
    l = proj.shape[0]
    n_blocks = l // CHUNK

    def body(ds_ref, ff_ref, b_ref, dff_ref, db_ref):
        r = lax.broadcasted_iota(jnp.int32, (CHUNK, CHUNK), 0)
        cidx = lax.broadcasted_iota(jnp.int32, (CHUNK, CHUNK), 1)
        upper = jnp.where(cidx >= r, 1.0, 0.0).astype(F32)
        carry = jnp.zeros((1, 128), F32)
        db = jnp.zeros((1, 128), F32)
        for blk in reversed(range(n_blocks)):
            rows = slice(blk * CHUNK, (blk + 1) * CHUNK)
            key_sum = jnp.zeros((CHUNK, 128), F32)
            for h in range(N_HEADS):
                select = jnp.where(cidx == h, 1.0, 0.0).astype(F32)
                key_sum = key_sum + jnp.dot(ds_ref[h, rows, :], select, precision=lax.Precision.HIGHEST,
                                            preferred_element_type=F32)
            suffix = jnp.dot(upper, key_sum, precision=lax.Precision.HIGHEST, preferred_element_type=F32) + carry
            carry = suffix[0:1, :]
            _, dsig = _log_forget(ff_ref[rows, :], b_ref[...], _row_valid(blk, CHUNK))
            dff = -suffix * dsig
            dff_ref[rows, :] = dff.astype(dff_ref.dtype)
            db = db + jnp.sum(dff, axis=0, keepdims=True)
        db_ref[...] = db

    return pl.pallas_call(
        body, name="fox_gate_bwd",
        out_shape=(jax.ShapeDtypeStruct((l, 128), MXU_DTYPE), jax.ShapeDtypeStruct((1, 128), F32)),
        grid=(1,),
        in_specs=[pl.BlockSpec((N_HEADS, l, 128), lambda i: (0, 0, 0)),
                  pl.BlockSpec((l, 128), lambda i: (0, FF_TILE)),
                  pl.BlockSpec((1, 128), lambda i: (0, 0))],
        out_specs=(pl.BlockSpec((l, 128), lambda i: (0, 0)), pl.BlockSpec((1, 128), lambda i: (0, 0))),
        compiler_params=_params(("arbitrary",)),
    )(ds_sum, proj, bias_row)


def _conv(u, w, b):
    return b + w[0:1, :] * pltpu.roll(u, 2, 0) + w[1:2, :] * pltpu.roll(u, 1, 0) + w[2:3, :] * u


def _conv_act_fwd(u, conv_w, conv_b, d_ff):
    l = u.shape[0]
    tc = _divisor_tile(d_ff, 256, 128)
    nt = d_ff // tc

    def body(ug_ref, uv_ref, wg_ref, wv_ref, bg_ref, bv_ref, a_ref, y_ref):
        yg = _conv(ug_ref[...], wg_ref[...], bg_ref[...])
        yv = _conv(uv_ref[...], wv_ref[...], bv_ref[...])
        act = yg * _sigmoid(yg) * yv
        a_ref[...] = jnp.where(_row_valid(0, l), act, 0.0).astype(a_ref.dtype)
        y_ref[0] = yg.astype(y_ref.dtype)
        y_ref[1] = yv.astype(y_ref.dtype)

    return pl.pallas_call(
        body, name="conv_act_fwd",
        out_shape=(jax.ShapeDtypeStruct((l, d_ff), MXU_DTYPE), jax.ShapeDtypeStruct((2, l, d_ff), MXU_DTYPE)),
        grid=(nt,),
        in_specs=[pl.BlockSpec((l, tc), lambda j: (0, j)), pl.BlockSpec((l, tc), lambda j: (0, j + nt)),
                  pl.BlockSpec((8, tc), lambda j: (0, j)), pl.BlockSpec((8, tc), lambda j: (0, j + nt)),
                  pl.BlockSpec((1, tc), lambda j: (0, j)), pl.BlockSpec((1, tc), lambda j: (0, j + nt))],
        out_specs=(pl.BlockSpec((l, tc), lambda j: (0, j)), pl.BlockSpec((2, l, tc), lambda j: (0, 0, j))),
        compiler_params=_params(("parallel",)),
    )(u, u, conv_w, conv_w, conv_b, conv_b)


def _conv_act_bwd(u, y, conv_w, d_act, d_ff):
    l = u.shape[0]
    tc = _divisor_tile(d_ff, 256, 128)
    nt = d_ff // tc

    def body(ug_ref, uv_ref, y_ref, wg_ref, wv_ref, da_ref, du_ref, dwb_ref):
        valid = _row_valid(0, l)
        ug, uv = ug_ref[...], uv_ref[...]
        wg, wv = wg_ref[...], wv_ref[...]
        yg, yv = y_ref[0].astype(F32), y_ref[1].astype(F32)
        sig = _sigmoid(yg)
        da = jnp.where(valid, da_ref[...], 0.0)
        d_yv = da * (yg * sig)
        d_yg = da * yv * (sig * (1.0 + yg * (1.0 - sig)))
        for idx, (dy, uu, w) in enumerate(((d_yg, ug, wg), (d_yv, uv, wv))):
            du = w[2:3, :] * dy + w[1:2, :] * pltpu.roll(dy, l - 1, 0) + w[0:1, :] * pltpu.roll(dy, l - 2, 0)
            du_ref[idx] = jnp.where(valid, du, 0.0).astype(du_ref.dtype)
            dwb_ref[idx, 0:1, :] = jnp.sum(dy * pltpu.roll(uu, 2, 0), axis=0, keepdims=True)
            dwb_ref[idx, 1:2, :] = jnp.sum(dy * pltpu.roll(uu, 1, 0), axis=0, keepdims=True)
            dwb_ref[idx, 2:3, :] = jnp.sum(dy * uu, axis=0, keepdims=True)
            dwb_ref[idx, 3:4, :] = jnp.sum(dy, axis=0, keepdims=True)
            dwb_ref[idx, 4:8, :] = jnp.zeros((4, tc), F32)

    return pl.pallas_call(
        body, name="conv_act_bwd",
        out_shape=(jax.ShapeDtypeStruct((2, l, d_ff), MXU_DTYPE), jax.ShapeDtypeStruct((2, 8, d_ff), F32)),
        grid=(nt,),
        in_specs=[pl.BlockSpec((l, tc), lambda j: (0, j)), pl.BlockSpec((l, tc), lambda j: (0, j + nt)),
                  pl.BlockSpec((2, l, tc), lambda j: (0, 0, j)),
                  pl.BlockSpec((8, tc), lambda j: (0, j)), pl.BlockSpec((8, tc), lambda j: (0, j + nt)),
                  pl.BlockSpec((l, tc), lambda j: (0, j))],
        out_specs=(pl.BlockSpec((2, l, tc), lambda j: (0, 0, j)), pl.BlockSpec((2, 8, tc), lambda j: (0, 0, j))),
        compiler_params=_params(("parallel",)),
    )(u, u, y, conv_w, conv_w, d_act)


def _adamw(w, g, m, v, name):
    shape = w.shape
    if w.ndim == 1:
        as2d = (1, shape[0])
    else:
        as2d = (int(np.prod(shape[:-1])), shape[-1])
    r, c = as2d
    tr = _divisor_tile(r, 256, 8)
    spec = pl.BlockSpec((tr, c), lambda i: (i, 0))

    def body(w_ref, g_ref, m_ref, v_ref, d_ref, nm_ref, nv_ref):
        d_ref[...], nm_ref[...], nv_ref[...] = _adamw_math(w_ref[...], g_ref[...], m_ref[...], v_ref[...])

    sds = jax.ShapeDtypeStruct(as2d, F32)
    outs = pl.pallas_call(
        body, name=name, out_shape=(sds, sds, sds), grid=(r // tr,),
        in_specs=[spec] * 4, out_specs=(spec,) * 3,
        compiler_params=_params(("parallel",)),
    )(w.reshape(as2d), g.reshape(as2d), m.reshape(as2d), v.reshape(as2d))
    return tuple(o.reshape(shape) for o in outs)


def _pad_rows(a, rows):
    return jnp.pad(a, ((0, rows - a.shape[0]), (0, 0)))


def kernel(x, meta_tokens, norm1_gain, w_in, b_forget, ret_norm_gain, w_out, norm2_gain, w_up, conv_w, conv_b, w_down, final_norm_gain, loss_target, m_meta_tokens, m_norm1_gain, m_w_in, m_b_forget, m_ret_norm_gain, m_w_out, m_norm2_gain, m_w_up, m_conv_w, m_conv_b, m_w_down, m_final_norm_gain, v_meta_tokens, v_norm1_gain, v_w_in, v_b_forget, v_ret_norm_gain, v_w_out, v_norm2_gain, v_w_up, v_conv_w, v_conv_b, v_w_down, v_final_norm_gain):
    seq, d = x.shape[1], x.shape[2]
    l = CHUNK + seq
    d_ff = w_down.shape[1] * N_DEV
    up_shard = w_up.shape[2]
    assert 4 * up_shard == d_ff and w_in.shape[2] == WIN_SHARD and d == 2 * GROUP
    dev = _device_index()
    mx, my, mc = _my_position()
    core = jnp.reshape(mc, (1,)).astype(jnp.int32)
    chip = jnp.reshape(2 * mx + my, (1,)).astype(jnp.int32)
    dev1 = jnp.reshape(dev, (1,)).astype(jnp.int32)

    small = jnp.concatenate([meta_tokens.reshape(-1, 128), conv_w[0].reshape(-1, 128)], axis=0)
    n_meta_rows = N_META * (d // N_DEV) // 128
    small_rows = small.shape[0]
    small_all = _all_gather(_pad_rows(small, -(-small_rows // 8) * 8), "gather_small")
    meta_full = jnp.transpose(small_all[:, :n_meta_rows].reshape(N_DEV, N_META, d // N_DEV), (1, 0, 2)).reshape(N_META, d)
    conv_w_full = _pad_rows(jnp.transpose(small_all[:, n_meta_rows:small_rows].reshape(N_DEV, 3, up_shard),
                                          (1, 0, 2)).reshape(3, 2 * d_ff), 8)
    to_rows = lambda t: jnp.pad(jnp.transpose(t[0]), ((0, WIN_ROWS - WIN_SHARD), (0, 0)))
    from_rows = lambda t: jnp.transpose(t[:WIN_SHARD])[None]
    w_in_rows = to_rows(w_in)
    out_rows = d // N_DEV
    mixer_rows = -(-(WIN_ROWS + out_rows) // 32) * 32
    mixer_shard = jnp.concatenate([w_in_rows.astype(WIRE_DTYPE), w_out[0].astype(WIRE_DTYPE),
                                   jnp.zeros((mixer_rows - WIN_ROWS - out_rows, d), WIRE_DTYPE)], axis=0)

    consts = _retention_consts(l)
    bias_row = jnp.pad(b_forget, ((0, 0), (0, 128 - N_HEADS)))
    h0, a = _embed_rmsnorm(x[0], meta_full, norm1_gain)
    mixer_blocks = _gather_ring(mixer_shard, dev1, a, "gather_w_in")
    start_up = _gather_start(w_up[0], dev1, mixer_blocks, "gather_w_up_start")
    w_in_full = _assemble_w_in(mixer_blocks).astype(MXU_DTYPE)
    proj = _mm_nt(a, w_in_full, F32, "mm_proj", after=start_up[4])
    ret_mix, ret_pre, ret_states = _retention_fwd(proj, ret_norm_gain, consts)
    cum_bc, cum_rows = _fox_prep(proj, bias_row)
    mix, lse_rows = _fox_fwd(proj, cum_bc, cum_rows, ret_mix)
    w_out_full = mixer_blocks[:, WIN_ROWS:WIN_ROWS + out_rows].reshape(d, d).astype(MXU_DTYPE)
    h1, cn = _out_proj_resid_rmsnorm(mix, w_out_full, h0, norm2_gain)
    w_up_blocks = _gather_finish(start_up, cn, "gather_w_up").astype(MXU_DTYPE)
    start_down = _gather_start(w_down[0], dev1, w_up_blocks, "gather_w_down_start")
    u = _mm_up(cn, w_up_blocks, start_down[4])
    pass_down = _gather_pass_start(start_down, u, "gather_w_down")
    act, conv_y = _conv_act_fwd(u, conv_w_full, conv_b + pass_down[4][0, 0], d_ff)
    w_down_full = _gather_pass_finish(pass_down, act, "gather_w_down").reshape(d_ff, d).astype(MXU_DTYPE)
    mlp_out = _mm_nn(act, w_down_full, F32, "mm_down", tm_cap=544, tk_cap=d_ff)
    d_h2, d_h2_b, dg_final, loss_part = _loss_head(h1, mlp_out, final_norm_gain.reshape(1, d), loss_target[0])

    gw_down = _mm_tn(act, d_h2_b, WIRE_DTYPE, "mm_gw_down", tm_cap=1408, tn_cap=1024)
    d2d_down = _reduce_scatter_d2d_start(gw_down.reshape(N_DEV, d_ff // N_DEV, d), d_h2, "rs_w_down")
    d_act = _mm_nt(d_h2_b, w_down_full, F32, "mm_d_act", after=d2d_down[4])
    rs_down = _reduce_scatter_ici_start(d2d_down, d_act, core, "rs_w_down")
    d_u, d_conv = _conv_act_bwd(u, conv_y, conv_w_full + rs_down[4][0, 0], d_act, d_ff)
    tm = _divisor_tile(l, 1088, 16)
    gw_up = _mm_gw_up(cn, d_u)
    d2d_up = _reduce_scatter_d2d_start(gw_up, d_act, "rs_w_up")
    d_cn = _mm_d_cn(d_u, w_up_blocks, d2d_up[4])
    rs_up = _reduce_scatter_ici_start(d2d_up, d_cn, core, "rs_w_up")
    d_h1, d_h1_b, dg_norm2 = _rmsnorm_bwd(d_h2, d_cn, h1, norm2_gain + rs_up[4][0, 0], "rmsnorm2_bwd", True)

    gw_out = _mm_tn(mix, d_h1_b, WIRE_DTYPE, "mm_gw_out")
    d2d_out = _reduce_scatter_d2d_start(gw_out.reshape(N_DEV, d // N_DEV, d), d_cn, "rs_w_out")
    d_mix = _mm_nt(d_h1_b, w_out_full, F32, "mm_d_mix", after=d2d_out[4])
    d_fq, d_fk, d_fv, ds_sum = _fox_bwd(proj, cum_bc, cum_rows, d_mix, lse_rows)
    d_ff_tile, db_forget_row = _fox_gate_bwd(ds_sum, proj, bias_row)
    d_ret, dg_ret = _retention_bwd(proj, ret_pre, ret_states, d_mix, ret_norm_gain, consts)
    rs_out = _reduce_scatter_ici_start(d2d_out, d_ret, core, "rs_w_out")
    d_proj = jnp.concatenate(
        [d_ret, d_fq, d_fk, d_fv, d_ff_tile, jnp.zeros((l, WIN_N - 7 * GROUP - 128), MXU_DTYPE)], axis=1)
    gw_in = _mm_tn(d_proj, a, WIRE_DTYPE, "mm_gw_in", tm_cap=1536, after=rs_out[4])
    rs_in = _reduce_scatter_start(_extract_w_in_windows(gw_in), core, "rs_w_in")
    d_a = _mm_nn(d_proj, w_in_full, F32, "mm_d_a", tm_cap=544, tn_cap=256, tk_cap=WIN_N, after=rs_in[4])
    d_front, d_tokens, dg_norm1 = _rmsnorm_bwd(d_h1, d_a, h0, norm1_gain + rs_in[4][0, 0], "rmsnorm1_bwd", False)
    grad_x = d_tokens[None]
    d_meta = d_front[PAD_ROWS:CHUNK]

    d_conv_w = jnp.concatenate([d_conv[0, 0:3], d_conv[1, 0:3]], axis=1)
    d_conv_b = jnp.concatenate([d_conv[0, 3:4], d_conv[1, 3:4]], axis=1)
    pieces = [loss_part[:, 0:1], dg_norm1, db_forget_row[:, 0:N_HEADS], dg_ret, dg_norm2, d_conv_b, dg_final,
              d_meta.reshape(1, -1), d_conv_w.reshape(1, -1)]
    sizes = [p.shape[1] for p in pieces]
    flat = jnp.concatenate(pieces, axis=1)
    padded = -(-flat.shape[1] // 1024) * 1024
    flat = jnp.pad(flat, ((0, 0), (0, padded - flat.shape[1]))).reshape(padded // 128, 128)
    small_ar = _small_all_reduce_start(flat, d_tokens, "all_reduce_small")

    lead = lambda outs: tuple(o[None] for o in outs)
    fin_down = lead(_reduce_scatter_finish(rs_down, small_ar[4], chip, w_down[0], m_w_down[0], v_w_down[0], "rs_w_down"))
    fin_up = lead(_reduce_scatter_finish(rs_up, fin_down[3], chip, w_up[0], m_w_up[0], v_w_up[0], "rs_w_up"))
    fin_out = lead(_reduce_scatter_finish(rs_out, fin_up[3], chip, w_out[0], m_w_out[0], v_w_out[0], "rs_w_out"))
    fin_in = tuple(from_rows(o) for o in _reduce_scatter_finish(
        rs_in, fin_out[3], chip, w_in_rows, to_rows(m_w_in), to_rows(v_w_in), "rs_w_in"))
    g_w_down, g_w_up, g_w_out, g_w_in = fin_down[0], fin_up[0], fin_out[0], fin_in[0]
    early = [fin_down[1:], fin_up[1:], fin_out[1:], fin_in[1:]]
    total = _small_all_reduce_finish(small_ar, fin_in[3], dev1, "all_reduce_small").reshape(1, padded)
    offs = np.concatenate([[0], np.cumsum(sizes)])
    take = lambda k: total[:, int(offs[k]):int(offs[k + 1])]
    loss = take(0).reshape(())
    g_norm1, g_bf, g_ret_gain, g_norm2 = take(1), take(2), take(3), take(4)
    g_conv_b, g_final = take(5), take(6).reshape(d)
    g_meta = lax.dynamic_slice(take(7).reshape(N_META, d), (jnp.int32(0), (dev * (d // N_DEV)).astype(jnp.int32)),
                               (N_META, d // N_DEV))
    g_conv_w = lax.dynamic_slice(take(8).reshape(3, 2 * d_ff), (jnp.int32(0), (dev * up_shard).astype(jnp.int32)),
                                 (3, up_shard))[None]

    weights = [meta_tokens, norm1_gain, w_in, b_forget, ret_norm_gain, w_out, norm2_gain, w_up, conv_w, conv_b,
               w_down, final_norm_gain]
    grads = [g_meta, g_norm1, g_w_in, g_bf, g_ret_gain, g_w_out, g_norm2, g_w_up, g_conv_w, g_conv_b, g_w_down,
             g_final]
    done = {"w_down": early[0], "w_up": early[1], "w_out": early[2], "w_in": early[3]}
    ms = [m_meta_tokens, m_norm1_gain, m_w_in, m_b_forget, m_ret_norm_gain, m_w_out, m_norm2_gain, m_w_up, m_conv_w,
          m_conv_b, m_w_down, m_final_norm_gain]
    vs = [v_meta_tokens, v_norm1_gain, v_w_in, v_b_forget, v_ret_norm_gain, v_w_out, v_norm2_gain, v_w_up, v_conv_w,
          v_conv_b, v_w_down, v_final_norm_gain]
    names = ["meta", "norm1", "w_in", "b_forget", "ret_gain", "w_out", "norm2", "w_up", "conv_w", "conv_b", "w_down",
             "final_gain"]
    deltas, new_ms, new_vs = [], [], []
    for w, g, m, v, n in zip(weights, grads, ms, vs, names):
        dl, nm, nv = done[n] if n in done else _adamw(w, g, m, v, "adamw_" + n)
        deltas.append(dl)
        new_ms.append(nm)
        new_vs.append(nv)
    return (loss, grad_x, *grads, *deltas, *new_ms, *new_vs)
```

```python
import numpy as np
import jax
import jax.numpy as jnp
from jax import lax
from jax.experimental import pallas as pl
from jax.experimental.pallas import tpu as pltpu

F32 = jnp.float32
MXU_DTYPE = jnp.bfloat16
WIRE_DTYPE = jnp.bfloat16

N_DEV = 8
N_META = 16
CHUNK = 128
PAD_ROWS = CHUNK - N_META
N_HEADS = 8
HEAD_DIM = 128
GROUP = N_HEADS * HEAD_DIM
IN_DIM = 7 * GROUP + N_HEADS
WIN_SHARD = IN_DIM // N_DEV
WIN_ROWS = 912
WIN_BLOCK = 1024
WIN_STRIDE = 896
WIN_N = 7680
ROPE_BASE = 10000.0
NORM_EPS = 1e-6
NEG_BIG = -1e30
ADAM_LR, ADAM_B1, ADAM_B2, ADAM_EPS, ADAM_WD, ADAM_STEP = 0.001, 0.9, 0.999, 1e-08, 0.01, 10
VMEM_LIMIT = 52 * 1024 * 1024
MESH = pl.DeviceIdType.MESH
ANY = pl.BlockSpec(memory_space=pl.ANY)
VMEM_SPEC = pl.BlockSpec(memory_space=pltpu.VMEM)


def _params(sem=None):
    kw = {"vmem_limit_bytes": VMEM_LIMIT}
    if sem is not None:
        kw["dimension_semantics"] = sem
    return pltpu.CompilerParams(**kw)


def _divisor_tile(n, cap, unit):
    if n <= cap:
        return n
    best = None
    for t in range(unit, cap + 1, unit):
        if n % t == 0:
            best = t
    assert best is not None, (n, cap, unit)
    return best


def _my_position():
    return lax.axis_index("x"), lax.axis_index("y"), lax.axis_index("c")


def _device_index():
    x, y, c = _my_position()
    return 4 * x + 2 * y + c


def _all_gather(shard, name):
    r, c = shard.shape

    def body(x_ref, out_ref, send_sems, recv_sems, local_sem):
        mx, my, mc = _my_position()
        me, sibling = (mx, my, mc), (mx, my, 1 - mc)
        chips = [(1 - mx, my), (mx, 1 - my), (1 - mx, 1 - my)]

        def slot(px, py, pc):
            return out_ref.at[4 * px + 2 * py + pc]

        def copy(k, block, to, src=None):
            return pltpu.make_async_remote_copy(
                src_ref=slot(*block) if src is None else src, dst_ref=slot(*block),
                send_sem=send_sems.at[k], recv_sem=recv_sems.at[k], device_id=to, device_id_type=MESH)

        mine = pltpu.make_async_copy(x_ref, slot(*me), local_sem)
        mine.start()
        first = [copy(0, me, sibling, src=x_ref)]
        first += [copy(1 + j, me, (*chip, mc), src=x_ref) for j, chip in enumerate(chips)]
        for cp in first:
            cp.start()
        passed = [copy(4 + j, (*chip, mc), sibling) for j, chip in enumerate(chips)]
        for j, chip in enumerate(chips):
            copy(1 + j, (*chip, mc), me).wait_recv()
            passed[j].start()
        copy(0, sibling, me).wait_recv()
        for j, chip in enumerate(chips):
            copy(4 + j, (*chip, 1 - mc), me).wait_recv()
        for cp in first + passed:
            cp.wait_send()
        mine.wait()

    return pl.pallas_call(
        body, name=name,
        out_shape=jax.ShapeDtypeStruct((N_DEV, r, c), shard.dtype),
        in_specs=[ANY], out_specs=ANY,
        scratch_shapes=[pltpu.SemaphoreType.DMA((7,)), pltpu.SemaphoreType.DMA((7,)), pltpu.SemaphoreType.DMA],
    )(shard)


HBM_SPEC = pl.BlockSpec(memory_space=pltpu.HBM)
SEM_SPEC = pl.BlockSpec(memory_space=pltpu.SEMAPHORE)
DATAFLOW_EFFECT = pltpu.SideEffectType.DATAFLOW_SIDE_EFFECTING


def _in_hbm(a):
    return pltpu.with_memory_space_constraint(a, pltpu.HBM)


def _split_start(src, land, make_copies, n_copies, after, name):
    if isinstance(land, tuple):
        land = lax.empty(land, src.dtype)
    land_shape = land.shape
    def body(src_ref, land_ref, after_ref, send_sems, recv_sems, src_thru, land_thru, token):
        for cp in make_copies(src_ref, land_ref, send_sems, recv_sems):
            cp.start()
        token[...] = jnp.zeros_like(token)

    return pl.pallas_call(
        body, name=name,
        out_shape=(pltpu.SemaphoreType.DMA((n_copies,)), pltpu.SemaphoreType.DMA((n_copies,)),
                   pltpu.HBM(src.shape, src.dtype), pltpu.HBM(land_shape, land.dtype),
                   jax.ShapeDtypeStruct((8, 128), F32)),
        in_specs=(HBM_SPEC, HBM_SPEC, ANY), out_specs=(SEM_SPEC, SEM_SPEC, HBM_SPEC, HBM_SPEC, VMEM_SPEC),
        input_output_aliases={0: 2, 1: 3},
        compiler_params=pltpu.CompilerParams(has_side_effects=DATAFLOW_EFFECT),
    )(_in_hbm(src), _in_hbm(land), after)


def _split_wait(started, after, make_copies, name):
    send_sems, recv_sems, src_thru, land_thru, _ = started

    def body(src_ref, land_ref, send_sems_ref, recv_sems_ref, after_ref, src_dead, land_out):
        for cp in make_copies(src_ref, land_ref, send_sems_ref, recv_sems_ref):
            cp.wait_send()
            cp.wait_recv()

    return pl.pallas_call(
        body, name=name,
        out_shape=(pltpu.HBM(src_thru.shape, src_thru.dtype), pltpu.HBM(land_thru.shape, land_thru.dtype)),
        in_specs=(HBM_SPEC, HBM_SPEC, SEM_SPEC, SEM_SPEC, ANY), out_specs=(HBM_SPEC, HBM_SPEC),
        input_output_aliases={0: 0, 1: 1},
        compiler_params=pltpu.CompilerParams(has_side_effects=DATAFLOW_EFFECT),
    )(src_thru, land_thru, send_sems, recv_sems, after)


def _gather_copies(x_ref, land_ref, send_sems, recv_sems):
    mx, my, mc = _my_position()
    me = 4 * mx + 2 * my + mc
    targets = [(mx, my, 1 - mc), (1 - mx, my, mc), (mx, 1 - my, mc), (1 - mx, 1 - my, mc)]
    return [pltpu.make_async_remote_copy(
        src_ref=land_ref.at[me], dst_ref=land_ref.at[me], send_sem=send_sems.at[k], recv_sem=recv_sems.at[k],
        device_id=t, device_id_type=MESH) for k, t in enumerate(targets)]


def _own_slot(shard, dev, name):
    r, c = shard.shape
    tr = _divisor_tile(r, 640, 16)

    def body(s_ref, x_ref, o_ref):
        o_ref[...] = x_ref[...].astype(o_ref.dtype)

    return pl.pallas_call(
        body, name=name,
        out_shape=jax.ShapeDtypeStruct((N_DEV, r, c), WIRE_DTYPE),
        grid_spec=pltpu.PrefetchScalarGridSpec(
            num_scalar_prefetch=1, grid=(r // tr,),
            in_specs=[pl.BlockSpec((tr, c), lambda i, s: (i, 0))],
            out_specs=pl.BlockSpec((None, tr, c), lambda i, s: (s[0], i, 0))),
        compiler_params=_params(("parallel",)),
    )(dev, shard)


def _gather_start(shard, dev, after, name):
    return _split_start(jnp.zeros((8, 128), F32), _own_slot(shard, dev, name + "_own"), _gather_copies, 4, after, name)


def _gather_ring(shard, dev, after, name):
    r, c = shard.shape
    half = r // 2
    assert half % 16 == 0

    def body(x_ref, after_ref, land_in, land_ref, send_sems, recv_sems):
        mx, my, mc = _my_position()
        sibling, x_nbr, y_nbr = (mx, my, 1 - mc), (1 - mx, my, mc), (mx, 1 - my, mc)
        first, second = pl.ds(0, half), pl.ds(half, half)

        def slot(px, py, pc):
            return land_ref.at[4 * px + 2 * py + pc]

        def copy(k, src, dst, to):
            return pltpu.make_async_remote_copy(src_ref=src, dst_ref=dst, send_sem=send_sems.at[k],
                                                recv_sem=recv_sems.at[k], device_id=to, device_id_type=MESH)

        def arrived(k, dst):
            copy(k, dst, dst, sibling).wait_recv()

        mine = slot(mx, my, mc)
        from_x, from_y, from_d = slot(1 - mx, my, mc), slot(mx, 1 - my, mc), slot(1 - mx, 1 - my, mc)
        sent = [copy(0, x_ref, mine, sibling), copy(1, x_ref, mine, x_nbr), copy(2, x_ref, mine, y_nbr)]
        for cp in sent:
            cp.start()

        def send(k, src, to):
            cp = copy(k, src, src, to)
            cp.start()
            sent.append(cp)

        arrived(1, from_x)
        send(3, from_x.at[first], y_nbr)
        send(5, from_x, sibling)
        arrived(2, from_y)
        send(4, from_y.at[second], x_nbr)
        send(6, from_y, sibling)
        arrived(3, from_d.at[first])
        send(7, from_d.at[first], sibling)
        arrived(4, from_d.at[second])
        send(8, from_d.at[second], sibling)
        arrived(0, slot(mx, my, 1 - mc))
        arrived(5, slot(1 - mx, my, 1 - mc))
        arrived(6, slot(mx, 1 - my, 1 - mc))
        arrived(7, slot(1 - mx, 1 - my, 1 - mc).at[first])
        arrived(8, slot(1 - mx, 1 - my, 1 - mc).at[second])
        for cp in sent:
            cp.wait_send()

    land = _own_slot(shard, dev, name + "_own")
    return pl.pallas_call(
        body, name=name,
        out_shape=jax.ShapeDtypeStruct(land.shape, land.dtype),
        in_specs=[ANY, ANY, ANY], out_specs=ANY,
        input_output_aliases={2: 0},
        scratch_shapes=[pltpu.SemaphoreType.DMA((9,)), pltpu.SemaphoreType.DMA((9,))],
    )(shard, after, land)


def _pass_copies(unused_ref, land_ref, send_sems, recv_sems):
    mx, my, mc = _my_position()
    chips = [(1 - mx, my), (mx, 1 - my), (1 - mx, 1 - my)]
    return [pltpu.make_async_remote_copy(
        src_ref=land_ref.at[4 * cx + 2 * cy + mc], dst_ref=land_ref.at[4 * cx + 2 * cy + mc],
        send_sem=send_sems.at[j], recv_sem=recv_sems.at[j],
        device_id=(mx, my, 1 - mc), device_id_type=MESH) for j, (cx, cy) in enumerate(chips)]


def _gather_pass_start(started, after, name):
    _, land = _split_wait(started, after, _gather_copies, name + "_wait")
    return _split_start(jnp.zeros((8, 128), F32), land, _pass_copies, 3, after, name + "_pass_start")


def _gather_pass_finish(pass_started, after, name):
    return _split_wait(pass_started, after, _pass_copies, name + "_pass_wait")[1]


def _ring_first_copies(unused_ref, land_ref, send_sems, recv_sems):
    mx, my, mc = _my_position()
    mine = land_ref.at[4 * mx + 2 * my + mc]
    targets = [(mx, my, 1 - mc), (1 - mx, my, mc), (mx, 1 - my, mc)]
    return [pltpu.make_async_remote_copy(
        src_ref=mine, dst_ref=mine, send_sem=send_sems.at[k], recv_sem=recv_sems.at[k],
        device_id=t, device_id_type=MESH) for k, t in enumerate(targets)]


def _ring_second_copies(unused_ref, land_ref, send_sems, recv_sems):
    mx, my, mc = _my_position()
    half = land_ref.shape[1] // 2
    from_x = land_ref.at[4 * (1 - mx) + 2 * my + mc, pl.ds(0, half)]
    from_y = land_ref.at[4 * mx + 2 * (1 - my) + mc, pl.ds(half, half)]
    return [pltpu.make_async_remote_copy(
                src_ref=from_x, dst_ref=from_x, send_sem=send_sems.at[0], recv_sem=recv_sems.at[0],
                device_id=(mx, 1 - my, mc), device_id_type=MESH),
            pltpu.make_async_remote_copy(
                src_ref=from_y, dst_ref=from_y, send_sem=send_sems.at[1], recv_sem=recv_sems.at[1],
                device_id=(1 - mx, my, mc), device_id_type=MESH)]


def _ring_start(shard, dev, after, name):
    return _split_start(jnp.zeros((8, 128), F32), _own_slot(shard, dev, name + "_own"), _ring_first_copies, 3, after,
                        name)


def _ring_forward(started, after, name):
    _, land = _split_wait(started, after, _ring_first_copies, name + "_wait")
    return _split_start(jnp.zeros((8, 128), F32), land, _ring_second_copies, 2, after, name + "_fwd_start")


def _ring_finish(forwarded, after, name):
    _, land = _split_wait(forwarded, after, _ring_second_copies, name + "_fwd_wait")
    return _sibling_pass(land, name)


def _gather_finish(started, after, name):
    _, land = _split_wait(started, after, _gather_copies, name + "_wait")
    return _sibling_pass(land, name)


def _sibling_pass(land, name):
    def body(land_in, land_ref, send_sems, recv_sems):
        mx, my, mc = _my_position()
        chips = [(1 - mx, my), (mx, 1 - my), (1 - mx, 1 - my)]
        copies = [pltpu.make_async_remote_copy(
            src_ref=land_ref.at[4 * cx + 2 * cy + mc], dst_ref=land_ref.at[4 * cx + 2 * cy + mc],
            send_sem=send_sems.at[j], recv_sem=recv_sems.at[j],
            device_id=(mx, my, 1 - mc), device_id_type=MESH) for j, (cx, cy) in enumerate(chips)]
        for cp in copies:
            cp.start()
        for j, (cx, cy) in enumerate(chips):
            copies[j].wait_send()
            pltpu.make_async_remote_copy(
                src_ref=land_ref.at[4 * cx + 2 * cy + 1 - mc], dst_ref=land_ref.at[4 * cx + 2 * cy + 1 - mc],
                send_sem=send_sems.at[j], recv_sem=recv_sems.at[j],
                device_id=(mx, my, 1 - mc), device_id_type=MESH).wait_recv()

    return pl.pallas_call(
        body, name=name + "_pass",
        out_shape=jax.ShapeDtypeStruct(land.shape, land.dtype),
        in_specs=[ANY], out_specs=ANY,
        input_output_aliases={0: 0},
        scratch_shapes=[pltpu.SemaphoreType.DMA((3,)), pltpu.SemaphoreType.DMA((3,))],
    )(land)


def _chip_copies(p_ref, land_ref, send_sems, recv_sems):
    mx, my, mc = _my_position()
    chips = [(1 - mx, my), (mx, 1 - my), (1 - mx, 1 - my)]
    return [pltpu.make_async_remote_copy(
        src_ref=p_ref.at[2 * cx + cy], dst_ref=land_ref.at[j], send_sem=send_sems.at[j], recv_sem=recv_sems.at[j],
        device_id=(cx, cy, mc), device_id_type=MESH) for j, (cx, cy) in enumerate(chips)]


def _reduce_scatter_start(g, core, name):
    pair = _pair_sum(g, _exchange_sibling(g, name + "_d2d"), core, name + "_pairsum")
    return _split_start(pair, (3,) + pair.shape[1:], _chip_copies, 3, g, name + "_ici_start")


def _sibling_copies(g_ref, land_ref, send_sems, recv_sems):
    mx, my, mc = _my_position()
    return [pltpu.make_async_remote_copy(
        src_ref=g_ref.at[2 * k + (1 - mc)], dst_ref=land_ref.at[k], send_sem=send_sems.at[k], recv_sem=recv_sems.at[k],
        device_id=(mx, my, 1 - mc), device_id_type=MESH) for k in range(4)]


def _reduce_scatter_d2d_start(g, after, name):
    return _split_start(g, (4,) + g.shape[1:], _sibling_copies, 4, after, name + "_d2d_start")


def _reduce_scatter_ici_start(d2d_started, after, core, name):
    g, from_sibling = _split_wait(d2d_started, after, _sibling_copies, name + "_d2d_wait")
    pair = _pair_sum(g, from_sibling, core, name + "_pairsum")
    return _split_start(pair, (3,) + pair.shape[1:], _chip_copies, 3, g, name + "_ici_start")


def _reduce_scatter_finish(started, after, chip, w, m, v, name):
    pair, from_chips = _split_wait(started, after, _chip_copies, name + "_ici_wait")
    return _final_sum_adamw(pair, from_chips, chip, w, m, v, name + "_sum_adamw")


def _exchange_sibling(g, name):
    _, r, c = g.shape

    def body(g_ref, out_ref, send_sems, recv_sems):
        mx, my, mc = _my_position()
        copies = [
            pltpu.make_async_remote_copy(
                src_ref=g_ref.at[2 * k + (1 - mc)], dst_ref=out_ref.at[k],
                send_sem=send_sems.at[k], recv_sem=recv_sems.at[k],
                device_id=(mx, my, 1 - mc), device_id_type=MESH)
            for k in range(4)]
        for cp in copies:
            cp.start()
        for cp in copies:
            cp.wait()

    return pl.pallas_call(
        body, name=name,
        out_shape=jax.ShapeDtypeStruct((4, r, c), g.dtype),
        in_specs=[ANY], out_specs=ANY,
        scratch_shapes=[pltpu.SemaphoreType.DMA((4,)), pltpu.SemaphoreType.DMA((4,))],
    )(g)


def _pair_sum(g, recv, core, name):
    _, r, c = g.shape
    tr = _divisor_tile(r, 512, 16)

    def body(s_ref, g_ref, r_ref, o_ref):
        o_ref[...] = (g_ref[...].astype(F32) + r_ref[...].astype(F32)).astype(o_ref.dtype)

    return pl.pallas_call(
        body, name=name,
        out_shape=jax.ShapeDtypeStruct((4, r, c), g.dtype),
        grid_spec=pltpu.PrefetchScalarGridSpec(
            num_scalar_prefetch=1, grid=(4, r // tr),
            in_specs=[pl.BlockSpec((None, tr, c), lambda k, i, s: (2 * k + s[0], i, 0)),
                      pl.BlockSpec((None, tr, c), lambda k, i, s: (k, i, 0))],
            out_specs=pl.BlockSpec((None, tr, c), lambda k, i, s: (k, i, 0))),
        compiler_params=_params(("parallel", "parallel")),
    )(core, g, recv)


def _adamw_math(w, g, m, v):
    nm = ADAM_B1 * m + (1.0 - ADAM_B1) * g
    nv = ADAM_B2 * v + (1.0 - ADAM_B2) * (g * g)
    m_hat = nm / (1.0 - ADAM_B1 ** ADAM_STEP)
    v_hat = nv / (1.0 - ADAM_B2 ** ADAM_STEP)
    return -ADAM_LR * (m_hat / (jnp.sqrt(v_hat) + ADAM_EPS) + ADAM_WD * w), nm, nv


def _final_sum_adamw(p, recv, chip, w, m, v, name):
    _, r, c = p.shape
    tr = _divisor_tile(r, 256, 16)
    tile = lambda: pl.BlockSpec((tr, c), lambda i, s: (i, 0))

    def body(s_ref, p_ref, r_ref, w_ref, m_ref, v_ref, g_ref, d_ref, nm_ref, nv_ref):
        g = p_ref[...].astype(F32)
        for j in range(3):
            g = g + r_ref[j].astype(F32)
        g_ref[...] = g
        d_ref[...], nm_ref[...], nv_ref[...] = _adamw_math(w_ref[...], g, m_ref[...], v_ref[...])

    sds = jax.ShapeDtypeStruct((r, c), F32)
    return pl.pallas_call(
        body, name=name,
        out_shape=(sds, sds, sds, sds),
        grid_spec=pltpu.PrefetchScalarGridSpec(
            num_scalar_prefetch=1, grid=(r // tr,),
            in_specs=[pl.BlockSpec((None, tr, c), lambda i, s: (s[0], i, 0)),
                      pl.BlockSpec((3, tr, c), lambda i, s: (0, i, 0)), tile(), tile(), tile()],
            out_specs=(tile(), tile(), tile(), tile())),
        compiler_params=_params(("parallel",)),
    )(chip, p, recv, w, m, v)


def _all_to_all_copies(v_ref, land_ref, send_sems, recv_sems):
    mx, my, mc = _my_position()
    me = 4 * mx + 2 * my + mc
    copies = []
    for rel in range(1, N_DEV):
        bx, by, bc = (rel >> 2) & 1, (rel >> 1) & 1, rel & 1
        target = (1 - mx if bx else mx, 1 - my if by else my, 1 - mc if bc else mc)
        copies.append(pltpu.make_async_remote_copy(
            src_ref=v_ref, dst_ref=land_ref.at[me], send_sem=send_sems.at[rel - 1], recv_sem=recv_sems.at[rel - 1],
            device_id=target, device_id_type=MESH))
    return copies


def _small_all_reduce_start(v, after, name):
    return _split_start(v, (N_DEV,) + v.shape, _all_to_all_copies, N_DEV - 1, after, name + "_start")


def _small_all_reduce_finish(started, after, dev, name):
    v, land = _split_wait(started, after, _all_to_all_copies, name + "_wait")
    rows = v.shape[0]

    def body(me_ref, v_ref, land_ref, o_ref):
        for j in range(N_DEV):
            @pl.when(me_ref[0] == j)
            def _():
                o_ref[...] = v_ref[...] if j == 0 else o_ref[...] + v_ref[...]

            @pl.when(me_ref[0] != j)
            def _():
                o_ref[...] = land_ref[j] if j == 0 else o_ref[...] + land_ref[j]

    return pl.pallas_call(
        body, name=name + "_sum",
        out_shape=jax.ShapeDtypeStruct((rows, 128), F32),
        grid_spec=pltpu.PrefetchScalarGridSpec(
            num_scalar_prefetch=1, grid=(1,),
            in_specs=[pl.BlockSpec((rows, 128), lambda i, s: (0, 0)),
                      pl.BlockSpec((N_DEV, rows, 128), lambda i, s: (0, 0, 0))],
            out_specs=pl.BlockSpec((rows, 128), lambda i, s: (0, 0))),
        compiler_params=_params(("arbitrary",)),
    )(dev, v, land)


def _assemble_w_in(blocks):
    rows, d = WIN_ROWS, blocks.shape[2]
    tc = _divisor_tile(d, 256, 128)
    n_tiles = WIN_N // 128
    last = (N_DEV * WIN_STRIDE) // 128

    def body(b_ref, o_ref):
        win = []
        for i in range(N_DEV):
            w = jnp.concatenate([b_ref[i].astype(F32), jnp.zeros((WIN_BLOCK - rows, tc), F32)], axis=0)
            win.append(pltpu.roll(w, i, 0) if i else w)
        for t in range(n_tiles):
            if t > last:
                o_ref[t * 128:(t + 1) * 128, :] = jnp.zeros((128, tc), o_ref.dtype)
                continue
            i = min(t // 7, N_DEV - 1)
            k = t - 7 * i
            val = win[i][k * 128:(k + 1) * 128, :]
            if k == 0 and i >= 1:
                val = val + win[i - 1][7 * 128:8 * 128, :]
            o_ref[t * 128:(t + 1) * 128, :] = val.astype(o_ref.dtype)

    return pl.pallas_call(
        body, name="assemble_w_in",
        out_shape=jax.ShapeDtypeStruct((WIN_N, d), blocks.dtype),
        grid=(d // tc,),
        in_specs=[pl.BlockSpec((N_DEV, rows, tc), lambda j: (0, 0, j))],
        out_specs=pl.BlockSpec((WIN_N, tc), lambda j: (0, j)),
        compiler_params=_params(("parallel",)),
    )(blocks)


def _extract_w_in_windows(g):
    _, d = g.shape
    tc = _divisor_tile(d, 256, 128)

    def body(g_ref, o_ref):
        for j in range(N_DEV):
            w = g_ref[WIN_STRIDE * j:WIN_STRIDE * j + WIN_BLOCK, :].astype(F32)
            w = pltpu.roll(w, WIN_BLOCK - j, 0) if j else w
            o_ref[j] = w[0:WIN_ROWS, :].astype(o_ref.dtype)

    return pl.pallas_call(
        body, name="extract_w_in_windows",
        out_shape=jax.ShapeDtypeStruct((N_DEV, WIN_ROWS, d), g.dtype),
        grid=(d // tc,),
        in_specs=[pl.BlockSpec((WIN_N, tc), lambda j: (0, j))],
        out_specs=pl.BlockSpec((N_DEV, WIN_ROWS, tc), lambda j: (0, 0, j)),
        compiler_params=_params(("parallel",)),
    )(g)


def _mm(a, b, *, a_spec, b_spec, o_spec, out_shape, grid, contract, nk, name, after=None):
    dn = (((contract[0],), (contract[1],)), ((), ()))
    tm, tn = o_spec.block_shape[-2:]
    behind = [] if after is None else [after]

    def body(a_ref, b_ref, *rest):
        o_ref, *scratch = rest[len(behind):]
        part = lax.dot_general(a_ref[...], b_ref[...], dn, preferred_element_type=F32)
        if nk == 1:
            o_ref[...] = part.astype(o_ref.dtype)
            return
        acc = scratch[0]
        k = pl.program_id(2)

        @pl.when(k == 0)
        def _():
            acc[...] = part

        @pl.when(k > 0)
        def _():
            acc[...] += part

        @pl.when(k == nk - 1)
        def _():
            o_ref[...] = acc[...].astype(o_ref.dtype)

    return pl.pallas_call(
        body, name=name, out_shape=out_shape, grid=grid,
        in_specs=[a_spec, b_spec] + [ANY] * len(behind), out_specs=o_spec,
        scratch_shapes=[] if nk == 1 else [pltpu.VMEM((tm, tn), F32)],
        compiler_params=_params(("parallel", "parallel", "arbitrary")),
    )(a, b, *behind)


def _mm_nn(a, b, out_dtype, name, tm_cap=1088, tn_cap=512, tk_cap=2048, after=None):
    m, k = a.shape
    _, n = b.shape
    tm, tn, tk = _divisor_tile(m, tm_cap, 16), _divisor_tile(n, tn_cap, 128), _divisor_tile(k, tk_cap, 128)
    return _mm(a, b,
               a_spec=pl.BlockSpec((tm, tk), lambda i, j, kk: (i, kk)),
               b_spec=pl.BlockSpec((tk, tn), lambda i, j, kk: (kk, j)),
               o_spec=pl.BlockSpec((tm, tn), lambda i, j, kk: (i, j)),
               out_shape=jax.ShapeDtypeStruct((m, n), out_dtype),
               grid=(m // tm, n // tn, k // tk), contract=(1, 0), nk=k // tk, name=name, after=after)


def _mm_nt(a, b, out_dtype, name, tm_cap=1088, tn_cap=512, tk_cap=2048, after=None):
    m, k = a.shape
    n, _ = b.shape
    tm, tn, tk = _divisor_tile(m, tm_cap, 16), _divisor_tile(n, tn_cap, 128), _divisor_tile(k, tk_cap, 128)
    return _mm(a, b,
               a_spec=pl.BlockSpec((tm, tk), lambda i, j, kk: (i, kk)),
               b_spec=pl.BlockSpec((tn, tk), lambda i, j, kk: (j, kk)),
               o_spec=pl.BlockSpec((tm, tn), lambda i, j, kk: (i, j)),
               out_shape=jax.ShapeDtypeStruct((m, n), out_dtype),
               grid=(m // tm, n // tn, k // tk), contract=(1, 1), nk=k // tk, name=name, after=after)


def _mm_tn(a, b, out_dtype, name, tm_cap=1024, tn_cap=512, after=None):
    l, m = a.shape
    _, n = b.shape
    tm, tn = _divisor_tile(m, tm_cap, 128), _divisor_tile(n, tn_cap, 128)
    return _mm(a, b,
               a_spec=pl.BlockSpec((l, tm), lambda i, j, kk: (0, i)),
               b_spec=pl.BlockSpec((l, tn), lambda i, j, kk: (0, j)),
               o_spec=pl.BlockSpec((tm, tn), lambda i, j, kk: (i, j)),
               out_shape=jax.ShapeDtypeStruct((m, n), out_dtype),
               grid=(m // tm, n // tn, 1), contract=(0, 0), nk=1, name=name, after=after)


def _pair_split(shard):
    left = shard % ATTN_BLOCK
    assert left in (0, CHUNK) and shard > left
    return shard - left, left


def _mm_up(cn, w_up_blocks, after):
    l, d = cn.shape
    n, _, shard = w_up_blocks.shape
    main, left = _pair_split(shard)
    tm = _divisor_tile(l, 544, 16)

    def body(a_ref, b_ref, after_ref, o_ref):
        a = a_ref[...]
        for s in range(2):
            o_ref[:, s * shard:s * shard + main] = _dot(a, b_ref[s, :, 0:main])
        if left:
            tail = _dot(a, jnp.concatenate([b_ref[0, :, main:], b_ref[1, :, main:]], axis=1))
            o_ref[:, main:shard] = tail[:, 0:left]
            o_ref[:, shard + main:2 * shard] = tail[:, left:]

    return pl.pallas_call(
        body, name="mm_up", out_shape=jax.ShapeDtypeStruct((l, n * shard), F32), grid=(l // tm, n // 2),
        in_specs=[pl.BlockSpec((tm, d), lambda i, j: (i, 0)),
                  pl.BlockSpec((2, d, shard), lambda i, j: (j, 0, 0)), ANY],
        out_specs=pl.BlockSpec((tm, 2 * shard), lambda i, j: (i, j)),
        compiler_params=_params(("parallel", "parallel")),
    )(cn, w_up_blocks, after)


def _mm_gw_up(cn, d_u):
    l, d = cn.shape
    _, _, d_ff = d_u.shape
    shard = 2 * d_ff // N_DEV
    pairs_per_half = d_ff // (2 * shard)
    tm = _divisor_tile(d, 512, 128)

    def body(a_ref, b_ref, o_ref):
        res = _dot_tn(a_ref[...], b_ref[...])
        o_ref[0] = res[:, 0:shard].astype(o_ref.dtype)
        o_ref[1] = res[:, shard:].astype(o_ref.dtype)

    return pl.pallas_call(
        body, name="mm_gw_up", out_shape=jax.ShapeDtypeStruct((N_DEV, d, shard), WIRE_DTYPE),
        grid=(d // tm, N_DEV // 2),
        in_specs=[pl.BlockSpec((l, tm), lambda i, j: (0, i)),
                  pl.BlockSpec((None, l, 2 * shard), lambda i, j: (j // pairs_per_half, 0, j % pairs_per_half))],
        out_specs=pl.BlockSpec((2, tm, shard), lambda i, j: (j, i, 0)),
        compiler_params=_params(("parallel", "parallel")),
    )(cn, d_u)


def _mm_d_cn(d_u, w_up_blocks, after):
    _, l, d_ff = d_u.shape
    n, d, shard = w_up_blocks.shape
    per = d_ff // shard
    main, left = _pair_split(shard)
    tm, tn = _divisor_tile(l, 544, 16), _divisor_tile(d, 256, 128)

    def body(a_ref, b_ref, after_ref, o_ref):
        acc = None
        for k in range(0, n, 2):
            half, c0 = k // per, (k % per) * shard
            parts = [_dot_nt(a_ref[half, :, c0 + s * shard:c0 + s * shard + main], b_ref[k + s, :, 0:main])
                     for s in range(2)]
            if left:
                a_tail = jnp.concatenate([a_ref[half, :, c0 + s * shard + main:c0 + (s + 1) * shard] for s in range(2)],
                                         axis=1)
                b_tail = jnp.concatenate([b_ref[k + s, :, main:] for s in range(2)], axis=1)
                parts.append(_dot_nt(a_tail, b_tail))
            for part in parts:
                acc = part if acc is None else acc + part
        o_ref[...] = acc

    return pl.pallas_call(
        body, name="mm_d_cn", out_shape=jax.ShapeDtypeStruct((l, d), F32), grid=(l // tm, d // tn),
        in_specs=[pl.BlockSpec((2, tm, d_ff), lambda i, j: (0, i, 0)),
                  pl.BlockSpec((n, tn, shard), lambda i, j: (0, j, 0)), ANY],
        out_specs=pl.BlockSpec((tm, tn), lambda i, j: (i, j)),
        compiler_params=_params(("parallel", "parallel")),
    )(d_u, w_up_blocks, after)


def _row_tile(l):
    return _divisor_tile(l, 544, 8)


def _rms(x, gain):
    return (x * lax.rsqrt(jnp.mean(x * x, axis=-1, keepdims=True) + NORM_EPS) * gain).astype(MXU_DTYPE)


def _embed_rmsnorm(x, meta, gain):
    seq, d = x.shape
    l = CHUNK + seq
    row = pl.BlockSpec((CHUNK, d), lambda i: (i, 0))

    def body(x_ref, m_ref, g_ref, h_ref, n_ref):
        @pl.when(pl.program_id(0) == 0)
        def _():
            h_ref[...] = jnp.concatenate([jnp.zeros((PAD_ROWS, d), F32), m_ref[...]], axis=0)

        @pl.when(pl.program_id(0) > 0)
        def _():
            h_ref[...] = x_ref[...]

        n_ref[...] = _rms(h_ref[...], g_ref[...])

    return pl.pallas_call(
        body, name="embed_rmsnorm1",
        out_shape=(jax.ShapeDtypeStruct((l, d), F32), jax.ShapeDtypeStruct((l, d), MXU_DTYPE)),
        grid=(l // CHUNK,),
        in_specs=[pl.BlockSpec((CHUNK, d), lambda i: (jnp.maximum(i - 1, 0), 0)),
                  pl.BlockSpec((N_META, d), lambda i: (0, 0)), pl.BlockSpec((1, d), lambda i: (0, 0))],
        out_specs=(row, row),
        compiler_params=_params(("parallel",)),
    )(x, meta, gain)


def _out_proj_resid_rmsnorm(mix, w_out, h0, gain):
    l, d = h0.shape
    tm = _divisor_tile(l, 272, 16)
    row = pl.BlockSpec((tm, d), lambda i: (i, 0))

    def body(a_ref, b_ref, h_ref, g_ref, s_ref, n_ref):
        x = h_ref[...] + _dot(a_ref[...], b_ref[...])
        s_ref[...] = x
        n_ref[...] = _rms(x, g_ref[...])

    return pl.pallas_call(
        body, name="mm_out_resid_rmsnorm2",
        out_shape=(jax.ShapeDtypeStruct((l, d), F32), jax.ShapeDtypeStruct((l, d), MXU_DTYPE)),
        grid=(l // tm,),
        in_specs=[row, pl.BlockSpec((d, d), lambda i: (0, 0)), row, pl.BlockSpec((1, d), lambda i: (0, 0))],
        out_specs=(row, row),
        compiler_params=_params(("parallel",)),
    )(mix, w_out, h0, gain)


def _rmsnorm_bwd(d_res, d_normed, x, gain, name, with_mxu_copy):
    l, d = x.shape
    tr = _row_tile(l) if with_mxu_copy else CHUNK
    row = pl.BlockSpec((tr, d), lambda i: (i, 0))
    vec = pl.BlockSpec((1, d), lambda i: (0, 0))

    def body(dres_ref, dn_ref, x_ref, g_ref, dx_ref, other_ref, dg_ref):
        i = pl.program_id(0)
        xv = x_ref[...]
        r = lax.rsqrt(jnp.mean(xv * xv, axis=-1, keepdims=True) + NORM_EPS)
        xh = xv * r
        dn = dn_ref[...]
        dxh = dn * g_ref[...]
        dx = dres_ref[...] + r * (dxh - xh * jnp.mean(dxh * xh, axis=-1, keepdims=True))
        if with_mxu_copy:
            dx_ref[...] = dx
            other_ref[...] = dx.astype(MXU_DTYPE)
        else:
            @pl.when(i == 0)
            def _():
                dx_ref[...] = dx

            @pl.when(i > 0)
            def _():
                other_ref[...] = dx

        @pl.when(i == 0)
        def _():
            dg_ref[...] = jnp.zeros_like(dg_ref)

        dg_ref[...] += jnp.sum(dn * xh, axis=0, keepdims=True)

    if with_mxu_copy:
        outs = [jax.ShapeDtypeStruct((l, d), F32), jax.ShapeDtypeStruct((l, d), MXU_DTYPE)]
        specs = [row, row]
    else:
        outs = [jax.ShapeDtypeStruct((CHUNK, d), F32), jax.ShapeDtypeStruct((l - CHUNK, d), F32)]
        specs = [pl.BlockSpec((CHUNK, d), lambda i: (0, 0)), pl.BlockSpec((CHUNK, d), lambda i: (jnp.maximum(i - 1, 0), 0))]
    outs.append(jax.ShapeDtypeStruct((1, d), F32))
    specs.append(vec)
    return pl.pallas_call(body, name=name, out_shape=tuple(outs), grid=(l // tr,),
                          in_specs=[row, row, row, vec], out_specs=tuple(specs),
                          compiler_params=_params(("arbitrary",)))(d_res, d_normed, x, gain)


def _loss_head(h1, mlp_out, gain, target):
    l, d = h1.shape
    n_blocks = l // CHUNK
    row = pl.BlockSpec((CHUNK, d), lambda i: (i, 0))
    vec = pl.BlockSpec((1, d), lambda i: (0, 0))
    tgt = pl.BlockSpec((CHUNK, d), lambda i: (jnp.maximum(i - 1, 0), 0))

    def body(h_ref, m_ref, g_ref, t_ref, dh_ref, dhb_ref, dg_ref, loss_ref, sq_ref):
        i = pl.program_id(0)
        x = h_ref[...] + m_ref[...]
        r = lax.rsqrt(jnp.mean(x * x, axis=-1, keepdims=True) + NORM_EPS)
        xh = x * r
        g = g_ref[...]
        real = i >= 1
        err = jnp.where(real, xh * g - t_ref[...], 0.0)
        dy = err * (1.0 / d)
        dxh = dy * g
        dh = r * (dxh - xh * jnp.mean(dxh * xh, axis=-1, keepdims=True))
        dh_ref[...] = dh
        dhb_ref[...] = dh.astype(MXU_DTYPE)

        @pl.when(i == 0)
        def _():
            dg_ref[...] = jnp.zeros_like(dg_ref)
            sq_ref[...] = jnp.zeros_like(sq_ref)

        dg_ref[...] += jnp.sum(dy * xh, axis=0, keepdims=True)
        sq_ref[...] += jnp.sum(err * err, axis=0, keepdims=True)

        @pl.when(i == n_blocks - 1)
        def _():
            total = jnp.sum(sq_ref[...], axis=-1, keepdims=True) * (0.5 / d)
            loss_ref[...] = jnp.broadcast_to(total, (1, 128))

    return pl.pallas_call(
        body, name="loss_head",
        out_shape=(jax.ShapeDtypeStruct((l, d), F32), jax.ShapeDtypeStruct((l, d), MXU_DTYPE),
                   jax.ShapeDtypeStruct((1, d), F32), jax.ShapeDtypeStruct((1, 128), F32)),
        grid=(n_blocks,), in_specs=[row, row, vec, tgt],
        out_specs=(row, row, vec, pl.BlockSpec((1, 128), lambda i: (0, 0))),
        scratch_shapes=[pltpu.VMEM((1, d), F32)],
        compiler_params=_params(("arbitrary",)),
    )(h1, mlp_out, gain, target)


def _dot(a, b):
    return jnp.dot(a, b, preferred_element_type=F32)


def _dot_nt(a, b):
    return lax.dot_general(a, b, (((1,), (1,)), ((), ())), preferred_element_type=F32)


def _dot_tn(a, b):
    return lax.dot_general(a, b, (((0,), (0,)), ((), ())), preferred_element_type=F32)


def _rope(t, cos2, sin2):
    return t * cos2 + pltpu.roll(t, HEAD_DIM // 2, 1) * sin2


def _rope_bwd(dr, cos2, sin2):
    return dr * cos2 + pltpu.roll(dr * sin2, HEAD_DIM // 2, 1)


def _sigmoid(x):
    return 1.0 / (1.0 + jnp.exp(-x))


def _row_valid(block, rows):
    r = block * CHUNK + lax.broadcasted_iota(jnp.int32, (rows, 1), 0)
    return r >= PAD_ROWS


def _retention_consts(l):
    pos = jnp.arange(l, dtype=F32) - PAD_ROWS
    inv_freq = 1.0 / (ROPE_BASE ** (jnp.arange(0, HEAD_DIM, 2, dtype=F32) / HEAD_DIM))
    ang = pos[:, None] * inv_freq[None, :]
    cos, sin = jnp.cos(ang), jnp.sin(ang)
    cos2 = jnp.concatenate([cos, cos], axis=-1)
    sin2 = jnp.concatenate([-sin, sin], axis=-1)
    log_g = jnp.log1p(-jnp.exp2(-5.0 - jnp.arange(N_HEADS, dtype=F32)))
    idx = jnp.arange(CHUNK, dtype=F32)
    diff = idx[:, None] - idx[None, :]
    decay = jnp.where(diff >= 0, jnp.exp(jnp.maximum(diff, 0.0)[None] * log_g[:, None, None]), 0.0)
    xi = jnp.exp((idx + 1.0)[None, :] * log_g[:, None])
    zeta = jnp.exp((CHUNK - 1.0 - idx)[None, :] * log_g[:, None])
    g_chunk = jnp.exp(CHUNK * log_g)
    bcast = lambda v: jnp.broadcast_to(v[:, :, None], (N_HEADS, CHUNK, HEAD_DIM))
    g_rows = jnp.broadcast_to(g_chunk[:, None, None], (N_HEADS, 8, HEAD_DIM))
    return cos2, sin2, decay, bcast(xi), bcast(zeta), g_rows


def _retention_fwd(proj, ret_gain, consts):
    l = proj.shape[0]
    n_chunks = l // CHUNK
    cos2, sin2, decay, xi, zeta, g_rows = consts
    scale = HEAD_DIM ** -0.5

    def body(p_ref, cos_ref, sin_ref, dec_ref, xi_ref, zeta_ref, gr_ref, gain_ref,
             mix_ref, o_ref, st_ref, state):
        c = pl.program_id(0)

        @pl.when(c == 0)
        def _():
            state[...] = jnp.zeros_like(state)

        cos_v, sin_v = cos_ref[...], sin_ref[...]
        valid = _row_valid(c, CHUNK)
        for h in range(N_HEADS):
            cols = slice(h * HEAD_DIM, (h + 1) * HEAD_DIM)
            q = p_ref[:, h * HEAD_DIM:(h + 1) * HEAD_DIM]
            k = p_ref[:, GROUP + h * HEAD_DIM:GROUP + (h + 1) * HEAD_DIM]
            v = p_ref[:, 2 * GROUP + h * HEAD_DIM:2 * GROUP + (h + 1) * HEAD_DIM]
            g = p_ref[:, 3 * GROUP + h * HEAD_DIM:3 * GROUP + (h + 1) * HEAD_DIM]
            rq = _rope(q, cos_v, sin_v).astype(MXU_DTYPE)
            rk = _rope(k, cos_v, sin_v) * scale
            rkb = rk.astype(MXU_DTYPE)
            vb = v.astype(MXU_DTYPE)
            st = state[h]
            st_ref[h] = st
            s = _dot_nt(rq, rkb) * dec_ref[h]
            o = _dot(s.astype(MXU_DTYPE), vb) + _dot(rq, st.astype(MXU_DTYPE)) * xi_ref[h]
            kz = (rk * zeta_ref[h]).astype(MXU_DTYPE)
            state[h] = gr_ref[h, 0:1, :] * st + _dot_tn(kz, vb)
            o_ref[:, cols] = o
            mu = jnp.mean(o, axis=-1, keepdims=True)
            oc = o - mu
            yn = oc * lax.rsqrt(jnp.mean(oc * oc, axis=-1, keepdims=True) + NORM_EPS)
            ret = (g * _sigmoid(g)) * (yn * gain_ref[:, cols])
            mix_ref[:, cols] = jnp.where(valid, ret, 0.0).astype(mix_ref.dtype)

    head_tab = pl.BlockSpec((N_HEADS, CHUNK, HEAD_DIM), lambda c: (0, 0, 0))
    return pl.pallas_call(
        body, name="retention_fwd",
        out_shape=(jax.ShapeDtypeStruct((l, 2 * GROUP), MXU_DTYPE), jax.ShapeDtypeStruct((l, GROUP), F32),
                   jax.ShapeDtypeStruct((n_chunks, N_HEADS, HEAD_DIM, HEAD_DIM), F32)),
        grid=(n_chunks,),
        in_specs=[pl.BlockSpec((CHUNK, 4 * GROUP), lambda c: (c, 0)),
                  pl.BlockSpec((CHUNK, HEAD_DIM), lambda c: (c, 0)),
                  pl.BlockSpec((CHUNK, HEAD_DIM), lambda c: (c, 0)),
                  head_tab, head_tab, head_tab,
                  pl.BlockSpec((N_HEADS, 8, HEAD_DIM), lambda c: (0, 0, 0)),
                  pl.BlockSpec((1, GROUP), lambda c: (0, 0))],
        out_specs=(pl.BlockSpec((CHUNK, GROUP), lambda c: (c, 0)),
                   pl.BlockSpec((CHUNK, GROUP), lambda c: (c, 0)),
                   pl.BlockSpec((None, N_HEADS, HEAD_DIM, HEAD_DIM), lambda c: (c, 0, 0, 0))),
        scratch_shapes=[pltpu.VMEM((N_HEADS, HEAD_DIM, HEAD_DIM), F32)],
        compiler_params=_params(("arbitrary",)),
    )(proj, cos2, sin2, decay, xi, zeta, g_rows, ret_gain)


def _retention_bwd(proj, o_pre, states, d_mix, ret_gain, consts):
    l = proj.shape[0]
    n_chunks = l // CHUNK
    cos2, sin2, decay, xi, zeta, g_rows = consts
    scale = HEAD_DIM ** -0.5
    rev = lambda c: n_chunks - 1 - c

    def body(p_ref, o_ref, st_ref, dm_ref, cos_ref, sin_ref, dec_ref, dect_ref, xi_ref, zeta_ref, gr_ref, gain_ref,
             dp_ref, dgain_ref, dstate):
        step = pl.program_id(0)

        @pl.when(step == 0)
        def _():
            dstate[...] = jnp.zeros_like(dstate)
            dgain_ref[...] = jnp.zeros_like(dgain_ref)

        cos_v, sin_v = cos_ref[...], sin_ref[...]
        valid = _row_valid(rev(step), CHUNK)
        for h in range(N_HEADS):
            cols = slice(h * HEAD_DIM, (h + 1) * HEAD_DIM)
            q = p_ref[:, h * HEAD_DIM:(h + 1) * HEAD_DIM]
            k = p_ref[:, GROUP + h * HEAD_DIM:GROUP + (h + 1) * HEAD_DIM]
            v = p_ref[:, 2 * GROUP + h * HEAD_DIM:2 * GROUP + (h + 1) * HEAD_DIM]
            g = p_ref[:, 3 * GROUP + h * HEAD_DIM:3 * GROUP + (h + 1) * HEAD_DIM]
            o = o_ref[:, cols]
            gain = gain_ref[:, cols]
            d_ret = jnp.where(valid, dm_ref[:, cols], 0.0)
            mu = jnp.mean(o, axis=-1, keepdims=True)
            oc = o - mu
            rstd = lax.rsqrt(jnp.mean(oc * oc, axis=-1, keepdims=True) + NORM_EPS)
            yn = oc * rstd
            sig = _sigmoid(g)
            gate = g * sig
            dgain_ref[:, cols] += jnp.sum(d_ret * gate * yn, axis=0, keepdims=True)
            d_g = d_ret * (yn * gain) * (sig * (1.0 + g * (1.0 - sig)))
            d_yn = d_ret * gate * gain
            d_o = rstd * (d_yn - jnp.mean(d_yn, axis=-1, keepdims=True)
                          - yn * jnp.mean(d_yn * yn, axis=-1, keepdims=True))
            rq = _rope(q, cos_v, sin_v)
            rk = _rope(k, cos_v, sin_v) * scale
            rqb, rkb, vb = rq.astype(MXU_DTYPE), rk.astype(MXU_DTYPE), v.astype(MXU_DTYPE)
            dob = d_o.astype(MXU_DTYPE)
            dec = dec_ref[h]
            xi_h, zeta_h = xi_ref[h], zeta_ref[h]
            st_b = st_ref[h].astype(MXU_DTYPE)
            dst = dstate[h]
            dst_b = dst.astype(MXU_DTYPE)
            dec_t = dect_ref[h]
            s_t_b = (_dot_nt(rkb, rqb) * dec_t).astype(MXU_DTYPE)
            da_b = (_dot_nt(dob, vb) * dec).astype(MXU_DTYPE)
            da_t_b = (_dot_nt(vb, dob) * dec_t).astype(MXU_DTYPE)
            doxi_b = (d_o * xi_h).astype(MXU_DTYPE)
            kz_b = (rk * zeta_h).astype(MXU_DTYPE)
            d_rq = _dot(da_b, rkb) + _dot_nt(doxi_b, st_b)
            d_rk = _dot(da_t_b, rqb) + _dot_nt(vb, dst_b) * zeta_h
            d_v = _dot(s_t_b, dob) + _dot(kz_b, dst_b)
            dstate[h] = gr_ref[h, 0:1, :] * dst + _dot_tn(rqb, doxi_b)
            d_q = _rope_bwd(d_rq, cos_v, sin_v)
            d_k = _rope_bwd(d_rk * scale, cos_v, sin_v)
            dp_ref[:, h * HEAD_DIM:(h + 1) * HEAD_DIM] = d_q.astype(dp_ref.dtype)
            dp_ref[:, GROUP + h * HEAD_DIM:GROUP + (h + 1) * HEAD_DIM] = d_k.astype(dp_ref.dtype)
            dp_ref[:, 2 * GROUP + h * HEAD_DIM:2 * GROUP + (h + 1) * HEAD_DIM] = d_v.astype(dp_ref.dtype)
            dp_ref[:, 3 * GROUP + h * HEAD_DIM:3 * GROUP + (h + 1) * HEAD_DIM] = d_g.astype(dp_ref.dtype)

    head_tab = pl.BlockSpec((N_HEADS, CHUNK, HEAD_DIM), lambda c: (0, 0, 0))
    return pl.pallas_call(
        body, name="retention_bwd",
        out_shape=(jax.ShapeDtypeStruct((l, 4 * GROUP), MXU_DTYPE), jax.ShapeDtypeStruct((1, GROUP), F32)),
        grid=(n_chunks,),
        in_specs=[pl.BlockSpec((CHUNK, 4 * GROUP), lambda c: (rev(c), 0)),
                  pl.BlockSpec((CHUNK, GROUP), lambda c: (rev(c), 0)),
                  pl.BlockSpec((None, N_HEADS, HEAD_DIM, HEAD_DIM), lambda c: (rev(c), 0, 0, 0)),
                  pl.BlockSpec((CHUNK, GROUP), lambda c: (rev(c), 0)),
                  pl.BlockSpec((CHUNK, HEAD_DIM), lambda c: (rev(c), 0)),
                  pl.BlockSpec((CHUNK, HEAD_DIM), lambda c: (rev(c), 0)),
                  head_tab, head_tab, head_tab, head_tab,
                  pl.BlockSpec((N_HEADS, 8, HEAD_DIM), lambda c: (0, 0, 0)),
                  pl.BlockSpec((1, GROUP), lambda c: (0, 0))],
        out_specs=(pl.BlockSpec((CHUNK, 4 * GROUP), lambda c: (rev(c), 0)),
                   pl.BlockSpec((1, GROUP), lambda c: (0, 0))),
        scratch_shapes=[pltpu.VMEM((N_HEADS, HEAD_DIM, HEAD_DIM), F32)],
        compiler_params=_params(("arbitrary",)),
    )(proj, o_pre, states, d_mix, cos2, sin2, decay, jnp.transpose(decay, (0, 2, 1)), xi, zeta, g_rows, ret_gain)


FF_TILE = (7 * GROUP) // 128


def _log_forget(ff, bias_row, valid):
    x = ff + bias_row
    e = jnp.exp(-jnp.abs(x))
    lf = jnp.minimum(x, 0.0) - jnp.log(1.0 + e)
    head_lane = lax.broadcasted_iota(jnp.int32, x.shape, 1) < N_HEADS
    keep = lambda t: jnp.where(head_lane, jnp.where(valid, t, 0.0), 0.0)
    return keep(lf), keep(jnp.where(x >= 0, e, 1.0) / (1.0 + e))


def _fox_prep(proj, bias_row):
    l = proj.shape[0]
    n_blocks = l // CHUNK

    def body(ff_ref, b_ref, bc_ref, rows_ref, cum):
        r = lax.broadcasted_iota(jnp.int32, (CHUNK, CHUNK), 0)
        cidx = lax.broadcasted_iota(jnp.int32, (CHUNK, CHUNK), 1)
        tri = jnp.where(r >= cidx, 1.0, 0.0).astype(F32)
        carry = jnp.zeros((1, 128), F32)
        for blk in range(n_blocks):
            rows = slice(blk * CHUNK, (blk + 1) * CHUNK)
            valid = _row_valid(blk, CHUNK)
            lf, _ = _log_forget(ff_ref[rows, :], b_ref[...], valid)
            local = jnp.dot(tri, lf, precision=lax.Precision.HIGHEST, preferred_element_type=F32) + carry
            carry = local[CHUNK - 1:CHUNK, :]
            masked = jnp.where(valid, local, -NEG_BIG)
            cum[rows, :] = masked
            t = masked.T
            for h in range(N_HEADS):
                rows_ref[h, :, rows] = t[h:h + 1, :]
        full = cum[...]
        for h in range(N_HEADS):
            bc_ref[h] = jnp.broadcast_to(full[:, h:h + 1], (l, 128))

    return pl.pallas_call(
        body, name="fox_prep",
        out_shape=(jax.ShapeDtypeStruct((N_HEADS, l, 128), F32), jax.ShapeDtypeStruct((N_HEADS, 1, l), F32)),
        grid=(1,),
        in_specs=[pl.BlockSpec((l, 128), lambda i: (0, FF_TILE)), pl.BlockSpec((1, 128), lambda i: (0, 0))],
        out_specs=(pl.BlockSpec((N_HEADS, l, 128), lambda i: (0, 0, 0)),
                   pl.BlockSpec((N_HEADS, 1, l), lambda i: (0, 0, 0))),
        scratch_shapes=[pltpu.VMEM((l, 128), F32)],
        compiler_params=_params(("arbitrary",)),
    )(proj, bias_row)


ATTN_BLOCK = 2 * CHUNK


def _attn_blocks(l):
    assert (l - CHUNK) % ATTN_BLOCK == 0
    return [(0, CHUNK)] + [(s, ATTN_BLOCK) for s in range(CHUNK, l, ATTN_BLOCK)]


def _rows_valid(start, size):
    return start + lax.broadcasted_iota(jnp.int32, (size, 1), 0) >= PAD_ROWS


def _fox_fwd(proj, cum_bc, cum_rows, mix):
    l = proj.shape[0]
    blocks = _attn_blocks(l)
    scale = HEAD_DIM ** -0.5
    qt, kt, vt = 4 * N_HEADS, 5 * N_HEADS, 6 * N_HEADS

    def body(q_ref, k_ref, v_ref, cbc_ref, crow_ref, mix_in, o_ref, lse_ref, qb_s, kb_s, vb_s):
        qb_s[...] = q_ref[...].astype(MXU_DTYPE)
        kb_s[...] = k_ref[...].astype(MXU_DTYPE)
        vb_s[...] = v_ref[...].astype(MXU_DTYPE)
        for p, (qs, qn) in enumerate(blocks):
            qb = qb_s[qs:qs + qn, :]
            cq = cbc_ref[qs:qs + qn, :]
            m = jnp.full((qn, 1), NEG_BIG, F32)
            lsum = jnp.zeros((qn, 1), F32)
            acc = jnp.zeros((qn, HEAD_DIM), F32)
            for j in range(p + 1):
                ks, kn = blocks[j]
                bias = jnp.tile(cq, (1, kn // CHUNK)) - crow_ref[:, ks:ks + kn]
                s = _dot_nt(qb, kb_s[ks:ks + kn, :]) * scale + bias
                if j == p:
                    q_pos = qs + lax.broadcasted_iota(jnp.int32, (qn, kn), 0)
                    k_pos = ks + lax.broadcasted_iota(jnp.int32, (qn, kn), 1)
                    s = jnp.where(k_pos <= q_pos, s, NEG_BIG)
                m_new = jnp.maximum(m, jnp.max(s, axis=-1, keepdims=True))
                alpha = jnp.exp(m - m_new)
                pr = jnp.exp(s - m_new)
                lsum = lsum * alpha + jnp.sum(pr, axis=-1, keepdims=True)
                acc = acc * alpha + _dot(pr.astype(MXU_DTYPE), vb_s[ks:ks + kn, :])
                m = m_new
            o = jnp.where(_rows_valid(qs, qn), acc * (1.0 / lsum), 0.0)
            o_ref[qs:qs + qn, :] = o.astype(o_ref.dtype)
            lse = m + jnp.log(lsum)
            lse_ref[:, qs:qs + qn] = jnp.broadcast_to(lse, (qn, CHUNK)).T[0:1, :]

    head_col = lambda t: pl.BlockSpec((l, HEAD_DIM), lambda h: (0, t + h))
    return pl.pallas_call(
        body, name="fox_fwd",
        out_shape=(jax.ShapeDtypeStruct(mix.shape, mix.dtype), jax.ShapeDtypeStruct((N_HEADS, 1, l), F32)),
        grid=(N_HEADS,),
        in_specs=[head_col(qt), head_col(kt), head_col(vt),
                  pl.BlockSpec((None, l, 128), lambda h: (h, 0, 0)),
                  pl.BlockSpec((None, 1, l), lambda h: (h, 0, 0)),
                  ANY],
        out_specs=(head_col(N_HEADS), pl.BlockSpec((None, 1, l), lambda h: (h, 0, 0))),
        input_output_aliases={5: 0},
        scratch_shapes=[pltpu.VMEM((l, HEAD_DIM), MXU_DTYPE)] * 3,
        compiler_params=_params(("parallel",)),
    )(proj, proj, proj, cum_bc, cum_rows, mix)


def _fox_bwd(proj, cum_bc, cum_rows, d_mix, lse_rows):
    l = proj.shape[0]
    blocks = _attn_blocks(l)
    scale = HEAD_DIM ** -0.5
    qt, kt, vt = 4 * N_HEADS, 5 * N_HEADS, 6 * N_HEADS

    def body(q_ref, k_ref, v_ref, do_ref, cbc_ref, crow_ref, lse_ref,
             dq_ref, dk_ref, dv_ref, ds_ref, dk_acc, dv_acc, qb_s, kb_s, vb_s, dob_s, p_s, dp_s):
        qb_s[...] = q_ref[...].astype(MXU_DTYPE)
        kb_s[...] = k_ref[...].astype(MXU_DTYPE)
        vb_s[...] = v_ref[...].astype(MXU_DTYPE)
        dob_s[...] = jnp.where(_rows_valid(0, l), do_ref[...], 0.0).astype(MXU_DTYPE)
        dk_acc[...] = jnp.zeros_like(dk_acc)
        dv_acc[...] = jnp.zeros_like(dv_acc)
        ds_ref[...] = jnp.zeros_like(ds_ref)
        shift_row = crow_ref[...] - lse_ref[...]

        for p, (qs, qn) in enumerate(blocks):
            qb, dob = qb_s[qs:qs + qn, :], dob_s[qs:qs + qn, :]
            shift = shift_row[:, qs:qs + qn]

            delta = jnp.zeros((1, qn), F32)
            for j in range(p + 1):
                ks, kn = blocks[j]
                ck = jnp.tile(cbc_ref[ks:ks + kn, :], (1, qn // CHUNK))
                s_t = _dot_nt(kb_s[ks:ks + kn, :], qb) * scale + (shift - ck)
                if j == p:
                    k_pos = ks + lax.broadcasted_iota(jnp.int32, (kn, qn), 0)
                    q_pos = qs + lax.broadcasted_iota(jnp.int32, (kn, qn), 1)
                    s_t = jnp.where(k_pos <= q_pos, s_t, NEG_BIG)
                p_t, dp_t = jnp.exp(s_t), _dot_nt(vb_s[ks:ks + kn, :], dob)
                p_s[j, 0:kn, 0:qn] = p_t
                dp_s[j, 0:kn, 0:qn] = dp_t
                delta = delta + jnp.sum(p_t * dp_t, axis=0, keepdims=True)
            dq = jnp.zeros((qn, HEAD_DIM), F32)
            for j in range(p + 1):
                ks, kn = blocks[j]
                rows = slice(ks, ks + kn)
                p_t, dp_t = p_s[j, 0:kn, 0:qn], dp_s[j, 0:kn, 0:qn]
                ds_t = p_t * (dp_t - delta)
                ds_b = ds_t.astype(MXU_DTYPE)
                dv_acc[rows, :] += _dot(p_t.astype(MXU_DTYPE), dob)
                dk_acc[rows, :] += _dot(ds_b, qb) * scale
                ds_ref[rows, :] += sum(ds_t[:, c:c + CHUNK] for c in range(0, qn, CHUNK))
                dq = dq + _dot_tn(ds_b, kb_s[rows, :])
            dq_ref[qs:qs + qn, :] = (dq * scale).astype(dq_ref.dtype)

        dk_ref[...] = dk_acc[...].astype(dk_ref.dtype)
        dv_ref[...] = dv_acc[...].astype(dv_ref.dtype)

    col = jax.ShapeDtypeStruct((l, GROUP), MXU_DTYPE)
    head_col = lambda t: pl.BlockSpec((l, HEAD_DIM), lambda h: (0, t + h))
    return pl.pallas_call(
        body, name="fox_bwd",
        out_shape=(col, col, col, jax.ShapeDtypeStruct((N_HEADS, l, 128), F32)),
        grid=(N_HEADS,),
        in_specs=[head_col(qt), head_col(kt), head_col(vt), head_col(N_HEADS),
                  pl.BlockSpec((None, l, 128), lambda h: (h, 0, 0)),
                  pl.BlockSpec((None, 1, l), lambda h: (h, 0, 0)),
                  pl.BlockSpec((None, 1, l), lambda h: (h, 0, 0))],
        out_specs=(head_col(0), head_col(0), head_col(0), pl.BlockSpec((None, l, 128), lambda h: (h, 0, 0))),
        scratch_shapes=([pltpu.VMEM((l, HEAD_DIM), F32)] * 2 + [pltpu.VMEM((l, HEAD_DIM), MXU_DTYPE)] * 4
                        + [pltpu.VMEM((len(blocks), ATTN_BLOCK, ATTN_BLOCK), F32)] * 2),
        compiler_params=_params(("parallel",)),
    )(proj, proj, proj, d_mix, cum_bc, cum_rows, lse_rows)


def _fox_gate_bwd(ds_sum, proj, bias_row):
    l = proj.shape[0]
    n_blocks = l // CHUNK

    def body(ds_ref, ff_ref, b_ref, dff_ref, db_ref):
        r = lax.broadcasted_iota(jnp.int32, (CHUNK, CHUNK), 0)
        cidx = lax.broadcasted_iota(jnp.int32, (CHUNK, CHUNK), 1)
        upper = jnp.where(cidx >= r, 1.0, 0.0).astype(F32)
        carry = jnp.zeros((1, 128), F32)
        db = jnp.zeros((1, 128), F32)
        for blk in reversed(range(n_blocks)):
            rows = slice(blk * CHUNK, (blk + 1) * CHUNK)
            key_sum = jnp.zeros((CHUNK, 128), F32)
            for h in range(N_HEADS):
                select = jnp.where(cidx == h, 1.0, 0.0).astype(F32)
                key_sum = key_sum + jnp.dot(ds_ref[h, rows, :], select, precision=lax.Precision.HIGHEST,
                                            preferred_element_type=F32)
            suffix = jnp.dot(upper, key_sum, precision=lax.Precision.HIGHEST, preferred_element_type=F32) + carry
            carry = suffix[0:1, :]
            _, dsig = _log_forget(ff_ref[rows, :], b_ref[...], _row_valid(blk, CHUNK))
            dff = -suffix * dsig
            dff_ref[rows, :] = dff.astype(dff_ref.dtype)
            db = db + jnp.sum(dff, axis=0, keepdims=True)
        db_ref[...] = db

    return pl.pallas_call(
        body, name="fox_gate_bwd",
        out_shape=(jax.ShapeDtypeStruct((l, 128), MXU_DTYPE), jax.ShapeDtypeStruct((1, 128), F32)),
        grid=(1,),
        in_specs=[pl.BlockSpec((N_HEADS, l, 128), lambda i: (0, 0, 0)),
                  pl.BlockSpec((l, 128), lambda i: (0, FF_TILE)),
                  pl.BlockSpec((1, 128), lambda i: (0, 0))],
        out_specs=(pl.BlockSpec((l, 128), lambda i: (0, 0)), pl.BlockSpec((1, 128), lambda i: (0, 0))),
        compiler_params=_params(("arbitrary",)),
    )(ds_sum, proj, bias_row)


def _conv(u, w, b):
    return b + w[0:1, :] * pltpu.roll(u, 2, 0) + w[1:2, :] * pltpu.roll(u, 1, 0) + w[2:3, :] * u


def _conv_act_fwd(u, conv_w, conv_b, d_ff):
    l = u.shape[0]
    tc = _divisor_tile(d_ff, 256, 128)
    nt = d_ff // tc

    def body(ug_ref, uv_ref, wg_ref, wv_ref, bg_ref, bv_ref, a_ref, y_ref):
        yg = _conv(ug_ref[...], wg_ref[...], bg_ref[...])
        yv = _conv(uv_ref[...], wv_ref[...], bv_ref[...])
        act = yg * _sigmoid(yg) * yv
        a_ref[...] = jnp.where(_row_valid(0, l), act, 0.0).astype(a_ref.dtype)
        y_ref[0] = yg.astype(y_ref.dtype)
        y_ref[1] = yv.astype(y_ref.dtype)

    return pl.pallas_call(
        body, name="conv_act_fwd",
        out_shape=(jax.ShapeDtypeStruct((l, d_ff), MXU_DTYPE), jax.ShapeDtypeStruct((2, l, d_ff), MXU_DTYPE)),
        grid=(nt,),
        in_specs=[pl.BlockSpec((l, tc), lambda j: (0, j)), pl.BlockSpec((l, tc), lambda j: (0, j + nt)),
                  pl.BlockSpec((8, tc), lambda j: (0, j)), pl.BlockSpec((8, tc), lambda j: (0, j + nt)),
                  pl.BlockSpec((1, tc), lambda j: (0, j)), pl.BlockSpec((1, tc), lambda j: (0, j + nt))],
        out_specs=(pl.BlockSpec((l, tc), lambda j: (0, j)), pl.BlockSpec((2, l, tc), lambda j: (0, 0, j))),
        compiler_params=_params(("parallel",)),
    )(u, u, conv_w, conv_w, conv_b, conv_b)


def _conv_act_bwd(u, y, conv_w, d_act, d_ff):
    l = u.shape[0]
    tc = _divisor_tile(d_ff, 256, 128)
    nt = d_ff // tc

    def body(ug_ref, uv_ref, y_ref, wg_ref, wv_ref, da_ref, du_ref, dwb_ref):
        valid = _row_valid(0, l)
        ug, uv = ug_ref[...], uv_ref[...]
        wg, wv = wg_ref[...], wv_ref[...]
        yg, yv = y_ref[0].astype(F32), y_ref[1].astype(F32)
        sig = _sigmoid(yg)
        da = jnp.where(valid, da_ref[...], 0.0)
        d_yv = da * (yg * sig)
        d_yg = da * yv * (sig * (1.0 + yg * (1.0 - sig)))
        for idx, (dy, uu, w) in enumerate(((d_yg, ug, wg), (d_yv, uv, wv))):
            du = w[2:3, :] * dy + w[1:2, :] * pltpu.roll(dy, l - 1, 0) + w[0:1, :] * pltpu.roll(dy, l - 2, 0)
            du_ref[idx] = jnp.where(valid, du, 0.0).astype(du_ref.dtype)
            dwb_ref[idx, 0:1, :] = jnp.sum(dy * pltpu.roll(uu, 2, 0), axis=0, keepdims=True)
            dwb_ref[idx, 1:2, :] = jnp.sum(dy * pltpu.roll(uu, 1, 0), axis=0, keepdims=True)
            dwb_ref[idx, 2:3, :] = jnp.sum(dy * uu, axis=0, keepdims=True)
            dwb_ref[idx, 3:4, :] = jnp.sum(dy, axis=0, keepdims=True)
            dwb_ref[idx, 4:8, :] = jnp.zeros((4, tc), F32)

    return pl.pallas_call(
        body, name="conv_act_bwd",
        out_shape=(jax.ShapeDtypeStruct((2, l, d_ff), MXU_DTYPE), jax.ShapeDtypeStruct((2, 8, d_ff), F32)),
        grid=(nt,),
        in_specs=[pl.BlockSpec((l, tc), lambda j: (0, j)), pl.BlockSpec((l, tc), lambda j: (0, j + nt)),
                  pl.BlockSpec((2, l, tc), lambda j: (0, 0, j)),
                  pl.BlockSpec((8, tc), lambda j: (0, j)), pl.BlockSpec((8, tc), lambda j: (0, j + nt)),
                  pl.BlockSpec((l, tc), lambda j: (0, j))],
        out_specs=(pl.BlockSpec((2, l, tc), lambda j: (0, 0, j)), pl.BlockSpec((2, 8, tc), lambda j: (0, 0, j))),
        compiler_params=_params(("parallel",)),
    )(u, u, y, conv_w, conv_w, d_act)


def _adamw(w, g, m, v, name):
    shape = w.shape
    if w.ndim == 1:
        as2d = (1, shape[0])
    else:
        as2d = (int(np.prod(shape[:-1])), shape[-1])
    r, c = as2d
    tr = _divisor_tile(r, 256, 8)
    spec = pl.BlockSpec((tr, c), lambda i: (i, 0))

    def body(w_ref, g_ref, m_ref, v_ref, d_ref, nm_ref, nv_ref):
        d_ref[...], nm_ref[...], nv_ref[...] = _adamw_math(w_ref[...], g_ref[...], m_ref[...], v_ref[...])

    sds = jax.ShapeDtypeStruct(as2d, F32)
    outs = pl.pallas_call(
        body, name=name, out_shape=(sds, sds, sds), grid=(r // tr,),
        in_specs=[spec] * 4, out_specs=(spec,) * 3,
        compiler_params=_params(("parallel",)),
    )(w.reshape(as2d), g.reshape(as2d), m.reshape(as2d), v.reshape(as2d))
    return tuple(o.reshape(shape) for o in outs)


def _pad_rows(a, rows):
    return jnp.pad(a, ((0, rows - a.shape[0]), (0, 0)))


def kernel(x, meta_tokens, norm1_gain, w_in, b_forget, ret_norm_gain, w_out, norm2_gain, w_up, conv_w, conv_b, w_down, final_norm_gain, loss_target, m_meta_tokens, m_norm1_gain, m_w_in, m_b_forget, m_ret_norm_gain, m_w_out, m_norm2_gain, m_w_up, m_conv_w, m_conv_b, m_w_down, m_final_norm_gain, v_meta_tokens, v_norm1_gain, v_w_in, v_b_forget, v_ret_norm_gain, v_w_out, v_norm2_gain, v_w_up, v_conv_w, v_conv_b, v_w_down, v_final_norm_gain):
    seq, d = x.shape[1], x.shape[2]
    l = CHUNK + seq
    d_ff = w_down.shape[1] * N_DEV
    up_shard = w_up.shape[2]
    assert 4 * up_shard == d_ff and w_in.shape[2] == WIN_SHARD and d == 2 * GROUP
    dev = _device_index()
    mx, my, mc = _my_position()
    core = jnp.reshape(mc, (1,)).astype(jnp.int32)
    chip = jnp.reshape(2 * mx + my, (1,)).astype(jnp.int32)
    dev1 = jnp.reshape(dev, (1,)).astype(jnp.int32)

    small = jnp.concatenate([meta_tokens.reshape(-1, 128), conv_w[0].reshape(-1, 128)], axis=0)
    n_meta_rows = N_META * (d // N_DEV) // 128
    small_rows = small.shape[0]
    small_all = _all_gather(_pad_rows(small, -(-small_rows // 8) * 8), "gather_small")
    meta_full = jnp.transpose(small_all[:, :n_meta_rows].reshape(N_DEV, N_META, d // N_DEV), (1, 0, 2)).reshape(N_META, d)
    conv_w_full = _pad_rows(jnp.transpose(small_all[:, n_meta_rows:small_rows].reshape(N_DEV, 3, up_shard),
                                          (1, 0, 2)).reshape(3, 2 * d_ff), 8)
    to_rows = lambda t: jnp.pad(jnp.transpose(t[0]), ((0, WIN_ROWS - WIN_SHARD), (0, 0)))
    from_rows = lambda t: jnp.transpose(t[:WIN_SHARD])[None]
    w_in_rows = to_rows(w_in)
    out_rows = d // N_DEV
    mixer_rows = -(-(WIN_ROWS + out_rows) // 32) * 32
    mixer_shard = jnp.concatenate([w_in_rows.astype(WIRE_DTYPE), w_out[0].astype(WIRE_DTYPE),
                                   jnp.zeros((mixer_rows - WIN_ROWS - out_rows, d), WIRE_DTYPE)], axis=0)

    consts = _retention_consts(l)
    bias_row = jnp.pad(b_forget, ((0, 0), (0, 128 - N_HEADS)))
    h0, a = _embed_rmsnorm(x[0], meta_full, norm1_gain)
    mixer_blocks = _gather_ring(mixer_shard, dev1, a, "gather_w_in")
    start_up = _ring_start(w_up[0], dev1, mixer_blocks, "gather_w_up_start")
    w_in_full = _assemble_w_in(mixer_blocks).astype(MXU_DTYPE)
    proj = _mm_nt(a, w_in_full, F32, "mm_proj", after=start_up[4])
    ret_mix, ret_pre, ret_states = _retention_fwd(proj, ret_norm_gain, consts)
    forward_up = _ring_forward(start_up, ret_pre, "gather_w_up")
    cum_bc, cum_rows = _fox_prep(proj, bias_row + forward_up[4][0:1, :])
    mix, lse_rows = _fox_fwd(proj, cum_bc, cum_rows, ret_mix)
    w_out_full = mixer_blocks[:, WIN_ROWS:WIN_ROWS + out_rows].reshape(d, d).astype(MXU_DTYPE)
    h1, cn = _out_proj_resid_rmsnorm(mix, w_out_full, h0, norm2_gain)
    w_up_blocks = _ring_finish(forward_up, cn, "gather_w_up").astype(MXU_DTYPE)
    start_down = _gather_start(w_down[0], dev1, w_up_blocks, "gather_w_down_start")
    u = _mm_up(cn, w_up_blocks, start_down[4])
    pass_down = _gather_pass_start(start_down, u, "gather_w_down")
    act, conv_y = _conv_act_fwd(u, conv_w_full, conv_b + pass_down[4][0, 0], d_ff)
    w_down_full = _gather_pass_finish(pass_down, act, "gather_w_down").reshape(d_ff, d).astype(MXU_DTYPE)
    mlp_out = _mm_nn(act, w_down_full, F32, "mm_down", tm_cap=544, tk_cap=d_ff)
    d_h2, d_h2_b, dg_final, loss_part = _loss_head(h1, mlp_out, final_norm_gain.reshape(1, d), loss_target[0])

    gw_down = _mm_tn(act, d_h2_b, WIRE_DTYPE, "mm_gw_down", tm_cap=1408, tn_cap=1024)
    d2d_down = _reduce_scatter_d2d_start(gw_down.reshape(N_DEV, d_ff // N_DEV, d), d_h2, "rs_w_down")
    d_act = _mm_nt(d_h2_b, w_down_full, F32, "mm_d_act", after=d2d_down[4])
    rs_down = _reduce_scatter_ici_start(d2d_down, d_act, core, "rs_w_down")
    d_u, d_conv = _conv_act_bwd(u, conv_y, conv_w_full + rs_down[4][0, 0], d_act, d_ff)
    tm = _divisor_tile(l, 1088, 16)
    gw_up = _mm_gw_up(cn, d_u)
    d2d_up = _reduce_scatter_d2d_start(gw_up, d_act, "rs_w_up")
    d_cn = _mm_d_cn(d_u, w_up_blocks, d2d_up[4])
    rs_up = _reduce_scatter_ici_start(d2d_up, d_cn, core, "rs_w_up")
    d_h1, d_h1_b, dg_norm2 = _rmsnorm_bwd(d_h2, d_cn, h1, norm2_gain + rs_up[4][0, 0], "rmsnorm2_bwd", True)

    gw_out = _mm_tn(mix, d_h1_b, WIRE_DTYPE, "mm_gw_out")
    d2d_out = _reduce_scatter_d2d_start(gw_out.reshape(N_DEV, d // N_DEV, d), d_cn, "rs_w_out")
    d_mix = _mm_nt(d_h1_b, w_out_full, F32, "mm_d_mix", after=d2d_out[4])
    d_fq, d_fk, d_fv, ds_sum = _fox_bwd(proj, cum_bc, cum_rows, d_mix, lse_rows)
    d_ff_tile, db_forget_row = _fox_gate_bwd(ds_sum, proj, bias_row)
    d_ret, dg_ret = _retention_bwd(proj, ret_pre, ret_states, d_mix, ret_norm_gain, consts)
    rs_out = _reduce_scatter_ici_start(d2d_out, d_ret, core, "rs_w_out")
    d_proj = jnp.concatenate(
        [d_ret, d_fq, d_fk, d_fv, d_ff_tile, jnp.zeros((l, WIN_N - 7 * GROUP - 128), MXU_DTYPE)], axis=1)
    gw_in = _mm_tn(d_proj, a, WIRE_DTYPE, "mm_gw_in", tm_cap=1536, after=rs_out[4])
    rs_in = _reduce_scatter_start(_extract_w_in_windows(gw_in), core, "rs_w_in")
    d_a = _mm_nn(d_proj, w_in_full, F32, "mm_d_a", tm_cap=544, tn_cap=256, tk_cap=WIN_N, after=rs_in[4])
    d_front, d_tokens, dg_norm1 = _rmsnorm_bwd(d_h1, d_a, h0, norm1_gain + rs_in[4][0, 0], "rmsnorm1_bwd", False)
    grad_x = d_tokens[None]
    d_meta = d_front[PAD_ROWS:CHUNK]

    d_conv_w = jnp.concatenate([d_conv[0, 0:3], d_conv[1, 0:3]], axis=1)
    d_conv_b = jnp.concatenate([d_conv[0, 3:4], d_conv[1, 3:4]], axis=1)
    pieces = [loss_part[:, 0:1], dg_norm1, db_forget_row[:, 0:N_HEADS], dg_ret, dg_norm2, d_conv_b, dg_final,
              d_meta.reshape(1, -1), d_conv_w.reshape(1, -1)]
    sizes = [p.shape[1] for p in pieces]
    flat = jnp.concatenate(pieces, axis=1)
    padded = -(-flat.shape[1] // 1024) * 1024
    flat = jnp.pad(flat, ((0, 0), (0, padded - flat.shape[1]))).reshape(padded // 128, 128)
    small_ar = _small_all_reduce_start(flat, d_tokens, "all_reduce_small")

    lead = lambda outs: tuple(o[None] for o in outs)
    fin_down = lead(_reduce_scatter_finish(rs_down, small_ar[4], chip, w_down[0], m_w_down[0], v_w_down[0], "rs_w_down"))
    fin_up = lead(_reduce_scatter_finish(rs_up, fin_down[3], chip, w_up[0], m_w_up[0], v_w_up[0], "rs_w_up"))
    fin_out = lead(_reduce_scatter_finish(rs_out, fin_up[3], chip, w_out[0], m_w_out[0], v_w_out[0], "rs_w_out"))
    fin_in = tuple(from_rows(o) for o in _reduce_scatter_finish(
        rs_in, fin_out[3], chip, w_in_rows, to_rows(m_w_in), to_rows(v_w_in), "rs_w_in"))
    g_w_down, g_w_up, g_w_out, g_w_in = fin_down[0], fin_up[0], fin_out[0], fin_in[0]
    early = [fin_down[1:], fin_up[1:], fin_out[1:], fin_in[1:]]
    total = _small_all_reduce_finish(small_ar, fin_in[3], dev1, "all_reduce_small").reshape(1, padded)
    offs = np.concatenate([[0], np.cumsum(sizes)])
    take = lambda k: total[:, int(offs[k]):int(offs[k + 1])]
    loss = take(0).reshape(())
    g_norm1, g_bf, g_ret_gain, g_norm2 = take(1), take(2), take(3), take(4)
    g_conv_b, g_final = take(5), take(6).reshape(d)
    g_meta = lax.dynamic_slice(take(7).reshape(N_META, d), (jnp.int32(0), (dev * (d // N_DEV)).astype(jnp.int32)),
                               (N_META, d // N_DEV))
    g_conv_w = lax.dynamic_slice(take(8).reshape(3, 2 * d_ff), (jnp.int32(0), (dev * up_shard).astype(jnp.int32)),
                                 (3, up_shard))[None]

    weights = [meta_tokens, norm1_gain, w_in, b_forget, ret_norm_gain, w_out, norm2_gain, w_up, conv_w, conv_b,
               w_down, final_norm_gain]
    grads = [g_meta, g_norm1, g_w_in, g_bf, g_ret_gain, g_w_out, g_norm2, g_w_up, g_conv_w, g_conv_b, g_w_down,
             g_final]
    done = {"w_down": early[0], "w_up": early[1], "w_out": early[2], "w_in": early[3]}
    ms = [m_meta_tokens, m_norm1_gain, m_w_in, m_b_forget, m_ret_norm_gain, m_w_out, m_norm2_gain, m_w_up, m_conv_w,
          m_conv_b, m_w_down, m_final_norm_gain]
    vs = [v_meta_tokens, v_norm1_gain, v_w_in, v_b_forget, v_ret_norm_gain, v_w_out, v_norm2_gain, v_w_up, v_conv_w,
          v_conv_b, v_w_down, v_final_norm_gain]
    names = ["meta", "norm1", "w_in", "b_forget", "ret_gain", "w_out", "norm2", "w_up", "conv_w", "conv_b", "w_down",
             "final_gain"]
    deltas, new_ms, new_vs = [], [], []
    for w, g, m, v, n in zip(weights, grads, ms, vs, names):
        dl, nm, nv = done[n] if n in done else _adamw(w, g, m, v, "adamw_" + n)
        deltas.append(dl)
        new_ms.append(nm)
        new_vs.append(nv)
    return (loss, grad_x, *grads, *deltas, *new_ms, *new_vs)
```

```python
import functools

import numpy as np
import jax
import jax.numpy as jnp
from jax import lax
from jax.experimental import pallas as pl
from jax.experimental.pallas import tpu as pltpu

F32 = jnp.float32
MXU_DTYPE = jnp.bfloat16
WIRE_DTYPE = jnp.bfloat16

N_DEV = 8
N_META = 16
CHUNK = 128
PAD_ROWS = CHUNK - N_META
N_HEADS = 8
HEAD_DIM = 128
GROUP = N_HEADS * HEAD_DIM
IN_DIM = 7 * GROUP + N_HEADS
WIN_SHARD = IN_DIM // N_DEV
WIN_ROWS = 912
WIN_BLOCK = 1024
WIN_STRIDE = 896
WIN_N = 7680
ROPE_BASE = 10000.0
NORM_EPS = 1e-6
NEG_BIG = -1e30
ADAM_LR, ADAM_B1, ADAM_B2, ADAM_EPS, ADAM_WD, ADAM_STEP = 0.001, 0.9, 0.999, 1e-08, 0.01, 10
VMEM_LIMIT = 52 * 1024 * 1024
MESH = pl.DeviceIdType.MESH
ANY = pl.BlockSpec(memory_space=pl.ANY)
VMEM_SPEC = pl.BlockSpec(memory_space=pltpu.VMEM)


def _params(sem=None):
    kw = {"vmem_limit_bytes": VMEM_LIMIT}
    if sem is not None:
        kw["dimension_semantics"] = sem
    return pltpu.CompilerParams(**kw)


def _divisor_tile(n, cap, unit):
    if n <= cap:
        return n
    best = None
    for t in range(unit, cap + 1, unit):
        if n % t == 0:
            best = t
    assert best is not None, (n, cap, unit)
    return best


def _my_position():
    return lax.axis_index("x"), lax.axis_index("y"), lax.axis_index("c")


def _device_index():
    x, y, c = _my_position()
    return 4 * x + 2 * y + c


def _all_gather(shard, name):
    r, c = shard.shape

    def body(x_ref, out_ref, send_sems, recv_sems, local_sem):
        mx, my, mc = _my_position()
        me, sibling = (mx, my, mc), (mx, my, 1 - mc)
        chips = [(1 - mx, my), (mx, 1 - my), (1 - mx, 1 - my)]

        def slot(px, py, pc):
            return out_ref.at[4 * px + 2 * py + pc]

        def copy(k, block, to, src=None):
            return pltpu.make_async_remote_copy(
                src_ref=slot(*block) if src is None else src, dst_ref=slot(*block),
                send_sem=send_sems.at[k], recv_sem=recv_sems.at[k], device_id=to, device_id_type=MESH)

        mine = pltpu.make_async_copy(x_ref, slot(*me), local_sem)
        mine.start()
        first = [copy(0, me, sibling, src=x_ref)]
        first += [copy(1 + j, me, (*chip, mc), src=x_ref) for j, chip in enumerate(chips)]
        for cp in first:
            cp.start()
        passed = [copy(4 + j, (*chip, mc), sibling) for j, chip in enumerate(chips)]
        for j, chip in enumerate(chips):
            copy(1 + j, (*chip, mc), me).wait_recv()
            passed[j].start()
        copy(0, sibling, me).wait_recv()
        for j, chip in enumerate(chips):
            copy(4 + j, (*chip, 1 - mc), me).wait_recv()
        for cp in first + passed:
            cp.wait_send()
        mine.wait()

    return pl.pallas_call(
        body, name=name,
        out_shape=jax.ShapeDtypeStruct((N_DEV, r, c), shard.dtype),
        in_specs=[ANY], out_specs=ANY,
        scratch_shapes=[pltpu.SemaphoreType.DMA((7,)), pltpu.SemaphoreType.DMA((7,)), pltpu.SemaphoreType.DMA],
    )(shard)


HBM_SPEC = pl.BlockSpec(memory_space=pltpu.HBM)
SEM_SPEC = pl.BlockSpec(memory_space=pltpu.SEMAPHORE)
DATAFLOW_EFFECT = pltpu.SideEffectType.DATAFLOW_SIDE_EFFECTING


def _in_hbm(a):
    return pltpu.with_memory_space_constraint(a, pltpu.HBM)


def _split_start(src, land, make_copies, n_copies, after, name):
    if isinstance(land, tuple):
        land = lax.empty(land, src.dtype)
    land_shape = land.shape
    def body(src_ref, land_ref, after_ref, send_sems, recv_sems, src_thru, land_thru, token):
        for cp in make_copies(src_ref, land_ref, send_sems, recv_sems):
            cp.start()
        token[...] = jnp.zeros_like(token)

    return pl.pallas_call(
        body, name=name,
        out_shape=(pltpu.SemaphoreType.DMA((n_copies,)), pltpu.SemaphoreType.DMA((n_copies,)),
                   pltpu.HBM(src.shape, src.dtype), pltpu.HBM(land_shape, land.dtype),
                   jax.ShapeDtypeStruct((8, 128), F32)),
        in_specs=(HBM_SPEC, HBM_SPEC, ANY), out_specs=(SEM_SPEC, SEM_SPEC, HBM_SPEC, HBM_SPEC, VMEM_SPEC),
        input_output_aliases={0: 2, 1: 3},
        compiler_params=pltpu.CompilerParams(has_side_effects=DATAFLOW_EFFECT),
    )(_in_hbm(src), _in_hbm(land), after)


def _split_wait(started, after, make_copies, name):
    send_sems, recv_sems, src_thru, land_thru, _ = started

    def body(src_ref, land_ref, send_sems_ref, recv_sems_ref, after_ref, src_dead, land_out):
        for cp in make_copies(src_ref, land_ref, send_sems_ref, recv_sems_ref):
            cp.wait_send()
            cp.wait_recv()

    return pl.pallas_call(
        body, name=name,
        out_shape=(pltpu.HBM(src_thru.shape, src_thru.dtype), pltpu.HBM(land_thru.shape, land_thru.dtype)),
        in_specs=(HBM_SPEC, HBM_SPEC, SEM_SPEC, SEM_SPEC, ANY), out_specs=(HBM_SPEC, HBM_SPEC),
        input_output_aliases={0: 0, 1: 1},
        compiler_params=pltpu.CompilerParams(has_side_effects=DATAFLOW_EFFECT),
    )(src_thru, land_thru, send_sems, recv_sems, after)


def _gather_copies(x_ref, land_ref, send_sems, recv_sems):
    mx, my, mc = _my_position()
    me = 4 * mx + 2 * my + mc
    targets = [(mx, my, 1 - mc), (1 - mx, my, mc), (mx, 1 - my, mc), (1 - mx, 1 - my, mc)]
    return [pltpu.make_async_remote_copy(
        src_ref=land_ref.at[me], dst_ref=land_ref.at[me], send_sem=send_sems.at[k], recv_sem=recv_sems.at[k],
        device_id=t, device_id_type=MESH) for k, t in enumerate(targets)]


def _own_slot(shard, dev, name):
    r, c = shard.shape
    tr = _divisor_tile(r, 640, 16)

    def body(s_ref, x_ref, o_ref):
        o_ref[...] = x_ref[...].astype(o_ref.dtype)

    return pl.pallas_call(
        body, name=name,
        out_shape=jax.ShapeDtypeStruct((N_DEV, r, c), WIRE_DTYPE),
        grid_spec=pltpu.PrefetchScalarGridSpec(
            num_scalar_prefetch=1, grid=(r // tr,),
            in_specs=[pl.BlockSpec((tr, c), lambda i, s: (i, 0))],
            out_specs=pl.BlockSpec((None, tr, c), lambda i, s: (s[0], i, 0))),
        compiler_params=_params(("parallel",)),
    )(dev, shard)


def _gather_start(shard, dev, after, name):
    return _split_start(jnp.zeros((8, 128), F32), _own_slot(shard, dev, name + "_own"), _gather_copies, 4, after, name)


def _gather_ring(shard, dev, after, name):
    r, c = shard.shape
    half = r // 2
    assert half % 16 == 0

    def body(x_ref, after_ref, land_in, land_ref, send_sems, recv_sems):
        mx, my, mc = _my_position()
        sibling, x_nbr, y_nbr = (mx, my, 1 - mc), (1 - mx, my, mc), (mx, 1 - my, mc)
        first, second = pl.ds(0, half), pl.ds(half, half)

        def slot(px, py, pc):
            return land_ref.at[4 * px + 2 * py + pc]

        def copy(k, src, dst, to):
            return pltpu.make_async_remote_copy(src_ref=src, dst_ref=dst, send_sem=send_sems.at[k],
                                                recv_sem=recv_sems.at[k], device_id=to, device_id_type=MESH)

        def arrived(k, dst):
            copy(k, dst, dst, sibling).wait_recv()

        mine = slot(mx, my, mc)
        from_x, from_y, from_d = slot(1 - mx, my, mc), slot(mx, 1 - my, mc), slot(1 - mx, 1 - my, mc)
        sent = [copy(0, x_ref, mine, sibling), copy(1, x_ref, mine, x_nbr), copy(2, x_ref, mine, y_nbr)]
        for cp in sent:
            cp.start()

        def send(k, src, to):
            cp = copy(k, src, src, to)
            cp.start()
            sent.append(cp)

        arrived(1, from_x)
        send(3, from_x.at[first], y_nbr)
        send(5, from_x, sibling)
        arrived(2, from_y)
        send(4, from_y.at[second], x_nbr)
        send(6, from_y, sibling)
        arrived(3, from_d.at[first])
        send(7, from_d.at[first], sibling)
        arrived(4, from_d.at[second])
        send(8, from_d.at[second], sibling)
        arrived(0, slot(mx, my, 1 - mc))
        arrived(5, slot(1 - mx, my, 1 - mc))
        arrived(6, slot(mx, 1 - my, 1 - mc))
        arrived(7, slot(1 - mx, 1 - my, 1 - mc).at[first])
        arrived(8, slot(1 - mx, 1 - my, 1 - mc).at[second])
        for cp in sent:
            cp.wait_send()

    land = _own_slot(shard, dev, name + "_own")
    return pl.pallas_call(
        body, name=name,
        out_shape=jax.ShapeDtypeStruct(land.shape, land.dtype),
        in_specs=[ANY, ANY, ANY], out_specs=ANY,
        input_output_aliases={2: 0},
        scratch_shapes=[pltpu.SemaphoreType.DMA((9,)), pltpu.SemaphoreType.DMA((9,))],
    )(shard, after, land)


def _pass_copies(unused_ref, land_ref, send_sems, recv_sems):
    mx, my, mc = _my_position()
    chips = [(1 - mx, my), (mx, 1 - my), (1 - mx, 1 - my)]
    return [pltpu.make_async_remote_copy(
        src_ref=land_ref.at[4 * cx + 2 * cy + mc], dst_ref=land_ref.at[4 * cx + 2 * cy + mc],
        send_sem=send_sems.at[j], recv_sem=recv_sems.at[j],
        device_id=(mx, my, 1 - mc), device_id_type=MESH) for j, (cx, cy) in enumerate(chips)]


def _gather_pass_start(started, after, name):
    _, land = _split_wait(started, after, _gather_copies, name + "_wait")
    return _split_start(jnp.zeros((8, 128), F32), land, _pass_copies, 3, after, name + "_pass_start")


def _gather_pass_finish(pass_started, after, name):
    return _split_wait(pass_started, after, _pass_copies, name + "_pass_wait")[1]


def _gather_finish(started, after, name):
    _, land = _split_wait(started, after, _gather_copies, name + "_wait")

    def body(land_in, land_ref, send_sems, recv_sems):
        mx, my, mc = _my_position()
        chips = [(1 - mx, my), (mx, 1 - my), (1 - mx, 1 - my)]
        copies = [pltpu.make_async_remote_copy(
            src_ref=land_ref.at[4 * cx + 2 * cy + mc], dst_ref=land_ref.at[4 * cx + 2 * cy + mc],
            send_sem=send_sems.at[j], recv_sem=recv_sems.at[j],
            device_id=(mx, my, 1 - mc), device_id_type=MESH) for j, (cx, cy) in enumerate(chips)]
        for cp in copies:
            cp.start()
        for j, (cx, cy) in enumerate(chips):
            copies[j].wait_send()
            pltpu.make_async_remote_copy(
                src_ref=land_ref.at[4 * cx + 2 * cy + 1 - mc], dst_ref=land_ref.at[4 * cx + 2 * cy + 1 - mc],
                send_sem=send_sems.at[j], recv_sem=recv_sems.at[j],
                device_id=(mx, my, 1 - mc), device_id_type=MESH).wait_recv()

    return pl.pallas_call(
        body, name=name + "_pass",
        out_shape=jax.ShapeDtypeStruct(land.shape, land.dtype),
        in_specs=[ANY], out_specs=ANY,
        input_output_aliases={0: 0},
        scratch_shapes=[pltpu.SemaphoreType.DMA((3,)), pltpu.SemaphoreType.DMA((3,))],
    )(land)


def _chip_copies(p_ref, land_ref, send_sems, recv_sems):
    mx, my, mc = _my_position()
    chips = [(1 - mx, my), (mx, 1 - my), (1 - mx, 1 - my)]
    return [pltpu.make_async_remote_copy(
        src_ref=p_ref.at[2 * cx + cy], dst_ref=land_ref.at[j], send_sem=send_sems.at[j], recv_sem=recv_sems.at[j],
        device_id=(cx, cy, mc), device_id_type=MESH) for j, (cx, cy) in enumerate(chips)]


def _reduce_scatter_start(g, core, name):
    pair = _pair_sum(g, _exchange_sibling(g, name + "_d2d"), core, name + "_pairsum")
    return _split_start(pair, (3,) + pair.shape[1:], _chip_copies, 3, g, name + "_ici_start")


def _sibling_copies(g_ref, land_ref, send_sems, recv_sems):
    mx, my, mc = _my_position()
    return [pltpu.make_async_remote_copy(
        src_ref=g_ref.at[2 * k + (1 - mc)], dst_ref=land_ref.at[k], send_sem=send_sems.at[k], recv_sem=recv_sems.at[k],
        device_id=(mx, my, 1 - mc), device_id_type=MESH) for k in range(4)]


def _reduce_scatter_d2d_start(g, after, name):
    return _split_start(g, (4,) + g.shape[1:], _sibling_copies, 4, after, name + "_d2d_start")


def _reduce_scatter_ici_start(d2d_started, after, core, name):
    g, from_sibling = _split_wait(d2d_started, after, _sibling_copies, name + "_d2d_wait")
    pair = _pair_sum(g, from_sibling, core, name + "_pairsum")
    return _split_start(pair, (3,) + pair.shape[1:], _chip_copies, 3, g, name + "_ici_start")


def _reduce_scatter_finish(started, after, chip, w, m, v, name):
    pair, from_chips = _split_wait(started, after, _chip_copies, name + "_ici_wait")
    return _final_sum_adamw(pair, from_chips, chip, w, m, v, name + "_sum_adamw")


def _exchange_sibling(g, name):
    _, r, c = g.shape

    def body(g_ref, out_ref, send_sems, recv_sems):
        mx, my, mc = _my_position()
        copies = [
            pltpu.make_async_remote_copy(
                src_ref=g_ref.at[2 * k + (1 - mc)], dst_ref=out_ref.at[k],
                send_sem=send_sems.at[k], recv_sem=recv_sems.at[k],
                device_id=(mx, my, 1 - mc), device_id_type=MESH)
            for k in range(4)]
        for cp in copies:
            cp.start()
        for cp in copies:
            cp.wait()

    return pl.pallas_call(
        body, name=name,
        out_shape=jax.ShapeDtypeStruct((4, r, c), g.dtype),
        in_specs=[ANY], out_specs=ANY,
        scratch_shapes=[pltpu.SemaphoreType.DMA((4,)), pltpu.SemaphoreType.DMA((4,))],
    )(g)


def _pair_sum(g, recv, core, name):
    _, r, c = g.shape
    tr = _divisor_tile(r, 512, 16)

    def body(s_ref, g_ref, r_ref, o_ref):
        o_ref[...] = (g_ref[...].astype(F32) + r_ref[...].astype(F32)).astype(o_ref.dtype)

    return pl.pallas_call(
        body, name=name,
        out_shape=jax.ShapeDtypeStruct((4, r, c), g.dtype),
        grid_spec=pltpu.PrefetchScalarGridSpec(
            num_scalar_prefetch=1, grid=(4, r // tr),
            in_specs=[pl.BlockSpec((None, tr, c), lambda k, i, s: (2 * k + s[0], i, 0)),
                      pl.BlockSpec((None, tr, c), lambda k, i, s: (k, i, 0))],
            out_specs=pl.BlockSpec((None, tr, c), lambda k, i, s: (k, i, 0))),
        compiler_params=_params(("parallel", "parallel")),
    )(core, g, recv)


def _adamw_math(w, g, m, v):
    nm = ADAM_B1 * m + (1.0 - ADAM_B1) * g
    nv = ADAM_B2 * v + (1.0 - ADAM_B2) * (g * g)
    m_hat = nm / (1.0 - ADAM_B1 ** ADAM_STEP)
    v_hat = nv / (1.0 - ADAM_B2 ** ADAM_STEP)
    return -ADAM_LR * (m_hat / (jnp.sqrt(v_hat) + ADAM_EPS) + ADAM_WD * w), nm, nv


def _final_sum_adamw(p, recv, chip, w, m, v, name):
    _, r, c = p.shape
    tr = _divisor_tile(r, 256, 16)
    tile = lambda: pl.BlockSpec((tr, c), lambda i, s: (i, 0))

    def body(s_ref, p_ref, r_ref, w_ref, m_ref, v_ref, g_ref, d_ref, nm_ref, nv_ref):
        g = p_ref[...].astype(F32)
        for j in range(3):
            g = g + r_ref[j].astype(F32)
        g_ref[...] = g
        d_ref[...], nm_ref[...], nv_ref[...] = _adamw_math(w_ref[...], g, m_ref[...], v_ref[...])

    sds = jax.ShapeDtypeStruct((r, c), F32)
    return pl.pallas_call(
        body, name=name,
        out_shape=(sds, sds, sds, sds),
        grid_spec=pltpu.PrefetchScalarGridSpec(
            num_scalar_prefetch=1, grid=(r // tr,),
            in_specs=[pl.BlockSpec((None, tr, c), lambda i, s: (s[0], i, 0)),
                      pl.BlockSpec((3, tr, c), lambda i, s: (0, i, 0)), tile(), tile(), tile()],
            out_specs=(tile(), tile(), tile(), tile())),
        compiler_params=_params(("parallel",)),
    )(chip, p, recv, w, m, v)


def _all_to_all_copies(v_ref, land_ref, send_sems, recv_sems):
    mx, my, mc = _my_position()
    me = 4 * mx + 2 * my + mc
    copies = []
    for rel in range(1, N_DEV):
        bx, by, bc = (rel >> 2) & 1, (rel >> 1) & 1, rel & 1
        target = (1 - mx if bx else mx, 1 - my if by else my, 1 - mc if bc else mc)
        copies.append(pltpu.make_async_remote_copy(
            src_ref=v_ref, dst_ref=land_ref.at[me], send_sem=send_sems.at[rel - 1], recv_sem=recv_sems.at[rel - 1],
            device_id=target, device_id_type=MESH))
    return copies


def _small_all_reduce_start(v, after, name):
    return _split_start(v, (N_DEV,) + v.shape, _all_to_all_copies, N_DEV - 1, after, name + "_start")


def _small_all_reduce_finish(started, after, dev, name):
    v, land = _split_wait(started, after, _all_to_all_copies, name + "_wait")
    rows = v.shape[0]

    def body(me_ref, v_ref, land_ref, o_ref):
        for j in range(N_DEV):
            @pl.when(me_ref[0] == j)
            def _():
                o_ref[...] = v_ref[...] if j == 0 else o_ref[...] + v_ref[...]

            @pl.when(me_ref[0] != j)
            def _():
                o_ref[...] = land_ref[j] if j == 0 else o_ref[...] + land_ref[j]

    return pl.pallas_call(
        body, name=name + "_sum",
        out_shape=jax.ShapeDtypeStruct((rows, 128), F32),
        grid_spec=pltpu.PrefetchScalarGridSpec(
            num_scalar_prefetch=1, grid=(1,),
            in_specs=[pl.BlockSpec((rows, 128), lambda i, s: (0, 0)),
                      pl.BlockSpec((N_DEV, rows, 128), lambda i, s: (0, 0, 0))],
            out_specs=pl.BlockSpec((rows, 128), lambda i, s: (0, 0))),
        compiler_params=_params(("arbitrary",)),
    )(dev, v, land)


def _assemble_w_in(blocks):
    rows, d = WIN_ROWS, blocks.shape[2]
    tc = _divisor_tile(d, 256, 128)
    n_tiles = WIN_N // 128
    last = (N_DEV * WIN_STRIDE) // 128

    def body(b_ref, o_ref):
        win = []
        for i in range(N_DEV):
            w = jnp.concatenate([b_ref[i].astype(F32), jnp.zeros((WIN_BLOCK - rows, tc), F32)], axis=0)
            win.append(pltpu.roll(w, i, 0) if i else w)
        for t in range(n_tiles):
            if t > last:
                o_ref[t * 128:(t + 1) * 128, :] = jnp.zeros((128, tc), o_ref.dtype)
                continue
            i = min(t // 7, N_DEV - 1)
            k = t - 7 * i
            val = win[i][k * 128:(k + 1) * 128, :]
            if k == 0 and i >= 1:
                val = val + win[i - 1][7 * 128:8 * 128, :]
            o_ref[t * 128:(t + 1) * 128, :] = val.astype(o_ref.dtype)

    return pl.pallas_call(
        body, name="assemble_w_in",
        out_shape=jax.ShapeDtypeStruct((WIN_N, d), blocks.dtype),
        grid=(d // tc,),
        in_specs=[pl.BlockSpec((N_DEV, rows, tc), lambda j: (0, 0, j))],
        out_specs=pl.BlockSpec((WIN_N, tc), lambda j: (0, j)),
        compiler_params=_params(("parallel",)),
    )(blocks)


def _extract_w_in_windows(g):
    _, d = g.shape
    tc = _divisor_tile(d, 256, 128)

    def body(g_ref, o_ref):
        for j in range(N_DEV):
            w = g_ref[WIN_STRIDE * j:WIN_STRIDE * j + WIN_BLOCK, :].astype(F32)
            w = pltpu.roll(w, WIN_BLOCK - j, 0) if j else w
            o_ref[j] = w[0:WIN_ROWS, :].astype(o_ref.dtype)

    return pl.pallas_call(
        body, name="extract_w_in_windows",
        out_shape=jax.ShapeDtypeStruct((N_DEV, WIN_ROWS, d), g.dtype),
        grid=(d // tc,),
        in_specs=[pl.BlockSpec((WIN_N, tc), lambda j: (0, j))],
        out_specs=pl.BlockSpec((N_DEV, WIN_ROWS, tc), lambda j: (0, 0, j)),
        compiler_params=_params(("parallel",)),
    )(g)


def _mm(a, b, *, a_spec, b_spec, o_spec, out_shape, grid, contract, nk, name, after=None):
    dn = (((contract[0],), (contract[1],)), ((), ()))
    tm, tn = o_spec.block_shape[-2:]
    behind = [] if after is None else [after]

    def body(a_ref, b_ref, *rest):
        o_ref, *scratch = rest[len(behind):]
        part = lax.dot_general(a_ref[...], b_ref[...], dn, preferred_element_type=F32)
        if nk == 1:
            o_ref[...] = part.astype(o_ref.dtype)
            return
        acc = scratch[0]
        k = pl.program_id(2)

        @pl.when(k == 0)
        def _():
            acc[...] = part

        @pl.when(k > 0)
        def _():
            acc[...] += part

        @pl.when(k == nk - 1)
        def _():
            o_ref[...] = acc[...].astype(o_ref.dtype)

    return pl.pallas_call(
        body, name=name, out_shape=out_shape, grid=grid,
        in_specs=[a_spec, b_spec] + [ANY] * len(behind), out_specs=o_spec,
        scratch_shapes=[] if nk == 1 else [pltpu.VMEM((tm, tn), F32)],
        compiler_params=_params(("parallel", "parallel", "arbitrary")),
    )(a, b, *behind)


def _mm_nn(a, b, out_dtype, name, tm_cap=1088, tn_cap=512, tk_cap=2048, after=None):
    m, k = a.shape
    _, n = b.shape
    tm, tn, tk = _divisor_tile(m, tm_cap, 16), _divisor_tile(n, tn_cap, 128), _divisor_tile(k, tk_cap, 128)
    return _mm(a, b,
               a_spec=pl.BlockSpec((tm, tk), lambda i, j, kk: (i, kk)),
               b_spec=pl.BlockSpec((tk, tn), lambda i, j, kk: (kk, j)),
               o_spec=pl.BlockSpec((tm, tn), lambda i, j, kk: (i, j)),
               out_shape=jax.ShapeDtypeStruct((m, n), out_dtype),
               grid=(m // tm, n // tn, k // tk), contract=(1, 0), nk=k // tk, name=name, after=after)


def _mm_nt(a, b, out_dtype, name, tm_cap=1088, tn_cap=512, tk_cap=2048, after=None):
    m, k = a.shape
    n, _ = b.shape
    tm, tn, tk = _divisor_tile(m, tm_cap, 16), _divisor_tile(n, tn_cap, 128), _divisor_tile(k, tk_cap, 128)
    return _mm(a, b,
               a_spec=pl.BlockSpec((tm, tk), lambda i, j, kk: (i, kk)),
               b_spec=pl.BlockSpec((tn, tk), lambda i, j, kk: (j, kk)),
               o_spec=pl.BlockSpec((tm, tn), lambda i, j, kk: (i, j)),
               out_shape=jax.ShapeDtypeStruct((m, n), out_dtype),
               grid=(m // tm, n // tn, k // tk), contract=(1, 1), nk=k // tk, name=name, after=after)


def _mm_tn(a, b, out_dtype, name, tm_cap=1024, tn_cap=512, after=None):
    l, m = a.shape
    _, n = b.shape
    tm, tn = _divisor_tile(m, tm_cap, 128), _divisor_tile(n, tn_cap, 128)
    return _mm(a, b,
               a_spec=pl.BlockSpec((l, tm), lambda i, j, kk: (0, i)),
               b_spec=pl.BlockSpec((l, tn), lambda i, j, kk: (0, j)),
               o_spec=pl.BlockSpec((tm, tn), lambda i, j, kk: (i, j)),
               out_shape=jax.ShapeDtypeStruct((m, n), out_dtype),
               grid=(m // tm, n // tn, 1), contract=(0, 0), nk=1, name=name, after=after)


def _pair_split(shard):
    left = shard % ATTN_BLOCK
    assert left in (0, CHUNK) and shard > left
    return shard - left, left


def _mm_up(cn, w_up_blocks, after):
    l, d = cn.shape
    n, _, shard = w_up_blocks.shape
    main, left = _pair_split(shard)
    tm = _divisor_tile(l, 544, 16)

    def body(a_ref, b_ref, after_ref, o_ref):
        a = a_ref[...]
        for s in range(2):
            o_ref[:, s * shard:s * shard + main] = _dot(a, b_ref[s, :, 0:main])
        if left:
            tail = _dot(a, jnp.concatenate([b_ref[0, :, main:], b_ref[1, :, main:]], axis=1))
            o_ref[:, main:shard] = tail[:, 0:left]
            o_ref[:, shard + main:2 * shard] = tail[:, left:]

    return pl.pallas_call(
        body, name="mm_up", out_shape=jax.ShapeDtypeStruct((l, n * shard), F32), grid=(l // tm, n // 2),
        in_specs=[pl.BlockSpec((tm, d), lambda i, j: (i, 0)),
                  pl.BlockSpec((2, d, shard), lambda i, j: (j, 0, 0)), ANY],
        out_specs=pl.BlockSpec((tm, 2 * shard), lambda i, j: (i, j)),
        compiler_params=_params(("parallel", "parallel")),
    )(cn, w_up_blocks, after)


def _mm_gw_up(cn, d_u):
    l, d = cn.shape
    _, _, d_ff = d_u.shape
    shard = 2 * d_ff // N_DEV
    pairs_per_half = d_ff // (2 * shard)
    tm = _divisor_tile(d, 512, 128)

    def body(a_ref, b_ref, o_ref):
        res = _dot_tn(a_ref[...], b_ref[...])
        o_ref[0] = res[:, 0:shard].astype(o_ref.dtype)
        o_ref[1] = res[:, shard:].astype(o_ref.dtype)

    return pl.pallas_call(
        body, name="mm_gw_up", out_shape=jax.ShapeDtypeStruct((N_DEV, d, shard), WIRE_DTYPE),
        grid=(d // tm, N_DEV // 2),
        in_specs=[pl.BlockSpec((l, tm), lambda i, j: (0, i)),
                  pl.BlockSpec((None, l, 2 * shard), lambda i, j: (j // pairs_per_half, 0, j % pairs_per_half))],
        out_specs=pl.BlockSpec((2, tm, shard), lambda i, j: (j, i, 0)),
        compiler_params=_params(("parallel", "parallel")),
    )(cn, d_u)


def _mm_d_cn(d_u, w_up_blocks, after):
    _, l, d_ff = d_u.shape
    n, d, shard = w_up_blocks.shape
    per = d_ff // shard
    main, left = _pair_split(shard)
    tm, tn = _divisor_tile(l, 544, 16), _divisor_tile(d, 256, 128)

    def body(a_ref, b_ref, after_ref, o_ref):
        acc = None
        for k in range(0, n, 2):
            half, c0 = k // per, (k % per) * shard
            parts = [_dot_nt(a_ref[half, :, c0 + s * shard:c0 + s * shard + main], b_ref[k + s, :, 0:main])
                     for s in range(2)]
            if left:
                a_tail = jnp.concatenate([a_ref[half, :, c0 + s * shard + main:c0 + (s + 1) * shard] for s in range(2)],
                                         axis=1)
                b_tail = jnp.concatenate([b_ref[k + s, :, main:] for s in range(2)], axis=1)
                parts.append(_dot_nt(a_tail, b_tail))
            for part in parts:
                acc = part if acc is None else acc + part
        o_ref[...] = acc

    return pl.pallas_call(
        body, name="mm_d_cn", out_shape=jax.ShapeDtypeStruct((l, d), F32), grid=(l // tm, d // tn),
        in_specs=[pl.BlockSpec((2, tm, d_ff), lambda i, j: (0, i, 0)),
                  pl.BlockSpec((n, tn, shard), lambda i, j: (0, j, 0)), ANY],
        out_specs=pl.BlockSpec((tm, tn), lambda i, j: (i, j)),
        compiler_params=_params(("parallel", "parallel")),
    )(d_u, w_up_blocks, after)


def _row_tile(l):
    return _divisor_tile(l, 544, 8)


def _rms(x, gain):
    return (x * lax.rsqrt(jnp.mean(x * x, axis=-1, keepdims=True) + NORM_EPS) * gain).astype(MXU_DTYPE)


def _embed_rmsnorm(x, meta, gain):
    seq, d = x.shape
    l = CHUNK + seq
    row = pl.BlockSpec((CHUNK, d), lambda i: (i, 0))

    def body(x_ref, m_ref, g_ref, h_ref, n_ref):
        @pl.when(pl.program_id(0) == 0)
        def _():
            h_ref[...] = jnp.concatenate([jnp.zeros((PAD_ROWS, d), F32), m_ref[...]], axis=0)

        @pl.when(pl.program_id(0) > 0)
        def _():
            h_ref[...] = x_ref[...]

        n_ref[...] = _rms(h_ref[...], g_ref[...])

    return pl.pallas_call(
        body, name="embed_rmsnorm1",
        out_shape=(jax.ShapeDtypeStruct((l, d), F32), jax.ShapeDtypeStruct((l, d), MXU_DTYPE)),
        grid=(l // CHUNK,),
        in_specs=[pl.BlockSpec((CHUNK, d), lambda i: (jnp.maximum(i - 1, 0), 0)),
                  pl.BlockSpec((N_META, d), lambda i: (0, 0)), pl.BlockSpec((1, d), lambda i: (0, 0))],
        out_specs=(row, row),
        compiler_params=_params(("parallel",)),
    )(x, meta, gain)


def _out_proj_resid_rmsnorm(mix, w_out, h0, gain):
    l, d = h0.shape
    tm = _divisor_tile(l, 272, 16)
    row = pl.BlockSpec((tm, d), lambda i: (i, 0))

    def body(a_ref, b_ref, h_ref, g_ref, s_ref, n_ref):
        x = h_ref[...] + _dot(a_ref[...], b_ref[...])
        s_ref[...] = x
        n_ref[...] = _rms(x, g_ref[...])

    return pl.pallas_call(
        body, name="mm_out_resid_rmsnorm2",
        out_shape=(jax.ShapeDtypeStruct((l, d), F32), jax.ShapeDtypeStruct((l, d), MXU_DTYPE)),
        grid=(l // tm,),
        in_specs=[row, pl.BlockSpec((d, d), lambda i: (0, 0)), row, pl.BlockSpec((1, d), lambda i: (0, 0))],
        out_specs=(row, row),
        compiler_params=_params(("parallel",)),
    )(mix, w_out, h0, gain)


def _rmsnorm_bwd(d_res, d_normed, x, gain, name, with_mxu_copy):
    l, d = x.shape
    tr = _row_tile(l) if with_mxu_copy else CHUNK
    row = pl.BlockSpec((tr, d), lambda i: (i, 0))
    vec = pl.BlockSpec((1, d), lambda i: (0, 0))

    def body(dres_ref, dn_ref, x_ref, g_ref, dx_ref, other_ref, dg_ref):
        i = pl.program_id(0)
        xv = x_ref[...]
        r = lax.rsqrt(jnp.mean(xv * xv, axis=-1, keepdims=True) + NORM_EPS)
        xh = xv * r
        dn = dn_ref[...]
        dxh = dn * g_ref[...]
        dx = dres_ref[...] + r * (dxh - xh * jnp.mean(dxh * xh, axis=-1, keepdims=True))
        if with_mxu_copy:
            dx_ref[...] = dx
            other_ref[...] = dx.astype(MXU_DTYPE)
        else:
            @pl.when(i == 0)
            def _():
                dx_ref[...] = dx

            @pl.when(i > 0)
            def _():
                other_ref[...] = dx

        @pl.when(i == 0)
        def _():
            dg_ref[...] = jnp.zeros_like(dg_ref)

        dg_ref[...] += jnp.sum(dn * xh, axis=0, keepdims=True)

    if with_mxu_copy:
        outs = [jax.ShapeDtypeStruct((l, d), F32), jax.ShapeDtypeStruct((l, d), MXU_DTYPE)]
        specs = [row, row]
    else:
        outs = [jax.ShapeDtypeStruct((CHUNK, d), F32), jax.ShapeDtypeStruct((l - CHUNK, d), F32)]
        specs = [pl.BlockSpec((CHUNK, d), lambda i: (0, 0)), pl.BlockSpec((CHUNK, d), lambda i: (jnp.maximum(i - 1, 0), 0))]
    outs.append(jax.ShapeDtypeStruct((1, d), F32))
    specs.append(vec)
    return pl.pallas_call(body, name=name, out_shape=tuple(outs), grid=(l // tr,),
                          in_specs=[row, row, row, vec], out_specs=tuple(specs),
                          compiler_params=_params(("arbitrary",)))(d_res, d_normed, x, gain)


def _loss_head(h1, mlp_out, gain, target):
    l, d = h1.shape
    n_blocks = l // CHUNK
    row = pl.BlockSpec((CHUNK, d), lambda i: (i, 0))
    vec = pl.BlockSpec((1, d), lambda i: (0, 0))
    tgt = pl.BlockSpec((CHUNK, d), lambda i: (jnp.maximum(i - 1, 0), 0))

    def body(h_ref, m_ref, g_ref, t_ref, dh_ref, dhb_ref, dg_ref, loss_ref, sq_ref):
        i = pl.program_id(0)
        x = h_ref[...] + m_ref[...]
        r = lax.rsqrt(jnp.mean(x * x, axis=-1, keepdims=True) + NORM_EPS)
        xh = x * r
        g = g_ref[...]
        real = i >= 1
        err = jnp.where(real, xh * g - t_ref[...], 0.0)
        dy = err * (1.0 / d)
        dxh = dy * g
        dh = r * (dxh - xh * jnp.mean(dxh * xh, axis=-1, keepdims=True))
        dh_ref[...] = dh
        dhb_ref[...] = dh.astype(MXU_DTYPE)

        @pl.when(i == 0)
        def _():
            dg_ref[...] = jnp.zeros_like(dg_ref)
            sq_ref[...] = jnp.zeros_like(sq_ref)

        dg_ref[...] += jnp.sum(dy * xh, axis=0, keepdims=True)
        sq_ref[...] += jnp.sum(err * err, axis=0, keepdims=True)

        @pl.when(i == n_blocks - 1)
        def _():
            total = jnp.sum(sq_ref[...], axis=-1, keepdims=True) * (0.5 / d)
            loss_ref[...] = jnp.broadcast_to(total, (1, 128))

    return pl.pallas_call(
        body, name="loss_head",
        out_shape=(jax.ShapeDtypeStruct((l, d), F32), jax.ShapeDtypeStruct((l, d), MXU_DTYPE),
                   jax.ShapeDtypeStruct((1, d), F32), jax.ShapeDtypeStruct((1, 128), F32)),
        grid=(n_blocks,), in_specs=[row, row, vec, tgt],
        out_specs=(row, row, vec, pl.BlockSpec((1, 128), lambda i: (0, 0))),
        scratch_shapes=[pltpu.VMEM((1, d), F32)],
        compiler_params=_params(("arbitrary",)),
    )(h1, mlp_out, gain, target)


def _dot(a, b):
    return jnp.dot(a, b, preferred_element_type=F32)


def _dot_nt(a, b):
    return lax.dot_general(a, b, (((1,), (1,)), ((), ())), preferred_element_type=F32)


def _dot_tn(a, b):
    return lax.dot_general(a, b, (((0,), (0,)), ((), ())), preferred_element_type=F32)


def _rope(t, cos2, sin2):
    return t * cos2 + pltpu.roll(t, HEAD_DIM // 2, 1) * sin2


def _rope_bwd(dr, cos2, sin2):
    return dr * cos2 + pltpu.roll(dr * sin2, HEAD_DIM // 2, 1)


def _sigmoid(x):
    return 1.0 / (1.0 + jnp.exp(-x))


def _row_valid(block, rows):
    r = block * CHUNK + lax.broadcasted_iota(jnp.int32, (rows, 1), 0)
    return r >= PAD_ROWS


def _retention_consts(l):
    pos = jnp.arange(l, dtype=F32) - PAD_ROWS
    inv_freq = 1.0 / (ROPE_BASE ** (jnp.arange(0, HEAD_DIM, 2, dtype=F32) / HEAD_DIM))
    ang = pos[:, None] * inv_freq[None, :]
    cos, sin = jnp.cos(ang), jnp.sin(ang)
    cos2 = jnp.concatenate([cos, cos], axis=-1)
    sin2 = jnp.concatenate([-sin, sin], axis=-1)
    log_g = jnp.log1p(-jnp.exp2(-5.0 - jnp.arange(N_HEADS, dtype=F32)))
    idx = jnp.arange(CHUNK, dtype=F32)
    diff = idx[:, None] - idx[None, :]
    decay = jnp.where(diff >= 0, jnp.exp(jnp.maximum(diff, 0.0)[None] * log_g[:, None, None]), 0.0)
    xi = jnp.exp((idx + 1.0)[None, :] * log_g[:, None])
    zeta = jnp.exp((CHUNK - 1.0 - idx)[None, :] * log_g[:, None])
    g_chunk = jnp.exp(CHUNK * log_g)
    bcast = lambda v: jnp.broadcast_to(v[:, :, None], (N_HEADS, CHUNK, HEAD_DIM))
    g_rows = jnp.broadcast_to(g_chunk[:, None, None], (N_HEADS, 8, HEAD_DIM))
    return cos2, sin2, decay, bcast(xi), bcast(zeta), g_rows


def _retention_fwd(proj, ret_gain, consts):
    l = proj.shape[0]
    n_chunks = l // CHUNK
    cos2, sin2, decay, xi, zeta, g_rows = consts
    scale = HEAD_DIM ** -0.5

    def body(p_ref, cos_ref, sin_ref, dec_ref, xi_ref, zeta_ref, gr_ref, gain_ref,
             mix_ref, o_ref, st_ref, state):
        c = pl.program_id(0)

        @pl.when(c == 0)
        def _():
            state[...] = jnp.zeros_like(state)

        cos_v, sin_v = cos_ref[...], sin_ref[...]
        valid = _row_valid(c, CHUNK)
        for h in range(N_HEADS):
            cols = slice(h * HEAD_DIM, (h + 1) * HEAD_DIM)
            q = p_ref[:, h * HEAD_DIM:(h + 1) * HEAD_DIM]
            k = p_ref[:, GROUP + h * HEAD_DIM:GROUP + (h + 1) * HEAD_DIM]
            v = p_ref[:, 2 * GROUP + h * HEAD_DIM:2 * GROUP + (h + 1) * HEAD_DIM]
            g = p_ref[:, 3 * GROUP + h * HEAD_DIM:3 * GROUP + (h + 1) * HEAD_DIM]
            rq = _rope(q, cos_v, sin_v).astype(MXU_DTYPE)
            rk = _rope(k, cos_v, sin_v) * scale
            rkb = rk.astype(MXU_DTYPE)
            vb = v.astype(MXU_DTYPE)
            st = state[h]
            st_ref[h] = st
            s = _dot_nt(rq, rkb) * dec_ref[h]
            o = _dot(s.astype(MXU_DTYPE), vb) + _dot(rq, st.astype(MXU_DTYPE)) * xi_ref[h]
            kz = (rk * zeta_ref[h]).astype(MXU_DTYPE)
            state[h] = gr_ref[h, 0:1, :] * st + _dot_tn(kz, vb)
            o_ref[:, cols] = o
            mu = jnp.mean(o, axis=-1, keepdims=True)
            oc = o - mu
            yn = oc * lax.rsqrt(jnp.mean(oc * oc, axis=-1, keepdims=True) + NORM_EPS)
            ret = (g * _sigmoid(g)) * (yn * gain_ref[:, cols])
            mix_ref[:, cols] = jnp.where(valid, ret, 0.0).astype(mix_ref.dtype)

    head_tab = pl.BlockSpec((N_HEADS, CHUNK, HEAD_DIM), lambda c: (0, 0, 0))
    return pl.pallas_call(
        body, name="retention_fwd",
        out_shape=(jax.ShapeDtypeStruct((l, 2 * GROUP), MXU_DTYPE), jax.ShapeDtypeStruct((l, GROUP), F32),
                   jax.ShapeDtypeStruct((n_chunks, N_HEADS, HEAD_DIM, HEAD_DIM), F32)),
        grid=(n_chunks,),
        in_specs=[pl.BlockSpec((CHUNK, 4 * GROUP), lambda c: (c, 0)),
                  pl.BlockSpec((CHUNK, HEAD_DIM), lambda c: (c, 0)),
                  pl.BlockSpec((CHUNK, HEAD_DIM), lambda c: (c, 0)),
                  head_tab, head_tab, head_tab,
                  pl.BlockSpec((N_HEADS, 8, HEAD_DIM), lambda c: (0, 0, 0)),
                  pl.BlockSpec((1, GROUP), lambda c: (0, 0))],
        out_specs=(pl.BlockSpec((CHUNK, GROUP), lambda c: (c, 0)),
                   pl.BlockSpec((CHUNK, GROUP), lambda c: (c, 0)),
                   pl.BlockSpec((None, N_HEADS, HEAD_DIM, HEAD_DIM), lambda c: (c, 0, 0, 0))),
        scratch_shapes=[pltpu.VMEM((N_HEADS, HEAD_DIM, HEAD_DIM), F32)],
        compiler_params=_params(("arbitrary",)),
    )(proj, cos2, sin2, decay, xi, zeta, g_rows, ret_gain)


def _retention_bwd(proj, o_pre, states, d_mix, ret_gain, consts):
    l = proj.shape[0]
    n_chunks = l // CHUNK
    cos2, sin2, decay, xi, zeta, g_rows = consts
    scale = HEAD_DIM ** -0.5
    rev = lambda c: n_chunks - 1 - c

    def body(p_ref, o_ref, st_ref, dm_ref, cos_ref, sin_ref, dec_ref, dect_ref, xi_ref, zeta_ref, gr_ref, gain_ref,
             dp_ref, dgain_ref, dstate):
        step = pl.program_id(0)

        @pl.when(step == 0)
        def _():
            dstate[...] = jnp.zeros_like(dstate)
            dgain_ref[...] = jnp.zeros_like(dgain_ref)

        cos_v, sin_v = cos_ref[...], sin_ref[...]
        valid = _row_valid(rev(step), CHUNK)
        for h in range(N_HEADS):
            cols = slice(h * HEAD_DIM, (h + 1) * HEAD_DIM)
            q = p_ref[:, h * HEAD_DIM:(h + 1) * HEAD_DIM]
            k = p_ref[:, GROUP + h * HEAD_DIM:GROUP + (h + 1) * HEAD_DIM]
            v = p_ref[:, 2 * GROUP + h * HEAD_DIM:2 * GROUP + (h + 1) * HEAD_DIM]
            g = p_ref[:, 3 * GROUP + h * HEAD_DIM:3 * GROUP + (h + 1) * HEAD_DIM]
            o = o_ref[:, cols]
            gain = gain_ref[:, cols]
            d_ret = jnp.where(valid, dm_ref[:, cols], 0.0)
            mu = jnp.mean(o, axis=-1, keepdims=True)
            oc = o - mu
            rstd = lax.rsqrt(jnp.mean(oc * oc, axis=-1, keepdims=True) + NORM_EPS)
            yn = oc * rstd
            sig = _sigmoid(g)
            gate = g * sig
            dgain_ref[:, cols] += jnp.sum(d_ret * gate * yn, axis=0, keepdims=True)
            d_g = d_ret * (yn * gain) * (sig * (1.0 + g * (1.0 - sig)))
            d_yn = d_ret * gate * gain
            d_o = rstd * (d_yn - jnp.mean(d_yn, axis=-1, keepdims=True)
                          - yn * jnp.mean(d_yn * yn, axis=-1, keepdims=True))
            rq = _rope(q, cos_v, sin_v)
            rk = _rope(k, cos_v, sin_v) * scale
            rqb, rkb, vb = rq.astype(MXU_DTYPE), rk.astype(MXU_DTYPE), v.astype(MXU_DTYPE)
            dob = d_o.astype(MXU_DTYPE)
            dec = dec_ref[h]
            xi_h, zeta_h = xi_ref[h], zeta_ref[h]
            st_b = st_ref[h].astype(MXU_DTYPE)
            dst = dstate[h]
            dst_b = dst.astype(MXU_DTYPE)
            dec_t = dect_ref[h]
            s_t_b = (_dot_nt(rkb, rqb) * dec_t).astype(MXU_DTYPE)
            da_b = (_dot_nt(dob, vb) * dec).astype(MXU_DTYPE)
            da_t_b = (_dot_nt(vb, dob) * dec_t).astype(MXU_DTYPE)
            doxi_b = (d_o * xi_h).astype(MXU_DTYPE)
            kz_b = (rk * zeta_h).astype(MXU_DTYPE)
            d_rq = _dot(da_b, rkb) + _dot_nt(doxi_b, st_b)
            d_rk = _dot(da_t_b, rqb) + _dot_nt(vb, dst_b) * zeta_h
            d_v = _dot(s_t_b, dob) + _dot(kz_b, dst_b)
            dstate[h] = gr_ref[h, 0:1, :] * dst + _dot_tn(rqb, doxi_b)
            d_q = _rope_bwd(d_rq, cos_v, sin_v)
            d_k = _rope_bwd(d_rk * scale, cos_v, sin_v)
            dp_ref[:, h * HEAD_DIM:(h + 1) * HEAD_DIM] = d_q.astype(dp_ref.dtype)
            dp_ref[:, GROUP + h * HEAD_DIM:GROUP + (h + 1) * HEAD_DIM] = d_k.astype(dp_ref.dtype)
            dp_ref[:, 2 * GROUP + h * HEAD_DIM:2 * GROUP + (h + 1) * HEAD_DIM] = d_v.astype(dp_ref.dtype)
            dp_ref[:, 3 * GROUP + h * HEAD_DIM:3 * GROUP + (h + 1) * HEAD_DIM] = d_g.astype(dp_ref.dtype)

    head_tab = pl.BlockSpec((N_HEADS, CHUNK, HEAD_DIM), lambda c: (0, 0, 0))
    return pl.pallas_call(
        body, name="retention_bwd",
        out_shape=(jax.ShapeDtypeStruct((l, 4 * GROUP), MXU_DTYPE), jax.ShapeDtypeStruct((1, GROUP), F32)),
        grid=(n_chunks,),
        in_specs=[pl.BlockSpec((CHUNK, 4 * GROUP), lambda c: (rev(c), 0)),
                  pl.BlockSpec((CHUNK, GROUP), lambda c: (rev(c), 0)),
                  pl.BlockSpec((None, N_HEADS, HEAD_DIM, HEAD_DIM), lambda c: (rev(c), 0, 0, 0)),
                  pl.BlockSpec((CHUNK, GROUP), lambda c: (rev(c), 0)),
                  pl.BlockSpec((CHUNK, HEAD_DIM), lambda c: (rev(c), 0)),
                  pl.BlockSpec((CHUNK, HEAD_DIM), lambda c: (rev(c), 0)),
                  head_tab, head_tab, head_tab, head_tab,
                  pl.BlockSpec((N_HEADS, 8, HEAD_DIM), lambda c: (0, 0, 0)),
                  pl.BlockSpec((1, GROUP), lambda c: (0, 0))],
        out_specs=(pl.BlockSpec((CHUNK, 4 * GROUP), lambda c: (rev(c), 0)),
                   pl.BlockSpec((1, GROUP), lambda c: (0, 0))),
        scratch_shapes=[pltpu.VMEM((N_HEADS, HEAD_DIM, HEAD_DIM), F32)],
        compiler_params=_params(("arbitrary",)),
    )(proj, o_pre, states, d_mix, cos2, sin2, decay, jnp.transpose(decay, (0, 2, 1)), xi, zeta, g_rows, ret_gain)


FF_TILE = (7 * GROUP) // 128


def _log_forget(ff, bias_row, valid):
    x = ff + bias_row
    e = jnp.exp(-jnp.abs(x))
    lf = jnp.minimum(x, 0.0) - jnp.log(1.0 + e)
    head_lane = lax.broadcasted_iota(jnp.int32, x.shape, 1) < N_HEADS
    keep = lambda t: jnp.where(head_lane, jnp.where(valid, t, 0.0), 0.0)
    return keep(lf), keep(jnp.where(x >= 0, e, 1.0) / (1.0 + e))


def _fox_prep(proj, bias_row):
    l = proj.shape[0]
    n_blocks = l // CHUNK

    def body(ff_ref, b_ref, bc_ref, rows_ref, cum):
        r = lax.broadcasted_iota(jnp.int32, (CHUNK, CHUNK), 0)
        cidx = lax.broadcasted_iota(jnp.int32, (CHUNK, CHUNK), 1)
        tri = jnp.where(r >= cidx, 1.0, 0.0).astype(F32)
        carry = jnp.zeros((1, 128), F32)
        for blk in range(n_blocks):
            rows = slice(blk * CHUNK, (blk + 1) * CHUNK)
            valid = _row_valid(blk, CHUNK)
            lf, _ = _log_forget(ff_ref[rows, :], b_ref[...], valid)
            local = jnp.dot(tri, lf, precision=lax.Precision.HIGHEST, preferred_element_type=F32) + carry
            carry = local[CHUNK - 1:CHUNK, :]
            masked = jnp.where(valid, local, -NEG_BIG)
            cum[rows, :] = masked
            t = masked.T
            for h in range(N_HEADS):
                rows_ref[h, :, rows] = t[h:h + 1, :]
        full = cum[...]
        for h in range(N_HEADS):
            bc_ref[h] = jnp.broadcast_to(full[:, h:h + 1], (l, 128))

    return pl.pallas_call(
        body, name="fox_prep",
        out_shape=(jax.ShapeDtypeStruct((N_HEADS, l, 128), F32), jax.ShapeDtypeStruct((N_HEADS, 1, l), F32)),
        grid=(1,),
        in_specs=[pl.BlockSpec((l, 128), lambda i: (0, FF_TILE)), pl.BlockSpec((1, 128), lambda i: (0, 0))],
        out_specs=(pl.BlockSpec((N_HEADS, l, 128), lambda i: (0, 0, 0)),
                   pl.BlockSpec((N_HEADS, 1, l), lambda i: (0, 0, 0))),
        scratch_shapes=[pltpu.VMEM((l, 128), F32)],
        compiler_params=_params(("arbitrary",)),
    )(proj, bias_row)


ATTN_BLOCK = 2 * CHUNK


def _attn_blocks(l):
    assert (l - CHUNK) % ATTN_BLOCK == 0
    return [(0, CHUNK)] + [(s, ATTN_BLOCK) for s in range(CHUNK, l, ATTN_BLOCK)]


def _rows_valid(start, size):
    return start + lax.broadcasted_iota(jnp.int32, (size, 1), 0) >= PAD_ROWS


def _fox_fwd(proj, cum_bc, cum_rows, mix):
    l = proj.shape[0]
    blocks = _attn_blocks(l)
    scale = HEAD_DIM ** -0.5
    qt, kt, vt = 4 * N_HEADS, 5 * N_HEADS, 6 * N_HEADS

    def body(q_ref, k_ref, v_ref, cbc_ref, crow_ref, mix_in, o_ref, lse_ref, qb_s, kb_s, vb_s):
        qb_s[...] = q_ref[...].astype(MXU_DTYPE)
        kb_s[...] = k_ref[...].astype(MXU_DTYPE)
        vb_s[...] = v_ref[...].astype(MXU_DTYPE)
        for p, (qs, qn) in enumerate(blocks):
            qb = qb_s[qs:qs + qn, :]
            cq = cbc_ref[qs:qs + qn, :]
            m = jnp.full((qn, 1), NEG_BIG, F32)
            lsum = jnp.zeros((qn, 1), F32)
            acc = jnp.zeros((qn, HEAD_DIM), F32)
            for j in range(p + 1):
                ks, kn = blocks[j]
                bias = jnp.tile(cq, (1, kn // CHUNK)) - crow_ref[:, ks:ks + kn]
                s = _dot_nt(qb, kb_s[ks:ks + kn, :]) * scale + bias
                if j == p:
                    q_pos = qs + lax.broadcasted_iota(jnp.int32, (qn, kn), 0)
                    k_pos = ks + lax.broadcasted_iota(jnp.int32, (qn, kn), 1)
                    s = jnp.where(k_pos <= q_pos, s, NEG_BIG)
                m_new = jnp.maximum(m, jnp.max(s, axis=-1, keepdims=True))
                alpha = jnp.exp(m - m_new)
                pr = jnp.exp(s - m_new)
                lsum = lsum * alpha + jnp.sum(pr, axis=-1, keepdims=True)
                acc = acc * alpha + _dot(pr.astype(MXU_DTYPE), vb_s[ks:ks + kn, :])
                m = m_new
            o = jnp.where(_rows_valid(qs, qn), acc * (1.0 / lsum), 0.0)
            o_ref[qs:qs + qn, :] = o.astype(o_ref.dtype)
            lse = m + jnp.log(lsum)
            lse_ref[:, qs:qs + qn] = jnp.broadcast_to(lse, (qn, CHUNK)).T[0:1, :]

    head_col = lambda t: pl.BlockSpec((l, HEAD_DIM), lambda h: (0, t + h))
    return pl.pallas_call(
        body, name="fox_fwd",
        out_shape=(jax.ShapeDtypeStruct(mix.shape, mix.dtype), jax.ShapeDtypeStruct((N_HEADS, 1, l), F32)),
        grid=(N_HEADS,),
        in_specs=[head_col(qt), head_col(kt), head_col(vt),
                  pl.BlockSpec((None, l, 128), lambda h: (h, 0, 0)),
                  pl.BlockSpec((None, 1, l), lambda h: (h, 0, 0)),
                  ANY],
        out_specs=(head_col(N_HEADS), pl.BlockSpec((None, 1, l), lambda h: (h, 0, 0))),
        input_output_aliases={5: 0},
        scratch_shapes=[pltpu.VMEM((l, HEAD_DIM), MXU_DTYPE)] * 3,
        compiler_params=_params(("parallel",)),
    )(proj, proj, proj, cum_bc, cum_rows, mix)


def _fox_bwd(proj, cum_bc, cum_rows, d_mix, lse_rows):
    l = proj.shape[0]
    blocks = _attn_blocks(l)
    scale = HEAD_DIM ** -0.5
    qt, kt, vt = 4 * N_HEADS, 5 * N_HEADS, 6 * N_HEADS

    def body(q_ref, k_ref, v_ref, do_ref, cbc_ref, crow_ref, lse_ref,
             dq_ref, dk_ref, dv_ref, ds_ref, dk_acc, dv_acc, qb_s, kb_s, vb_s, dob_s, p_s, dp_s):
        qb_s[...] = q_ref[...].astype(MXU_DTYPE)
        kb_s[...] = k_ref[...].astype(MXU_DTYPE)
        vb_s[...] = v_ref[...].astype(MXU_DTYPE)
        dob_s[...] = jnp.where(_rows_valid(0, l), do_ref[...], 0.0).astype(MXU_DTYPE)
        dk_acc[...] = jnp.zeros_like(dk_acc)
        dv_acc[...] = jnp.zeros_like(dv_acc)
        ds_ref[...] = jnp.zeros_like(ds_ref)
        shift_row = crow_ref[...] - lse_ref[...]

        for p, (qs, qn) in enumerate(blocks):
            qb, dob = qb_s[qs:qs + qn, :], dob_s[qs:qs + qn, :]
            shift = shift_row[:, qs:qs + qn]

            delta = jnp.zeros((1, qn), F32)
            for j in range(p + 1):
                ks, kn = blocks[j]
                ck = jnp.tile(cbc_ref[ks:ks + kn, :], (1, qn // CHUNK))
                s_t = _dot_nt(kb_s[ks:ks + kn, :], qb) * scale + (shift - ck)
                if j == p:
                    k_pos = ks + lax.broadcasted_iota(jnp.int32, (kn, qn), 0)
                    q_pos = qs + lax.broadcasted_iota(jnp.int32, (kn, qn), 1)
                    s_t = jnp.where(k_pos <= q_pos, s_t, NEG_BIG)
                p_t, dp_t = jnp.exp(s_t), _dot_nt(vb_s[ks:ks + kn, :], dob)
                p_s[j, 0:kn, 0:qn] = p_t
                dp_s[j, 0:kn, 0:qn] = dp_t
                delta = delta + jnp.sum(p_t * dp_t, axis=0, keepdims=True)
            dq = jnp.zeros((qn, HEAD_DIM), F32)
            for j in range(p + 1):
                ks, kn = blocks[j]
                rows = slice(ks, ks + kn)
                p_t, dp_t = p_s[j, 0:kn, 0:qn], dp_s[j, 0:kn, 0:qn]
                ds_t = p_t * (dp_t - delta)
                ds_b = ds_t.astype(MXU_DTYPE)
                dv_acc[rows, :] += _dot(p_t.astype(MXU_DTYPE), dob)
                dk_acc[rows, :] += _dot(ds_b, qb) * scale
                ds_ref[rows, :] += sum(ds_t[:, c:c + CHUNK] for c in range(0, qn, CHUNK))
                dq = dq + _dot_tn(ds_b, kb_s[rows, :])
            dq_ref[qs:qs + qn, :] = (dq * scale).astype(dq_ref.dtype)

        dk_ref[...] = dk_acc[...].astype(dk_ref.dtype)
        dv_ref[...] = dv_acc[...].astype(dv_ref.dtype)

    col = jax.ShapeDtypeStruct((l, GROUP), MXU_DTYPE)
    head_col = lambda t: pl.BlockSpec((l, HEAD_DIM), lambda h: (0, t + h))
    return pl.pallas_call(
        body, name="fox_bwd",
        out_shape=(col, col, col, jax.ShapeDtypeStruct((N_HEADS, l, 128), F32)),
        grid=(N_HEADS,),
        in_specs=[head_col(qt), head_col(kt), head_col(vt), head_col(N_HEADS),
                  pl.BlockSpec((None, l, 128), lambda h: (h, 0, 0)),
                  pl.BlockSpec((None, 1, l), lambda h: (h, 0, 0)),
                  pl.BlockSpec((None, 1, l), lambda h: (h, 0, 0))],
        out_specs=(head_col(0), head_col(0), head_col(0), pl.BlockSpec((None, l, 128), lambda h: (h, 0, 0))),
        scratch_shapes=([pltpu.VMEM((l, HEAD_DIM), F32)] * 2 + [pltpu.VMEM((l, HEAD_DIM), MXU_DTYPE)] * 4
                        + [pltpu.VMEM((len(blocks), ATTN_BLOCK, ATTN_BLOCK), F32)] * 2),
        compiler_params=_params(("parallel",)),
    )(proj, proj, proj, d_mix, cum_bc, cum_rows, lse_rows)


def _fox_gate_bwd(ds_sum, proj, bias_row):
    l = proj.shape[0]
    n_blocks = l // CHUNK

    def body(ds_ref, ff_ref, b_ref, dff_ref, db_ref):
        r = lax.broadcasted_iota(jnp.int32, (CHUNK, CHUNK), 0)
        cidx = lax.broadcasted_iota(jnp.int32, (CHUNK, CHUNK), 1)
        upper = jnp.where(cidx >= r, 1.0, 0.0).astype(F32)
        carry = jnp.zeros((1, 128), F32)
        db = jnp.zeros((1, 128), F32)
        for blk in reversed(range(n_blocks)):
            rows = slice(blk * CHUNK, (blk + 1) * CHUNK)
            key_sum = jnp.zeros((CHUNK, 128), F32)
            for h in range(N_HEADS):
                select = jnp.where(cidx == h, 1.0, 0.0).astype(F32)
                key_sum = key_sum + jnp.dot(ds_ref[h, rows, :], select, precision=lax.Precision.HIGHEST,
                                            preferred_element_type=F32)
            suffix = jnp.dot(upper, key_sum, precision=lax.Precision.HIGHEST, preferred_element_type=F32) + carry
            carry = suffix[0:1, :]
            _, dsig = _log_forget(ff_ref[rows, :], b_ref[...], _row_valid(blk, CHUNK))
            dff = -suffix * dsig
            dff_ref[rows, :] = dff.astype(dff_ref.dtype)
            db = db + jnp.sum(dff, axis=0, keepdims=True)
        db_ref[...] = db

    return pl.pallas_call(
        body, name="fox_gate_bwd",
        out_shape=(jax.ShapeDtypeStruct((l, 128), MXU_DTYPE), jax.ShapeDtypeStruct((1, 128), F32)),
        grid=(1,),
        in_specs=[pl.BlockSpec((N_HEADS, l, 128), lambda i: (0, 0, 0)),
                  pl.BlockSpec((l, 128), lambda i: (0, FF_TILE)),
                  pl.BlockSpec((1, 128), lambda i: (0, 0))],
        out_specs=(pl.BlockSpec((l, 128), lambda i: (0, 0)), pl.BlockSpec((1, 128), lambda i: (0, 0))),
        compiler_params=_params(("arbitrary",)),
    )(ds_sum, proj, bias_row)


def _conv(u, w, b):
    return b + w[0:1, :] * pltpu.roll(u, 2, 0) + w[1:2, :] * pltpu.roll(u, 1, 0) + w[2:3, :] * u


def _conv_act_fwd(u, conv_w, conv_b, d_ff):
    l = u.shape[0]
    tc = _divisor_tile(d_ff, 256, 128)
    nt = d_ff // tc

    def body(ug_ref, uv_ref, wg_ref, wv_ref, bg_ref, bv_ref, a_ref, y_ref):
        yg = _conv(ug_ref[...], wg_ref[...], bg_ref[...])
        yv = _conv(uv_ref[...], wv_ref[...], bv_ref[...])
        act = yg * _sigmoid(yg) * yv
        a_ref[...] = act.astype(a_ref.dtype)
        a_ref[0:CHUNK, :] = jnp.where(_row_valid(0, CHUNK), act[0:CHUNK, :], 0.0).astype(a_ref.dtype)
        y_ref[0] = yg.astype(y_ref.dtype)
        y_ref[1] = yv.astype(y_ref.dtype)

    return pl.pallas_call(
        body, name="conv_act_fwd",
        out_shape=(jax.ShapeDtypeStruct((l, d_ff), MXU_DTYPE), jax.ShapeDtypeStruct((2, l, d_ff), MXU_DTYPE)),
        grid=(nt,),
        in_specs=[pl.BlockSpec((l, tc), lambda j: (0, j)), pl.BlockSpec((l, tc), lambda j: (0, j + nt)),
                  pl.BlockSpec((8, tc), lambda j: (0, j)), pl.BlockSpec((8, tc), lambda j: (0, j + nt)),
                  pl.BlockSpec((1, tc), lambda j: (0, j)), pl.BlockSpec((1, tc), lambda j: (0, j + nt))],
        out_specs=(pl.BlockSpec((l, tc), lambda j: (0, j)), pl.BlockSpec((2, l, tc), lambda j: (0, 0, j))),
        compiler_params=_params(("parallel",)),
    )(u, u, conv_w, conv_w, conv_b, conv_b)


def _conv_act_bwd(u, y, conv_w, d_act, d_ff):
    l = u.shape[0]
    tc = _divisor_tile(d_ff, 256, 128)
    nt = d_ff // tc

    def body(ug_ref, uv_ref, y_ref, wg_ref, wv_ref, da_ref, du_ref, dwb_ref):
        valid = _row_valid(0, l)
        ug, uv = ug_ref[...], uv_ref[...]
        wg, wv = wg_ref[...], wv_ref[...]
        yg, yv = y_ref[0].astype(F32), y_ref[1].astype(F32)
        sig = _sigmoid(yg)
        da = jnp.where(valid, da_ref[...], 0.0)
        d_yv = da * (yg * sig)
        d_yg = da * yv * (sig * (1.0 + yg * (1.0 - sig)))
        for idx, (dy, uu, w) in enumerate(((d_yg, ug, wg), (d_yv, uv, wv))):
            du = w[2:3, :] * dy + w[1:2, :] * pltpu.roll(dy, l - 1, 0) + w[0:1, :] * pltpu.roll(dy, l - 2, 0)
            du_ref[idx] = du.astype(du_ref.dtype)
            du_ref[idx, 0:CHUNK, :] = jnp.where(_row_valid(0, CHUNK), du[0:CHUNK, :], 0.0).astype(du_ref.dtype)
            dwb_ref[idx, 0:1, :] = jnp.sum(dy * pltpu.roll(uu, 2, 0), axis=0, keepdims=True)
            dwb_ref[idx, 1:2, :] = jnp.sum(dy * pltpu.roll(uu, 1, 0), axis=0, keepdims=True)
            dwb_ref[idx, 2:3, :] = jnp.sum(dy * uu, axis=0, keepdims=True)
            dwb_ref[idx, 3:4, :] = jnp.sum(dy, axis=0, keepdims=True)
            dwb_ref[idx, 4:8, :] = jnp.zeros((4, tc), F32)

    return pl.pallas_call(
        body, name="conv_act_bwd",
        out_shape=(jax.ShapeDtypeStruct((2, l, d_ff), MXU_DTYPE), jax.ShapeDtypeStruct((2, 8, d_ff), F32)),
        grid=(nt,),
        in_specs=[pl.BlockSpec((l, tc), lambda j: (0, j)), pl.BlockSpec((l, tc), lambda j: (0, j + nt)),
                  pl.BlockSpec((2, l, tc), lambda j: (0, 0, j)),
                  pl.BlockSpec((8, tc), lambda j: (0, j)), pl.BlockSpec((8, tc), lambda j: (0, j + nt)),
                  pl.BlockSpec((l, tc), lambda j: (0, j))],
        out_specs=(pl.BlockSpec((2, l, tc), lambda j: (0, 0, j)), pl.BlockSpec((2, 8, tc), lambda j: (0, 0, j))),
        compiler_params=_params(("parallel",)),
    )(u, u, y, conv_w, conv_w, d_act)


def _adamw(w, g, m, v, name):
    shape = w.shape
    if w.ndim == 1:
        as2d = (1, shape[0])
    else:
        as2d = (int(np.prod(shape[:-1])), shape[-1])
    r, c = as2d
    tr = _divisor_tile(r, 256, 8)
    spec = pl.BlockSpec((tr, c), lambda i: (i, 0))

    def body(w_ref, g_ref, m_ref, v_ref, d_ref, nm_ref, nv_ref):
        d_ref[...], nm_ref[...], nv_ref[...] = _adamw_math(w_ref[...], g_ref[...], m_ref[...], v_ref[...])

    sds = jax.ShapeDtypeStruct(as2d, F32)
    outs = pl.pallas_call(
        body, name=name, out_shape=(sds, sds, sds), grid=(r // tr,),
        in_specs=[spec] * 4, out_specs=(spec,) * 3,
        compiler_params=_params(("parallel",)),
    )(w.reshape(as2d), g.reshape(as2d), m.reshape(as2d), v.reshape(as2d))
    return tuple(o.reshape(shape) for o in outs)


def _pad_rows(a, rows):
    return jnp.pad(a, ((0, rows - a.shape[0]), (0, 0)))


def kernel(x, meta_tokens, norm1_gain, w_in, b_forget, ret_norm_gain, w_out, norm2_gain, w_up, conv_w, conv_b, w_down, final_norm_gain, loss_target, m_meta_tokens, m_norm1_gain, m_w_in, m_b_forget, m_ret_norm_gain, m_w_out, m_norm2_gain, m_w_up, m_conv_w, m_conv_b, m_w_down, m_final_norm_gain, v_meta_tokens, v_norm1_gain, v_w_in, v_b_forget, v_ret_norm_gain, v_w_out, v_norm2_gain, v_w_up, v_conv_w, v_conv_b, v_w_down, v_final_norm_gain):
    seq, d = x.shape[1], x.shape[2]
    l = CHUNK + seq
    d_ff = w_down.shape[1] * N_DEV
    up_shard = w_up.shape[2]
    assert 4 * up_shard == d_ff and w_in.shape[2] == WIN_SHARD and d == 2 * GROUP
    dev = _device_index()
    mx, my, mc = _my_position()
    core = jnp.reshape(mc, (1,)).astype(jnp.int32)
    chip = jnp.reshape(2 * mx + my, (1,)).astype(jnp.int32)
    dev1 = jnp.reshape(dev, (1,)).astype(jnp.int32)

    small = jnp.concatenate([meta_tokens.reshape(-1, 128), conv_w[0].reshape(-1, 128)], axis=0)
    n_meta_rows = N_META * (d // N_DEV) // 128
    small_rows = small.shape[0]
    small_all = _all_gather(_pad_rows(small, -(-small_rows // 8) * 8), "gather_small")
    meta_full = jnp.transpose(small_all[:, :n_meta_rows].reshape(N_DEV, N_META, d // N_DEV), (1, 0, 2)).reshape(N_META, d)
    conv_w_full = _pad_rows(jnp.transpose(small_all[:, n_meta_rows:small_rows].reshape(N_DEV, 3, up_shard),
                                          (1, 0, 2)).reshape(3, 2 * d_ff), 8)
    to_rows = lambda t: jnp.pad(jnp.transpose(t[0]), ((0, WIN_ROWS - WIN_SHARD), (0, 0)))
    from_rows = lambda t: jnp.transpose(t[:WIN_SHARD])[None]
    w_in_rows = to_rows(w_in)
    out_rows = d // N_DEV
    mixer_rows = -(-(WIN_ROWS + out_rows) // 32) * 32
    mixer_shard = jnp.concatenate([w_in_rows.astype(WIRE_DTYPE), w_out[0].astype(WIRE_DTYPE),
                                   jnp.zeros((mixer_rows - WIN_ROWS - out_rows, d), WIRE_DTYPE)], axis=0)

    consts = _retention_consts(l)
    bias_row = jnp.pad(b_forget, ((0, 0), (0, 128 - N_HEADS)))
    h0, a = _embed_rmsnorm(x[0], meta_full, norm1_gain)
    mixer_blocks = _gather_ring(mixer_shard, dev1, a, "gather_w_in")
    start_up = _gather_start(w_up[0], dev1, mixer_blocks, "gather_w_up_start")
    w_in_full = _assemble_w_in(mixer_blocks).astype(MXU_DTYPE)
    proj = _mm_nt(a, w_in_full, F32, "mm_proj", after=start_up[4])
    ret_mix, ret_pre, ret_states = _retention_fwd(proj, ret_norm_gain, consts)
    cum_bc, cum_rows = _fox_prep(proj, bias_row)
    mix, lse_rows = _fox_fwd(proj, cum_bc, cum_rows, ret_mix)
    w_out_full = mixer_blocks[:, WIN_ROWS:WIN_ROWS + out_rows].reshape(d, d).astype(MXU_DTYPE)
    h1, cn = _out_proj_resid_rmsnorm(mix, w_out_full, h0, norm2_gain)
    w_up_blocks = _gather_finish(start_up, cn, "gather_w_up").astype(MXU_DTYPE)
    start_down = _gather_start(w_down[0], dev1, w_up_blocks, "gather_w_down_start")
    u = _mm_up(cn, w_up_blocks, start_down[4])
    pass_down = _gather_pass_start(start_down, u, "gather_w_down")
    act, conv_y = _conv_act_fwd(u, conv_w_full, conv_b + pass_down[4][0, 0], d_ff)
    w_down_full = _gather_pass_finish(pass_down, act, "gather_w_down").reshape(d_ff, d).astype(MXU_DTYPE)
    mlp_out = _mm_nn(act, w_down_full, F32, "mm_down", tm_cap=544, tk_cap=d_ff)
    d_h2, d_h2_b, dg_final, loss_part = _loss_head(h1, mlp_out, final_norm_gain.reshape(1, d), loss_target[0])

    gw_down = _mm_tn(act, d_h2_b, WIRE_DTYPE, "mm_gw_down", tm_cap=1408, tn_cap=1024)
    d2d_down = _reduce_scatter_d2d_start(gw_down.reshape(N_DEV, d_ff // N_DEV, d), d_h2, "rs_w_down")
    d_act = _mm_nt(d_h2_b, w_down_full, F32, "mm_d_act", after=d2d_down[4])
    rs_down = _reduce_scatter_ici_start(d2d_down, d_act, core, "rs_w_down")
    d_u, d_conv = _conv_act_bwd(u, conv_y, conv_w_full + rs_down[4][0, 0], d_act, d_ff)
    tm = _divisor_tile(l, 1088, 16)
    gw_up = _mm_gw_up(cn, d_u)
    d2d_up = _reduce_scatter_d2d_start(gw_up, d_act, "rs_w_up")
    d_cn = _mm_d_cn(d_u, w_up_blocks, d2d_up[4])
    rs_up = _reduce_scatter_ici_start(d2d_up, d_cn, core, "rs_w_up")
    d_h1, d_h1_b, dg_norm2 = _rmsnorm_bwd(d_h2, d_cn, h1, norm2_gain + rs_up[4][0, 0], "rmsnorm2_bwd", True)

    gw_out = _mm_tn(mix, d_h1_b, WIRE_DTYPE, "mm_gw_out")
    d2d_out = _reduce_scatter_d2d_start(gw_out.reshape(N_DEV, d // N_DEV, d), d_cn, "rs_w_out")
    d_mix = _mm_nt(d_h1_b, w_out_full, F32, "mm_d_mix", after=d2d_out[4])
    d_fq, d_fk, d_fv, ds_sum = _fox_bwd(proj, cum_bc, cum_rows, d_mix, lse_rows)
    d_ff_tile, db_forget_row = _fox_gate_bwd(ds_sum, proj, bias_row)
    d_ret, dg_ret = _retention_bwd(proj, ret_pre, ret_states, d_mix, ret_norm_gain, consts)
    rs_out = _reduce_scatter_ici_start(d2d_out, d_ret, core, "rs_w_out")
    d_proj = jnp.concatenate(
        [d_ret, d_fq, d_fk, d_fv, d_ff_tile, jnp.zeros((l, WIN_N - 7 * GROUP - 128), MXU_DTYPE)], axis=1)
    gw_in = _mm_tn(d_proj, a, WIRE_DTYPE, "mm_gw_in", tm_cap=1536, after=rs_out[4])
    rs_in = _reduce_scatter_start(_extract_w_in_windows(gw_in), core, "rs_w_in")
    d_a = _mm_nn(d_proj, w_in_full, F32, "mm_d_a", tm_cap=544, tn_cap=256, tk_cap=WIN_N, after=rs_in[4])
    d_front, d_tokens, dg_norm1 = _rmsnorm_bwd(d_h1, d_a, h0, norm1_gain + rs_in[4][0, 0], "rmsnorm1_bwd", False)
    grad_x = d_tokens[None]
    d_meta = d_front[PAD_ROWS:CHUNK]

    d_conv_w = jnp.concatenate([d_conv[0, 0:3], d_conv[1, 0:3]], axis=1)
    d_conv_b = jnp.concatenate([d_conv[0, 3:4], d_conv[1, 3:4]], axis=1)
    pieces = [loss_part[:, 0:1], dg_norm1, db_forget_row[:, 0:N_HEADS], dg_ret, dg_norm2, d_conv_b, dg_final,
              d_meta.reshape(1, -1), d_conv_w.reshape(1, -1)]
    sizes = [p.shape[1] for p in pieces]
    flat = jnp.concatenate(pieces, axis=1)
    padded = -(-flat.shape[1] // 1024) * 1024
    flat = jnp.pad(flat, ((0, 0), (0, padded - flat.shape[1]))).reshape(padded // 128, 128)
    small_ar = _small_all_reduce_start(flat, d_tokens, "all_reduce_small")

    lead = lambda outs: tuple(o[None] for o in outs)
    fin_down = lead(_reduce_scatter_finish(rs_down, small_ar[4], chip, w_down[0], m_w_down[0], v_w_down[0], "rs_w_down"))
    fin_up = lead(_reduce_scatter_finish(rs_up, fin_down[3], chip, w_up[0], m_w_up[0], v_w_up[0], "rs_w_up"))
    fin_out = lead(_reduce_scatter_finish(rs_out, fin_up[3], chip, w_out[0], m_w_out[0], v_w_out[0], "rs_w_out"))
    fin_in = tuple(from_rows(o) for o in _reduce_scatter_finish(
        rs_in, fin_out[3], chip, w_in_rows, to_rows(m_w_in), to_rows(v_w_in), "rs_w_in"))
    g_w_down, g_w_up, g_w_out, g_w_in = fin_down[0], fin_up[0], fin_out[0], fin_in[0]
    early = [fin_down[1:], fin_up[1:], fin_out[1:], fin_in[1:]]
    total = _small_all_reduce_finish(small_ar, fin_in[3], dev1, "all_reduce_small").reshape(1, padded)
    offs = np.concatenate([[0], np.cumsum(sizes)])
    take = lambda k: total[:, int(offs[k]):int(offs[k + 1])]
    loss = take(0).reshape(())
    g_norm1, g_bf, g_ret_gain, g_norm2 = take(1), take(2), take(3), take(4)
    g_conv_b, g_final = take(5), take(6).reshape(d)
    g_meta = lax.dynamic_slice(take(7).reshape(N_META, d), (jnp.int32(0), (dev * (d // N_DEV)).astype(jnp.int32)),
                               (N_META, d // N_DEV))
    g_conv_w = lax.dynamic_slice(take(8).reshape(3, 2 * d_ff), (jnp.int32(0), (dev * up_shard).astype(jnp.int32)),
                                 (3, up_shard))[None]

    weights = [meta_tokens, norm1_gain, w_in, b_forget, ret_norm_gain, w_out, norm2_gain, w_up, conv_w, conv_b,
               w_down, final_norm_gain]
    grads = [g_meta, g_norm1, g_w_in, g_bf, g_ret_gain, g_w_out, g_norm2, g_w_up, g_conv_w, g_conv_b, g_w_down,
             g_final]
    done = {"w_down": early[0], "w_up": early[1], "w_out": early[2], "w_in": early[3]}
    ms = [m_meta_tokens, m_norm1_gain, m_w_in, m_b_forget, m_ret_norm_gain, m_w_out, m_norm2_gain, m_w_up, m_conv_w,
          m_conv_b, m_w_down, m_final_norm_gain]
    vs = [v_meta_tokens, v_norm1_gain, v_w_in, v_b_forget, v_ret_norm_gain, v_w_out, v_norm2_gain, v_w_up, v_conv_w,
          v_conv_b, v_w_down, v_final_norm_gain]
    names = ["meta", "norm1", "w_in", "b_forget", "ret_gain", "w_out", "norm2", "w_up", "conv_w", "conv_b", "w_down",
             "final_gain"]
    replicated = ["norm1", "b_forget", "ret_gain", "norm2", "conv_b", "final_gain"]
    side_by_side = lambda arrays: jnp.concatenate([arrays[names.index(n)].reshape(1, -1) for n in replicated], axis=1)
    rep_out = _adamw(side_by_side(weights), total[:, int(offs[1]):int(offs[7])], side_by_side(ms), side_by_side(vs),
                     "adamw_replicated")
    rep_offs = np.cumsum([0] + [weights[names.index(n)].size for n in replicated])
    for k, n in enumerate(replicated):
        done[n] = tuple(o[:, int(rep_offs[k]):int(rep_offs[k + 1])].reshape(weights[names.index(n)].shape)
                        for o in rep_out)
    deltas, new_ms, new_vs = [], [], []
    for w, g, m, v, n in zip(weights, grads, ms, vs, names):
        dl, nm, nv = done[n] if n in done else _adamw(w, g, m, v, "adamw_" + n)
        deltas.append(dl)
        new_ms.append(nm)
        new_vs.append(nv)
    return (loss, grad_x, *grads, *deltas, *new_ms, *new_vs)
```

```python
import functools

import numpy as np
import jax
import jax.numpy as jnp
from jax import lax
from jax.experimental import pallas as pl
from jax.experimental.pallas import tpu as pltpu

F32 = jnp.float32
MXU_DTYPE = jnp.bfloat16
WIRE_DTYPE = jnp.bfloat16

N_DEV = 8
N_META = 16
CHUNK = 128
PAD_ROWS = CHUNK - N_META
N_HEADS = 8
HEAD_DIM = 128
GROUP = N_HEADS * HEAD_DIM
IN_DIM = 7 * GROUP + N_HEADS
WIN_SHARD = IN_DIM // N_DEV
WIN_ROWS = 912
WIN_BLOCK = 1024
WIN_STRIDE = 896
WIN_N = 7680
ROPE_BASE = 10000.0
NORM_EPS = 1e-6
NEG_BIG = -1e30
ADAM_LR, ADAM_B1, ADAM_B2, ADAM_EPS, ADAM_WD, ADAM_STEP = 0.001, 0.9, 0.999, 1e-08, 0.01, 10
VMEM_LIMIT = 52 * 1024 * 1024
MESH = pl.DeviceIdType.MESH
ANY = pl.BlockSpec(memory_space=pl.ANY)
VMEM_SPEC = pl.BlockSpec(memory_space=pltpu.VMEM)


def _params(sem=None):
    kw = {"vmem_limit_bytes": VMEM_LIMIT}
    if sem is not None:
        kw["dimension_semantics"] = sem
    return pltpu.CompilerParams(**kw)


def _divisor_tile(n, cap, unit):
    if n <= cap:
        return n
    best = None
    for t in range(unit, cap + 1, unit):
        if n % t == 0:
            best = t
    assert best is not None, (n, cap, unit)
    return best


def _my_position():
    return lax.axis_index("x"), lax.axis_index("y"), lax.axis_index("c")


def _device_index():
    x, y, c = _my_position()
    return 4 * x + 2 * y + c


def _all_gather(shard, name):
    r, c = shard.shape

    def body(x_ref, out_ref, send_sems, recv_sems, local_sem):
        mx, my, mc = _my_position()
        me, sibling = (mx, my, mc), (mx, my, 1 - mc)
        chips = [(1 - mx, my), (mx, 1 - my), (1 - mx, 1 - my)]

        def slot(px, py, pc):
            return out_ref.at[4 * px + 2 * py + pc]

        def copy(k, block, to, src=None):
            return pltpu.make_async_remote_copy(
                src_ref=slot(*block) if src is None else src, dst_ref=slot(*block),
                send_sem=send_sems.at[k], recv_sem=recv_sems.at[k], device_id=to, device_id_type=MESH)

        mine = pltpu.make_async_copy(x_ref, slot(*me), local_sem)
        mine.start()
        first = [copy(0, me, sibling, src=x_ref)]
        first += [copy(1 + j, me, (*chip, mc), src=x_ref) for j, chip in enumerate(chips)]
        for cp in first:
            cp.start()
        passed = [copy(4 + j, (*chip, mc), sibling) for j, chip in enumerate(chips)]
        for j, chip in enumerate(chips):
            copy(1 + j, (*chip, mc), me).wait_recv()
            passed[j].start()
        copy(0, sibling, me).wait_recv()
        for j, chip in enumerate(chips):
            copy(4 + j, (*chip, 1 - mc), me).wait_recv()
        for cp in first + passed:
            cp.wait_send()
        mine.wait()

    return pl.pallas_call(
        body, name=name,
        out_shape=jax.ShapeDtypeStruct((N_DEV, r, c), shard.dtype),
        in_specs=[ANY], out_specs=ANY,
        scratch_shapes=[pltpu.SemaphoreType.DMA((7,)), pltpu.SemaphoreType.DMA((7,)), pltpu.SemaphoreType.DMA],
    )(shard)


HBM_SPEC = pl.BlockSpec(memory_space=pltpu.HBM)
SEM_SPEC = pl.BlockSpec(memory_space=pltpu.SEMAPHORE)
DATAFLOW_EFFECT = pltpu.SideEffectType.DATAFLOW_SIDE_EFFECTING


def _in_hbm(a):
    return pltpu.with_memory_space_constraint(a, pltpu.HBM)


def _split_start(src, land, make_copies, n_copies, after, name):
    if isinstance(land, tuple):
        land = lax.empty(land, src.dtype)
    land_shape = land.shape
    def body(src_ref, land_ref, after_ref, send_sems, recv_sems, src_thru, land_thru, token):
        for cp in make_copies(src_ref, land_ref, send_sems, recv_sems):
            cp.start()
        token[...] = jnp.zeros_like(token)

    return pl.pallas_call(
        body, name=name,
        out_shape=(pltpu.SemaphoreType.DMA((n_copies,)), pltpu.SemaphoreType.DMA((n_copies,)),
                   pltpu.HBM(src.shape, src.dtype), pltpu.HBM(land_shape, land.dtype),
                   jax.ShapeDtypeStruct((8, 128), F32)),
        in_specs=(HBM_SPEC, HBM_SPEC, ANY), out_specs=(SEM_SPEC, SEM_SPEC, HBM_SPEC, HBM_SPEC, VMEM_SPEC),
        input_output_aliases={0: 2, 1: 3},
        compiler_params=pltpu.CompilerParams(has_side_effects=DATAFLOW_EFFECT),
    )(_in_hbm(src), _in_hbm(land), after)


def _split_wait(started, after, make_copies, name):
    send_sems, recv_sems, src_thru, land_thru, _ = started

    def body(src_ref, land_ref, send_sems_ref, recv_sems_ref, after_ref, src_dead, land_out):
        for cp in make_copies(src_ref, land_ref, send_sems_ref, recv_sems_ref):
            cp.wait_send()
            cp.wait_recv()

    return pl.pallas_call(
        body, name=name,
        out_shape=(pltpu.HBM(src_thru.shape, src_thru.dtype), pltpu.HBM(land_thru.shape, land_thru.dtype)),
        in_specs=(HBM_SPEC, HBM_SPEC, SEM_SPEC, SEM_SPEC, ANY), out_specs=(HBM_SPEC, HBM_SPEC),
        input_output_aliases={0: 0, 1: 1},
        compiler_params=pltpu.CompilerParams(has_side_effects=DATAFLOW_EFFECT),
    )(src_thru, land_thru, send_sems, recv_sems, after)


def _gather_copies(x_ref, land_ref, send_sems, recv_sems):
    mx, my, mc = _my_position()
    me = 4 * mx + 2 * my + mc
    targets = [(mx, my, 1 - mc), (1 - mx, my, mc), (mx, 1 - my, mc), (1 - mx, 1 - my, mc)]
    return [pltpu.make_async_remote_copy(
        src_ref=land_ref.at[me], dst_ref=land_ref.at[me], send_sem=send_sems.at[k], recv_sem=recv_sems.at[k],
        device_id=t, device_id_type=MESH) for k, t in enumerate(targets)]


def _own_slot(shard, dev, name):
    r, c = shard.shape
    tr = _divisor_tile(r, 640, 16)

    def body(s_ref, x_ref, o_ref):
        o_ref[...] = x_ref[...].astype(o_ref.dtype)

    return pl.pallas_call(
        body, name=name,
        out_shape=jax.ShapeDtypeStruct((N_DEV, r, c), WIRE_DTYPE),
        grid_spec=pltpu.PrefetchScalarGridSpec(
            num_scalar_prefetch=1, grid=(r // tr,),
            in_specs=[pl.BlockSpec((tr, c), lambda i, s: (i, 0))],
            out_specs=pl.BlockSpec((None, tr, c), lambda i, s: (s[0], i, 0))),
        compiler_params=_params(("parallel",)),
    )(dev, shard)


def _gather_start(shard, dev, after, name):
    return _split_start(jnp.zeros((8, 128), F32), _own_slot(shard, dev, name + "_own"), _gather_copies, 4, after, name)


def _gather_ring(shard, dev, after, name):
    r, c = shard.shape
    half = r // 2
    assert half % 16 == 0

    def body(x_ref, after_ref, land_in, land_ref, send_sems, recv_sems):
        mx, my, mc = _my_position()
        sibling, x_nbr, y_nbr = (mx, my, 1 - mc), (1 - mx, my, mc), (mx, 1 - my, mc)
        first, second = pl.ds(0, half), pl.ds(half, half)

        def slot(px, py, pc):
            return land_ref.at[4 * px + 2 * py + pc]

        def copy(k, src, dst, to):
            return pltpu.make_async_remote_copy(src_ref=src, dst_ref=dst, send_sem=send_sems.at[k],
                                                recv_sem=recv_sems.at[k], device_id=to, device_id_type=MESH)

        def arrived(k, dst):
            copy(k, dst, dst, sibling).wait_recv()

        mine = slot(mx, my, mc)
        from_x, from_y, from_d = slot(1 - mx, my, mc), slot(mx, 1 - my, mc), slot(1 - mx, 1 - my, mc)
        sent = [copy(0, x_ref, mine, sibling), copy(1, x_ref, mine, x_nbr), copy(2, x_ref, mine, y_nbr)]
        for cp in sent:
            cp.start()

        def send(k, src, to):
            cp = copy(k, src, src, to)
            cp.start()
            sent.append(cp)

        arrived(1, from_x)
        send(3, from_x.at[first], y_nbr)
        send(5, from_x, sibling)
        arrived(2, from_y)
        send(4, from_y.at[second], x_nbr)
        send(6, from_y, sibling)
        arrived(3, from_d.at[first])
        send(7, from_d.at[first], sibling)
        arrived(4, from_d.at[second])
        send(8, from_d.at[second], sibling)
        arrived(0, slot(mx, my, 1 - mc))
        arrived(5, slot(1 - mx, my, 1 - mc))
        arrived(6, slot(mx, 1 - my, 1 - mc))
        arrived(7, slot(1 - mx, 1 - my, 1 - mc).at[first])
        arrived(8, slot(1 - mx, 1 - my, 1 - mc).at[second])
        for cp in sent:
            cp.wait_send()

    land = _own_slot(shard, dev, name + "_own")
    return pl.pallas_call(
        body, name=name,
        out_shape=jax.ShapeDtypeStruct(land.shape, land.dtype),
        in_specs=[ANY, ANY, ANY], out_specs=ANY,
        input_output_aliases={2: 0},
        scratch_shapes=[pltpu.SemaphoreType.DMA((9,)), pltpu.SemaphoreType.DMA((9,))],
    )(shard, after, land)


def _pass_copies(unused_ref, land_ref, send_sems, recv_sems):
    mx, my, mc = _my_position()
    chips = [(1 - mx, my), (mx, 1 - my), (1 - mx, 1 - my)]
    return [pltpu.make_async_remote_copy(
        src_ref=land_ref.at[4 * cx + 2 * cy + mc], dst_ref=land_ref.at[4 * cx + 2 * cy + mc],
        send_sem=send_sems.at[j], recv_sem=recv_sems.at[j],
        device_id=(mx, my, 1 - mc), device_id_type=MESH) for j, (cx, cy) in enumerate(chips)]


def _gather_pass_start(started, after, name):
    _, land = _split_wait(started, after, _gather_copies, name + "_wait")
    return _split_start(jnp.zeros((8, 128), F32), land, _pass_copies, 3, after, name + "_pass_start")


def _gather_pass_finish(pass_started, after, name):
    return _split_wait(pass_started, after, _pass_copies, name + "_pass_wait")[1]


def _gather_finish(started, after, name):
    _, land = _split_wait(started, after, _gather_copies, name + "_wait")

    def body(land_in, land_ref, send_sems, recv_sems):
        mx, my, mc = _my_position()
        chips = [(1 - mx, my), (mx, 1 - my), (1 - mx, 1 - my)]
        copies = [pltpu.make_async_remote_copy(
            src_ref=land_ref.at[4 * cx + 2 * cy + mc], dst_ref=land_ref.at[4 * cx + 2 * cy + mc],
            send_sem=send_sems.at[j], recv_sem=recv_sems.at[j],
            device_id=(mx, my, 1 - mc), device_id_type=MESH) for j, (cx, cy) in enumerate(chips)]
        for cp in copies:
            cp.start()
        for j, (cx, cy) in enumerate(chips):
            copies[j].wait_send()
            pltpu.make_async_remote_copy(
                src_ref=land_ref.at[4 * cx + 2 * cy + 1 - mc], dst_ref=land_ref.at[4 * cx + 2 * cy + 1 - mc],
                send_sem=send_sems.at[j], recv_sem=recv_sems.at[j],
                device_id=(mx, my, 1 - mc), device_id_type=MESH).wait_recv()

    return pl.pallas_call(
        body, name=name + "_pass",
        out_shape=jax.ShapeDtypeStruct(land.shape, land.dtype),
        in_specs=[ANY], out_specs=ANY,
        input_output_aliases={0: 0},
        scratch_shapes=[pltpu.SemaphoreType.DMA((3,)), pltpu.SemaphoreType.DMA((3,))],
    )(land)


def _chip_copies(p_ref, land_ref, send_sems, recv_sems):
    mx, my, mc = _my_position()
    chips = [(1 - mx, my), (mx, 1 - my), (1 - mx, 1 - my)]
    return [pltpu.make_async_remote_copy(
        src_ref=p_ref.at[2 * cx + cy], dst_ref=land_ref.at[j], send_sem=send_sems.at[j], recv_sem=recv_sems.at[j],
        device_id=(cx, cy, mc), device_id_type=MESH) for j, (cx, cy) in enumerate(chips)]


def _reduce_scatter_start(g, core, name):
    pair = _pair_sum(g, _exchange_sibling(g, name + "_d2d"), core, name + "_pairsum")
    return _split_start(pair, (3,) + pair.shape[1:], _chip_copies, 3, g, name + "_ici_start")


def _sibling_copies(g_ref, land_ref, send_sems, recv_sems):
    mx, my, mc = _my_position()
    return [pltpu.make_async_remote_copy(
        src_ref=g_ref.at[2 * k + (1 - mc)], dst_ref=land_ref.at[k], send_sem=send_sems.at[k], recv_sem=recv_sems.at[k],
        device_id=(mx, my, 1 - mc), device_id_type=MESH) for k in range(4)]


def _reduce_scatter_d2d_start(g, after, name):
    return _split_start(g, (4,) + g.shape[1:], _sibling_copies, 4, after, name + "_d2d_start")


def _reduce_scatter_ici_start(d2d_started, after, core, name):
    g, from_sibling = _split_wait(d2d_started, after, _sibling_copies, name + "_d2d_wait")
    pair = _pair_sum(g, from_sibling, core, name + "_pairsum")
    return _split_start(pair, (3,) + pair.shape[1:], _chip_copies, 3, g, name + "_ici_start")


def _reduce_scatter_finish(started, after, chip, w, m, v, name):
    pair, from_chips = _split_wait(started, after, _chip_copies, name + "_ici_wait")
    return _final_sum_adamw(pair, from_chips, chip, w, m, v, name + "_sum_adamw")


def _exchange_sibling(g, name):
    _, r, c = g.shape

    def body(g_ref, out_ref, send_sems, recv_sems):
        mx, my, mc = _my_position()
        copies = [
            pltpu.make_async_remote_copy(
                src_ref=g_ref.at[2 * k + (1 - mc)], dst_ref=out_ref.at[k],
                send_sem=send_sems.at[k], recv_sem=recv_sems.at[k],
                device_id=(mx, my, 1 - mc), device_id_type=MESH)
            for k in range(4)]
        for cp in copies:
            cp.start()
        for cp in copies:
            cp.wait()

    return pl.pallas_call(
        body, name=name,
        out_shape=jax.ShapeDtypeStruct((4, r, c), g.dtype),
        in_specs=[ANY], out_specs=ANY,
        scratch_shapes=[pltpu.SemaphoreType.DMA((4,)), pltpu.SemaphoreType.DMA((4,))],
    )(g)


def _pair_sum(g, recv, core, name):
    _, r, c = g.shape
    tr = _divisor_tile(r, 512, 16)

    def body(s_ref, g_ref, r_ref, o_ref):
        o_ref[...] = (g_ref[...].astype(F32) + r_ref[...].astype(F32)).astype(o_ref.dtype)

    return pl.pallas_call(
        body, name=name,
        out_shape=jax.ShapeDtypeStruct((4, r, c), g.dtype),
        grid_spec=pltpu.PrefetchScalarGridSpec(
            num_scalar_prefetch=1, grid=(4, r // tr),
            in_specs=[pl.BlockSpec((None, tr, c), lambda k, i, s: (2 * k + s[0], i, 0)),
                      pl.BlockSpec((None, tr, c), lambda k, i, s: (k, i, 0))],
            out_specs=pl.BlockSpec((None, tr, c), lambda k, i, s: (k, i, 0))),
        compiler_params=_params(("parallel", "parallel")),
    )(core, g, recv)


def _adamw_math(w, g, m, v):
    nm = ADAM_B1 * m + (1.0 - ADAM_B1) * g
    nv = ADAM_B2 * v + (1.0 - ADAM_B2) * (g * g)
    m_hat = nm / (1.0 - ADAM_B1 ** ADAM_STEP)
    v_hat = nv / (1.0 - ADAM_B2 ** ADAM_STEP)
    return -ADAM_LR * (m_hat / (jnp.sqrt(v_hat) + ADAM_EPS) + ADAM_WD * w), nm, nv


def _final_sum_adamw(p, recv, chip, w, m, v, name):
    _, r, c = p.shape
    tr = _divisor_tile(r, 256, 16)
    tile = lambda: pl.BlockSpec((tr, c), lambda i, s: (i, 0))

    def body(s_ref, p_ref, r_ref, w_ref, m_ref, v_ref, g_ref, d_ref, nm_ref, nv_ref):
        g = p_ref[...].astype(F32)
        for j in range(3):
            g = g + r_ref[j].astype(F32)
        g_ref[...] = g
        d_ref[...], nm_ref[...], nv_ref[...] = _adamw_math(w_ref[...], g, m_ref[...], v_ref[...])

    sds = jax.ShapeDtypeStruct((r, c), F32)
    return pl.pallas_call(
        body, name=name,
        out_shape=(sds, sds, sds, sds),
        grid_spec=pltpu.PrefetchScalarGridSpec(
            num_scalar_prefetch=1, grid=(r // tr,),
            in_specs=[pl.BlockSpec((None, tr, c), lambda i, s: (s[0], i, 0)),
                      pl.BlockSpec((3, tr, c), lambda i, s: (0, i, 0)), tile(), tile(), tile()],
            out_specs=(tile(), tile(), tile(), tile())),
        compiler_params=_params(("parallel",)),
    )(chip, p, recv, w, m, v)


def _all_to_all_copies(v_ref, land_ref, send_sems, recv_sems):
    mx, my, mc = _my_position()
    me = 4 * mx + 2 * my + mc
    copies = []
    for rel in range(1, N_DEV):
        bx, by, bc = (rel >> 2) & 1, (rel >> 1) & 1, rel & 1
        target = (1 - mx if bx else mx, 1 - my if by else my, 1 - mc if bc else mc)
        copies.append(pltpu.make_async_remote_copy(
            src_ref=v_ref, dst_ref=land_ref.at[me], send_sem=send_sems.at[rel - 1], recv_sem=recv_sems.at[rel - 1],
            device_id=target, device_id_type=MESH))
    return copies


def _small_all_reduce_start(v, after, name):
    return _split_start(v, (N_DEV,) + v.shape, _all_to_all_copies, N_DEV - 1, after, name + "_start")


def _small_all_reduce_finish(started, after, dev, name):
    v, land = _split_wait(started, after, _all_to_all_copies, name + "_wait")
    rows = v.shape[0]

    def body(me_ref, v_ref, land_ref, o_ref):
        for j in range(N_DEV):
            @pl.when(me_ref[0] == j)
            def _():
                o_ref[...] = v_ref[...] if j == 0 else o_ref[...] + v_ref[...]

            @pl.when(me_ref[0] != j)
            def _():
                o_ref[...] = land_ref[j] if j == 0 else o_ref[...] + land_ref[j]

    return pl.pallas_call(
        body, name=name + "_sum",
        out_shape=jax.ShapeDtypeStruct((rows, 128), F32),
        grid_spec=pltpu.PrefetchScalarGridSpec(
            num_scalar_prefetch=1, grid=(1,),
            in_specs=[pl.BlockSpec((rows, 128), lambda i, s: (0, 0)),
                      pl.BlockSpec((N_DEV, rows, 128), lambda i, s: (0, 0, 0))],
            out_specs=pl.BlockSpec((rows, 128), lambda i, s: (0, 0))),
        compiler_params=_params(("arbitrary",)),
    )(dev, v, land)


def _assemble_w_in(blocks):
    rows, d = WIN_ROWS, blocks.shape[2]
    tc = _divisor_tile(d, 256, 128)
    n_tiles = WIN_N // 128
    last = (N_DEV * WIN_STRIDE) // 128

    def body(b_ref, o_ref):
        win = []
        for i in range(N_DEV):
            w = jnp.concatenate([b_ref[i].astype(F32), jnp.zeros((WIN_BLOCK - rows, tc), F32)], axis=0)
            win.append(pltpu.roll(w, i, 0) if i else w)
        for t in range(n_tiles):
            if t > last:
                o_ref[t * 128:(t + 1) * 128, :] = jnp.zeros((128, tc), o_ref.dtype)
                continue
            i = min(t // 7, N_DEV - 1)
            k = t - 7 * i
            val = win[i][k * 128:(k + 1) * 128, :]
            if k == 0 and i >= 1:
                val = val + win[i - 1][7 * 128:8 * 128, :]
            o_ref[t * 128:(t + 1) * 128, :] = val.astype(o_ref.dtype)

    return pl.pallas_call(
        body, name="assemble_w_in",
        out_shape=jax.ShapeDtypeStruct((WIN_N, d), blocks.dtype),
        grid=(d // tc,),
        in_specs=[pl.BlockSpec((N_DEV, rows, tc), lambda j: (0, 0, j))],
        out_specs=pl.BlockSpec((WIN_N, tc), lambda j: (0, j)),
        compiler_params=_params(("parallel",)),
    )(blocks)


def _extract_w_in_windows(g):
    _, d = g.shape
    tc = _divisor_tile(d, 256, 128)

    def body(g_ref, o_ref):
        for j in range(N_DEV):
            w = g_ref[WIN_STRIDE * j:WIN_STRIDE * j + WIN_BLOCK, :].astype(F32)
            w = pltpu.roll(w, WIN_BLOCK - j, 0) if j else w
            o_ref[j] = w[0:WIN_ROWS, :].astype(o_ref.dtype)

    return pl.pallas_call(
        body, name="extract_w_in_windows",
        out_shape=jax.ShapeDtypeStruct((N_DEV, WIN_ROWS, d), g.dtype),
        grid=(d // tc,),
        in_specs=[pl.BlockSpec((WIN_N, tc), lambda j: (0, j))],
        out_specs=pl.BlockSpec((N_DEV, WIN_ROWS, tc), lambda j: (0, 0, j)),
        compiler_params=_params(("parallel",)),
    )(g)


def _mm(a, b, *, a_spec, b_spec, o_spec, out_shape, grid, contract, nk, name, after=None):
    dn = (((contract[0],), (contract[1],)), ((), ()))
    tm, tn = o_spec.block_shape[-2:]
    behind = [] if after is None else [after]

    def body(a_ref, b_ref, *rest):
        o_ref, *scratch = rest[len(behind):]
        part = lax.dot_general(a_ref[...], b_ref[...], dn, preferred_element_type=F32)
        if nk == 1:
            o_ref[...] = part.astype(o_ref.dtype)
            return
        acc = scratch[0]
        k = pl.program_id(2)

        @pl.when(k == 0)
        def _():
            acc[...] = part

        @pl.when(k > 0)
        def _():
            acc[...] += part

        @pl.when(k == nk - 1)
        def _():
            o_ref[...] = acc[...].astype(o_ref.dtype)

    return pl.pallas_call(
        body, name=name, out_shape=out_shape, grid=grid,
        in_specs=[a_spec, b_spec] + [ANY] * len(behind), out_specs=o_spec,
        scratch_shapes=[] if nk == 1 else [pltpu.VMEM((tm, tn), F32)],
        compiler_params=_params(("parallel", "parallel", "arbitrary")),
    )(a, b, *behind)


def _mm_nn(a, b, out_dtype, name, tm_cap=1088, tn_cap=512, tk_cap=2048, after=None):
    m, k = a.shape
    _, n = b.shape
    tm, tn, tk = _divisor_tile(m, tm_cap, 16), _divisor_tile(n, tn_cap, 128), _divisor_tile(k, tk_cap, 128)
    return _mm(a, b,
               a_spec=pl.BlockSpec((tm, tk), lambda i, j, kk: (i, kk)),
               b_spec=pl.BlockSpec((tk, tn), lambda i, j, kk: (kk, j)),
               o_spec=pl.BlockSpec((tm, tn), lambda i, j, kk: (i, j)),
               out_shape=jax.ShapeDtypeStruct((m, n), out_dtype),
               grid=(m // tm, n // tn, k // tk), contract=(1, 0), nk=k // tk, name=name, after=after)


def _mm_nt(a, b, out_dtype, name, tm_cap=1088, tn_cap=512, tk_cap=2048, after=None):
    m, k = a.shape
    n, _ = b.shape
    tm, tn, tk = _divisor_tile(m, tm_cap, 16), _divisor_tile(n, tn_cap, 128), _divisor_tile(k, tk_cap, 128)
    return _mm(a, b,
               a_spec=pl.BlockSpec((tm, tk), lambda i, j, kk: (i, kk)),
               b_spec=pl.BlockSpec((tn, tk), lambda i, j, kk: (j, kk)),
               o_spec=pl.BlockSpec((tm, tn), lambda i, j, kk: (i, j)),
               out_shape=jax.ShapeDtypeStruct((m, n), out_dtype),
               grid=(m // tm, n // tn, k // tk), contract=(1, 1), nk=k // tk, name=name, after=after)


def _mm_tn(a, b, out_dtype, name, tm_cap=1024, tn_cap=512, after=None):
    l, m = a.shape
    _, n = b.shape
    tm, tn = _divisor_tile(m, tm_cap, 128), _divisor_tile(n, tn_cap, 128)
    return _mm(a, b,
               a_spec=pl.BlockSpec((l, tm), lambda i, j, kk: (0, i)),
               b_spec=pl.BlockSpec((l, tn), lambda i, j, kk: (0, j)),
               o_spec=pl.BlockSpec((tm, tn), lambda i, j, kk: (i, j)),
               out_shape=jax.ShapeDtypeStruct((m, n), out_dtype),
               grid=(m // tm, n // tn, 1), contract=(0, 0), nk=1, name=name, after=after)


def _pair_split(shard):
    left = shard % ATTN_BLOCK
    assert left in (0, CHUNK) and shard > left
    return shard - left, left


def _mm_up(cn, w_up_blocks, after):
    l, d = cn.shape
    n, _, shard = w_up_blocks.shape
    main, left = _pair_split(shard)
    tm = _divisor_tile(l, 544, 16)

    def body(a_ref, b_ref, after_ref, o_ref):
        a = a_ref[...]
        for s in range(2):
            o_ref[:, s * shard:s * shard + main] = _dot(a, b_ref[s, :, 0:main])
        if left:
            tail = _dot(a, jnp.concatenate([b_ref[0, :, main:], b_ref[1, :, main:]], axis=1))
            o_ref[:, main:shard] = tail[:, 0:left]
            o_ref[:, shard + main:2 * shard] = tail[:, left:]

    return pl.pallas_call(
        body, name="mm_up", out_shape=jax.ShapeDtypeStruct((l, n * shard), F32), grid=(l // tm, n // 2),
        in_specs=[pl.BlockSpec((tm, d), lambda i, j: (i, 0)),
                  pl.BlockSpec((2, d, shard), lambda i, j: (j, 0, 0)), ANY],
        out_specs=pl.BlockSpec((tm, 2 * shard), lambda i, j: (i, j)),
        compiler_params=_params(("parallel", "parallel")),
    )(cn, w_up_blocks, after)


def _mm_gw_up(cn, d_u):
    l, d = cn.shape
    _, _, d_ff = d_u.shape
    shard = 2 * d_ff // N_DEV
    pairs_per_half = d_ff // (2 * shard)
    tm = _divisor_tile(d, 512, 128)

    def body(a_ref, b_ref, o_ref):
        res = _dot_tn(a_ref[...], b_ref[...])
        o_ref[0] = res[:, 0:shard].astype(o_ref.dtype)
        o_ref[1] = res[:, shard:].astype(o_ref.dtype)

    return pl.pallas_call(
        body, name="mm_gw_up", out_shape=jax.ShapeDtypeStruct((N_DEV, d, shard), WIRE_DTYPE),
        grid=(d // tm, N_DEV // 2),
        in_specs=[pl.BlockSpec((l, tm), lambda i, j: (0, i)),
                  pl.BlockSpec((None, l, 2 * shard), lambda i, j: (j // pairs_per_half, 0, j % pairs_per_half))],
        out_specs=pl.BlockSpec((2, tm, shard), lambda i, j: (j, i, 0)),
        compiler_params=_params(("parallel", "parallel")),
    )(cn, d_u)


def _mm_d_cn(d_u, w_up_blocks, after):
    _, l, d_ff = d_u.shape
    n, d, shard = w_up_blocks.shape
    per = d_ff // shard
    main, left = _pair_split(shard)
    tm, tn = _divisor_tile(l, 544, 16), _divisor_tile(d, 256, 128)

    def body(a_ref, b_ref, after_ref, o_ref):
        acc = None
        for k in range(0, n, 2):
            half, c0 = k // per, (k % per) * shard
            parts = [_dot_nt(a_ref[half, :, c0 + s * shard:c0 + s * shard + main], b_ref[k + s, :, 0:main])
                     for s in range(2)]
            if left:
                a_tail = jnp.concatenate([a_ref[half, :, c0 + s * shard + main:c0 + (s + 1) * shard] for s in range(2)],
                                         axis=1)
                b_tail = jnp.concatenate([b_ref[k + s, :, main:] for s in range(2)], axis=1)
                parts.append(_dot_nt(a_tail, b_tail))
            for part in parts:
                acc = part if acc is None else acc + part
        o_ref[...] = acc

    return pl.pallas_call(
        body, name="mm_d_cn", out_shape=jax.ShapeDtypeStruct((l, d), F32), grid=(l // tm, d // tn),
        in_specs=[pl.BlockSpec((2, tm, d_ff), lambda i, j: (0, i, 0)),
                  pl.BlockSpec((n, tn, shard), lambda i, j: (0, j, 0)), ANY],
        out_specs=pl.BlockSpec((tm, tn), lambda i, j: (i, j)),
        compiler_params=_params(("parallel", "parallel")),
    )(d_u, w_up_blocks, after)


def _row_tile(l):
    return _divisor_tile(l, 544, 8)


def _rms(x, gain):
    return (x * lax.rsqrt(jnp.mean(x * x, axis=-1, keepdims=True) + NORM_EPS) * gain).astype(MXU_DTYPE)


def _embed_rmsnorm(x, meta, gain):
    seq, d = x.shape
    l = CHUNK + seq
    row = pl.BlockSpec((CHUNK, d), lambda i: (i, 0))

    def body(x_ref, m_ref, g_ref, h_ref, n_ref):
        @pl.when(pl.program_id(0) == 0)
        def _():
            h_ref[...] = jnp.concatenate([jnp.zeros((PAD_ROWS, d), F32), m_ref[...]], axis=0)

        @pl.when(pl.program_id(0) > 0)
        def _():
            h_ref[...] = x_ref[...]

        n_ref[...] = _rms(h_ref[...], g_ref[...])

    return pl.pallas_call(
        body, name="embed_rmsnorm1",
        out_shape=(jax.ShapeDtypeStruct((l, d), F32), jax.ShapeDtypeStruct((l, d), MXU_DTYPE)),
        grid=(l // CHUNK,),
        in_specs=[pl.BlockSpec((CHUNK, d), lambda i: (jnp.maximum(i - 1, 0), 0)),
                  pl.BlockSpec((N_META, d), lambda i: (0, 0)), pl.BlockSpec((1, d), lambda i: (0, 0))],
        out_specs=(row, row),
        compiler_params=_params(("parallel",)),
    )(x, meta, gain)


def _out_proj_resid_rmsnorm(mix, w_out, h0, gain):
    l, d = h0.shape
    tm = _divisor_tile(l, 272, 16)
    row = pl.BlockSpec((tm, d), lambda i: (i, 0))

    def body(a_ref, b_ref, h_ref, g_ref, s_ref, n_ref):
        x = h_ref[...] + _dot(a_ref[...], b_ref[...])
        s_ref[...] = x
        n_ref[...] = _rms(x, g_ref[...])

    return pl.pallas_call(
        body, name="mm_out_resid_rmsnorm2",
        out_shape=(jax.ShapeDtypeStruct((l, d), F32), jax.ShapeDtypeStruct((l, d), MXU_DTYPE)),
        grid=(l // tm,),
        in_specs=[row, pl.BlockSpec((d, d), lambda i: (0, 0)), row, pl.BlockSpec((1, d), lambda i: (0, 0))],
        out_specs=(row, row),
        compiler_params=_params(("parallel",)),
    )(mix, w_out, h0, gain)


def _rmsnorm_bwd(d_res, d_normed, x, gain, name, with_mxu_copy):
    l, d = x.shape
    tr = _row_tile(l) if with_mxu_copy else CHUNK
    row = pl.BlockSpec((tr, d), lambda i: (i, 0))
    vec = pl.BlockSpec((1, d), lambda i: (0, 0))

    def body(dres_ref, dn_ref, x_ref, g_ref, dx_ref, other_ref, dg_ref):
        i = pl.program_id(0)
        xv = x_ref[...]
        r = lax.rsqrt(jnp.mean(xv * xv, axis=-1, keepdims=True) + NORM_EPS)
        xh = xv * r
        dn = dn_ref[...]
        dxh = dn * g_ref[...]
        dx = dres_ref[...] + r * (dxh - xh * jnp.mean(dxh * xh, axis=-1, keepdims=True))
        if with_mxu_copy:
            dx_ref[...] = dx
            other_ref[...] = dx.astype(MXU_DTYPE)
        else:
            @pl.when(i == 0)
            def _():
                dx_ref[...] = dx

            @pl.when(i > 0)
            def _():
                other_ref[...] = dx

        @pl.when(i == 0)
        def _():
            dg_ref[...] = jnp.zeros_like(dg_ref)

        dg_ref[...] += jnp.sum(dn * xh, axis=0, keepdims=True)

    if with_mxu_copy:
        outs = [jax.ShapeDtypeStruct((l, d), F32), jax.ShapeDtypeStruct((l, d), MXU_DTYPE)]
        specs = [row, row]
    else:
        outs = [jax.ShapeDtypeStruct((CHUNK, d), F32), jax.ShapeDtypeStruct((l - CHUNK, d), F32)]
        specs = [pl.BlockSpec((CHUNK, d), lambda i: (0, 0)), pl.BlockSpec((CHUNK, d), lambda i: (jnp.maximum(i - 1, 0), 0))]
    outs.append(jax.ShapeDtypeStruct((1, d), F32))
    specs.append(vec)
    return pl.pallas_call(body, name=name, out_shape=tuple(outs), grid=(l // tr,),
                          in_specs=[row, row, row, vec], out_specs=tuple(specs),
                          compiler_params=_params(("arbitrary",)))(d_res, d_normed, x, gain)


def _loss_head(h1, mlp_out, gain, target):
    l, d = h1.shape
    n_blocks = l // CHUNK
    row = pl.BlockSpec((CHUNK, d), lambda i: (i, 0))
    vec = pl.BlockSpec((1, d), lambda i: (0, 0))
    tgt = pl.BlockSpec((CHUNK, d), lambda i: (jnp.maximum(i - 1, 0), 0))

    def body(h_ref, m_ref, g_ref, t_ref, dh_ref, dhb_ref, dg_ref, loss_ref, sq_ref):
        i = pl.program_id(0)
        x = h_ref[...] + m_ref[...]
        r = lax.rsqrt(jnp.mean(x * x, axis=-1, keepdims=True) + NORM_EPS)
        xh = x * r
        g = g_ref[...]
        real = i >= 1
        err = jnp.where(real, xh * g - t_ref[...], 0.0)
        dy = err * (1.0 / d)
        dxh = dy * g
        dh = r * (dxh - xh * jnp.mean(dxh * xh, axis=-1, keepdims=True))
        dh_ref[...] = dh
        dhb_ref[...] = dh.astype(MXU_DTYPE)

        @pl.when(i == 0)
        def _():
            dg_ref[...] = jnp.zeros_like(dg_ref)
            sq_ref[...] = jnp.zeros_like(sq_ref)

        dg_ref[...] += jnp.sum(dy * xh, axis=0, keepdims=True)
        sq_ref[...] += jnp.sum(err * err, axis=0, keepdims=True)

        @pl.when(i == n_blocks - 1)
        def _():
            total = jnp.sum(sq_ref[...], axis=-1, keepdims=True) * (0.5 / d)
            loss_ref[...] = jnp.broadcast_to(total, (1, 128))

    return pl.pallas_call(
        body, name="loss_head",
        out_shape=(jax.ShapeDtypeStruct((l, d), F32), jax.ShapeDtypeStruct((l, d), MXU_DTYPE),
                   jax.ShapeDtypeStruct((1, d), F32), jax.ShapeDtypeStruct((1, 128), F32)),
        grid=(n_blocks,), in_specs=[row, row, vec, tgt],
        out_specs=(row, row, vec, pl.BlockSpec((1, 128), lambda i: (0, 0))),
        scratch_shapes=[pltpu.VMEM((1, d), F32)],
        compiler_params=_params(("arbitrary",)),
    )(h1, mlp_out, gain, target)


def _dot(a, b):
    return jnp.dot(a, b, preferred_element_type=F32)


def _dot_nt(a, b):
    return lax.dot_general(a, b, (((1,), (1,)), ((), ())), preferred_element_type=F32)


def _dot_tn(a, b):
    return lax.dot_general(a, b, (((0,), (0,)), ((), ())), preferred_element_type=F32)


def _rope(t, cos2, sin2):
    return t * cos2 + pltpu.roll(t, HEAD_DIM // 2, 1) * sin2


def _rope_bwd(dr, cos2, sin2):
    return dr * cos2 + pltpu.roll(dr * sin2, HEAD_DIM // 2, 1)


def _sigmoid(x):
    return 1.0 / (1.0 + jnp.exp(-x))


def _row_valid(block, rows):
    r = block * CHUNK + lax.broadcasted_iota(jnp.int32, (rows, 1), 0)
    return r >= PAD_ROWS


def _retention_consts(l):
    pos = jnp.arange(l, dtype=F32) - PAD_ROWS
    inv_freq = 1.0 / (ROPE_BASE ** (jnp.arange(0, HEAD_DIM, 2, dtype=F32) / HEAD_DIM))
    ang = pos[:, None] * inv_freq[None, :]
    cos, sin = jnp.cos(ang), jnp.sin(ang)
    cos2 = jnp.concatenate([cos, cos], axis=-1)
    sin2 = jnp.concatenate([-sin, sin], axis=-1)
    log_g = jnp.log1p(-jnp.exp2(-5.0 - jnp.arange(N_HEADS, dtype=F32)))
    idx = jnp.arange(CHUNK, dtype=F32)
    diff = idx[:, None] - idx[None, :]
    decay = jnp.where(diff >= 0, jnp.exp(jnp.maximum(diff, 0.0)[None] * log_g[:, None, None]), 0.0)
    xi = jnp.exp((idx + 1.0)[None, :] * log_g[:, None])
    zeta = jnp.exp((CHUNK - 1.0 - idx)[None, :] * log_g[:, None])
    g_chunk = jnp.exp(CHUNK * log_g)
    bcast = lambda v: jnp.broadcast_to(v[:, :, None], (N_HEADS, CHUNK, HEAD_DIM))
    g_rows = jnp.broadcast_to(g_chunk[:, None, None], (N_HEADS, 8, HEAD_DIM))
    return cos2, sin2, decay, bcast(xi), bcast(zeta), g_rows


def _retention_fwd(proj, ret_gain, consts):
    l = proj.shape[0]
    n_chunks = l // CHUNK
    cos2, sin2, decay, xi, zeta, g_rows = consts
    scale = HEAD_DIM ** -0.5

    def body(p_ref, cos_ref, sin_ref, dec_ref, xi_ref, zeta_ref, gr_ref, gain_ref,
             mix_ref, o_ref, st_ref, state):
        c = pl.program_id(0)

        @pl.when(c == 0)
        def _():
            state[...] = jnp.zeros_like(state)

        cos_v, sin_v = cos_ref[...], sin_ref[...]
        valid = _row_valid(c, CHUNK)
        for h in range(N_HEADS):
            cols = slice(h * HEAD_DIM, (h + 1) * HEAD_DIM)
            q = p_ref[:, h * HEAD_DIM:(h + 1) * HEAD_DIM]
            k = p_ref[:, GROUP + h * HEAD_DIM:GROUP + (h + 1) * HEAD_DIM]
            v = p_ref[:, 2 * GROUP + h * HEAD_DIM:2 * GROUP + (h + 1) * HEAD_DIM]
            g = p_ref[:, 3 * GROUP + h * HEAD_DIM:3 * GROUP + (h + 1) * HEAD_DIM]
            rq = _rope(q, cos_v, sin_v).astype(MXU_DTYPE)
            rk = _rope(k, cos_v, sin_v) * scale
            rkb = rk.astype(MXU_DTYPE)
            vb = v.astype(MXU_DTYPE)
            st = state[h]
            st_ref[h] = st
            s = _dot_nt(rq, rkb) * dec_ref[h]
            o = _dot(s.astype(MXU_DTYPE), vb) + _dot(rq, st.astype(MXU_DTYPE)) * xi_ref[h]
            kz = (rk * zeta_ref[h]).astype(MXU_DTYPE)
            state[h] = gr_ref[h, 0:1, :] * st + _dot_tn(kz, vb)
            o_ref[:, cols] = o
            mu = jnp.mean(o, axis=-1, keepdims=True)
            oc = o - mu
            yn = oc * lax.rsqrt(jnp.mean(oc * oc, axis=-1, keepdims=True) + NORM_EPS)
            ret = (g * _sigmoid(g)) * (yn * gain_ref[:, cols])
            mix_ref[:, cols] = jnp.where(valid, ret, 0.0).astype(mix_ref.dtype)

    head_tab = pl.BlockSpec((N_HEADS, CHUNK, HEAD_DIM), lambda c: (0, 0, 0))
    return pl.pallas_call(
        body, name="retention_fwd",
        out_shape=(jax.ShapeDtypeStruct((l, 2 * GROUP), MXU_DTYPE), jax.ShapeDtypeStruct((l, GROUP), F32),
                   jax.ShapeDtypeStruct((n_chunks, N_HEADS, HEAD_DIM, HEAD_DIM), F32)),
        grid=(n_chunks,),
        in_specs=[pl.BlockSpec((CHUNK, 4 * GROUP), lambda c: (c, 0)),
                  pl.BlockSpec((CHUNK, HEAD_DIM), lambda c: (c, 0)),
                  pl.BlockSpec((CHUNK, HEAD_DIM), lambda c: (c, 0)),
                  head_tab, head_tab, head_tab,
                  pl.BlockSpec((N_HEADS, 8, HEAD_DIM), lambda c: (0, 0, 0)),
                  pl.BlockSpec((1, GROUP), lambda c: (0, 0))],
        out_specs=(pl.BlockSpec((CHUNK, GROUP), lambda c: (c, 0)),
                   pl.BlockSpec((CHUNK, GROUP), lambda c: (c, 0)),
                   pl.BlockSpec((None, N_HEADS, HEAD_DIM, HEAD_DIM), lambda c: (c, 0, 0, 0))),
        scratch_shapes=[pltpu.VMEM((N_HEADS, HEAD_DIM, HEAD_DIM), F32)],
        compiler_params=_params(("arbitrary",)),
    )(proj, cos2, sin2, decay, xi, zeta, g_rows, ret_gain)


def _retention_bwd(proj, o_pre, states, d_mix, ret_gain, consts):
    l = proj.shape[0]
    n_chunks = l // CHUNK
    cos2, sin2, decay, xi, zeta, g_rows = consts
    scale = HEAD_DIM ** -0.5
    rev = lambda c: n_chunks - 1 - c

    def body(p_ref, o_ref, st_ref, dm_ref, cos_ref, sin_ref, dec_ref, dect_ref, xi_ref, zeta_ref, gr_ref, gain_ref,
             dp_ref, dgain_ref, dstate):
        step = pl.program_id(0)

        @pl.when(step == 0)
        def _():
            dstate[...] = jnp.zeros_like(dstate)
            dgain_ref[...] = jnp.zeros_like(dgain_ref)

        cos_v, sin_v = cos_ref[...], sin_ref[...]
        valid = _row_valid(rev(step), CHUNK)
        for h in range(N_HEADS):
            cols = slice(h * HEAD_DIM, (h + 1) * HEAD_DIM)
            q = p_ref[:, h * HEAD_DIM:(h + 1) * HEAD_DIM]
            k = p_ref[:, GROUP + h * HEAD_DIM:GROUP + (h + 1) * HEAD_DIM]
            v = p_ref[:, 2 * GROUP + h * HEAD_DIM:2 * GROUP + (h + 1) * HEAD_DIM]
            g = p_ref[:, 3 * GROUP + h * HEAD_DIM:3 * GROUP + (h + 1) * HEAD_DIM]
            o = o_ref[:, cols]
            gain = gain_ref[:, cols]
            d_ret = jnp.where(valid, dm_ref[:, cols], 0.0)
            mu = jnp.mean(o, axis=-1, keepdims=True)
            oc = o - mu
            rstd = lax.rsqrt(jnp.mean(oc * oc, axis=-1, keepdims=True) + NORM_EPS)
            yn = oc * rstd
            sig = _sigmoid(g)
            gate = g * sig
            dgain_ref[:, cols] += jnp.sum(d_ret * gate * yn, axis=0, keepdims=True)
            d_g = d_ret * (yn * gain) * (sig * (1.0 + g * (1.0 - sig)))
            d_yn = d_ret * gate * gain
            d_o = rstd * (d_yn - jnp.mean(d_yn, axis=-1, keepdims=True)
                          - yn * jnp.mean(d_yn * yn, axis=-1, keepdims=True))
            rq = _rope(q, cos_v, sin_v)
            rk = _rope(k, cos_v, sin_v) * scale
            rqb, rkb, vb = rq.astype(MXU_DTYPE), rk.astype(MXU_DTYPE), v.astype(MXU_DTYPE)
            dob = d_o.astype(MXU_DTYPE)
            dec = dec_ref[h]
            xi_h, zeta_h = xi_ref[h], zeta_ref[h]
            st_b = st_ref[h].astype(MXU_DTYPE)
            dst = dstate[h]
            dst_b = dst.astype(MXU_DTYPE)
            dec_t = dect_ref[h]
            s_t_b = (_dot_nt(rkb, rqb) * dec_t).astype(MXU_DTYPE)
            da_b = (_dot_nt(dob, vb) * dec).astype(MXU_DTYPE)
            da_t_b = (_dot_nt(vb, dob) * dec_t).astype(MXU_DTYPE)
            doxi_b = (d_o * xi_h).astype(MXU_DTYPE)
            kz_b = (rk * zeta_h).astype(MXU_DTYPE)
            d_rq = _dot(da_b, rkb) + _dot_nt(doxi_b, st_b)
            d_rk = _dot(da_t_b, rqb) + _dot_nt(vb, dst_b) * zeta_h
            d_v = _dot(s_t_b, dob) + _dot(kz_b, dst_b)
            dstate[h] = gr_ref[h, 0:1, :] * dst + _dot_tn(rqb, doxi_b)
            d_q = _rope_bwd(d_rq, cos_v, sin_v)
            d_k = _rope_bwd(d_rk * scale, cos_v, sin_v)
            dp_ref[:, h * HEAD_DIM:(h + 1) * HEAD_DIM] = d_q.astype(dp_ref.dtype)
            dp_ref[:, GROUP + h * HEAD_DIM:GROUP + (h + 1) * HEAD_DIM] = d_k.astype(dp_ref.dtype)
            dp_ref[:, 2 * GROUP + h * HEAD_DIM:2 * GROUP + (h + 1) * HEAD_DIM] = d_v.astype(dp_ref.dtype)
            dp_ref[:, 3 * GROUP + h * HEAD_DIM:3 * GROUP + (h + 1) * HEAD_DIM] = d_g.astype(dp_ref.dtype)

    head_tab = pl.BlockSpec((N_HEADS, CHUNK, HEAD_DIM), lambda c: (0, 0, 0))
    return pl.pallas_call(
        body, name="retention_bwd",
        out_shape=(jax.ShapeDtypeStruct((l, WIN_N), MXU_DTYPE), jax.ShapeDtypeStruct((1, GROUP), F32)),
        grid=(n_chunks,),
        in_specs=[pl.BlockSpec((CHUNK, 4 * GROUP), lambda c: (rev(c), 0)),
                  pl.BlockSpec((CHUNK, GROUP), lambda c: (rev(c), 0)),
                  pl.BlockSpec((None, N_HEADS, HEAD_DIM, HEAD_DIM), lambda c: (rev(c), 0, 0, 0)),
                  pl.BlockSpec((CHUNK, GROUP), lambda c: (rev(c), 0)),
                  pl.BlockSpec((CHUNK, HEAD_DIM), lambda c: (rev(c), 0)),
                  pl.BlockSpec((CHUNK, HEAD_DIM), lambda c: (rev(c), 0)),
                  head_tab, head_tab, head_tab, head_tab,
                  pl.BlockSpec((N_HEADS, 8, HEAD_DIM), lambda c: (0, 0, 0)),
                  pl.BlockSpec((1, GROUP), lambda c: (0, 0))],
        out_specs=(pl.BlockSpec((CHUNK, 4 * GROUP), lambda c: (rev(c), 0)),
                   pl.BlockSpec((1, GROUP), lambda c: (0, 0))),
        scratch_shapes=[pltpu.VMEM((N_HEADS, HEAD_DIM, HEAD_DIM), F32)],
        compiler_params=_params(("arbitrary",)),
    )(proj, o_pre, states, d_mix, cos2, sin2, decay, jnp.transpose(decay, (0, 2, 1)), xi, zeta, g_rows, ret_gain)


FF_TILE = (7 * GROUP) // 128


def _log_forget(ff, bias_row, valid):
    x = ff + bias_row
    e = jnp.exp(-jnp.abs(x))
    lf = jnp.minimum(x, 0.0) - jnp.log(1.0 + e)
    head_lane = lax.broadcasted_iota(jnp.int32, x.shape, 1) < N_HEADS
    keep = lambda t: jnp.where(head_lane, jnp.where(valid, t, 0.0), 0.0)
    return keep(lf), keep(jnp.where(x >= 0, e, 1.0) / (1.0 + e))


def _fox_prep(proj, bias_row):
    l = proj.shape[0]
    n_blocks = l // CHUNK

    def body(ff_ref, b_ref, bc_ref, rows_ref, cum):
        r = lax.broadcasted_iota(jnp.int32, (CHUNK, CHUNK), 0)
        cidx = lax.broadcasted_iota(jnp.int32, (CHUNK, CHUNK), 1)
        tri = jnp.where(r >= cidx, 1.0, 0.0).astype(F32)
        carry = jnp.zeros((1, 128), F32)
        for blk in range(n_blocks):
            rows = slice(blk * CHUNK, (blk + 1) * CHUNK)
            valid = _row_valid(blk, CHUNK)
            lf, _ = _log_forget(ff_ref[rows, :], b_ref[...], valid)
            local = jnp.dot(tri, lf, precision=lax.Precision.HIGHEST, preferred_element_type=F32) + carry
            carry = local[CHUNK - 1:CHUNK, :]
            masked = jnp.where(valid, local, -NEG_BIG)
            cum[rows, :] = masked
            t = masked.T
            for h in range(N_HEADS):
                rows_ref[h, :, rows] = t[h:h + 1, :]
        full = cum[...]
        for h in range(N_HEADS):
            bc_ref[h] = jnp.broadcast_to(full[:, h:h + 1], (l, 128))

    return pl.pallas_call(
        body, name="fox_prep",
        out_shape=(jax.ShapeDtypeStruct((N_HEADS, l, 128), F32), jax.ShapeDtypeStruct((N_HEADS, 1, l), F32)),
        grid=(1,),
        in_specs=[pl.BlockSpec((l, 128), lambda i: (0, FF_TILE)), pl.BlockSpec((1, 128), lambda i: (0, 0))],
        out_specs=(pl.BlockSpec((N_HEADS, l, 128), lambda i: (0, 0, 0)),
                   pl.BlockSpec((N_HEADS, 1, l), lambda i: (0, 0, 0))),
        scratch_shapes=[pltpu.VMEM((l, 128), F32)],
        compiler_params=_params(("arbitrary",)),
    )(proj, bias_row)


ATTN_BLOCK = 2 * CHUNK


def _attn_blocks(l):
    assert (l - CHUNK) % ATTN_BLOCK == 0
    return [(0, CHUNK)] + [(s, ATTN_BLOCK) for s in range(CHUNK, l, ATTN_BLOCK)]


def _rows_valid(start, size):
    return start + lax.broadcasted_iota(jnp.int32, (size, 1), 0) >= PAD_ROWS


def _fox_fwd(proj, cum_bc, cum_rows, mix):
    l = proj.shape[0]
    blocks = _attn_blocks(l)
    scale = HEAD_DIM ** -0.5
    qt, kt, vt = 4 * N_HEADS, 5 * N_HEADS, 6 * N_HEADS

    def body(q_ref, k_ref, v_ref, cbc_ref, crow_ref, mix_in, o_ref, lse_ref, qb_s, kb_s, vb_s):
        qb_s[...] = q_ref[...].astype(MXU_DTYPE)
        kb_s[...] = k_ref[...].astype(MXU_DTYPE)
        vb_s[...] = v_ref[...].astype(MXU_DTYPE)
        for p, (qs, qn) in enumerate(blocks):
            qb = qb_s[qs:qs + qn, :]
            cq = cbc_ref[qs:qs + qn, :]
            m = jnp.full((qn, 1), NEG_BIG, F32)
            lsum = jnp.zeros((qn, 1), F32)
            acc = jnp.zeros((qn, HEAD_DIM), F32)
            for j in range(p + 1):
                ks, kn = blocks[j]
                bias = jnp.tile(cq, (1, kn // CHUNK)) - crow_ref[:, ks:ks + kn]
                s = _dot_nt(qb, kb_s[ks:ks + kn, :]) * scale + bias
                if j == p:
                    q_pos = qs + lax.broadcasted_iota(jnp.int32, (qn, kn), 0)
                    k_pos = ks + lax.broadcasted_iota(jnp.int32, (qn, kn), 1)
                    s = jnp.where(k_pos <= q_pos, s, NEG_BIG)
                m_new = jnp.maximum(m, jnp.max(s, axis=-1, keepdims=True))
                alpha = jnp.exp(m - m_new)
                pr = jnp.exp(s - m_new)
                lsum = lsum * alpha + jnp.sum(pr, axis=-1, keepdims=True)
                acc = acc * alpha + _dot(pr.astype(MXU_DTYPE), vb_s[ks:ks + kn, :])
                m = m_new
            o = jnp.where(_rows_valid(qs, qn), acc * (1.0 / lsum), 0.0)
            o_ref[qs:qs + qn, :] = o.astype(o_ref.dtype)
            lse = m + jnp.log(lsum)
            lse_ref[:, qs:qs + qn] = jnp.broadcast_to(lse, (qn, CHUNK)).T[0:1, :]

    head_col = lambda t: pl.BlockSpec((l, HEAD_DIM), lambda h: (0, t + h))
    return pl.pallas_call(
        body, name="fox_fwd",
        out_shape=(jax.ShapeDtypeStruct(mix.shape, mix.dtype), jax.ShapeDtypeStruct((N_HEADS, 1, l), F32)),
        grid=(N_HEADS,),
        in_specs=[head_col(qt), head_col(kt), head_col(vt),
                  pl.BlockSpec((None, l, 128), lambda h: (h, 0, 0)),
                  pl.BlockSpec((None, 1, l), lambda h: (h, 0, 0)),
                  ANY],
        out_specs=(head_col(N_HEADS), pl.BlockSpec((None, 1, l), lambda h: (h, 0, 0))),
        input_output_aliases={5: 0},
        scratch_shapes=[pltpu.VMEM((l, HEAD_DIM), MXU_DTYPE)] * 3,
        compiler_params=_params(("parallel",)),
    )(proj, proj, proj, cum_bc, cum_rows, mix)


def _fox_bwd(proj, cum_bc, cum_rows, d_mix, lse_rows, d_proj):
    l = proj.shape[0]
    blocks = _attn_blocks(l)
    scale = HEAD_DIM ** -0.5
    qt, kt, vt = 4 * N_HEADS, 5 * N_HEADS, 6 * N_HEADS

    def backward(q_ref, k_ref, v_ref, do_ref, cbc_ref, crow_ref, lse_ref,
                 dq_ref, ds_ref, dk_acc, dv_acc, qb_s, kb_s, vb_s, dob_s, p_s, dp_s):
        qb_s[...] = q_ref[...].astype(MXU_DTYPE)
        kb_s[...] = k_ref[...].astype(MXU_DTYPE)
        vb_s[...] = v_ref[...].astype(MXU_DTYPE)
        dob_s[...] = jnp.where(_rows_valid(0, l), do_ref[...], 0.0).astype(MXU_DTYPE)
        dk_acc[...] = jnp.zeros_like(dk_acc)
        dv_acc[...] = jnp.zeros_like(dv_acc)
        ds_ref[...] = jnp.zeros_like(ds_ref)
        shift_row = crow_ref[...] - lse_ref[...]

        for p, (qs, qn) in enumerate(blocks):
            qb, dob = qb_s[qs:qs + qn, :], dob_s[qs:qs + qn, :]
            shift = shift_row[:, qs:qs + qn]

            delta = jnp.zeros((1, qn), F32)
            for j in range(p + 1):
                ks, kn = blocks[j]
                ck = jnp.tile(cbc_ref[ks:ks + kn, :], (1, qn // CHUNK))
                s_t = _dot_nt(kb_s[ks:ks + kn, :], qb) * scale + (shift - ck)
                if j == p:
                    k_pos = ks + lax.broadcasted_iota(jnp.int32, (kn, qn), 0)
                    q_pos = qs + lax.broadcasted_iota(jnp.int32, (kn, qn), 1)
                    s_t = jnp.where(k_pos <= q_pos, s_t, NEG_BIG)
                p_t, dp_t = jnp.exp(s_t), _dot_nt(vb_s[ks:ks + kn, :], dob)
                p_s[j, 0:kn, 0:qn] = p_t
                dp_s[j, 0:kn, 0:qn] = dp_t
                delta = delta + jnp.sum(p_t * dp_t, axis=0, keepdims=True)
            dq = jnp.zeros((qn, HEAD_DIM), F32)
            for j in range(p + 1):
                ks, kn = blocks[j]
                rows = slice(ks, ks + kn)
                p_t, dp_t = p_s[j, 0:kn, 0:qn], dp_s[j, 0:kn, 0:qn]
                ds_t = p_t * (dp_t - delta)
                ds_b = ds_t.astype(MXU_DTYPE)
                dv_acc[rows, :] += _dot(p_t.astype(MXU_DTYPE), dob)
                dk_acc[rows, :] += _dot(ds_b, qb) * scale
                ds_ref[rows, :] += sum(ds_t[:, c:c + CHUNK] for c in range(0, qn, CHUNK))
                dq = dq + _dot_tn(ds_b, kb_s[rows, :])
            dq_ref[qs:qs + qn, :] = (dq * scale).astype(dq_ref.dtype)

    def body(q_ref, k_ref, v_ref, do_ref, cbc_ref, crow_ref, lse_ref, dproj_in, out_ref, ds_ref, dk_acc, dv_acc, *rest):
        which = pl.program_id(1)

        @pl.when(which == 0)
        def _():
            backward(q_ref, k_ref, v_ref, do_ref, cbc_ref, crow_ref, lse_ref, out_ref, ds_ref, dk_acc, dv_acc, *rest)

        @pl.when(which == 1)
        def _():
            out_ref[...] = dk_acc[...].astype(out_ref.dtype)

        @pl.when(which == 2)
        def _():
            out_ref[...] = dv_acc[...].astype(out_ref.dtype)

    head_col = lambda t: pl.BlockSpec((l, HEAD_DIM), lambda h, w: (0, t + h))
    per_head = lambda shape: pl.BlockSpec((None,) + shape, lambda h, w: (h, 0, 0))
    return pl.pallas_call(
        body, name="fox_bwd",
        out_shape=(jax.ShapeDtypeStruct(d_proj.shape, d_proj.dtype), jax.ShapeDtypeStruct((N_HEADS, l, 128), F32)),
        grid=(N_HEADS, 3),
        in_specs=[head_col(qt), head_col(kt), head_col(vt), head_col(N_HEADS),
                  per_head((l, 128)), per_head((1, l)), per_head((1, l)), ANY],
        out_specs=(pl.BlockSpec((l, HEAD_DIM), lambda h, w: (0, qt + N_HEADS * w + h)), per_head((l, 128))),
        input_output_aliases={7: 0},
        scratch_shapes=([pltpu.VMEM((l, HEAD_DIM), F32)] * 2 + [pltpu.VMEM((l, HEAD_DIM), MXU_DTYPE)] * 4
                        + [pltpu.VMEM((len(blocks), ATTN_BLOCK, ATTN_BLOCK), F32)] * 2),
        compiler_params=_params(("parallel", "arbitrary")),
    )(proj, proj, proj, d_mix, cum_bc, cum_rows, lse_rows, d_proj)


def _fox_gate_bwd(ds_sum, proj, bias_row, d_proj):
    l = proj.shape[0]
    n_blocks = l // CHUNK
    tail = WIN_N - 7 * GROUP

    def body(ds_ref, ff_ref, b_ref, dproj_in, dff_ref, db_ref):
        r = lax.broadcasted_iota(jnp.int32, (CHUNK, CHUNK), 0)
        cidx = lax.broadcasted_iota(jnp.int32, (CHUNK, CHUNK), 1)
        upper = jnp.where(cidx >= r, 1.0, 0.0).astype(F32)
        carry = jnp.zeros((1, 128), F32)
        db = jnp.zeros((1, 128), F32)
        for blk in reversed(range(n_blocks)):
            rows = slice(blk * CHUNK, (blk + 1) * CHUNK)
            key_sum = jnp.zeros((CHUNK, 128), F32)
            for h in range(N_HEADS):
                select = jnp.where(cidx == h, 1.0, 0.0).astype(F32)
                key_sum = key_sum + jnp.dot(ds_ref[h, rows, :], select, precision=lax.Precision.HIGHEST,
                                            preferred_element_type=F32)
            suffix = jnp.dot(upper, key_sum, precision=lax.Precision.HIGHEST, preferred_element_type=F32) + carry
            carry = suffix[0:1, :]
            _, dsig = _log_forget(ff_ref[rows, :], b_ref[...], _row_valid(blk, CHUNK))
            dff = -suffix * dsig
            dff_ref[rows, 0:128] = dff.astype(dff_ref.dtype)
            dff_ref[rows, 128:tail] = jnp.zeros((CHUNK, tail - 128), dff_ref.dtype)
            db = db + jnp.sum(dff, axis=0, keepdims=True)
        db_ref[...] = db

    return pl.pallas_call(
        body, name="fox_gate_bwd",
        out_shape=(jax.ShapeDtypeStruct(d_proj.shape, d_proj.dtype), jax.ShapeDtypeStruct((1, 128), F32)),
        grid=(1,),
        in_specs=[pl.BlockSpec((N_HEADS, l, 128), lambda i: (0, 0, 0)),
                  pl.BlockSpec((l, 128), lambda i: (0, FF_TILE)),
                  pl.BlockSpec((1, 128), lambda i: (0, 0)), ANY],
        out_specs=(pl.BlockSpec((l, tail), lambda i: (0, 7 * GROUP // tail)), pl.BlockSpec((1, 128), lambda i: (0, 0))),
        input_output_aliases={3: 0},
        compiler_params=_params(("arbitrary",)),
    )(ds_sum, proj, bias_row, d_proj)


def _conv(u, w, b):
    return b + w[0:1, :] * pltpu.roll(u, 2, 0) + w[1:2, :] * pltpu.roll(u, 1, 0) + w[2:3, :] * u


def _conv_act_fwd(u, conv_w, conv_b, d_ff):
    l = u.shape[0]
    tc = _divisor_tile(d_ff, 256, 128)
    nt = d_ff // tc

    def body(ug_ref, uv_ref, wg_ref, wv_ref, bg_ref, bv_ref, a_ref, y_ref):
        yg = _conv(ug_ref[...], wg_ref[...], bg_ref[...])
        yv = _conv(uv_ref[...], wv_ref[...], bv_ref[...])
        act = yg * _sigmoid(yg) * yv
        a_ref[...] = jnp.where(_row_valid(0, l), act, 0.0).astype(a_ref.dtype)
        y_ref[0] = yg.astype(y_ref.dtype)
        y_ref[1] = yv.astype(y_ref.dtype)

    return pl.pallas_call(
        body, name="conv_act_fwd",
        out_shape=(jax.ShapeDtypeStruct((l, d_ff), MXU_DTYPE), jax.ShapeDtypeStruct((2, l, d_ff), MXU_DTYPE)),
        grid=(nt,),
        in_specs=[pl.BlockSpec((l, tc), lambda j: (0, j)), pl.BlockSpec((l, tc), lambda j: (0, j + nt)),
                  pl.BlockSpec((8, tc), lambda j: (0, j)), pl.BlockSpec((8, tc), lambda j: (0, j + nt)),
                  pl.BlockSpec((1, tc), lambda j: (0, j)), pl.BlockSpec((1, tc), lambda j: (0, j + nt))],
        out_specs=(pl.BlockSpec((l, tc), lambda j: (0, j)), pl.BlockSpec((2, l, tc), lambda j: (0, 0, j))),
        compiler_params=_params(("parallel",)),
    )(u, u, conv_w, conv_w, conv_b, conv_b)


def _conv_act_bwd(u, y, conv_w, d_act, d_ff):
    l = u.shape[0]
    tc = _divisor_tile(d_ff, 256, 128)
    nt = d_ff // tc

    def body(ug_ref, uv_ref, y_ref, wg_ref, wv_ref, da_ref, du_ref, dwb_ref):
        valid = _row_valid(0, l)
        ug, uv = ug_ref[...], uv_ref[...]
        wg, wv = wg_ref[...], wv_ref[...]
        yg, yv = y_ref[0].astype(F32), y_ref[1].astype(F32)
        sig = _sigmoid(yg)
        da = jnp.where(valid, da_ref[...], 0.0)
        d_yv = da * (yg * sig)
        d_yg = da * yv * (sig * (1.0 + yg * (1.0 - sig)))
        for idx, (dy, uu, w) in enumerate(((d_yg, ug, wg), (d_yv, uv, wv))):
            du = w[2:3, :] * dy + w[1:2, :] * pltpu.roll(dy, l - 1, 0) + w[0:1, :] * pltpu.roll(dy, l - 2, 0)
            du_ref[idx] = jnp.where(valid, du, 0.0).astype(du_ref.dtype)
            dwb_ref[idx, 0:1, :] = jnp.sum(dy * pltpu.roll(uu, 2, 0), axis=0, keepdims=True)
            dwb_ref[idx, 1:2, :] = jnp.sum(dy * pltpu.roll(uu, 1, 0), axis=0, keepdims=True)
            dwb_ref[idx, 2:3, :] = jnp.sum(dy * uu, axis=0, keepdims=True)
            dwb_ref[idx, 3:4, :] = jnp.sum(dy, axis=0, keepdims=True)
            dwb_ref[idx, 4:8, :] = jnp.zeros((4, tc), F32)

    return pl.pallas_call(
        body, name="conv_act_bwd",
        out_shape=(jax.ShapeDtypeStruct((2, l, d_ff), MXU_DTYPE), jax.ShapeDtypeStruct((2, 8, d_ff), F32)),
        grid=(nt,),
        in_specs=[pl.BlockSpec((l, tc), lambda j: (0, j)), pl.BlockSpec((l, tc), lambda j: (0, j + nt)),
                  pl.BlockSpec((2, l, tc), lambda j: (0, 0, j)),
                  pl.BlockSpec((8, tc), lambda j: (0, j)), pl.BlockSpec((8, tc), lambda j: (0, j + nt)),
                  pl.BlockSpec((l, tc), lambda j: (0, j))],
        out_specs=(pl.BlockSpec((2, l, tc), lambda j: (0, 0, j)), pl.BlockSpec((2, 8, tc), lambda j: (0, 0, j))),
        compiler_params=_params(("parallel",)),
    )(u, u, y, conv_w, conv_w, d_act)


def _adamw(w, g, m, v, name):
    shape = w.shape
    if w.ndim == 1:
        as2d = (1, shape[0])
    else:
        as2d = (int(np.prod(shape[:-1])), shape[-1])
    r, c = as2d
    tr = _divisor_tile(r, 256, 8)
    spec = pl.BlockSpec((tr, c), lambda i: (i, 0))

    def body(w_ref, g_ref, m_ref, v_ref, d_ref, nm_ref, nv_ref):
        d_ref[...], nm_ref[...], nv_ref[...] = _adamw_math(w_ref[...], g_ref[...], m_ref[...], v_ref[...])

    sds = jax.ShapeDtypeStruct(as2d, F32)
    outs = pl.pallas_call(
        body, name=name, out_shape=(sds, sds, sds), grid=(r // tr,),
        in_specs=[spec] * 4, out_specs=(spec,) * 3,
        compiler_params=_params(("parallel",)),
    )(w.reshape(as2d), g.reshape(as2d), m.reshape(as2d), v.reshape(as2d))
    return tuple(o.reshape(shape) for o in outs)


def _pad_rows(a, rows):
    return jnp.pad(a, ((0, rows - a.shape[0]), (0, 0)))


def kernel(x, meta_tokens, norm1_gain, w_in, b_forget, ret_norm_gain, w_out, norm2_gain, w_up, conv_w, conv_b, w_down, final_norm_gain, loss_target, m_meta_tokens, m_norm1_gain, m_w_in, m_b_forget, m_ret_norm_gain, m_w_out, m_norm2_gain, m_w_up, m_conv_w, m_conv_b, m_w_down, m_final_norm_gain, v_meta_tokens, v_norm1_gain, v_w_in, v_b_forget, v_ret_norm_gain, v_w_out, v_norm2_gain, v_w_up, v_conv_w, v_conv_b, v_w_down, v_final_norm_gain):
    seq, d = x.shape[1], x.shape[2]
    l = CHUNK + seq
    d_ff = w_down.shape[1] * N_DEV
    up_shard = w_up.shape[2]
    assert 4 * up_shard == d_ff and w_in.shape[2] == WIN_SHARD and d == 2 * GROUP
    dev = _device_index()
    mx, my, mc = _my_position()
    core = jnp.reshape(mc, (1,)).astype(jnp.int32)
    chip = jnp.reshape(2 * mx + my, (1,)).astype(jnp.int32)
    dev1 = jnp.reshape(dev, (1,)).astype(jnp.int32)

    small = jnp.concatenate([meta_tokens.reshape(-1, 128), conv_w[0].reshape(-1, 128)], axis=0)
    n_meta_rows = N_META * (d // N_DEV) // 128
    small_rows = small.shape[0]
    small_all = _all_gather(_pad_rows(small, -(-small_rows // 8) * 8), "gather_small")
    meta_full = jnp.transpose(small_all[:, :n_meta_rows].reshape(N_DEV, N_META, d // N_DEV), (1, 0, 2)).reshape(N_META, d)
    conv_w_full = _pad_rows(jnp.transpose(small_all[:, n_meta_rows:small_rows].reshape(N_DEV, 3, up_shard),
                                          (1, 0, 2)).reshape(3, 2 * d_ff), 8)
    to_rows = lambda t: jnp.pad(jnp.transpose(t[0]), ((0, WIN_ROWS - WIN_SHARD), (0, 0)))
    from_rows = lambda t: jnp.transpose(t[:WIN_SHARD])[None]
    w_in_rows = to_rows(w_in)
    out_rows = d // N_DEV
    mixer_rows = -(-(WIN_ROWS + out_rows) // 32) * 32
    mixer_shard = jnp.concatenate([w_in_rows.astype(WIRE_DTYPE), w_out[0].astype(WIRE_DTYPE),
                                   jnp.zeros((mixer_rows - WIN_ROWS - out_rows, d), WIRE_DTYPE)], axis=0)

    consts = _retention_consts(l)
    bias_row = jnp.pad(b_forget, ((0, 0), (0, 128 - N_HEADS)))
    h0, a = _embed_rmsnorm(x[0], meta_full, norm1_gain)
    mixer_blocks = _gather_ring(mixer_shard, dev1, a, "gather_w_in")
    start_up = _gather_start(w_up[0], dev1, mixer_blocks, "gather_w_up_start")
    w_in_full = _assemble_w_in(mixer_blocks).astype(MXU_DTYPE)
    proj = _mm_nt(a, w_in_full, F32, "mm_proj", after=start_up[4])
    ret_mix, ret_pre, ret_states = _retention_fwd(proj, ret_norm_gain, consts)
    cum_bc, cum_rows = _fox_prep(proj, bias_row)
    mix, lse_rows = _fox_fwd(proj, cum_bc, cum_rows, ret_mix)
    w_out_full = mixer_blocks[:, WIN_ROWS:WIN_ROWS + out_rows].reshape(d, d).astype(MXU_DTYPE)
    h1, cn = _out_proj_resid_rmsnorm(mix, w_out_full, h0, norm2_gain)
    w_up_blocks = _gather_finish(start_up, cn, "gather_w_up").astype(MXU_DTYPE)
    start_down = _gather_start(w_down[0], dev1, w_up_blocks, "gather_w_down_start")
    u = _mm_up(cn, w_up_blocks, start_down[4])
    pass_down = _gather_pass_start(start_down, u, "gather_w_down")
    act, conv_y = _conv_act_fwd(u, conv_w_full, conv_b + pass_down[4][0, 0], d_ff)
    w_down_full = _gather_pass_finish(pass_down, act, "gather_w_down").reshape(d_ff, d).astype(MXU_DTYPE)
    mlp_out = _mm_nn(act, w_down_full, F32, "mm_down", tm_cap=544, tk_cap=d_ff)
    d_h2, d_h2_b, dg_final, loss_part = _loss_head(h1, mlp_out, final_norm_gain.reshape(1, d), loss_target[0])

    gw_down = _mm_tn(act, d_h2_b, WIRE_DTYPE, "mm_gw_down", tm_cap=1408, tn_cap=1024)
    d2d_down = _reduce_scatter_d2d_start(gw_down.reshape(N_DEV, d_ff // N_DEV, d), d_h2, "rs_w_down")
    d_act = _mm_nt(d_h2_b, w_down_full, F32, "mm_d_act", after=d2d_down[4])
    rs_down = _reduce_scatter_ici_start(d2d_down, d_act, core, "rs_w_down")
    d_u, d_conv = _conv_act_bwd(u, conv_y, conv_w_full + rs_down[4][0, 0], d_act, d_ff)
    tm = _divisor_tile(l, 1088, 16)
    gw_up = _mm_gw_up(cn, d_u)
    d2d_up = _reduce_scatter_d2d_start(gw_up, d_act, "rs_w_up")
    d_cn = _mm_d_cn(d_u, w_up_blocks, d2d_up[4])
    rs_up = _reduce_scatter_ici_start(d2d_up, d_cn, core, "rs_w_up")
    d_h1, d_h1_b, dg_norm2 = _rmsnorm_bwd(d_h2, d_cn, h1, norm2_gain + rs_up[4][0, 0], "rmsnorm2_bwd", True)

    gw_out = _mm_tn(mix, d_h1_b, WIRE_DTYPE, "mm_gw_out")
    d2d_out = _reduce_scatter_d2d_start(gw_out.reshape(N_DEV, d // N_DEV, d), d_cn, "rs_w_out")
    d_mix = _mm_nt(d_h1_b, w_out_full, F32, "mm_d_mix", after=d2d_out[4])
    d_proj, dg_ret = _retention_bwd(proj, ret_pre, ret_states, d_mix, ret_norm_gain, consts)
    d_proj, ds_sum = _fox_bwd(proj, cum_bc, cum_rows, d_mix, lse_rows, d_proj)
    d_proj, db_forget_row = _fox_gate_bwd(ds_sum, proj, bias_row, d_proj)
    rs_out = _reduce_scatter_ici_start(d2d_out, d_proj, core, "rs_w_out")
    gw_in = _mm_tn(d_proj, a, WIRE_DTYPE, "mm_gw_in", tm_cap=1536, after=rs_out[4])
    rs_in = _reduce_scatter_start(_extract_w_in_windows(gw_in), core, "rs_w_in")
    d_a = _mm_nn(d_proj, w_in_full, F32, "mm_d_a", tm_cap=544, tn_cap=256, tk_cap=WIN_N, after=rs_in[4])
    d_front, d_tokens, dg_norm1 = _rmsnorm_bwd(d_h1, d_a, h0, norm1_gain + rs_in[4][0, 0], "rmsnorm1_bwd", False)
    grad_x = d_tokens[None]
    d_meta = d_front[PAD_ROWS:CHUNK]

    d_conv_w = jnp.concatenate([d_conv[0, 0:3], d_conv[1, 0:3]], axis=1)
    d_conv_b = jnp.concatenate([d_conv[0, 3:4], d_conv[1, 3:4]], axis=1)
    pieces = [loss_part[:, 0:1], dg_norm1, db_forget_row[:, 0:N_HEADS], dg_ret, dg_norm2, d_conv_b, dg_final,
              d_meta.reshape(1, -1), d_conv_w.reshape(1, -1)]
    sizes = [p.shape[1] for p in pieces]
    flat = jnp.concatenate(pieces, axis=1)
    padded = -(-flat.shape[1] // 1024) * 1024
    flat = jnp.pad(flat, ((0, 0), (0, padded - flat.shape[1]))).reshape(padded // 128, 128)
    small_ar = _small_all_reduce_start(flat, d_tokens, "all_reduce_small")

    lead = lambda outs: tuple(o[None] for o in outs)
    fin_down = lead(_reduce_scatter_finish(rs_down, small_ar[4], chip, w_down[0], m_w_down[0], v_w_down[0], "rs_w_down"))
    fin_up = lead(_reduce_scatter_finish(rs_up, fin_down[3], chip, w_up[0], m_w_up[0], v_w_up[0], "rs_w_up"))
    fin_out = lead(_reduce_scatter_finish(rs_out, fin_up[3], chip, w_out[0], m_w_out[0], v_w_out[0], "rs_w_out"))
    fin_in = tuple(from_rows(o) for o in _reduce_scatter_finish(
        rs_in, fin_out[3], chip, w_in_rows, to_rows(m_w_in), to_rows(v_w_in), "rs_w_in"))
    g_w_down, g_w_up, g_w_out, g_w_in = fin_down[0], fin_up[0], fin_out[0], fin_in[0]
    early = [fin_down[1:], fin_up[1:], fin_out[1:], fin_in[1:]]
    total = _small_all_reduce_finish(small_ar, fin_in[3], dev1, "all_reduce_small").reshape(1, padded)
    offs = np.concatenate([[0], np.cumsum(sizes)])
    take = lambda k: total[:, int(offs[k]):int(offs[k + 1])]
    loss = take(0).reshape(())
    g_norm1, g_bf, g_ret_gain, g_norm2 = take(1), take(2), take(3), take(4)
    g_conv_b, g_final = take(5), take(6).reshape(d)
    g_meta = lax.dynamic_slice(take(7).reshape(N_META, d), (jnp.int32(0), (dev * (d // N_DEV)).astype(jnp.int32)),
                               (N_META, d // N_DEV))
    g_conv_w = lax.dynamic_slice(take(8).reshape(3, 2 * d_ff), (jnp.int32(0), (dev * up_shard).astype(jnp.int32)),
                                 (3, up_shard))[None]

    weights = [meta_tokens, norm1_gain, w_in, b_forget, ret_norm_gain, w_out, norm2_gain, w_up, conv_w, conv_b,
               w_down, final_norm_gain]
    grads = [g_meta, g_norm1, g_w_in, g_bf, g_ret_gain, g_w_out, g_norm2, g_w_up, g_conv_w, g_conv_b, g_w_down,
             g_final]
    done = {"w_down": early[0], "w_up": early[1], "w_out": early[2], "w_in": early[3]}
    ms = [m_meta_tokens, m_norm1_gain, m_w_in, m_b_forget, m_ret_norm_gain, m_w_out, m_norm2_gain, m_w_up, m_conv_w,
          m_conv_b, m_w_down, m_final_norm_gain]
    vs = [v_meta_tokens, v_norm1_gain, v_w_in, v_b_forget, v_ret_norm_gain, v_w_out, v_norm2_gain, v_w_up, v_conv_w,
          v_conv_b, v_w_down, v_final_norm_gain]
    names = ["meta", "norm1", "w_in", "b_forget", "ret_gain", "w_out", "norm2", "w_up", "conv_w", "conv_b", "w_down",
             "final_gain"]
    deltas, new_ms, new_vs = [], [], []
    for w, g, m, v, n in zip(weights, grads, ms, vs, names):
        dl, nm, nv = done[n] if n in done else _adamw(w, g, m, v, "adamw_" + n)
        deltas.append(dl)
        new_ms.append(nm)
        new_vs.append(nv)
    return (loss, grad_x, *grads, *deltas, *new_ms, *new_vs)
```

```python
import functools

import numpy as np
import jax
import jax.numpy as jnp
from jax import lax
from jax.experimental import pallas as pl
from jax.experimental.pallas import tpu as pltpu

F32 = jnp.float32
MXU_DTYPE = jnp.bfloat16
WIRE_DTYPE = jnp.bfloat16

N_DEV = 8
N_META = 16
CHUNK = 128
PAD_ROWS = CHUNK - N_META
N_HEADS = 8
HEAD_DIM = 128
GROUP = N_HEADS * HEAD_DIM
IN_DIM = 7 * GROUP + N_HEADS
WIN_SHARD = IN_DIM // N_DEV
WIN_ROWS = 912
WIN_BLOCK = 1024
WIN_STRIDE = 896
WIN_N = 7680
ROPE_BASE = 10000.0
NORM_EPS = 1e-6
NEG_BIG = -1e30
ADAM_LR, ADAM_B1, ADAM_B2, ADAM_EPS, ADAM_WD, ADAM_STEP = 0.001, 0.9, 0.999, 1e-08, 0.01, 10
VMEM_LIMIT = 52 * 1024 * 1024
MESH = pl.DeviceIdType.MESH
ANY = pl.BlockSpec(memory_space=pl.ANY)
VMEM_SPEC = pl.BlockSpec(memory_space=pltpu.VMEM)


def _params(sem=None):
    kw = {"vmem_limit_bytes": VMEM_LIMIT}
    if sem is not None:
        kw["dimension_semantics"] = sem
    return pltpu.CompilerParams(**kw)


def _divisor_tile(n, cap, unit):
    if n <= cap:
        return n
    best = None
    for t in range(unit, cap + 1, unit):
        if n % t == 0:
            best = t
    assert best is not None, (n, cap, unit)
    return best


def _my_position():
    return lax.axis_index("x"), lax.axis_index("y"), lax.axis_index("c")


def _device_index():
    x, y, c = _my_position()
    return 4 * x + 2 * y + c


def _all_gather(shard, name):
    r, c = shard.shape

    def body(x_ref, out_ref, send_sems, recv_sems, local_sem):
        mx, my, mc = _my_position()
        me, sibling = (mx, my, mc), (mx, my, 1 - mc)
        chips = [(1 - mx, my), (mx, 1 - my), (1 - mx, 1 - my)]

        def slot(px, py, pc):
            return out_ref.at[4 * px + 2 * py + pc]

        def copy(k, block, to, src=None):
            return pltpu.make_async_remote_copy(
                src_ref=slot(*block) if src is None else src, dst_ref=slot(*block),
                send_sem=send_sems.at[k], recv_sem=recv_sems.at[k], device_id=to, device_id_type=MESH)

        mine = pltpu.make_async_copy(x_ref, slot(*me), local_sem)
        mine.start()
        first = [copy(0, me, sibling, src=x_ref)]
        first += [copy(1 + j, me, (*chip, mc), src=x_ref) for j, chip in enumerate(chips)]
        for cp in first:
            cp.start()
        passed = [copy(4 + j, (*chip, mc), sibling) for j, chip in enumerate(chips)]
        for j, chip in enumerate(chips):
            copy(1 + j, (*chip, mc), me).wait_recv()
            passed[j].start()
        copy(0, sibling, me).wait_recv()
        for j, chip in enumerate(chips):
            copy(4 + j, (*chip, 1 - mc), me).wait_recv()
        for cp in first + passed:
            cp.wait_send()
        mine.wait()

    return pl.pallas_call(
        body, name=name,
        out_shape=jax.ShapeDtypeStruct((N_DEV, r, c), shard.dtype),
        in_specs=[ANY], out_specs=ANY,
        scratch_shapes=[pltpu.SemaphoreType.DMA((7,)), pltpu.SemaphoreType.DMA((7,)), pltpu.SemaphoreType.DMA],
    )(shard)


HBM_SPEC = pl.BlockSpec(memory_space=pltpu.HBM)
SEM_SPEC = pl.BlockSpec(memory_space=pltpu.SEMAPHORE)
DATAFLOW_EFFECT = pltpu.SideEffectType.DATAFLOW_SIDE_EFFECTING


def _in_hbm(a):
    return pltpu.with_memory_space_constraint(a, pltpu.HBM)


def _split_start(src, land, make_copies, n_copies, after, name):
    if isinstance(land, tuple):
        land = lax.empty(land, src.dtype)
    land_shape = land.shape
    def body(src_ref, land_ref, after_ref, send_sems, recv_sems, src_thru, land_thru, token):
        for cp in make_copies(src_ref, land_ref, send_sems, recv_sems):
            cp.start()
        token[...] = jnp.zeros_like(token)

    return pl.pallas_call(
        body, name=name,
        out_shape=(pltpu.SemaphoreType.DMA((n_copies,)), pltpu.SemaphoreType.DMA((n_copies,)),
                   pltpu.HBM(src.shape, src.dtype), pltpu.HBM(land_shape, land.dtype),
                   jax.ShapeDtypeStruct((8, 128), F32)),
        in_specs=(HBM_SPEC, HBM_SPEC, ANY), out_specs=(SEM_SPEC, SEM_SPEC, HBM_SPEC, HBM_SPEC, VMEM_SPEC),
        input_output_aliases={0: 2, 1: 3},
        compiler_params=pltpu.CompilerParams(has_side_effects=DATAFLOW_EFFECT),
    )(_in_hbm(src), _in_hbm(land), after)


def _split_wait(started, after, make_copies, name):
    send_sems, recv_sems, src_thru, land_thru, _ = started

    def body(src_ref, land_ref, send_sems_ref, recv_sems_ref, after_ref, src_dead, land_out):
        for cp in make_copies(src_ref, land_ref, send_sems_ref, recv_sems_ref):
            cp.wait_send()
            cp.wait_recv()

    return pl.pallas_call(
        body, name=name,
        out_shape=(pltpu.HBM(src_thru.shape, src_thru.dtype), pltpu.HBM(land_thru.shape, land_thru.dtype)),
        in_specs=(HBM_SPEC, HBM_SPEC, SEM_SPEC, SEM_SPEC, ANY), out_specs=(HBM_SPEC, HBM_SPEC),
        input_output_aliases={0: 0, 1: 1},
        compiler_params=pltpu.CompilerParams(has_side_effects=DATAFLOW_EFFECT),
    )(src_thru, land_thru, send_sems, recv_sems, after)


def _gather_copies(x_ref, land_ref, send_sems, recv_sems):
    mx, my, mc = _my_position()
    me = 4 * mx + 2 * my + mc
    targets = [(mx, my, 1 - mc), (1 - mx, my, mc), (mx, 1 - my, mc), (1 - mx, 1 - my, mc)]
    return [pltpu.make_async_remote_copy(
        src_ref=land_ref.at[me], dst_ref=land_ref.at[me], send_sem=send_sems.at[k], recv_sem=recv_sems.at[k],
        device_id=t, device_id_type=MESH) for k, t in enumerate(targets)]


def _own_slot(shard, dev, name):
    r, c = shard.shape
    tr = _divisor_tile(r, 640, 16)

    def body(s_ref, x_ref, o_ref):
        o_ref[...] = x_ref[...].astype(o_ref.dtype)

    return pl.pallas_call(
        body, name=name,
        out_shape=jax.ShapeDtypeStruct((N_DEV, r, c), WIRE_DTYPE),
        grid_spec=pltpu.PrefetchScalarGridSpec(
            num_scalar_prefetch=1, grid=(r // tr,),
            in_specs=[pl.BlockSpec((tr, c), lambda i, s: (i, 0))],
            out_specs=pl.BlockSpec((None, tr, c), lambda i, s: (s[0], i, 0))),
        compiler_params=_params(("parallel",)),
    )(dev, shard)


def _gather_start(shard, dev, after, name):
    return _split_start(jnp.zeros((8, 128), F32), _own_slot(shard, dev, name + "_own"), _gather_copies, 4, after, name)


def _gather_ring(shard, small, dev, name):
    r, c = shard.shape
    half = r // 2
    assert half % 16 == 0

    def body(x_ref, small_ref, land_in, land_ref, small_land, send_sems, recv_sems, local_sem):
        mx, my, mc = _my_position()
        sibling, x_nbr, y_nbr = (mx, my, 1 - mc), (1 - mx, my, mc), (mx, 1 - my, mc)
        first, second = pl.ds(0, half), pl.ds(half, half)

        def slot(px, py, pc):
            return land_ref.at[4 * px + 2 * py + pc]

        def copy(k, src, dst, to):
            return pltpu.make_async_remote_copy(src_ref=src, dst_ref=dst, send_sem=send_sems.at[k],
                                                recv_sem=recv_sems.at[k], device_id=to, device_id_type=MESH)

        def arrived(k, dst):
            copy(k, dst, dst, sibling).wait_recv()

        mine = slot(mx, my, mc)
        from_x, from_y, from_d = slot(1 - mx, my, mc), slot(mx, 1 - my, mc), slot(1 - mx, 1 - my, mc)
        sent = [copy(0, x_ref, mine, sibling), copy(1, x_ref, mine, x_nbr), copy(2, x_ref, mine, y_nbr)]
        for cp in sent:
            cp.start()

        def send(k, src, to):
            cp = copy(k, src, src, to)
            cp.start()
            sent.append(cp)

        my_small = small_land.at[4 * mx + 2 * my + mc]
        own_small = pltpu.make_async_copy(small_ref, my_small, local_sem)
        own_small.start()
        for rel in range(1, N_DEV):
            bx, by, bc = (rel >> 2) & 1, (rel >> 1) & 1, rel & 1
            cp = copy(8 + rel, small_ref, my_small, (1 - mx if bx else mx, 1 - my if by else my, 1 - mc if bc else mc))
            cp.start()
            sent.append(cp)

        arrived(1, from_x)
        send(3, from_x.at[first], y_nbr)
        send(5, from_x, sibling)
        arrived(2, from_y)
        send(4, from_y.at[second], x_nbr)
        send(6, from_y, sibling)
        arrived(3, from_d.at[first])
        send(7, from_d.at[first], sibling)
        arrived(4, from_d.at[second])
        send(8, from_d.at[second], sibling)
        arrived(0, slot(mx, my, 1 - mc))
        arrived(5, slot(1 - mx, my, 1 - mc))
        arrived(6, slot(mx, 1 - my, 1 - mc))
        arrived(7, slot(1 - mx, 1 - my, 1 - mc).at[first])
        arrived(8, slot(1 - mx, 1 - my, 1 - mc).at[second])
        for rel in range(1, N_DEV):
            arrived(8 + rel, my_small)
        for cp in sent:
            cp.wait_send()
        own_small.wait()

    land = _own_slot(shard, dev, name + "_own")
    return pl.pallas_call(
        body, name=name,
        out_shape=(jax.ShapeDtypeStruct(land.shape, land.dtype), jax.ShapeDtypeStruct((N_DEV,) + small.shape, small.dtype)),
        in_specs=[ANY, ANY, ANY], out_specs=(ANY, ANY),
        input_output_aliases={2: 0},
        scratch_shapes=[pltpu.SemaphoreType.DMA((16,)), pltpu.SemaphoreType.DMA((16,)), pltpu.SemaphoreType.DMA],
    )(shard, small, land)


def _pass_copies(unused_ref, land_ref, send_sems, recv_sems):
    mx, my, mc = _my_position()
    chips = [(1 - mx, my), (mx, 1 - my), (1 - mx, 1 - my)]
    return [pltpu.make_async_remote_copy(
        src_ref=land_ref.at[4 * cx + 2 * cy + mc], dst_ref=land_ref.at[4 * cx + 2 * cy + mc],
        send_sem=send_sems.at[j], recv_sem=recv_sems.at[j],
        device_id=(mx, my, 1 - mc), device_id_type=MESH) for j, (cx, cy) in enumerate(chips)]


def _gather_pass_start(started, after, name):
    _, land = _split_wait(started, after, _gather_copies, name + "_wait")
    return _split_start(jnp.zeros((8, 128), F32), land, _pass_copies, 3, after, name + "_pass_start")


def _gather_pass_finish(pass_started, after, name):
    return _split_wait(pass_started, after, _pass_copies, name + "_pass_wait")[1]


def _gather_finish(started, after, name):
    _, land = _split_wait(started, after, _gather_copies, name + "_wait")

    def body(land_in, land_ref, send_sems, recv_sems):
        mx, my, mc = _my_position()
        chips = [(1 - mx, my), (mx, 1 - my), (1 - mx, 1 - my)]
        copies = [pltpu.make_async_remote_copy(
            src_ref=land_ref.at[4 * cx + 2 * cy + mc], dst_ref=land_ref.at[4 * cx + 2 * cy + mc],
            send_sem=send_sems.at[j], recv_sem=recv_sems.at[j],
            device_id=(mx, my, 1 - mc), device_id_type=MESH) for j, (cx, cy) in enumerate(chips)]
        for cp in copies:
            cp.start()
        for j, (cx, cy) in enumerate(chips):
            copies[j].wait_send()
            pltpu.make_async_remote_copy(
                src_ref=land_ref.at[4 * cx + 2 * cy + 1 - mc], dst_ref=land_ref.at[4 * cx + 2 * cy + 1 - mc],
                send_sem=send_sems.at[j], recv_sem=recv_sems.at[j],
                device_id=(mx, my, 1 - mc), device_id_type=MESH).wait_recv()

    return pl.pallas_call(
        body, name=name + "_pass",
        out_shape=jax.ShapeDtypeStruct(land.shape, land.dtype),
        in_specs=[ANY], out_specs=ANY,
        input_output_aliases={0: 0},
        scratch_shapes=[pltpu.SemaphoreType.DMA((3,)), pltpu.SemaphoreType.DMA((3,))],
    )(land)


def _chip_copies(p_ref, land_ref, send_sems, recv_sems):
    mx, my, mc = _my_position()
    chips = [(1 - mx, my), (mx, 1 - my), (1 - mx, 1 - my)]
    return [pltpu.make_async_remote_copy(
        src_ref=p_ref.at[2 * cx + cy], dst_ref=land_ref.at[j], send_sem=send_sems.at[j], recv_sem=recv_sems.at[j],
        device_id=(cx, cy, mc), device_id_type=MESH) for j, (cx, cy) in enumerate(chips)]


def _reduce_scatter_start(g, core, name):
    pair = _pair_sum(g, _exchange_sibling(g, name + "_d2d"), core, name + "_pairsum")
    return _split_start(pair, (3,) + pair.shape[1:], _chip_copies, 3, g, name + "_ici_start")


def _sibling_copies(g_ref, land_ref, send_sems, recv_sems):
    mx, my, mc = _my_position()
    return [pltpu.make_async_remote_copy(
        src_ref=g_ref.at[2 * k + (1 - mc)], dst_ref=land_ref.at[k], send_sem=send_sems.at[k], recv_sem=recv_sems.at[k],
        device_id=(mx, my, 1 - mc), device_id_type=MESH) for k in range(4)]


def _reduce_scatter_d2d_start(g, after, name):
    return _split_start(g, (4,) + g.shape[1:], _sibling_copies, 4, after, name + "_d2d_start")


def _reduce_scatter_ici_start(d2d_started, after, core, name):
    g, from_sibling = _split_wait(d2d_started, after, _sibling_copies, name + "_d2d_wait")
    pair = _pair_sum(g, from_sibling, core, name + "_pairsum")
    return _split_start(pair, (3,) + pair.shape[1:], _chip_copies, 3, g, name + "_ici_start")


def _reduce_scatter_finish(started, after, chip, w, m, v, name):
    pair, from_chips = _split_wait(started, after, _chip_copies, name + "_ici_wait")
    return _final_sum_adamw(pair, from_chips, chip, w, m, v, name + "_sum_adamw")


def _exchange_sibling(g, name):
    _, r, c = g.shape

    def body(g_ref, out_ref, send_sems, recv_sems):
        mx, my, mc = _my_position()
        copies = [
            pltpu.make_async_remote_copy(
                src_ref=g_ref.at[2 * k + (1 - mc)], dst_ref=out_ref.at[k],
                send_sem=send_sems.at[k], recv_sem=recv_sems.at[k],
                device_id=(mx, my, 1 - mc), device_id_type=MESH)
            for k in range(4)]
        for cp in copies:
            cp.start()
        for cp in copies:
            cp.wait()

    return pl.pallas_call(
        body, name=name,
        out_shape=jax.ShapeDtypeStruct((4, r, c), g.dtype),
        in_specs=[ANY], out_specs=ANY,
        scratch_shapes=[pltpu.SemaphoreType.DMA((4,)), pltpu.SemaphoreType.DMA((4,))],
    )(g)


def _pair_sum(g, recv, core, name):
    _, r, c = g.shape
    tr = _divisor_tile(r, 512, 16)

    def body(s_ref, g_ref, r_ref, o_ref):
        o_ref[...] = (g_ref[...].astype(F32) + r_ref[...].astype(F32)).astype(o_ref.dtype)

    return pl.pallas_call(
        body, name=name,
        out_shape=jax.ShapeDtypeStruct((4, r, c), g.dtype),
        grid_spec=pltpu.PrefetchScalarGridSpec(
            num_scalar_prefetch=1, grid=(4, r // tr),
            in_specs=[pl.BlockSpec((None, tr, c), lambda k, i, s: (2 * k + s[0], i, 0)),
                      pl.BlockSpec((None, tr, c), lambda k, i, s: (k, i, 0))],
            out_specs=pl.BlockSpec((None, tr, c), lambda k, i, s: (k, i, 0))),
        compiler_params=_params(("parallel", "parallel")),
    )(core, g, recv)


def _adamw_math(w, g, m, v):
    nm = ADAM_B1 * m + (1.0 - ADAM_B1) * g
    nv = ADAM_B2 * v + (1.0 - ADAM_B2) * (g * g)
    m_hat = nm / (1.0 - ADAM_B1 ** ADAM_STEP)
    v_hat = nv / (1.0 - ADAM_B2 ** ADAM_STEP)
    return -ADAM_LR * (m_hat / (jnp.sqrt(v_hat) + ADAM_EPS) + ADAM_WD * w), nm, nv


def _final_sum_adamw(p, recv, chip, w, m, v, name):
    _, r, c = p.shape
    tr = _divisor_tile(r, 256, 16)
    tile = lambda: pl.BlockSpec((tr, c), lambda i, s: (i, 0))

    def body(s_ref, p_ref, r_ref, w_ref, m_ref, v_ref, g_ref, d_ref, nm_ref, nv_ref):
        g = p_ref[...].astype(F32)
        for j in range(3):
            g = g + r_ref[j].astype(F32)
        g_ref[...] = g
        d_ref[...], nm_ref[...], nv_ref[...] = _adamw_math(w_ref[...], g, m_ref[...], v_ref[...])

    sds = jax.ShapeDtypeStruct((r, c), F32)
    return pl.pallas_call(
        body, name=name,
        out_shape=(sds, sds, sds, sds),
        grid_spec=pltpu.PrefetchScalarGridSpec(
            num_scalar_prefetch=1, grid=(r // tr,),
            in_specs=[pl.BlockSpec((None, tr, c), lambda i, s: (s[0], i, 0)),
                      pl.BlockSpec((3, tr, c), lambda i, s: (0, i, 0)), tile(), tile(), tile()],
            out_specs=(tile(), tile(), tile(), tile())),
        compiler_params=_params(("parallel",)),
    )(chip, p, recv, w, m, v)


def _all_to_all_copies(v_ref, land_ref, send_sems, recv_sems):
    mx, my, mc = _my_position()
    me = 4 * mx + 2 * my + mc
    copies = []
    for rel in range(1, N_DEV):
        bx, by, bc = (rel >> 2) & 1, (rel >> 1) & 1, rel & 1
        target = (1 - mx if bx else mx, 1 - my if by else my, 1 - mc if bc else mc)
        copies.append(pltpu.make_async_remote_copy(
            src_ref=v_ref, dst_ref=land_ref.at[me], send_sem=send_sems.at[rel - 1], recv_sem=recv_sems.at[rel - 1],
            device_id=target, device_id_type=MESH))
    return copies


def _small_all_reduce_start(v, after, name):
    return _split_start(v, (N_DEV,) + v.shape, _all_to_all_copies, N_DEV - 1, after, name + "_start")


def _small_all_reduce_finish(started, after, dev, name):
    v, land = _split_wait(started, after, _all_to_all_copies, name + "_wait")
    rows = v.shape[0]

    def body(me_ref, v_ref, land_ref, o_ref):
        for j in range(N_DEV):
            @pl.when(me_ref[0] == j)
            def _():
                o_ref[...] = v_ref[...] if j == 0 else o_ref[...] + v_ref[...]

            @pl.when(me_ref[0] != j)
            def _():
                o_ref[...] = land_ref[j] if j == 0 else o_ref[...] + land_ref[j]

    return pl.pallas_call(
        body, name=name + "_sum",
        out_shape=jax.ShapeDtypeStruct((rows, 128), F32),
        grid_spec=pltpu.PrefetchScalarGridSpec(
            num_scalar_prefetch=1, grid=(1,),
            in_specs=[pl.BlockSpec((rows, 128), lambda i, s: (0, 0)),
                      pl.BlockSpec((N_DEV, rows, 128), lambda i, s: (0, 0, 0))],
            out_specs=pl.BlockSpec((rows, 128), lambda i, s: (0, 0))),
        compiler_params=_params(("arbitrary",)),
    )(dev, v, land)


def _assemble_w_in(blocks):
    rows, d = WIN_ROWS, blocks.shape[2]
    tc = _divisor_tile(d, 256, 128)
    n_tiles = WIN_N // 128
    last = (N_DEV * WIN_STRIDE) // 128

    def body(b_ref, o_ref):
        win = []
        for i in range(N_DEV):
            w = jnp.concatenate([b_ref[i].astype(F32), jnp.zeros((WIN_BLOCK - rows, tc), F32)], axis=0)
            win.append(pltpu.roll(w, i, 0) if i else w)
        for t in range(n_tiles):
            if t > last:
                o_ref[t * 128:(t + 1) * 128, :] = jnp.zeros((128, tc), o_ref.dtype)
                continue
            i = min(t // 7, N_DEV - 1)
            k = t - 7 * i
            val = win[i][k * 128:(k + 1) * 128, :]
            if k == 0 and i >= 1:
                val = val + win[i - 1][7 * 128:8 * 128, :]
            o_ref[t * 128:(t + 1) * 128, :] = val.astype(o_ref.dtype)

    return pl.pallas_call(
        body, name="assemble_w_in",
        out_shape=jax.ShapeDtypeStruct((WIN_N, d), blocks.dtype),
        grid=(d // tc,),
        in_specs=[pl.BlockSpec((N_DEV, rows, tc), lambda j: (0, 0, j))],
        out_specs=pl.BlockSpec((WIN_N, tc), lambda j: (0, j)),
        compiler_params=_params(("parallel",)),
    )(blocks)


def _extract_w_in_windows(g):
    _, d = g.shape
    tc = _divisor_tile(d, 256, 128)

    def body(g_ref, o_ref):
        for j in range(N_DEV):
            w = g_ref[WIN_STRIDE * j:WIN_STRIDE * j + WIN_BLOCK, :].astype(F32)
            w = pltpu.roll(w, WIN_BLOCK - j, 0) if j else w
            o_ref[j] = w[0:WIN_ROWS, :].astype(o_ref.dtype)

    return pl.pallas_call(
        body, name="extract_w_in_windows",
        out_shape=jax.ShapeDtypeStruct((N_DEV, WIN_ROWS, d), g.dtype),
        grid=(d // tc,),
        in_specs=[pl.BlockSpec((WIN_N, tc), lambda j: (0, j))],
        out_specs=pl.BlockSpec((N_DEV, WIN_ROWS, tc), lambda j: (0, 0, j)),
        compiler_params=_params(("parallel",)),
    )(g)


def _mm(a, b, *, a_spec, b_spec, o_spec, out_shape, grid, contract, nk, name, after=None):
    dn = (((contract[0],), (contract[1],)), ((), ()))
    tm, tn = o_spec.block_shape[-2:]
    behind = [] if after is None else [after]

    def body(a_ref, b_ref, *rest):
        o_ref, *scratch = rest[len(behind):]
        part = lax.dot_general(a_ref[...], b_ref[...], dn, preferred_element_type=F32)
        if nk == 1:
            o_ref[...] = part.astype(o_ref.dtype)
            return
        acc = scratch[0]
        k = pl.program_id(2)

        @pl.when(k == 0)
        def _():
            acc[...] = part

        @pl.when(k > 0)
        def _():
            acc[...] += part

        @pl.when(k == nk - 1)
        def _():
            o_ref[...] = acc[...].astype(o_ref.dtype)

    return pl.pallas_call(
        body, name=name, out_shape=out_shape, grid=grid,
        in_specs=[a_spec, b_spec] + [ANY] * len(behind), out_specs=o_spec,
        scratch_shapes=[] if nk == 1 else [pltpu.VMEM((tm, tn), F32)],
        compiler_params=_params(("parallel", "parallel", "arbitrary")),
    )(a, b, *behind)


def _mm_nn(a, b, out_dtype, name, tm_cap=1088, tn_cap=512, tk_cap=2048, after=None):
    m, k = a.shape
    _, n = b.shape
    tm, tn, tk = _divisor_tile(m, tm_cap, 16), _divisor_tile(n, tn_cap, 128), _divisor_tile(k, tk_cap, 128)
    return _mm(a, b,
               a_spec=pl.BlockSpec((tm, tk), lambda i, j, kk: (i, kk)),
               b_spec=pl.BlockSpec((tk, tn), lambda i, j, kk: (kk, j)),
               o_spec=pl.BlockSpec((tm, tn), lambda i, j, kk: (i, j)),
               out_shape=jax.ShapeDtypeStruct((m, n), out_dtype),
               grid=(m // tm, n // tn, k // tk), contract=(1, 0), nk=k // tk, name=name, after=after)


def _mm_nt(a, b, out_dtype, name, tm_cap=1088, tn_cap=512, tk_cap=2048, after=None):
    m, k = a.shape
    n, _ = b.shape
    tm, tn, tk = _divisor_tile(m, tm_cap, 16), _divisor_tile(n, tn_cap, 128), _divisor_tile(k, tk_cap, 128)
    return _mm(a, b,
               a_spec=pl.BlockSpec((tm, tk), lambda i, j, kk: (i, kk)),
               b_spec=pl.BlockSpec((tn, tk), lambda i, j, kk: (j, kk)),
               o_spec=pl.BlockSpec((tm, tn), lambda i, j, kk: (i, j)),
               out_shape=jax.ShapeDtypeStruct((m, n), out_dtype),
               grid=(m // tm, n // tn, k // tk), contract=(1, 1), nk=k // tk, name=name, after=after)


def _mm_tn(a, b, out_dtype, name, tm_cap=1024, tn_cap=512, after=None):
    l, m = a.shape
    _, n = b.shape
    tm, tn = _divisor_tile(m, tm_cap, 128), _divisor_tile(n, tn_cap, 128)
    return _mm(a, b,
               a_spec=pl.BlockSpec((l, tm), lambda i, j, kk: (0, i)),
               b_spec=pl.BlockSpec((l, tn), lambda i, j, kk: (0, j)),
               o_spec=pl.BlockSpec((tm, tn), lambda i, j, kk: (i, j)),
               out_shape=jax.ShapeDtypeStruct((m, n), out_dtype),
               grid=(m // tm, n // tn, 1), contract=(0, 0), nk=1, name=name, after=after)


def _pair_split(shard):
    left = shard % ATTN_BLOCK
    assert left in (0, CHUNK) and shard > left
    return shard - left, left


def _mm_up(cn, w_up_blocks, after):
    l, d = cn.shape
    n, _, shard = w_up_blocks.shape
    main, left = _pair_split(shard)
    tm = _divisor_tile(l, 544, 16)

    def body(a_ref, b_ref, after_ref, o_ref):
        a = a_ref[...]
        for s in range(2):
            o_ref[:, s * shard:s * shard + main] = _dot(a, b_ref[s, :, 0:main])
        if left:
            tail = _dot(a, jnp.concatenate([b_ref[0, :, main:], b_ref[1, :, main:]], axis=1))
            o_ref[:, main:shard] = tail[:, 0:left]
            o_ref[:, shard + main:2 * shard] = tail[:, left:]

    return pl.pallas_call(
        body, name="mm_up", out_shape=jax.ShapeDtypeStruct((l, n * shard), F32), grid=(l // tm, n // 2),
        in_specs=[pl.BlockSpec((tm, d), lambda i, j: (i, 0)),
                  pl.BlockSpec((2, d, shard), lambda i, j: (j, 0, 0)), ANY],
        out_specs=pl.BlockSpec((tm, 2 * shard), lambda i, j: (i, j)),
        compiler_params=_params(("parallel", "parallel")),
    )(cn, w_up_blocks, after)


def _mm_gw_up(cn, d_u):
    l, d = cn.shape
    _, _, d_ff = d_u.shape
    shard = 2 * d_ff // N_DEV
    pairs_per_half = d_ff // (2 * shard)
    tm = _divisor_tile(d, 512, 128)

    def body(a_ref, b_ref, o_ref):
        res = _dot_tn(a_ref[...], b_ref[...])
        o_ref[0] = res[:, 0:shard].astype(o_ref.dtype)
        o_ref[1] = res[:, shard:].astype(o_ref.dtype)

    return pl.pallas_call(
        body, name="mm_gw_up", out_shape=jax.ShapeDtypeStruct((N_DEV, d, shard), WIRE_DTYPE),
        grid=(d // tm, N_DEV // 2),
        in_specs=[pl.BlockSpec((l, tm), lambda i, j: (0, i)),
                  pl.BlockSpec((None, l, 2 * shard), lambda i, j: (j // pairs_per_half, 0, j % pairs_per_half))],
        out_specs=pl.BlockSpec((2, tm, shard), lambda i, j: (j, i, 0)),
        compiler_params=_params(("parallel", "parallel")),
    )(cn, d_u)


def _mm_d_cn(d_u, w_up_blocks, after):
    _, l, d_ff = d_u.shape
    n, d, shard = w_up_blocks.shape
    per = d_ff // shard
    main, left = _pair_split(shard)
    tm, tn = _divisor_tile(l, 544, 16), _divisor_tile(d, 256, 128)

    def body(a_ref, b_ref, after_ref, o_ref):
        acc = None
        for k in range(0, n, 2):
            half, c0 = k // per, (k % per) * shard
            parts = [_dot_nt(a_ref[half, :, c0 + s * shard:c0 + s * shard + main], b_ref[k + s, :, 0:main])
                     for s in range(2)]
            if left:
                a_tail = jnp.concatenate([a_ref[half, :, c0 + s * shard + main:c0 + (s + 1) * shard] for s in range(2)],
                                         axis=1)
                b_tail = jnp.concatenate([b_ref[k + s, :, main:] for s in range(2)], axis=1)
                parts.append(_dot_nt(a_tail, b_tail))
            for part in parts:
                acc = part if acc is None else acc + part
        o_ref[...] = acc

    return pl.pallas_call(
        body, name="mm_d_cn", out_shape=jax.ShapeDtypeStruct((l, d), F32), grid=(l // tm, d // tn),
        in_specs=[pl.BlockSpec((2, tm, d_ff), lambda i, j: (0, i, 0)),
                  pl.BlockSpec((n, tn, shard), lambda i, j: (0, j, 0)), ANY],
        out_specs=pl.BlockSpec((tm, tn), lambda i, j: (i, j)),
        compiler_params=_params(("parallel", "parallel")),
    )(d_u, w_up_blocks, after)


def _row_tile(l):
    return _divisor_tile(l, 544, 8)


def _rms(x, gain):
    return (x * lax.rsqrt(jnp.mean(x * x, axis=-1, keepdims=True) + NORM_EPS) * gain).astype(MXU_DTYPE)


def _embed_rmsnorm(x, meta, gain):
    seq, d = x.shape
    l = CHUNK + seq
    row = pl.BlockSpec((CHUNK, d), lambda i: (i, 0))

    def body(x_ref, m_ref, g_ref, h_ref, n_ref):
        @pl.when(pl.program_id(0) == 0)
        def _():
            h_ref[...] = jnp.concatenate([jnp.zeros((PAD_ROWS, d), F32), m_ref[...]], axis=0)

        @pl.when(pl.program_id(0) > 0)
        def _():
            h_ref[...] = x_ref[...]

        n_ref[...] = _rms(h_ref[...], g_ref[...])

    return pl.pallas_call(
        body, name="embed_rmsnorm1",
        out_shape=(jax.ShapeDtypeStruct((l, d), F32), jax.ShapeDtypeStruct((l, d), MXU_DTYPE)),
        grid=(l // CHUNK,),
        in_specs=[pl.BlockSpec((CHUNK, d), lambda i: (jnp.maximum(i - 1, 0), 0)),
                  pl.BlockSpec((N_META, d), lambda i: (0, 0)), pl.BlockSpec((1, d), lambda i: (0, 0))],
        out_specs=(row, row),
        compiler_params=_params(("parallel",)),
    )(x, meta, gain)


def _out_proj_resid_rmsnorm(mix, w_out, h0, gain):
    l, d = h0.shape
    tm = _divisor_tile(l, 272, 16)
    row = pl.BlockSpec((tm, d), lambda i: (i, 0))

    def body(a_ref, b_ref, h_ref, g_ref, s_ref, n_ref):
        x = h_ref[...] + _dot(a_ref[...], b_ref[...])
        s_ref[...] = x
        n_ref[...] = _rms(x, g_ref[...])

    return pl.pallas_call(
        body, name="mm_out_resid_rmsnorm2",
        out_shape=(jax.ShapeDtypeStruct((l, d), F32), jax.ShapeDtypeStruct((l, d), MXU_DTYPE)),
        grid=(l // tm,),
        in_specs=[row, pl.BlockSpec((d, d), lambda i: (0, 0)), row, pl.BlockSpec((1, d), lambda i: (0, 0))],
        out_specs=(row, row),
        compiler_params=_params(("parallel",)),
    )(mix, w_out, h0, gain)


def _rmsnorm_bwd(d_res, d_normed, x, gain, name, with_mxu_copy):
    l, d = x.shape
    tr = _row_tile(l) if with_mxu_copy else CHUNK
    row = pl.BlockSpec((tr, d), lambda i: (i, 0))
    vec = pl.BlockSpec((1, d), lambda i: (0, 0))

    def body(dres_ref, dn_ref, x_ref, g_ref, dx_ref, other_ref, dg_ref):
        i = pl.program_id(0)
        xv = x_ref[...]
        r = lax.rsqrt(jnp.mean(xv * xv, axis=-1, keepdims=True) + NORM_EPS)
        xh = xv * r
        dn = dn_ref[...]
        dxh = dn * g_ref[...]
        dx = dres_ref[...] + r * (dxh - xh * jnp.mean(dxh * xh, axis=-1, keepdims=True))
        if with_mxu_copy:
            dx_ref[...] = dx
            other_ref[...] = dx.astype(MXU_DTYPE)
        else:
            @pl.when(i == 0)
            def _():
                dx_ref[...] = dx

            @pl.when(i > 0)
            def _():
                other_ref[...] = dx

        @pl.when(i == 0)
        def _():
            dg_ref[...] = jnp.zeros_like(dg_ref)

        dg_ref[...] += jnp.sum(dn * xh, axis=0, keepdims=True)

    if with_mxu_copy:
        outs = [jax.ShapeDtypeStruct((l, d), F32), jax.ShapeDtypeStruct((l, d), MXU_DTYPE)]
        specs = [row, row]
    else:
        outs = [jax.ShapeDtypeStruct((CHUNK, d), F32), jax.ShapeDtypeStruct((l - CHUNK, d), F32)]
        specs = [pl.BlockSpec((CHUNK, d), lambda i: (0, 0)), pl.BlockSpec((CHUNK, d), lambda i: (jnp.maximum(i - 1, 0), 0))]
    outs.append(jax.ShapeDtypeStruct((1, d), F32))
    specs.append(vec)
    return pl.pallas_call(body, name=name, out_shape=tuple(outs), grid=(l // tr,),
                          in_specs=[row, row, row, vec], out_specs=tuple(specs),
                          compiler_params=_params(("arbitrary",)))(d_res, d_normed, x, gain)


def _loss_head(h1, mlp_out, gain, target):
    l, d = h1.shape
    n_blocks = l // CHUNK
    row = pl.BlockSpec((CHUNK, d), lambda i: (i, 0))
    vec = pl.BlockSpec((1, d), lambda i: (0, 0))
    tgt = pl.BlockSpec((CHUNK, d), lambda i: (jnp.maximum(i - 1, 0), 0))

    def body(h_ref, m_ref, g_ref, t_ref, dh_ref, dhb_ref, dg_ref, loss_ref, sq_ref):
        i = pl.program_id(0)
        x = h_ref[...] + m_ref[...]
        r = lax.rsqrt(jnp.mean(x * x, axis=-1, keepdims=True) + NORM_EPS)
        xh = x * r
        g = g_ref[...]
        real = i >= 1
        err = jnp.where(real, xh * g - t_ref[...], 0.0)
        dy = err * (1.0 / d)
        dxh = dy * g
        dh = r * (dxh - xh * jnp.mean(dxh * xh, axis=-1, keepdims=True))
        dh_ref[...] = dh
        dhb_ref[...] = dh.astype(MXU_DTYPE)

        @pl.when(i == 0)
        def _():
            dg_ref[...] = jnp.zeros_like(dg_ref)
            sq_ref[...] = jnp.zeros_like(sq_ref)

        dg_ref[...] += jnp.sum(dy * xh, axis=0, keepdims=True)
        sq_ref[...] += jnp.sum(err * err, axis=0, keepdims=True)

        @pl.when(i == n_blocks - 1)
        def _():
            total = jnp.sum(sq_ref[...], axis=-1, keepdims=True) * (0.5 / d)
            loss_ref[...] = jnp.broadcast_to(total, (1, 128))

    return pl.pallas_call(
        body, name="loss_head",
        out_shape=(jax.ShapeDtypeStruct((l, d), F32), jax.ShapeDtypeStruct((l, d), MXU_DTYPE),
                   jax.ShapeDtypeStruct((1, d), F32), jax.ShapeDtypeStruct((1, 128), F32)),
        grid=(n_blocks,), in_specs=[row, row, vec, tgt],
        out_specs=(row, row, vec, pl.BlockSpec((1, 128), lambda i: (0, 0))),
        scratch_shapes=[pltpu.VMEM((1, d), F32)],
        compiler_params=_params(("arbitrary",)),
    )(h1, mlp_out, gain, target)


def _dot(a, b):
    return jnp.dot(a, b, preferred_element_type=F32)


def _dot_nt(a, b):
    return lax.dot_general(a, b, (((1,), (1,)), ((), ())), preferred_element_type=F32)


def _dot_tn(a, b):
    return lax.dot_general(a, b, (((0,), (0,)), ((), ())), preferred_element_type=F32)


def _rope(t, cos2, sin2):
    return t * cos2 + pltpu.roll(t, HEAD_DIM // 2, 1) * sin2


def _rope_bwd(dr, cos2, sin2):
    return dr * cos2 + pltpu.roll(dr * sin2, HEAD_DIM // 2, 1)


def _sigmoid(x):
    return 1.0 / (1.0 + jnp.exp(-x))


def _row_valid(block, rows):
    r = block * CHUNK + lax.broadcasted_iota(jnp.int32, (rows, 1), 0)
    return r >= PAD_ROWS


def _retention_consts(l):
    pos = jnp.arange(l, dtype=F32) - PAD_ROWS
    inv_freq = 1.0 / (ROPE_BASE ** (jnp.arange(0, HEAD_DIM, 2, dtype=F32) / HEAD_DIM))
    ang = pos[:, None] * inv_freq[None, :]
    cos, sin = jnp.cos(ang), jnp.sin(ang)
    cos2 = jnp.concatenate([cos, cos], axis=-1)
    sin2 = jnp.concatenate([-sin, sin], axis=-1)
    log_g = jnp.log1p(-jnp.exp2(-5.0 - jnp.arange(N_HEADS, dtype=F32)))
    idx = jnp.arange(CHUNK, dtype=F32)
    diff = idx[:, None] - idx[None, :]
    decay = jnp.where(diff >= 0, jnp.exp(jnp.maximum(diff, 0.0)[None] * log_g[:, None, None]), 0.0)
    xi = jnp.exp((idx + 1.0)[None, :] * log_g[:, None])
    zeta = jnp.exp((CHUNK - 1.0 - idx)[None, :] * log_g[:, None])
    g_chunk = jnp.exp(CHUNK * log_g)
    bcast = lambda v: jnp.broadcast_to(v[:, :, None], (N_HEADS, CHUNK, HEAD_DIM))
    g_rows = jnp.broadcast_to(g_chunk[:, None, None], (N_HEADS, 8, HEAD_DIM))
    return cos2, sin2, decay, bcast(xi), bcast(zeta), g_rows


def _retention_fwd(proj, ret_gain, consts):
    l = proj.shape[0]
    n_chunks = l // CHUNK
    cos2, sin2, decay, xi, zeta, g_rows = consts
    scale = HEAD_DIM ** -0.5

    def body(p_ref, cos_ref, sin_ref, dec_ref, xi_ref, zeta_ref, gr_ref, gain_ref,
             mix_ref, o_ref, st_ref, state):
        c = pl.program_id(0)

        @pl.when(c == 0)
        def _():
            state[...] = jnp.zeros_like(state)

        cos_v, sin_v = cos_ref[...], sin_ref[...]
        valid = _row_valid(c, CHUNK)
        for h in range(N_HEADS):
            cols = slice(h * HEAD_DIM, (h + 1) * HEAD_DIM)
            q = p_ref[:, h * HEAD_DIM:(h + 1) * HEAD_DIM]
            k = p_ref[:, GROUP + h * HEAD_DIM:GROUP + (h + 1) * HEAD_DIM]
            v = p_ref[:, 2 * GROUP + h * HEAD_DIM:2 * GROUP + (h + 1) * HEAD_DIM]
            g = p_ref[:, 3 * GROUP + h * HEAD_DIM:3 * GROUP + (h + 1) * HEAD_DIM]
            rq = _rope(q, cos_v, sin_v).astype(MXU_DTYPE)
            rk = _rope(k, cos_v, sin_v) * scale
            rkb = rk.astype(MXU_DTYPE)
            vb = v.astype(MXU_DTYPE)
            st = state[h]
            st_ref[h] = st
            s = _dot_nt(rq, rkb) * dec_ref[h]
            o = _dot(s.astype(MXU_DTYPE), vb) + _dot(rq, st.astype(MXU_DTYPE)) * xi_ref[h]
            kz = (rk * zeta_ref[h]).astype(MXU_DTYPE)
            state[h] = gr_ref[h, 0:1, :] * st + _dot_tn(kz, vb)
            o_ref[:, cols] = o
            mu = jnp.mean(o, axis=-1, keepdims=True)
            oc = o - mu
            yn = oc * lax.rsqrt(jnp.mean(oc * oc, axis=-1, keepdims=True) + NORM_EPS)
            ret = (g * _sigmoid(g)) * (yn * gain_ref[:, cols])
            mix_ref[:, cols] = jnp.where(valid, ret, 0.0).astype(mix_ref.dtype)

    head_tab = pl.BlockSpec((N_HEADS, CHUNK, HEAD_DIM), lambda c: (0, 0, 0))
    return pl.pallas_call(
        body, name="retention_fwd",
        out_shape=(jax.ShapeDtypeStruct((l, 2 * GROUP), MXU_DTYPE), jax.ShapeDtypeStruct((l, GROUP), F32),
                   jax.ShapeDtypeStruct((n_chunks, N_HEADS, HEAD_DIM, HEAD_DIM), F32)),
        grid=(n_chunks,),
        in_specs=[pl.BlockSpec((CHUNK, 4 * GROUP), lambda c: (c, 0)),
                  pl.BlockSpec((CHUNK, HEAD_DIM), lambda c: (c, 0)),
                  pl.BlockSpec((CHUNK, HEAD_DIM), lambda c: (c, 0)),
                  head_tab, head_tab, head_tab,
                  pl.BlockSpec((N_HEADS, 8, HEAD_DIM), lambda c: (0, 0, 0)),
                  pl.BlockSpec((1, GROUP), lambda c: (0, 0))],
        out_specs=(pl.BlockSpec((CHUNK, GROUP), lambda c: (c, 0)),
                   pl.BlockSpec((CHUNK, GROUP), lambda c: (c, 0)),
                   pl.BlockSpec((None, N_HEADS, HEAD_DIM, HEAD_DIM), lambda c: (c, 0, 0, 0))),
        scratch_shapes=[pltpu.VMEM((N_HEADS, HEAD_DIM, HEAD_DIM), F32)],
        compiler_params=_params(("arbitrary",)),
    )(proj, cos2, sin2, decay, xi, zeta, g_rows, ret_gain)


def _retention_bwd(proj, o_pre, states, d_mix, ret_gain, consts):
    l = proj.shape[0]
    n_chunks = l // CHUNK
    cos2, sin2, decay, xi, zeta, g_rows = consts
    scale = HEAD_DIM ** -0.5
    rev = lambda c: n_chunks - 1 - c

    def body(p_ref, o_ref, st_ref, dm_ref, cos_ref, sin_ref, dec_ref, dect_ref, xi_ref, zeta_ref, gr_ref, gain_ref,
             dp_ref, dgain_ref, dstate):
        step = pl.program_id(0)

        @pl.when(step == 0)
        def _():
            dstate[...] = jnp.zeros_like(dstate)
            dgain_ref[...] = jnp.zeros_like(dgain_ref)

        cos_v, sin_v = cos_ref[...], sin_ref[...]
        valid = _row_valid(rev(step), CHUNK)
        for h in range(N_HEADS):
            cols = slice(h * HEAD_DIM, (h + 1) * HEAD_DIM)
            q = p_ref[:, h * HEAD_DIM:(h + 1) * HEAD_DIM]
            k = p_ref[:, GROUP + h * HEAD_DIM:GROUP + (h + 1) * HEAD_DIM]
            v = p_ref[:, 2 * GROUP + h * HEAD_DIM:2 * GROUP + (h + 1) * HEAD_DIM]
            g = p_ref[:, 3 * GROUP + h * HEAD_DIM:3 * GROUP + (h + 1) * HEAD_DIM]
            o = o_ref[:, cols]
            gain = gain_ref[:, cols]
            d_ret = jnp.where(valid, dm_ref[:, cols], 0.0)
            mu = jnp.mean(o, axis=-1, keepdims=True)
            oc = o - mu
            rstd = lax.rsqrt(jnp.mean(oc * oc, axis=-1, keepdims=True) + NORM_EPS)
            yn = oc * rstd
            sig = _sigmoid(g)
            gate = g * sig
            dgain_ref[:, cols] += jnp.sum(d_ret * gate * yn, axis=0, keepdims=True)
            d_g = d_ret * (yn * gain) * (sig * (1.0 + g * (1.0 - sig)))
            d_yn = d_ret * gate * gain
            d_o = rstd * (d_yn - jnp.mean(d_yn, axis=-1, keepdims=True)
                          - yn * jnp.mean(d_yn * yn, axis=-1, keepdims=True))
            rq = _rope(q, cos_v, sin_v)
            rk = _rope(k, cos_v, sin_v) * scale
            rqb, rkb, vb = rq.astype(MXU_DTYPE), rk.astype(MXU_DTYPE), v.astype(MXU_DTYPE)
            dob = d_o.astype(MXU_DTYPE)
            dec = dec_ref[h]
            xi_h, zeta_h = xi_ref[h], zeta_ref[h]
            st_b = st_ref[h].astype(MXU_DTYPE)
            dst = dstate[h]
            dst_b = dst.astype(MXU_DTYPE)
            dec_t = dect_ref[h]
            s_t_b = (_dot_nt(rkb, rqb) * dec_t).astype(MXU_DTYPE)
            da_b = (_dot_nt(dob, vb) * dec).astype(MXU_DTYPE)
            da_t_b = (_dot_nt(vb, dob) * dec_t).astype(MXU_DTYPE)
            doxi_b = (d_o * xi_h).astype(MXU_DTYPE)
            kz_b = (rk * zeta_h).astype(MXU_DTYPE)
            d_rq = _dot(da_b, rkb) + _dot_nt(doxi_b, st_b)
            d_rk = _dot(da_t_b, rqb) + _dot_nt(vb, dst_b) * zeta_h
            d_v = _dot(s_t_b, dob) + _dot(kz_b, dst_b)
            dstate[h] = gr_ref[h, 0:1, :] * dst + _dot_tn(rqb, doxi_b)
            d_q = _rope_bwd(d_rq, cos_v, sin_v)
            d_k = _rope_bwd(d_rk * scale, cos_v, sin_v)
            dp_ref[:, h * HEAD_DIM:(h + 1) * HEAD_DIM] = d_q.astype(dp_ref.dtype)
            dp_ref[:, GROUP + h * HEAD_DIM:GROUP + (h + 1) * HEAD_DIM] = d_k.astype(dp_ref.dtype)
            dp_ref[:, 2 * GROUP + h * HEAD_DIM:2 * GROUP + (h + 1) * HEAD_DIM] = d_v.astype(dp_ref.dtype)
            dp_ref[:, 3 * GROUP + h * HEAD_DIM:3 * GROUP + (h + 1) * HEAD_DIM] = d_g.astype(dp_ref.dtype)

    head_tab = pl.BlockSpec((N_HEADS, CHUNK, HEAD_DIM), lambda c: (0, 0, 0))
    return pl.pallas_call(
        body, name="retention_bwd",
        out_shape=(jax.ShapeDtypeStruct((l, 4 * GROUP), MXU_DTYPE), jax.ShapeDtypeStruct((1, GROUP), F32)),
        grid=(n_chunks,),
        in_specs=[pl.BlockSpec((CHUNK, 4 * GROUP), lambda c: (rev(c), 0)),
                  pl.BlockSpec((CHUNK, GROUP), lambda c: (rev(c), 0)),
                  pl.BlockSpec((None, N_HEADS, HEAD_DIM, HEAD_DIM), lambda c: (rev(c), 0, 0, 0)),
                  pl.BlockSpec((CHUNK, GROUP), lambda c: (rev(c), 0)),
                  pl.BlockSpec((CHUNK, HEAD_DIM), lambda c: (rev(c), 0)),
                  pl.BlockSpec((CHUNK, HEAD_DIM), lambda c: (rev(c), 0)),
                  head_tab, head_tab, head_tab, head_tab,
                  pl.BlockSpec((N_HEADS, 8, HEAD_DIM), lambda c: (0, 0, 0)),
                  pl.BlockSpec((1, GROUP), lambda c: (0, 0))],
        out_specs=(pl.BlockSpec((CHUNK, 4 * GROUP), lambda c: (rev(c), 0)),
                   pl.BlockSpec((1, GROUP), lambda c: (0, 0))),
        scratch_shapes=[pltpu.VMEM((N_HEADS, HEAD_DIM, HEAD_DIM), F32)],
        compiler_params=_params(("arbitrary",)),
    )(proj, o_pre, states, d_mix, cos2, sin2, decay, jnp.transpose(decay, (0, 2, 1)), xi, zeta, g_rows, ret_gain)


FF_TILE = (7 * GROUP) // 128


def _log_forget(ff, bias_row, valid):
    x = ff + bias_row
    e = jnp.exp(-jnp.abs(x))
    lf = jnp.minimum(x, 0.0) - jnp.log(1.0 + e)
    head_lane = lax.broadcasted_iota(jnp.int32, x.shape, 1) < N_HEADS
    keep = lambda t: jnp.where(head_lane, jnp.where(valid, t, 0.0), 0.0)
    return keep(lf), keep(jnp.where(x >= 0, e, 1.0) / (1.0 + e))


def _fox_prep(proj, bias_row):
    l = proj.shape[0]
    n_blocks = l // CHUNK

    def body(ff_ref, b_ref, bc_ref, rows_ref, cum):
        r = lax.broadcasted_iota(jnp.int32, (CHUNK, CHUNK), 0)
        cidx = lax.broadcasted_iota(jnp.int32, (CHUNK, CHUNK), 1)
        tri = jnp.where(r >= cidx, 1.0, 0.0).astype(F32)
        carry = jnp.zeros((1, 128), F32)
        for blk in range(n_blocks):
            rows = slice(blk * CHUNK, (blk + 1) * CHUNK)
            valid = _row_valid(blk, CHUNK)
            lf, _ = _log_forget(ff_ref[rows, :], b_ref[...], valid)
            local = jnp.dot(tri, lf, precision=lax.Precision.HIGHEST, preferred_element_type=F32) + carry
            carry = local[CHUNK - 1:CHUNK, :]
            masked = jnp.where(valid, local, -NEG_BIG)
            cum[rows, :] = masked
            t = masked.T
            for h in range(N_HEADS):
                rows_ref[h, :, rows] = t[h:h + 1, :]
        full = cum[...]
        for h in range(N_HEADS):
            bc_ref[h] = jnp.broadcast_to(full[:, h:h + 1], (l, 128))

    return pl.pallas_call(
        body, name="fox_prep",
        out_shape=(jax.ShapeDtypeStruct((N_HEADS, l, 128), F32), jax.ShapeDtypeStruct((N_HEADS, 1, l), F32)),
        grid=(1,),
        in_specs=[pl.BlockSpec((l, 128), lambda i: (0, FF_TILE)), pl.BlockSpec((1, 128), lambda i: (0, 0))],
        out_specs=(pl.BlockSpec((N_HEADS, l, 128), lambda i: (0, 0, 0)),
                   pl.BlockSpec((N_HEADS, 1, l), lambda i: (0, 0, 0))),
        scratch_shapes=[pltpu.VMEM((l, 128), F32)],
        compiler_params=_params(("arbitrary",)),
    )(proj, bias_row)


ATTN_BLOCK = 2 * CHUNK


def _attn_blocks(l):
    assert (l - CHUNK) % ATTN_BLOCK == 0
    return [(0, CHUNK)] + [(s, ATTN_BLOCK) for s in range(CHUNK, l, ATTN_BLOCK)]


def _rows_valid(start, size):
    return start + lax.broadcasted_iota(jnp.int32, (size, 1), 0) >= PAD_ROWS


def _fox_fwd(proj, cum_bc, cum_rows, mix):
    l = proj.shape[0]
    blocks = _attn_blocks(l)
    scale = HEAD_DIM ** -0.5
    qt, kt, vt = 4 * N_HEADS, 5 * N_HEADS, 6 * N_HEADS

    def body(q_ref, k_ref, v_ref, cbc_ref, crow_ref, mix_in, o_ref, lse_ref, qb_s, kb_s, vb_s):
        qb_s[...] = q_ref[...].astype(MXU_DTYPE)
        kb_s[...] = k_ref[...].astype(MXU_DTYPE)
        vb_s[...] = v_ref[...].astype(MXU_DTYPE)
        for p, (qs, qn) in enumerate(blocks):
            qb = qb_s[qs:qs + qn, :]
            cq = cbc_ref[qs:qs + qn, :]
            m = jnp.full((qn, 1), NEG_BIG, F32)
            lsum = jnp.zeros((qn, 1), F32)
            acc = jnp.zeros((qn, HEAD_DIM), F32)
            for j in range(p + 1):
                ks, kn = blocks[j]
                bias = jnp.tile(cq, (1, kn // CHUNK)) - crow_ref[:, ks:ks + kn]
                s = _dot_nt(qb, kb_s[ks:ks + kn, :]) * scale + bias
                if j == p:
                    q_pos = qs + lax.broadcasted_iota(jnp.int32, (qn, kn), 0)
                    k_pos = ks + lax.broadcasted_iota(jnp.int32, (qn, kn), 1)
                    s = jnp.where(k_pos <= q_pos, s, NEG_BIG)
                m_new = jnp.maximum(m, jnp.max(s, axis=-1, keepdims=True))
                alpha = jnp.exp(m - m_new)
                pr = jnp.exp(s - m_new)
                lsum = lsum * alpha + jnp.sum(pr, axis=-1, keepdims=True)
                acc = acc * alpha + _dot(pr.astype(MXU_DTYPE), vb_s[ks:ks + kn, :])
                m = m_new
            o = jnp.where(_rows_valid(qs, qn), acc * (1.0 / lsum), 0.0)
            o_ref[qs:qs + qn, :] = o.astype(o_ref.dtype)
            lse = m + jnp.log(lsum)
            lse_ref[:, qs:qs + qn] = jnp.broadcast_to(lse, (qn, CHUNK)).T[0:1, :]

    head_col = lambda t: pl.BlockSpec((l, HEAD_DIM), lambda h: (0, t + h))
    return pl.pallas_call(
        body, name="fox_fwd",
        out_shape=(jax.ShapeDtypeStruct(mix.shape, mix.dtype), jax.ShapeDtypeStruct((N_HEADS, 1, l), F32)),
        grid=(N_HEADS,),
        in_specs=[head_col(qt), head_col(kt), head_col(vt),
                  pl.BlockSpec((None, l, 128), lambda h: (h, 0, 0)),
                  pl.BlockSpec((None, 1, l), lambda h: (h, 0, 0)),
                  ANY],
        out_specs=(head_col(N_HEADS), pl.BlockSpec((None, 1, l), lambda h: (h, 0, 0))),
        input_output_aliases={5: 0},
        scratch_shapes=[pltpu.VMEM((l, HEAD_DIM), MXU_DTYPE)] * 3,
        compiler_params=_params(("parallel",)),
    )(proj, proj, proj, cum_bc, cum_rows, mix)


def _fox_bwd(proj, cum_bc, cum_rows, d_mix, lse_rows):
    l = proj.shape[0]
    blocks = _attn_blocks(l)
    scale = HEAD_DIM ** -0.5
    qt, kt, vt = 4 * N_HEADS, 5 * N_HEADS, 6 * N_HEADS

    def body(q_ref, k_ref, v_ref, do_ref, cbc_ref, crow_ref, lse_ref,
             dq_ref, dk_ref, dv_ref, ds_ref, dk_acc, dv_acc, qb_s, kb_s, vb_s, dob_s, p_s, dp_s):
        qb_s[...] = q_ref[...].astype(MXU_DTYPE)
        kb_s[...] = k_ref[...].astype(MXU_DTYPE)
        vb_s[...] = v_ref[...].astype(MXU_DTYPE)
        dob_s[...] = jnp.where(_rows_valid(0, l), do_ref[...], 0.0).astype(MXU_DTYPE)
        dk_acc[...] = jnp.zeros_like(dk_acc)
        dv_acc[...] = jnp.zeros_like(dv_acc)
        ds_ref[...] = jnp.zeros_like(ds_ref)
        shift_row = crow_ref[...] - lse_ref[...]

        for p, (qs, qn) in enumerate(blocks):
            qb, dob = qb_s[qs:qs + qn, :], dob_s[qs:qs + qn, :]
            shift = shift_row[:, qs:qs + qn]

            delta = jnp.zeros((1, qn), F32)
            for j in range(p + 1):
                ks, kn = blocks[j]
                ck = jnp.tile(cbc_ref[ks:ks + kn, :], (1, qn // CHUNK))
                s_t = _dot_nt(kb_s[ks:ks + kn, :], qb) * scale + (shift - ck)
                if j == p:
                    k_pos = ks + lax.broadcasted_iota(jnp.int32, (kn, qn), 0)
                    q_pos = qs + lax.broadcasted_iota(jnp.int32, (kn, qn), 1)
                    s_t = jnp.where(k_pos <= q_pos, s_t, NEG_BIG)
                p_t, dp_t = jnp.exp(s_t), _dot_nt(vb_s[ks:ks + kn, :], dob)
                p_s[j, 0:kn, 0:qn] = p_t
                dp_s[j, 0:kn, 0:qn] = dp_t
                delta = delta + jnp.sum(p_t * dp_t, axis=0, keepdims=True)
            dq = jnp.zeros((qn, HEAD_DIM), F32)
            for j in range(p + 1):
                ks, kn = blocks[j]
                rows = slice(ks, ks + kn)
                p_t, dp_t = p_s[j, 0:kn, 0:qn], dp_s[j, 0:kn, 0:qn]
                ds_t = p_t * (dp_t - delta)
                ds_b = ds_t.astype(MXU_DTYPE)
                dv_acc[rows, :] += _dot(p_t.astype(MXU_DTYPE), dob)
                dk_acc[rows, :] += _dot(ds_b, qb) * scale
                ds_ref[rows, :] += sum(ds_t[:, c:c + CHUNK] for c in range(0, qn, CHUNK))
                dq = dq + _dot_tn(ds_b, kb_s[rows, :])
            dq_ref[qs:qs + qn, :] = (dq * scale).astype(dq_ref.dtype)

        dk_ref[...] = dk_acc[...].astype(dk_ref.dtype)
        dv_ref[...] = dv_acc[...].astype(dv_ref.dtype)

    col = jax.ShapeDtypeStruct((l, GROUP), MXU_DTYPE)
    head_col = lambda t: pl.BlockSpec((l, HEAD_DIM), lambda h: (0, t + h))
    return pl.pallas_call(
        body, name="fox_bwd",
        out_shape=(col, col, col, jax.ShapeDtypeStruct((N_HEADS, l, 128), F32)),
        grid=(N_HEADS,),
        in_specs=[head_col(qt), head_col(kt), head_col(vt), head_col(N_HEADS),
                  pl.BlockSpec((None, l, 128), lambda h: (h, 0, 0)),
                  pl.BlockSpec((None, 1, l), lambda h: (h, 0, 0)),
                  pl.BlockSpec((None, 1, l), lambda h: (h, 0, 0))],
        out_specs=(head_col(0), head_col(0), head_col(0), pl.BlockSpec((None, l, 128), lambda h: (h, 0, 0))),
        scratch_shapes=([pltpu.VMEM((l, HEAD_DIM), F32)] * 2 + [pltpu.VMEM((l, HEAD_DIM), MXU_DTYPE)] * 4
                        + [pltpu.VMEM((len(blocks), ATTN_BLOCK, ATTN_BLOCK), F32)] * 2),
        compiler_params=_params(("parallel",)),
    )(proj, proj, proj, d_mix, cum_bc, cum_rows, lse_rows)


def _fox_gate_bwd(ds_sum, proj, bias_row):
    l = proj.shape[0]
    n_blocks = l // CHUNK

    def body(ds_ref, ff_ref, b_ref, dff_ref, db_ref):
        r = lax.broadcasted_iota(jnp.int32, (CHUNK, CHUNK), 0)
        cidx = lax.broadcasted_iota(jnp.int32, (CHUNK, CHUNK), 1)
        upper = jnp.where(cidx >= r, 1.0, 0.0).astype(F32)
        carry = jnp.zeros((1, 128), F32)
        db = jnp.zeros((1, 128), F32)
        for blk in reversed(range(n_blocks)):
            rows = slice(blk * CHUNK, (blk + 1) * CHUNK)
            key_sum = jnp.zeros((CHUNK, 128), F32)
            for h in range(N_HEADS):
                select = jnp.where(cidx == h, 1.0, 0.0).astype(F32)
                key_sum = key_sum + jnp.dot(ds_ref[h, rows, :], select, precision=lax.Precision.HIGHEST,
                                            preferred_element_type=F32)
            suffix = jnp.dot(upper, key_sum, precision=lax.Precision.HIGHEST, preferred_element_type=F32) + carry
            carry = suffix[0:1, :]
            _, dsig = _log_forget(ff_ref[rows, :], b_ref[...], _row_valid(blk, CHUNK))
            dff = -suffix * dsig
            dff_ref[rows, :] = dff.astype(dff_ref.dtype)
            db = db + jnp.sum(dff, axis=0, keepdims=True)
        db_ref[...] = db

    return pl.pallas_call(
        body, name="fox_gate_bwd",
        out_shape=(jax.ShapeDtypeStruct((l, 128), MXU_DTYPE), jax.ShapeDtypeStruct((1, 128), F32)),
        grid=(1,),
        in_specs=[pl.BlockSpec((N_HEADS, l, 128), lambda i: (0, 0, 0)),
                  pl.BlockSpec((l, 128), lambda i: (0, FF_TILE)),
                  pl.BlockSpec((1, 128), lambda i: (0, 0))],
        out_specs=(pl.BlockSpec((l, 128), lambda i: (0, 0)), pl.BlockSpec((1, 128), lambda i: (0, 0))),
        compiler_params=_params(("arbitrary",)),
    )(ds_sum, proj, bias_row)


def _conv(u, w, b):
    return b + w[0:1, :] * pltpu.roll(u, 2, 0) + w[1:2, :] * pltpu.roll(u, 1, 0) + w[2:3, :] * u


def _conv_act_fwd(u, conv_w, conv_b, d_ff):
    l = u.shape[0]
    tc = _divisor_tile(d_ff, 256, 128)
    nt = d_ff // tc

    def body(ug_ref, uv_ref, wg_ref, wv_ref, bg_ref, bv_ref, a_ref, y_ref):
        yg = _conv(ug_ref[...], wg_ref[...], bg_ref[...])
        yv = _conv(uv_ref[...], wv_ref[...], bv_ref[...])
        act = yg * _sigmoid(yg) * yv
        a_ref[...] = jnp.where(_row_valid(0, l), act, 0.0).astype(a_ref.dtype)
        y_ref[0] = yg.astype(y_ref.dtype)
        y_ref[1] = yv.astype(y_ref.dtype)

    return pl.pallas_call(
        body, name="conv_act_fwd",
        out_shape=(jax.ShapeDtypeStruct((l, d_ff), MXU_DTYPE), jax.ShapeDtypeStruct((2, l, d_ff), MXU_DTYPE)),
        grid=(nt,),
        in_specs=[pl.BlockSpec((l, tc), lambda j: (0, j)), pl.BlockSpec((l, tc), lambda j: (0, j + nt)),
                  pl.BlockSpec((8, tc), lambda j: (0, j)), pl.BlockSpec((8, tc), lambda j: (0, j + nt)),
                  pl.BlockSpec((1, tc), lambda j: (0, j)), pl.BlockSpec((1, tc), lambda j: (0, j + nt))],
        out_specs=(pl.BlockSpec((l, tc), lambda j: (0, j)), pl.BlockSpec((2, l, tc), lambda j: (0, 0, j))),
        compiler_params=_params(("parallel",)),
    )(u, u, conv_w, conv_w, conv_b, conv_b)


def _conv_act_bwd(u, y, conv_w, d_act, d_ff):
    l = u.shape[0]
    tc = _divisor_tile(d_ff, 256, 128)
    nt = d_ff // tc

    def body(ug_ref, uv_ref, y_ref, wg_ref, wv_ref, da_ref, du_ref, dwb_ref):
        valid = _row_valid(0, l)
        ug, uv = ug_ref[...], uv_ref[...]
        wg, wv = wg_ref[...], wv_ref[...]
        yg, yv = y_ref[0].astype(F32), y_ref[1].astype(F32)
        sig = _sigmoid(yg)
        da = jnp.where(valid, da_ref[...], 0.0)
        d_yv = da * (yg * sig)
        d_yg = da * yv * (sig * (1.0 + yg * (1.0 - sig)))
        for idx, (dy, uu, w) in enumerate(((d_yg, ug, wg), (d_yv, uv, wv))):
            du = w[2:3, :] * dy + w[1:2, :] * pltpu.roll(dy, l - 1, 0) + w[0:1, :] * pltpu.roll(dy, l - 2, 0)
            du_ref[idx] = jnp.where(valid, du, 0.0).astype(du_ref.dtype)
            dwb_ref[idx, 0:1, :] = jnp.sum(dy * pltpu.roll(uu, 2, 0), axis=0, keepdims=True)
            dwb_ref[idx, 1:2, :] = jnp.sum(dy * pltpu.roll(uu, 1, 0), axis=0, keepdims=True)
            dwb_ref[idx, 2:3, :] = jnp.sum(dy * uu, axis=0, keepdims=True)
            dwb_ref[idx, 3:4, :] = jnp.sum(dy, axis=0, keepdims=True)
            dwb_ref[idx, 4:8, :] = jnp.zeros((4, tc), F32)

    return pl.pallas_call(
        body, name="conv_act_bwd",
        out_shape=(jax.ShapeDtypeStruct((2, l, d_ff), MXU_DTYPE), jax.ShapeDtypeStruct((2, 8, d_ff), F32)),
        grid=(nt,),
        in_specs=[pl.BlockSpec((l, tc), lambda j: (0, j)), pl.BlockSpec((l, tc), lambda j: (0, j + nt)),
                  pl.BlockSpec((2, l, tc), lambda j: (0, 0, j)),
                  pl.BlockSpec((8, tc), lambda j: (0, j)), pl.BlockSpec((8, tc), lambda j: (0, j + nt)),
                  pl.BlockSpec((l, tc), lambda j: (0, j))],
        out_specs=(pl.BlockSpec((2, l, tc), lambda j: (0, 0, j)), pl.BlockSpec((2, 8, tc), lambda j: (0, 0, j))),
        compiler_params=_params(("parallel",)),
    )(u, u, y, conv_w, conv_w, d_act)


def _adamw(w, g, m, v, name):
    shape = w.shape
    if w.ndim == 1:
        as2d = (1, shape[0])
    else:
        as2d = (int(np.prod(shape[:-1])), shape[-1])
    r, c = as2d
    tr = _divisor_tile(r, 256, 8)
    spec = pl.BlockSpec((tr, c), lambda i: (i, 0))

    def body(w_ref, g_ref, m_ref, v_ref, d_ref, nm_ref, nv_ref):
        d_ref[...], nm_ref[...], nv_ref[...] = _adamw_math(w_ref[...], g_ref[...], m_ref[...], v_ref[...])

    sds = jax.ShapeDtypeStruct(as2d, F32)
    outs = pl.pallas_call(
        body, name=name, out_shape=(sds, sds, sds), grid=(r // tr,),
        in_specs=[spec] * 4, out_specs=(spec,) * 3,
        compiler_params=_params(("parallel",)),
    )(w.reshape(as2d), g.reshape(as2d), m.reshape(as2d), v.reshape(as2d))
    return tuple(o.reshape(shape) for o in outs)


def _pad_rows(a, rows):
    return jnp.pad(a, ((0, rows - a.shape[0]), (0, 0)))


def kernel(x, meta_tokens, norm1_gain, w_in, b_forget, ret_norm_gain, w_out, norm2_gain, w_up, conv_w, conv_b, w_down, final_norm_gain, loss_target, m_meta_tokens, m_norm1_gain, m_w_in, m_b_forget, m_ret_norm_gain, m_w_out, m_norm2_gain, m_w_up, m_conv_w, m_conv_b, m_w_down, m_final_norm_gain, v_meta_tokens, v_norm1_gain, v_w_in, v_b_forget, v_ret_norm_gain, v_w_out, v_norm2_gain, v_w_up, v_conv_w, v_conv_b, v_w_down, v_final_norm_gain):
    seq, d = x.shape[1], x.shape[2]
    l = CHUNK + seq
    d_ff = w_down.shape[1] * N_DEV
    up_shard = w_up.shape[2]
    assert 4 * up_shard == d_ff and w_in.shape[2] == WIN_SHARD and d == 2 * GROUP
    dev = _device_index()
    mx, my, mc = _my_position()
    core = jnp.reshape(mc, (1,)).astype(jnp.int32)
    chip = jnp.reshape(2 * mx + my, (1,)).astype(jnp.int32)
    dev1 = jnp.reshape(dev, (1,)).astype(jnp.int32)

    small = jnp.concatenate([meta_tokens.reshape(-1, 128), conv_w[0].reshape(-1, 128)], axis=0)
    n_meta_rows = N_META * (d // N_DEV) // 128
    small_rows = small.shape[0]
    to_rows = lambda t: jnp.pad(jnp.transpose(t[0]), ((0, WIN_ROWS - WIN_SHARD), (0, 0)))
    from_rows = lambda t: jnp.transpose(t[:WIN_SHARD])[None]
    w_in_rows = to_rows(w_in)
    out_rows = d // N_DEV
    mixer_rows = -(-(WIN_ROWS + out_rows) // 32) * 32
    mixer_shard = jnp.concatenate([w_in_rows.astype(WIRE_DTYPE), w_out[0].astype(WIRE_DTYPE),
                                   jnp.zeros((mixer_rows - WIN_ROWS - out_rows, d), WIRE_DTYPE)], axis=0)

    consts = _retention_consts(l)
    bias_row = jnp.pad(b_forget, ((0, 0), (0, 128 - N_HEADS)))
    mixer_blocks, small_all = _gather_ring(mixer_shard, _pad_rows(small, -(-small_rows // 8) * 8), dev1, "gather_w_in")
    start_up = _gather_start(w_up[0], dev1, mixer_blocks, "gather_w_up_start")
    meta_full = jnp.transpose(small_all[:, :n_meta_rows].reshape(N_DEV, N_META, d // N_DEV), (1, 0, 2)).reshape(N_META, d)
    conv_w_full = _pad_rows(jnp.transpose(small_all[:, n_meta_rows:small_rows].reshape(N_DEV, 3, up_shard),
                                          (1, 0, 2)).reshape(3, 2 * d_ff), 8)
    h0, a = _embed_rmsnorm(x[0], meta_full, norm1_gain)
    w_in_full = _assemble_w_in(mixer_blocks).astype(MXU_DTYPE)
    proj = _mm_nt(a, w_in_full, F32, "mm_proj", after=start_up[4])
    ret_mix, ret_pre, ret_states = _retention_fwd(proj, ret_norm_gain, consts)
    cum_bc, cum_rows = _fox_prep(proj, bias_row)
    mix, lse_rows = _fox_fwd(proj, cum_bc, cum_rows, ret_mix)
    w_out_full = mixer_blocks[:, WIN_ROWS:WIN_ROWS + out_rows].reshape(d, d).astype(MXU_DTYPE)
    h1, cn = _out_proj_resid_rmsnorm(mix, w_out_full, h0, norm2_gain)
    w_up_blocks = _gather_finish(start_up, cn, "gather_w_up").astype(MXU_DTYPE)
    start_down = _gather_start(w_down[0], dev1, w_up_blocks, "gather_w_down_start")
    u = _mm_up(cn, w_up_blocks, start_down[4])
    pass_down = _gather_pass_start(start_down, u, "gather_w_down")
    act, conv_y = _conv_act_fwd(u, conv_w_full, conv_b + pass_down[4][0, 0], d_ff)
    w_down_full = _gather_pass_finish(pass_down, act, "gather_w_down").reshape(d_ff, d).astype(MXU_DTYPE)
    mlp_out = _mm_nn(act, w_down_full, F32, "mm_down", tm_cap=544, tk_cap=d_ff)
    d_h2, d_h2_b, dg_final, loss_part = _loss_head(h1, mlp_out, final_norm_gain.reshape(1, d), loss_target[0])

    gw_down = _mm_tn(act, d_h2_b, WIRE_DTYPE, "mm_gw_down", tm_cap=1408, tn_cap=1024)
    d2d_down = _reduce_scatter_d2d_start(gw_down.reshape(N_DEV, d_ff // N_DEV, d), d_h2, "rs_w_down")
    d_act = _mm_nt(d_h2_b, w_down_full, F32, "mm_d_act", after=d2d_down[4])
    rs_down = _reduce_scatter_ici_start(d2d_down, d_act, core, "rs_w_down")
    d_u, d_conv = _conv_act_bwd(u, conv_y, conv_w_full + rs_down[4][0, 0], d_act, d_ff)
    tm = _divisor_tile(l, 1088, 16)
    gw_up = _mm_gw_up(cn, d_u)
    d2d_up = _reduce_scatter_d2d_start(gw_up, d_act, "rs_w_up")
    d_cn = _mm_d_cn(d_u, w_up_blocks, d2d_up[4])
    rs_up = _reduce_scatter_ici_start(d2d_up, d_cn, core, "rs_w_up")
    d_h1, d_h1_b, dg_norm2 = _rmsnorm_bwd(d_h2, d_cn, h1, norm2_gain + rs_up[4][0, 0], "rmsnorm2_bwd", True)

    gw_out = _mm_tn(mix, d_h1_b, WIRE_DTYPE, "mm_gw_out")
    d2d_out = _reduce_scatter_d2d_start(gw_out.reshape(N_DEV, d // N_DEV, d), d_cn, "rs_w_out")
    d_mix = _mm_nt(d_h1_b, w_out_full, F32, "mm_d_mix", after=d2d_out[4])
    d_fq, d_fk, d_fv, ds_sum = _fox_bwd(proj, cum_bc, cum_rows, d_mix, lse_rows)
    d_ff_tile, db_forget_row = _fox_gate_bwd(ds_sum, proj, bias_row)
    d_ret, dg_ret = _retention_bwd(proj, ret_pre, ret_states, d_mix, ret_norm_gain, consts)
    rs_out = _reduce_scatter_ici_start(d2d_out, d_ret, core, "rs_w_out")
    d_proj = jnp.concatenate(
        [d_ret, d_fq, d_fk, d_fv, d_ff_tile, jnp.zeros((l, WIN_N - 7 * GROUP - 128), MXU_DTYPE)], axis=1)
    gw_in = _mm_tn(d_proj, a, WIRE_DTYPE, "mm_gw_in", tm_cap=1536, after=rs_out[4])
    rs_in = _reduce_scatter_start(_extract_w_in_windows(gw_in), core, "rs_w_in")
    d_a = _mm_nn(d_proj, w_in_full, F32, "mm_d_a", tm_cap=544, tn_cap=256, tk_cap=WIN_N, after=rs_in[4])
    d_front, d_tokens, dg_norm1 = _rmsnorm_bwd(d_h1, d_a, h0, norm1_gain + rs_in[4][0, 0], "rmsnorm1_bwd", False)
    grad_x = d_tokens[None]
    d_meta = d_front[PAD_ROWS:CHUNK]

    d_conv_w = jnp.concatenate([d_conv[0, 0:3], d_conv[1, 0:3]], axis=1)
    d_conv_b = jnp.concatenate([d_conv[0, 3:4], d_conv[1, 3:4]], axis=1)
    pieces = [loss_part[:, 0:1], dg_norm1, db_forget_row[:, 0:N_HEADS], dg_ret, dg_norm2, d_conv_b, dg_final,
              d_meta.reshape(1, -1), d_conv_w.reshape(1, -1)]
    sizes = [p.shape[1] for p in pieces]
    flat = jnp.concatenate(pieces, axis=1)
    padded = -(-flat.shape[1] // 1024) * 1024
    flat = jnp.pad(flat, ((0, 0), (0, padded - flat.shape[1]))).reshape(padded // 128, 128)
    small_ar = _small_all_reduce_start(flat, d_tokens, "all_reduce_small")

    lead = lambda outs: tuple(o[None] for o in outs)
    fin_down = lead(_reduce_scatter_finish(rs_down, small_ar[4], chip, w_down[0], m_w_down[0], v_w_down[0], "rs_w_down"))
    fin_up = lead(_reduce_scatter_finish(rs_up, fin_down[3], chip, w_up[0], m_w_up[0], v_w_up[0], "rs_w_up"))
    fin_out = lead(_reduce_scatter_finish(rs_out, fin_up[3], chip, w_out[0], m_w_out[0], v_w_out[0], "rs_w_out"))
    fin_in = tuple(from_rows(o) for o in _reduce_scatter_finish(
        rs_in, fin_out[3], chip, w_in_rows, to_rows(m_w_in), to_rows(v_w_in), "rs_w_in"))
    g_w_down, g_w_up, g_w_out, g_w_in = fin_down[0], fin_up[0], fin_out[0], fin_in[0]
    early = [fin_down[1:], fin_up[1:], fin_out[1:], fin_in[1:]]
    total = _small_all_reduce_finish(small_ar, fin_in[3], dev1, "all_reduce_small").reshape(1, padded)
    offs = np.concatenate([[0], np.cumsum(sizes)])
    take = lambda k: total[:, int(offs[k]):int(offs[k + 1])]
    loss = take(0).reshape(())
    g_norm1, g_bf, g_ret_gain, g_norm2 = take(1), take(2), take(3), take(4)
    g_conv_b, g_final = take(5), take(6).reshape(d)
    g_meta = lax.dynamic_slice(take(7).reshape(N_META, d), (jnp.int32(0), (dev * (d // N_DEV)).astype(jnp.int32)),
                               (N_META, d // N_DEV))
    g_conv_w = lax.dynamic_slice(take(8).reshape(3, 2 * d_ff), (jnp.int32(0), (dev * up_shard).astype(jnp.int32)),
                                 (3, up_shard))[None]

    weights = [meta_tokens, norm1_gain, w_in, b_forget, ret_norm_gain, w_out, norm2_gain, w_up, conv_w, conv_b,
               w_down, final_norm_gain]
    grads = [g_meta, g_norm1, g_w_in, g_bf, g_ret_gain, g_w_out, g_norm2, g_w_up, g_conv_w, g_conv_b, g_w_down,
             g_final]
    done = {"w_down": early[0], "w_up": early[1], "w_out": early[2], "w_in": early[3]}
    ms = [m_meta_tokens, m_norm1_gain, m_w_in, m_b_forget, m_ret_norm_gain, m_w_out, m_norm2_gain, m_w_up, m_conv_w,
          m_conv_b, m_w_down, m_final_norm_gain]
    vs = [v_meta_tokens, v_norm1_gain, v_w_in, v_b_forget, v_ret_norm_gain, v_w_out, v_norm2_gain, v_w_up, v_conv_w,
          v_conv_b, v_w_down, v_final_norm_gain]
    names = ["meta", "norm1", "w_in", "b_forget", "ret_gain", "w_out", "norm2", "w_up", "conv_w", "conv_b", "w_down",
             "final_gain"]
    deltas, new_ms, new_vs = [], [], []
    for w, g, m, v, n in zip(weights, grads, ms, vs, names):
        dl, nm, nv = done[n] if n in done else _adamw(w, g, m, v, "adamw_" + n)
        deltas.append(dl)
        new_ms.append(nm)
        new_vs.append(nv)
    return (loss, grad_x, *grads, *deltas, *new_ms, *new_vs)
```

```python
import numpy as np
import jax
import jax.numpy as jnp
from jax import lax
from jax.experimental import pallas as pl
from jax.experimental.pallas import tpu as pltpu

F32 = jnp.float32
MXU_DTYPE = jnp.bfloat16
WIRE_DTYPE = jnp.bfloat16

N_DEV = 8
N_META = 16
CHUNK = 128
PAD_ROWS = CHUNK - N_META
N_HEADS = 8
HEAD_DIM = 128
GROUP = N_HEADS * HEAD_DIM
IN_DIM = 7 * GROUP + N_HEADS
WIN_SHARD = IN_DIM // N_DEV
WIN_ROWS = 912
WIN_BLOCK = 1024
WIN_STRIDE = 896
WIN_N = 7680
ROPE_BASE = 10000.0
NORM_EPS = 1e-6
NEG_BIG = -1e30
ADAM_LR, ADAM_B1, ADAM_B2, ADAM_EPS, ADAM_WD, ADAM_STEP = 0.001, 0.9, 0.999, 1e-08, 0.01, 10
VMEM_LIMIT = 52 * 1024 * 1024
MESH = pl.DeviceIdType.MESH
ANY = pl.BlockSpec(memory_space=pl.ANY)
VMEM_SPEC = pl.BlockSpec(memory_space=pltpu.VMEM)


def _params(sem=None):
    kw = {"vmem_limit_bytes": VMEM_LIMIT}
    if sem is not None:
        kw["dimension_semantics"] = sem
    return pltpu.CompilerParams(**kw)


def _divisor_tile(n, cap, unit):
    if n <= cap:
        return n
    best = None
    for t in range(unit, cap + 1, unit):
        if n % t == 0:
            best = t
    assert best is not None, (n, cap, unit)
    return best


def _my_position():
    return lax.axis_index("x"), lax.axis_index("y"), lax.axis_index("c")


def _device_index():
    x, y, c = _my_position()
    return 4 * x + 2 * y + c


HBM_SPEC = pl.BlockSpec(memory_space=pltpu.HBM)
SEM_SPEC = pl.BlockSpec(memory_space=pltpu.SEMAPHORE)
DATAFLOW_EFFECT = pltpu.SideEffectType.DATAFLOW_SIDE_EFFECTING


def _in_hbm(a):
    return pltpu.with_memory_space_constraint(a, pltpu.HBM)


def _split_start(src, land, make_copies, n_copies, after, name):
    if isinstance(land, tuple):
        land = lax.empty(land, src.dtype)
    land_shape = land.shape
    def body(src_ref, land_ref, after_ref, send_sems, recv_sems, src_thru, land_thru, token):
        for cp in make_copies(src_ref, land_ref, send_sems, recv_sems):
            cp.start()
        token[...] = jnp.zeros_like(token)

    return pl.pallas_call(
        body, name=name,
        out_shape=(pltpu.SemaphoreType.DMA((n_copies,)), pltpu.SemaphoreType.DMA((n_copies,)),
                   pltpu.HBM(src.shape, src.dtype), pltpu.HBM(land_shape, land.dtype),
                   jax.ShapeDtypeStruct((8, 128), F32)),
        in_specs=(HBM_SPEC, HBM_SPEC, ANY), out_specs=(SEM_SPEC, SEM_SPEC, HBM_SPEC, HBM_SPEC, VMEM_SPEC),
        input_output_aliases={0: 2, 1: 3},
        compiler_params=pltpu.CompilerParams(has_side_effects=DATAFLOW_EFFECT),
    )(_in_hbm(src), _in_hbm(land), after)


def _split_wait(started, after, make_copies, name):
    send_sems, recv_sems, src_thru, land_thru, _ = started

    def body(src_ref, land_ref, send_sems_ref, recv_sems_ref, after_ref, src_dead, land_out):
        for cp in make_copies(src_ref, land_ref, send_sems_ref, recv_sems_ref):
            cp.wait_send()
            cp.wait_recv()

    return pl.pallas_call(
        body, name=name,
        out_shape=(pltpu.HBM(src_thru.shape, src_thru.dtype), pltpu.HBM(land_thru.shape, land_thru.dtype)),
        in_specs=(HBM_SPEC, HBM_SPEC, SEM_SPEC, SEM_SPEC, ANY), out_specs=(HBM_SPEC, HBM_SPEC),
        input_output_aliases={0: 0, 1: 1},
        compiler_params=pltpu.CompilerParams(has_side_effects=DATAFLOW_EFFECT),
    )(src_thru, land_thru, send_sems, recv_sems, after)


def _gather_copies(x_ref, land_ref, send_sems, recv_sems):
    mx, my, mc = _my_position()
    me = 4 * mx + 2 * my + mc
    targets = [(mx, my, 1 - mc), (1 - mx, my, mc), (mx, 1 - my, mc), (1 - mx, 1 - my, mc)]
    return [pltpu.make_async_remote_copy(
        src_ref=land_ref.at[me], dst_ref=land_ref.at[me], send_sem=send_sems.at[k], recv_sem=recv_sems.at[k],
        device_id=t, device_id_type=MESH) for k, t in enumerate(targets)]


def _own_slot(shard, dev, name):
    r, c = shard.shape
    tr = _divisor_tile(r, 640, 16)

    def body(s_ref, x_ref, o_ref):
        o_ref[...] = x_ref[...].astype(o_ref.dtype)

    return pl.pallas_call(
        body, name=name,
        out_shape=jax.ShapeDtypeStruct((N_DEV, r, c), WIRE_DTYPE),
        grid_spec=pltpu.PrefetchScalarGridSpec(
            num_scalar_prefetch=1, grid=(r // tr,),
            in_specs=[pl.BlockSpec((tr, c), lambda i, s: (i, 0))],
            out_specs=pl.BlockSpec((None, tr, c), lambda i, s: (s[0], i, 0))),
        compiler_params=_params(("parallel",)),
    )(dev, shard)


def _gather_start(shard, dev, after, name):
    return _split_start(jnp.zeros((8, 128), F32), _own_slot(shard, dev, name + "_own"), _gather_copies, 4, after, name)


def _gather_ring_with_norm(shard, small, dev, x, gain, name):
    r, c = shard.shape
    half = r // 2
    assert half % 16 == 0
    seq, d = x.shape
    n_tiles = seq // CHUNK

    def body(x_ref, small_ref, tok_ref, g_ref, land_in, land_ref, small_land, h_ref, n_ref,
             send_sems, recv_sems, local_sem, xbuf, nbuf, in_sems, h_sems, n_sems):
        mx, my, mc = _my_position()
        sibling, x_nbr, y_nbr = (mx, my, 1 - mc), (1 - mx, my, mc), (mx, 1 - my, mc)
        first, second = pl.ds(0, half), pl.ds(half, half)

        def slot(px, py, pc):
            return land_ref.at[4 * px + 2 * py + pc]

        def copy(k, src, dst, to):
            return pltpu.make_async_remote_copy(src_ref=src, dst_ref=dst, send_sem=send_sems.at[k],
                                                recv_sem=recv_sems.at[k], device_id=to, device_id_type=MESH)

        def arrived(k, dst):
            copy(k, dst, dst, sibling).wait_recv()

        mine = slot(mx, my, mc)
        from_x, from_y, from_d = slot(1 - mx, my, mc), slot(mx, 1 - my, mc), slot(1 - mx, 1 - my, mc)
        sent = [copy(0, x_ref, mine, sibling), copy(1, x_ref, mine, x_nbr), copy(2, x_ref, mine, y_nbr)]
        for cp in sent:
            cp.start()

        def send(k, src, to):
            cp = copy(k, src, src, to)
            cp.start()
            sent.append(cp)

        my_small = small_land.at[4 * mx + 2 * my + mc]
        own_small = pltpu.make_async_copy(small_ref, my_small, local_sem)
        own_small.start()
        for rel in range(1, N_DEV):
            bx, by, bc = (rel >> 2) & 1, (rel >> 1) & 1, rel & 1
            cp = copy(8 + rel, small_ref, my_small, (1 - mx if bx else mx, 1 - my if by else my, 1 - mc if bc else mc))
            cp.start()
            sent.append(cp)

        def read(t):
            return pltpu.make_async_copy(tok_ref.at[pl.ds(t * CHUNK, CHUNK)], xbuf.at[t % 2], in_sems.at[t % 2])

        def writes(t):
            rows = pl.ds(CHUNK + t * CHUNK, CHUNK)
            return (pltpu.make_async_copy(xbuf.at[t % 2], h_ref.at[rows], h_sems.at[t % 2]),
                    pltpu.make_async_copy(nbuf.at[t % 2], n_ref.at[rows], n_sems.at[t % 2]))

        read(0).start()
        for t in range(n_tiles):
            if t + 1 < n_tiles:
                if t >= 1:
                    for w in writes(t - 1):
                        w.wait()
                read(t + 1).start()
            read(t).wait()
            nbuf[t % 2] = _rms(xbuf[t % 2], g_ref[...])
            for w in writes(t):
                w.start()
        for t in range(max(n_tiles - 2, 0), n_tiles):
            for w in writes(t):
                w.wait()

        arrived(1, from_x)
        send(3, from_x.at[first], y_nbr)
        send(5, from_x, sibling)
        arrived(2, from_y)
        send(4, from_y.at[second], x_nbr)
        send(6, from_y, sibling)
        arrived(3, from_d.at[first])
        send(7, from_d.at[first], sibling)
        arrived(4, from_d.at[second])
        send(8, from_d.at[second], sibling)
        arrived(0, slot(mx, my, 1 - mc))
        arrived(5, slot(1 - mx, my, 1 - mc))
        arrived(6, slot(mx, 1 - my, 1 - mc))
        arrived(7, slot(1 - mx, 1 - my, 1 - mc).at[first])
        arrived(8, slot(1 - mx, 1 - my, 1 - mc).at[second])
        for rel in range(1, N_DEV):
            arrived(8 + rel, my_small)
        for cp in sent:
            cp.wait_send()
        own_small.wait()

    land = _own_slot(shard, dev, name + "_own")
    return pl.pallas_call(
        body, name=name,
        out_shape=(jax.ShapeDtypeStruct(land.shape, land.dtype), jax.ShapeDtypeStruct((N_DEV,) + small.shape, small.dtype),
                   jax.ShapeDtypeStruct((CHUNK + seq, d), F32), jax.ShapeDtypeStruct((CHUNK + seq, d), MXU_DTYPE)),
        in_specs=[ANY, ANY, ANY, VMEM_SPEC, ANY], out_specs=(ANY, ANY, ANY, ANY),
        input_output_aliases={4: 0},
        scratch_shapes=[pltpu.SemaphoreType.DMA((16,)), pltpu.SemaphoreType.DMA((16,)), pltpu.SemaphoreType.DMA,
                        pltpu.VMEM((2, CHUNK, d), F32), pltpu.VMEM((2, CHUNK, d), MXU_DTYPE),
                        pltpu.SemaphoreType.DMA((2,)), pltpu.SemaphoreType.DMA((2,)), pltpu.SemaphoreType.DMA((2,))],
        compiler_params=_params(),
    )(shard, small, x, gain, land)


def _embed_front(h0, normed, meta, gain):
    d = h0.shape[1]

    def body(h_in, n_in, m_ref, g_ref, h_ref, n_ref):
        front = jnp.concatenate([jnp.zeros((PAD_ROWS, d), F32), m_ref[...]], axis=0)
        h_ref[...] = front
        n_ref[...] = _rms(front, g_ref[...])

    blk = pl.BlockSpec((CHUNK, d), lambda i: (0, 0))
    return pl.pallas_call(
        body, name="embed_front",
        out_shape=(jax.ShapeDtypeStruct(h0.shape, h0.dtype), jax.ShapeDtypeStruct(normed.shape, normed.dtype)),
        grid=(1,),
        in_specs=[ANY, ANY, pl.BlockSpec((N_META, d), lambda i: (0, 0)), pl.BlockSpec((1, d), lambda i: (0, 0))],
        out_specs=(blk, blk),
        input_output_aliases={0: 0, 1: 1},
        compiler_params=_params(("arbitrary",)),
    )(h0, normed, meta, gain)


def _pass_copies(unused_ref, land_ref, send_sems, recv_sems):
    mx, my, mc = _my_position()
    chips = [(1 - mx, my), (mx, 1 - my), (1 - mx, 1 - my)]
    return [pltpu.make_async_remote_copy(
        src_ref=land_ref.at[4 * cx + 2 * cy + mc], dst_ref=land_ref.at[4 * cx + 2 * cy + mc],
        send_sem=send_sems.at[j], recv_sem=recv_sems.at[j],
        device_id=(mx, my, 1 - mc), device_id_type=MESH) for j, (cx, cy) in enumerate(chips)]


def _gather_pass_start(started, after, name):
    _, land = _split_wait(started, after, _gather_copies, name + "_wait")
    return _split_start(jnp.zeros((8, 128), F32), land, _pass_copies, 3, after, name + "_pass_start")


def _gather_pass_finish(pass_started, after, name):
    return _split_wait(pass_started, after, _pass_copies, name + "_pass_wait")[1]


def _gather_finish(started, after, name):
    _, land = _split_wait(started, after, _gather_copies, name + "_wait")

    def body(land_in, land_ref, send_sems, recv_sems):
        mx, my, mc = _my_position()
        chips = [(1 - mx, my), (mx, 1 - my), (1 - mx, 1 - my)]
        copies = [pltpu.make_async_remote_copy(
            src_ref=land_ref.at[4 * cx + 2 * cy + mc], dst_ref=land_ref.at[4 * cx + 2 * cy + mc],
            send_sem=send_sems.at[j], recv_sem=recv_sems.at[j],
            device_id=(mx, my, 1 - mc), device_id_type=MESH) for j, (cx, cy) in enumerate(chips)]
        for cp in copies:
            cp.start()
        for j, (cx, cy) in enumerate(chips):
            copies[j].wait_send()
            pltpu.make_async_remote_copy(
                src_ref=land_ref.at[4 * cx + 2 * cy + 1 - mc], dst_ref=land_ref.at[4 * cx + 2 * cy + 1 - mc],
                send_sem=send_sems.at[j], recv_sem=recv_sems.at[j],
                device_id=(mx, my, 1 - mc), device_id_type=MESH).wait_recv()

    return pl.pallas_call(
        body, name=name + "_pass",
        out_shape=jax.ShapeDtypeStruct(land.shape, land.dtype),
        in_specs=[ANY], out_specs=ANY,
        input_output_aliases={0: 0},
        scratch_shapes=[pltpu.SemaphoreType.DMA((3,)), pltpu.SemaphoreType.DMA((3,))],
    )(land)


def _chip_copies(p_ref, land_ref, send_sems, recv_sems):
    mx, my, mc = _my_position()
    chips = [(1 - mx, my), (mx, 1 - my), (1 - mx, 1 - my)]
    return [pltpu.make_async_remote_copy(
        src_ref=p_ref.at[2 * cx + cy], dst_ref=land_ref.at[j], send_sem=send_sems.at[j], recv_sem=recv_sems.at[j],
        device_id=(cx, cy, mc), device_id_type=MESH) for j, (cx, cy) in enumerate(chips)]


def _reduce_scatter_start(g, core, name):
    pair = _pair_sum(g, _exchange_sibling(g, name + "_d2d"), core, name + "_pairsum")
    return _split_start(pair, (3,) + pair.shape[1:], _chip_copies, 3, g, name + "_ici_start")


def _sibling_copies(g_ref, land_ref, send_sems, recv_sems):
    mx, my, mc = _my_position()
    return [pltpu.make_async_remote_copy(
        src_ref=g_ref.at[2 * k + (1 - mc)], dst_ref=land_ref.at[k], send_sem=send_sems.at[k], recv_sem=recv_sems.at[k],
        device_id=(mx, my, 1 - mc), device_id_type=MESH) for k in range(4)]


def _reduce_scatter_d2d_start(g, after, name):
    return _split_start(g, (4,) + g.shape[1:], _sibling_copies, 4, after, name + "_d2d_start")


def _reduce_scatter_ici_start(d2d_started, after, core, name):
    g, from_sibling = _split_wait(d2d_started, after, _sibling_copies, name + "_d2d_wait")
    pair = _pair_sum(g, from_sibling, core, name + "_pairsum")
    return _split_start(pair, (3,) + pair.shape[1:], _chip_copies, 3, g, name + "_ici_start")


def _reduce_scatter_finish(started, after, chip, w, m, v, name):
    pair, from_chips = _split_wait(started, after, _chip_copies, name + "_ici_wait")
    return _final_sum_adamw(pair, from_chips, chip, w, m, v, name + "_sum_adamw")


def _exchange_sibling(g, name):
    _, r, c = g.shape

    def body(g_ref, out_ref, send_sems, recv_sems):
        mx, my, mc = _my_position()
        copies = [
            pltpu.make_async_remote_copy(
                src_ref=g_ref.at[2 * k + (1 - mc)], dst_ref=out_ref.at[k],
                send_sem=send_sems.at[k], recv_sem=recv_sems.at[k],
                device_id=(mx, my, 1 - mc), device_id_type=MESH)
            for k in range(4)]
        for cp in copies:
            cp.start()
        for cp in copies:
            cp.wait()

    return pl.pallas_call(
        body, name=name,
        out_shape=jax.ShapeDtypeStruct((4, r, c), g.dtype),
        in_specs=[ANY], out_specs=ANY,
        scratch_shapes=[pltpu.SemaphoreType.DMA((4,)), pltpu.SemaphoreType.DMA((4,))],
    )(g)


def _pair_sum(g, recv, core, name):
    _, r, c = g.shape
    tr = _divisor_tile(r, 512, 16)

    def body(s_ref, g_ref, r_ref, o_ref):
        o_ref[...] = (g_ref[...].astype(F32) + r_ref[...].astype(F32)).astype(o_ref.dtype)

    return pl.pallas_call(
        body, name=name,
        out_shape=jax.ShapeDtypeStruct((4, r, c), g.dtype),
        grid_spec=pltpu.PrefetchScalarGridSpec(
            num_scalar_prefetch=1, grid=(4, r // tr),
            in_specs=[pl.BlockSpec((None, tr, c), lambda k, i, s: (2 * k + s[0], i, 0)),
                      pl.BlockSpec((None, tr, c), lambda k, i, s: (k, i, 0))],
            out_specs=pl.BlockSpec((None, tr, c), lambda k, i, s: (k, i, 0))),
        compiler_params=_params(("parallel", "parallel")),
    )(core, g, recv)


def _adamw_math(w, g, m, v):
    nm = ADAM_B1 * m + (1.0 - ADAM_B1) * g
    nv = ADAM_B2 * v + (1.0 - ADAM_B2) * (g * g)
    m_hat = nm / (1.0 - ADAM_B1 ** ADAM_STEP)
    v_hat = nv / (1.0 - ADAM_B2 ** ADAM_STEP)
    return -ADAM_LR * (m_hat / (jnp.sqrt(v_hat) + ADAM_EPS) + ADAM_WD * w), nm, nv


def _final_sum_adamw(p, recv, chip, w, m, v, name):
    _, r, c = p.shape
    tr = _divisor_tile(r, 256, 16)
    tile = lambda: pl.BlockSpec((tr, c), lambda i, s: (i, 0))

    def body(s_ref, p_ref, r_ref, w_ref, m_ref, v_ref, g_ref, d_ref, nm_ref, nv_ref):
        g = p_ref[...].astype(F32)
        for j in range(3):
            g = g + r_ref[j].astype(F32)
        g_ref[...] = g
        d_ref[...], nm_ref[...], nv_ref[...] = _adamw_math(w_ref[...], g, m_ref[...], v_ref[...])

    sds = jax.ShapeDtypeStruct((r, c), F32)
    return pl.pallas_call(
        body, name=name,
        out_shape=(sds, sds, sds, sds),
        grid_spec=pltpu.PrefetchScalarGridSpec(
            num_scalar_prefetch=1, grid=(r // tr,),
            in_specs=[pl.BlockSpec((None, tr, c), lambda i, s: (s[0], i, 0)),
                      pl.BlockSpec((3, tr, c), lambda i, s: (0, i, 0)), tile(), tile(), tile()],
            out_specs=(tile(), tile(), tile(), tile())),
        compiler_params=_params(("parallel",)),
    )(chip, p, recv, w, m, v)


def _all_to_all_copies(v_ref, land_ref, send_sems, recv_sems):
    mx, my, mc = _my_position()
    me = 4 * mx + 2 * my + mc
    copies = []
    for rel in range(1, N_DEV):
        bx, by, bc = (rel >> 2) & 1, (rel >> 1) & 1, rel & 1
        target = (1 - mx if bx else mx, 1 - my if by else my, 1 - mc if bc else mc)
        copies.append(pltpu.make_async_remote_copy(
            src_ref=v_ref, dst_ref=land_ref.at[me], send_sem=send_sems.at[rel - 1], recv_sem=recv_sems.at[rel - 1],
            device_id=target, device_id_type=MESH))
    return copies


def _small_all_reduce_start(v, after, name):
    return _split_start(v, (N_DEV,) + v.shape, _all_to_all_copies, N_DEV - 1, after, name + "_start")


def _small_all_reduce_finish(started, after, dev, name):
    v, land = _split_wait(started, after, _all_to_all_copies, name + "_wait")
    rows = v.shape[0]

    def body(me_ref, v_ref, land_ref, o_ref):
        for j in range(N_DEV):
            @pl.when(me_ref[0] == j)
            def _():
                o_ref[...] = v_ref[...] if j == 0 else o_ref[...] + v_ref[...]

            @pl.when(me_ref[0] != j)
            def _():
                o_ref[...] = land_ref[j] if j == 0 else o_ref[...] + land_ref[j]

    return pl.pallas_call(
        body, name=name + "_sum",
        out_shape=jax.ShapeDtypeStruct((rows, 128), F32),
        grid_spec=pltpu.PrefetchScalarGridSpec(
            num_scalar_prefetch=1, grid=(1,),
            in_specs=[pl.BlockSpec((rows, 128), lambda i, s: (0, 0)),
                      pl.BlockSpec((N_DEV, rows, 128), lambda i, s: (0, 0, 0))],
            out_specs=pl.BlockSpec((rows, 128), lambda i, s: (0, 0))),
        compiler_params=_params(("arbitrary",)),
    )(dev, v, land)


def _assemble_w_in(blocks):
    rows, d = WIN_ROWS, blocks.shape[2]
    tc = _divisor_tile(d, 256, 128)
    n_tiles = WIN_N // 128
    last = (N_DEV * WIN_STRIDE) // 128

    def body(b_ref, o_ref):
        win = []
        for i in range(N_DEV):
            w = jnp.concatenate([b_ref[i].astype(F32), jnp.zeros((WIN_BLOCK - rows, tc), F32)], axis=0)
            win.append(pltpu.roll(w, i, 0) if i else w)
        for t in range(n_tiles):
            if t > last:
                o_ref[t * 128:(t + 1) * 128, :] = jnp.zeros((128, tc), o_ref.dtype)
                continue
            i = min(t // 7, N_DEV - 1)
            k = t - 7 * i
            val = win[i][k * 128:(k + 1) * 128, :]
            if k == 0 and i >= 1:
                val = val + win[i - 1][7 * 128:8 * 128, :]
            o_ref[t * 128:(t + 1) * 128, :] = val.astype(o_ref.dtype)

    return pl.pallas_call(
        body, name="assemble_w_in",
        out_shape=jax.ShapeDtypeStruct((WIN_N, d), blocks.dtype),
        grid=(d // tc,),
        in_specs=[pl.BlockSpec((N_DEV, rows, tc), lambda j: (0, 0, j))],
        out_specs=pl.BlockSpec((WIN_N, tc), lambda j: (0, j)),
        compiler_params=_params(("parallel",)),
    )(blocks)


def _extract_w_in_windows(g):
    _, d = g.shape
    tc = _divisor_tile(d, 256, 128)

    def body(g_ref, o_ref):
        for j in range(N_DEV):
            w = g_ref[WIN_STRIDE * j:WIN_STRIDE * j + WIN_BLOCK, :].astype(F32)
            w = pltpu.roll(w, WIN_BLOCK - j, 0) if j else w
            o_ref[j] = w[0:WIN_ROWS, :].astype(o_ref.dtype)

    return pl.pallas_call(
        body, name="extract_w_in_windows",
        out_shape=jax.ShapeDtypeStruct((N_DEV, WIN_ROWS, d), g.dtype),
        grid=(d // tc,),
        in_specs=[pl.BlockSpec((WIN_N, tc), lambda j: (0, j))],
        out_specs=pl.BlockSpec((N_DEV, WIN_ROWS, tc), lambda j: (0, 0, j)),
        compiler_params=_params(("parallel",)),
    )(g)


def _mm(a, b, *, a_spec, b_spec, o_spec, out_shape, grid, contract, nk, name, after=None):
    dn = (((contract[0],), (contract[1],)), ((), ()))
    tm, tn = o_spec.block_shape[-2:]
    behind = [] if after is None else [after]

    def body(a_ref, b_ref, *rest):
        o_ref, *scratch = rest[len(behind):]
        part = lax.dot_general(a_ref[...], b_ref[...], dn, preferred_element_type=F32)
        if nk == 1:
            o_ref[...] = part.astype(o_ref.dtype)
            return
        acc = scratch[0]
        k = pl.program_id(2)

        @pl.when(k == 0)
        def _():
            acc[...] = part

        @pl.when(k > 0)
        def _():
            acc[...] += part

        @pl.when(k == nk - 1)
        def _():
            o_ref[...] = acc[...].astype(o_ref.dtype)

    return pl.pallas_call(
        body, name=name, out_shape=out_shape, grid=grid,
        in_specs=[a_spec, b_spec] + [ANY] * len(behind), out_specs=o_spec,
        scratch_shapes=[] if nk == 1 else [pltpu.VMEM((tm, tn), F32)],
        compiler_params=_params(("parallel", "parallel", "arbitrary")),
    )(a, b, *behind)


def _mm_nn(a, b, out_dtype, name, tm_cap=1088, tn_cap=512, tk_cap=2048, after=None):
    m, k = a.shape
    _, n = b.shape
    tm, tn, tk = _divisor_tile(m, tm_cap, 16), _divisor_tile(n, tn_cap, 128), _divisor_tile(k, tk_cap, 128)
    return _mm(a, b,
               a_spec=pl.BlockSpec((tm, tk), lambda i, j, kk: (i, kk)),
               b_spec=pl.BlockSpec((tk, tn), lambda i, j, kk: (kk, j)),
               o_spec=pl.BlockSpec((tm, tn), lambda i, j, kk: (i, j)),
               out_shape=jax.ShapeDtypeStruct((m, n), out_dtype),
               grid=(m // tm, n // tn, k // tk), contract=(1, 0), nk=k // tk, name=name, after=after)


def _mm_nt(a, b, out_dtype, name, tm_cap=1088, tn_cap=512, tk_cap=2048, after=None):
    m, k = a.shape
    n, _ = b.shape
    tm, tn, tk = _divisor_tile(m, tm_cap, 16), _divisor_tile(n, tn_cap, 128), _divisor_tile(k, tk_cap, 128)
    return _mm(a, b,
               a_spec=pl.BlockSpec((tm, tk), lambda i, j, kk: (i, kk)),
               b_spec=pl.BlockSpec((tn, tk), lambda i, j, kk: (j, kk)),
               o_spec=pl.BlockSpec((tm, tn), lambda i, j, kk: (i, j)),
               out_shape=jax.ShapeDtypeStruct((m, n), out_dtype),
               grid=(m // tm, n // tn, k // tk), contract=(1, 1), nk=k // tk, name=name, after=after)


def _mm_tn(a, b, out_dtype, name, tm_cap=1024, tn_cap=512, after=None):
    l, m = a.shape
    _, n = b.shape
    tm, tn = _divisor_tile(m, tm_cap, 128), _divisor_tile(n, tn_cap, 128)
    return _mm(a, b,
               a_spec=pl.BlockSpec((l, tm), lambda i, j, kk: (0, i)),
               b_spec=pl.BlockSpec((l, tn), lambda i, j, kk: (0, j)),
               o_spec=pl.BlockSpec((tm, tn), lambda i, j, kk: (i, j)),
               out_shape=jax.ShapeDtypeStruct((m, n), out_dtype),
               grid=(m // tm, n // tn, 1), contract=(0, 0), nk=1, name=name, after=after)


def _pair_split(shard):
    left = shard % ATTN_BLOCK
    assert left in (0, CHUNK) and shard > left
    return shard - left, left


def _mm_up(cn, w_up_blocks, after):
    l, d = cn.shape
    n, _, shard = w_up_blocks.shape
    main, left = _pair_split(shard)
    tm = _divisor_tile(l, 544, 16)

    def body(a_ref, b_ref, after_ref, o_ref):
        a = a_ref[...]
        for s in range(2):
            o_ref[:, s * shard:s * shard + main] = _dot(a, b_ref[s, :, 0:main])
        if left:
            tail = _dot(a, jnp.concatenate([b_ref[0, :, main:], b_ref[1, :, main:]], axis=1))
            o_ref[:, main:shard] = tail[:, 0:left]
            o_ref[:, shard + main:2 * shard] = tail[:, left:]

    return pl.pallas_call(
        body, name="mm_up", out_shape=jax.ShapeDtypeStruct((l, n * shard), F32), grid=(l // tm, n // 2),
        in_specs=[pl.BlockSpec((tm, d), lambda i, j: (i, 0)),
                  pl.BlockSpec((2, d, shard), lambda i, j: (j, 0, 0)), ANY],
        out_specs=pl.BlockSpec((tm, 2 * shard), lambda i, j: (i, j)),
        compiler_params=_params(("parallel", "parallel")),
    )(cn, w_up_blocks, after)


def _mm_gw_up(cn, d_u):
    l, d = cn.shape
    _, _, d_ff = d_u.shape
    shard = 2 * d_ff // N_DEV
    pairs_per_half = d_ff // (2 * shard)
    tm = _divisor_tile(d, 512, 128)

    def body(a_ref, b_ref, o_ref):
        res = _dot_tn(a_ref[...], b_ref[...])
        o_ref[0] = res[:, 0:shard].astype(o_ref.dtype)
        o_ref[1] = res[:, shard:].astype(o_ref.dtype)

    return pl.pallas_call(
        body, name="mm_gw_up", out_shape=jax.ShapeDtypeStruct((N_DEV, d, shard), WIRE_DTYPE),
        grid=(d // tm, N_DEV // 2),
        in_specs=[pl.BlockSpec((l, tm), lambda i, j: (0, i)),
                  pl.BlockSpec((None, l, 2 * shard), lambda i, j: (j // pairs_per_half, 0, j % pairs_per_half))],
        out_specs=pl.BlockSpec((2, tm, shard), lambda i, j: (j, i, 0)),
        compiler_params=_params(("parallel", "parallel")),
    )(cn, d_u)


def _mm_d_cn(d_u, w_up_blocks, after):
    _, l, d_ff = d_u.shape
    n, d, shard = w_up_blocks.shape
    per = d_ff // shard
    main, left = _pair_split(shard)
    tm, tn = _divisor_tile(l, 544, 16), _divisor_tile(d, 256, 128)

    def body(a_ref, b_ref, after_ref, o_ref):
        acc = None
        for k in range(0, n, 2):
            half, c0 = k // per, (k % per) * shard
            parts = [_dot_nt(a_ref[half, :, c0 + s * shard:c0 + s * shard + main], b_ref[k + s, :, 0:main])
                     for s in range(2)]
            if left:
                a_tail = jnp.concatenate([a_ref[half, :, c0 + s * shard + main:c0 + (s + 1) * shard] for s in range(2)],
                                         axis=1)
                b_tail = jnp.concatenate([b_ref[k + s, :, main:] for s in range(2)], axis=1)
                parts.append(_dot_nt(a_tail, b_tail))
            for part in parts:
                acc = part if acc is None else acc + part
        o_ref[...] = acc

    return pl.pallas_call(
        body, name="mm_d_cn", out_shape=jax.ShapeDtypeStruct((l, d), F32), grid=(l // tm, d // tn),
        in_specs=[pl.BlockSpec((2, tm, d_ff), lambda i, j: (0, i, 0)),
                  pl.BlockSpec((n, tn, shard), lambda i, j: (0, j, 0)), ANY],
        out_specs=pl.BlockSpec((tm, tn), lambda i, j: (i, j)),
        compiler_params=_params(("parallel", "parallel")),
    )(d_u, w_up_blocks, after)


def _row_tile(l):
    return _divisor_tile(l, 544, 8)


def _rms(x, gain):
    return (x * lax.rsqrt(jnp.mean(x * x, axis=-1, keepdims=True) + NORM_EPS) * gain).astype(MXU_DTYPE)


def _out_proj_resid_rmsnorm(mix, w_out, h0, gain):
    l, d = h0.shape
    tm = _divisor_tile(l, 272, 16)
    row = pl.BlockSpec((tm, d), lambda i: (i, 0))

    def body(a_ref, b_ref, h_ref, g_ref, s_ref, n_ref):
        x = h_ref[...] + _dot(a_ref[...], b_ref[...])
        s_ref[...] = x
        n_ref[...] = _rms(x, g_ref[...])

    return pl.pallas_call(
        body, name="mm_out_resid_rmsnorm2",
        out_shape=(jax.ShapeDtypeStruct((l, d), F32), jax.ShapeDtypeStruct((l, d), MXU_DTYPE)),
        grid=(l // tm,),
        in_specs=[row, pl.BlockSpec((d, d), lambda i: (0, 0)), row, pl.BlockSpec((1, d), lambda i: (0, 0))],
        out_specs=(row, row),
        compiler_params=_params(("parallel",)),
    )(mix, w_out, h0, gain)


def _rmsnorm_bwd(d_res, d_normed, x, gain, name, with_mxu_copy):
    l, d = x.shape
    tr = _row_tile(l) if with_mxu_copy else CHUNK
    row = pl.BlockSpec((tr, d), lambda i: (i, 0))
    vec = pl.BlockSpec((1, d), lambda i: (0, 0))

    def body(dres_ref, dn_ref, x_ref, g_ref, dx_ref, other_ref, dg_ref):
        i = pl.program_id(0)
        xv = x_ref[...]
        r = lax.rsqrt(jnp.mean(xv * xv, axis=-1, keepdims=True) + NORM_EPS)
        xh = xv * r
        dn = dn_ref[...]
        dxh = dn * g_ref[...]
        dx = dres_ref[...] + r * (dxh - xh * jnp.mean(dxh * xh, axis=-1, keepdims=True))
        if with_mxu_copy:
            dx_ref[...] = dx
            other_ref[...] = dx.astype(MXU_DTYPE)
        else:
            @pl.when(i == 0)
            def _():
                dx_ref[...] = dx

            @pl.when(i > 0)
            def _():
                other_ref[...] = dx

        @pl.when(i == 0)
        def _():
            dg_ref[...] = jnp.zeros_like(dg_ref)

        dg_ref[...] += jnp.sum(dn * xh, axis=0, keepdims=True)

    if with_mxu_copy:
        outs = [jax.ShapeDtypeStruct((l, d), F32), jax.ShapeDtypeStruct((l, d), MXU_DTYPE)]
        specs = [row, row]
    else:
        outs = [jax.ShapeDtypeStruct((CHUNK, d), F32), jax.ShapeDtypeStruct((l - CHUNK, d), F32)]
        specs = [pl.BlockSpec((CHUNK, d), lambda i: (0, 0)), pl.BlockSpec((CHUNK, d), lambda i: (jnp.maximum(i - 1, 0), 0))]
    outs.append(jax.ShapeDtypeStruct((1, d), F32))
    specs.append(vec)
    return pl.pallas_call(body, name=name, out_shape=tuple(outs), grid=(l // tr,),
                          in_specs=[row, row, row, vec], out_specs=tuple(specs),
                          compiler_params=_params(("arbitrary",)))(d_res, d_normed, x, gain)


def _loss_head(h1, mlp_out, gain, target):
    l, d = h1.shape
    n_blocks = l // CHUNK
    row = pl.BlockSpec((CHUNK, d), lambda i: (i, 0))
    vec = pl.BlockSpec((1, d), lambda i: (0, 0))
    tgt = pl.BlockSpec((CHUNK, d), lambda i: (jnp.maximum(i - 1, 0), 0))

    def body(h_ref, m_ref, g_ref, t_ref, dh_ref, dhb_ref, dg_ref, loss_ref, sq_ref):
        i = pl.program_id(0)
        x = h_ref[...] + m_ref[...]
        r = lax.rsqrt(jnp.mean(x * x, axis=-1, keepdims=True) + NORM_EPS)
        xh = x * r
        g = g_ref[...]
        real = i >= 1
        err = jnp.where(real, xh * g - t_ref[...], 0.0)
        dy = err * (1.0 / d)
        dxh = dy * g
        dh = r * (dxh - xh * jnp.mean(dxh * xh, axis=-1, keepdims=True))
        dh_ref[...] = dh
        dhb_ref[...] = dh.astype(MXU_DTYPE)

        @pl.when(i == 0)
        def _():
            dg_ref[...] = jnp.zeros_like(dg_ref)
            sq_ref[...] = jnp.zeros_like(sq_ref)

        dg_ref[...] += jnp.sum(dy * xh, axis=0, keepdims=True)
        sq_ref[...] += jnp.sum(err * err, axis=0, keepdims=True)

        @pl.when(i == n_blocks - 1)
        def _():
            total = jnp.sum(sq_ref[...], axis=-1, keepdims=True) * (0.5 / d)
            loss_ref[...] = jnp.broadcast_to(total, (1, 128))

    return pl.pallas_call(
        body, name="loss_head",
        out_shape=(jax.ShapeDtypeStruct((l, d), F32), jax.ShapeDtypeStruct((l, d), MXU_DTYPE),
                   jax.ShapeDtypeStruct((1, d), F32), jax.ShapeDtypeStruct((1, 128), F32)),
        grid=(n_blocks,), in_specs=[row, row, vec, tgt],
        out_specs=(row, row, vec, pl.BlockSpec((1, 128), lambda i: (0, 0))),
        scratch_shapes=[pltpu.VMEM((1, d), F32)],
        compiler_params=_params(("arbitrary",)),
    )(h1, mlp_out, gain, target)


def _dot(a, b):
    return jnp.dot(a, b, preferred_element_type=F32)


def _dot_nt(a, b):
    return lax.dot_general(a, b, (((1,), (1,)), ((), ())), preferred_element_type=F32)


def _dot_tn(a, b):
    return lax.dot_general(a, b, (((0,), (0,)), ((), ())), preferred_element_type=F32)


def _rope(t, cos2, sin2):
    return t * cos2 + pltpu.roll(t, HEAD_DIM // 2, 1) * sin2


def _rope_bwd(dr, cos2, sin2):
    return dr * cos2 + pltpu.roll(dr * sin2, HEAD_DIM // 2, 1)


def _sigmoid(x):
    return 1.0 / (1.0 + jnp.exp(-x))


def _row_valid(block, rows):
    r = block * CHUNK + lax.broadcasted_iota(jnp.int32, (rows, 1), 0)
    return r >= PAD_ROWS


def _retention_consts(l):
    pos = jnp.arange(l, dtype=F32) - PAD_ROWS
    inv_freq = 1.0 / (ROPE_BASE ** (jnp.arange(0, HEAD_DIM, 2, dtype=F32) / HEAD_DIM))
    ang = pos[:, None] * inv_freq[None, :]
    cos, sin = jnp.cos(ang), jnp.sin(ang)
    cos2 = jnp.concatenate([cos, cos], axis=-1)
    sin2 = jnp.concatenate([-sin, sin], axis=-1)
    log_g = jnp.log1p(-jnp.exp2(-5.0 - jnp.arange(N_HEADS, dtype=F32)))
    idx = jnp.arange(CHUNK, dtype=F32)
    diff = idx[:, None] - idx[None, :]
    decay = jnp.where(diff >= 0, jnp.exp(jnp.maximum(diff, 0.0)[None] * log_g[:, None, None]), 0.0)
    xi = jnp.exp((idx + 1.0)[None, :] * log_g[:, None])
    zeta = jnp.exp((CHUNK - 1.0 - idx)[None, :] * log_g[:, None])
    g_chunk = jnp.exp(CHUNK * log_g)
    bcast = lambda v: jnp.broadcast_to(v[:, :, None], (N_HEADS, CHUNK, HEAD_DIM))
    g_rows = jnp.broadcast_to(g_chunk[:, None, None], (N_HEADS, 8, HEAD_DIM))
    return cos2, sin2, decay, bcast(xi), bcast(zeta), g_rows


def _retention_fwd(proj, ret_gain, consts):
    l = proj.shape[0]
    n_chunks = l // CHUNK
    cos2, sin2, decay, xi, zeta, g_rows = consts
    scale = HEAD_DIM ** -0.5

    def body(p_ref, cos_ref, sin_ref, dec_ref, xi_ref, zeta_ref, gr_ref, gain_ref,
             mix_ref, o_ref, st_ref, state):
        c = pl.program_id(0)

        @pl.when(c == 0)
        def _():
            state[...] = jnp.zeros_like(state)

        cos_v, sin_v = cos_ref[...], sin_ref[...]
        valid = _row_valid(c, CHUNK)
        for h in range(N_HEADS):
            cols = slice(h * HEAD_DIM, (h + 1) * HEAD_DIM)
            q = p_ref[:, h * HEAD_DIM:(h + 1) * HEAD_DIM]
            k = p_ref[:, GROUP + h * HEAD_DIM:GROUP + (h + 1) * HEAD_DIM]
            v = p_ref[:, 2 * GROUP + h * HEAD_DIM:2 * GROUP + (h + 1) * HEAD_DIM]
            g = p_ref[:, 3 * GROUP + h * HEAD_DIM:3 * GROUP + (h + 1) * HEAD_DIM]
            rq = _rope(q, cos_v, sin_v).astype(MXU_DTYPE)
            rk = _rope(k, cos_v, sin_v) * scale
            rkb = rk.astype(MXU_DTYPE)
            vb = v.astype(MXU_DTYPE)
            st = state[h]
            st_ref[h] = st
            s = _dot_nt(rq, rkb) * dec_ref[h]
            o = _dot(s.astype(MXU_DTYPE), vb) + _dot(rq, st.astype(MXU_DTYPE)) * xi_ref[h]
            kz = (rk * zeta_ref[h]).astype(MXU_DTYPE)
            state[h] = gr_ref[h, 0:1, :] * st + _dot_tn(kz, vb)
            o_ref[:, cols] = o
            mu = jnp.mean(o, axis=-1, keepdims=True)
            oc = o - mu
            yn = oc * lax.rsqrt(jnp.mean(oc * oc, axis=-1, keepdims=True) + NORM_EPS)
            ret = (g * _sigmoid(g)) * (yn * gain_ref[:, cols])
            mix_ref[:, cols] = jnp.where(valid, ret, 0.0).astype(mix_ref.dtype)

    head_tab = pl.BlockSpec((N_HEADS, CHUNK, HEAD_DIM), lambda c: (0, 0, 0))
    return pl.pallas_call(
        body, name="retention_fwd",
        out_shape=(jax.ShapeDtypeStruct((l, 2 * GROUP), MXU_DTYPE), jax.ShapeDtypeStruct((l, GROUP), F32),
                   jax.ShapeDtypeStruct((n_chunks, N_HEADS, HEAD_DIM, HEAD_DIM), F32)),
        grid=(n_chunks,),
        in_specs=[pl.BlockSpec((CHUNK, 4 * GROUP), lambda c: (c, 0)),
                  pl.BlockSpec((CHUNK, HEAD_DIM), lambda c: (c, 0)),
                  pl.BlockSpec((CHUNK, HEAD_DIM), lambda c: (c, 0)),
                  head_tab, head_tab, head_tab,
                  pl.BlockSpec((N_HEADS, 8, HEAD_DIM), lambda c: (0, 0, 0)),
                  pl.BlockSpec((1, GROUP), lambda c: (0, 0))],
        out_specs=(pl.BlockSpec((CHUNK, GROUP), lambda c: (c, 0)),
                   pl.BlockSpec((CHUNK, GROUP), lambda c: (c, 0)),
                   pl.BlockSpec((None, N_HEADS, HEAD_DIM, HEAD_DIM), lambda c: (c, 0, 0, 0))),
        scratch_shapes=[pltpu.VMEM((N_HEADS, HEAD_DIM, HEAD_DIM), F32)],
        compiler_params=_params(("arbitrary",)),
    )(proj, cos2, sin2, decay, xi, zeta, g_rows, ret_gain)


def _retention_bwd(proj, o_pre, states, d_mix, ret_gain, consts):
    l = proj.shape[0]
    n_chunks = l // CHUNK
    cos2, sin2, decay, xi, zeta, g_rows = consts
    scale = HEAD_DIM ** -0.5
    rev = lambda c: n_chunks - 1 - c

    def body(p_ref, o_ref, st_ref, dm_ref, cos_ref, sin_ref, dec_ref, dect_ref, xi_ref, zeta_ref, gr_ref, gain_ref,
             dp_ref, dgain_ref, dstate):
        step = pl.program_id(0)

        @pl.when(step == 0)
        def _():
            dstate[...] = jnp.zeros_like(dstate)
            dgain_ref[...] = jnp.zeros_like(dgain_ref)

        cos_v, sin_v = cos_ref[...], sin_ref[...]
        valid = _row_valid(rev(step), CHUNK)
        for h in range(N_HEADS):
            cols = slice(h * HEAD_DIM, (h + 1) * HEAD_DIM)
            q = p_ref[:, h * HEAD_DIM:(h + 1) * HEAD_DIM]
            k = p_ref[:, GROUP + h * HEAD_DIM:GROUP + (h + 1) * HEAD_DIM]
            v = p_ref[:, 2 * GROUP + h * HEAD_DIM:2 * GROUP + (h + 1) * HEAD_DIM]
            g = p_ref[:, 3 * GROUP + h * HEAD_DIM:3 * GROUP + (h + 1) * HEAD_DIM]
            o = o_ref[:, cols]
            gain = gain_ref[:, cols]
            d_ret = jnp.where(valid, dm_ref[:, cols], 0.0)
            mu = jnp.mean(o, axis=-1, keepdims=True)
            oc = o - mu
            rstd = lax.rsqrt(jnp.mean(oc * oc, axis=-1, keepdims=True) + NORM_EPS)
            yn = oc * rstd
            sig = _sigmoid(g)
            gate = g * sig
            dgain_ref[:, cols] += jnp.sum(d_ret * gate * yn, axis=0, keepdims=True)
            d_g = d_ret * (yn * gain) * (sig * (1.0 + g * (1.0 - sig)))
            d_yn = d_ret * gate * gain
            d_o = rstd * (d_yn - jnp.mean(d_yn, axis=-1, keepdims=True)
                          - yn * jnp.mean(d_yn * yn, axis=-1, keepdims=True))
            rq = _rope(q, cos_v, sin_v)
            rk = _rope(k, cos_v, sin_v) * scale
            rqb, rkb, vb = rq.astype(MXU_DTYPE), rk.astype(MXU_DTYPE), v.astype(MXU_DTYPE)
            dob = d_o.astype(MXU_DTYPE)
            dec = dec_ref[h]
            xi_h, zeta_h = xi_ref[h], zeta_ref[h]
            st_b = st_ref[h].astype(MXU_DTYPE)
            dst = dstate[h]
            dst_b = dst.astype(MXU_DTYPE)
            dec_t = dect_ref[h]
            s_t_b = (_dot_nt(rkb, rqb) * dec_t).astype(MXU_DTYPE)
            da_b = (_dot_nt(dob, vb) * dec).astype(MXU_DTYPE)
            da_t_b = (_dot_nt(vb, dob) * dec_t).astype(MXU_DTYPE)
            doxi_b = (d_o * xi_h).astype(MXU_DTYPE)
            kz_b = (rk * zeta_h).astype(MXU_DTYPE)
            d_rq = _dot(da_b, rkb) + _dot_nt(doxi_b, st_b)
            d_rk = _dot(da_t_b, rqb) + _dot_nt(vb, dst_b) * zeta_h
            d_v = _dot(s_t_b, dob) + _dot(kz_b, dst_b)
            dstate[h] = gr_ref[h, 0:1, :] * dst + _dot_tn(rqb, doxi_b)
            d_q = _rope_bwd(d_rq, cos_v, sin_v)
            d_k = _rope_bwd(d_rk * scale, cos_v, sin_v)
            dp_ref[:, h * HEAD_DIM:(h + 1) * HEAD_DIM] = d_q.astype(dp_ref.dtype)
            dp_ref[:, GROUP + h * HEAD_DIM:GROUP + (h + 1) * HEAD_DIM] = d_k.astype(dp_ref.dtype)
            dp_ref[:, 2 * GROUP + h * HEAD_DIM:2 * GROUP + (h + 1) * HEAD_DIM] = d_v.astype(dp_ref.dtype)
            dp_ref[:, 3 * GROUP + h * HEAD_DIM:3 * GROUP + (h + 1) * HEAD_DIM] = d_g.astype(dp_ref.dtype)

    head_tab = pl.BlockSpec((N_HEADS, CHUNK, HEAD_DIM), lambda c: (0, 0, 0))
    return pl.pallas_call(
        body, name="retention_bwd",
        out_shape=(jax.ShapeDtypeStruct((l, 4 * GROUP), MXU_DTYPE), jax.ShapeDtypeStruct((1, GROUP), F32)),
        grid=(n_chunks,),
        in_specs=[pl.BlockSpec((CHUNK, 4 * GROUP), lambda c: (rev(c), 0)),
                  pl.BlockSpec((CHUNK, GROUP), lambda c: (rev(c), 0)),
                  pl.BlockSpec((None, N_HEADS, HEAD_DIM, HEAD_DIM), lambda c: (rev(c), 0, 0, 0)),
                  pl.BlockSpec((CHUNK, GROUP), lambda c: (rev(c), 0)),
                  pl.BlockSpec((CHUNK, HEAD_DIM), lambda c: (rev(c), 0)),
                  pl.BlockSpec((CHUNK, HEAD_DIM), lambda c: (rev(c), 0)),
                  head_tab, head_tab, head_tab, head_tab,
                  pl.BlockSpec((N_HEADS, 8, HEAD_DIM), lambda c: (0, 0, 0)),
                  pl.BlockSpec((1, GROUP), lambda c: (0, 0))],
        out_specs=(pl.BlockSpec((CHUNK, 4 * GROUP), lambda c: (rev(c), 0)),
                   pl.BlockSpec((1, GROUP), lambda c: (0, 0))),
        scratch_shapes=[pltpu.VMEM((N_HEADS, HEAD_DIM, HEAD_DIM), F32)],
        compiler_params=_params(("arbitrary",)),
    )(proj, o_pre, states, d_mix, cos2, sin2, decay, jnp.transpose(decay, (0, 2, 1)), xi, zeta, g_rows, ret_gain)


FF_TILE = (7 * GROUP) // 128


def _log_forget(ff, bias_row, valid):
    x = ff + bias_row
    e = jnp.exp(-jnp.abs(x))
    lf = jnp.minimum(x, 0.0) - jnp.log(1.0 + e)
    head_lane = lax.broadcasted_iota(jnp.int32, x.shape, 1) < N_HEADS
    keep = lambda t: jnp.where(head_lane, jnp.where(valid, t, 0.0), 0.0)
    return keep(lf), keep(jnp.where(x >= 0, e, 1.0) / (1.0 + e))


def _fox_prep(proj, bias_row):
    l = proj.shape[0]
    n_blocks = l // CHUNK

    def body(ff_ref, b_ref, bc_ref, rows_ref, cum):
        r = lax.broadcasted_iota(jnp.int32, (CHUNK, CHUNK), 0)
        cidx = lax.broadcasted_iota(jnp.int32, (CHUNK, CHUNK), 1)
        tri = jnp.where(r >= cidx, 1.0, 0.0).astype(F32)
        carry = jnp.zeros((1, 128), F32)
        for blk in range(n_blocks):
            rows = slice(blk * CHUNK, (blk + 1) * CHUNK)
            valid = _row_valid(blk, CHUNK)
            lf, _ = _log_forget(ff_ref[rows, :], b_ref[...], valid)
            local = jnp.dot(tri, lf, precision=lax.Precision.HIGHEST, preferred_element_type=F32) + carry
            carry = local[CHUNK - 1:CHUNK, :]
            masked = jnp.where(valid, local, -NEG_BIG)
            cum[rows, :] = masked
            t = masked.T
            for h in range(N_HEADS):
                rows_ref[h, :, rows] = t[h:h + 1, :]
        full = cum[...]
        for h in range(N_HEADS):
            bc_ref[h] = jnp.broadcast_to(full[:, h:h + 1], (l, 128))

    return pl.pallas_call(
        body, name="fox_prep",
        out_shape=(jax.ShapeDtypeStruct((N_HEADS, l, 128), F32), jax.ShapeDtypeStruct((N_HEADS, 1, l), F32)),
        grid=(1,),
        in_specs=[pl.BlockSpec((l, 128), lambda i: (0, FF_TILE)), pl.BlockSpec((1, 128), lambda i: (0, 0))],
        out_specs=(pl.BlockSpec((N_HEADS, l, 128), lambda i: (0, 0, 0)),
                   pl.BlockSpec((N_HEADS, 1, l), lambda i: (0, 0, 0))),
        scratch_shapes=[pltpu.VMEM((l, 128), F32)],
        compiler_params=_params(("arbitrary",)),
    )(proj, bias_row)


ATTN_BLOCK = 2 * CHUNK


def _attn_blocks(l):
    assert (l - CHUNK) % ATTN_BLOCK == 0
    return [(0, CHUNK)] + [(s, ATTN_BLOCK) for s in range(CHUNK, l, ATTN_BLOCK)]


def _rows_valid(start, size):
    return start + lax.broadcasted_iota(jnp.int32, (size, 1), 0) >= PAD_ROWS


def _fox_fwd(proj, cum_bc, cum_rows, mix):
    l = proj.shape[0]
    blocks = _attn_blocks(l)
    scale = HEAD_DIM ** -0.5
    qt, kt, vt = 4 * N_HEADS, 5 * N_HEADS, 6 * N_HEADS

    def body(q_ref, k_ref, v_ref, cbc_ref, crow_ref, mix_in, o_ref, lse_ref, qb_s, kb_s, vb_s):
        qb_s[...] = q_ref[...].astype(MXU_DTYPE)
        kb_s[...] = k_ref[...].astype(MXU_DTYPE)
        vb_s[...] = v_ref[...].astype(MXU_DTYPE)
        for p, (qs, qn) in enumerate(blocks):
            qb = qb_s[qs:qs + qn, :]
            cq = cbc_ref[qs:qs + qn, :]
            m = jnp.full((qn, 1), NEG_BIG, F32)
            lsum = jnp.zeros((qn, 1), F32)
            acc = jnp.zeros((qn, HEAD_DIM), F32)
            for j in range(p + 1):
                ks, kn = blocks[j]
                bias = jnp.tile(cq, (1, kn // CHUNK)) - crow_ref[:, ks:ks + kn]
                s = _dot_nt(qb, kb_s[ks:ks + kn, :]) * scale + bias
                if j == p:
                    q_pos = qs + lax.broadcasted_iota(jnp.int32, (qn, kn), 0)
                    k_pos = ks + lax.broadcasted_iota(jnp.int32, (qn, kn), 1)
                    s = jnp.where(k_pos <= q_pos, s, NEG_BIG)
                m_new = jnp.maximum(m, jnp.max(s, axis=-1, keepdims=True))
                alpha = jnp.exp(m - m_new)
                pr = jnp.exp(s - m_new)
                lsum = lsum * alpha + jnp.sum(pr, axis=-1, keepdims=True)
                acc = acc * alpha + _dot(pr.astype(MXU_DTYPE), vb_s[ks:ks + kn, :])
                m = m_new
            o = jnp.where(_rows_valid(qs, qn), acc * (1.0 / lsum), 0.0)
            o_ref[qs:qs + qn, :] = o.astype(o_ref.dtype)
            lse = m + jnp.log(lsum)
            lse_ref[:, qs:qs + qn] = jnp.broadcast_to(lse, (qn, CHUNK)).T[0:1, :]

    head_col = lambda t: pl.BlockSpec((l, HEAD_DIM), lambda h: (0, t + h))
    return pl.pallas_call(
        body, name="fox_fwd",
        out_shape=(jax.ShapeDtypeStruct(mix.shape, mix.dtype), jax.ShapeDtypeStruct((N_HEADS, 1, l), F32)),
        grid=(N_HEADS,),
        in_specs=[head_col(qt), head_col(kt), head_col(vt),
                  pl.BlockSpec((None, l, 128), lambda h: (h, 0, 0)),
                  pl.BlockSpec((None, 1, l), lambda h: (h, 0, 0)),
                  ANY],
        out_specs=(head_col(N_HEADS), pl.BlockSpec((None, 1, l), lambda h: (h, 0, 0))),
        input_output_aliases={5: 0},
        scratch_shapes=[pltpu.VMEM((l, HEAD_DIM), MXU_DTYPE)] * 3,
        compiler_params=_params(("parallel",)),
    )(proj, proj, proj, cum_bc, cum_rows, mix)


def _fox_bwd(proj, cum_bc, cum_rows, d_mix, lse_rows):
    l = proj.shape[0]
    blocks = _attn_blocks(l)
    scale = HEAD_DIM ** -0.5
    qt, kt, vt = 4 * N_HEADS, 5 * N_HEADS, 6 * N_HEADS

    def body(q_ref, k_ref, v_ref, do_ref, cbc_ref, crow_ref, lse_ref,
             dq_ref, dk_ref, dv_ref, ds_ref, dk_acc, dv_acc, qb_s, kb_s, vb_s, dob_s, p_s, dp_s):
        qb_s[...] = q_ref[...].astype(MXU_DTYPE)
        kb_s[...] = k_ref[...].astype(MXU_DTYPE)
        vb_s[...] = v_ref[...].astype(MXU_DTYPE)
        dob_s[...] = jnp.where(_rows_valid(0, l), do_ref[...], 0.0).astype(MXU_DTYPE)
        dk_acc[...] = jnp.zeros_like(dk_acc)
        dv_acc[...] = jnp.zeros_like(dv_acc)
        ds_ref[...] = jnp.zeros_like(ds_ref)
        shift_row = crow_ref[...] - lse_ref[...]

        for p, (qs, qn) in enumerate(blocks):
            qb, dob = qb_s[qs:qs + qn, :], dob_s[qs:qs + qn, :]
            shift = shift_row[:, qs:qs + qn]

            delta = jnp.zeros((1, qn), F32)
            for j in range(p + 1):
                ks, kn = blocks[j]
                ck = jnp.tile(cbc_ref[ks:ks + kn, :], (1, qn // CHUNK))
                s_t = _dot_nt(kb_s[ks:ks + kn, :], qb) * scale + (shift - ck)
                if j == p:
                    k_pos = ks + lax.broadcasted_iota(jnp.int32, (kn, qn), 0)
                    q_pos = qs + lax.broadcasted_iota(jnp.int32, (kn, qn), 1)
                    s_t = jnp.where(k_pos <= q_pos, s_t, NEG_BIG)
                p_t, dp_t = jnp.exp(s_t), _dot_nt(vb_s[ks:ks + kn, :], dob)
                p_s[j, 0:kn, 0:qn] = p_t
                dp_s[j, 0:kn, 0:qn] = dp_t
                delta = delta + jnp.sum(p_t * dp_t, axis=0, keepdims=True)
            dq = jnp.zeros((qn, HEAD_DIM), F32)
            for j in range(p + 1):
                ks, kn = blocks[j]
                rows = slice(ks, ks + kn)
                p_t, dp_t = p_s[j, 0:kn, 0:qn], dp_s[j, 0:kn, 0:qn]
                ds_t = p_t * (dp_t - delta)
                ds_b = ds_t.astype(MXU_DTYPE)
                dv_acc[rows, :] += _dot(p_t.astype(MXU_DTYPE), dob)
                dk_acc[rows, :] += _dot(ds_b, qb) * scale
                ds_ref[rows, :] += sum(ds_t[:, c:c + CHUNK] for c in range(0, qn, CHUNK))
                dq = dq + _dot_tn(ds_b, kb_s[rows, :])
            dq_ref[qs:qs + qn, :] = (dq * scale).astype(dq_ref.dtype)

        dk_ref[...] = dk_acc[...].astype(dk_ref.dtype)
        dv_ref[...] = dv_acc[...].astype(dv_ref.dtype)

    col = jax.ShapeDtypeStruct((l, GROUP), MXU_DTYPE)
    head_col = lambda t: pl.BlockSpec((l, HEAD_DIM), lambda h: (0, t + h))
    return pl.pallas_call(
        body, name="fox_bwd",
        out_shape=(col, col, col, jax.ShapeDtypeStruct((N_HEADS, l, 128), F32)),
        grid=(N_HEADS,),
        in_specs=[head_col(qt), head_col(kt), head_col(vt), head_col(N_HEADS),
                  pl.BlockSpec((None, l, 128), lambda h: (h, 0, 0)),
                  pl.BlockSpec((None, 1, l), lambda h: (h, 0, 0)),
                  pl.BlockSpec((None, 1, l), lambda h: (h, 0, 0))],
        out_specs=(head_col(0), head_col(0), head_col(0), pl.BlockSpec((None, l, 128), lambda h: (h, 0, 0))),
        scratch_shapes=([pltpu.VMEM((l, HEAD_DIM), F32)] * 2 + [pltpu.VMEM((l, HEAD_DIM), MXU_DTYPE)] * 4
                        + [pltpu.VMEM((len(blocks), ATTN_BLOCK, ATTN_BLOCK), F32)] * 2),
        compiler_params=_params(("parallel",)),
    )(proj, proj, proj, d_mix, cum_bc, cum_rows, lse_rows)


def _fox_gate_bwd(ds_sum, proj, bias_row):
    l = proj.shape[0]
    n_blocks = l // CHUNK

    def body(ds_ref, ff_ref, b_ref, dff_ref, db_ref):
        r = lax.broadcasted_iota(jnp.int32, (CHUNK, CHUNK), 0)
        cidx = lax.broadcasted_iota(jnp.int32, (CHUNK, CHUNK), 1)
        upper = jnp.where(cidx >= r, 1.0, 0.0).astype(F32)
        carry = jnp.zeros((1, 128), F32)
        db = jnp.zeros((1, 128), F32)
        for blk in reversed(range(n_blocks)):
            rows = slice(blk * CHUNK, (blk + 1) * CHUNK)
            key_sum = jnp.zeros((CHUNK, 128), F32)
            for h in range(N_HEADS):
                select = jnp.where(cidx == h, 1.0, 0.0).astype(F32)
                key_sum = key_sum + jnp.dot(ds_ref[h, rows, :], select, precision=lax.Precision.HIGHEST,
                                            preferred_element_type=F32)
            suffix = jnp.dot(upper, key_sum, precision=lax.Precision.HIGHEST, preferred_element_type=F32) + carry
            carry = suffix[0:1, :]
            _, dsig = _log_forget(ff_ref[rows, :], b_ref[...], _row_valid(blk, CHUNK))
            dff = -suffix * dsig
            dff_ref[rows, :] = dff.astype(dff_ref.dtype)
            db = db + jnp.sum(dff, axis=0, keepdims=True)
        db_ref[...] = db

    return pl.pallas_call(
        body, name="fox_gate_bwd",
        out_shape=(jax.ShapeDtypeStruct((l, 128), MXU_DTYPE), jax.ShapeDtypeStruct((1, 128), F32)),
        grid=(1,),
        in_specs=[pl.BlockSpec((N_HEADS, l, 128), lambda i: (0, 0, 0)),
                  pl.BlockSpec((l, 128), lambda i: (0, FF_TILE)),
                  pl.BlockSpec((1, 128), lambda i: (0, 0))],
        out_specs=(pl.BlockSpec((l, 128), lambda i: (0, 0)), pl.BlockSpec((1, 128), lambda i: (0, 0))),
        compiler_params=_params(("arbitrary",)),
    )(ds_sum, proj, bias_row)


def _conv(u, w, b):
    return b + w[0:1, :] * pltpu.roll(u, 2, 0) + w[1:2, :] * pltpu.roll(u, 1, 0) + w[2:3, :] * u


def _conv_act_fwd(u, conv_w, conv_b, d_ff):
    l = u.shape[0]
    tc = _divisor_tile(d_ff, 256, 128)
    nt = d_ff // tc

    def body(ug_ref, uv_ref, wg_ref, wv_ref, bg_ref, bv_ref, a_ref, y_ref):
        yg = _conv(ug_ref[...], wg_ref[...], bg_ref[...])
        yv = _conv(uv_ref[...], wv_ref[...], bv_ref[...])
        act = yg * _sigmoid(yg) * yv
        a_ref[...] = jnp.where(_row_valid(0, l), act, 0.0).astype(a_ref.dtype)
        y_ref[0] = yg.astype(y_ref.dtype)
        y_ref[1] = yv.astype(y_ref.dtype)

    return pl.pallas_call(
        body, name="conv_act_fwd",
        out_shape=(jax.ShapeDtypeStruct((l, d_ff), MXU_DTYPE), jax.ShapeDtypeStruct((2, l, d_ff), MXU_DTYPE)),
        grid=(nt,),
        in_specs=[pl.BlockSpec((l, tc), lambda j: (0, j)), pl.BlockSpec((l, tc), lambda j: (0, j + nt)),
                  pl.BlockSpec((8, tc), lambda j: (0, j)), pl.BlockSpec((8, tc), lambda j: (0, j + nt)),
                  pl.BlockSpec((1, tc), lambda j: (0, j)), pl.BlockSpec((1, tc), lambda j: (0, j + nt))],
        out_specs=(pl.BlockSpec((l, tc), lambda j: (0, j)), pl.BlockSpec((2, l, tc), lambda j: (0, 0, j))),
        compiler_params=_params(("parallel",)),
    )(u, u, conv_w, conv_w, conv_b, conv_b)


def _conv_act_bwd(u, y, conv_w, d_act, d_ff):
    l = u.shape[0]
    tc = _divisor_tile(d_ff, 256, 128)
    nt = d_ff // tc

    def body(ug_ref, uv_ref, y_ref, wg_ref, wv_ref, da_ref, du_ref, dwb_ref):
        valid = _row_valid(0, l)
        ug, uv = ug_ref[...], uv_ref[...]
        wg, wv = wg_ref[...], wv_ref[...]
        yg, yv = y_ref[0].astype(F32), y_ref[1].astype(F32)
        sig = _sigmoid(yg)
        da = jnp.where(valid, da_ref[...], 0.0)
        d_yv = da * (yg * sig)
        d_yg = da * yv * (sig * (1.0 + yg * (1.0 - sig)))
        for idx, (dy, uu, w) in enumerate(((d_yg, ug, wg), (d_yv, uv, wv))):
            du = w[2:3, :] * dy + w[1:2, :] * pltpu.roll(dy, l - 1, 0) + w[0:1, :] * pltpu.roll(dy, l - 2, 0)
            du_ref[idx] = jnp.where(valid, du, 0.0).astype(du_ref.dtype)
            dwb_ref[idx, 0:1, :] = jnp.sum(dy * pltpu.roll(uu, 2, 0), axis=0, keepdims=True)
            dwb_ref[idx, 1:2, :] = jnp.sum(dy * pltpu.roll(uu, 1, 0), axis=0, keepdims=True)
            dwb_ref[idx, 2:3, :] = jnp.sum(dy * uu, axis=0, keepdims=True)
            dwb_ref[idx, 3:4, :] = jnp.sum(dy, axis=0, keepdims=True)
            dwb_ref[idx, 4:8, :] = jnp.zeros((4, tc), F32)

    return pl.pallas_call(
        body, name="conv_act_bwd",
        out_shape=(jax.ShapeDtypeStruct((2, l, d_ff), MXU_DTYPE), jax.ShapeDtypeStruct((2, 8, d_ff), F32)),
        grid=(nt,),
        in_specs=[pl.BlockSpec((l, tc), lambda j: (0, j)), pl.BlockSpec((l, tc), lambda j: (0, j + nt)),
                  pl.BlockSpec((2, l, tc), lambda j: (0, 0, j)),
                  pl.BlockSpec((8, tc), lambda j: (0, j)), pl.BlockSpec((8, tc), lambda j: (0, j + nt)),
                  pl.BlockSpec((l, tc), lambda j: (0, j))],
        out_specs=(pl.BlockSpec((2, l, tc), lambda j: (0, 0, j)), pl.BlockSpec((2, 8, tc), lambda j: (0, 0, j))),
        compiler_params=_params(("parallel",)),
    )(u, u, y, conv_w, conv_w, d_act)


def _adamw(w, g, m, v, name):
    shape = w.shape
    if w.ndim == 1:
        as2d = (1, shape[0])
    else:
        as2d = (int(np.prod(shape[:-1])), shape[-1])
    r, c = as2d
    tr = _divisor_tile(r, 256, 8)
    spec = pl.BlockSpec((tr, c), lambda i: (i, 0))

    def body(w_ref, g_ref, m_ref, v_ref, d_ref, nm_ref, nv_ref):
        d_ref[...], nm_ref[...], nv_ref[...] = _adamw_math(w_ref[...], g_ref[...], m_ref[...], v_ref[...])

    sds = jax.ShapeDtypeStruct(as2d, F32)
    outs = pl.pallas_call(
        body, name=name, out_shape=(sds, sds, sds), grid=(r // tr,),
        in_specs=[spec] * 4, out_specs=(spec,) * 3,
        compiler_params=_params(("parallel",)),
    )(w.reshape(as2d), g.reshape(as2d), m.reshape(as2d), v.reshape(as2d))
    return tuple(o.reshape(shape) for o in outs)


def _pad_rows(a, rows):
    return jnp.pad(a, ((0, rows - a.shape[0]), (0, 0)))


def kernel(x, meta_tokens, norm1_gain, w_in, b_forget, ret_norm_gain, w_out, norm2_gain, w_up, conv_w, conv_b, w_down, final_norm_gain, loss_target, m_meta_tokens, m_norm1_gain, m_w_in, m_b_forget, m_ret_norm_gain, m_w_out, m_norm2_gain, m_w_up, m_conv_w, m_conv_b, m_w_down, m_final_norm_gain, v_meta_tokens, v_norm1_gain, v_w_in, v_b_forget, v_ret_norm_gain, v_w_out, v_norm2_gain, v_w_up, v_conv_w, v_conv_b, v_w_down, v_final_norm_gain):
    seq, d = x.shape[1], x.shape[2]
    l = CHUNK + seq
    d_ff = w_down.shape[1] * N_DEV
    up_shard = w_up.shape[2]
    assert 4 * up_shard == d_ff and w_in.shape[2] == WIN_SHARD and d == 2 * GROUP
    dev = _device_index()
    mx, my, mc = _my_position()
    core = jnp.reshape(mc, (1,)).astype(jnp.int32)
    chip = jnp.reshape(2 * mx + my, (1,)).astype(jnp.int32)
    dev1 = jnp.reshape(dev, (1,)).astype(jnp.int32)

    small = jnp.concatenate([meta_tokens.reshape(-1, 128), conv_w[0].reshape(-1, 128)], axis=0)
    n_meta_rows = N_META * (d // N_DEV) // 128
    small_rows = small.shape[0]
    to_rows = lambda t: jnp.pad(jnp.transpose(t[0]), ((0, WIN_ROWS - WIN_SHARD), (0, 0)))
    from_rows = lambda t: jnp.transpose(t[:WIN_SHARD])[None]
    w_in_rows = to_rows(w_in)
    out_rows = d // N_DEV
    mixer_rows = -(-(WIN_ROWS + out_rows) // 32) * 32
    mixer_shard = jnp.concatenate([w_in_rows.astype(WIRE_DTYPE), w_out[0].astype(WIRE_DTYPE),
                                   jnp.zeros((mixer_rows - WIN_ROWS - out_rows, d), WIRE_DTYPE)], axis=0)

    consts = _retention_consts(l)
    bias_row = jnp.pad(b_forget, ((0, 0), (0, 128 - N_HEADS)))
    mixer_blocks, small_all, h0, a = _gather_ring_with_norm(
        mixer_shard, _pad_rows(small, -(-small_rows // 8) * 8), dev1, x[0], norm1_gain, "gather_w_in")
    start_up = _gather_start(w_up[0], dev1, mixer_blocks, "gather_w_up_start")
    meta_full = jnp.transpose(small_all[:, :n_meta_rows].reshape(N_DEV, N_META, d // N_DEV), (1, 0, 2)).reshape(N_META, d)
    conv_w_full = _pad_rows(jnp.transpose(small_all[:, n_meta_rows:small_rows].reshape(N_DEV, 3, up_shard),
                                          (1, 0, 2)).reshape(3, 2 * d_ff), 8)
    h0, a = _embed_front(h0, a, meta_full, norm1_gain)
    w_in_full = _assemble_w_in(mixer_blocks).astype(MXU_DTYPE)
    proj = _mm_nt(a, w_in_full, F32, "mm_proj", after=start_up[4])
    ret_mix, ret_pre, ret_states = _retention_fwd(proj, ret_norm_gain, consts)
    cum_bc, cum_rows = _fox_prep(proj, bias_row)
    mix, lse_rows = _fox_fwd(proj, cum_bc, cum_rows, ret_mix)
    w_out_full = mixer_blocks[:, WIN_ROWS:WIN_ROWS + out_rows].reshape(d, d).astype(MXU_DTYPE)
    h1, cn = _out_proj_resid_rmsnorm(mix, w_out_full, h0, norm2_gain)
    w_up_blocks = _gather_finish(start_up, cn, "gather_w_up").astype(MXU_DTYPE)
    start_down = _gather_start(w_down[0], dev1, w_up_blocks, "gather_w_down_start")
    u = _mm_up(cn, w_up_blocks, start_down[4])
    pass_down = _gather_pass_start(start_down, u, "gather_w_down")
    act, conv_y = _conv_act_fwd(u, conv_w_full, conv_b + pass_down[4][0, 0], d_ff)
    w_down_full = _gather_pass_finish(pass_down, act, "gather_w_down").reshape(d_ff, d).astype(MXU_DTYPE)
    mlp_out = _mm_nn(act, w_down_full, F32, "mm_down", tm_cap=544, tk_cap=d_ff)
    d_h2, d_h2_b, dg_final, loss_part = _loss_head(h1, mlp_out, final_norm_gain.reshape(1, d), loss_target[0])

    gw_down = _mm_tn(act, d_h2_b, WIRE_DTYPE, "mm_gw_down", tm_cap=1408, tn_cap=1024)
    d2d_down = _reduce_scatter_d2d_start(gw_down.reshape(N_DEV, d_ff // N_DEV, d), d_h2, "rs_w_down")
    d_act = _mm_nt(d_h2_b, w_down_full, F32, "mm_d_act", after=d2d_down[4])
    rs_down = _reduce_scatter_ici_start(d2d_down, d_act, core, "rs_w_down")
    d_u, d_conv = _conv_act_bwd(u, conv_y, conv_w_full + rs_down[4][0, 0], d_act, d_ff)
    tm = _divisor_tile(l, 1088, 16)
    gw_up = _mm_gw_up(cn, d_u)
    d2d_up = _reduce_scatter_d2d_start(gw_up, d_act, "rs_w_up")
    d_cn = _mm_d_cn(d_u, w_up_blocks, d2d_up[4])
    rs_up = _reduce_scatter_ici_start(d2d_up, d_cn, core, "rs_w_up")
    d_h1, d_h1_b, dg_norm2 = _rmsnorm_bwd(d_h2, d_cn, h1, norm2_gain + rs_up[4][0, 0], "rmsnorm2_bwd", True)

    gw_out = _mm_tn(mix, d_h1_b, WIRE_DTYPE, "mm_gw_out")
    d2d_out = _reduce_scatter_d2d_start(gw_out.reshape(N_DEV, d // N_DEV, d), d_cn, "rs_w_out")
    d_mix = _mm_nt(d_h1_b, w_out_full, F32, "mm_d_mix", after=d2d_out[4])
    d_fq, d_fk, d_fv, ds_sum = _fox_bwd(proj, cum_bc, cum_rows, d_mix, lse_rows)
    d_ff_tile, db_forget_row = _fox_gate_bwd(ds_sum, proj, bias_row)
    d_ret, dg_ret = _retention_bwd(proj, ret_pre, ret_states, d_mix, ret_norm_gain, consts)
    rs_out = _reduce_scatter_ici_start(d2d_out, d_ret, core, "rs_w_out")
    d_proj = jnp.concatenate(
        [d_ret, d_fq, d_fk, d_fv, d_ff_tile, jnp.zeros((l, WIN_N - 7 * GROUP - 128), MXU_DTYPE)], axis=1)
    gw_in = _mm_tn(d_proj, a, WIRE_DTYPE, "mm_gw_in", tm_cap=1536, after=rs_out[4])
    rs_in = _reduce_scatter_start(_extract_w_in_windows(gw_in), core, "rs_w_in")
    d_a = _mm_nn(d_proj, w_in_full, F32, "mm_d_a", tm_cap=544, tn_cap=256, tk_cap=WIN_N, after=rs_in[4])
    d_front, d_tokens, dg_norm1 = _rmsnorm_bwd(d_h1, d_a, h0, norm1_gain + rs_in[4][0, 0], "rmsnorm1_bwd", False)
    grad_x = d_tokens[None]
    d_meta = d_front[PAD_ROWS:CHUNK]

    d_conv_w = jnp.concatenate([d_conv[0, 0:3], d_conv[1, 0:3]], axis=1)
    d_conv_b = jnp.concatenate([d_conv[0, 3:4], d_conv[1, 3:4]], axis=1)
    pieces = [loss_part[:, 0:1], dg_norm1, db_forget_row[:, 0:N_HEADS], dg_ret, dg_norm2, d_conv_b, dg_final,
              d_meta.reshape(1, -1), d_conv_w.reshape(1, -1)]
    sizes = [p.shape[1] for p in pieces]
    flat = jnp.concatenate(pieces, axis=1)
    padded = -(-flat.shape[1] // 1024) * 1024
    flat = jnp.pad(flat, ((0, 0), (0, padded - flat.shape[1]))).reshape(padded // 128, 128)
    small_ar = _small_all_reduce_start(flat, d_tokens, "all_reduce_small")

    lead = lambda outs: tuple(o[None] for o in outs)
    fin_down = lead(_reduce_scatter_finish(rs_down, small_ar[4], chip, w_down[0], m_w_down[0], v_w_down[0], "rs_w_down"))
    fin_up = lead(_reduce_scatter_finish(rs_up, fin_down[3], chip, w_up[0], m_w_up[0], v_w_up[0], "rs_w_up"))
    fin_out = lead(_reduce_scatter_finish(rs_out, fin_up[3], chip, w_out[0], m_w_out[0], v_w_out[0], "rs_w_out"))
    fin_in = tuple(from_rows(o) for o in _reduce_scatter_finish(
        rs_in, fin_out[3], chip, w_in_rows, to_rows(m_w_in), to_rows(v_w_in), "rs_w_in"))
    g_w_down, g_w_up, g_w_out, g_w_in = fin_down[0], fin_up[0], fin_out[0], fin_in[0]
    early = [fin_down[1:], fin_up[1:], fin_out[1:], fin_in[1:]]
    total = _small_all_reduce_finish(small_ar, fin_in[3], dev1, "all_reduce_small").reshape(1, padded)
    offs = np.concatenate([[0], np.cumsum(sizes)])
    take = lambda k: total[:, int(offs[k]):int(offs[k + 1])]
    loss = take(0).reshape(())
    g_norm1, g_bf, g_ret_gain, g_norm2 = take(1), take(2), take(3), take(4)
    g_conv_b, g_final = take(5), take(6).reshape(d)
    g_meta = lax.dynamic_slice(take(7).reshape(N_META, d), (jnp.int32(0), (dev * (d // N_DEV)).astype(jnp.int32)),
                               (N_META, d // N_DEV))
    g_conv_w = lax.dynamic_slice(take(8).reshape(3, 2 * d_ff), (jnp.int32(0), (dev * up_shard).astype(jnp.int32)),
                                 (3, up_shard))[None]

    weights = [meta_tokens, norm1_gain, w_in, b_forget, ret_norm_gain, w_out, norm2_gain, w_up, conv_w, conv_b,
               w_down, final_norm_gain]
    grads = [g_meta, g_norm1, g_w_in, g_bf, g_ret_gain, g_w_out, g_norm2, g_w_up, g_conv_w, g_conv_b, g_w_down,
             g_final]
    done = {"w_down": early[0], "w_up": early[1], "w_out": early[2], "w_in": early[3]}
    ms = [m_meta_tokens, m_norm1_gain, m_w_in, m_b_forget, m_ret_norm_gain, m_w_out, m_norm2_gain, m_w_up, m_conv_w,
          m_conv_b, m_w_down, m_final_norm_gain]
    vs = [v_meta_tokens, v_norm1_gain, v_w_in, v_b_forget, v_ret_norm_gain, v_w_out, v_norm2_gain, v_w_up, v_conv_w,
          v_conv_b, v_w_down, v_final_norm_gain]
    names = ["meta", "norm1", "w_in", "b_forget", "ret_gain", "w_out", "norm2", "w_up", "conv_w", "conv_b", "w_down",
             "final_gain"]
    deltas, new_ms, new_vs = [], [], []
    for w, g, m, v, n in zip(weights, grads, ms, vs, names):
        dl, nm, nv = done[n] if n in done else _adamw(w, g, m, v, "adamw_" + n)
        deltas.append(dl)
        new_ms.append(nm)
        new_vs.append(nv)
    return (loss, grad_x, *grads, *deltas, *new_ms, *new_vs)
```

```python
import numpy as np
import jax
import jax.numpy as jnp
from jax import lax
from jax.experimental import pallas as pl
from jax.experimental.pallas import tpu as pltpu

F32 = jnp.float32
MXU_DTYPE = jnp.bfloat16
WIRE_DTYPE = jnp.bfloat16

N_DEV = 8
N_META = 16
CHUNK = 128
PAD_ROWS = CHUNK - N_META
N_HEADS = 8
HEAD_DIM = 128
GROUP = N_HEADS * HEAD_DIM
IN_DIM = 7 * GROUP + N_HEADS
WIN_SHARD = IN_DIM // N_DEV
WIN_ROWS = 912
WIN_BLOCK = 1024
WIN_STRIDE = 896
WIN_N = 7680
ROPE_BASE = 10000.0
NORM_EPS = 1e-6
NEG_BIG = -1e30
ADAM_LR, ADAM_B1, ADAM_B2, ADAM_EPS, ADAM_WD, ADAM_STEP = 0.001, 0.9, 0.999, 1e-08, 0.01, 10
VMEM_LIMIT = 52 * 1024 * 1024
MESH = pl.DeviceIdType.MESH
ANY = pl.BlockSpec(memory_space=pl.ANY)
VMEM_SPEC = pl.BlockSpec(memory_space=pltpu.VMEM)


def _params(sem=None):
    kw = {"vmem_limit_bytes": VMEM_LIMIT}
    if sem is not None:
        kw["dimension_semantics"] = sem
    return pltpu.CompilerParams(**kw)


def _divisor_tile(n, cap, unit):
    if n <= cap:
        return n
    best = None
    for t in range(unit, cap + 1, unit):
        if n % t == 0:
            best = t
    assert best is not None, (n, cap, unit)
    return best


def _my_position():
    return lax.axis_index("x"), lax.axis_index("y"), lax.axis_index("c")


def _device_index():
    x, y, c = _my_position()
    return 4 * x + 2 * y + c


HBM_SPEC = pl.BlockSpec(memory_space=pltpu.HBM)
SEM_SPEC = pl.BlockSpec(memory_space=pltpu.SEMAPHORE)
DATAFLOW_EFFECT = pltpu.SideEffectType.DATAFLOW_SIDE_EFFECTING


def _in_hbm(a):
    return pltpu.with_memory_space_constraint(a, pltpu.HBM)


def _split_start(src, land, make_copies, n_copies, after, name):
    if isinstance(land, tuple):
        land = lax.empty(land, src.dtype)
    land_shape = land.shape
    def body(src_ref, land_ref, after_ref, send_sems, recv_sems, src_thru, land_thru, token):
        for cp in make_copies(src_ref, land_ref, send_sems, recv_sems):
            cp.start()
        token[...] = jnp.zeros_like(token)

    return pl.pallas_call(
        body, name=name,
        out_shape=(pltpu.SemaphoreType.DMA((n_copies,)), pltpu.SemaphoreType.DMA((n_copies,)),
                   pltpu.HBM(src.shape, src.dtype), pltpu.HBM(land_shape, land.dtype),
                   jax.ShapeDtypeStruct((8, 128), F32)),
        in_specs=(HBM_SPEC, HBM_SPEC, ANY), out_specs=(SEM_SPEC, SEM_SPEC, HBM_SPEC, HBM_SPEC, VMEM_SPEC),
        input_output_aliases={0: 2, 1: 3},
        compiler_params=pltpu.CompilerParams(has_side_effects=DATAFLOW_EFFECT),
    )(_in_hbm(src), _in_hbm(land), after)


def _split_wait(started, after, make_copies, name):
    send_sems, recv_sems, src_thru, land_thru, _ = started

    def body(src_ref, land_ref, send_sems_ref, recv_sems_ref, after_ref, src_dead, land_out):
        for cp in make_copies(src_ref, land_ref, send_sems_ref, recv_sems_ref):
            cp.wait_send()
            cp.wait_recv()

    return pl.pallas_call(
        body, name=name,
        out_shape=(pltpu.HBM(src_thru.shape, src_thru.dtype), pltpu.HBM(land_thru.shape, land_thru.dtype)),
        in_specs=(HBM_SPEC, HBM_SPEC, SEM_SPEC, SEM_SPEC, ANY), out_specs=(HBM_SPEC, HBM_SPEC),
        input_output_aliases={0: 0, 1: 1},
        compiler_params=pltpu.CompilerParams(has_side_effects=DATAFLOW_EFFECT),
    )(src_thru, land_thru, send_sems, recv_sems, after)


def _gather_copies(x_ref, land_ref, send_sems, recv_sems):
    mx, my, mc = _my_position()
    me = 4 * mx + 2 * my + mc
    targets = [(mx, my, 1 - mc), (1 - mx, my, mc), (mx, 1 - my, mc), (1 - mx, 1 - my, mc)]
    return [pltpu.make_async_remote_copy(
        src_ref=land_ref.at[me], dst_ref=land_ref.at[me], send_sem=send_sems.at[k], recv_sem=recv_sems.at[k],
        device_id=t, device_id_type=MESH) for k, t in enumerate(targets)]


def _own_slot(shard, dev, name):
    r, c = shard.shape
    tr = _divisor_tile(r, 640, 16)

    def body(s_ref, x_ref, o_ref):
        o_ref[...] = x_ref[...].astype(o_ref.dtype)

    return pl.pallas_call(
        body, name=name,
        out_shape=jax.ShapeDtypeStruct((N_DEV, r, c), WIRE_DTYPE),
        grid_spec=pltpu.PrefetchScalarGridSpec(
            num_scalar_prefetch=1, grid=(r // tr,),
            in_specs=[pl.BlockSpec((tr, c), lambda i, s: (i, 0))],
            out_specs=pl.BlockSpec((None, tr, c), lambda i, s: (s[0], i, 0))),
        compiler_params=_params(("parallel",)),
    )(dev, shard)


def _gather_start(land, after, name):
    return _split_start(jnp.zeros((8, 128), F32), land, _gather_copies, 4, after, name)


def _gather_ring_with_norm(shard, small, dev, x, gain, later_shards, name):
    r, c = shard.shape
    half = r // 2
    assert half % 16 == 0
    seq, d = x.shape
    n_tiles = seq // CHUNK

    n_later = len(later_shards)
    later_tiles = [_divisor_tile(s.shape[0], CHUNK, 16) for s in later_shards]

    def body(x_ref, small_ref, tok_ref, g_ref, land_in, *rest):
        later_refs, rest = rest[:n_later], rest[n_later:]
        land_ref, small_land, h_ref, n_ref = rest[:4]
        later_lands, rest = rest[4:4 + n_later], rest[4 + n_later:]
        send_sems, recv_sems, local_sem, xbuf, nbuf, in_sems, h_sems, n_sems = rest[:8]
        later_scratch = rest[8:]
        mx, my, mc = _my_position()
        sibling, x_nbr, y_nbr = (mx, my, 1 - mc), (1 - mx, my, mc), (mx, 1 - my, mc)
        first, second = pl.ds(0, half), pl.ds(half, half)

        def slot(px, py, pc):
            return land_ref.at[4 * px + 2 * py + pc]

        def copy(k, src, dst, to):
            return pltpu.make_async_remote_copy(src_ref=src, dst_ref=dst, send_sem=send_sems.at[k],
                                                recv_sem=recv_sems.at[k], device_id=to, device_id_type=MESH)

        def arrived(k, dst):
            copy(k, dst, dst, sibling).wait_recv()

        mine = slot(mx, my, mc)
        from_x, from_y, from_d = slot(1 - mx, my, mc), slot(mx, 1 - my, mc), slot(1 - mx, 1 - my, mc)
        sent = [copy(0, x_ref, mine, sibling), copy(1, x_ref, mine, x_nbr), copy(2, x_ref, mine, y_nbr)]
        for cp in sent:
            cp.start()

        def send(k, src, to):
            cp = copy(k, src, src, to)
            cp.start()
            sent.append(cp)

        my_small = small_land.at[4 * mx + 2 * my + mc]
        own_small = pltpu.make_async_copy(small_ref, my_small, local_sem)
        own_small.start()
        for rel in range(1, N_DEV):
            bx, by, bc = (rel >> 2) & 1, (rel >> 1) & 1, rel & 1
            cp = copy(8 + rel, small_ref, my_small, (1 - mx if bx else mx, 1 - my if by else my, 1 - mc if bc else mc))
            cp.start()
            sent.append(cp)

        def read(t):
            return pltpu.make_async_copy(tok_ref.at[pl.ds(t * CHUNK, CHUNK)], xbuf.at[t % 2], in_sems.at[t % 2])

        def writes(t):
            rows = pl.ds(CHUNK + t * CHUNK, CHUNK)
            return (pltpu.make_async_copy(xbuf.at[t % 2], h_ref.at[rows], h_sems.at[t % 2]),
                    pltpu.make_async_copy(nbuf.at[t % 2], n_ref.at[rows], n_sems.at[t % 2]))

        read(0).start()
        for t in range(n_tiles):
            if t + 1 < n_tiles:
                if t >= 1:
                    for w in writes(t - 1):
                        w.wait()
                read(t + 1).start()
            read(t).wait()
            nbuf[t % 2] = _rms(xbuf[t % 2], g_ref[...])
            for w in writes(t):
                w.start()
        for t in range(max(n_tiles - 2, 0), n_tiles):
            for w in writes(t):
                w.wait()

        for k in range(n_later):
            src, tile = later_refs[k], later_tiles[k]
            dst = later_lands[k].at[4 * mx + 2 * my + mc]
            inbuf, outbuf, isem, osem = later_scratch[4 * k:4 * k + 4]
            steps = src.shape[0] // tile

            def rd(t, src=src, tile=tile, inbuf=inbuf, isem=isem):
                return pltpu.make_async_copy(src.at[pl.ds(t * tile, tile)], inbuf.at[t % 2], isem.at[t % 2])

            def wr(t, dst=dst, tile=tile, outbuf=outbuf, osem=osem):
                return pltpu.make_async_copy(outbuf.at[t % 2], dst.at[pl.ds(t * tile, tile)], osem.at[t % 2])

            rd(0).start()
            for t in range(steps):
                if t + 1 < steps:
                    if t >= 1:
                        wr(t - 1).wait()
                    rd(t + 1).start()
                rd(t).wait()
                outbuf[t % 2] = inbuf[t % 2].astype(outbuf.dtype)
                wr(t).start()
            for t in range(max(steps - 2, 0), steps):
                wr(t).wait()

        arrived(1, from_x)
        send(3, from_x.at[first], y_nbr)
        send(5, from_x, sibling)
        arrived(2, from_y)
        send(4, from_y.at[second], x_nbr)
        send(6, from_y, sibling)
        arrived(3, from_d.at[first])
        send(7, from_d.at[first], sibling)
        arrived(4, from_d.at[second])
        send(8, from_d.at[second], sibling)
        arrived(0, slot(mx, my, 1 - mc))
        arrived(5, slot(1 - mx, my, 1 - mc))
        arrived(6, slot(mx, 1 - my, 1 - mc))
        arrived(7, slot(1 - mx, 1 - my, 1 - mc).at[first])
        arrived(8, slot(1 - mx, 1 - my, 1 - mc).at[second])
        for rel in range(1, N_DEV):
            arrived(8 + rel, my_small)
        for cp in sent:
            cp.wait_send()
        own_small.wait()

    land = _own_slot(shard, dev, name + "_own")
    later_scratch_shapes = []
    for s, tile in zip(later_shards, later_tiles):
        later_scratch_shapes += [pltpu.VMEM((2, tile, s.shape[1]), s.dtype), pltpu.VMEM((2, tile, s.shape[1]), WIRE_DTYPE),
                                 pltpu.SemaphoreType.DMA((2,)), pltpu.SemaphoreType.DMA((2,))]
    return pl.pallas_call(
        body, name=name,
        out_shape=(jax.ShapeDtypeStruct(land.shape, land.dtype), jax.ShapeDtypeStruct((N_DEV,) + small.shape, small.dtype),
                   jax.ShapeDtypeStruct((CHUNK + seq, d), F32), jax.ShapeDtypeStruct((CHUNK + seq, d), MXU_DTYPE),
                   *[jax.ShapeDtypeStruct((N_DEV,) + s.shape, WIRE_DTYPE) for s in later_shards]),
        in_specs=[ANY, ANY, ANY, VMEM_SPEC, ANY] + [ANY] * n_later, out_specs=(ANY,) * (4 + n_later),
        input_output_aliases={4: 0},
        scratch_shapes=[pltpu.SemaphoreType.DMA((16,)), pltpu.SemaphoreType.DMA((16,)), pltpu.SemaphoreType.DMA,
                        pltpu.VMEM((2, CHUNK, d), F32), pltpu.VMEM((2, CHUNK, d), MXU_DTYPE),
                        pltpu.SemaphoreType.DMA((2,)), pltpu.SemaphoreType.DMA((2,)), pltpu.SemaphoreType.DMA((2,))]
        + later_scratch_shapes,
        compiler_params=_params(),
    )(shard, small, x, gain, land, *later_shards)


def _embed_front(h0, normed, meta, gain):
    d = h0.shape[1]

    def body(h_in, n_in, m_ref, g_ref, h_ref, n_ref):
        front = jnp.concatenate([jnp.zeros((PAD_ROWS, d), F32), m_ref[...]], axis=0)
        h_ref[...] = front
        n_ref[...] = _rms(front, g_ref[...])

    blk = pl.BlockSpec((CHUNK, d), lambda i: (0, 0))
    return pl.pallas_call(
        body, name="embed_front",
        out_shape=(jax.ShapeDtypeStruct(h0.shape, h0.dtype), jax.ShapeDtypeStruct(normed.shape, normed.dtype)),
        grid=(1,),
        in_specs=[ANY, ANY, pl.BlockSpec((N_META, d), lambda i: (0, 0)), pl.BlockSpec((1, d), lambda i: (0, 0))],
        out_specs=(blk, blk),
        input_output_aliases={0: 0, 1: 1},
        compiler_params=_params(("arbitrary",)),
    )(h0, normed, meta, gain)


def _pass_copies(unused_ref, land_ref, send_sems, recv_sems):
    mx, my, mc = _my_position()
    chips = [(1 - mx, my), (mx, 1 - my), (1 - mx, 1 - my)]
    return [pltpu.make_async_remote_copy(
        src_ref=land_ref.at[4 * cx + 2 * cy + mc], dst_ref=land_ref.at[4 * cx + 2 * cy + mc],
        send_sem=send_sems.at[j], recv_sem=recv_sems.at[j],
        device_id=(mx, my, 1 - mc), device_id_type=MESH) for j, (cx, cy) in enumerate(chips)]


def _gather_pass_start(started, after, name):
    _, land = _split_wait(started, after, _gather_copies, name + "_wait")
    return _split_start(jnp.zeros((8, 128), F32), land, _pass_copies, 3, after, name + "_pass_start")


def _gather_pass_finish(pass_started, after, name):
    return _split_wait(pass_started, after, _pass_copies, name + "_pass_wait")[1]


def _gather_finish(started, after, name):
    _, land = _split_wait(started, after, _gather_copies, name + "_wait")

    def body(land_in, land_ref, send_sems, recv_sems):
        mx, my, mc = _my_position()
        chips = [(1 - mx, my), (mx, 1 - my), (1 - mx, 1 - my)]
        copies = [pltpu.make_async_remote_copy(
            src_ref=land_ref.at[4 * cx + 2 * cy + mc], dst_ref=land_ref.at[4 * cx + 2 * cy + mc],
            send_sem=send_sems.at[j], recv_sem=recv_sems.at[j],
            device_id=(mx, my, 1 - mc), device_id_type=MESH) for j, (cx, cy) in enumerate(chips)]
        for cp in copies:
            cp.start()
        for j, (cx, cy) in enumerate(chips):
            copies[j].wait_send()
            pltpu.make_async_remote_copy(
                src_ref=land_ref.at[4 * cx + 2 * cy + 1 - mc], dst_ref=land_ref.at[4 * cx + 2 * cy + 1 - mc],
                send_sem=send_sems.at[j], recv_sem=recv_sems.at[j],
                device_id=(mx, my, 1 - mc), device_id_type=MESH).wait_recv()

    return pl.pallas_call(
        body, name=name + "_pass",
        out_shape=jax.ShapeDtypeStruct(land.shape, land.dtype),
        in_specs=[ANY], out_specs=ANY,
        input_output_aliases={0: 0},
        scratch_shapes=[pltpu.SemaphoreType.DMA((3,)), pltpu.SemaphoreType.DMA((3,))],
    )(land)


def _chip_copies(p_ref, land_ref, send_sems, recv_sems):
    mx, my, mc = _my_position()
    chips = [(1 - mx, my), (mx, 1 - my), (1 - mx, 1 - my)]
    return [pltpu.make_async_remote_copy(
        src_ref=p_ref.at[2 * cx + cy], dst_ref=land_ref.at[j], send_sem=send_sems.at[j], recv_sem=recv_sems.at[j],
        device_id=(cx, cy, mc), device_id_type=MESH) for j, (cx, cy) in enumerate(chips)]


def _reduce_scatter_start(g, core, name):
    pair = _pair_sum(g, _exchange_sibling(g, name + "_d2d"), core, name + "_pairsum")
    return _split_start(pair, (3,) + pair.shape[1:], _chip_copies, 3, g, name + "_ici_start")


def _sibling_copies(g_ref, land_ref, send_sems, recv_sems):
    mx, my, mc = _my_position()
    return [pltpu.make_async_remote_copy(
        src_ref=g_ref.at[2 * k + (1 - mc)], dst_ref=land_ref.at[k], send_sem=send_sems.at[k], recv_sem=recv_sems.at[k],
        device_id=(mx, my, 1 - mc), device_id_type=MESH) for k in range(4)]


def _reduce_scatter_d2d_start(g, after, name):
    return _split_start(g, (4,) + g.shape[1:], _sibling_copies, 4, after, name + "_d2d_start")


def _reduce_scatter_ici_start(d2d_started, after, core, name):
    g, from_sibling = _split_wait(d2d_started, after, _sibling_copies, name + "_d2d_wait")
    pair = _pair_sum(g, from_sibling, core, name + "_pairsum")
    return _split_start(pair, (3,) + pair.shape[1:], _chip_copies, 3, g, name + "_ici_start")


def _reduce_scatter_finish(started, after, chip, w, m, v, name):
    pair, from_chips = _split_wait(started, after, _chip_copies, name + "_ici_wait")
    return _final_sum_adamw(pair, from_chips, chip, w, m, v, name + "_sum_adamw")


def _exchange_sibling(g, name):
    _, r, c = g.shape

    def body(g_ref, out_ref, send_sems, recv_sems):
        mx, my, mc = _my_position()
        copies = [
            pltpu.make_async_remote_copy(
                src_ref=g_ref.at[2 * k + (1 - mc)], dst_ref=out_ref.at[k],
                send_sem=send_sems.at[k], recv_sem=recv_sems.at[k],
                device_id=(mx, my, 1 - mc), device_id_type=MESH)
            for k in range(4)]
        for cp in copies:
            cp.start()
        for cp in copies:
            cp.wait()

    return pl.pallas_call(
        body, name=name,
        out_shape=jax.ShapeDtypeStruct((4, r, c), g.dtype),
        in_specs=[ANY], out_specs=ANY,
        scratch_shapes=[pltpu.SemaphoreType.DMA((4,)), pltpu.SemaphoreType.DMA((4,))],
    )(g)


def _pair_sum(g, recv, core, name):
    _, r, c = g.shape
    tr = _divisor_tile(r, 512, 16)

    def body(s_ref, g_ref, r_ref, o_ref):
        o_ref[...] = (g_ref[...].astype(F32) + r_ref[...].astype(F32)).astype(o_ref.dtype)

    return pl.pallas_call(
        body, name=name,
        out_shape=jax.ShapeDtypeStruct((4, r, c), g.dtype),
        grid_spec=pltpu.PrefetchScalarGridSpec(
            num_scalar_prefetch=1, grid=(4, r // tr),
            in_specs=[pl.BlockSpec((None, tr, c), lambda k, i, s: (2 * k + s[0], i, 0)),
                      pl.BlockSpec((None, tr, c), lambda k, i, s: (k, i, 0))],
            out_specs=pl.BlockSpec((None, tr, c), lambda k, i, s: (k, i, 0))),
        compiler_params=_params(("parallel", "parallel")),
    )(core, g, recv)


def _adamw_math(w, g, m, v):
    nm = ADAM_B1 * m + (1.0 - ADAM_B1) * g
    nv = ADAM_B2 * v + (1.0 - ADAM_B2) * (g * g)
    m_hat = nm / (1.0 - ADAM_B1 ** ADAM_STEP)
    v_hat = nv / (1.0 - ADAM_B2 ** ADAM_STEP)
    return -ADAM_LR * (m_hat / (jnp.sqrt(v_hat) + ADAM_EPS) + ADAM_WD * w), nm, nv


def _final_sum_adamw(p, recv, chip, w, m, v, name):
    _, r, c = p.shape
    tr = _divisor_tile(r, 256, 16)
    tile = lambda: pl.BlockSpec((tr, c), lambda i, s: (i, 0))

    def body(s_ref, p_ref, r_ref, w_ref, m_ref, v_ref, g_ref, d_ref, nm_ref, nv_ref):
        g = p_ref[...].astype(F32)
        for j in range(3):
            g = g + r_ref[j].astype(F32)
        g_ref[...] = g
        d_ref[...], nm_ref[...], nv_ref[...] = _adamw_math(w_ref[...], g, m_ref[...], v_ref[...])

    sds = jax.ShapeDtypeStruct((r, c), F32)
    return pl.pallas_call(
        body, name=name,
        out_shape=(sds, sds, sds, sds),
        grid_spec=pltpu.PrefetchScalarGridSpec(
            num_scalar_prefetch=1, grid=(r // tr,),
            in_specs=[pl.BlockSpec((None, tr, c), lambda i, s: (s[0], i, 0)),
                      pl.BlockSpec((3, tr, c), lambda i, s: (0, i, 0)), tile(), tile(), tile()],
            out_specs=(tile(), tile(), tile(), tile())),
        compiler_params=_params(("parallel",)),
    )(chip, p, recv, w, m, v)


def _all_to_all_copies(v_ref, land_ref, send_sems, recv_sems):
    mx, my, mc = _my_position()
    me = 4 * mx + 2 * my + mc
    copies = []
    for rel in range(1, N_DEV):
        bx, by, bc = (rel >> 2) & 1, (rel >> 1) & 1, rel & 1
        target = (1 - mx if bx else mx, 1 - my if by else my, 1 - mc if bc else mc)
        copies.append(pltpu.make_async_remote_copy(
            src_ref=v_ref, dst_ref=land_ref.at[me], send_sem=send_sems.at[rel - 1], recv_sem=recv_sems.at[rel - 1],
            device_id=target, device_id_type=MESH))
    return copies


def _small_all_reduce_start(v, after, name):
    return _split_start(v, (N_DEV,) + v.shape, _all_to_all_copies, N_DEV - 1, after, name + "_start")


def _small_all_reduce_finish(started, after, dev, name):
    v, land = _split_wait(started, after, _all_to_all_copies, name + "_wait")
    rows = v.shape[0]

    def body(me_ref, v_ref, land_ref, o_ref):
        for j in range(N_DEV):
            @pl.when(me_ref[0] == j)
            def _():
                o_ref[...] = v_ref[...] if j == 0 else o_ref[...] + v_ref[...]

            @pl.when(me_ref[0] != j)
            def _():
                o_ref[...] = land_ref[j] if j == 0 else o_ref[...] + land_ref[j]

    return pl.pallas_call(
        body, name=name + "_sum",
        out_shape=jax.ShapeDtypeStruct((rows, 128), F32),
        grid_spec=pltpu.PrefetchScalarGridSpec(
            num_scalar_prefetch=1, grid=(1,),
            in_specs=[pl.BlockSpec((rows, 128), lambda i, s: (0, 0)),
                      pl.BlockSpec((N_DEV, rows, 128), lambda i, s: (0, 0, 0))],
            out_specs=pl.BlockSpec((rows, 128), lambda i, s: (0, 0))),
        compiler_params=_params(("arbitrary",)),
    )(dev, v, land)


def _assemble_w_in(blocks):
    rows, d = WIN_ROWS, blocks.shape[2]
    tc = _divisor_tile(d, 256, 128)
    n_tiles = WIN_N // 128
    last = (N_DEV * WIN_STRIDE) // 128

    def body(b_ref, o_ref):
        win = []
        for i in range(N_DEV):
            w = jnp.concatenate([b_ref[i].astype(F32), jnp.zeros((WIN_BLOCK - rows, tc), F32)], axis=0)
            win.append(pltpu.roll(w, i, 0) if i else w)
        for t in range(n_tiles):
            if t > last:
                o_ref[t * 128:(t + 1) * 128, :] = jnp.zeros((128, tc), o_ref.dtype)
                continue
            i = min(t // 7, N_DEV - 1)
            k = t - 7 * i
            val = win[i][k * 128:(k + 1) * 128, :]
            if k == 0 and i >= 1:
                val = val + win[i - 1][7 * 128:8 * 128, :]
            o_ref[t * 128:(t + 1) * 128, :] = val.astype(o_ref.dtype)

    return pl.pallas_call(
        body, name="assemble_w_in",
        out_shape=jax.ShapeDtypeStruct((WIN_N, d), blocks.dtype),
        grid=(d // tc,),
        in_specs=[pl.BlockSpec((N_DEV, rows, tc), lambda j: (0, 0, j))],
        out_specs=pl.BlockSpec((WIN_N, tc), lambda j: (0, j)),
        compiler_params=_params(("parallel",)),
    )(blocks)


def _extract_w_in_windows(g):
    _, d = g.shape
    tc = _divisor_tile(d, 256, 128)

    def body(g_ref, o_ref):
        for j in range(N_DEV):
            w = g_ref[WIN_STRIDE * j:WIN_STRIDE * j + WIN_BLOCK, :].astype(F32)
            w = pltpu.roll(w, WIN_BLOCK - j, 0) if j else w
            o_ref[j] = w[0:WIN_ROWS, :].astype(o_ref.dtype)

    return pl.pallas_call(
        body, name="extract_w_in_windows",
        out_shape=jax.ShapeDtypeStruct((N_DEV, WIN_ROWS, d), g.dtype),
        grid=(d // tc,),
        in_specs=[pl.BlockSpec((WIN_N, tc), lambda j: (0, j))],
        out_specs=pl.BlockSpec((N_DEV, WIN_ROWS, tc), lambda j: (0, 0, j)),
        compiler_params=_params(("parallel",)),
    )(g)


def _mm(a, b, *, a_spec, b_spec, o_spec, out_shape, grid, contract, nk, name, after=None):
    dn = (((contract[0],), (contract[1],)), ((), ()))
    tm, tn = o_spec.block_shape[-2:]
    behind = [] if after is None else [after]

    def body(a_ref, b_ref, *rest):
        o_ref, *scratch = rest[len(behind):]
        part = lax.dot_general(a_ref[...], b_ref[...], dn, preferred_element_type=F32)
        if nk == 1:
            o_ref[...] = part.astype(o_ref.dtype)
            return
        acc = scratch[0]
        k = pl.program_id(2)

        @pl.when(k == 0)
        def _():
            acc[...] = part

        @pl.when(k > 0)
        def _():
            acc[...] += part

        @pl.when(k == nk - 1)
        def _():
            o_ref[...] = acc[...].astype(o_ref.dtype)

    return pl.pallas_call(
        body, name=name, out_shape=out_shape, grid=grid,
        in_specs=[a_spec, b_spec] + [ANY] * len(behind), out_specs=o_spec,
        scratch_shapes=[] if nk == 1 else [pltpu.VMEM((tm, tn), F32)],
        compiler_params=_params(("parallel", "parallel", "arbitrary")),
    )(a, b, *behind)


def _mm_nn(a, b, out_dtype, name, tm_cap=1088, tn_cap=512, tk_cap=2048, after=None):
    m, k = a.shape
    _, n = b.shape
    tm, tn, tk = _divisor_tile(m, tm_cap, 16), _divisor_tile(n, tn_cap, 128), _divisor_tile(k, tk_cap, 128)
    return _mm(a, b,
               a_spec=pl.BlockSpec((tm, tk), lambda i, j, kk: (i, kk)),
               b_spec=pl.BlockSpec((tk, tn), lambda i, j, kk: (kk, j)),
               o_spec=pl.BlockSpec((tm, tn), lambda i, j, kk: (i, j)),
               out_shape=jax.ShapeDtypeStruct((m, n), out_dtype),
               grid=(m // tm, n // tn, k // tk), contract=(1, 0), nk=k // tk, name=name, after=after)


def _mm_nt(a, b, out_dtype, name, tm_cap=1088, tn_cap=512, tk_cap=2048, after=None):
    m, k = a.shape
    n, _ = b.shape
    tm, tn, tk = _divisor_tile(m, tm_cap, 16), _divisor_tile(n, tn_cap, 128), _divisor_tile(k, tk_cap, 128)
    return _mm(a, b,
               a_spec=pl.BlockSpec((tm, tk), lambda i, j, kk: (i, kk)),
               b_spec=pl.BlockSpec((tn, tk), lambda i, j, kk: (j, kk)),
               o_spec=pl.BlockSpec((tm, tn), lambda i, j, kk: (i, j)),
               out_shape=jax.ShapeDtypeStruct((m, n), out_dtype),
               grid=(m // tm, n // tn, k // tk), contract=(1, 1), nk=k // tk, name=name, after=after)


def _mm_tn(a, b, out_dtype, name, tm_cap=1024, tn_cap=512, after=None):
    l, m = a.shape
    _, n = b.shape
    tm, tn = _divisor_tile(m, tm_cap, 128), _divisor_tile(n, tn_cap, 128)
    return _mm(a, b,
               a_spec=pl.BlockSpec((l, tm), lambda i, j, kk: (0, i)),
               b_spec=pl.BlockSpec((l, tn), lambda i, j, kk: (0, j)),
               o_spec=pl.BlockSpec((tm, tn), lambda i, j, kk: (i, j)),
               out_shape=jax.ShapeDtypeStruct((m, n), out_dtype),
               grid=(m // tm, n // tn, 1), contract=(0, 0), nk=1, name=name, after=after)


def _pair_split(shard):
    left = shard % ATTN_BLOCK
    assert left in (0, CHUNK) and shard > left
    return shard - left, left


def _mm_up(cn, w_up_blocks, after):
    l, d = cn.shape
    n, _, shard = w_up_blocks.shape
    main, left = _pair_split(shard)
    tm = _divisor_tile(l, 544, 16)

    def body(a_ref, b_ref, after_ref, o_ref):
        a = a_ref[...]
        for s in range(2):
            o_ref[:, s * shard:s * shard + main] = _dot(a, b_ref[s, :, 0:main])
        if left:
            tail = _dot(a, jnp.concatenate([b_ref[0, :, main:], b_ref[1, :, main:]], axis=1))
            o_ref[:, main:shard] = tail[:, 0:left]
            o_ref[:, shard + main:2 * shard] = tail[:, left:]

    return pl.pallas_call(
        body, name="mm_up", out_shape=jax.ShapeDtypeStruct((l, n * shard), F32), grid=(l // tm, n // 2),
        in_specs=[pl.BlockSpec((tm, d), lambda i, j: (i, 0)),
                  pl.BlockSpec((2, d, shard), lambda i, j: (j, 0, 0)), ANY],
        out_specs=pl.BlockSpec((tm, 2 * shard), lambda i, j: (i, j)),
        compiler_params=_params(("parallel", "parallel")),
    )(cn, w_up_blocks, after)


def _mm_gw_up(cn, d_u):
    l, d = cn.shape
    _, _, d_ff = d_u.shape
    shard = 2 * d_ff // N_DEV
    pairs_per_half = d_ff // (2 * shard)
    tm = _divisor_tile(d, 512, 128)

    def body(a_ref, b_ref, o_ref):
        res = _dot_tn(a_ref[...], b_ref[...])
        o_ref[0] = res[:, 0:shard].astype(o_ref.dtype)
        o_ref[1] = res[:, shard:].astype(o_ref.dtype)

    return pl.pallas_call(
        body, name="mm_gw_up", out_shape=jax.ShapeDtypeStruct((N_DEV, d, shard), WIRE_DTYPE),
        grid=(d // tm, N_DEV // 2),
        in_specs=[pl.BlockSpec((l, tm), lambda i, j: (0, i)),
                  pl.BlockSpec((None, l, 2 * shard), lambda i, j: (j // pairs_per_half, 0, j % pairs_per_half))],
        out_specs=pl.BlockSpec((2, tm, shard), lambda i, j: (j, i, 0)),
        compiler_params=_params(("parallel", "parallel")),
    )(cn, d_u)


def _mm_d_cn(d_u, w_up_blocks, after):
    _, l, d_ff = d_u.shape
    n, d, shard = w_up_blocks.shape
    per = d_ff // shard
    main, left = _pair_split(shard)
    tm, tn = _divisor_tile(l, 544, 16), _divisor_tile(d, 256, 128)

    def body(a_ref, b_ref, after_ref, o_ref):
        acc = None
        for k in range(0, n, 2):
            half, c0 = k // per, (k % per) * shard
            parts = [_dot_nt(a_ref[half, :, c0 + s * shard:c0 + s * shard + main], b_ref[k + s, :, 0:main])
                     for s in range(2)]
            if left:
                a_tail = jnp.concatenate([a_ref[half, :, c0 + s * shard + main:c0 + (s + 1) * shard] for s in range(2)],
                                         axis=1)
                b_tail = jnp.concatenate([b_ref[k + s, :, main:] for s in range(2)], axis=1)
                parts.append(_dot_nt(a_tail, b_tail))
            for part in parts:
                acc = part if acc is None else acc + part
        o_ref[...] = acc

    return pl.pallas_call(
        body, name="mm_d_cn", out_shape=jax.ShapeDtypeStruct((l, d), F32), grid=(l // tm, d // tn),
        in_specs=[pl.BlockSpec((2, tm, d_ff), lambda i, j: (0, i, 0)),
                  pl.BlockSpec((n, tn, shard), lambda i, j: (0, j, 0)), ANY],
        out_specs=pl.BlockSpec((tm, tn), lambda i, j: (i, j)),
        compiler_params=_params(("parallel", "parallel")),
    )(d_u, w_up_blocks, after)


def _row_tile(l):
    return _divisor_tile(l, 544, 8)


def _rms(x, gain):
    return (x * lax.rsqrt(jnp.mean(x * x, axis=-1, keepdims=True) + NORM_EPS) * gain).astype(MXU_DTYPE)


def _out_proj_resid_rmsnorm(mix, w_out, h0, gain):
    l, d = h0.shape
    tm = _divisor_tile(l, 272, 16)
    row = pl.BlockSpec((tm, d), lambda i: (i, 0))

    def body(a_ref, b_ref, h_ref, g_ref, s_ref, n_ref):
        x = h_ref[...] + _dot(a_ref[...], b_ref[...])
        s_ref[...] = x
        n_ref[...] = _rms(x, g_ref[...])

    return pl.pallas_call(
        body, name="mm_out_resid_rmsnorm2",
        out_shape=(jax.ShapeDtypeStruct((l, d), F32), jax.ShapeDtypeStruct((l, d), MXU_DTYPE)),
        grid=(l // tm,),
        in_specs=[row, pl.BlockSpec((d, d), lambda i: (0, 0)), row, pl.BlockSpec((1, d), lambda i: (0, 0))],
        out_specs=(row, row),
        compiler_params=_params(("parallel",)),
    )(mix, w_out, h0, gain)


def _rmsnorm_bwd(d_res, d_normed, x, gain, name, with_mxu_copy):
    l, d = x.shape
    tr = _row_tile(l) if with_mxu_copy else CHUNK
    row = pl.BlockSpec((tr, d), lambda i: (i, 0))
    vec = pl.BlockSpec((1, d), lambda i: (0, 0))

    def body(dres_ref, dn_ref, x_ref, g_ref, dx_ref, other_ref, dg_ref):
        i = pl.program_id(0)
        xv = x_ref[...]
        r = lax.rsqrt(jnp.mean(xv * xv, axis=-1, keepdims=True) + NORM_EPS)
        xh = xv * r
        dn = dn_ref[...]
        dxh = dn * g_ref[...]
        dx = dres_ref[...] + r * (dxh - xh * jnp.mean(dxh * xh, axis=-1, keepdims=True))
        if with_mxu_copy:
            dx_ref[...] = dx
            other_ref[...] = dx.astype(MXU_DTYPE)
        else:
            @pl.when(i == 0)
            def _():
                dx_ref[...] = dx

            @pl.when(i > 0)
            def _():
                other_ref[...] = dx

        @pl.when(i == 0)
        def _():
            dg_ref[...] = jnp.zeros_like(dg_ref)

        dg_ref[...] += jnp.sum(dn * xh, axis=0, keepdims=True)

    if with_mxu_copy:
        outs = [jax.ShapeDtypeStruct((l, d), F32), jax.ShapeDtypeStruct((l, d), MXU_DTYPE)]
        specs = [row, row]
    else:
        outs = [jax.ShapeDtypeStruct((CHUNK, d), F32), jax.ShapeDtypeStruct((l - CHUNK, d), F32)]
        specs = [pl.BlockSpec((CHUNK, d), lambda i: (0, 0)), pl.BlockSpec((CHUNK, d), lambda i: (jnp.maximum(i - 1, 0), 0))]
    outs.append(jax.ShapeDtypeStruct((1, d), F32))
    specs.append(vec)
    return pl.pallas_call(body, name=name, out_shape=tuple(outs), grid=(l // tr,),
                          in_specs=[row, row, row, vec], out_specs=tuple(specs),
                          compiler_params=_params(("arbitrary",)))(d_res, d_normed, x, gain)


def _loss_head(h1, mlp_out, gain, target):
    l, d = h1.shape
    n_blocks = l // CHUNK
    row = pl.BlockSpec((CHUNK, d), lambda i: (i, 0))
    vec = pl.BlockSpec((1, d), lambda i: (0, 0))
    tgt = pl.BlockSpec((CHUNK, d), lambda i: (jnp.maximum(i - 1, 0), 0))

    def body(h_ref, m_ref, g_ref, t_ref, dh_ref, dhb_ref, dg_ref, loss_ref, sq_ref):
        i = pl.program_id(0)
        x = h_ref[...] + m_ref[...]
        r = lax.rsqrt(jnp.mean(x * x, axis=-1, keepdims=True) + NORM_EPS)
        xh = x * r
        g = g_ref[...]
        real = i >= 1
        err = jnp.where(real, xh * g - t_ref[...], 0.0)
        dy = err * (1.0 / d)
        dxh = dy * g
        dh = r * (dxh - xh * jnp.mean(dxh * xh, axis=-1, keepdims=True))
        dh_ref[...] = dh
        dhb_ref[...] = dh.astype(MXU_DTYPE)

        @pl.when(i == 0)
        def _():
            dg_ref[...] = jnp.zeros_like(dg_ref)
            sq_ref[...] = jnp.zeros_like(sq_ref)

        dg_ref[...] += jnp.sum(dy * xh, axis=0, keepdims=True)
        sq_ref[...] += jnp.sum(err * err, axis=0, keepdims=True)

        @pl.when(i == n_blocks - 1)
        def _():
            total = jnp.sum(sq_ref[...], axis=-1, keepdims=True) * (0.5 / d)
            loss_ref[...] = jnp.broadcast_to(total, (1, 128))

    return pl.pallas_call(
        body, name="loss_head",
        out_shape=(jax.ShapeDtypeStruct((l, d), F32), jax.ShapeDtypeStruct((l, d), MXU_DTYPE),
                   jax.ShapeDtypeStruct((1, d), F32), jax.ShapeDtypeStruct((1, 128), F32)),
        grid=(n_blocks,), in_specs=[row, row, vec, tgt],
        out_specs=(row, row, vec, pl.BlockSpec((1, 128), lambda i: (0, 0))),
        scratch_shapes=[pltpu.VMEM((1, d), F32)],
        compiler_params=_params(("arbitrary",)),
    )(h1, mlp_out, gain, target)


def _dot(a, b):
    return jnp.dot(a, b, preferred_element_type=F32)


def _dot_nt(a, b):
    return lax.dot_general(a, b, (((1,), (1,)), ((), ())), preferred_element_type=F32)


def _dot_tn(a, b):
    return lax.dot_general(a, b, (((0,), (0,)), ((), ())), preferred_element_type=F32)


def _rope(t, cos2, sin2):
    return t * cos2 + pltpu.roll(t, HEAD_DIM // 2, 1) * sin2


def _rope_bwd(dr, cos2, sin2):
    return dr * cos2 + pltpu.roll(dr * sin2, HEAD_DIM // 2, 1)


def _sigmoid(x):
    return 1.0 / (1.0 + jnp.exp(-x))


def _row_valid(block, rows):
    r = block * CHUNK + lax.broadcasted_iota(jnp.int32, (rows, 1), 0)
    return r >= PAD_ROWS


def _retention_consts(l):
    pos = jnp.arange(l, dtype=F32) - PAD_ROWS
    inv_freq = 1.0 / (ROPE_BASE ** (jnp.arange(0, HEAD_DIM, 2, dtype=F32) / HEAD_DIM))
    ang = pos[:, None] * inv_freq[None, :]
    cos, sin = jnp.cos(ang), jnp.sin(ang)
    cos2 = jnp.concatenate([cos, cos], axis=-1)
    sin2 = jnp.concatenate([-sin, sin], axis=-1)
    log_g = jnp.log1p(-jnp.exp2(-5.0 - jnp.arange(N_HEADS, dtype=F32)))
    idx = jnp.arange(CHUNK, dtype=F32)
    diff = idx[:, None] - idx[None, :]
    decay = jnp.where(diff >= 0, jnp.exp(jnp.maximum(diff, 0.0)[None] * log_g[:, None, None]), 0.0)
    xi = jnp.exp((idx + 1.0)[None, :] * log_g[:, None])
    zeta = jnp.exp((CHUNK - 1.0 - idx)[None, :] * log_g[:, None])
    g_chunk = jnp.exp(CHUNK * log_g)
    bcast = lambda v: jnp.broadcast_to(v[:, :, None], (N_HEADS, CHUNK, HEAD_DIM))
    g_rows = jnp.broadcast_to(g_chunk[:, None, None], (N_HEADS, 8, HEAD_DIM))
    return cos2, sin2, decay, bcast(xi), bcast(zeta), g_rows


def _retention_fwd(proj, ret_gain, consts):
    l = proj.shape[0]
    n_chunks = l // CHUNK
    cos2, sin2, decay, xi, zeta, g_rows = consts
    scale = HEAD_DIM ** -0.5

    def body(p_ref, cos_ref, sin_ref, dec_ref, xi_ref, zeta_ref, gr_ref, gain_ref,
             mix_ref, o_ref, st_ref, state):
        c = pl.program_id(0)

        @pl.when(c == 0)
        def _():
            state[...] = jnp.zeros_like(state)

        cos_v, sin_v = cos_ref[...], sin_ref[...]
        valid = _row_valid(c, CHUNK)
        for h in range(N_HEADS):
            cols = slice(h * HEAD_DIM, (h + 1) * HEAD_DIM)
            q = p_ref[:, h * HEAD_DIM:(h + 1) * HEAD_DIM]
            k = p_ref[:, GROUP + h * HEAD_DIM:GROUP + (h + 1) * HEAD_DIM]
            v = p_ref[:, 2 * GROUP + h * HEAD_DIM:2 * GROUP + (h + 1) * HEAD_DIM]
            g = p_ref[:, 3 * GROUP + h * HEAD_DIM:3 * GROUP + (h + 1) * HEAD_DIM]
            rq = _rope(q, cos_v, sin_v).astype(MXU_DTYPE)
            rk = _rope(k, cos_v, sin_v) * scale
            rkb = rk.astype(MXU_DTYPE)
            vb = v.astype(MXU_DTYPE)
            st = state[h]
            st_ref[h] = st
            s = _dot_nt(rq, rkb) * dec_ref[h]
            o = _dot(s.astype(MXU_DTYPE), vb) + _dot(rq, st.astype(MXU_DTYPE)) * xi_ref[h]
            kz = (rk * zeta_ref[h]).astype(MXU_DTYPE)
            state[h] = gr_ref[h, 0:1, :] * st + _dot_tn(kz, vb)
            o_ref[:, cols] = o
            mu = jnp.mean(o, axis=-1, keepdims=True)
            oc = o - mu
            yn = oc * lax.rsqrt(jnp.mean(oc * oc, axis=-1, keepdims=True) + NORM_EPS)
            ret = (g * _sigmoid(g)) * (yn * gain_ref[:, cols])
            mix_ref[:, cols] = jnp.where(valid, ret, 0.0).astype(mix_ref.dtype)

    head_tab = pl.BlockSpec((N_HEADS, CHUNK, HEAD_DIM), lambda c: (0, 0, 0))
    return pl.pallas_call(
        body, name="retention_fwd",
        out_shape=(jax.ShapeDtypeStruct((l, 2 * GROUP), MXU_DTYPE), jax.ShapeDtypeStruct((l, GROUP), F32),
                   jax.ShapeDtypeStruct((n_chunks, N_HEADS, HEAD_DIM, HEAD_DIM), F32)),
        grid=(n_chunks,),
        in_specs=[pl.BlockSpec((CHUNK, 4 * GROUP), lambda c: (c, 0)),
                  pl.BlockSpec((CHUNK, HEAD_DIM), lambda c: (c, 0)),
                  pl.BlockSpec((CHUNK, HEAD_DIM), lambda c: (c, 0)),
                  head_tab, head_tab, head_tab,
                  pl.BlockSpec((N_HEADS, 8, HEAD_DIM), lambda c: (0, 0, 0)),
                  pl.BlockSpec((1, GROUP), lambda c: (0, 0))],
        out_specs=(pl.BlockSpec((CHUNK, GROUP), lambda c: (c, 0)),
                   pl.BlockSpec((CHUNK, GROUP), lambda c: (c, 0)),
                   pl.BlockSpec((None, N_HEADS, HEAD_DIM, HEAD_DIM), lambda c: (c, 0, 0, 0))),
        scratch_shapes=[pltpu.VMEM((N_HEADS, HEAD_DIM, HEAD_DIM), F32)],
        compiler_params=_params(("arbitrary",)),
    )(proj, cos2, sin2, decay, xi, zeta, g_rows, ret_gain)


def _retention_bwd(proj, o_pre, states, d_mix, ret_gain, consts):
    l = proj.shape[0]
    n_chunks = l // CHUNK
    cos2, sin2, decay, xi, zeta, g_rows = consts
    scale = HEAD_DIM ** -0.5
    rev = lambda c: n_chunks - 1 - c

    def body(p_ref, o_ref, st_ref, dm_ref, cos_ref, sin_ref, dec_ref, dect_ref, xi_ref, zeta_ref, gr_ref, gain_ref,
             dp_ref, dgain_ref, dstate):
        step = pl.program_id(0)

        @pl.when(step == 0)
        def _():
            dstate[...] = jnp.zeros_like(dstate)
            dgain_ref[...] = jnp.zeros_like(dgain_ref)

        cos_v, sin_v = cos_ref[...], sin_ref[...]
        valid = _row_valid(rev(step), CHUNK)
        for h in range(N_HEADS):
            cols = slice(h * HEAD_DIM, (h + 1) * HEAD_DIM)
            q = p_ref[:, h * HEAD_DIM:(h + 1) * HEAD_DIM]
            k = p_ref[:, GROUP + h * HEAD_DIM:GROUP + (h + 1) * HEAD_DIM]
            v = p_ref[:, 2 * GROUP + h * HEAD_DIM:2 * GROUP + (h + 1) * HEAD_DIM]
            g = p_ref[:, 3 * GROUP + h * HEAD_DIM:3 * GROUP + (h + 1) * HEAD_DIM]
            o = o_ref[:, cols]
            gain = gain_ref[:, cols]
            d_ret = jnp.where(valid, dm_ref[:, cols], 0.0)
            mu = jnp.mean(o, axis=-1, keepdims=True)
            oc = o - mu
            rstd = lax.rsqrt(jnp.mean(oc * oc, axis=-1, keepdims=True) + NORM_EPS)
            yn = oc * rstd
            sig = _sigmoid(g)
            gate = g * sig
            dgain_ref[:, cols] += jnp.sum(d_ret * gate * yn, axis=0, keepdims=True)
            d_g = d_ret * (yn * gain) * (sig * (1.0 + g * (1.0 - sig)))
            d_yn = d_ret * gate * gain
            d_o = rstd * (d_yn - jnp.mean(d_yn, axis=-1, keepdims=True)
                          - yn * jnp.mean(d_yn * yn, axis=-1, keepdims=True))
            rq = _rope(q, cos_v, sin_v)
            rk = _rope(k, cos_v, sin_v) * scale
            rqb, rkb, vb = rq.astype(MXU_DTYPE), rk.astype(MXU_DTYPE), v.astype(MXU_DTYPE)
            dob = d_o.astype(MXU_DTYPE)
            dec = dec_ref[h]
            xi_h, zeta_h = xi_ref[h], zeta_ref[h]
            st_b = st_ref[h].astype(MXU_DTYPE)
            dst = dstate[h]
            dst_b = dst.astype(MXU_DTYPE)
            dec_t = dect_ref[h]
            s_t_b = (_dot_nt(rkb, rqb) * dec_t).astype(MXU_DTYPE)
            da_b = (_dot_nt(dob, vb) * dec).astype(MXU_DTYPE)
            da_t_b = (_dot_nt(vb, dob) * dec_t).astype(MXU_DTYPE)
            doxi_b = (d_o * xi_h).astype(MXU_DTYPE)
            kz_b = (rk * zeta_h).astype(MXU_DTYPE)
            d_rq = _dot(da_b, rkb) + _dot_nt(doxi_b, st_b)
            d_rk = _dot(da_t_b, rqb) + _dot_nt(vb, dst_b) * zeta_h
            d_v = _dot(s_t_b, dob) + _dot(kz_b, dst_b)
            dstate[h] = gr_ref[h, 0:1, :] * dst + _dot_tn(rqb, doxi_b)
            d_q = _rope_bwd(d_rq, cos_v, sin_v)
            d_k = _rope_bwd(d_rk * scale, cos_v, sin_v)
            dp_ref[:, h * HEAD_DIM:(h + 1) * HEAD_DIM] = d_q.astype(dp_ref.dtype)
            dp_ref[:, GROUP + h * HEAD_DIM:GROUP + (h + 1) * HEAD_DIM] = d_k.astype(dp_ref.dtype)
            dp_ref[:, 2 * GROUP + h * HEAD_DIM:2 * GROUP + (h + 1) * HEAD_DIM] = d_v.astype(dp_ref.dtype)
            dp_ref[:, 3 * GROUP + h * HEAD_DIM:3 * GROUP + (h + 1) * HEAD_DIM] = d_g.astype(dp_ref.dtype)

    head_tab = pl.BlockSpec((N_HEADS, CHUNK, HEAD_DIM), lambda c: (0, 0, 0))
    return pl.pallas_call(
        body, name="retention_bwd",
        out_shape=(jax.ShapeDtypeStruct((l, 4 * GROUP), MXU_DTYPE), jax.ShapeDtypeStruct((1, GROUP), F32)),
        grid=(n_chunks,),
        in_specs=[pl.BlockSpec((CHUNK, 4 * GROUP), lambda c: (rev(c), 0)),
                  pl.BlockSpec((CHUNK, GROUP), lambda c: (rev(c), 0)),
                  pl.BlockSpec((None, N_HEADS, HEAD_DIM, HEAD_DIM), lambda c: (rev(c), 0, 0, 0)),
                  pl.BlockSpec((CHUNK, GROUP), lambda c: (rev(c), 0)),
                  pl.BlockSpec((CHUNK, HEAD_DIM), lambda c: (rev(c), 0)),
                  pl.BlockSpec((CHUNK, HEAD_DIM), lambda c: (rev(c), 0)),
                  head_tab, head_tab, head_tab, head_tab,
                  pl.BlockSpec((N_HEADS, 8, HEAD_DIM), lambda c: (0, 0, 0)),
                  pl.BlockSpec((1, GROUP), lambda c: (0, 0))],
        out_specs=(pl.BlockSpec((CHUNK, 4 * GROUP), lambda c: (rev(c), 0)),
                   pl.BlockSpec((1, GROUP), lambda c: (0, 0))),
        scratch_shapes=[pltpu.VMEM((N_HEADS, HEAD_DIM, HEAD_DIM), F32)],
        compiler_params=_params(("arbitrary",)),
    )(proj, o_pre, states, d_mix, cos2, sin2, decay, jnp.transpose(decay, (0, 2, 1)), xi, zeta, g_rows, ret_gain)


FF_TILE = (7 * GROUP) // 128


def _log_forget(ff, bias_row, valid):
    x = ff + bias_row
    e = jnp.exp(-jnp.abs(x))
    lf = jnp.minimum(x, 0.0) - jnp.log(1.0 + e)
    head_lane = lax.broadcasted_iota(jnp.int32, x.shape, 1) < N_HEADS
    keep = lambda t: jnp.where(head_lane, jnp.where(valid, t, 0.0), 0.0)
    return keep(lf), keep(jnp.where(x >= 0, e, 1.0) / (1.0 + e))


def _fox_prep(proj, bias_row):
    l = proj.shape[0]
    n_blocks = l // CHUNK

    def body(ff_ref, b_ref, bc_ref, rows_ref, cum):
        r = lax.broadcasted_iota(jnp.int32, (CHUNK, CHUNK), 0)
        cidx = lax.broadcasted_iota(jnp.int32, (CHUNK, CHUNK), 1)
        tri = jnp.where(r >= cidx, 1.0, 0.0).astype(F32)
        carry = jnp.zeros((1, 128), F32)
        for blk in range(n_blocks):
            rows = slice(blk * CHUNK, (blk + 1) * CHUNK)
            valid = _row_valid(blk, CHUNK)
            lf, _ = _log_forget(ff_ref[rows, :], b_ref[...], valid)
            local = jnp.dot(tri, lf, precision=lax.Precision.HIGHEST, preferred_element_type=F32) + carry
            carry = local[CHUNK - 1:CHUNK, :]
            masked = jnp.where(valid, local, -NEG_BIG)
            cum[rows, :] = masked
            t = masked.T
            for h in range(N_HEADS):
                rows_ref[h, :, rows] = t[h:h + 1, :]
        full = cum[...]
        for h in range(N_HEADS):
            bc_ref[h] = jnp.broadcast_to(full[:, h:h + 1], (l, 128))

    return pl.pallas_call(
        body, name="fox_prep",
        out_shape=(jax.ShapeDtypeStruct((N_HEADS, l, 128), F32), jax.ShapeDtypeStruct((N_HEADS, 1, l), F32)),
        grid=(1,),
        in_specs=[pl.BlockSpec((l, 128), lambda i: (0, FF_TILE)), pl.BlockSpec((1, 128), lambda i: (0, 0))],
        out_specs=(pl.BlockSpec((N_HEADS, l, 128), lambda i: (0, 0, 0)),
                   pl.BlockSpec((N_HEADS, 1, l), lambda i: (0, 0, 0))),
        scratch_shapes=[pltpu.VMEM((l, 128), F32)],
        compiler_params=_params(("arbitrary",)),
    )(proj, bias_row)


ATTN_BLOCK = 2 * CHUNK


def _attn_blocks(l):
    assert (l - CHUNK) % ATTN_BLOCK == 0
    return [(0, CHUNK)] + [(s, ATTN_BLOCK) for s in range(CHUNK, l, ATTN_BLOCK)]


def _rows_valid(start, size):
    return start + lax.broadcasted_iota(jnp.int32, (size, 1), 0) >= PAD_ROWS


def _fox_fwd(proj, cum_bc, cum_rows, mix):
    l = proj.shape[0]
    blocks = _attn_blocks(l)
    scale = HEAD_DIM ** -0.5
    qt, kt, vt = 4 * N_HEADS, 5 * N_HEADS, 6 * N_HEADS

    def body(q_ref, k_ref, v_ref, cbc_ref, crow_ref, mix_in, o_ref, lse_ref, qb_s, kb_s, vb_s):
        qb_s[...] = q_ref[...].astype(MXU_DTYPE)
        kb_s[...] = k_ref[...].astype(MXU_DTYPE)
        vb_s[...] = v_ref[...].astype(MXU_DTYPE)
        for p, (qs, qn) in enumerate(blocks):
            qb = qb_s[qs:qs + qn, :]
            cq = cbc_ref[qs:qs + qn, :]
            m = jnp.full((qn, 1), NEG_BIG, F32)
            lsum = jnp.zeros((qn, 1), F32)
            acc = jnp.zeros((qn, HEAD_DIM), F32)
            for j in range(p + 1):
                ks, kn = blocks[j]
                bias = jnp.tile(cq, (1, kn // CHUNK)) - crow_ref[:, ks:ks + kn]
                s = _dot_nt(qb, kb_s[ks:ks + kn, :]) * scale + bias
                if j == p:
                    q_pos = qs + lax.broadcasted_iota(jnp.int32, (qn, kn), 0)
                    k_pos = ks + lax.broadcasted_iota(jnp.int32, (qn, kn), 1)
                    s = jnp.where(k_pos <= q_pos, s, NEG_BIG)
                m_new = jnp.maximum(m, jnp.max(s, axis=-1, keepdims=True))
                alpha = jnp.exp(m - m_new)
                pr = jnp.exp(s - m_new)
                lsum = lsum * alpha + jnp.sum(pr, axis=-1, keepdims=True)
                acc = acc * alpha + _dot(pr.astype(MXU_DTYPE), vb_s[ks:ks + kn, :])
                m = m_new
            o = jnp.where(_rows_valid(qs, qn), acc * (1.0 / lsum), 0.0)
            o_ref[qs:qs + qn, :] = o.astype(o_ref.dtype)
            lse = m + jnp.log(lsum)
            lse_ref[:, qs:qs + qn] = jnp.broadcast_to(lse, (qn, CHUNK)).T[0:1, :]

    head_col = lambda t: pl.BlockSpec((l, HEAD_DIM), lambda h: (0, t + h))
    return pl.pallas_call(
        body, name="fox_fwd",
        out_shape=(jax.ShapeDtypeStruct(mix.shape, mix.dtype), jax.ShapeDtypeStruct((N_HEADS, 1, l), F32)),
        grid=(N_HEADS,),
        in_specs=[head_col(qt), head_col(kt), head_col(vt),
                  pl.BlockSpec((None, l, 128), lambda h: (h, 0, 0)),
                  pl.BlockSpec((None, 1, l), lambda h: (h, 0, 0)),
                  ANY],
        out_specs=(head_col(N_HEADS), pl.BlockSpec((None, 1, l), lambda h: (h, 0, 0))),
        input_output_aliases={5: 0},
        scratch_shapes=[pltpu.VMEM((l, HEAD_DIM), MXU_DTYPE)] * 3,
        compiler_params=_params(("parallel",)),
    )(proj, proj, proj, cum_bc, cum_rows, mix)


def _fox_bwd(proj, cum_bc, cum_rows, d_mix, lse_rows):
    l = proj.shape[0]
    blocks = _attn_blocks(l)
    scale = HEAD_DIM ** -0.5
    qt, kt, vt = 4 * N_HEADS, 5 * N_HEADS, 6 * N_HEADS

    def body(q_ref, k_ref, v_ref, do_ref, cbc_ref, crow_ref, lse_ref,
             dq_ref, dk_ref, dv_ref, ds_ref, dk_acc, dv_acc, qb_s, kb_s, vb_s, dob_s, p_s, dp_s):
        qb_s[...] = q_ref[...].astype(MXU_DTYPE)
        kb_s[...] = k_ref[...].astype(MXU_DTYPE)
        vb_s[...] = v_ref[...].astype(MXU_DTYPE)
        dob_s[...] = jnp.where(_rows_valid(0, l), do_ref[...], 0.0).astype(MXU_DTYPE)
        dk_acc[...] = jnp.zeros_like(dk_acc)
        dv_acc[...] = jnp.zeros_like(dv_acc)
        ds_ref[...] = jnp.zeros_like(ds_ref)
        shift_row = crow_ref[...] - lse_ref[...]

        for p, (qs, qn) in enumerate(blocks):
            qb, dob = qb_s[qs:qs + qn, :], dob_s[qs:qs + qn, :]
            shift = shift_row[:, qs:qs + qn]

            delta = jnp.zeros((1, qn), F32)
            for j in range(p + 1):
                ks, kn = blocks[j]
                ck = jnp.tile(cbc_ref[ks:ks + kn, :], (1, qn // CHUNK))
                s_t = _dot_nt(kb_s[ks:ks + kn, :], qb) * scale + (shift - ck)
                if j == p:
                    k_pos = ks + lax.broadcasted_iota(jnp.int32, (kn, qn), 0)
                    q_pos = qs + lax.broadcasted_iota(jnp.int32, (kn, qn), 1)
                    s_t = jnp.where(k_pos <= q_pos, s_t, NEG_BIG)
                p_t, dp_t = jnp.exp(s_t), _dot_nt(vb_s[ks:ks + kn, :], dob)
                p_s[j, 0:kn, 0:qn] = p_t
                dp_s[j, 0:kn, 0:qn] = dp_t
                delta = delta + jnp.sum(p_t * dp_t, axis=0, keepdims=True)
            dq = jnp.zeros((qn, HEAD_DIM), F32)
            for j in range(p + 1):
                ks, kn = blocks[j]
                rows = slice(ks, ks + kn)
                p_t, dp_t = p_s[j, 0:kn, 0:qn], dp_s[j, 0:kn, 0:qn]
                ds_t = p_t * (dp_t - delta)
                ds_b = ds_t.astype(MXU_DTYPE)
                dv_acc[rows, :] += _dot(p_t.astype(MXU_DTYPE), dob)
                dk_acc[rows, :] += _dot(ds_b, qb) * scale
                ds_ref[rows, :] += sum(ds_t[:, c:c + CHUNK] for c in range(0, qn, CHUNK))
                dq = dq + _dot_tn(ds_b, kb_s[rows, :])
            dq_ref[qs:qs + qn, :] = (dq * scale).astype(dq_ref.dtype)

        dk_ref[...] = dk_acc[...].astype(dk_ref.dtype)
        dv_ref[...] = dv_acc[...].astype(dv_ref.dtype)

    col = jax.ShapeDtypeStruct((l, GROUP), MXU_DTYPE)
    head_col = lambda t: pl.BlockSpec((l, HEAD_DIM), lambda h: (0, t + h))
    return pl.pallas_call(
        body, name="fox_bwd",
        out_shape=(col, col, col, jax.ShapeDtypeStruct((N_HEADS, l, 128), F32)),
        grid=(N_HEADS,),
        in_specs=[head_col(qt), head_col(kt), head_col(vt), head_col(N_HEADS),
                  pl.BlockSpec((None, l, 128), lambda h: (h, 0, 0)),
                  pl.BlockSpec((None, 1, l), lambda h: (h, 0, 0)),
                  pl.BlockSpec((None, 1, l), lambda h: (h, 0, 0))],
        out_specs=(head_col(0), head_col(0), head_col(0), pl.BlockSpec((None, l, 128), lambda h: (h, 0, 0))),
        scratch_shapes=([pltpu.VMEM((l, HEAD_DIM), F32)] * 2 + [pltpu.VMEM((l, HEAD_DIM), MXU_DTYPE)] * 4
                        + [pltpu.VMEM((len(blocks), ATTN_BLOCK, ATTN_BLOCK), F32)] * 2),
        compiler_params=_params(("parallel",)),
    )(proj, proj, proj, d_mix, cum_bc, cum_rows, lse_rows)


def _fox_gate_bwd(ds_sum, proj, bias_row):
    l = proj.shape[0]
    n_blocks = l // CHUNK

    def body(ds_ref, ff_ref, b_ref, dff_ref, db_ref):
        r = lax.broadcasted_iota(jnp.int32, (CHUNK, CHUNK), 0)
        cidx = lax.broadcasted_iota(jnp.int32, (CHUNK, CHUNK), 1)
        upper = jnp.where(cidx >= r, 1.0, 0.0).astype(F32)
        carry = jnp.zeros((1, 128), F32)
        db = jnp.zeros((1, 128), F32)
        for blk in reversed(range(n_blocks)):
            rows = slice(blk * CHUNK, (blk + 1) * CHUNK)
            key_sum = jnp.zeros((CHUNK, 128), F32)
            for h in range(N_HEADS):
                select = jnp.where(cidx == h, 1.0, 0.0).astype(F32)
                key_sum = key_sum + jnp.dot(ds_ref[h, rows, :], select, precision=lax.Precision.HIGHEST,
                                            preferred_element_type=F32)
            suffix = jnp.dot(upper, key_sum, precision=lax.Precision.HIGHEST, preferred_element_type=F32) + carry
            carry = suffix[0:1, :]
            _, dsig = _log_forget(ff_ref[rows, :], b_ref[...], _row_valid(blk, CHUNK))
            dff = -suffix * dsig
            dff_ref[rows, :] = dff.astype(dff_ref.dtype)
            db = db + jnp.sum(dff, axis=0, keepdims=True)
        db_ref[...] = db

    return pl.pallas_call(
        body, name="fox_gate_bwd",
        out_shape=(jax.ShapeDtypeStruct((l, 128), MXU_DTYPE), jax.ShapeDtypeStruct((1, 128), F32)),
        grid=(1,),
        in_specs=[pl.BlockSpec((N_HEADS, l, 128), lambda i: (0, 0, 0)),
                  pl.BlockSpec((l, 128), lambda i: (0, FF_TILE)),
                  pl.BlockSpec((1, 128), lambda i: (0, 0))],
        out_specs=(pl.BlockSpec((l, 128), lambda i: (0, 0)), pl.BlockSpec((1, 128), lambda i: (0, 0))),
        compiler_params=_params(("arbitrary",)),
    )(ds_sum, proj, bias_row)


def _conv(u, w, b):
    return b + w[0:1, :] * pltpu.roll(u, 2, 0) + w[1:2, :] * pltpu.roll(u, 1, 0) + w[2:3, :] * u


def _conv_act_fwd(u, conv_w, conv_b, d_ff):
    l = u.shape[0]
    tc = _divisor_tile(d_ff, 256, 128)
    nt = d_ff // tc

    def body(ug_ref, uv_ref, wg_ref, wv_ref, bg_ref, bv_ref, a_ref, y_ref):
        yg = _conv(ug_ref[...], wg_ref[...], bg_ref[...])
        yv = _conv(uv_ref[...], wv_ref[...], bv_ref[...])
        act = yg * _sigmoid(yg) * yv
        a_ref[...] = jnp.where(_row_valid(0, l), act, 0.0).astype(a_ref.dtype)
        y_ref[0] = yg.astype(y_ref.dtype)
        y_ref[1] = yv.astype(y_ref.dtype)

    return pl.pallas_call(
        body, name="conv_act_fwd",
        out_shape=(jax.ShapeDtypeStruct((l, d_ff), MXU_DTYPE), jax.ShapeDtypeStruct((2, l, d_ff), MXU_DTYPE)),
        grid=(nt,),
        in_specs=[pl.BlockSpec((l, tc), lambda j: (0, j)), pl.BlockSpec((l, tc), lambda j: (0, j + nt)),
                  pl.BlockSpec((8, tc), lambda j: (0, j)), pl.BlockSpec((8, tc), lambda j: (0, j + nt)),
                  pl.BlockSpec((1, tc), lambda j: (0, j)), pl.BlockSpec((1, tc), lambda j: (0, j + nt))],
        out_specs=(pl.BlockSpec((l, tc), lambda j: (0, j)), pl.BlockSpec((2, l, tc), lambda j: (0, 0, j))),
        compiler_params=_params(("parallel",)),
    )(u, u, conv_w, conv_w, conv_b, conv_b)


def _conv_act_bwd(u, y, conv_w, d_act, d_ff):
    l = u.shape[0]
    tc = _divisor_tile(d_ff, 256, 128)
    nt = d_ff // tc

    def body(ug_ref, uv_ref, y_ref, wg_ref, wv_ref, da_ref, du_ref, dwb_ref):
        valid = _row_valid(0, l)
        ug, uv = ug_ref[...], uv_ref[...]
        wg, wv = wg_ref[...], wv_ref[...]
        yg, yv = y_ref[0].astype(F32), y_ref[1].astype(F32)
        sig = _sigmoid(yg)
        da = jnp.where(valid, da_ref[...], 0.0)
        d_yv = da * (yg * sig)
        d_yg = da * yv * (sig * (1.0 + yg * (1.0 - sig)))
        for idx, (dy, uu, w) in enumerate(((d_yg, ug, wg), (d_yv, uv, wv))):
            du = w[2:3, :] * dy + w[1:2, :] * pltpu.roll(dy, l - 1, 0) + w[0:1, :] * pltpu.roll(dy, l - 2, 0)
            du_ref[idx] = jnp.where(valid, du, 0.0).astype(du_ref.dtype)
            dwb_ref[idx, 0:1, :] = jnp.sum(dy * pltpu.roll(uu, 2, 0), axis=0, keepdims=True)
            dwb_ref[idx, 1:2, :] = jnp.sum(dy * pltpu.roll(uu, 1, 0), axis=0, keepdims=True)
            dwb_ref[idx, 2:3, :] = jnp.sum(dy * uu, axis=0, keepdims=True)
            dwb_ref[idx, 3:4, :] = jnp.sum(dy, axis=0, keepdims=True)
            dwb_ref[idx, 4:8, :] = jnp.zeros((4, tc), F32)

    return pl.pallas_call(
        body, name="conv_act_bwd",
        out_shape=(jax.ShapeDtypeStruct((2, l, d_ff), MXU_DTYPE), jax.ShapeDtypeStruct((2, 8, d_ff), F32)),
        grid=(nt,),
        in_specs=[pl.BlockSpec((l, tc), lambda j: (0, j)), pl.BlockSpec((l, tc), lambda j: (0, j + nt)),
                  pl.BlockSpec((2, l, tc), lambda j: (0, 0, j)),
                  pl.BlockSpec((8, tc), lambda j: (0, j)), pl.BlockSpec((8, tc), lambda j: (0, j + nt)),
                  pl.BlockSpec((l, tc), lambda j: (0, j))],
        out_specs=(pl.BlockSpec((2, l, tc), lambda j: (0, 0, j)), pl.BlockSpec((2, 8, tc), lambda j: (0, 0, j))),
        compiler_params=_params(("parallel",)),
    )(u, u, y, conv_w, conv_w, d_act)


def _adamw(w, g, m, v, name):
    shape = w.shape
    if w.ndim == 1:
        as2d = (1, shape[0])
    else:
        as2d = (int(np.prod(shape[:-1])), shape[-1])
    r, c = as2d
    tr = _divisor_tile(r, 256, 8)
    spec = pl.BlockSpec((tr, c), lambda i: (i, 0))

    def body(w_ref, g_ref, m_ref, v_ref, d_ref, nm_ref, nv_ref):
        d_ref[...], nm_ref[...], nv_ref[...] = _adamw_math(w_ref[...], g_ref[...], m_ref[...], v_ref[...])

    sds = jax.ShapeDtypeStruct(as2d, F32)
    outs = pl.pallas_call(
        body, name=name, out_shape=(sds, sds, sds), grid=(r // tr,),
        in_specs=[spec] * 4, out_specs=(spec,) * 3,
        compiler_params=_params(("parallel",)),
    )(w.reshape(as2d), g.reshape(as2d), m.reshape(as2d), v.reshape(as2d))
    return tuple(o.reshape(shape) for o in outs)


def _pad_rows(a, rows):
    return jnp.pad(a, ((0, rows - a.shape[0]), (0, 0)))


def kernel(x, meta_tokens, norm1_gain, w_in, b_forget, ret_norm_gain, w_out, norm2_gain, w_up, conv_w, conv_b, w_down, final_norm_gain, loss_target, m_meta_tokens, m_norm1_gain, m_w_in, m_b_forget, m_ret_norm_gain, m_w_out, m_norm2_gain, m_w_up, m_conv_w, m_conv_b, m_w_down, m_final_norm_gain, v_meta_tokens, v_norm1_gain, v_w_in, v_b_forget, v_ret_norm_gain, v_w_out, v_norm2_gain, v_w_up, v_conv_w, v_conv_b, v_w_down, v_final_norm_gain):
    seq, d = x.shape[1], x.shape[2]
    l = CHUNK + seq
    d_ff = w_down.shape[1] * N_DEV
    up_shard = w_up.shape[2]
    assert 4 * up_shard == d_ff and w_in.shape[2] == WIN_SHARD and d == 2 * GROUP
    dev = _device_index()
    mx, my, mc = _my_position()
    core = jnp.reshape(mc, (1,)).astype(jnp.int32)
    chip = jnp.reshape(2 * mx + my, (1,)).astype(jnp.int32)
    dev1 = jnp.reshape(dev, (1,)).astype(jnp.int32)

    small = jnp.concatenate([meta_tokens.reshape(-1, 128), conv_w[0].reshape(-1, 128)], axis=0)
    n_meta_rows = N_META * (d // N_DEV) // 128
    small_rows = small.shape[0]
    to_rows = lambda t: jnp.pad(jnp.transpose(t[0]), ((0, WIN_ROWS - WIN_SHARD), (0, 0)))
    from_rows = lambda t: jnp.transpose(t[:WIN_SHARD])[None]
    w_in_rows = to_rows(w_in)
    out_rows = d // N_DEV
    mixer_rows = -(-(WIN_ROWS + out_rows) // 32) * 32
    mixer_shard = jnp.concatenate([w_in_rows.astype(WIRE_DTYPE), w_out[0].astype(WIRE_DTYPE),
                                   jnp.zeros((mixer_rows - WIN_ROWS - out_rows, d), WIRE_DTYPE)], axis=0)

    consts = _retention_consts(l)
    bias_row = jnp.pad(b_forget, ((0, 0), (0, 128 - N_HEADS)))
    mixer_blocks, small_all, h0, a, up_land, down_land = _gather_ring_with_norm(
        mixer_shard, _pad_rows(small, -(-small_rows // 8) * 8), dev1, x[0], norm1_gain, [w_up[0], w_down[0]],
        "gather_w_in")
    start_up = _gather_start(up_land, mixer_blocks, "gather_w_up_start")
    meta_full = jnp.transpose(small_all[:, :n_meta_rows].reshape(N_DEV, N_META, d // N_DEV), (1, 0, 2)).reshape(N_META, d)
    conv_w_full = _pad_rows(jnp.transpose(small_all[:, n_meta_rows:small_rows].reshape(N_DEV, 3, up_shard),
                                          (1, 0, 2)).reshape(3, 2 * d_ff), 8)
    h0, a = _embed_front(h0, a, meta_full, norm1_gain)
    w_in_full = _assemble_w_in(mixer_blocks).astype(MXU_DTYPE)
    proj = _mm_nt(a, w_in_full, F32, "mm_proj", after=start_up[4])
    ret_mix, ret_pre, ret_states = _retention_fwd(proj, ret_norm_gain, consts)
    cum_bc, cum_rows = _fox_prep(proj, bias_row)
    mix, lse_rows = _fox_fwd(proj, cum_bc, cum_rows, ret_mix)
    w_out_full = mixer_blocks[:, WIN_ROWS:WIN_ROWS + out_rows].reshape(d, d).astype(MXU_DTYPE)
    h1, cn = _out_proj_resid_rmsnorm(mix, w_out_full, h0, norm2_gain)
    w_up_blocks = _gather_finish(start_up, cn, "gather_w_up").astype(MXU_DTYPE)
    start_down = _gather_start(down_land, w_up_blocks, "gather_w_down_start")
    u = _mm_up(cn, w_up_blocks, start_down[4])
    pass_down = _gather_pass_start(start_down, u, "gather_w_down")
    act, conv_y = _conv_act_fwd(u, conv_w_full, conv_b + pass_down[4][0, 0], d_ff)
    w_down_full = _gather_pass_finish(pass_down, act, "gather_w_down").reshape(d_ff, d).astype(MXU_DTYPE)
    mlp_out = _mm_nn(act, w_down_full, F32, "mm_down", tm_cap=544, tk_cap=d_ff)
    d_h2, d_h2_b, dg_final, loss_part = _loss_head(h1, mlp_out, final_norm_gain.reshape(1, d), loss_target[0])

    gw_down = _mm_tn(act, d_h2_b, WIRE_DTYPE, "mm_gw_down", tm_cap=1408, tn_cap=1024)
    d2d_down = _reduce_scatter_d2d_start(gw_down.reshape(N_DEV, d_ff // N_DEV, d), d_h2, "rs_w_down")
    d_act = _mm_nt(d_h2_b, w_down_full, F32, "mm_d_act", after=d2d_down[4])
    rs_down = _reduce_scatter_ici_start(d2d_down, d_act, core, "rs_w_down")
    d_u, d_conv = _conv_act_bwd(u, conv_y, conv_w_full + rs_down[4][0, 0], d_act, d_ff)
    tm = _divisor_tile(l, 1088, 16)
    gw_up = _mm_gw_up(cn, d_u)
    d2d_up = _reduce_scatter_d2d_start(gw_up, d_act, "rs_w_up")
    d_cn = _mm_d_cn(d_u, w_up_blocks, d2d_up[4])
    rs_up = _reduce_scatter_ici_start(d2d_up, d_cn, core, "rs_w_up")
    d_h1, d_h1_b, dg_norm2 = _rmsnorm_bwd(d_h2, d_cn, h1, norm2_gain + rs_up[4][0, 0], "rmsnorm2_bwd", True)

    gw_out = _mm_tn(mix, d_h1_b, WIRE_DTYPE, "mm_gw_out")
    d2d_out = _reduce_scatter_d2d_start(gw_out.reshape(N_DEV, d // N_DEV, d), d_cn, "rs_w_out")
    d_mix = _mm_nt(d_h1_b, w_out_full, F32, "mm_d_mix", after=d2d_out[4])
    d_fq, d_fk, d_fv, ds_sum = _fox_bwd(proj, cum_bc, cum_rows, d_mix, lse_rows)
    d_ff_tile, db_forget_row = _fox_gate_bwd(ds_sum, proj, bias_row)
    d_ret, dg_ret = _retention_bwd(proj, ret_pre, ret_states, d_mix, ret_norm_gain, consts)
    rs_out = _reduce_scatter_ici_start(d2d_out, d_ret, core, "rs_w_out")
    d_proj = jnp.concatenate(
        [d_ret, d_fq, d_fk, d_fv, d_ff_tile, jnp.zeros((l, WIN_N - 7 * GROUP - 128), MXU_DTYPE)], axis=1)
    gw_in = _mm_tn(d_proj, a, WIRE_DTYPE, "mm_gw_in", tm_cap=1536, after=rs_out[4])
    rs_in = _reduce_scatter_start(_extract_w_in_windows(gw_in), core, "rs_w_in")
    d_a = _mm_nn(d_proj, w_in_full, F32, "mm_d_a", tm_cap=544, tn_cap=256, tk_cap=WIN_N, after=rs_in[4])
    d_front, d_tokens, dg_norm1 = _rmsnorm_bwd(d_h1, d_a, h0, norm1_gain + rs_in[4][0, 0], "rmsnorm1_bwd", False)
    grad_x = d_tokens[None]
    d_meta = d_front[PAD_ROWS:CHUNK]

    d_conv_w = jnp.concatenate([d_conv[0, 0:3], d_conv[1, 0:3]], axis=1)
    d_conv_b = jnp.concatenate([d_conv[0, 3:4], d_conv[1, 3:4]], axis=1)
    pieces = [loss_part[:, 0:1], dg_norm1, db_forget_row[:, 0:N_HEADS], dg_ret, dg_norm2, d_conv_b, dg_final,
              d_meta.reshape(1, -1), d_conv_w.reshape(1, -1)]
    sizes = [p.shape[1] for p in pieces]
    flat = jnp.concatenate(pieces, axis=1)
    padded = -(-flat.shape[1] // 1024) * 1024
    flat = jnp.pad(flat, ((0, 0), (0, padded - flat.shape[1]))).reshape(padded // 128, 128)
    small_ar = _small_all_reduce_start(flat, d_tokens, "all_reduce_small")

    lead = lambda outs: tuple(o[None] for o in outs)
    fin_down = lead(_reduce_scatter_finish(rs_down, small_ar[4], chip, w_down[0], m_w_down[0], v_w_down[0], "rs_w_down"))
    fin_up = lead(_reduce_scatter_finish(rs_up, fin_down[3], chip, w_up[0], m_w_up[0], v_w_up[0], "rs_w_up"))
    fin_out = lead(_reduce_scatter_finish(rs_out, fin_up[3], chip, w_out[0], m_w_out[0], v_w_out[0], "rs_w_out"))
    fin_in = tuple(from_rows(o) for o in _reduce_scatter_finish(
        rs_in, fin_out[3], chip, w_in_rows, to_rows(m_w_in), to_rows(v_w_in), "rs_w_in"))
    g_w_down, g_w_up, g_w_out, g_w_in = fin_down[0], fin_up[0], fin_out[0], fin_in[0]
    early = [fin_down[1:], fin_up[1:], fin_out[1:], fin_in[1:]]
    total = _small_all_reduce_finish(small_ar, fin_in[3], dev1, "all_reduce_small").reshape(1, padded)
    offs = np.concatenate([[0], np.cumsum(sizes)])
    take = lambda k: total[:, int(offs[k]):int(offs[k + 1])]
    loss = take(0).reshape(())
    g_norm1, g_bf, g_ret_gain, g_norm2 = take(1), take(2), take(3), take(4)
    g_conv_b, g_final = take(5), take(6).reshape(d)
    g_meta = lax.dynamic_slice(take(7).reshape(N_META, d), (jnp.int32(0), (dev * (d // N_DEV)).astype(jnp.int32)),
                               (N_META, d // N_DEV))
    g_conv_w = lax.dynamic_slice(take(8).reshape(3, 2 * d_ff), (jnp.int32(0), (dev * up_shard).astype(jnp.int32)),
                                 (3, up_shard))[None]

    weights = [meta_tokens, norm1_gain, w_in, b_forget, ret_norm_gain, w_out, norm2_gain, w_up, conv_w, conv_b,
               w_down, final_norm_gain]
    grads = [g_meta, g_norm1, g_w_in, g_bf, g_ret_gain, g_w_out, g_norm2, g_w_up, g_conv_w, g_conv_b, g_w_down,
             g_final]
    done = {"w_down": early[0], "w_up": early[1], "w_out": early[2], "w_in": early[3]}
    ms = [m_meta_tokens, m_norm1_gain, m_w_in, m_b_forget, m_ret_norm_gain, m_w_out, m_norm2_gain, m_w_up, m_conv_w,
          m_conv_b, m_w_down, m_final_norm_gain]
    vs = [v_meta_tokens, v_norm1_gain, v_w_in, v_b_forget, v_ret_norm_gain, v_w_out, v_norm2_gain, v_w_up, v_conv_w,
          v_conv_b, v_w_down, v_final_norm_gain]
    names = ["meta", "norm1", "w_in", "b_forget", "ret_gain", "w_out", "norm2", "w_up", "conv_w", "conv_b", "w_down",
             "final_gain"]
    deltas, new_ms, new_vs = [], [], []
    for w, g, m, v, n in zip(weights, grads, ms, vs, names):
        dl, nm, nv = done[n] if n in done else _adamw(w, g, m, v, "adamw_" + n)
        deltas.append(dl)
        new_ms.append(nm)
        new_vs.append(nv)
    return (loss, grad_x, *grads, *deltas, *new_ms, *new_vs)
```
